```python
import math
import jax, jax.numpy as jnp
from jax import lax
import numpy as np

D_MODEL = 1024
BATCH = 8
SEQ = 4096
DEPTH = 1

MEM_LEN = 256
HEAD_DIM = 64
LRU_WIDTH = 512
LRU_BLOCKS = 8
LRU_BLOCK = LRU_WIDTH // LRU_BLOCKS
CONV_WIDTH = 4
LRU_C = 8.0
SWA_Q_HEADS = 4
SWA_KV_HEADS = 2
SWA_GROUP = SWA_Q_HEADS // SWA_KV_HEADS
SWA_WIDTH = SWA_Q_HEADS * HEAD_DIM
SWA_KV_WIDTH = SWA_KV_HEADS * HEAD_DIM
WINDOW = 128
BLOCK = 128
XATTN_HEADS = 4
XATTN_WIDTH = XATTN_HEADS * HEAD_DIM
D_MIX = LRU_WIDTH + SWA_WIDTH + XATTN_WIDTH
IN_SPLITS = (LRU_WIDTH, LRU_WIDTH, SWA_WIDTH, SWA_KV_WIDTH, SWA_KV_WIDTH, SWA_WIDTH, XATTN_WIDTH, XATTN_WIDTH)
D_IN = sum(IN_SPLITS)
ROPE_THETA = 500000.0
ROPE_DIM = HEAD_DIM // 4
EPS = 1e-6
NEG_INF = -1e30

kernel_name = "hymba_style_rglru_swa_sink_memxattn"


def _rmsnorm(x, g):
    xf = x.astype(jnp.float32)
    y = xf * lax.rsqrt(jnp.mean(xf * xf, axis=-1, keepdims=True) + EPS)
    return (y * g.astype(jnp.float32)).astype(x.dtype)


def _rope_tables(seq):
    pos = jnp.arange(seq, dtype=jnp.float32)
    inv_freq = ROPE_THETA ** (-(jnp.arange(0, ROPE_DIM, 2, dtype=jnp.float32) / ROPE_DIM))
    ang = pos[:, None] * inv_freq[None, :]
    return jnp.cos(ang), jnp.sin(ang)


def _partial_rope(t, cos, sin):
    tf = t.astype(jnp.float32)
    half = ROPE_DIM // 2
    x1, x2, rest = tf[..., :half], tf[..., half:ROPE_DIM], tf[..., ROPE_DIM:]
    c, s = cos[None, :, None, :], sin[None, :, None, :]
    return jnp.concatenate([x1 * c - x2 * s, x2 * c + x1 * s, rest], axis=-1).astype(t.dtype)


def _rg_lru(u, conv_w, conv_b, w_rg, b_rg, w_ig, b_ig, lam):
    B, S, W = u.shape
    xc = lax.conv_general_dilated(
        u, conv_w[:, None, :].astype(u.dtype), window_strides=(1,),
        padding=[(CONV_WIDTH - 1, 0)], dimension_numbers=("NWC", "WIO", "NWC"),
        feature_group_count=W) + conv_b.astype(u.dtype)
    xf = xc.astype(jnp.float32)
    xblk = xf.reshape(B, S, LRU_BLOCKS, LRU_BLOCK)
    r = jax.nn.sigmoid(jnp.einsum("bsnd,nde->bsne", xblk, w_rg.astype(jnp.float32)).reshape(B, S, W)
                       + b_rg.astype(jnp.float32))
    i = jax.nn.sigmoid(jnp.einsum("bsnd,nde->bsne", xblk, w_ig.astype(jnp.float32)).reshape(B, S, W)
                       + b_ig.astype(jnp.float32))
    log_a = -LRU_C * r * jax.nn.softplus(-lam.astype(jnp.float32))
    a = jnp.exp(log_a)
    b = jnp.sqrt(-jnp.expm1(2.0 * log_a)) * (i * xf)

    def combine(c1, c2):
        a1, b1 = c1
        a2, b2 = c2
        return a1 * a2, a2 * b1 + b2

    _, h = lax.associative_scan(combine, (a, b), axis=1)
    return h.astype(u.dtype)


def _band(t):
    B, S, H, D = t.shape
    tb = t.reshape(B, S // BLOCK, BLOCK, H, D)
    prev = jnp.pad(tb, ((0, 0), (1, 0), (0, 0), (0, 0), (0, 0)))[:, :-1]
    return jnp.concatenate([prev, tb], axis=2)


def _swa_sink_attention(q, k, v, sinks):
    B, S, Hq, D = q.shape
    nb = S // BLOCK
    qb = q.reshape(B, nb, BLOCK, SWA_KV_HEADS, SWA_GROUP, D)
    kb, vb = _band(k), _band(v)
    s = jnp.einsum("bnqhgd,bnkhd->bnhgqk", qb, kb,
                   preferred_element_type=jnp.float32) * (1.0 / math.sqrt(D))
    qi = jnp.arange(BLOCK)[:, None]
    kj = jnp.arange(2 * BLOCK)[None, :]
    rel = qi + BLOCK - kj
    band_mask = (rel >= 0) & (rel < WINDOW)
    blk = jnp.arange(nb)[:, None, None]
    mask = band_mask[None] & ((blk > 0) | (kj >= BLOCK)[None])
    s = jnp.where(mask[None, :, None, None], s, NEG_INF)
    sink = sinks.astype(jnp.float32).reshape(SWA_KV_HEADS, SWA_GROUP)[None, None, :, :, None, None]
    m = jnp.maximum(jnp.max(s, axis=-1, keepdims=True), sink)
    p = jnp.exp(s - m)
    denom = jnp.sum(p, axis=-1, keepdims=True) + jnp.exp(sink - m)
    o = jnp.einsum("bnhgqk,bnkhd->bnqhgd", (p / denom).astype(v.dtype), vb)
    return o.reshape(B, S, Hq * D)


def _memory_attention(q, km, vm):
    B, S, H, D = q.shape
    s = jnp.einsum("bshd,bmhd->bhsm", q, km, preferred_element_type=jnp.float32) * (1.0 / math.sqrt(D))
    p = jax.nn.softmax(s, axis=-1)
    o = jnp.einsum("bhsm,bmhd->bshd", p.astype(vm.dtype), vm)
    return o.reshape(B, S, H * D)


def _fwd_setup_inputs(seed: int = 0) -> dict:
    key = jax.random.key(seed)
    ks = jax.random.split(key, 20)
    f32 = jnp.float32
    nrm = lambda k, shape, scale: jax.random.normal(k, shape, f32) * scale
    x = jax.random.normal(ks[0], (BATCH, SEQ, D_MODEL), f32)
    mem = jax.random.normal(ks[1], (BATCH, MEM_LEN, D_MODEL), f32)
    u = jax.random.uniform(ks[11], (DEPTH, LRU_WIDTH), f32, 0.9, 0.999) ** (1.0 / LRU_C)
    lru_lambda = jnp.log(u) - jnp.log1p(-u)
    return {
        "x": x,
        "mem": mem,
        "norm_g": 1.0 + nrm(ks[2], (DEPTH, D_MODEL), 0.02),
        "mem_norm_g": 1.0 + nrm(ks[3], (DEPTH, D_MODEL), 0.02),
        "w_in": nrm(ks[4], (DEPTH, D_MODEL, D_IN), D_MODEL ** -0.5),
        "conv_w": nrm(ks[5], (DEPTH, CONV_WIDTH, LRU_WIDTH), CONV_WIDTH ** -0.5),
        "conv_b": nrm(ks[6], (DEPTH, LRU_WIDTH), 0.01),
        "w_rg": nrm(ks[7], (DEPTH, LRU_BLOCKS, LRU_BLOCK, LRU_BLOCK), LRU_BLOCK ** -0.5),
        "b_rg": nrm(ks[8], (DEPTH, LRU_WIDTH), 0.01),
        "w_ig": nrm(ks[9], (DEPTH, LRU_BLOCKS, LRU_BLOCK, LRU_BLOCK), LRU_BLOCK ** -0.5),
        "b_ig": nrm(ks[10], (DEPTH, LRU_WIDTH), 0.01),
        "lru_lambda": lru_lambda,
        "q_norm_g": 1.0 + nrm(ks[12], (DEPTH, HEAD_DIM), 0.02),
        "k_norm_g": 1.0 + nrm(ks[13], (DEPTH, HEAD_DIM), 0.02),
        "sinks": nrm(ks[14], (DEPTH, SWA_Q_HEADS), 0.5),
        "w_mem_kv": nrm(ks[15], (DEPTH, D_MODEL, 2 * XATTN_WIDTH), D_MODEL ** -0.5),
        "xq_norm_g": 1.0 + nrm(ks[16], (DEPTH, HEAD_DIM), 0.02),
        "xk_norm_g": 1.0 + nrm(ks[17], (DEPTH, HEAD_DIM), 0.02),
        "out_norm_g": 1.0 + nrm(ks[18], (DEPTH, D_MIX), 0.02),
        "w_out": nrm(ks[19], (DEPTH, D_MIX, D_MODEL), D_MIX ** -0.5),
    }


def _fwd_reference(x, mem, norm_g, mem_norm_g, w_in, conv_w, conv_b, w_rg, b_rg, w_ig, b_ig, lru_lambda,
              q_norm_g, k_norm_g, sinks, w_mem_kv, xq_norm_g, xk_norm_g, out_norm_g, w_out):
    B, S, _ = x.shape
    M = mem.shape[1]
    cos, sin = _rope_tables(S)
    split_idx = list(np.cumsum(IN_SPLITS)[:-1])
    out_idx = [LRU_WIDTH, LRU_WIDTH + SWA_WIDTH]
    h = x
    for l in range(DEPTH):
        xn = _rmsnorm(h, norm_g[l])
        proj = jnp.einsum("bsd,de->bse", xn, w_in[l])
        (lru_x, lru_gate, sq, sk, sv, swa_gate, xq, x_gate) = jnp.split(proj, split_idx, axis=-1)

        y_a = _rg_lru(lru_x, conv_w[l], conv_b[l], w_rg[l], b_rg[l], w_ig[l], b_ig[l], lru_lambda[l])

        q = _partial_rope(_rmsnorm(sq.reshape(B, S, SWA_Q_HEADS, HEAD_DIM), q_norm_g[l]), cos, sin)
        k = _partial_rope(_rmsnorm(sk.reshape(B, S, SWA_KV_HEADS, HEAD_DIM), k_norm_g[l]), cos, sin)
        v = sv.reshape(B, S, SWA_KV_HEADS, HEAD_DIM)
        y_b = _swa_sink_attention(q, k, v, sinks[l])

        mn = _rmsnorm(mem, mem_norm_g[l])
        mkv = jnp.einsum("bmd,de->bme", mn, w_mem_kv[l])
        km, vm = jnp.split(mkv, 2, axis=-1)
        km = _rmsnorm(km.reshape(B, M, XATTN_HEADS, HEAD_DIM), xk_norm_g[l])
        vm = vm.reshape(B, M, XATTN_HEADS, HEAD_DIM)
        qx = _rmsnorm(xq.reshape(B, S, XATTN_HEADS, HEAD_DIM), xq_norm_g[l])
        y_c = _memory_attention(qx, km, vm)

        g_a, g_b, g_c = jnp.split(out_norm_g[l], out_idx)
        y = jnp.concatenate([
            _rmsnorm(y_a, g_a) * jax.nn.silu(lru_gate),
            _rmsnorm(y_b, g_b) * jax.nn.silu(swa_gate),
            _rmsnorm(y_c, g_c) * jax.nn.silu(x_gate),
        ], axis=-1)
        h = h + jnp.einsum("bse,ed->bsd", y, w_out[l]).astype(h.dtype)
    return h


import jax as _jax
import jax.numpy as _jnp

TWIN_FORMAT = 'train_step'
FWD_PARAMS = ['x', 'mem', 'norm_g', 'mem_norm_g', 'w_in', 'conv_w', 'conv_b', 'w_rg', 'b_rg', 'w_ig', 'b_ig', 'lru_lambda', 'q_norm_g', 'k_norm_g', 'sinks', 'w_mem_kv', 'xq_norm_g', 'xk_norm_g', 'out_norm_g', 'w_out']
TWIN_WEIGHTS = ['norm_g', 'mem_norm_g', 'w_in', 'conv_w', 'conv_b', 'w_rg', 'b_rg', 'w_ig', 'b_ig', 'lru_lambda', 'q_norm_g', 'k_norm_g', 'sinks', 'w_mem_kv', 'xq_norm_g', 'xk_norm_g', 'out_norm_g', 'w_out']
TWIN_DIFF_INPUT = 'x'
TWIN_INPUTS = ['x', 'mem', 'norm_g', 'mem_norm_g', 'w_in', 'conv_w', 'conv_b', 'w_rg', 'b_rg', 'w_ig', 'b_ig', 'lru_lambda', 'q_norm_g', 'k_norm_g', 'sinks', 'w_mem_kv', 'xq_norm_g', 'xk_norm_g', 'out_norm_g', 'w_out', 'loss_target', 'm_norm_g', 'm_mem_norm_g', 'm_w_in', 'm_conv_w', 'm_conv_b', 'm_w_rg', 'm_b_rg', 'm_w_ig', 'm_b_ig', 'm_lru_lambda', 'm_q_norm_g', 'm_k_norm_g', 'm_sinks', 'm_w_mem_kv', 'm_xq_norm_g', 'm_xk_norm_g', 'm_out_norm_g', 'm_w_out', 'v_norm_g', 'v_mem_norm_g', 'v_w_in', 'v_conv_w', 'v_conv_b', 'v_w_rg', 'v_b_rg', 'v_w_ig', 'v_b_ig', 'v_lru_lambda', 'v_q_norm_g', 'v_k_norm_g', 'v_sinks', 'v_w_mem_kv', 'v_xq_norm_g', 'v_xk_norm_g', 'v_out_norm_g', 'v_w_out']
TWIN_OUTPUTS = ['loss', 'grad_x', 'grad_norm_g', 'grad_mem_norm_g', 'grad_w_in', 'grad_conv_w', 'grad_conv_b', 'grad_w_rg', 'grad_b_rg', 'grad_w_ig', 'grad_b_ig', 'grad_lru_lambda', 'grad_q_norm_g', 'grad_k_norm_g', 'grad_sinks', 'grad_w_mem_kv', 'grad_xq_norm_g', 'grad_xk_norm_g', 'grad_out_norm_g', 'grad_w_out', 'delta_norm_g', 'delta_mem_norm_g', 'delta_w_in', 'delta_conv_w', 'delta_conv_b', 'delta_w_rg', 'delta_b_rg', 'delta_w_ig', 'delta_b_ig', 'delta_lru_lambda', 'delta_q_norm_g', 'delta_k_norm_g', 'delta_sinks', 'delta_w_mem_kv', 'delta_xq_norm_g', 'delta_xk_norm_g', 'delta_out_norm_g', 'delta_w_out', 'new_m_norm_g', 'new_m_mem_norm_g', 'new_m_w_in', 'new_m_conv_w', 'new_m_conv_b', 'new_m_w_rg', 'new_m_b_rg', 'new_m_w_ig', 'new_m_b_ig', 'new_m_lru_lambda', 'new_m_q_norm_g', 'new_m_k_norm_g', 'new_m_sinks', 'new_m_w_mem_kv', 'new_m_xq_norm_g', 'new_m_xk_norm_g', 'new_m_out_norm_g', 'new_m_w_out', 'new_v_norm_g', 'new_v_mem_norm_g', 'new_v_w_in', 'new_v_conv_w', 'new_v_conv_b', 'new_v_w_rg', 'new_v_b_rg', 'new_v_w_ig', 'new_v_b_ig', 'new_v_lru_lambda', 'new_v_q_norm_g', 'new_v_k_norm_g', 'new_v_sinks', 'new_v_w_mem_kv', 'new_v_xq_norm_g', 'new_v_xk_norm_g', 'new_v_out_norm_g', 'new_v_w_out']
TWIN_LEAF_KINDS = {'loss': 'loss', 'grad_x': 'grad_x', 'grad_norm_g': 'grad_w', 'grad_mem_norm_g': 'grad_w', 'grad_w_in': 'grad_w', 'grad_conv_w': 'grad_w', 'grad_conv_b': 'grad_w', 'grad_w_rg': 'grad_w', 'grad_b_rg': 'grad_w', 'grad_w_ig': 'grad_w', 'grad_b_ig': 'grad_w', 'grad_lru_lambda': 'grad_w', 'grad_q_norm_g': 'grad_w', 'grad_k_norm_g': 'grad_w', 'grad_sinks': 'grad_w', 'grad_w_mem_kv': 'grad_w', 'grad_xq_norm_g': 'grad_w', 'grad_xk_norm_g': 'grad_w', 'grad_out_norm_g': 'grad_w', 'grad_w_out': 'grad_w', 'delta_norm_g': 'delta_w', 'delta_mem_norm_g': 'delta_w', 'delta_w_in': 'delta_w', 'delta_conv_w': 'delta_w', 'delta_conv_b': 'delta_w', 'delta_w_rg': 'delta_w', 'delta_b_rg': 'delta_w', 'delta_w_ig': 'delta_w', 'delta_b_ig': 'delta_w', 'delta_lru_lambda': 'delta_w', 'delta_q_norm_g': 'delta_w', 'delta_k_norm_g': 'delta_w', 'delta_sinks': 'delta_w', 'delta_w_mem_kv': 'delta_w', 'delta_xq_norm_g': 'delta_w', 'delta_xk_norm_g': 'delta_w', 'delta_out_norm_g': 'delta_w', 'delta_w_out': 'delta_w', 'new_m_norm_g': 'new_m', 'new_m_mem_norm_g': 'new_m', 'new_m_w_in': 'new_m', 'new_m_conv_w': 'new_m', 'new_m_conv_b': 'new_m', 'new_m_w_rg': 'new_m', 'new_m_b_rg': 'new_m', 'new_m_w_ig': 'new_m', 'new_m_b_ig': 'new_m', 'new_m_lru_lambda': 'new_m', 'new_m_q_norm_g': 'new_m', 'new_m_k_norm_g': 'new_m', 'new_m_sinks': 'new_m', 'new_m_w_mem_kv': 'new_m', 'new_m_xq_norm_g': 'new_m', 'new_m_xk_norm_g': 'new_m', 'new_m_out_norm_g': 'new_m', 'new_m_w_out': 'new_m', 'new_v_norm_g': 'new_v', 'new_v_mem_norm_g': 'new_v', 'new_v_w_in': 'new_v', 'new_v_conv_w': 'new_v', 'new_v_conv_b': 'new_v', 'new_v_w_rg': 'new_v', 'new_v_b_rg': 'new_v', 'new_v_w_ig': 'new_v', 'new_v_b_ig': 'new_v', 'new_v_lru_lambda': 'new_v', 'new_v_q_norm_g': 'new_v', 'new_v_k_norm_g': 'new_v', 'new_v_sinks': 'new_v', 'new_v_w_mem_kv': 'new_v', 'new_v_xq_norm_g': 'new_v', 'new_v_xk_norm_g': 'new_v', 'new_v_out_norm_g': 'new_v', 'new_v_w_out': 'new_v'}


def _forward(args):
    return _fwd_reference(*[args[k] for k in FWD_PARAMS])


def _output_shape():
    out = _jax.eval_shape(lambda: _forward(_fwd_setup_inputs(0)))
    return out.shape, out.dtype

N_MICROBATCH = 1
ADAM_LR = 0.001
ADAM_B1 = 0.9
ADAM_B2 = 0.999
ADAM_EPS = 1e-08
ADAM_WD = 0.01
ADAM_STEP = 10
PER_EXAMPLE_BATCH_AXIS = {'x': 0, 'mem': 0, 'loss_target': 0}
SHARED_INPUTS = []
_WEIGHT_DTYPES = {'norm_g': _jnp.float32, 'mem_norm_g': _jnp.float32, 'w_in': _jnp.float32, 'conv_w': _jnp.float32, 'conv_b': _jnp.float32, 'w_rg': _jnp.float32, 'b_rg': _jnp.float32, 'w_ig': _jnp.float32, 'b_ig': _jnp.float32, 'lru_lambda': _jnp.float32, 'q_norm_g': _jnp.float32, 'k_norm_g': _jnp.float32, 'sinks': _jnp.float32, 'w_mem_kv': _jnp.float32, 'xq_norm_g': _jnp.float32, 'xk_norm_g': _jnp.float32, 'out_norm_g': _jnp.float32, 'w_out': _jnp.float32}
MOMENT_SCALE = {'norm_g': 1.322849e+01, 'mem_norm_g': 2.048607e-01, 'w_in': 3.803947e-01, 'conv_w': 7.375690e-01, 'conv_b': 8.548832e+00, 'w_rg': 4.990083e-01, 'b_rg': 1.969380e-01, 'w_ig': 9.147617e-01, 'b_ig': 3.486836e-01, 'lru_lambda': 2.321674e-01, 'q_norm_g': 4.802747e-01, 'k_norm_g': 4.387567e-01, 'sinks': 3.353291e-02, 'w_mem_kv': 2.708106e-01, 'xq_norm_g': 4.808297e-01, 'xk_norm_g': 4.780157e-01, 'out_norm_g': 1.564809e+01, 'w_out': 4.806981e-01}


def _to_microbatches(a, axis):
    t = _jnp.moveaxis(a, axis, 0)
    t = t.reshape((N_MICROBATCH, t.shape[0] // N_MICROBATCH) + t.shape[1:])
    return _jnp.moveaxis(t, 1, axis + 1)


def setup_inputs(seed: int = 0) -> dict:
    inp = _fwd_setup_inputs(seed)
    key = _jax.random.fold_in(_jax.random.key(seed), 7919)
    shape, _ = _output_shape()
    out = dict(inp)
    out["loss_target"] = _jax.random.normal(_jax.random.fold_in(key, 0), shape, _jnp.float32)
    for i, name in enumerate(TWIN_WEIGHTS):
        w = inp[name].astype(_jnp.float32)
        if MOMENT_SCALE is None:
            s = _jnp.sqrt(_jnp.mean(_jnp.square(w)) + 1e-30)
        else:
            s = MOMENT_SCALE[name]
        km, kv = _jax.random.split(_jax.random.fold_in(key, i + 1))
        out[name] = w
        out["m_" + name] = s * _jax.random.normal(km, w.shape, _jnp.float32)
        out["v_" + name] = (s * s) * _jax.random.uniform(kv, w.shape, _jnp.float32, 0.5, 1.5)
    if N_MICROBATCH > 1:
        for name, axis in PER_EXAMPLE_BATCH_AXIS.items():
            out[name] = _to_microbatches(out[name], axis)
    return {'x': out['x'], 'mem': out['mem'], 'norm_g': out['norm_g'], 'mem_norm_g': out['mem_norm_g'], 'w_in': out['w_in'], 'conv_w': out['conv_w'], 'conv_b': out['conv_b'], 'w_rg': out['w_rg'], 'b_rg': out['b_rg'], 'w_ig': out['w_ig'], 'b_ig': out['b_ig'], 'lru_lambda': out['lru_lambda'], 'q_norm_g': out['q_norm_g'], 'k_norm_g': out['k_norm_g'], 'sinks': out['sinks'], 'w_mem_kv': out['w_mem_kv'], 'xq_norm_g': out['xq_norm_g'], 'xk_norm_g': out['xk_norm_g'], 'out_norm_g': out['out_norm_g'], 'w_out': out['w_out'], 'loss_target': out['loss_target'], 'm_norm_g': out['m_norm_g'], 'm_mem_norm_g': out['m_mem_norm_g'], 'm_w_in': out['m_w_in'], 'm_conv_w': out['m_conv_w'], 'm_conv_b': out['m_conv_b'], 'm_w_rg': out['m_w_rg'], 'm_b_rg': out['m_b_rg'], 'm_w_ig': out['m_w_ig'], 'm_b_ig': out['m_b_ig'], 'm_lru_lambda': out['m_lru_lambda'], 'm_q_norm_g': out['m_q_norm_g'], 'm_k_norm_g': out['m_k_norm_g'], 'm_sinks': out['m_sinks'], 'm_w_mem_kv': out['m_w_mem_kv'], 'm_xq_norm_g': out['m_xq_norm_g'], 'm_xk_norm_g': out['m_xk_norm_g'], 'm_out_norm_g': out['m_out_norm_g'], 'm_w_out': out['m_w_out'], 'v_norm_g': out['v_norm_g'], 'v_mem_norm_g': out['v_mem_norm_g'], 'v_w_in': out['v_w_in'], 'v_conv_w': out['v_conv_w'], 'v_conv_b': out['v_conv_b'], 'v_w_rg': out['v_w_rg'], 'v_b_rg': out['v_b_rg'], 'v_w_ig': out['v_w_ig'], 'v_b_ig': out['v_b_ig'], 'v_lru_lambda': out['v_lru_lambda'], 'v_q_norm_g': out['v_q_norm_g'], 'v_k_norm_g': out['v_k_norm_g'], 'v_sinks': out['v_sinks'], 'v_w_mem_kv': out['v_w_mem_kv'], 'v_xq_norm_g': out['v_xq_norm_g'], 'v_xk_norm_g': out['v_xk_norm_g'], 'v_out_norm_g': out['v_out_norm_g'], 'v_w_out': out['v_w_out']}


def _loss(weights, diff, rest, loss_target):
    with _jax.named_scope("forward"):
        args = {**rest, TWIN_DIFF_INPUT: diff, **{k: w.astype(_WEIGHT_DTYPES[k]) for k, w in weights.items()}}
        y = _forward(args)
    with _jax.named_scope("loss_head"):
        err = _jnp.square(y.astype(_jnp.float32) - loss_target)
        return 0.5 * _jnp.sum(_jnp.mean(err, axis=-1)) if err.ndim else 0.5 * err


def _adamw(w, g, m, v):
    m = ADAM_B1 * m + (1.0 - ADAM_B1) * g
    v = ADAM_B2 * v + (1.0 - ADAM_B2) * _jnp.square(g)
    m_hat = m / (1.0 - ADAM_B1 ** ADAM_STEP)
    v_hat = v / (1.0 - ADAM_B2 ** ADAM_STEP)
    delta = -ADAM_LR * (m_hat / (_jnp.sqrt(v_hat) + ADAM_EPS) + ADAM_WD * w)
    return delta, m, v


def reference(x, mem, norm_g, mem_norm_g, w_in, conv_w, conv_b, w_rg, b_rg, w_ig, b_ig, lru_lambda, q_norm_g, k_norm_g, sinks, w_mem_kv, xq_norm_g, xk_norm_g, out_norm_g, w_out, loss_target, m_norm_g, m_mem_norm_g, m_w_in, m_conv_w, m_conv_b, m_w_rg, m_b_rg, m_w_ig, m_b_ig, m_lru_lambda, m_q_norm_g, m_k_norm_g, m_sinks, m_w_mem_kv, m_xq_norm_g, m_xk_norm_g, m_out_norm_g, m_w_out, v_norm_g, v_mem_norm_g, v_w_in, v_conv_w, v_conv_b, v_w_rg, v_b_rg, v_w_ig, v_b_ig, v_lru_lambda, v_q_norm_g, v_k_norm_g, v_sinks, v_w_mem_kv, v_xq_norm_g, v_xk_norm_g, v_out_norm_g, v_w_out):
    given = dict(x=x, mem=mem, norm_g=norm_g, mem_norm_g=mem_norm_g, w_in=w_in, conv_w=conv_w, conv_b=conv_b, w_rg=w_rg, b_rg=b_rg, w_ig=w_ig, b_ig=b_ig, lru_lambda=lru_lambda, q_norm_g=q_norm_g, k_norm_g=k_norm_g, sinks=sinks, w_mem_kv=w_mem_kv, xq_norm_g=xq_norm_g, xk_norm_g=xk_norm_g, out_norm_g=out_norm_g, w_out=w_out, loss_target=loss_target, m_norm_g=m_norm_g, m_mem_norm_g=m_mem_norm_g, m_w_in=m_w_in, m_conv_w=m_conv_w, m_conv_b=m_conv_b, m_w_rg=m_w_rg, m_b_rg=m_b_rg, m_w_ig=m_w_ig, m_b_ig=m_b_ig, m_lru_lambda=m_lru_lambda, m_q_norm_g=m_q_norm_g, m_k_norm_g=m_k_norm_g, m_sinks=m_sinks, m_w_mem_kv=m_w_mem_kv, m_xq_norm_g=m_xq_norm_g, m_xk_norm_g=m_xk_norm_g, m_out_norm_g=m_out_norm_g, m_w_out=m_w_out, v_norm_g=v_norm_g, v_mem_norm_g=v_mem_norm_g, v_w_in=v_w_in, v_conv_w=v_conv_w, v_conv_b=v_conv_b, v_w_rg=v_w_rg, v_b_rg=v_b_rg, v_w_ig=v_w_ig, v_b_ig=v_b_ig, v_lru_lambda=v_lru_lambda, v_q_norm_g=v_q_norm_g, v_k_norm_g=v_k_norm_g, v_sinks=v_sinks, v_w_mem_kv=v_w_mem_kv, v_xq_norm_g=v_xq_norm_g, v_xk_norm_g=v_xk_norm_g, v_out_norm_g=v_out_norm_g, v_w_out=v_w_out)
    weights = {n: given[n] for n in TWIN_WEIGHTS}
    shared = {n: given[n] for n in SHARED_INPUTS}
    per_example = {n: given[n] for n in ['x', 'mem']}
    grad_fn = _jax.value_and_grad(_loss, argnums=(0, 1))

    def one_microbatch(ex, loss_target):
        ex = dict(ex)
        diff = ex.pop(TWIN_DIFF_INPUT)
        return grad_fn(weights, diff, {**shared, **ex}, loss_target)

    if N_MICROBATCH == 1:
        loss, (grad_w, grad_x) = one_microbatch(per_example, given["loss_target"])
    else:
        def body(carry, xs):
            loss_sum, grad_sum = carry
            l_k, (gw_k, gx_k) = one_microbatch(xs[0], xs[1])
            with _jax.named_scope("update"):
                return (loss_sum + l_k, _jax.tree.map(_jnp.add, grad_sum, gw_k)), gx_k

        init = (_jnp.zeros((), _jnp.float32), _jax.tree.map(_jnp.zeros_like, weights))
        (loss, grad_w), grad_x = _jax.lax.scan(body, init, (per_example, given["loss_target"]))
    with _jax.named_scope("update"):
        delta_w, new_m, new_v = {}, {}, {}
        for n in TWIN_WEIGHTS:
            delta_w[n], new_m[n], new_v[n] = _adamw(weights[n], grad_w[n], given["m_" + n], given["v_" + n])
    return (loss, grad_x, *[grad_w[n] for n in TWIN_WEIGHTS], *[delta_w[n] for n in TWIN_WEIGHTS],
            *[new_m[n] for n in TWIN_WEIGHTS], *[new_v[n] for n in TWIN_WEIGHTS])
```

```python
import functools
import math

import jax
import jax.numpy as jnp
from jax import lax
from jax.experimental import pallas as pl
from jax.experimental.pallas import tpu as pltpu

F32 = jnp.float32
_MXU = jnp.bfloat16
_WIRE = jnp.bfloat16

D_MODEL = 1024
MEM_LEN = 256
HEAD = 64
LRU_W = 512
LRU_BLOCKS = 8
CONV_K = 4
LRU_C = 8.0
SWA_W = 256
KV_W = 128
XATT_W = 256
BLOCK = 128
D_IN = 2304
ROPE_THETA = 500000.0
ROPE_DIM = 16
EPS = 1e-6
NEG_INF = -1e30
C_LRUX, C_LRUG, C_SQ, C_SK, C_SV, C_SWAG, C_XQ, C_XG = 0, 512, 1024, 1280, 1408, 1536, 1792, 2048

ADAM_LR, ADAM_B1, ADAM_B2, ADAM_EPS, ADAM_WD, ADAM_STEP = 0.001, 0.9, 0.999, 1e-08, 0.01, 10

N_CHIPS = 4
ROW_TILE = 256
VMEM_LIMIT = 56 * 1024 * 1024
MESH = pl.DeviceIdType.MESH


def _mm(a, b):
    return jnp.dot(a.astype(_MXU), b.astype(_MXU), preferred_element_type=F32)


def _mm_nt(a, b):
    return lax.dot_general(a.astype(_MXU), b.astype(_MXU), (((1,), (1,)), ((), ())), preferred_element_type=F32)


def _mm_tn(a, b):
    return lax.dot_general(a.astype(_MXU), b.astype(_MXU), (((0,), (0,)), ((), ())), preferred_element_type=F32)


def _group_matrix(width):
    r = lax.shift_right_logical(lax.broadcasted_iota(jnp.int32, (width, width), 0), 6)
    c = lax.shift_right_logical(lax.broadcasted_iota(jnp.int32, (width, width), 1), 6)
    return (r == c).astype(_MXU)


def _seg_mean(x, gm):
    hi = x.astype(_MXU)
    lo = (x - hi.astype(F32)).astype(_MXU)
    s = jnp.dot(hi, gm, preferred_element_type=F32) + jnp.dot(lo, gm, preferred_element_type=F32)
    return s * (1.0 / HEAD)


def _row_mean(x):
    return jnp.mean(x, axis=-1, keepdims=True)


def _col_sum(x):
    return jnp.sum(x, axis=0, keepdims=True)


def _sigmoid(x):
    return jax.nn.sigmoid(x)


def _softplus(z):
    e = jnp.exp(-jnp.abs(z))
    u = 1.0 + e
    log1p_e = jnp.where(u == 1.0, e, jnp.log(u) * (e / (u - 1.0)))
    return jnp.maximum(z, 0.0) + log1p_e


def _rope(t, c, s1, s2):
    return t * c + pltpu.roll(t, 120, 1) * s1 + pltpu.roll(t, 8, 1) * s2


def _rope_bwd(d, c, s1, s2):
    return d * c + pltpu.roll(d * s1, 8, 1) + pltpu.roll(d * s2, 120, 1)


def _lane_mask(width, lo, hi):
    lane = lax.broadcasted_iota(jnp.int32, (1, width), 1)
    return ((lane >= lo) & (lane < hi)).astype(F32)


def _swa_mask(first_block):
    qi = lax.broadcasted_iota(jnp.int32, (2 * BLOCK, 2 * BLOCK), 0) & (BLOCK - 1)
    kj = lax.broadcasted_iota(jnp.int32, (2 * BLOCK, 2 * BLOCK), 1)
    rel = qi + BLOCK - kj
    ok = (rel >= 0) & (rel < BLOCK)
    return ok & (jnp.logical_not(first_block) | (kj >= BLOCK))


def _sink_col(sink_ref, h):
    row = lax.broadcasted_iota(jnp.int32, (2 * BLOCK, 1), 0)
    return jnp.where(row < BLOCK, sink_ref[0, 2 * h], sink_ref[0, 2 * h + 1])


def _stack_heads(t128, h, scale):
    m = _lane_mask(KV_W, HEAD * h, HEAD * (h + 1))
    rolled = pltpu.roll(t128, HEAD, 1)
    even, odd = (t128, rolled) if h == 0 else (rolled, t128)
    return jnp.concatenate([even * m, odd * m], axis=0) * scale


def _unstack_heads(s256, h):
    m = _lane_mask(KV_W, HEAD * h, HEAD * (h + 1))
    even, odd = s256[:BLOCK] * m, s256[BLOCK:] * m
    if h == 0:
        return even + pltpu.roll(odd, HEAD, 1)
    return pltpu.roll(even, HEAD, 1) + odd


def _swa_probs(qs, kb, mask, sinkc):
    s = _mm_nt(qs, kb)
    s = jnp.where(mask, s, NEG_INF)
    m = jnp.maximum(jnp.max(s, axis=-1, keepdims=True), sinkc)
    p = jnp.exp(s - m)
    esink = jnp.exp(sinkc - m)
    inv = 1.0 / (jnp.sum(p, axis=-1, keepdims=True) + esink)
    return p * inv, esink * inv


def _lru_gates(xc, wg_ref, brg, big, lam):
    p0 = _mm(xc[:, :256], wg_ref[0])
    p1 = _mm(xc[:, 256:], wg_ref[1])
    rg = _sigmoid(jnp.concatenate([p0[:, :256], p1[:, :256]], axis=1) + brg)
    ig = _sigmoid(jnp.concatenate([p0[:, 256:], p1[:, 256:]], axis=1) + big)
    sp = _softplus(-lam)
    la = (-LRU_C) * rg * sp
    a = jnp.exp(la)
    th = jnp.tanh(la)
    one_minus_a2 = (-2.0 * th) / (1.0 - th)
    return rg, ig, sp, a, jnp.sqrt(one_minus_a2)


def _const_spec(shape, single=False):
    zeros = (0,) * len(shape)
    if single:
        return pl.BlockSpec(shape, lambda i: zeros, pipeline_mode=pl.Buffered(1))
    return pl.BlockSpec(shape, lambda i: zeros)


def _chip_of(x, y):
    return 2 * x + y


def gather_weights(win_t, wout, wkv, convw):
    arrs = (win_t, wout, wkv)
    n = len(arrs)

    def body(a0, a1, a2, cw, o0, o1, o2, ocw, send, recv, lsem):
        ins, outs = (a0, a1, a2), (o0, o1, o2)
        x, y, c = lax.axis_index("x"), lax.axis_index("y"), lax.axis_index("c")
        sibling = (x, y, 1 - c)
        chips = [(1 - x, y), (x, 1 - y), (1 - x, 1 - y)]
        me = _chip_of(x, y)

        def rows(a, chip, half):
            r = ins[a].shape[0]
            return pl.ds(pl.multiple_of(chip * r + half * (r // 2), 16), r // 2)

        def own_half(a, half):
            r = ins[a].shape[0]
            return ins[a].at[pl.ds(pl.multiple_of(half * (r // 2), 16), r // 2)]

        def copy(k, src, dst, to):
            return pltpu.make_async_remote_copy(src_ref=src, dst_ref=dst, send_sem=send.at[k], recv_sem=recv.at[k],
                                                device_id=to, device_id_type=MESH)

        locals_ = []
        for a in range(n):
            r = ins[a].shape[0]
            locals_.append(pltpu.make_async_copy(ins[a], outs[a].at[pl.ds(pl.multiple_of(me * r, 16), r)], lsem.at[a]))
        locals_.append(pltpu.make_async_copy(cw, ocw.at[pl.ds(pl.multiple_of(me * 8, 8), 8)], lsem.at[n]))
        for cp in locals_:
            cp.start()

        first, passed = [], []
        for a in range(n):
            for j, chip in enumerate(chips):
                first.append(copy(a * 6 + j, own_half(a, c), outs[a].at[rows(a, me, c)], (*chip, c)))
        for j, chip in enumerate(chips):
            first.append(copy(n * 6 + j, cw, ocw.at[pl.ds(pl.multiple_of(me * 8, 8), 8)], (*chip, c)))
        for cp in first:
            cp.start()
        for a in range(n):
            for j, chip in enumerate(chips):
                got = outs[a].at[rows(a, _chip_of(*chip), c)]
                copy(a * 6 + j, got, got, (*chip, c)).wait_recv()
                fwd = copy(a * 6 + 3 + j, got, got, sibling)
                fwd.start()
                passed.append(fwd)
        for a in range(n):
            for j, chip in enumerate(chips):
                got = outs[a].at[rows(a, _chip_of(*chip), 1 - c)]
                copy(a * 6 + 3 + j, got, got, sibling).wait_recv()
        for j, chip in enumerate(chips):
            got = ocw.at[pl.ds(pl.multiple_of(_chip_of(*chip) * 8, 8), 8)]
            copy(n * 6 + j, got, got, (*chip, c)).wait_recv()
        for cp in first + passed:
            cp.wait_send()
        for cp in locals_:
            cp.wait()

    vm = pl.BlockSpec(memory_space=pltpu.VMEM)
    out_shape = tuple(jax.ShapeDtypeStruct((N_CHIPS * a.shape[0],) + a.shape[1:], a.dtype) for a in arrs) + (
        jax.ShapeDtypeStruct((N_CHIPS * 8, 128), F32),)
    n_rdma = n * 6 + 3
    return pl.pallas_call(
        body, name="gather_weights", out_shape=out_shape,
        in_specs=[vm] * 4, out_specs=(vm,) * 4,
        scratch_shapes=[pltpu.SemaphoreType.DMA((n_rdma,)), pltpu.SemaphoreType.DMA((n_rdma,)),
                        pltpu.SemaphoreType.DMA((n + 1,))],
        compiler_params=pltpu.CompilerParams(vmem_limit_bytes=VMEM_LIMIT),
    )(win_t, wout, wkv, convw)


def mem_fwd(mem, mem_g, wkv, xk_g):
    def body(mem_ref, g_ref, w_ref, xk_ref, km_ref, vm_ref):
        mem_v = mem_ref[...]
        mn = mem_v * lax.rsqrt(_row_mean(mem_v * mem_v) + EPS) * g_ref[...]
        mkv = _mm(mn, w_ref[...])
        kpre = mkv[:, :XATT_W]
        gm = _group_matrix(XATT_W)
        km_ref[...] = kpre * lax.rsqrt(_seg_mean(kpre * kpre, gm) + EPS) * xk_ref[...]
        vm_ref[...] = mkv[:, XATT_W:]

    vm = pl.BlockSpec(memory_space=pltpu.VMEM)
    return pl.pallas_call(
        body, name="mem_fwd",
        out_shape=(jax.ShapeDtypeStruct((MEM_LEN, XATT_W), F32), jax.ShapeDtypeStruct((MEM_LEN, XATT_W), F32)),
        in_specs=[vm] * 4, out_specs=(vm, vm),
    )(mem, mem_g, wkv, xk_g)


def mem_bwd(mem, mem_g, wkv, xk_g, dkm, dvm):
    def body(mem_ref, g_ref, w_ref, xk_ref, dkm_ref, dvm_ref, gw_ref, gg_ref, gxk_ref):
        mem_v = mem_ref[...]
        mh = mem_v * lax.rsqrt(_row_mean(mem_v * mem_v) + EPS)
        mn = mh * g_ref[...]
        mkv = _mm(mn, w_ref[...])
        kpre = mkv[:, :XATT_W]
        gm = _group_matrix(XATT_W)
        rk = lax.rsqrt(_seg_mean(kpre * kpre, gm) + EPS)
        kn = kpre * rk
        dk = dkm_ref[...]
        gxk_ref[...] = _col_sum(dk * kn)
        dkn = dk * xk_ref[...]
        dkpre = rk * (dkn - kn * _seg_mean(dkn * kn, gm))
        dmkv = jnp.concatenate([dkpre, dvm_ref[...]], axis=1)
        gw_ref[...] = _mm_tn(mn, dmkv)
        dmn = _mm_nt(dmkv, w_ref[...])
        gg_ref[...] = _col_sum(dmn * mh)

    vm = pl.BlockSpec(memory_space=pltpu.VMEM)
    return pl.pallas_call(
        body, name="mem_bwd",
        out_shape=(jax.ShapeDtypeStruct((D_MODEL, 2 * XATT_W), F32), jax.ShapeDtypeStruct((1, D_MODEL), F32),
                   jax.ShapeDtypeStruct((1, XATT_W), F32)),
        in_specs=[vm] * 6, out_specs=(vm, vm, vm),
    )(mem, mem_g, wkv, xk_g, dkm, dvm)


def layer_fwd(x, tgt, rc, rs1, rs2, ng, win_t, cw, cb, wg, brg, big, lam, qg, kg, xqg, sinks, km, vm, og, wout):
    seq = x.shape[0]
    tm = min(ROW_TILE, seq)
    nt = seq // tm
    nb = tm // BLOCK

    def body(x_ref, t_ref, c_ref, s1_ref, s2_ref, ng_ref, win_ref, cw_ref, cb_ref, wg_ref, brg_ref, big_ref, lam_ref,
             qg_ref, kg_ref, xqg_ref, sink_ref, km_ref, vm_ref, og_ref, wout_ref,
             proj_ref, ya_ref, yb_ref, yc_ref, ycat_ref, xn_ref, dout_ref, loss_ref,
             ext_ref, a_scr, b_scr, hc_ref, kp_ref, vp_ref, lacc_ref):
        i = pl.program_id(0)

        @pl.when(i == 0)
        def _():
            ext_ref[0:8, :] = jnp.zeros((8, LRU_W), F32)
            hc_ref[...] = jnp.zeros_like(hc_ref)
            kp_ref[...] = jnp.zeros_like(kp_ref)
            vp_ref[...] = jnp.zeros_like(vp_ref)
            lacc_ref[...] = jnp.zeros_like(lacc_ref)

        xv = x_ref[...]
        xn = (xv * lax.rsqrt(_row_mean(xv * xv) + EPS) * ng_ref[...]).astype(_MXU)
        xn_ref[...] = xn.astype(xn_ref.dtype)
        proj_ref[...] = _mm_nt(xn, win_ref[...])

        u = proj_ref[:, C_LRUX:C_LRUX + LRU_W]
        ext_ref[8:8 + tm, :] = u
        xc = cb_ref[...]
        for k in range(CONV_K):
            xc = xc + cw_ref[k:k + 1, :] * ext_ref[pl.ds(5 + k, tm), :]
        ext_ref[0:8, :] = u[tm - 8:tm, :]
        rg, ig, sp, a, sq = _lru_gates(xc, wg_ref, brg_ref[...], big_ref[...], lam_ref[...])
        a_scr[...] = a
        b_scr[...] = sq * (ig * xc)
        row8 = lax.broadcasted_iota(jnp.int32, (8, LRU_W), 0)

        def scan_step(g, carry):
            r0 = pl.multiple_of(g * 8, 8)
            av = a_scr[pl.ds(r0, 8), :]
            bv = b_scr[pl.ds(r0, 8), :]
            for d in (1, 2, 4):
                a_sh = jnp.where(row8 >= d, pltpu.roll(av, d, 0), 1.0)
                b_sh = jnp.where(row8 >= d, pltpu.roll(bv, d, 0), 0.0)
                bv = bv + av * b_sh
                av = av * a_sh
            hv = bv + av * carry
            ya_ref[pl.ds(r0, 8), :] = hv
            return hv[7:8, :]

        hc_ref[0:1, :] = lax.fori_loop(0, tm // 8, scan_step, hc_ref[0:1, :])

        gm128 = _group_matrix(KV_W)
        cv, s1v, s2v = c_ref[...], s1_ref[...], s2_ref[...]

        def head_norm_rope(t, g):
            n = t * lax.rsqrt(_seg_mean(t * t, gm128) + EPS)
            return _rope(n * g, cv, s1v, s2v)

        qs_ = (head_norm_rope(proj_ref[:, C_SQ:C_SQ + 128], qg_ref[...]),
               head_norm_rope(proj_ref[:, C_SQ + 128:C_SQ + 256], qg_ref[...]))
        kr = head_norm_rope(proj_ref[:, C_SK:C_SK + KV_W], kg_ref[...])
        sv = proj_ref[:, C_SV:C_SV + KV_W]
        kext = jnp.concatenate([kp_ref[...], kr], axis=0).astype(_MXU)
        vext = jnp.concatenate([vp_ref[...], sv], axis=0).astype(_MXU)
        kp_ref[...] = kr[tm - BLOCK:tm, :]
        vp_ref[...] = sv[tm - BLOCK:tm, :]
        for b in range(nb):
            mask = _swa_mask((i == 0) & (b == 0)) if b == 0 else _swa_mask(False)
            kb = kext[BLOCK * b:BLOCK * b + 2 * BLOCK]
            vb = vext[BLOCK * b:BLOCK * b + 2 * BLOCK]
            for h in range(2):
                qstk = _stack_heads(qs_[h][BLOCK * b:BLOCK * (b + 1)], h, 0.125)
                p, _ = _swa_probs(qstk, kb, mask, _sink_col(sink_ref, h))
                o = _mm(p, vb)
                yb_ref[BLOCK * b:BLOCK * (b + 1), KV_W * h:KV_W * (h + 1)] = _unstack_heads(o, h)

        gm256 = _group_matrix(XATT_W)
        xq = proj_ref[:, C_XQ:C_XQ + XATT_W]
        qx = xq * lax.rsqrt(_seg_mean(xq * xq, gm256) + EPS) * xqg_ref[...]
        kmv = km_ref[...].astype(_MXU)
        vmv = vm_ref[...].astype(_MXU)
        yc = jnp.zeros((tm, XATT_W), F32)
        for j in range(4):
            mj = _lane_mask(XATT_W, HEAD * j, HEAD * (j + 1))
            s = _mm_nt(qx * (mj * 0.125), kmv)
            p = jnp.exp(s - jnp.max(s, axis=-1, keepdims=True))
            p = p * (1.0 / jnp.sum(p, axis=-1, keepdims=True))
            yc = yc + _mm(p, vmv) * mj
        yc_ref[...] = yc

        def gated(y, g, gate):
            return y * lax.rsqrt(_row_mean(y * y) + EPS) * g * (gate * _sigmoid(gate))

        ogv = og_ref[...]
        za = gated(ya_ref[...], ogv[:, :512], proj_ref[:, C_LRUG:C_LRUG + LRU_W])
        zb = gated(yb_ref[...], ogv[:, 512:768], proj_ref[:, C_SWAG:C_SWAG + SWA_W])
        zc = gated(yc, ogv[:, 768:], proj_ref[:, C_XG:C_XG + XATT_W])
        ycat_ref[:, 0:512] = za.astype(ycat_ref.dtype)
        ycat_ref[:, 512:768] = zb.astype(ycat_ref.dtype)
        ycat_ref[:, 768:1024] = zc.astype(ycat_ref.dtype)
        out = xv + _mm(ycat_ref[...], wout_ref[...])
        err = out - t_ref[...]
        dout_ref[...] = err * (1.0 / D_MODEL)
        lacc_ref[...] = lacc_ref[...] + (0.5 / D_MODEL) * jnp.sum(err * err)

        @pl.when(i == nt - 1)
        def _():
            loss_ref[...] = lacc_ref[...]

    def rows(ncol):
        return pl.BlockSpec((tm, ncol), lambda i: (i, 0))

    in_specs = [rows(D_MODEL), rows(D_MODEL), rows(128), rows(128), rows(128),
                _const_spec((1, D_MODEL)), _const_spec((D_IN, D_MODEL), True), _const_spec((CONV_K, LRU_W)),
                _const_spec((1, LRU_W)), _const_spec((2, 256, 512), True), _const_spec((1, LRU_W)),
                _const_spec((1, LRU_W)), _const_spec((1, LRU_W)), _const_spec((1, 128)), _const_spec((1, 128)),
                _const_spec((1, XATT_W)), pl.BlockSpec(memory_space=pltpu.SMEM),
                _const_spec((MEM_LEN, XATT_W), True), _const_spec((MEM_LEN, XATT_W), True),
                _const_spec((1, D_MODEL)), _const_spec((D_MODEL, D_MODEL), True)]
    out_shape = (jax.ShapeDtypeStruct((seq, D_IN), F32), jax.ShapeDtypeStruct((seq, LRU_W), F32),
                 jax.ShapeDtypeStruct((seq, SWA_W), F32), jax.ShapeDtypeStruct((seq, XATT_W), F32),
                 jax.ShapeDtypeStruct((seq, D_MODEL), _MXU), jax.ShapeDtypeStruct((seq, D_MODEL), _MXU),
                 jax.ShapeDtypeStruct((seq, D_MODEL), F32), jax.ShapeDtypeStruct((8, 128), F32))
    out_specs = (rows(D_IN), rows(LRU_W), rows(SWA_W), rows(XATT_W), rows(D_MODEL), rows(D_MODEL), rows(D_MODEL),
                 _const_spec((8, 128)))
    scratch = [pltpu.VMEM((tm + 8, LRU_W), F32), pltpu.VMEM((tm, LRU_W), F32), pltpu.VMEM((tm, LRU_W), F32),
               pltpu.VMEM((8, LRU_W), F32), pltpu.VMEM((BLOCK, KV_W), F32), pltpu.VMEM((BLOCK, KV_W), F32),
               pltpu.VMEM((8, 128), F32)]
    return pl.pallas_call(
        body, name="layer_fwd", grid=(nt,), out_shape=out_shape, in_specs=in_specs, out_specs=out_specs,
        scratch_shapes=scratch,
        compiler_params=pltpu.CompilerParams(dimension_semantics=("arbitrary",), vmem_limit_bytes=VMEM_LIMIT),
    )(x, tgt, rc, rs1, rs2, ng, win_t, cw, cb, wg, brg, big, lam, qg, kg, xqg, sinks, km, vm, og, wout)


def wgrad_out(ycat, dout):
    seq = ycat.shape[0]
    tk = min(512, seq)
    nk = seq // tk

    def body(y_ref, d_ref, o_ref):
        @pl.when(pl.program_id(0) == 0)
        def _():
            o_ref[...] = jnp.zeros_like(o_ref)

        o_ref[...] += _mm_tn(y_ref[...], d_ref[...])

    return pl.pallas_call(
        body, name="wgrad_out", grid=(nk,), out_shape=jax.ShapeDtypeStruct((D_MODEL, D_MODEL), F32),
        in_specs=[pl.BlockSpec((tk, D_MODEL), lambda k: (k, 0)), pl.BlockSpec((tk, D_MODEL), lambda k: (k, 0))],
        out_specs=pl.BlockSpec((D_MODEL, D_MODEL), lambda k: (0, 0)),
        compiler_params=pltpu.CompilerParams(dimension_semantics=("arbitrary",), vmem_limit_bytes=VMEM_LIMIT),
    )(ycat, dout)


def wgrad_in(dproj, xn):
    seq = xn.shape[0]
    nblk = D_IN // 256

    def body(d_ref, x_ref, o_ref):
        o_ref[...] = _mm_tn(d_ref[...], x_ref[...])

    return pl.pallas_call(
        body, name="wgrad_in", grid=(nblk,), out_shape=jax.ShapeDtypeStruct((D_IN, D_MODEL), F32),
        in_specs=[pl.BlockSpec((seq, 256), lambda j: (0, j)), _const_spec((seq, D_MODEL), True)],
        out_specs=pl.BlockSpec((256, D_MODEL), lambda j: (j, 0)),
        compiler_params=pltpu.CompilerParams(dimension_semantics=("arbitrary",), vmem_limit_bytes=VMEM_LIMIT),
    )(dproj, xn)


def layer_bwd(x, dout, proj, ya, yb, yc, rc, rs1, rs2, ng, win_t, cw, cb, wg, brg, big, lam, qg, kg, xqg, sinks, km,
              vm, og, wout):
    seq = x.shape[0]
    tm = min(ROW_TILE, seq)
    nt = seq // tm
    nb = tm // BLOCK

    def body(x_ref, dout_ref, proj_ref, ya_ref, yb_ref, yc_ref, c_ref, s1_ref, s2_ref,
             uh_ref, yah_ref, kvh_ref, ch_ref, s1h_ref, s2h_ref,
             ng_ref, win_ref, cw_ref, cb_ref, wg_ref, brg_ref, big_ref, lam_ref, qg_ref, kg_ref, xqg_ref, sink_ref,
             km_ref, vm_ref, og_ref, wout_ref,
             gx_ref, dproj_ref, gwg_ref, dkm_ref, dvm_ref, gng_ref, gog_ref, gcb_ref, gbrg_ref, gbig_ref, glam_ref,
             gcw_ref, gqn_ref, gkn_ref, gxqn_ref, gsink_ref,
             ext_ref, hext_ref, aext_ref, an_scr, dh_scr, g_scr, dxc_ext, gcar_ref, dkx_ref, dvx_ref, dkcar_ref,
             dvcar_ref, dq_scr):
        i = pl.program_id(0)
        tile = nt - 1 - i
        first_tile = tile == 0

        @pl.when(i == 0)
        def _():
            for r in (gwg_ref, dkm_ref, dvm_ref, gng_ref, gog_ref, gcb_ref, gbrg_ref, gbig_ref, glam_ref, gcw_ref,
                      gqn_ref, gkn_ref, gxqn_ref, gsink_ref, gcar_ref, dkcar_ref, dvcar_ref):
                r[...] = jnp.zeros_like(r)
            dxc_ext[tm:tm + 8, :] = jnp.zeros((8, LRU_W), F32)
            aext_ref[tm:tm + 8, :] = jnp.zeros((8, LRU_W), F32)

        xv = x_ref[...]
        dov = dout_ref[...]
        dz = _mm_nt(dov, wout_ref[...])
        ogv = og_ref[...]

        def group_bwd(y, gate, g, dzg):
            r = lax.rsqrt(_row_mean(y * y) + EPS)
            n = y * r
            sg = _sigmoid(gate)
            dgate = dzg * (n * g) * (sg * (1.0 + gate * (1.0 - sg)))
            dng = dzg * (gate * sg)
            dn = dng * g
            return r * (dn - n * _row_mean(dn * n)), dgate, _col_sum(dng * n)

        dya, dga, goa = group_bwd(ya_ref[...], proj_ref[:, C_LRUG:C_LRUG + LRU_W], ogv[:, :512], dz[:, :512])
        dyb, dgb, gob = group_bwd(yb_ref[...], proj_ref[:, C_SWAG:C_SWAG + SWA_W], ogv[:, 512:768], dz[:, 512:768])
        dyc, dgc, goc = group_bwd(yc_ref[...], proj_ref[:, C_XG:C_XG + XATT_W], ogv[:, 768:], dz[:, 768:])
        gog_ref[...] += jnp.concatenate([goa, gob, goc], axis=1)
        dproj_ref[:, C_LRUG:C_LRUG + LRU_W] = dga.astype(dproj_ref.dtype)
        dproj_ref[:, C_SWAG:C_SWAG + SWA_W] = dgb.astype(dproj_ref.dtype)
        dproj_ref[:, C_XG:C_XG + XATT_W] = dgc.astype(dproj_ref.dtype)

        gm256 = _group_matrix(XATT_W)
        xq = proj_ref[:, C_XQ:C_XQ + XATT_W]
        rq = lax.rsqrt(_seg_mean(xq * xq, gm256) + EPS)
        qn = xq * rq
        qx = qn * xqg_ref[...]
        kmv = km_ref[...].astype(_MXU)
        vmv = vm_ref[...].astype(_MXU)
        dqx = jnp.zeros((tm, XATT_W), F32)
        dkm_t = jnp.zeros((MEM_LEN, XATT_W), F32)
        dvm_t = jnp.zeros((MEM_LEN, XATT_W), F32)
        for j in range(4):
            mj = _lane_mask(XATT_W, HEAD * j, HEAD * (j + 1))
            qm = (qx * (mj * 0.125)).astype(_MXU)
            s = _mm_nt(qm, kmv)
            p = jnp.exp(s - jnp.max(s, axis=-1, keepdims=True))
            p = p * (1.0 / jnp.sum(p, axis=-1, keepdims=True))
            doj = (dyc * mj).astype(_MXU)
            dvm_t = dvm_t + _mm_tn(p, doj)
            dp = _mm_nt(doj, vmv)
            ds = p * (dp - jnp.sum(p * dp, axis=-1, keepdims=True))
            dqx = dqx + _mm(ds, kmv) * (mj * 0.125)
            dkm_t = dkm_t + _mm_tn(ds, qm)
        dkm_ref[...] += dkm_t
        dvm_ref[...] += dvm_t
        gxqn_ref[...] += _col_sum(dqx * qn)
        dqn = dqx * xqg_ref[...]
        dproj_ref[:, C_XQ:C_XQ + XATT_W] = (rq * (dqn - qn * _seg_mean(dqn * qn, gm256))).astype(dproj_ref.dtype)

        gm128 = _group_matrix(KV_W)
        cv, s1v, s2v = c_ref[...], s1_ref[...], s2_ref[...]

        def head_norm(t):
            r = lax.rsqrt(_seg_mean(t * t, gm128) + EPS)
            return t * r, r

        qn_, qr_ = zip(head_norm(proj_ref[:, C_SQ:C_SQ + 128]), head_norm(proj_ref[:, C_SQ + 128:C_SQ + 256]))
        qrope = [_rope(qn_[h] * qg_ref[...], cv, s1v, s2v) for h in range(2)]
        kn, krr = head_norm(proj_ref[:, C_SK:C_SK + KV_W])
        kr = _rope(kn * kg_ref[...], cv, s1v, s2v)
        khn, _ = head_norm(kvh_ref[:, 0:KV_W])
        khr = _rope(khn * kg_ref[...], ch_ref[...], s1h_ref[...], s2h_ref[...])
        kext = jnp.concatenate([khr, kr], axis=0).astype(_MXU)
        vext = jnp.concatenate([kvh_ref[:, KV_W:2 * KV_W], proj_ref[:, C_SV:C_SV + KV_W]], axis=0).astype(_MXU)
        dkx_ref[0:tm, :] = jnp.zeros((tm, KV_W), F32)
        dvx_ref[0:tm, :] = jnp.zeros((tm, KV_W), F32)
        dkx_ref[tm:tm + BLOCK, :] = dkcar_ref[...]
        dvx_ref[tm:tm + BLOCK, :] = dvcar_ref[...]
        lane128 = lax.broadcasted_iota(jnp.int32, (1, 128), 1)
        row256 = lax.broadcasted_iota(jnp.int32, (2 * BLOCK, 1), 0)
        gsink = jnp.zeros((1, 128), F32)
        for b in range(nb):
            mask = _swa_mask(first_tile & (b == 0)) if b == 0 else _swa_mask(False)
            kb = kext[BLOCK * b:BLOCK * b + 2 * BLOCK]
            vb = vext[BLOCK * b:BLOCK * b + 2 * BLOCK]
            for h in range(2):
                qstk = _stack_heads(qrope[h][BLOCK * b:BLOCK * (b + 1)], h, 0.125).astype(_MXU)
                p, psink = _swa_probs(qstk, kb, mask, _sink_col(sink_ref, h))
                do = _stack_heads(dyb[BLOCK * b:BLOCK * (b + 1), KV_W * h:KV_W * (h + 1)], h, 1.0).astype(_MXU)
                dvx_ref[BLOCK * b:BLOCK * b + 2 * BLOCK, :] += _mm_tn(p, do)
                dp = _mm_nt(do, vb)
                delta = jnp.sum(p * dp, axis=-1, keepdims=True)
                ds = p * (dp - delta)
                dsink = -psink * delta
                gsink = gsink + jnp.where(lane128 == 2 * h, jnp.sum(jnp.where(row256 < BLOCK, dsink, 0.0)), 0.0)
                gsink = gsink + jnp.where(lane128 == 2 * h + 1, jnp.sum(jnp.where(row256 >= BLOCK, dsink, 0.0)), 0.0)
                dq_scr[h, BLOCK * b:BLOCK * (b + 1), :] = _unstack_heads(_mm(ds, kb) * 0.125, h)
                dkx_ref[BLOCK * b:BLOCK * b + 2 * BLOCK, :] += _mm_tn(ds, qstk)
        gsink_ref[...] += gsink
        dkcar_ref[...] = dkx_ref[0:BLOCK, :]
        dvcar_ref[...] = dvx_ref[0:BLOCK, :]
        dkg = _rope_bwd(dkx_ref[BLOCK:BLOCK + tm, :], cv, s1v, s2v)
        gkn = _col_sum(dkg * kn)
        dkn = dkg * kg_ref[...]
        dproj_ref[:, C_SK:C_SK + KV_W] = (krr * (dkn - kn * _seg_mean(dkn * kn, gm128))).astype(dproj_ref.dtype)
        dproj_ref[:, C_SV:C_SV + KV_W] = dvx_ref[BLOCK:BLOCK + tm, :].astype(dproj_ref.dtype)
        gqn = jnp.zeros((1, 128), F32)
        for h in range(2):
            dqg = _rope_bwd(dq_scr[h], cv, s1v, s2v)
            gqn = gqn + _col_sum(dqg * qn_[h])
            dqn_ = dqg * qg_ref[...]
            dproj_ref[:, C_SQ + 128 * h:C_SQ + 128 * (h + 1)] = (
                qr_[h] * (dqn_ - qn_[h] * _seg_mean(dqn_ * qn_[h], gm128))).astype(dproj_ref.dtype)
        gqn_ref[...] += gqn
        gkn_ref[...] += gkn

        u = proj_ref[:, C_LRUX:C_LRUX + LRU_W]
        ext_ref[0:8, :] = jnp.where(first_tile, 0.0, uh_ref[...])
        ext_ref[8:8 + tm, :] = u
        us = [ext_ref[pl.ds(5 + k, tm), :] for k in range(CONV_K)]
        xc = cb_ref[...]
        for k in range(CONV_K):
            xc = xc + cw_ref[k:k + 1, :] * us[k]
        rg, ig, sp, a, sq = _lru_gates(xc, wg_ref, brg_ref[...], big_ref[...], lam_ref[...])
        hext_ref[0:8, :] = jnp.where(first_tile, 0.0, yah_ref[...])
        hext_ref[8:8 + tm, :] = ya_ref[...]
        hprev = hext_ref[pl.ds(7, tm), :]
        aext_ref[0:tm, :] = a
        an_scr[...] = aext_ref[pl.ds(1, tm), :]
        dh_scr[...] = dya
        dh_scr[tm - 1:tm, :] = dh_scr[tm - 1:tm, :] + gcar_ref[0:1, :]
        row8 = lax.broadcasted_iota(jnp.int32, (8, LRU_W), 0)

        def scan_step(gi, carry):
            r0 = pl.multiple_of((tm // 8 - 1 - gi) * 8, 8)
            av = an_scr[pl.ds(r0, 8), :]
            bv = dh_scr[pl.ds(r0, 8), :]
            for d in (1, 2, 4):
                a_sh = jnp.where(row8 < 8 - d, pltpu.roll(av, 8 - d, 0), 1.0)
                b_sh = jnp.where(row8 < 8 - d, pltpu.roll(bv, 8 - d, 0), 0.0)
                bv = bv + av * b_sh
                av = av * a_sh
            gv = bv + av * carry
            g_scr[pl.ds(r0, 8), :] = gv
            return gv[0:1, :]

        g0 = lax.fori_loop(0, tm // 8, scan_step, jnp.zeros((1, LRU_W), F32))
        gcar_ref[0:1, :] = a[0:1, :] * g0
        gv = g_scr[...]
        da = gv * hprev
        dig = gv * sq * xc
        dxc = gv * sq * ig
        dla = da * a - gv * (ig * xc) * ((a * a) / sq)
        drg = dla * ((-LRU_C) * sp)
        glam_ref[...] += _col_sum(dla * rg)
        dpr = drg * rg * (1.0 - rg)
        dpi = dig * ig * (1.0 - ig)
        gbrg_ref[...] += _col_sum(dpr)
        gbig_ref[...] += _col_sum(dpi)
        dpre0 = jnp.concatenate([dpr[:, :256], dpi[:, :256]], axis=1).astype(_MXU)
        dpre1 = jnp.concatenate([dpr[:, 256:], dpi[:, 256:]], axis=1).astype(_MXU)
        gwg_ref[0] += _mm_tn(xc[:, :256], dpre0)
        gwg_ref[1] += _mm_tn(xc[:, 256:], dpre1)
        dxc = dxc + jnp.concatenate([_mm_nt(dpre0, wg_ref[0]), _mm_nt(dpre1, wg_ref[1])], axis=1)
        gcb_ref[...] += _col_sum(dxc)
        for k in range(CONV_K):
            gcw_ref[k:k + 1, :] += _col_sum(dxc * us[k])
        dxc_ext[0:tm, :] = dxc
        du = jnp.zeros((tm, LRU_W), F32)
        for k in range(CONV_K):
            du = du + cw_ref[k:k + 1, :] * dxc_ext[pl.ds(3 - k, tm), :]
        dxc_ext[tm:tm + 8, :] = dxc[0:8, :]
        dproj_ref[:, C_LRUX:C_LRUX + LRU_W] = du.astype(dproj_ref.dtype)

        dxn = _mm(dproj_ref[...], win_ref[...])
        rx = lax.rsqrt(_row_mean(xv * xv) + EPS)
        xh = xv * rx
        gng_ref[...] += _col_sum(dxn * xh)
        dxh = dxn * ng_ref[...]
        gx_ref[...] = dov + rx * (dxh - xh * _row_mean(dxh * xh))

        @pl.when(i == nt - 1)
        def _():
            glam_ref[...] = glam_ref[...] * (LRU_C * _sigmoid(-lam_ref[...]))

    def rows(ncol, arr_cols_block=0):
        return pl.BlockSpec((tm, ncol), lambda i: (nt - 1 - i, arr_cols_block))

    def halo(nrow, ncol, colblk=0):
        per = tm // nrow
        return pl.BlockSpec((nrow, ncol), lambda i: (jnp.maximum((nt - 1 - i) * per - 1, 0), colblk))

    in_specs = [rows(D_MODEL), rows(D_MODEL), rows(D_IN), rows(LRU_W), rows(SWA_W), rows(XATT_W),
                rows(128), rows(128), rows(128),
                halo(8, LRU_W), halo(8, LRU_W), halo(BLOCK, 2 * KV_W, C_SK // (2 * KV_W)),
                halo(BLOCK, 128), halo(BLOCK, 128), halo(BLOCK, 128),
                _const_spec((1, D_MODEL)), _const_spec((D_IN, D_MODEL), True), _const_spec((CONV_K, LRU_W)),
                _const_spec((1, LRU_W)), _const_spec((2, 256, 512), True), _const_spec((1, LRU_W)),
                _const_spec((1, LRU_W)), _const_spec((1, LRU_W)), _const_spec((1, 128)), _const_spec((1, 128)),
                _const_spec((1, XATT_W)), pl.BlockSpec(memory_space=pltpu.SMEM),
                _const_spec((MEM_LEN, XATT_W), True), _const_spec((MEM_LEN, XATT_W), True),
                _const_spec((1, D_MODEL)), _const_spec((D_MODEL, D_MODEL), True)]
    small = [(2, 256, 512), (MEM_LEN, XATT_W), (MEM_LEN, XATT_W), (1, D_MODEL), (1, D_MODEL), (1, LRU_W), (1, LRU_W),
             (1, LRU_W), (1, LRU_W), (CONV_K, LRU_W), (1, 128), (1, 128), (1, XATT_W), (1, 128)]
    out_shape = (jax.ShapeDtypeStruct((seq, D_MODEL), F32), jax.ShapeDtypeStruct((seq, D_IN), _MXU)) + tuple(
        jax.ShapeDtypeStruct(s, F32) for s in small)
    out_specs = (rows(D_MODEL), rows(D_IN)) + tuple(_const_spec(s) for s in small)
    scratch = [pltpu.VMEM((tm + 8, LRU_W), F32), pltpu.VMEM((tm + 8, LRU_W), F32), pltpu.VMEM((tm + 8, LRU_W), F32),
               pltpu.VMEM((tm, LRU_W), F32), pltpu.VMEM((tm, LRU_W), F32), pltpu.VMEM((tm, LRU_W), F32),
               pltpu.VMEM((tm + 8, LRU_W), F32),
               pltpu.VMEM((8, LRU_W), F32), pltpu.VMEM((tm + BLOCK, KV_W), F32), pltpu.VMEM((tm + BLOCK, KV_W), F32),
               pltpu.VMEM((BLOCK, KV_W), F32), pltpu.VMEM((BLOCK, KV_W), F32), pltpu.VMEM((2, tm, 128), F32)]
    return pl.pallas_call(
        body, name="layer_bwd", grid=(nt,), out_shape=out_shape, in_specs=in_specs, out_specs=out_specs,
        scratch_shapes=scratch,
        compiler_params=pltpu.CompilerParams(dimension_semantics=("arbitrary",), vmem_limit_bytes=VMEM_LIMIT),
    )(x, dout, proj, ya, yb, yc, rc, rs1, rs2, proj, ya, proj, rc, rs1, rs2,
      ng, win_t, cw, cb, wg, brg, big, lam, qg, kg, xqg, sinks, km, vm, og, wout)


def reduce_grads(g_in, g_out, g_kv, g_small):
    bigs = (g_in, g_out, g_kv)
    nbig = len(bigs)

    def body(b0, b1, b2, sm, o0, o1, o2, osm, r1_0, r1_1, r1_2, r1s, w0, w1, w2, r2_0, r2_1, r2_2, r2s, send, recv):
        big, outs = (b0, b1, b2), (o0, o1, o2)
        r1, wire, r2 = (r1_0, r1_1, r1_2), (w0, w1, w2), (r2_0, r2_1, r2_2)
        x, y, c = lax.axis_index("x"), lax.axis_index("y"), lax.axis_index("c")
        sibling = (x, y, 1 - c)
        chips = [(1 - x, y), (x, 1 - y), (1 - x, 1 - y)]
        me = _chip_of(x, y)

        def copy(k, src, dst, to):
            return pltpu.make_async_remote_copy(src_ref=src, dst_ref=dst, send_sem=send.at[k], recv_sem=recv.at[k],
                                                device_id=to, device_id_type=MESH)

        step1 = [copy(a, big[a].at[:, 1 - c], r1[a], sibling) for a in range(nbig)]
        step1.append(copy(nbig, sm.at[1 - c], r1s, sibling))
        for cp in step1:
            cp.start()
        for a in range(nbig):
            copy(a, r1[a], r1[a], sibling).wait_recv()
            for k in range(N_CHIPS):
                r1[a][k] = big[a][k, c] + r1[a][k]
                wire[a][k] = r1[a][k].astype(wire[a].dtype)
        copy(nbig, r1s, r1s, sibling).wait_recv()
        r1s[...] = sm[c] + r1s[...]

        step2 = []
        for j, chip in enumerate(chips):
            k = _chip_of(*chip)
            for a in range(nbig):
                step2.append(copy(4 + j * 4 + a, wire[a].at[k], r2[a].at[j], (*chip, c)))
            step2.append(copy(4 + j * 4 + nbig, r1s, r2s.at[j], (*chip, c)))
        for cp in step2:
            cp.start()
        for j in range(3):
            for a in range(nbig):
                copy(4 + j * 4 + a, r2[a].at[j], r2[a].at[j], sibling).wait_recv()
            copy(4 + j * 4 + nbig, r2s.at[j], r2s.at[j], sibling).wait_recv()
        for a in range(nbig):
            tot = r1[a][me]
            for j in range(3):
                tot = tot + r2[a][j].astype(F32)
            outs[a][c] = tot
        ids = [_chip_of(*chip) for chip in chips]
        tot = None
        for k in range(N_CHIPS):
            term = jnp.where(k == me, r1s[...],
                             jnp.where(k == ids[0], r2s[0], jnp.where(k == ids[1], r2s[1], r2s[2])))
            tot = term if tot is None else tot + term
        osm[c] = tot

        step3 = [copy(16 + a, outs[a].at[c], outs[a].at[c], sibling) for a in range(nbig)]
        step3.append(copy(16 + nbig, osm.at[c], osm.at[c], sibling))
        for cp in step3:
            cp.start()
        for a in range(nbig):
            other = outs[a].at[1 - c]
            copy(16 + a, other, other, sibling).wait_recv()
        other = osm.at[1 - c]
        copy(16 + nbig, other, other, sibling).wait_recv()
        for cp in step1 + step2 + step3:
            cp.wait_send()

    vm = pl.BlockSpec(memory_space=pltpu.VMEM)
    half = [b.shape[2:] for b in bigs]
    out_shape = tuple(jax.ShapeDtypeStruct((2,) + h, F32) for h in half) + (jax.ShapeDtypeStruct(g_small.shape, F32),)
    sm_half = g_small.shape[1:]
    scratch = ([pltpu.VMEM((N_CHIPS,) + h, F32) for h in half] + [pltpu.VMEM(sm_half, F32)]
               + [pltpu.VMEM((N_CHIPS,) + h, _WIRE) for h in half]
               + [pltpu.VMEM((3,) + h, _WIRE) for h in half] + [pltpu.VMEM((3,) + sm_half, F32)]
               + [pltpu.SemaphoreType.DMA((20,)), pltpu.SemaphoreType.DMA((20,))])
    return pl.pallas_call(
        body, name="reduce_grads", out_shape=out_shape, in_specs=[vm] * 4, out_specs=(vm,) * 4,
        scratch_shapes=scratch, compiler_params=pltpu.CompilerParams(vmem_limit_bytes=VMEM_LIMIT),
    )(g_in, g_out, g_kv, g_small)


def adamw(w, g, m, v, name):
    rows_, cols = w.shape
    tr = rows_ if rows_ <= 576 else 256
    assert rows_ % tr == 0

    def body(w_ref, g_ref, m_ref, v_ref, d_ref, nm_ref, nv_ref):
        gv = g_ref[...]
        nm = ADAM_B1 * m_ref[...] + (1.0 - ADAM_B1) * gv
        nv = ADAM_B2 * v_ref[...] + (1.0 - ADAM_B2) * (gv * gv)
        m_hat = nm / (1.0 - ADAM_B1 ** ADAM_STEP)
        v_hat = nv / (1.0 - ADAM_B2 ** ADAM_STEP)
        d_ref[...] = (-ADAM_LR) * (m_hat / (jnp.sqrt(v_hat) + ADAM_EPS) + ADAM_WD * w_ref[...])
        nm_ref[...] = nm
        nv_ref[...] = nv

    spec = pl.BlockSpec((tr, cols), lambda i: (i, 0))
    shp = jax.ShapeDtypeStruct(w.shape, F32)
    return pl.pallas_call(
        body, name=name, grid=(rows_ // tr,), out_shape=(shp, shp, shp), in_specs=[spec] * 4, out_specs=(spec,) * 3,
        compiler_params=pltpu.CompilerParams(dimension_semantics=("arbitrary",)),
    )(w, g, m, v)


SMALL = (("norm_g", 1024), ("mem_norm_g", 1024), ("conv_w", 512), ("conv_b", 512), ("w_rg", 32768), ("b_rg", 512),
         ("w_ig", 32768), ("b_ig", 512), ("lru_lambda", 512), ("q_norm_g", 128), ("k_norm_g", 128), ("sinks", 128),
         ("xq_norm_g", 128), ("xk_norm_g", 128), ("out_norm_g", 1024))
SMALL_ROWS = 576


def _pack(parts, rows_):
    flat = jnp.concatenate([p.reshape(-1) for p in parts])
    return jnp.pad(flat, (0, rows_ * 128 - flat.shape[0])).reshape(rows_, 128)


def _pad_to(v, n):
    v = v.reshape(-1)
    return jnp.pad(v, (0, n - v.shape[0]))


def _block_diag_gates(w_rg, w_ig):
    def bd(w4):
        z = jnp.zeros((4, HEAD, 4, HEAD), w4.dtype)
        idx = jnp.arange(4)
        return z.at[idx, :, idx, :].set(w4).reshape(256, 256)

    return jnp.stack([jnp.concatenate([bd(w_rg[4 * h:4 * h + 4]), bd(w_ig[4 * h:4 * h + 4])], axis=1) for h in (0, 1)])


def _diag_blocks(g):
    out = []
    for part in (0, 1):
        blocks = []
        for h in (0, 1):
            sub = g[h, :, 256 * part:256 * (part + 1)].reshape(4, HEAD, 4, HEAD)
            blocks.append(jnp.stack([sub[n, :, n, :] for n in range(4)]))
        out.append(jnp.concatenate(blocks, axis=0))
    return out


def _rope_tables(seq):
    pos = jnp.arange(seq, dtype=F32)
    inv_freq = ROPE_THETA ** (-(jnp.arange(0, ROPE_DIM, 2, dtype=F32) / ROPE_DIM))
    ang = pos[:, None] * inv_freq[None, :]
    cos, sin = jnp.cos(ang), jnp.sin(ang)
    z = lambda n: jnp.zeros((seq, n), F32)
    c64 = jnp.concatenate([cos, cos, jnp.ones((seq, HEAD - ROPE_DIM), F32)], axis=1)
    s1_64 = jnp.concatenate([-sin, z(HEAD - 8)], axis=1)
    s2_64 = jnp.concatenate([z(8), sin, z(HEAD - ROPE_DIM)], axis=1)
    return tuple(jnp.concatenate([t, t], axis=1) for t in (c64, s1_64, s2_64))


def kernel(x, mem, norm_g, mem_norm_g, w_in, conv_w, conv_b, w_rg, b_rg, w_ig, b_ig, lru_lambda, q_norm_g, k_norm_g, sinks, w_mem_kv, xq_norm_g, xk_norm_g, out_norm_g, w_out, loss_target, m_norm_g, m_mem_norm_g, m_w_in, m_conv_w, m_conv_b, m_w_rg, m_b_rg, m_w_ig, m_b_ig, m_lru_lambda, m_q_norm_g, m_k_norm_g, m_sinks, m_w_mem_kv, m_xq_norm_g, m_xk_norm_g, m_out_norm_g, m_w_out, v_norm_g, v_mem_norm_g, v_w_in, v_conv_w, v_conv_b, v_w_rg, v_b_rg, v_w_ig, v_b_ig, v_lru_lambda, v_q_norm_g, v_k_norm_g, v_sinks, v_w_mem_kv, v_xq_norm_g, v_xk_norm_g, v_out_norm_g, v_w_out):
    seq = x.shape[1]
    chip = 2 * lax.axis_index("x") + lax.axis_index("y")
    xs, tgt, mems = x[0], loss_target[0], mem[0]

    win_t_sh = w_in[0].T.astype(_MXU)
    cw_sh = jnp.pad(conv_w[0], ((0, 4), (0, 0)))
    win_t, wout, wkv, cw_all = gather_weights(win_t_sh, w_out[0].astype(_MXU), w_mem_kv[0].astype(_MXU), cw_sh)
    cw = cw_all.reshape(N_CHIPS, 8, 128)[:, :CONV_K].transpose(1, 0, 2).reshape(CONV_K, LRU_W)

    rc, rs1, rs2 = _rope_tables(seq)
    wg = _block_diag_gates(w_rg[0], w_ig[0]).astype(_MXU)
    qg = jnp.tile(q_norm_g, (1, 2))
    kg = jnp.tile(k_norm_g, (1, 2))
    xqg = jnp.tile(xq_norm_g, (1, 4))
    xkg = jnp.tile(xk_norm_g, (1, 4))

    km, vm = mem_fwd(mems, mem_norm_g, wkv, xkg)
    proj, ya, yb, yc, ycat, xn, dout, loss8 = layer_fwd(
        xs, tgt, rc, rs1, rs2, norm_g, win_t, cw, conv_b, wg, b_rg, b_ig, lru_lambda, qg, kg, xqg, sinks, km, vm,
        out_norm_g, wout)
    g_wout = wgrad_out(ycat, dout)
    (gx, dproj, g_wg, dkm, dvm, g_ng, g_og, g_cb, g_brg, g_big, g_lam, g_cw, g_qn, g_kn, g_xqn, g_sink) = layer_bwd(
        xs, dout, proj, ya, yb, yc, rc, rs1, rs2, norm_g, win_t, cw, conv_b, wg, b_rg, b_ig, lru_lambda, qg, kg, xqg,
        sinks, km, vm, out_norm_g, wout)
    g_win_t = wgrad_in(dproj, xn)
    g_wkv, g_mng, g_xkn = mem_bwd(mems, mem_norm_g, wkv, xkg, dkm, dvm)

    g_wrg, g_wig = _diag_blocks(g_wg)
    fold = lambda v, n: v.reshape(n, HEAD).sum(axis=0)
    small_g = _pack([g_ng, g_mng, g_cw, g_cb, g_wrg, g_brg, g_wig, g_big, g_lam, _pad_to(fold(g_qn, 2), 128),
                     _pad_to(fold(g_kn, 2), 128), g_sink, _pad_to(fold(g_xqn, 4), 128), _pad_to(fold(g_xkn, 4), 128),
                     g_og], SMALL_ROWS)
    r_in, r_out, r_kv, r_small = reduce_grads(
        g_win_t.reshape(N_CHIPS, 2, D_IN // 8, D_MODEL), g_wout.reshape(N_CHIPS, 2, D_MODEL // 8, D_MODEL),
        g_wkv.reshape(N_CHIPS, 2, D_MODEL // 8, 2 * XATT_W), small_g.reshape(2, SMALL_ROWS // 2, 128))
    loss = lax.psum(loss8[0, 0], ("x", "y", "c"))

    flat = r_small.reshape(-1)
    sizes = (1024, 1024, 2048, 512, 32768, 512, 32768, 512, 512, 128, 128, 128, 128, 128, 1024)
    offs = [0]
    for s in sizes:
        offs.append(offs[-1] + s)
    piece = {name: flat[offs[k]:offs[k + 1]] for k, (name, _) in enumerate(SMALL)}
    g_cw_mine = lax.dynamic_slice(piece["conv_w"].reshape(CONV_K, LRU_W), (0, chip * 128), (CONV_K, 128))
    grads = {
        "norm_g": piece["norm_g"].reshape(1, 1024), "mem_norm_g": piece["mem_norm_g"].reshape(1, 1024),
        "w_in": r_in.reshape(D_IN // 4, D_MODEL).T[None], "conv_w": g_cw_mine[None],
        "conv_b": piece["conv_b"].reshape(1, 512), "w_rg": piece["w_rg"].reshape(1, 8, HEAD, HEAD),
        "b_rg": piece["b_rg"].reshape(1, 512), "w_ig": piece["w_ig"].reshape(1, 8, HEAD, HEAD),
        "b_ig": piece["b_ig"].reshape(1, 512), "lru_lambda": piece["lru_lambda"].reshape(1, 512),
        "q_norm_g": piece["q_norm_g"][:HEAD].reshape(1, HEAD), "k_norm_g": piece["k_norm_g"][:HEAD].reshape(1, HEAD),
        "sinks": piece["sinks"][:4].reshape(1, 4), "w_mem_kv": r_kv.reshape(D_MODEL // 4, 2 * XATT_W)[None],
        "xq_norm_g": piece["xq_norm_g"][:HEAD].reshape(1, HEAD), "xk_norm_g": piece["xk_norm_g"][:HEAD].reshape(1, HEAD),
        "out_norm_g": piece["out_norm_g"].reshape(1, 1024), "w_out": r_out.reshape(D_MODEL // 4, D_MODEL)[None],
    }
    weights = dict(norm_g=norm_g, mem_norm_g=mem_norm_g, w_in=w_in, conv_w=conv_w, conv_b=conv_b, w_rg=w_rg, b_rg=b_rg,
                   w_ig=w_ig, b_ig=b_ig, lru_lambda=lru_lambda, q_norm_g=q_norm_g, k_norm_g=k_norm_g, sinks=sinks,
                   w_mem_kv=w_mem_kv, xq_norm_g=xq_norm_g, xk_norm_g=xk_norm_g, out_norm_g=out_norm_g, w_out=w_out)
    ms = dict(norm_g=m_norm_g, mem_norm_g=m_mem_norm_g, w_in=m_w_in, conv_w=m_conv_w, conv_b=m_conv_b, w_rg=m_w_rg,
              b_rg=m_b_rg, w_ig=m_w_ig, b_ig=m_b_ig, lru_lambda=m_lru_lambda, q_norm_g=m_q_norm_g, k_norm_g=m_k_norm_g,
              sinks=m_sinks, w_mem_kv=m_w_mem_kv, xq_norm_g=m_xq_norm_g, xk_norm_g=m_xk_norm_g,
              out_norm_g=m_out_norm_g, w_out=m_w_out)
    vs = dict(norm_g=v_norm_g, mem_norm_g=v_mem_norm_g, w_in=v_w_in, conv_w=v_conv_w, conv_b=v_conv_b, w_rg=v_w_rg,
              b_rg=v_b_rg, w_ig=v_w_ig, b_ig=v_b_ig, lru_lambda=v_lru_lambda, q_norm_g=v_q_norm_g, k_norm_g=v_k_norm_g,
              sinks=v_sinks, w_mem_kv=v_w_mem_kv, xq_norm_g=v_xq_norm_g, xk_norm_g=v_xk_norm_g,
              out_norm_g=v_out_norm_g, w_out=v_w_out)

    delta, new_m, new_v = {}, {}, {}
    for name in ("w_in", "w_mem_kv", "w_out"):
        shp = weights[name].shape
        d2, m2, v2 = adamw(weights[name][0], grads[name][0], ms[name][0], vs[name][0], "adamw_" + name)
        delta[name], new_m[name], new_v[name] = d2.reshape(shp), m2.reshape(shp), v2.reshape(shp)
    small_names = [n for n, _ in SMALL]
    packs = [_pack([_pad_to(d[n], sz) for n, sz in SMALL], SMALL_ROWS) for d in (weights, grads, ms, vs)]
    d_p, m_p, v_p = adamw(*packs, "adamw_small")
    offs2 = [0]
    for _, sz in SMALL:
        offs2.append(offs2[-1] + sz)
    for out_d, pk in ((delta, d_p), (new_m, m_p), (new_v, v_p)):
        fl = pk.reshape(-1)
        for k, n in enumerate(small_names):
            shp = weights[n].shape
            out_d[n] = fl[offs2[k]:offs2[k] + math.prod(shp)].reshape(shp)

    order = ("norm_g", "mem_norm_g", "w_in", "conv_w", "conv_b", "w_rg", "b_rg", "w_ig", "b_ig", "lru_lambda",
             "q_norm_g", "k_norm_g", "sinks", "w_mem_kv", "xq_norm_g", "xk_norm_g", "out_norm_g", "w_out")
    return (loss, gx[None], *[grads[n] for n in order], *[delta[n] for n in order], *[new_m[n] for n in order],
            *[new_v[n] for n in order])
```

```python
import functools
import math

import jax
import jax.numpy as jnp
import numpy as np
from jax import lax
from jax.experimental import pallas as pl
from jax.experimental.pallas import tpu as pltpu

F32 = jnp.float32
_MXU = jnp.bfloat16
_WIRE = jnp.bfloat16

D_MODEL = 1024
MEM_LEN = 256
HEAD = 64
LRU_W = 512
LRU_BLOCKS = 8
CONV_K = 4
LRU_C = 8.0
SWA_W = 256
KV_W = 128
XATT_W = 256
BLOCK = 128
D_IN = 2304
ROPE_THETA = 500000.0
ROPE_DIM = 16
EPS = 1e-6
NEG_INF = -1e30
C_LRUX, C_LRUG, C_SQ, C_SK, C_SV, C_SWAG, C_XQ, C_XG = 0, 512, 1024, 1280, 1408, 1536, 1792, 2048

ADAM_LR, ADAM_B1, ADAM_B2, ADAM_EPS, ADAM_WD, ADAM_STEP = 0.001, 0.9, 0.999, 1e-08, 0.01, 10

N_CHIPS = 4
ROW_TILE = 256
VMEM_LIMIT = 56 * 1024 * 1024
ADAM_BLOCK_BYTES = 1280 * 1024
MESH = pl.DeviceIdType.MESH


def _mm(a, b):
    return jnp.dot(a.astype(_MXU), b.astype(_MXU), preferred_element_type=F32)


def _mm_nt(a, b):
    return lax.dot_general(a.astype(_MXU), b.astype(_MXU), (((1,), (1,)), ((), ())), preferred_element_type=F32)


def _mm_tn(a, b):
    return lax.dot_general(a.astype(_MXU), b.astype(_MXU), (((0,), (0,)), ((), ())), preferred_element_type=F32)


def _group_matrix(width):
    r = lax.shift_right_logical(lax.broadcasted_iota(jnp.int32, (width, width), 0), 6)
    c = lax.shift_right_logical(lax.broadcasted_iota(jnp.int32, (width, width), 1), 6)
    return (r == c).astype(_MXU)


def _seg_mean(x, gm):
    hi = x.astype(_MXU)
    lo = (x - hi.astype(F32)).astype(_MXU)
    s = jnp.dot(hi, gm, preferred_element_type=F32) + jnp.dot(lo, gm, preferred_element_type=F32)
    return s * (1.0 / HEAD)


def _row_mean(x):
    return jnp.mean(x, axis=-1, keepdims=True)


def _col_sum(x):
    return jnp.sum(x, axis=0, keepdims=True)


def _sigmoid(x):
    return jax.nn.sigmoid(x)


def _softplus(z):
    e = jnp.exp(-jnp.abs(z))
    u = 1.0 + e
    log1p_e = jnp.where(u == 1.0, e, jnp.log(u) * (e / (u - 1.0)))
    return jnp.maximum(z, 0.0) + log1p_e


def _rope(t, c, s1, s2):
    return t * c + pltpu.roll(t, 120, 1) * s1 + pltpu.roll(t, 8, 1) * s2


def _rope_bwd(d, c, s1, s2):
    return d * c + pltpu.roll(d * s1, 8, 1) + pltpu.roll(d * s2, 120, 1)


def _lane_mask(width, lo, hi):
    lane = lax.broadcasted_iota(jnp.int32, (1, width), 1)
    return ((lane >= lo) & (lane < hi)).astype(F32)


def _swa_mask(first_block):
    qi = lax.broadcasted_iota(jnp.int32, (2 * BLOCK, 2 * BLOCK), 0) & (BLOCK - 1)
    kj = lax.broadcasted_iota(jnp.int32, (2 * BLOCK, 2 * BLOCK), 1)
    rel = qi + BLOCK - kj
    ok = (rel >= 0) & (rel < BLOCK)
    return ok & (jnp.logical_not(first_block) | (kj >= BLOCK))


def _sink_col(sink_ref, h):
    row = lax.broadcasted_iota(jnp.int32, (2 * BLOCK, 1), 0)
    return jnp.where(row < BLOCK, sink_ref[0, 2 * h], sink_ref[0, 2 * h + 1])


def _stack_heads(t128, h, scale):
    m = _lane_mask(KV_W, HEAD * h, HEAD * (h + 1))
    rolled = pltpu.roll(t128, HEAD, 1)
    even, odd = (t128, rolled) if h == 0 else (rolled, t128)
    return jnp.concatenate([even * m, odd * m], axis=0) * scale


def _unstack_heads(s256, h):
    m = _lane_mask(KV_W, HEAD * h, HEAD * (h + 1))
    even, odd = s256[:BLOCK] * m, s256[BLOCK:] * m
    if h == 0:
        return even + pltpu.roll(odd, HEAD, 1)
    return pltpu.roll(even, HEAD, 1) + odd


def _swa_probs(qs, kb, mask, sinkc):
    s = _mm_nt(qs, kb)
    s = jnp.where(mask, s, NEG_INF)
    m = jnp.maximum(jnp.max(s, axis=-1, keepdims=True), sinkc)
    p = jnp.exp(s - m)
    esink = jnp.exp(sinkc - m)
    inv = 1.0 / (jnp.sum(p, axis=-1, keepdims=True) + esink)
    return p * inv, esink * inv


def _lru_gates(xc, wg_ref, brg, big, lam):
    p0 = _mm(xc[:, :256], wg_ref[0])
    p1 = _mm(xc[:, 256:], wg_ref[1])
    rg = _sigmoid(jnp.concatenate([p0[:, :256], p1[:, :256]], axis=1) + brg)
    ig = _sigmoid(jnp.concatenate([p0[:, 256:], p1[:, 256:]], axis=1) + big)
    sp = _softplus(-lam)
    la = (-LRU_C) * rg * sp
    a = jnp.exp(la)
    th = jnp.tanh(la)
    one_minus_a2 = (-2.0 * th) / (1.0 - th)
    return rg, ig, sp, a, jnp.sqrt(one_minus_a2)


def _const_spec(shape, single=False):
    zeros = (0,) * len(shape)
    if single:
        return pl.BlockSpec(shape, lambda i: zeros, pipeline_mode=pl.Buffered(1))
    return pl.BlockSpec(shape, lambda i: zeros)


def _chip_of(x, y):
    return 2 * x + y


def gather_weights(win_t, wout, wkv, convw):
    arrs = (win_t, wout, wkv)
    n = len(arrs)

    def body(a0, a1, a2, cw, o0, o1, o2, ocw, send, recv, lsem):
        ins, outs = (a0, a1, a2), (o0, o1, o2)
        x, y, c = lax.axis_index("x"), lax.axis_index("y"), lax.axis_index("c")
        sibling = (x, y, 1 - c)
        chips = [(1 - x, y), (x, 1 - y), (1 - x, 1 - y)]
        me = _chip_of(x, y)

        def rows(a, chip, half):
            r = ins[a].shape[0]
            return pl.ds(pl.multiple_of(chip * r + half * (r // 2), 16), r // 2)

        def own_half(a, half):
            r = ins[a].shape[0]
            return ins[a].at[pl.ds(pl.multiple_of(half * (r // 2), 16), r // 2)]

        def copy(k, src, dst, to):
            return pltpu.make_async_remote_copy(src_ref=src, dst_ref=dst, send_sem=send.at[k], recv_sem=recv.at[k],
                                                device_id=to, device_id_type=MESH)

        locals_ = []
        for a in range(n):
            r = ins[a].shape[0]
            locals_.append(pltpu.make_async_copy(ins[a], outs[a].at[pl.ds(pl.multiple_of(me * r, 16), r)], lsem.at[a]))
        locals_.append(pltpu.make_async_copy(cw, ocw.at[pl.ds(pl.multiple_of(me * 8, 8), 8)], lsem.at[n]))
        for cp in locals_:
            cp.start()

        first, passed = [], []
        for a in range(n):
            for j, chip in enumerate(chips):
                first.append(copy(a * 6 + j, own_half(a, c), outs[a].at[rows(a, me, c)], (*chip, c)))
        for j, chip in enumerate(chips):
            first.append(copy(n * 6 + j, cw, ocw.at[pl.ds(pl.multiple_of(me * 8, 8), 8)], (*chip, c)))
        for cp in first:
            cp.start()
        for a in range(n):
            for j, chip in enumerate(chips):
                got = outs[a].at[rows(a, _chip_of(*chip), c)]
                copy(a * 6 + j, got, got, (*chip, c)).wait_recv()
                fwd = copy(a * 6 + 3 + j, got, got, sibling)
                fwd.start()
                passed.append(fwd)
        for a in range(n):
            for j, chip in enumerate(chips):
                got = outs[a].at[rows(a, _chip_of(*chip), 1 - c)]
                copy(a * 6 + 3 + j, got, got, sibling).wait_recv()
        for j, chip in enumerate(chips):
            got = ocw.at[pl.ds(pl.multiple_of(_chip_of(*chip) * 8, 8), 8)]
            copy(n * 6 + j, got, got, (*chip, c)).wait_recv()
        for cp in first + passed:
            cp.wait_send()
        for cp in locals_:
            cp.wait()

    vm = pl.BlockSpec(memory_space=pltpu.VMEM)
    out_shape = tuple(jax.ShapeDtypeStruct((N_CHIPS * a.shape[0],) + a.shape[1:], a.dtype) for a in arrs) + (
        jax.ShapeDtypeStruct((N_CHIPS * 8, 128), F32),)
    n_rdma = n * 6 + 3
    return pl.pallas_call(
        body, name="gather_weights", out_shape=out_shape,
        in_specs=[vm] * 4, out_specs=(vm,) * 4,
        scratch_shapes=[pltpu.SemaphoreType.DMA((n_rdma,)), pltpu.SemaphoreType.DMA((n_rdma,)),
                        pltpu.SemaphoreType.DMA((n + 1,))],
        compiler_params=pltpu.CompilerParams(vmem_limit_bytes=VMEM_LIMIT),
    )(win_t, wout, wkv, convw)


def mem_fwd(mem, mem_g, wkv, xk_g):
    def body(mem_ref, g_ref, w_ref, xk_ref, km_ref, vm_ref):
        mem_v = mem_ref[...]
        mn = mem_v * lax.rsqrt(_row_mean(mem_v * mem_v) + EPS) * g_ref[...]
        mkv = _mm(mn, w_ref[...])
        kpre = mkv[:, :XATT_W]
        gm = _group_matrix(XATT_W)
        km_ref[...] = kpre * lax.rsqrt(_seg_mean(kpre * kpre, gm) + EPS) * xk_ref[...]
        vm_ref[...] = mkv[:, XATT_W:]

    vm = pl.BlockSpec(memory_space=pltpu.VMEM)
    return pl.pallas_call(
        body, name="mem_fwd",
        out_shape=(jax.ShapeDtypeStruct((MEM_LEN, XATT_W), F32), jax.ShapeDtypeStruct((MEM_LEN, XATT_W), F32)),
        in_specs=[vm] * 4, out_specs=(vm, vm),
    )(mem, mem_g, wkv, xk_g)


def mem_bwd(mem, mem_g, wkv, xk_g, dkm, dvm):
    def body(mem_ref, g_ref, w_ref, xk_ref, dkm_ref, dvm_ref, gw_ref, gg_ref, gxk_ref):
        mem_v = mem_ref[...]
        mh = mem_v * lax.rsqrt(_row_mean(mem_v * mem_v) + EPS)
        mn = mh * g_ref[...]
        mkv = _mm(mn, w_ref[...])
        kpre = mkv[:, :XATT_W]
        gm = _group_matrix(XATT_W)
        rk = lax.rsqrt(_seg_mean(kpre * kpre, gm) + EPS)
        kn = kpre * rk
        dk = dkm_ref[...]
        gxk_ref[...] = _col_sum(dk * kn)
        dkn = dk * xk_ref[...]
        dkpre = rk * (dkn - kn * _seg_mean(dkn * kn, gm))
        dmkv = jnp.concatenate([dkpre, dvm_ref[...]], axis=1)
        gw_ref[...] = _mm_tn(mn, dmkv)
        dmn = _mm_nt(dmkv, w_ref[...])
        gg_ref[...] = _col_sum(dmn * mh)

    vm = pl.BlockSpec(memory_space=pltpu.VMEM)
    return pl.pallas_call(
        body, name="mem_bwd",
        out_shape=(jax.ShapeDtypeStruct((D_MODEL, 2 * XATT_W), F32), jax.ShapeDtypeStruct((1, D_MODEL), F32),
                   jax.ShapeDtypeStruct((1, XATT_W), F32)),
        in_specs=[vm] * 6, out_specs=(vm, vm, vm),
    )(mem, mem_g, wkv, xk_g, dkm, dvm)


def layer_fwd(x, tgt, rc, rs1, rs2, ng, win_t, cw, cb, wg, brg, big, lam, qg, kg, xqg, sinks, km, vm, og, wout):
    seq = x.shape[0]
    tm = min(ROW_TILE, seq)
    nt = seq // tm
    nb = tm // BLOCK

    def body(x_ref, t_ref, c_ref, s1_ref, s2_ref, ng_ref, win_ref, cw_ref, cb_ref, wg_ref, brg_ref, big_ref, lam_ref,
             qg_ref, kg_ref, xqg_ref, sink_ref, km_ref, vm_ref, og_ref, wout_ref,
             proj_ref, ya_ref, yb_ref, yc_ref, ycat_ref, xn_ref, dout_ref, loss_ref,
             ext_ref, a_scr, b_scr, hc_ref, kp_ref, vp_ref, lacc_ref):
        i = pl.program_id(0)

        @pl.when(i == 0)
        def _():
            ext_ref[0:8, :] = jnp.zeros((8, LRU_W), F32)
            hc_ref[...] = jnp.zeros_like(hc_ref)
            kp_ref[...] = jnp.zeros_like(kp_ref)
            vp_ref[...] = jnp.zeros_like(vp_ref)
            lacc_ref[...] = jnp.zeros_like(lacc_ref)

        xv = x_ref[...]
        xn = (xv * lax.rsqrt(_row_mean(xv * xv) + EPS) * ng_ref[...]).astype(_MXU)
        xn_ref[...] = xn.astype(xn_ref.dtype)
        proj_ref[...] = _mm_nt(xn, win_ref[...])

        u = proj_ref[:, C_LRUX:C_LRUX + LRU_W]
        ext_ref[8:8 + tm, :] = u
        xc = cb_ref[...]
        for k in range(CONV_K):
            xc = xc + cw_ref[k:k + 1, :] * ext_ref[pl.ds(5 + k, tm), :]
        ext_ref[0:8, :] = u[tm - 8:tm, :]
        rg, ig, sp, a, sq = _lru_gates(xc, wg_ref, brg_ref[...], big_ref[...], lam_ref[...])
        a_scr[...] = a
        b_scr[...] = sq * (ig * xc)
        row8 = lax.broadcasted_iota(jnp.int32, (8, LRU_W), 0)

        def scan_step(g, carry):
            r0 = pl.multiple_of(g * 8, 8)
            av = a_scr[pl.ds(r0, 8), :]
            bv = b_scr[pl.ds(r0, 8), :]
            for d in (1, 2, 4):
                a_sh = jnp.where(row8 >= d, pltpu.roll(av, d, 0), 1.0)
                b_sh = jnp.where(row8 >= d, pltpu.roll(bv, d, 0), 0.0)
                bv = bv + av * b_sh
                av = av * a_sh
            hv = bv + av * carry
            ya_ref[pl.ds(r0, 8), :] = hv
            return hv[7:8, :]

        hc_ref[0:1, :] = lax.fori_loop(0, tm // 8, scan_step, hc_ref[0:1, :])

        gm128 = _group_matrix(KV_W)
        cv, s1v, s2v = c_ref[...], s1_ref[...], s2_ref[...]

        def head_norm_rope(t, g):
            n = t * lax.rsqrt(_seg_mean(t * t, gm128) + EPS)
            return _rope(n * g, cv, s1v, s2v)

        qs_ = (head_norm_rope(proj_ref[:, C_SQ:C_SQ + 128], qg_ref[...]),
               head_norm_rope(proj_ref[:, C_SQ + 128:C_SQ + 256], qg_ref[...]))
        kr = head_norm_rope(proj_ref[:, C_SK:C_SK + KV_W], kg_ref[...])
        sv = proj_ref[:, C_SV:C_SV + KV_W]
        kext = jnp.concatenate([kp_ref[...], kr], axis=0).astype(_MXU)
        vext = jnp.concatenate([vp_ref[...], sv], axis=0).astype(_MXU)
        kp_ref[...] = kr[tm - BLOCK:tm, :]
        vp_ref[...] = sv[tm - BLOCK:tm, :]
        for b in range(nb):
            mask = _swa_mask((i == 0) & (b == 0)) if b == 0 else _swa_mask(False)
            kb = kext[BLOCK * b:BLOCK * b + 2 * BLOCK]
            vb = vext[BLOCK * b:BLOCK * b + 2 * BLOCK]
            for h in range(2):
                qstk = _stack_heads(qs_[h][BLOCK * b:BLOCK * (b + 1)], h, 0.125)
                p, _ = _swa_probs(qstk, kb, mask, _sink_col(sink_ref, h))
                o = _mm(p, vb)
                yb_ref[BLOCK * b:BLOCK * (b + 1), KV_W * h:KV_W * (h + 1)] = _unstack_heads(o, h)

        gm256 = _group_matrix(XATT_W)
        xq = proj_ref[:, C_XQ:C_XQ + XATT_W]
        qx = xq * lax.rsqrt(_seg_mean(xq * xq, gm256) + EPS) * xqg_ref[...]
        kmv = km_ref[...].astype(_MXU)
        vmv = vm_ref[...].astype(_MXU)
        yc = jnp.zeros((tm, XATT_W), F32)
        for j in range(4):
            mj = _lane_mask(XATT_W, HEAD * j, HEAD * (j + 1))
            s = _mm_nt(qx * (mj * 0.125), kmv)
            p = jnp.exp(s - jnp.max(s, axis=-1, keepdims=True))
            p = p * (1.0 / jnp.sum(p, axis=-1, keepdims=True))
            yc = yc + _mm(p, vmv) * mj
        yc_ref[...] = yc

        def gated(y, g, gate):
            return y * lax.rsqrt(_row_mean(y * y) + EPS) * g * (gate * _sigmoid(gate))

        ogv = og_ref[...]
        za = gated(ya_ref[...], ogv[:, :512], proj_ref[:, C_LRUG:C_LRUG + LRU_W])
        zb = gated(yb_ref[...], ogv[:, 512:768], proj_ref[:, C_SWAG:C_SWAG + SWA_W])
        zc = gated(yc, ogv[:, 768:], proj_ref[:, C_XG:C_XG + XATT_W])
        ycat_ref[:, 0:512] = za.astype(ycat_ref.dtype)
        ycat_ref[:, 512:768] = zb.astype(ycat_ref.dtype)
        ycat_ref[:, 768:1024] = zc.astype(ycat_ref.dtype)
        out = xv + _mm(ycat_ref[...], wout_ref[...])
        err = out - t_ref[...]
        dout_ref[...] = err * (1.0 / D_MODEL)
        lacc_ref[...] = lacc_ref[...] + (0.5 / D_MODEL) * jnp.sum(err * err)

        @pl.when(i == nt - 1)
        def _():
            loss_ref[...] = lacc_ref[...]

    def rows(ncol):
        return pl.BlockSpec((tm, ncol), lambda i: (i, 0))

    in_specs = [rows(D_MODEL), rows(D_MODEL), rows(128), rows(128), rows(128),
                _const_spec((1, D_MODEL)), _const_spec((D_IN, D_MODEL), True), _const_spec((CONV_K, LRU_W)),
                _const_spec((1, LRU_W)), _const_spec((2, 256, 512), True), _const_spec((1, LRU_W)),
                _const_spec((1, LRU_W)), _const_spec((1, LRU_W)), _const_spec((1, 128)), _const_spec((1, 128)),
                _const_spec((1, XATT_W)), pl.BlockSpec(memory_space=pltpu.SMEM),
                _const_spec((MEM_LEN, XATT_W), True), _const_spec((MEM_LEN, XATT_W), True),
                _const_spec((1, D_MODEL)), _const_spec((D_MODEL, D_MODEL), True)]
    out_shape = (jax.ShapeDtypeStruct((seq, D_IN), F32), jax.ShapeDtypeStruct((seq, LRU_W), F32),
                 jax.ShapeDtypeStruct((seq, SWA_W), F32), jax.ShapeDtypeStruct((seq, XATT_W), F32),
                 jax.ShapeDtypeStruct((seq, D_MODEL), _MXU), jax.ShapeDtypeStruct((seq, D_MODEL), _MXU),
                 jax.ShapeDtypeStruct((seq, D_MODEL), F32), jax.ShapeDtypeStruct((8, 128), F32))
    out_specs = (rows(D_IN), rows(LRU_W), rows(SWA_W), rows(XATT_W), rows(D_MODEL), rows(D_MODEL), rows(D_MODEL),
                 _const_spec((8, 128)))
    scratch = [pltpu.VMEM((tm + 8, LRU_W), F32), pltpu.VMEM((tm, LRU_W), F32), pltpu.VMEM((tm, LRU_W), F32),
               pltpu.VMEM((8, LRU_W), F32), pltpu.VMEM((BLOCK, KV_W), F32), pltpu.VMEM((BLOCK, KV_W), F32),
               pltpu.VMEM((8, 128), F32)]
    return pl.pallas_call(
        body, name="layer_fwd", grid=(nt,), out_shape=out_shape, in_specs=in_specs, out_specs=out_specs,
        scratch_shapes=scratch,
        compiler_params=pltpu.CompilerParams(dimension_semantics=("arbitrary",), vmem_limit_bytes=VMEM_LIMIT),
    )(x, tgt, rc, rs1, rs2, ng, win_t, cw, cb, wg, brg, big, lam, qg, kg, xqg, sinks, km, vm, og, wout)


def wgrad_out(ycat, dout):
    seq = ycat.shape[0]
    tk = min(512, seq)
    nk = seq // tk

    def body(y_ref, d_ref, o_ref):
        @pl.when(pl.program_id(0) == 0)
        def _():
            o_ref[...] = jnp.zeros_like(o_ref)

        o_ref[...] += _mm_tn(y_ref[...], d_ref[...])

    return pl.pallas_call(
        body, name="wgrad_out", grid=(nk,), out_shape=jax.ShapeDtypeStruct((D_MODEL, D_MODEL), F32),
        in_specs=[pl.BlockSpec((tk, D_MODEL), lambda k: (k, 0)), pl.BlockSpec((tk, D_MODEL), lambda k: (k, 0))],
        out_specs=pl.BlockSpec((D_MODEL, D_MODEL), lambda k: (0, 0)),
        compiler_params=pltpu.CompilerParams(dimension_semantics=("arbitrary",), vmem_limit_bytes=VMEM_LIMIT),
    )(ycat, dout)


def wgrad_in(dproj, xn):
    seq = xn.shape[0]
    nblk = D_IN // 256

    def body(d_ref, x_ref, o_ref):
        o_ref[...] = _mm_tn(d_ref[...], x_ref[...])

    return pl.pallas_call(
        body, name="wgrad_in", grid=(nblk,), out_shape=jax.ShapeDtypeStruct((D_IN, D_MODEL), F32),
        in_specs=[pl.BlockSpec((seq, 256), lambda j: (0, j)), _const_spec((seq, D_MODEL), True)],
        out_specs=pl.BlockSpec((256, D_MODEL), lambda j: (j, 0)),
        compiler_params=pltpu.CompilerParams(dimension_semantics=("arbitrary",), vmem_limit_bytes=VMEM_LIMIT),
    )(dproj, xn)


def layer_bwd(x, dout, proj, ya, yb, yc, rc, rs1, rs2, ng, win_t, cw, cb, wg, brg, big, lam, qg, kg, xqg, sinks, km,
              vm, og, wout):
    seq = x.shape[0]
    tm = min(ROW_TILE, seq)
    nt = seq // tm
    nb = tm // BLOCK

    def body(x_ref, dout_ref, proj_ref, ya_ref, yb_ref, yc_ref, c_ref, s1_ref, s2_ref,
             uh_ref, yah_ref, kvh_ref, ch_ref, s1h_ref, s2h_ref,
             ng_ref, win_ref, cw_ref, cb_ref, wg_ref, brg_ref, big_ref, lam_ref, qg_ref, kg_ref, xqg_ref, sink_ref,
             km_ref, vm_ref, og_ref, wout_ref,
             gx_ref, dproj_ref, gwg_ref, dkm_ref, dvm_ref, gng_ref, gog_ref, gcb_ref, gbrg_ref, gbig_ref, glam_ref,
             gcw_ref, gqn_ref, gkn_ref, gxqn_ref, gsink_ref,
             ext_ref, hext_ref, aext_ref, an_scr, dh_scr, g_scr, dxc_ext, gcar_ref, dkx_ref, dvx_ref, dkcar_ref,
             dvcar_ref, dq_scr):
        i = pl.program_id(0)
        tile = nt - 1 - i
        first_tile = tile == 0

        @pl.when(i == 0)
        def _():
            for r in (gwg_ref, dkm_ref, dvm_ref, gng_ref, gog_ref, gcb_ref, gbrg_ref, gbig_ref, glam_ref, gcw_ref,
                      gqn_ref, gkn_ref, gxqn_ref, gsink_ref, gcar_ref, dkcar_ref, dvcar_ref):
                r[...] = jnp.zeros_like(r)
            dxc_ext[tm:tm + 8, :] = jnp.zeros((8, LRU_W), F32)
            aext_ref[tm:tm + 8, :] = jnp.zeros((8, LRU_W), F32)

        xv = x_ref[...]
        dov = dout_ref[...]
        dz = _mm_nt(dov, wout_ref[...])
        ogv = og_ref[...]

        def group_bwd(y, gate, g, dzg):
            r = lax.rsqrt(_row_mean(y * y) + EPS)
            n = y * r
            sg = _sigmoid(gate)
            dgate = dzg * (n * g) * (sg * (1.0 + gate * (1.0 - sg)))
            dng = dzg * (gate * sg)
            dn = dng * g
            return r * (dn - n * _row_mean(dn * n)), dgate, _col_sum(dng * n)

        dya, dga, goa = group_bwd(ya_ref[...], proj_ref[:, C_LRUG:C_LRUG + LRU_W], ogv[:, :512], dz[:, :512])
        dyb, dgb, gob = group_bwd(yb_ref[...], proj_ref[:, C_SWAG:C_SWAG + SWA_W], ogv[:, 512:768], dz[:, 512:768])
        dyc, dgc, goc = group_bwd(yc_ref[...], proj_ref[:, C_XG:C_XG + XATT_W], ogv[:, 768:], dz[:, 768:])
        gog_ref[...] += jnp.concatenate([goa, gob, goc], axis=1)
        dproj_ref[:, C_LRUG:C_LRUG + LRU_W] = dga.astype(dproj_ref.dtype)
        dproj_ref[:, C_SWAG:C_SWAG + SWA_W] = dgb.astype(dproj_ref.dtype)
        dproj_ref[:, C_XG:C_XG + XATT_W] = dgc.astype(dproj_ref.dtype)

        gm256 = _group_matrix(XATT_W)
        xq = proj_ref[:, C_XQ:C_XQ + XATT_W]
        rq = lax.rsqrt(_seg_mean(xq * xq, gm256) + EPS)
        qn = xq * rq
        qx = qn * xqg_ref[...]
        kmv = km_ref[...].astype(_MXU)
        vmv = vm_ref[...].astype(_MXU)
        dqx = jnp.zeros((tm, XATT_W), F32)
        dkm_t = jnp.zeros((MEM_LEN, XATT_W), F32)
        dvm_t = jnp.zeros((MEM_LEN, XATT_W), F32)
        for j in range(4):
            mj = _lane_mask(XATT_W, HEAD * j, HEAD * (j + 1))
            qm = (qx * (mj * 0.125)).astype(_MXU)
            s = _mm_nt(qm, kmv)
            p = jnp.exp(s - jnp.max(s, axis=-1, keepdims=True))
            p = p * (1.0 / jnp.sum(p, axis=-1, keepdims=True))
            doj = (dyc * mj).astype(_MXU)
            dvm_t = dvm_t + _mm_tn(p, doj)
            dp = _mm_nt(doj, vmv)
            ds = p * (dp - jnp.sum(p * dp, axis=-1, keepdims=True))
            dqx = dqx + _mm(ds, kmv) * (mj * 0.125)
            dkm_t = dkm_t + _mm_tn(ds, qm)
        dkm_ref[...] += dkm_t
        dvm_ref[...] += dvm_t
        gxqn_ref[...] += _col_sum(dqx * qn)
        dqn = dqx * xqg_ref[...]
        dproj_ref[:, C_XQ:C_XQ + XATT_W] = (rq * (dqn - qn * _seg_mean(dqn * qn, gm256))).astype(dproj_ref.dtype)

        gm128 = _group_matrix(KV_W)
        cv, s1v, s2v = c_ref[...], s1_ref[...], s2_ref[...]

        def head_norm(t):
            r = lax.rsqrt(_seg_mean(t * t, gm128) + EPS)
            return t * r, r

        qn_, qr_ = zip(head_norm(proj_ref[:, C_SQ:C_SQ + 128]), head_norm(proj_ref[:, C_SQ + 128:C_SQ + 256]))
        qrope = [_rope(qn_[h] * qg_ref[...], cv, s1v, s2v) for h in range(2)]
        kn, krr = head_norm(proj_ref[:, C_SK:C_SK + KV_W])
        kr = _rope(kn * kg_ref[...], cv, s1v, s2v)
        khn, _ = head_norm(kvh_ref[:, 0:KV_W])
        khr = _rope(khn * kg_ref[...], ch_ref[...], s1h_ref[...], s2h_ref[...])
        kext = jnp.concatenate([khr, kr], axis=0).astype(_MXU)
        vext = jnp.concatenate([kvh_ref[:, KV_W:2 * KV_W], proj_ref[:, C_SV:C_SV + KV_W]], axis=0).astype(_MXU)
        dkx_ref[0:tm, :] = jnp.zeros((tm, KV_W), F32)
        dvx_ref[0:tm, :] = jnp.zeros((tm, KV_W), F32)
        dkx_ref[tm:tm + BLOCK, :] = dkcar_ref[...]
        dvx_ref[tm:tm + BLOCK, :] = dvcar_ref[...]
        lane128 = lax.broadcasted_iota(jnp.int32, (1, 128), 1)
        row256 = lax.broadcasted_iota(jnp.int32, (2 * BLOCK, 1), 0)
        gsink = jnp.zeros((1, 128), F32)
        for b in range(nb):
            mask = _swa_mask(first_tile & (b == 0)) if b == 0 else _swa_mask(False)
            kb = kext[BLOCK * b:BLOCK * b + 2 * BLOCK]
            vb = vext[BLOCK * b:BLOCK * b + 2 * BLOCK]
            for h in range(2):
                qstk = _stack_heads(qrope[h][BLOCK * b:BLOCK * (b + 1)], h, 0.125).astype(_MXU)
                p, psink = _swa_probs(qstk, kb, mask, _sink_col(sink_ref, h))
                do = _stack_heads(dyb[BLOCK * b:BLOCK * (b + 1), KV_W * h:KV_W * (h + 1)], h, 1.0).astype(_MXU)
                dvx_ref[BLOCK * b:BLOCK * b + 2 * BLOCK, :] += _mm_tn(p, do)
                dp = _mm_nt(do, vb)
                delta = jnp.sum(p * dp, axis=-1, keepdims=True)
                ds = p * (dp - delta)
                dsink = -psink * delta
                gsink = gsink + jnp.where(lane128 == 2 * h, jnp.sum(jnp.where(row256 < BLOCK, dsink, 0.0)), 0.0)
                gsink = gsink + jnp.where(lane128 == 2 * h + 1, jnp.sum(jnp.where(row256 >= BLOCK, dsink, 0.0)), 0.0)
                dq_scr[h, BLOCK * b:BLOCK * (b + 1), :] = _unstack_heads(_mm(ds, kb) * 0.125, h)
                dkx_ref[BLOCK * b:BLOCK * b + 2 * BLOCK, :] += _mm_tn(ds, qstk)
        gsink_ref[...] += gsink
        dkcar_ref[...] = dkx_ref[0:BLOCK, :]
        dvcar_ref[...] = dvx_ref[0:BLOCK, :]
        dkg = _rope_bwd(dkx_ref[BLOCK:BLOCK + tm, :], cv, s1v, s2v)
        gkn = _col_sum(dkg * kn)
        dkn = dkg * kg_ref[...]
        dproj_ref[:, C_SK:C_SK + KV_W] = (krr * (dkn - kn * _seg_mean(dkn * kn, gm128))).astype(dproj_ref.dtype)
        dproj_ref[:, C_SV:C_SV + KV_W] = dvx_ref[BLOCK:BLOCK + tm, :].astype(dproj_ref.dtype)
        gqn = jnp.zeros((1, 128), F32)
        for h in range(2):
            dqg = _rope_bwd(dq_scr[h], cv, s1v, s2v)
            gqn = gqn + _col_sum(dqg * qn_[h])
            dqn_ = dqg * qg_ref[...]
            dproj_ref[:, C_SQ + 128 * h:C_SQ + 128 * (h + 1)] = (
                qr_[h] * (dqn_ - qn_[h] * _seg_mean(dqn_ * qn_[h], gm128))).astype(dproj_ref.dtype)
        gqn_ref[...] += gqn
        gkn_ref[...] += gkn

        u = proj_ref[:, C_LRUX:C_LRUX + LRU_W]
        ext_ref[0:8, :] = jnp.where(first_tile, 0.0, uh_ref[...])
        ext_ref[8:8 + tm, :] = u
        us = [ext_ref[pl.ds(5 + k, tm), :] for k in range(CONV_K)]
        xc = cb_ref[...]
        for k in range(CONV_K):
            xc = xc + cw_ref[k:k + 1, :] * us[k]
        rg, ig, sp, a, sq = _lru_gates(xc, wg_ref, brg_ref[...], big_ref[...], lam_ref[...])
        hext_ref[0:8, :] = jnp.where(first_tile, 0.0, yah_ref[...])
        hext_ref[8:8 + tm, :] = ya_ref[...]
        hprev = hext_ref[pl.ds(7, tm), :]
        aext_ref[0:tm, :] = a
        an_scr[...] = aext_ref[pl.ds(1, tm), :]
        dh_scr[...] = dya
        dh_scr[tm - 1:tm, :] = dh_scr[tm - 1:tm, :] + gcar_ref[0:1, :]
        row8 = lax.broadcasted_iota(jnp.int32, (8, LRU_W), 0)

        def scan_step(gi, carry):
            r0 = pl.multiple_of((tm // 8 - 1 - gi) * 8, 8)
            av = an_scr[pl.ds(r0, 8), :]
            bv = dh_scr[pl.ds(r0, 8), :]
            for d in (1, 2, 4):
                a_sh = jnp.where(row8 < 8 - d, pltpu.roll(av, 8 - d, 0), 1.0)
                b_sh = jnp.where(row8 < 8 - d, pltpu.roll(bv, 8 - d, 0), 0.0)
                bv = bv + av * b_sh
                av = av * a_sh
            gv = bv + av * carry
            g_scr[pl.ds(r0, 8), :] = gv
            return gv[0:1, :]

        g0 = lax.fori_loop(0, tm // 8, scan_step, jnp.zeros((1, LRU_W), F32))
        gcar_ref[0:1, :] = a[0:1, :] * g0
        gv = g_scr[...]
        da = gv * hprev
        dig = gv * sq * xc
        dxc = gv * sq * ig
        dla = da * a - gv * (ig * xc) * ((a * a) / sq)
        drg = dla * ((-LRU_C) * sp)
        glam_ref[...] += _col_sum(dla * rg)
        dpr = drg * rg * (1.0 - rg)
        dpi = dig * ig * (1.0 - ig)
        gbrg_ref[...] += _col_sum(dpr)
        gbig_ref[...] += _col_sum(dpi)
        dpre0 = jnp.concatenate([dpr[:, :256], dpi[:, :256]], axis=1).astype(_MXU)
        dpre1 = jnp.concatenate([dpr[:, 256:], dpi[:, 256:]], axis=1).astype(_MXU)
        gwg_ref[0] += _mm_tn(xc[:, :256], dpre0)
        gwg_ref[1] += _mm_tn(xc[:, 256:], dpre1)
        dxc = dxc + jnp.concatenate([_mm_nt(dpre0, wg_ref[0]), _mm_nt(dpre1, wg_ref[1])], axis=1)
        gcb_ref[...] += _col_sum(dxc)
        for k in range(CONV_K):
            gcw_ref[k:k + 1, :] += _col_sum(dxc * us[k])
        dxc_ext[0:tm, :] = dxc
        du = jnp.zeros((tm, LRU_W), F32)
        for k in range(CONV_K):
            du = du + cw_ref[k:k + 1, :] * dxc_ext[pl.ds(3 - k, tm), :]
        dxc_ext[tm:tm + 8, :] = dxc[0:8, :]
        dproj_ref[:, C_LRUX:C_LRUX + LRU_W] = du.astype(dproj_ref.dtype)

        dxn = _mm(dproj_ref[...], win_ref[...])
        rx = lax.rsqrt(_row_mean(xv * xv) + EPS)
        xh = xv * rx
        gng_ref[...] += _col_sum(dxn * xh)
        dxh = dxn * ng_ref[...]
        gx_ref[...] = dov + rx * (dxh - xh * _row_mean(dxh * xh))

        @pl.when(i == nt - 1)
        def _():
            glam_ref[...] = glam_ref[...] * (LRU_C * _sigmoid(-lam_ref[...]))

    def rows(ncol, arr_cols_block=0):
        return pl.BlockSpec((tm, ncol), lambda i: (nt - 1 - i, arr_cols_block))

    def halo(nrow, ncol, colblk=0):
        per = tm // nrow
        return pl.BlockSpec((nrow, ncol), lambda i: (jnp.maximum((nt - 1 - i) * per - 1, 0), colblk))

    in_specs = [rows(D_MODEL), rows(D_MODEL), rows(D_IN), rows(LRU_W), rows(SWA_W), rows(XATT_W),
                rows(128), rows(128), rows(128),
                halo(8, LRU_W), halo(8, LRU_W), halo(BLOCK, 2 * KV_W, C_SK // (2 * KV_W)),
                halo(BLOCK, 128), halo(BLOCK, 128), halo(BLOCK, 128),
                _const_spec((1, D_MODEL)), _const_spec((D_IN, D_MODEL), True), _const_spec((CONV_K, LRU_W)),
                _const_spec((1, LRU_W)), _const_spec((2, 256, 512), True), _const_spec((1, LRU_W)),
                _const_spec((1, LRU_W)), _const_spec((1, LRU_W)), _const_spec((1, 128)), _const_spec((1, 128)),
                _const_spec((1, XATT_W)), pl.BlockSpec(memory_space=pltpu.SMEM),
                _const_spec((MEM_LEN, XATT_W), True), _const_spec((MEM_LEN, XATT_W), True),
                _const_spec((1, D_MODEL)), _const_spec((D_MODEL, D_MODEL), True)]
    small = [(2, 256, 512), (MEM_LEN, XATT_W), (MEM_LEN, XATT_W), (1, D_MODEL), (1, D_MODEL), (1, LRU_W), (1, LRU_W),
             (1, LRU_W), (1, LRU_W), (CONV_K, LRU_W), (1, 128), (1, 128), (1, XATT_W), (1, 128)]
    out_shape = (jax.ShapeDtypeStruct((seq, D_MODEL), F32), jax.ShapeDtypeStruct((seq, D_IN), _MXU)) + tuple(
        jax.ShapeDtypeStruct(s, F32) for s in small)
    out_specs = (rows(D_MODEL), rows(D_IN)) + tuple(_const_spec(s) for s in small)
    scratch = [pltpu.VMEM((tm + 8, LRU_W), F32), pltpu.VMEM((tm + 8, LRU_W), F32), pltpu.VMEM((tm + 8, LRU_W), F32),
               pltpu.VMEM((tm, LRU_W), F32), pltpu.VMEM((tm, LRU_W), F32), pltpu.VMEM((tm, LRU_W), F32),
               pltpu.VMEM((tm + 8, LRU_W), F32),
               pltpu.VMEM((8, LRU_W), F32), pltpu.VMEM((tm + BLOCK, KV_W), F32), pltpu.VMEM((tm + BLOCK, KV_W), F32),
               pltpu.VMEM((BLOCK, KV_W), F32), pltpu.VMEM((BLOCK, KV_W), F32), pltpu.VMEM((2, tm, 128), F32)]
    return pl.pallas_call(
        body, name="layer_bwd", grid=(nt,), out_shape=out_shape, in_specs=in_specs, out_specs=out_specs,
        scratch_shapes=scratch,
        compiler_params=pltpu.CompilerParams(dimension_semantics=("arbitrary",), vmem_limit_bytes=VMEM_LIMIT),
    )(x, dout, proj, ya, yb, yc, rc, rs1, rs2, proj, ya, proj, rc, rs1, rs2,
      ng, win_t, cw, cb, wg, brg, big, lam, qg, kg, xqg, sinks, km, vm, og, wout)


def reduce_grads(g_in, g_out, g_kv, g_small):
    bigs = (g_in, g_out, g_kv)
    nbig = len(bigs)

    def body(b0, b1, b2, sm, o0, o1, o2, osm, r1_0, r1_1, r1_2, r1s, w0, w1, w2, r2_0, r2_1, r2_2, r2s, send, recv):
        big, outs = (b0, b1, b2), (o0, o1, o2)
        r1, wire, r2 = (r1_0, r1_1, r1_2), (w0, w1, w2), (r2_0, r2_1, r2_2)
        x, y, c = lax.axis_index("x"), lax.axis_index("y"), lax.axis_index("c")
        sibling = (x, y, 1 - c)
        chips = [(1 - x, y), (x, 1 - y), (1 - x, 1 - y)]
        me = _chip_of(x, y)

        def copy(k, src, dst, to):
            return pltpu.make_async_remote_copy(src_ref=src, dst_ref=dst, send_sem=send.at[k], recv_sem=recv.at[k],
                                                device_id=to, device_id_type=MESH)

        step1 = [copy(a, big[a].at[:, 1 - c], r1[a], sibling) for a in range(nbig)]
        step1.append(copy(nbig, sm.at[1 - c], r1s, sibling))
        for cp in step1:
            cp.start()
        step2 = []
        for a in range(nbig):
            copy(a, r1[a], r1[a], sibling).wait_recv()
            for k in range(N_CHIPS):
                r1[a][k] = big[a][k, c] + r1[a][k]
                wire[a][k] = r1[a][k].astype(wire[a].dtype)
            for j, chip in enumerate(chips):
                step2.append(copy(4 + j * 4 + a, wire[a].at[_chip_of(*chip)], r2[a].at[j], (*chip, c)))
                step2[-1].start()
        copy(nbig, r1s, r1s, sibling).wait_recv()
        r1s[...] = sm[c] + r1s[...]
        for j, chip in enumerate(chips):
            step2.append(copy(4 + j * 4 + nbig, r1s, r2s.at[j], (*chip, c)))
            step2[-1].start()
        step3 = []
        for a in range(nbig):
            tot = r1[a][me]
            for j in range(3):
                copy(4 + j * 4 + a, r2[a].at[j], r2[a].at[j], sibling).wait_recv()
                tot = tot + r2[a][j].astype(F32)
            outs[a][c] = tot
            step3.append(copy(16 + a, outs[a].at[c], outs[a].at[c], sibling))
            step3[-1].start()
        for j in range(3):
            copy(4 + j * 4 + nbig, r2s.at[j], r2s.at[j], sibling).wait_recv()
        ids = [_chip_of(*chip) for chip in chips]
        tot = None
        for k in range(N_CHIPS):
            term = jnp.where(k == me, r1s[...],
                             jnp.where(k == ids[0], r2s[0], jnp.where(k == ids[1], r2s[1], r2s[2])))
            tot = term if tot is None else tot + term
        osm[c] = tot

        step3.append(copy(16 + nbig, osm.at[c], osm.at[c], sibling))
        step3[-1].start()
        for a in range(nbig):
            other = outs[a].at[1 - c]
            copy(16 + a, other, other, sibling).wait_recv()
        other = osm.at[1 - c]
        copy(16 + nbig, other, other, sibling).wait_recv()
        for cp in step1 + step2 + step3:
            cp.wait_send()

    vm = pl.BlockSpec(memory_space=pltpu.VMEM)
    half = [b.shape[2:] for b in bigs]
    out_shape = tuple(jax.ShapeDtypeStruct((2,) + h, F32) for h in half) + (jax.ShapeDtypeStruct(g_small.shape, F32),)
    sm_half = g_small.shape[1:]
    scratch = ([pltpu.VMEM((N_CHIPS,) + h, F32) for h in half] + [pltpu.VMEM(sm_half, F32)]
               + [pltpu.VMEM((N_CHIPS,) + h, _WIRE) for h in half]
               + [pltpu.VMEM((3,) + h, _WIRE) for h in half] + [pltpu.VMEM((3,) + sm_half, F32)]
               + [pltpu.SemaphoreType.DMA((20,)), pltpu.SemaphoreType.DMA((20,))])
    return pl.pallas_call(
        body, name="reduce_grads", out_shape=out_shape, in_specs=[vm] * 4, out_specs=(vm,) * 4,
        scratch_shapes=scratch, compiler_params=pltpu.CompilerParams(vmem_limit_bytes=VMEM_LIMIT),
    )(g_in, g_out, g_kv, g_small)


def adamw(w, g, m, v, name):
    rows_, cols = w.shape
    tr = max(t for t in range(8, rows_ + 1, 8) if rows_ % t == 0 and t * cols * 4 <= ADAM_BLOCK_BYTES)

    def body(w_ref, g_ref, m_ref, v_ref, d_ref, nm_ref, nv_ref):
        gv = g_ref[...]
        nm = ADAM_B1 * m_ref[...] + (1.0 - ADAM_B1) * gv
        nv = ADAM_B2 * v_ref[...] + (1.0 - ADAM_B2) * (gv * gv)
        m_hat = nm / (1.0 - ADAM_B1 ** ADAM_STEP)
        v_hat = nv / (1.0 - ADAM_B2 ** ADAM_STEP)
        d_ref[...] = (-ADAM_LR) * (m_hat / (jnp.sqrt(v_hat) + ADAM_EPS) + ADAM_WD * w_ref[...])
        nm_ref[...] = nm
        nv_ref[...] = nv

    spec = pl.BlockSpec((tr, cols), lambda i: (i, 0))
    shp = jax.ShapeDtypeStruct(w.shape, F32)
    return pl.pallas_call(
        body, name=name, grid=(rows_ // tr,), out_shape=(shp, shp, shp), in_specs=[spec] * 4, out_specs=(spec,) * 3,
        compiler_params=pltpu.CompilerParams(dimension_semantics=("arbitrary",)),
    )(w, g, m, v)


SMALL = (("norm_g", 1024), ("mem_norm_g", 1024), ("conv_w", 512), ("conv_b", 512), ("w_rg", 32768), ("b_rg", 512),
         ("w_ig", 32768), ("b_ig", 512), ("lru_lambda", 512), ("q_norm_g", 128), ("k_norm_g", 128), ("sinks", 128),
         ("xq_norm_g", 128), ("xk_norm_g", 128), ("out_norm_g", 1024))
SMALL_ROWS = 576


def _pack(parts, rows_):
    flat = jnp.concatenate([p.reshape(-1) for p in parts])
    return jnp.pad(flat, (0, rows_ * 128 - flat.shape[0])).reshape(rows_, 128)


def _pad_to(v, n):
    v = v.reshape(-1)
    return jnp.pad(v, (0, n - v.shape[0]))


def _block_diag_gates(w_rg, w_ig):
    def bd(w4):
        z = jnp.zeros((4, HEAD, 4, HEAD), w4.dtype)
        idx = jnp.arange(4)
        return z.at[idx, :, idx, :].set(w4).reshape(256, 256)

    return jnp.stack([jnp.concatenate([bd(w_rg[4 * h:4 * h + 4]), bd(w_ig[4 * h:4 * h + 4])], axis=1) for h in (0, 1)])


def _diag_blocks(g):
    out = []
    for part in (0, 1):
        blocks = []
        for h in (0, 1):
            sub = g[h, :, 256 * part:256 * (part + 1)].reshape(4, HEAD, 4, HEAD)
            blocks.append(jnp.stack([sub[n, :, n, :] for n in range(4)]))
        out.append(jnp.concatenate(blocks, axis=0))
    return out


def _rope_tables(seq):
    pos = np.arange(seq, dtype=np.float32)
    inv_freq = (np.float32(ROPE_THETA) ** (-(np.arange(0, ROPE_DIM, 2, dtype=np.float32) / np.float32(ROPE_DIM)))
                ).astype(np.float32)
    ang = (pos[:, None] * inv_freq[None, :]).astype(np.float32)
    cos, sin = np.cos(ang).astype(np.float32), np.sin(ang).astype(np.float32)
    z = lambda n: np.zeros((seq, n), np.float32)
    c64 = np.concatenate([cos, cos, np.ones((seq, HEAD - ROPE_DIM), np.float32)], axis=1)
    s1_64 = np.concatenate([-sin, z(HEAD - 8)], axis=1)
    s2_64 = np.concatenate([z(8), sin, z(HEAD - ROPE_DIM)], axis=1)
    return tuple(jnp.asarray(np.concatenate([t, t], axis=1)) for t in (c64, s1_64, s2_64))


def kernel(x, mem, norm_g, mem_norm_g, w_in, conv_w, conv_b, w_rg, b_rg, w_ig, b_ig, lru_lambda, q_norm_g, k_norm_g, sinks, w_mem_kv, xq_norm_g, xk_norm_g, out_norm_g, w_out, loss_target, m_norm_g, m_mem_norm_g, m_w_in, m_conv_w, m_conv_b, m_w_rg, m_b_rg, m_w_ig, m_b_ig, m_lru_lambda, m_q_norm_g, m_k_norm_g, m_sinks, m_w_mem_kv, m_xq_norm_g, m_xk_norm_g, m_out_norm_g, m_w_out, v_norm_g, v_mem_norm_g, v_w_in, v_conv_w, v_conv_b, v_w_rg, v_b_rg, v_w_ig, v_b_ig, v_lru_lambda, v_q_norm_g, v_k_norm_g, v_sinks, v_w_mem_kv, v_xq_norm_g, v_xk_norm_g, v_out_norm_g, v_w_out):
    seq = x.shape[1]
    chip = 2 * lax.axis_index("x") + lax.axis_index("y")
    xs, tgt, mems = x[0], loss_target[0], mem[0]

    win_t_sh = w_in[0].T.astype(_MXU)
    cw_sh = jnp.pad(conv_w[0], ((0, 4), (0, 0)))
    win_t, wout, wkv, cw_all = gather_weights(win_t_sh, w_out[0].astype(_MXU), w_mem_kv[0].astype(_MXU), cw_sh)
    cw = cw_all.reshape(N_CHIPS, 8, 128)[:, :CONV_K].transpose(1, 0, 2).reshape(CONV_K, LRU_W)

    rc, rs1, rs2 = _rope_tables(seq)
    wg = _block_diag_gates(w_rg[0], w_ig[0]).astype(_MXU)
    qg = jnp.tile(q_norm_g, (1, 2))
    kg = jnp.tile(k_norm_g, (1, 2))
    xqg = jnp.tile(xq_norm_g, (1, 4))
    xkg = jnp.tile(xk_norm_g, (1, 4))

    km, vm = mem_fwd(mems, mem_norm_g, wkv, xkg)
    proj, ya, yb, yc, ycat, xn, dout, loss8 = layer_fwd(
        xs, tgt, rc, rs1, rs2, norm_g, win_t, cw, conv_b, wg, b_rg, b_ig, lru_lambda, qg, kg, xqg, sinks, km, vm,
        out_norm_g, wout)
    g_wout = wgrad_out(ycat, dout)
    (gx, dproj, g_wg, dkm, dvm, g_ng, g_og, g_cb, g_brg, g_big, g_lam, g_cw, g_qn, g_kn, g_xqn, g_sink) = layer_bwd(
        xs, dout, proj, ya, yb, yc, rc, rs1, rs2, norm_g, win_t, cw, conv_b, wg, b_rg, b_ig, lru_lambda, qg, kg, xqg,
        sinks, km, vm, out_norm_g, wout)
    g_win_t = wgrad_in(dproj, xn)
    g_wkv, g_mng, g_xkn = mem_bwd(mems, mem_norm_g, wkv, xkg, dkm, dvm)

    g_wrg, g_wig = _diag_blocks(g_wg)
    fold = lambda v, n: v.reshape(n, HEAD).sum(axis=0)
    small_g = _pack([g_ng, g_mng, g_cw, g_cb, g_wrg, g_brg, g_wig, g_big, g_lam, _pad_to(fold(g_qn, 2), 128),
                     _pad_to(fold(g_kn, 2), 128), g_sink, _pad_to(fold(g_xqn, 4), 128), _pad_to(fold(g_xkn, 4), 128),
                     g_og, loss8[0:1]], SMALL_ROWS)
    r_in, r_out, r_kv, r_small = reduce_grads(
        g_win_t.reshape(N_CHIPS, 2, D_IN // 8, D_MODEL), g_wout.reshape(N_CHIPS, 2, D_MODEL // 8, D_MODEL),
        g_wkv.reshape(N_CHIPS, 2, D_MODEL // 8, 2 * XATT_W), small_g.reshape(2, SMALL_ROWS // 2, 128))

    flat = r_small.reshape(-1)
    sizes = (1024, 1024, 2048, 512, 32768, 512, 32768, 512, 512, 128, 128, 128, 128, 128, 1024)
    offs = [0]
    for s in sizes:
        offs.append(offs[-1] + s)
    loss = flat[offs[-1]]
    piece = {name: flat[offs[k]:offs[k + 1]] for k, (name, _) in enumerate(SMALL)}
    g_cw_mine = lax.dynamic_slice(piece["conv_w"].reshape(CONV_K, LRU_W), (0, chip * 128), (CONV_K, 128))
    grads = {
        "norm_g": piece["norm_g"].reshape(1, 1024), "mem_norm_g": piece["mem_norm_g"].reshape(1, 1024),
        "w_in": r_in.reshape(D_IN // 4, D_MODEL).T[None], "conv_w": g_cw_mine[None],
        "conv_b": piece["conv_b"].reshape(1, 512), "w_rg": piece["w_rg"].reshape(1, 8, HEAD, HEAD),
        "b_rg": piece["b_rg"].reshape(1, 512), "w_ig": piece["w_ig"].reshape(1, 8, HEAD, HEAD),
        "b_ig": piece["b_ig"].reshape(1, 512), "lru_lambda": piece["lru_lambda"].reshape(1, 512),
        "q_norm_g": piece["q_norm_g"][:HEAD].reshape(1, HEAD), "k_norm_g": piece["k_norm_g"][:HEAD].reshape(1, HEAD),
        "sinks": piece["sinks"][:4].reshape(1, 4), "w_mem_kv": r_kv.reshape(D_MODEL // 4, 2 * XATT_W)[None],
        "xq_norm_g": piece["xq_norm_g"][:HEAD].reshape(1, HEAD), "xk_norm_g": piece["xk_norm_g"][:HEAD].reshape(1, HEAD),
        "out_norm_g": piece["out_norm_g"].reshape(1, 1024), "w_out": r_out.reshape(D_MODEL // 4, D_MODEL)[None],
    }
    weights = dict(norm_g=norm_g, mem_norm_g=mem_norm_g, w_in=w_in, conv_w=conv_w, conv_b=conv_b, w_rg=w_rg, b_rg=b_rg,
                   w_ig=w_ig, b_ig=b_ig, lru_lambda=lru_lambda, q_norm_g=q_norm_g, k_norm_g=k_norm_g, sinks=sinks,
                   w_mem_kv=w_mem_kv, xq_norm_g=xq_norm_g, xk_norm_g=xk_norm_g, out_norm_g=out_norm_g, w_out=w_out)
    ms = dict(norm_g=m_norm_g, mem_norm_g=m_mem_norm_g, w_in=m_w_in, conv_w=m_conv_w, conv_b=m_conv_b, w_rg=m_w_rg,
              b_rg=m_b_rg, w_ig=m_w_ig, b_ig=m_b_ig, lru_lambda=m_lru_lambda, q_norm_g=m_q_norm_g, k_norm_g=m_k_norm_g,
              sinks=m_sinks, w_mem_kv=m_w_mem_kv, xq_norm_g=m_xq_norm_g, xk_norm_g=m_xk_norm_g,
              out_norm_g=m_out_norm_g, w_out=m_w_out)
    vs = dict(norm_g=v_norm_g, mem_norm_g=v_mem_norm_g, w_in=v_w_in, conv_w=v_conv_w, conv_b=v_conv_b, w_rg=v_w_rg,
              b_rg=v_b_rg, w_ig=v_w_ig, b_ig=v_b_ig, lru_lambda=v_lru_lambda, q_norm_g=v_q_norm_g, k_norm_g=v_k_norm_g,
              sinks=v_sinks, w_mem_kv=v_w_mem_kv, xq_norm_g=v_xq_norm_g, xk_norm_g=v_xk_norm_g,
              out_norm_g=v_out_norm_g, w_out=v_w_out)

    delta, new_m, new_v = {}, {}, {}
    d2, m2, v2 = adamw(w_in[0].T, r_in.reshape(D_IN // 4, D_MODEL), m_w_in[0].T, v_w_in[0].T, "adamw_w_in")
    delta["w_in"], new_m["w_in"], new_v["w_in"] = d2.T[None], m2.T[None], v2.T[None]
    for name in ("w_mem_kv", "w_out"):
        shp = weights[name].shape
        d2, m2, v2 = adamw(weights[name][0], grads[name][0], ms[name][0], vs[name][0], "adamw_" + name)
        delta[name], new_m[name], new_v[name] = d2.reshape(shp), m2.reshape(shp), v2.reshape(shp)
    small_names = [n for n, _ in SMALL]
    packs = [_pack([_pad_to(d[n], sz) for n, sz in SMALL], SMALL_ROWS) for d in (weights, grads, ms, vs)]
    d_p, m_p, v_p = adamw(*packs, "adamw_small")
    offs2 = [0]
    for _, sz in SMALL:
        offs2.append(offs2[-1] + sz)
    for out_d, pk in ((delta, d_p), (new_m, m_p), (new_v, v_p)):
        fl = pk.reshape(-1)
        for k, n in enumerate(small_names):
            shp = weights[n].shape
            out_d[n] = fl[offs2[k]:offs2[k] + math.prod(shp)].reshape(shp)

    order = ("norm_g", "mem_norm_g", "w_in", "conv_w", "conv_b", "w_rg", "b_rg", "w_ig", "b_ig", "lru_lambda",
             "q_norm_g", "k_norm_g", "sinks", "w_mem_kv", "xq_norm_g", "xk_norm_g", "out_norm_g", "w_out")
    return (loss, gx[None], *[grads[n] for n in order], *[delta[n] for n in order], *[new_m[n] for n in order],
            *[new_v[n] for n in order])
```

```python
import functools
import math

import jax
import jax.numpy as jnp
import numpy as np
from jax import lax
from jax.experimental import pallas as pl
from jax.experimental.pallas import tpu as pltpu

F32 = jnp.float32
_MXU = jnp.bfloat16
_WIRE = jnp.bfloat16

D_MODEL = 1024
MEM_LEN = 256
HEAD = 64
LRU_W = 512
LRU_BLOCKS = 8
CONV_K = 4
LRU_C = 8.0
SWA_W = 256
KV_W = 128
XATT_W = 256
BLOCK = 128
D_IN = 2304
ROPE_THETA = 500000.0
ROPE_DIM = 16
EPS = 1e-6
NEG_INF = -1e30
C_LRUX, C_LRUG, C_SQ, C_SK, C_SV, C_SWAG, C_XQ, C_XG = 0, 512, 1024, 1280, 1408, 1536, 1792, 2048

ADAM_LR, ADAM_B1, ADAM_B2, ADAM_EPS, ADAM_WD, ADAM_STEP = 0.001, 0.9, 0.999, 1e-08, 0.01, 10

N_CHIPS = 4
ROW_TILE = 256
VMEM_LIMIT = 56 * 1024 * 1024
ADAM_BLOCK_BYTES = 1280 * 1024
MESH = pl.DeviceIdType.MESH


def _mm(a, b):
    return jnp.dot(a.astype(_MXU), b.astype(_MXU), preferred_element_type=F32)


def _mm_nt(a, b):
    return lax.dot_general(a.astype(_MXU), b.astype(_MXU), (((1,), (1,)), ((), ())), preferred_element_type=F32)


def _mm_tn(a, b):
    return lax.dot_general(a.astype(_MXU), b.astype(_MXU), (((0,), (0,)), ((), ())), preferred_element_type=F32)


def _group_matrix(width):
    r = lax.shift_right_logical(lax.broadcasted_iota(jnp.int32, (width, width), 0), 6)
    c = lax.shift_right_logical(lax.broadcasted_iota(jnp.int32, (width, width), 1), 6)
    return (r == c).astype(_MXU)


def _seg_mean(x, gm):
    return jnp.dot(x.astype(_MXU), gm, preferred_element_type=F32) * (1.0 / HEAD)


def _row_mean(x):
    return jnp.mean(x, axis=-1, keepdims=True)


def _col_sum(x):
    return jnp.sum(x, axis=0, keepdims=True)


def _sigmoid(x):
    return jax.nn.sigmoid(x)


def _softplus(z):
    e = jnp.exp(-jnp.abs(z))
    u = 1.0 + e
    log1p_e = jnp.where(u == 1.0, e, jnp.log(u) * (e / (u - 1.0)))
    return jnp.maximum(z, 0.0) + log1p_e


def _rope(t, c, s1, s2):
    return t * c + pltpu.roll(t, 120, 1) * s1 + pltpu.roll(t, 8, 1) * s2


def _rope_bwd(d, c, s1, s2):
    return d * c + pltpu.roll(d * s1, 8, 1) + pltpu.roll(d * s2, 120, 1)


def _lane_mask(width, lo, hi):
    lane = lax.broadcasted_iota(jnp.int32, (1, width), 1)
    return ((lane >= lo) & (lane < hi)).astype(F32)


def _swa_mask(first_block):
    qi = lax.broadcasted_iota(jnp.int32, (BLOCK, 2 * BLOCK), 0)
    kj = lax.broadcasted_iota(jnp.int32, (BLOCK, 2 * BLOCK), 1)
    rel = qi + BLOCK - kj
    ok = (rel >= 0) & (rel < BLOCK)
    return ok & (jnp.logical_not(first_block) | (kj >= BLOCK))


def _place_kv(t, scale):
    lo = t * (_lane_mask(KV_W, 0, HEAD) * scale)
    hi = t * (_lane_mask(KV_W, HEAD, KV_W) * scale)
    return [a.astype(_MXU) for a in (lo, pltpu.roll(lo, HEAD, 1), pltpu.roll(hi, HEAD, 1), hi)]


def _unplace_kv(d):
    return (_lane_mask(KV_W, 0, HEAD) * (d[0] + pltpu.roll(d[1], HEAD, 1))
            + _lane_mask(KV_W, HEAD, KV_W) * (d[3] + pltpu.roll(d[2], HEAD, 1)))


def _swa_probs(qh, ka, mask, sink):
    s = _mm_nt(qh, ka)
    s = jnp.where(mask, s, NEG_INF)
    m = jnp.maximum(jnp.max(s, axis=-1, keepdims=True), sink)
    p = jnp.exp(s - m)
    esink = jnp.exp(sink - m)
    inv = 1.0 / (jnp.sum(p, axis=-1, keepdims=True) + esink)
    return p * inv, esink * inv


def _mem_probs(s_all):
    out = []
    for j in range(4):
        s = s_all[:, MEM_LEN * j:MEM_LEN * (j + 1)]
        p = jnp.exp(s - jnp.max(s, axis=-1, keepdims=True))
        out.append(p * (1.0 / jnp.sum(p, axis=-1, keepdims=True)))
    return out


def _head_rows(t, scale):
    return jnp.concatenate([t * (_lane_mask(XATT_W, HEAD * j, HEAD * (j + 1)) * scale) for j in range(4)], axis=0)


def _lru_gates(xc, wg_ref, brg, big, lam):
    p0 = _mm(xc[:, :256], wg_ref[0])
    p1 = _mm(xc[:, 256:], wg_ref[1])
    rg = _sigmoid(jnp.concatenate([p0[:, :256], p1[:, :256]], axis=1) + brg)
    ig = _sigmoid(jnp.concatenate([p0[:, 256:], p1[:, 256:]], axis=1) + big)
    sp = _softplus(-lam)
    la = (-LRU_C) * rg * sp
    a = jnp.exp(la)
    th = jnp.tanh(la)
    one_minus_a2 = (-2.0 * th) / (1.0 - th)
    return rg, ig, sp, a, jnp.sqrt(one_minus_a2)


def _const_spec(shape, single=False):
    zeros = (0,) * len(shape)
    if single:
        return pl.BlockSpec(shape, lambda i: zeros, pipeline_mode=pl.Buffered(1))
    return pl.BlockSpec(shape, lambda i: zeros)


def _chip_of(x, y):
    return 2 * x + y


def gather_weights(win_t, wout, wkv, convw):
    arrs = (win_t, wout, wkv)
    n = len(arrs)

    def body(a0, a1, a2, cw, o0, o1, o2, ocw, send, recv, lsem):
        ins, outs = (a0, a1, a2), (o0, o1, o2)
        x, y, c = lax.axis_index("x"), lax.axis_index("y"), lax.axis_index("c")
        sibling = (x, y, 1 - c)
        chips = [(1 - x, y), (x, 1 - y), (1 - x, 1 - y)]
        me = _chip_of(x, y)

        def rows(a, chip, half):
            r = ins[a].shape[0]
            return pl.ds(pl.multiple_of(chip * r + half * (r // 2), 16), r // 2)

        def own_half(a, half):
            r = ins[a].shape[0]
            return ins[a].at[pl.ds(pl.multiple_of(half * (r // 2), 16), r // 2)]

        def copy(k, src, dst, to):
            return pltpu.make_async_remote_copy(src_ref=src, dst_ref=dst, send_sem=send.at[k], recv_sem=recv.at[k],
                                                device_id=to, device_id_type=MESH)

        locals_ = []
        for a in range(n):
            r = ins[a].shape[0]
            locals_.append(pltpu.make_async_copy(ins[a], outs[a].at[pl.ds(pl.multiple_of(me * r, 16), r)], lsem.at[a]))
        locals_.append(pltpu.make_async_copy(cw, ocw.at[pl.ds(pl.multiple_of(me * 8, 8), 8)], lsem.at[n]))
        for cp in locals_:
            cp.start()

        first, passed = [], []
        for a in range(n):
            for j, chip in enumerate(chips):
                first.append(copy(a * 6 + j, own_half(a, c), outs[a].at[rows(a, me, c)], (*chip, c)))
        for j, chip in enumerate(chips):
            first.append(copy(n * 6 + j, cw, ocw.at[pl.ds(pl.multiple_of(me * 8, 8), 8)], (*chip, c)))
        for cp in first:
            cp.start()
        for a in range(n):
            for j, chip in enumerate(chips):
                got = outs[a].at[rows(a, _chip_of(*chip), c)]
                copy(a * 6 + j, got, got, (*chip, c)).wait_recv()
                fwd = copy(a * 6 + 3 + j, got, got, sibling)
                fwd.start()
                passed.append(fwd)
        for a in range(n):
            for j, chip in enumerate(chips):
                got = outs[a].at[rows(a, _chip_of(*chip), 1 - c)]
                copy(a * 6 + 3 + j, got, got, sibling).wait_recv()
        for j, chip in enumerate(chips):
            got = ocw.at[pl.ds(pl.multiple_of(_chip_of(*chip) * 8, 8), 8)]
            copy(n * 6 + j, got, got, (*chip, c)).wait_recv()
        for cp in first + passed:
            cp.wait_send()
        for cp in locals_:
            cp.wait()

    vm = pl.BlockSpec(memory_space=pltpu.VMEM)
    out_shape = tuple(jax.ShapeDtypeStruct((N_CHIPS * a.shape[0],) + a.shape[1:], a.dtype) for a in arrs) + (
        jax.ShapeDtypeStruct((N_CHIPS * 8, 128), F32),)
    n_rdma = n * 6 + 3
    return pl.pallas_call(
        body, name="gather_weights", out_shape=out_shape,
        in_specs=[vm] * 4, out_specs=(vm,) * 4,
        scratch_shapes=[pltpu.SemaphoreType.DMA((n_rdma,)), pltpu.SemaphoreType.DMA((n_rdma,)),
                        pltpu.SemaphoreType.DMA((n + 1,))],
        compiler_params=pltpu.CompilerParams(vmem_limit_bytes=VMEM_LIMIT),
    )(win_t, wout, wkv, convw)


def mem_fwd(mem, mem_g, wkv, xk_g):
    def body(mem_ref, g_ref, w_ref, xk_ref, km_ref, vm_ref):
        mem_v = mem_ref[...]
        mn = mem_v * lax.rsqrt(_row_mean(mem_v * mem_v) + EPS) * g_ref[...]
        mkv = _mm(mn, w_ref[...])
        kpre = mkv[:, :XATT_W]
        gm = _group_matrix(XATT_W)
        km = kpre * lax.rsqrt(_seg_mean(kpre * kpre, gm) + EPS) * xk_ref[...]
        km_ref[...] = _head_rows(km, 0.125).astype(km_ref.dtype)
        vm_ref[...] = _head_rows(mkv[:, XATT_W:], 1.0).astype(vm_ref.dtype)

    vm = pl.BlockSpec(memory_space=pltpu.VMEM)
    rows_shape = jax.ShapeDtypeStruct((4 * MEM_LEN, XATT_W), _MXU)
    return pl.pallas_call(
        body, name="mem_fwd", out_shape=(rows_shape, rows_shape), in_specs=[vm] * 4, out_specs=(vm, vm),
    )(mem, mem_g, wkv, xk_g)


def mem_bwd(mem, mem_g, wkv, xk_g, dkm, dvm):
    def body(mem_ref, g_ref, w_ref, xk_ref, dkm_ref, dvm_ref, gw_ref, gg_ref, gxk_ref):
        mem_v = mem_ref[...]
        mh = mem_v * lax.rsqrt(_row_mean(mem_v * mem_v) + EPS)
        mn = mh * g_ref[...]
        mkv = _mm(mn, w_ref[...])
        kpre = mkv[:, :XATT_W]
        gm = _group_matrix(XATT_W)
        rk = lax.rsqrt(_seg_mean(kpre * kpre, gm) + EPS)
        kn = kpre * rk
        dk = jnp.zeros((MEM_LEN, XATT_W), F32)
        dv = jnp.zeros((MEM_LEN, XATT_W), F32)
        for j in range(4):
            mj = _lane_mask(XATT_W, HEAD * j, HEAD * (j + 1))
            dk = dk + dkm_ref[MEM_LEN * j:MEM_LEN * (j + 1), :] * (mj * 0.125)
            dv = dv + dvm_ref[MEM_LEN * j:MEM_LEN * (j + 1), :] * mj
        gxk_ref[...] = _col_sum(dk * kn)
        dkn = dk * xk_ref[...]
        dkpre = rk * (dkn - kn * _seg_mean(dkn * kn, gm))
        dmkv = jnp.concatenate([dkpre, dv], axis=1)
        gw_ref[...] = _mm_tn(mn, dmkv)
        dmn = _mm_nt(dmkv, w_ref[...])
        gg_ref[...] = _col_sum(dmn * mh)

    vm = pl.BlockSpec(memory_space=pltpu.VMEM)
    return pl.pallas_call(
        body, name="mem_bwd",
        out_shape=(jax.ShapeDtypeStruct((D_MODEL, 2 * XATT_W), F32), jax.ShapeDtypeStruct((1, D_MODEL), F32),
                   jax.ShapeDtypeStruct((1, XATT_W), F32)),
        in_specs=[vm] * 6, out_specs=(vm, vm, vm),
    )(mem, mem_g, wkv, xk_g, dkm, dvm)


def layer_fwd(x, tgt, rc, rs1, rs2, ng, win_t, cw, cb, wg, brg, big, lam, qg, kg, xqg, sinks, km, vm, og, wout):
    seq = x.shape[0]
    tm = min(ROW_TILE, seq)
    nt = seq // tm
    nb = tm // BLOCK

    def body(x_ref, t_ref, c_ref, s1_ref, s2_ref, ng_ref, win_ref, cw_ref, cb_ref, wg_ref, brg_ref, big_ref, lam_ref,
             qg_ref, kg_ref, xqg_ref, sink_ref, km_ref, vm_ref, og_ref, wout_ref,
             proj_ref, ya_ref, yb_ref, yc_ref, ycat_ref, xn_ref, dout_ref, loss_ref,
             ext_ref, a_scr, b_scr, hc_ref, kp_ref, vp_ref, lacc_ref):
        i = pl.program_id(0)

        @pl.when(i == 0)
        def _():
            ext_ref[0:8, :] = jnp.zeros((8, LRU_W), F32)
            hc_ref[...] = jnp.zeros_like(hc_ref)
            kp_ref[...] = jnp.zeros_like(kp_ref)
            vp_ref[...] = jnp.zeros_like(vp_ref)
            lacc_ref[...] = jnp.zeros_like(lacc_ref)

        xv = x_ref[...]
        xn = (xv * lax.rsqrt(_row_mean(xv * xv) + EPS) * ng_ref[...]).astype(_MXU)
        xn_ref[...] = xn.astype(xn_ref.dtype)
        proj_ref[...] = _mm_nt(xn, win_ref[...])

        u = proj_ref[:, C_LRUX:C_LRUX + LRU_W]
        ext_ref[8:8 + tm, :] = u
        xc = cb_ref[...]
        for k in range(CONV_K):
            xc = xc + cw_ref[k:k + 1, :] * ext_ref[pl.ds(5 + k, tm), :]
        ext_ref[0:8, :] = u[tm - 8:tm, :]
        rg, ig, sp, a, sq = _lru_gates(xc, wg_ref, brg_ref[...], big_ref[...], lam_ref[...])
        a_scr[...] = a
        b_scr[...] = sq * (ig * xc)
        row8 = lax.broadcasted_iota(jnp.int32, (8, LRU_W), 0)

        def scan_step(g, carry):
            r0 = pl.multiple_of(g * 8, 8)
            av = a_scr[pl.ds(r0, 8), :]
            bv = b_scr[pl.ds(r0, 8), :]
            for d in (1, 2, 4):
                a_sh = jnp.where(row8 >= d, pltpu.roll(av, d, 0), 1.0)
                b_sh = jnp.where(row8 >= d, pltpu.roll(bv, d, 0), 0.0)
                bv = bv + av * b_sh
                av = av * a_sh
            hv = bv + av * carry
            ya_ref[pl.ds(r0, 8), :] = hv
            return hv[7:8, :]

        hc_ref[0:1, :] = lax.fori_loop(0, tm // 8, scan_step, hc_ref[0:1, :], unroll=True)

        gm128 = _group_matrix(KV_W)
        cv, s1v, s2v = c_ref[...], s1_ref[...], s2_ref[...]

        def head_norm_rope(t, g):
            n = t * lax.rsqrt(_seg_mean(t * t, gm128) + EPS)
            return _rope(n * g, cv, s1v, s2v)

        qs_ = (head_norm_rope(proj_ref[:, C_SQ:C_SQ + 128], qg_ref[...]).astype(_MXU),
               head_norm_rope(proj_ref[:, C_SQ + 128:C_SQ + 256], qg_ref[...]).astype(_MXU))
        kr = head_norm_rope(proj_ref[:, C_SK:C_SK + KV_W], kg_ref[...])
        sv = proj_ref[:, C_SV:C_SV + KV_W]
        ka = _place_kv(jnp.concatenate([kp_ref[...], kr], axis=0), 0.125)
        va = _place_kv(jnp.concatenate([vp_ref[...], sv], axis=0), 1.0)
        kp_ref[...] = kr[tm - BLOCK:tm, :]
        vp_ref[...] = sv[tm - BLOCK:tm, :]
        for b in range(nb):
            mask = _swa_mask((i == 0) & (b == 0)) if b == 0 else _swa_mask(False)
            band = slice(BLOCK * b, BLOCK * b + 2 * BLOCK)
            ps = [_swa_probs(qs_[j // 2][BLOCK * b:BLOCK * (b + 1)], ka[j][band], mask, sink_ref[0, j])[0].astype(_MXU)
                  for j in range(4)]
            for h in range(2):
                yb_ref[BLOCK * b:BLOCK * (b + 1), KV_W * h:KV_W * (h + 1)] = _mm(
                    jnp.concatenate(ps[2 * h:2 * h + 2], axis=1),
                    jnp.concatenate([va[2 * h][band], va[2 * h + 1][band]], axis=0))

        gm256 = _group_matrix(XATT_W)
        xq = proj_ref[:, C_XQ:C_XQ + XATT_W]
        qx = xq * lax.rsqrt(_seg_mean(xq * xq, gm256) + EPS) * xqg_ref[...]
        pm = _mem_probs(_mm_nt(qx, km_ref[...]))
        yc = _mm(jnp.concatenate([p.astype(_MXU) for p in pm], axis=1), vm_ref[...])
        yc_ref[...] = yc

        def gated(y, g, gate):
            return y * lax.rsqrt(_row_mean(y * y) + EPS) * g * (gate * _sigmoid(gate))

        ogv = og_ref[...]
        za = gated(ya_ref[...], ogv[:, :512], proj_ref[:, C_LRUG:C_LRUG + LRU_W])
        zb = gated(yb_ref[...], ogv[:, 512:768], proj_ref[:, C_SWAG:C_SWAG + SWA_W])
        zc = gated(yc, ogv[:, 768:], proj_ref[:, C_XG:C_XG + XATT_W])
        ycat_ref[:, 0:512] = za.astype(ycat_ref.dtype)
        ycat_ref[:, 512:768] = zb.astype(ycat_ref.dtype)
        ycat_ref[:, 768:1024] = zc.astype(ycat_ref.dtype)
        out = xv + _mm(ycat_ref[...], wout_ref[...])
        err = out - t_ref[...]
        dout_ref[...] = err * (1.0 / D_MODEL)
        lacc_ref[...] = lacc_ref[...] + (0.5 / D_MODEL) * jnp.sum(err * err)

        @pl.when(i == nt - 1)
        def _():
            loss_ref[...] = lacc_ref[...]

    def rows(ncol):
        return pl.BlockSpec((tm, ncol), lambda i: (i, 0))

    in_specs = [rows(D_MODEL), rows(D_MODEL), rows(128), rows(128), rows(128),
                _const_spec((1, D_MODEL)), _const_spec((D_IN, D_MODEL), True), _const_spec((CONV_K, LRU_W)),
                _const_spec((1, LRU_W)), _const_spec((2, 256, 512), True), _const_spec((1, LRU_W)),
                _const_spec((1, LRU_W)), _const_spec((1, LRU_W)), _const_spec((1, 128)), _const_spec((1, 128)),
                _const_spec((1, XATT_W)), pl.BlockSpec(memory_space=pltpu.SMEM),
                _const_spec((4 * MEM_LEN, XATT_W), True), _const_spec((4 * MEM_LEN, XATT_W), True),
                _const_spec((1, D_MODEL)), _const_spec((D_MODEL, D_MODEL), True)]
    out_shape = (jax.ShapeDtypeStruct((seq, D_IN), F32), jax.ShapeDtypeStruct((seq, LRU_W), F32),
                 jax.ShapeDtypeStruct((seq, SWA_W), F32), jax.ShapeDtypeStruct((seq, XATT_W), F32),
                 jax.ShapeDtypeStruct((seq, D_MODEL), _MXU), jax.ShapeDtypeStruct((seq, D_MODEL), _MXU),
                 jax.ShapeDtypeStruct((seq, D_MODEL), F32), jax.ShapeDtypeStruct((8, 128), F32))
    out_specs = (rows(D_IN), rows(LRU_W), rows(SWA_W), rows(XATT_W), rows(D_MODEL), rows(D_MODEL), rows(D_MODEL),
                 _const_spec((8, 128)))
    scratch = [pltpu.VMEM((tm + 8, LRU_W), F32), pltpu.VMEM((tm, LRU_W), F32), pltpu.VMEM((tm, LRU_W), F32),
               pltpu.VMEM((8, LRU_W), F32), pltpu.VMEM((BLOCK, KV_W), F32), pltpu.VMEM((BLOCK, KV_W), F32),
               pltpu.VMEM((8, 128), F32)]
    return pl.pallas_call(
        body, name="layer_fwd", grid=(nt,), out_shape=out_shape, in_specs=in_specs, out_specs=out_specs,
        scratch_shapes=scratch,
        compiler_params=pltpu.CompilerParams(dimension_semantics=("arbitrary",), vmem_limit_bytes=VMEM_LIMIT),
    )(x, tgt, rc, rs1, rs2, ng, win_t, cw, cb, wg, brg, big, lam, qg, kg, xqg, sinks, km, vm, og, wout)


def wgrad_out(ycat, dout):
    seq = ycat.shape[0]
    tk = min(512, seq)
    nk = seq // tk

    def body(y_ref, d_ref, o_ref):
        @pl.when(pl.program_id(0) == 0)
        def _():
            o_ref[...] = jnp.zeros_like(o_ref)

        o_ref[...] += _mm_tn(y_ref[...], d_ref[...])

    return pl.pallas_call(
        body, name="wgrad_out", grid=(nk,), out_shape=jax.ShapeDtypeStruct((D_MODEL, D_MODEL), F32),
        in_specs=[pl.BlockSpec((tk, D_MODEL), lambda k: (k, 0)), pl.BlockSpec((tk, D_MODEL), lambda k: (k, 0))],
        out_specs=pl.BlockSpec((D_MODEL, D_MODEL), lambda k: (0, 0)),
        compiler_params=pltpu.CompilerParams(dimension_semantics=("arbitrary",), vmem_limit_bytes=VMEM_LIMIT),
    )(ycat, dout)


def wgrad_in(dproj, xn):
    seq = xn.shape[0]
    nblk = D_IN // 256

    def body(d_ref, x_ref, o_ref):
        o_ref[...] = _mm_tn(d_ref[...], x_ref[...])

    return pl.pallas_call(
        body, name="wgrad_in", grid=(nblk,), out_shape=jax.ShapeDtypeStruct((D_IN, D_MODEL), F32),
        in_specs=[pl.BlockSpec((seq, 256), lambda j: (0, j)), _const_spec((seq, D_MODEL), True)],
        out_specs=pl.BlockSpec((256, D_MODEL), lambda j: (j, 0)),
        compiler_params=pltpu.CompilerParams(dimension_semantics=("arbitrary",), vmem_limit_bytes=VMEM_LIMIT),
    )(dproj, xn)


def layer_bwd(x, dout, proj, ya, yb, yc, rc, rs1, rs2, ng, win_t, cw, cb, wg, brg, big, lam, qg, kg, xqg, sinks, km,
              vm, og, wout):
    seq = x.shape[0]
    tm = min(ROW_TILE, seq)
    nt = seq // tm
    nb = tm // BLOCK

    def body(x_ref, dout_ref, proj_ref, ya_ref, yb_ref, yc_ref, c_ref, s1_ref, s2_ref,
             uh_ref, yah_ref, kvh_ref, ch_ref, s1h_ref, s2h_ref,
             ng_ref, win_ref, cw_ref, cb_ref, wg_ref, brg_ref, big_ref, lam_ref, qg_ref, kg_ref, xqg_ref, sink_ref,
             km_ref, vm_ref, og_ref, wout_ref,
             gx_ref, dproj_ref, gwg_ref, dkm_ref, dvm_ref, gng_ref, gog_ref, gcb_ref, gbrg_ref, gbig_ref, glam_ref,
             gcw_ref, gqn_ref, gkn_ref, gxqn_ref, gsink_ref,
             ext_ref, hext_ref, aext_ref, an_scr, dh_scr, g_scr, dxc_ext, gcar_ref, dkcar_ref, dvcar_ref):
        i = pl.program_id(0)
        tile = nt - 1 - i
        first_tile = tile == 0

        @pl.when(i == 0)
        def _():
            for r in (gwg_ref, dkm_ref, dvm_ref, gng_ref, gog_ref, gcb_ref, gbrg_ref, gbig_ref, glam_ref, gcw_ref,
                      gqn_ref, gkn_ref, gxqn_ref, gsink_ref, gcar_ref, dkcar_ref, dvcar_ref):
                r[...] = jnp.zeros_like(r)
            dxc_ext[tm:tm + 8, :] = jnp.zeros((8, LRU_W), F32)
            aext_ref[tm:tm + 8, :] = jnp.zeros((8, LRU_W), F32)

        xv = x_ref[...]
        dov = dout_ref[...]
        dz = _mm_nt(dov, wout_ref[...])
        ogv = og_ref[...]

        def group_bwd(y, gate, g, dzg):
            r = lax.rsqrt(_row_mean(y * y) + EPS)
            n = y * r
            sg = _sigmoid(gate)
            dgate = dzg * (n * g) * (sg * (1.0 + gate * (1.0 - sg)))
            dng = dzg * (gate * sg)
            dn = dng * g
            return r * (dn - n * _row_mean(dn * n)), dgate, _col_sum(dng * n)

        dya, dga, goa = group_bwd(ya_ref[...], proj_ref[:, C_LRUG:C_LRUG + LRU_W], ogv[:, :512], dz[:, :512])
        dyb, dgb, gob = group_bwd(yb_ref[...], proj_ref[:, C_SWAG:C_SWAG + SWA_W], ogv[:, 512:768], dz[:, 512:768])
        dyc, dgc, goc = group_bwd(yc_ref[...], proj_ref[:, C_XG:C_XG + XATT_W], ogv[:, 768:], dz[:, 768:])
        gog_ref[...] += jnp.concatenate([goa, gob, goc], axis=1)
        dproj_ref[:, C_LRUG:C_LRUG + LRU_W] = dga.astype(dproj_ref.dtype)
        dproj_ref[:, C_SWAG:C_SWAG + SWA_W] = dgb.astype(dproj_ref.dtype)
        dproj_ref[:, C_XG:C_XG + XATT_W] = dgc.astype(dproj_ref.dtype)

        gm256 = _group_matrix(XATT_W)
        xq = proj_ref[:, C_XQ:C_XQ + XATT_W]
        rq = lax.rsqrt(_seg_mean(xq * xq, gm256) + EPS)
        qn = xq * rq
        qx = qn * xqg_ref[...]
        qxb = qx.astype(_MXU)
        dycb = dyc.astype(_MXU)
        pm = _mem_probs(_mm_nt(qxb, km_ref[...]))
        dp_all = _mm_nt(dycb, vm_ref[...])
        dsm = []
        for j in range(4):
            dp = dp_all[:, MEM_LEN * j:MEM_LEN * (j + 1)]
            dsm.append((pm[j] * (dp - jnp.sum(pm[j] * dp, axis=-1, keepdims=True))).astype(_MXU))
        ds_all = jnp.concatenate(dsm, axis=1)
        dvm_ref[...] += _mm_tn(jnp.concatenate([p.astype(_MXU) for p in pm], axis=1), dycb)
        dkm_ref[...] += _mm_tn(ds_all, qxb)
        dqx = _mm(ds_all, km_ref[...])
        gxqn_ref[...] += _col_sum(dqx * qn)
        dqn = dqx * xqg_ref[...]
        dproj_ref[:, C_XQ:C_XQ + XATT_W] = (rq * (dqn - qn * _seg_mean(dqn * qn, gm256))).astype(dproj_ref.dtype)

        gm128 = _group_matrix(KV_W)
        cv, s1v, s2v = c_ref[...], s1_ref[...], s2_ref[...]

        def head_norm(t):
            r = lax.rsqrt(_seg_mean(t * t, gm128) + EPS)
            return t * r, r

        qn_, qr_ = zip(head_norm(proj_ref[:, C_SQ:C_SQ + 128]), head_norm(proj_ref[:, C_SQ + 128:C_SQ + 256]))
        qrope = [_rope(qn_[h] * qg_ref[...], cv, s1v, s2v).astype(_MXU) for h in range(2)]
        kn, krr = head_norm(proj_ref[:, C_SK:C_SK + KV_W])
        kr = _rope(kn * kg_ref[...], cv, s1v, s2v)
        khn, _ = head_norm(kvh_ref[:, 0:KV_W])
        khr = _rope(khn * kg_ref[...], ch_ref[...], s1h_ref[...], s2h_ref[...])
        ka = _place_kv(jnp.concatenate([khr, kr], axis=0), 0.125)
        va = _place_kv(jnp.concatenate([kvh_ref[:, KV_W:2 * KV_W], proj_ref[:, C_SV:C_SV + KV_W]], axis=0), 1.0)
        lane128 = lax.broadcasted_iota(jnp.int32, (1, 128), 1)
        gsink = jnp.zeros((1, 128), F32)
        dk_band, dv_band, dq_blk = [], [], []
        for b in range(nb):
            mask = _swa_mask(first_tile & (b == 0)) if b == 0 else _swa_mask(False)
            band = slice(BLOCK * b, BLOCK * b + 2 * BLOCK)
            dka, dva, dsb = [], [], []
            for j in range(4):
                qh = qrope[j // 2][BLOCK * b:BLOCK * (b + 1)]
                doh = dyb[BLOCK * b:BLOCK * (b + 1), KV_W * (j // 2):KV_W * (j // 2 + 1)].astype(_MXU)
                p, psink = _swa_probs(qh, ka[j][band], mask, sink_ref[0, j])
                dp = _mm_nt(doh, va[j][band])
                delta = jnp.sum(p * dp, axis=-1, keepdims=True)
                ds = (p * (dp - delta)).astype(_MXU)
                gsink = gsink + jnp.where(lane128 == j, jnp.sum(-psink * delta), 0.0)
                dva.append(_mm_tn(p, doh))
                dka.append(_mm_tn(ds, qh))
                dsb.append(ds)
            dk_band.append(_unplace_kv(dka) * 0.125)
            dv_band.append(_unplace_kv(dva))
            dq_blk.append([_mm(jnp.concatenate(dsb[2 * h:2 * h + 2], axis=1),
                               jnp.concatenate([ka[2 * h][band], ka[2 * h + 1][band]], axis=0)) for h in range(2)])
        gsink_ref[...] += gsink
        dk_rows = [dk_band[b][BLOCK:] + (dk_band[b + 1][:BLOCK] if b + 1 < nb else dkcar_ref[...]) for b in range(nb)]
        dv_rows = [dv_band[b][BLOCK:] + (dv_band[b + 1][:BLOCK] if b + 1 < nb else dvcar_ref[...]) for b in range(nb)]
        dkcar_ref[...] = dk_band[0][:BLOCK]
        dvcar_ref[...] = dv_band[0][:BLOCK]
        dkg = _rope_bwd(jnp.concatenate(dk_rows, axis=0), cv, s1v, s2v)
        gkn = _col_sum(dkg * kn)
        dkn = dkg * kg_ref[...]
        dproj_ref[:, C_SK:C_SK + KV_W] = (krr * (dkn - kn * _seg_mean(dkn * kn, gm128))).astype(dproj_ref.dtype)
        dproj_ref[:, C_SV:C_SV + KV_W] = jnp.concatenate(dv_rows, axis=0).astype(dproj_ref.dtype)
        gqn = jnp.zeros((1, 128), F32)
        for h in range(2):
            dqg = _rope_bwd(jnp.concatenate([dq_blk[b][h] for b in range(nb)], axis=0), cv, s1v, s2v)
            gqn = gqn + _col_sum(dqg * qn_[h])
            dqn_ = dqg * qg_ref[...]
            dproj_ref[:, C_SQ + 128 * h:C_SQ + 128 * (h + 1)] = (
                qr_[h] * (dqn_ - qn_[h] * _seg_mean(dqn_ * qn_[h], gm128))).astype(dproj_ref.dtype)
        gqn_ref[...] += gqn
        gkn_ref[...] += gkn

        u = proj_ref[:, C_LRUX:C_LRUX + LRU_W]
        ext_ref[0:8, :] = jnp.where(first_tile, 0.0, uh_ref[...])
        ext_ref[8:8 + tm, :] = u
        us = [ext_ref[pl.ds(5 + k, tm), :] for k in range(CONV_K)]
        xc = cb_ref[...]
        for k in range(CONV_K):
            xc = xc + cw_ref[k:k + 1, :] * us[k]
        rg, ig, sp, a, sq = _lru_gates(xc, wg_ref, brg_ref[...], big_ref[...], lam_ref[...])
        hext_ref[0:8, :] = jnp.where(first_tile, 0.0, yah_ref[...])
        hext_ref[8:8 + tm, :] = ya_ref[...]
        hprev = hext_ref[pl.ds(7, tm), :]
        aext_ref[0:tm, :] = a
        an_scr[...] = aext_ref[pl.ds(1, tm), :]
        dh_scr[...] = dya
        dh_scr[tm - 1:tm, :] = dh_scr[tm - 1:tm, :] + gcar_ref[0:1, :]
        row8 = lax.broadcasted_iota(jnp.int32, (8, LRU_W), 0)

        def scan_step(gi, carry):
            r0 = pl.multiple_of((tm // 8 - 1 - gi) * 8, 8)
            av = an_scr[pl.ds(r0, 8), :]
            bv = dh_scr[pl.ds(r0, 8), :]
            for d in (1, 2, 4):
                a_sh = jnp.where(row8 < 8 - d, pltpu.roll(av, 8 - d, 0), 1.0)
                b_sh = jnp.where(row8 < 8 - d, pltpu.roll(bv, 8 - d, 0), 0.0)
                bv = bv + av * b_sh
                av = av * a_sh
            gv = bv + av * carry
            g_scr[pl.ds(r0, 8), :] = gv
            return gv[0:1, :]

        g0 = lax.fori_loop(0, tm // 8, scan_step, jnp.zeros((1, LRU_W), F32), unroll=True)
        gcar_ref[0:1, :] = a[0:1, :] * g0
        gv = g_scr[...]
        da = gv * hprev
        dig = gv * sq * xc
        dxc = gv * sq * ig
        dla = da * a - gv * (ig * xc) * ((a * a) / sq)
        drg = dla * ((-LRU_C) * sp)
        glam_ref[...] += _col_sum(dla * rg)
        dpr = drg * rg * (1.0 - rg)
        dpi = dig * ig * (1.0 - ig)
        gbrg_ref[...] += _col_sum(dpr)
        gbig_ref[...] += _col_sum(dpi)
        dpre0 = jnp.concatenate([dpr[:, :256], dpi[:, :256]], axis=1).astype(_MXU)
        dpre1 = jnp.concatenate([dpr[:, 256:], dpi[:, 256:]], axis=1).astype(_MXU)
        gwg_ref[0] += _mm_tn(xc[:, :256], dpre0)
        gwg_ref[1] += _mm_tn(xc[:, 256:], dpre1)
        dxc = dxc + jnp.concatenate([_mm_nt(dpre0, wg_ref[0]), _mm_nt(dpre1, wg_ref[1])], axis=1)
        gcb_ref[...] += _col_sum(dxc)
        for k in range(CONV_K):
            gcw_ref[k:k + 1, :] += _col_sum(dxc * us[k])
        dxc_ext[0:tm, :] = dxc
        du = jnp.zeros((tm, LRU_W), F32)
        for k in range(CONV_K):
            du = du + cw_ref[k:k + 1, :] * dxc_ext[pl.ds(3 - k, tm), :]
        dxc_ext[tm:tm + 8, :] = dxc[0:8, :]
        dproj_ref[:, C_LRUX:C_LRUX + LRU_W] = du.astype(dproj_ref.dtype)

        dxn = _mm(dproj_ref[...], win_ref[...])
        rx = lax.rsqrt(_row_mean(xv * xv) + EPS)
        xh = xv * rx
        gng_ref[...] += _col_sum(dxn * xh)
        dxh = dxn * ng_ref[...]
        gx_ref[...] = dov + rx * (dxh - xh * _row_mean(dxh * xh))

        @pl.when(i == nt - 1)
        def _():
            glam_ref[...] = glam_ref[...] * (LRU_C * _sigmoid(-lam_ref[...]))

    def rows(ncol, arr_cols_block=0):
        return pl.BlockSpec((tm, ncol), lambda i: (nt - 1 - i, arr_cols_block))

    def halo(nrow, ncol, colblk=0):
        per = tm // nrow
        return pl.BlockSpec((nrow, ncol), lambda i: (jnp.maximum((nt - 1 - i) * per - 1, 0), colblk))

    in_specs = [rows(D_MODEL), rows(D_MODEL), rows(D_IN), rows(LRU_W), rows(SWA_W), rows(XATT_W),
                rows(128), rows(128), rows(128),
                halo(8, LRU_W), halo(8, LRU_W), halo(BLOCK, 2 * KV_W, C_SK // (2 * KV_W)),
                halo(BLOCK, 128), halo(BLOCK, 128), halo(BLOCK, 128),
                _const_spec((1, D_MODEL)), _const_spec((D_IN, D_MODEL), True), _const_spec((CONV_K, LRU_W)),
                _const_spec((1, LRU_W)), _const_spec((2, 256, 512), True), _const_spec((1, LRU_W)),
                _const_spec((1, LRU_W)), _const_spec((1, LRU_W)), _const_spec((1, 128)), _const_spec((1, 128)),
                _const_spec((1, XATT_W)), pl.BlockSpec(memory_space=pltpu.SMEM),
                _const_spec((4 * MEM_LEN, XATT_W), True), _const_spec((4 * MEM_LEN, XATT_W), True),
                _const_spec((1, D_MODEL)), _const_spec((D_MODEL, D_MODEL), True)]
    small = [(2, 256, 512), (4 * MEM_LEN, XATT_W), (4 * MEM_LEN, XATT_W), (1, D_MODEL), (1, D_MODEL), (1, LRU_W), (1, LRU_W),
             (1, LRU_W), (1, LRU_W), (CONV_K, LRU_W), (1, 128), (1, 128), (1, XATT_W), (1, 128)]
    out_shape = (jax.ShapeDtypeStruct((seq, D_MODEL), F32), jax.ShapeDtypeStruct((seq, D_IN), _MXU)) + tuple(
        jax.ShapeDtypeStruct(s, F32) for s in small)
    out_specs = (rows(D_MODEL), rows(D_IN)) + tuple(_const_spec(s) for s in small)
    scratch = [pltpu.VMEM((tm + 8, LRU_W), F32), pltpu.VMEM((tm + 8, LRU_W), F32), pltpu.VMEM((tm + 8, LRU_W), F32),
               pltpu.VMEM((tm, LRU_W), F32), pltpu.VMEM((tm, LRU_W), F32), pltpu.VMEM((tm, LRU_W), F32),
               pltpu.VMEM((tm + 8, LRU_W), F32),
               pltpu.VMEM((8, LRU_W), F32), pltpu.VMEM((BLOCK, KV_W), F32), pltpu.VMEM((BLOCK, KV_W), F32)]
    return pl.pallas_call(
        body, name="layer_bwd", grid=(nt,), out_shape=out_shape, in_specs=in_specs, out_specs=out_specs,
        scratch_shapes=scratch,
        compiler_params=pltpu.CompilerParams(dimension_semantics=("arbitrary",), vmem_limit_bytes=VMEM_LIMIT),
    )(x, dout, proj, ya, yb, yc, rc, rs1, rs2, proj, ya, proj, rc, rs1, rs2,
      ng, win_t, cw, cb, wg, brg, big, lam, qg, kg, xqg, sinks, km, vm, og, wout)


def reduce_grads(g_in, g_out, g_kv, g_small):
    bigs = (g_in, g_out, g_kv)
    nbig = len(bigs)

    def body(b0, b1, b2, sm, o0, o1, o2, osm, r1_0, r1_1, r1_2, r1s, w0, w1, w2, r2_0, r2_1, r2_2, r2s, send, recv):
        big, outs = (b0, b1, b2), (o0, o1, o2)
        r1, wire, r2 = (r1_0, r1_1, r1_2), (w0, w1, w2), (r2_0, r2_1, r2_2)
        x, y, c = lax.axis_index("x"), lax.axis_index("y"), lax.axis_index("c")
        sibling = (x, y, 1 - c)
        chips = [(1 - x, y), (x, 1 - y), (1 - x, 1 - y)]
        me = _chip_of(x, y)

        def copy(k, src, dst, to):
            return pltpu.make_async_remote_copy(src_ref=src, dst_ref=dst, send_sem=send.at[k], recv_sem=recv.at[k],
                                                device_id=to, device_id_type=MESH)

        step1 = [copy(a, big[a].at[:, 1 - c], r1[a], sibling) for a in range(nbig)]
        step1.append(copy(nbig, sm.at[1 - c], r1s, sibling))
        for cp in step1:
            cp.start()
        step2 = []
        for a in range(nbig):
            copy(a, r1[a], r1[a], sibling).wait_recv()
            for k in range(N_CHIPS):
                r1[a][k] = big[a][k, c] + r1[a][k]
                wire[a][k] = r1[a][k].astype(wire[a].dtype)
            for j, chip in enumerate(chips):
                step2.append(copy(4 + j * 4 + a, wire[a].at[_chip_of(*chip)], r2[a].at[j], (*chip, c)))
                step2[-1].start()
        copy(nbig, r1s, r1s, sibling).wait_recv()
        r1s[...] = sm[c] + r1s[...]
        for j, chip in enumerate(chips):
            step2.append(copy(4 + j * 4 + nbig, r1s, r2s.at[j], (*chip, c)))
            step2[-1].start()
        step3 = []
        for a in range(nbig):
            tot = r1[a][me]
            for j in range(3):
                copy(4 + j * 4 + a, r2[a].at[j], r2[a].at[j], sibling).wait_recv()
                tot = tot + r2[a][j].astype(F32)
            outs[a][c] = tot
            step3.append(copy(16 + a, outs[a].at[c], outs[a].at[c], sibling))
            step3[-1].start()
        for j in range(3):
            copy(4 + j * 4 + nbig, r2s.at[j], r2s.at[j], sibling).wait_recv()
        ids = [_chip_of(*chip) for chip in chips]
        tot = None
        for k in range(N_CHIPS):
            term = jnp.where(k == me, r1s[...],
                             jnp.where(k == ids[0], r2s[0], jnp.where(k == ids[1], r2s[1], r2s[2])))
            tot = term if tot is None else tot + term
        osm[c] = tot

        step3.append(copy(16 + nbig, osm.at[c], osm.at[c], sibling))
        step3[-1].start()
        for a in range(nbig):
            other = outs[a].at[1 - c]
            copy(16 + a, other, other, sibling).wait_recv()
        other = osm.at[1 - c]
        copy(16 + nbig, other, other, sibling).wait_recv()
        for cp in step1 + step2 + step3:
            cp.wait_send()

    vm = pl.BlockSpec(memory_space=pltpu.VMEM)
    half = [b.shape[2:] for b in bigs]
    out_shape = tuple(jax.ShapeDtypeStruct((2,) + h, F32) for h in half) + (jax.ShapeDtypeStruct(g_small.shape, F32),)
    sm_half = g_small.shape[1:]
    scratch = ([pltpu.VMEM((N_CHIPS,) + h, F32) for h in half] + [pltpu.VMEM(sm_half, F32)]
               + [pltpu.VMEM((N_CHIPS,) + h, _WIRE) for h in half]
               + [pltpu.VMEM((3,) + h, _WIRE) for h in half] + [pltpu.VMEM((3,) + sm_half, F32)]
               + [pltpu.SemaphoreType.DMA((20,)), pltpu.SemaphoreType.DMA((20,))])
    return pl.pallas_call(
        body, name="reduce_grads", out_shape=out_shape, in_specs=[vm] * 4, out_specs=(vm,) * 4,
        scratch_shapes=scratch, compiler_params=pltpu.CompilerParams(vmem_limit_bytes=VMEM_LIMIT),
    )(g_in, g_out, g_kv, g_small)


def adamw(w, g, m, v, name):
    rows_, cols = w.shape
    tr = max(t for t in range(8, rows_ + 1, 8) if rows_ % t == 0 and t * cols * 4 <= ADAM_BLOCK_BYTES)

    def body(w_ref, g_ref, m_ref, v_ref, d_ref, nm_ref, nv_ref):
        gv = g_ref[...]
        nm = ADAM_B1 * m_ref[...] + (1.0 - ADAM_B1) * gv
        nv = ADAM_B2 * v_ref[...] + (1.0 - ADAM_B2) * (gv * gv)
        m_hat = nm / (1.0 - ADAM_B1 ** ADAM_STEP)
        v_hat = nv / (1.0 - ADAM_B2 ** ADAM_STEP)
        d_ref[...] = (-ADAM_LR) * (m_hat / (jnp.sqrt(v_hat) + ADAM_EPS) + ADAM_WD * w_ref[...])
        nm_ref[...] = nm
        nv_ref[...] = nv

    spec = pl.BlockSpec((tr, cols), lambda i: (i, 0))
    shp = jax.ShapeDtypeStruct(w.shape, F32)
    return pl.pallas_call(
        body, name=name, grid=(rows_ // tr,), out_shape=(shp, shp, shp), in_specs=[spec] * 4, out_specs=(spec,) * 3,
        compiler_params=pltpu.CompilerParams(dimension_semantics=("arbitrary",)),
    )(w, g, m, v)


SMALL = (("norm_g", 1024), ("mem_norm_g", 1024), ("conv_w", 512), ("conv_b", 512), ("w_rg", 32768), ("b_rg", 512),
         ("w_ig", 32768), ("b_ig", 512), ("lru_lambda", 512), ("q_norm_g", 128), ("k_norm_g", 128), ("sinks", 128),
         ("xq_norm_g", 128), ("xk_norm_g", 128), ("out_norm_g", 1024))
SMALL_ROWS = 576


def _pack(parts, rows_):
    flat = jnp.concatenate([p.reshape(-1) for p in parts])
    return jnp.pad(flat, (0, rows_ * 128 - flat.shape[0])).reshape(rows_, 128)


def _pad_to(v, n):
    v = v.reshape(-1)
    return jnp.pad(v, (0, n - v.shape[0]))


def _block_diag_gates(w_rg, w_ig):
    def bd(w4):
        z = jnp.zeros((4, HEAD, 4, HEAD), w4.dtype)
        idx = jnp.arange(4)
        return z.at[idx, :, idx, :].set(w4).reshape(256, 256)

    return jnp.stack([jnp.concatenate([bd(w_rg[4 * h:4 * h + 4]), bd(w_ig[4 * h:4 * h + 4])], axis=1) for h in (0, 1)])


def _diag_blocks(g):
    out = []
    for part in (0, 1):
        blocks = []
        for h in (0, 1):
            sub = g[h, :, 256 * part:256 * (part + 1)].reshape(4, HEAD, 4, HEAD)
            blocks.append(jnp.stack([sub[n, :, n, :] for n in range(4)]))
        out.append(jnp.concatenate(blocks, axis=0))
    return out


def _rope_tables(seq):
    pos = np.arange(seq, dtype=np.float32)
    inv_freq = (np.float32(ROPE_THETA) ** (-(np.arange(0, ROPE_DIM, 2, dtype=np.float32) / np.float32(ROPE_DIM)))
                ).astype(np.float32)
    ang = (pos[:, None] * inv_freq[None, :]).astype(np.float32)
    cos, sin = np.cos(ang).astype(np.float32), np.sin(ang).astype(np.float32)
    z = lambda n: np.zeros((seq, n), np.float32)
    c64 = np.concatenate([cos, cos, np.ones((seq, HEAD - ROPE_DIM), np.float32)], axis=1)
    s1_64 = np.concatenate([-sin, z(HEAD - 8)], axis=1)
    s2_64 = np.concatenate([z(8), sin, z(HEAD - ROPE_DIM)], axis=1)
    return tuple(jnp.asarray(np.concatenate([t, t], axis=1)) for t in (c64, s1_64, s2_64))


def kernel(x, mem, norm_g, mem_norm_g, w_in, conv_w, conv_b, w_rg, b_rg, w_ig, b_ig, lru_lambda, q_norm_g, k_norm_g, sinks, w_mem_kv, xq_norm_g, xk_norm_g, out_norm_g, w_out, loss_target, m_norm_g, m_mem_norm_g, m_w_in, m_conv_w, m_conv_b, m_w_rg, m_b_rg, m_w_ig, m_b_ig, m_lru_lambda, m_q_norm_g, m_k_norm_g, m_sinks, m_w_mem_kv, m_xq_norm_g, m_xk_norm_g, m_out_norm_g, m_w_out, v_norm_g, v_mem_norm_g, v_w_in, v_conv_w, v_conv_b, v_w_rg, v_b_rg, v_w_ig, v_b_ig, v_lru_lambda, v_q_norm_g, v_k_norm_g, v_sinks, v_w_mem_kv, v_xq_norm_g, v_xk_norm_g, v_out_norm_g, v_w_out):
    seq = x.shape[1]
    chip = 2 * lax.axis_index("x") + lax.axis_index("y")
    xs, tgt, mems = x[0], loss_target[0], mem[0]

    win_t_sh = w_in[0].T.astype(_MXU)
    cw_sh = jnp.pad(conv_w[0], ((0, 4), (0, 0)))
    win_t, wout, wkv, cw_all = gather_weights(win_t_sh, w_out[0].astype(_MXU), w_mem_kv[0].astype(_MXU), cw_sh)
    cw = cw_all.reshape(N_CHIPS, 8, 128)[:, :CONV_K].transpose(1, 0, 2).reshape(CONV_K, LRU_W)

    rc, rs1, rs2 = _rope_tables(seq)
    wg = _block_diag_gates(w_rg[0], w_ig[0]).astype(_MXU)
    qg = jnp.tile(q_norm_g, (1, 2))
    kg = jnp.tile(k_norm_g, (1, 2))
    xqg = jnp.tile(xq_norm_g, (1, 4))
    xkg = jnp.tile(xk_norm_g, (1, 4))

    km, vm = mem_fwd(mems, mem_norm_g, wkv, xkg)
    proj, ya, yb, yc, ycat, xn, dout, loss8 = layer_fwd(
        xs, tgt, rc, rs1, rs2, norm_g, win_t, cw, conv_b, wg, b_rg, b_ig, lru_lambda, qg, kg, xqg, sinks, km, vm,
        out_norm_g, wout)
    g_wout = wgrad_out(ycat, dout)
    (gx, dproj, g_wg, dkm, dvm, g_ng, g_og, g_cb, g_brg, g_big, g_lam, g_cw, g_qn, g_kn, g_xqn, g_sink) = layer_bwd(
        xs, dout, proj, ya, yb, yc, rc, rs1, rs2, norm_g, win_t, cw, conv_b, wg, b_rg, b_ig, lru_lambda, qg, kg, xqg,
        sinks, km, vm, out_norm_g, wout)
    g_win_t = wgrad_in(dproj, xn)
    g_wkv, g_mng, g_xkn = mem_bwd(mems, mem_norm_g, wkv, xkg, dkm, dvm)

    g_wrg, g_wig = _diag_blocks(g_wg)
    fold = lambda v, n: v.reshape(n, HEAD).sum(axis=0)
    small_g = _pack([g_ng, g_mng, g_cw, g_cb, g_wrg, g_brg, g_wig, g_big, g_lam, _pad_to(fold(g_qn, 2), 128),
                     _pad_to(fold(g_kn, 2), 128), g_sink, _pad_to(fold(g_xqn, 4), 128), _pad_to(fold(g_xkn, 4), 128),
                     g_og, loss8[0:1]], SMALL_ROWS)
    r_in, r_out, r_kv, r_small = reduce_grads(
        g_win_t.reshape(N_CHIPS, 2, D_IN // 8, D_MODEL), g_wout.reshape(N_CHIPS, 2, D_MODEL // 8, D_MODEL),
        g_wkv.reshape(N_CHIPS, 2, D_MODEL // 8, 2 * XATT_W), small_g.reshape(2, SMALL_ROWS // 2, 128))

    flat = r_small.reshape(-1)
    sizes = (1024, 1024, 2048, 512, 32768, 512, 32768, 512, 512, 128, 128, 128, 128, 128, 1024)
    offs = [0]
    for s in sizes:
        offs.append(offs[-1] + s)
    loss = flat[offs[-1]]
    piece = {name: flat[offs[k]:offs[k + 1]] for k, (name, _) in enumerate(SMALL)}
    g_cw_mine = lax.dynamic_slice(piece["conv_w"].reshape(CONV_K, LRU_W), (0, chip * 128), (CONV_K, 128))
    grads = {
        "norm_g": piece["norm_g"].reshape(1, 1024), "mem_norm_g": piece["mem_norm_g"].reshape(1, 1024),
        "w_in": r_in.reshape(D_IN // 4, D_MODEL).T[None], "conv_w": g_cw_mine[None],
        "conv_b": piece["conv_b"].reshape(1, 512), "w_rg": piece["w_rg"].reshape(1, 8, HEAD, HEAD),
        "b_rg": piece["b_rg"].reshape(1, 512), "w_ig": piece["w_ig"].reshape(1, 8, HEAD, HEAD),
        "b_ig": piece["b_ig"].reshape(1, 512), "lru_lambda": piece["lru_lambda"].reshape(1, 512),
        "q_norm_g": piece["q_norm_g"][:HEAD].reshape(1, HEAD), "k_norm_g": piece["k_norm_g"][:HEAD].reshape(1, HEAD),
        "sinks": piece["sinks"][:4].reshape(1, 4), "w_mem_kv": r_kv.reshape(D_MODEL // 4, 2 * XATT_W)[None],
        "xq_norm_g": piece["xq_norm_g"][:HEAD].reshape(1, HEAD), "xk_norm_g": piece["xk_norm_g"][:HEAD].reshape(1, HEAD),
        "out_norm_g": piece["out_norm_g"].reshape(1, 1024), "w_out": r_out.reshape(D_MODEL // 4, D_MODEL)[None],
    }
    weights = dict(norm_g=norm_g, mem_norm_g=mem_norm_g, w_in=w_in, conv_w=conv_w, conv_b=conv_b, w_rg=w_rg, b_rg=b_rg,
                   w_ig=w_ig, b_ig=b_ig, lru_lambda=lru_lambda, q_norm_g=q_norm_g, k_norm_g=k_norm_g, sinks=sinks,
                   w_mem_kv=w_mem_kv, xq_norm_g=xq_norm_g, xk_norm_g=xk_norm_g, out_norm_g=out_norm_g, w_out=w_out)
    ms = dict(norm_g=m_norm_g, mem_norm_g=m_mem_norm_g, w_in=m_w_in, conv_w=m_conv_w, conv_b=m_conv_b, w_rg=m_w_rg,
              b_rg=m_b_rg, w_ig=m_w_ig, b_ig=m_b_ig, lru_lambda=m_lru_lambda, q_norm_g=m_q_norm_g, k_norm_g=m_k_norm_g,
              sinks=m_sinks, w_mem_kv=m_w_mem_kv, xq_norm_g=m_xq_norm_g, xk_norm_g=m_xk_norm_g,
              out_norm_g=m_out_norm_g, w_out=m_w_out)
    vs = dict(norm_g=v_norm_g, mem_norm_g=v_mem_norm_g, w_in=v_w_in, conv_w=v_conv_w, conv_b=v_conv_b, w_rg=v_w_rg,
              b_rg=v_b_rg, w_ig=v_w_ig, b_ig=v_b_ig, lru_lambda=v_lru_lambda, q_norm_g=v_q_norm_g, k_norm_g=v_k_norm_g,
              sinks=v_sinks, w_mem_kv=v_w_mem_kv, xq_norm_g=v_xq_norm_g, xk_norm_g=v_xk_norm_g,
              out_norm_g=v_out_norm_g, w_out=v_w_out)

    delta, new_m, new_v = {}, {}, {}
    d2, m2, v2 = adamw(w_in[0].T, r_in.reshape(D_IN // 4, D_MODEL), m_w_in[0].T, v_w_in[0].T, "adamw_w_in")
    delta["w_in"], new_m["w_in"], new_v["w_in"] = d2.T[None], m2.T[None], v2.T[None]
    for name in ("w_mem_kv", "w_out"):
        shp = weights[name].shape
        d2, m2, v2 = adamw(weights[name][0], grads[name][0], ms[name][0], vs[name][0], "adamw_" + name)
        delta[name], new_m[name], new_v[name] = d2.reshape(shp), m2.reshape(shp), v2.reshape(shp)
    small_names = [n for n, _ in SMALL]
    packs = [_pack([_pad_to(d[n], sz) for n, sz in SMALL], SMALL_ROWS) for d in (weights, grads, ms, vs)]
    d_p, m_p, v_p = adamw(*packs, "adamw_small")
    offs2 = [0]
    for _, sz in SMALL:
        offs2.append(offs2[-1] + sz)
    for out_d, pk in ((delta, d_p), (new_m, m_p), (new_v, v_p)):
        fl = pk.reshape(-1)
        for k, n in enumerate(small_names):
            shp = weights[n].shape
            out_d[n] = fl[offs2[k]:offs2[k] + math.prod(shp)].reshape(shp)

    order = ("norm_g", "mem_norm_g", "w_in", "conv_w", "conv_b", "w_rg", "b_rg", "w_ig", "b_ig", "lru_lambda",
             "q_norm_g", "k_norm_g", "sinks", "w_mem_kv", "xq_norm_g", "xk_norm_g", "out_norm_g", "w_out")
    return (loss, gx[None], *[grads[n] for n in order], *[delta[n] for n in order], *[new_m[n] for n in order],
            *[new_v[n] for n in order])
```

```python
import functools
import math

import jax
import jax.numpy as jnp
import numpy as np
from jax import lax
from jax.experimental import pallas as pl
from jax.experimental.pallas import tpu as pltpu

F32 = jnp.float32
_MXU = jnp.bfloat16
_WIRE = jnp.bfloat16

D_MODEL = 1024
MEM_LEN = 256
HEAD = 64
LRU_W = 512
LRU_BLOCKS = 8
CONV_K = 4
LRU_C = 8.0
SWA_W = 256
KV_W = 128
XATT_W = 256
BLOCK = 128
D_IN = 2304
ROPE_THETA = 500000.0
ROPE_DIM = 16
EPS = 1e-6
NEG_INF = -1e30
C_LRUX, C_LRUG, C_SQ, C_SK, C_SV, C_SWAG, C_XQ, C_XG = 0, 512, 1024, 1280, 1408, 1536, 1792, 2048

ADAM_LR, ADAM_B1, ADAM_B2, ADAM_EPS, ADAM_WD, ADAM_STEP = 0.001, 0.9, 0.999, 1e-08, 0.01, 10

N_CHIPS = 4
ROW_TILE = 256
VMEM_LIMIT = 56 * 1024 * 1024
ADAM_BLOCK_BYTES = 1280 * 1024
MESH = pl.DeviceIdType.MESH


def _mm(a, b):
    return jnp.dot(a.astype(_MXU), b.astype(_MXU), preferred_element_type=F32)


def _mm_nt(a, b):
    return lax.dot_general(a.astype(_MXU), b.astype(_MXU), (((1,), (1,)), ((), ())), preferred_element_type=F32)


def _mm_tn(a, b):
    return lax.dot_general(a.astype(_MXU), b.astype(_MXU), (((0,), (0,)), ((), ())), preferred_element_type=F32)


def _group_matrix(width):
    r = lax.shift_right_logical(lax.broadcasted_iota(jnp.int32, (width, width), 0), 6)
    c = lax.shift_right_logical(lax.broadcasted_iota(jnp.int32, (width, width), 1), 6)
    return (r == c).astype(_MXU)


def _seg_mean(x, gm):
    return jnp.dot(x.astype(_MXU), gm, preferred_element_type=F32) * (1.0 / HEAD)


def _row_mean(x):
    return jnp.mean(x, axis=-1, keepdims=True)


def _col_sum(x):
    return jnp.sum(x, axis=0, keepdims=True)


def _sigmoid(x):
    return jax.nn.sigmoid(x)


def _softplus(z):
    e = jnp.exp(-jnp.abs(z))
    u = 1.0 + e
    log1p_e = jnp.where(u == 1.0, e, jnp.log(u) * (e / (u - 1.0)))
    return jnp.maximum(z, 0.0) + log1p_e


def _rope(t, c, s1, s2):
    return t * c + pltpu.roll(t, 120, 1) * s1 + pltpu.roll(t, 8, 1) * s2


def _rope_bwd(d, c, s1, s2):
    return d * c + pltpu.roll(d * s1, 8, 1) + pltpu.roll(d * s2, 120, 1)


def _lane_mask(width, lo, hi):
    lane = lax.broadcasted_iota(jnp.int32, (1, width), 1)
    return ((lane >= lo) & (lane < hi)).astype(F32)


def _swa_mask(first_block):
    qi = lax.broadcasted_iota(jnp.int32, (BLOCK, 2 * BLOCK), 0)
    kj = lax.broadcasted_iota(jnp.int32, (BLOCK, 2 * BLOCK), 1)
    rel = qi + BLOCK - kj
    ok = (rel >= 0) & (rel < BLOCK)
    return ok & (jnp.logical_not(first_block) | (kj >= BLOCK))


def _place_kv(t, scale):
    lo = t * (_lane_mask(KV_W, 0, HEAD) * scale)
    hi = t * (_lane_mask(KV_W, HEAD, KV_W) * scale)
    return [a.astype(_MXU) for a in (lo, pltpu.roll(lo, HEAD, 1), pltpu.roll(hi, HEAD, 1), hi)]


def _unplace_kv(d):
    return (_lane_mask(KV_W, 0, HEAD) * (d[0] + pltpu.roll(d[1], HEAD, 1))
            + _lane_mask(KV_W, HEAD, KV_W) * (d[3] + pltpu.roll(d[2], HEAD, 1)))


def _swa_probs(qh, ka, mask, sink):
    s = _mm_nt(qh, ka)
    s = jnp.where(mask, s, NEG_INF)
    m = jnp.maximum(jnp.max(s, axis=-1, keepdims=True), sink)
    p = jnp.exp(s - m)
    esink = jnp.exp(sink - m)
    inv = 1.0 / (jnp.sum(p, axis=-1, keepdims=True) + esink)
    return p * inv, esink * inv


def _mem_probs(s_all):
    out = []
    for j in range(4):
        s = s_all[:, MEM_LEN * j:MEM_LEN * (j + 1)]
        p = jnp.exp(s - jnp.max(s, axis=-1, keepdims=True))
        out.append(p * (1.0 / jnp.sum(p, axis=-1, keepdims=True)))
    return out


def _head_rows(t, scale):
    return jnp.concatenate([t * (_lane_mask(XATT_W, HEAD * j, HEAD * (j + 1)) * scale) for j in range(4)], axis=0)


def _lru_gates(xc, wg_ref, brg, big, lam):
    p0 = _mm(xc[:, :256], wg_ref[0])
    p1 = _mm(xc[:, 256:], wg_ref[1])
    rg = _sigmoid(jnp.concatenate([p0[:, :256], p1[:, :256]], axis=1) + brg)
    ig = _sigmoid(jnp.concatenate([p0[:, 256:], p1[:, 256:]], axis=1) + big)
    sp = _softplus(-lam)
    la = (-LRU_C) * rg * sp
    a = jnp.exp(la)
    th = jnp.tanh(la)
    one_minus_a2 = (-2.0 * th) / (1.0 - th)
    return rg, ig, sp, a, jnp.sqrt(one_minus_a2)


def _const_spec(shape, single=False):
    zeros = (0,) * len(shape)
    if single:
        return pl.BlockSpec(shape, lambda i: zeros, pipeline_mode=pl.Buffered(1))
    return pl.BlockSpec(shape, lambda i: zeros)


def _chip_of(x, y):
    return 2 * x + y


def _partners(x, y, c):
    north = c == 1
    near = (jnp.where(north, 1 - x, x), jnp.where(north, y, 1 - y))
    far = (jnp.where(north, x, 1 - x), jnp.where(north, 1 - y, y))
    return near, far, (1 - x, 1 - y)


def gather_weights(win_t, wout, wkv, convw):
    arrs = (win_t, wout, wkv)
    n = len(arrs)

    def body(a0, a1, a2, cw, o0, o1, o2, ocw, send, recv, lsem):
        ins, outs = (a0, a1, a2), (o0, o1, o2)
        x, y, c = lax.axis_index("x"), lax.axis_index("y"), lax.axis_index("c")
        sibling = (x, y, 1 - c)
        near, far, diag = _partners(x, y, c)
        chips = [near, far, diag]
        me = _chip_of(x, y)

        def rows(a, chip, half):
            r = ins[a].shape[0]
            return pl.ds(pl.multiple_of(chip * r + half * (r // 2), 16), r // 2)

        def own_half(a, half):
            r = ins[a].shape[0]
            return ins[a].at[pl.ds(pl.multiple_of(half * (r // 2), 16), r // 2)]

        def copy(k, src, dst, to):
            return pltpu.make_async_remote_copy(src_ref=src, dst_ref=dst, send_sem=send.at[k], recv_sem=recv.at[k],
                                                device_id=to, device_id_type=MESH)

        locals_ = []
        for a in range(n):
            r = ins[a].shape[0]
            locals_.append(pltpu.make_async_copy(ins[a], outs[a].at[pl.ds(pl.multiple_of(me * r, 16), r)], lsem.at[a]))
        locals_.append(pltpu.make_async_copy(cw, ocw.at[pl.ds(pl.multiple_of(me * 8, 8), 8)], lsem.at[n]))
        for cp in locals_:
            cp.start()

        sent = []
        for a in range(n):
            for j in range(2):
                sent.append(copy(a * 6 + j, own_half(a, c), outs[a].at[rows(a, me, c)], (*chips[j], c)))
        for j, chip in enumerate(chips):
            sent.append(copy(n * 6 + j, cw, ocw.at[pl.ds(pl.multiple_of(me * 8, 8), 8)], (*chip, c)))
        for cp in sent:
            cp.start()
        for a in range(n):
            for j in range(3):
                got = outs[a].at[rows(a, _chip_of(*chips[j]), c)]
                copy(a * 6 + j, got, got, sibling).wait_recv()
                if j == 0:
                    sent.append(copy(a * 6 + 2, got, got, (*far, c)))
                    sent[-1].start()
                sent.append(copy(a * 6 + 3 + j, got, got, sibling))
                sent[-1].start()
        for a in range(n):
            for j in range(3):
                got = outs[a].at[rows(a, _chip_of(*chips[(1, 0, 2)[j]]), 1 - c)]
                copy(a * 6 + 3 + j, got, got, sibling).wait_recv()
        for j, chip in enumerate(chips):
            got = ocw.at[pl.ds(pl.multiple_of(_chip_of(*chip) * 8, 8), 8)]
            copy(n * 6 + j, got, got, (*chip, c)).wait_recv()
        for cp in sent:
            cp.wait_send()
        for cp in locals_:
            cp.wait()

    vm = pl.BlockSpec(memory_space=pltpu.VMEM)
    out_shape = tuple(jax.ShapeDtypeStruct((N_CHIPS * a.shape[0],) + a.shape[1:], a.dtype) for a in arrs) + (
        jax.ShapeDtypeStruct((N_CHIPS * 8, 128), F32),)
    n_rdma = n * 6 + 3
    return pl.pallas_call(
        body, name="gather_weights", out_shape=out_shape,
        in_specs=[vm] * 4, out_specs=(vm,) * 4,
        scratch_shapes=[pltpu.SemaphoreType.DMA((n_rdma,)), pltpu.SemaphoreType.DMA((n_rdma,)),
                        pltpu.SemaphoreType.DMA((n + 1,))],
        compiler_params=pltpu.CompilerParams(vmem_limit_bytes=VMEM_LIMIT),
    )(win_t, wout, wkv, convw)


def mem_fwd(mem, mem_g, wkv, xk_g):
    def body(mem_ref, g_ref, w_ref, xk_ref, km_ref, vm_ref):
        mem_v = mem_ref[...]
        mn = mem_v * lax.rsqrt(_row_mean(mem_v * mem_v) + EPS) * g_ref[...]
        mkv = _mm(mn, w_ref[...])
        kpre = mkv[:, :XATT_W]
        gm = _group_matrix(XATT_W)
        km = kpre * lax.rsqrt(_seg_mean(kpre * kpre, gm) + EPS) * xk_ref[...]
        km_ref[...] = _head_rows(km, 0.125).astype(km_ref.dtype)
        vm_ref[...] = _head_rows(mkv[:, XATT_W:], 1.0).astype(vm_ref.dtype)

    vm = pl.BlockSpec(memory_space=pltpu.VMEM)
    rows_shape = jax.ShapeDtypeStruct((4 * MEM_LEN, XATT_W), _MXU)
    return pl.pallas_call(
        body, name="mem_fwd", out_shape=(rows_shape, rows_shape), in_specs=[vm] * 4, out_specs=(vm, vm),
    )(mem, mem_g, wkv, xk_g)


def mem_bwd(mem, mem_g, wkv, xk_g, dkm, dvm):
    def body(mem_ref, g_ref, w_ref, xk_ref, dkm_ref, dvm_ref, gw_ref, gg_ref, gxk_ref):
        mem_v = mem_ref[...]
        mh = mem_v * lax.rsqrt(_row_mean(mem_v * mem_v) + EPS)
        mn = mh * g_ref[...]
        mkv = _mm(mn, w_ref[...])
        kpre = mkv[:, :XATT_W]
        gm = _group_matrix(XATT_W)
        rk = lax.rsqrt(_seg_mean(kpre * kpre, gm) + EPS)
        kn = kpre * rk
        dk = jnp.zeros((MEM_LEN, XATT_W), F32)
        dv = jnp.zeros((MEM_LEN, XATT_W), F32)
        for j in range(4):
            mj = _lane_mask(XATT_W, HEAD * j, HEAD * (j + 1))
            dk = dk + dkm_ref[MEM_LEN * j:MEM_LEN * (j + 1), :] * (mj * 0.125)
            dv = dv + dvm_ref[MEM_LEN * j:MEM_LEN * (j + 1), :] * mj
        gxk_ref[...] = _col_sum(dk * kn)
        dkn = dk * xk_ref[...]
        dkpre = rk * (dkn - kn * _seg_mean(dkn * kn, gm))
        dmkv = jnp.concatenate([dkpre, dv], axis=1)
        gw_ref[...] = _mm_tn(mn, dmkv)
        dmn = _mm_nt(dmkv, w_ref[...])
        gg_ref[...] = _col_sum(dmn * mh)

    vm = pl.BlockSpec(memory_space=pltpu.VMEM)
    return pl.pallas_call(
        body, name="mem_bwd",
        out_shape=(jax.ShapeDtypeStruct((D_MODEL, 2 * XATT_W), F32), jax.ShapeDtypeStruct((1, D_MODEL), F32),
                   jax.ShapeDtypeStruct((1, XATT_W), F32)),
        in_specs=[vm] * 6, out_specs=(vm, vm, vm),
    )(mem, mem_g, wkv, xk_g, dkm, dvm)


def layer_fwd(x, tgt, rc, rs1, rs2, ng, win_t, cw, cb, wg, brg, big, lam, qg, kg, xqg, sinks, km, vm, og, wout):
    seq = x.shape[0]
    tm = min(ROW_TILE, seq)
    nt = seq // tm
    nb = tm // BLOCK

    def body(x_ref, t_ref, c_ref, s1_ref, s2_ref, ng_ref, win_ref, cw_ref, cb_ref, wg_ref, brg_ref, big_ref, lam_ref,
             qg_ref, kg_ref, xqg_ref, sink_ref, km_ref, vm_ref, og_ref, wout_ref,
             proj_ref, ya_ref, yb_ref, yc_ref, ycat_ref, xn_ref, dout_ref, loss_ref,
             ext_ref, a_scr, b_scr, hc_ref, kp_ref, vp_ref, lacc_ref):
        i = pl.program_id(0)

        @pl.when(i == 0)
        def _():
            ext_ref[0:8, :] = jnp.zeros((8, LRU_W), F32)
            hc_ref[...] = jnp.zeros_like(hc_ref)
            kp_ref[...] = jnp.zeros_like(kp_ref)
            vp_ref[...] = jnp.zeros_like(vp_ref)
            lacc_ref[...] = jnp.zeros_like(lacc_ref)

        xv = x_ref[...]
        xn = (xv * lax.rsqrt(_row_mean(xv * xv) + EPS) * ng_ref[...]).astype(_MXU)
        xn_ref[...] = xn.astype(xn_ref.dtype)
        proj_ref[...] = _mm_nt(xn, win_ref[...])

        u = proj_ref[:, C_LRUX:C_LRUX + LRU_W]
        ext_ref[8:8 + tm, :] = u
        xc = cb_ref[...]
        for k in range(CONV_K):
            xc = xc + cw_ref[k:k + 1, :] * ext_ref[pl.ds(5 + k, tm), :]
        ext_ref[0:8, :] = u[tm - 8:tm, :]
        rg, ig, sp, a, sq = _lru_gates(xc, wg_ref, brg_ref[...], big_ref[...], lam_ref[...])
        a_scr[...] = a
        b_scr[...] = sq * (ig * xc)
        row8 = lax.broadcasted_iota(jnp.int32, (8, LRU_W), 0)

        def scan_step(g, carry):
            r0 = pl.multiple_of(g * 8, 8)
            av = a_scr[pl.ds(r0, 8), :]
            bv = b_scr[pl.ds(r0, 8), :]
            for d in (1, 2, 4):
                a_sh = jnp.where(row8 >= d, pltpu.roll(av, d, 0), 1.0)
                b_sh = jnp.where(row8 >= d, pltpu.roll(bv, d, 0), 0.0)
                bv = bv + av * b_sh
                av = av * a_sh
            hv = bv + av * carry
            ya_ref[pl.ds(r0, 8), :] = hv
            return hv[7:8, :]

        hc_ref[0:1, :] = lax.fori_loop(0, tm // 8, scan_step, hc_ref[0:1, :], unroll=True)

        gm128 = _group_matrix(KV_W)
        cv, s1v, s2v = c_ref[...], s1_ref[...], s2_ref[...]

        def head_norm_rope(t, g):
            n = t * lax.rsqrt(_seg_mean(t * t, gm128) + EPS)
            return _rope(n * g, cv, s1v, s2v)

        qs_ = (head_norm_rope(proj_ref[:, C_SQ:C_SQ + 128], qg_ref[...]).astype(_MXU),
               head_norm_rope(proj_ref[:, C_SQ + 128:C_SQ + 256], qg_ref[...]).astype(_MXU))
        kr = head_norm_rope(proj_ref[:, C_SK:C_SK + KV_W], kg_ref[...])
        sv = proj_ref[:, C_SV:C_SV + KV_W]
        ka = _place_kv(jnp.concatenate([kp_ref[...], kr], axis=0), 0.125)
        va = _place_kv(jnp.concatenate([vp_ref[...], sv], axis=0), 1.0)
        kp_ref[...] = kr[tm - BLOCK:tm, :]
        vp_ref[...] = sv[tm - BLOCK:tm, :]
        for b in range(nb):
            mask = _swa_mask((i == 0) & (b == 0)) if b == 0 else _swa_mask(False)
            band = slice(BLOCK * b, BLOCK * b + 2 * BLOCK)
            ps = [_swa_probs(qs_[j // 2][BLOCK * b:BLOCK * (b + 1)], ka[j][band], mask, sink_ref[0, j])[0].astype(_MXU)
                  for j in range(4)]
            for h in range(2):
                yb_ref[BLOCK * b:BLOCK * (b + 1), KV_W * h:KV_W * (h + 1)] = _mm(
                    jnp.concatenate(ps[2 * h:2 * h + 2], axis=1),
                    jnp.concatenate([va[2 * h][band], va[2 * h + 1][band]], axis=0))

        gm256 = _group_matrix(XATT_W)
        xq = proj_ref[:, C_XQ:C_XQ + XATT_W]
        qx = xq * lax.rsqrt(_seg_mean(xq * xq, gm256) + EPS) * xqg_ref[...]
        pm = _mem_probs(_mm_nt(qx, km_ref[...]))
        yc = _mm(jnp.concatenate([p.astype(_MXU) for p in pm], axis=1), vm_ref[...])
        yc_ref[...] = yc

        def gated(y, g, gate):
            return y * lax.rsqrt(_row_mean(y * y) + EPS) * g * (gate * _sigmoid(gate))

        ogv = og_ref[...]
        za = gated(ya_ref[...], ogv[:, :512], proj_ref[:, C_LRUG:C_LRUG + LRU_W])
        zb = gated(yb_ref[...], ogv[:, 512:768], proj_ref[:, C_SWAG:C_SWAG + SWA_W])
        zc = gated(yc, ogv[:, 768:], proj_ref[:, C_XG:C_XG + XATT_W])
        ycat_ref[:, 0:512] = za.astype(ycat_ref.dtype)
        ycat_ref[:, 512:768] = zb.astype(ycat_ref.dtype)
        ycat_ref[:, 768:1024] = zc.astype(ycat_ref.dtype)
        out = xv + _mm(ycat_ref[...], wout_ref[...])
        err = out - t_ref[...]
        dout_ref[...] = err * (1.0 / D_MODEL)
        lacc_ref[...] = lacc_ref[...] + (0.5 / D_MODEL) * jnp.sum(err * err)

        @pl.when(i == nt - 1)
        def _():
            loss_ref[...] = lacc_ref[...]

    def rows(ncol):
        return pl.BlockSpec((tm, ncol), lambda i: (i, 0))

    in_specs = [rows(D_MODEL), rows(D_MODEL), rows(128), rows(128), rows(128),
                _const_spec((1, D_MODEL)), _const_spec((D_IN, D_MODEL), True), _const_spec((CONV_K, LRU_W)),
                _const_spec((1, LRU_W)), _const_spec((2, 256, 512), True), _const_spec((1, LRU_W)),
                _const_spec((1, LRU_W)), _const_spec((1, LRU_W)), _const_spec((1, 128)), _const_spec((1, 128)),
                _const_spec((1, XATT_W)), pl.BlockSpec(memory_space=pltpu.SMEM),
                _const_spec((4 * MEM_LEN, XATT_W), True), _const_spec((4 * MEM_LEN, XATT_W), True),
                _const_spec((1, D_MODEL)), _const_spec((D_MODEL, D_MODEL), True)]
    out_shape = (jax.ShapeDtypeStruct((seq, D_IN), F32), jax.ShapeDtypeStruct((seq, LRU_W), F32),
                 jax.ShapeDtypeStruct((seq, SWA_W), F32), jax.ShapeDtypeStruct((seq, XATT_W), F32),
                 jax.ShapeDtypeStruct((seq, D_MODEL), _MXU), jax.ShapeDtypeStruct((seq, D_MODEL), _MXU),
                 jax.ShapeDtypeStruct((seq, D_MODEL), F32), jax.ShapeDtypeStruct((8, 128), F32))
    out_specs = (rows(D_IN), rows(LRU_W), rows(SWA_W), rows(XATT_W), rows(D_MODEL), rows(D_MODEL), rows(D_MODEL),
                 _const_spec((8, 128)))
    scratch = [pltpu.VMEM((tm + 8, LRU_W), F32), pltpu.VMEM((tm, LRU_W), F32), pltpu.VMEM((tm, LRU_W), F32),
               pltpu.VMEM((8, LRU_W), F32), pltpu.VMEM((BLOCK, KV_W), F32), pltpu.VMEM((BLOCK, KV_W), F32),
               pltpu.VMEM((8, 128), F32)]
    return pl.pallas_call(
        body, name="layer_fwd", grid=(nt,), out_shape=out_shape, in_specs=in_specs, out_specs=out_specs,
        scratch_shapes=scratch,
        compiler_params=pltpu.CompilerParams(dimension_semantics=("arbitrary",), vmem_limit_bytes=VMEM_LIMIT),
    )(x, tgt, rc, rs1, rs2, ng, win_t, cw, cb, wg, brg, big, lam, qg, kg, xqg, sinks, km, vm, og, wout)


def wgrad_out(ycat, dout):
    seq = ycat.shape[0]
    tk = min(512, seq)
    nk = seq // tk

    def body(y_ref, d_ref, o_ref):
        @pl.when(pl.program_id(0) == 0)
        def _():
            o_ref[...] = jnp.zeros_like(o_ref)

        o_ref[...] += _mm_tn(y_ref[...], d_ref[...])

    return pl.pallas_call(
        body, name="wgrad_out", grid=(nk,), out_shape=jax.ShapeDtypeStruct((D_MODEL, D_MODEL), F32),
        in_specs=[pl.BlockSpec((tk, D_MODEL), lambda k: (k, 0)), pl.BlockSpec((tk, D_MODEL), lambda k: (k, 0))],
        out_specs=pl.BlockSpec((D_MODEL, D_MODEL), lambda k: (0, 0)),
        compiler_params=pltpu.CompilerParams(dimension_semantics=("arbitrary",), vmem_limit_bytes=VMEM_LIMIT),
    )(ycat, dout)


def wgrad_in(dproj, xn):
    seq = xn.shape[0]
    nblk = D_IN // 256

    def body(d_ref, x_ref, o_ref):
        o_ref[...] = _mm_tn(d_ref[...], x_ref[...])

    return pl.pallas_call(
        body, name="wgrad_in", grid=(nblk,), out_shape=jax.ShapeDtypeStruct((D_IN, D_MODEL), F32),
        in_specs=[pl.BlockSpec((seq, 256), lambda j: (0, j)), _const_spec((seq, D_MODEL), True)],
        out_specs=pl.BlockSpec((256, D_MODEL), lambda j: (j, 0)),
        compiler_params=pltpu.CompilerParams(dimension_semantics=("arbitrary",), vmem_limit_bytes=VMEM_LIMIT),
    )(dproj, xn)


def layer_bwd(x, dout, proj, ya, yb, yc, rc, rs1, rs2, ng, win_t, cw, cb, wg, brg, big, lam, qg, kg, xqg, sinks, km,
              vm, og, wout):
    seq = x.shape[0]
    tm = min(ROW_TILE, seq)
    nt = seq // tm
    nb = tm // BLOCK

    def body(x_ref, dout_ref, proj_ref, ya_ref, yb_ref, yc_ref, c_ref, s1_ref, s2_ref,
             uh_ref, yah_ref, kvh_ref, ch_ref, s1h_ref, s2h_ref,
             ng_ref, win_ref, cw_ref, cb_ref, wg_ref, brg_ref, big_ref, lam_ref, qg_ref, kg_ref, xqg_ref, sink_ref,
             km_ref, vm_ref, og_ref, wout_ref,
             gx_ref, dproj_ref, gwg_ref, dkm_ref, dvm_ref, gng_ref, gog_ref, gcb_ref, gbrg_ref, gbig_ref, glam_ref,
             gcw_ref, gqn_ref, gkn_ref, gxqn_ref, gsink_ref,
             ext_ref, hext_ref, aext_ref, an_scr, dh_scr, g_scr, dxc_ext, gcar_ref, dkcar_ref, dvcar_ref):
        i = pl.program_id(0)
        tile = nt - 1 - i
        first_tile = tile == 0

        @pl.when(i == 0)
        def _():
            for r in (gwg_ref, dkm_ref, dvm_ref, gng_ref, gog_ref, gcb_ref, gbrg_ref, gbig_ref, glam_ref, gcw_ref,
                      gqn_ref, gkn_ref, gxqn_ref, gsink_ref, gcar_ref, dkcar_ref, dvcar_ref):
                r[...] = jnp.zeros_like(r)
            dxc_ext[tm:tm + 8, :] = jnp.zeros((8, LRU_W), F32)
            aext_ref[tm:tm + 8, :] = jnp.zeros((8, LRU_W), F32)

        xv = x_ref[...]
        dov = dout_ref[...]
        dz = _mm_nt(dov, wout_ref[...])
        ogv = og_ref[...]

        def group_bwd(y, gate, g, dzg):
            r = lax.rsqrt(_row_mean(y * y) + EPS)
            n = y * r
            sg = _sigmoid(gate)
            dgate = dzg * (n * g) * (sg * (1.0 + gate * (1.0 - sg)))
            dng = dzg * (gate * sg)
            dn = dng * g
            return r * (dn - n * _row_mean(dn * n)), dgate, _col_sum(dng * n)

        dya, dga, goa = group_bwd(ya_ref[...], proj_ref[:, C_LRUG:C_LRUG + LRU_W], ogv[:, :512], dz[:, :512])
        dyb, dgb, gob = group_bwd(yb_ref[...], proj_ref[:, C_SWAG:C_SWAG + SWA_W], ogv[:, 512:768], dz[:, 512:768])
        dyc, dgc, goc = group_bwd(yc_ref[...], proj_ref[:, C_XG:C_XG + XATT_W], ogv[:, 768:], dz[:, 768:])
        gog_ref[...] += jnp.concatenate([goa, gob, goc], axis=1)
        dproj_ref[:, C_LRUG:C_LRUG + LRU_W] = dga.astype(dproj_ref.dtype)
        dproj_ref[:, C_SWAG:C_SWAG + SWA_W] = dgb.astype(dproj_ref.dtype)
        dproj_ref[:, C_XG:C_XG + XATT_W] = dgc.astype(dproj_ref.dtype)

        gm256 = _group_matrix(XATT_W)
        xq = proj_ref[:, C_XQ:C_XQ + XATT_W]
        rq = lax.rsqrt(_seg_mean(xq * xq, gm256) + EPS)
        qn = xq * rq
        qx = qn * xqg_ref[...]
        qxb = qx.astype(_MXU)
        dycb = dyc.astype(_MXU)
        pm = _mem_probs(_mm_nt(qxb, km_ref[...]))
        dp_all = _mm_nt(dycb, vm_ref[...])
        dsm = []
        for j in range(4):
            dp = dp_all[:, MEM_LEN * j:MEM_LEN * (j + 1)]
            dsm.append((pm[j] * (dp - jnp.sum(pm[j] * dp, axis=-1, keepdims=True))).astype(_MXU))
        ds_all = jnp.concatenate(dsm, axis=1)
        dvm_ref[...] += _mm_tn(jnp.concatenate([p.astype(_MXU) for p in pm], axis=1), dycb)
        dkm_ref[...] += _mm_tn(ds_all, qxb)
        dqx = _mm(ds_all, km_ref[...])
        gxqn_ref[...] += _col_sum(dqx * qn)
        dqn = dqx * xqg_ref[...]
        dproj_ref[:, C_XQ:C_XQ + XATT_W] = (rq * (dqn - qn * _seg_mean(dqn * qn, gm256))).astype(dproj_ref.dtype)

        gm128 = _group_matrix(KV_W)
        cv, s1v, s2v = c_ref[...], s1_ref[...], s2_ref[...]

        def head_norm(t):
            r = lax.rsqrt(_seg_mean(t * t, gm128) + EPS)
            return t * r, r

        qn_, qr_ = zip(head_norm(proj_ref[:, C_SQ:C_SQ + 128]), head_norm(proj_ref[:, C_SQ + 128:C_SQ + 256]))
        qrope = [_rope(qn_[h] * qg_ref[...], cv, s1v, s2v).astype(_MXU) for h in range(2)]
        kn, krr = head_norm(proj_ref[:, C_SK:C_SK + KV_W])
        kr = _rope(kn * kg_ref[...], cv, s1v, s2v)
        khn, _ = head_norm(kvh_ref[:, 0:KV_W])
        khr = _rope(khn * kg_ref[...], ch_ref[...], s1h_ref[...], s2h_ref[...])
        ka = _place_kv(jnp.concatenate([khr, kr], axis=0), 0.125)
        va = _place_kv(jnp.concatenate([kvh_ref[:, KV_W:2 * KV_W], proj_ref[:, C_SV:C_SV + KV_W]], axis=0), 1.0)
        lane128 = lax.broadcasted_iota(jnp.int32, (1, 128), 1)
        gsink = jnp.zeros((1, 128), F32)
        dk_band, dv_band, dq_blk = [], [], []
        for b in range(nb):
            mask = _swa_mask(first_tile & (b == 0)) if b == 0 else _swa_mask(False)
            band = slice(BLOCK * b, BLOCK * b + 2 * BLOCK)
            dka, dva, dsb = [], [], []
            for j in range(4):
                qh = qrope[j // 2][BLOCK * b:BLOCK * (b + 1)]
                doh = dyb[BLOCK * b:BLOCK * (b + 1), KV_W * (j // 2):KV_W * (j // 2 + 1)].astype(_MXU)
                p, psink = _swa_probs(qh, ka[j][band], mask, sink_ref[0, j])
                dp = _mm_nt(doh, va[j][band])
                delta = jnp.sum(p * dp, axis=-1, keepdims=True)
                ds = (p * (dp - delta)).astype(_MXU)
                gsink = gsink + jnp.where(lane128 == j, jnp.sum(-psink * delta), 0.0)
                dva.append(_mm_tn(p, doh))
                dka.append(_mm_tn(ds, qh))
                dsb.append(ds)
            dk_band.append(_unplace_kv(dka) * 0.125)
            dv_band.append(_unplace_kv(dva))
            dq_blk.append([_mm(jnp.concatenate(dsb[2 * h:2 * h + 2], axis=1),
                               jnp.concatenate([ka[2 * h][band], ka[2 * h + 1][band]], axis=0)) for h in range(2)])
        gsink_ref[...] += gsink
        dk_rows = [dk_band[b][BLOCK:] + (dk_band[b + 1][:BLOCK] if b + 1 < nb else dkcar_ref[...]) for b in range(nb)]
        dv_rows = [dv_band[b][BLOCK:] + (dv_band[b + 1][:BLOCK] if b + 1 < nb else dvcar_ref[...]) for b in range(nb)]
        dkcar_ref[...] = dk_band[0][:BLOCK]
        dvcar_ref[...] = dv_band[0][:BLOCK]
        dkg = _rope_bwd(jnp.concatenate(dk_rows, axis=0), cv, s1v, s2v)
        gkn = _col_sum(dkg * kn)
        dkn = dkg * kg_ref[...]
        dproj_ref[:, C_SK:C_SK + KV_W] = (krr * (dkn - kn * _seg_mean(dkn * kn, gm128))).astype(dproj_ref.dtype)
        dproj_ref[:, C_SV:C_SV + KV_W] = jnp.concatenate(dv_rows, axis=0).astype(dproj_ref.dtype)
        gqn = jnp.zeros((1, 128), F32)
        for h in range(2):
            dqg = _rope_bwd(jnp.concatenate([dq_blk[b][h] for b in range(nb)], axis=0), cv, s1v, s2v)
            gqn = gqn + _col_sum(dqg * qn_[h])
            dqn_ = dqg * qg_ref[...]
            dproj_ref[:, C_SQ + 128 * h:C_SQ + 128 * (h + 1)] = (
                qr_[h] * (dqn_ - qn_[h] * _seg_mean(dqn_ * qn_[h], gm128))).astype(dproj_ref.dtype)
        gqn_ref[...] += gqn
        gkn_ref[...] += gkn

        u = proj_ref[:, C_LRUX:C_LRUX + LRU_W]
        ext_ref[0:8, :] = jnp.where(first_tile, 0.0, uh_ref[...])
        ext_ref[8:8 + tm, :] = u
        us = [ext_ref[pl.ds(5 + k, tm), :] for k in range(CONV_K)]
        xc = cb_ref[...]
        for k in range(CONV_K):
            xc = xc + cw_ref[k:k + 1, :] * us[k]
        rg, ig, sp, a, sq = _lru_gates(xc, wg_ref, brg_ref[...], big_ref[...], lam_ref[...])
        hext_ref[0:8, :] = jnp.where(first_tile, 0.0, yah_ref[...])
        hext_ref[8:8 + tm, :] = ya_ref[...]
        hprev = hext_ref[pl.ds(7, tm), :]
        aext_ref[0:tm, :] = a
        an_scr[...] = aext_ref[pl.ds(1, tm), :]
        dh_scr[...] = dya
        dh_scr[tm - 1:tm, :] = dh_scr[tm - 1:tm, :] + gcar_ref[0:1, :]
        row8 = lax.broadcasted_iota(jnp.int32, (8, LRU_W), 0)

        def scan_step(gi, carry):
            r0 = pl.multiple_of((tm // 8 - 1 - gi) * 8, 8)
            av = an_scr[pl.ds(r0, 8), :]
            bv = dh_scr[pl.ds(r0, 8), :]
            for d in (1, 2, 4):
                a_sh = jnp.where(row8 < 8 - d, pltpu.roll(av, 8 - d, 0), 1.0)
                b_sh = jnp.where(row8 < 8 - d, pltpu.roll(bv, 8 - d, 0), 0.0)
                bv = bv + av * b_sh
                av = av * a_sh
            gv = bv + av * carry
            g_scr[pl.ds(r0, 8), :] = gv
            return gv[0:1, :]

        g0 = lax.fori_loop(0, tm // 8, scan_step, jnp.zeros((1, LRU_W), F32), unroll=True)
        gcar_ref[0:1, :] = a[0:1, :] * g0
        gv = g_scr[...]
        da = gv * hprev
        dig = gv * sq * xc
        dxc = gv * sq * ig
        dla = da * a - gv * (ig * xc) * ((a * a) / sq)
        drg = dla * ((-LRU_C) * sp)
        glam_ref[...] += _col_sum(dla * rg)
        dpr = drg * rg * (1.0 - rg)
        dpi = dig * ig * (1.0 - ig)
        gbrg_ref[...] += _col_sum(dpr)
        gbig_ref[...] += _col_sum(dpi)
        dpre0 = jnp.concatenate([dpr[:, :256], dpi[:, :256]], axis=1).astype(_MXU)
        dpre1 = jnp.concatenate([dpr[:, 256:], dpi[:, 256:]], axis=1).astype(_MXU)
        gwg_ref[0] += _mm_tn(xc[:, :256], dpre0)
        gwg_ref[1] += _mm_tn(xc[:, 256:], dpre1)
        dxc = dxc + jnp.concatenate([_mm_nt(dpre0, wg_ref[0]), _mm_nt(dpre1, wg_ref[1])], axis=1)
        gcb_ref[...] += _col_sum(dxc)
        for k in range(CONV_K):
            gcw_ref[k:k + 1, :] += _col_sum(dxc * us[k])
        dxc_ext[0:tm, :] = dxc
        du = jnp.zeros((tm, LRU_W), F32)
        for k in range(CONV_K):
            du = du + cw_ref[k:k + 1, :] * dxc_ext[pl.ds(3 - k, tm), :]
        dxc_ext[tm:tm + 8, :] = dxc[0:8, :]
        dproj_ref[:, C_LRUX:C_LRUX + LRU_W] = du.astype(dproj_ref.dtype)

        dxn = _mm(dproj_ref[...], win_ref[...])
        rx = lax.rsqrt(_row_mean(xv * xv) + EPS)
        xh = xv * rx
        gng_ref[...] += _col_sum(dxn * xh)
        dxh = dxn * ng_ref[...]
        gx_ref[...] = dov + rx * (dxh - xh * _row_mean(dxh * xh))

        @pl.when(i == nt - 1)
        def _():
            glam_ref[...] = glam_ref[...] * (LRU_C * _sigmoid(-lam_ref[...]))

    def rows(ncol, arr_cols_block=0):
        return pl.BlockSpec((tm, ncol), lambda i: (nt - 1 - i, arr_cols_block))

    def halo(nrow, ncol, colblk=0):
        per = tm // nrow
        return pl.BlockSpec((nrow, ncol), lambda i: (jnp.maximum((nt - 1 - i) * per - 1, 0), colblk))

    in_specs = [rows(D_MODEL), rows(D_MODEL), rows(D_IN), rows(LRU_W), rows(SWA_W), rows(XATT_W),
                rows(128), rows(128), rows(128),
                halo(8, LRU_W), halo(8, LRU_W), halo(BLOCK, 2 * KV_W, C_SK // (2 * KV_W)),
                halo(BLOCK, 128), halo(BLOCK, 128), halo(BLOCK, 128),
                _const_spec((1, D_MODEL)), _const_spec((D_IN, D_MODEL), True), _const_spec((CONV_K, LRU_W)),
                _const_spec((1, LRU_W)), _const_spec((2, 256, 512), True), _const_spec((1, LRU_W)),
                _const_spec((1, LRU_W)), _const_spec((1, LRU_W)), _const_spec((1, 128)), _const_spec((1, 128)),
                _const_spec((1, XATT_W)), pl.BlockSpec(memory_space=pltpu.SMEM),
                _const_spec((4 * MEM_LEN, XATT_W), True), _const_spec((4 * MEM_LEN, XATT_W), True),
                _const_spec((1, D_MODEL)), _const_spec((D_MODEL, D_MODEL), True)]
    small = [(2, 256, 512), (4 * MEM_LEN, XATT_W), (4 * MEM_LEN, XATT_W), (1, D_MODEL), (1, D_MODEL), (1, LRU_W), (1, LRU_W),
             (1, LRU_W), (1, LRU_W), (CONV_K, LRU_W), (1, 128), (1, 128), (1, XATT_W), (1, 128)]
    out_shape = (jax.ShapeDtypeStruct((seq, D_MODEL), F32), jax.ShapeDtypeStruct((seq, D_IN), _MXU)) + tuple(
        jax.ShapeDtypeStruct(s, F32) for s in small)
    out_specs = (rows(D_MODEL), rows(D_IN)) + tuple(_const_spec(s) for s in small)
    scratch = [pltpu.VMEM((tm + 8, LRU_W), F32), pltpu.VMEM((tm + 8, LRU_W), F32), pltpu.VMEM((tm + 8, LRU_W), F32),
               pltpu.VMEM((tm, LRU_W), F32), pltpu.VMEM((tm, LRU_W), F32), pltpu.VMEM((tm, LRU_W), F32),
               pltpu.VMEM((tm + 8, LRU_W), F32),
               pltpu.VMEM((8, LRU_W), F32), pltpu.VMEM((BLOCK, KV_W), F32), pltpu.VMEM((BLOCK, KV_W), F32)]
    return pl.pallas_call(
        body, name="layer_bwd", grid=(nt,), out_shape=out_shape, in_specs=in_specs, out_specs=out_specs,
        scratch_shapes=scratch,
        compiler_params=pltpu.CompilerParams(dimension_semantics=("arbitrary",), vmem_limit_bytes=VMEM_LIMIT),
    )(x, dout, proj, ya, yb, yc, rc, rs1, rs2, proj, ya, proj, rc, rs1, rs2,
      ng, win_t, cw, cb, wg, brg, big, lam, qg, kg, xqg, sinks, km, vm, og, wout)


def reduce_grads(g_in, g_out, g_kv, g_small):
    bigs = (g_in, g_out, g_kv)
    nbig = len(bigs)

    def body(b0, b1, b2, sm, o0, o1, o2, osm, r1_0, r1_1, r1_2, r1s, w0, w1, w2, r2_0, r2_1, r2_2, r2s, t0, t1, t2, ps,
             send, recv):
        big, outs = (b0, b1, b2), (o0, o1, o2)
        r1, wire, r2, wire2 = (r1_0, r1_1, r1_2), (w0, w1, w2), (r2_0, r2_1, r2_2), (t0, t1, t2)
        x, y, c = lax.axis_index("x"), lax.axis_index("y"), lax.axis_index("c")
        sibling = (x, y, 1 - c)
        near, far, diag = _partners(x, y, c)
        me, near_id, far_id, diag_id = _chip_of(x, y), _chip_of(*near), _chip_of(*far), _chip_of(*diag)

        def copy(k, src, dst, to):
            return pltpu.make_async_remote_copy(src_ref=src, dst_ref=dst, send_sem=send.at[k], recv_sem=recv.at[k],
                                                device_id=to, device_id_type=MESH)

        step1 = [copy(a, big[a].at[:, 1 - c], r1[a], sibling) for a in range(nbig)]
        step1.append(copy(nbig, sm.at[1 - c], r1s, sibling))
        for cp in step1:
            cp.start()
        step2 = []
        for a in range(nbig):
            copy(a, r1[a], r1[a], sibling).wait_recv()
            for k in range(N_CHIPS):
                r1[a][k] = big[a][k, c] + r1[a][k]
                wire[a][k] = r1[a][k].astype(wire[a].dtype)
            step2.append(copy(4 + 2 * a, wire[a].at[near_id], r2[a].at[0], (*near, c)))
            step2.append(copy(5 + 2 * a, wire[a].at[diag_id], r2[a].at[1], (*near, c)))
            step2[-2].start()
            step2[-1].start()
        copy(nbig, r1s, r1s, sibling).wait_recv()
        r1s[...] = sm[c] + r1s[...]
        step2.append(copy(10, r1s, r2s.at[0], (*near, c)))
        step2[-1].start()
        for a in range(nbig):
            copy(4 + 2 * a, r2[a].at[0], r2[a].at[0], sibling).wait_recv()
            copy(5 + 2 * a, r2[a].at[1], r2[a].at[1], sibling).wait_recv()
            r1[a][me] = r1[a][me] + r2[a][0].astype(F32)
            wire2[a][...] = (r1[a][far_id] + r2[a][1].astype(F32)).astype(wire2[a].dtype)
            step2.append(copy(11 + a, wire2[a], r2[a].at[2], (*far, c)))
            step2[-1].start()
        copy(10, r2s.at[0], r2s.at[0], sibling).wait_recv()
        ps[...] = r1s[...] + r2s[0]
        step2.append(copy(14, ps, r2s.at[1], (*far, c)))
        step2[-1].start()
        step3 = []
        for a in range(nbig):
            copy(11 + a, r2[a].at[2], r2[a].at[2], sibling).wait_recv()
            outs[a][c] = r1[a][me] + r2[a][2].astype(F32)
            step3.append(copy(15 + a, outs[a].at[c], outs[a].at[c], sibling))
            step3[-1].start()
        copy(14, r2s.at[1], r2s.at[1], sibling).wait_recv()
        osm[c] = ps[...] + r2s[1]
        step3.append(copy(15 + nbig, osm.at[c], osm.at[c], sibling))
        step3[-1].start()
        for a in range(nbig):
            other = outs[a].at[1 - c]
            copy(15 + a, other, other, sibling).wait_recv()
        other = osm.at[1 - c]
        copy(15 + nbig, other, other, sibling).wait_recv()
        for cp in step1 + step2 + step3:
            cp.wait_send()

    vm = pl.BlockSpec(memory_space=pltpu.VMEM)
    half = [b.shape[2:] for b in bigs]
    out_shape = tuple(jax.ShapeDtypeStruct((2,) + h, F32) for h in half) + (jax.ShapeDtypeStruct(g_small.shape, F32),)
    sm_half = g_small.shape[1:]
    scratch = ([pltpu.VMEM((N_CHIPS,) + h, F32) for h in half] + [pltpu.VMEM(sm_half, F32)]
               + [pltpu.VMEM((N_CHIPS,) + h, _WIRE) for h in half]
               + [pltpu.VMEM((3,) + h, _WIRE) for h in half] + [pltpu.VMEM((2,) + sm_half, F32)]
               + [pltpu.VMEM(h, _WIRE) for h in half] + [pltpu.VMEM(sm_half, F32)]
               + [pltpu.SemaphoreType.DMA((20,)), pltpu.SemaphoreType.DMA((20,))])
    return pl.pallas_call(
        body, name="reduce_grads", out_shape=out_shape, in_specs=[vm] * 4, out_specs=(vm,) * 4,
        scratch_shapes=scratch, compiler_params=pltpu.CompilerParams(vmem_limit_bytes=VMEM_LIMIT),
    )(g_in, g_out, g_kv, g_small)


def adamw(w, g, m, v, name):
    rows_, cols = w.shape
    tr = max(t for t in range(8, rows_ + 1, 8) if rows_ % t == 0 and t * cols * 4 <= ADAM_BLOCK_BYTES)

    def body(w_ref, g_ref, m_ref, v_ref, d_ref, nm_ref, nv_ref):
        gv = g_ref[...]
        nm = ADAM_B1 * m_ref[...] + (1.0 - ADAM_B1) * gv
        nv = ADAM_B2 * v_ref[...] + (1.0 - ADAM_B2) * (gv * gv)
        m_hat = nm / (1.0 - ADAM_B1 ** ADAM_STEP)
        v_hat = nv / (1.0 - ADAM_B2 ** ADAM_STEP)
        d_ref[...] = (-ADAM_LR) * (m_hat / (jnp.sqrt(v_hat) + ADAM_EPS) + ADAM_WD * w_ref[...])
        nm_ref[...] = nm
        nv_ref[...] = nv

    spec = pl.BlockSpec((tr, cols), lambda i: (i, 0))
    shp = jax.ShapeDtypeStruct(w.shape, F32)
    return pl.pallas_call(
        body, name=name, grid=(rows_ // tr,), out_shape=(shp, shp, shp), in_specs=[spec] * 4, out_specs=(spec,) * 3,
        compiler_params=pltpu.CompilerParams(dimension_semantics=("arbitrary",)),
    )(w, g, m, v)


SMALL = (("norm_g", 1024), ("mem_norm_g", 1024), ("conv_w", 512), ("conv_b", 512), ("w_rg", 32768), ("b_rg", 512),
         ("w_ig", 32768), ("b_ig", 512), ("lru_lambda", 512), ("q_norm_g", 128), ("k_norm_g", 128), ("sinks", 128),
         ("xq_norm_g", 128), ("xk_norm_g", 128), ("out_norm_g", 1024))
SMALL_ROWS = 576


def _pack(parts, rows_):
    flat = jnp.concatenate([p.reshape(-1) for p in parts])
    return jnp.pad(flat, (0, rows_ * 128 - flat.shape[0])).reshape(rows_, 128)


def _pad_to(v, n):
    v = v.reshape(-1)
    return jnp.pad(v, (0, n - v.shape[0]))


def _block_diag_gates(w_rg, w_ig):
    def bd(w4):
        z = jnp.zeros((4, HEAD, 4, HEAD), w4.dtype)
        idx = jnp.arange(4)
        return z.at[idx, :, idx, :].set(w4).reshape(256, 256)

    return jnp.stack([jnp.concatenate([bd(w_rg[4 * h:4 * h + 4]), bd(w_ig[4 * h:4 * h + 4])], axis=1) for h in (0, 1)])


def _diag_blocks(g):
    out = []
    for part in (0, 1):
        blocks = []
        for h in (0, 1):
            sub = g[h, :, 256 * part:256 * (part + 1)].reshape(4, HEAD, 4, HEAD)
            blocks.append(jnp.stack([sub[n, :, n, :] for n in range(4)]))
        out.append(jnp.concatenate(blocks, axis=0))
    return out


def _rope_tables(seq):
    pos = np.arange(seq, dtype=np.float32)
    inv_freq = (np.float32(ROPE_THETA) ** (-(np.arange(0, ROPE_DIM, 2, dtype=np.float32) / np.float32(ROPE_DIM)))
                ).astype(np.float32)
    ang = (pos[:, None] * inv_freq[None, :]).astype(np.float32)
    cos, sin = np.cos(ang).astype(np.float32), np.sin(ang).astype(np.float32)
    z = lambda n: np.zeros((seq, n), np.float32)
    c64 = np.concatenate([cos, cos, np.ones((seq, HEAD - ROPE_DIM), np.float32)], axis=1)
    s1_64 = np.concatenate([-sin, z(HEAD - 8)], axis=1)
    s2_64 = np.concatenate([z(8), sin, z(HEAD - ROPE_DIM)], axis=1)
    return tuple(jnp.asarray(np.concatenate([t, t], axis=1)) for t in (c64, s1_64, s2_64))


def kernel(x, mem, norm_g, mem_norm_g, w_in, conv_w, conv_b, w_rg, b_rg, w_ig, b_ig, lru_lambda, q_norm_g, k_norm_g, sinks, w_mem_kv, xq_norm_g, xk_norm_g, out_norm_g, w_out, loss_target, m_norm_g, m_mem_norm_g, m_w_in, m_conv_w, m_conv_b, m_w_rg, m_b_rg, m_w_ig, m_b_ig, m_lru_lambda, m_q_norm_g, m_k_norm_g, m_sinks, m_w_mem_kv, m_xq_norm_g, m_xk_norm_g, m_out_norm_g, m_w_out, v_norm_g, v_mem_norm_g, v_w_in, v_conv_w, v_conv_b, v_w_rg, v_b_rg, v_w_ig, v_b_ig, v_lru_lambda, v_q_norm_g, v_k_norm_g, v_sinks, v_w_mem_kv, v_xq_norm_g, v_xk_norm_g, v_out_norm_g, v_w_out):
    seq = x.shape[1]
    chip = 2 * lax.axis_index("x") + lax.axis_index("y")
    xs, tgt, mems = x[0], loss_target[0], mem[0]

    win_t_sh = w_in[0].T.astype(_MXU)
    cw_sh = jnp.pad(conv_w[0], ((0, 4), (0, 0)))
    win_t, wout, wkv, cw_all = gather_weights(win_t_sh, w_out[0].astype(_MXU), w_mem_kv[0].astype(_MXU), cw_sh)
    cw = cw_all.reshape(N_CHIPS, 8, 128)[:, :CONV_K].transpose(1, 0, 2).reshape(CONV_K, LRU_W)

    rc, rs1, rs2 = _rope_tables(seq)
    wg = _block_diag_gates(w_rg[0], w_ig[0]).astype(_MXU)
    qg = jnp.tile(q_norm_g, (1, 2))
    kg = jnp.tile(k_norm_g, (1, 2))
    xqg = jnp.tile(xq_norm_g, (1, 4))
    xkg = jnp.tile(xk_norm_g, (1, 4))

    km, vm = mem_fwd(mems, mem_norm_g, wkv, xkg)
    proj, ya, yb, yc, ycat, xn, dout, loss8 = layer_fwd(
        xs, tgt, rc, rs1, rs2, norm_g, win_t, cw, conv_b, wg, b_rg, b_ig, lru_lambda, qg, kg, xqg, sinks, km, vm,
        out_norm_g, wout)
    g_wout = wgrad_out(ycat, dout)
    (gx, dproj, g_wg, dkm, dvm, g_ng, g_og, g_cb, g_brg, g_big, g_lam, g_cw, g_qn, g_kn, g_xqn, g_sink) = layer_bwd(
        xs, dout, proj, ya, yb, yc, rc, rs1, rs2, norm_g, win_t, cw, conv_b, wg, b_rg, b_ig, lru_lambda, qg, kg, xqg,
        sinks, km, vm, out_norm_g, wout)
    g_win_t = wgrad_in(dproj, xn)
    g_wkv, g_mng, g_xkn = mem_bwd(mems, mem_norm_g, wkv, xkg, dkm, dvm)

    g_wrg, g_wig = _diag_blocks(g_wg)
    fold = lambda v, n: v.reshape(n, HEAD).sum(axis=0)
    small_g = _pack([g_ng, g_mng, g_cw, g_cb, g_wrg, g_brg, g_wig, g_big, g_lam, _pad_to(fold(g_qn, 2), 128),
                     _pad_to(fold(g_kn, 2), 128), g_sink, _pad_to(fold(g_xqn, 4), 128), _pad_to(fold(g_xkn, 4), 128),
                     g_og, loss8[0:1]], SMALL_ROWS)
    r_in, r_out, r_kv, r_small = reduce_grads(
        g_win_t.reshape(N_CHIPS, 2, D_IN // 8, D_MODEL), g_wout.reshape(N_CHIPS, 2, D_MODEL // 8, D_MODEL),
        g_wkv.reshape(N_CHIPS, 2, D_MODEL // 8, 2 * XATT_W), small_g.reshape(2, SMALL_ROWS // 2, 128))

    flat = r_small.reshape(-1)
    sizes = (1024, 1024, 2048, 512, 32768, 512, 32768, 512, 512, 128, 128, 128, 128, 128, 1024)
    offs = [0]
    for s in sizes:
        offs.append(offs[-1] + s)
    loss = flat[offs[-1]]
    piece = {name: flat[offs[k]:offs[k + 1]] for k, (name, _) in enumerate(SMALL)}
    g_cw_mine = lax.dynamic_slice(piece["conv_w"].reshape(CONV_K, LRU_W), (0, chip * 128), (CONV_K, 128))
    grads = {
        "norm_g": piece["norm_g"].reshape(1, 1024), "mem_norm_g": piece["mem_norm_g"].reshape(1, 1024),
        "w_in": r_in.reshape(D_IN // 4, D_MODEL).T[None], "conv_w": g_cw_mine[None],
        "conv_b": piece["conv_b"].reshape(1, 512), "w_rg": piece["w_rg"].reshape(1, 8, HEAD, HEAD),
        "b_rg": piece["b_rg"].reshape(1, 512), "w_ig": piece["w_ig"].reshape(1, 8, HEAD, HEAD),
        "b_ig": piece["b_ig"].reshape(1, 512), "lru_lambda": piece["lru_lambda"].reshape(1, 512),
        "q_norm_g": piece["q_norm_g"][:HEAD].reshape(1, HEAD), "k_norm_g": piece["k_norm_g"][:HEAD].reshape(1, HEAD),
        "sinks": piece["sinks"][:4].reshape(1, 4), "w_mem_kv": r_kv.reshape(D_MODEL // 4, 2 * XATT_W)[None],
        "xq_norm_g": piece["xq_norm_g"][:HEAD].reshape(1, HEAD), "xk_norm_g": piece["xk_norm_g"][:HEAD].reshape(1, HEAD),
        "out_norm_g": piece["out_norm_g"].reshape(1, 1024), "w_out": r_out.reshape(D_MODEL // 4, D_MODEL)[None],
    }
    weights = dict(norm_g=norm_g, mem_norm_g=mem_norm_g, w_in=w_in, conv_w=conv_w, conv_b=conv_b, w_rg=w_rg, b_rg=b_rg,
                   w_ig=w_ig, b_ig=b_ig, lru_lambda=lru_lambda, q_norm_g=q_norm_g, k_norm_g=k_norm_g, sinks=sinks,
                   w_mem_kv=w_mem_kv, xq_norm_g=xq_norm_g, xk_norm_g=xk_norm_g, out_norm_g=out_norm_g, w_out=w_out)
    ms = dict(norm_g=m_norm_g, mem_norm_g=m_mem_norm_g, w_in=m_w_in, conv_w=m_conv_w, conv_b=m_conv_b, w_rg=m_w_rg,
              b_rg=m_b_rg, w_ig=m_w_ig, b_ig=m_b_ig, lru_lambda=m_lru_lambda, q_norm_g=m_q_norm_g, k_norm_g=m_k_norm_g,
              sinks=m_sinks, w_mem_kv=m_w_mem_kv, xq_norm_g=m_xq_norm_g, xk_norm_g=m_xk_norm_g,
              out_norm_g=m_out_norm_g, w_out=m_w_out)
    vs = dict(norm_g=v_norm_g, mem_norm_g=v_mem_norm_g, w_in=v_w_in, conv_w=v_conv_w, conv_b=v_conv_b, w_rg=v_w_rg,
              b_rg=v_b_rg, w_ig=v_w_ig, b_ig=v_b_ig, lru_lambda=v_lru_lambda, q_norm_g=v_q_norm_g, k_norm_g=v_k_norm_g,
              sinks=v_sinks, w_mem_kv=v_w_mem_kv, xq_norm_g=v_xq_norm_g, xk_norm_g=v_xk_norm_g,
              out_norm_g=v_out_norm_g, w_out=v_w_out)

    delta, new_m, new_v = {}, {}, {}
    d2, m2, v2 = adamw(w_in[0].T, r_in.reshape(D_IN // 4, D_MODEL), m_w_in[0].T, v_w_in[0].T, "adamw_w_in")
    delta["w_in"], new_m["w_in"], new_v["w_in"] = d2.T[None], m2.T[None], v2.T[None]
    for name in ("w_mem_kv", "w_out"):
        shp = weights[name].shape
        d2, m2, v2 = adamw(weights[name][0], grads[name][0], ms[name][0], vs[name][0], "adamw_" + name)
        delta[name], new_m[name], new_v[name] = d2.reshape(shp), m2.reshape(shp), v2.reshape(shp)
    small_names = [n for n, _ in SMALL]
    packs = [_pack([_pad_to(d[n], sz) for n, sz in SMALL], SMALL_ROWS) for d in (weights, grads, ms, vs)]
    d_p, m_p, v_p = adamw(*packs, "adamw_small")
    offs2 = [0]
    for _, sz in SMALL:
        offs2.append(offs2[-1] + sz)
    for out_d, pk in ((delta, d_p), (new_m, m_p), (new_v, v_p)):
        fl = pk.reshape(-1)
        for k, n in enumerate(small_names):
            shp = weights[n].shape
            out_d[n] = fl[offs2[k]:offs2[k] + math.prod(shp)].reshape(shp)

    order = ("norm_g", "mem_norm_g", "w_in", "conv_w", "conv_b", "w_rg", "b_rg", "w_ig", "b_ig", "lru_lambda",
             "q_norm_g", "k_norm_g", "sinks", "w_mem_kv", "xq_norm_g", "xk_norm_g", "out_norm_g", "w_out")
    return (loss, gx[None], *[grads[n] for n in order], *[delta[n] for n in order], *[new_m[n] for n in order],
            *[new_v[n] for n in order])
```

```python
import functools
import math

import jax
import jax.numpy as jnp
import numpy as np
from jax import lax
from jax.experimental import pallas as pl
from jax.experimental.pallas import tpu as pltpu

F32 = jnp.float32
_MXU = jnp.bfloat16
_WIRE = jnp.bfloat16

D_MODEL = 1024
MEM_LEN = 256
HEAD = 64
LRU_W = 512
LRU_BLOCKS = 8
CONV_K = 4
LRU_C = 8.0
SWA_W = 256
KV_W = 128
XATT_W = 256
BLOCK = 128
D_IN = 2304
ROPE_THETA = 500000.0
ROPE_DIM = 16
EPS = 1e-6
NEG_INF = -1e30
C_LRUX, C_LRUG, C_SQ, C_SK, C_SV, C_SWAG, C_XQ, C_XG = 0, 512, 1024, 1280, 1408, 1536, 1792, 2048

ADAM_LR, ADAM_B1, ADAM_B2, ADAM_EPS, ADAM_WD, ADAM_STEP = 0.001, 0.9, 0.999, 1e-08, 0.01, 10

N_CHIPS = 4
ROW_TILE = 256
VMEM_LIMIT = 56 * 1024 * 1024
ADAM_BLOCK_BYTES = 1280 * 1024
GATHER_PIECES = (3, 1, 1)
MESH = pl.DeviceIdType.MESH


def _mm(a, b):
    return jnp.dot(a.astype(_MXU), b.astype(_MXU), preferred_element_type=F32)


def _mm_nt(a, b):
    return lax.dot_general(a.astype(_MXU), b.astype(_MXU), (((1,), (1,)), ((), ())), preferred_element_type=F32)


def _mm_tn(a, b):
    return lax.dot_general(a.astype(_MXU), b.astype(_MXU), (((0,), (0,)), ((), ())), preferred_element_type=F32)


def _group_matrix(width):
    r = lax.shift_right_logical(lax.broadcasted_iota(jnp.int32, (width, width), 0), 6)
    c = lax.shift_right_logical(lax.broadcasted_iota(jnp.int32, (width, width), 1), 6)
    return (r == c).astype(_MXU)


def _seg_mean(x, gm):
    return jnp.dot(x.astype(_MXU), gm, preferred_element_type=F32) * (1.0 / HEAD)


def _row_mean(x):
    return jnp.mean(x, axis=-1, keepdims=True)


def _col_sum(x):
    return jnp.sum(x, axis=0, keepdims=True)


def _sigmoid(x):
    return jax.nn.sigmoid(x)


def _softplus(z):
    e = jnp.exp(-jnp.abs(z))
    u = 1.0 + e
    log1p_e = jnp.where(u == 1.0, e, jnp.log(u) * (e / (u - 1.0)))
    return jnp.maximum(z, 0.0) + log1p_e


def _rope(t, c, s1, s2):
    return t * c + pltpu.roll(t, 120, 1) * s1 + pltpu.roll(t, 8, 1) * s2


def _rope_bwd(d, c, s1, s2):
    return d * c + pltpu.roll(d * s1, 8, 1) + pltpu.roll(d * s2, 120, 1)


def _lane_mask(width, lo, hi):
    lane = lax.broadcasted_iota(jnp.int32, (1, width), 1)
    return ((lane >= lo) & (lane < hi)).astype(F32)


def _swa_mask(first_block):
    qi = lax.broadcasted_iota(jnp.int32, (BLOCK, 2 * BLOCK), 0)
    kj = lax.broadcasted_iota(jnp.int32, (BLOCK, 2 * BLOCK), 1)
    rel = qi + BLOCK - kj
    ok = (rel >= 0) & (rel < BLOCK)
    return ok & (jnp.logical_not(first_block) | (kj >= BLOCK))


def _place_kv(t, scale):
    lo = t * (_lane_mask(KV_W, 0, HEAD) * scale)
    hi = t * (_lane_mask(KV_W, HEAD, KV_W) * scale)
    return [a.astype(_MXU) for a in (lo, pltpu.roll(lo, HEAD, 1), pltpu.roll(hi, HEAD, 1), hi)]


def _unplace_kv(d):
    return (_lane_mask(KV_W, 0, HEAD) * (d[0] + pltpu.roll(d[1], HEAD, 1))
            + _lane_mask(KV_W, HEAD, KV_W) * (d[3] + pltpu.roll(d[2], HEAD, 1)))


def _swa_probs(qh, ka, mask, sink):
    s = _mm_nt(qh, ka)
    s = jnp.where(mask, s, NEG_INF)
    m = jnp.maximum(jnp.max(s, axis=-1, keepdims=True), sink)
    p = jnp.exp(s - m)
    esink = jnp.exp(sink - m)
    inv = 1.0 / (jnp.sum(p, axis=-1, keepdims=True) + esink)
    return p * inv, esink * inv


def _mem_probs(s_all):
    out = []
    for j in range(4):
        s = s_all[:, MEM_LEN * j:MEM_LEN * (j + 1)]
        p = jnp.exp(s - jnp.max(s, axis=-1, keepdims=True))
        out.append(p * (1.0 / jnp.sum(p, axis=-1, keepdims=True)))
    return out


def _head_rows(t, scale):
    return jnp.concatenate([t * (_lane_mask(XATT_W, HEAD * j, HEAD * (j + 1)) * scale) for j in range(4)], axis=0)


def _lru_gates(xc, wg_ref, brg, big, lam):
    p0 = _mm(xc[:, :256], wg_ref[0])
    p1 = _mm(xc[:, 256:], wg_ref[1])
    rg = _sigmoid(jnp.concatenate([p0[:, :256], p1[:, :256]], axis=1) + brg)
    ig = _sigmoid(jnp.concatenate([p0[:, 256:], p1[:, 256:]], axis=1) + big)
    sp = _softplus(-lam)
    la = (-LRU_C) * rg * sp
    a = jnp.exp(la)
    th = jnp.tanh(la)
    one_minus_a2 = (-2.0 * th) / (1.0 - th)
    return rg, ig, sp, a, jnp.sqrt(one_minus_a2)


def _const_spec(shape, single=False):
    zeros = (0,) * len(shape)
    if single:
        return pl.BlockSpec(shape, lambda i: zeros, pipeline_mode=pl.Buffered(1))
    return pl.BlockSpec(shape, lambda i: zeros)


def _chip_of(x, y):
    return 2 * x + y


def _partners(x, y, c):
    north = c == 1
    near = (jnp.where(north, 1 - x, x), jnp.where(north, y, 1 - y))
    far = (jnp.where(north, x, 1 - x), jnp.where(north, 1 - y, y))
    return near, far, (1 - x, 1 - y)


def gather_weights(win_t, wout, wkv, convw):
    arrs = (win_t, wout, wkv)
    n = len(arrs)
    pieces = []
    for a, arr in enumerate(arrs):
        half = arr.shape[0] // 2
        step = half // GATHER_PIECES[a]
        pieces += [(a, off, step) for off in range(0, half, step)]
    npc = len(pieces)

    def body(a0, a1, a2, cw, o0, o1, o2, ocw, send, recv, lsem):
        ins, outs = (a0, a1, a2), (o0, o1, o2)
        x, y, c = lax.axis_index("x"), lax.axis_index("y"), lax.axis_index("c")
        sibling = (x, y, 1 - c)
        near, far, diag = _partners(x, y, c)
        chips = [near, far, diag]
        me = _chip_of(x, y)

        def landed(p, chip, half):
            a, off, rows_ = pieces[p]
            r = ins[a].shape[0]
            return outs[a].at[pl.ds(pl.multiple_of(chip * r + half * (r // 2) + off, 16), rows_)]

        def mine(p):
            a, off, rows_ = pieces[p]
            return ins[a].at[pl.ds(pl.multiple_of(c * (ins[a].shape[0] // 2) + off, 16), rows_)]

        def copy(k, src, dst, to):
            return pltpu.make_async_remote_copy(src_ref=src, dst_ref=dst, send_sem=send.at[k], recv_sem=recv.at[k],
                                                device_id=to, device_id_type=MESH)

        def cw_rows(chip):
            return ocw.at[pl.ds(pl.multiple_of(chip * 8, 8), 8)]

        locals_ = []
        for a in range(n):
            r = ins[a].shape[0]
            locals_.append(pltpu.make_async_copy(ins[a], outs[a].at[pl.ds(pl.multiple_of(me * r, 16), r)], lsem.at[a]))
        locals_.append(pltpu.make_async_copy(cw, cw_rows(me), lsem.at[n]))
        for cp in locals_:
            cp.start()

        sent = []
        for p in range(npc):
            for j in range(2):
                sent.append(copy(p * 6 + j, mine(p), landed(p, me, c), (*chips[j], c)))
        for j, chip in enumerate(chips):
            sent.append(copy(npc * 6 + j, cw, cw_rows(me), (*chip, c)))
        for cp in sent:
            cp.start()
        for p in range(npc):
            for j in range(3):
                got = landed(p, _chip_of(*chips[j]), c)
                copy(p * 6 + j, got, got, sibling).wait_recv()
                if j == 0:
                    sent.append(copy(p * 6 + 2, got, got, (*far, c)))
                    sent[-1].start()
                sent.append(copy(p * 6 + 3 + j, got, got, sibling))
                sent[-1].start()
        for p in range(npc):
            for j in range(3):
                got = landed(p, _chip_of(*chips[(1, 0, 2)[j]]), 1 - c)
                copy(p * 6 + 3 + j, got, got, sibling).wait_recv()
        for j, chip in enumerate(chips):
            got = cw_rows(_chip_of(*chip))
            copy(npc * 6 + j, got, got, (*chip, c)).wait_recv()
        for cp in sent:
            cp.wait_send()
        for cp in locals_:
            cp.wait()

    vm = pl.BlockSpec(memory_space=pltpu.VMEM)
    out_shape = tuple(jax.ShapeDtypeStruct((N_CHIPS * a.shape[0],) + a.shape[1:], a.dtype) for a in arrs) + (
        jax.ShapeDtypeStruct((N_CHIPS * 8, 128), F32),)
    n_rdma = npc * 6 + 3
    return pl.pallas_call(
        body, name="gather_weights", out_shape=out_shape,
        in_specs=[vm] * 4, out_specs=(vm,) * 4,
        scratch_shapes=[pltpu.SemaphoreType.DMA((n_rdma,)), pltpu.SemaphoreType.DMA((n_rdma,)),
                        pltpu.SemaphoreType.DMA((n + 1,))],
        compiler_params=pltpu.CompilerParams(vmem_limit_bytes=VMEM_LIMIT),
    )(win_t, wout, wkv, convw)


def mem_fwd(mem, mem_g, wkv, xk_g):
    def body(mem_ref, g_ref, w_ref, xk_ref, km_ref, vm_ref):
        mem_v = mem_ref[...]
        mn = mem_v * lax.rsqrt(_row_mean(mem_v * mem_v) + EPS) * g_ref[...]
        mkv = _mm(mn, w_ref[...])
        kpre = mkv[:, :XATT_W]
        gm = _group_matrix(XATT_W)
        km = kpre * lax.rsqrt(_seg_mean(kpre * kpre, gm) + EPS) * xk_ref[...]
        km_ref[...] = _head_rows(km, 0.125).astype(km_ref.dtype)
        vm_ref[...] = _head_rows(mkv[:, XATT_W:], 1.0).astype(vm_ref.dtype)

    vm = pl.BlockSpec(memory_space=pltpu.VMEM)
    rows_shape = jax.ShapeDtypeStruct((4 * MEM_LEN, XATT_W), _MXU)
    return pl.pallas_call(
        body, name="mem_fwd", out_shape=(rows_shape, rows_shape), in_specs=[vm] * 4, out_specs=(vm, vm),
    )(mem, mem_g, wkv, xk_g)


def mem_bwd(mem, mem_g, wkv, xk_g, dkm, dvm):
    def body(mem_ref, g_ref, w_ref, xk_ref, dkm_ref, dvm_ref, gw_ref, gg_ref, gxk_ref):
        mem_v = mem_ref[...]
        mh = mem_v * lax.rsqrt(_row_mean(mem_v * mem_v) + EPS)
        mn = mh * g_ref[...]
        mkv = _mm(mn, w_ref[...])
        kpre = mkv[:, :XATT_W]
        gm = _group_matrix(XATT_W)
        rk = lax.rsqrt(_seg_mean(kpre * kpre, gm) + EPS)
        kn = kpre * rk
        dk = jnp.zeros((MEM_LEN, XATT_W), F32)
        dv = jnp.zeros((MEM_LEN, XATT_W), F32)
        for j in range(4):
            mj = _lane_mask(XATT_W, HEAD * j, HEAD * (j + 1))
            dk = dk + dkm_ref[MEM_LEN * j:MEM_LEN * (j + 1), :] * (mj * 0.125)
            dv = dv + dvm_ref[MEM_LEN * j:MEM_LEN * (j + 1), :] * mj
        gxk_ref[...] = _col_sum(dk * kn)
        dkn = dk * xk_ref[...]
        dkpre = rk * (dkn - kn * _seg_mean(dkn * kn, gm))
        dmkv = jnp.concatenate([dkpre, dv], axis=1)
        gw_ref[...] = _mm_tn(mn, dmkv)
        dmn = _mm_nt(dmkv, w_ref[...])
        gg_ref[...] = _col_sum(dmn * mh)

    vm = pl.BlockSpec(memory_space=pltpu.VMEM)
    return pl.pallas_call(
        body, name="mem_bwd",
        out_shape=(jax.ShapeDtypeStruct((D_MODEL, 2 * XATT_W), F32), jax.ShapeDtypeStruct((1, D_MODEL), F32),
                   jax.ShapeDtypeStruct((1, XATT_W), F32)),
        in_specs=[vm] * 6, out_specs=(vm, vm, vm),
    )(mem, mem_g, wkv, xk_g, dkm, dvm)


def layer_fwd(x, tgt, rc, rs1, rs2, ng, win_t, cw, cb, wg, brg, big, lam, qg, kg, xqg, sinks, km, vm, og, wout):
    seq = x.shape[0]
    tm = min(ROW_TILE, seq)
    nt = seq // tm
    nb = tm // BLOCK

    def body(x_ref, t_ref, c_ref, s1_ref, s2_ref, ng_ref, win_ref, cw_ref, cb_ref, wg_ref, brg_ref, big_ref, lam_ref,
             qg_ref, kg_ref, xqg_ref, sink_ref, km_ref, vm_ref, og_ref, wout_ref,
             proj_ref, ya_ref, yb_ref, yc_ref, ycat_ref, xn_ref, dout_ref, loss_ref,
             ext_ref, a_scr, b_scr, hc_ref, kp_ref, vp_ref, lacc_ref):
        i = pl.program_id(0)

        @pl.when(i == 0)
        def _():
            ext_ref[0:8, :] = jnp.zeros((8, LRU_W), F32)
            hc_ref[...] = jnp.zeros_like(hc_ref)
            kp_ref[...] = jnp.zeros_like(kp_ref)
            vp_ref[...] = jnp.zeros_like(vp_ref)
            lacc_ref[...] = jnp.zeros_like(lacc_ref)

        xv = x_ref[...]
        xn = (xv * lax.rsqrt(_row_mean(xv * xv) + EPS) * ng_ref[...]).astype(_MXU)
        xn_ref[...] = xn.astype(xn_ref.dtype)
        proj_ref[...] = _mm_nt(xn, win_ref[...])

        u = proj_ref[:, C_LRUX:C_LRUX + LRU_W]
        ext_ref[8:8 + tm, :] = u
        xc = cb_ref[...]
        for k in range(CONV_K):
            xc = xc + cw_ref[k:k + 1, :] * ext_ref[pl.ds(5 + k, tm), :]
        ext_ref[0:8, :] = u[tm - 8:tm, :]
        rg, ig, sp, a, sq = _lru_gates(xc, wg_ref, brg_ref[...], big_ref[...], lam_ref[...])
        a_scr[...] = a
        b_scr[...] = sq * (ig * xc)
        row8 = lax.broadcasted_iota(jnp.int32, (8, LRU_W), 0)

        def scan_step(g, carry):
            r0 = pl.multiple_of(g * 8, 8)
            av = a_scr[pl.ds(r0, 8), :]
            bv = b_scr[pl.ds(r0, 8), :]
            for d in (1, 2, 4):
                a_sh = jnp.where(row8 >= d, pltpu.roll(av, d, 0), 1.0)
                b_sh = jnp.where(row8 >= d, pltpu.roll(bv, d, 0), 0.0)
                bv = bv + av * b_sh
                av = av * a_sh
            hv = bv + av * carry
            ya_ref[pl.ds(r0, 8), :] = hv
            return hv[7:8, :]

        hc_ref[0:1, :] = lax.fori_loop(0, tm // 8, scan_step, hc_ref[0:1, :], unroll=True)

        gm128 = _group_matrix(KV_W)
        cv, s1v, s2v = c_ref[...], s1_ref[...], s2_ref[...]

        def head_norm_rope(t, g):
            n = t * lax.rsqrt(_seg_mean(t * t, gm128) + EPS)
            return _rope(n * g, cv, s1v, s2v)

        qs_ = (head_norm_rope(proj_ref[:, C_SQ:C_SQ + 128], qg_ref[...]).astype(_MXU),
               head_norm_rope(proj_ref[:, C_SQ + 128:C_SQ + 256], qg_ref[...]).astype(_MXU))
        kr = head_norm_rope(proj_ref[:, C_SK:C_SK + KV_W], kg_ref[...])
        sv = proj_ref[:, C_SV:C_SV + KV_W]
        ka = _place_kv(jnp.concatenate([kp_ref[...], kr], axis=0), 0.125)
        va = _place_kv(jnp.concatenate([vp_ref[...], sv], axis=0), 1.0)
        kp_ref[...] = kr[tm - BLOCK:tm, :]
        vp_ref[...] = sv[tm - BLOCK:tm, :]
        for b in range(nb):
            mask = _swa_mask((i == 0) & (b == 0)) if b == 0 else _swa_mask(False)
            band = slice(BLOCK * b, BLOCK * b + 2 * BLOCK)
            ps = [_swa_probs(qs_[j // 2][BLOCK * b:BLOCK * (b + 1)], ka[j][band], mask, sink_ref[0, j])[0].astype(_MXU)
                  for j in range(4)]
            for h in range(2):
                yb_ref[BLOCK * b:BLOCK * (b + 1), KV_W * h:KV_W * (h + 1)] = _mm(
                    jnp.concatenate(ps[2 * h:2 * h + 2], axis=1),
                    jnp.concatenate([va[2 * h][band], va[2 * h + 1][band]], axis=0))

        gm256 = _group_matrix(XATT_W)
        xq = proj_ref[:, C_XQ:C_XQ + XATT_W]
        qx = xq * lax.rsqrt(_seg_mean(xq * xq, gm256) + EPS) * xqg_ref[...]
        pm = _mem_probs(_mm_nt(qx, km_ref[...]))
        yc = _mm(jnp.concatenate([p.astype(_MXU) for p in pm], axis=1), vm_ref[...])
        yc_ref[...] = yc

        def gated(y, g, gate):
            return y * lax.rsqrt(_row_mean(y * y) + EPS) * g * (gate * _sigmoid(gate))

        ogv = og_ref[...]
        za = gated(ya_ref[...], ogv[:, :512], proj_ref[:, C_LRUG:C_LRUG + LRU_W])
        zb = gated(yb_ref[...], ogv[:, 512:768], proj_ref[:, C_SWAG:C_SWAG + SWA_W])
        zc = gated(yc, ogv[:, 768:], proj_ref[:, C_XG:C_XG + XATT_W])
        ycat_ref[:, 0:512] = za.astype(ycat_ref.dtype)
        ycat_ref[:, 512:768] = zb.astype(ycat_ref.dtype)
        ycat_ref[:, 768:1024] = zc.astype(ycat_ref.dtype)
        out = xv + _mm(ycat_ref[...], wout_ref[...])
        err = out - t_ref[...]
        dout_ref[...] = (err * (1.0 / D_MODEL)).astype(dout_ref.dtype)
        lacc_ref[...] = lacc_ref[...] + (0.5 / D_MODEL) * jnp.sum(err * err)

        @pl.when(i == nt - 1)
        def _():
            loss_ref[...] = lacc_ref[...]

    def rows(ncol):
        return pl.BlockSpec((tm, ncol), lambda i: (i, 0))

    in_specs = [rows(D_MODEL), rows(D_MODEL), rows(128), rows(128), rows(128),
                _const_spec((1, D_MODEL)), _const_spec((D_IN, D_MODEL), True), _const_spec((CONV_K, LRU_W)),
                _const_spec((1, LRU_W)), _const_spec((2, 256, 512), True), _const_spec((1, LRU_W)),
                _const_spec((1, LRU_W)), _const_spec((1, LRU_W)), _const_spec((1, 128)), _const_spec((1, 128)),
                _const_spec((1, XATT_W)), pl.BlockSpec(memory_space=pltpu.SMEM),
                _const_spec((4 * MEM_LEN, XATT_W), True), _const_spec((4 * MEM_LEN, XATT_W), True),
                _const_spec((1, D_MODEL)), _const_spec((D_MODEL, D_MODEL), True)]
    out_shape = (jax.ShapeDtypeStruct((seq, D_IN), F32), jax.ShapeDtypeStruct((seq, LRU_W), F32),
                 jax.ShapeDtypeStruct((seq, SWA_W), F32), jax.ShapeDtypeStruct((seq, XATT_W), F32),
                 jax.ShapeDtypeStruct((seq, D_MODEL), _MXU), jax.ShapeDtypeStruct((seq, D_MODEL), _MXU),
                 jax.ShapeDtypeStruct((seq, D_MODEL), _MXU), jax.ShapeDtypeStruct((8, 128), F32))
    out_specs = (rows(D_IN), rows(LRU_W), rows(SWA_W), rows(XATT_W), rows(D_MODEL), rows(D_MODEL), rows(D_MODEL),
                 _const_spec((8, 128)))
    scratch = [pltpu.VMEM((tm + 8, LRU_W), F32), pltpu.VMEM((tm, LRU_W), F32), pltpu.VMEM((tm, LRU_W), F32),
               pltpu.VMEM((8, LRU_W), F32), pltpu.VMEM((BLOCK, KV_W), F32), pltpu.VMEM((BLOCK, KV_W), F32),
               pltpu.VMEM((8, 128), F32)]
    return pl.pallas_call(
        body, name="layer_fwd", grid=(nt,), out_shape=out_shape, in_specs=in_specs, out_specs=out_specs,
        scratch_shapes=scratch,
        compiler_params=pltpu.CompilerParams(dimension_semantics=("arbitrary",), vmem_limit_bytes=VMEM_LIMIT),
    )(x, tgt, rc, rs1, rs2, ng, win_t, cw, cb, wg, brg, big, lam, qg, kg, xqg, sinks, km, vm, og, wout)


def wgrad_out(ycat, dout):
    seq = ycat.shape[0]

    def body(y_ref, d_ref, o_ref):
        o_ref[...] = _mm_tn(y_ref[...], d_ref[...])

    return pl.pallas_call(
        body, name="wgrad_out", grid=(D_MODEL // 256,), out_shape=jax.ShapeDtypeStruct((D_MODEL, D_MODEL), F32),
        in_specs=[pl.BlockSpec((seq, 256), lambda j: (0, j)), _const_spec((seq, D_MODEL), True)],
        out_specs=pl.BlockSpec((256, D_MODEL), lambda j: (j, 0)),
        compiler_params=pltpu.CompilerParams(dimension_semantics=("arbitrary",), vmem_limit_bytes=VMEM_LIMIT),
    )(ycat, dout)


def wgrad_in(dproj, xn):
    seq = xn.shape[0]
    nblk = D_IN // 256

    def body(d_ref, x_ref, o_ref):
        o_ref[...] = _mm_tn(d_ref[...], x_ref[...])

    return pl.pallas_call(
        body, name="wgrad_in", grid=(nblk,), out_shape=jax.ShapeDtypeStruct((D_IN, D_MODEL), F32),
        in_specs=[pl.BlockSpec((seq, 256), lambda j: (0, j)), _const_spec((seq, D_MODEL), True)],
        out_specs=pl.BlockSpec((256, D_MODEL), lambda j: (j, 0)),
        compiler_params=pltpu.CompilerParams(dimension_semantics=("arbitrary",), vmem_limit_bytes=VMEM_LIMIT),
    )(dproj, xn)


def layer_bwd(x, dout, proj, ya, yb, yc, rc, rs1, rs2, ng, win_t, cw, cb, wg, brg, big, lam, qg, kg, xqg, sinks, km,
              vm, og, wout):
    seq = x.shape[0]
    tm = min(ROW_TILE, seq)
    nt = seq // tm
    nb = tm // BLOCK

    def body(x_ref, dout_ref, proj_ref, ya_ref, yb_ref, yc_ref, c_ref, s1_ref, s2_ref,
             uh_ref, yah_ref, kvh_ref, ch_ref, s1h_ref, s2h_ref,
             ng_ref, win_ref, cw_ref, cb_ref, wg_ref, brg_ref, big_ref, lam_ref, qg_ref, kg_ref, xqg_ref, sink_ref,
             km_ref, vm_ref, og_ref, wout_ref,
             gx_ref, dproj_ref, gwg_ref, dkm_ref, dvm_ref, gng_ref, gog_ref, gcb_ref, gbrg_ref, gbig_ref, glam_ref,
             gcw_ref, gqn_ref, gkn_ref, gxqn_ref, gsink_ref,
             ext_ref, hext_ref, aext_ref, an_scr, dh_scr, g_scr, dxc_ext, gcar_ref, dkcar_ref, dvcar_ref):
        i = pl.program_id(0)
        tile = nt - 1 - i
        first_tile = tile == 0

        @pl.when(i == 0)
        def _():
            for r in (gwg_ref, dkm_ref, dvm_ref, gng_ref, gog_ref, gcb_ref, gbrg_ref, gbig_ref, glam_ref, gcw_ref,
                      gqn_ref, gkn_ref, gxqn_ref, gsink_ref, gcar_ref, dkcar_ref, dvcar_ref):
                r[...] = jnp.zeros_like(r)
            dxc_ext[tm:tm + 8, :] = jnp.zeros((8, LRU_W), F32)
            aext_ref[tm:tm + 8, :] = jnp.zeros((8, LRU_W), F32)

        xv = x_ref[...]
        dov = dout_ref[...]
        dz = _mm_nt(dov, wout_ref[...])
        ogv = og_ref[...]

        def group_bwd(y, gate, g, dzg):
            r = lax.rsqrt(_row_mean(y * y) + EPS)
            n = y * r
            sg = _sigmoid(gate)
            dgate = dzg * (n * g) * (sg * (1.0 + gate * (1.0 - sg)))
            dng = dzg * (gate * sg)
            dn = dng * g
            return r * (dn - n * _row_mean(dn * n)), dgate, _col_sum(dng * n)

        dya, dga, goa = group_bwd(ya_ref[...], proj_ref[:, C_LRUG:C_LRUG + LRU_W], ogv[:, :512], dz[:, :512])
        dyb, dgb, gob = group_bwd(yb_ref[...], proj_ref[:, C_SWAG:C_SWAG + SWA_W], ogv[:, 512:768], dz[:, 512:768])
        dyc, dgc, goc = group_bwd(yc_ref[...], proj_ref[:, C_XG:C_XG + XATT_W], ogv[:, 768:], dz[:, 768:])
        gog_ref[...] += jnp.concatenate([goa, gob, goc], axis=1)
        dproj_ref[:, C_LRUG:C_LRUG + LRU_W] = dga.astype(dproj_ref.dtype)
        dproj_ref[:, C_SWAG:C_SWAG + SWA_W] = dgb.astype(dproj_ref.dtype)
        dproj_ref[:, C_XG:C_XG + XATT_W] = dgc.astype(dproj_ref.dtype)

        gm256 = _group_matrix(XATT_W)
        xq = proj_ref[:, C_XQ:C_XQ + XATT_W]
        rq = lax.rsqrt(_seg_mean(xq * xq, gm256) + EPS)
        qn = xq * rq
        qx = qn * xqg_ref[...]
        qxb = qx.astype(_MXU)
        dycb = dyc.astype(_MXU)
        pm = _mem_probs(_mm_nt(qxb, km_ref[...]))
        dp_all = _mm_nt(dycb, vm_ref[...])
        dsm = []
        for j in range(4):
            dp = dp_all[:, MEM_LEN * j:MEM_LEN * (j + 1)]
            dsm.append((pm[j] * (dp - jnp.sum(pm[j] * dp, axis=-1, keepdims=True))).astype(_MXU))
        ds_all = jnp.concatenate(dsm, axis=1)
        dvm_ref[...] += _mm_tn(jnp.concatenate([p.astype(_MXU) for p in pm], axis=1), dycb)
        dkm_ref[...] += _mm_tn(ds_all, qxb)
        dqx = _mm(ds_all, km_ref[...])
        gxqn_ref[...] += _col_sum(dqx * qn)
        dqn = dqx * xqg_ref[...]
        dproj_ref[:, C_XQ:C_XQ + XATT_W] = (rq * (dqn - qn * _seg_mean(dqn * qn, gm256))).astype(dproj_ref.dtype)

        gm128 = _group_matrix(KV_W)
        cv, s1v, s2v = c_ref[...], s1_ref[...], s2_ref[...]

        def head_norm(t):
            r = lax.rsqrt(_seg_mean(t * t, gm128) + EPS)
            return t * r, r

        qn_, qr_ = zip(head_norm(proj_ref[:, C_SQ:C_SQ + 128]), head_norm(proj_ref[:, C_SQ + 128:C_SQ + 256]))
        qrope = [_rope(qn_[h] * qg_ref[...], cv, s1v, s2v).astype(_MXU) for h in range(2)]
        kn, krr = head_norm(proj_ref[:, C_SK:C_SK + KV_W])
        kr = _rope(kn * kg_ref[...], cv, s1v, s2v)
        khn, _ = head_norm(kvh_ref[:, 0:KV_W])
        khr = _rope(khn * kg_ref[...], ch_ref[...], s1h_ref[...], s2h_ref[...])
        ka = _place_kv(jnp.concatenate([khr, kr], axis=0), 0.125)
        va = _place_kv(jnp.concatenate([kvh_ref[:, KV_W:2 * KV_W], proj_ref[:, C_SV:C_SV + KV_W]], axis=0), 1.0)
        lane128 = lax.broadcasted_iota(jnp.int32, (1, 128), 1)
        gsink = jnp.zeros((1, 128), F32)
        dk_band, dv_band, dq_blk = [], [], []
        for b in range(nb):
            mask = _swa_mask(first_tile & (b == 0)) if b == 0 else _swa_mask(False)
            band = slice(BLOCK * b, BLOCK * b + 2 * BLOCK)
            dka, dva, dsb = [], [], []
            for j in range(4):
                qh = qrope[j // 2][BLOCK * b:BLOCK * (b + 1)]
                doh = dyb[BLOCK * b:BLOCK * (b + 1), KV_W * (j // 2):KV_W * (j // 2 + 1)].astype(_MXU)
                p, psink = _swa_probs(qh, ka[j][band], mask, sink_ref[0, j])
                dp = _mm_nt(doh, va[j][band])
                delta = jnp.sum(p * dp, axis=-1, keepdims=True)
                ds = (p * (dp - delta)).astype(_MXU)
                gsink = gsink + jnp.where(lane128 == j, jnp.sum(-psink * delta), 0.0)
                dva.append(_mm_tn(p, doh))
                dka.append(_mm_tn(ds, qh))
                dsb.append(ds)
            dk_band.append(_unplace_kv(dka) * 0.125)
            dv_band.append(_unplace_kv(dva))
            dq_blk.append([_mm(jnp.concatenate(dsb[2 * h:2 * h + 2], axis=1),
                               jnp.concatenate([ka[2 * h][band], ka[2 * h + 1][band]], axis=0)) for h in range(2)])
        gsink_ref[...] += gsink
        dk_rows = [dk_band[b][BLOCK:] + (dk_band[b + 1][:BLOCK] if b + 1 < nb else dkcar_ref[...]) for b in range(nb)]
        dv_rows = [dv_band[b][BLOCK:] + (dv_band[b + 1][:BLOCK] if b + 1 < nb else dvcar_ref[...]) for b in range(nb)]
        dkcar_ref[...] = dk_band[0][:BLOCK]
        dvcar_ref[...] = dv_band[0][:BLOCK]
        dkg = _rope_bwd(jnp.concatenate(dk_rows, axis=0), cv, s1v, s2v)
        gkn = _col_sum(dkg * kn)
        dkn = dkg * kg_ref[...]
        dproj_ref[:, C_SK:C_SK + KV_W] = (krr * (dkn - kn * _seg_mean(dkn * kn, gm128))).astype(dproj_ref.dtype)
        dproj_ref[:, C_SV:C_SV + KV_W] = jnp.concatenate(dv_rows, axis=0).astype(dproj_ref.dtype)
        gqn = jnp.zeros((1, 128), F32)
        for h in range(2):
            dqg = _rope_bwd(jnp.concatenate([dq_blk[b][h] for b in range(nb)], axis=0), cv, s1v, s2v)
            gqn = gqn + _col_sum(dqg * qn_[h])
            dqn_ = dqg * qg_ref[...]
            dproj_ref[:, C_SQ + 128 * h:C_SQ + 128 * (h + 1)] = (
                qr_[h] * (dqn_ - qn_[h] * _seg_mean(dqn_ * qn_[h], gm128))).astype(dproj_ref.dtype)
        gqn_ref[...] += gqn
        gkn_ref[...] += gkn

        u = proj_ref[:, C_LRUX:C_LRUX + LRU_W]
        ext_ref[0:8, :] = jnp.where(first_tile, 0.0, uh_ref[...])
        ext_ref[8:8 + tm, :] = u
        us = [ext_ref[pl.ds(5 + k, tm), :] for k in range(CONV_K)]
        xc = cb_ref[...]
        for k in range(CONV_K):
            xc = xc + cw_ref[k:k + 1, :] * us[k]
        rg, ig, sp, a, sq = _lru_gates(xc, wg_ref, brg_ref[...], big_ref[...], lam_ref[...])
        hext_ref[0:8, :] = jnp.where(first_tile, 0.0, yah_ref[...])
        hext_ref[8:8 + tm, :] = ya_ref[...]
        hprev = hext_ref[pl.ds(7, tm), :]
        aext_ref[0:tm, :] = a
        an_scr[...] = aext_ref[pl.ds(1, tm), :]
        dh_scr[...] = dya
        dh_scr[tm - 1:tm, :] = dh_scr[tm - 1:tm, :] + gcar_ref[0:1, :]
        row8 = lax.broadcasted_iota(jnp.int32, (8, LRU_W), 0)

        def scan_step(gi, carry):
            r0 = pl.multiple_of((tm // 8 - 1 - gi) * 8, 8)
            av = an_scr[pl.ds(r0, 8), :]
            bv = dh_scr[pl.ds(r0, 8), :]
            for d in (1, 2, 4):
                a_sh = jnp.where(row8 < 8 - d, pltpu.roll(av, 8 - d, 0), 1.0)
                b_sh = jnp.where(row8 < 8 - d, pltpu.roll(bv, 8 - d, 0), 0.0)
                bv = bv + av * b_sh
                av = av * a_sh
            gv = bv + av * carry
            g_scr[pl.ds(r0, 8), :] = gv
            return gv[0:1, :]

        g0 = lax.fori_loop(0, tm // 8, scan_step, jnp.zeros((1, LRU_W), F32), unroll=True)
        gcar_ref[0:1, :] = a[0:1, :] * g0
        gv = g_scr[...]
        da = gv * hprev
        dig = gv * sq * xc
        dxc = gv * sq * ig
        dla = da * a - gv * (ig * xc) * ((a * a) / sq)
        drg = dla * ((-LRU_C) * sp)
        glam_ref[...] += _col_sum(dla * rg)
        dpr = drg * rg * (1.0 - rg)
        dpi = dig * ig * (1.0 - ig)
        gbrg_ref[...] += _col_sum(dpr)
        gbig_ref[...] += _col_sum(dpi)
        dpre0 = jnp.concatenate([dpr[:, :256], dpi[:, :256]], axis=1).astype(_MXU)
        dpre1 = jnp.concatenate([dpr[:, 256:], dpi[:, 256:]], axis=1).astype(_MXU)
        gwg_ref[0] += _mm_tn(xc[:, :256], dpre0)
        gwg_ref[1] += _mm_tn(xc[:, 256:], dpre1)
        dxc = dxc + jnp.concatenate([_mm_nt(dpre0, wg_ref[0]), _mm_nt(dpre1, wg_ref[1])], axis=1)
        gcb_ref[...] += _col_sum(dxc)
        for k in range(CONV_K):
            gcw_ref[k:k + 1, :] += _col_sum(dxc * us[k])
        dxc_ext[0:tm, :] = dxc
        du = jnp.zeros((tm, LRU_W), F32)
        for k in range(CONV_K):
            du = du + cw_ref[k:k + 1, :] * dxc_ext[pl.ds(3 - k, tm), :]
        dxc_ext[tm:tm + 8, :] = dxc[0:8, :]
        dproj_ref[:, C_LRUX:C_LRUX + LRU_W] = du.astype(dproj_ref.dtype)

        dxn = _mm(dproj_ref[...], win_ref[...])
        rx = lax.rsqrt(_row_mean(xv * xv) + EPS)
        xh = xv * rx
        gng_ref[...] += _col_sum(dxn * xh)
        dxh = dxn * ng_ref[...]
        gx_ref[...] = dov.astype(F32) + rx * (dxh - xh * _row_mean(dxh * xh))

        @pl.when(i == nt - 1)
        def _():
            glam_ref[...] = glam_ref[...] * (LRU_C * _sigmoid(-lam_ref[...]))

    def rows(ncol, arr_cols_block=0):
        return pl.BlockSpec((tm, ncol), lambda i: (nt - 1 - i, arr_cols_block))

    def halo(nrow, ncol, colblk=0):
        per = tm // nrow
        return pl.BlockSpec((nrow, ncol), lambda i: (jnp.maximum((nt - 1 - i) * per - 1, 0), colblk))

    in_specs = [rows(D_MODEL), rows(D_MODEL), rows(D_IN), rows(LRU_W), rows(SWA_W), rows(XATT_W),
                rows(128), rows(128), rows(128),
                halo(8, LRU_W), halo(8, LRU_W), halo(BLOCK, 2 * KV_W, C_SK // (2 * KV_W)),
                halo(BLOCK, 128), halo(BLOCK, 128), halo(BLOCK, 128),
                _const_spec((1, D_MODEL)), _const_spec((D_IN, D_MODEL), True), _const_spec((CONV_K, LRU_W)),
                _const_spec((1, LRU_W)), _const_spec((2, 256, 512), True), _const_spec((1, LRU_W)),
                _const_spec((1, LRU_W)), _const_spec((1, LRU_W)), _const_spec((1, 128)), _const_spec((1, 128)),
                _const_spec((1, XATT_W)), pl.BlockSpec(memory_space=pltpu.SMEM),
                _const_spec((4 * MEM_LEN, XATT_W), True), _const_spec((4 * MEM_LEN, XATT_W), True),
                _const_spec((1, D_MODEL)), _const_spec((D_MODEL, D_MODEL), True)]
    small = [(2, 256, 512), (4 * MEM_LEN, XATT_W), (4 * MEM_LEN, XATT_W), (1, D_MODEL), (1, D_MODEL), (1, LRU_W), (1, LRU_W),
             (1, LRU_W), (1, LRU_W), (CONV_K, LRU_W), (1, 128), (1, 128), (1, XATT_W), (1, 128)]
    out_shape = (jax.ShapeDtypeStruct((seq, D_MODEL), F32), jax.ShapeDtypeStruct((seq, D_IN), _MXU)) + tuple(
        jax.ShapeDtypeStruct(s, F32) for s in small)
    out_specs = (rows(D_MODEL), rows(D_IN)) + tuple(_const_spec(s) for s in small)
    scratch = [pltpu.VMEM((tm + 8, LRU_W), F32), pltpu.VMEM((tm + 8, LRU_W), F32), pltpu.VMEM((tm + 8, LRU_W), F32),
               pltpu.VMEM((tm, LRU_W), F32), pltpu.VMEM((tm, LRU_W), F32), pltpu.VMEM((tm, LRU_W), F32),
               pltpu.VMEM((tm + 8, LRU_W), F32),
               pltpu.VMEM((8, LRU_W), F32), pltpu.VMEM((BLOCK, KV_W), F32), pltpu.VMEM((BLOCK, KV_W), F32)]
    return pl.pallas_call(
        body, name="layer_bwd", grid=(nt,), out_shape=out_shape, in_specs=in_specs, out_specs=out_specs,
        scratch_shapes=scratch,
        compiler_params=pltpu.CompilerParams(dimension_semantics=("arbitrary",), vmem_limit_bytes=VMEM_LIMIT),
    )(x, dout, proj, ya, yb, yc, rc, rs1, rs2, proj, ya, proj, rc, rs1, rs2,
      ng, win_t, cw, cb, wg, brg, big, lam, qg, kg, xqg, sinks, km, vm, og, wout)


def reduce_grads(g_in, g_out, g_kv, g_small):
    bigs = (g_in, g_out, g_kv)
    nbig = len(bigs)

    def body(b0, b1, b2, sm, o0, o1, o2, osm, r1_0, r1_1, r1_2, r1s, w0, w1, w2, r2_0, r2_1, r2_2, r2s, t0, t1, t2, ps,
             own0, own1, own2, send, recv, lsem):
        big, outs, own = (b0, b1, b2), (o0, o1, o2), (own0, own1, own2)
        r1, wire, r2, wire2 = (r1_0, r1_1, r1_2), (w0, w1, w2), (r2_0, r2_1, r2_2), (t0, t1, t2)
        x, y, c = lax.axis_index("x"), lax.axis_index("y"), lax.axis_index("c")
        sibling = (x, y, 1 - c)
        near, far, diag = _partners(x, y, c)
        me, near_id, far_id, diag_id = _chip_of(x, y), _chip_of(*near), _chip_of(*far), _chip_of(*diag)

        def copy(k, src, dst, to):
            return pltpu.make_async_remote_copy(src_ref=src, dst_ref=dst, send_sem=send.at[k], recv_sem=recv.at[k],
                                                device_id=to, device_id_type=MESH)

        step1 = [copy(a, big[a].at[:, 1 - c], r1[a], sibling) for a in range(nbig)]
        step1.append(copy(nbig, sm.at[1 - c], r1s, sibling))
        loads = [pltpu.make_async_copy(big[a].at[:, c], own[a], lsem.at[a]) for a in range(nbig)]
        for cp in step1 + loads:
            cp.start()
        step2 = []
        for a in range(nbig):
            loads[a].wait()
            copy(a, r1[a], r1[a], sibling).wait_recv()
            for k in range(N_CHIPS):
                r1[a][k] = own[a][k] + r1[a][k]
                wire[a][k] = r1[a][k].astype(wire[a].dtype)
            step2.append(copy(4 + 2 * a, wire[a].at[near_id], r2[a].at[0], (*near, c)))
            step2.append(copy(5 + 2 * a, wire[a].at[diag_id], r2[a].at[1], (*near, c)))
            step2[-2].start()
            step2[-1].start()
        copy(nbig, r1s, r1s, sibling).wait_recv()
        r1s[...] = sm[c] + r1s[...]
        step2.append(copy(10, r1s, r2s.at[0], (*near, c)))
        step2[-1].start()
        for a in range(nbig):
            copy(4 + 2 * a, r2[a].at[0], r2[a].at[0], sibling).wait_recv()
            copy(5 + 2 * a, r2[a].at[1], r2[a].at[1], sibling).wait_recv()
            r1[a][me] = r1[a][me] + r2[a][0].astype(F32)
            wire2[a][...] = (r1[a][far_id] + r2[a][1].astype(F32)).astype(wire2[a].dtype)
            step2.append(copy(11 + a, wire2[a], r2[a].at[2], (*far, c)))
            step2[-1].start()
        copy(10, r2s.at[0], r2s.at[0], sibling).wait_recv()
        ps[...] = r1s[...] + r2s[0]
        step2.append(copy(14, ps, r2s.at[1], (*far, c)))
        step2[-1].start()
        step3 = []
        for a in range(nbig):
            copy(11 + a, r2[a].at[2], r2[a].at[2], sibling).wait_recv()
            outs[a][c] = r1[a][me] + r2[a][2].astype(F32)
            step3.append(copy(15 + a, outs[a].at[c], outs[a].at[c], sibling))
            step3[-1].start()
        copy(14, r2s.at[1], r2s.at[1], sibling).wait_recv()
        osm[c] = ps[...] + r2s[1]
        step3.append(copy(15 + nbig, osm.at[c], osm.at[c], sibling))
        step3[-1].start()
        for a in range(nbig):
            other = outs[a].at[1 - c]
            copy(15 + a, other, other, sibling).wait_recv()
        other = osm.at[1 - c]
        copy(15 + nbig, other, other, sibling).wait_recv()
        for cp in step1 + step2 + step3:
            cp.wait_send()

    vm = pl.BlockSpec(memory_space=pltpu.VMEM)
    half = [b.shape[2:] for b in bigs]
    out_shape = tuple(jax.ShapeDtypeStruct((2,) + h, F32) for h in half) + (jax.ShapeDtypeStruct(g_small.shape, F32),)
    sm_half = g_small.shape[1:]
    scratch = ([pltpu.VMEM((N_CHIPS,) + h, F32) for h in half] + [pltpu.VMEM(sm_half, F32)]
               + [pltpu.VMEM((N_CHIPS,) + h, _WIRE) for h in half]
               + [pltpu.VMEM((3,) + h, _WIRE) for h in half] + [pltpu.VMEM((2,) + sm_half, F32)]
               + [pltpu.VMEM(h, _WIRE) for h in half] + [pltpu.VMEM(sm_half, F32)]
               + [pltpu.VMEM((N_CHIPS,) + h, F32) for h in half]
               + [pltpu.SemaphoreType.DMA((20,)), pltpu.SemaphoreType.DMA((20,)), pltpu.SemaphoreType.DMA((nbig,))])
    hbm = pl.BlockSpec(memory_space=pl.ANY)
    return pl.pallas_call(
        body, name="reduce_grads", out_shape=out_shape, in_specs=[hbm] * nbig + [vm], out_specs=(vm,) * 4,
        scratch_shapes=scratch, compiler_params=pltpu.CompilerParams(vmem_limit_bytes=VMEM_LIMIT),
    )(g_in, g_out, g_kv, g_small)


def adamw(w, g, m, v, name):
    rows_, cols = w.shape
    tr = max(t for t in range(8, rows_ + 1, 8) if rows_ % t == 0 and t * cols * 4 <= ADAM_BLOCK_BYTES)

    def body(w_ref, g_ref, m_ref, v_ref, d_ref, nm_ref, nv_ref):
        gv = g_ref[...]
        nm = ADAM_B1 * m_ref[...] + (1.0 - ADAM_B1) * gv
        nv = ADAM_B2 * v_ref[...] + (1.0 - ADAM_B2) * (gv * gv)
        m_hat = nm / (1.0 - ADAM_B1 ** ADAM_STEP)
        v_hat = nv / (1.0 - ADAM_B2 ** ADAM_STEP)
        d_ref[...] = (-ADAM_LR) * (m_hat / (jnp.sqrt(v_hat) + ADAM_EPS) + ADAM_WD * w_ref[...])
        nm_ref[...] = nm
        nv_ref[...] = nv

    spec = pl.BlockSpec((tr, cols), lambda i: (i, 0))
    shp = jax.ShapeDtypeStruct(w.shape, F32)
    return pl.pallas_call(
        body, name=name, grid=(rows_ // tr,), out_shape=(shp, shp, shp), in_specs=[spec] * 4, out_specs=(spec,) * 3,
        compiler_params=pltpu.CompilerParams(dimension_semantics=("arbitrary",)),
    )(w, g, m, v)


SMALL = (("norm_g", 1024), ("mem_norm_g", 1024), ("conv_w", 512), ("conv_b", 512), ("w_rg", 32768), ("b_rg", 512),
         ("w_ig", 32768), ("b_ig", 512), ("lru_lambda", 512), ("q_norm_g", 128), ("k_norm_g", 128), ("sinks", 128),
         ("xq_norm_g", 128), ("xk_norm_g", 128), ("out_norm_g", 1024))
SMALL_ROWS = 576


def _pack(parts, rows_):
    flat = jnp.concatenate([p.reshape(-1) for p in parts])
    return jnp.pad(flat, (0, rows_ * 128 - flat.shape[0])).reshape(rows_, 128)


def _pad_to(v, n):
    v = v.reshape(-1)
    return jnp.pad(v, (0, n - v.shape[0]))


def _block_diag_gates(w_rg, w_ig):
    def bd(w4):
        z = jnp.zeros((4, HEAD, 4, HEAD), w4.dtype)
        idx = jnp.arange(4)
        return z.at[idx, :, idx, :].set(w4).reshape(256, 256)

    return jnp.stack([jnp.concatenate([bd(w_rg[4 * h:4 * h + 4]), bd(w_ig[4 * h:4 * h + 4])], axis=1) for h in (0, 1)])


def _diag_blocks(g):
    out = []
    for part in (0, 1):
        blocks = []
        for h in (0, 1):
            sub = g[h, :, 256 * part:256 * (part + 1)].reshape(4, HEAD, 4, HEAD)
            blocks.append(jnp.stack([sub[n, :, n, :] for n in range(4)]))
        out.append(jnp.concatenate(blocks, axis=0))
    return out


def _rope_tables(seq):
    pos = np.arange(seq, dtype=np.float32)
    inv_freq = (np.float32(ROPE_THETA) ** (-(np.arange(0, ROPE_DIM, 2, dtype=np.float32) / np.float32(ROPE_DIM)))
                ).astype(np.float32)
    ang = (pos[:, None] * inv_freq[None, :]).astype(np.float32)
    cos, sin = np.cos(ang).astype(np.float32), np.sin(ang).astype(np.float32)
    z = lambda n: np.zeros((seq, n), np.float32)
    c64 = np.concatenate([cos, cos, np.ones((seq, HEAD - ROPE_DIM), np.float32)], axis=1)
    s1_64 = np.concatenate([-sin, z(HEAD - 8)], axis=1)
    s2_64 = np.concatenate([z(8), sin, z(HEAD - ROPE_DIM)], axis=1)
    return tuple(jnp.asarray(np.concatenate([t, t], axis=1)) for t in (c64, s1_64, s2_64))


def kernel(x, mem, norm_g, mem_norm_g, w_in, conv_w, conv_b, w_rg, b_rg, w_ig, b_ig, lru_lambda, q_norm_g, k_norm_g, sinks, w_mem_kv, xq_norm_g, xk_norm_g, out_norm_g, w_out, loss_target, m_norm_g, m_mem_norm_g, m_w_in, m_conv_w, m_conv_b, m_w_rg, m_b_rg, m_w_ig, m_b_ig, m_lru_lambda, m_q_norm_g, m_k_norm_g, m_sinks, m_w_mem_kv, m_xq_norm_g, m_xk_norm_g, m_out_norm_g, m_w_out, v_norm_g, v_mem_norm_g, v_w_in, v_conv_w, v_conv_b, v_w_rg, v_b_rg, v_w_ig, v_b_ig, v_lru_lambda, v_q_norm_g, v_k_norm_g, v_sinks, v_w_mem_kv, v_xq_norm_g, v_xk_norm_g, v_out_norm_g, v_w_out):
    seq = x.shape[1]
    chip = 2 * lax.axis_index("x") + lax.axis_index("y")
    xs, tgt, mems = x[0], loss_target[0], mem[0]

    win_t_sh = w_in[0].T.astype(_MXU)
    cw_sh = jnp.pad(conv_w[0], ((0, 4), (0, 0)))
    win_t, wout, wkv, cw_all = gather_weights(win_t_sh, w_out[0].astype(_MXU), w_mem_kv[0].astype(_MXU), cw_sh)
    cw = cw_all.reshape(N_CHIPS, 8, 128)[:, :CONV_K].transpose(1, 0, 2).reshape(CONV_K, LRU_W)

    rc, rs1, rs2 = _rope_tables(seq)
    wg = _block_diag_gates(w_rg[0], w_ig[0]).astype(_MXU)
    qg = jnp.tile(q_norm_g, (1, 2))
    kg = jnp.tile(k_norm_g, (1, 2))
    xqg = jnp.tile(xq_norm_g, (1, 4))
    xkg = jnp.tile(xk_norm_g, (1, 4))

    km, vm = mem_fwd(mems, mem_norm_g, wkv, xkg)
    proj, ya, yb, yc, ycat, xn, dout, loss8 = layer_fwd(
        xs, tgt, rc, rs1, rs2, norm_g, win_t, cw, conv_b, wg, b_rg, b_ig, lru_lambda, qg, kg, xqg, sinks, km, vm,
        out_norm_g, wout)
    g_wout = wgrad_out(ycat, dout)
    (gx, dproj, g_wg, dkm, dvm, g_ng, g_og, g_cb, g_brg, g_big, g_lam, g_cw, g_qn, g_kn, g_xqn, g_sink) = layer_bwd(
        xs, dout, proj, ya, yb, yc, rc, rs1, rs2, norm_g, win_t, cw, conv_b, wg, b_rg, b_ig, lru_lambda, qg, kg, xqg,
        sinks, km, vm, out_norm_g, wout)
    g_win_t = wgrad_in(dproj, xn)
    g_wkv, g_mng, g_xkn = mem_bwd(mems, mem_norm_g, wkv, xkg, dkm, dvm)

    g_wrg, g_wig = _diag_blocks(g_wg)
    fold = lambda v, n: v.reshape(n, HEAD).sum(axis=0)
    small_g = _pack([g_ng, g_mng, g_cw, g_cb, g_wrg, g_brg, g_wig, g_big, g_lam, _pad_to(fold(g_qn, 2), 128),
                     _pad_to(fold(g_kn, 2), 128), g_sink, _pad_to(fold(g_xqn, 4), 128), _pad_to(fold(g_xkn, 4), 128),
                     g_og, loss8[0:1]], SMALL_ROWS)
    r_in, r_out, r_kv, r_small = reduce_grads(
        g_win_t.reshape(N_CHIPS, 2, D_IN // 8, D_MODEL), g_wout.reshape(N_CHIPS, 2, D_MODEL // 8, D_MODEL),
        g_wkv.reshape(N_CHIPS, 2, D_MODEL // 8, 2 * XATT_W), small_g.reshape(2, SMALL_ROWS // 2, 128))

    flat = r_small.reshape(-1)
    sizes = (1024, 1024, 2048, 512, 32768, 512, 32768, 512, 512, 128, 128, 128, 128, 128, 1024)
    offs = [0]
    for s in sizes:
        offs.append(offs[-1] + s)
    loss = flat[offs[-1]]
    piece = {name: flat[offs[k]:offs[k + 1]] for k, (name, _) in enumerate(SMALL)}
    g_cw_mine = lax.dynamic_slice(piece["conv_w"].reshape(CONV_K, LRU_W), (0, chip * 128), (CONV_K, 128))
    grads = {
        "norm_g": piece["norm_g"].reshape(1, 1024), "mem_norm_g": piece["mem_norm_g"].reshape(1, 1024),
        "w_in": r_in.reshape(D_IN // 4, D_MODEL).T[None], "conv_w": g_cw_mine[None],
        "conv_b": piece["conv_b"].reshape(1, 512), "w_rg": piece["w_rg"].reshape(1, 8, HEAD, HEAD),
        "b_rg": piece["b_rg"].reshape(1, 512), "w_ig": piece["w_ig"].reshape(1, 8, HEAD, HEAD),
        "b_ig": piece["b_ig"].reshape(1, 512), "lru_lambda": piece["lru_lambda"].reshape(1, 512),
        "q_norm_g": piece["q_norm_g"][:HEAD].reshape(1, HEAD), "k_norm_g": piece["k_norm_g"][:HEAD].reshape(1, HEAD),
        "sinks": piece["sinks"][:4].reshape(1, 4), "w_mem_kv": r_kv.reshape(D_MODEL // 4, 2 * XATT_W)[None],
        "xq_norm_g": piece["xq_norm_g"][:HEAD].reshape(1, HEAD), "xk_norm_g": piece["xk_norm_g"][:HEAD].reshape(1, HEAD),
        "out_norm_g": piece["out_norm_g"].reshape(1, 1024), "w_out": r_out.reshape(D_MODEL // 4, D_MODEL)[None],
    }
    weights = dict(norm_g=norm_g, mem_norm_g=mem_norm_g, w_in=w_in, conv_w=conv_w, conv_b=conv_b, w_rg=w_rg, b_rg=b_rg,
                   w_ig=w_ig, b_ig=b_ig, lru_lambda=lru_lambda, q_norm_g=q_norm_g, k_norm_g=k_norm_g, sinks=sinks,
                   w_mem_kv=w_mem_kv, xq_norm_g=xq_norm_g, xk_norm_g=xk_norm_g, out_norm_g=out_norm_g, w_out=w_out)
    ms = dict(norm_g=m_norm_g, mem_norm_g=m_mem_norm_g, w_in=m_w_in, conv_w=m_conv_w, conv_b=m_conv_b, w_rg=m_w_rg,
              b_rg=m_b_rg, w_ig=m_w_ig, b_ig=m_b_ig, lru_lambda=m_lru_lambda, q_norm_g=m_q_norm_g, k_norm_g=m_k_norm_g,
              sinks=m_sinks, w_mem_kv=m_w_mem_kv, xq_norm_g=m_xq_norm_g, xk_norm_g=m_xk_norm_g,
              out_norm_g=m_out_norm_g, w_out=m_w_out)
    vs = dict(norm_g=v_norm_g, mem_norm_g=v_mem_norm_g, w_in=v_w_in, conv_w=v_conv_w, conv_b=v_conv_b, w_rg=v_w_rg,
              b_rg=v_b_rg, w_ig=v_w_ig, b_ig=v_b_ig, lru_lambda=v_lru_lambda, q_norm_g=v_q_norm_g, k_norm_g=v_k_norm_g,
              sinks=v_sinks, w_mem_kv=v_w_mem_kv, xq_norm_g=v_xq_norm_g, xk_norm_g=v_xk_norm_g,
              out_norm_g=v_out_norm_g, w_out=v_w_out)

    delta, new_m, new_v = {}, {}, {}
    d2, m2, v2 = adamw(w_in[0].T, r_in.reshape(D_IN // 4, D_MODEL), m_w_in[0].T, v_w_in[0].T, "adamw_w_in")
    delta["w_in"], new_m["w_in"], new_v["w_in"] = d2.T[None], m2.T[None], v2.T[None]
    for name in ("w_mem_kv", "w_out"):
        shp = weights[name].shape
        d2, m2, v2 = adamw(weights[name][0], grads[name][0], ms[name][0], vs[name][0], "adamw_" + name)
        delta[name], new_m[name], new_v[name] = d2.reshape(shp), m2.reshape(shp), v2.reshape(shp)
    small_names = [n for n, _ in SMALL]
    packs = [_pack([_pad_to(d[n], sz) for n, sz in SMALL], SMALL_ROWS) for d in (weights, grads, ms, vs)]
    d_p, m_p, v_p = adamw(*packs, "adamw_small")
    offs2 = [0]
    for _, sz in SMALL:
        offs2.append(offs2[-1] + sz)
    for out_d, pk in ((delta, d_p), (new_m, m_p), (new_v, v_p)):
        fl = pk.reshape(-1)
        for k, n in enumerate(small_names):
            shp = weights[n].shape
            out_d[n] = fl[offs2[k]:offs2[k] + math.prod(shp)].reshape(shp)

    order = ("norm_g", "mem_norm_g", "w_in", "conv_w", "conv_b", "w_rg", "b_rg", "w_ig", "b_ig", "lru_lambda",
             "q_norm_g", "k_norm_g", "sinks", "w_mem_kv", "xq_norm_g", "xk_norm_g", "out_norm_g", "w_out")
    return (loss, gx[None], *[grads[n] for n in order], *[delta[n] for n in order], *[new_m[n] for n in order],
            *[new_v[n] for n in order])
```

```python
import functools
import math

import jax
import jax.numpy as jnp
import numpy as np
from jax import lax
from jax.experimental import pallas as pl
from jax.experimental.pallas import tpu as pltpu

F32 = jnp.float32
_MXU = jnp.bfloat16
_WIRE = jnp.bfloat16

D_MODEL = 1024
MEM_LEN = 256
HEAD = 64
LRU_W = 512
LRU_BLOCKS = 8
CONV_K = 4
LRU_C = 8.0
SWA_W = 256
KV_W = 128
XATT_W = 256
BLOCK = 128
D_IN = 2304
ROPE_THETA = 500000.0
ROPE_DIM = 16
EPS = 1e-6
NEG_INF = -1e30
C_LRUX, C_LRUG, C_SQ, C_SK, C_SV, C_SWAG, C_XQ, C_XG = 0, 512, 1024, 1280, 1408, 1536, 1792, 2048

ADAM_LR, ADAM_B1, ADAM_B2, ADAM_EPS, ADAM_WD, ADAM_STEP = 0.001, 0.9, 0.999, 1e-08, 0.01, 10

N_CHIPS = 4
ROW_TILE = 256
VMEM_LIMIT = 56 * 1024 * 1024
ADAM_BLOCK_BYTES = 1280 * 1024
GATHER_PIECES = (3, 1, 1)
MESH = pl.DeviceIdType.MESH


def _mm(a, b):
    return jnp.dot(a.astype(_MXU), b.astype(_MXU), preferred_element_type=F32)


def _mm_nt(a, b):
    return lax.dot_general(a.astype(_MXU), b.astype(_MXU), (((1,), (1,)), ((), ())), preferred_element_type=F32)


def _mm_tn(a, b):
    return lax.dot_general(a.astype(_MXU), b.astype(_MXU), (((0,), (0,)), ((), ())), preferred_element_type=F32)


def _group_matrix(width):
    r = lax.shift_right_logical(lax.broadcasted_iota(jnp.int32, (width, width), 0), 6)
    c = lax.shift_right_logical(lax.broadcasted_iota(jnp.int32, (width, width), 1), 6)
    return (r == c).astype(_MXU)


def _seg_mean(x, gm):
    return jnp.dot(x.astype(_MXU), gm, preferred_element_type=F32) * (1.0 / HEAD)


def _row_mean(x):
    return jnp.mean(x, axis=-1, keepdims=True)


def _col_sum(x):
    return jnp.sum(x, axis=0, keepdims=True)


def _sigmoid(x):
    return jax.nn.sigmoid(x)


def _softplus(z):
    e = jnp.exp(-jnp.abs(z))
    u = 1.0 + e
    log1p_e = jnp.where(u == 1.0, e, jnp.log(u) * (e / (u - 1.0)))
    return jnp.maximum(z, 0.0) + log1p_e


def _rope(t, c, s1, s2):
    return t * c + pltpu.roll(t, 120, 1) * s1 + pltpu.roll(t, 8, 1) * s2


def _rope_bwd(d, c, s1, s2):
    return d * c + pltpu.roll(d * s1, 8, 1) + pltpu.roll(d * s2, 120, 1)


def _lane_mask(width, lo, hi):
    lane = lax.broadcasted_iota(jnp.int32, (1, width), 1)
    return ((lane >= lo) & (lane < hi)).astype(F32)


def _swa_mask(first_block):
    qi = lax.broadcasted_iota(jnp.int32, (BLOCK, 2 * BLOCK), 0)
    kj = lax.broadcasted_iota(jnp.int32, (BLOCK, 2 * BLOCK), 1)
    rel = qi + BLOCK - kj
    ok = (rel >= 0) & (rel < BLOCK)
    return ok & (jnp.logical_not(first_block) | (kj >= BLOCK))


def _place_kv(t, scale):
    lo = t * (_lane_mask(KV_W, 0, HEAD) * scale)
    hi = t * (_lane_mask(KV_W, HEAD, KV_W) * scale)
    return [a.astype(_MXU) for a in (lo, pltpu.roll(lo, HEAD, 1), pltpu.roll(hi, HEAD, 1), hi)]


def _unplace_kv(d):
    return (_lane_mask(KV_W, 0, HEAD) * (d[0] + pltpu.roll(d[1], HEAD, 1))
            + _lane_mask(KV_W, HEAD, KV_W) * (d[3] + pltpu.roll(d[2], HEAD, 1)))


def _swa_probs(qh, ka, mask, sink):
    s = _mm_nt(qh, ka)
    s = jnp.where(mask, s, NEG_INF)
    m = jnp.maximum(jnp.max(s, axis=-1, keepdims=True), sink)
    p = jnp.exp(s - m)
    esink = jnp.exp(sink - m)
    inv = 1.0 / (jnp.sum(p, axis=-1, keepdims=True) + esink)
    return p * inv, esink * inv


def _mem_probs(s_all):
    out = []
    for j in range(4):
        s = s_all[:, MEM_LEN * j:MEM_LEN * (j + 1)]
        p = jnp.exp(s - jnp.max(s, axis=-1, keepdims=True))
        out.append(p * (1.0 / jnp.sum(p, axis=-1, keepdims=True)))
    return out


def _head_rows(t, scale):
    return jnp.concatenate([t * (_lane_mask(XATT_W, HEAD * j, HEAD * (j + 1)) * scale) for j in range(4)], axis=0)


def _lru_gates(xc, wg_ref, brg, big, lam):
    p0 = _mm(xc[:, :256], wg_ref[0])
    p1 = _mm(xc[:, 256:], wg_ref[1])
    rg = _sigmoid(jnp.concatenate([p0[:, :256], p1[:, :256]], axis=1) + brg)
    ig = _sigmoid(jnp.concatenate([p0[:, 256:], p1[:, 256:]], axis=1) + big)
    sp = _softplus(-lam)
    la = (-LRU_C) * rg * sp
    a = jnp.exp(la)
    th = jnp.tanh(la)
    one_minus_a2 = (-2.0 * th) / (1.0 - th)
    return rg, ig, sp, a, jnp.sqrt(one_minus_a2)


def _const_spec(shape, single=False):
    zeros = (0,) * len(shape)
    if single:
        return pl.BlockSpec(shape, lambda i: zeros, pipeline_mode=pl.Buffered(1))
    return pl.BlockSpec(shape, lambda i: zeros)


def _chip_of(x, y):
    return 2 * x + y


def _partners(x, y, c):
    north = c == 1
    near = (jnp.where(north, 1 - x, x), jnp.where(north, y, 1 - y))
    far = (jnp.where(north, x, 1 - x), jnp.where(north, 1 - y, y))
    return near, far, (1 - x, 1 - y)


def gather_weights(win_t, wout, wkv, convw):
    arrs = (win_t, wout, wkv)
    n = len(arrs)
    pieces = []
    for a, arr in enumerate(arrs):
        half = arr.shape[0] // 2
        step = half // GATHER_PIECES[a]
        pieces += [(a, off, step) for off in range(0, half, step)]
    npc = len(pieces)

    def body(a0, a1, a2, cw, o0, o1, o2, ocw, send, recv, lsem):
        ins, outs = (a0, a1, a2), (o0, o1, o2)
        x, y, c = lax.axis_index("x"), lax.axis_index("y"), lax.axis_index("c")
        sibling = (x, y, 1 - c)
        near, far, diag = _partners(x, y, c)
        chips = [near, far, diag]
        me = _chip_of(x, y)

        def landed(p, chip, half):
            a, off, rows_ = pieces[p]
            r = ins[a].shape[0]
            return outs[a].at[pl.ds(pl.multiple_of(chip * r + half * (r // 2) + off, 16), rows_)]

        def mine(p):
            a, off, rows_ = pieces[p]
            return ins[a].at[pl.ds(pl.multiple_of(c * (ins[a].shape[0] // 2) + off, 16), rows_)]

        def copy(k, src, dst, to):
            return pltpu.make_async_remote_copy(src_ref=src, dst_ref=dst, send_sem=send.at[k], recv_sem=recv.at[k],
                                                device_id=to, device_id_type=MESH)

        def cw_rows(chip):
            return ocw.at[pl.ds(pl.multiple_of(chip * 8, 8), 8)]

        locals_ = []
        for a in range(n):
            r = ins[a].shape[0]
            locals_.append(pltpu.make_async_copy(ins[a], outs[a].at[pl.ds(pl.multiple_of(me * r, 16), r)], lsem.at[a]))
        locals_.append(pltpu.make_async_copy(cw, cw_rows(me), lsem.at[n]))
        for cp in locals_:
            cp.start()

        sent = []
        for p in range(npc):
            for j in range(2):
                sent.append(copy(p * 6 + j, mine(p), landed(p, me, c), (*chips[j], c)))
        for j, chip in enumerate(chips):
            sent.append(copy(npc * 6 + j, cw, cw_rows(me), (*chip, c)))
        for cp in sent:
            cp.start()
        for p in range(npc):
            for j in range(3):
                got = landed(p, _chip_of(*chips[j]), c)
                copy(p * 6 + j, got, got, sibling).wait_recv()
                if j == 0:
                    sent.append(copy(p * 6 + 2, got, got, (*far, c)))
                    sent[-1].start()
                sent.append(copy(p * 6 + 3 + j, got, got, sibling))
                sent[-1].start()
        for p in range(npc):
            for j in range(3):
                got = landed(p, _chip_of(*chips[(1, 0, 2)[j]]), 1 - c)
                copy(p * 6 + 3 + j, got, got, sibling).wait_recv()
        for j, chip in enumerate(chips):
            got = cw_rows(_chip_of(*chip))
            copy(npc * 6 + j, got, got, (*chip, c)).wait_recv()
        for cp in sent:
            cp.wait_send()
        for cp in locals_:
            cp.wait()

    vm = pl.BlockSpec(memory_space=pltpu.VMEM)
    out_shape = tuple(jax.ShapeDtypeStruct((N_CHIPS * a.shape[0],) + a.shape[1:], a.dtype) for a in arrs) + (
        jax.ShapeDtypeStruct((N_CHIPS * 8, 128), F32),)
    n_rdma = npc * 6 + 3
    return pl.pallas_call(
        body, name="gather_weights", out_shape=out_shape,
        in_specs=[vm] * 4, out_specs=(vm,) * 4,
        scratch_shapes=[pltpu.SemaphoreType.DMA((n_rdma,)), pltpu.SemaphoreType.DMA((n_rdma,)),
                        pltpu.SemaphoreType.DMA((n + 1,))],
        compiler_params=pltpu.CompilerParams(vmem_limit_bytes=VMEM_LIMIT),
    )(win_t, wout, wkv, convw)


def mem_fwd(mem, mem_g, wkv, xk_g):
    def body(mem_ref, g_ref, w_ref, xk_ref, km_ref, vm_ref):
        mem_v = mem_ref[...]
        mn = mem_v * lax.rsqrt(_row_mean(mem_v * mem_v) + EPS) * g_ref[...]
        mkv = _mm(mn, w_ref[...])
        kpre = mkv[:, :XATT_W]
        gm = _group_matrix(XATT_W)
        km = kpre * lax.rsqrt(_seg_mean(kpre * kpre, gm) + EPS) * xk_ref[...]
        km_ref[...] = _head_rows(km, 0.125).astype(km_ref.dtype)
        vm_ref[...] = _head_rows(mkv[:, XATT_W:], 1.0).astype(vm_ref.dtype)

    vm = pl.BlockSpec(memory_space=pltpu.VMEM)
    rows_shape = jax.ShapeDtypeStruct((4 * MEM_LEN, XATT_W), _MXU)
    return pl.pallas_call(
        body, name="mem_fwd", out_shape=(rows_shape, rows_shape), in_specs=[vm] * 4, out_specs=(vm, vm),
    )(mem, mem_g, wkv, xk_g)


def mem_bwd(mem, mem_g, wkv, xk_g, dkm, dvm):
    def body(mem_ref, g_ref, w_ref, xk_ref, dkm_ref, dvm_ref, gw_ref, gg_ref, gxk_ref):
        mem_v = mem_ref[...]
        mh = mem_v * lax.rsqrt(_row_mean(mem_v * mem_v) + EPS)
        mn = mh * g_ref[...]
        mkv = _mm(mn, w_ref[...])
        kpre = mkv[:, :XATT_W]
        gm = _group_matrix(XATT_W)
        rk = lax.rsqrt(_seg_mean(kpre * kpre, gm) + EPS)
        kn = kpre * rk
        dk = jnp.zeros((MEM_LEN, XATT_W), F32)
        dv = jnp.zeros((MEM_LEN, XATT_W), F32)
        for j in range(4):
            mj = _lane_mask(XATT_W, HEAD * j, HEAD * (j + 1))
            dk = dk + dkm_ref[MEM_LEN * j:MEM_LEN * (j + 1), :] * (mj * 0.125)
            dv = dv + dvm_ref[MEM_LEN * j:MEM_LEN * (j + 1), :] * mj
        gxk_ref[...] = _col_sum(dk * kn)
        dkn = dk * xk_ref[...]
        dkpre = rk * (dkn - kn * _seg_mean(dkn * kn, gm))
        dmkv = jnp.concatenate([dkpre, dv], axis=1)
        gw_ref[...] = _mm_tn(mn, dmkv)
        dmn = _mm_nt(dmkv, w_ref[...])
        gg_ref[...] = _col_sum(dmn * mh)

    vm = pl.BlockSpec(memory_space=pltpu.VMEM)
    return pl.pallas_call(
        body, name="mem_bwd",
        out_shape=(jax.ShapeDtypeStruct((D_MODEL, 2 * XATT_W), F32), jax.ShapeDtypeStruct((1, D_MODEL), F32),
                   jax.ShapeDtypeStruct((1, XATT_W), F32)),
        in_specs=[vm] * 6, out_specs=(vm, vm, vm),
    )(mem, mem_g, wkv, xk_g, dkm, dvm)


def layer_fwd(x, tgt, rc, rs1, rs2, ng, win_t, cw, cb, wg, brg, big, lam, qg, kg, xqg, sinks, km, vm, og, wout):
    seq = x.shape[0]
    tm = min(ROW_TILE, seq)
    nt = seq // tm
    nb = tm // BLOCK

    def body(x_ref, t_ref, c_ref, s1_ref, s2_ref, ng_ref, win_ref, cw_ref, cb_ref, wg_ref, brg_ref, big_ref, lam_ref,
             qg_ref, kg_ref, xqg_ref, sink_ref, km_ref, vm_ref, og_ref, wout_ref,
             proj_ref, ya_ref, yb_ref, yc_ref, ycat_ref, xn_ref, dout_ref, pswa_ref, pmem_ref, psink_ref, gates_ref,
             a_ref, loss_ref,
             ext_ref, b_scr, hc_ref, kp_ref, vp_ref, lacc_ref):
        i = pl.program_id(0)

        @pl.when(i == 0)
        def _():
            ext_ref[0:8, :] = jnp.zeros((8, LRU_W), F32)
            hc_ref[...] = jnp.zeros_like(hc_ref)
            kp_ref[...] = jnp.zeros_like(kp_ref)
            vp_ref[...] = jnp.zeros_like(vp_ref)
            lacc_ref[...] = jnp.zeros_like(lacc_ref)

        xv = x_ref[...]
        xn = (xv * lax.rsqrt(_row_mean(xv * xv) + EPS) * ng_ref[...]).astype(_MXU)
        xn_ref[...] = xn.astype(xn_ref.dtype)
        proj_ref[...] = _mm_nt(xn, win_ref[...])

        u = proj_ref[:, C_LRUX:C_LRUX + LRU_W]
        ext_ref[8:8 + tm, :] = u
        xc = cb_ref[...]
        for k in range(CONV_K):
            xc = xc + cw_ref[k:k + 1, :] * ext_ref[pl.ds(5 + k, tm), :]
        ext_ref[0:8, :] = u[tm - 8:tm, :]
        rg, ig, sp, a, sq = _lru_gates(xc, wg_ref, brg_ref[...], big_ref[...], lam_ref[...])
        for k, t in enumerate((xc, rg, ig, sq)):
            gates_ref[:, LRU_W * k:LRU_W * (k + 1)] = t.astype(gates_ref.dtype)
        a_ref[...] = a
        b_scr[...] = sq * (ig * xc)
        row8 = lax.broadcasted_iota(jnp.int32, (8, LRU_W), 0)

        def scan_step(g, carry):
            r0 = pl.multiple_of(g * 8, 8)
            av = a_ref[pl.ds(r0, 8), :]
            bv = b_scr[pl.ds(r0, 8), :]
            for d in (1, 2, 4):
                a_sh = jnp.where(row8 >= d, pltpu.roll(av, d, 0), 1.0)
                b_sh = jnp.where(row8 >= d, pltpu.roll(bv, d, 0), 0.0)
                bv = bv + av * b_sh
                av = av * a_sh
            hv = bv + av * carry
            ya_ref[pl.ds(r0, 8), :] = hv
            return hv[7:8, :]

        hc_ref[0:1, :] = lax.fori_loop(0, tm // 8, scan_step, hc_ref[0:1, :], unroll=True)

        gm128 = _group_matrix(KV_W)
        cv, s1v, s2v = c_ref[...], s1_ref[...], s2_ref[...]

        def head_norm_rope(t, g):
            n = t * lax.rsqrt(_seg_mean(t * t, gm128) + EPS)
            return _rope(n * g, cv, s1v, s2v)

        qs_ = (head_norm_rope(proj_ref[:, C_SQ:C_SQ + 128], qg_ref[...]).astype(_MXU),
               head_norm_rope(proj_ref[:, C_SQ + 128:C_SQ + 256], qg_ref[...]).astype(_MXU))
        kr = head_norm_rope(proj_ref[:, C_SK:C_SK + KV_W], kg_ref[...])
        sv = proj_ref[:, C_SV:C_SV + KV_W]
        ka = _place_kv(jnp.concatenate([kp_ref[...], kr], axis=0), 0.125)
        va = _place_kv(jnp.concatenate([vp_ref[...], sv], axis=0), 1.0)
        kp_ref[...] = kr[tm - BLOCK:tm, :]
        vp_ref[...] = sv[tm - BLOCK:tm, :]
        lane128 = lax.broadcasted_iota(jnp.int32, (1, 128), 1)
        for b in range(nb):
            mask = _swa_mask((i == 0) & (b == 0)) if b == 0 else _swa_mask(False)
            band = slice(BLOCK * b, BLOCK * b + 2 * BLOCK)
            blk = slice(BLOCK * b, BLOCK * (b + 1))
            psink = jnp.zeros((BLOCK, 128), F32)
            for j in range(4):
                p, pk = _swa_probs(qs_[j // 2][blk], ka[j][band], mask, sink_ref[0, j])
                pswa_ref[blk, 2 * BLOCK * j:2 * BLOCK * (j + 1)] = p.astype(pswa_ref.dtype)
                psink = jnp.where(lane128 == j, pk, psink)
            psink_ref[blk, :] = psink
            for h in range(2):
                yb_ref[blk, KV_W * h:KV_W * (h + 1)] = _mm(
                    pswa_ref[blk, 4 * BLOCK * h:4 * BLOCK * (h + 1)],
                    jnp.concatenate([va[2 * h][band], va[2 * h + 1][band]], axis=0))

        gm256 = _group_matrix(XATT_W)
        xq = proj_ref[:, C_XQ:C_XQ + XATT_W]
        qx = xq * lax.rsqrt(_seg_mean(xq * xq, gm256) + EPS) * xqg_ref[...]
        pm = _mem_probs(_mm_nt(qx, km_ref[...]))
        for j in range(4):
            pmem_ref[:, MEM_LEN * j:MEM_LEN * (j + 1)] = pm[j].astype(pmem_ref.dtype)
        yc = _mm(pmem_ref[...], vm_ref[...])
        yc_ref[...] = yc

        def gated(y, g, gate):
            return y * lax.rsqrt(_row_mean(y * y) + EPS) * g * (gate * _sigmoid(gate))

        ogv = og_ref[...]
        za = gated(ya_ref[...], ogv[:, :512], proj_ref[:, C_LRUG:C_LRUG + LRU_W])
        zb = gated(yb_ref[...], ogv[:, 512:768], proj_ref[:, C_SWAG:C_SWAG + SWA_W])
        zc = gated(yc, ogv[:, 768:], proj_ref[:, C_XG:C_XG + XATT_W])
        ycat_ref[:, 0:512] = za.astype(ycat_ref.dtype)
        ycat_ref[:, 512:768] = zb.astype(ycat_ref.dtype)
        ycat_ref[:, 768:1024] = zc.astype(ycat_ref.dtype)
        out = xv + _mm(ycat_ref[...], wout_ref[...])
        err = out - t_ref[...]
        dout_ref[...] = (err * (1.0 / D_MODEL)).astype(dout_ref.dtype)
        lacc_ref[...] = lacc_ref[...] + (0.5 / D_MODEL) * jnp.sum(err * err)

        @pl.when(i == nt - 1)
        def _():
            loss_ref[...] = lacc_ref[...]

    def rows(ncol):
        return pl.BlockSpec((tm, ncol), lambda i: (i, 0))

    in_specs = [rows(D_MODEL), rows(D_MODEL), rows(128), rows(128), rows(128),
                _const_spec((1, D_MODEL)), _const_spec((D_IN, D_MODEL), True), _const_spec((CONV_K, LRU_W)),
                _const_spec((1, LRU_W)), _const_spec((2, 256, 512), True), _const_spec((1, LRU_W)),
                _const_spec((1, LRU_W)), _const_spec((1, LRU_W)), _const_spec((1, 128)), _const_spec((1, 128)),
                _const_spec((1, XATT_W)), pl.BlockSpec(memory_space=pltpu.SMEM),
                _const_spec((4 * MEM_LEN, XATT_W), True), _const_spec((4 * MEM_LEN, XATT_W), True),
                _const_spec((1, D_MODEL)), _const_spec((D_MODEL, D_MODEL), True)]
    out_shape = (jax.ShapeDtypeStruct((seq, D_IN), F32), jax.ShapeDtypeStruct((seq, LRU_W), F32),
                 jax.ShapeDtypeStruct((seq, SWA_W), F32), jax.ShapeDtypeStruct((seq, XATT_W), F32),
                 jax.ShapeDtypeStruct((seq, D_MODEL), _MXU), jax.ShapeDtypeStruct((seq, D_MODEL), _MXU),
                 jax.ShapeDtypeStruct((seq, D_MODEL), _MXU), jax.ShapeDtypeStruct((seq, 4 * 2 * BLOCK), _MXU),
                 jax.ShapeDtypeStruct((seq, 4 * MEM_LEN), _MXU), jax.ShapeDtypeStruct((seq, 128), F32),
                 jax.ShapeDtypeStruct((seq, 4 * LRU_W), _MXU), jax.ShapeDtypeStruct((seq, LRU_W), F32),
                 jax.ShapeDtypeStruct((8, 128), F32))
    out_specs = (rows(D_IN), rows(LRU_W), rows(SWA_W), rows(XATT_W), rows(D_MODEL), rows(D_MODEL), rows(D_MODEL),
                 rows(4 * 2 * BLOCK), rows(4 * MEM_LEN), rows(128), rows(4 * LRU_W), rows(LRU_W),
                 _const_spec((8, 128)))
    scratch = [pltpu.VMEM((tm + 8, LRU_W), F32), pltpu.VMEM((tm, LRU_W), F32),
               pltpu.VMEM((8, LRU_W), F32), pltpu.VMEM((BLOCK, KV_W), F32), pltpu.VMEM((BLOCK, KV_W), F32),
               pltpu.VMEM((8, 128), F32)]
    return pl.pallas_call(
        body, name="layer_fwd", grid=(nt,), out_shape=out_shape, in_specs=in_specs, out_specs=out_specs,
        scratch_shapes=scratch,
        compiler_params=pltpu.CompilerParams(dimension_semantics=("arbitrary",), vmem_limit_bytes=VMEM_LIMIT),
    )(x, tgt, rc, rs1, rs2, ng, win_t, cw, cb, wg, brg, big, lam, qg, kg, xqg, sinks, km, vm, og, wout)


def wgrad_out(ycat, dout):
    seq = ycat.shape[0]

    def body(y_ref, d_ref, o_ref):
        o_ref[...] = _mm_tn(y_ref[...], d_ref[...])

    return pl.pallas_call(
        body, name="wgrad_out", grid=(D_MODEL // 256,), out_shape=jax.ShapeDtypeStruct((D_MODEL, D_MODEL), F32),
        in_specs=[pl.BlockSpec((seq, 256), lambda j: (0, j)), _const_spec((seq, D_MODEL), True)],
        out_specs=pl.BlockSpec((256, D_MODEL), lambda j: (j, 0)),
        compiler_params=pltpu.CompilerParams(dimension_semantics=("arbitrary",), vmem_limit_bytes=VMEM_LIMIT),
    )(ycat, dout)


def wgrad_in(dproj, xn):
    seq = xn.shape[0]
    nblk = D_IN // 256

    def body(d_ref, x_ref, o_ref):
        o_ref[...] = _mm_tn(d_ref[...], x_ref[...])

    return pl.pallas_call(
        body, name="wgrad_in", grid=(nblk,), out_shape=jax.ShapeDtypeStruct((D_IN, D_MODEL), F32),
        in_specs=[pl.BlockSpec((seq, 256), lambda j: (0, j)), _const_spec((seq, D_MODEL), True)],
        out_specs=pl.BlockSpec((256, D_MODEL), lambda j: (j, 0)),
        compiler_params=pltpu.CompilerParams(dimension_semantics=("arbitrary",), vmem_limit_bytes=VMEM_LIMIT),
    )(dproj, xn)


def layer_bwd(x, dout, proj, ya, yb, yc, pswa, pmem, psink, gates, a_all, rc, rs1, rs2, ng, win_t, cw, cb, wg, brg, big, lam, qg, kg, xqg, sinks, km,
              vm, og, wout):
    seq = x.shape[0]
    tm = min(ROW_TILE, seq)
    nt = seq // tm
    nb = tm // BLOCK

    def body(x_ref, dout_ref, proj_ref, ya_ref, yb_ref, yc_ref, pswa_ref, pmem_ref, psink_ref, gates_ref, a_ref,
             c_ref, s1_ref, s2_ref,
             yah_ref, kvh_ref, ch_ref, s1h_ref, s2h_ref,
             ng_ref, win_ref, cw_ref, cb_ref, wg_ref, brg_ref, big_ref, lam_ref, qg_ref, kg_ref, xqg_ref, sink_ref,
             km_ref, vm_ref, og_ref, wout_ref,
             gx_ref, dproj_ref, gwg_ref, dkm_ref, dvm_ref, gng_ref, gog_ref, gcb_ref, gbrg_ref, gbig_ref, glam_ref,
             gcw_ref, gqn_ref, gkn_ref, gxqn_ref, gsink_ref,
             hext_ref, aext_ref, an_scr, dh_scr, g_scr, dxc_ext, gcar_ref, dkcar_ref, dvcar_ref):
        i = pl.program_id(0)
        tile = nt - 1 - i
        first_tile = tile == 0

        @pl.when(i == 0)
        def _():
            for r in (gwg_ref, dkm_ref, dvm_ref, gng_ref, gog_ref, gcb_ref, gbrg_ref, gbig_ref, glam_ref, gcw_ref,
                      gqn_ref, gkn_ref, gxqn_ref, gsink_ref, gcar_ref, dkcar_ref, dvcar_ref):
                r[...] = jnp.zeros_like(r)
            dxc_ext[tm:tm + 8, :] = jnp.zeros((8, LRU_W), F32)
            aext_ref[tm:tm + 8, :] = jnp.zeros((8, LRU_W), F32)

        xv = x_ref[...]
        dov = dout_ref[...]
        dz = _mm_nt(dov, wout_ref[...])
        ogv = og_ref[...]

        def group_bwd(y, gate, g, dzg):
            r = lax.rsqrt(_row_mean(y * y) + EPS)
            n = y * r
            sg = _sigmoid(gate)
            dgate = dzg * (n * g) * (sg * (1.0 + gate * (1.0 - sg)))
            dng = dzg * (gate * sg)
            dn = dng * g
            return r * (dn - n * _row_mean(dn * n)), dgate, _col_sum(dng * n)

        dya, dga, goa = group_bwd(ya_ref[...], proj_ref[:, C_LRUG:C_LRUG + LRU_W], ogv[:, :512], dz[:, :512])
        dyb, dgb, gob = group_bwd(yb_ref[...], proj_ref[:, C_SWAG:C_SWAG + SWA_W], ogv[:, 512:768], dz[:, 512:768])
        dyc, dgc, goc = group_bwd(yc_ref[...], proj_ref[:, C_XG:C_XG + XATT_W], ogv[:, 768:], dz[:, 768:])
        gog_ref[...] += jnp.concatenate([goa, gob, goc], axis=1)
        dproj_ref[:, C_LRUG:C_LRUG + LRU_W] = dga.astype(dproj_ref.dtype)
        dproj_ref[:, C_SWAG:C_SWAG + SWA_W] = dgb.astype(dproj_ref.dtype)
        dproj_ref[:, C_XG:C_XG + XATT_W] = dgc.astype(dproj_ref.dtype)

        gm256 = _group_matrix(XATT_W)
        xq = proj_ref[:, C_XQ:C_XQ + XATT_W]
        rq = lax.rsqrt(_seg_mean(xq * xq, gm256) + EPS)
        qn = xq * rq
        qx = qn * xqg_ref[...]
        qxb = qx.astype(_MXU)
        dycb = dyc.astype(_MXU)
        dp_all = _mm_nt(dycb, vm_ref[...])
        dsm = []
        for j in range(4):
            pj = pmem_ref[:, MEM_LEN * j:MEM_LEN * (j + 1)].astype(F32)
            dp = dp_all[:, MEM_LEN * j:MEM_LEN * (j + 1)]
            dsm.append((pj * (dp - jnp.sum(pj * dp, axis=-1, keepdims=True))).astype(_MXU))
        ds_all = jnp.concatenate(dsm, axis=1)
        dvm_ref[...] += _mm_tn(pmem_ref[...], dycb)
        dkm_ref[...] += _mm_tn(ds_all, qxb)
        dqx = _mm(ds_all, km_ref[...])
        gxqn_ref[...] += _col_sum(dqx * qn)
        dqn = dqx * xqg_ref[...]
        dproj_ref[:, C_XQ:C_XQ + XATT_W] = (rq * (dqn - qn * _seg_mean(dqn * qn, gm256))).astype(dproj_ref.dtype)

        gm128 = _group_matrix(KV_W)
        cv, s1v, s2v = c_ref[...], s1_ref[...], s2_ref[...]

        def head_norm(t):
            r = lax.rsqrt(_seg_mean(t * t, gm128) + EPS)
            return t * r, r

        qn_, qr_ = zip(head_norm(proj_ref[:, C_SQ:C_SQ + 128]), head_norm(proj_ref[:, C_SQ + 128:C_SQ + 256]))
        qrope = [_rope(qn_[h] * qg_ref[...], cv, s1v, s2v).astype(_MXU) for h in range(2)]
        kn, krr = head_norm(proj_ref[:, C_SK:C_SK + KV_W])
        kr = _rope(kn * kg_ref[...], cv, s1v, s2v)
        khn, _ = head_norm(kvh_ref[:, 0:KV_W])
        khr = _rope(khn * kg_ref[...], ch_ref[...], s1h_ref[...], s2h_ref[...])
        ka = _place_kv(jnp.concatenate([khr, kr], axis=0), 0.125)
        va = _place_kv(jnp.concatenate([kvh_ref[:, KV_W:2 * KV_W], proj_ref[:, C_SV:C_SV + KV_W]], axis=0), 1.0)
        lane128 = lax.broadcasted_iota(jnp.int32, (1, 128), 1)
        gsink = jnp.zeros((1, 128), F32)
        dk_band, dv_band, dq_blk = [], [], []
        for b in range(nb):
            band = slice(BLOCK * b, BLOCK * b + 2 * BLOCK)
            blk = slice(BLOCK * b, BLOCK * (b + 1))
            dka, dva, dsb = [], [], []
            deltas = jnp.zeros((BLOCK, 128), F32)
            for j in range(4):
                qh = qrope[j // 2][blk]
                doh = dyb[blk, KV_W * (j // 2):KV_W * (j // 2 + 1)].astype(_MXU)
                pb = pswa_ref[blk, 2 * BLOCK * j:2 * BLOCK * (j + 1)]
                p = pb.astype(F32)
                dp = _mm_nt(doh, va[j][band])
                delta = jnp.sum(p * dp, axis=-1, keepdims=True)
                ds = (p * (dp - delta)).astype(_MXU)
                deltas = jnp.where(lane128 == j, delta, deltas)
                dva.append(_mm_tn(pb, doh))
                dka.append(_mm_tn(ds, qh))
                dsb.append(ds)
            gsink = gsink - _col_sum(psink_ref[blk, :] * deltas)
            dk_band.append(_unplace_kv(dka) * 0.125)
            dv_band.append(_unplace_kv(dva))
            dq_blk.append([_mm(jnp.concatenate(dsb[2 * h:2 * h + 2], axis=1),
                               jnp.concatenate([ka[2 * h][band], ka[2 * h + 1][band]], axis=0)) for h in range(2)])
        gsink_ref[...] += gsink
        dk_rows = [dk_band[b][BLOCK:] + (dk_band[b + 1][:BLOCK] if b + 1 < nb else dkcar_ref[...]) for b in range(nb)]
        dv_rows = [dv_band[b][BLOCK:] + (dv_band[b + 1][:BLOCK] if b + 1 < nb else dvcar_ref[...]) for b in range(nb)]
        dkcar_ref[...] = dk_band[0][:BLOCK]
        dvcar_ref[...] = dv_band[0][:BLOCK]
        dkg = _rope_bwd(jnp.concatenate(dk_rows, axis=0), cv, s1v, s2v)
        gkn = _col_sum(dkg * kn)
        dkn = dkg * kg_ref[...]
        dproj_ref[:, C_SK:C_SK + KV_W] = (krr * (dkn - kn * _seg_mean(dkn * kn, gm128))).astype(dproj_ref.dtype)
        dproj_ref[:, C_SV:C_SV + KV_W] = jnp.concatenate(dv_rows, axis=0).astype(dproj_ref.dtype)
        gqn = jnp.zeros((1, 128), F32)
        for h in range(2):
            dqg = _rope_bwd(jnp.concatenate([dq_blk[b][h] for b in range(nb)], axis=0), cv, s1v, s2v)
            gqn = gqn + _col_sum(dqg * qn_[h])
            dqn_ = dqg * qg_ref[...]
            dproj_ref[:, C_SQ + 128 * h:C_SQ + 128 * (h + 1)] = (
                qr_[h] * (dqn_ - qn_[h] * _seg_mean(dqn_ * qn_[h], gm128))).astype(dproj_ref.dtype)
        gqn_ref[...] += gqn
        gkn_ref[...] += gkn

        u = proj_ref[:, C_LRUX:C_LRUX + LRU_W]
        xc, rg, ig, sq = (gates_ref[:, LRU_W * k:LRU_W * (k + 1)].astype(F32) for k in range(4))
        a = a_ref[...]
        sp = _softplus(-lam_ref[...])
        hext_ref[0:8, :] = jnp.where(first_tile, 0.0, yah_ref[...])
        hext_ref[8:8 + tm, :] = ya_ref[...]
        hprev = hext_ref[pl.ds(7, tm), :]
        aext_ref[0:tm, :] = a
        an_scr[...] = aext_ref[pl.ds(1, tm), :]
        dh_scr[...] = dya
        dh_scr[tm - 1:tm, :] = dh_scr[tm - 1:tm, :] + gcar_ref[0:1, :]
        row8 = lax.broadcasted_iota(jnp.int32, (8, LRU_W), 0)

        def scan_step(gi, carry):
            r0 = pl.multiple_of((tm // 8 - 1 - gi) * 8, 8)
            av = an_scr[pl.ds(r0, 8), :]
            bv = dh_scr[pl.ds(r0, 8), :]
            for d in (1, 2, 4):
                a_sh = jnp.where(row8 < 8 - d, pltpu.roll(av, 8 - d, 0), 1.0)
                b_sh = jnp.where(row8 < 8 - d, pltpu.roll(bv, 8 - d, 0), 0.0)
                bv = bv + av * b_sh
                av = av * a_sh
            gv = bv + av * carry
            g_scr[pl.ds(r0, 8), :] = gv
            return gv[0:1, :]

        g0 = lax.fori_loop(0, tm // 8, scan_step, jnp.zeros((1, LRU_W), F32), unroll=True)
        gcar_ref[0:1, :] = a[0:1, :] * g0
        gv = g_scr[...]
        da = gv * hprev
        dig = gv * sq * xc
        dxc = gv * sq * ig
        dla = da * a - gv * (ig * xc) * ((a * a) / sq)
        drg = dla * ((-LRU_C) * sp)
        glam_ref[...] += _col_sum(dla * rg)
        dpr = drg * rg * (1.0 - rg)
        dpi = dig * ig * (1.0 - ig)
        gbrg_ref[...] += _col_sum(dpr)
        gbig_ref[...] += _col_sum(dpi)
        dpre0 = jnp.concatenate([dpr[:, :256], dpi[:, :256]], axis=1).astype(_MXU)
        dpre1 = jnp.concatenate([dpr[:, 256:], dpi[:, 256:]], axis=1).astype(_MXU)
        gwg_ref[0] += _mm_tn(xc[:, :256], dpre0)
        gwg_ref[1] += _mm_tn(xc[:, 256:], dpre1)
        dxc = dxc + jnp.concatenate([_mm_nt(dpre0, wg_ref[0]), _mm_nt(dpre1, wg_ref[1])], axis=1)
        gcb_ref[...] += _col_sum(dxc)
        dxc_ext[0:tm, :] = dxc
        du = jnp.zeros((tm, LRU_W), F32)
        for k in range(CONV_K):
            later = dxc_ext[pl.ds(3 - k, tm), :]
            gcw_ref[k:k + 1, :] += _col_sum(later * u)
            du = du + cw_ref[k:k + 1, :] * later
        dxc_ext[tm:tm + 8, :] = dxc[0:8, :]
        dproj_ref[:, C_LRUX:C_LRUX + LRU_W] = du.astype(dproj_ref.dtype)

        dxn = _mm(dproj_ref[...], win_ref[...])
        rx = lax.rsqrt(_row_mean(xv * xv) + EPS)
        xh = xv * rx
        gng_ref[...] += _col_sum(dxn * xh)
        dxh = dxn * ng_ref[...]
        gx_ref[...] = dov.astype(F32) + rx * (dxh - xh * _row_mean(dxh * xh))

        @pl.when(i == nt - 1)
        def _():
            glam_ref[...] = glam_ref[...] * (LRU_C * _sigmoid(-lam_ref[...]))

    def rows(ncol, arr_cols_block=0):
        return pl.BlockSpec((tm, ncol), lambda i: (nt - 1 - i, arr_cols_block))

    def halo(nrow, ncol, colblk=0):
        per = tm // nrow
        return pl.BlockSpec((nrow, ncol), lambda i: (jnp.maximum((nt - 1 - i) * per - 1, 0), colblk))

    in_specs = [rows(D_MODEL), rows(D_MODEL), rows(D_IN), rows(LRU_W), rows(SWA_W), rows(XATT_W),
                rows(4 * 2 * BLOCK), rows(4 * MEM_LEN), rows(128), rows(4 * LRU_W), rows(LRU_W),
                rows(128), rows(128), rows(128),
                halo(8, LRU_W), halo(BLOCK, 2 * KV_W, C_SK // (2 * KV_W)),
                halo(BLOCK, 128), halo(BLOCK, 128), halo(BLOCK, 128),
                _const_spec((1, D_MODEL)), _const_spec((D_IN, D_MODEL), True), _const_spec((CONV_K, LRU_W)),
                _const_spec((1, LRU_W)), _const_spec((2, 256, 512), True), _const_spec((1, LRU_W)),
                _const_spec((1, LRU_W)), _const_spec((1, LRU_W)), _const_spec((1, 128)), _const_spec((1, 128)),
                _const_spec((1, XATT_W)), pl.BlockSpec(memory_space=pltpu.SMEM),
                _const_spec((4 * MEM_LEN, XATT_W), True), _const_spec((4 * MEM_LEN, XATT_W), True),
                _const_spec((1, D_MODEL)), _const_spec((D_MODEL, D_MODEL), True)]
    small = [(2, 256, 512), (4 * MEM_LEN, XATT_W), (4 * MEM_LEN, XATT_W), (1, D_MODEL), (1, D_MODEL), (1, LRU_W), (1, LRU_W),
             (1, LRU_W), (1, LRU_W), (CONV_K, LRU_W), (1, 128), (1, 128), (1, XATT_W), (1, 128)]
    out_shape = (jax.ShapeDtypeStruct((seq, D_MODEL), F32), jax.ShapeDtypeStruct((seq, D_IN), _MXU)) + tuple(
        jax.ShapeDtypeStruct(s, F32) for s in small)
    out_specs = (rows(D_MODEL), rows(D_IN)) + tuple(_const_spec(s) for s in small)
    scratch = [pltpu.VMEM((tm + 8, LRU_W), F32), pltpu.VMEM((tm + 8, LRU_W), F32),
               pltpu.VMEM((tm, LRU_W), F32), pltpu.VMEM((tm, LRU_W), F32), pltpu.VMEM((tm, LRU_W), F32),
               pltpu.VMEM((tm + 8, LRU_W), F32),
               pltpu.VMEM((8, LRU_W), F32), pltpu.VMEM((BLOCK, KV_W), F32), pltpu.VMEM((BLOCK, KV_W), F32)]
    return pl.pallas_call(
        body, name="layer_bwd", grid=(nt,), out_shape=out_shape, in_specs=in_specs, out_specs=out_specs,
        scratch_shapes=scratch,
        compiler_params=pltpu.CompilerParams(dimension_semantics=("arbitrary",), vmem_limit_bytes=VMEM_LIMIT),
    )(x, dout, proj, ya, yb, yc, pswa, pmem, psink, gates, a_all, rc, rs1, rs2, ya, proj, rc, rs1, rs2,
      ng, win_t, cw, cb, wg, brg, big, lam, qg, kg, xqg, sinks, km, vm, og, wout)


def reduce_grads(g_in, g_out, g_kv, g_small):
    bigs = (g_in, g_out, g_kv)
    nbig = len(bigs)

    def body(b0, b1, b2, sm, o0, o1, o2, osm, r1_0, r1_1, r1_2, r1s, w0, w1, w2, r2_0, r2_1, r2_2, r2s, t0, t1, t2, ps,
             own0, own1, own2, send, recv, lsem):
        big, outs, own = (b0, b1, b2), (o0, o1, o2), (own0, own1, own2)
        r1, wire, r2, wire2 = (r1_0, r1_1, r1_2), (w0, w1, w2), (r2_0, r2_1, r2_2), (t0, t1, t2)
        x, y, c = lax.axis_index("x"), lax.axis_index("y"), lax.axis_index("c")
        sibling = (x, y, 1 - c)
        near, far, diag = _partners(x, y, c)
        me, near_id, far_id, diag_id = _chip_of(x, y), _chip_of(*near), _chip_of(*far), _chip_of(*diag)

        def copy(k, src, dst, to):
            return pltpu.make_async_remote_copy(src_ref=src, dst_ref=dst, send_sem=send.at[k], recv_sem=recv.at[k],
                                                device_id=to, device_id_type=MESH)

        step1 = [copy(a, big[a].at[:, 1 - c], r1[a], sibling) for a in range(nbig)]
        step1.append(copy(nbig, sm.at[1 - c], r1s, sibling))
        loads = [pltpu.make_async_copy(big[a].at[:, c], own[a], lsem.at[a]) for a in range(nbig)]
        for cp in step1 + loads:
            cp.start()
        step2 = []
        for a in range(nbig):
            loads[a].wait()
            copy(a, r1[a], r1[a], sibling).wait_recv()
            for k in range(N_CHIPS):
                r1[a][k] = own[a][k] + r1[a][k]
                wire[a][k] = r1[a][k].astype(wire[a].dtype)
            step2.append(copy(4 + 2 * a, wire[a].at[near_id], r2[a].at[0], (*near, c)))
            step2.append(copy(5 + 2 * a, wire[a].at[diag_id], r2[a].at[1], (*near, c)))
            step2[-2].start()
            step2[-1].start()
        copy(nbig, r1s, r1s, sibling).wait_recv()
        r1s[...] = sm[c] + r1s[...]
        step2.append(copy(10, r1s, r2s.at[0], (*near, c)))
        step2[-1].start()
        for a in range(nbig):
            copy(4 + 2 * a, r2[a].at[0], r2[a].at[0], sibling).wait_recv()
            copy(5 + 2 * a, r2[a].at[1], r2[a].at[1], sibling).wait_recv()
            r1[a][me] = r1[a][me] + r2[a][0].astype(F32)
            wire2[a][...] = (r1[a][far_id] + r2[a][1].astype(F32)).astype(wire2[a].dtype)
            step2.append(copy(11 + a, wire2[a], r2[a].at[2], (*far, c)))
            step2[-1].start()
        copy(10, r2s.at[0], r2s.at[0], sibling).wait_recv()
        ps[...] = r1s[...] + r2s[0]
        step2.append(copy(14, ps, r2s.at[1], (*far, c)))
        step2[-1].start()
        step3 = []
        for a in range(nbig):
            copy(11 + a, r2[a].at[2], r2[a].at[2], sibling).wait_recv()
            outs[a][c] = r1[a][me] + r2[a][2].astype(F32)
            step3.append(copy(15 + a, outs[a].at[c], outs[a].at[c], sibling))
            step3[-1].start()
        copy(14, r2s.at[1], r2s.at[1], sibling).wait_recv()
        osm[c] = ps[...] + r2s[1]
        step3.append(copy(15 + nbig, osm.at[c], osm.at[c], sibling))
        step3[-1].start()
        for a in range(nbig):
            other = outs[a].at[1 - c]
            copy(15 + a, other, other, sibling).wait_recv()
        other = osm.at[1 - c]
        copy(15 + nbig, other, other, sibling).wait_recv()
        for cp in step1 + step2 + step3:
            cp.wait_send()

    vm = pl.BlockSpec(memory_space=pltpu.VMEM)
    half = [b.shape[2:] for b in bigs]
    out_shape = tuple(jax.ShapeDtypeStruct((2,) + h, F32) for h in half) + (jax.ShapeDtypeStruct(g_small.shape, F32),)
    sm_half = g_small.shape[1:]
    scratch = ([pltpu.VMEM((N_CHIPS,) + h, F32) for h in half] + [pltpu.VMEM(sm_half, F32)]
               + [pltpu.VMEM((N_CHIPS,) + h, _WIRE) for h in half]
               + [pltpu.VMEM((3,) + h, _WIRE) for h in half] + [pltpu.VMEM((2,) + sm_half, F32)]
               + [pltpu.VMEM(h, _WIRE) for h in half] + [pltpu.VMEM(sm_half, F32)]
               + [pltpu.VMEM((N_CHIPS,) + h, F32) for h in half]
               + [pltpu.SemaphoreType.DMA((20,)), pltpu.SemaphoreType.DMA((20,)), pltpu.SemaphoreType.DMA((nbig,))])
    hbm = pl.BlockSpec(memory_space=pl.ANY)
    return pl.pallas_call(
        body, name="reduce_grads", out_shape=out_shape, in_specs=[hbm] * nbig + [vm], out_specs=(vm,) * 4,
        scratch_shapes=scratch, compiler_params=pltpu.CompilerParams(vmem_limit_bytes=VMEM_LIMIT),
    )(g_in, g_out, g_kv, g_small)


def adamw(w, g, m, v, name):
    rows_, cols = w.shape
    tr = max(t for t in range(8, rows_ + 1, 8) if rows_ % t == 0 and t * cols * 4 <= ADAM_BLOCK_BYTES)

    def body(w_ref, g_ref, m_ref, v_ref, d_ref, nm_ref, nv_ref):
        gv = g_ref[...]
        nm = ADAM_B1 * m_ref[...] + (1.0 - ADAM_B1) * gv
        nv = ADAM_B2 * v_ref[...] + (1.0 - ADAM_B2) * (gv * gv)
        m_hat = nm / (1.0 - ADAM_B1 ** ADAM_STEP)
        v_hat = nv / (1.0 - ADAM_B2 ** ADAM_STEP)
        d_ref[...] = (-ADAM_LR) * (m_hat / (jnp.sqrt(v_hat) + ADAM_EPS) + ADAM_WD * w_ref[...])
        nm_ref[...] = nm
        nv_ref[...] = nv

    spec = pl.BlockSpec((tr, cols), lambda i: (i, 0))
    shp = jax.ShapeDtypeStruct(w.shape, F32)
    return pl.pallas_call(
        body, name=name, grid=(rows_ // tr,), out_shape=(shp, shp, shp), in_specs=[spec] * 4, out_specs=(spec,) * 3,
        compiler_params=pltpu.CompilerParams(dimension_semantics=("arbitrary",)),
    )(w, g, m, v)


SMALL = (("norm_g", 1024), ("mem_norm_g", 1024), ("conv_w", 512), ("conv_b", 512), ("w_rg", 32768), ("b_rg", 512),
         ("w_ig", 32768), ("b_ig", 512), ("lru_lambda", 512), ("q_norm_g", 128), ("k_norm_g", 128), ("sinks", 128),
         ("xq_norm_g", 128), ("xk_norm_g", 128), ("out_norm_g", 1024))
SMALL_ROWS = 576


def _pack(parts, rows_):
    flat = jnp.concatenate([p.reshape(-1) for p in parts])
    return jnp.pad(flat, (0, rows_ * 128 - flat.shape[0])).reshape(rows_, 128)


def _pad_to(v, n):
    v = v.reshape(-1)
    return jnp.pad(v, (0, n - v.shape[0]))


def _block_diag_gates(w_rg, w_ig):
    def bd(w4):
        z = jnp.zeros((4, HEAD, 4, HEAD), w4.dtype)
        idx = jnp.arange(4)
        return z.at[idx, :, idx, :].set(w4).reshape(256, 256)

    return jnp.stack([jnp.concatenate([bd(w_rg[4 * h:4 * h + 4]), bd(w_ig[4 * h:4 * h + 4])], axis=1) for h in (0, 1)])


def _diag_blocks(g):
    out = []
    for part in (0, 1):
        blocks = []
        for h in (0, 1):
            sub = g[h, :, 256 * part:256 * (part + 1)].reshape(4, HEAD, 4, HEAD)
            blocks.append(jnp.stack([sub[n, :, n, :] for n in range(4)]))
        out.append(jnp.concatenate(blocks, axis=0))
    return out


def _rope_tables(seq):
    pos = np.arange(seq, dtype=np.float32)
    inv_freq = (np.float32(ROPE_THETA) ** (-(np.arange(0, ROPE_DIM, 2, dtype=np.float32) / np.float32(ROPE_DIM)))
                ).astype(np.float32)
    ang = (pos[:, None] * inv_freq[None, :]).astype(np.float32)
    cos, sin = np.cos(ang).astype(np.float32), np.sin(ang).astype(np.float32)
    z = lambda n: np.zeros((seq, n), np.float32)
    c64 = np.concatenate([cos, cos, np.ones((seq, HEAD - ROPE_DIM), np.float32)], axis=1)
    s1_64 = np.concatenate([-sin, z(HEAD - 8)], axis=1)
    s2_64 = np.concatenate([z(8), sin, z(HEAD - ROPE_DIM)], axis=1)
    return tuple(jnp.asarray(np.concatenate([t, t], axis=1)) for t in (c64, s1_64, s2_64))


def kernel(x, mem, norm_g, mem_norm_g, w_in, conv_w, conv_b, w_rg, b_rg, w_ig, b_ig, lru_lambda, q_norm_g, k_norm_g, sinks, w_mem_kv, xq_norm_g, xk_norm_g, out_norm_g, w_out, loss_target, m_norm_g, m_mem_norm_g, m_w_in, m_conv_w, m_conv_b, m_w_rg, m_b_rg, m_w_ig, m_b_ig, m_lru_lambda, m_q_norm_g, m_k_norm_g, m_sinks, m_w_mem_kv, m_xq_norm_g, m_xk_norm_g, m_out_norm_g, m_w_out, v_norm_g, v_mem_norm_g, v_w_in, v_conv_w, v_conv_b, v_w_rg, v_b_rg, v_w_ig, v_b_ig, v_lru_lambda, v_q_norm_g, v_k_norm_g, v_sinks, v_w_mem_kv, v_xq_norm_g, v_xk_norm_g, v_out_norm_g, v_w_out):
    seq = x.shape[1]
    chip = 2 * lax.axis_index("x") + lax.axis_index("y")
    xs, tgt, mems = x[0], loss_target[0], mem[0]

    win_t_sh = w_in[0].T.astype(_MXU)
    cw_sh = jnp.pad(conv_w[0], ((0, 4), (0, 0)))
    win_t, wout, wkv, cw_all = gather_weights(win_t_sh, w_out[0].astype(_MXU), w_mem_kv[0].astype(_MXU), cw_sh)
    cw = cw_all.reshape(N_CHIPS, 8, 128)[:, :CONV_K].transpose(1, 0, 2).reshape(CONV_K, LRU_W)

    rc, rs1, rs2 = _rope_tables(seq)
    wg = _block_diag_gates(w_rg[0], w_ig[0]).astype(_MXU)
    qg = jnp.tile(q_norm_g, (1, 2))
    kg = jnp.tile(k_norm_g, (1, 2))
    xqg = jnp.tile(xq_norm_g, (1, 4))
    xkg = jnp.tile(xk_norm_g, (1, 4))

    km, vm = mem_fwd(mems, mem_norm_g, wkv, xkg)
    proj, ya, yb, yc, ycat, xn, dout, pswa, pmem, psink, gates, a_all, loss8 = layer_fwd(
        xs, tgt, rc, rs1, rs2, norm_g, win_t, cw, conv_b, wg, b_rg, b_ig, lru_lambda, qg, kg, xqg, sinks, km, vm,
        out_norm_g, wout)
    g_wout = wgrad_out(ycat, dout)
    (gx, dproj, g_wg, dkm, dvm, g_ng, g_og, g_cb, g_brg, g_big, g_lam, g_cw, g_qn, g_kn, g_xqn, g_sink) = layer_bwd(
        xs, dout, proj, ya, yb, yc, pswa, pmem, psink, gates, a_all, rc, rs1, rs2, norm_g, win_t, cw, conv_b, wg, b_rg, b_ig,
        lru_lambda, qg, kg, xqg, sinks, km, vm, out_norm_g, wout)
    g_win_t = wgrad_in(dproj, xn)
    g_wkv, g_mng, g_xkn = mem_bwd(mems, mem_norm_g, wkv, xkg, dkm, dvm)

    g_wrg, g_wig = _diag_blocks(g_wg)
    fold = lambda v, n: v.reshape(n, HEAD).sum(axis=0)
    small_g = _pack([g_ng, g_mng, g_cw, g_cb, g_wrg, g_brg, g_wig, g_big, g_lam, _pad_to(fold(g_qn, 2), 128),
                     _pad_to(fold(g_kn, 2), 128), g_sink, _pad_to(fold(g_xqn, 4), 128), _pad_to(fold(g_xkn, 4), 128),
                     g_og, loss8[0:1]], SMALL_ROWS)
    r_in, r_out, r_kv, r_small = reduce_grads(
        g_win_t.reshape(N_CHIPS, 2, D_IN // 8, D_MODEL), g_wout.reshape(N_CHIPS, 2, D_MODEL // 8, D_MODEL),
        g_wkv.reshape(N_CHIPS, 2, D_MODEL // 8, 2 * XATT_W), small_g.reshape(2, SMALL_ROWS // 2, 128))

    flat = r_small.reshape(-1)
    sizes = (1024, 1024, 2048, 512, 32768, 512, 32768, 512, 512, 128, 128, 128, 128, 128, 1024)
    offs = [0]
    for s in sizes:
        offs.append(offs[-1] + s)
    loss = flat[offs[-1]]
    piece = {name: flat[offs[k]:offs[k + 1]] for k, (name, _) in enumerate(SMALL)}
    g_cw_mine = lax.dynamic_slice(piece["conv_w"].reshape(CONV_K, LRU_W), (0, chip * 128), (CONV_K, 128))
    grads = {
        "norm_g": piece["norm_g"].reshape(1, 1024), "mem_norm_g": piece["mem_norm_g"].reshape(1, 1024),
        "w_in": r_in.reshape(D_IN // 4, D_MODEL).T[None], "conv_w": g_cw_mine[None],
        "conv_b": piece["conv_b"].reshape(1, 512), "w_rg": piece["w_rg"].reshape(1, 8, HEAD, HEAD),
        "b_rg": piece["b_rg"].reshape(1, 512), "w_ig": piece["w_ig"].reshape(1, 8, HEAD, HEAD),
        "b_ig": piece["b_ig"].reshape(1, 512), "lru_lambda": piece["lru_lambda"].reshape(1, 512),
        "q_norm_g": piece["q_norm_g"][:HEAD].reshape(1, HEAD), "k_norm_g": piece["k_norm_g"][:HEAD].reshape(1, HEAD),
        "sinks": piece["sinks"][:4].reshape(1, 4), "w_mem_kv": r_kv.reshape(D_MODEL // 4, 2 * XATT_W)[None],
        "xq_norm_g": piece["xq_norm_g"][:HEAD].reshape(1, HEAD), "xk_norm_g": piece["xk_norm_g"][:HEAD].reshape(1, HEAD),
        "out_norm_g": piece["out_norm_g"].reshape(1, 1024), "w_out": r_out.reshape(D_MODEL // 4, D_MODEL)[None],
    }
    weights = dict(norm_g=norm_g, mem_norm_g=mem_norm_g, w_in=w_in, conv_w=conv_w, conv_b=conv_b, w_rg=w_rg, b_rg=b_rg,
                   w_ig=w_ig, b_ig=b_ig, lru_lambda=lru_lambda, q_norm_g=q_norm_g, k_norm_g=k_norm_g, sinks=sinks,
                   w_mem_kv=w_mem_kv, xq_norm_g=xq_norm_g, xk_norm_g=xk_norm_g, out_norm_g=out_norm_g, w_out=w_out)
    ms = dict(norm_g=m_norm_g, mem_norm_g=m_mem_norm_g, w_in=m_w_in, conv_w=m_conv_w, conv_b=m_conv_b, w_rg=m_w_rg,
              b_rg=m_b_rg, w_ig=m_w_ig, b_ig=m_b_ig, lru_lambda=m_lru_lambda, q_norm_g=m_q_norm_g, k_norm_g=m_k_norm_g,
              sinks=m_sinks, w_mem_kv=m_w_mem_kv, xq_norm_g=m_xq_norm_g, xk_norm_g=m_xk_norm_g,
              out_norm_g=m_out_norm_g, w_out=m_w_out)
    vs = dict(norm_g=v_norm_g, mem_norm_g=v_mem_norm_g, w_in=v_w_in, conv_w=v_conv_w, conv_b=v_conv_b, w_rg=v_w_rg,
              b_rg=v_b_rg, w_ig=v_w_ig, b_ig=v_b_ig, lru_lambda=v_lru_lambda, q_norm_g=v_q_norm_g, k_norm_g=v_k_norm_g,
              sinks=v_sinks, w_mem_kv=v_w_mem_kv, xq_norm_g=v_xq_norm_g, xk_norm_g=v_xk_norm_g,
              out_norm_g=v_out_norm_g, w_out=v_w_out)

    delta, new_m, new_v = {}, {}, {}
    d2, m2, v2 = adamw(w_in[0].T, r_in.reshape(D_IN // 4, D_MODEL), m_w_in[0].T, v_w_in[0].T, "adamw_w_in")
    delta["w_in"], new_m["w_in"], new_v["w_in"] = d2.T[None], m2.T[None], v2.T[None]
    for name in ("w_mem_kv", "w_out"):
        shp = weights[name].shape
        d2, m2, v2 = adamw(weights[name][0], grads[name][0], ms[name][0], vs[name][0], "adamw_" + name)
        delta[name], new_m[name], new_v[name] = d2.reshape(shp), m2.reshape(shp), v2.reshape(shp)
    small_names = [n for n, _ in SMALL]
    packs = [_pack([_pad_to(d[n], sz) for n, sz in SMALL], SMALL_ROWS) for d in (weights, grads, ms, vs)]
    d_p, m_p, v_p = adamw(*packs, "adamw_small")
    offs2 = [0]
    for _, sz in SMALL:
        offs2.append(offs2[-1] + sz)
    for out_d, pk in ((delta, d_p), (new_m, m_p), (new_v, v_p)):
        fl = pk.reshape(-1)
        for k, n in enumerate(small_names):
            shp = weights[n].shape
            out_d[n] = fl[offs2[k]:offs2[k] + math.prod(shp)].reshape(shp)

    order = ("norm_g", "mem_norm_g", "w_in", "conv_w", "conv_b", "w_rg", "b_rg", "w_ig", "b_ig", "lru_lambda",
             "q_norm_g", "k_norm_g", "sinks", "w_mem_kv", "xq_norm_g", "xk_norm_g", "out_norm_g", "w_out")
    return (loss, gx[None], *[grads[n] for n in order], *[delta[n] for n in order], *[new_m[n] for n in order],
            *[new_v[n] for n in order])
```

```python
import functools
import math

import jax
import jax.numpy as jnp
import numpy as np
from jax import lax
from jax.experimental import pallas as pl
from jax.experimental.pallas import tpu as pltpu

F32 = jnp.float32
_MXU = jnp.bfloat16
_WIRE = jnp.bfloat16

D_MODEL = 1024
MEM_LEN = 256
HEAD = 64
LRU_W = 512
LRU_BLOCKS = 8
CONV_K = 4
LRU_C = 8.0
SWA_W = 256
KV_W = 128
XATT_W = 256
BLOCK = 128
D_IN = 2304
ROPE_THETA = 500000.0
ROPE_DIM = 16
EPS = 1e-6
NEG_INF = -1e30
C_LRUX, C_LRUG, C_SQ, C_SK, C_SV, C_SWAG, C_XQ, C_XG = 0, 512, 1024, 1280, 1408, 1536, 1792, 2048

ADAM_LR, ADAM_B1, ADAM_B2, ADAM_EPS, ADAM_WD, ADAM_STEP = 0.001, 0.9, 0.999, 1e-08, 0.01, 10

N_CHIPS = 4
ROW_TILE = 256
VMEM_LIMIT = 56 * 1024 * 1024
ADAM_BLOCK_BYTES = 1280 * 1024
GATHER_PIECES = (1, 1, 1)
MESH = pl.DeviceIdType.MESH


def _mm(a, b):
    return jnp.dot(a.astype(_MXU), b.astype(_MXU), preferred_element_type=F32)


def _mm_nt(a, b):
    return lax.dot_general(a.astype(_MXU), b.astype(_MXU), (((1,), (1,)), ((), ())), preferred_element_type=F32)


def _mm_tn(a, b):
    return lax.dot_general(a.astype(_MXU), b.astype(_MXU), (((0,), (0,)), ((), ())), preferred_element_type=F32)


def _group_matrix(width):
    r = lax.shift_right_logical(lax.broadcasted_iota(jnp.int32, (width, width), 0), 6)
    c = lax.shift_right_logical(lax.broadcasted_iota(jnp.int32, (width, width), 1), 6)
    return (r == c).astype(_MXU)


def _seg_mean(x, gm):
    return jnp.dot(x.astype(_MXU), gm, preferred_element_type=F32) * (1.0 / HEAD)


def _row_mean(x):
    return jnp.mean(x, axis=-1, keepdims=True)


def _col_sum(x):
    return jnp.sum(x, axis=0, keepdims=True)


def _sigmoid(x):
    return jax.nn.sigmoid(x)


def _softplus(z):
    e = jnp.exp(-jnp.abs(z))
    u = 1.0 + e
    log1p_e = jnp.where(u == 1.0, e, jnp.log(u) * (e / (u - 1.0)))
    return jnp.maximum(z, 0.0) + log1p_e


def _rope(t, c, s1, s2):
    return t * c + pltpu.roll(t, 120, 1) * s1 + pltpu.roll(t, 8, 1) * s2


def _rope_bwd(d, c, s1, s2):
    return d * c + pltpu.roll(d * s1, 8, 1) + pltpu.roll(d * s2, 120, 1)


def _lane_mask(width, lo, hi):
    lane = lax.broadcasted_iota(jnp.int32, (1, width), 1)
    return ((lane >= lo) & (lane < hi)).astype(F32)


def _swa_mask(first_block):
    qi = lax.broadcasted_iota(jnp.int32, (BLOCK, 2 * BLOCK), 0)
    kj = lax.broadcasted_iota(jnp.int32, (BLOCK, 2 * BLOCK), 1)
    rel = qi + BLOCK - kj
    ok = (rel >= 0) & (rel < BLOCK)
    return ok & (jnp.logical_not(first_block) | (kj >= BLOCK))


def _place_kv(t, scale):
    lo = t * (_lane_mask(KV_W, 0, HEAD) * scale)
    hi = t * (_lane_mask(KV_W, HEAD, KV_W) * scale)
    return [a.astype(_MXU) for a in (lo, pltpu.roll(lo, HEAD, 1), pltpu.roll(hi, HEAD, 1), hi)]


def _unplace_kv(d):
    return (_lane_mask(KV_W, 0, HEAD) * (d[0] + pltpu.roll(d[1], HEAD, 1))
            + _lane_mask(KV_W, HEAD, KV_W) * (d[3] + pltpu.roll(d[2], HEAD, 1)))


def _swa_probs(qh, ka, mask, sink):
    s = _mm_nt(qh, ka)
    s = jnp.where(mask, s, NEG_INF)
    m = jnp.maximum(jnp.max(s, axis=-1, keepdims=True), sink)
    p = jnp.exp(s - m)
    esink = jnp.exp(sink - m)
    inv = 1.0 / (jnp.sum(p, axis=-1, keepdims=True) + esink)
    return p * inv, esink * inv


def _mem_probs(s_all):
    out = []
    for j in range(4):
        s = s_all[:, MEM_LEN * j:MEM_LEN * (j + 1)]
        p = jnp.exp(s - jnp.max(s, axis=-1, keepdims=True))
        out.append(p * (1.0 / jnp.sum(p, axis=-1, keepdims=True)))
    return out


def _head_rows(t, scale):
    return jnp.concatenate([t * (_lane_mask(XATT_W, HEAD * j, HEAD * (j + 1)) * scale) for j in range(4)], axis=0)


def _lru_gates(xc, wg_ref, brg, big, lam):
    p0 = _mm(xc[:, :256], wg_ref[0])
    p1 = _mm(xc[:, 256:], wg_ref[1])
    rg = _sigmoid(jnp.concatenate([p0[:, :256], p1[:, :256]], axis=1) + brg)
    ig = _sigmoid(jnp.concatenate([p0[:, 256:], p1[:, 256:]], axis=1) + big)
    sp = _softplus(-lam)
    la = (-LRU_C) * rg * sp
    a = jnp.exp(la)
    th = jnp.tanh(la)
    one_minus_a2 = (-2.0 * th) / (1.0 - th)
    return rg, ig, sp, a, jnp.sqrt(one_minus_a2)


def _const_spec(shape, single=False):
    zeros = (0,) * len(shape)
    if single:
        return pl.BlockSpec(shape, lambda i: zeros, pipeline_mode=pl.Buffered(1))
    return pl.BlockSpec(shape, lambda i: zeros)


def _chip_of(x, y):
    return 2 * x + y


def _partners(x, y, c):
    north = c == 1
    near = (jnp.where(north, 1 - x, x), jnp.where(north, y, 1 - y))
    far = (jnp.where(north, x, 1 - x), jnp.where(north, 1 - y, y))
    return near, far, (1 - x, 1 - y)


def gather_weights(win_t, wout, wkv, convw):
    arrs = (win_t, wout, wkv)
    n = len(arrs)
    pieces = []
    for a, arr in enumerate(arrs):
        half = arr.shape[0] // 2
        step = half // GATHER_PIECES[a]
        pieces += [(a, off, step) for off in range(0, half, step)]
    npc = len(pieces)

    def body(a0, a1, a2, cw, o0, o1, o2, ocw, send, recv, lsem):
        ins, outs = (a0, a1, a2), (o0, o1, o2)
        x, y, c = lax.axis_index("x"), lax.axis_index("y"), lax.axis_index("c")
        sibling = (x, y, 1 - c)
        near, far, diag = _partners(x, y, c)
        chips = [near, far, diag]
        me = _chip_of(x, y)

        def landed(p, chip, half):
            a, off, rows_ = pieces[p]
            r = ins[a].shape[0]
            return outs[a].at[pl.ds(pl.multiple_of(chip * r + half * (r // 2) + off, 16), rows_)]

        def mine(p):
            a, off, rows_ = pieces[p]
            return ins[a].at[pl.ds(pl.multiple_of(c * (ins[a].shape[0] // 2) + off, 16), rows_)]

        def copy(k, src, dst, to):
            return pltpu.make_async_remote_copy(src_ref=src, dst_ref=dst, send_sem=send.at[k], recv_sem=recv.at[k],
                                                device_id=to, device_id_type=MESH)

        def cw_rows(chip):
            return ocw.at[pl.ds(pl.multiple_of(chip * 8, 8), 8)]

        locals_ = []
        for a in range(n):
            r = ins[a].shape[0]
            locals_.append(pltpu.make_async_copy(ins[a], outs[a].at[pl.ds(pl.multiple_of(me * r, 16), r)], lsem.at[a]))
        locals_.append(pltpu.make_async_copy(cw, cw_rows(me), lsem.at[n]))
        for cp in locals_:
            cp.start()

        sent = []
        for p in range(npc):
            for j in range(2):
                sent.append(copy(p * 6 + j, mine(p), landed(p, me, c), (*chips[j], c)))
        for j, chip in enumerate(chips):
            sent.append(copy(npc * 6 + j, cw, cw_rows(me), (*chip, c)))
        for cp in sent:
            cp.start()
        for p in range(npc):
            for j in range(3):
                got = landed(p, _chip_of(*chips[j]), c)
                copy(p * 6 + j, got, got, sibling).wait_recv()
                if j == 0:
                    sent.append(copy(p * 6 + 2, got, got, (*far, c)))
                    sent[-1].start()
                sent.append(copy(p * 6 + 3 + j, got, got, sibling))
                sent[-1].start()
        for p in range(npc):
            for j in range(3):
                got = landed(p, _chip_of(*chips[(1, 0, 2)[j]]), 1 - c)
                copy(p * 6 + 3 + j, got, got, sibling).wait_recv()
        for j, chip in enumerate(chips):
            got = cw_rows(_chip_of(*chip))
            copy(npc * 6 + j, got, got, (*chip, c)).wait_recv()
        for cp in sent:
            cp.wait_send()
        for cp in locals_:
            cp.wait()

    vm = pl.BlockSpec(memory_space=pltpu.VMEM)
    out_shape = tuple(jax.ShapeDtypeStruct((N_CHIPS * a.shape[0],) + a.shape[1:], a.dtype) for a in arrs) + (
        jax.ShapeDtypeStruct((N_CHIPS * 8, 128), F32),)
    n_rdma = npc * 6 + 3
    return pl.pallas_call(
        body, name="gather_weights", out_shape=out_shape,
        in_specs=[vm] * 4, out_specs=(vm,) * 4,
        scratch_shapes=[pltpu.SemaphoreType.DMA((n_rdma,)), pltpu.SemaphoreType.DMA((n_rdma,)),
                        pltpu.SemaphoreType.DMA((n + 1,))],
        compiler_params=pltpu.CompilerParams(vmem_limit_bytes=VMEM_LIMIT),
    )(win_t, wout, wkv, convw)


def mem_fwd(mem, mem_g, wkv, xk_g):
    def body(mem_ref, g_ref, w_ref, xk_ref, km_ref, vm_ref):
        mem_v = mem_ref[...]
        mn = mem_v * lax.rsqrt(_row_mean(mem_v * mem_v) + EPS) * g_ref[...]
        mkv = _mm(mn, w_ref[...])
        kpre = mkv[:, :XATT_W]
        gm = _group_matrix(XATT_W)
        km = kpre * lax.rsqrt(_seg_mean(kpre * kpre, gm) + EPS) * xk_ref[...]
        km_ref[...] = _head_rows(km, 0.125).astype(km_ref.dtype)
        vm_ref[...] = _head_rows(mkv[:, XATT_W:], 1.0).astype(vm_ref.dtype)

    vm = pl.BlockSpec(memory_space=pltpu.VMEM)
    rows_shape = jax.ShapeDtypeStruct((4 * MEM_LEN, XATT_W), _MXU)
    return pl.pallas_call(
        body, name="mem_fwd", out_shape=(rows_shape, rows_shape), in_specs=[vm] * 4, out_specs=(vm, vm),
    )(mem, mem_g, wkv, xk_g)


def mem_bwd(mem, mem_g, wkv, xk_g, dkm, dvm):
    def body(mem_ref, g_ref, w_ref, xk_ref, dkm_ref, dvm_ref, gw_ref, gg_ref, gxk_ref):
        mem_v = mem_ref[...]
        mh = mem_v * lax.rsqrt(_row_mean(mem_v * mem_v) + EPS)
        mn = mh * g_ref[...]
        mkv = _mm(mn, w_ref[...])
        kpre = mkv[:, :XATT_W]
        gm = _group_matrix(XATT_W)
        rk = lax.rsqrt(_seg_mean(kpre * kpre, gm) + EPS)
        kn = kpre * rk
        dk = jnp.zeros((MEM_LEN, XATT_W), F32)
        dv = jnp.zeros((MEM_LEN, XATT_W), F32)
        for j in range(4):
            mj = _lane_mask(XATT_W, HEAD * j, HEAD * (j + 1))
            dk = dk + dkm_ref[MEM_LEN * j:MEM_LEN * (j + 1), :] * (mj * 0.125)
            dv = dv + dvm_ref[MEM_LEN * j:MEM_LEN * (j + 1), :] * mj
        gxk_ref[...] = _col_sum(dk * kn)
        dkn = dk * xk_ref[...]
        dkpre = rk * (dkn - kn * _seg_mean(dkn * kn, gm))
        dmkv = jnp.concatenate([dkpre, dv], axis=1)
        gw_ref[...] = _mm_tn(mn, dmkv)
        dmn = _mm_nt(dmkv, w_ref[...])
        gg_ref[...] = _col_sum(dmn * mh)

    vm = pl.BlockSpec(memory_space=pltpu.VMEM)
    return pl.pallas_call(
        body, name="mem_bwd",
        out_shape=(jax.ShapeDtypeStruct((D_MODEL, 2 * XATT_W), F32), jax.ShapeDtypeStruct((1, D_MODEL), F32),
                   jax.ShapeDtypeStruct((1, XATT_W), F32)),
        in_specs=[vm] * 6, out_specs=(vm, vm, vm),
    )(mem, mem_g, wkv, xk_g, dkm, dvm)


def layer_fwd(x, tgt, rc, rs1, rs2, ng, win_t, cw, cb, wg, brg, big, lam, qg, kg, xqg, sinks, km, vm, og, wout):
    seq = x.shape[0]
    tm = min(ROW_TILE, seq)
    nt = seq // tm
    nb = tm // BLOCK

    def body(x_ref, t_ref, c_ref, s1_ref, s2_ref, ng_ref, win_ref, cw_ref, cb_ref, wg_ref, brg_ref, big_ref, lam_ref,
             qg_ref, kg_ref, xqg_ref, sink_ref, km_ref, vm_ref, og_ref, wout_ref,
             proj_ref, ya_ref, yb_ref, yc_ref, ycat_ref, xn_ref, dout_ref, pswa_ref, pmem_ref, psink_ref, gates_ref,
             a_ref, loss_ref,
             ext_ref, b_scr, hc_ref, kp_ref, vp_ref, lacc_ref):
        i = pl.program_id(0)

        @pl.when(i == 0)
        def _():
            ext_ref[0:8, :] = jnp.zeros((8, LRU_W), F32)
            hc_ref[...] = jnp.zeros_like(hc_ref)
            kp_ref[...] = jnp.zeros_like(kp_ref)
            vp_ref[...] = jnp.zeros_like(vp_ref)
            lacc_ref[...] = jnp.zeros_like(lacc_ref)

        xv = x_ref[...]
        xn = (xv * lax.rsqrt(_row_mean(xv * xv) + EPS) * ng_ref[...]).astype(_MXU)
        xn_ref[...] = xn.astype(xn_ref.dtype)
        proj_ref[...] = _mm_nt(xn, win_ref[...])

        u = proj_ref[:, C_LRUX:C_LRUX + LRU_W]
        ext_ref[8:8 + tm, :] = u
        xc = cb_ref[...]
        for k in range(CONV_K):
            xc = xc + cw_ref[k:k + 1, :] * ext_ref[pl.ds(5 + k, tm), :]
        ext_ref[0:8, :] = u[tm - 8:tm, :]
        rg, ig, sp, a, sq = _lru_gates(xc, wg_ref, brg_ref[...], big_ref[...], lam_ref[...])
        for k, t in enumerate((xc, rg, ig, sq)):
            gates_ref[:, LRU_W * k:LRU_W * (k + 1)] = t.astype(gates_ref.dtype)
        a_ref[...] = a
        b_scr[...] = sq * (ig * xc)
        row8 = lax.broadcasted_iota(jnp.int32, (8, LRU_W), 0)

        def scan_step(g, carry):
            r0 = pl.multiple_of(g * 8, 8)
            av = a_ref[pl.ds(r0, 8), :]
            bv = b_scr[pl.ds(r0, 8), :]
            for d in (1, 2, 4):
                a_sh = jnp.where(row8 >= d, pltpu.roll(av, d, 0), 1.0)
                b_sh = jnp.where(row8 >= d, pltpu.roll(bv, d, 0), 0.0)
                bv = bv + av * b_sh
                av = av * a_sh
            hv = bv + av * carry
            ya_ref[pl.ds(r0, 8), :] = hv
            return hv[7:8, :]

        hc_ref[0:1, :] = lax.fori_loop(0, tm // 8, scan_step, hc_ref[0:1, :], unroll=True)

        gm128 = _group_matrix(KV_W)
        cv, s1v, s2v = c_ref[...], s1_ref[...], s2_ref[...]

        def head_norm_rope(t, g):
            n = t * lax.rsqrt(_seg_mean(t * t, gm128) + EPS)
            return _rope(n * g, cv, s1v, s2v)

        qs_ = (head_norm_rope(proj_ref[:, C_SQ:C_SQ + 128], qg_ref[...]).astype(_MXU),
               head_norm_rope(proj_ref[:, C_SQ + 128:C_SQ + 256], qg_ref[...]).astype(_MXU))
        kr = head_norm_rope(proj_ref[:, C_SK:C_SK + KV_W], kg_ref[...])
        sv = proj_ref[:, C_SV:C_SV + KV_W]
        ka = _place_kv(jnp.concatenate([kp_ref[...], kr], axis=0), 0.125)
        va = _place_kv(jnp.concatenate([vp_ref[...], sv], axis=0), 1.0)
        kp_ref[...] = kr[tm - BLOCK:tm, :]
        vp_ref[...] = sv[tm - BLOCK:tm, :]
        lane128 = lax.broadcasted_iota(jnp.int32, (1, 128), 1)
        for b in range(nb):
            mask = _swa_mask((i == 0) & (b == 0)) if b == 0 else _swa_mask(False)
            band = slice(BLOCK * b, BLOCK * b + 2 * BLOCK)
            blk = slice(BLOCK * b, BLOCK * (b + 1))
            psink = jnp.zeros((BLOCK, 128), F32)
            for j in range(4):
                p, pk = _swa_probs(qs_[j // 2][blk], ka[j][band], mask, sink_ref[0, j])
                pswa_ref[blk, 2 * BLOCK * j:2 * BLOCK * (j + 1)] = p.astype(pswa_ref.dtype)
                psink = jnp.where(lane128 == j, pk, psink)
            psink_ref[blk, :] = psink
            for h in range(2):
                yb_ref[blk, KV_W * h:KV_W * (h + 1)] = _mm(
                    pswa_ref[blk, 4 * BLOCK * h:4 * BLOCK * (h + 1)],
                    jnp.concatenate([va[2 * h][band], va[2 * h + 1][band]], axis=0))

        gm256 = _group_matrix(XATT_W)
        xq = proj_ref[:, C_XQ:C_XQ + XATT_W]
        qx = xq * lax.rsqrt(_seg_mean(xq * xq, gm256) + EPS) * xqg_ref[...]
        pm = _mem_probs(_mm_nt(qx, km_ref[...]))
        for j in range(4):
            pmem_ref[:, MEM_LEN * j:MEM_LEN * (j + 1)] = pm[j].astype(pmem_ref.dtype)
        yc = _mm(pmem_ref[...], vm_ref[...])
        yc_ref[...] = yc

        def gated(y, g, gate):
            return y * lax.rsqrt(_row_mean(y * y) + EPS) * g * (gate * _sigmoid(gate))

        ogv = og_ref[...]
        za = gated(ya_ref[...], ogv[:, :512], proj_ref[:, C_LRUG:C_LRUG + LRU_W])
        zb = gated(yb_ref[...], ogv[:, 512:768], proj_ref[:, C_SWAG:C_SWAG + SWA_W])
        zc = gated(yc, ogv[:, 768:], proj_ref[:, C_XG:C_XG + XATT_W])
        ycat_ref[:, 0:512] = za.astype(ycat_ref.dtype)
        ycat_ref[:, 512:768] = zb.astype(ycat_ref.dtype)
        ycat_ref[:, 768:1024] = zc.astype(ycat_ref.dtype)
        out = xv + _mm(ycat_ref[...], wout_ref[...])
        err = out - t_ref[...]
        dout_ref[...] = (err * (1.0 / D_MODEL)).astype(dout_ref.dtype)
        lacc_ref[...] = lacc_ref[...] + (0.5 / D_MODEL) * jnp.sum(err * err)

        @pl.when(i == nt - 1)
        def _():
            loss_ref[...] = lacc_ref[...]

    def rows(ncol):
        return pl.BlockSpec((tm, ncol), lambda i: (i, 0))

    in_specs = [rows(D_MODEL), rows(D_MODEL), rows(128), rows(128), rows(128),
                _const_spec((1, D_MODEL)), _const_spec((D_IN, D_MODEL), True), _const_spec((CONV_K, LRU_W)),
                _const_spec((1, LRU_W)), _const_spec((2, 256, 512), True), _const_spec((1, LRU_W)),
                _const_spec((1, LRU_W)), _const_spec((1, LRU_W)), _const_spec((1, 128)), _const_spec((1, 128)),
                _const_spec((1, XATT_W)), pl.BlockSpec(memory_space=pltpu.SMEM),
                _const_spec((4 * MEM_LEN, XATT_W), True), _const_spec((4 * MEM_LEN, XATT_W), True),
                _const_spec((1, D_MODEL)), _const_spec((D_MODEL, D_MODEL), True)]
    out_shape = (jax.ShapeDtypeStruct((seq, D_IN), F32), jax.ShapeDtypeStruct((seq, LRU_W), F32),
                 jax.ShapeDtypeStruct((seq, SWA_W), F32), jax.ShapeDtypeStruct((seq, XATT_W), F32),
                 jax.ShapeDtypeStruct((seq, D_MODEL), _MXU), jax.ShapeDtypeStruct((seq, D_MODEL), _MXU),
                 jax.ShapeDtypeStruct((seq, D_MODEL), _MXU), jax.ShapeDtypeStruct((seq, 4 * 2 * BLOCK), _MXU),
                 jax.ShapeDtypeStruct((seq, 4 * MEM_LEN), _MXU), jax.ShapeDtypeStruct((seq, 128), F32),
                 jax.ShapeDtypeStruct((seq, 4 * LRU_W), _MXU), jax.ShapeDtypeStruct((seq, LRU_W), F32),
                 jax.ShapeDtypeStruct((8, 128), F32))
    out_specs = (rows(D_IN), rows(LRU_W), rows(SWA_W), rows(XATT_W), rows(D_MODEL), rows(D_MODEL), rows(D_MODEL),
                 rows(4 * 2 * BLOCK), rows(4 * MEM_LEN), rows(128), rows(4 * LRU_W), rows(LRU_W),
                 _const_spec((8, 128)))
    scratch = [pltpu.VMEM((tm + 8, LRU_W), F32), pltpu.VMEM((tm, LRU_W), F32),
               pltpu.VMEM((8, LRU_W), F32), pltpu.VMEM((BLOCK, KV_W), F32), pltpu.VMEM((BLOCK, KV_W), F32),
               pltpu.VMEM((8, 128), F32)]
    return pl.pallas_call(
        body, name="layer_fwd", grid=(nt,), out_shape=out_shape, in_specs=in_specs, out_specs=out_specs,
        scratch_shapes=scratch,
        compiler_params=pltpu.CompilerParams(dimension_semantics=("arbitrary",), vmem_limit_bytes=VMEM_LIMIT),
    )(x, tgt, rc, rs1, rs2, ng, win_t, cw, cb, wg, brg, big, lam, qg, kg, xqg, sinks, km, vm, og, wout)


def wgrad_out(ycat, dout):
    seq = ycat.shape[0]

    def body(y_ref, d_ref, o_ref):
        o_ref[...] = _mm_tn(y_ref[...], d_ref[...])

    return pl.pallas_call(
        body, name="wgrad_out", grid=(D_MODEL // 256,), out_shape=jax.ShapeDtypeStruct((D_MODEL, D_MODEL), F32),
        in_specs=[pl.BlockSpec((seq, 256), lambda j: (0, j)), _const_spec((seq, D_MODEL), True)],
        out_specs=pl.BlockSpec((256, D_MODEL), lambda j: (j, 0)),
        compiler_params=pltpu.CompilerParams(dimension_semantics=("arbitrary",), vmem_limit_bytes=VMEM_LIMIT),
    )(ycat, dout)


def wgrad_in_reduce(dproj, xn, bigs, g_small):
    seq = xn.shape[0]
    nblk = D_IN // 256
    nbig = len(bigs)
    stage_at = (0, 2, 4, 6, nblk - 1)

    def body(d_ref, x_ref, *refs):
        nres = nbig + 1
        ins, o_ref, res = refs[:nres], refs[nres], refs[nres + 1:2 * nres + 1]
        sums, rest, fsem = refs[2 * nres + 1:3 * nres + 1], refs[3 * nres + 1:-1], refs[-1]
        stages = _reduce_protocol(*_split_reduce_refs(ins + sums + rest, nbig, True))
        for at, stage in zip(stage_at, stages):
            pl.when(pl.program_id(0) == at)(stage)

        @pl.when(pl.program_id(0) == nblk - 1)
        def _():
            out = [pltpu.make_async_copy(sums[k], res[k], fsem.at[k]) for k in range(nres)]
            for cp in out:
                cp.start()
            for cp in out:
                cp.wait()

        o_ref[...] = _mm_tn(d_ref[...], x_ref[...])

    red_shape, scratch = _reduce_buffers(bigs, g_small)
    scratch = [pltpu.VMEM(r.shape, r.dtype) for r in red_shape] + scratch + [pltpu.SemaphoreType.DMA((nbig + 1,))]
    vm = pl.BlockSpec(memory_space=pltpu.VMEM)
    hbm = pl.BlockSpec(memory_space=pl.ANY)
    return pl.pallas_call(
        body, name="wgrad_in_reduce", grid=(nblk,),
        out_shape=(jax.ShapeDtypeStruct((D_IN, D_MODEL), F32), *red_shape),
        in_specs=[pl.BlockSpec((seq, 256), lambda j: (0, j)), _const_spec((seq, D_MODEL), True)] + [hbm] * nbig + [vm],
        out_specs=(pl.BlockSpec((256, D_MODEL), lambda j: (j, 0)),) + (hbm,) * len(red_shape),
        scratch_shapes=scratch,
        compiler_params=pltpu.CompilerParams(dimension_semantics=("arbitrary",), vmem_limit_bytes=VMEM_LIMIT),
    )(dproj, xn, *bigs, g_small)


def layer_bwd(x, dout, proj, ya, yb, yc, pswa, pmem, psink, gates, a_all, rc, rs1, rs2, ng, win_t, cw, cb, wg, brg, big, lam, qg, kg, xqg, sinks, km,
              vm, og, wout):
    seq = x.shape[0]
    tm = min(ROW_TILE, seq)
    nt = seq // tm
    nb = tm // BLOCK

    def body(x_ref, dout_ref, proj_ref, ya_ref, yb_ref, yc_ref, pswa_ref, pmem_ref, psink_ref, gates_ref, a_ref,
             c_ref, s1_ref, s2_ref,
             yah_ref, kvh_ref, ch_ref, s1h_ref, s2h_ref,
             ng_ref, win_ref, cw_ref, cb_ref, wg_ref, brg_ref, big_ref, lam_ref, qg_ref, kg_ref, xqg_ref, sink_ref,
             km_ref, vm_ref, og_ref, wout_ref,
             gx_ref, dproj_ref, gwg_ref, dkm_ref, dvm_ref, gng_ref, gog_ref, gcb_ref, gbrg_ref, gbig_ref, glam_ref,
             gcw_ref, gqn_ref, gkn_ref, gxqn_ref, gsink_ref,
             hext_ref, aext_ref, an_scr, dh_scr, g_scr, dxc_ext, gcar_ref, dkcar_ref, dvcar_ref):
        i = pl.program_id(0)
        tile = nt - 1 - i
        first_tile = tile == 0

        @pl.when(i == 0)
        def _():
            for r in (gwg_ref, dkm_ref, dvm_ref, gng_ref, gog_ref, gcb_ref, gbrg_ref, gbig_ref, glam_ref, gcw_ref,
                      gqn_ref, gkn_ref, gxqn_ref, gsink_ref, gcar_ref, dkcar_ref, dvcar_ref):
                r[...] = jnp.zeros_like(r)
            dxc_ext[tm:tm + 8, :] = jnp.zeros((8, LRU_W), F32)
            aext_ref[tm:tm + 8, :] = jnp.zeros((8, LRU_W), F32)

        xv = x_ref[...]
        dov = dout_ref[...]
        dz = _mm_nt(dov, wout_ref[...])
        ogv = og_ref[...]

        def group_bwd(y, gate, g, dzg):
            r = lax.rsqrt(_row_mean(y * y) + EPS)
            n = y * r
            sg = _sigmoid(gate)
            dgate = dzg * (n * g) * (sg * (1.0 + gate * (1.0 - sg)))
            dng = dzg * (gate * sg)
            dn = dng * g
            return r * (dn - n * _row_mean(dn * n)), dgate, _col_sum(dng * n)

        dya, dga, goa = group_bwd(ya_ref[...], proj_ref[:, C_LRUG:C_LRUG + LRU_W], ogv[:, :512], dz[:, :512])
        dyb, dgb, gob = group_bwd(yb_ref[...], proj_ref[:, C_SWAG:C_SWAG + SWA_W], ogv[:, 512:768], dz[:, 512:768])
        dyc, dgc, goc = group_bwd(yc_ref[...], proj_ref[:, C_XG:C_XG + XATT_W], ogv[:, 768:], dz[:, 768:])
        gog_ref[...] += jnp.concatenate([goa, gob, goc], axis=1)
        dproj_ref[:, C_LRUG:C_LRUG + LRU_W] = dga.astype(dproj_ref.dtype)
        dproj_ref[:, C_SWAG:C_SWAG + SWA_W] = dgb.astype(dproj_ref.dtype)
        dproj_ref[:, C_XG:C_XG + XATT_W] = dgc.astype(dproj_ref.dtype)

        gm256 = _group_matrix(XATT_W)
        xq = proj_ref[:, C_XQ:C_XQ + XATT_W]
        rq = lax.rsqrt(_seg_mean(xq * xq, gm256) + EPS)
        qn = xq * rq
        qx = qn * xqg_ref[...]
        qxb = qx.astype(_MXU)
        dycb = dyc.astype(_MXU)
        dp_all = _mm_nt(dycb, vm_ref[...])
        dsm = []
        for j in range(4):
            pj = pmem_ref[:, MEM_LEN * j:MEM_LEN * (j + 1)].astype(F32)
            dp = dp_all[:, MEM_LEN * j:MEM_LEN * (j + 1)]
            dsm.append((pj * (dp - jnp.sum(pj * dp, axis=-1, keepdims=True))).astype(_MXU))
        ds_all = jnp.concatenate(dsm, axis=1)
        dvm_ref[...] += _mm_tn(pmem_ref[...], dycb)
        dkm_ref[...] += _mm_tn(ds_all, qxb)
        dqx = _mm(ds_all, km_ref[...])
        gxqn_ref[...] += _col_sum(dqx * qn)
        dqn = dqx * xqg_ref[...]
        dproj_ref[:, C_XQ:C_XQ + XATT_W] = (rq * (dqn - qn * _seg_mean(dqn * qn, gm256))).astype(dproj_ref.dtype)

        gm128 = _group_matrix(KV_W)
        cv, s1v, s2v = c_ref[...], s1_ref[...], s2_ref[...]

        def head_norm(t):
            r = lax.rsqrt(_seg_mean(t * t, gm128) + EPS)
            return t * r, r

        qn_, qr_ = zip(head_norm(proj_ref[:, C_SQ:C_SQ + 128]), head_norm(proj_ref[:, C_SQ + 128:C_SQ + 256]))
        qrope = [_rope(qn_[h] * qg_ref[...], cv, s1v, s2v).astype(_MXU) for h in range(2)]
        kn, krr = head_norm(proj_ref[:, C_SK:C_SK + KV_W])
        kr = _rope(kn * kg_ref[...], cv, s1v, s2v)
        khn, _ = head_norm(kvh_ref[:, 0:KV_W])
        khr = _rope(khn * kg_ref[...], ch_ref[...], s1h_ref[...], s2h_ref[...])
        ka = _place_kv(jnp.concatenate([khr, kr], axis=0), 0.125)
        va = _place_kv(jnp.concatenate([kvh_ref[:, KV_W:2 * KV_W], proj_ref[:, C_SV:C_SV + KV_W]], axis=0), 1.0)
        lane128 = lax.broadcasted_iota(jnp.int32, (1, 128), 1)
        gsink = jnp.zeros((1, 128), F32)
        dk_band, dv_band, dq_blk = [], [], []
        for b in range(nb):
            band = slice(BLOCK * b, BLOCK * b + 2 * BLOCK)
            blk = slice(BLOCK * b, BLOCK * (b + 1))
            dka, dva, dsb = [], [], []
            deltas = jnp.zeros((BLOCK, 128), F32)
            for j in range(4):
                qh = qrope[j // 2][blk]
                doh = dyb[blk, KV_W * (j // 2):KV_W * (j // 2 + 1)].astype(_MXU)
                pb = pswa_ref[blk, 2 * BLOCK * j:2 * BLOCK * (j + 1)]
                p = pb.astype(F32)
                dp = _mm_nt(doh, va[j][band])
                delta = jnp.sum(p * dp, axis=-1, keepdims=True)
                ds = (p * (dp - delta)).astype(_MXU)
                deltas = jnp.where(lane128 == j, delta, deltas)
                dva.append(_mm_tn(pb, doh))
                dka.append(_mm_tn(ds, qh))
                dsb.append(ds)
            gsink = gsink - _col_sum(psink_ref[blk, :] * deltas)
            dk_band.append(_unplace_kv(dka) * 0.125)
            dv_band.append(_unplace_kv(dva))
            dq_blk.append([_mm(jnp.concatenate(dsb[2 * h:2 * h + 2], axis=1),
                               jnp.concatenate([ka[2 * h][band], ka[2 * h + 1][band]], axis=0)) for h in range(2)])
        gsink_ref[...] += gsink
        dk_rows = [dk_band[b][BLOCK:] + (dk_band[b + 1][:BLOCK] if b + 1 < nb else dkcar_ref[...]) for b in range(nb)]
        dv_rows = [dv_band[b][BLOCK:] + (dv_band[b + 1][:BLOCK] if b + 1 < nb else dvcar_ref[...]) for b in range(nb)]
        dkcar_ref[...] = dk_band[0][:BLOCK]
        dvcar_ref[...] = dv_band[0][:BLOCK]
        dkg = _rope_bwd(jnp.concatenate(dk_rows, axis=0), cv, s1v, s2v)
        gkn = _col_sum(dkg * kn)
        dkn = dkg * kg_ref[...]
        dproj_ref[:, C_SK:C_SK + KV_W] = (krr * (dkn - kn * _seg_mean(dkn * kn, gm128))).astype(dproj_ref.dtype)
        dproj_ref[:, C_SV:C_SV + KV_W] = jnp.concatenate(dv_rows, axis=0).astype(dproj_ref.dtype)
        gqn = jnp.zeros((1, 128), F32)
        for h in range(2):
            dqg = _rope_bwd(jnp.concatenate([dq_blk[b][h] for b in range(nb)], axis=0), cv, s1v, s2v)
            gqn = gqn + _col_sum(dqg * qn_[h])
            dqn_ = dqg * qg_ref[...]
            dproj_ref[:, C_SQ + 128 * h:C_SQ + 128 * (h + 1)] = (
                qr_[h] * (dqn_ - qn_[h] * _seg_mean(dqn_ * qn_[h], gm128))).astype(dproj_ref.dtype)
        gqn_ref[...] += gqn
        gkn_ref[...] += gkn

        u = proj_ref[:, C_LRUX:C_LRUX + LRU_W]
        xc, rg, ig, sq = (gates_ref[:, LRU_W * k:LRU_W * (k + 1)].astype(F32) for k in range(4))
        a = a_ref[...]
        sp = _softplus(-lam_ref[...])
        hext_ref[0:8, :] = jnp.where(first_tile, 0.0, yah_ref[...])
        hext_ref[8:8 + tm, :] = ya_ref[...]
        hprev = hext_ref[pl.ds(7, tm), :]
        aext_ref[0:tm, :] = a
        an_scr[...] = aext_ref[pl.ds(1, tm), :]
        dh_scr[...] = dya
        dh_scr[tm - 1:tm, :] = dh_scr[tm - 1:tm, :] + gcar_ref[0:1, :]
        row8 = lax.broadcasted_iota(jnp.int32, (8, LRU_W), 0)

        def scan_step(gi, carry):
            r0 = pl.multiple_of((tm // 8 - 1 - gi) * 8, 8)
            av = an_scr[pl.ds(r0, 8), :]
            bv = dh_scr[pl.ds(r0, 8), :]
            for d in (1, 2, 4):
                a_sh = jnp.where(row8 < 8 - d, pltpu.roll(av, 8 - d, 0), 1.0)
                b_sh = jnp.where(row8 < 8 - d, pltpu.roll(bv, 8 - d, 0), 0.0)
                bv = bv + av * b_sh
                av = av * a_sh
            gv = bv + av * carry
            g_scr[pl.ds(r0, 8), :] = gv
            return gv[0:1, :]

        g0 = lax.fori_loop(0, tm // 8, scan_step, jnp.zeros((1, LRU_W), F32), unroll=True)
        gcar_ref[0:1, :] = a[0:1, :] * g0
        gv = g_scr[...]
        da = gv * hprev
        dig = gv * sq * xc
        dxc = gv * sq * ig
        dla = da * a - gv * (ig * xc) * ((a * a) / sq)
        drg = dla * ((-LRU_C) * sp)
        glam_ref[...] += _col_sum(dla * rg)
        dpr = drg * rg * (1.0 - rg)
        dpi = dig * ig * (1.0 - ig)
        gbrg_ref[...] += _col_sum(dpr)
        gbig_ref[...] += _col_sum(dpi)
        dpre0 = jnp.concatenate([dpr[:, :256], dpi[:, :256]], axis=1).astype(_MXU)
        dpre1 = jnp.concatenate([dpr[:, 256:], dpi[:, 256:]], axis=1).astype(_MXU)
        gwg_ref[0] += _mm_tn(xc[:, :256], dpre0)
        gwg_ref[1] += _mm_tn(xc[:, 256:], dpre1)
        dxc = dxc + jnp.concatenate([_mm_nt(dpre0, wg_ref[0]), _mm_nt(dpre1, wg_ref[1])], axis=1)
        gcb_ref[...] += _col_sum(dxc)
        dxc_ext[0:tm, :] = dxc
        du = jnp.zeros((tm, LRU_W), F32)
        for k in range(CONV_K):
            later = dxc_ext[pl.ds(3 - k, tm), :]
            gcw_ref[k:k + 1, :] += _col_sum(later * u)
            du = du + cw_ref[k:k + 1, :] * later
        dxc_ext[tm:tm + 8, :] = dxc[0:8, :]
        dproj_ref[:, C_LRUX:C_LRUX + LRU_W] = du.astype(dproj_ref.dtype)

        dxn = _mm(dproj_ref[...], win_ref[...])
        rx = lax.rsqrt(_row_mean(xv * xv) + EPS)
        xh = xv * rx
        gng_ref[...] += _col_sum(dxn * xh)
        dxh = dxn * ng_ref[...]
        gx_ref[...] = dov.astype(F32) + rx * (dxh - xh * _row_mean(dxh * xh))

        @pl.when(i == nt - 1)
        def _():
            glam_ref[...] = glam_ref[...] * (LRU_C * _sigmoid(-lam_ref[...]))

    def rows(ncol, arr_cols_block=0):
        return pl.BlockSpec((tm, ncol), lambda i: (nt - 1 - i, arr_cols_block))

    def halo(nrow, ncol, colblk=0):
        per = tm // nrow
        return pl.BlockSpec((nrow, ncol), lambda i: (jnp.maximum((nt - 1 - i) * per - 1, 0), colblk))

    in_specs = [rows(D_MODEL), rows(D_MODEL), rows(D_IN), rows(LRU_W), rows(SWA_W), rows(XATT_W),
                rows(4 * 2 * BLOCK), rows(4 * MEM_LEN), rows(128), rows(4 * LRU_W), rows(LRU_W),
                rows(128), rows(128), rows(128),
                halo(8, LRU_W), halo(BLOCK, 2 * KV_W, C_SK // (2 * KV_W)),
                halo(BLOCK, 128), halo(BLOCK, 128), halo(BLOCK, 128),
                _const_spec((1, D_MODEL)), _const_spec((D_IN, D_MODEL), True), _const_spec((CONV_K, LRU_W)),
                _const_spec((1, LRU_W)), _const_spec((2, 256, 512), True), _const_spec((1, LRU_W)),
                _const_spec((1, LRU_W)), _const_spec((1, LRU_W)), _const_spec((1, 128)), _const_spec((1, 128)),
                _const_spec((1, XATT_W)), pl.BlockSpec(memory_space=pltpu.SMEM),
                _const_spec((4 * MEM_LEN, XATT_W), True), _const_spec((4 * MEM_LEN, XATT_W), True),
                _const_spec((1, D_MODEL)), _const_spec((D_MODEL, D_MODEL), True)]
    small = [(2, 256, 512), (4 * MEM_LEN, XATT_W), (4 * MEM_LEN, XATT_W), (1, D_MODEL), (1, D_MODEL), (1, LRU_W), (1, LRU_W),
             (1, LRU_W), (1, LRU_W), (CONV_K, LRU_W), (1, 128), (1, 128), (1, XATT_W), (1, 128)]
    out_shape = (jax.ShapeDtypeStruct((seq, D_MODEL), F32), jax.ShapeDtypeStruct((seq, D_IN), _MXU)) + tuple(
        jax.ShapeDtypeStruct(s, F32) for s in small)
    out_specs = (rows(D_MODEL), rows(D_IN)) + tuple(_const_spec(s) for s in small)
    scratch = [pltpu.VMEM((tm + 8, LRU_W), F32), pltpu.VMEM((tm + 8, LRU_W), F32),
               pltpu.VMEM((tm, LRU_W), F32), pltpu.VMEM((tm, LRU_W), F32), pltpu.VMEM((tm, LRU_W), F32),
               pltpu.VMEM((tm + 8, LRU_W), F32),
               pltpu.VMEM((8, LRU_W), F32), pltpu.VMEM((BLOCK, KV_W), F32), pltpu.VMEM((BLOCK, KV_W), F32)]
    return pl.pallas_call(
        body, name="layer_bwd", grid=(nt,), out_shape=out_shape, in_specs=in_specs, out_specs=out_specs,
        scratch_shapes=scratch,
        compiler_params=pltpu.CompilerParams(dimension_semantics=("arbitrary",), vmem_limit_bytes=VMEM_LIMIT),
    )(x, dout, proj, ya, yb, yc, pswa, pmem, psink, gates, a_all, rc, rs1, rs2, ya, proj, rc, rs1, rs2,
      ng, win_t, cw, cb, wg, brg, big, lam, qg, kg, xqg, sinks, km, vm, og, wout)


def _reduce_protocol(big, sm, outs, osm, r1, r1s, wire, r2, r2s, wire2, ps, own, send, recv, lsem):
    nbig = len(big)
    x, y, c = lax.axis_index("x"), lax.axis_index("y"), lax.axis_index("c")
    sibling = (x, y, 1 - c)
    near, far, diag = _partners(x, y, c)
    me, near_id, far_id, diag_id = _chip_of(x, y), _chip_of(*near), _chip_of(*far), _chip_of(*diag)

    def copy(k, src, dst, to):
        return pltpu.make_async_remote_copy(src_ref=src, dst_ref=dst, send_sem=send.at[k], recv_sem=recv.at[k],
                                            device_id=to, device_id_type=MESH)

    def sent(stage):
        cps = []
        for a in range(nbig):
            if stage == 0:
                cps.append(copy(5 * a, big[a].at[:, 1 - c], r1[a], sibling))
            elif stage == 1:
                cps.append(copy(5 * a + 1, wire[a].at[near_id], r2[a].at[0], (*near, c)))
                cps.append(copy(5 * a + 2, wire[a].at[diag_id], r2[a].at[1], (*near, c)))
            elif stage == 2:
                cps.append(copy(5 * a + 3, wire2[a], r2[a].at[2], (*far, c)))
            elif stage == 3:
                cps.append(copy(5 * a + 4, outs[a].at[c], outs[a].at[c], sibling))
        if sm is not None:
            src, dst, to = ((sm.at[1 - c], r1s, sibling), (r1s, r2s.at[0], (*near, c)), (ps, r2s.at[1], (*far, c)),
                            (osm.at[c], osm.at[c], sibling))[stage]
            cps.append(copy(5 * nbig + stage, src, dst, to))
        return cps

    def arrived(k, ref):
        copy(k, ref, ref, sibling).wait_recv()

    def loads():
        return [pltpu.make_async_copy(big[a].at[:, c], own[a], lsem.at[a]) for a in range(nbig)]

    def stage0():
        for cp in sent(0) + loads():
            cp.start()

    def stage1():
        for a in range(nbig):
            loads()[a].wait()
            arrived(5 * a, r1[a])
            for k in range(N_CHIPS):
                r1[a][k] = own[a][k] + r1[a][k]
                wire[a][k] = r1[a][k].astype(wire[a].dtype)
        if sm is not None:
            arrived(5 * nbig, r1s)
            r1s[...] = sm[c] + r1s[...]
        for cp in sent(1):
            cp.start()

    def stage2():
        for a in range(nbig):
            arrived(5 * a + 1, r2[a].at[0])
            arrived(5 * a + 2, r2[a].at[1])
            r1[a][me] = r1[a][me] + r2[a][0].astype(F32)
            wire2[a][...] = (r1[a][far_id] + r2[a][1].astype(F32)).astype(wire2[a].dtype)
        if sm is not None:
            arrived(5 * nbig + 1, r2s.at[0])
            ps[...] = r1s[...] + r2s[0]
        for cp in sent(2):
            cp.start()

    def stage3():
        for a in range(nbig):
            arrived(5 * a + 3, r2[a].at[2])
            outs[a][c] = r1[a][me] + r2[a][2].astype(F32)
        if sm is not None:
            arrived(5 * nbig + 2, r2s.at[1])
            osm[c] = ps[...] + r2s[1]
        for cp in sent(3):
            cp.start()

    def stage4():
        for a in range(nbig):
            arrived(5 * a + 4, outs[a].at[1 - c])
        if sm is not None:
            arrived(5 * nbig + 3, osm.at[1 - c])
        for stage in range(4):
            for cp in sent(stage):
                cp.wait_send()

    return [stage0, stage1, stage2, stage3, stage4]


def _reduce_buffers(bigs, g_small):
    half = [b.shape[2:] for b in bigs]
    sm_half = None if g_small is None else g_small.shape[1:]
    out_shape = [jax.ShapeDtypeStruct((2,) + h, F32) for h in half]
    small = lambda lead: [] if g_small is None else [pltpu.VMEM(lead + sm_half, F32)]
    if g_small is not None:
        out_shape.append(jax.ShapeDtypeStruct(g_small.shape, F32))
    n_sem = 5 * len(bigs) + 4
    scratch = ([pltpu.VMEM((N_CHIPS,) + h, F32) for h in half] + small(())
               + [pltpu.VMEM((N_CHIPS,) + h, _WIRE) for h in half]
               + [pltpu.VMEM((3,) + h, _WIRE) for h in half] + small((2,))
               + [pltpu.VMEM(h, _WIRE) for h in half] + small(())
               + [pltpu.VMEM((N_CHIPS,) + h, F32) for h in half]
               + [pltpu.SemaphoreType.DMA((n_sem,)), pltpu.SemaphoreType.DMA((n_sem,)),
                  pltpu.SemaphoreType.DMA((len(bigs),))])
    return out_shape, scratch


def _split_reduce_refs(refs, nbig, has_small):
    it = iter(refs)
    take = lambda n: [next(it) for _ in range(n)]
    one = lambda: next(it) if has_small else None
    big, sm = take(nbig), one()
    outs, osm = take(nbig), one()
    r1, r1s, wire, r2, r2s, wire2, ps, own = take(nbig), one(), take(nbig), take(nbig), one(), take(nbig), one(), take(nbig)
    send, recv, lsem = take(3)
    return big, sm, outs, osm, r1, r1s, wire, r2, r2s, wire2, ps, own, send, recv, lsem


def reduce_grads(bigs, g_small, name):
    nbig = len(bigs)

    def body(*refs):
        for stage in _reduce_protocol(*_split_reduce_refs(refs, nbig, g_small is not None)):
            stage()

    out_shape, scratch = _reduce_buffers(bigs, g_small)
    vm = pl.BlockSpec(memory_space=pltpu.VMEM)
    hbm = pl.BlockSpec(memory_space=pl.ANY)
    operands = list(bigs) + ([] if g_small is None else [g_small])
    return pl.pallas_call(
        body, name=name, out_shape=tuple(out_shape), in_specs=[hbm] * nbig + [vm] * (g_small is not None),
        out_specs=(vm,) * len(out_shape), scratch_shapes=scratch,
        compiler_params=pltpu.CompilerParams(vmem_limit_bytes=VMEM_LIMIT),
    )(*operands)


def adamw(w, g, m, v, name):
    rows_, cols = w.shape
    tr = max(t for t in range(8, rows_ + 1, 8) if rows_ % t == 0 and t * cols * 4 <= ADAM_BLOCK_BYTES)

    def body(w_ref, g_ref, m_ref, v_ref, d_ref, nm_ref, nv_ref):
        gv = g_ref[...]
        nm = ADAM_B1 * m_ref[...] + (1.0 - ADAM_B1) * gv
        nv = ADAM_B2 * v_ref[...] + (1.0 - ADAM_B2) * (gv * gv)
        m_hat = nm / (1.0 - ADAM_B1 ** ADAM_STEP)
        v_hat = nv / (1.0 - ADAM_B2 ** ADAM_STEP)
        d_ref[...] = (-ADAM_LR) * (m_hat / (jnp.sqrt(v_hat) + ADAM_EPS) + ADAM_WD * w_ref[...])
        nm_ref[...] = nm
        nv_ref[...] = nv

    spec = pl.BlockSpec((tr, cols), lambda i: (i, 0))
    shp = jax.ShapeDtypeStruct(w.shape, F32)
    return pl.pallas_call(
        body, name=name, grid=(rows_ // tr,), out_shape=(shp, shp, shp), in_specs=[spec] * 4, out_specs=(spec,) * 3,
        compiler_params=pltpu.CompilerParams(dimension_semantics=("arbitrary",)),
    )(w, g, m, v)


SMALL = (("norm_g", 1024), ("mem_norm_g", 1024), ("conv_w", 512), ("conv_b", 512), ("w_rg", 32768), ("b_rg", 512),
         ("w_ig", 32768), ("b_ig", 512), ("lru_lambda", 512), ("q_norm_g", 128), ("k_norm_g", 128), ("sinks", 128),
         ("xq_norm_g", 128), ("xk_norm_g", 128), ("out_norm_g", 1024))
SMALL_ROWS = 576


def _pack(parts, rows_):
    flat = jnp.concatenate([p.reshape(-1) for p in parts])
    return jnp.pad(flat, (0, rows_ * 128 - flat.shape[0])).reshape(rows_, 128)


def _pad_to(v, n):
    v = v.reshape(-1)
    return jnp.pad(v, (0, n - v.shape[0]))


def _block_diag_gates(w_rg, w_ig):
    def bd(w4):
        z = jnp.zeros((4, HEAD, 4, HEAD), w4.dtype)
        idx = jnp.arange(4)
        return z.at[idx, :, idx, :].set(w4).reshape(256, 256)

    return jnp.stack([jnp.concatenate([bd(w_rg[4 * h:4 * h + 4]), bd(w_ig[4 * h:4 * h + 4])], axis=1) for h in (0, 1)])


def _diag_blocks(g):
    out = []
    for part in (0, 1):
        blocks = []
        for h in (0, 1):
            sub = g[h, :, 256 * part:256 * (part + 1)].reshape(4, HEAD, 4, HEAD)
            blocks.append(jnp.stack([sub[n, :, n, :] for n in range(4)]))
        out.append(jnp.concatenate(blocks, axis=0))
    return out


def _rope_tables(seq):
    pos = np.arange(seq, dtype=np.float32)
    inv_freq = (np.float32(ROPE_THETA) ** (-(np.arange(0, ROPE_DIM, 2, dtype=np.float32) / np.float32(ROPE_DIM)))
                ).astype(np.float32)
    ang = (pos[:, None] * inv_freq[None, :]).astype(np.float32)
    cos, sin = np.cos(ang).astype(np.float32), np.sin(ang).astype(np.float32)
    z = lambda n: np.zeros((seq, n), np.float32)
    c64 = np.concatenate([cos, cos, np.ones((seq, HEAD - ROPE_DIM), np.float32)], axis=1)
    s1_64 = np.concatenate([-sin, z(HEAD - 8)], axis=1)
    s2_64 = np.concatenate([z(8), sin, z(HEAD - ROPE_DIM)], axis=1)
    return tuple(jnp.asarray(np.concatenate([t, t], axis=1)) for t in (c64, s1_64, s2_64))


def kernel(x, mem, norm_g, mem_norm_g, w_in, conv_w, conv_b, w_rg, b_rg, w_ig, b_ig, lru_lambda, q_norm_g, k_norm_g, sinks, w_mem_kv, xq_norm_g, xk_norm_g, out_norm_g, w_out, loss_target, m_norm_g, m_mem_norm_g, m_w_in, m_conv_w, m_conv_b, m_w_rg, m_b_rg, m_w_ig, m_b_ig, m_lru_lambda, m_q_norm_g, m_k_norm_g, m_sinks, m_w_mem_kv, m_xq_norm_g, m_xk_norm_g, m_out_norm_g, m_w_out, v_norm_g, v_mem_norm_g, v_w_in, v_conv_w, v_conv_b, v_w_rg, v_b_rg, v_w_ig, v_b_ig, v_lru_lambda, v_q_norm_g, v_k_norm_g, v_sinks, v_w_mem_kv, v_xq_norm_g, v_xk_norm_g, v_out_norm_g, v_w_out):
    seq = x.shape[1]
    chip = 2 * lax.axis_index("x") + lax.axis_index("y")
    xs, tgt, mems = x[0], loss_target[0], mem[0]

    win_t_sh = w_in[0].T.astype(_MXU)
    cw_sh = jnp.pad(conv_w[0], ((0, 4), (0, 0)))
    win_t, wout, wkv, cw_all = gather_weights(win_t_sh, w_out[0].astype(_MXU), w_mem_kv[0].astype(_MXU), cw_sh)
    cw = cw_all.reshape(N_CHIPS, 8, 128)[:, :CONV_K].transpose(1, 0, 2).reshape(CONV_K, LRU_W)

    rc, rs1, rs2 = _rope_tables(seq)
    wg = _block_diag_gates(w_rg[0], w_ig[0]).astype(_MXU)
    qg = jnp.tile(q_norm_g, (1, 2))
    kg = jnp.tile(k_norm_g, (1, 2))
    xqg = jnp.tile(xq_norm_g, (1, 4))
    xkg = jnp.tile(xk_norm_g, (1, 4))

    km, vm = mem_fwd(mems, mem_norm_g, wkv, xkg)
    proj, ya, yb, yc, ycat, xn, dout, pswa, pmem, psink, gates, a_all, loss8 = layer_fwd(
        xs, tgt, rc, rs1, rs2, norm_g, win_t, cw, conv_b, wg, b_rg, b_ig, lru_lambda, qg, kg, xqg, sinks, km, vm,
        out_norm_g, wout)
    g_wout = wgrad_out(ycat, dout)
    (gx, dproj, g_wg, dkm, dvm, g_ng, g_og, g_cb, g_brg, g_big, g_lam, g_cw, g_qn, g_kn, g_xqn, g_sink) = layer_bwd(
        xs, dout, proj, ya, yb, yc, pswa, pmem, psink, gates, a_all, rc, rs1, rs2, norm_g, win_t, cw, conv_b, wg, b_rg, b_ig,
        lru_lambda, qg, kg, xqg, sinks, km, vm, out_norm_g, wout)
    g_wkv, g_mng, g_xkn = mem_bwd(mems, mem_norm_g, wkv, xkg, dkm, dvm)

    g_wrg, g_wig = _diag_blocks(g_wg)
    fold = lambda v, n: v.reshape(n, HEAD).sum(axis=0)
    small_g = _pack([g_ng, g_mng, g_cw, g_cb, g_wrg, g_brg, g_wig, g_big, g_lam, _pad_to(fold(g_qn, 2), 128),
                     _pad_to(fold(g_kn, 2), 128), g_sink, _pad_to(fold(g_xqn, 4), 128), _pad_to(fold(g_xkn, 4), 128),
                     g_og, loss8[0:1]], SMALL_ROWS)
    early = [g_wout.reshape(N_CHIPS, 2, D_MODEL // 8, D_MODEL), g_wkv.reshape(N_CHIPS, 2, D_MODEL // 8, 2 * XATT_W)]
    g_win_t, r_out, r_kv, r_small = wgrad_in_reduce(dproj, xn, early, small_g.reshape(2, SMALL_ROWS // 2, 128))
    (r_in,) = reduce_grads([g_win_t.reshape(N_CHIPS, 2, D_IN // 8, D_MODEL)], None, "reduce_w_in")

    flat = r_small.reshape(-1)
    sizes = (1024, 1024, 2048, 512, 32768, 512, 32768, 512, 512, 128, 128, 128, 128, 128, 1024)
    offs = [0]
    for s in sizes:
        offs.append(offs[-1] + s)
    loss = flat[offs[-1]]
    piece = {name: flat[offs[k]:offs[k + 1]] for k, (name, _) in enumerate(SMALL)}
    g_cw_mine = lax.dynamic_slice(piece["conv_w"].reshape(CONV_K, LRU_W), (0, chip * 128), (CONV_K, 128))
    grads = {
        "norm_g": piece["norm_g"].reshape(1, 1024), "mem_norm_g": piece["mem_norm_g"].reshape(1, 1024),
        "w_in": r_in.reshape(D_IN // 4, D_MODEL).T[None], "conv_w": g_cw_mine[None],
        "conv_b": piece["conv_b"].reshape(1, 512), "w_rg": piece["w_rg"].reshape(1, 8, HEAD, HEAD),
        "b_rg": piece["b_rg"].reshape(1, 512), "w_ig": piece["w_ig"].reshape(1, 8, HEAD, HEAD),
        "b_ig": piece["b_ig"].reshape(1, 512), "lru_lambda": piece["lru_lambda"].reshape(1, 512),
        "q_norm_g": piece["q_norm_g"][:HEAD].reshape(1, HEAD), "k_norm_g": piece["k_norm_g"][:HEAD].reshape(1, HEAD),
        "sinks": piece["sinks"][:4].reshape(1, 4), "w_mem_kv": r_kv.reshape(D_MODEL // 4, 2 * XATT_W)[None],
        "xq_norm_g": piece["xq_norm_g"][:HEAD].reshape(1, HEAD), "xk_norm_g": piece["xk_norm_g"][:HEAD].reshape(1, HEAD),
        "out_norm_g": piece["out_norm_g"].reshape(1, 1024), "w_out": r_out.reshape(D_MODEL // 4, D_MODEL)[None],
    }
    weights = dict(norm_g=norm_g, mem_norm_g=mem_norm_g, w_in=w_in, conv_w=conv_w, conv_b=conv_b, w_rg=w_rg, b_rg=b_rg,
                   w_ig=w_ig, b_ig=b_ig, lru_lambda=lru_lambda, q_norm_g=q_norm_g, k_norm_g=k_norm_g, sinks=sinks,
                   w_mem_kv=w_mem_kv, xq_norm_g=xq_norm_g, xk_norm_g=xk_norm_g, out_norm_g=out_norm_g, w_out=w_out)
    ms = dict(norm_g=m_norm_g, mem_norm_g=m_mem_norm_g, w_in=m_w_in, conv_w=m_conv_w, conv_b=m_conv_b, w_rg=m_w_rg,
              b_rg=m_b_rg, w_ig=m_w_ig, b_ig=m_b_ig, lru_lambda=m_lru_lambda, q_norm_g=m_q_norm_g, k_norm_g=m_k_norm_g,
              sinks=m_sinks, w_mem_kv=m_w_mem_kv, xq_norm_g=m_xq_norm_g, xk_norm_g=m_xk_norm_g,
              out_norm_g=m_out_norm_g, w_out=m_w_out)
    vs = dict(norm_g=v_norm_g, mem_norm_g=v_mem_norm_g, w_in=v_w_in, conv_w=v_conv_w, conv_b=v_conv_b, w_rg=v_w_rg,
              b_rg=v_b_rg, w_ig=v_w_ig, b_ig=v_b_ig, lru_lambda=v_lru_lambda, q_norm_g=v_q_norm_g, k_norm_g=v_k_norm_g,
              sinks=v_sinks, w_mem_kv=v_w_mem_kv, xq_norm_g=v_xq_norm_g, xk_norm_g=v_xk_norm_g,
              out_norm_g=v_out_norm_g, w_out=v_w_out)

    delta, new_m, new_v = {}, {}, {}
    d2, m2, v2 = adamw(w_in[0].T, r_in.reshape(D_IN // 4, D_MODEL), m_w_in[0].T, v_w_in[0].T, "adamw_w_in")
    delta["w_in"], new_m["w_in"], new_v["w_in"] = d2.T[None], m2.T[None], v2.T[None]
    for name in ("w_mem_kv", "w_out"):
        shp = weights[name].shape
        d2, m2, v2 = adamw(weights[name][0], grads[name][0], ms[name][0], vs[name][0], "adamw_" + name)
        delta[name], new_m[name], new_v[name] = d2.reshape(shp), m2.reshape(shp), v2.reshape(shp)
    small_names = [n for n, _ in SMALL]
    packs = [_pack([_pad_to(d[n], sz) for n, sz in SMALL], SMALL_ROWS) for d in (weights, grads, ms, vs)]
    d_p, m_p, v_p = adamw(*packs, "adamw_small")
    offs2 = [0]
    for _, sz in SMALL:
        offs2.append(offs2[-1] + sz)
    for out_d, pk in ((delta, d_p), (new_m, m_p), (new_v, v_p)):
        fl = pk.reshape(-1)
        for k, n in enumerate(small_names):
            shp = weights[n].shape
            out_d[n] = fl[offs2[k]:offs2[k] + math.prod(shp)].reshape(shp)

    order = ("norm_g", "mem_norm_g", "w_in", "conv_w", "conv_b", "w_rg", "b_rg", "w_ig", "b_ig", "lru_lambda",
             "q_norm_g", "k_norm_g", "sinks", "w_mem_kv", "xq_norm_g", "xk_norm_g", "out_norm_g", "w_out")
    return (loss, gx[None], *[grads[n] for n in order], *[delta[n] for n in order], *[new_m[n] for n in order],
            *[new_v[n] for n in order])
```

```python
import functools
import math

import jax
import jax.numpy as jnp
import numpy as np
from jax import lax
from jax.experimental import pallas as pl
from jax.experimental.pallas import tpu as pltpu

F32 = jnp.float32
_MXU = jnp.bfloat16
_WIRE = jnp.bfloat16

D_MODEL = 1024
MEM_LEN = 256
HEAD = 64
LRU_W = 512
LRU_BLOCKS = 8
CONV_K = 4
LRU_C = 8.0
SWA_W = 256
KV_W = 128
XATT_W = 256
BLOCK = 128
D_IN = 2304
ROPE_THETA = 500000.0
ROPE_DIM = 16
EPS = 1e-6
NEG_INF = -1e30
C_LRUX, C_LRUG, C_SQ, C_SK, C_SV, C_SWAG, C_XQ, C_XG = 0, 512, 1024, 1280, 1408, 1536, 1792, 2048

ADAM_LR, ADAM_B1, ADAM_B2, ADAM_EPS, ADAM_WD, ADAM_STEP = 0.001, 0.9, 0.999, 1e-08, 0.01, 10

N_CHIPS = 4
ROW_TILE = 256
VMEM_LIMIT = 56 * 1024 * 1024
ADAM_BLOCK_BYTES = 1280 * 1024
GATHER_PIECES = (1, 1, 1)
MESH = pl.DeviceIdType.MESH


def _mm(a, b):
    return jnp.dot(a.astype(_MXU), b.astype(_MXU), preferred_element_type=F32)


def _mm_nt(a, b):
    return lax.dot_general(a.astype(_MXU), b.astype(_MXU), (((1,), (1,)), ((), ())), preferred_element_type=F32)


def _mm_tn(a, b):
    return lax.dot_general(a.astype(_MXU), b.astype(_MXU), (((0,), (0,)), ((), ())), preferred_element_type=F32)


def _group_matrix(width):
    r = lax.shift_right_logical(lax.broadcasted_iota(jnp.int32, (width, width), 0), 6)
    c = lax.shift_right_logical(lax.broadcasted_iota(jnp.int32, (width, width), 1), 6)
    return (r == c).astype(_MXU)


def _seg_mean(x, gm):
    return jnp.dot(x.astype(_MXU), gm, preferred_element_type=F32) * (1.0 / HEAD)


def _row_mean(x):
    return jnp.mean(x, axis=-1, keepdims=True)


def _col_sum(x):
    return jnp.sum(x, axis=0, keepdims=True)


def _sigmoid(x):
    return jax.nn.sigmoid(x)


def _softplus(z):
    e = jnp.exp(-jnp.abs(z))
    u = 1.0 + e
    log1p_e = jnp.where(u == 1.0, e, jnp.log(u) * (e / (u - 1.0)))
    return jnp.maximum(z, 0.0) + log1p_e


def _rope(t, c, s1, s2):
    return t * c + pltpu.roll(t, 120, 1) * s1 + pltpu.roll(t, 8, 1) * s2


def _rope_bwd(d, c, s1, s2):
    return d * c + pltpu.roll(d * s1, 8, 1) + pltpu.roll(d * s2, 120, 1)


def _lane_mask(width, lo, hi):
    lane = lax.broadcasted_iota(jnp.int32, (1, width), 1)
    return ((lane >= lo) & (lane < hi)).astype(F32)


def _swa_mask(first_block):
    qi = lax.broadcasted_iota(jnp.int32, (BLOCK, 2 * BLOCK), 0)
    kj = lax.broadcasted_iota(jnp.int32, (BLOCK, 2 * BLOCK), 1)
    rel = qi + BLOCK - kj
    ok = (rel >= 0) & (rel < BLOCK)
    return ok & (jnp.logical_not(first_block) | (kj >= BLOCK))


def _place_kv(t, scale):
    lo = t * (_lane_mask(KV_W, 0, HEAD) * scale)
    hi = t * (_lane_mask(KV_W, HEAD, KV_W) * scale)
    return [a.astype(_MXU) for a in (lo, pltpu.roll(lo, HEAD, 1), pltpu.roll(hi, HEAD, 1), hi)]


def _unplace_kv(d):
    return (_lane_mask(KV_W, 0, HEAD) * (d[0] + pltpu.roll(d[1], HEAD, 1))
            + _lane_mask(KV_W, HEAD, KV_W) * (d[3] + pltpu.roll(d[2], HEAD, 1)))


def _swa_probs(qh, ka, mask, sink):
    s = _mm_nt(qh, ka)
    s = jnp.where(mask, s, NEG_INF)
    m = jnp.maximum(jnp.max(s, axis=-1, keepdims=True), sink)
    p = jnp.exp(s - m)
    esink = jnp.exp(sink - m)
    inv = 1.0 / (jnp.sum(p, axis=-1, keepdims=True) + esink)
    return p * inv, esink * inv


def _mem_probs(s_all):
    out = []
    for j in range(4):
        s = s_all[:, MEM_LEN * j:MEM_LEN * (j + 1)]
        p = jnp.exp(s - jnp.max(s, axis=-1, keepdims=True))
        out.append(p * (1.0 / jnp.sum(p, axis=-1, keepdims=True)))
    return out


def _head_rows(t, scale):
    return jnp.concatenate([t * (_lane_mask(XATT_W, HEAD * j, HEAD * (j + 1)) * scale) for j in range(4)], axis=0)


def _lru_gates(xc, wg_ref, brg, big, lam):
    p0 = _mm(xc[:, :256], wg_ref[0])
    p1 = _mm(xc[:, 256:], wg_ref[1])
    rg = _sigmoid(jnp.concatenate([p0[:, :256], p1[:, :256]], axis=1) + brg)
    ig = _sigmoid(jnp.concatenate([p0[:, 256:], p1[:, 256:]], axis=1) + big)
    sp = _softplus(-lam)
    la = (-LRU_C) * rg * sp
    a = jnp.exp(la)
    th = jnp.tanh(la)
    one_minus_a2 = (-2.0 * th) / (1.0 - th)
    return rg, ig, sp, a, jnp.sqrt(one_minus_a2)


def _const_spec(shape, single=False):
    zeros = (0,) * len(shape)
    if single:
        return pl.BlockSpec(shape, lambda i: zeros, pipeline_mode=pl.Buffered(1))
    return pl.BlockSpec(shape, lambda i: zeros)


def _chip_of(x, y):
    return 2 * x + y


def _partners(x, y, c):
    north = c == 1
    near = (jnp.where(north, 1 - x, x), jnp.where(north, y, 1 - y))
    far = (jnp.where(north, x, 1 - x), jnp.where(north, 1 - y, y))
    return near, far, (1 - x, 1 - y)


def gather_weights(win_t, wout, wkv, convw):
    arrs = (win_t, wout, wkv)
    n = len(arrs)
    pieces = []
    for a, arr in enumerate(arrs):
        half = arr.shape[0] // 2
        step = half // GATHER_PIECES[a]
        pieces += [(a, off, step) for off in range(0, half, step)]
    npc = len(pieces)

    def body(a0, a1, a2, cw, o0, o1, o2, ocw, send, recv, lsem):
        ins, outs = (a0, a1, a2), (o0, o1, o2)
        x, y, c = lax.axis_index("x"), lax.axis_index("y"), lax.axis_index("c")
        sibling = (x, y, 1 - c)
        near, far, diag = _partners(x, y, c)
        chips = [near, far, diag]
        me = _chip_of(x, y)

        def landed(p, chip, half):
            a, off, rows_ = pieces[p]
            r = ins[a].shape[0]
            return outs[a].at[pl.ds(pl.multiple_of(chip * r + half * (r // 2) + off, 16), rows_)]

        def mine(p):
            a, off, rows_ = pieces[p]
            return ins[a].at[pl.ds(pl.multiple_of(c * (ins[a].shape[0] // 2) + off, 16), rows_)]

        def copy(k, src, dst, to):
            return pltpu.make_async_remote_copy(src_ref=src, dst_ref=dst, send_sem=send.at[k], recv_sem=recv.at[k],
                                                device_id=to, device_id_type=MESH)

        def cw_rows(chip):
            return ocw.at[pl.ds(pl.multiple_of(chip * 8, 8), 8)]

        locals_ = []
        for a in range(n):
            r = ins[a].shape[0]
            locals_.append(pltpu.make_async_copy(ins[a], outs[a].at[pl.ds(pl.multiple_of(me * r, 16), r)], lsem.at[a]))
        locals_.append(pltpu.make_async_copy(cw, cw_rows(me), lsem.at[n]))
        for cp in locals_:
            cp.start()

        sent = []
        for p in range(npc):
            for j in range(2):
                sent.append(copy(p * 6 + j, mine(p), landed(p, me, c), (*chips[j], c)))
        for j, chip in enumerate(chips):
            sent.append(copy(npc * 6 + j, cw, cw_rows(me), (*chip, c)))
        for cp in sent:
            cp.start()
        for p in range(npc):
            for j in range(3):
                got = landed(p, _chip_of(*chips[j]), c)
                copy(p * 6 + j, got, got, sibling).wait_recv()
                if j == 0:
                    sent.append(copy(p * 6 + 2, got, got, (*far, c)))
                    sent[-1].start()
                sent.append(copy(p * 6 + 3 + j, got, got, sibling))
                sent[-1].start()
        for p in range(npc):
            for j in range(3):
                got = landed(p, _chip_of(*chips[(1, 0, 2)[j]]), 1 - c)
                copy(p * 6 + 3 + j, got, got, sibling).wait_recv()
        for j, chip in enumerate(chips):
            got = cw_rows(_chip_of(*chip))
            copy(npc * 6 + j, got, got, (*chip, c)).wait_recv()
        for cp in sent:
            cp.wait_send()
        for cp in locals_:
            cp.wait()

    vm = pl.BlockSpec(memory_space=pltpu.VMEM)
    out_shape = tuple(jax.ShapeDtypeStruct((N_CHIPS * a.shape[0],) + a.shape[1:], a.dtype) for a in arrs) + (
        jax.ShapeDtypeStruct((N_CHIPS * 8, 128), F32),)
    n_rdma = npc * 6 + 3
    return pl.pallas_call(
        body, name="gather_weights", out_shape=out_shape,
        in_specs=[vm] * 4, out_specs=(vm,) * 4,
        scratch_shapes=[pltpu.SemaphoreType.DMA((n_rdma,)), pltpu.SemaphoreType.DMA((n_rdma,)),
                        pltpu.SemaphoreType.DMA((n + 1,))],
        compiler_params=pltpu.CompilerParams(vmem_limit_bytes=VMEM_LIMIT),
    )(win_t, wout, wkv, convw)


def mem_fwd(mem, mem_g, wkv, xk_g):
    def body(mem_ref, g_ref, w_ref, xk_ref, km_ref, vm_ref):
        mem_v = mem_ref[...]
        mn = mem_v * lax.rsqrt(_row_mean(mem_v * mem_v) + EPS) * g_ref[...]
        mkv = _mm(mn, w_ref[...])
        kpre = mkv[:, :XATT_W]
        gm = _group_matrix(XATT_W)
        km = kpre * lax.rsqrt(_seg_mean(kpre * kpre, gm) + EPS) * xk_ref[...]
        km_ref[...] = _head_rows(km, 0.125).astype(km_ref.dtype)
        vm_ref[...] = _head_rows(mkv[:, XATT_W:], 1.0).astype(vm_ref.dtype)

    vm = pl.BlockSpec(memory_space=pltpu.VMEM)
    rows_shape = jax.ShapeDtypeStruct((4 * MEM_LEN, XATT_W), _MXU)
    return pl.pallas_call(
        body, name="mem_fwd", out_shape=(rows_shape, rows_shape), in_specs=[vm] * 4, out_specs=(vm, vm),
    )(mem, mem_g, wkv, xk_g)


def mem_bwd(mem, mem_g, wkv, xk_g, dkm, dvm):
    def body(mem_ref, g_ref, w_ref, xk_ref, dkm_ref, dvm_ref, gw_ref, gg_ref, gxk_ref):
        mem_v = mem_ref[...]
        mh = mem_v * lax.rsqrt(_row_mean(mem_v * mem_v) + EPS)
        mn = mh * g_ref[...]
        mkv = _mm(mn, w_ref[...])
        kpre = mkv[:, :XATT_W]
        gm = _group_matrix(XATT_W)
        rk = lax.rsqrt(_seg_mean(kpre * kpre, gm) + EPS)
        kn = kpre * rk
        dk = jnp.zeros((MEM_LEN, XATT_W), F32)
        dv = jnp.zeros((MEM_LEN, XATT_W), F32)
        for j in range(4):
            mj = _lane_mask(XATT_W, HEAD * j, HEAD * (j + 1))
            dk = dk + dkm_ref[MEM_LEN * j:MEM_LEN * (j + 1), :] * (mj * 0.125)
            dv = dv + dvm_ref[MEM_LEN * j:MEM_LEN * (j + 1), :] * mj
        gxk_ref[...] = _col_sum(dk * kn)
        dkn = dk * xk_ref[...]
        dkpre = rk * (dkn - kn * _seg_mean(dkn * kn, gm))
        dmkv = jnp.concatenate([dkpre, dv], axis=1)
        gw_ref[...] = _mm_tn(mn, dmkv)
        dmn = _mm_nt(dmkv, w_ref[...])
        gg_ref[...] = _col_sum(dmn * mh)

    vm = pl.BlockSpec(memory_space=pltpu.VMEM)
    return pl.pallas_call(
        body, name="mem_bwd",
        out_shape=(jax.ShapeDtypeStruct((D_MODEL, 2 * XATT_W), F32), jax.ShapeDtypeStruct((1, D_MODEL), F32),
                   jax.ShapeDtypeStruct((1, XATT_W), F32)),
        in_specs=[vm] * 6, out_specs=(vm, vm, vm),
    )(mem, mem_g, wkv, xk_g, dkm, dvm)


def layer_fwd(x, tgt, rc, rs1, rs2, ng, win_t, cw, cb, wg, brg, big, lam, qg, kg, xqg, sinks, km, vm, og, wout):
    seq = x.shape[0]
    tm = min(ROW_TILE, seq)
    nt = seq // tm
    nb = tm // BLOCK

    def body(x_ref, t_ref, c_ref, s1_ref, s2_ref, ng_ref, win_ref, cw_ref, cb_ref, wg_ref, brg_ref, big_ref, lam_ref,
             qg_ref, kg_ref, xqg_ref, sink_ref, km_ref, vm_ref, og_ref, wout_ref,
             proj_ref, ya_ref, yb_ref, yc_ref, ycat_ref, xn_ref, dout_ref, pswa_ref, pmem_ref, psink_ref, gates_ref,
             a_ref, loss_ref,
             ext_ref, b_scr, hc_ref, kp_ref, vp_ref, lacc_ref):
        i = pl.program_id(0)

        @pl.when(i == 0)
        def _():
            ext_ref[0:8, :] = jnp.zeros((8, LRU_W), F32)
            hc_ref[...] = jnp.zeros_like(hc_ref)
            kp_ref[...] = jnp.zeros_like(kp_ref)
            vp_ref[...] = jnp.zeros_like(vp_ref)
            lacc_ref[...] = jnp.zeros_like(lacc_ref)

        xv = x_ref[...]
        xn = (xv * lax.rsqrt(_row_mean(xv * xv) + EPS) * ng_ref[...]).astype(_MXU)
        xn_ref[...] = xn.astype(xn_ref.dtype)
        proj_ref[...] = _mm_nt(xn, win_ref[...])

        u = proj_ref[:, C_LRUX:C_LRUX + LRU_W]
        ext_ref[8:8 + tm, :] = u
        xc = cb_ref[...]
        for k in range(CONV_K):
            xc = xc + cw_ref[k:k + 1, :] * ext_ref[pl.ds(5 + k, tm), :]
        ext_ref[0:8, :] = u[tm - 8:tm, :]
        rg, ig, sp, a, sq = _lru_gates(xc, wg_ref, brg_ref[...], big_ref[...], lam_ref[...])
        for k, t in enumerate((xc, rg, ig, sq)):
            gates_ref[:, LRU_W * k:LRU_W * (k + 1)] = t.astype(gates_ref.dtype)
        a_ref[...] = a
        b_scr[...] = sq * (ig * xc)
        row8 = lax.broadcasted_iota(jnp.int32, (8, LRU_W), 0)

        def scan_step(g, carry):
            r0 = pl.multiple_of(g * 8, 8)
            av = a_ref[pl.ds(r0, 8), :]
            bv = b_scr[pl.ds(r0, 8), :]
            for d in (1, 2, 4):
                a_sh = jnp.where(row8 >= d, pltpu.roll(av, d, 0), 1.0)
                b_sh = jnp.where(row8 >= d, pltpu.roll(bv, d, 0), 0.0)
                bv = bv + av * b_sh
                av = av * a_sh
            hv = bv + av * carry
            ya_ref[pl.ds(r0, 8), :] = hv
            return hv[7:8, :]

        hc_ref[0:1, :] = lax.fori_loop(0, tm // 8, scan_step, hc_ref[0:1, :], unroll=True)

        gm128 = _group_matrix(KV_W)
        cv, s1v, s2v = c_ref[...], s1_ref[...], s2_ref[...]

        def head_norm_rope(t, g):
            n = t * lax.rsqrt(_seg_mean(t * t, gm128) + EPS)
            return _rope(n * g, cv, s1v, s2v)

        qs_ = (head_norm_rope(proj_ref[:, C_SQ:C_SQ + 128], qg_ref[...]).astype(_MXU),
               head_norm_rope(proj_ref[:, C_SQ + 128:C_SQ + 256], qg_ref[...]).astype(_MXU))
        kr = head_norm_rope(proj_ref[:, C_SK:C_SK + KV_W], kg_ref[...])
        sv = proj_ref[:, C_SV:C_SV + KV_W]
        ka = _place_kv(jnp.concatenate([kp_ref[...], kr], axis=0), 0.125)
        va = _place_kv(jnp.concatenate([vp_ref[...], sv], axis=0), 1.0)
        kp_ref[...] = kr[tm - BLOCK:tm, :]
        vp_ref[...] = sv[tm - BLOCK:tm, :]
        lane128 = lax.broadcasted_iota(jnp.int32, (1, 128), 1)
        for b in range(nb):
            mask = _swa_mask((i == 0) & (b == 0)) if b == 0 else _swa_mask(False)
            band = slice(BLOCK * b, BLOCK * b + 2 * BLOCK)
            blk = slice(BLOCK * b, BLOCK * (b + 1))
            psink = jnp.zeros((BLOCK, 128), F32)
            for j in range(4):
                p, pk = _swa_probs(qs_[j // 2][blk], ka[j][band], mask, sink_ref[0, j])
                pswa_ref[blk, 2 * BLOCK * j:2 * BLOCK * (j + 1)] = p.astype(pswa_ref.dtype)
                psink = jnp.where(lane128 == j, pk, psink)
            psink_ref[blk, :] = psink
            for h in range(2):
                yb_ref[blk, KV_W * h:KV_W * (h + 1)] = _mm(
                    pswa_ref[blk, 4 * BLOCK * h:4 * BLOCK * (h + 1)],
                    jnp.concatenate([va[2 * h][band], va[2 * h + 1][band]], axis=0))

        gm256 = _group_matrix(XATT_W)
        xq = proj_ref[:, C_XQ:C_XQ + XATT_W]
        qx = xq * lax.rsqrt(_seg_mean(xq * xq, gm256) + EPS) * xqg_ref[...]
        pm = _mem_probs(_mm_nt(qx, km_ref[...]))
        for j in range(4):
            pmem_ref[:, MEM_LEN * j:MEM_LEN * (j + 1)] = pm[j].astype(pmem_ref.dtype)
        yc = _mm(pmem_ref[...], vm_ref[...])
        yc_ref[...] = yc

        def gated(y, g, gate):
            return y * lax.rsqrt(_row_mean(y * y) + EPS) * g * (gate * _sigmoid(gate))

        ogv = og_ref[...]
        za = gated(ya_ref[...], ogv[:, :512], proj_ref[:, C_LRUG:C_LRUG + LRU_W])
        zb = gated(yb_ref[...], ogv[:, 512:768], proj_ref[:, C_SWAG:C_SWAG + SWA_W])
        zc = gated(yc, ogv[:, 768:], proj_ref[:, C_XG:C_XG + XATT_W])
        ycat_ref[:, 0:512] = za.astype(ycat_ref.dtype)
        ycat_ref[:, 512:768] = zb.astype(ycat_ref.dtype)
        ycat_ref[:, 768:1024] = zc.astype(ycat_ref.dtype)
        out = xv + _mm(ycat_ref[...], wout_ref[...])
        err = out - t_ref[...]
        dout_ref[...] = (err * (1.0 / D_MODEL)).astype(dout_ref.dtype)
        lacc_ref[...] = lacc_ref[...] + (0.5 / D_MODEL) * jnp.sum(err * err)

        @pl.when(i == nt - 1)
        def _():
            loss_ref[...] = lacc_ref[...]

    def rows(ncol):
        return pl.BlockSpec((tm, ncol), lambda i: (i, 0))

    in_specs = [rows(D_MODEL), rows(D_MODEL), rows(128), rows(128), rows(128),
                _const_spec((1, D_MODEL)), _const_spec((D_IN, D_MODEL), True), _const_spec((CONV_K, LRU_W)),
                _const_spec((1, LRU_W)), _const_spec((2, 256, 512), True), _const_spec((1, LRU_W)),
                _const_spec((1, LRU_W)), _const_spec((1, LRU_W)), _const_spec((1, 128)), _const_spec((1, 128)),
                _const_spec((1, XATT_W)), pl.BlockSpec(memory_space=pltpu.SMEM),
                _const_spec((4 * MEM_LEN, XATT_W), True), _const_spec((4 * MEM_LEN, XATT_W), True),
                _const_spec((1, D_MODEL)), _const_spec((D_MODEL, D_MODEL), True)]
    out_shape = (jax.ShapeDtypeStruct((seq, D_IN), F32), jax.ShapeDtypeStruct((seq, LRU_W), F32),
                 jax.ShapeDtypeStruct((seq, SWA_W), F32), jax.ShapeDtypeStruct((seq, XATT_W), F32),
                 jax.ShapeDtypeStruct((seq, D_MODEL), _MXU), jax.ShapeDtypeStruct((seq, D_MODEL), _MXU),
                 jax.ShapeDtypeStruct((seq, D_MODEL), _MXU), jax.ShapeDtypeStruct((seq, 4 * 2 * BLOCK), _MXU),
                 jax.ShapeDtypeStruct((seq, 4 * MEM_LEN), _MXU), jax.ShapeDtypeStruct((seq, 128), F32),
                 jax.ShapeDtypeStruct((seq, 4 * LRU_W), _MXU), jax.ShapeDtypeStruct((seq, LRU_W), F32),
                 jax.ShapeDtypeStruct((8, 128), F32))
    out_specs = (rows(D_IN), rows(LRU_W), rows(SWA_W), rows(XATT_W), rows(D_MODEL), rows(D_MODEL), rows(D_MODEL),
                 rows(4 * 2 * BLOCK), rows(4 * MEM_LEN), rows(128), rows(4 * LRU_W), rows(LRU_W),
                 _const_spec((8, 128)))
    scratch = [pltpu.VMEM((tm + 8, LRU_W), F32), pltpu.VMEM((tm, LRU_W), F32),
               pltpu.VMEM((8, LRU_W), F32), pltpu.VMEM((BLOCK, KV_W), F32), pltpu.VMEM((BLOCK, KV_W), F32),
               pltpu.VMEM((8, 128), F32)]
    return pl.pallas_call(
        body, name="layer_fwd", grid=(nt,), out_shape=out_shape, in_specs=in_specs, out_specs=out_specs,
        scratch_shapes=scratch,
        compiler_params=pltpu.CompilerParams(dimension_semantics=("arbitrary",), vmem_limit_bytes=VMEM_LIMIT),
    )(x, tgt, rc, rs1, rs2, ng, win_t, cw, cb, wg, brg, big, lam, qg, kg, xqg, sinks, km, vm, og, wout)


def wgrad_out(ycat, dout):
    seq = ycat.shape[0]

    def body(y_ref, d_ref, o_ref):
        o_ref[...] = _mm_tn(y_ref[...], d_ref[...])

    return pl.pallas_call(
        body, name="wgrad_out", grid=(D_MODEL // 256,), out_shape=jax.ShapeDtypeStruct((D_MODEL, D_MODEL), F32),
        in_specs=[pl.BlockSpec((seq, 256), lambda j: (0, j)), _const_spec((seq, D_MODEL), True)],
        out_specs=pl.BlockSpec((256, D_MODEL), lambda j: (j, 0)),
        compiler_params=pltpu.CompilerParams(dimension_semantics=("arbitrary",), vmem_limit_bytes=VMEM_LIMIT),
    )(ycat, dout)


def wgrad_in_reduce(dproj, xn, bigs, g_small):
    seq = xn.shape[0]
    nblk = D_IN // 256
    nbig = len(bigs)
    stage_at = (0, 1, 6, nblk - 1, nblk - 1)

    def body(d_ref, x_ref, *refs):
        nres = nbig + 1
        ins, o_ref, res = refs[:nres], refs[nres], refs[nres + 1:2 * nres + 1]
        sums, rest, fsem = refs[2 * nres + 1:3 * nres + 1], refs[3 * nres + 1:-1], refs[-1]
        stages = _reduce_protocol(*_split_reduce_refs(ins + sums + rest, nbig, True))
        for at, stage in zip(stage_at, stages):
            pl.when(pl.program_id(0) == at)(stage)

        @pl.when(pl.program_id(0) == nblk - 1)
        def _():
            out = [pltpu.make_async_copy(sums[k], res[k], fsem.at[k]) for k in range(nres)]
            for cp in out:
                cp.start()
            for cp in out:
                cp.wait()

        o_ref[...] = _mm_tn(d_ref[...], x_ref[...])

    red_shape, scratch = _reduce_buffers(bigs, g_small)
    scratch = [pltpu.VMEM(r.shape, r.dtype) for r in red_shape] + scratch + [pltpu.SemaphoreType.DMA((nbig + 1,))]
    vm = pl.BlockSpec(memory_space=pltpu.VMEM)
    hbm = pl.BlockSpec(memory_space=pl.ANY)
    return pl.pallas_call(
        body, name="wgrad_in_reduce", grid=(nblk,),
        out_shape=(jax.ShapeDtypeStruct((D_IN, D_MODEL), F32), *red_shape),
        in_specs=[pl.BlockSpec((seq, 256), lambda j: (0, j)), _const_spec((seq, D_MODEL), True)] + [hbm] * nbig + [vm],
        out_specs=(pl.BlockSpec((256, D_MODEL), lambda j: (j, 0)),) + (hbm,) * len(red_shape),
        scratch_shapes=scratch,
        compiler_params=pltpu.CompilerParams(dimension_semantics=("arbitrary",), vmem_limit_bytes=VMEM_LIMIT),
    )(dproj, xn, *bigs, g_small)


def layer_bwd(x, dout, proj, ya, yb, yc, pswa, pmem, psink, gates, a_all, rc, rs1, rs2, ng, win_t, cw, cb, wg, brg, big, lam, qg, kg, xqg, sinks, km,
              vm, og, wout):
    seq = x.shape[0]
    tm = min(ROW_TILE, seq)
    nt = seq // tm
    nb = tm // BLOCK

    def body(x_ref, dout_ref, proj_ref, ya_ref, yb_ref, yc_ref, pswa_ref, pmem_ref, psink_ref, gates_ref, a_ref,
             c_ref, s1_ref, s2_ref,
             yah_ref, kvh_ref, ch_ref, s1h_ref, s2h_ref,
             ng_ref, win_ref, cw_ref, cb_ref, wg_ref, brg_ref, big_ref, lam_ref, qg_ref, kg_ref, xqg_ref, sink_ref,
             km_ref, vm_ref, og_ref, wout_ref,
             gx_ref, dproj_ref, gwg_ref, dkm_ref, dvm_ref, gng_ref, gog_ref, gcb_ref, gbrg_ref, gbig_ref, glam_ref,
             gcw_ref, gqn_ref, gkn_ref, gxqn_ref, gsink_ref,
             hext_ref, aext_ref, an_scr, dh_scr, g_scr, dxc_ext, gcar_ref, dkcar_ref, dvcar_ref):
        i = pl.program_id(0)
        tile = nt - 1 - i
        first_tile = tile == 0

        @pl.when(i == 0)
        def _():
            for r in (gwg_ref, dkm_ref, dvm_ref, gng_ref, gog_ref, gcb_ref, gbrg_ref, gbig_ref, glam_ref, gcw_ref,
                      gqn_ref, gkn_ref, gxqn_ref, gsink_ref, gcar_ref, dkcar_ref, dvcar_ref):
                r[...] = jnp.zeros_like(r)
            dxc_ext[tm:tm + 8, :] = jnp.zeros((8, LRU_W), F32)
            aext_ref[tm:tm + 8, :] = jnp.zeros((8, LRU_W), F32)

        xv = x_ref[...]
        dov = dout_ref[...]
        dz = _mm_nt(dov, wout_ref[...])
        ogv = og_ref[...]

        def group_bwd(y, gate, g, dzg):
            r = lax.rsqrt(_row_mean(y * y) + EPS)
            n = y * r
            sg = _sigmoid(gate)
            dgate = dzg * (n * g) * (sg * (1.0 + gate * (1.0 - sg)))
            dng = dzg * (gate * sg)
            dn = dng * g
            return r * (dn - n * _row_mean(dn * n)), dgate, _col_sum(dng * n)

        dya, dga, goa = group_bwd(ya_ref[...], proj_ref[:, C_LRUG:C_LRUG + LRU_W], ogv[:, :512], dz[:, :512])
        dyb, dgb, gob = group_bwd(yb_ref[...], proj_ref[:, C_SWAG:C_SWAG + SWA_W], ogv[:, 512:768], dz[:, 512:768])
        dyc, dgc, goc = group_bwd(yc_ref[...], proj_ref[:, C_XG:C_XG + XATT_W], ogv[:, 768:], dz[:, 768:])
        gog_ref[...] += jnp.concatenate([goa, gob, goc], axis=1)
        dproj_ref[:, C_LRUG:C_LRUG + LRU_W] = dga.astype(dproj_ref.dtype)
        dproj_ref[:, C_SWAG:C_SWAG + SWA_W] = dgb.astype(dproj_ref.dtype)
        dproj_ref[:, C_XG:C_XG + XATT_W] = dgc.astype(dproj_ref.dtype)

        gm256 = _group_matrix(XATT_W)
        xq = proj_ref[:, C_XQ:C_XQ + XATT_W]
        rq = lax.rsqrt(_seg_mean(xq * xq, gm256) + EPS)
        qn = xq * rq
        qx = qn * xqg_ref[...]
        qxb = qx.astype(_MXU)
        dycb = dyc.astype(_MXU)
        dp_all = _mm_nt(dycb, vm_ref[...])
        dsm = []
        for j in range(4):
            pj = pmem_ref[:, MEM_LEN * j:MEM_LEN * (j + 1)].astype(F32)
            dp = dp_all[:, MEM_LEN * j:MEM_LEN * (j + 1)]
            dsm.append((pj * (dp - jnp.sum(pj * dp, axis=-1, keepdims=True))).astype(_MXU))
        ds_all = jnp.concatenate(dsm, axis=1)
        dvm_ref[...] += _mm_tn(pmem_ref[...], dycb)
        dkm_ref[...] += _mm_tn(ds_all, qxb)
        dqx = _mm(ds_all, km_ref[...])
        gxqn_ref[...] += _col_sum(dqx * qn)
        dqn = dqx * xqg_ref[...]
        dproj_ref[:, C_XQ:C_XQ + XATT_W] = (rq * (dqn - qn * _seg_mean(dqn * qn, gm256))).astype(dproj_ref.dtype)

        gm128 = _group_matrix(KV_W)
        cv, s1v, s2v = c_ref[...], s1_ref[...], s2_ref[...]

        def head_norm(t):
            r = lax.rsqrt(_seg_mean(t * t, gm128) + EPS)
            return t * r, r

        qn_, qr_ = zip(head_norm(proj_ref[:, C_SQ:C_SQ + 128]), head_norm(proj_ref[:, C_SQ + 128:C_SQ + 256]))
        qrope = [_rope(qn_[h] * qg_ref[...], cv, s1v, s2v).astype(_MXU) for h in range(2)]
        kn, krr = head_norm(proj_ref[:, C_SK:C_SK + KV_W])
        kr = _rope(kn * kg_ref[...], cv, s1v, s2v)
        khn, _ = head_norm(kvh_ref[:, 0:KV_W])
        khr = _rope(khn * kg_ref[...], ch_ref[...], s1h_ref[...], s2h_ref[...])
        ka = _place_kv(jnp.concatenate([khr, kr], axis=0), 0.125)
        va = _place_kv(jnp.concatenate([kvh_ref[:, KV_W:2 * KV_W], proj_ref[:, C_SV:C_SV + KV_W]], axis=0), 1.0)
        lane128 = lax.broadcasted_iota(jnp.int32, (1, 128), 1)
        gsink = jnp.zeros((1, 128), F32)
        dk_band, dv_band, dq_blk = [], [], []
        for b in range(nb):
            band = slice(BLOCK * b, BLOCK * b + 2 * BLOCK)
            blk = slice(BLOCK * b, BLOCK * (b + 1))
            dka, dva, dsb = [], [], []
            deltas = jnp.zeros((BLOCK, 128), F32)
            for j in range(4):
                qh = qrope[j // 2][blk]
                doh = dyb[blk, KV_W * (j // 2):KV_W * (j // 2 + 1)].astype(_MXU)
                pb = pswa_ref[blk, 2 * BLOCK * j:2 * BLOCK * (j + 1)]
                p = pb.astype(F32)
                dp = _mm_nt(doh, va[j][band])
                delta = jnp.sum(p * dp, axis=-1, keepdims=True)
                ds = (p * (dp - delta)).astype(_MXU)
                deltas = jnp.where(lane128 == j, delta, deltas)
                dva.append(_mm_tn(pb, doh))
                dka.append(_mm_tn(ds, qh))
                dsb.append(ds)
            gsink = gsink - _col_sum(psink_ref[blk, :] * deltas)
            dk_band.append(_unplace_kv(dka) * 0.125)
            dv_band.append(_unplace_kv(dva))
            dq_blk.append([_mm(jnp.concatenate(dsb[2 * h:2 * h + 2], axis=1),
                               jnp.concatenate([ka[2 * h][band], ka[2 * h + 1][band]], axis=0)) for h in range(2)])
        gsink_ref[...] += gsink
        dk_rows = [dk_band[b][BLOCK:] + (dk_band[b + 1][:BLOCK] if b + 1 < nb else dkcar_ref[...]) for b in range(nb)]
        dv_rows = [dv_band[b][BLOCK:] + (dv_band[b + 1][:BLOCK] if b + 1 < nb else dvcar_ref[...]) for b in range(nb)]
        dkcar_ref[...] = dk_band[0][:BLOCK]
        dvcar_ref[...] = dv_band[0][:BLOCK]
        dkg = _rope_bwd(jnp.concatenate(dk_rows, axis=0), cv, s1v, s2v)
        gkn = _col_sum(dkg * kn)
        dkn = dkg * kg_ref[...]
        dproj_ref[:, C_SK:C_SK + KV_W] = (krr * (dkn - kn * _seg_mean(dkn * kn, gm128))).astype(dproj_ref.dtype)
        dproj_ref[:, C_SV:C_SV + KV_W] = jnp.concatenate(dv_rows, axis=0).astype(dproj_ref.dtype)
        gqn = jnp.zeros((1, 128), F32)
        for h in range(2):
            dqg = _rope_bwd(jnp.concatenate([dq_blk[b][h] for b in range(nb)], axis=0), cv, s1v, s2v)
            gqn = gqn + _col_sum(dqg * qn_[h])
            dqn_ = dqg * qg_ref[...]
            dproj_ref[:, C_SQ + 128 * h:C_SQ + 128 * (h + 1)] = (
                qr_[h] * (dqn_ - qn_[h] * _seg_mean(dqn_ * qn_[h], gm128))).astype(dproj_ref.dtype)
        gqn_ref[...] += gqn
        gkn_ref[...] += gkn

        u = proj_ref[:, C_LRUX:C_LRUX + LRU_W]
        xc, rg, ig, sq = (gates_ref[:, LRU_W * k:LRU_W * (k + 1)].astype(F32) for k in range(4))
        a = a_ref[...]
        sp = _softplus(-lam_ref[...])
        hext_ref[0:8, :] = jnp.where(first_tile, 0.0, yah_ref[...])
        hext_ref[8:8 + tm, :] = ya_ref[...]
        hprev = hext_ref[pl.ds(7, tm), :]
        aext_ref[0:tm, :] = a
        an_scr[...] = aext_ref[pl.ds(1, tm), :]
        dh_scr[...] = dya
        dh_scr[tm - 1:tm, :] = dh_scr[tm - 1:tm, :] + gcar_ref[0:1, :]
        row8 = lax.broadcasted_iota(jnp.int32, (8, LRU_W), 0)

        def scan_step(gi, carry):
            r0 = pl.multiple_of((tm // 8 - 1 - gi) * 8, 8)
            av = an_scr[pl.ds(r0, 8), :]
            bv = dh_scr[pl.ds(r0, 8), :]
            for d in (1, 2, 4):
                a_sh = jnp.where(row8 < 8 - d, pltpu.roll(av, 8 - d, 0), 1.0)
                b_sh = jnp.where(row8 < 8 - d, pltpu.roll(bv, 8 - d, 0), 0.0)
                bv = bv + av * b_sh
                av = av * a_sh
            gv = bv + av * carry
            g_scr[pl.ds(r0, 8), :] = gv
            return gv[0:1, :]

        g0 = lax.fori_loop(0, tm // 8, scan_step, jnp.zeros((1, LRU_W), F32), unroll=True)
        gcar_ref[0:1, :] = a[0:1, :] * g0
        gv = g_scr[...]
        da = gv * hprev
        dig = gv * sq * xc
        dxc = gv * sq * ig
        dla = da * a - gv * (ig * xc) * ((a * a) / sq)
        drg = dla * ((-LRU_C) * sp)
        glam_ref[...] += _col_sum(dla * rg)
        dpr = drg * rg * (1.0 - rg)
        dpi = dig * ig * (1.0 - ig)
        gbrg_ref[...] += _col_sum(dpr)
        gbig_ref[...] += _col_sum(dpi)
        dpre0 = jnp.concatenate([dpr[:, :256], dpi[:, :256]], axis=1).astype(_MXU)
        dpre1 = jnp.concatenate([dpr[:, 256:], dpi[:, 256:]], axis=1).astype(_MXU)
        gwg_ref[0] += _mm_tn(xc[:, :256], dpre0)
        gwg_ref[1] += _mm_tn(xc[:, 256:], dpre1)
        dxc = dxc + jnp.concatenate([_mm_nt(dpre0, wg_ref[0]), _mm_nt(dpre1, wg_ref[1])], axis=1)
        gcb_ref[...] += _col_sum(dxc)
        dxc_ext[0:tm, :] = dxc
        du = jnp.zeros((tm, LRU_W), F32)
        for k in range(CONV_K):
            later = dxc_ext[pl.ds(3 - k, tm), :]
            gcw_ref[k:k + 1, :] += _col_sum(later * u)
            du = du + cw_ref[k:k + 1, :] * later
        dxc_ext[tm:tm + 8, :] = dxc[0:8, :]
        dproj_ref[:, C_LRUX:C_LRUX + LRU_W] = du.astype(dproj_ref.dtype)

        dxn = _mm(dproj_ref[...], win_ref[...])
        rx = lax.rsqrt(_row_mean(xv * xv) + EPS)
        xh = xv * rx
        gng_ref[...] += _col_sum(dxn * xh)
        dxh = dxn * ng_ref[...]
        gx_ref[...] = dov.astype(F32) + rx * (dxh - xh * _row_mean(dxh * xh))

        @pl.when(i == nt - 1)
        def _():
            glam_ref[...] = glam_ref[...] * (LRU_C * _sigmoid(-lam_ref[...]))

    def rows(ncol, arr_cols_block=0):
        return pl.BlockSpec((tm, ncol), lambda i: (nt - 1 - i, arr_cols_block))

    def halo(nrow, ncol, colblk=0):
        per = tm // nrow
        return pl.BlockSpec((nrow, ncol), lambda i: (jnp.maximum((nt - 1 - i) * per - 1, 0), colblk))

    in_specs = [rows(D_MODEL), rows(D_MODEL), rows(D_IN), rows(LRU_W), rows(SWA_W), rows(XATT_W),
                rows(4 * 2 * BLOCK), rows(4 * MEM_LEN), rows(128), rows(4 * LRU_W), rows(LRU_W),
                rows(128), rows(128), rows(128),
                halo(8, LRU_W), halo(BLOCK, 2 * KV_W, C_SK // (2 * KV_W)),
                halo(BLOCK, 128), halo(BLOCK, 128), halo(BLOCK, 128),
                _const_spec((1, D_MODEL)), _const_spec((D_IN, D_MODEL), True), _const_spec((CONV_K, LRU_W)),
                _const_spec((1, LRU_W)), _const_spec((2, 256, 512), True), _const_spec((1, LRU_W)),
                _const_spec((1, LRU_W)), _const_spec((1, LRU_W)), _const_spec((1, 128)), _const_spec((1, 128)),
                _const_spec((1, XATT_W)), pl.BlockSpec(memory_space=pltpu.SMEM),
                _const_spec((4 * MEM_LEN, XATT_W), True), _const_spec((4 * MEM_LEN, XATT_W), True),
                _const_spec((1, D_MODEL)), _const_spec((D_MODEL, D_MODEL), True)]
    small = [(2, 256, 512), (4 * MEM_LEN, XATT_W), (4 * MEM_LEN, XATT_W), (1, D_MODEL), (1, D_MODEL), (1, LRU_W), (1, LRU_W),
             (1, LRU_W), (1, LRU_W), (CONV_K, LRU_W), (1, 128), (1, 128), (1, XATT_W), (1, 128)]
    out_shape = (jax.ShapeDtypeStruct((seq, D_MODEL), F32), jax.ShapeDtypeStruct((seq, D_IN), _MXU)) + tuple(
        jax.ShapeDtypeStruct(s, F32) for s in small)
    out_specs = (rows(D_MODEL), rows(D_IN)) + tuple(_const_spec(s) for s in small)
    scratch = [pltpu.VMEM((tm + 8, LRU_W), F32), pltpu.VMEM((tm + 8, LRU_W), F32),
               pltpu.VMEM((tm, LRU_W), F32), pltpu.VMEM((tm, LRU_W), F32), pltpu.VMEM((tm, LRU_W), F32),
               pltpu.VMEM((tm + 8, LRU_W), F32),
               pltpu.VMEM((8, LRU_W), F32), pltpu.VMEM((BLOCK, KV_W), F32), pltpu.VMEM((BLOCK, KV_W), F32)]
    return pl.pallas_call(
        body, name="layer_bwd", grid=(nt,), out_shape=out_shape, in_specs=in_specs, out_specs=out_specs,
        scratch_shapes=scratch,
        compiler_params=pltpu.CompilerParams(dimension_semantics=("arbitrary",), vmem_limit_bytes=VMEM_LIMIT),
    )(x, dout, proj, ya, yb, yc, pswa, pmem, psink, gates, a_all, rc, rs1, rs2, ya, proj, rc, rs1, rs2,
      ng, win_t, cw, cb, wg, brg, big, lam, qg, kg, xqg, sinks, km, vm, og, wout)


def _reduce_protocol(big, sm, outs, osm, r1, r1s, wire, r2, r2s, wire2, ps, own, send, recv, lsem):
    nbig = len(big)
    x, y, c = lax.axis_index("x"), lax.axis_index("y"), lax.axis_index("c")
    sibling = (x, y, 1 - c)
    near, far, diag = _partners(x, y, c)
    me, near_id, far_id, diag_id = _chip_of(x, y), _chip_of(*near), _chip_of(*far), _chip_of(*diag)

    def copy(k, src, dst, to):
        return pltpu.make_async_remote_copy(src_ref=src, dst_ref=dst, send_sem=send.at[k], recv_sem=recv.at[k],
                                            device_id=to, device_id_type=MESH)

    def sent(stage):
        cps = []
        for a in range(nbig):
            if stage == 0:
                cps.append(copy(5 * a, big[a].at[:, 1 - c], r1[a], sibling))
            elif stage == 1:
                cps.append(copy(5 * a + 1, wire[a].at[near_id], r2[a].at[0], (*near, c)))
                cps.append(copy(5 * a + 2, wire[a].at[diag_id], r2[a].at[1], (*near, c)))
            elif stage == 2:
                cps.append(copy(5 * a + 3, wire2[a], r2[a].at[2], (*far, c)))
            elif stage == 3:
                cps.append(copy(5 * a + 4, outs[a].at[c], outs[a].at[c], sibling))
        if sm is not None:
            src, dst, to = ((sm.at[1 - c], r1s, sibling), (r1s, r2s.at[0], (*near, c)), (ps, r2s.at[1], (*far, c)),
                            (osm.at[c], osm.at[c], sibling))[stage]
            cps.append(copy(5 * nbig + stage, src, dst, to))
        return cps

    def arrived(k, ref):
        copy(k, ref, ref, sibling).wait_recv()

    def loads():
        return [pltpu.make_async_copy(big[a].at[:, c], own[a], lsem.at[a]) for a in range(nbig)]

    def stage0():
        for cp in sent(0) + loads():
            cp.start()

    def stage1():
        for a in range(nbig):
            loads()[a].wait()
            arrived(5 * a, r1[a])
            for k in range(N_CHIPS):
                r1[a][k] = own[a][k] + r1[a][k]
                wire[a][k] = r1[a][k].astype(wire[a].dtype)
        if sm is not None:
            arrived(5 * nbig, r1s)
            r1s[...] = sm[c] + r1s[...]
        for cp in sent(1):
            cp.start()

    def stage2():
        for a in range(nbig):
            arrived(5 * a + 1, r2[a].at[0])
            arrived(5 * a + 2, r2[a].at[1])
            r1[a][me] = r1[a][me] + r2[a][0].astype(F32)
            wire2[a][...] = (r1[a][far_id] + r2[a][1].astype(F32)).astype(wire2[a].dtype)
        if sm is not None:
            arrived(5 * nbig + 1, r2s.at[0])
            ps[...] = r1s[...] + r2s[0]
        for cp in sent(2):
            cp.start()

    def stage3():
        for a in range(nbig):
            arrived(5 * a + 3, r2[a].at[2])
            outs[a][c] = r1[a][me] + r2[a][2].astype(F32)
        if sm is not None:
            arrived(5 * nbig + 2, r2s.at[1])
            osm[c] = ps[...] + r2s[1]
        for cp in sent(3):
            cp.start()

    def stage4():
        for a in range(nbig):
            arrived(5 * a + 4, outs[a].at[1 - c])
        if sm is not None:
            arrived(5 * nbig + 3, osm.at[1 - c])
        for stage in range(4):
            for cp in sent(stage):
                cp.wait_send()

    return [stage0, stage1, stage2, stage3, stage4]


def _reduce_buffers(bigs, g_small):
    half = [b.shape[2:] for b in bigs]
    sm_half = None if g_small is None else g_small.shape[1:]
    out_shape = [jax.ShapeDtypeStruct((2,) + h, F32) for h in half]
    small = lambda lead: [] if g_small is None else [pltpu.VMEM(lead + sm_half, F32)]
    if g_small is not None:
        out_shape.append(jax.ShapeDtypeStruct(g_small.shape, F32))
    n_sem = 5 * len(bigs) + 4
    scratch = ([pltpu.VMEM((N_CHIPS,) + h, F32) for h in half] + small(())
               + [pltpu.VMEM((N_CHIPS,) + h, _WIRE) for h in half]
               + [pltpu.VMEM((3,) + h, _WIRE) for h in half] + small((2,))
               + [pltpu.VMEM(h, _WIRE) for h in half] + small(())
               + [pltpu.VMEM((N_CHIPS,) + h, F32) for h in half]
               + [pltpu.SemaphoreType.DMA((n_sem,)), pltpu.SemaphoreType.DMA((n_sem,)),
                  pltpu.SemaphoreType.DMA((len(bigs),))])
    return out_shape, scratch


def _split_reduce_refs(refs, nbig, has_small):
    it = iter(refs)
    take = lambda n: [next(it) for _ in range(n)]
    one = lambda: next(it) if has_small else None
    big, sm = take(nbig), one()
    outs, osm = take(nbig), one()
    r1, r1s, wire, r2, r2s, wire2, ps, own = take(nbig), one(), take(nbig), take(nbig), one(), take(nbig), one(), take(nbig)
    send, recv, lsem = take(3)
    return big, sm, outs, osm, r1, r1s, wire, r2, r2s, wire2, ps, own, send, recv, lsem


def reduce_grads(bigs, g_small, name):
    nbig = len(bigs)

    def body(*refs):
        for stage in _reduce_protocol(*_split_reduce_refs(refs, nbig, g_small is not None)):
            stage()

    out_shape, scratch = _reduce_buffers(bigs, g_small)
    vm = pl.BlockSpec(memory_space=pltpu.VMEM)
    hbm = pl.BlockSpec(memory_space=pl.ANY)
    operands = list(bigs) + ([] if g_small is None else [g_small])
    return pl.pallas_call(
        body, name=name, out_shape=tuple(out_shape), in_specs=[hbm] * nbig + [vm] * (g_small is not None),
        out_specs=(vm,) * len(out_shape), scratch_shapes=scratch,
        compiler_params=pltpu.CompilerParams(vmem_limit_bytes=VMEM_LIMIT),
    )(*operands)


def adamw(w, g, m, v, name):
    rows_, cols = w.shape
    tr = max(t for t in range(8, rows_ + 1, 8) if rows_ % t == 0 and t * cols * 4 <= ADAM_BLOCK_BYTES)

    def body(w_ref, g_ref, m_ref, v_ref, d_ref, nm_ref, nv_ref):
        gv = g_ref[...]
        nm = ADAM_B1 * m_ref[...] + (1.0 - ADAM_B1) * gv
        nv = ADAM_B2 * v_ref[...] + (1.0 - ADAM_B2) * (gv * gv)
        m_hat = nm / (1.0 - ADAM_B1 ** ADAM_STEP)
        v_hat = nv / (1.0 - ADAM_B2 ** ADAM_STEP)
        d_ref[...] = (-ADAM_LR) * (m_hat / (jnp.sqrt(v_hat) + ADAM_EPS) + ADAM_WD * w_ref[...])
        nm_ref[...] = nm
        nv_ref[...] = nv

    spec = pl.BlockSpec((tr, cols), lambda i: (i, 0))
    shp = jax.ShapeDtypeStruct(w.shape, F32)
    return pl.pallas_call(
        body, name=name, grid=(rows_ // tr,), out_shape=(shp, shp, shp), in_specs=[spec] * 4, out_specs=(spec,) * 3,
        compiler_params=pltpu.CompilerParams(dimension_semantics=("arbitrary",)),
    )(w, g, m, v)


SMALL = (("norm_g", 1024), ("mem_norm_g", 1024), ("conv_w", 512), ("conv_b", 512), ("w_rg", 32768), ("b_rg", 512),
         ("w_ig", 32768), ("b_ig", 512), ("lru_lambda", 512), ("q_norm_g", 128), ("k_norm_g", 128), ("sinks", 128),
         ("xq_norm_g", 128), ("xk_norm_g", 128), ("out_norm_g", 1024))
SMALL_ROWS = 576


def _pack(parts, rows_):
    flat = jnp.concatenate([p.reshape(-1) for p in parts])
    return jnp.pad(flat, (0, rows_ * 128 - flat.shape[0])).reshape(rows_, 128)


def _pad_to(v, n):
    v = v.reshape(-1)
    return jnp.pad(v, (0, n - v.shape[0]))


def _block_diag_gates(w_rg, w_ig):
    eye = jnp.eye(4, dtype=w_rg.dtype)

    def bd(w4):
        return (w4[:, :, None, :] * eye[:, None, :, None]).reshape(256, 256)

    return jnp.stack([jnp.concatenate([bd(w_rg[4 * h:4 * h + 4]), bd(w_ig[4 * h:4 * h + 4])], axis=1) for h in (0, 1)])


def _diag_blocks(g):
    g6 = g.reshape(2, 4, HEAD, 2, 4, HEAD)
    d = (g6 * jnp.eye(4, dtype=g.dtype)[None, :, None, None, :, None]).sum(axis=4)
    return d[:, :, :, 0].reshape(8, HEAD, HEAD), d[:, :, :, 1].reshape(8, HEAD, HEAD)


def _rope_tables(seq):
    pos = np.arange(seq, dtype=np.float32)
    inv_freq = (np.float32(ROPE_THETA) ** (-(np.arange(0, ROPE_DIM, 2, dtype=np.float32) / np.float32(ROPE_DIM)))
                ).astype(np.float32)
    ang = (pos[:, None] * inv_freq[None, :]).astype(np.float32)
    cos, sin = np.cos(ang).astype(np.float32), np.sin(ang).astype(np.float32)
    z = lambda n: np.zeros((seq, n), np.float32)
    c64 = np.concatenate([cos, cos, np.ones((seq, HEAD - ROPE_DIM), np.float32)], axis=1)
    s1_64 = np.concatenate([-sin, z(HEAD - 8)], axis=1)
    s2_64 = np.concatenate([z(8), sin, z(HEAD - ROPE_DIM)], axis=1)
    return tuple(jnp.asarray(np.concatenate([t, t], axis=1)) for t in (c64, s1_64, s2_64))


def kernel(x, mem, norm_g, mem_norm_g, w_in, conv_w, conv_b, w_rg, b_rg, w_ig, b_ig, lru_lambda, q_norm_g, k_norm_g, sinks, w_mem_kv, xq_norm_g, xk_norm_g, out_norm_g, w_out, loss_target, m_norm_g, m_mem_norm_g, m_w_in, m_conv_w, m_conv_b, m_w_rg, m_b_rg, m_w_ig, m_b_ig, m_lru_lambda, m_q_norm_g, m_k_norm_g, m_sinks, m_w_mem_kv, m_xq_norm_g, m_xk_norm_g, m_out_norm_g, m_w_out, v_norm_g, v_mem_norm_g, v_w_in, v_conv_w, v_conv_b, v_w_rg, v_b_rg, v_w_ig, v_b_ig, v_lru_lambda, v_q_norm_g, v_k_norm_g, v_sinks, v_w_mem_kv, v_xq_norm_g, v_xk_norm_g, v_out_norm_g, v_w_out):
    seq = x.shape[1]
    chip = 2 * lax.axis_index("x") + lax.axis_index("y")
    xs, tgt, mems = x[0], loss_target[0], mem[0]

    win_t_sh = w_in[0].T.astype(_MXU)
    cw_sh = jnp.pad(conv_w[0], ((0, 4), (0, 0)))
    win_t, wout, wkv, cw_all = gather_weights(win_t_sh, w_out[0].astype(_MXU), w_mem_kv[0].astype(_MXU), cw_sh)
    cw = cw_all.reshape(N_CHIPS, 8, 128)[:, :CONV_K].transpose(1, 0, 2).reshape(CONV_K, LRU_W)

    rc, rs1, rs2 = _rope_tables(seq)
    wg = _block_diag_gates(w_rg[0], w_ig[0]).astype(_MXU)
    qg = jnp.tile(q_norm_g, (1, 2))
    kg = jnp.tile(k_norm_g, (1, 2))
    xqg = jnp.tile(xq_norm_g, (1, 4))
    xkg = jnp.tile(xk_norm_g, (1, 4))

    km, vm = mem_fwd(mems, mem_norm_g, wkv, xkg)
    proj, ya, yb, yc, ycat, xn, dout, pswa, pmem, psink, gates, a_all, loss8 = layer_fwd(
        xs, tgt, rc, rs1, rs2, norm_g, win_t, cw, conv_b, wg, b_rg, b_ig, lru_lambda, qg, kg, xqg, sinks, km, vm,
        out_norm_g, wout)
    g_wout = wgrad_out(ycat, dout)
    (gx, dproj, g_wg, dkm, dvm, g_ng, g_og, g_cb, g_brg, g_big, g_lam, g_cw, g_qn, g_kn, g_xqn, g_sink) = layer_bwd(
        xs, dout, proj, ya, yb, yc, pswa, pmem, psink, gates, a_all, rc, rs1, rs2, norm_g, win_t, cw, conv_b, wg, b_rg, b_ig,
        lru_lambda, qg, kg, xqg, sinks, km, vm, out_norm_g, wout)
    g_wkv, g_mng, g_xkn = mem_bwd(mems, mem_norm_g, wkv, xkg, dkm, dvm)

    g_wrg, g_wig = _diag_blocks(g_wg)
    fold = lambda v, n: v.reshape(n, HEAD).sum(axis=0)
    small_g = _pack([g_ng, g_mng, g_cw, g_cb, g_wrg, g_brg, g_wig, g_big, g_lam, _pad_to(fold(g_qn, 2), 128),
                     _pad_to(fold(g_kn, 2), 128), g_sink, _pad_to(fold(g_xqn, 4), 128), _pad_to(fold(g_xkn, 4), 128),
                     g_og, loss8[0:1]], SMALL_ROWS)
    early = [g_wout.reshape(N_CHIPS, 2, D_MODEL // 8, D_MODEL), g_wkv.reshape(N_CHIPS, 2, D_MODEL // 8, 2 * XATT_W)]
    g_win_t, r_out, r_kv, r_small = wgrad_in_reduce(dproj, xn, early, small_g.reshape(2, SMALL_ROWS // 2, 128))
    (r_in,) = reduce_grads([g_win_t.reshape(N_CHIPS, 2, D_IN // 8, D_MODEL)], None, "reduce_w_in")

    flat = r_small.reshape(-1)
    sizes = (1024, 1024, 2048, 512, 32768, 512, 32768, 512, 512, 128, 128, 128, 128, 128, 1024)
    offs = [0]
    for s in sizes:
        offs.append(offs[-1] + s)
    loss = flat[offs[-1]]
    piece = {name: flat[offs[k]:offs[k + 1]] for k, (name, _) in enumerate(SMALL)}
    g_cw_mine = lax.dynamic_slice(piece["conv_w"].reshape(CONV_K, LRU_W), (0, chip * 128), (CONV_K, 128))
    grads = {
        "norm_g": piece["norm_g"].reshape(1, 1024), "mem_norm_g": piece["mem_norm_g"].reshape(1, 1024),
        "w_in": r_in.reshape(D_IN // 4, D_MODEL).T[None], "conv_w": g_cw_mine[None],
        "conv_b": piece["conv_b"].reshape(1, 512), "w_rg": piece["w_rg"].reshape(1, 8, HEAD, HEAD),
        "b_rg": piece["b_rg"].reshape(1, 512), "w_ig": piece["w_ig"].reshape(1, 8, HEAD, HEAD),
        "b_ig": piece["b_ig"].reshape(1, 512), "lru_lambda": piece["lru_lambda"].reshape(1, 512),
        "q_norm_g": piece["q_norm_g"][:HEAD].reshape(1, HEAD), "k_norm_g": piece["k_norm_g"][:HEAD].reshape(1, HEAD),
        "sinks": piece["sinks"][:4].reshape(1, 4), "w_mem_kv": r_kv.reshape(D_MODEL // 4, 2 * XATT_W)[None],
        "xq_norm_g": piece["xq_norm_g"][:HEAD].reshape(1, HEAD), "xk_norm_g": piece["xk_norm_g"][:HEAD].reshape(1, HEAD),
        "out_norm_g": piece["out_norm_g"].reshape(1, 1024), "w_out": r_out.reshape(D_MODEL // 4, D_MODEL)[None],
    }
    weights = dict(norm_g=norm_g, mem_norm_g=mem_norm_g, w_in=w_in, conv_w=conv_w, conv_b=conv_b, w_rg=w_rg, b_rg=b_rg,
                   w_ig=w_ig, b_ig=b_ig, lru_lambda=lru_lambda, q_norm_g=q_norm_g, k_norm_g=k_norm_g, sinks=sinks,
                   w_mem_kv=w_mem_kv, xq_norm_g=xq_norm_g, xk_norm_g=xk_norm_g, out_norm_g=out_norm_g, w_out=w_out)
    ms = dict(norm_g=m_norm_g, mem_norm_g=m_mem_norm_g, w_in=m_w_in, conv_w=m_conv_w, conv_b=m_conv_b, w_rg=m_w_rg,
              b_rg=m_b_rg, w_ig=m_w_ig, b_ig=m_b_ig, lru_lambda=m_lru_lambda, q_norm_g=m_q_norm_g, k_norm_g=m_k_norm_g,
              sinks=m_sinks, w_mem_kv=m_w_mem_kv, xq_norm_g=m_xq_norm_g, xk_norm_g=m_xk_norm_g,
              out_norm_g=m_out_norm_g, w_out=m_w_out)
    vs = dict(norm_g=v_norm_g, mem_norm_g=v_mem_norm_g, w_in=v_w_in, conv_w=v_conv_w, conv_b=v_conv_b, w_rg=v_w_rg,
              b_rg=v_b_rg, w_ig=v_w_ig, b_ig=v_b_ig, lru_lambda=v_lru_lambda, q_norm_g=v_q_norm_g, k_norm_g=v_k_norm_g,
              sinks=v_sinks, w_mem_kv=v_w_mem_kv, xq_norm_g=v_xq_norm_g, xk_norm_g=v_xk_norm_g,
              out_norm_g=v_out_norm_g, w_out=v_w_out)

    delta, new_m, new_v = {}, {}, {}
    d2, m2, v2 = adamw(w_in[0].T, r_in.reshape(D_IN // 4, D_MODEL), m_w_in[0].T, v_w_in[0].T, "adamw_w_in")
    delta["w_in"], new_m["w_in"], new_v["w_in"] = d2.T[None], m2.T[None], v2.T[None]
    for name in ("w_mem_kv", "w_out"):
        shp = weights[name].shape
        d2, m2, v2 = adamw(weights[name][0], grads[name][0], ms[name][0], vs[name][0], "adamw_" + name)
        delta[name], new_m[name], new_v[name] = d2.reshape(shp), m2.reshape(shp), v2.reshape(shp)
    small_names = [n for n, _ in SMALL]
    packs = [_pack([_pad_to(d[n], sz) for n, sz in SMALL], SMALL_ROWS) for d in (weights, grads, ms, vs)]
    d_p, m_p, v_p = adamw(*packs, "adamw_small")
    offs2 = [0]
    for _, sz in SMALL:
        offs2.append(offs2[-1] + sz)
    for out_d, pk in ((delta, d_p), (new_m, m_p), (new_v, v_p)):
        fl = pk.reshape(-1)
        for k, n in enumerate(small_names):
            shp = weights[n].shape
            out_d[n] = fl[offs2[k]:offs2[k] + math.prod(shp)].reshape(shp)

    order = ("norm_g", "mem_norm_g", "w_in", "conv_w", "conv_b", "w_rg", "b_rg", "w_ig", "b_ig", "lru_lambda",
             "q_norm_g", "k_norm_g", "sinks", "w_mem_kv", "xq_norm_g", "xk_norm_g", "out_norm_g", "w_out")
    return (loss, gx[None], *[grads[n] for n in order], *[delta[n] for n in order], *[new_m[n] for n in order],
            *[new_v[n] for n in order])
```

```python
import functools
import math

import jax
import jax.numpy as jnp
import numpy as np
from jax import lax
from jax.experimental import pallas as pl
from jax.experimental.pallas import tpu as pltpu

F32 = jnp.float32
_MXU = jnp.bfloat16
_WIRE = jnp.bfloat16

D_MODEL = 1024
MEM_LEN = 256
HEAD = 64
LRU_W = 512
LRU_BLOCKS = 8
CONV_K = 4
LRU_C = 8.0
SWA_W = 256
KV_W = 128
XATT_W = 256
BLOCK = 128
D_IN = 2304
ROPE_THETA = 500000.0
ROPE_DIM = 16
EPS = 1e-6
NEG_INF = -1e30
C_LRUX, C_LRUG, C_SQ, C_SK, C_SV, C_SWAG, C_XQ, C_XG = 0, 512, 1024, 1280, 1408, 1536, 1792, 2048

ADAM_LR, ADAM_B1, ADAM_B2, ADAM_EPS, ADAM_WD, ADAM_STEP = 0.001, 0.9, 0.999, 1e-08, 0.01, 10

N_CHIPS = 4
ROW_TILE = 256
VMEM_LIMIT = 56 * 1024 * 1024
ADAM_BLOCK_BYTES = 1280 * 1024
GATHER_PIECES = (1, 1, 1)
MESH = pl.DeviceIdType.MESH


def _mm(a, b):
    return jnp.dot(a.astype(_MXU), b.astype(_MXU), preferred_element_type=F32)


def _mm_nt(a, b):
    return lax.dot_general(a.astype(_MXU), b.astype(_MXU), (((1,), (1,)), ((), ())), preferred_element_type=F32)


def _mm_tn(a, b):
    return lax.dot_general(a.astype(_MXU), b.astype(_MXU), (((0,), (0,)), ((), ())), preferred_element_type=F32)


def _group_matrix(width):
    r = lax.shift_right_logical(lax.broadcasted_iota(jnp.int32, (width, width), 0), 6)
    c = lax.shift_right_logical(lax.broadcasted_iota(jnp.int32, (width, width), 1), 6)
    return (r == c).astype(_MXU)


def _seg_mean(x, gm):
    return jnp.dot(x.astype(_MXU), gm, preferred_element_type=F32) * (1.0 / HEAD)


def _row_mean(x):
    return jnp.mean(x, axis=-1, keepdims=True)


def _col_sum(x):
    return jnp.sum(x, axis=0, keepdims=True)


def _sigmoid(x):
    return jax.nn.sigmoid(x)


def _softplus(z):
    e = jnp.exp(-jnp.abs(z))
    u = 1.0 + e
    log1p_e = jnp.where(u == 1.0, e, jnp.log(u) * (e / (u - 1.0)))
    return jnp.maximum(z, 0.0) + log1p_e


def _rope(t, c, s1, s2):
    return t * c + pltpu.roll(t, 120, 1) * s1 + pltpu.roll(t, 8, 1) * s2


def _rope_bwd(d, c, s1, s2):
    return d * c + pltpu.roll(d * s1, 8, 1) + pltpu.roll(d * s2, 120, 1)


def _lane_mask(width, lo, hi):
    lane = lax.broadcasted_iota(jnp.int32, (1, width), 1)
    return ((lane >= lo) & (lane < hi)).astype(F32)


def _swa_mask(first_block):
    qi = lax.broadcasted_iota(jnp.int32, (BLOCK, 2 * BLOCK), 0)
    kj = lax.broadcasted_iota(jnp.int32, (BLOCK, 2 * BLOCK), 1)
    rel = qi + BLOCK - kj
    ok = (rel >= 0) & (rel < BLOCK)
    return ok & (jnp.logical_not(first_block) | (kj >= BLOCK))


def _place_kv(t, scale):
    lo = t * (_lane_mask(KV_W, 0, HEAD) * scale)
    hi = t * (_lane_mask(KV_W, HEAD, KV_W) * scale)
    return [a.astype(_MXU) for a in (lo, pltpu.roll(lo, HEAD, 1), pltpu.roll(hi, HEAD, 1), hi)]


def _unplace_kv(d):
    return (_lane_mask(KV_W, 0, HEAD) * (d[0] + pltpu.roll(d[1], HEAD, 1))
            + _lane_mask(KV_W, HEAD, KV_W) * (d[3] + pltpu.roll(d[2], HEAD, 1)))


def _swa_probs(qh, ka, mask, sink):
    s = _mm_nt(qh, ka)
    s = jnp.where(mask, s, NEG_INF)
    m = jnp.maximum(jnp.max(s, axis=-1, keepdims=True), sink)
    p = jnp.exp(s - m)
    esink = jnp.exp(sink - m)
    inv = 1.0 / (jnp.sum(p, axis=-1, keepdims=True) + esink)
    return p * inv, esink * inv


def _mem_probs(s_all):
    out = []
    for j in range(4):
        s = s_all[:, MEM_LEN * j:MEM_LEN * (j + 1)]
        p = jnp.exp(s - jnp.max(s, axis=-1, keepdims=True))
        out.append(p * (1.0 / jnp.sum(p, axis=-1, keepdims=True)))
    return out


def _head_rows(t, scale):
    return jnp.concatenate([t * (_lane_mask(XATT_W, HEAD * j, HEAD * (j + 1)) * scale) for j in range(4)], axis=0)


def _lru_gates(xc, wg_ref, brg, big, lam):
    p0 = _mm(xc[:, :256], wg_ref[0])
    p1 = _mm(xc[:, 256:], wg_ref[1])
    rg = _sigmoid(jnp.concatenate([p0[:, :256], p1[:, :256]], axis=1) + brg)
    ig = _sigmoid(jnp.concatenate([p0[:, 256:], p1[:, 256:]], axis=1) + big)
    sp = _softplus(-lam)
    la = (-LRU_C) * rg * sp
    a = jnp.exp(la)
    th = jnp.tanh(la)
    one_minus_a2 = (-2.0 * th) / (1.0 - th)
    return rg, ig, sp, a, jnp.sqrt(one_minus_a2)


def _const_spec(shape, single=False):
    zeros = (0,) * len(shape)
    if single:
        return pl.BlockSpec(shape, lambda i: zeros, pipeline_mode=pl.Buffered(1))
    return pl.BlockSpec(shape, lambda i: zeros)


def _chip_of(x, y):
    return 2 * x + y


def _partners(x, y, c):
    north = c == 1
    near = (jnp.where(north, 1 - x, x), jnp.where(north, y, 1 - y))
    far = (jnp.where(north, x, 1 - x), jnp.where(north, 1 - y, y))
    return near, far, (1 - x, 1 - y)


def gather_weights(win_t, wout, wkv, convw):
    arrs = (win_t, wout, wkv)
    n = len(arrs)
    pieces = []
    for a, arr in enumerate(arrs):
        half = arr.shape[0] // 2
        step = half // GATHER_PIECES[a]
        pieces += [(a, off, step) for off in range(0, half, step)]
    npc = len(pieces)

    def body(a0, a1, a2, cw, o0, o1, o2, ocw, send, recv, lsem):
        ins, outs = (a0, a1, a2), (o0, o1, o2)
        x, y, c = lax.axis_index("x"), lax.axis_index("y"), lax.axis_index("c")
        sibling = (x, y, 1 - c)
        near, far, diag = _partners(x, y, c)
        chips = [near, far, diag]
        me = _chip_of(x, y)

        def landed(p, chip, half):
            a, off, rows_ = pieces[p]
            r = ins[a].shape[0]
            return outs[a].at[pl.ds(pl.multiple_of(chip * r + half * (r // 2) + off, 16), rows_)]

        def mine(p):
            a, off, rows_ = pieces[p]
            return ins[a].at[pl.ds(pl.multiple_of(c * (ins[a].shape[0] // 2) + off, 16), rows_)]

        def copy(k, src, dst, to):
            return pltpu.make_async_remote_copy(src_ref=src, dst_ref=dst, send_sem=send.at[k], recv_sem=recv.at[k],
                                                device_id=to, device_id_type=MESH)

        def cw_rows(chip):
            return ocw.at[pl.ds(pl.multiple_of(chip * 8, 8), 8)]

        locals_ = []
        for a in range(n):
            r = ins[a].shape[0]
            locals_.append(pltpu.make_async_copy(ins[a], outs[a].at[pl.ds(pl.multiple_of(me * r, 16), r)], lsem.at[a]))
        locals_.append(pltpu.make_async_copy(cw, cw_rows(me), lsem.at[n]))
        for cp in locals_:
            cp.start()

        sent = []
        for p in range(npc):
            for j in range(2):
                sent.append(copy(p * 6 + j, mine(p), landed(p, me, c), (*chips[j], c)))
        for j, chip in enumerate(chips):
            sent.append(copy(npc * 6 + j, cw, cw_rows(me), (*chip, c)))
        for cp in sent:
            cp.start()
        for p in range(npc):
            for j in range(3):
                got = landed(p, _chip_of(*chips[j]), c)
                copy(p * 6 + j, got, got, sibling).wait_recv()
                if j == 0:
                    sent.append(copy(p * 6 + 2, got, got, (*far, c)))
                    sent[-1].start()
                sent.append(copy(p * 6 + 3 + j, got, got, sibling))
                sent[-1].start()
        for p in range(npc):
            for j in range(3):
                got = landed(p, _chip_of(*chips[(1, 0, 2)[j]]), 1 - c)
                copy(p * 6 + 3 + j, got, got, sibling).wait_recv()
        for j, chip in enumerate(chips):
            got = cw_rows(_chip_of(*chip))
            copy(npc * 6 + j, got, got, (*chip, c)).wait_recv()
        for cp in sent:
            cp.wait_send()
        for cp in locals_:
            cp.wait()

    vm = pl.BlockSpec(memory_space=pltpu.VMEM)
    out_shape = tuple(jax.ShapeDtypeStruct((N_CHIPS * a.shape[0],) + a.shape[1:], a.dtype) for a in arrs) + (
        jax.ShapeDtypeStruct((N_CHIPS * 8, 128), F32),)
    n_rdma = npc * 6 + 3
    return pl.pallas_call(
        body, name="gather_weights", out_shape=out_shape,
        in_specs=[vm] * 4, out_specs=(vm,) * 4,
        scratch_shapes=[pltpu.SemaphoreType.DMA((n_rdma,)), pltpu.SemaphoreType.DMA((n_rdma,)),
                        pltpu.SemaphoreType.DMA((n + 1,))],
        compiler_params=pltpu.CompilerParams(vmem_limit_bytes=VMEM_LIMIT),
    )(win_t, wout, wkv, convw)


def mem_fwd(mem, mem_g, wkv, xk_g):
    def body(mem_ref, g_ref, w_ref, xk_ref, km_ref, vm_ref):
        mem_v = mem_ref[...]
        mn = mem_v * lax.rsqrt(_row_mean(mem_v * mem_v) + EPS) * g_ref[...]
        mkv = _mm(mn, w_ref[...])
        kpre = mkv[:, :XATT_W]
        gm = _group_matrix(XATT_W)
        km = kpre * lax.rsqrt(_seg_mean(kpre * kpre, gm) + EPS) * xk_ref[...]
        km_ref[...] = _head_rows(km, 0.125).astype(km_ref.dtype)
        vm_ref[...] = _head_rows(mkv[:, XATT_W:], 1.0).astype(vm_ref.dtype)

    vm = pl.BlockSpec(memory_space=pltpu.VMEM)
    rows_shape = jax.ShapeDtypeStruct((4 * MEM_LEN, XATT_W), _MXU)
    return pl.pallas_call(
        body, name="mem_fwd", out_shape=(rows_shape, rows_shape), in_specs=[vm] * 4, out_specs=(vm, vm),
    )(mem, mem_g, wkv, xk_g)


def mem_bwd(mem, mem_g, wkv, xk_g, dkm, dvm):
    def body(mem_ref, g_ref, w_ref, xk_ref, dkm_ref, dvm_ref, gw_ref, gg_ref, gxk_ref):
        mem_v = mem_ref[...]
        mh = mem_v * lax.rsqrt(_row_mean(mem_v * mem_v) + EPS)
        mn = mh * g_ref[...]
        mkv = _mm(mn, w_ref[...])
        kpre = mkv[:, :XATT_W]
        gm = _group_matrix(XATT_W)
        rk = lax.rsqrt(_seg_mean(kpre * kpre, gm) + EPS)
        kn = kpre * rk
        dk = jnp.zeros((MEM_LEN, XATT_W), F32)
        dv = jnp.zeros((MEM_LEN, XATT_W), F32)
        for j in range(4):
            mj = _lane_mask(XATT_W, HEAD * j, HEAD * (j + 1))
            dk = dk + dkm_ref[MEM_LEN * j:MEM_LEN * (j + 1), :] * (mj * 0.125)
            dv = dv + dvm_ref[MEM_LEN * j:MEM_LEN * (j + 1), :] * mj
        gxk_ref[...] = _col_sum(dk * kn)
        dkn = dk * xk_ref[...]
        dkpre = rk * (dkn - kn * _seg_mean(dkn * kn, gm))
        dmkv = jnp.concatenate([dkpre, dv], axis=1)
        gw_ref[...] = _mm_tn(mn, dmkv)
        dmn = _mm_nt(dmkv, w_ref[...])
        gg_ref[...] = _col_sum(dmn * mh)

    vm = pl.BlockSpec(memory_space=pltpu.VMEM)
    return pl.pallas_call(
        body, name="mem_bwd",
        out_shape=(jax.ShapeDtypeStruct((D_MODEL, 2 * XATT_W), F32), jax.ShapeDtypeStruct((1, D_MODEL), F32),
                   jax.ShapeDtypeStruct((1, XATT_W), F32)),
        in_specs=[vm] * 6, out_specs=(vm, vm, vm),
    )(mem, mem_g, wkv, xk_g, dkm, dvm)


def layer_fwd(x, tgt, rc, rs1, rs2, ng, win_t, cw, cb, wg, brg, big, lam, qg, kg, xqg, sinks, km, vm, og, wout):
    seq = x.shape[0]
    tm = min(ROW_TILE, seq)
    nt = seq // tm
    nb = tm // BLOCK

    def body(x_ref, t_ref, c_ref, s1_ref, s2_ref, ng_ref, win_ref, cw_ref, cb_ref, wg_ref, brg_ref, big_ref, lam_ref,
             qg_ref, kg_ref, xqg_ref, sink_ref, km_ref, vm_ref, og_ref, wout_ref,
             proj_ref, ya_ref, yb_ref, yc_ref, ycat_ref, xn_ref, dout_ref, pswa_ref, pmem_ref, psink_ref, gates_ref,
             a_ref, loss_ref,
             ext_ref, b_scr, hc_ref, kp_ref, vp_ref, lacc_ref):
        i = pl.program_id(0)

        @pl.when(i == 0)
        def _():
            ext_ref[0:8, :] = jnp.zeros((8, LRU_W), F32)
            hc_ref[...] = jnp.zeros_like(hc_ref)
            kp_ref[...] = jnp.zeros_like(kp_ref)
            vp_ref[...] = jnp.zeros_like(vp_ref)
            lacc_ref[...] = jnp.zeros_like(lacc_ref)

        xv = x_ref[...]
        xn = (xv * lax.rsqrt(_row_mean(xv * xv) + EPS) * ng_ref[...]).astype(_MXU)
        xn_ref[...] = xn.astype(xn_ref.dtype)
        proj_ref[...] = _mm_nt(xn, win_ref[...])

        u = proj_ref[:, C_LRUX:C_LRUX + LRU_W]
        ext_ref[8:8 + tm, :] = u
        xc = cb_ref[...]
        for k in range(CONV_K):
            xc = xc + cw_ref[k:k + 1, :] * ext_ref[pl.ds(5 + k, tm), :]
        ext_ref[0:8, :] = u[tm - 8:tm, :]
        rg, ig, sp, a, sq = _lru_gates(xc, wg_ref, brg_ref[...], big_ref[...], lam_ref[...])
        for k, t in enumerate((xc, rg, ig, sq)):
            gates_ref[:, LRU_W * k:LRU_W * (k + 1)] = t.astype(gates_ref.dtype)
        a_ref[...] = a
        b_scr[...] = sq * (ig * xc)
        row8 = lax.broadcasted_iota(jnp.int32, (8, LRU_W), 0)

        def scan_step(g, carry):
            r0 = pl.multiple_of(g * 8, 8)
            av = a_ref[pl.ds(r0, 8), :]
            bv = b_scr[pl.ds(r0, 8), :]
            for d in (1, 2, 4):
                a_sh = jnp.where(row8 >= d, pltpu.roll(av, d, 0), 1.0)
                b_sh = jnp.where(row8 >= d, pltpu.roll(bv, d, 0), 0.0)
                bv = bv + av * b_sh
                av = av * a_sh
            hv = bv + av * carry
            ya_ref[pl.ds(r0, 8), :] = hv
            return hv[7:8, :]

        hc_ref[0:1, :] = lax.fori_loop(0, tm // 8, scan_step, hc_ref[0:1, :], unroll=True)

        gm128 = _group_matrix(KV_W)
        cv, s1v, s2v = c_ref[...], s1_ref[...], s2_ref[...]

        def head_norm_rope(t, g):
            n = t * lax.rsqrt(_seg_mean(t * t, gm128) + EPS)
            return _rope(n * g, cv, s1v, s2v)

        qs_ = (head_norm_rope(proj_ref[:, C_SQ:C_SQ + 128], qg_ref[...]).astype(_MXU),
               head_norm_rope(proj_ref[:, C_SQ + 128:C_SQ + 256], qg_ref[...]).astype(_MXU))
        kr = head_norm_rope(proj_ref[:, C_SK:C_SK + KV_W], kg_ref[...])
        sv = proj_ref[:, C_SV:C_SV + KV_W]
        ka = _place_kv(jnp.concatenate([kp_ref[...], kr], axis=0), 0.125)
        va = _place_kv(jnp.concatenate([vp_ref[...], sv], axis=0), 1.0)
        kp_ref[...] = kr[tm - BLOCK:tm, :]
        vp_ref[...] = sv[tm - BLOCK:tm, :]
        lane128 = lax.broadcasted_iota(jnp.int32, (1, 128), 1)
        for b in range(nb):
            mask = _swa_mask((i == 0) & (b == 0)) if b == 0 else _swa_mask(False)
            band = slice(BLOCK * b, BLOCK * b + 2 * BLOCK)
            blk = slice(BLOCK * b, BLOCK * (b + 1))
            psink = jnp.zeros((BLOCK, 128), F32)
            for j in range(4):
                p, pk = _swa_probs(qs_[j // 2][blk], ka[j][band], mask, sink_ref[0, j])
                pswa_ref[blk, 2 * BLOCK * j:2 * BLOCK * (j + 1)] = p.astype(pswa_ref.dtype)
                psink = jnp.where(lane128 == j, pk, psink)
            psink_ref[blk, :] = psink
            for h in range(2):
                yb_ref[blk, KV_W * h:KV_W * (h + 1)] = _mm(
                    pswa_ref[blk, 4 * BLOCK * h:4 * BLOCK * (h + 1)],
                    jnp.concatenate([va[2 * h][band], va[2 * h + 1][band]], axis=0))

        gm256 = _group_matrix(XATT_W)
        xq = proj_ref[:, C_XQ:C_XQ + XATT_W]
        qx = xq * lax.rsqrt(_seg_mean(xq * xq, gm256) + EPS) * xqg_ref[...]
        pm = _mem_probs(_mm_nt(qx, km_ref[...]))
        for j in range(4):
            pmem_ref[:, MEM_LEN * j:MEM_LEN * (j + 1)] = pm[j].astype(pmem_ref.dtype)
        yc = _mm(pmem_ref[...], vm_ref[...])
        yc_ref[...] = yc

        def gated(y, g, gate):
            return y * lax.rsqrt(_row_mean(y * y) + EPS) * g * (gate * _sigmoid(gate))

        ogv = og_ref[...]
        za = gated(ya_ref[...], ogv[:, :512], proj_ref[:, C_LRUG:C_LRUG + LRU_W])
        zb = gated(yb_ref[...], ogv[:, 512:768], proj_ref[:, C_SWAG:C_SWAG + SWA_W])
        zc = gated(yc, ogv[:, 768:], proj_ref[:, C_XG:C_XG + XATT_W])
        ycat_ref[:, 0:512] = za.astype(ycat_ref.dtype)
        ycat_ref[:, 512:768] = zb.astype(ycat_ref.dtype)
        ycat_ref[:, 768:1024] = zc.astype(ycat_ref.dtype)
        out = xv + _mm(ycat_ref[...], wout_ref[...])
        err = out - t_ref[...]
        dout_ref[...] = (err * (1.0 / D_MODEL)).astype(dout_ref.dtype)
        lacc_ref[...] = lacc_ref[...] + (0.5 / D_MODEL) * jnp.sum(err * err)

        @pl.when(i == nt - 1)
        def _():
            loss_ref[...] = lacc_ref[...]

    def rows(ncol):
        return pl.BlockSpec((tm, ncol), lambda i: (i, 0))

    in_specs = [rows(D_MODEL), rows(D_MODEL), rows(128), rows(128), rows(128),
                _const_spec((1, D_MODEL)), _const_spec((D_IN, D_MODEL), True), _const_spec((CONV_K, LRU_W)),
                _const_spec((1, LRU_W)), _const_spec((2, 256, 512), True), _const_spec((1, LRU_W)),
                _const_spec((1, LRU_W)), _const_spec((1, LRU_W)), _const_spec((1, 128)), _const_spec((1, 128)),
                _const_spec((1, XATT_W)), pl.BlockSpec(memory_space=pltpu.SMEM),
                _const_spec((4 * MEM_LEN, XATT_W), True), _const_spec((4 * MEM_LEN, XATT_W), True),
                _const_spec((1, D_MODEL)), _const_spec((D_MODEL, D_MODEL), True)]
    out_shape = (jax.ShapeDtypeStruct((seq, D_IN), F32), jax.ShapeDtypeStruct((seq, LRU_W), F32),
                 jax.ShapeDtypeStruct((seq, SWA_W), F32), jax.ShapeDtypeStruct((seq, XATT_W), F32),
                 jax.ShapeDtypeStruct((seq, D_MODEL), _MXU), jax.ShapeDtypeStruct((seq, D_MODEL), _MXU),
                 jax.ShapeDtypeStruct((seq, D_MODEL), _MXU), jax.ShapeDtypeStruct((seq, 4 * 2 * BLOCK), _MXU),
                 jax.ShapeDtypeStruct((seq, 4 * MEM_LEN), _MXU), jax.ShapeDtypeStruct((seq, 128), F32),
                 jax.ShapeDtypeStruct((seq, 4 * LRU_W), _MXU), jax.ShapeDtypeStruct((seq, LRU_W), F32),
                 jax.ShapeDtypeStruct((8, 128), F32))
    out_specs = (rows(D_IN), rows(LRU_W), rows(SWA_W), rows(XATT_W), rows(D_MODEL), rows(D_MODEL), rows(D_MODEL),
                 rows(4 * 2 * BLOCK), rows(4 * MEM_LEN), rows(128), rows(4 * LRU_W), rows(LRU_W),
                 _const_spec((8, 128)))
    scratch = [pltpu.VMEM((tm + 8, LRU_W), F32), pltpu.VMEM((tm, LRU_W), F32),
               pltpu.VMEM((8, LRU_W), F32), pltpu.VMEM((BLOCK, KV_W), F32), pltpu.VMEM((BLOCK, KV_W), F32),
               pltpu.VMEM((8, 128), F32)]
    return pl.pallas_call(
        body, name="layer_fwd", grid=(nt,), out_shape=out_shape, in_specs=in_specs, out_specs=out_specs,
        scratch_shapes=scratch,
        compiler_params=pltpu.CompilerParams(dimension_semantics=("arbitrary",), vmem_limit_bytes=VMEM_LIMIT),
    )(x, tgt, rc, rs1, rs2, ng, win_t, cw, cb, wg, brg, big, lam, qg, kg, xqg, sinks, km, vm, og, wout)


def wgrad_out(ycat, dout):
    seq = ycat.shape[0]

    def body(y_ref, d_ref, o_ref):
        o_ref[...] = _mm_tn(y_ref[...], d_ref[...])

    return pl.pallas_call(
        body, name="wgrad_out", grid=(D_MODEL // 256,), out_shape=jax.ShapeDtypeStruct((D_MODEL, D_MODEL), F32),
        in_specs=[pl.BlockSpec((seq, 256), lambda j: (0, j)), _const_spec((seq, D_MODEL), True)],
        out_specs=pl.BlockSpec((256, D_MODEL), lambda j: (j, 0)),
        compiler_params=pltpu.CompilerParams(dimension_semantics=("arbitrary",), vmem_limit_bytes=VMEM_LIMIT),
    )(ycat, dout)


def wgrad_in_reduce(dproj, xn, bigs, g_small):
    seq = xn.shape[0]
    nblk = D_IN // 256
    nbig = len(bigs)
    stage_at = (0, 1, 4, 7, nblk - 1)

    def body(d_ref, x_ref, *refs):
        nres = nbig + 1
        _hosted_reduce(pl.program_id(0), stage_at, refs[:nres], refs[nres + 1:2 * nres + 1], refs[2 * nres + 1:], True)
        refs[nres][...] = _mm_tn(d_ref[...], x_ref[...])

    red_shape, scratch = _hosted_reduce_shapes(bigs, g_small)
    vm = pl.BlockSpec(memory_space=pltpu.VMEM)
    hbm = pl.BlockSpec(memory_space=pl.ANY)
    return pl.pallas_call(
        body, name="wgrad_in_reduce", grid=(nblk,),
        out_shape=(jax.ShapeDtypeStruct((D_IN, D_MODEL), F32), *red_shape),
        in_specs=[pl.BlockSpec((seq, 256), lambda j: (0, j)), _const_spec((seq, D_MODEL), True)] + [hbm] * nbig + [vm],
        out_specs=(pl.BlockSpec((256, D_MODEL), lambda j: (j, 0)),) + (hbm,) * len(red_shape),
        scratch_shapes=scratch,
        compiler_params=pltpu.CompilerParams(dimension_semantics=("arbitrary",), vmem_limit_bytes=VMEM_LIMIT),
    )(dproj, xn, *bigs, g_small)


def layer_bwd(x, dout, proj, ya, yb, yc, pswa, pmem, psink, gates, a_all, rc, rs1, rs2, ng, win_t, cw, cb, wg, brg, big, lam, qg, kg, xqg, sinks, km,
              vm, og, wout, g_early):
    seq = x.shape[0]
    tm = min(ROW_TILE, seq)
    nt = seq // tm
    nb = tm // BLOCK
    stage_at = tuple(min(s, nt - 1) for s in (0, 2, 6, 10, 12))

    def body(x_ref, dout_ref, proj_ref, ya_ref, yb_ref, yc_ref, pswa_ref, pmem_ref, psink_ref, gates_ref, a_ref,
             c_ref, s1_ref, s2_ref,
             yah_ref, kvh_ref, ch_ref, s1h_ref, s2h_ref,
             ng_ref, win_ref, cw_ref, cb_ref, wg_ref, brg_ref, big_ref, lam_ref, qg_ref, kg_ref, xqg_ref, sink_ref,
             km_ref, vm_ref, og_ref, wout_ref, early_ref,
             gx_ref, dproj_ref, gwg_ref, dkm_ref, dvm_ref, gng_ref, gog_ref, gcb_ref, gbrg_ref, gbig_ref, glam_ref,
             gcw_ref, gqn_ref, gkn_ref, gxqn_ref, gsink_ref, early_sum_ref,
             hext_ref, aext_ref, an_scr, dh_scr, g_scr, dxc_ext, gcar_ref, dkcar_ref, dvcar_ref, *reduce_scratch):
        i = pl.program_id(0)
        tile = nt - 1 - i
        first_tile = tile == 0

        @pl.when(i == 0)
        def _():
            for r in (gwg_ref, dkm_ref, dvm_ref, gng_ref, gog_ref, gcb_ref, gbrg_ref, gbig_ref, glam_ref, gcw_ref,
                      gqn_ref, gkn_ref, gxqn_ref, gsink_ref, gcar_ref, dkcar_ref, dvcar_ref):
                r[...] = jnp.zeros_like(r)
            dxc_ext[tm:tm + 8, :] = jnp.zeros((8, LRU_W), F32)
            aext_ref[tm:tm + 8, :] = jnp.zeros((8, LRU_W), F32)

        xv = x_ref[...]
        dov = dout_ref[...]
        dz = _mm_nt(dov, wout_ref[...])
        ogv = og_ref[...]

        def group_bwd(y, gate, g, dzg):
            r = lax.rsqrt(_row_mean(y * y) + EPS)
            n = y * r
            sg = _sigmoid(gate)
            dgate = dzg * (n * g) * (sg * (1.0 + gate * (1.0 - sg)))
            dng = dzg * (gate * sg)
            dn = dng * g
            return r * (dn - n * _row_mean(dn * n)), dgate, _col_sum(dng * n)

        dya, dga, goa = group_bwd(ya_ref[...], proj_ref[:, C_LRUG:C_LRUG + LRU_W], ogv[:, :512], dz[:, :512])
        dyb, dgb, gob = group_bwd(yb_ref[...], proj_ref[:, C_SWAG:C_SWAG + SWA_W], ogv[:, 512:768], dz[:, 512:768])
        dyc, dgc, goc = group_bwd(yc_ref[...], proj_ref[:, C_XG:C_XG + XATT_W], ogv[:, 768:], dz[:, 768:])
        gog_ref[...] += jnp.concatenate([goa, gob, goc], axis=1)
        dproj_ref[:, C_LRUG:C_LRUG + LRU_W] = dga.astype(dproj_ref.dtype)
        dproj_ref[:, C_SWAG:C_SWAG + SWA_W] = dgb.astype(dproj_ref.dtype)
        dproj_ref[:, C_XG:C_XG + XATT_W] = dgc.astype(dproj_ref.dtype)

        gm256 = _group_matrix(XATT_W)
        xq = proj_ref[:, C_XQ:C_XQ + XATT_W]
        rq = lax.rsqrt(_seg_mean(xq * xq, gm256) + EPS)
        qn = xq * rq
        qx = qn * xqg_ref[...]
        qxb = qx.astype(_MXU)
        dycb = dyc.astype(_MXU)
        dp_all = _mm_nt(dycb, vm_ref[...])
        dsm = []
        for j in range(4):
            pj = pmem_ref[:, MEM_LEN * j:MEM_LEN * (j + 1)].astype(F32)
            dp = dp_all[:, MEM_LEN * j:MEM_LEN * (j + 1)]
            dsm.append((pj * (dp - jnp.sum(pj * dp, axis=-1, keepdims=True))).astype(_MXU))
        ds_all = jnp.concatenate(dsm, axis=1)
        dvm_ref[...] += _mm_tn(pmem_ref[...], dycb)
        dkm_ref[...] += _mm_tn(ds_all, qxb)
        dqx = _mm(ds_all, km_ref[...])
        gxqn_ref[...] += _col_sum(dqx * qn)
        dqn = dqx * xqg_ref[...]
        dproj_ref[:, C_XQ:C_XQ + XATT_W] = (rq * (dqn - qn * _seg_mean(dqn * qn, gm256))).astype(dproj_ref.dtype)

        gm128 = _group_matrix(KV_W)
        cv, s1v, s2v = c_ref[...], s1_ref[...], s2_ref[...]

        def head_norm(t):
            r = lax.rsqrt(_seg_mean(t * t, gm128) + EPS)
            return t * r, r

        qn_, qr_ = zip(head_norm(proj_ref[:, C_SQ:C_SQ + 128]), head_norm(proj_ref[:, C_SQ + 128:C_SQ + 256]))
        qrope = [_rope(qn_[h] * qg_ref[...], cv, s1v, s2v).astype(_MXU) for h in range(2)]
        kn, krr = head_norm(proj_ref[:, C_SK:C_SK + KV_W])
        kr = _rope(kn * kg_ref[...], cv, s1v, s2v)
        khn, _ = head_norm(kvh_ref[:, 0:KV_W])
        khr = _rope(khn * kg_ref[...], ch_ref[...], s1h_ref[...], s2h_ref[...])
        ka = _place_kv(jnp.concatenate([khr, kr], axis=0), 0.125)
        va = _place_kv(jnp.concatenate([kvh_ref[:, KV_W:2 * KV_W], proj_ref[:, C_SV:C_SV + KV_W]], axis=0), 1.0)
        lane128 = lax.broadcasted_iota(jnp.int32, (1, 128), 1)
        gsink = jnp.zeros((1, 128), F32)
        dk_band, dv_band, dq_blk = [], [], []
        for b in range(nb):
            band = slice(BLOCK * b, BLOCK * b + 2 * BLOCK)
            blk = slice(BLOCK * b, BLOCK * (b + 1))
            dka, dva, dsb = [], [], []
            deltas = jnp.zeros((BLOCK, 128), F32)
            for j in range(4):
                qh = qrope[j // 2][blk]
                doh = dyb[blk, KV_W * (j // 2):KV_W * (j // 2 + 1)].astype(_MXU)
                pb = pswa_ref[blk, 2 * BLOCK * j:2 * BLOCK * (j + 1)]
                p = pb.astype(F32)
                dp = _mm_nt(doh, va[j][band])
                delta = jnp.sum(p * dp, axis=-1, keepdims=True)
                ds = (p * (dp - delta)).astype(_MXU)
                deltas = jnp.where(lane128 == j, delta, deltas)
                dva.append(_mm_tn(pb, doh))
                dka.append(_mm_tn(ds, qh))
                dsb.append(ds)
            gsink = gsink - _col_sum(psink_ref[blk, :] * deltas)
            dk_band.append(_unplace_kv(dka) * 0.125)
            dv_band.append(_unplace_kv(dva))
            dq_blk.append([_mm(jnp.concatenate(dsb[2 * h:2 * h + 2], axis=1),
                               jnp.concatenate([ka[2 * h][band], ka[2 * h + 1][band]], axis=0)) for h in range(2)])
        gsink_ref[...] += gsink
        dk_rows = [dk_band[b][BLOCK:] + (dk_band[b + 1][:BLOCK] if b + 1 < nb else dkcar_ref[...]) for b in range(nb)]
        dv_rows = [dv_band[b][BLOCK:] + (dv_band[b + 1][:BLOCK] if b + 1 < nb else dvcar_ref[...]) for b in range(nb)]
        dkcar_ref[...] = dk_band[0][:BLOCK]
        dvcar_ref[...] = dv_band[0][:BLOCK]
        dkg = _rope_bwd(jnp.concatenate(dk_rows, axis=0), cv, s1v, s2v)
        gkn = _col_sum(dkg * kn)
        dkn = dkg * kg_ref[...]
        dproj_ref[:, C_SK:C_SK + KV_W] = (krr * (dkn - kn * _seg_mean(dkn * kn, gm128))).astype(dproj_ref.dtype)
        dproj_ref[:, C_SV:C_SV + KV_W] = jnp.concatenate(dv_rows, axis=0).astype(dproj_ref.dtype)
        gqn = jnp.zeros((1, 128), F32)
        for h in range(2):
            dqg = _rope_bwd(jnp.concatenate([dq_blk[b][h] for b in range(nb)], axis=0), cv, s1v, s2v)
            gqn = gqn + _col_sum(dqg * qn_[h])
            dqn_ = dqg * qg_ref[...]
            dproj_ref[:, C_SQ + 128 * h:C_SQ + 128 * (h + 1)] = (
                qr_[h] * (dqn_ - qn_[h] * _seg_mean(dqn_ * qn_[h], gm128))).astype(dproj_ref.dtype)
        gqn_ref[...] += gqn
        gkn_ref[...] += gkn

        u = proj_ref[:, C_LRUX:C_LRUX + LRU_W]
        xc, rg, ig, sq = (gates_ref[:, LRU_W * k:LRU_W * (k + 1)].astype(F32) for k in range(4))
        a = a_ref[...]
        sp = _softplus(-lam_ref[...])
        hext_ref[0:8, :] = jnp.where(first_tile, 0.0, yah_ref[...])
        hext_ref[8:8 + tm, :] = ya_ref[...]
        hprev = hext_ref[pl.ds(7, tm), :]
        aext_ref[0:tm, :] = a
        an_scr[...] = aext_ref[pl.ds(1, tm), :]
        dh_scr[...] = dya
        dh_scr[tm - 1:tm, :] = dh_scr[tm - 1:tm, :] + gcar_ref[0:1, :]
        row8 = lax.broadcasted_iota(jnp.int32, (8, LRU_W), 0)

        def scan_step(gi, carry):
            r0 = pl.multiple_of((tm // 8 - 1 - gi) * 8, 8)
            av = an_scr[pl.ds(r0, 8), :]
            bv = dh_scr[pl.ds(r0, 8), :]
            for d in (1, 2, 4):
                a_sh = jnp.where(row8 < 8 - d, pltpu.roll(av, 8 - d, 0), 1.0)
                b_sh = jnp.where(row8 < 8 - d, pltpu.roll(bv, 8 - d, 0), 0.0)
                bv = bv + av * b_sh
                av = av * a_sh
            gv = bv + av * carry
            g_scr[pl.ds(r0, 8), :] = gv
            return gv[0:1, :]

        g0 = lax.fori_loop(0, tm // 8, scan_step, jnp.zeros((1, LRU_W), F32), unroll=True)
        gcar_ref[0:1, :] = a[0:1, :] * g0
        gv = g_scr[...]
        da = gv * hprev
        dig = gv * sq * xc
        dxc = gv * sq * ig
        dla = da * a - gv * (ig * xc) * ((a * a) / sq)
        drg = dla * ((-LRU_C) * sp)
        glam_ref[...] += _col_sum(dla * rg)
        dpr = drg * rg * (1.0 - rg)
        dpi = dig * ig * (1.0 - ig)
        gbrg_ref[...] += _col_sum(dpr)
        gbig_ref[...] += _col_sum(dpi)
        dpre0 = jnp.concatenate([dpr[:, :256], dpi[:, :256]], axis=1).astype(_MXU)
        dpre1 = jnp.concatenate([dpr[:, 256:], dpi[:, 256:]], axis=1).astype(_MXU)
        gwg_ref[0] += _mm_tn(xc[:, :256], dpre0)
        gwg_ref[1] += _mm_tn(xc[:, 256:], dpre1)
        dxc = dxc + jnp.concatenate([_mm_nt(dpre0, wg_ref[0]), _mm_nt(dpre1, wg_ref[1])], axis=1)
        gcb_ref[...] += _col_sum(dxc)
        dxc_ext[0:tm, :] = dxc
        du = jnp.zeros((tm, LRU_W), F32)
        for k in range(CONV_K):
            later = dxc_ext[pl.ds(3 - k, tm), :]
            gcw_ref[k:k + 1, :] += _col_sum(later * u)
            du = du + cw_ref[k:k + 1, :] * later
        dxc_ext[tm:tm + 8, :] = dxc[0:8, :]
        dproj_ref[:, C_LRUX:C_LRUX + LRU_W] = du.astype(dproj_ref.dtype)

        dxn = _mm(dproj_ref[...], win_ref[...])
        rx = lax.rsqrt(_row_mean(xv * xv) + EPS)
        xh = xv * rx
        gng_ref[...] += _col_sum(dxn * xh)
        dxh = dxn * ng_ref[...]
        gx_ref[...] = dov.astype(F32) + rx * (dxh - xh * _row_mean(dxh * xh))

        @pl.when(i == nt - 1)
        def _():
            glam_ref[...] = glam_ref[...] * (LRU_C * _sigmoid(-lam_ref[...]))

        _hosted_reduce(i, stage_at, (early_ref,), (early_sum_ref,), reduce_scratch, False)

    def rows(ncol, arr_cols_block=0):
        return pl.BlockSpec((tm, ncol), lambda i: (nt - 1 - i, arr_cols_block))

    def halo(nrow, ncol, colblk=0):
        per = tm // nrow
        return pl.BlockSpec((nrow, ncol), lambda i: (jnp.maximum((nt - 1 - i) * per - 1, 0), colblk))

    in_specs = [rows(D_MODEL), rows(D_MODEL), rows(D_IN), rows(LRU_W), rows(SWA_W), rows(XATT_W),
                rows(4 * 2 * BLOCK), rows(4 * MEM_LEN), rows(128), rows(4 * LRU_W), rows(LRU_W),
                rows(128), rows(128), rows(128),
                halo(8, LRU_W), halo(BLOCK, 2 * KV_W, C_SK // (2 * KV_W)),
                halo(BLOCK, 128), halo(BLOCK, 128), halo(BLOCK, 128),
                _const_spec((1, D_MODEL)), _const_spec((D_IN, D_MODEL), True), _const_spec((CONV_K, LRU_W)),
                _const_spec((1, LRU_W)), _const_spec((2, 256, 512), True), _const_spec((1, LRU_W)),
                _const_spec((1, LRU_W)), _const_spec((1, LRU_W)), _const_spec((1, 128)), _const_spec((1, 128)),
                _const_spec((1, XATT_W)), pl.BlockSpec(memory_space=pltpu.SMEM),
                _const_spec((4 * MEM_LEN, XATT_W), True), _const_spec((4 * MEM_LEN, XATT_W), True),
                _const_spec((1, D_MODEL)), _const_spec((D_MODEL, D_MODEL), True), pl.BlockSpec(memory_space=pl.ANY)]
    red_shape, red_scratch = _hosted_reduce_shapes([g_early], None)
    small = [(2, 256, 512), (4 * MEM_LEN, XATT_W), (4 * MEM_LEN, XATT_W), (1, D_MODEL), (1, D_MODEL), (1, LRU_W), (1, LRU_W),
             (1, LRU_W), (1, LRU_W), (CONV_K, LRU_W), (1, 128), (1, 128), (1, XATT_W), (1, 128)]
    out_shape = (jax.ShapeDtypeStruct((seq, D_MODEL), F32), jax.ShapeDtypeStruct((seq, D_IN), _MXU)) + tuple(
        jax.ShapeDtypeStruct(s, F32) for s in small) + tuple(red_shape)
    out_specs = (rows(D_MODEL), rows(D_IN)) + tuple(_const_spec(s) for s in small) + (
        pl.BlockSpec(memory_space=pl.ANY),)
    scratch = [pltpu.VMEM((tm + 8, LRU_W), F32), pltpu.VMEM((tm + 8, LRU_W), F32),
               pltpu.VMEM((tm, LRU_W), F32), pltpu.VMEM((tm, LRU_W), F32), pltpu.VMEM((tm, LRU_W), F32),
               pltpu.VMEM((tm + 8, LRU_W), F32),
               pltpu.VMEM((8, LRU_W), F32), pltpu.VMEM((BLOCK, KV_W), F32), pltpu.VMEM((BLOCK, KV_W), F32)] + red_scratch
    return pl.pallas_call(
        body, name="layer_bwd", grid=(nt,), out_shape=out_shape, in_specs=in_specs, out_specs=out_specs,
        scratch_shapes=scratch,
        compiler_params=pltpu.CompilerParams(dimension_semantics=("arbitrary",), vmem_limit_bytes=VMEM_LIMIT),
    )(x, dout, proj, ya, yb, yc, pswa, pmem, psink, gates, a_all, rc, rs1, rs2, ya, proj, rc, rs1, rs2,
      ng, win_t, cw, cb, wg, brg, big, lam, qg, kg, xqg, sinks, km, vm, og, wout, g_early)


def _reduce_protocol(big, sm, outs, osm, r1, r1s, wire, r2, r2s, wire2, ps, own, send, recv, lsem):
    nbig = len(big)
    x, y, c = lax.axis_index("x"), lax.axis_index("y"), lax.axis_index("c")
    sibling = (x, y, 1 - c)
    near, far, diag = _partners(x, y, c)
    me, near_id, far_id, diag_id = _chip_of(x, y), _chip_of(*near), _chip_of(*far), _chip_of(*diag)

    def copy(k, src, dst, to):
        return pltpu.make_async_remote_copy(src_ref=src, dst_ref=dst, send_sem=send.at[k], recv_sem=recv.at[k],
                                            device_id=to, device_id_type=MESH)

    def sent(stage):
        cps = []
        for a in range(nbig):
            if stage == 0:
                cps.append(copy(5 * a, big[a].at[:, 1 - c], r1[a], sibling))
            elif stage == 1:
                cps.append(copy(5 * a + 1, wire[a].at[near_id], r2[a].at[0], (*near, c)))
                cps.append(copy(5 * a + 2, wire[a].at[diag_id], r2[a].at[1], (*near, c)))
            elif stage == 2:
                cps.append(copy(5 * a + 3, wire2[a], r2[a].at[2], (*far, c)))
            elif stage == 3:
                cps.append(copy(5 * a + 4, outs[a].at[c], outs[a].at[c], sibling))
        if sm is not None:
            src, dst, to = ((sm.at[1 - c], r1s, sibling), (r1s, r2s.at[0], (*near, c)), (ps, r2s.at[1], (*far, c)),
                            (osm.at[c], osm.at[c], sibling))[stage]
            cps.append(copy(5 * nbig + stage, src, dst, to))
        return cps

    def arrived(k, ref):
        copy(k, ref, ref, sibling).wait_recv()

    def loads():
        return [pltpu.make_async_copy(big[a].at[:, c], own[a], lsem.at[a]) for a in range(nbig)]

    def stage0():
        for cp in sent(0) + loads():
            cp.start()

    def stage1():
        for a in range(nbig):
            loads()[a].wait()
            arrived(5 * a, r1[a])
            for k in range(N_CHIPS):
                r1[a][k] = own[a][k] + r1[a][k]
                wire[a][k] = r1[a][k].astype(wire[a].dtype)
        if sm is not None:
            arrived(5 * nbig, r1s)
            r1s[...] = sm[c] + r1s[...]
        for cp in sent(1):
            cp.start()

    def stage2():
        for a in range(nbig):
            arrived(5 * a + 1, r2[a].at[0])
            arrived(5 * a + 2, r2[a].at[1])
            r1[a][me] = r1[a][me] + r2[a][0].astype(F32)
            wire2[a][...] = (r1[a][far_id] + r2[a][1].astype(F32)).astype(wire2[a].dtype)
        if sm is not None:
            arrived(5 * nbig + 1, r2s.at[0])
            ps[...] = r1s[...] + r2s[0]
        for cp in sent(2):
            cp.start()

    def stage3():
        for a in range(nbig):
            arrived(5 * a + 3, r2[a].at[2])
            outs[a][c] = r1[a][me] + r2[a][2].astype(F32)
        if sm is not None:
            arrived(5 * nbig + 2, r2s.at[1])
            osm[c] = ps[...] + r2s[1]
        for cp in sent(3):
            cp.start()

    def stage4():
        for a in range(nbig):
            arrived(5 * a + 4, outs[a].at[1 - c])
        if sm is not None:
            arrived(5 * nbig + 3, osm.at[1 - c])
        for stage in range(4):
            for cp in sent(stage):
                cp.wait_send()

    return [stage0, stage1, stage2, stage3, stage4]


def _reduce_buffers(bigs, g_small):
    half = [b.shape[2:] for b in bigs]
    sm_half = None if g_small is None else g_small.shape[1:]
    out_shape = [jax.ShapeDtypeStruct((2,) + h, F32) for h in half]
    small = lambda lead: [] if g_small is None else [pltpu.VMEM(lead + sm_half, F32)]
    if g_small is not None:
        out_shape.append(jax.ShapeDtypeStruct(g_small.shape, F32))
    n_sem = 5 * len(bigs) + 4
    scratch = ([pltpu.VMEM((N_CHIPS,) + h, F32) for h in half] + small(())
               + [pltpu.VMEM((N_CHIPS,) + h, _WIRE) for h in half]
               + [pltpu.VMEM((3,) + h, _WIRE) for h in half] + small((2,))
               + [pltpu.VMEM(h, _WIRE) for h in half] + small(())
               + [pltpu.VMEM((N_CHIPS,) + h, F32) for h in half]
               + [pltpu.SemaphoreType.DMA((n_sem,)), pltpu.SemaphoreType.DMA((n_sem,)),
                  pltpu.SemaphoreType.DMA((len(bigs),))])
    return out_shape, scratch


def _split_reduce_refs(refs, nbig, has_small):
    it = iter(refs)
    take = lambda n: [next(it) for _ in range(n)]
    one = lambda: next(it) if has_small else None
    big, sm = take(nbig), one()
    outs, osm = take(nbig), one()
    r1, r1s, wire, r2, r2s, wire2, ps, own = take(nbig), one(), take(nbig), take(nbig), one(), take(nbig), one(), take(nbig)
    send, recv, lsem = take(3)
    return big, sm, outs, osm, r1, r1s, wire, r2, r2s, wire2, ps, own, send, recv, lsem


def _hosted_reduce_shapes(bigs, g_small):
    red_shape, scratch = _reduce_buffers(bigs, g_small)
    nres = len(red_shape)
    return red_shape, [pltpu.VMEM(r.shape, r.dtype) for r in red_shape] + scratch + [pltpu.SemaphoreType.DMA((nres,))]


def _hosted_reduce(step, stage_at, operands, results, scratch, has_small):
    nres = len(results)
    sums, rest, fsem = scratch[:nres], scratch[nres:-1], scratch[-1]
    refs = tuple(operands) + tuple(sums) + tuple(rest)
    for at, stage in zip(stage_at, _reduce_protocol(*_split_reduce_refs(refs, nres - has_small, has_small))):
        pl.when(step == at)(stage)

    @pl.when(step == stage_at[-1])
    def _():
        out = [pltpu.make_async_copy(sums[k], results[k], fsem.at[k]) for k in range(nres)]
        for cp in out:
            cp.start()
        for cp in out:
            cp.wait()


def reduce_grads(bigs, g_small, name):
    nbig = len(bigs)

    def body(*refs):
        for stage in _reduce_protocol(*_split_reduce_refs(refs, nbig, g_small is not None)):
            stage()

    out_shape, scratch = _reduce_buffers(bigs, g_small)
    vm = pl.BlockSpec(memory_space=pltpu.VMEM)
    hbm = pl.BlockSpec(memory_space=pl.ANY)
    operands = list(bigs) + ([] if g_small is None else [g_small])
    return pl.pallas_call(
        body, name=name, out_shape=tuple(out_shape), in_specs=[hbm] * nbig + [vm] * (g_small is not None),
        out_specs=(vm,) * len(out_shape), scratch_shapes=scratch,
        compiler_params=pltpu.CompilerParams(vmem_limit_bytes=VMEM_LIMIT),
    )(*operands)


def adamw(w, g, m, v, name):
    rows_, cols = w.shape
    tr = max(t for t in range(8, rows_ + 1, 8) if rows_ % t == 0 and t * cols * 4 <= ADAM_BLOCK_BYTES)

    def body(w_ref, g_ref, m_ref, v_ref, d_ref, nm_ref, nv_ref):
        gv = g_ref[...]
        nm = ADAM_B1 * m_ref[...] + (1.0 - ADAM_B1) * gv
        nv = ADAM_B2 * v_ref[...] + (1.0 - ADAM_B2) * (gv * gv)
        m_hat = nm / (1.0 - ADAM_B1 ** ADAM_STEP)
        v_hat = nv / (1.0 - ADAM_B2 ** ADAM_STEP)
        d_ref[...] = (-ADAM_LR) * (m_hat / (jnp.sqrt(v_hat) + ADAM_EPS) + ADAM_WD * w_ref[...])
        nm_ref[...] = nm
        nv_ref[...] = nv

    spec = pl.BlockSpec((tr, cols), lambda i: (i, 0))
    shp = jax.ShapeDtypeStruct(w.shape, F32)
    return pl.pallas_call(
        body, name=name, grid=(rows_ // tr,), out_shape=(shp, shp, shp), in_specs=[spec] * 4, out_specs=(spec,) * 3,
        compiler_params=pltpu.CompilerParams(dimension_semantics=("arbitrary",)),
    )(w, g, m, v)


SMALL = (("norm_g", 1024), ("mem_norm_g", 1024), ("conv_w", 512), ("conv_b", 512), ("w_rg", 32768), ("b_rg", 512),
         ("w_ig", 32768), ("b_ig", 512), ("lru_lambda", 512), ("q_norm_g", 128), ("k_norm_g", 128), ("sinks", 128),
         ("xq_norm_g", 128), ("xk_norm_g", 128), ("out_norm_g", 1024))
SMALL_ROWS = 576


def _pack(parts, rows_):
    flat = jnp.concatenate([p.reshape(-1) for p in parts])
    return jnp.pad(flat, (0, rows_ * 128 - flat.shape[0])).reshape(rows_, 128)


def _pad_to(v, n):
    v = v.reshape(-1)
    return jnp.pad(v, (0, n - v.shape[0]))


def _block_diag_gates(w_rg, w_ig):
    eye = jnp.eye(4, dtype=w_rg.dtype)

    def bd(w4):
        return (w4[:, :, None, :] * eye[:, None, :, None]).reshape(256, 256)

    return jnp.stack([jnp.concatenate([bd(w_rg[4 * h:4 * h + 4]), bd(w_ig[4 * h:4 * h + 4])], axis=1) for h in (0, 1)])


def _diag_blocks(g):
    g6 = g.reshape(2, 4, HEAD, 2, 4, HEAD)
    d = (g6 * jnp.eye(4, dtype=g.dtype)[None, :, None, None, :, None]).sum(axis=4)
    return d[:, :, :, 0].reshape(8, HEAD, HEAD), d[:, :, :, 1].reshape(8, HEAD, HEAD)


def _rope_tables(seq):
    pos = np.arange(seq, dtype=np.float32)
    inv_freq = (np.float32(ROPE_THETA) ** (-(np.arange(0, ROPE_DIM, 2, dtype=np.float32) / np.float32(ROPE_DIM)))
                ).astype(np.float32)
    ang = (pos[:, None] * inv_freq[None, :]).astype(np.float32)
    cos, sin = np.cos(ang).astype(np.float32), np.sin(ang).astype(np.float32)
    z = lambda n: np.zeros((seq, n), np.float32)
    c64 = np.concatenate([cos, cos, np.ones((seq, HEAD - ROPE_DIM), np.float32)], axis=1)
    s1_64 = np.concatenate([-sin, z(HEAD - 8)], axis=1)
    s2_64 = np.concatenate([z(8), sin, z(HEAD - ROPE_DIM)], axis=1)
    return tuple(jnp.asarray(np.concatenate([t, t], axis=1)) for t in (c64, s1_64, s2_64))


def kernel(x, mem, norm_g, mem_norm_g, w_in, conv_w, conv_b, w_rg, b_rg, w_ig, b_ig, lru_lambda, q_norm_g, k_norm_g, sinks, w_mem_kv, xq_norm_g, xk_norm_g, out_norm_g, w_out, loss_target, m_norm_g, m_mem_norm_g, m_w_in, m_conv_w, m_conv_b, m_w_rg, m_b_rg, m_w_ig, m_b_ig, m_lru_lambda, m_q_norm_g, m_k_norm_g, m_sinks, m_w_mem_kv, m_xq_norm_g, m_xk_norm_g, m_out_norm_g, m_w_out, v_norm_g, v_mem_norm_g, v_w_in, v_conv_w, v_conv_b, v_w_rg, v_b_rg, v_w_ig, v_b_ig, v_lru_lambda, v_q_norm_g, v_k_norm_g, v_sinks, v_w_mem_kv, v_xq_norm_g, v_xk_norm_g, v_out_norm_g, v_w_out):
    seq = x.shape[1]
    chip = 2 * lax.axis_index("x") + lax.axis_index("y")
    xs, tgt, mems = x[0], loss_target[0], mem[0]

    win_t_sh = w_in[0].T.astype(_MXU)
    cw_sh = jnp.pad(conv_w[0], ((0, 4), (0, 0)))
    win_t, wout, wkv, cw_all = gather_weights(win_t_sh, w_out[0].astype(_MXU), w_mem_kv[0].astype(_MXU), cw_sh)
    cw = cw_all.reshape(N_CHIPS, 8, 128)[:, :CONV_K].transpose(1, 0, 2).reshape(CONV_K, LRU_W)

    rc, rs1, rs2 = _rope_tables(seq)
    wg = _block_diag_gates(w_rg[0], w_ig[0]).astype(_MXU)
    qg = jnp.tile(q_norm_g, (1, 2))
    kg = jnp.tile(k_norm_g, (1, 2))
    xqg = jnp.tile(xq_norm_g, (1, 4))
    xkg = jnp.tile(xk_norm_g, (1, 4))

    km, vm = mem_fwd(mems, mem_norm_g, wkv, xkg)
    proj, ya, yb, yc, ycat, xn, dout, pswa, pmem, psink, gates, a_all, loss8 = layer_fwd(
        xs, tgt, rc, rs1, rs2, norm_g, win_t, cw, conv_b, wg, b_rg, b_ig, lru_lambda, qg, kg, xqg, sinks, km, vm,
        out_norm_g, wout)
    g_wout = wgrad_out(ycat, dout)
    (gx, dproj, g_wg, dkm, dvm, g_ng, g_og, g_cb, g_brg, g_big, g_lam, g_cw, g_qn, g_kn, g_xqn, g_sink,
     r_out) = layer_bwd(
        xs, dout, proj, ya, yb, yc, pswa, pmem, psink, gates, a_all, rc, rs1, rs2, norm_g, win_t, cw, conv_b, wg, b_rg,
        b_ig, lru_lambda, qg, kg, xqg, sinks, km, vm, out_norm_g, wout,
        g_wout.reshape(N_CHIPS, 2, D_MODEL // 8, D_MODEL))
    g_wkv, g_mng, g_xkn = mem_bwd(mems, mem_norm_g, wkv, xkg, dkm, dvm)

    g_wrg, g_wig = _diag_blocks(g_wg)
    fold = lambda v, n: v.reshape(n, HEAD).sum(axis=0)
    small_g = _pack([g_ng, g_mng, g_cw, g_cb, g_wrg, g_brg, g_wig, g_big, g_lam, _pad_to(fold(g_qn, 2), 128),
                     _pad_to(fold(g_kn, 2), 128), g_sink, _pad_to(fold(g_xqn, 4), 128), _pad_to(fold(g_xkn, 4), 128),
                     g_og, loss8[0:1]], SMALL_ROWS)
    g_win_t, r_kv, r_small = wgrad_in_reduce(dproj, xn, [g_wkv.reshape(N_CHIPS, 2, D_MODEL // 8, 2 * XATT_W)],
                                             small_g.reshape(2, SMALL_ROWS // 2, 128))
    (r_in,) = reduce_grads([g_win_t.reshape(N_CHIPS, 2, D_IN // 8, D_MODEL)], None, "reduce_w_in")

    flat = r_small.reshape(-1)
    sizes = (1024, 1024, 2048, 512, 32768, 512, 32768, 512, 512, 128, 128, 128, 128, 128, 1024)
    offs = [0]
    for s in sizes:
        offs.append(offs[-1] + s)
    loss = flat[offs[-1]]
    piece = {name: flat[offs[k]:offs[k + 1]] for k, (name, _) in enumerate(SMALL)}
    g_cw_mine = lax.dynamic_slice(piece["conv_w"].reshape(CONV_K, LRU_W), (0, chip * 128), (CONV_K, 128))
    grads = {
        "norm_g": piece["norm_g"].reshape(1, 1024), "mem_norm_g": piece["mem_norm_g"].reshape(1, 1024),
        "w_in": r_in.reshape(D_IN // 4, D_MODEL).T[None], "conv_w": g_cw_mine[None],
        "conv_b": piece["conv_b"].reshape(1, 512), "w_rg": piece["w_rg"].reshape(1, 8, HEAD, HEAD),
        "b_rg": piece["b_rg"].reshape(1, 512), "w_ig": piece["w_ig"].reshape(1, 8, HEAD, HEAD),
        "b_ig": piece["b_ig"].reshape(1, 512), "lru_lambda": piece["lru_lambda"].reshape(1, 512),
        "q_norm_g": piece["q_norm_g"][:HEAD].reshape(1, HEAD), "k_norm_g": piece["k_norm_g"][:HEAD].reshape(1, HEAD),
        "sinks": piece["sinks"][:4].reshape(1, 4), "w_mem_kv": r_kv.reshape(D_MODEL // 4, 2 * XATT_W)[None],
        "xq_norm_g": piece["xq_norm_g"][:HEAD].reshape(1, HEAD), "xk_norm_g": piece["xk_norm_g"][:HEAD].reshape(1, HEAD),
        "out_norm_g": piece["out_norm_g"].reshape(1, 1024), "w_out": r_out.reshape(D_MODEL // 4, D_MODEL)[None],
    }
    weights = dict(norm_g=norm_g, mem_norm_g=mem_norm_g, w_in=w_in, conv_w=conv_w, conv_b=conv_b, w_rg=w_rg, b_rg=b_rg,
                   w_ig=w_ig, b_ig=b_ig, lru_lambda=lru_lambda, q_norm_g=q_norm_g, k_norm_g=k_norm_g, sinks=sinks,
                   w_mem_kv=w_mem_kv, xq_norm_g=xq_norm_g, xk_norm_g=xk_norm_g, out_norm_g=out_norm_g, w_out=w_out)
    ms = dict(norm_g=m_norm_g, mem_norm_g=m_mem_norm_g, w_in=m_w_in, conv_w=m_conv_w, conv_b=m_conv_b, w_rg=m_w_rg,
              b_rg=m_b_rg, w_ig=m_w_ig, b_ig=m_b_ig, lru_lambda=m_lru_lambda, q_norm_g=m_q_norm_g, k_norm_g=m_k_norm_g,
              sinks=m_sinks, w_mem_kv=m_w_mem_kv, xq_norm_g=m_xq_norm_g, xk_norm_g=m_xk_norm_g,
              out_norm_g=m_out_norm_g, w_out=m_w_out)
    vs = dict(norm_g=v_norm_g, mem_norm_g=v_mem_norm_g, w_in=v_w_in, conv_w=v_conv_w, conv_b=v_conv_b, w_rg=v_w_rg,
              b_rg=v_b_rg, w_ig=v_w_ig, b_ig=v_b_ig, lru_lambda=v_lru_lambda, q_norm_g=v_q_norm_g, k_norm_g=v_k_norm_g,
              sinks=v_sinks, w_mem_kv=v_w_mem_kv, xq_norm_g=v_xq_norm_g, xk_norm_g=v_xk_norm_g,
              out_norm_g=v_out_norm_g, w_out=v_w_out)

    delta, new_m, new_v = {}, {}, {}
    d2, m2, v2 = adamw(w_in[0].T, r_in.reshape(D_IN // 4, D_MODEL), m_w_in[0].T, v_w_in[0].T, "adamw_w_in")
    delta["w_in"], new_m["w_in"], new_v["w_in"] = d2.T[None], m2.T[None], v2.T[None]
    for name in ("w_mem_kv", "w_out"):
        shp = weights[name].shape
        d2, m2, v2 = adamw(weights[name][0], grads[name][0], ms[name][0], vs[name][0], "adamw_" + name)
        delta[name], new_m[name], new_v[name] = d2.reshape(shp), m2.reshape(shp), v2.reshape(shp)
    small_names = [n for n, _ in SMALL]
    packs = [_pack([_pad_to(d[n], sz) for n, sz in SMALL], SMALL_ROWS) for d in (weights, grads, ms, vs)]
    d_p, m_p, v_p = adamw(*packs, "adamw_small")
    offs2 = [0]
    for _, sz in SMALL:
        offs2.append(offs2[-1] + sz)
    for out_d, pk in ((delta, d_p), (new_m, m_p), (new_v, v_p)):
        fl = pk.reshape(-1)
        for k, n in enumerate(small_names):
            shp = weights[n].shape
            out_d[n] = fl[offs2[k]:offs2[k] + math.prod(shp)].reshape(shp)

    order = ("norm_g", "mem_norm_g", "w_in", "conv_w", "conv_b", "w_rg", "b_rg", "w_ig", "b_ig", "lru_lambda",
             "q_norm_g", "k_norm_g", "sinks", "w_mem_kv", "xq_norm_g", "xk_norm_g", "out_norm_g", "w_out")
    return (loss, gx[None], *[grads[n] for n in order], *[delta[n] for n in order], *[new_m[n] for n in order],
            *[new_v[n] for n in order])
```

```python
import jax
import jax.numpy as jnp
import numpy as np
from jax import lax
from jax.experimental import pallas as pl
from jax.experimental.pallas import tpu as pltpu

F32 = jnp.float32
_MXU = jnp.bfloat16
_WIRE = jnp.bfloat16

D_MODEL = 1024
MEM_LEN = 256
HEAD = 64
LRU_W = 512
LRU_BLOCKS = 8
CONV_K = 4
LRU_C = 8.0
SWA_W = 256
KV_W = 128
XATT_W = 256
BLOCK = 128
D_IN = 2304
ROPE_THETA = 500000.0
ROPE_DIM = 16
EPS = 1e-6
NEG_INF = -1e30
C_LRUX, C_LRUG, C_SQ, C_SK, C_SV, C_SWAG, C_XQ, C_XG = 0, 512, 1024, 1280, 1408, 1536, 1792, 2048

ADAM_LR, ADAM_B1, ADAM_B2, ADAM_EPS, ADAM_WD, ADAM_STEP = 0.001, 0.9, 0.999, 1e-08, 0.01, 10

N_CHIPS = 4
ROW_TILE = 256
VMEM_LIMIT = 56 * 1024 * 1024
ADAM_BLOCK_BYTES = 1280 * 1024
GATHER_PIECES = (1, 1, 1)
MESH = pl.DeviceIdType.MESH


def _mm(a, b):
    return jnp.dot(a.astype(_MXU), b.astype(_MXU), preferred_element_type=F32)


def _mm_nt(a, b):
    return lax.dot_general(a.astype(_MXU), b.astype(_MXU), (((1,), (1,)), ((), ())), preferred_element_type=F32)


def _mm_tn(a, b):
    return lax.dot_general(a.astype(_MXU), b.astype(_MXU), (((0,), (0,)), ((), ())), preferred_element_type=F32)


def _group_matrix(width):
    r = lax.shift_right_logical(lax.broadcasted_iota(jnp.int32, (width, width), 0), 6)
    c = lax.shift_right_logical(lax.broadcasted_iota(jnp.int32, (width, width), 1), 6)
    return (r == c).astype(_MXU)


def _seg_mean(x, gm):
    return jnp.dot(x.astype(_MXU), gm, preferred_element_type=F32) * (1.0 / HEAD)


def _row_mean(x):
    return jnp.mean(x, axis=-1, keepdims=True)


def _col_sum(x):
    return jnp.sum(x, axis=0, keepdims=True)


def _sigmoid(x):
    return jax.nn.sigmoid(x)


def _softplus(z):
    e = jnp.exp(-jnp.abs(z))
    u = 1.0 + e
    log1p_e = jnp.where(u == 1.0, e, jnp.log(u) * (e / (u - 1.0)))
    return jnp.maximum(z, 0.0) + log1p_e


def _rope(t, c, s1, s2):
    return t * c + pltpu.roll(t, 120, 1) * s1 + pltpu.roll(t, 8, 1) * s2


def _rope_bwd(d, c, s1, s2):
    return d * c + pltpu.roll(d * s1, 8, 1) + pltpu.roll(d * s2, 120, 1)


def _lane_mask(width, lo, hi):
    lane = lax.broadcasted_iota(jnp.int32, (1, width), 1)
    return ((lane >= lo) & (lane < hi)).astype(F32)


def _swa_mask(first_block):
    qi = lax.broadcasted_iota(jnp.int32, (BLOCK, 2 * BLOCK), 0)
    kj = lax.broadcasted_iota(jnp.int32, (BLOCK, 2 * BLOCK), 1)
    rel = qi + BLOCK - kj
    ok = (rel >= 0) & (rel < BLOCK)
    return ok & (jnp.logical_not(first_block) | (kj >= BLOCK))


def _place_kv(t, scale):
    lo = t * (_lane_mask(KV_W, 0, HEAD) * scale)
    hi = t * (_lane_mask(KV_W, HEAD, KV_W) * scale)
    return [a.astype(_MXU) for a in (lo, pltpu.roll(lo, HEAD, 1), pltpu.roll(hi, HEAD, 1), hi)]


def _unplace_kv(d):
    return (_lane_mask(KV_W, 0, HEAD) * (d[0] + pltpu.roll(d[1], HEAD, 1))
            + _lane_mask(KV_W, HEAD, KV_W) * (d[3] + pltpu.roll(d[2], HEAD, 1)))


def _swa_probs(qh, ka, mask, sink):
    s = _mm_nt(qh, ka)
    s = jnp.where(mask, s, NEG_INF)
    m = jnp.maximum(jnp.max(s, axis=-1, keepdims=True), sink)
    p = jnp.exp(s - m)
    esink = jnp.exp(sink - m)
    inv = 1.0 / (jnp.sum(p, axis=-1, keepdims=True) + esink)
    return p * inv, esink * inv


def _mem_probs(s_all):
    out = []
    for j in range(4):
        s = s_all[:, MEM_LEN * j:MEM_LEN * (j + 1)]
        p = jnp.exp(s - jnp.max(s, axis=-1, keepdims=True))
        out.append(p * (1.0 / jnp.sum(p, axis=-1, keepdims=True)))
    return out


def _head_rows(t, scale):
    return jnp.concatenate([t * (_lane_mask(XATT_W, HEAD * j, HEAD * (j + 1)) * scale) for j in range(4)], axis=0)


def _lru_gates(xc, wg_ref, brg, big, lam):
    p0 = _mm(xc[:, :256], wg_ref[0])
    p1 = _mm(xc[:, 256:], wg_ref[1])
    rg = _sigmoid(jnp.concatenate([p0[:, :256], p1[:, :256]], axis=1) + brg)
    ig = _sigmoid(jnp.concatenate([p0[:, 256:], p1[:, 256:]], axis=1) + big)
    sp = _softplus(-lam)
    la = (-LRU_C) * rg * sp
    a = jnp.exp(la)
    th = jnp.tanh(la)
    one_minus_a2 = (-2.0 * th) / (1.0 - th)
    return rg, ig, sp, a, jnp.sqrt(one_minus_a2)


def _const_spec(shape, single=False):
    zeros = (0,) * len(shape)
    if single:
        return pl.BlockSpec(shape, lambda i: zeros, pipeline_mode=pl.Buffered(1))
    return pl.BlockSpec(shape, lambda i: zeros)


def _chip_of(x, y):
    return 2 * x + y


def _partners(x, y, c):
    north = c == 1
    near = (jnp.where(north, 1 - x, x), jnp.where(north, y, 1 - y))
    far = (jnp.where(north, x, 1 - x), jnp.where(north, 1 - y, y))
    return near, far, (1 - x, 1 - y)


def gather_weights(win_t, wout, wkv, convw):
    arrs = (win_t, wout, wkv)
    n = len(arrs)
    pieces = []
    for a, arr in enumerate(arrs):
        half = arr.shape[0] // 2
        step = half // GATHER_PIECES[a]
        pieces += [(a, off, step) for off in range(0, half, step)]
    npc = len(pieces)

    def body(a0, a1, a2, cw, o0, o1, o2, ocw, send, recv, lsem):
        ins, outs = (a0, a1, a2), (o0, o1, o2)
        x, y, c = lax.axis_index("x"), lax.axis_index("y"), lax.axis_index("c")
        sibling = (x, y, 1 - c)
        near, far, diag = _partners(x, y, c)
        chips = [near, far, diag]
        me = _chip_of(x, y)

        def landed(p, chip, half):
            a, off, rows_ = pieces[p]
            r = ins[a].shape[0]
            return outs[a].at[pl.ds(pl.multiple_of(chip * r + half * (r // 2) + off, 16), rows_)]

        def mine(p):
            a, off, rows_ = pieces[p]
            return ins[a].at[pl.ds(pl.multiple_of(c * (ins[a].shape[0] // 2) + off, 16), rows_)]

        def copy(k, src, dst, to):
            return pltpu.make_async_remote_copy(src_ref=src, dst_ref=dst, send_sem=send.at[k], recv_sem=recv.at[k],
                                                device_id=to, device_id_type=MESH)

        def cw_rows(chip):
            return ocw.at[pl.ds(pl.multiple_of(chip * 8, 8), 8)]

        locals_ = []
        for a in range(n):
            r = ins[a].shape[0]
            locals_.append(pltpu.make_async_copy(ins[a], outs[a].at[pl.ds(pl.multiple_of(me * r, 16), r)], lsem.at[a]))
        locals_.append(pltpu.make_async_copy(cw, cw_rows(me), lsem.at[n]))
        for cp in locals_:
            cp.start()

        sent = []
        for p in range(npc):
            for j in range(2):
                sent.append(copy(p * 6 + j, mine(p), landed(p, me, c), (*chips[j], c)))
        for j, chip in enumerate(chips):
            sent.append(copy(npc * 6 + j, cw, cw_rows(me), (*chip, c)))
        for cp in sent:
            cp.start()
        for p in range(npc):
            for j in range(3):
                got = landed(p, _chip_of(*chips[j]), c)
                copy(p * 6 + j, got, got, sibling).wait_recv()
                if j == 0:
                    sent.append(copy(p * 6 + 2, got, got, (*far, c)))
                    sent[-1].start()
                sent.append(copy(p * 6 + 3 + j, got, got, sibling))
                sent[-1].start()
        for p in range(npc):
            for j in range(3):
                got = landed(p, _chip_of(*chips[(1, 0, 2)[j]]), 1 - c)
                copy(p * 6 + 3 + j, got, got, sibling).wait_recv()
        for j, chip in enumerate(chips):
            got = cw_rows(_chip_of(*chip))
            copy(npc * 6 + j, got, got, (*chip, c)).wait_recv()
        for cp in sent:
            cp.wait_send()
        for cp in locals_:
            cp.wait()

    vm = pl.BlockSpec(memory_space=pltpu.VMEM)
    out_shape = tuple(jax.ShapeDtypeStruct((N_CHIPS * a.shape[0],) + a.shape[1:], a.dtype) for a in arrs) + (
        jax.ShapeDtypeStruct((N_CHIPS * 8, 128), F32),)
    n_rdma = npc * 6 + 3
    return pl.pallas_call(
        body, name="gather_weights", out_shape=out_shape,
        in_specs=[vm] * 4, out_specs=(vm,) * 4,
        scratch_shapes=[pltpu.SemaphoreType.DMA((n_rdma,)), pltpu.SemaphoreType.DMA((n_rdma,)),
                        pltpu.SemaphoreType.DMA((n + 1,))],
        compiler_params=pltpu.CompilerParams(vmem_limit_bytes=VMEM_LIMIT),
    )(win_t, wout, wkv, convw)


def mem_fwd(mem, mem_g, wkv, xk_g):
    def body(mem_ref, g_ref, w_ref, xk_ref, km_ref, vm_ref):
        mem_v = mem_ref[...]
        mn = mem_v * lax.rsqrt(_row_mean(mem_v * mem_v) + EPS) * g_ref[...]
        mkv = _mm(mn, w_ref[...])
        kpre = mkv[:, :XATT_W]
        gm = _group_matrix(XATT_W)
        km = kpre * lax.rsqrt(_seg_mean(kpre * kpre, gm) + EPS) * xk_ref[...]
        km_ref[...] = _head_rows(km, 0.125).astype(km_ref.dtype)
        vm_ref[...] = _head_rows(mkv[:, XATT_W:], 1.0).astype(vm_ref.dtype)

    vm = pl.BlockSpec(memory_space=pltpu.VMEM)
    rows_shape = jax.ShapeDtypeStruct((4 * MEM_LEN, XATT_W), _MXU)
    return pl.pallas_call(
        body, name="mem_fwd", out_shape=(rows_shape, rows_shape), in_specs=[vm] * 4, out_specs=(vm, vm),
    )(mem, mem_g, wkv, xk_g)


def mem_bwd(mem, mem_g, wkv, xk_g, dkm, dvm):
    def body(mem_ref, g_ref, w_ref, xk_ref, dkm_ref, dvm_ref, gw_ref, gg_ref, gxk_ref):
        mem_v = mem_ref[...]
        mh = mem_v * lax.rsqrt(_row_mean(mem_v * mem_v) + EPS)
        mn = mh * g_ref[...]
        mkv = _mm(mn, w_ref[...])
        kpre = mkv[:, :XATT_W]
        gm = _group_matrix(XATT_W)
        rk = lax.rsqrt(_seg_mean(kpre * kpre, gm) + EPS)
        kn = kpre * rk
        dk = jnp.zeros((MEM_LEN, XATT_W), F32)
        dv = jnp.zeros((MEM_LEN, XATT_W), F32)
        for j in range(4):
            mj = _lane_mask(XATT_W, HEAD * j, HEAD * (j + 1))
            dk = dk + dkm_ref[MEM_LEN * j:MEM_LEN * (j + 1), :] * (mj * 0.125)
            dv = dv + dvm_ref[MEM_LEN * j:MEM_LEN * (j + 1), :] * mj
        gxk_ref[...] = _col_sum(dk * kn)
        dkn = dk * xk_ref[...]
        dkpre = rk * (dkn - kn * _seg_mean(dkn * kn, gm))
        dmkv = jnp.concatenate([dkpre, dv], axis=1)
        gw_ref[...] = _mm_tn(mn, dmkv)
        dmn = _mm_nt(dmkv, w_ref[...])
        gg_ref[...] = _col_sum(dmn * mh)

    vm = pl.BlockSpec(memory_space=pltpu.VMEM)
    return pl.pallas_call(
        body, name="mem_bwd",
        out_shape=(jax.ShapeDtypeStruct((D_MODEL, 2 * XATT_W), F32), jax.ShapeDtypeStruct((1, D_MODEL), F32),
                   jax.ShapeDtypeStruct((1, XATT_W), F32)),
        in_specs=[vm] * 6, out_specs=(vm, vm, vm),
    )(mem, mem_g, wkv, xk_g, dkm, dvm)


def layer_fwd(x, tgt, rc, rs1, rs2, ng, win_t, cw, cb, wg, brg, big, lam, qg, kg, xqg, sinks, km, vm, og, wout):
    seq = x.shape[0]
    tm = min(ROW_TILE, seq)
    nt = seq // tm
    nb = tm // BLOCK

    def body(x_ref, t_ref, c_ref, s1_ref, s2_ref, ng_ref, win_ref, cw_ref, cb_ref, wg_ref, brg_ref, big_ref, lam_ref,
             qg_ref, kg_ref, xqg_ref, sink_ref, km_ref, vm_ref, og_ref, wout_ref,
             proj_ref, ya_ref, yb_ref, yc_ref, ycat_ref, xn_ref, dout_ref, pswa_ref, pmem_ref, psink_ref, gates_ref,
             a_ref, loss_ref,
             ext_ref, b_scr, hc_ref, kp_ref, vp_ref, lacc_ref):
        i = pl.program_id(0)

        @pl.when(i == 0)
        def _():
            ext_ref[0:8, :] = jnp.zeros((8, LRU_W), F32)
            hc_ref[...] = jnp.zeros_like(hc_ref)
            kp_ref[...] = jnp.zeros_like(kp_ref)
            vp_ref[...] = jnp.zeros_like(vp_ref)
            lacc_ref[...] = jnp.zeros_like(lacc_ref)

        xv = x_ref[...]
        xn = (xv * lax.rsqrt(_row_mean(xv * xv) + EPS) * ng_ref[...]).astype(_MXU)
        xn_ref[...] = xn.astype(xn_ref.dtype)
        proj_ref[...] = _mm_nt(xn, win_ref[...])

        u = proj_ref[:, C_LRUX:C_LRUX + LRU_W]
        ext_ref[8:8 + tm, :] = u
        xc = cb_ref[...]
        for k in range(CONV_K):
            xc = xc + cw_ref[k:k + 1, :] * ext_ref[pl.ds(5 + k, tm), :]
        ext_ref[0:8, :] = u[tm - 8:tm, :]
        rg, ig, sp, a, sq = _lru_gates(xc, wg_ref, brg_ref[...], big_ref[...], lam_ref[...])
        for k, t in enumerate((xc, rg, ig, sq)):
            gates_ref[:, LRU_W * k:LRU_W * (k + 1)] = t.astype(gates_ref.dtype)
        a_ref[...] = a
        b_scr[...] = sq * (ig * xc)
        row8 = lax.broadcasted_iota(jnp.int32, (8, LRU_W), 0)

        def scan_step(g, carry):
            r0 = pl.multiple_of(g * 8, 8)
            av = a_ref[pl.ds(r0, 8), :]
            bv = b_scr[pl.ds(r0, 8), :]
            for d in (1, 2, 4):
                a_sh = jnp.where(row8 >= d, pltpu.roll(av, d, 0), 1.0)
                b_sh = jnp.where(row8 >= d, pltpu.roll(bv, d, 0), 0.0)
                bv = bv + av * b_sh
                av = av * a_sh
            hv = bv + av * carry
            ya_ref[pl.ds(r0, 8), :] = hv
            return hv[7:8, :]

        hc_ref[0:1, :] = lax.fori_loop(0, tm // 8, scan_step, hc_ref[0:1, :], unroll=True)

        gm128 = _group_matrix(KV_W)
        cv, s1v, s2v = c_ref[...], s1_ref[...], s2_ref[...]

        def head_norm_rope(t, g):
            n = t * lax.rsqrt(_seg_mean(t * t, gm128) + EPS)
            return _rope(n * g, cv, s1v, s2v)

        qs_ = (head_norm_rope(proj_ref[:, C_SQ:C_SQ + 128], qg_ref[...]).astype(_MXU),
               head_norm_rope(proj_ref[:, C_SQ + 128:C_SQ + 256], qg_ref[...]).astype(_MXU))
        kr = head_norm_rope(proj_ref[:, C_SK:C_SK + KV_W], kg_ref[...])
        sv = proj_ref[:, C_SV:C_SV + KV_W]
        ka = _place_kv(jnp.concatenate([kp_ref[...], kr], axis=0), 0.125)
        va = _place_kv(jnp.concatenate([vp_ref[...], sv], axis=0), 1.0)
        kp_ref[...] = kr[tm - BLOCK:tm, :]
        vp_ref[...] = sv[tm - BLOCK:tm, :]
        lane128 = lax.broadcasted_iota(jnp.int32, (1, 128), 1)
        for b in range(nb):
            mask = _swa_mask((i == 0) & (b == 0)) if b == 0 else _swa_mask(False)
            band = slice(BLOCK * b, BLOCK * b + 2 * BLOCK)
            blk = slice(BLOCK * b, BLOCK * (b + 1))
            psink = jnp.zeros((BLOCK, 128), F32)
            for j in range(4):
                p, pk = _swa_probs(qs_[j // 2][blk], ka[j][band], mask, sink_ref[0, j])
                pswa_ref[blk, 2 * BLOCK * j:2 * BLOCK * (j + 1)] = p.astype(pswa_ref.dtype)
                psink = jnp.where(lane128 == j, pk, psink)
            psink_ref[blk, :] = psink
            for h in range(2):
                yb_ref[blk, KV_W * h:KV_W * (h + 1)] = _mm(
                    pswa_ref[blk, 4 * BLOCK * h:4 * BLOCK * (h + 1)],
                    jnp.concatenate([va[2 * h][band], va[2 * h + 1][band]], axis=0))

        gm256 = _group_matrix(XATT_W)
        xq = proj_ref[:, C_XQ:C_XQ + XATT_W]
        qx = xq * lax.rsqrt(_seg_mean(xq * xq, gm256) + EPS) * xqg_ref[...]
        pm = _mem_probs(_mm_nt(qx, km_ref[...]))
        for j in range(4):
            pmem_ref[:, MEM_LEN * j:MEM_LEN * (j + 1)] = pm[j].astype(pmem_ref.dtype)
        yc = _mm(pmem_ref[...], vm_ref[...])
        yc_ref[...] = yc

        def gated(y, g, gate):
            return y * lax.rsqrt(_row_mean(y * y) + EPS) * g * (gate * _sigmoid(gate))

        ogv = og_ref[...]
        za = gated(ya_ref[...], ogv[:, :512], proj_ref[:, C_LRUG:C_LRUG + LRU_W])
        zb = gated(yb_ref[...], ogv[:, 512:768], proj_ref[:, C_SWAG:C_SWAG + SWA_W])
        zc = gated(yc, ogv[:, 768:], proj_ref[:, C_XG:C_XG + XATT_W])
        ycat_ref[:, 0:512] = za.astype(ycat_ref.dtype)
        ycat_ref[:, 512:768] = zb.astype(ycat_ref.dtype)
        ycat_ref[:, 768:1024] = zc.astype(ycat_ref.dtype)
        out = xv + _mm(ycat_ref[...], wout_ref[...])
        err = out - t_ref[...]
        dout_ref[...] = (err * (1.0 / D_MODEL)).astype(dout_ref.dtype)
        lacc_ref[...] = lacc_ref[...] + (0.5 / D_MODEL) * jnp.sum(err * err)

        @pl.when(i == nt - 1)
        def _():
            loss_ref[...] = lacc_ref[...]

    def rows(ncol):
        return pl.BlockSpec((tm, ncol), lambda i: (i, 0))

    in_specs = [rows(D_MODEL), rows(D_MODEL), rows(128), rows(128), rows(128),
                _const_spec((1, D_MODEL)), _const_spec((D_IN, D_MODEL), True), _const_spec((CONV_K, LRU_W)),
                _const_spec((1, LRU_W)), _const_spec((2, 256, 512), True), _const_spec((1, LRU_W)),
                _const_spec((1, LRU_W)), _const_spec((1, LRU_W)), _const_spec((1, 128)), _const_spec((1, 128)),
                _const_spec((1, XATT_W)), pl.BlockSpec(memory_space=pltpu.SMEM),
                _const_spec((4 * MEM_LEN, XATT_W), True), _const_spec((4 * MEM_LEN, XATT_W), True),
                _const_spec((1, D_MODEL)), _const_spec((D_MODEL, D_MODEL), True)]
    out_shape = (jax.ShapeDtypeStruct((seq, D_IN), F32), jax.ShapeDtypeStruct((seq, LRU_W), F32),
                 jax.ShapeDtypeStruct((seq, SWA_W), F32), jax.ShapeDtypeStruct((seq, XATT_W), F32),
                 jax.ShapeDtypeStruct((seq, D_MODEL), _MXU), jax.ShapeDtypeStruct((seq, D_MODEL), _MXU),
                 jax.ShapeDtypeStruct((seq, D_MODEL), _MXU), jax.ShapeDtypeStruct((seq, 4 * 2 * BLOCK), _MXU),
                 jax.ShapeDtypeStruct((seq, 4 * MEM_LEN), _MXU), jax.ShapeDtypeStruct((seq, 128), F32),
                 jax.ShapeDtypeStruct((seq, 4 * LRU_W), _MXU), jax.ShapeDtypeStruct((seq, LRU_W), F32),
                 jax.ShapeDtypeStruct((8, 128), F32))
    out_specs = (rows(D_IN), rows(LRU_W), rows(SWA_W), rows(XATT_W), rows(D_MODEL), rows(D_MODEL), rows(D_MODEL),
                 rows(4 * 2 * BLOCK), rows(4 * MEM_LEN), rows(128), rows(4 * LRU_W), rows(LRU_W),
                 _const_spec((8, 128)))
    scratch = [pltpu.VMEM((tm + 8, LRU_W), F32), pltpu.VMEM((tm, LRU_W), F32),
               pltpu.VMEM((8, LRU_W), F32), pltpu.VMEM((BLOCK, KV_W), F32), pltpu.VMEM((BLOCK, KV_W), F32),
               pltpu.VMEM((8, 128), F32)]
    return pl.pallas_call(
        body, name="layer_fwd", grid=(nt,), out_shape=out_shape, in_specs=in_specs, out_specs=out_specs,
        scratch_shapes=scratch,
        compiler_params=pltpu.CompilerParams(dimension_semantics=("arbitrary",), vmem_limit_bytes=VMEM_LIMIT),
    )(x, tgt, rc, rs1, rs2, ng, win_t, cw, cb, wg, brg, big, lam, qg, kg, xqg, sinks, km, vm, og, wout)


def wgrad_out(ycat, dout):
    seq = ycat.shape[0]

    def body(y_ref, d_ref, o_ref):
        o_ref[...] = _mm_tn(y_ref[...], d_ref[...])

    return pl.pallas_call(
        body, name="wgrad_out", grid=(D_MODEL // 256,), out_shape=jax.ShapeDtypeStruct((D_MODEL, D_MODEL), F32),
        in_specs=[pl.BlockSpec((seq, 256), lambda j: (0, j)), _const_spec((seq, D_MODEL), True)],
        out_specs=pl.BlockSpec((256, D_MODEL), lambda j: (j, 0)),
        compiler_params=pltpu.CompilerParams(dimension_semantics=("arbitrary",), vmem_limit_bytes=VMEM_LIMIT),
    )(ycat, dout)


def wgrad_in_reduce(dproj, xn, bigs, g_small):
    seq = xn.shape[0]
    nblk = D_IN // 256
    nbig = len(bigs)
    stage_at = (0, 1, 6, nblk - 1, nblk - 1)

    def body(d_ref, x_ref, *refs):
        nres = nbig + 1
        _hosted_reduce(pl.program_id(0), stage_at, refs[:nres], refs[nres + 1:2 * nres + 1], refs[2 * nres + 1:], True)
        refs[nres][...] = _mm_tn(d_ref[...], x_ref[...])

    red_shape, scratch = _hosted_reduce_shapes(bigs, g_small)
    vm = pl.BlockSpec(memory_space=pltpu.VMEM)
    hbm = pl.BlockSpec(memory_space=pl.ANY)
    return pl.pallas_call(
        body, name="wgrad_in_reduce", grid=(nblk,),
        out_shape=(jax.ShapeDtypeStruct((D_IN, D_MODEL), F32), *red_shape),
        in_specs=[pl.BlockSpec((seq, 256), lambda j: (0, j)), _const_spec((seq, D_MODEL), True)] + [hbm] * nbig + [vm],
        out_specs=(pl.BlockSpec((256, D_MODEL), lambda j: (j, 0)),) + (hbm,) * len(red_shape),
        scratch_shapes=scratch,
        compiler_params=pltpu.CompilerParams(dimension_semantics=("arbitrary",), vmem_limit_bytes=VMEM_LIMIT),
    )(dproj, xn, *bigs, g_small)


def layer_bwd(x, dout, proj, ya, yb, yc, pswa, pmem, psink, gates, a_all, rc, rs1, rs2, ng, win_t, cw, cb, wg, brg, big, lam, qg, kg, xqg, sinks, km,
              vm, og, wout):
    seq = x.shape[0]
    tm = min(ROW_TILE, seq)
    nt = seq // tm
    nb = tm // BLOCK

    def body(x_ref, dout_ref, proj_ref, ya_ref, yb_ref, yc_ref, pswa_ref, pmem_ref, psink_ref, gates_ref, a_ref,
             c_ref, s1_ref, s2_ref,
             yah_ref, kvh_ref, ch_ref, s1h_ref, s2h_ref,
             ng_ref, win_ref, cw_ref, cb_ref, wg_ref, brg_ref, big_ref, lam_ref, qg_ref, kg_ref, xqg_ref, sink_ref,
             km_ref, vm_ref, og_ref, wout_ref,
             gx_ref, dproj_ref, gwg_ref, dkm_ref, dvm_ref, gng_ref, gog_ref, gcb_ref, gbrg_ref, gbig_ref, glam_ref,
             gcw_ref, gqn_ref, gkn_ref, gxqn_ref, gsink_ref,
             hext_ref, aext_ref, an_scr, dh_scr, g_scr, dxc_ext, gcar_ref, dkcar_ref, dvcar_ref):
        i = pl.program_id(0)
        tile = nt - 1 - i
        first_tile = tile == 0

        @pl.when(i == 0)
        def _():
            for r in (gwg_ref, dkm_ref, dvm_ref, gng_ref, gog_ref, gcb_ref, gbrg_ref, gbig_ref, glam_ref, gcw_ref,
                      gqn_ref, gkn_ref, gxqn_ref, gsink_ref, gcar_ref, dkcar_ref, dvcar_ref):
                r[...] = jnp.zeros_like(r)
            dxc_ext[tm:tm + 8, :] = jnp.zeros((8, LRU_W), F32)
            aext_ref[tm:tm + 8, :] = jnp.zeros((8, LRU_W), F32)

        xv = x_ref[...]
        dov = dout_ref[...]
        dz = _mm_nt(dov, wout_ref[...])
        ogv = og_ref[...]

        def group_bwd(y, gate, g, dzg):
            r = lax.rsqrt(_row_mean(y * y) + EPS)
            n = y * r
            sg = _sigmoid(gate)
            dgate = dzg * (n * g) * (sg * (1.0 + gate * (1.0 - sg)))
            dng = dzg * (gate * sg)
            dn = dng * g
            return r * (dn - n * _row_mean(dn * n)), dgate, _col_sum(dng * n)

        dya, dga, goa = group_bwd(ya_ref[...], proj_ref[:, C_LRUG:C_LRUG + LRU_W], ogv[:, :512], dz[:, :512])
        dyb, dgb, gob = group_bwd(yb_ref[...], proj_ref[:, C_SWAG:C_SWAG + SWA_W], ogv[:, 512:768], dz[:, 512:768])
        dyc, dgc, goc = group_bwd(yc_ref[...], proj_ref[:, C_XG:C_XG + XATT_W], ogv[:, 768:], dz[:, 768:])
        gog_ref[...] += jnp.concatenate([goa, gob, goc], axis=1)
        dproj_ref[:, C_LRUG:C_LRUG + LRU_W] = dga.astype(dproj_ref.dtype)
        dproj_ref[:, C_SWAG:C_SWAG + SWA_W] = dgb.astype(dproj_ref.dtype)
        dproj_ref[:, C_XG:C_XG + XATT_W] = dgc.astype(dproj_ref.dtype)

        gm256 = _group_matrix(XATT_W)
        xq = proj_ref[:, C_XQ:C_XQ + XATT_W]
        rq = lax.rsqrt(_seg_mean(xq * xq, gm256) + EPS)
        qn = xq * rq
        qx = qn * xqg_ref[...]
        qxb = qx.astype(_MXU)
        dycb = dyc.astype(_MXU)
        dp_all = _mm_nt(dycb, vm_ref[...])
        dsm = []
        for j in range(4):
            pj = pmem_ref[:, MEM_LEN * j:MEM_LEN * (j + 1)].astype(F32)
            dp = dp_all[:, MEM_LEN * j:MEM_LEN * (j + 1)]
            dsm.append((pj * (dp - jnp.sum(pj * dp, axis=-1, keepdims=True))).astype(_MXU))
        ds_all = jnp.concatenate(dsm, axis=1)
        dvm_ref[...] += _mm_tn(pmem_ref[...], dycb)
        dkm_ref[...] += _mm_tn(ds_all, qxb)
        dqx = _mm(ds_all, km_ref[...])
        gxqn_ref[...] += _col_sum(dqx * qn)
        dqn = dqx * xqg_ref[...]
        dproj_ref[:, C_XQ:C_XQ + XATT_W] = (rq * (dqn - qn * _seg_mean(dqn * qn, gm256))).astype(dproj_ref.dtype)

        gm128 = _group_matrix(KV_W)
        cv, s1v, s2v = c_ref[...], s1_ref[...], s2_ref[...]

        def head_norm(t):
            r = lax.rsqrt(_seg_mean(t * t, gm128) + EPS)
            return t * r, r

        qn_, qr_ = zip(head_norm(proj_ref[:, C_SQ:C_SQ + 128]), head_norm(proj_ref[:, C_SQ + 128:C_SQ + 256]))
        qrope = [_rope(qn_[h] * qg_ref[...], cv, s1v, s2v).astype(_MXU) for h in range(2)]
        kn, krr = head_norm(proj_ref[:, C_SK:C_SK + KV_W])
        kr = _rope(kn * kg_ref[...], cv, s1v, s2v)
        khn, _ = head_norm(kvh_ref[:, 0:KV_W])
        khr = _rope(khn * kg_ref[...], ch_ref[...], s1h_ref[...], s2h_ref[...])
        ka = _place_kv(jnp.concatenate([khr, kr], axis=0), 0.125)
        va = _place_kv(jnp.concatenate([kvh_ref[:, KV_W:2 * KV_W], proj_ref[:, C_SV:C_SV + KV_W]], axis=0), 1.0)
        lane128 = lax.broadcasted_iota(jnp.int32, (1, 128), 1)
        gsink = jnp.zeros((1, 128), F32)
        dk_band, dv_band, dq_blk = [], [], []
        for b in range(nb):
            band = slice(BLOCK * b, BLOCK * b + 2 * BLOCK)
            blk = slice(BLOCK * b, BLOCK * (b + 1))
            dka, dva, dsb = [], [], []
            deltas = jnp.zeros((BLOCK, 128), F32)
            for j in range(4):
                qh = qrope[j // 2][blk]
                doh = dyb[blk, KV_W * (j // 2):KV_W * (j // 2 + 1)].astype(_MXU)
                pb = pswa_ref[blk, 2 * BLOCK * j:2 * BLOCK * (j + 1)]
                p = pb.astype(F32)
                dp = _mm_nt(doh, va[j][band])
                delta = jnp.sum(p * dp, axis=-1, keepdims=True)
                ds = (p * (dp - delta)).astype(_MXU)
                deltas = jnp.where(lane128 == j, delta, deltas)
                dva.append(_mm_tn(pb, doh))
                dka.append(_mm_tn(ds, qh))
                dsb.append(ds)
            gsink = gsink - _col_sum(psink_ref[blk, :] * deltas)
            dk_band.append(_unplace_kv(dka) * 0.125)
            dv_band.append(_unplace_kv(dva))
            dq_blk.append([_mm(jnp.concatenate(dsb[2 * h:2 * h + 2], axis=1),
                               jnp.concatenate([ka[2 * h][band], ka[2 * h + 1][band]], axis=0)) for h in range(2)])
        gsink_ref[...] += gsink
        dk_rows = [dk_band[b][BLOCK:] + (dk_band[b + 1][:BLOCK] if b + 1 < nb else dkcar_ref[...]) for b in range(nb)]
        dv_rows = [dv_band[b][BLOCK:] + (dv_band[b + 1][:BLOCK] if b + 1 < nb else dvcar_ref[...]) for b in range(nb)]
        dkcar_ref[...] = dk_band[0][:BLOCK]
        dvcar_ref[...] = dv_band[0][:BLOCK]
        dkg = _rope_bwd(jnp.concatenate(dk_rows, axis=0), cv, s1v, s2v)
        gkn = _col_sum(dkg * kn)
        dkn = dkg * kg_ref[...]
        dproj_ref[:, C_SK:C_SK + KV_W] = (krr * (dkn - kn * _seg_mean(dkn * kn, gm128))).astype(dproj_ref.dtype)
        dproj_ref[:, C_SV:C_SV + KV_W] = jnp.concatenate(dv_rows, axis=0).astype(dproj_ref.dtype)
        gqn = jnp.zeros((1, 128), F32)
        for h in range(2):
            dqg = _rope_bwd(jnp.concatenate([dq_blk[b][h] for b in range(nb)], axis=0), cv, s1v, s2v)
            gqn = gqn + _col_sum(dqg * qn_[h])
            dqn_ = dqg * qg_ref[...]
            dproj_ref[:, C_SQ + 128 * h:C_SQ + 128 * (h + 1)] = (
                qr_[h] * (dqn_ - qn_[h] * _seg_mean(dqn_ * qn_[h], gm128))).astype(dproj_ref.dtype)
        gqn_ref[...] += gqn
        gkn_ref[...] += gkn

        u = proj_ref[:, C_LRUX:C_LRUX + LRU_W]
        xc, rg, ig, sq = (gates_ref[:, LRU_W * k:LRU_W * (k + 1)].astype(F32) for k in range(4))
        a = a_ref[...]
        sp = _softplus(-lam_ref[...])
        hext_ref[0:8, :] = jnp.where(first_tile, 0.0, yah_ref[...])
        hext_ref[8:8 + tm, :] = ya_ref[...]
        hprev = hext_ref[pl.ds(7, tm), :]
        aext_ref[0:tm, :] = a
        an_scr[...] = aext_ref[pl.ds(1, tm), :]
        dh_scr[...] = dya
        dh_scr[tm - 1:tm, :] = dh_scr[tm - 1:tm, :] + gcar_ref[0:1, :]
        row8 = lax.broadcasted_iota(jnp.int32, (8, LRU_W), 0)

        def scan_step(gi, carry):
            r0 = pl.multiple_of((tm // 8 - 1 - gi) * 8, 8)
            av = an_scr[pl.ds(r0, 8), :]
            bv = dh_scr[pl.ds(r0, 8), :]
            for d in (1, 2, 4):
                a_sh = jnp.where(row8 < 8 - d, pltpu.roll(av, 8 - d, 0), 1.0)
                b_sh = jnp.where(row8 < 8 - d, pltpu.roll(bv, 8 - d, 0), 0.0)
                bv = bv + av * b_sh
                av = av * a_sh
            gv = bv + av * carry
            g_scr[pl.ds(r0, 8), :] = gv
            return gv[0:1, :]

        g0 = lax.fori_loop(0, tm // 8, scan_step, jnp.zeros((1, LRU_W), F32), unroll=True)
        gcar_ref[0:1, :] = a[0:1, :] * g0
        gv = g_scr[...]
        da = gv * hprev
        dig = gv * sq * xc
        dxc = gv * sq * ig
        dla = da * a - gv * (ig * xc) * ((a * a) / sq)
        drg = dla * ((-LRU_C) * sp)
        glam_ref[...] += _col_sum(dla * rg)
        dpr = drg * rg * (1.0 - rg)
        dpi = dig * ig * (1.0 - ig)
        gbrg_ref[...] += _col_sum(dpr)
        gbig_ref[...] += _col_sum(dpi)
        dpre0 = jnp.concatenate([dpr[:, :256], dpi[:, :256]], axis=1).astype(_MXU)
        dpre1 = jnp.concatenate([dpr[:, 256:], dpi[:, 256:]], axis=1).astype(_MXU)
        gwg_ref[0] += _mm_tn(xc[:, :256], dpre0)
        gwg_ref[1] += _mm_tn(xc[:, 256:], dpre1)
        dxc = dxc + jnp.concatenate([_mm_nt(dpre0, wg_ref[0]), _mm_nt(dpre1, wg_ref[1])], axis=1)
        gcb_ref[...] += _col_sum(dxc)
        dxc_ext[0:tm, :] = dxc
        du = jnp.zeros((tm, LRU_W), F32)
        for k in range(CONV_K):
            later = dxc_ext[pl.ds(3 - k, tm), :]
            gcw_ref[k:k + 1, :] += _col_sum(later * u)
            du = du + cw_ref[k:k + 1, :] * later
        dxc_ext[tm:tm + 8, :] = dxc[0:8, :]
        dproj_ref[:, C_LRUX:C_LRUX + LRU_W] = du.astype(dproj_ref.dtype)

        dxn = _mm(dproj_ref[...], win_ref[...])
        rx = lax.rsqrt(_row_mean(xv * xv) + EPS)
        xh = xv * rx
        gng_ref[...] += _col_sum(dxn * xh)
        dxh = dxn * ng_ref[...]
        gx_ref[...] = dov.astype(F32) + rx * (dxh - xh * _row_mean(dxh * xh))

        @pl.when(i == nt - 1)
        def _():
            glam_ref[...] = glam_ref[...] * (LRU_C * _sigmoid(-lam_ref[...]))

    def rows(ncol, arr_cols_block=0):
        return pl.BlockSpec((tm, ncol), lambda i: (nt - 1 - i, arr_cols_block))

    def halo(nrow, ncol, colblk=0):
        per = tm // nrow
        return pl.BlockSpec((nrow, ncol), lambda i: (jnp.maximum((nt - 1 - i) * per - 1, 0), colblk))

    in_specs = [rows(D_MODEL), rows(D_MODEL), rows(D_IN), rows(LRU_W), rows(SWA_W), rows(XATT_W),
                rows(4 * 2 * BLOCK), rows(4 * MEM_LEN), rows(128), rows(4 * LRU_W), rows(LRU_W),
                rows(128), rows(128), rows(128),
                halo(8, LRU_W), halo(BLOCK, 2 * KV_W, C_SK // (2 * KV_W)),
                halo(BLOCK, 128), halo(BLOCK, 128), halo(BLOCK, 128),
                _const_spec((1, D_MODEL)), _const_spec((D_IN, D_MODEL), True), _const_spec((CONV_K, LRU_W)),
                _const_spec((1, LRU_W)), _const_spec((2, 256, 512), True), _const_spec((1, LRU_W)),
                _const_spec((1, LRU_W)), _const_spec((1, LRU_W)), _const_spec((1, 128)), _const_spec((1, 128)),
                _const_spec((1, XATT_W)), pl.BlockSpec(memory_space=pltpu.SMEM),
                _const_spec((4 * MEM_LEN, XATT_W), True), _const_spec((4 * MEM_LEN, XATT_W), True),
                _const_spec((1, D_MODEL)), _const_spec((D_MODEL, D_MODEL), True)]
    small = [(2, 256, 512), (4 * MEM_LEN, XATT_W), (4 * MEM_LEN, XATT_W), (1, D_MODEL), (1, D_MODEL), (1, LRU_W), (1, LRU_W),
             (1, LRU_W), (1, LRU_W), (CONV_K, LRU_W), (1, 128), (1, 128), (1, XATT_W), (1, 128)]
    out_shape = (jax.ShapeDtypeStruct((seq, D_MODEL), F32), jax.ShapeDtypeStruct((seq, D_IN), _MXU)) + tuple(
        jax.ShapeDtypeStruct(s, F32) for s in small)
    out_specs = (rows(D_MODEL), rows(D_IN)) + tuple(_const_spec(s) for s in small)
    scratch = [pltpu.VMEM((tm + 8, LRU_W), F32), pltpu.VMEM((tm + 8, LRU_W), F32),
               pltpu.VMEM((tm, LRU_W), F32), pltpu.VMEM((tm, LRU_W), F32), pltpu.VMEM((tm, LRU_W), F32),
               pltpu.VMEM((tm + 8, LRU_W), F32),
               pltpu.VMEM((8, LRU_W), F32), pltpu.VMEM((BLOCK, KV_W), F32), pltpu.VMEM((BLOCK, KV_W), F32)]
    return pl.pallas_call(
        body, name="layer_bwd", grid=(nt,), out_shape=out_shape, in_specs=in_specs, out_specs=out_specs,
        scratch_shapes=scratch,
        compiler_params=pltpu.CompilerParams(dimension_semantics=("arbitrary",), vmem_limit_bytes=VMEM_LIMIT),
    )(x, dout, proj, ya, yb, yc, pswa, pmem, psink, gates, a_all, rc, rs1, rs2, ya, proj, rc, rs1, rs2,
      ng, win_t, cw, cb, wg, brg, big, lam, qg, kg, xqg, sinks, km, vm, og, wout)


def _reduce_protocol(big, sm, outs, osm, r1, r1s, wire, r2, r2s, wire2, ps, own, send, recv, lsem):
    nbig = len(big)
    x, y, c = lax.axis_index("x"), lax.axis_index("y"), lax.axis_index("c")
    sibling = (x, y, 1 - c)
    near, far, diag = _partners(x, y, c)
    me, near_id, far_id, diag_id = _chip_of(x, y), _chip_of(*near), _chip_of(*far), _chip_of(*diag)

    def copy(k, src, dst, to):
        return pltpu.make_async_remote_copy(src_ref=src, dst_ref=dst, send_sem=send.at[k], recv_sem=recv.at[k],
                                            device_id=to, device_id_type=MESH)

    def sent(stage):
        cps = []
        for a in range(nbig):
            if stage == 0:
                cps.append(copy(5 * a, big[a].at[:, 1 - c], r1[a], sibling))
            elif stage == 1:
                cps.append(copy(5 * a + 1, wire[a].at[near_id], r2[a].at[0], (*near, c)))
                cps.append(copy(5 * a + 2, wire[a].at[diag_id], r2[a].at[1], (*near, c)))
            elif stage == 2:
                cps.append(copy(5 * a + 3, wire2[a], r2[a].at[2], (*far, c)))
            elif stage == 3:
                cps.append(copy(5 * a + 4, outs[a].at[c], outs[a].at[c], sibling))
        if sm is not None:
            src, dst, to = ((sm.at[1 - c], r1s, sibling), (r1s, r2s.at[0], (*near, c)), (ps, r2s.at[1], (*far, c)),
                            (osm.at[c], osm.at[c], sibling))[stage]
            cps.append(copy(5 * nbig + stage, src, dst, to))
        return cps

    def arrived(k, ref):
        copy(k, ref, ref, sibling).wait_recv()

    def loads():
        return [pltpu.make_async_copy(big[a].at[:, c], own[a], lsem.at[a]) for a in range(nbig)]

    def stage0():
        for cp in sent(0) + loads():
            cp.start()

    def stage1():
        for a in range(nbig):
            loads()[a].wait()
            arrived(5 * a, r1[a])
            for k in range(N_CHIPS):
                r1[a][k] = own[a][k] + r1[a][k]
                wire[a][k] = r1[a][k].astype(wire[a].dtype)
        if sm is not None:
            arrived(5 * nbig, r1s)
            r1s[...] = sm[c] + r1s[...]
        for cp in sent(1):
            cp.start()

    def stage2():
        for a in range(nbig):
            arrived(5 * a + 1, r2[a].at[0])
            arrived(5 * a + 2, r2[a].at[1])
            r1[a][me] = r1[a][me] + r2[a][0].astype(F32)
            wire2[a][...] = (r1[a][far_id] + r2[a][1].astype(F32)).astype(wire2[a].dtype)
        if sm is not None:
            arrived(5 * nbig + 1, r2s.at[0])
            ps[...] = r1s[...] + r2s[0]
        for cp in sent(2):
            cp.start()

    def stage3():
        for a in range(nbig):
            arrived(5 * a + 3, r2[a].at[2])
            outs[a][c] = r1[a][me] + r2[a][2].astype(F32)
        if sm is not None:
            arrived(5 * nbig + 2, r2s.at[1])
            osm[c] = ps[...] + r2s[1]
        for cp in sent(3):
            cp.start()

    def stage4():
        for a in range(nbig):
            arrived(5 * a + 4, outs[a].at[1 - c])
        if sm is not None:
            arrived(5 * nbig + 3, osm.at[1 - c])
        for stage in range(4):
            for cp in sent(stage):
                cp.wait_send()

    return [stage0, stage1, stage2, stage3, stage4]


def _reduce_buffers(bigs, g_small):
    half = [b.shape[2:] for b in bigs]
    sm_half = None if g_small is None else g_small.shape[1:]
    out_shape = [jax.ShapeDtypeStruct((2,) + h, F32) for h in half]
    small = lambda lead: [] if g_small is None else [pltpu.VMEM(lead + sm_half, F32)]
    if g_small is not None:
        out_shape.append(jax.ShapeDtypeStruct(g_small.shape, F32))
    n_sem = 5 * len(bigs) + 4
    scratch = ([pltpu.VMEM((N_CHIPS,) + h, F32) for h in half] + small(())
               + [pltpu.VMEM((N_CHIPS,) + h, _WIRE) for h in half]
               + [pltpu.VMEM((3,) + h, _WIRE) for h in half] + small((2,))
               + [pltpu.VMEM(h, _WIRE) for h in half] + small(())
               + [pltpu.VMEM((N_CHIPS,) + h, F32) for h in half]
               + [pltpu.SemaphoreType.DMA((n_sem,)), pltpu.SemaphoreType.DMA((n_sem,)),
                  pltpu.SemaphoreType.DMA((len(bigs),))])
    return out_shape, scratch


def _split_reduce_refs(refs, nbig, has_small):
    it = iter(refs)
    take = lambda n: [next(it) for _ in range(n)]
    one = lambda: next(it) if has_small else None
    big, sm = take(nbig), one()
    outs, osm = take(nbig), one()
    r1, r1s, wire, r2, r2s, wire2, ps, own = take(nbig), one(), take(nbig), take(nbig), one(), take(nbig), one(), take(nbig)
    send, recv, lsem = take(3)
    return big, sm, outs, osm, r1, r1s, wire, r2, r2s, wire2, ps, own, send, recv, lsem


def _hosted_reduce_shapes(bigs, g_small):
    red_shape, scratch = _reduce_buffers(bigs, g_small)
    nres = len(red_shape)
    return red_shape, [pltpu.VMEM(r.shape, r.dtype) for r in red_shape] + scratch + [pltpu.SemaphoreType.DMA((nres,))]


def _hosted_reduce(step, stage_at, operands, results, scratch, has_small):
    nres = len(results)
    sums, rest, fsem = scratch[:nres], scratch[nres:-1], scratch[-1]
    refs = tuple(operands) + tuple(sums) + tuple(rest)
    for at, stage in zip(stage_at, _reduce_protocol(*_split_reduce_refs(refs, nres - has_small, has_small))):
        pl.when(step == at)(stage)

    @pl.when(step == stage_at[-1])
    def _():
        out = [pltpu.make_async_copy(sums[k], results[k], fsem.at[k]) for k in range(nres)]
        for cp in out:
            cp.start()
        for cp in out:
            cp.wait()


def reduce_grads(bigs, g_small, name):
    nbig = len(bigs)

    def body(*refs):
        for stage in _reduce_protocol(*_split_reduce_refs(refs, nbig, g_small is not None)):
            stage()

    out_shape, scratch = _reduce_buffers(bigs, g_small)
    vm = pl.BlockSpec(memory_space=pltpu.VMEM)
    hbm = pl.BlockSpec(memory_space=pl.ANY)
    operands = list(bigs) + ([] if g_small is None else [g_small])
    return pl.pallas_call(
        body, name=name, out_shape=tuple(out_shape), in_specs=[hbm] * nbig + [vm] * (g_small is not None),
        out_specs=(vm,) * len(out_shape), scratch_shapes=scratch,
        compiler_params=pltpu.CompilerParams(vmem_limit_bytes=VMEM_LIMIT),
    )(*operands)


def adamw(w, g, m, v, name):
    rows_, cols = w.shape
    tr = max(t for t in range(8, rows_ + 1, 8) if rows_ % t == 0 and t * cols * 4 <= ADAM_BLOCK_BYTES)

    def body(w_ref, g_ref, m_ref, v_ref, d_ref, nm_ref, nv_ref):
        d_ref[...], nm_ref[...], nv_ref[...] = _adam_update(w_ref[...], g_ref[...], m_ref[...], v_ref[...])

    spec = pl.BlockSpec((tr, cols), lambda i: (i, 0))
    shp = jax.ShapeDtypeStruct(w.shape, F32)
    return pl.pallas_call(
        body, name=name, grid=(rows_ // tr,), out_shape=(shp, shp, shp), in_specs=[spec] * 4, out_specs=(spec,) * 3,
        compiler_params=pltpu.CompilerParams(dimension_semantics=("arbitrary",)),
    )(w, g, m, v)


def _adam_update(w, g, m, v):
    nm = ADAM_B1 * m + (1.0 - ADAM_B1) * g
    nv = ADAM_B2 * v + (1.0 - ADAM_B2) * (g * g)
    m_hat = nm / (1.0 - ADAM_B1 ** ADAM_STEP)
    v_hat = nv / (1.0 - ADAM_B2 ** ADAM_STEP)
    return (-ADAM_LR) * (m_hat / (jnp.sqrt(v_hat) + ADAM_EPS) + ADAM_WD * w), nm, nv


def adamw_vectors(g_pack, g_mats, ws, ms, vs):
    nvec, nmat = len(SMALL_VECTORS), len(g_mats)
    n = nvec + nmat

    def body(*refs):
        pk = refs[0]
        gm_refs = refs[1:1 + nmat]
        w_refs, m_refs, v_refs = (refs[1 + nmat + k * n:1 + nmat + (k + 1) * n] for k in range(3))
        outs = refs[1 + nmat + 3 * n:]
        g_out, d_out, nm_out, nv_out = outs[:nvec], outs[nvec:nvec + n], outs[nvec + n:nvec + 2 * n], outs[nvec + 2 * n:]
        chip = 2 * lax.axis_index("x") + lax.axis_index("y")
        for k, (name, row, width) in enumerate(SMALL_VECTORS):
            if name == "conv_w":
                g = jnp.concatenate([pk[pl.ds(row + 4 * t + chip, 1), :] for t in range(CONV_K)], axis=0)[None]
            elif width >= 128:
                g = jnp.concatenate([pk[row + r:row + r + 1, :] for r in range(width // 128)], axis=1)
            else:
                g = pk[row:row + 1, 0:width]
            g_out[k][...] = g
            d_out[k][...], nm_out[k][...], nv_out[k][...] = _adam_update(w_refs[k][...], g, m_refs[k][...], v_refs[k][...])
        for k in range(nvec, n):
            d_out[k][...], nm_out[k][...], nv_out[k][...] = _adam_update(
                w_refs[k][...], gm_refs[k - nvec][...], m_refs[k][...], v_refs[k][...])

    vm = pl.BlockSpec(memory_space=pltpu.VMEM)
    like = [jax.ShapeDtypeStruct(w.shape, F32) for w in ws]
    out_shape = like[:nvec] + like * 3
    return pl.pallas_call(
        body, name="adamw_vectors", out_shape=tuple(out_shape), in_specs=[vm] * (1 + nmat + 3 * n),
        out_specs=(vm,) * len(out_shape),
    )(g_pack, *g_mats, *ws, *ms, *vs)


SMALL_VECTORS = (("norm_g", 0, 1024), ("mem_norm_g", 8, 1024), ("conv_w", 16, 512), ("conv_b", 32, 512),
                 ("b_rg", 292, 512), ("b_ig", 552, 512), ("lru_lambda", 556, 512), ("q_norm_g", 560, 64),
                 ("k_norm_g", 561, 64), ("sinks", 562, 4), ("xq_norm_g", 563, 64), ("xk_norm_g", 564, 64),
                 ("out_norm_g", 565, 1024))
SMALL_MATRICES = (("w_rg", 36), ("w_ig", 296))
LOSS_ROW = 573
SMALL_ROWS = 576


def _pack(parts, rows_):
    flat = jnp.concatenate([p.reshape(-1) for p in parts])
    return jnp.pad(flat, (0, rows_ * 128 - flat.shape[0])).reshape(rows_, 128)


def _pad_to(v, n):
    v = v.reshape(-1)
    return jnp.pad(v, (0, n - v.shape[0]))


def _block_diag_gates(w_rg, w_ig):
    eye = jnp.eye(4, dtype=w_rg.dtype)

    def bd(w4):
        return (w4[:, :, None, :] * eye[:, None, :, None]).reshape(256, 256)

    return jnp.stack([jnp.concatenate([bd(w_rg[4 * h:4 * h + 4]), bd(w_ig[4 * h:4 * h + 4])], axis=1) for h in (0, 1)])


def _diag_blocks(g):
    g6 = g.reshape(2, 4, HEAD, 2, 4, HEAD)
    d = (g6 * jnp.eye(4, dtype=g.dtype)[None, :, None, None, :, None]).sum(axis=4)
    return d[:, :, :, 0].reshape(8, HEAD, HEAD), d[:, :, :, 1].reshape(8, HEAD, HEAD)


def _rope_tables(seq):
    pos = np.arange(seq, dtype=np.float32)
    inv_freq = (np.float32(ROPE_THETA) ** (-(np.arange(0, ROPE_DIM, 2, dtype=np.float32) / np.float32(ROPE_DIM)))
                ).astype(np.float32)
    ang = (pos[:, None] * inv_freq[None, :]).astype(np.float32)
    cos, sin = np.cos(ang).astype(np.float32), np.sin(ang).astype(np.float32)
    z = lambda n: np.zeros((seq, n), np.float32)
    c64 = np.concatenate([cos, cos, np.ones((seq, HEAD - ROPE_DIM), np.float32)], axis=1)
    s1_64 = np.concatenate([-sin, z(HEAD - 8)], axis=1)
    s2_64 = np.concatenate([z(8), sin, z(HEAD - ROPE_DIM)], axis=1)
    return tuple(jnp.asarray(np.concatenate([t, t], axis=1)) for t in (c64, s1_64, s2_64))


def kernel(x, mem, norm_g, mem_norm_g, w_in, conv_w, conv_b, w_rg, b_rg, w_ig, b_ig, lru_lambda, q_norm_g, k_norm_g, sinks, w_mem_kv, xq_norm_g, xk_norm_g, out_norm_g, w_out, loss_target, m_norm_g, m_mem_norm_g, m_w_in, m_conv_w, m_conv_b, m_w_rg, m_b_rg, m_w_ig, m_b_ig, m_lru_lambda, m_q_norm_g, m_k_norm_g, m_sinks, m_w_mem_kv, m_xq_norm_g, m_xk_norm_g, m_out_norm_g, m_w_out, v_norm_g, v_mem_norm_g, v_w_in, v_conv_w, v_conv_b, v_w_rg, v_b_rg, v_w_ig, v_b_ig, v_lru_lambda, v_q_norm_g, v_k_norm_g, v_sinks, v_w_mem_kv, v_xq_norm_g, v_xk_norm_g, v_out_norm_g, v_w_out):
    seq = x.shape[1]
    xs, tgt, mems = x[0], loss_target[0], mem[0]

    win_t_sh = w_in[0].T.astype(_MXU)
    cw_sh = jnp.pad(conv_w[0], ((0, 4), (0, 0)))
    win_t, wout, wkv, cw_all = gather_weights(win_t_sh, w_out[0].astype(_MXU), w_mem_kv[0].astype(_MXU), cw_sh)
    cw = cw_all.reshape(N_CHIPS, 8, 128)[:, :CONV_K].transpose(1, 0, 2).reshape(CONV_K, LRU_W)

    rc, rs1, rs2 = _rope_tables(seq)
    wg = _block_diag_gates(w_rg[0], w_ig[0]).astype(_MXU)
    qg = jnp.tile(q_norm_g, (1, 2))
    kg = jnp.tile(k_norm_g, (1, 2))
    xqg = jnp.tile(xq_norm_g, (1, 4))
    xkg = jnp.tile(xk_norm_g, (1, 4))

    km, vm = mem_fwd(mems, mem_norm_g, wkv, xkg)
    proj, ya, yb, yc, ycat, xn, dout, pswa, pmem, psink, gates, a_all, loss8 = layer_fwd(
        xs, tgt, rc, rs1, rs2, norm_g, win_t, cw, conv_b, wg, b_rg, b_ig, lru_lambda, qg, kg, xqg, sinks, km, vm,
        out_norm_g, wout)
    g_wout = wgrad_out(ycat, dout)
    (gx, dproj, g_wg, dkm, dvm, g_ng, g_og, g_cb, g_brg, g_big, g_lam, g_cw, g_qn, g_kn, g_xqn, g_sink) = layer_bwd(
        xs, dout, proj, ya, yb, yc, pswa, pmem, psink, gates, a_all, rc, rs1, rs2, norm_g, win_t, cw, conv_b, wg, b_rg,
        b_ig, lru_lambda, qg, kg, xqg, sinks, km, vm, out_norm_g, wout)
    g_wkv, g_mng, g_xkn = mem_bwd(mems, mem_norm_g, wkv, xkg, dkm, dvm)

    g_wrg, g_wig = _diag_blocks(g_wg)
    fold = lambda v, n: v.reshape(n, HEAD).sum(axis=0)
    small_g = _pack([g_ng, g_mng, g_cw, g_cb, g_wrg, g_brg, g_wig, g_big, g_lam, _pad_to(fold(g_qn, 2), 128),
                     _pad_to(fold(g_kn, 2), 128), g_sink, _pad_to(fold(g_xqn, 4), 128), _pad_to(fold(g_xkn, 4), 128),
                     g_og, loss8[0:1]], SMALL_ROWS)
    early = [g_wout.reshape(N_CHIPS, 2, D_MODEL // 8, D_MODEL), g_wkv.reshape(N_CHIPS, 2, D_MODEL // 8, 2 * XATT_W)]
    g_win_t, r_out, r_kv, r_small = wgrad_in_reduce(dproj, xn, early, small_g.reshape(2, SMALL_ROWS // 2, 128))
    (r_in,) = reduce_grads([g_win_t.reshape(N_CHIPS, 2, D_IN // 8, D_MODEL)], None, "reduce_w_in")

    r_small = r_small.reshape(SMALL_ROWS, 128)
    loss = r_small[LOSS_ROW, 0]
    grads = {"w_in": r_in.reshape(D_IN // 4, D_MODEL).T[None], "w_mem_kv": r_kv.reshape(D_MODEL // 4, 2 * XATT_W)[None],
             "w_out": r_out.reshape(D_MODEL // 4, D_MODEL)[None]}
    for name, row in SMALL_MATRICES:
        grads[name] = r_small[row:row + 256].reshape(1, LRU_BLOCKS, HEAD, HEAD)
    weights = dict(norm_g=norm_g, mem_norm_g=mem_norm_g, w_in=w_in, conv_w=conv_w, conv_b=conv_b, w_rg=w_rg, b_rg=b_rg,
                   w_ig=w_ig, b_ig=b_ig, lru_lambda=lru_lambda, q_norm_g=q_norm_g, k_norm_g=k_norm_g, sinks=sinks,
                   w_mem_kv=w_mem_kv, xq_norm_g=xq_norm_g, xk_norm_g=xk_norm_g, out_norm_g=out_norm_g, w_out=w_out)
    ms = dict(norm_g=m_norm_g, mem_norm_g=m_mem_norm_g, w_in=m_w_in, conv_w=m_conv_w, conv_b=m_conv_b, w_rg=m_w_rg,
              b_rg=m_b_rg, w_ig=m_w_ig, b_ig=m_b_ig, lru_lambda=m_lru_lambda, q_norm_g=m_q_norm_g, k_norm_g=m_k_norm_g,
              sinks=m_sinks, w_mem_kv=m_w_mem_kv, xq_norm_g=m_xq_norm_g, xk_norm_g=m_xk_norm_g,
              out_norm_g=m_out_norm_g, w_out=m_w_out)
    vs = dict(norm_g=v_norm_g, mem_norm_g=v_mem_norm_g, w_in=v_w_in, conv_w=v_conv_w, conv_b=v_conv_b, w_rg=v_w_rg,
              b_rg=v_b_rg, w_ig=v_w_ig, b_ig=v_b_ig, lru_lambda=v_lru_lambda, q_norm_g=v_q_norm_g, k_norm_g=v_k_norm_g,
              sinks=v_sinks, w_mem_kv=v_w_mem_kv, xq_norm_g=v_xq_norm_g, xk_norm_g=v_xk_norm_g,
              out_norm_g=v_out_norm_g, w_out=v_w_out)

    delta, new_m, new_v = {}, {}, {}
    d2, m2, v2 = adamw(w_in[0].T, r_in.reshape(D_IN // 4, D_MODEL), m_w_in[0].T, v_w_in[0].T, "adamw_w_in")
    delta["w_in"], new_m["w_in"], new_v["w_in"] = d2.T[None], m2.T[None], v2.T[None]
    for name in ("w_mem_kv", "w_out"):
        shp = weights[name].shape
        d2, m2, v2 = adamw(weights[name][0], grads[name][0], ms[name][0], vs[name][0], "adamw_" + name)
        delta[name], new_m[name], new_v[name] = d2.reshape(shp), m2.reshape(shp), v2.reshape(shp)
    vec_names = [n for n, _, _ in SMALL_VECTORS]
    small_names = vec_names + [n for n, _ in SMALL_MATRICES]
    res = adamw_vectors(r_small, [grads[n] for n, _ in SMALL_MATRICES], [weights[n] for n in small_names],
                        [ms[n] for n in small_names], [vs[n] for n in small_names])
    nvec, nall = len(vec_names), len(small_names)
    grads.update(zip(vec_names, res[:nvec]))
    delta.update(zip(small_names, res[nvec:nvec + nall]))
    new_m.update(zip(small_names, res[nvec + nall:nvec + 2 * nall]))
    new_v.update(zip(small_names, res[nvec + 2 * nall:]))

    order = ("norm_g", "mem_norm_g", "w_in", "conv_w", "conv_b", "w_rg", "b_rg", "w_ig", "b_ig", "lru_lambda",
             "q_norm_g", "k_norm_g", "sinks", "w_mem_kv", "xq_norm_g", "xk_norm_g", "out_norm_g", "w_out")
    return (loss, gx[None], *[grads[n] for n in order], *[delta[n] for n in order], *[new_m[n] for n in order],
            *[new_v[n] for n in order])
```

```python
import jax
import jax.numpy as jnp
import numpy as np
from jax import lax
from jax.experimental import pallas as pl
from jax.experimental.pallas import tpu as pltpu

F32 = jnp.float32
_MXU = jnp.bfloat16
_WIRE = jnp.bfloat16

D_MODEL = 1024
MEM_LEN = 256
HEAD = 64
LRU_W = 512
LRU_BLOCKS = 8
CONV_K = 4
LRU_C = 8.0
SWA_W = 256
KV_W = 128
XATT_W = 256
BLOCK = 128
D_IN = 2304
ROPE_THETA = 500000.0
ROPE_DIM = 16
EPS = 1e-6
NEG_INF = -1e30
C_LRUX, C_LRUG, C_SQ, C_SK, C_SV, C_SWAG, C_XQ, C_XG = 0, 512, 1024, 1280, 1408, 1536, 1792, 2048

ADAM_LR, ADAM_B1, ADAM_B2, ADAM_EPS, ADAM_WD, ADAM_STEP = 0.001, 0.9, 0.999, 1e-08, 0.01, 10

N_CHIPS = 4
ROW_TILE = 256
VMEM_LIMIT = 56 * 1024 * 1024
ADAM_BLOCK_BYTES = 1280 * 1024
GATHER_PIECES = (1, 1, 1)
MESH = pl.DeviceIdType.MESH


def _mm(a, b):
    return jnp.dot(a.astype(_MXU), b.astype(_MXU), preferred_element_type=F32)


def _mm_nt(a, b):
    return lax.dot_general(a.astype(_MXU), b.astype(_MXU), (((1,), (1,)), ((), ())), preferred_element_type=F32)


def _mm_tn(a, b):
    return lax.dot_general(a.astype(_MXU), b.astype(_MXU), (((0,), (0,)), ((), ())), preferred_element_type=F32)


def _group_matrix(width):
    r = lax.shift_right_logical(lax.broadcasted_iota(jnp.int32, (width, width), 0), 6)
    c = lax.shift_right_logical(lax.broadcasted_iota(jnp.int32, (width, width), 1), 6)
    return (r == c).astype(_MXU)


def _seg_mean(x, gm):
    return jnp.dot(x.astype(_MXU), gm, preferred_element_type=F32) * (1.0 / HEAD)


def _row_mean(x):
    return jnp.mean(x, axis=-1, keepdims=True)


def _col_sum(x):
    return jnp.sum(x, axis=0, keepdims=True)


def _sigmoid(x):
    return jax.nn.sigmoid(x)


def _softplus(z):
    e = jnp.exp(-jnp.abs(z))
    u = 1.0 + e
    log1p_e = jnp.where(u == 1.0, e, jnp.log(u) * (e / (u - 1.0)))
    return jnp.maximum(z, 0.0) + log1p_e


def _rope(t, c, s1, s2):
    return t * c + pltpu.roll(t, 120, 1) * s1 + pltpu.roll(t, 8, 1) * s2


def _rope_bwd(d, c, s1, s2):
    return d * c + pltpu.roll(d * s1, 8, 1) + pltpu.roll(d * s2, 120, 1)


def _lane_mask(width, lo, hi):
    lane = lax.broadcasted_iota(jnp.int32, (1, width), 1)
    return ((lane >= lo) & (lane < hi)).astype(F32)


def _swa_mask(first_block):
    qi = lax.broadcasted_iota(jnp.int32, (BLOCK, 2 * BLOCK), 0)
    kj = lax.broadcasted_iota(jnp.int32, (BLOCK, 2 * BLOCK), 1)
    rel = qi + BLOCK - kj
    ok = (rel >= 0) & (rel < BLOCK)
    return ok & (jnp.logical_not(first_block) | (kj >= BLOCK))


def _place_kv(t, scale):
    lo = t * (_lane_mask(KV_W, 0, HEAD) * scale)
    hi = t * (_lane_mask(KV_W, HEAD, KV_W) * scale)
    return [a.astype(_MXU) for a in (lo, pltpu.roll(lo, HEAD, 1), pltpu.roll(hi, HEAD, 1), hi)]


def _unplace_kv(d):
    return (_lane_mask(KV_W, 0, HEAD) * (d[0] + pltpu.roll(d[1], HEAD, 1))
            + _lane_mask(KV_W, HEAD, KV_W) * (d[3] + pltpu.roll(d[2], HEAD, 1)))


def _swa_probs(qh, ka, mask, sink):
    s = _mm_nt(qh, ka)
    s = jnp.where(mask, s, NEG_INF)
    m = jnp.maximum(jnp.max(s, axis=-1, keepdims=True), sink)
    p = jnp.exp(s - m)
    esink = jnp.exp(sink - m)
    inv = 1.0 / (jnp.sum(p, axis=-1, keepdims=True) + esink)
    return p * inv, esink * inv


def _mem_probs(s_all):
    out = []
    for j in range(4):
        s = s_all[:, MEM_LEN * j:MEM_LEN * (j + 1)]
        p = jnp.exp(s - jnp.max(s, axis=-1, keepdims=True))
        out.append(p * (1.0 / jnp.sum(p, axis=-1, keepdims=True)))
    return out


def _head_rows(t, scale):
    return jnp.concatenate([t * (_lane_mask(XATT_W, HEAD * j, HEAD * (j + 1)) * scale) for j in range(4)], axis=0)


def _lru_gates(xc, wg_ref, brg, big, lam):
    p0 = _mm(xc[:, :256], wg_ref[0])
    p1 = _mm(xc[:, 256:], wg_ref[1])
    rg = _sigmoid(jnp.concatenate([p0[:, :256], p1[:, :256]], axis=1) + brg)
    ig = _sigmoid(jnp.concatenate([p0[:, 256:], p1[:, 256:]], axis=1) + big)
    sp = _softplus(-lam)
    la = (-LRU_C) * rg * sp
    a = jnp.exp(la)
    th = jnp.tanh(la)
    one_minus_a2 = (-2.0 * th) / (1.0 - th)
    return rg, ig, sp, a, jnp.sqrt(one_minus_a2)


def _const_spec(shape, single=False):
    zeros = (0,) * len(shape)
    if single:
        return pl.BlockSpec(shape, lambda i: zeros, pipeline_mode=pl.Buffered(1))
    return pl.BlockSpec(shape, lambda i: zeros)


def _chip_of(x, y):
    return 2 * x + y


def _partners(x, y, c):
    north = c == 1
    near = (jnp.where(north, 1 - x, x), jnp.where(north, y, 1 - y))
    far = (jnp.where(north, x, 1 - x), jnp.where(north, 1 - y, y))
    return near, far, (1 - x, 1 - y)


def gather_weights(win_t, wout, wkv, convw):
    arrs = (win_t, wout, wkv)
    n = len(arrs)
    pieces = []
    for a, arr in enumerate(arrs):
        half = arr.shape[0] // 2
        step = half // GATHER_PIECES[a]
        pieces += [(a, off, step) for off in range(0, half, step)]
    npc = len(pieces)

    def body(a0, a1, a2, cw, o0, o1, o2, ocw, send, recv, lsem):
        ins, outs = (a0, a1, a2), (o0, o1, o2)
        x, y, c = lax.axis_index("x"), lax.axis_index("y"), lax.axis_index("c")
        sibling = (x, y, 1 - c)
        near, far, diag = _partners(x, y, c)
        chips = [near, far, diag]
        me = _chip_of(x, y)

        def landed(p, chip, half):
            a, off, rows_ = pieces[p]
            r = ins[a].shape[0]
            return outs[a].at[pl.ds(pl.multiple_of(chip * r + half * (r // 2) + off, 16), rows_)]

        def mine(p):
            a, off, rows_ = pieces[p]
            return ins[a].at[pl.ds(pl.multiple_of(c * (ins[a].shape[0] // 2) + off, 16), rows_)]

        def copy(k, src, dst, to):
            return pltpu.make_async_remote_copy(src_ref=src, dst_ref=dst, send_sem=send.at[k], recv_sem=recv.at[k],
                                                device_id=to, device_id_type=MESH)

        def cw_rows(chip):
            return ocw.at[pl.ds(pl.multiple_of(chip * 8, 8), 8)]

        locals_ = []
        for a in range(n):
            r = ins[a].shape[0]
            locals_.append(pltpu.make_async_copy(ins[a], outs[a].at[pl.ds(pl.multiple_of(me * r, 16), r)], lsem.at[a]))
        locals_.append(pltpu.make_async_copy(cw, cw_rows(me), lsem.at[n]))
        for cp in locals_:
            cp.start()

        sent = []
        for p in range(npc):
            for j in range(2):
                sent.append(copy(p * 6 + j, mine(p), landed(p, me, c), (*chips[j], c)))
        for j, chip in enumerate(chips):
            sent.append(copy(npc * 6 + j, cw, cw_rows(me), (*chip, c)))
        for cp in sent:
            cp.start()
        for p in range(npc):
            for j in range(3):
                got = landed(p, _chip_of(*chips[j]), c)
                copy(p * 6 + j, got, got, sibling).wait_recv()
                if j == 0:
                    sent.append(copy(p * 6 + 2, got, got, (*far, c)))
                    sent[-1].start()
                sent.append(copy(p * 6 + 3 + j, got, got, sibling))
                sent[-1].start()
        for p in range(npc):
            for j in range(3):
                got = landed(p, _chip_of(*chips[(1, 0, 2)[j]]), 1 - c)
                copy(p * 6 + 3 + j, got, got, sibling).wait_recv()
        for j, chip in enumerate(chips):
            got = cw_rows(_chip_of(*chip))
            copy(npc * 6 + j, got, got, (*chip, c)).wait_recv()
        for cp in sent:
            cp.wait_send()
        for cp in locals_:
            cp.wait()

    vm = pl.BlockSpec(memory_space=pltpu.VMEM)
    out_shape = tuple(jax.ShapeDtypeStruct((N_CHIPS * a.shape[0],) + a.shape[1:], a.dtype) for a in arrs) + (
        jax.ShapeDtypeStruct((N_CHIPS * 8, 128), F32),)
    n_rdma = npc * 6 + 3
    return pl.pallas_call(
        body, name="gather_weights", out_shape=out_shape,
        in_specs=[vm] * 4, out_specs=(vm,) * 4,
        scratch_shapes=[pltpu.SemaphoreType.DMA((n_rdma,)), pltpu.SemaphoreType.DMA((n_rdma,)),
                        pltpu.SemaphoreType.DMA((n + 1,))],
        compiler_params=pltpu.CompilerParams(vmem_limit_bytes=VMEM_LIMIT),
    )(win_t, wout, wkv, convw)


def mem_fwd(mem, mem_g, wkv, xk_g):
    def body(mem_ref, g_ref, w_ref, xk_ref, km_ref, vm_ref):
        mem_v = mem_ref[...]
        mn = mem_v * lax.rsqrt(_row_mean(mem_v * mem_v) + EPS) * g_ref[...]
        mkv = _mm(mn, w_ref[...])
        kpre = mkv[:, :XATT_W]
        gm = _group_matrix(XATT_W)
        km = kpre * lax.rsqrt(_seg_mean(kpre * kpre, gm) + EPS) * xk_ref[...]
        km_ref[...] = _head_rows(km, 0.125).astype(km_ref.dtype)
        vm_ref[...] = _head_rows(mkv[:, XATT_W:], 1.0).astype(vm_ref.dtype)

    vm = pl.BlockSpec(memory_space=pltpu.VMEM)
    rows_shape = jax.ShapeDtypeStruct((4 * MEM_LEN, XATT_W), _MXU)
    return pl.pallas_call(
        body, name="mem_fwd", out_shape=(rows_shape, rows_shape), in_specs=[vm] * 4, out_specs=(vm, vm),
    )(mem, mem_g, wkv, xk_g)


def mem_bwd(mem, mem_g, wkv, xk_g, dkm, dvm):
    def body(mem_ref, g_ref, w_ref, xk_ref, dkm_ref, dvm_ref, gw_ref, gg_ref, gxk_ref):
        mem_v = mem_ref[...]
        mh = mem_v * lax.rsqrt(_row_mean(mem_v * mem_v) + EPS)
        mn = mh * g_ref[...]
        mkv = _mm(mn, w_ref[...])
        kpre = mkv[:, :XATT_W]
        gm = _group_matrix(XATT_W)
        rk = lax.rsqrt(_seg_mean(kpre * kpre, gm) + EPS)
        kn = kpre * rk
        dk = jnp.zeros((MEM_LEN, XATT_W), F32)
        dv = jnp.zeros((MEM_LEN, XATT_W), F32)
        for j in range(4):
            mj = _lane_mask(XATT_W, HEAD * j, HEAD * (j + 1))
            dk = dk + dkm_ref[:, MEM_LEN * j:MEM_LEN * (j + 1)].T * (mj * 0.125)
            dv = dv + dvm_ref[:, MEM_LEN * j:MEM_LEN * (j + 1)].T * mj
        gxk_ref[...] = _col_sum(dk * kn)
        dkn = dk * xk_ref[...]
        dkpre = rk * (dkn - kn * _seg_mean(dkn * kn, gm))
        dmkv = jnp.concatenate([dkpre, dv], axis=1)
        gw_ref[...] = _mm_tn(mn, dmkv)
        dmn = _mm_nt(dmkv, w_ref[...])
        gg_ref[...] = _col_sum(dmn * mh)

    vm = pl.BlockSpec(memory_space=pltpu.VMEM)
    return pl.pallas_call(
        body, name="mem_bwd",
        out_shape=(jax.ShapeDtypeStruct((D_MODEL, 2 * XATT_W), F32), jax.ShapeDtypeStruct((1, D_MODEL), F32),
                   jax.ShapeDtypeStruct((1, XATT_W), F32)),
        in_specs=[vm] * 6, out_specs=(vm, vm, vm),
    )(mem, mem_g, wkv, xk_g, dkm, dvm)


def layer_fwd(x, tgt, rc, rs1, rs2, ng, win_t, cw, cb, wg, brg, big, lam, qg, kg, xqg, sinks, km, vm, og, wout):
    seq = x.shape[0]
    tm = min(ROW_TILE, seq)
    nt = seq // tm
    nb = tm // BLOCK

    def body(x_ref, t_ref, c_ref, s1_ref, s2_ref, ng_ref, win_ref, cw_ref, cb_ref, wg_ref, brg_ref, big_ref, lam_ref,
             qg_ref, kg_ref, xqg_ref, sink_ref, km_ref, vm_ref, og_ref, wout_ref,
             proj_ref, ya_ref, yb_ref, yc_ref, ycat_ref, xn_ref, dout_ref, pswa_ref, pmem_ref, psink_ref, gates_ref,
             a_ref, loss_ref,
             ext_ref, b_scr, hc_ref, kp_ref, vp_ref, lacc_ref):
        i = pl.program_id(0)

        @pl.when(i == 0)
        def _():
            ext_ref[0:8, :] = jnp.zeros((8, LRU_W), F32)
            hc_ref[...] = jnp.zeros_like(hc_ref)
            kp_ref[...] = jnp.zeros_like(kp_ref)
            vp_ref[...] = jnp.zeros_like(vp_ref)
            lacc_ref[...] = jnp.zeros_like(lacc_ref)

        xv = x_ref[...]
        xn = (xv * lax.rsqrt(_row_mean(xv * xv) + EPS) * ng_ref[...]).astype(_MXU)
        xn_ref[...] = xn.astype(xn_ref.dtype)
        proj_ref[...] = _mm_nt(xn, win_ref[...])

        u = proj_ref[:, C_LRUX:C_LRUX + LRU_W]
        ext_ref[8:8 + tm, :] = u
        xc = cb_ref[...]
        for k in range(CONV_K):
            xc = xc + cw_ref[k:k + 1, :] * ext_ref[pl.ds(5 + k, tm), :]
        ext_ref[0:8, :] = u[tm - 8:tm, :]
        rg, ig, sp, a, sq = _lru_gates(xc, wg_ref, brg_ref[...], big_ref[...], lam_ref[...])
        for k, t in enumerate((xc, rg, ig, sq)):
            gates_ref[:, LRU_W * k:LRU_W * (k + 1)] = t.astype(gates_ref.dtype)
        a_ref[...] = a
        b_scr[...] = sq * (ig * xc)
        row8 = lax.broadcasted_iota(jnp.int32, (8, LRU_W), 0)

        def scan_step(g, carry):
            r0 = pl.multiple_of(g * 8, 8)
            av = a_ref[pl.ds(r0, 8), :]
            bv = b_scr[pl.ds(r0, 8), :]
            for d in (1, 2, 4):
                a_sh = jnp.where(row8 >= d, pltpu.roll(av, d, 0), 1.0)
                b_sh = jnp.where(row8 >= d, pltpu.roll(bv, d, 0), 0.0)
                bv = bv + av * b_sh
                av = av * a_sh
            hv = bv + av * carry
            ya_ref[pl.ds(r0, 8), :] = hv
            return hv[7:8, :]

        hc_ref[0:1, :] = lax.fori_loop(0, tm // 8, scan_step, hc_ref[0:1, :], unroll=True)

        gm128 = _group_matrix(KV_W)
        cv, s1v, s2v = c_ref[...], s1_ref[...], s2_ref[...]

        def head_norm_rope(t, g):
            n = t * lax.rsqrt(_seg_mean(t * t, gm128) + EPS)
            return _rope(n * g, cv, s1v, s2v)

        qs_ = (head_norm_rope(proj_ref[:, C_SQ:C_SQ + 128], qg_ref[...]).astype(_MXU),
               head_norm_rope(proj_ref[:, C_SQ + 128:C_SQ + 256], qg_ref[...]).astype(_MXU))
        kr = head_norm_rope(proj_ref[:, C_SK:C_SK + KV_W], kg_ref[...])
        sv = proj_ref[:, C_SV:C_SV + KV_W]
        ka = _place_kv(jnp.concatenate([kp_ref[...], kr], axis=0), 0.125)
        va = _place_kv(jnp.concatenate([vp_ref[...], sv], axis=0), 1.0)
        kp_ref[...] = kr[tm - BLOCK:tm, :]
        vp_ref[...] = sv[tm - BLOCK:tm, :]
        lane128 = lax.broadcasted_iota(jnp.int32, (1, 128), 1)
        for b in range(nb):
            mask = _swa_mask((i == 0) & (b == 0)) if b == 0 else _swa_mask(False)
            band = slice(BLOCK * b, BLOCK * b + 2 * BLOCK)
            blk = slice(BLOCK * b, BLOCK * (b + 1))
            psink = jnp.zeros((BLOCK, 128), F32)
            for j in range(4):
                p, pk = _swa_probs(qs_[j // 2][blk], ka[j][band], mask, sink_ref[0, j])
                pswa_ref[blk, 2 * BLOCK * j:2 * BLOCK * (j + 1)] = p.astype(pswa_ref.dtype)
                psink = jnp.where(lane128 == j, pk, psink)
            psink_ref[blk, :] = psink
            for h in range(2):
                yb_ref[blk, KV_W * h:KV_W * (h + 1)] = _mm(
                    pswa_ref[blk, 4 * BLOCK * h:4 * BLOCK * (h + 1)],
                    jnp.concatenate([va[2 * h][band], va[2 * h + 1][band]], axis=0))

        gm256 = _group_matrix(XATT_W)
        xq = proj_ref[:, C_XQ:C_XQ + XATT_W]
        qx = xq * lax.rsqrt(_seg_mean(xq * xq, gm256) + EPS) * xqg_ref[...]
        pm = _mem_probs(_mm_nt(qx, km_ref[...]))
        for j in range(4):
            pmem_ref[:, MEM_LEN * j:MEM_LEN * (j + 1)] = pm[j].astype(pmem_ref.dtype)
        yc = _mm(pmem_ref[...], vm_ref[...])
        yc_ref[...] = yc

        def gated(y, g, gate):
            return y * lax.rsqrt(_row_mean(y * y) + EPS) * g * (gate * _sigmoid(gate))

        ogv = og_ref[...]
        za = gated(ya_ref[...], ogv[:, :512], proj_ref[:, C_LRUG:C_LRUG + LRU_W])
        zb = gated(yb_ref[...], ogv[:, 512:768], proj_ref[:, C_SWAG:C_SWAG + SWA_W])
        zc = gated(yc, ogv[:, 768:], proj_ref[:, C_XG:C_XG + XATT_W])
        ycat_ref[:, 0:512] = za.astype(ycat_ref.dtype)
        ycat_ref[:, 512:768] = zb.astype(ycat_ref.dtype)
        ycat_ref[:, 768:1024] = zc.astype(ycat_ref.dtype)
        out = xv + _mm(ycat_ref[...], wout_ref[...])
        err = out - t_ref[...]
        dout_ref[...] = (err * (1.0 / D_MODEL)).astype(dout_ref.dtype)
        lacc_ref[...] = lacc_ref[...] + (0.5 / D_MODEL) * jnp.sum(err * err)

        @pl.when(i == nt - 1)
        def _():
            loss_ref[...] = lacc_ref[...]

    def rows(ncol):
        return pl.BlockSpec((tm, ncol), lambda i: (i, 0))

    in_specs = [rows(D_MODEL), rows(D_MODEL), rows(128), rows(128), rows(128),
                _const_spec((1, D_MODEL)), _const_spec((D_IN, D_MODEL), True), _const_spec((CONV_K, LRU_W)),
                _const_spec((1, LRU_W)), _const_spec((2, 256, 512), True), _const_spec((1, LRU_W)),
                _const_spec((1, LRU_W)), _const_spec((1, LRU_W)), _const_spec((1, 128)), _const_spec((1, 128)),
                _const_spec((1, XATT_W)), pl.BlockSpec(memory_space=pltpu.SMEM),
                _const_spec((4 * MEM_LEN, XATT_W), True), _const_spec((4 * MEM_LEN, XATT_W), True),
                _const_spec((1, D_MODEL)), _const_spec((D_MODEL, D_MODEL), True)]
    out_shape = (jax.ShapeDtypeStruct((seq, D_IN), F32), jax.ShapeDtypeStruct((seq, LRU_W), F32),
                 jax.ShapeDtypeStruct((seq, SWA_W), F32), jax.ShapeDtypeStruct((seq, XATT_W), F32),
                 jax.ShapeDtypeStruct((seq, D_MODEL), _MXU), jax.ShapeDtypeStruct((seq, D_MODEL), _MXU),
                 jax.ShapeDtypeStruct((seq, D_MODEL), _MXU), jax.ShapeDtypeStruct((seq, 4 * 2 * BLOCK), _MXU),
                 jax.ShapeDtypeStruct((seq, 4 * MEM_LEN), _MXU), jax.ShapeDtypeStruct((seq, 128), F32),
                 jax.ShapeDtypeStruct((seq, 4 * LRU_W), _MXU), jax.ShapeDtypeStruct((seq, LRU_W), F32),
                 jax.ShapeDtypeStruct((8, 128), F32))
    out_specs = (rows(D_IN), rows(LRU_W), rows(SWA_W), rows(XATT_W), rows(D_MODEL), rows(D_MODEL), rows(D_MODEL),
                 rows(4 * 2 * BLOCK), rows(4 * MEM_LEN), rows(128), rows(4 * LRU_W), rows(LRU_W),
                 _const_spec((8, 128)))
    scratch = [pltpu.VMEM((tm + 8, LRU_W), F32), pltpu.VMEM((tm, LRU_W), F32),
               pltpu.VMEM((8, LRU_W), F32), pltpu.VMEM((BLOCK, KV_W), F32), pltpu.VMEM((BLOCK, KV_W), F32),
               pltpu.VMEM((8, 128), F32)]
    return pl.pallas_call(
        body, name="layer_fwd", grid=(nt,), out_shape=out_shape, in_specs=in_specs, out_specs=out_specs,
        scratch_shapes=scratch,
        compiler_params=pltpu.CompilerParams(dimension_semantics=("arbitrary",), vmem_limit_bytes=VMEM_LIMIT),
    )(x, tgt, rc, rs1, rs2, ng, win_t, cw, cb, wg, brg, big, lam, qg, kg, xqg, sinks, km, vm, og, wout)


def wgrad_reduce(lhs, rhs, bigs, g_small, stage_at, name):
    seq, ncol = rhs.shape
    nblk = lhs.shape[1] // 256
    nres = len(bigs) + (g_small is not None)

    def body(l_ref, r_ref, *refs):
        _hosted_reduce(pl.program_id(0), stage_at, refs[:nres], refs[nres + 1:2 * nres + 1], refs[2 * nres + 1:],
                       g_small is not None)
        refs[nres][...] = _mm_tn(l_ref[...], r_ref[...])

    red_shape, scratch = _hosted_reduce_shapes(bigs, g_small)
    vm = pl.BlockSpec(memory_space=pltpu.VMEM)
    hbm = pl.BlockSpec(memory_space=pl.ANY)
    operands = list(bigs) + ([] if g_small is None else [g_small])
    return pl.pallas_call(
        body, name=name, grid=(nblk,),
        out_shape=(jax.ShapeDtypeStruct((lhs.shape[1], ncol), F32), *red_shape),
        in_specs=[pl.BlockSpec((seq, 256), lambda j: (0, j)), _const_spec((seq, ncol), True)] + [hbm] * len(bigs)
        + [vm] * (g_small is not None),
        out_specs=(pl.BlockSpec((256, ncol), lambda j: (j, 0)),) + (hbm,) * len(red_shape),
        scratch_shapes=scratch,
        compiler_params=pltpu.CompilerParams(dimension_semantics=("arbitrary",), vmem_limit_bytes=VMEM_LIMIT),
    )(lhs, rhs, *operands)


def layer_bwd(x, dout, proj, ya, yb, yc, pswa, pmem, psink, gates, a_all, rc, rs1, rs2, ng, win_t, cw, cb, wg, brg, big, lam, qg, kg, xqg, sinks, km,
              vm, og, wout):
    seq = x.shape[0]
    tm = min(ROW_TILE, seq)
    nt = seq // tm
    nb = tm // BLOCK

    def body(x_ref, dout_ref, proj_ref, ya_ref, yb_ref, yc_ref, pswa_ref, pmem_ref, psink_ref, gates_ref, a_ref,
             c_ref, s1_ref, s2_ref,
             yah_ref, kvh_ref, ch_ref, s1h_ref, s2h_ref,
             ng_ref, win_ref, cw_ref, cb_ref, wg_ref, brg_ref, big_ref, lam_ref, qg_ref, kg_ref, xqg_ref, sink_ref,
             km_ref, vm_ref, og_ref, wout_ref,
             gx_ref, dproj_ref, gwg_ref, dkm_ref, dvm_ref, gng_ref, gog_ref, gcb_ref, gbrg_ref, gbig_ref, glam_ref,
             gcw_ref, gqn_ref, gkn_ref, gxqn_ref, gsink_ref,
             hext_ref, aext_ref, an_scr, dh_scr, g_scr, dxc_ext, gcar_ref, dkcar_ref, dvcar_ref):
        i = pl.program_id(0)
        tile = nt - 1 - i
        first_tile = tile == 0

        @pl.when(i == 0)
        def _():
            for r in (gwg_ref, dkm_ref, dvm_ref, gng_ref, gog_ref, gcb_ref, gbrg_ref, gbig_ref, glam_ref, gcw_ref,
                      gqn_ref, gkn_ref, gxqn_ref, gsink_ref, gcar_ref, dkcar_ref, dvcar_ref):
                r[...] = jnp.zeros_like(r)
            dxc_ext[tm:tm + 8, :] = jnp.zeros((8, LRU_W), F32)
            aext_ref[tm:tm + 8, :] = jnp.zeros((8, LRU_W), F32)

        xv = x_ref[...]
        dov = dout_ref[...]
        dz = _mm_nt(dov, wout_ref[...])
        ogv = og_ref[...]

        def group_bwd(y, gate, g, dzg):
            r = lax.rsqrt(_row_mean(y * y) + EPS)
            n = y * r
            sg = _sigmoid(gate)
            dgate = dzg * (n * g) * (sg * (1.0 + gate * (1.0 - sg)))
            dng = dzg * (gate * sg)
            dn = dng * g
            return r * (dn - n * _row_mean(dn * n)), dgate, _col_sum(dng * n)

        dya, dga, goa = group_bwd(ya_ref[...], proj_ref[:, C_LRUG:C_LRUG + LRU_W], ogv[:, :512], dz[:, :512])
        dyb, dgb, gob = group_bwd(yb_ref[...], proj_ref[:, C_SWAG:C_SWAG + SWA_W], ogv[:, 512:768], dz[:, 512:768])
        dyc, dgc, goc = group_bwd(yc_ref[...], proj_ref[:, C_XG:C_XG + XATT_W], ogv[:, 768:], dz[:, 768:])
        gog_ref[...] += jnp.concatenate([goa, gob, goc], axis=1)
        dproj_ref[:, C_LRUG:C_LRUG + LRU_W] = dga.astype(dproj_ref.dtype)
        dproj_ref[:, C_SWAG:C_SWAG + SWA_W] = dgb.astype(dproj_ref.dtype)
        dproj_ref[:, C_XG:C_XG + XATT_W] = dgc.astype(dproj_ref.dtype)

        gm256 = _group_matrix(XATT_W)
        xq = proj_ref[:, C_XQ:C_XQ + XATT_W]
        rq = lax.rsqrt(_seg_mean(xq * xq, gm256) + EPS)
        qn = xq * rq
        qx = qn * xqg_ref[...]
        qxb = qx.astype(_MXU)
        dycb = dyc.astype(_MXU)
        dp_all = _mm_nt(dycb, vm_ref[...])
        dsm = []
        for j in range(4):
            pj = pmem_ref[:, MEM_LEN * j:MEM_LEN * (j + 1)].astype(F32)
            dp = dp_all[:, MEM_LEN * j:MEM_LEN * (j + 1)]
            dsm.append((pj * (dp - jnp.sum(pj * dp, axis=-1, keepdims=True))).astype(_MXU))
        ds_all = jnp.concatenate(dsm, axis=1)
        dvm_ref[...] += _mm_tn(dycb, pmem_ref[...])
        dkm_ref[...] += _mm_tn(qxb, ds_all)
        dqx = _mm(ds_all, km_ref[...])
        gxqn_ref[...] += _col_sum(dqx * qn)
        dqn = dqx * xqg_ref[...]
        dproj_ref[:, C_XQ:C_XQ + XATT_W] = (rq * (dqn - qn * _seg_mean(dqn * qn, gm256))).astype(dproj_ref.dtype)

        gm128 = _group_matrix(KV_W)
        cv, s1v, s2v = c_ref[...], s1_ref[...], s2_ref[...]

        def head_norm(t):
            r = lax.rsqrt(_seg_mean(t * t, gm128) + EPS)
            return t * r, r

        qn_, qr_ = zip(head_norm(proj_ref[:, C_SQ:C_SQ + 128]), head_norm(proj_ref[:, C_SQ + 128:C_SQ + 256]))
        qrope = [_rope(qn_[h] * qg_ref[...], cv, s1v, s2v).astype(_MXU) for h in range(2)]
        kn, krr = head_norm(proj_ref[:, C_SK:C_SK + KV_W])
        kr = _rope(kn * kg_ref[...], cv, s1v, s2v)
        khn, _ = head_norm(kvh_ref[:, 0:KV_W])
        khr = _rope(khn * kg_ref[...], ch_ref[...], s1h_ref[...], s2h_ref[...])
        ka = _place_kv(jnp.concatenate([khr, kr], axis=0), 0.125)
        va = _place_kv(jnp.concatenate([kvh_ref[:, KV_W:2 * KV_W], proj_ref[:, C_SV:C_SV + KV_W]], axis=0), 1.0)
        lane128 = lax.broadcasted_iota(jnp.int32, (1, 128), 1)
        gsink = jnp.zeros((1, 128), F32)
        dk_band, dv_band, dq_blk = [], [], []
        for b in range(nb):
            band = slice(BLOCK * b, BLOCK * b + 2 * BLOCK)
            blk = slice(BLOCK * b, BLOCK * (b + 1))
            dka, dva, dsb = [], [], []
            deltas = jnp.zeros((BLOCK, 128), F32)
            for j in range(4):
                qh = qrope[j // 2][blk]
                doh = dyb[blk, KV_W * (j // 2):KV_W * (j // 2 + 1)].astype(_MXU)
                pb = pswa_ref[blk, 2 * BLOCK * j:2 * BLOCK * (j + 1)]
                p = pb.astype(F32)
                dp = _mm_nt(doh, va[j][band])
                delta = jnp.sum(p * dp, axis=-1, keepdims=True)
                ds = (p * (dp - delta)).astype(_MXU)
                deltas = jnp.where(lane128 == j, delta, deltas)
                dva.append(_mm_tn(pb, doh))
                dka.append(_mm_tn(ds, qh))
                dsb.append(ds)
            gsink = gsink - _col_sum(psink_ref[blk, :] * deltas)
            dk_band.append(_unplace_kv(dka) * 0.125)
            dv_band.append(_unplace_kv(dva))
            dq_blk.append([_mm(jnp.concatenate(dsb[2 * h:2 * h + 2], axis=1),
                               jnp.concatenate([ka[2 * h][band], ka[2 * h + 1][band]], axis=0)) for h in range(2)])
        gsink_ref[...] += gsink
        dk_rows = [dk_band[b][BLOCK:] + (dk_band[b + 1][:BLOCK] if b + 1 < nb else dkcar_ref[...]) for b in range(nb)]
        dv_rows = [dv_band[b][BLOCK:] + (dv_band[b + 1][:BLOCK] if b + 1 < nb else dvcar_ref[...]) for b in range(nb)]
        dkcar_ref[...] = dk_band[0][:BLOCK]
        dvcar_ref[...] = dv_band[0][:BLOCK]
        dkg = _rope_bwd(jnp.concatenate(dk_rows, axis=0), cv, s1v, s2v)
        gkn = _col_sum(dkg * kn)
        dkn = dkg * kg_ref[...]
        dproj_ref[:, C_SK:C_SK + KV_W] = (krr * (dkn - kn * _seg_mean(dkn * kn, gm128))).astype(dproj_ref.dtype)
        dproj_ref[:, C_SV:C_SV + KV_W] = jnp.concatenate(dv_rows, axis=0).astype(dproj_ref.dtype)
        gqn = jnp.zeros((1, 128), F32)
        for h in range(2):
            dqg = _rope_bwd(jnp.concatenate([dq_blk[b][h] for b in range(nb)], axis=0), cv, s1v, s2v)
            gqn = gqn + _col_sum(dqg * qn_[h])
            dqn_ = dqg * qg_ref[...]
            dproj_ref[:, C_SQ + 128 * h:C_SQ + 128 * (h + 1)] = (
                qr_[h] * (dqn_ - qn_[h] * _seg_mean(dqn_ * qn_[h], gm128))).astype(dproj_ref.dtype)
        gqn_ref[...] += gqn
        gkn_ref[...] += gkn

        u = proj_ref[:, C_LRUX:C_LRUX + LRU_W]
        xc, rg, ig, sq = (gates_ref[:, LRU_W * k:LRU_W * (k + 1)].astype(F32) for k in range(4))
        a = a_ref[...]
        sp = _softplus(-lam_ref[...])
        hext_ref[0:8, :] = jnp.where(first_tile, 0.0, yah_ref[...])
        hext_ref[8:8 + tm, :] = ya_ref[...]
        hprev = hext_ref[pl.ds(7, tm), :]
        aext_ref[0:tm, :] = a
        an_scr[...] = aext_ref[pl.ds(1, tm), :]
        dh_scr[...] = dya
        dh_scr[tm - 1:tm, :] = dh_scr[tm - 1:tm, :] + gcar_ref[0:1, :]
        row8 = lax.broadcasted_iota(jnp.int32, (8, LRU_W), 0)

        def scan_step(gi, carry):
            r0 = pl.multiple_of((tm // 8 - 1 - gi) * 8, 8)
            av = an_scr[pl.ds(r0, 8), :]
            bv = dh_scr[pl.ds(r0, 8), :]
            for d in (1, 2, 4):
                a_sh = jnp.where(row8 < 8 - d, pltpu.roll(av, 8 - d, 0), 1.0)
                b_sh = jnp.where(row8 < 8 - d, pltpu.roll(bv, 8 - d, 0), 0.0)
                bv = bv + av * b_sh
                av = av * a_sh
            gv = bv + av * carry
            g_scr[pl.ds(r0, 8), :] = gv
            return gv[0:1, :]

        g0 = lax.fori_loop(0, tm // 8, scan_step, jnp.zeros((1, LRU_W), F32), unroll=True)
        gcar_ref[0:1, :] = a[0:1, :] * g0
        gv = g_scr[...]
        da = gv * hprev
        dig = gv * sq * xc
        dxc = gv * sq * ig
        dla = da * a - gv * (ig * xc) * ((a * a) / sq)
        drg = dla * ((-LRU_C) * sp)
        glam_ref[...] += _col_sum(dla * rg)
        dpr = drg * rg * (1.0 - rg)
        dpi = dig * ig * (1.0 - ig)
        gbrg_ref[...] += _col_sum(dpr)
        gbig_ref[...] += _col_sum(dpi)
        dpre0 = jnp.concatenate([dpr[:, :256], dpi[:, :256]], axis=1).astype(_MXU)
        dpre1 = jnp.concatenate([dpr[:, 256:], dpi[:, 256:]], axis=1).astype(_MXU)
        gwg_ref[0] += _mm_tn(xc[:, :256], dpre0)
        gwg_ref[1] += _mm_tn(xc[:, 256:], dpre1)
        dxc = dxc + jnp.concatenate([_mm_nt(dpre0, wg_ref[0]), _mm_nt(dpre1, wg_ref[1])], axis=1)
        gcb_ref[...] += _col_sum(dxc)
        dxc_ext[0:tm, :] = dxc
        du = jnp.zeros((tm, LRU_W), F32)
        for k in range(CONV_K):
            later = dxc_ext[pl.ds(3 - k, tm), :]
            gcw_ref[k:k + 1, :] += _col_sum(later * u)
            du = du + cw_ref[k:k + 1, :] * later
        dxc_ext[tm:tm + 8, :] = dxc[0:8, :]
        dproj_ref[:, C_LRUX:C_LRUX + LRU_W] = du.astype(dproj_ref.dtype)

        dxn = _mm(dproj_ref[...], win_ref[...])
        rx = lax.rsqrt(_row_mean(xv * xv) + EPS)
        xh = xv * rx
        gng_ref[...] += _col_sum(dxn * xh)
        dxh = dxn * ng_ref[...]
        gx_ref[...] = dov.astype(F32) + rx * (dxh - xh * _row_mean(dxh * xh))

        @pl.when(i == nt - 1)
        def _():
            glam_ref[...] = glam_ref[...] * (LRU_C * _sigmoid(-lam_ref[...]))

    def rows(ncol, arr_cols_block=0):
        return pl.BlockSpec((tm, ncol), lambda i: (nt - 1 - i, arr_cols_block))

    def halo(nrow, ncol, colblk=0):
        per = tm // nrow
        return pl.BlockSpec((nrow, ncol), lambda i: (jnp.maximum((nt - 1 - i) * per - 1, 0), colblk))

    in_specs = [rows(D_MODEL), rows(D_MODEL), rows(D_IN), rows(LRU_W), rows(SWA_W), rows(XATT_W),
                rows(4 * 2 * BLOCK), rows(4 * MEM_LEN), rows(128), rows(4 * LRU_W), rows(LRU_W),
                rows(128), rows(128), rows(128),
                halo(8, LRU_W), halo(BLOCK, 2 * KV_W, C_SK // (2 * KV_W)),
                halo(BLOCK, 128), halo(BLOCK, 128), halo(BLOCK, 128),
                _const_spec((1, D_MODEL)), _const_spec((D_IN, D_MODEL), True), _const_spec((CONV_K, LRU_W)),
                _const_spec((1, LRU_W)), _const_spec((2, 256, 512), True), _const_spec((1, LRU_W)),
                _const_spec((1, LRU_W)), _const_spec((1, LRU_W)), _const_spec((1, 128)), _const_spec((1, 128)),
                _const_spec((1, XATT_W)), pl.BlockSpec(memory_space=pltpu.SMEM),
                _const_spec((4 * MEM_LEN, XATT_W), True), _const_spec((4 * MEM_LEN, XATT_W), True),
                _const_spec((1, D_MODEL)), _const_spec((D_MODEL, D_MODEL), True)]
    small = [(2, 256, 512), (XATT_W, 4 * MEM_LEN), (XATT_W, 4 * MEM_LEN), (1, D_MODEL), (1, D_MODEL), (1, LRU_W), (1, LRU_W),
             (1, LRU_W), (1, LRU_W), (CONV_K, LRU_W), (1, 128), (1, 128), (1, XATT_W), (1, 128)]
    out_shape = (jax.ShapeDtypeStruct((seq, D_MODEL), F32), jax.ShapeDtypeStruct((seq, D_IN), _MXU)) + tuple(
        jax.ShapeDtypeStruct(s, F32) for s in small)
    out_specs = (rows(D_MODEL), rows(D_IN)) + tuple(_const_spec(s) for s in small)
    scratch = [pltpu.VMEM((tm + 8, LRU_W), F32), pltpu.VMEM((tm + 8, LRU_W), F32),
               pltpu.VMEM((tm, LRU_W), F32), pltpu.VMEM((tm, LRU_W), F32), pltpu.VMEM((tm, LRU_W), F32),
               pltpu.VMEM((tm + 8, LRU_W), F32),
               pltpu.VMEM((8, LRU_W), F32), pltpu.VMEM((BLOCK, KV_W), F32), pltpu.VMEM((BLOCK, KV_W), F32)]
    return pl.pallas_call(
        body, name="layer_bwd", grid=(nt,), out_shape=out_shape, in_specs=in_specs, out_specs=out_specs,
        scratch_shapes=scratch,
        compiler_params=pltpu.CompilerParams(dimension_semantics=("arbitrary",), vmem_limit_bytes=VMEM_LIMIT),
    )(x, dout, proj, ya, yb, yc, pswa, pmem, psink, gates, a_all, rc, rs1, rs2, ya, proj, rc, rs1, rs2,
      ng, win_t, cw, cb, wg, brg, big, lam, qg, kg, xqg, sinks, km, vm, og, wout)


def _reduce_protocol(big, sm, outs, osm, r1, r1s, wire, r2, r2s, wire2, ps, own, send, recv, lsem):
    nbig = len(big)
    x, y, c = lax.axis_index("x"), lax.axis_index("y"), lax.axis_index("c")
    sibling = (x, y, 1 - c)
    near, far, diag = _partners(x, y, c)
    me, near_id, far_id, diag_id = _chip_of(x, y), _chip_of(*near), _chip_of(*far), _chip_of(*diag)

    def copy(k, src, dst, to):
        return pltpu.make_async_remote_copy(src_ref=src, dst_ref=dst, send_sem=send.at[k], recv_sem=recv.at[k],
                                            device_id=to, device_id_type=MESH)

    def sent(stage):
        cps = []
        for a in range(nbig):
            if stage == 0:
                cps.append(copy(5 * a, big[a].at[:, 1 - c], r1[a], sibling))
            elif stage == 1:
                cps.append(copy(5 * a + 1, wire[a].at[near_id], r2[a].at[0], (*near, c)))
                cps.append(copy(5 * a + 2, wire[a].at[diag_id], r2[a].at[1], (*near, c)))
            elif stage == 2:
                cps.append(copy(5 * a + 3, wire2[a], r2[a].at[2], (*far, c)))
            elif stage == 3:
                cps.append(copy(5 * a + 4, outs[a].at[c], outs[a].at[c], sibling))
        if sm is not None:
            src, dst, to = ((sm.at[1 - c], r1s, sibling), (r1s, r2s.at[0], (*near, c)), (ps, r2s.at[1], (*far, c)),
                            (osm.at[c], osm.at[c], sibling))[stage]
            cps.append(copy(5 * nbig + stage, src, dst, to))
        return cps

    def arrived(k, ref):
        copy(k, ref, ref, sibling).wait_recv()

    def loads():
        return [pltpu.make_async_copy(big[a].at[:, c], own[a], lsem.at[a]) for a in range(nbig)]

    def stage0():
        for cp in sent(0) + loads():
            cp.start()

    def stage1():
        for a in range(nbig):
            loads()[a].wait()
            arrived(5 * a, r1[a])
            for k in range(N_CHIPS):
                r1[a][k] = own[a][k] + r1[a][k]
                wire[a][k] = r1[a][k].astype(wire[a].dtype)
        if sm is not None:
            arrived(5 * nbig, r1s)
            r1s[...] = sm[c] + r1s[...]
        for cp in sent(1):
            cp.start()

    def stage2():
        for a in range(nbig):
            arrived(5 * a + 1, r2[a].at[0])
            arrived(5 * a + 2, r2[a].at[1])
            r1[a][me] = r1[a][me] + r2[a][0].astype(F32)
            wire2[a][...] = (r1[a][far_id] + r2[a][1].astype(F32)).astype(wire2[a].dtype)
        if sm is not None:
            arrived(5 * nbig + 1, r2s.at[0])
            ps[...] = r1s[...] + r2s[0]
        for cp in sent(2):
            cp.start()

    def stage3():
        for a in range(nbig):
            arrived(5 * a + 3, r2[a].at[2])
            outs[a][c] = r1[a][me] + r2[a][2].astype(F32)
        if sm is not None:
            arrived(5 * nbig + 2, r2s.at[1])
            osm[c] = ps[...] + r2s[1]
        for cp in sent(3):
            cp.start()

    def stage4():
        for a in range(nbig):
            arrived(5 * a + 4, outs[a].at[1 - c])
        if sm is not None:
            arrived(5 * nbig + 3, osm.at[1 - c])
        for stage in range(4):
            for cp in sent(stage):
                cp.wait_send()

    return [stage0, stage1, stage2, stage3, stage4]


def _reduce_buffers(bigs, g_small):
    half = [b.shape[2:] for b in bigs]
    sm_half = None if g_small is None else g_small.shape[1:]
    out_shape = [jax.ShapeDtypeStruct((2,) + h, F32) for h in half]
    small = lambda lead: [] if g_small is None else [pltpu.VMEM(lead + sm_half, F32)]
    if g_small is not None:
        out_shape.append(jax.ShapeDtypeStruct(g_small.shape, F32))
    n_sem = 5 * len(bigs) + 4
    scratch = ([pltpu.VMEM((N_CHIPS,) + h, F32) for h in half] + small(())
               + [pltpu.VMEM((N_CHIPS,) + h, _WIRE) for h in half]
               + [pltpu.VMEM((3,) + h, _WIRE) for h in half] + small((2,))
               + [pltpu.VMEM(h, _WIRE) for h in half] + small(())
               + [pltpu.VMEM((N_CHIPS,) + h, F32) for h in half]
               + [pltpu.SemaphoreType.DMA((n_sem,)), pltpu.SemaphoreType.DMA((n_sem,)),
                  pltpu.SemaphoreType.DMA((len(bigs),))])
    return out_shape, scratch


def _split_reduce_refs(refs, nbig, has_small):
    it = iter(refs)
    take = lambda n: [next(it) for _ in range(n)]
    one = lambda: next(it) if has_small else None
    big, sm = take(nbig), one()
    outs, osm = take(nbig), one()
    r1, r1s, wire, r2, r2s, wire2, ps, own = take(nbig), one(), take(nbig), take(nbig), one(), take(nbig), one(), take(nbig)
    send, recv, lsem = take(3)
    return big, sm, outs, osm, r1, r1s, wire, r2, r2s, wire2, ps, own, send, recv, lsem


def _hosted_reduce_shapes(bigs, g_small):
    red_shape, scratch = _reduce_buffers(bigs, g_small)
    nres = len(red_shape)
    return red_shape, [pltpu.VMEM(r.shape, r.dtype) for r in red_shape] + scratch + [pltpu.SemaphoreType.DMA((nres,))]


def _hosted_reduce(step, stage_at, operands, results, scratch, has_small):
    nres = len(results)
    sums, rest, fsem = scratch[:nres], scratch[nres:-1], scratch[-1]
    refs = tuple(operands) + tuple(sums) + tuple(rest)
    for at, stage in zip(stage_at, _reduce_protocol(*_split_reduce_refs(refs, nres - has_small, has_small))):
        pl.when(step == at)(stage)

    @pl.when(step == stage_at[-1])
    def _():
        out = [pltpu.make_async_copy(sums[k], results[k], fsem.at[k]) for k in range(nres)]
        for cp in out:
            cp.start()
        for cp in out:
            cp.wait()


def reduce_grads(bigs, g_small, name):
    nbig = len(bigs)

    def body(*refs):
        for stage in _reduce_protocol(*_split_reduce_refs(refs, nbig, g_small is not None)):
            stage()

    out_shape, scratch = _reduce_buffers(bigs, g_small)
    vm = pl.BlockSpec(memory_space=pltpu.VMEM)
    hbm = pl.BlockSpec(memory_space=pl.ANY)
    operands = list(bigs) + ([] if g_small is None else [g_small])
    return pl.pallas_call(
        body, name=name, out_shape=tuple(out_shape), in_specs=[hbm] * nbig + [vm] * (g_small is not None),
        out_specs=(vm,) * len(out_shape), scratch_shapes=scratch,
        compiler_params=pltpu.CompilerParams(vmem_limit_bytes=VMEM_LIMIT),
    )(*operands)


def adamw(w, g, m, v, name):
    rows_, cols = w.shape
    tr = max(t for t in range(8, rows_ + 1, 8) if rows_ % t == 0 and t * cols * 4 <= ADAM_BLOCK_BYTES)

    def body(w_ref, g_ref, m_ref, v_ref, d_ref, nm_ref, nv_ref):
        d_ref[...], nm_ref[...], nv_ref[...] = _adam_update(w_ref[...], g_ref[...], m_ref[...], v_ref[...])

    spec = pl.BlockSpec((tr, cols), lambda i: (i, 0))
    shp = jax.ShapeDtypeStruct(w.shape, F32)
    return pl.pallas_call(
        body, name=name, grid=(rows_ // tr,), out_shape=(shp, shp, shp), in_specs=[spec] * 4, out_specs=(spec,) * 3,
        compiler_params=pltpu.CompilerParams(dimension_semantics=("arbitrary",)),
    )(w, g, m, v)


def _adam_update(w, g, m, v):
    nm = ADAM_B1 * m + (1.0 - ADAM_B1) * g
    nv = ADAM_B2 * v + (1.0 - ADAM_B2) * (g * g)
    m_hat = nm / (1.0 - ADAM_B1 ** ADAM_STEP)
    v_hat = nv / (1.0 - ADAM_B2 ** ADAM_STEP)
    return (-ADAM_LR) * (m_hat / (jnp.sqrt(v_hat) + ADAM_EPS) + ADAM_WD * w), nm, nv


def adamw_vectors(g_pack, g_mats, ws, ms, vs):
    nvec, nmat = len(SMALL_VECTORS), len(g_mats)
    n = nvec + nmat

    def body(*refs):
        pk = refs[0]
        gm_refs = refs[1:1 + nmat]
        w_refs, m_refs, v_refs = (refs[1 + nmat + k * n:1 + nmat + (k + 1) * n] for k in range(3))
        outs = refs[1 + nmat + 3 * n:]
        g_out, d_out, nm_out, nv_out = outs[:nvec], outs[nvec:nvec + n], outs[nvec + n:nvec + 2 * n], outs[nvec + 2 * n:]
        chip = 2 * lax.axis_index("x") + lax.axis_index("y")
        for k, (name, row, width) in enumerate(SMALL_VECTORS):
            if name == "conv_w":
                g = jnp.concatenate([pk[pl.ds(row + 4 * t + chip, 1), :] for t in range(CONV_K)], axis=0)[None]
            elif width >= 128:
                g = jnp.concatenate([pk[row + r:row + r + 1, :] for r in range(width // 128)], axis=1)
            else:
                g = pk[row:row + 1, 0:width]
            g_out[k][...] = g
            d_out[k][...], nm_out[k][...], nv_out[k][...] = _adam_update(w_refs[k][...], g, m_refs[k][...], v_refs[k][...])
        for k in range(nvec, n):
            d_out[k][...], nm_out[k][...], nv_out[k][...] = _adam_update(
                w_refs[k][...], gm_refs[k - nvec][...], m_refs[k][...], v_refs[k][...])

    vm = pl.BlockSpec(memory_space=pltpu.VMEM)
    like = [jax.ShapeDtypeStruct(w.shape, F32) for w in ws]
    out_shape = like[:nvec] + like * 3
    return pl.pallas_call(
        body, name="adamw_vectors", out_shape=tuple(out_shape), in_specs=[vm] * (1 + nmat + 3 * n),
        out_specs=(vm,) * len(out_shape),
    )(g_pack, *g_mats, *ws, *ms, *vs)


SMALL_VECTORS = (("norm_g", 0, 1024), ("mem_norm_g", 8, 1024), ("conv_w", 16, 512), ("conv_b", 32, 512),
                 ("b_rg", 292, 512), ("b_ig", 552, 512), ("lru_lambda", 556, 512), ("q_norm_g", 560, 64),
                 ("k_norm_g", 561, 64), ("sinks", 562, 4), ("xq_norm_g", 563, 64), ("xk_norm_g", 564, 64),
                 ("out_norm_g", 565, 1024))
SMALL_MATRICES = (("w_rg", 36), ("w_ig", 296))
LOSS_ROW = 573
SMALL_ROWS = 576


def _pack(parts, rows_):
    flat = jnp.concatenate([p.reshape(-1) for p in parts])
    return jnp.pad(flat, (0, rows_ * 128 - flat.shape[0])).reshape(rows_, 128)


def _pad_to(v, n):
    v = v.reshape(-1)
    return jnp.pad(v, (0, n - v.shape[0]))


def _block_diag_gates(w_rg, w_ig):
    eye = jnp.eye(4, dtype=w_rg.dtype)

    def bd(w4):
        return (w4[:, :, None, :] * eye[:, None, :, None]).reshape(256, 256)

    return jnp.stack([jnp.concatenate([bd(w_rg[4 * h:4 * h + 4]), bd(w_ig[4 * h:4 * h + 4])], axis=1) for h in (0, 1)])


def _diag_blocks(g):
    g6 = g.reshape(2, 4, HEAD, 2, 4, HEAD)
    d = (g6 * jnp.eye(4, dtype=g.dtype)[None, :, None, None, :, None]).sum(axis=4)
    return d[:, :, :, 0].reshape(8, HEAD, HEAD), d[:, :, :, 1].reshape(8, HEAD, HEAD)


def _rope_tables(seq):
    pos = np.arange(seq, dtype=np.float32)
    inv_freq = (np.float32(ROPE_THETA) ** (-(np.arange(0, ROPE_DIM, 2, dtype=np.float32) / np.float32(ROPE_DIM)))
                ).astype(np.float32)
    ang = (pos[:, None] * inv_freq[None, :]).astype(np.float32)
    cos, sin = np.cos(ang).astype(np.float32), np.sin(ang).astype(np.float32)
    z = lambda n: np.zeros((seq, n), np.float32)
    c64 = np.concatenate([cos, cos, np.ones((seq, HEAD - ROPE_DIM), np.float32)], axis=1)
    s1_64 = np.concatenate([-sin, z(HEAD - 8)], axis=1)
    s2_64 = np.concatenate([z(8), sin, z(HEAD - ROPE_DIM)], axis=1)
    return tuple(jnp.asarray(np.concatenate([t, t], axis=1)) for t in (c64, s1_64, s2_64))


def kernel(x, mem, norm_g, mem_norm_g, w_in, conv_w, conv_b, w_rg, b_rg, w_ig, b_ig, lru_lambda, q_norm_g, k_norm_g, sinks, w_mem_kv, xq_norm_g, xk_norm_g, out_norm_g, w_out, loss_target, m_norm_g, m_mem_norm_g, m_w_in, m_conv_w, m_conv_b, m_w_rg, m_b_rg, m_w_ig, m_b_ig, m_lru_lambda, m_q_norm_g, m_k_norm_g, m_sinks, m_w_mem_kv, m_xq_norm_g, m_xk_norm_g, m_out_norm_g, m_w_out, v_norm_g, v_mem_norm_g, v_w_in, v_conv_w, v_conv_b, v_w_rg, v_b_rg, v_w_ig, v_b_ig, v_lru_lambda, v_q_norm_g, v_k_norm_g, v_sinks, v_w_mem_kv, v_xq_norm_g, v_xk_norm_g, v_out_norm_g, v_w_out):
    seq = x.shape[1]
    xs, tgt, mems = x[0], loss_target[0], mem[0]

    win_t_sh = w_in[0].T.astype(_MXU)
    cw_sh = jnp.pad(conv_w[0], ((0, 4), (0, 0)))
    win_t, wout, wkv, cw_all = gather_weights(win_t_sh, w_out[0].astype(_MXU), w_mem_kv[0].astype(_MXU), cw_sh)
    cw = cw_all.reshape(N_CHIPS, 8, 128)[:, :CONV_K].transpose(1, 0, 2).reshape(CONV_K, LRU_W)

    rc, rs1, rs2 = _rope_tables(seq)
    wg = _block_diag_gates(w_rg[0], w_ig[0]).astype(_MXU)
    qg = jnp.tile(q_norm_g, (1, 2))
    kg = jnp.tile(k_norm_g, (1, 2))
    xqg = jnp.tile(xq_norm_g, (1, 4))
    xkg = jnp.tile(xk_norm_g, (1, 4))

    km, vm = mem_fwd(mems, mem_norm_g, wkv, xkg)
    proj, ya, yb, yc, ycat, xn, dout, pswa, pmem, psink, gates, a_all, loss8 = layer_fwd(
        xs, tgt, rc, rs1, rs2, norm_g, win_t, cw, conv_b, wg, b_rg, b_ig, lru_lambda, qg, kg, xqg, sinks, km, vm,
        out_norm_g, wout)
    (gx, dproj, g_wg, dkm, dvm, g_ng, g_og, g_cb, g_brg, g_big, g_lam, g_cw, g_qn, g_kn, g_xqn, g_sink) = layer_bwd(
        xs, dout, proj, ya, yb, yc, pswa, pmem, psink, gates, a_all, rc, rs1, rs2, norm_g, win_t, cw, conv_b, wg, b_rg,
        b_ig, lru_lambda, qg, kg, xqg, sinks, km, vm, out_norm_g, wout)
    g_wkv, g_mng, g_xkn = mem_bwd(mems, mem_norm_g, wkv, xkg, dkm, dvm)

    g_wrg, g_wig = _diag_blocks(g_wg)
    fold = lambda v, n: v.reshape(n, HEAD).sum(axis=0)
    small_g = _pack([g_ng, g_mng, g_cw, g_cb, g_wrg, g_brg, g_wig, g_big, g_lam, _pad_to(fold(g_qn, 2), 128),
                     _pad_to(fold(g_kn, 2), 128), g_sink, _pad_to(fold(g_xqn, 4), 128), _pad_to(fold(g_xkn, 4), 128),
                     g_og, loss8[0:1]], SMALL_ROWS)
    g_wout, r_kv, r_small = wgrad_reduce(
        ycat, dout, [g_wkv.reshape(N_CHIPS, 2, D_MODEL // 8, 2 * XATT_W)], small_g.reshape(2, SMALL_ROWS // 2, 128),
        (0, 1, 2, 3, 3), "wgrad_out")
    g_win_t, r_out = wgrad_reduce(dproj, xn, [g_wout.reshape(N_CHIPS, 2, D_MODEL // 8, D_MODEL)], None,
                                  (0, 1, 5, 8, 8), "wgrad_in")
    (r_in,) = reduce_grads([g_win_t.reshape(N_CHIPS, 2, D_IN // 8, D_MODEL)], None, "reduce_w_in")

    r_small = r_small.reshape(SMALL_ROWS, 128)
    loss = r_small[LOSS_ROW, 0]
    grads = {"w_in": r_in.reshape(D_IN // 4, D_MODEL).T[None], "w_mem_kv": r_kv.reshape(D_MODEL // 4, 2 * XATT_W)[None],
             "w_out": r_out.reshape(D_MODEL // 4, D_MODEL)[None]}
    for name, row in SMALL_MATRICES:
        grads[name] = r_small[row:row + 256].reshape(1, LRU_BLOCKS, HEAD, HEAD)
    weights = dict(norm_g=norm_g, mem_norm_g=mem_norm_g, w_in=w_in, conv_w=conv_w, conv_b=conv_b, w_rg=w_rg, b_rg=b_rg,
                   w_ig=w_ig, b_ig=b_ig, lru_lambda=lru_lambda, q_norm_g=q_norm_g, k_norm_g=k_norm_g, sinks=sinks,
                   w_mem_kv=w_mem_kv, xq_norm_g=xq_norm_g, xk_norm_g=xk_norm_g, out_norm_g=out_norm_g, w_out=w_out)
    ms = dict(norm_g=m_norm_g, mem_norm_g=m_mem_norm_g, w_in=m_w_in, conv_w=m_conv_w, conv_b=m_conv_b, w_rg=m_w_rg,
              b_rg=m_b_rg, w_ig=m_w_ig, b_ig=m_b_ig, lru_lambda=m_lru_lambda, q_norm_g=m_q_norm_g, k_norm_g=m_k_norm_g,
              sinks=m_sinks, w_mem_kv=m_w_mem_kv, xq_norm_g=m_xq_norm_g, xk_norm_g=m_xk_norm_g,
              out_norm_g=m_out_norm_g, w_out=m_w_out)
    vs = dict(norm_g=v_norm_g, mem_norm_g=v_mem_norm_g, w_in=v_w_in, conv_w=v_conv_w, conv_b=v_conv_b, w_rg=v_w_rg,
              b_rg=v_b_rg, w_ig=v_w_ig, b_ig=v_b_ig, lru_lambda=v_lru_lambda, q_norm_g=v_q_norm_g, k_norm_g=v_k_norm_g,
              sinks=v_sinks, w_mem_kv=v_w_mem_kv, xq_norm_g=v_xq_norm_g, xk_norm_g=v_xk_norm_g,
              out_norm_g=v_out_norm_g, w_out=v_w_out)

    delta, new_m, new_v = {}, {}, {}
    d2, m2, v2 = adamw(w_in[0].T, r_in.reshape(D_IN // 4, D_MODEL), m_w_in[0].T, v_w_in[0].T, "adamw_w_in")
    delta["w_in"], new_m["w_in"], new_v["w_in"] = d2.T[None], m2.T[None], v2.T[None]
    for name in ("w_mem_kv", "w_out"):
        shp = weights[name].shape
        d2, m2, v2 = adamw(weights[name][0], grads[name][0], ms[name][0], vs[name][0], "adamw_" + name)
        delta[name], new_m[name], new_v[name] = d2.reshape(shp), m2.reshape(shp), v2.reshape(shp)
    vec_names = [n for n, _, _ in SMALL_VECTORS]
    small_names = vec_names + [n for n, _ in SMALL_MATRICES]
    res = adamw_vectors(r_small, [grads[n] for n, _ in SMALL_MATRICES], [weights[n] for n in small_names],
                        [ms[n] for n in small_names], [vs[n] for n in small_names])
    nvec, nall = len(vec_names), len(small_names)
    grads.update(zip(vec_names, res[:nvec]))
    delta.update(zip(small_names, res[nvec:nvec + nall]))
    new_m.update(zip(small_names, res[nvec + nall:nvec + 2 * nall]))
    new_v.update(zip(small_names, res[nvec + 2 * nall:]))

    order = ("norm_g", "mem_norm_g", "w_in", "conv_w", "conv_b", "w_rg", "b_rg", "w_ig", "b_ig", "lru_lambda",
             "q_norm_g", "k_norm_g", "sinks", "w_mem_kv", "xq_norm_g", "xk_norm_g", "out_norm_g", "w_out")
    return (loss, gx[None], *[grads[n] for n in order], *[delta[n] for n in order], *[new_m[n] for n in order],
            *[new_v[n] for n in order])
```

```python
import jax
import jax.numpy as jnp
import numpy as np
from jax import lax
from jax.experimental import pallas as pl
from jax.experimental.pallas import tpu as pltpu

F32 = jnp.float32
_MXU = jnp.bfloat16
_WIRE = jnp.bfloat16

D_MODEL = 1024
MEM_LEN = 256
HEAD = 64
LRU_W = 512
LRU_BLOCKS = 8
CONV_K = 4
LRU_C = 8.0
SWA_W = 256
KV_W = 128
XATT_W = 256
BLOCK = 128
D_IN = 2304
ROPE_THETA = 500000.0
ROPE_DIM = 16
EPS = 1e-6
NEG_INF = -1e30
C_LRUX, C_LRUG, C_SQ, C_SK, C_SV, C_SWAG, C_XQ, C_XG = 0, 512, 1024, 1280, 1408, 1536, 1792, 2048

ADAM_LR, ADAM_B1, ADAM_B2, ADAM_EPS, ADAM_WD, ADAM_STEP = 0.001, 0.9, 0.999, 1e-08, 0.01, 10

N_CHIPS = 4
ROW_TILE = 256
VMEM_LIMIT = 56 * 1024 * 1024
ADAM_BLOCK_BYTES = 1280 * 1024
GATHER_PIECES = (1, 1, 1)
MESH = pl.DeviceIdType.MESH


def _mm(a, b):
    return jnp.dot(a.astype(_MXU), b.astype(_MXU), preferred_element_type=F32)


def _mm_nt(a, b):
    return lax.dot_general(a.astype(_MXU), b.astype(_MXU), (((1,), (1,)), ((), ())), preferred_element_type=F32)


def _mm_tn(a, b):
    return lax.dot_general(a.astype(_MXU), b.astype(_MXU), (((0,), (0,)), ((), ())), preferred_element_type=F32)


def _group_matrix(width):
    r = lax.shift_right_logical(lax.broadcasted_iota(jnp.int32, (width, width), 0), 6)
    c = lax.shift_right_logical(lax.broadcasted_iota(jnp.int32, (width, width), 1), 6)
    return (r == c).astype(_MXU)


def _seg_mean(x, gm):
    return jnp.dot(x.astype(_MXU), gm, preferred_element_type=F32) * (1.0 / HEAD)


def _row_mean(x):
    return jnp.mean(x, axis=-1, keepdims=True)


def _col_sum(x):
    return jnp.sum(x, axis=0, keepdims=True)


def _sigmoid(x):
    return jax.nn.sigmoid(x)


def _softplus(z):
    e = jnp.exp(-jnp.abs(z))
    u = 1.0 + e
    log1p_e = jnp.where(u == 1.0, e, jnp.log(u) * (e / (u - 1.0)))
    return jnp.maximum(z, 0.0) + log1p_e


def _rope(t, c, s1, s2):
    return t * c + pltpu.roll(t, 120, 1) * s1 + pltpu.roll(t, 8, 1) * s2


def _rope_bwd(d, c, s1, s2):
    return d * c + pltpu.roll(d * s1, 8, 1) + pltpu.roll(d * s2, 120, 1)


def _lane_mask(width, lo, hi):
    lane = lax.broadcasted_iota(jnp.int32, (1, width), 1)
    return ((lane >= lo) & (lane < hi)).astype(F32)


def _swa_mask(first_block):
    qi = lax.broadcasted_iota(jnp.int32, (BLOCK, 2 * BLOCK), 0)
    kj = lax.broadcasted_iota(jnp.int32, (BLOCK, 2 * BLOCK), 1)
    rel = qi + BLOCK - kj
    ok = (rel >= 0) & (rel < BLOCK)
    return ok & (jnp.logical_not(first_block) | (kj >= BLOCK))


def _place_kv(t, scale):
    lo = t * (_lane_mask(KV_W, 0, HEAD) * scale)
    hi = t * (_lane_mask(KV_W, HEAD, KV_W) * scale)
    return [a.astype(_MXU) for a in (lo, pltpu.roll(lo, HEAD, 1), pltpu.roll(hi, HEAD, 1), hi)]


def _unplace_kv(d):
    return (_lane_mask(KV_W, 0, HEAD) * (d[0] + pltpu.roll(d[1], HEAD, 1))
            + _lane_mask(KV_W, HEAD, KV_W) * (d[3] + pltpu.roll(d[2], HEAD, 1)))


def _swa_probs(qh, ka, mask, sink):
    s = _mm_nt(qh, ka)
    s = jnp.where(mask, s, NEG_INF)
    m = jnp.maximum(jnp.max(s, axis=-1, keepdims=True), sink)
    p = jnp.exp(s - m)
    esink = jnp.exp(sink - m)
    inv = 1.0 / (jnp.sum(p, axis=-1, keepdims=True) + esink)
    return p * inv, esink * inv


def _mem_probs(s_all):
    out = []
    for j in range(4):
        s = s_all[:, MEM_LEN * j:MEM_LEN * (j + 1)]
        p = jnp.exp(s - jnp.max(s, axis=-1, keepdims=True))
        out.append(p * (1.0 / jnp.sum(p, axis=-1, keepdims=True)))
    return out


def _head_rows(t, scale):
    return jnp.concatenate([t * (_lane_mask(XATT_W, HEAD * j, HEAD * (j + 1)) * scale) for j in range(4)], axis=0)


def _lru_gates(xc, wg_ref, brg, big, lam):
    p0 = _mm(xc[:, :256], wg_ref[0])
    p1 = _mm(xc[:, 256:], wg_ref[1])
    rg = _sigmoid(jnp.concatenate([p0[:, :256], p1[:, :256]], axis=1) + brg)
    ig = _sigmoid(jnp.concatenate([p0[:, 256:], p1[:, 256:]], axis=1) + big)
    sp = _softplus(-lam)
    la = (-LRU_C) * rg * sp
    a = jnp.exp(la)
    th = jnp.tanh(la)
    one_minus_a2 = (-2.0 * th) / (1.0 - th)
    return rg, ig, sp, a, jnp.sqrt(one_minus_a2)


def _const_spec(shape, single=False):
    zeros = (0,) * len(shape)
    if single:
        return pl.BlockSpec(shape, lambda i: zeros, pipeline_mode=pl.Buffered(1))
    return pl.BlockSpec(shape, lambda i: zeros)


def _chip_of(x, y):
    return 2 * x + y


def _partners(x, y, c):
    north = c == 1
    near = (jnp.where(north, 1 - x, x), jnp.where(north, y, 1 - y))
    far = (jnp.where(north, x, 1 - x), jnp.where(north, 1 - y, y))
    return near, far, (1 - x, 1 - y)


def gather_weights(win_t, wout, wkv, convw):
    arrs = (win_t, wout, wkv)
    n = len(arrs)
    pieces = []
    for a, arr in enumerate(arrs):
        half = arr.shape[0] // 2
        step = half // GATHER_PIECES[a]
        pieces += [(a, off, step) for off in range(0, half, step)]
    npc = len(pieces)

    def body(a0, a1, a2, cw, o0, o1, o2, ocw, send, recv, lsem):
        ins, outs = (a0, a1, a2), (o0, o1, o2)
        x, y, c = lax.axis_index("x"), lax.axis_index("y"), lax.axis_index("c")
        sibling = (x, y, 1 - c)
        near, far, diag = _partners(x, y, c)
        chips = [near, far, diag]
        me = _chip_of(x, y)

        def landed(p, chip, half):
            a, off, rows_ = pieces[p]
            r = ins[a].shape[0]
            return outs[a].at[pl.ds(pl.multiple_of(chip * r + half * (r // 2) + off, 16), rows_)]

        def mine(p):
            a, off, rows_ = pieces[p]
            return ins[a].at[pl.ds(pl.multiple_of(c * (ins[a].shape[0] // 2) + off, 16), rows_)]

        def copy(k, src, dst, to):
            return pltpu.make_async_remote_copy(src_ref=src, dst_ref=dst, send_sem=send.at[k], recv_sem=recv.at[k],
                                                device_id=to, device_id_type=MESH)

        def cw_rows(chip):
            return ocw.at[pl.ds(pl.multiple_of(chip * 8, 8), 8)]

        locals_ = []
        for a in range(n):
            r = ins[a].shape[0]
            locals_.append(pltpu.make_async_copy(ins[a], outs[a].at[pl.ds(pl.multiple_of(me * r, 16), r)], lsem.at[a]))
        locals_.append(pltpu.make_async_copy(cw, cw_rows(me), lsem.at[n]))
        for cp in locals_:
            cp.start()

        sent = []
        for p in range(npc):
            for j in range(2):
                sent.append(copy(p * 6 + j, mine(p), landed(p, me, c), (*chips[j], c)))
        for j, chip in enumerate(chips):
            sent.append(copy(npc * 6 + j, cw, cw_rows(me), (*chip, c)))
        for cp in sent:
            cp.start()
        for p in range(npc):
            for j in range(3):
                got = landed(p, _chip_of(*chips[j]), c)
                copy(p * 6 + j, got, got, sibling).wait_recv()
                if j == 0:
                    sent.append(copy(p * 6 + 2, got, got, (*far, c)))
                    sent[-1].start()
                sent.append(copy(p * 6 + 3 + j, got, got, sibling))
                sent[-1].start()
        for p in range(npc):
            for j in range(3):
                got = landed(p, _chip_of(*chips[(1, 0, 2)[j]]), 1 - c)
                copy(p * 6 + 3 + j, got, got, sibling).wait_recv()
        for j, chip in enumerate(chips):
            got = cw_rows(_chip_of(*chip))
            copy(npc * 6 + j, got, got, (*chip, c)).wait_recv()
        for cp in sent:
            cp.wait_send()
        for cp in locals_:
            cp.wait()

    vm = pl.BlockSpec(memory_space=pltpu.VMEM)
    out_shape = tuple(jax.ShapeDtypeStruct((N_CHIPS * a.shape[0],) + a.shape[1:], a.dtype) for a in arrs) + (
        jax.ShapeDtypeStruct((N_CHIPS * 8, 128), F32),)
    n_rdma = npc * 6 + 3
    return pl.pallas_call(
        body, name="gather_weights", out_shape=out_shape,
        in_specs=[vm] * 4, out_specs=(vm,) * 4,
        scratch_shapes=[pltpu.SemaphoreType.DMA((n_rdma,)), pltpu.SemaphoreType.DMA((n_rdma,)),
                        pltpu.SemaphoreType.DMA((n + 1,))],
        compiler_params=pltpu.CompilerParams(vmem_limit_bytes=VMEM_LIMIT),
    )(win_t, wout, wkv, convw)


def mem_fwd(mem, mem_g, wkv, xk_g):
    def body(mem_ref, g_ref, w_ref, xk_ref, km_ref, vm_ref):
        mem_v = mem_ref[...]
        mn = mem_v * lax.rsqrt(_row_mean(mem_v * mem_v) + EPS) * g_ref[...]
        mkv = _mm(mn, w_ref[...])
        kpre = mkv[:, :XATT_W]
        gm = _group_matrix(XATT_W)
        km = kpre * lax.rsqrt(_seg_mean(kpre * kpre, gm) + EPS) * xk_ref[...]
        km_ref[...] = _head_rows(km, 0.125).astype(km_ref.dtype)
        vm_ref[...] = _head_rows(mkv[:, XATT_W:], 1.0).astype(vm_ref.dtype)

    vm = pl.BlockSpec(memory_space=pltpu.VMEM)
    rows_shape = jax.ShapeDtypeStruct((4 * MEM_LEN, XATT_W), _MXU)
    return pl.pallas_call(
        body, name="mem_fwd", out_shape=(rows_shape, rows_shape), in_specs=[vm] * 4, out_specs=(vm, vm),
    )(mem, mem_g, wkv, xk_g)


def mem_bwd(mem, mem_g, wkv, xk_g, dkm, dvm):
    def body(mem_ref, g_ref, w_ref, xk_ref, dkm_ref, dvm_ref, gw_ref, gg_ref, gxk_ref):
        mem_v = mem_ref[...]
        mh = mem_v * lax.rsqrt(_row_mean(mem_v * mem_v) + EPS)
        mn = mh * g_ref[...]
        mkv = _mm(mn, w_ref[...])
        kpre = mkv[:, :XATT_W]
        gm = _group_matrix(XATT_W)
        rk = lax.rsqrt(_seg_mean(kpre * kpre, gm) + EPS)
        kn = kpre * rk
        dk = jnp.zeros((MEM_LEN, XATT_W), F32)
        dv = jnp.zeros((MEM_LEN, XATT_W), F32)
        for j in range(4):
            mj = _lane_mask(XATT_W, HEAD * j, HEAD * (j + 1))
            dk = dk + dkm_ref[:, MEM_LEN * j:MEM_LEN * (j + 1)].T * (mj * 0.125)
            dv = dv + dvm_ref[:, MEM_LEN * j:MEM_LEN * (j + 1)].T * mj
        gxk_ref[...] = _col_sum(dk * kn)
        dkn = dk * xk_ref[...]
        dkpre = rk * (dkn - kn * _seg_mean(dkn * kn, gm))
        dmkv = jnp.concatenate([dkpre, dv], axis=1)
        gw_ref[...] = _mm_tn(mn, dmkv)
        dmn = _mm_nt(dmkv, w_ref[...])
        gg_ref[...] = _col_sum(dmn * mh)

    vm = pl.BlockSpec(memory_space=pltpu.VMEM)
    return pl.pallas_call(
        body, name="mem_bwd",
        out_shape=(jax.ShapeDtypeStruct((D_MODEL, 2 * XATT_W), F32), jax.ShapeDtypeStruct((1, D_MODEL), F32),
                   jax.ShapeDtypeStruct((1, XATT_W), F32)),
        in_specs=[vm] * 6, out_specs=(vm, vm, vm),
    )(mem, mem_g, wkv, xk_g, dkm, dvm)


def layer_fwd(x, tgt, rc, rs1, rs2, ng, win_t, cw, cb, wg, brg, big, lam, qg, kg, xqg, sinks, km, vm, og, wout):
    seq = x.shape[0]
    tm = min(ROW_TILE, seq)
    nt = seq // tm
    nb = tm // BLOCK

    def body(x_ref, t_ref, c_ref, s1_ref, s2_ref, ng_ref, win_ref, cw_ref, cb_ref, wg_ref, brg_ref, big_ref, lam_ref,
             qg_ref, kg_ref, xqg_ref, sink_ref, km_ref, vm_ref, og_ref, wout_ref,
             proj_ref, ya_ref, yb_ref, yc_ref, ycat_ref, xn_ref, dout_ref, pswa_ref, pmem_ref, psink_ref, gates_ref,
             a_ref, loss_ref,
             ext_ref, b_scr, hc_ref, kp_ref, vp_ref, lacc_ref):
        i = pl.program_id(0)

        @pl.when(i == 0)
        def _():
            ext_ref[0:8, :] = jnp.zeros((8, LRU_W), F32)
            hc_ref[...] = jnp.zeros_like(hc_ref)
            kp_ref[...] = jnp.zeros_like(kp_ref)
            vp_ref[...] = jnp.zeros_like(vp_ref)
            lacc_ref[...] = jnp.zeros_like(lacc_ref)

        xv = x_ref[...]
        xn = (xv * lax.rsqrt(_row_mean(xv * xv) + EPS) * ng_ref[...]).astype(_MXU)
        xn_ref[...] = xn.astype(xn_ref.dtype)
        proj_ref[...] = _mm_nt(xn, win_ref[...])

        u = proj_ref[:, C_LRUX:C_LRUX + LRU_W]
        ext_ref[8:8 + tm, :] = u
        xc = cb_ref[...]
        for k in range(CONV_K):
            xc = xc + cw_ref[k:k + 1, :] * ext_ref[pl.ds(5 + k, tm), :]
        ext_ref[0:8, :] = u[tm - 8:tm, :]
        rg, ig, sp, a, sq = _lru_gates(xc, wg_ref, brg_ref[...], big_ref[...], lam_ref[...])
        for k, t in enumerate((xc, rg, ig, sq)):
            gates_ref[:, LRU_W * k:LRU_W * (k + 1)] = t.astype(gates_ref.dtype)
        a_ref[...] = a
        b_scr[...] = sq * (ig * xc)
        row8 = lax.broadcasted_iota(jnp.int32, (8, LRU_W), 0)

        def scan_step(g, carry):
            r0 = pl.multiple_of(g * 8, 8)
            av = a_ref[pl.ds(r0, 8), :]
            bv = b_scr[pl.ds(r0, 8), :]
            for d in (1, 2, 4):
                a_sh = jnp.where(row8 >= d, pltpu.roll(av, d, 0), 1.0)
                b_sh = jnp.where(row8 >= d, pltpu.roll(bv, d, 0), 0.0)
                bv = bv + av * b_sh
                av = av * a_sh
            hv = bv + av * carry
            ya_ref[pl.ds(r0, 8), :] = hv
            return hv[7:8, :]

        hc_ref[0:1, :] = lax.fori_loop(0, tm // 8, scan_step, hc_ref[0:1, :], unroll=True)

        gm128 = _group_matrix(KV_W)
        cv, s1v, s2v = c_ref[...], s1_ref[...], s2_ref[...]

        def head_norm_rope(t, g):
            n = t * lax.rsqrt(_seg_mean(t * t, gm128) + EPS)
            return _rope(n * g, cv, s1v, s2v)

        qs_ = (head_norm_rope(proj_ref[:, C_SQ:C_SQ + 128], qg_ref[...]).astype(_MXU),
               head_norm_rope(proj_ref[:, C_SQ + 128:C_SQ + 256], qg_ref[...]).astype(_MXU))
        kr = head_norm_rope(proj_ref[:, C_SK:C_SK + KV_W], kg_ref[...])
        sv = proj_ref[:, C_SV:C_SV + KV_W]
        ka = _place_kv(jnp.concatenate([kp_ref[...], kr], axis=0), 0.125)
        va = _place_kv(jnp.concatenate([vp_ref[...], sv], axis=0), 1.0)
        kp_ref[...] = kr[tm - BLOCK:tm, :]
        vp_ref[...] = sv[tm - BLOCK:tm, :]
        lane128 = lax.broadcasted_iota(jnp.int32, (1, 128), 1)
        for b in range(nb):
            mask = _swa_mask((i == 0) & (b == 0)) if b == 0 else _swa_mask(False)
            band = slice(BLOCK * b, BLOCK * b + 2 * BLOCK)
            blk = slice(BLOCK * b, BLOCK * (b + 1))
            psink = jnp.zeros((BLOCK, 128), F32)
            for j in range(4):
                p, pk = _swa_probs(qs_[j // 2][blk], ka[j][band], mask, sink_ref[0, j])
                pswa_ref[blk, 2 * BLOCK * j:2 * BLOCK * (j + 1)] = p.astype(pswa_ref.dtype)
                psink = jnp.where(lane128 == j, pk, psink)
            psink_ref[blk, :] = psink
            for h in range(2):
                yb_ref[blk, KV_W * h:KV_W * (h + 1)] = _mm(
                    pswa_ref[blk, 4 * BLOCK * h:4 * BLOCK * (h + 1)],
                    jnp.concatenate([va[2 * h][band], va[2 * h + 1][band]], axis=0))

        gm256 = _group_matrix(XATT_W)
        xq = proj_ref[:, C_XQ:C_XQ + XATT_W]
        qx = xq * lax.rsqrt(_seg_mean(xq * xq, gm256) + EPS) * xqg_ref[...]
        pm = _mem_probs(_mm_nt(qx, km_ref[...]))
        for j in range(4):
            pmem_ref[:, MEM_LEN * j:MEM_LEN * (j + 1)] = pm[j].astype(pmem_ref.dtype)
        yc = _mm(pmem_ref[...], vm_ref[...])
        yc_ref[...] = yc

        def gated(y, g, gate):
            return y * lax.rsqrt(_row_mean(y * y) + EPS) * g * (gate * _sigmoid(gate))

        ogv = og_ref[...]
        za = gated(ya_ref[...], ogv[:, :512], proj_ref[:, C_LRUG:C_LRUG + LRU_W])
        zb = gated(yb_ref[...], ogv[:, 512:768], proj_ref[:, C_SWAG:C_SWAG + SWA_W])
        zc = gated(yc, ogv[:, 768:], proj_ref[:, C_XG:C_XG + XATT_W])
        ycat_ref[:, 0:512] = za.astype(ycat_ref.dtype)
        ycat_ref[:, 512:768] = zb.astype(ycat_ref.dtype)
        ycat_ref[:, 768:1024] = zc.astype(ycat_ref.dtype)
        out = xv + _mm(ycat_ref[...], wout_ref[...])
        err = out - t_ref[...]
        dout_ref[...] = (err * (1.0 / D_MODEL)).astype(dout_ref.dtype)
        lacc_ref[...] = lacc_ref[...] + (0.5 / D_MODEL) * jnp.sum(err * err)

        @pl.when(i == nt - 1)
        def _():
            loss_ref[...] = lacc_ref[...]

    def rows(ncol):
        return pl.BlockSpec((tm, ncol), lambda i: (i, 0))

    in_specs = [rows(D_MODEL), rows(D_MODEL), rows(128), rows(128), rows(128),
                _const_spec((1, D_MODEL)), _const_spec((D_IN, D_MODEL), True), _const_spec((CONV_K, LRU_W)),
                _const_spec((1, LRU_W)), _const_spec((2, 256, 512), True), _const_spec((1, LRU_W)),
                _const_spec((1, LRU_W)), _const_spec((1, LRU_W)), _const_spec((1, 128)), _const_spec((1, 128)),
                _const_spec((1, XATT_W)), pl.BlockSpec(memory_space=pltpu.SMEM),
                _const_spec((4 * MEM_LEN, XATT_W), True), _const_spec((4 * MEM_LEN, XATT_W), True),
                _const_spec((1, D_MODEL)), _const_spec((D_MODEL, D_MODEL), True)]
    out_shape = (jax.ShapeDtypeStruct((seq, D_IN), F32), jax.ShapeDtypeStruct((seq, LRU_W), F32),
                 jax.ShapeDtypeStruct((seq, SWA_W), F32), jax.ShapeDtypeStruct((seq, XATT_W), F32),
                 jax.ShapeDtypeStruct((seq, D_MODEL), _MXU), jax.ShapeDtypeStruct((seq, D_MODEL), _MXU),
                 jax.ShapeDtypeStruct((seq, D_MODEL), _MXU), jax.ShapeDtypeStruct((seq, 4 * 2 * BLOCK), _MXU),
                 jax.ShapeDtypeStruct((seq, 4 * MEM_LEN), _MXU), jax.ShapeDtypeStruct((seq, 128), F32),
                 jax.ShapeDtypeStruct((seq, 4 * LRU_W), _MXU), jax.ShapeDtypeStruct((seq, LRU_W), F32),
                 jax.ShapeDtypeStruct((8, 128), F32))
    out_specs = (rows(D_IN), rows(LRU_W), rows(SWA_W), rows(XATT_W), rows(D_MODEL), rows(D_MODEL), rows(D_MODEL),
                 rows(4 * 2 * BLOCK), rows(4 * MEM_LEN), rows(128), rows(4 * LRU_W), rows(LRU_W),
                 _const_spec((8, 128)))
    scratch = [pltpu.VMEM((tm + 8, LRU_W), F32), pltpu.VMEM((tm, LRU_W), F32),
               pltpu.VMEM((8, LRU_W), F32), pltpu.VMEM((BLOCK, KV_W), F32), pltpu.VMEM((BLOCK, KV_W), F32),
               pltpu.VMEM((8, 128), F32)]
    return pl.pallas_call(
        body, name="layer_fwd", grid=(nt,), out_shape=out_shape, in_specs=in_specs, out_specs=out_specs,
        scratch_shapes=scratch,
        compiler_params=pltpu.CompilerParams(dimension_semantics=("arbitrary",), vmem_limit_bytes=VMEM_LIMIT),
    )(x, tgt, rc, rs1, rs2, ng, win_t, cw, cb, wg, brg, big, lam, qg, kg, xqg, sinks, km, vm, og, wout)


def wgrad_reduce(lhs, rhs, bigs, g_small, stage_at, name):
    seq, ncol = rhs.shape
    nblk = lhs.shape[1] // 256
    nres = len(bigs) + (g_small is not None)

    def body(l_ref, r_ref, *refs):
        if nres:
            _hosted_reduce(pl.program_id(0), stage_at, refs[:nres], refs[nres + 1:2 * nres + 1], refs[2 * nres + 1:],
                           g_small is not None)
        refs[nres][...] = _mm_tn(l_ref[...], r_ref[...])

    red_shape, scratch = _hosted_reduce_shapes(bigs, g_small) if nres else ([], [])
    vm = pl.BlockSpec(memory_space=pltpu.VMEM)
    hbm = pl.BlockSpec(memory_space=pl.ANY)
    operands = list(bigs) + ([] if g_small is None else [g_small])
    return pl.pallas_call(
        body, name=name, grid=(nblk,),
        out_shape=(jax.ShapeDtypeStruct((lhs.shape[1], ncol), F32), *red_shape),
        in_specs=[pl.BlockSpec((seq, 256), lambda j: (0, j)), _const_spec((seq, ncol), True)] + [hbm] * len(bigs)
        + [vm] * (g_small is not None),
        out_specs=(pl.BlockSpec((256, ncol), lambda j: (j, 0)),) + (hbm,) * len(red_shape),
        scratch_shapes=scratch,
        compiler_params=pltpu.CompilerParams(dimension_semantics=("arbitrary",), vmem_limit_bytes=VMEM_LIMIT),
    )(lhs, rhs, *operands)


def layer_bwd(x, dout, proj, ya, yb, yc, pswa, pmem, psink, gates, a_all, rc, rs1, rs2, ng, win_t, cw, cb, wg, brg, big, lam, qg, kg, xqg, sinks, km,
              vm, og, wout):
    seq = x.shape[0]
    tm = min(ROW_TILE, seq)
    nt = seq // tm
    nb = tm // BLOCK

    def body(x_ref, dout_ref, proj_ref, ya_ref, yb_ref, yc_ref, pswa_ref, pmem_ref, psink_ref, gates_ref, a_ref,
             c_ref, s1_ref, s2_ref,
             yah_ref, kvh_ref, ch_ref, s1h_ref, s2h_ref,
             ng_ref, win_ref, cw_ref, cb_ref, wg_ref, brg_ref, big_ref, lam_ref, qg_ref, kg_ref, xqg_ref, sink_ref,
             km_ref, vm_ref, og_ref, wout_ref,
             gx_ref, dproj_ref, gwg_ref, dkm_ref, dvm_ref, gng_ref, gog_ref, gcb_ref, gbrg_ref, gbig_ref, glam_ref,
             gcw_ref, gqn_ref, gkn_ref, gxqn_ref, gsink_ref,
             hext_ref, aext_ref, an_scr, dh_scr, g_scr, dxc_ext, gcar_ref, dkcar_ref, dvcar_ref):
        i = pl.program_id(0)
        tile = nt - 1 - i
        first_tile = tile == 0

        @pl.when(i == 0)
        def _():
            for r in (gwg_ref, dkm_ref, dvm_ref, gng_ref, gog_ref, gcb_ref, gbrg_ref, gbig_ref, glam_ref, gcw_ref,
                      gqn_ref, gkn_ref, gxqn_ref, gsink_ref, gcar_ref, dkcar_ref, dvcar_ref):
                r[...] = jnp.zeros_like(r)
            dxc_ext[tm:tm + 8, :] = jnp.zeros((8, LRU_W), F32)
            aext_ref[tm:tm + 8, :] = jnp.zeros((8, LRU_W), F32)

        xv = x_ref[...]
        dov = dout_ref[...]
        dz = _mm_nt(dov, wout_ref[...])
        ogv = og_ref[...]

        def group_bwd(y, gate, g, dzg):
            r = lax.rsqrt(_row_mean(y * y) + EPS)
            n = y * r
            sg = _sigmoid(gate)
            dgate = dzg * (n * g) * (sg * (1.0 + gate * (1.0 - sg)))
            dng = dzg * (gate * sg)
            dn = dng * g
            return r * (dn - n * _row_mean(dn * n)), dgate, _col_sum(dng * n)

        dya, dga, goa = group_bwd(ya_ref[...], proj_ref[:, C_LRUG:C_LRUG + LRU_W], ogv[:, :512], dz[:, :512])
        dyb, dgb, gob = group_bwd(yb_ref[...], proj_ref[:, C_SWAG:C_SWAG + SWA_W], ogv[:, 512:768], dz[:, 512:768])
        dyc, dgc, goc = group_bwd(yc_ref[...], proj_ref[:, C_XG:C_XG + XATT_W], ogv[:, 768:], dz[:, 768:])
        gog_ref[...] += jnp.concatenate([goa, gob, goc], axis=1)
        dproj_ref[:, C_LRUG:C_LRUG + LRU_W] = dga.astype(dproj_ref.dtype)
        dproj_ref[:, C_SWAG:C_SWAG + SWA_W] = dgb.astype(dproj_ref.dtype)
        dproj_ref[:, C_XG:C_XG + XATT_W] = dgc.astype(dproj_ref.dtype)

        gm256 = _group_matrix(XATT_W)
        xq = proj_ref[:, C_XQ:C_XQ + XATT_W]
        rq = lax.rsqrt(_seg_mean(xq * xq, gm256) + EPS)
        qn = xq * rq
        qx = qn * xqg_ref[...]
        qxb = qx.astype(_MXU)
        dycb = dyc.astype(_MXU)
        dp_all = _mm_nt(dycb, vm_ref[...])
        dsm = []
        for j in range(4):
            pj = pmem_ref[:, MEM_LEN * j:MEM_LEN * (j + 1)].astype(F32)
            dp = dp_all[:, MEM_LEN * j:MEM_LEN * (j + 1)]
            dsm.append((pj * (dp - jnp.sum(pj * dp, axis=-1, keepdims=True))).astype(_MXU))
        ds_all = jnp.concatenate(dsm, axis=1)
        dvm_ref[...] += _mm_tn(dycb, pmem_ref[...])
        dkm_ref[...] += _mm_tn(qxb, ds_all)
        dqx = _mm(ds_all, km_ref[...])
        gxqn_ref[...] += _col_sum(dqx * qn)
        dqn = dqx * xqg_ref[...]
        dproj_ref[:, C_XQ:C_XQ + XATT_W] = (rq * (dqn - qn * _seg_mean(dqn * qn, gm256))).astype(dproj_ref.dtype)

        gm128 = _group_matrix(KV_W)
        cv, s1v, s2v = c_ref[...], s1_ref[...], s2_ref[...]

        def head_norm(t):
            r = lax.rsqrt(_seg_mean(t * t, gm128) + EPS)
            return t * r, r

        qn_, qr_ = zip(head_norm(proj_ref[:, C_SQ:C_SQ + 128]), head_norm(proj_ref[:, C_SQ + 128:C_SQ + 256]))
        qrope = [_rope(qn_[h] * qg_ref[...], cv, s1v, s2v).astype(_MXU) for h in range(2)]
        kn, krr = head_norm(proj_ref[:, C_SK:C_SK + KV_W])
        kr = _rope(kn * kg_ref[...], cv, s1v, s2v)
        khn, _ = head_norm(kvh_ref[:, 0:KV_W])
        khr = _rope(khn * kg_ref[...], ch_ref[...], s1h_ref[...], s2h_ref[...])
        ka = _place_kv(jnp.concatenate([khr, kr], axis=0), 0.125)
        va = _place_kv(jnp.concatenate([kvh_ref[:, KV_W:2 * KV_W], proj_ref[:, C_SV:C_SV + KV_W]], axis=0), 1.0)
        lane128 = lax.broadcasted_iota(jnp.int32, (1, 128), 1)
        gsink = jnp.zeros((1, 128), F32)
        dk_band, dv_band, dq_blk = [], [], []
        for b in range(nb):
            band = slice(BLOCK * b, BLOCK * b + 2 * BLOCK)
            blk = slice(BLOCK * b, BLOCK * (b + 1))
            dka, dva, dsb = [], [], []
            deltas = jnp.zeros((BLOCK, 128), F32)
            for j in range(4):
                qh = qrope[j // 2][blk]
                doh = dyb[blk, KV_W * (j // 2):KV_W * (j // 2 + 1)].astype(_MXU)
                pb = pswa_ref[blk, 2 * BLOCK * j:2 * BLOCK * (j + 1)]
                p = pb.astype(F32)
                dp = _mm_nt(doh, va[j][band])
                delta = jnp.sum(p * dp, axis=-1, keepdims=True)
                ds = (p * (dp - delta)).astype(_MXU)
                deltas = jnp.where(lane128 == j, delta, deltas)
                dva.append(_mm_tn(pb, doh))
                dka.append(_mm_tn(ds, qh))
                dsb.append(ds)
            gsink = gsink - _col_sum(psink_ref[blk, :] * deltas)
            dk_band.append(_unplace_kv(dka) * 0.125)
            dv_band.append(_unplace_kv(dva))
            dq_blk.append([_mm(jnp.concatenate(dsb[2 * h:2 * h + 2], axis=1),
                               jnp.concatenate([ka[2 * h][band], ka[2 * h + 1][band]], axis=0)) for h in range(2)])
        gsink_ref[...] += gsink
        dk_rows = [dk_band[b][BLOCK:] + (dk_band[b + 1][:BLOCK] if b + 1 < nb else dkcar_ref[...]) for b in range(nb)]
        dv_rows = [dv_band[b][BLOCK:] + (dv_band[b + 1][:BLOCK] if b + 1 < nb else dvcar_ref[...]) for b in range(nb)]
        dkcar_ref[...] = dk_band[0][:BLOCK]
        dvcar_ref[...] = dv_band[0][:BLOCK]
        dkg = _rope_bwd(jnp.concatenate(dk_rows, axis=0), cv, s1v, s2v)
        gkn = _col_sum(dkg * kn)
        dkn = dkg * kg_ref[...]
        dproj_ref[:, C_SK:C_SK + KV_W] = (krr * (dkn - kn * _seg_mean(dkn * kn, gm128))).astype(dproj_ref.dtype)
        dproj_ref[:, C_SV:C_SV + KV_W] = jnp.concatenate(dv_rows, axis=0).astype(dproj_ref.dtype)
        gqn = jnp.zeros((1, 128), F32)
        for h in range(2):
            dqg = _rope_bwd(jnp.concatenate([dq_blk[b][h] for b in range(nb)], axis=0), cv, s1v, s2v)
            gqn = gqn + _col_sum(dqg * qn_[h])
            dqn_ = dqg * qg_ref[...]
            dproj_ref[:, C_SQ + 128 * h:C_SQ + 128 * (h + 1)] = (
                qr_[h] * (dqn_ - qn_[h] * _seg_mean(dqn_ * qn_[h], gm128))).astype(dproj_ref.dtype)
        gqn_ref[...] += gqn
        gkn_ref[...] += gkn

        u = proj_ref[:, C_LRUX:C_LRUX + LRU_W]
        xc, rg, ig, sq = (gates_ref[:, LRU_W * k:LRU_W * (k + 1)].astype(F32) for k in range(4))
        a = a_ref[...]
        sp = _softplus(-lam_ref[...])
        hext_ref[0:8, :] = jnp.where(first_tile, 0.0, yah_ref[...])
        hext_ref[8:8 + tm, :] = ya_ref[...]
        hprev = hext_ref[pl.ds(7, tm), :]
        aext_ref[0:tm, :] = a
        an_scr[...] = aext_ref[pl.ds(1, tm), :]
        dh_scr[...] = dya
        dh_scr[tm - 1:tm, :] = dh_scr[tm - 1:tm, :] + gcar_ref[0:1, :]
        row8 = lax.broadcasted_iota(jnp.int32, (8, LRU_W), 0)

        def scan_step(gi, carry):
            r0 = pl.multiple_of((tm // 8 - 1 - gi) * 8, 8)
            av = an_scr[pl.ds(r0, 8), :]
            bv = dh_scr[pl.ds(r0, 8), :]
            for d in (1, 2, 4):
                a_sh = jnp.where(row8 < 8 - d, pltpu.roll(av, 8 - d, 0), 1.0)
                b_sh = jnp.where(row8 < 8 - d, pltpu.roll(bv, 8 - d, 0), 0.0)
                bv = bv + av * b_sh
                av = av * a_sh
            gv = bv + av * carry
            g_scr[pl.ds(r0, 8), :] = gv
            return gv[0:1, :]

        g0 = lax.fori_loop(0, tm // 8, scan_step, jnp.zeros((1, LRU_W), F32), unroll=True)
        gcar_ref[0:1, :] = a[0:1, :] * g0
        gv = g_scr[...]
        da = gv * hprev
        dig = gv * sq * xc
        dxc = gv * sq * ig
        dla = da * a - gv * (ig * xc) * ((a * a) / sq)
        drg = dla * ((-LRU_C) * sp)
        glam_ref[...] += _col_sum(dla * rg)
        dpr = drg * rg * (1.0 - rg)
        dpi = dig * ig * (1.0 - ig)
        gbrg_ref[...] += _col_sum(dpr)
        gbig_ref[...] += _col_sum(dpi)
        dpre0 = jnp.concatenate([dpr[:, :256], dpi[:, :256]], axis=1).astype(_MXU)
        dpre1 = jnp.concatenate([dpr[:, 256:], dpi[:, 256:]], axis=1).astype(_MXU)
        gwg_ref[0] += _mm_tn(xc[:, :256], dpre0)
        gwg_ref[1] += _mm_tn(xc[:, 256:], dpre1)
        dxc = dxc + jnp.concatenate([_mm_nt(dpre0, wg_ref[0]), _mm_nt(dpre1, wg_ref[1])], axis=1)
        gcb_ref[...] += _col_sum(dxc)
        dxc_ext[0:tm, :] = dxc
        du = jnp.zeros((tm, LRU_W), F32)
        for k in range(CONV_K):
            later = dxc_ext[pl.ds(3 - k, tm), :]
            gcw_ref[k:k + 1, :] += _col_sum(later * u)
            du = du + cw_ref[k:k + 1, :] * later
        dxc_ext[tm:tm + 8, :] = dxc[0:8, :]
        dproj_ref[:, C_LRUX:C_LRUX + LRU_W] = du.astype(dproj_ref.dtype)

        dxn = _mm(dproj_ref[...], win_ref[...])
        rx = lax.rsqrt(_row_mean(xv * xv) + EPS)
        xh = xv * rx
        gng_ref[...] += _col_sum(dxn * xh)
        dxh = dxn * ng_ref[...]
        gx_ref[...] = dov.astype(F32) + rx * (dxh - xh * _row_mean(dxh * xh))

        @pl.when(i == nt - 1)
        def _():
            glam_ref[...] = glam_ref[...] * (LRU_C * _sigmoid(-lam_ref[...]))

    def rows(ncol, arr_cols_block=0):
        return pl.BlockSpec((tm, ncol), lambda i: (nt - 1 - i, arr_cols_block))

    def halo(nrow, ncol, colblk=0):
        per = tm // nrow
        return pl.BlockSpec((nrow, ncol), lambda i: (jnp.maximum((nt - 1 - i) * per - 1, 0), colblk))

    in_specs = [rows(D_MODEL), rows(D_MODEL), rows(D_IN), rows(LRU_W), rows(SWA_W), rows(XATT_W),
                rows(4 * 2 * BLOCK), rows(4 * MEM_LEN), rows(128), rows(4 * LRU_W), rows(LRU_W),
                rows(128), rows(128), rows(128),
                halo(8, LRU_W), halo(BLOCK, 2 * KV_W, C_SK // (2 * KV_W)),
                halo(BLOCK, 128), halo(BLOCK, 128), halo(BLOCK, 128),
                _const_spec((1, D_MODEL)), _const_spec((D_IN, D_MODEL), True), _const_spec((CONV_K, LRU_W)),
                _const_spec((1, LRU_W)), _const_spec((2, 256, 512), True), _const_spec((1, LRU_W)),
                _const_spec((1, LRU_W)), _const_spec((1, LRU_W)), _const_spec((1, 128)), _const_spec((1, 128)),
                _const_spec((1, XATT_W)), pl.BlockSpec(memory_space=pltpu.SMEM),
                _const_spec((4 * MEM_LEN, XATT_W), True), _const_spec((4 * MEM_LEN, XATT_W), True),
                _const_spec((1, D_MODEL)), _const_spec((D_MODEL, D_MODEL), True)]
    small = [(2, 256, 512), (XATT_W, 4 * MEM_LEN), (XATT_W, 4 * MEM_LEN), (1, D_MODEL), (1, D_MODEL), (1, LRU_W), (1, LRU_W),
             (1, LRU_W), (1, LRU_W), (CONV_K, LRU_W), (1, 128), (1, 128), (1, XATT_W), (1, 128)]
    out_shape = (jax.ShapeDtypeStruct((seq, D_MODEL), F32), jax.ShapeDtypeStruct((seq, D_IN), _MXU)) + tuple(
        jax.ShapeDtypeStruct(s, F32) for s in small)
    out_specs = (rows(D_MODEL), rows(D_IN)) + tuple(_const_spec(s) for s in small)
    scratch = [pltpu.VMEM((tm + 8, LRU_W), F32), pltpu.VMEM((tm + 8, LRU_W), F32),
               pltpu.VMEM((tm, LRU_W), F32), pltpu.VMEM((tm, LRU_W), F32), pltpu.VMEM((tm, LRU_W), F32),
               pltpu.VMEM((tm + 8, LRU_W), F32),
               pltpu.VMEM((8, LRU_W), F32), pltpu.VMEM((BLOCK, KV_W), F32), pltpu.VMEM((BLOCK, KV_W), F32)]
    return pl.pallas_call(
        body, name="layer_bwd", grid=(nt,), out_shape=out_shape, in_specs=in_specs, out_specs=out_specs,
        scratch_shapes=scratch,
        compiler_params=pltpu.CompilerParams(dimension_semantics=("arbitrary",), vmem_limit_bytes=VMEM_LIMIT),
    )(x, dout, proj, ya, yb, yc, pswa, pmem, psink, gates, a_all, rc, rs1, rs2, ya, proj, rc, rs1, rs2,
      ng, win_t, cw, cb, wg, brg, big, lam, qg, kg, xqg, sinks, km, vm, og, wout)


def _reduce_protocol(big, sm, outs, osm, r1, r1s, wire, r2, r2s, wire2, ps, own, send, recv, lsem):
    nbig = len(big)
    x, y, c = lax.axis_index("x"), lax.axis_index("y"), lax.axis_index("c")
    sibling = (x, y, 1 - c)
    near, far, diag = _partners(x, y, c)
    me, near_id, far_id, diag_id = _chip_of(x, y), _chip_of(*near), _chip_of(*far), _chip_of(*diag)

    def copy(k, src, dst, to):
        return pltpu.make_async_remote_copy(src_ref=src, dst_ref=dst, send_sem=send.at[k], recv_sem=recv.at[k],
                                            device_id=to, device_id_type=MESH)

    def sent(stage):
        cps = []
        for a in range(nbig):
            if stage == 0:
                cps.append(copy(5 * a, big[a].at[:, 1 - c], r1[a], sibling))
            elif stage == 1:
                cps.append(copy(5 * a + 1, wire[a].at[near_id], r2[a].at[0], (*near, c)))
                cps.append(copy(5 * a + 2, wire[a].at[diag_id], r2[a].at[1], (*near, c)))
            elif stage == 2:
                cps.append(copy(5 * a + 3, wire2[a], r2[a].at[2], (*far, c)))
            elif stage == 3:
                cps.append(copy(5 * a + 4, outs[a].at[c], outs[a].at[c], sibling))
        if sm is not None:
            src, dst, to = ((sm.at[1 - c], r1s, sibling), (r1s, r2s.at[0], (*near, c)), (ps, r2s.at[1], (*far, c)),
                            (osm.at[c], osm.at[c], sibling))[stage]
            cps.append(copy(5 * nbig + stage, src, dst, to))
        return cps

    def arrived(k, ref):
        copy(k, ref, ref, sibling).wait_recv()

    def loads():
        return [pltpu.make_async_copy(big[a].at[:, c], own[a], lsem.at[a]) for a in range(nbig)]

    def stage0():
        for cp in sent(0) + loads():
            cp.start()

    def stage1():
        for a in range(nbig):
            loads()[a].wait()
            arrived(5 * a, r1[a])
            for k in range(N_CHIPS):
                r1[a][k] = own[a][k] + r1[a][k]
                wire[a][k] = r1[a][k].astype(wire[a].dtype)
        if sm is not None:
            arrived(5 * nbig, r1s)
            r1s[...] = sm[c] + r1s[...]
        for cp in sent(1):
            cp.start()

    def stage2():
        for a in range(nbig):
            arrived(5 * a + 1, r2[a].at[0])
            arrived(5 * a + 2, r2[a].at[1])
            r1[a][me] = r1[a][me] + r2[a][0].astype(F32)
            wire2[a][...] = (r1[a][far_id] + r2[a][1].astype(F32)).astype(wire2[a].dtype)
        if sm is not None:
            arrived(5 * nbig + 1, r2s.at[0])
            ps[...] = r1s[...] + r2s[0]
        for cp in sent(2):
            cp.start()

    def stage3():
        for a in range(nbig):
            arrived(5 * a + 3, r2[a].at[2])
            outs[a][c] = r1[a][me] + r2[a][2].astype(F32)
        if sm is not None:
            arrived(5 * nbig + 2, r2s.at[1])
            osm[c] = ps[...] + r2s[1]
        for cp in sent(3):
            cp.start()

    def stage4():
        for a in range(nbig):
            arrived(5 * a + 4, outs[a].at[1 - c])
        if sm is not None:
            arrived(5 * nbig + 3, osm.at[1 - c])
        for stage in range(4):
            for cp in sent(stage):
                cp.wait_send()

    return [stage0, stage1, stage2, stage3, stage4]


def _reduce_buffers(bigs, g_small):
    half = [b.shape[2:] for b in bigs]
    sm_half = None if g_small is None else g_small.shape[1:]
    out_shape = [jax.ShapeDtypeStruct((2,) + h, F32) for h in half]
    small = lambda lead: [] if g_small is None else [pltpu.VMEM(lead + sm_half, F32)]
    if g_small is not None:
        out_shape.append(jax.ShapeDtypeStruct(g_small.shape, F32))
    n_sem = 5 * len(bigs) + 4
    scratch = ([pltpu.VMEM((N_CHIPS,) + h, F32) for h in half] + small(())
               + [pltpu.VMEM((N_CHIPS,) + h, _WIRE) for h in half]
               + [pltpu.VMEM((3,) + h, _WIRE) for h in half] + small((2,))
               + [pltpu.VMEM(h, _WIRE) for h in half] + small(())
               + [pltpu.VMEM((N_CHIPS,) + h, F32) for h in half]
               + [pltpu.SemaphoreType.DMA((n_sem,)), pltpu.SemaphoreType.DMA((n_sem,)),
                  pltpu.SemaphoreType.DMA((len(bigs),))])
    return out_shape, scratch


def _split_reduce_refs(refs, nbig, has_small):
    it = iter(refs)
    take = lambda n: [next(it) for _ in range(n)]
    one = lambda: next(it) if has_small else None
    big, sm = take(nbig), one()
    outs, osm = take(nbig), one()
    r1, r1s, wire, r2, r2s, wire2, ps, own = take(nbig), one(), take(nbig), take(nbig), one(), take(nbig), one(), take(nbig)
    send, recv, lsem = take(3)
    return big, sm, outs, osm, r1, r1s, wire, r2, r2s, wire2, ps, own, send, recv, lsem


def _hosted_reduce_shapes(bigs, g_small):
    red_shape, scratch = _reduce_buffers(bigs, g_small)
    nres = len(red_shape)
    return red_shape, [pltpu.VMEM(r.shape, r.dtype) for r in red_shape] + scratch + [pltpu.SemaphoreType.DMA((nres,))]


def _hosted_reduce(step, stage_at, operands, results, scratch, has_small):
    nres = len(results)
    sums, rest, fsem = scratch[:nres], scratch[nres:-1], scratch[-1]
    refs = tuple(operands) + tuple(sums) + tuple(rest)
    for at, stage in zip(stage_at, _reduce_protocol(*_split_reduce_refs(refs, nres - has_small, has_small))):
        pl.when(step == at)(stage)

    @pl.when(step == stage_at[-1])
    def _():
        out = [pltpu.make_async_copy(sums[k], results[k], fsem.at[k]) for k in range(nres)]
        for cp in out:
            cp.start()
        for cp in out:
            cp.wait()


def reduce_grads(bigs, g_small, name):
    nbig = len(bigs)

    def body(*refs):
        for stage in _reduce_protocol(*_split_reduce_refs(refs, nbig, g_small is not None)):
            stage()

    out_shape, scratch = _reduce_buffers(bigs, g_small)
    vm = pl.BlockSpec(memory_space=pltpu.VMEM)
    hbm = pl.BlockSpec(memory_space=pl.ANY)
    operands = list(bigs) + ([] if g_small is None else [g_small])
    return pl.pallas_call(
        body, name=name, out_shape=tuple(out_shape), in_specs=[hbm] * nbig + [vm] * (g_small is not None),
        out_specs=(vm,) * len(out_shape), scratch_shapes=scratch,
        compiler_params=pltpu.CompilerParams(vmem_limit_bytes=VMEM_LIMIT),
    )(*operands)


def adamw(w, g, m, v, name):
    rows_, cols = w.shape
    tr = max(t for t in range(8, rows_ + 1, 8) if rows_ % t == 0 and t * cols * 4 <= ADAM_BLOCK_BYTES)

    def body(w_ref, g_ref, m_ref, v_ref, d_ref, nm_ref, nv_ref):
        d_ref[...], nm_ref[...], nv_ref[...] = _adam_update(w_ref[...], g_ref[...], m_ref[...], v_ref[...])

    spec = pl.BlockSpec((tr, cols), lambda i: (i, 0))
    shp = jax.ShapeDtypeStruct(w.shape, F32)
    return pl.pallas_call(
        body, name=name, grid=(rows_ // tr,), out_shape=(shp, shp, shp), in_specs=[spec] * 4, out_specs=(spec,) * 3,
        compiler_params=pltpu.CompilerParams(dimension_semantics=("arbitrary",)),
    )(w, g, m, v)


def _adam_update(w, g, m, v):
    nm = ADAM_B1 * m + (1.0 - ADAM_B1) * g
    nv = ADAM_B2 * v + (1.0 - ADAM_B2) * (g * g)
    m_hat = nm / (1.0 - ADAM_B1 ** ADAM_STEP)
    v_hat = nv / (1.0 - ADAM_B2 ** ADAM_STEP)
    return (-ADAM_LR) * (m_hat / (jnp.sqrt(v_hat) + ADAM_EPS) + ADAM_WD * w), nm, nv


def adamw_vectors(g_pack, g_mats, ws, ms, vs):
    nvec, nmat = len(SMALL_VECTORS), len(g_mats)
    n = nvec + nmat

    def body(*refs):
        pk = refs[0]
        gm_refs = refs[1:1 + nmat]
        w_refs, m_refs, v_refs = (refs[1 + nmat + k * n:1 + nmat + (k + 1) * n] for k in range(3))
        outs = refs[1 + nmat + 3 * n:]
        g_out, d_out, nm_out, nv_out = outs[:nvec], outs[nvec:nvec + n], outs[nvec + n:nvec + 2 * n], outs[nvec + 2 * n:]
        chip = 2 * lax.axis_index("x") + lax.axis_index("y")
        for k, (name, row, width) in enumerate(SMALL_VECTORS):
            if name == "conv_w":
                g = jnp.concatenate([pk[pl.ds(row + 4 * t + chip, 1), :] for t in range(CONV_K)], axis=0)[None]
            elif width >= 128:
                g = jnp.concatenate([pk[row + r:row + r + 1, :] for r in range(width // 128)], axis=1)
            else:
                g = pk[row:row + 1, 0:width]
            g_out[k][...] = g
            d_out[k][...], nm_out[k][...], nv_out[k][...] = _adam_update(w_refs[k][...], g, m_refs[k][...], v_refs[k][...])
        for k in range(nvec, n):
            d_out[k][...], nm_out[k][...], nv_out[k][...] = _adam_update(
                w_refs[k][...], gm_refs[k - nvec][...], m_refs[k][...], v_refs[k][...])

    vm = pl.BlockSpec(memory_space=pltpu.VMEM)
    like = [jax.ShapeDtypeStruct(w.shape, F32) for w in ws]
    out_shape = like[:nvec] + like * 3
    return pl.pallas_call(
        body, name="adamw_vectors", out_shape=tuple(out_shape), in_specs=[vm] * (1 + nmat + 3 * n),
        out_specs=(vm,) * len(out_shape),
    )(g_pack, *g_mats, *ws, *ms, *vs)


SMALL_VECTORS = (("norm_g", 0, 1024), ("mem_norm_g", 8, 1024), ("conv_w", 16, 512), ("conv_b", 32, 512),
                 ("b_rg", 292, 512), ("b_ig", 552, 512), ("lru_lambda", 556, 512), ("q_norm_g", 560, 64),
                 ("k_norm_g", 561, 64), ("sinks", 562, 4), ("xq_norm_g", 563, 64), ("xk_norm_g", 564, 64),
                 ("out_norm_g", 565, 1024))
SMALL_MATRICES = (("w_rg", 36), ("w_ig", 296))
LOSS_ROW = 573
SMALL_ROWS = 576


def _pack(parts, rows_):
    flat = jnp.concatenate([p.reshape(-1) for p in parts])
    return jnp.pad(flat, (0, rows_ * 128 - flat.shape[0])).reshape(rows_, 128)


def _pad_to(v, n):
    v = v.reshape(-1)
    return jnp.pad(v, (0, n - v.shape[0]))


def _block_diag_gates(w_rg, w_ig):
    eye = jnp.eye(4, dtype=w_rg.dtype)

    def bd(w4):
        return (w4[:, :, None, :] * eye[:, None, :, None]).reshape(256, 256)

    return jnp.stack([jnp.concatenate([bd(w_rg[4 * h:4 * h + 4]), bd(w_ig[4 * h:4 * h + 4])], axis=1) for h in (0, 1)])


def _diag_blocks(g):
    g6 = g.reshape(2, 4, HEAD, 2, 4, HEAD)
    d = (g6 * jnp.eye(4, dtype=g.dtype)[None, :, None, None, :, None]).sum(axis=4)
    return d[:, :, :, 0].reshape(8, HEAD, HEAD), d[:, :, :, 1].reshape(8, HEAD, HEAD)


def _rope_tables(seq):
    pos = np.arange(seq, dtype=np.float32)
    inv_freq = (np.float32(ROPE_THETA) ** (-(np.arange(0, ROPE_DIM, 2, dtype=np.float32) / np.float32(ROPE_DIM)))
                ).astype(np.float32)
    ang = (pos[:, None] * inv_freq[None, :]).astype(np.float32)
    cos, sin = np.cos(ang).astype(np.float32), np.sin(ang).astype(np.float32)
    z = lambda n: np.zeros((seq, n), np.float32)
    c64 = np.concatenate([cos, cos, np.ones((seq, HEAD - ROPE_DIM), np.float32)], axis=1)
    s1_64 = np.concatenate([-sin, z(HEAD - 8)], axis=1)
    s2_64 = np.concatenate([z(8), sin, z(HEAD - ROPE_DIM)], axis=1)
    return tuple(jnp.asarray(np.concatenate([t, t], axis=1)) for t in (c64, s1_64, s2_64))


def kernel(x, mem, norm_g, mem_norm_g, w_in, conv_w, conv_b, w_rg, b_rg, w_ig, b_ig, lru_lambda, q_norm_g, k_norm_g, sinks, w_mem_kv, xq_norm_g, xk_norm_g, out_norm_g, w_out, loss_target, m_norm_g, m_mem_norm_g, m_w_in, m_conv_w, m_conv_b, m_w_rg, m_b_rg, m_w_ig, m_b_ig, m_lru_lambda, m_q_norm_g, m_k_norm_g, m_sinks, m_w_mem_kv, m_xq_norm_g, m_xk_norm_g, m_out_norm_g, m_w_out, v_norm_g, v_mem_norm_g, v_w_in, v_conv_w, v_conv_b, v_w_rg, v_b_rg, v_w_ig, v_b_ig, v_lru_lambda, v_q_norm_g, v_k_norm_g, v_sinks, v_w_mem_kv, v_xq_norm_g, v_xk_norm_g, v_out_norm_g, v_w_out):
    seq = x.shape[1]
    xs, tgt, mems = x[0], loss_target[0], mem[0]

    win_t_sh = w_in[0].T.astype(_MXU)
    cw_sh = jnp.pad(conv_w[0], ((0, 4), (0, 0)))
    win_t, wout, wkv, cw_all = gather_weights(win_t_sh, w_out[0].astype(_MXU), w_mem_kv[0].astype(_MXU), cw_sh)
    cw = cw_all.reshape(N_CHIPS, 8, 128)[:, :CONV_K].transpose(1, 0, 2).reshape(CONV_K, LRU_W)

    rc, rs1, rs2 = _rope_tables(seq)
    wg = _block_diag_gates(w_rg[0], w_ig[0]).astype(_MXU)
    qg = jnp.tile(q_norm_g, (1, 2))
    kg = jnp.tile(k_norm_g, (1, 2))
    xqg = jnp.tile(xq_norm_g, (1, 4))
    xkg = jnp.tile(xk_norm_g, (1, 4))

    km, vm = mem_fwd(mems, mem_norm_g, wkv, xkg)
    proj, ya, yb, yc, ycat, xn, dout, pswa, pmem, psink, gates, a_all, loss8 = layer_fwd(
        xs, tgt, rc, rs1, rs2, norm_g, win_t, cw, conv_b, wg, b_rg, b_ig, lru_lambda, qg, kg, xqg, sinks, km, vm,
        out_norm_g, wout)
    (g_wout,) = wgrad_reduce(ycat, dout, [], None, (), "wgrad_out")
    (gx, dproj, g_wg, dkm, dvm, g_ng, g_og, g_cb, g_brg, g_big, g_lam, g_cw, g_qn, g_kn, g_xqn, g_sink) = layer_bwd(
        xs, dout, proj, ya, yb, yc, pswa, pmem, psink, gates, a_all, rc, rs1, rs2, norm_g, win_t, cw, conv_b, wg, b_rg,
        b_ig, lru_lambda, qg, kg, xqg, sinks, km, vm, out_norm_g, wout)
    g_wkv, g_mng, g_xkn = mem_bwd(mems, mem_norm_g, wkv, xkg, dkm, dvm)

    g_wrg, g_wig = _diag_blocks(g_wg)
    fold = lambda v, n: v.reshape(n, HEAD).sum(axis=0)
    small_g = _pack([g_ng, g_mng, g_cw, g_cb, g_wrg, g_brg, g_wig, g_big, g_lam, _pad_to(fold(g_qn, 2), 128),
                     _pad_to(fold(g_kn, 2), 128), g_sink, _pad_to(fold(g_xqn, 4), 128), _pad_to(fold(g_xkn, 4), 128),
                     g_og, loss8[0:1]], SMALL_ROWS)
    early = [g_wout.reshape(N_CHIPS, 2, D_MODEL // 8, D_MODEL), g_wkv.reshape(N_CHIPS, 2, D_MODEL // 8, 2 * XATT_W)]
    g_win_t, r_out, r_kv, r_small = wgrad_reduce(dproj, xn, early, small_g.reshape(2, SMALL_ROWS // 2, 128),
                                                 (0, 1, 6, 8, 8), "wgrad_in")
    (r_in,) = reduce_grads([g_win_t.reshape(N_CHIPS, 2, D_IN // 8, D_MODEL)], None, "reduce_w_in")

    r_small = r_small.reshape(SMALL_ROWS, 128)
    loss = r_small[LOSS_ROW, 0]
    grads = {"w_in": r_in.reshape(D_IN // 4, D_MODEL).T[None], "w_mem_kv": r_kv.reshape(D_MODEL // 4, 2 * XATT_W)[None],
             "w_out": r_out.reshape(D_MODEL // 4, D_MODEL)[None]}
    for name, row in SMALL_MATRICES:
        grads[name] = r_small[row:row + 256].reshape(1, LRU_BLOCKS, HEAD, HEAD)
    weights = dict(norm_g=norm_g, mem_norm_g=mem_norm_g, w_in=w_in, conv_w=conv_w, conv_b=conv_b, w_rg=w_rg, b_rg=b_rg,
                   w_ig=w_ig, b_ig=b_ig, lru_lambda=lru_lambda, q_norm_g=q_norm_g, k_norm_g=k_norm_g, sinks=sinks,
                   w_mem_kv=w_mem_kv, xq_norm_g=xq_norm_g, xk_norm_g=xk_norm_g, out_norm_g=out_norm_g, w_out=w_out)
    ms = dict(norm_g=m_norm_g, mem_norm_g=m_mem_norm_g, w_in=m_w_in, conv_w=m_conv_w, conv_b=m_conv_b, w_rg=m_w_rg,
              b_rg=m_b_rg, w_ig=m_w_ig, b_ig=m_b_ig, lru_lambda=m_lru_lambda, q_norm_g=m_q_norm_g, k_norm_g=m_k_norm_g,
              sinks=m_sinks, w_mem_kv=m_w_mem_kv, xq_norm_g=m_xq_norm_g, xk_norm_g=m_xk_norm_g,
              out_norm_g=m_out_norm_g, w_out=m_w_out)
    vs = dict(norm_g=v_norm_g, mem_norm_g=v_mem_norm_g, w_in=v_w_in, conv_w=v_conv_w, conv_b=v_conv_b, w_rg=v_w_rg,
              b_rg=v_b_rg, w_ig=v_w_ig, b_ig=v_b_ig, lru_lambda=v_lru_lambda, q_norm_g=v_q_norm_g, k_norm_g=v_k_norm_g,
              sinks=v_sinks, w_mem_kv=v_w_mem_kv, xq_norm_g=v_xq_norm_g, xk_norm_g=v_xk_norm_g,
              out_norm_g=v_out_norm_g, w_out=v_w_out)

    delta, new_m, new_v = {}, {}, {}
    d2, m2, v2 = adamw(w_in[0].T, r_in.reshape(D_IN // 4, D_MODEL), m_w_in[0].T, v_w_in[0].T, "adamw_w_in")
    delta["w_in"], new_m["w_in"], new_v["w_in"] = d2.T[None], m2.T[None], v2.T[None]
    for name in ("w_mem_kv", "w_out"):
        shp = weights[name].shape
        d2, m2, v2 = adamw(weights[name][0], grads[name][0], ms[name][0], vs[name][0], "adamw_" + name)
        delta[name], new_m[name], new_v[name] = d2.reshape(shp), m2.reshape(shp), v2.reshape(shp)
    vec_names = [n for n, _, _ in SMALL_VECTORS]
    small_names = vec_names + [n for n, _ in SMALL_MATRICES]
    res = adamw_vectors(r_small, [grads[n] for n, _ in SMALL_MATRICES], [weights[n] for n in small_names],
                        [ms[n] for n in small_names], [vs[n] for n in small_names])
    nvec, nall = len(vec_names), len(small_names)
    grads.update(zip(vec_names, res[:nvec]))
    delta.update(zip(small_names, res[nvec:nvec + nall]))
    new_m.update(zip(small_names, res[nvec + nall:nvec + 2 * nall]))
    new_v.update(zip(small_names, res[nvec + 2 * nall:]))

    order = ("norm_g", "mem_norm_g", "w_in", "conv_w", "conv_b", "w_rg", "b_rg", "w_ig", "b_ig", "lru_lambda",
             "q_norm_g", "k_norm_g", "sinks", "w_mem_kv", "xq_norm_g", "xk_norm_g", "out_norm_g", "w_out")
    return (loss, gx[None], *[grads[n] for n in order], *[delta[n] for n in order], *[new_m[n] for n in order],
            *[new_v[n] for n in order])
```

```python
import jax
import jax.numpy as jnp
import numpy as np
from jax import lax
from jax.experimental import pallas as pl
from jax.experimental.pallas import tpu as pltpu

F32 = jnp.float32
_MXU = jnp.bfloat16
_WIRE = jnp.bfloat16

D_MODEL = 1024
MEM_LEN = 256
HEAD = 64
LRU_W = 512
LRU_BLOCKS = 8
CONV_K = 4
LRU_C = 8.0
SWA_W = 256
KV_W = 128
XATT_W = 256
BLOCK = 128
D_IN = 2304
ROPE_THETA = 500000.0
ROPE_DIM = 16
EPS = 1e-6
NEG_INF = -1e30
C_LRUX, C_LRUG, C_SQ, C_SK, C_SV, C_SWAG, C_XQ, C_XG = 0, 512, 1024, 1280, 1408, 1536, 1792, 2048

ADAM_LR, ADAM_B1, ADAM_B2, ADAM_EPS, ADAM_WD, ADAM_STEP = 0.001, 0.9, 0.999, 1e-08, 0.01, 10

N_CHIPS = 4
ROW_TILE = 256
VMEM_LIMIT = 56 * 1024 * 1024
ADAM_BLOCK_BYTES = 1280 * 1024
MESH = pl.DeviceIdType.MESH


def _mm(a, b):
    return jnp.dot(a.astype(_MXU), b.astype(_MXU), preferred_element_type=F32)


def _mm_nt(a, b):
    return lax.dot_general(a.astype(_MXU), b.astype(_MXU), (((1,), (1,)), ((), ())), preferred_element_type=F32)


def _mm_tn(a, b):
    return lax.dot_general(a.astype(_MXU), b.astype(_MXU), (((0,), (0,)), ((), ())), preferred_element_type=F32)


def _group_matrix(width):
    r = lax.shift_right_logical(lax.broadcasted_iota(jnp.int32, (width, width), 0), 6)
    c = lax.shift_right_logical(lax.broadcasted_iota(jnp.int32, (width, width), 1), 6)
    return (r == c).astype(_MXU)


def _seg_mean(x, gm):
    return jnp.dot(x.astype(_MXU), gm, preferred_element_type=F32) * (1.0 / HEAD)


def _row_mean(x):
    return jnp.mean(x, axis=-1, keepdims=True)


def _col_sum(x):
    return jnp.sum(x, axis=0, keepdims=True)


def _sigmoid(x):
    return jax.nn.sigmoid(x)


def _softplus(z):
    e = jnp.exp(-jnp.abs(z))
    u = 1.0 + e
    log1p_e = jnp.where(u == 1.0, e, jnp.log(u) * (e / (u - 1.0)))
    return jnp.maximum(z, 0.0) + log1p_e


def _rope(t, c, s1, s2):
    return t * c + pltpu.roll(t, 120, 1) * s1 + pltpu.roll(t, 8, 1) * s2


def _rope_bwd(d, c, s1, s2):
    return d * c + pltpu.roll(d * s1, 8, 1) + pltpu.roll(d * s2, 120, 1)


def _lane_mask(width, lo, hi):
    lane = lax.broadcasted_iota(jnp.int32, (1, width), 1)
    return ((lane >= lo) & (lane < hi)).astype(F32)


def _swa_mask(first_block):
    qi = lax.broadcasted_iota(jnp.int32, (BLOCK, 2 * BLOCK), 0)
    kj = lax.broadcasted_iota(jnp.int32, (BLOCK, 2 * BLOCK), 1)
    rel = qi + BLOCK - kj
    ok = (rel >= 0) & (rel < BLOCK)
    return ok & (jnp.logical_not(first_block) | (kj >= BLOCK))


def _place_kv(t, scale):
    lo = t * (_lane_mask(KV_W, 0, HEAD) * scale)
    hi = t * (_lane_mask(KV_W, HEAD, KV_W) * scale)
    return [a.astype(_MXU) for a in (lo, pltpu.roll(lo, HEAD, 1), pltpu.roll(hi, HEAD, 1), hi)]


def _unplace_kv(d):
    return (_lane_mask(KV_W, 0, HEAD) * (d[0] + pltpu.roll(d[1], HEAD, 1))
            + _lane_mask(KV_W, HEAD, KV_W) * (d[3] + pltpu.roll(d[2], HEAD, 1)))


def _swa_probs(qh, ka, mask, sink):
    s = _mm_nt(qh, ka)
    s = jnp.where(mask, s, NEG_INF)
    m = jnp.maximum(jnp.max(s, axis=-1, keepdims=True), sink)
    p = jnp.exp(s - m)
    esink = jnp.exp(sink - m)
    inv = 1.0 / (jnp.sum(p, axis=-1, keepdims=True) + esink)
    return p * inv, esink * inv


def _mem_probs(s_all):
    out = []
    for j in range(4):
        s = s_all[:, MEM_LEN * j:MEM_LEN * (j + 1)]
        p = jnp.exp(s - jnp.max(s, axis=-1, keepdims=True))
        out.append(p * (1.0 / jnp.sum(p, axis=-1, keepdims=True)))
    return out


def _head_rows(t, scale):
    return jnp.concatenate([t * (_lane_mask(XATT_W, HEAD * j, HEAD * (j + 1)) * scale) for j in range(4)], axis=0)


def _lru_gates(xc, wg_ref, brg, big, lam):
    p0 = _mm(xc[:, :256], wg_ref[0])
    p1 = _mm(xc[:, 256:], wg_ref[1])
    rg = _sigmoid(jnp.concatenate([p0[:, :256], p1[:, :256]], axis=1) + brg)
    ig = _sigmoid(jnp.concatenate([p0[:, 256:], p1[:, 256:]], axis=1) + big)
    sp = _softplus(-lam)
    la = (-LRU_C) * rg * sp
    a = jnp.exp(la)
    th = jnp.tanh(la)
    one_minus_a2 = (-2.0 * th) / (1.0 - th)
    return rg, ig, sp, a, jnp.sqrt(one_minus_a2)


def _const_spec(shape, single=False):
    zeros = (0,) * len(shape)
    if single:
        return pl.BlockSpec(shape, lambda i: zeros, pipeline_mode=pl.Buffered(1))
    return pl.BlockSpec(shape, lambda i: zeros)


def _chip_of(x, y):
    return 2 * x + y


def _partners(x, y, c):
    north = c == 1
    near = (jnp.where(north, 1 - x, x), jnp.where(north, y, 1 - y))
    far = (jnp.where(north, x, 1 - x), jnp.where(north, 1 - y, y))
    return near, far, (1 - x, 1 - y)


def gather_weights(win_t, wout, wkv, convw):
    arrs = (win_t, wout, wkv)
    n = len(arrs)
    pieces = [(a, 0, arr.shape[0] // 2) for a, arr in enumerate(arrs)]
    npc = len(pieces)

    def body(a0, a1, a2, cw, o0, o1, o2, ocw, send, recv, lsem):
        ins, outs = (a0, a1, a2), (o0, o1, o2)
        x, y, c = lax.axis_index("x"), lax.axis_index("y"), lax.axis_index("c")
        sibling = (x, y, 1 - c)
        near, far, diag = _partners(x, y, c)
        chips = [near, far, diag]
        me = _chip_of(x, y)

        def landed(p, chip, half):
            a, off, rows_ = pieces[p]
            r = ins[a].shape[0]
            return outs[a].at[pl.ds(pl.multiple_of(chip * r + half * (r // 2) + off, 16), rows_)]

        def mine(p):
            a, off, rows_ = pieces[p]
            return ins[a].at[pl.ds(pl.multiple_of(c * (ins[a].shape[0] // 2) + off, 16), rows_)]

        def copy(k, src, dst, to):
            return pltpu.make_async_remote_copy(src_ref=src, dst_ref=dst, send_sem=send.at[k], recv_sem=recv.at[k],
                                                device_id=to, device_id_type=MESH)

        def cw_rows(chip):
            return ocw.at[pl.ds(pl.multiple_of(chip * 8, 8), 8)]

        locals_ = []
        for a in range(n):
            r = ins[a].shape[0]
            locals_.append(pltpu.make_async_copy(ins[a], outs[a].at[pl.ds(pl.multiple_of(me * r, 16), r)], lsem.at[a]))
        locals_.append(pltpu.make_async_copy(cw, cw_rows(me), lsem.at[n]))
        for cp in locals_:
            cp.start()

        sent = []
        for p in range(npc):
            for j in range(2):
                sent.append(copy(p * 6 + j, mine(p), landed(p, me, c), (*chips[j], c)))
        for j, chip in enumerate(chips):
            sent.append(copy(npc * 6 + j, cw, cw_rows(me), (*chip, c)))
        for cp in sent:
            cp.start()
        for p in range(npc):
            for j in range(3):
                got = landed(p, _chip_of(*chips[j]), c)
                copy(p * 6 + j, got, got, sibling).wait_recv()
                if j == 0:
                    sent.append(copy(p * 6 + 2, got, got, (*far, c)))
                    sent[-1].start()
                sent.append(copy(p * 6 + 3 + j, got, got, sibling))
                sent[-1].start()
        for p in range(npc):
            for j in range(3):
                got = landed(p, _chip_of(*chips[(1, 0, 2)[j]]), 1 - c)
                copy(p * 6 + 3 + j, got, got, sibling).wait_recv()
        for j, chip in enumerate(chips):
            got = cw_rows(_chip_of(*chip))
            copy(npc * 6 + j, got, got, (*chip, c)).wait_recv()
        for cp in sent:
            cp.wait_send()
        for cp in locals_:
            cp.wait()

    vm = pl.BlockSpec(memory_space=pltpu.VMEM)
    out_shape = tuple(jax.ShapeDtypeStruct((N_CHIPS * a.shape[0],) + a.shape[1:], a.dtype) for a in arrs) + (
        jax.ShapeDtypeStruct((N_CHIPS * 8, 128), F32),)
    n_rdma = npc * 6 + 3
    return pl.pallas_call(
        body, name="gather_weights", out_shape=out_shape,
        in_specs=[vm] * 4, out_specs=(vm,) * 4,
        scratch_shapes=[pltpu.SemaphoreType.DMA((n_rdma,)), pltpu.SemaphoreType.DMA((n_rdma,)),
                        pltpu.SemaphoreType.DMA((n + 1,))],
        compiler_params=pltpu.CompilerParams(vmem_limit_bytes=VMEM_LIMIT),
    )(win_t, wout, wkv, convw)


def mem_fwd(mem, mem_g, wkv, xk_g):
    def body(mem_ref, g_ref, w_ref, xk_ref, km_ref, vm_ref):
        mem_v = mem_ref[...]
        mn = mem_v * lax.rsqrt(_row_mean(mem_v * mem_v) + EPS) * g_ref[...]
        mkv = _mm(mn, w_ref[...])
        kpre = mkv[:, :XATT_W]
        gm = _group_matrix(XATT_W)
        km = kpre * lax.rsqrt(_seg_mean(kpre * kpre, gm) + EPS) * xk_ref[...]
        km_ref[...] = _head_rows(km, 0.125).astype(km_ref.dtype)
        vm_ref[...] = _head_rows(mkv[:, XATT_W:], 1.0).astype(vm_ref.dtype)

    vm = pl.BlockSpec(memory_space=pltpu.VMEM)
    rows_shape = jax.ShapeDtypeStruct((4 * MEM_LEN, XATT_W), _MXU)
    return pl.pallas_call(
        body, name="mem_fwd", out_shape=(rows_shape, rows_shape), in_specs=[vm] * 4, out_specs=(vm, vm),
    )(mem, mem_g, wkv, xk_g)


def mem_bwd(mem, mem_g, wkv, xk_g, dkm, dvm):
    def body(mem_ref, g_ref, w_ref, xk_ref, dkm_ref, dvm_ref, gw_ref, gg_ref, gxk_ref):
        mem_v = mem_ref[...]
        mh = mem_v * lax.rsqrt(_row_mean(mem_v * mem_v) + EPS)
        mn = mh * g_ref[...]
        mkv = _mm(mn, w_ref[...])
        kpre = mkv[:, :XATT_W]
        gm = _group_matrix(XATT_W)
        rk = lax.rsqrt(_seg_mean(kpre * kpre, gm) + EPS)
        kn = kpre * rk
        dk = jnp.zeros((MEM_LEN, XATT_W), F32)
        dv = jnp.zeros((MEM_LEN, XATT_W), F32)
        for j in range(4):
            mj = _lane_mask(XATT_W, HEAD * j, HEAD * (j + 1))
            dk = dk + dkm_ref[:, MEM_LEN * j:MEM_LEN * (j + 1)].T * (mj * 0.125)
            dv = dv + dvm_ref[:, MEM_LEN * j:MEM_LEN * (j + 1)].T * mj
        gxk_ref[...] = _col_sum(dk * kn)
        dkn = dk * xk_ref[...]
        dkpre = rk * (dkn - kn * _seg_mean(dkn * kn, gm))
        dmkv = jnp.concatenate([dkpre, dv], axis=1)
        gw_ref[...] = _mm_tn(mn, dmkv)
        dmn = _mm_nt(dmkv, w_ref[...])
        gg_ref[...] = _col_sum(dmn * mh)

    vm = pl.BlockSpec(memory_space=pltpu.VMEM)
    return pl.pallas_call(
        body, name="mem_bwd",
        out_shape=(jax.ShapeDtypeStruct((D_MODEL, 2 * XATT_W), F32), jax.ShapeDtypeStruct((1, D_MODEL), F32),
                   jax.ShapeDtypeStruct((1, XATT_W), F32)),
        in_specs=[vm] * 6, out_specs=(vm, vm, vm),
    )(mem, mem_g, wkv, xk_g, dkm, dvm)


def layer_fwd(x, tgt, rc, rs1, rs2, ng, win_t, cw, cb, wg, brg, big, lam, qg, kg, xqg, sinks, km, vm, og, wout):
    seq = x.shape[0]
    tm = min(ROW_TILE, seq)
    nt = seq // tm
    nb = tm // BLOCK

    def body(x_ref, t_ref, c_ref, s1_ref, s2_ref, ng_ref, win_ref, cw_ref, cb_ref, wg_ref, brg_ref, big_ref, lam_ref,
             qg_ref, kg_ref, xqg_ref, sink_ref, km_ref, vm_ref, og_ref, wout_ref,
             proj_ref, ya_ref, yb_ref, yc_ref, ycat_ref, xn_ref, dout_ref, pswa_ref, pmem_ref, psink_ref, gates_ref,
             a_ref, loss_ref,
             ext_ref, b_scr, hc_ref, kp_ref, vp_ref, lacc_ref):
        i = pl.program_id(0)

        @pl.when(i == 0)
        def _():
            ext_ref[0:8, :] = jnp.zeros((8, LRU_W), F32)
            hc_ref[...] = jnp.zeros_like(hc_ref)
            kp_ref[...] = jnp.zeros_like(kp_ref)
            vp_ref[...] = jnp.zeros_like(vp_ref)
            lacc_ref[...] = jnp.zeros_like(lacc_ref)

        xv = x_ref[...]
        xn = (xv * lax.rsqrt(_row_mean(xv * xv) + EPS) * ng_ref[...]).astype(_MXU)
        xn_ref[...] = xn.astype(xn_ref.dtype)
        proj_ref[...] = _mm_nt(xn, win_ref[...])

        u = proj_ref[:, C_LRUX:C_LRUX + LRU_W]
        ext_ref[8:8 + tm, :] = u
        xc = cb_ref[...]
        for k in range(CONV_K):
            xc = xc + cw_ref[k:k + 1, :] * ext_ref[pl.ds(5 + k, tm), :]
        ext_ref[0:8, :] = u[tm - 8:tm, :]
        rg, ig, sp, a, sq = _lru_gates(xc, wg_ref, brg_ref[...], big_ref[...], lam_ref[...])
        for k, t in enumerate((xc, rg, ig, sq)):
            gates_ref[:, LRU_W * k:LRU_W * (k + 1)] = t.astype(gates_ref.dtype)
        a_ref[...] = a
        b_scr[...] = sq * (ig * xc)
        row8 = lax.broadcasted_iota(jnp.int32, (8, LRU_W), 0)

        def scan_step(g, carry):
            r0 = pl.multiple_of(g * 8, 8)
            av = a_ref[pl.ds(r0, 8), :]
            bv = b_scr[pl.ds(r0, 8), :]
            for d in (1, 2, 4):
                a_sh = jnp.where(row8 >= d, pltpu.roll(av, d, 0), 1.0)
                b_sh = jnp.where(row8 >= d, pltpu.roll(bv, d, 0), 0.0)
                bv = bv + av * b_sh
                av = av * a_sh
            hv = bv + av * carry
            ya_ref[pl.ds(r0, 8), :] = hv
            return hv[7:8, :]

        hc_ref[0:1, :] = lax.fori_loop(0, tm // 8, scan_step, hc_ref[0:1, :], unroll=True)

        gm128 = _group_matrix(KV_W)
        cv, s1v, s2v = c_ref[...], s1_ref[...], s2_ref[...]

        def head_norm_rope(t, g):
            n = t * lax.rsqrt(_seg_mean(t * t, gm128) + EPS)
            return _rope(n * g, cv, s1v, s2v)

        qs_ = (head_norm_rope(proj_ref[:, C_SQ:C_SQ + 128], qg_ref[...]).astype(_MXU),
               head_norm_rope(proj_ref[:, C_SQ + 128:C_SQ + 256], qg_ref[...]).astype(_MXU))
        kr = head_norm_rope(proj_ref[:, C_SK:C_SK + KV_W], kg_ref[...])
        sv = proj_ref[:, C_SV:C_SV + KV_W]
        ka = _place_kv(jnp.concatenate([kp_ref[...], kr], axis=0), 0.125)
        va = _place_kv(jnp.concatenate([vp_ref[...], sv], axis=0), 1.0)
        kp_ref[...] = kr[tm - BLOCK:tm, :]
        vp_ref[...] = sv[tm - BLOCK:tm, :]
        lane128 = lax.broadcasted_iota(jnp.int32, (1, 128), 1)
        for b in range(nb):
            mask = _swa_mask((i == 0) & (b == 0)) if b == 0 else _swa_mask(False)
            band = slice(BLOCK * b, BLOCK * b + 2 * BLOCK)
            blk = slice(BLOCK * b, BLOCK * (b + 1))
            psink = jnp.zeros((BLOCK, 128), F32)
            for j in range(4):
                p, pk = _swa_probs(qs_[j // 2][blk], ka[j][band], mask, sink_ref[0, j])
                pswa_ref[blk, 2 * BLOCK * j:2 * BLOCK * (j + 1)] = p.astype(pswa_ref.dtype)
                psink = jnp.where(lane128 == j, pk, psink)
            psink_ref[blk, :] = psink
            for h in range(2):
                yb_ref[blk, KV_W * h:KV_W * (h + 1)] = _mm(
                    pswa_ref[blk, 4 * BLOCK * h:4 * BLOCK * (h + 1)],
                    jnp.concatenate([va[2 * h][band], va[2 * h + 1][band]], axis=0))

        gm256 = _group_matrix(XATT_W)
        xq = proj_ref[:, C_XQ:C_XQ + XATT_W]
        qx = xq * lax.rsqrt(_seg_mean(xq * xq, gm256) + EPS) * xqg_ref[...]
        pm = _mem_probs(_mm_nt(qx, km_ref[...]))
        for j in range(4):
            pmem_ref[:, MEM_LEN * j:MEM_LEN * (j + 1)] = pm[j].astype(pmem_ref.dtype)
        yc = _mm(pmem_ref[...], vm_ref[...])
        yc_ref[...] = yc

        def gated(y, g, gate):
            return y * lax.rsqrt(_row_mean(y * y) + EPS) * g * (gate * _sigmoid(gate))

        ogv = og_ref[...]
        za = gated(ya_ref[...], ogv[:, :512], proj_ref[:, C_LRUG:C_LRUG + LRU_W])
        zb = gated(yb_ref[...], ogv[:, 512:768], proj_ref[:, C_SWAG:C_SWAG + SWA_W])
        zc = gated(yc, ogv[:, 768:], proj_ref[:, C_XG:C_XG + XATT_W])
        ycat_ref[:, 0:512] = za.astype(ycat_ref.dtype)
        ycat_ref[:, 512:768] = zb.astype(ycat_ref.dtype)
        ycat_ref[:, 768:1024] = zc.astype(ycat_ref.dtype)
        out = xv + _mm(ycat_ref[...], wout_ref[...])
        err = out - t_ref[...]
        dout_ref[...] = (err * (1.0 / D_MODEL)).astype(dout_ref.dtype)
        lacc_ref[...] = lacc_ref[...] + (0.5 / D_MODEL) * jnp.sum(err * err)

        @pl.when(i == nt - 1)
        def _():
            loss_ref[...] = lacc_ref[...]

    def rows(ncol):
        return pl.BlockSpec((tm, ncol), lambda i: (i, 0))

    in_specs = [rows(D_MODEL), rows(D_MODEL), rows(128), rows(128), rows(128),
                _const_spec((1, D_MODEL)), _const_spec((D_IN, D_MODEL), True), _const_spec((CONV_K, LRU_W)),
                _const_spec((1, LRU_W)), _const_spec((2, 256, 512), True), _const_spec((1, LRU_W)),
                _const_spec((1, LRU_W)), _const_spec((1, LRU_W)), _const_spec((1, 128)), _const_spec((1, 128)),
                _const_spec((1, XATT_W)), pl.BlockSpec(memory_space=pltpu.SMEM),
                _const_spec((4 * MEM_LEN, XATT_W), True), _const_spec((4 * MEM_LEN, XATT_W), True),
                _const_spec((1, D_MODEL)), _const_spec((D_MODEL, D_MODEL), True)]
    out_shape = (jax.ShapeDtypeStruct((seq, D_IN), F32), jax.ShapeDtypeStruct((seq, LRU_W), F32),
                 jax.ShapeDtypeStruct((seq, SWA_W), F32), jax.ShapeDtypeStruct((seq, XATT_W), F32),
                 jax.ShapeDtypeStruct((seq, D_MODEL), _MXU), jax.ShapeDtypeStruct((seq, D_MODEL), _MXU),
                 jax.ShapeDtypeStruct((seq, D_MODEL), _MXU), jax.ShapeDtypeStruct((seq, 4 * 2 * BLOCK), _MXU),
                 jax.ShapeDtypeStruct((seq, 4 * MEM_LEN), _MXU), jax.ShapeDtypeStruct((seq, 128), F32),
                 jax.ShapeDtypeStruct((seq, 4 * LRU_W), _MXU), jax.ShapeDtypeStruct((seq, LRU_W), F32),
                 jax.ShapeDtypeStruct((8, 128), F32))
    out_specs = (rows(D_IN), rows(LRU_W), rows(SWA_W), rows(XATT_W), rows(D_MODEL), rows(D_MODEL), rows(D_MODEL),
                 rows(4 * 2 * BLOCK), rows(4 * MEM_LEN), rows(128), rows(4 * LRU_W), rows(LRU_W),
                 _const_spec((8, 128)))
    scratch = [pltpu.VMEM((tm + 8, LRU_W), F32), pltpu.VMEM((tm, LRU_W), F32),
               pltpu.VMEM((8, LRU_W), F32), pltpu.VMEM((BLOCK, KV_W), F32), pltpu.VMEM((BLOCK, KV_W), F32),
               pltpu.VMEM((8, 128), F32)]
    return pl.pallas_call(
        body, name="layer_fwd", grid=(nt,), out_shape=out_shape, in_specs=in_specs, out_specs=out_specs,
        scratch_shapes=scratch,
        compiler_params=pltpu.CompilerParams(dimension_semantics=("arbitrary",), vmem_limit_bytes=VMEM_LIMIT),
    )(x, tgt, rc, rs1, rs2, ng, win_t, cw, cb, wg, brg, big, lam, qg, kg, xqg, sinks, km, vm, og, wout)


def wgrad_reduce(lhs, rhs, bigs, g_small, stage_at, name):
    seq, ncol = rhs.shape
    nblk = lhs.shape[1] // 256
    nres = len(bigs) + (g_small is not None)

    def body(l_ref, r_ref, *refs):
        if nres:
            _hosted_reduce(pl.program_id(0), stage_at, refs[:nres], refs[nres + 1:2 * nres + 1], refs[2 * nres + 1:],
                           g_small is not None)
        refs[nres][...] = _mm_tn(l_ref[...], r_ref[...])

    red_shape, scratch = _hosted_reduce_shapes(bigs, g_small) if nres else ([], [])
    vm = pl.BlockSpec(memory_space=pltpu.VMEM)
    hbm = pl.BlockSpec(memory_space=pl.ANY)
    operands = list(bigs) + ([] if g_small is None else [g_small])
    return pl.pallas_call(
        body, name=name, grid=(nblk,),
        out_shape=(jax.ShapeDtypeStruct((lhs.shape[1], ncol), F32), *red_shape),
        in_specs=[pl.BlockSpec((seq, 256), lambda j: (0, j)), _const_spec((seq, ncol), True)] + [hbm] * len(bigs)
        + [vm] * (g_small is not None),
        out_specs=(pl.BlockSpec((256, ncol), lambda j: (j, 0)),) + (hbm,) * len(red_shape),
        scratch_shapes=scratch,
        compiler_params=pltpu.CompilerParams(dimension_semantics=("arbitrary",), vmem_limit_bytes=VMEM_LIMIT),
    )(lhs, rhs, *operands)


def layer_bwd(x, dout, proj, ya, yb, yc, pswa, pmem, psink, gates, a_all, rc, rs1, rs2, ng, win_t, cw, wg, lam, qg, kg,
              xqg, km, vm, og, wout):
    seq = x.shape[0]
    tm = min(ROW_TILE, seq)
    nt = seq // tm
    nb = tm // BLOCK

    def body(x_ref, dout_ref, proj_ref, ya_ref, yb_ref, yc_ref, pswa_ref, pmem_ref, psink_ref, gates_ref, a_ref,
             c_ref, s1_ref, s2_ref,
             yah_ref, kvh_ref, ch_ref, s1h_ref, s2h_ref,
             ng_ref, win_ref, cw_ref, wg_ref, lam_ref, qg_ref, kg_ref, xqg_ref, km_ref, vm_ref, og_ref, wout_ref,
             gx_ref, dproj_ref, gwg_ref, dkm_ref, dvm_ref, gng_ref, gog_ref, gcb_ref, gbrg_ref, gbig_ref, glam_ref,
             gcw_ref, gqn_ref, gkn_ref, gxqn_ref, gsink_ref,
             hext_ref, aext_ref, an_scr, dh_scr, g_scr, dxc_ext, gcar_ref, dkcar_ref, dvcar_ref):
        i = pl.program_id(0)
        tile = nt - 1 - i
        first_tile = tile == 0

        @pl.when(i == 0)
        def _():
            for r in (gwg_ref, dkm_ref, dvm_ref, gng_ref, gog_ref, gcb_ref, gbrg_ref, gbig_ref, glam_ref, gcw_ref,
                      gqn_ref, gkn_ref, gxqn_ref, gsink_ref, gcar_ref, dkcar_ref, dvcar_ref):
                r[...] = jnp.zeros_like(r)
            dxc_ext[tm:tm + 8, :] = jnp.zeros((8, LRU_W), F32)
            aext_ref[tm:tm + 8, :] = jnp.zeros((8, LRU_W), F32)

        xv = x_ref[...]
        dov = dout_ref[...]
        dz = _mm_nt(dov, wout_ref[...])
        ogv = og_ref[...]

        def group_bwd(y, gate, g, dzg):
            r = lax.rsqrt(_row_mean(y * y) + EPS)
            n = y * r
            sg = _sigmoid(gate)
            dgate = dzg * (n * g) * (sg * (1.0 + gate * (1.0 - sg)))
            dng = dzg * (gate * sg)
            dn = dng * g
            return r * (dn - n * _row_mean(dn * n)), dgate, _col_sum(dng * n)

        dya, dga, goa = group_bwd(ya_ref[...], proj_ref[:, C_LRUG:C_LRUG + LRU_W], ogv[:, :512], dz[:, :512])
        dyb, dgb, gob = group_bwd(yb_ref[...], proj_ref[:, C_SWAG:C_SWAG + SWA_W], ogv[:, 512:768], dz[:, 512:768])
        dyc, dgc, goc = group_bwd(yc_ref[...], proj_ref[:, C_XG:C_XG + XATT_W], ogv[:, 768:], dz[:, 768:])
        gog_ref[...] += jnp.concatenate([goa, gob, goc], axis=1)
        dproj_ref[:, C_LRUG:C_LRUG + LRU_W] = dga.astype(dproj_ref.dtype)
        dproj_ref[:, C_SWAG:C_SWAG + SWA_W] = dgb.astype(dproj_ref.dtype)
        dproj_ref[:, C_XG:C_XG + XATT_W] = dgc.astype(dproj_ref.dtype)

        gm256 = _group_matrix(XATT_W)
        xq = proj_ref[:, C_XQ:C_XQ + XATT_W]
        rq = lax.rsqrt(_seg_mean(xq * xq, gm256) + EPS)
        qn = xq * rq
        qx = qn * xqg_ref[...]
        qxb = qx.astype(_MXU)
        dycb = dyc.astype(_MXU)
        dp_all = _mm_nt(dycb, vm_ref[...])
        dsm = []
        for j in range(4):
            pj = pmem_ref[:, MEM_LEN * j:MEM_LEN * (j + 1)].astype(F32)
            dp = dp_all[:, MEM_LEN * j:MEM_LEN * (j + 1)]
            dsm.append((pj * (dp - jnp.sum(pj * dp, axis=-1, keepdims=True))).astype(_MXU))
        ds_all = jnp.concatenate(dsm, axis=1)
        dvm_ref[...] += _mm_tn(dycb, pmem_ref[...])
        dkm_ref[...] += _mm_tn(qxb, ds_all)
        dqx = _mm(ds_all, km_ref[...])
        gxqn_ref[...] += _col_sum(dqx * qn)
        dqn = dqx * xqg_ref[...]
        dproj_ref[:, C_XQ:C_XQ + XATT_W] = (rq * (dqn - qn * _seg_mean(dqn * qn, gm256))).astype(dproj_ref.dtype)

        gm128 = _group_matrix(KV_W)
        cv, s1v, s2v = c_ref[...], s1_ref[...], s2_ref[...]

        def head_norm(t):
            r = lax.rsqrt(_seg_mean(t * t, gm128) + EPS)
            return t * r, r

        qn_, qr_ = zip(head_norm(proj_ref[:, C_SQ:C_SQ + 128]), head_norm(proj_ref[:, C_SQ + 128:C_SQ + 256]))
        qrope = [_rope(qn_[h] * qg_ref[...], cv, s1v, s2v).astype(_MXU) for h in range(2)]
        kn, krr = head_norm(proj_ref[:, C_SK:C_SK + KV_W])
        kr = _rope(kn * kg_ref[...], cv, s1v, s2v)
        khn, _ = head_norm(kvh_ref[:, 0:KV_W])
        khr = _rope(khn * kg_ref[...], ch_ref[...], s1h_ref[...], s2h_ref[...])
        ka = _place_kv(jnp.concatenate([khr, kr], axis=0), 0.125)
        va = _place_kv(jnp.concatenate([kvh_ref[:, KV_W:2 * KV_W], proj_ref[:, C_SV:C_SV + KV_W]], axis=0), 1.0)
        lane128 = lax.broadcasted_iota(jnp.int32, (1, 128), 1)
        gsink = jnp.zeros((1, 128), F32)
        dk_band, dv_band, dq_blk = [], [], []
        for b in range(nb):
            band = slice(BLOCK * b, BLOCK * b + 2 * BLOCK)
            blk = slice(BLOCK * b, BLOCK * (b + 1))
            dka, dva, dsb = [], [], []
            deltas = jnp.zeros((BLOCK, 128), F32)
            for j in range(4):
                qh = qrope[j // 2][blk]
                doh = dyb[blk, KV_W * (j // 2):KV_W * (j // 2 + 1)].astype(_MXU)
                pb = pswa_ref[blk, 2 * BLOCK * j:2 * BLOCK * (j + 1)]
                p = pb.astype(F32)
                dp = _mm_nt(doh, va[j][band])
                delta = jnp.sum(p * dp, axis=-1, keepdims=True)
                ds = (p * (dp - delta)).astype(_MXU)
                deltas = jnp.where(lane128 == j, delta, deltas)
                dva.append(_mm_tn(pb, doh))
                dka.append(_mm_tn(ds, qh))
                dsb.append(ds)
            gsink = gsink - _col_sum(psink_ref[blk, :] * deltas)
            dk_band.append(_unplace_kv(dka) * 0.125)
            dv_band.append(_unplace_kv(dva))
            dq_blk.append([_mm(jnp.concatenate(dsb[2 * h:2 * h + 2], axis=1),
                               jnp.concatenate([ka[2 * h][band], ka[2 * h + 1][band]], axis=0)) for h in range(2)])
        gsink_ref[...] += gsink
        dk_rows = [dk_band[b][BLOCK:] + (dk_band[b + 1][:BLOCK] if b + 1 < nb else dkcar_ref[...]) for b in range(nb)]
        dv_rows = [dv_band[b][BLOCK:] + (dv_band[b + 1][:BLOCK] if b + 1 < nb else dvcar_ref[...]) for b in range(nb)]
        dkcar_ref[...] = dk_band[0][:BLOCK]
        dvcar_ref[...] = dv_band[0][:BLOCK]
        dkg = _rope_bwd(jnp.concatenate(dk_rows, axis=0), cv, s1v, s2v)
        gkn = _col_sum(dkg * kn)
        dkn = dkg * kg_ref[...]
        dproj_ref[:, C_SK:C_SK + KV_W] = (krr * (dkn - kn * _seg_mean(dkn * kn, gm128))).astype(dproj_ref.dtype)
        dproj_ref[:, C_SV:C_SV + KV_W] = jnp.concatenate(dv_rows, axis=0).astype(dproj_ref.dtype)
        gqn = jnp.zeros((1, 128), F32)
        for h in range(2):
            dqg = _rope_bwd(jnp.concatenate([dq_blk[b][h] for b in range(nb)], axis=0), cv, s1v, s2v)
            gqn = gqn + _col_sum(dqg * qn_[h])
            dqn_ = dqg * qg_ref[...]
            dproj_ref[:, C_SQ + 128 * h:C_SQ + 128 * (h + 1)] = (
                qr_[h] * (dqn_ - qn_[h] * _seg_mean(dqn_ * qn_[h], gm128))).astype(dproj_ref.dtype)
        gqn_ref[...] += gqn
        gkn_ref[...] += gkn

        u = proj_ref[:, C_LRUX:C_LRUX + LRU_W]
        xc, rg, ig, sq = (gates_ref[:, LRU_W * k:LRU_W * (k + 1)].astype(F32) for k in range(4))
        a = a_ref[...]
        sp = _softplus(-lam_ref[...])
        hext_ref[0:8, :] = jnp.where(first_tile, 0.0, yah_ref[...])
        hext_ref[8:8 + tm, :] = ya_ref[...]
        hprev = hext_ref[pl.ds(7, tm), :]
        aext_ref[0:tm, :] = a
        an_scr[...] = aext_ref[pl.ds(1, tm), :]
        dh_scr[...] = dya
        dh_scr[tm - 1:tm, :] = dh_scr[tm - 1:tm, :] + gcar_ref[0:1, :]
        row8 = lax.broadcasted_iota(jnp.int32, (8, LRU_W), 0)

        def scan_step(gi, carry):
            r0 = pl.multiple_of((tm // 8 - 1 - gi) * 8, 8)
            av = an_scr[pl.ds(r0, 8), :]
            bv = dh_scr[pl.ds(r0, 8), :]
            for d in (1, 2, 4):
                a_sh = jnp.where(row8 < 8 - d, pltpu.roll(av, 8 - d, 0), 1.0)
                b_sh = jnp.where(row8 < 8 - d, pltpu.roll(bv, 8 - d, 0), 0.0)
                bv = bv + av * b_sh
                av = av * a_sh
            gv = bv + av * carry
            g_scr[pl.ds(r0, 8), :] = gv
            return gv[0:1, :]

        g0 = lax.fori_loop(0, tm // 8, scan_step, jnp.zeros((1, LRU_W), F32), unroll=True)
        gcar_ref[0:1, :] = a[0:1, :] * g0
        gv = g_scr[...]
        da = gv * hprev
        dig = gv * sq * xc
        dxc = gv * sq * ig
        dla = da * a - gv * (ig * xc) * ((a * a) / sq)
        drg = dla * ((-LRU_C) * sp)
        glam_ref[...] += _col_sum(dla * rg)
        dpr = drg * rg * (1.0 - rg)
        dpi = dig * ig * (1.0 - ig)
        gbrg_ref[...] += _col_sum(dpr)
        gbig_ref[...] += _col_sum(dpi)
        dpre0 = jnp.concatenate([dpr[:, :256], dpi[:, :256]], axis=1).astype(_MXU)
        dpre1 = jnp.concatenate([dpr[:, 256:], dpi[:, 256:]], axis=1).astype(_MXU)
        gwg_ref[0] += _mm_tn(xc[:, :256], dpre0)
        gwg_ref[1] += _mm_tn(xc[:, 256:], dpre1)
        dxc = dxc + jnp.concatenate([_mm_nt(dpre0, wg_ref[0]), _mm_nt(dpre1, wg_ref[1])], axis=1)
        gcb_ref[...] += _col_sum(dxc)
        dxc_ext[0:tm, :] = dxc
        du = jnp.zeros((tm, LRU_W), F32)
        for k in range(CONV_K):
            later = dxc_ext[pl.ds(3 - k, tm), :]
            gcw_ref[k:k + 1, :] += _col_sum(later * u)
            du = du + cw_ref[k:k + 1, :] * later
        dxc_ext[tm:tm + 8, :] = dxc[0:8, :]
        dproj_ref[:, C_LRUX:C_LRUX + LRU_W] = du.astype(dproj_ref.dtype)

        dxn = _mm(dproj_ref[...], win_ref[...])
        rx = lax.rsqrt(_row_mean(xv * xv) + EPS)
        xh = xv * rx
        gng_ref[...] += _col_sum(dxn * xh)
        dxh = dxn * ng_ref[...]
        gx_ref[...] = dov.astype(F32) + rx * (dxh - xh * _row_mean(dxh * xh))

        @pl.when(i == nt - 1)
        def _():
            glam_ref[...] = glam_ref[...] * (LRU_C * _sigmoid(-lam_ref[...]))

    def rows(ncol, arr_cols_block=0):
        return pl.BlockSpec((tm, ncol), lambda i: (nt - 1 - i, arr_cols_block))

    def halo(nrow, ncol, colblk=0):
        per = tm // nrow
        return pl.BlockSpec((nrow, ncol), lambda i: (jnp.maximum((nt - 1 - i) * per - 1, 0), colblk))

    in_specs = [rows(D_MODEL), rows(D_MODEL), rows(D_IN), rows(LRU_W), rows(SWA_W), rows(XATT_W),
                rows(4 * 2 * BLOCK), rows(4 * MEM_LEN), rows(128), rows(4 * LRU_W), rows(LRU_W),
                rows(128), rows(128), rows(128),
                halo(8, LRU_W), halo(BLOCK, 2 * KV_W, C_SK // (2 * KV_W)),
                halo(BLOCK, 128), halo(BLOCK, 128), halo(BLOCK, 128),
                _const_spec((1, D_MODEL)), _const_spec((D_IN, D_MODEL), True), _const_spec((CONV_K, LRU_W)),
                _const_spec((2, 256, 512), True), _const_spec((1, LRU_W)), _const_spec((1, 128)), _const_spec((1, 128)),
                _const_spec((1, XATT_W)),
                _const_spec((4 * MEM_LEN, XATT_W), True), _const_spec((4 * MEM_LEN, XATT_W), True),
                _const_spec((1, D_MODEL)), _const_spec((D_MODEL, D_MODEL), True)]
    small = [(2, 256, 512), (XATT_W, 4 * MEM_LEN), (XATT_W, 4 * MEM_LEN), (1, D_MODEL), (1, D_MODEL), (1, LRU_W), (1, LRU_W),
             (1, LRU_W), (1, LRU_W), (CONV_K, LRU_W), (1, 128), (1, 128), (1, XATT_W), (1, 128)]
    out_shape = (jax.ShapeDtypeStruct((seq, D_MODEL), F32), jax.ShapeDtypeStruct((seq, D_IN), _MXU)) + tuple(
        jax.ShapeDtypeStruct(s, F32) for s in small)
    out_specs = (rows(D_MODEL), rows(D_IN)) + tuple(_const_spec(s) for s in small)
    scratch = [pltpu.VMEM((tm + 8, LRU_W), F32), pltpu.VMEM((tm + 8, LRU_W), F32),
               pltpu.VMEM((tm, LRU_W), F32), pltpu.VMEM((tm, LRU_W), F32), pltpu.VMEM((tm, LRU_W), F32),
               pltpu.VMEM((tm + 8, LRU_W), F32),
               pltpu.VMEM((8, LRU_W), F32), pltpu.VMEM((BLOCK, KV_W), F32), pltpu.VMEM((BLOCK, KV_W), F32)]
    return pl.pallas_call(
        body, name="layer_bwd", grid=(nt,), out_shape=out_shape, in_specs=in_specs, out_specs=out_specs,
        scratch_shapes=scratch,
        compiler_params=pltpu.CompilerParams(dimension_semantics=("arbitrary",), vmem_limit_bytes=VMEM_LIMIT),
    )(x, dout, proj, ya, yb, yc, pswa, pmem, psink, gates, a_all, rc, rs1, rs2, ya, proj, rc, rs1, rs2,
      ng, win_t, cw, wg, lam, qg, kg, xqg, km, vm, og, wout)


def _reduce_protocol(big, sm, outs, osm, r1, r1s, wire, r2, r2s, wire2, ps, own, send, recv, lsem):
    nbig = len(big)
    x, y, c = lax.axis_index("x"), lax.axis_index("y"), lax.axis_index("c")
    sibling = (x, y, 1 - c)
    near, far, diag = _partners(x, y, c)
    me, near_id, far_id, diag_id = _chip_of(x, y), _chip_of(*near), _chip_of(*far), _chip_of(*diag)

    def copy(k, src, dst, to):
        return pltpu.make_async_remote_copy(src_ref=src, dst_ref=dst, send_sem=send.at[k], recv_sem=recv.at[k],
                                            device_id=to, device_id_type=MESH)

    def sent(stage):
        cps = []
        for a in range(nbig):
            if stage == 0:
                cps.append(copy(5 * a, big[a].at[:, 1 - c], r1[a], sibling))
            elif stage == 1:
                cps.append(copy(5 * a + 1, wire[a].at[near_id], r2[a].at[0], (*near, c)))
                cps.append(copy(5 * a + 2, wire[a].at[diag_id], r2[a].at[1], (*near, c)))
            elif stage == 2:
                cps.append(copy(5 * a + 3, wire2[a], r2[a].at[2], (*far, c)))
            elif stage == 3:
                cps.append(copy(5 * a + 4, outs[a].at[c], outs[a].at[c], sibling))
        if sm is not None:
            src, dst, to = ((sm.at[1 - c], r1s, sibling), (r1s, r2s.at[0], (*near, c)), (ps, r2s.at[1], (*far, c)),
                            (osm.at[c], osm.at[c], sibling))[stage]
            cps.append(copy(5 * nbig + stage, src, dst, to))
        return cps

    def arrived(k, ref):
        copy(k, ref, ref, sibling).wait_recv()

    def loads():
        return [pltpu.make_async_copy(big[a].at[:, c], own[a], lsem.at[a]) for a in range(nbig)]

    def stage0():
        for cp in sent(0) + loads():
            cp.start()

    def stage1():
        for a in range(nbig):
            loads()[a].wait()
            arrived(5 * a, r1[a])
            for k in range(N_CHIPS):
                r1[a][k] = own[a][k] + r1[a][k]
                wire[a][k] = r1[a][k].astype(wire[a].dtype)
        if sm is not None:
            arrived(5 * nbig, r1s)
            r1s[...] = sm[c] + r1s[...]
        for cp in sent(1):
            cp.start()

    def stage2():
        for a in range(nbig):
            arrived(5 * a + 1, r2[a].at[0])
            arrived(5 * a + 2, r2[a].at[1])
            r1[a][me] = r1[a][me] + r2[a][0].astype(F32)
            wire2[a][...] = (r1[a][far_id] + r2[a][1].astype(F32)).astype(wire2[a].dtype)
        if sm is not None:
            arrived(5 * nbig + 1, r2s.at[0])
            ps[...] = r1s[...] + r2s[0]
        for cp in sent(2):
            cp.start()

    def stage3():
        for a in range(nbig):
            arrived(5 * a + 3, r2[a].at[2])
            outs[a][c] = r1[a][me] + r2[a][2].astype(F32)
        if sm is not None:
            arrived(5 * nbig + 2, r2s.at[1])
            osm[c] = ps[...] + r2s[1]
        for cp in sent(3):
            cp.start()

    def stage4():
        for a in range(nbig):
            arrived(5 * a + 4, outs[a].at[1 - c])
        if sm is not None:
            arrived(5 * nbig + 3, osm.at[1 - c])
        for stage in range(4):
            for cp in sent(stage):
                cp.wait_send()

    return [stage0, stage1, stage2, stage3, stage4]


def _reduce_buffers(bigs, g_small):
    half = [b.shape[2:] for b in bigs]
    sm_half = None if g_small is None else g_small.shape[1:]
    out_shape = [jax.ShapeDtypeStruct((2,) + h, F32) for h in half]
    small = lambda lead: [] if g_small is None else [pltpu.VMEM(lead + sm_half, F32)]
    if g_small is not None:
        out_shape.append(jax.ShapeDtypeStruct(g_small.shape, F32))
    n_sem = 5 * len(bigs) + 4
    scratch = ([pltpu.VMEM((N_CHIPS,) + h, F32) for h in half] + small(())
               + [pltpu.VMEM((N_CHIPS,) + h, _WIRE) for h in half]
               + [pltpu.VMEM((3,) + h, _WIRE) for h in half] + small((2,))
               + [pltpu.VMEM(h, _WIRE) for h in half] + small(())
               + [pltpu.VMEM((N_CHIPS,) + h, F32) for h in half]
               + [pltpu.SemaphoreType.DMA((n_sem,)), pltpu.SemaphoreType.DMA((n_sem,)),
                  pltpu.SemaphoreType.DMA((len(bigs),))])
    return out_shape, scratch


def _split_reduce_refs(refs, nbig, has_small):
    it = iter(refs)
    take = lambda n: [next(it) for _ in range(n)]
    one = lambda: next(it) if has_small else None
    big, sm = take(nbig), one()
    outs, osm = take(nbig), one()
    r1, r1s, wire, r2, r2s, wire2, ps, own = take(nbig), one(), take(nbig), take(nbig), one(), take(nbig), one(), take(nbig)
    send, recv, lsem = take(3)
    return big, sm, outs, osm, r1, r1s, wire, r2, r2s, wire2, ps, own, send, recv, lsem


def _hosted_reduce_shapes(bigs, g_small):
    red_shape, scratch = _reduce_buffers(bigs, g_small)
    nres = len(red_shape)
    return red_shape, [pltpu.VMEM(r.shape, r.dtype) for r in red_shape] + scratch + [pltpu.SemaphoreType.DMA((nres,))]


def _hosted_reduce(step, stage_at, operands, results, scratch, has_small):
    nres = len(results)
    sums, rest, fsem = scratch[:nres], scratch[nres:-1], scratch[-1]
    refs = tuple(operands) + tuple(sums) + tuple(rest)
    for at, stage in zip(stage_at, _reduce_protocol(*_split_reduce_refs(refs, nres - has_small, has_small))):
        pl.when(step == at)(stage)

    @pl.when(step == stage_at[-1])
    def _():
        out = [pltpu.make_async_copy(sums[k], results[k], fsem.at[k]) for k in range(nres)]
        for cp in out:
            cp.start()
        for cp in out:
            cp.wait()


def reduce_grads(bigs, g_small, name):
    nbig = len(bigs)

    def body(*refs):
        for stage in _reduce_protocol(*_split_reduce_refs(refs, nbig, g_small is not None)):
            stage()

    out_shape, scratch = _reduce_buffers(bigs, g_small)
    vm = pl.BlockSpec(memory_space=pltpu.VMEM)
    hbm = pl.BlockSpec(memory_space=pl.ANY)
    operands = list(bigs) + ([] if g_small is None else [g_small])
    return pl.pallas_call(
        body, name=name, out_shape=tuple(out_shape), in_specs=[hbm] * nbig + [vm] * (g_small is not None),
        out_specs=(vm,) * len(out_shape), scratch_shapes=scratch,
        compiler_params=pltpu.CompilerParams(vmem_limit_bytes=VMEM_LIMIT),
    )(*operands)


def adamw(w, g, m, v, name):
    rows_, cols = w.shape
    tr = max(t for t in range(8, rows_ + 1, 8) if rows_ % t == 0 and t * cols * 4 <= ADAM_BLOCK_BYTES)

    def body(w_ref, g_ref, m_ref, v_ref, d_ref, nm_ref, nv_ref):
        d_ref[...], nm_ref[...], nv_ref[...] = _adam_update(w_ref[...], g_ref[...], m_ref[...], v_ref[...])

    spec = pl.BlockSpec((tr, cols), lambda i: (i, 0))
    shp = jax.ShapeDtypeStruct(w.shape, F32)
    return pl.pallas_call(
        body, name=name, grid=(rows_ // tr,), out_shape=(shp, shp, shp), in_specs=[spec] * 4, out_specs=(spec,) * 3,
        compiler_params=pltpu.CompilerParams(dimension_semantics=("arbitrary",)),
    )(w, g, m, v)


def _adam_update(w, g, m, v):
    nm = ADAM_B1 * m + (1.0 - ADAM_B1) * g
    nv = ADAM_B2 * v + (1.0 - ADAM_B2) * (g * g)
    m_hat = nm / (1.0 - ADAM_B1 ** ADAM_STEP)
    v_hat = nv / (1.0 - ADAM_B2 ** ADAM_STEP)
    return (-ADAM_LR) * (m_hat / (jnp.sqrt(v_hat) + ADAM_EPS) + ADAM_WD * w), nm, nv


def adamw_vectors(g_pack, g_mats, ws, ms, vs):
    nvec, nmat = len(SMALL_VECTORS), len(g_mats)
    n = nvec + nmat

    def body(*refs):
        pk = refs[0]
        gm_refs = refs[1:1 + nmat]
        w_refs, m_refs, v_refs = (refs[1 + nmat + k * n:1 + nmat + (k + 1) * n] for k in range(3))
        outs = refs[1 + nmat + 3 * n:]
        g_out, d_out, nm_out, nv_out = outs[:nvec], outs[nvec:nvec + n], outs[nvec + n:nvec + 2 * n], outs[nvec + 2 * n:]
        chip = 2 * lax.axis_index("x") + lax.axis_index("y")
        for k, (name, row, width) in enumerate(SMALL_VECTORS):
            if name == "conv_w":
                g = jnp.concatenate([pk[pl.ds(row + 4 * t + chip, 1), :] for t in range(CONV_K)], axis=0)[None]
            elif width >= 128:
                g = jnp.concatenate([pk[row + r:row + r + 1, :] for r in range(width // 128)], axis=1)
            else:
                g = pk[row:row + 1, 0:width]
            g_out[k][...] = g
            d_out[k][...], nm_out[k][...], nv_out[k][...] = _adam_update(w_refs[k][...], g, m_refs[k][...], v_refs[k][...])
        for k in range(nvec, n):
            d_out[k][...], nm_out[k][...], nv_out[k][...] = _adam_update(
                w_refs[k][...], gm_refs[k - nvec][...], m_refs[k][...], v_refs[k][...])

    vm = pl.BlockSpec(memory_space=pltpu.VMEM)
    like = [jax.ShapeDtypeStruct(w.shape, F32) for w in ws]
    out_shape = like[:nvec] + like * 3
    return pl.pallas_call(
        body, name="adamw_vectors", out_shape=tuple(out_shape), in_specs=[vm] * (1 + nmat + 3 * n),
        out_specs=(vm,) * len(out_shape),
    )(g_pack, *g_mats, *ws, *ms, *vs)


SMALL_VECTORS = (("norm_g", 0, 1024), ("mem_norm_g", 8, 1024), ("conv_w", 16, 512), ("conv_b", 32, 512),
                 ("b_rg", 292, 512), ("b_ig", 552, 512), ("lru_lambda", 556, 512), ("q_norm_g", 560, 64),
                 ("k_norm_g", 561, 64), ("sinks", 562, 4), ("xq_norm_g", 563, 64), ("xk_norm_g", 564, 64),
                 ("out_norm_g", 565, 1024))
SMALL_MATRICES = (("w_rg", 36), ("w_ig", 296))
LOSS_ROW = 573
SMALL_ROWS = 576


def _pack(parts, rows_):
    flat = jnp.concatenate([p.reshape(-1) for p in parts])
    return jnp.pad(flat, (0, rows_ * 128 - flat.shape[0])).reshape(rows_, 128)


def _pad_to(v, n):
    v = v.reshape(-1)
    return jnp.pad(v, (0, n - v.shape[0]))


def _block_diag_gates(w_rg, w_ig):
    eye = jnp.eye(4, dtype=w_rg.dtype)

    def bd(w4):
        return (w4[:, :, None, :] * eye[:, None, :, None]).reshape(256, 256)

    return jnp.stack([jnp.concatenate([bd(w_rg[4 * h:4 * h + 4]), bd(w_ig[4 * h:4 * h + 4])], axis=1) for h in (0, 1)])


def _diag_blocks(g):
    g6 = g.reshape(2, 4, HEAD, 2, 4, HEAD)
    d = (g6 * jnp.eye(4, dtype=g.dtype)[None, :, None, None, :, None]).sum(axis=4)
    return d[:, :, :, 0].reshape(8, HEAD, HEAD), d[:, :, :, 1].reshape(8, HEAD, HEAD)


def _rope_tables(seq):
    pos = np.arange(seq, dtype=np.float32)
    inv_freq = (np.float32(ROPE_THETA) ** (-(np.arange(0, ROPE_DIM, 2, dtype=np.float32) / np.float32(ROPE_DIM)))
                ).astype(np.float32)
    ang = (pos[:, None] * inv_freq[None, :]).astype(np.float32)
    cos, sin = np.cos(ang).astype(np.float32), np.sin(ang).astype(np.float32)
    z = lambda n: np.zeros((seq, n), np.float32)
    c64 = np.concatenate([cos, cos, np.ones((seq, HEAD - ROPE_DIM), np.float32)], axis=1)
    s1_64 = np.concatenate([-sin, z(HEAD - 8)], axis=1)
    s2_64 = np.concatenate([z(8), sin, z(HEAD - ROPE_DIM)], axis=1)
    return tuple(jnp.asarray(np.concatenate([t, t], axis=1)) for t in (c64, s1_64, s2_64))


def kernel(x, mem, norm_g, mem_norm_g, w_in, conv_w, conv_b, w_rg, b_rg, w_ig, b_ig, lru_lambda, q_norm_g, k_norm_g, sinks, w_mem_kv, xq_norm_g, xk_norm_g, out_norm_g, w_out, loss_target, m_norm_g, m_mem_norm_g, m_w_in, m_conv_w, m_conv_b, m_w_rg, m_b_rg, m_w_ig, m_b_ig, m_lru_lambda, m_q_norm_g, m_k_norm_g, m_sinks, m_w_mem_kv, m_xq_norm_g, m_xk_norm_g, m_out_norm_g, m_w_out, v_norm_g, v_mem_norm_g, v_w_in, v_conv_w, v_conv_b, v_w_rg, v_b_rg, v_w_ig, v_b_ig, v_lru_lambda, v_q_norm_g, v_k_norm_g, v_sinks, v_w_mem_kv, v_xq_norm_g, v_xk_norm_g, v_out_norm_g, v_w_out):
    seq = x.shape[1]
    xs, tgt, mems = x[0], loss_target[0], mem[0]

    win_t_sh = w_in[0].T.astype(_MXU)
    cw_sh = jnp.pad(conv_w[0], ((0, 4), (0, 0)))
    win_t, wout, wkv, cw_all = gather_weights(win_t_sh, w_out[0].astype(_MXU), w_mem_kv[0].astype(_MXU), cw_sh)
    cw = cw_all.reshape(N_CHIPS, 8, 128)[:, :CONV_K].transpose(1, 0, 2).reshape(CONV_K, LRU_W)

    rc, rs1, rs2 = _rope_tables(seq)
    wg = _block_diag_gates(w_rg[0], w_ig[0]).astype(_MXU)
    qg = jnp.tile(q_norm_g, (1, 2))
    kg = jnp.tile(k_norm_g, (1, 2))
    xqg = jnp.tile(xq_norm_g, (1, 4))
    xkg = jnp.tile(xk_norm_g, (1, 4))

    km, vm = mem_fwd(mems, mem_norm_g, wkv, xkg)
    proj, ya, yb, yc, ycat, xn, dout, pswa, pmem, psink, gates, a_all, loss8 = layer_fwd(
        xs, tgt, rc, rs1, rs2, norm_g, win_t, cw, conv_b, wg, b_rg, b_ig, lru_lambda, qg, kg, xqg, sinks, km, vm,
        out_norm_g, wout)
    (g_wout,) = wgrad_reduce(ycat, dout, [], None, (), "wgrad_out")
    (gx, dproj, g_wg, dkm, dvm, g_ng, g_og, g_cb, g_brg, g_big, g_lam, g_cw, g_qn, g_kn, g_xqn, g_sink) = layer_bwd(
        xs, dout, proj, ya, yb, yc, pswa, pmem, psink, gates, a_all, rc, rs1, rs2, norm_g, win_t, cw, wg, lru_lambda, qg,
        kg, xqg, km, vm, out_norm_g, wout)
    g_wkv, g_mng, g_xkn = mem_bwd(mems, mem_norm_g, wkv, xkg, dkm, dvm)

    g_wrg, g_wig = _diag_blocks(g_wg)
    fold = lambda v, n: v.reshape(n, HEAD).sum(axis=0)
    small_g = _pack([g_ng, g_mng, g_cw, g_cb, g_wrg, g_brg, g_wig, g_big, g_lam, _pad_to(fold(g_qn, 2), 128),
                     _pad_to(fold(g_kn, 2), 128), g_sink, _pad_to(fold(g_xqn, 4), 128), _pad_to(fold(g_xkn, 4), 128),
                     g_og, loss8[0:1]], SMALL_ROWS)
    early = [g_wout.reshape(N_CHIPS, 2, D_MODEL // 8, D_MODEL), g_wkv.reshape(N_CHIPS, 2, D_MODEL // 8, 2 * XATT_W)]
    g_win_t, r_out, r_kv, r_small = wgrad_reduce(dproj, xn, early, small_g.reshape(2, SMALL_ROWS // 2, 128),
                                                 (0, 1, 6, 8, 8), "wgrad_in")
    (r_in,) = reduce_grads([g_win_t.reshape(N_CHIPS, 2, D_IN // 8, D_MODEL)], None, "reduce_w_in")

    r_small = r_small.reshape(SMALL_ROWS, 128)
    loss = r_small[LOSS_ROW, 0]
    grads = {"w_in": r_in.reshape(D_IN // 4, D_MODEL).T[None], "w_mem_kv": r_kv.reshape(D_MODEL // 4, 2 * XATT_W)[None],
             "w_out": r_out.reshape(D_MODEL // 4, D_MODEL)[None]}
    for name, row in SMALL_MATRICES:
        grads[name] = r_small[row:row + 256].reshape(1, LRU_BLOCKS, HEAD, HEAD)
    weights = dict(norm_g=norm_g, mem_norm_g=mem_norm_g, w_in=w_in, conv_w=conv_w, conv_b=conv_b, w_rg=w_rg, b_rg=b_rg,
                   w_ig=w_ig, b_ig=b_ig, lru_lambda=lru_lambda, q_norm_g=q_norm_g, k_norm_g=k_norm_g, sinks=sinks,
                   w_mem_kv=w_mem_kv, xq_norm_g=xq_norm_g, xk_norm_g=xk_norm_g, out_norm_g=out_norm_g, w_out=w_out)
    ms = dict(norm_g=m_norm_g, mem_norm_g=m_mem_norm_g, w_in=m_w_in, conv_w=m_conv_w, conv_b=m_conv_b, w_rg=m_w_rg,
              b_rg=m_b_rg, w_ig=m_w_ig, b_ig=m_b_ig, lru_lambda=m_lru_lambda, q_norm_g=m_q_norm_g, k_norm_g=m_k_norm_g,
              sinks=m_sinks, w_mem_kv=m_w_mem_kv, xq_norm_g=m_xq_norm_g, xk_norm_g=m_xk_norm_g,
              out_norm_g=m_out_norm_g, w_out=m_w_out)
    vs = dict(norm_g=v_norm_g, mem_norm_g=v_mem_norm_g, w_in=v_w_in, conv_w=v_conv_w, conv_b=v_conv_b, w_rg=v_w_rg,
              b_rg=v_b_rg, w_ig=v_w_ig, b_ig=v_b_ig, lru_lambda=v_lru_lambda, q_norm_g=v_q_norm_g, k_norm_g=v_k_norm_g,
              sinks=v_sinks, w_mem_kv=v_w_mem_kv, xq_norm_g=v_xq_norm_g, xk_norm_g=v_xk_norm_g,
              out_norm_g=v_out_norm_g, w_out=v_w_out)

    delta, new_m, new_v = {}, {}, {}
    d2, m2, v2 = adamw(w_in[0].T, r_in.reshape(D_IN // 4, D_MODEL), m_w_in[0].T, v_w_in[0].T, "adamw_w_in")
    delta["w_in"], new_m["w_in"], new_v["w_in"] = d2.T[None], m2.T[None], v2.T[None]
    for name in ("w_mem_kv", "w_out"):
        shp = weights[name].shape
        d2, m2, v2 = adamw(weights[name][0], grads[name][0], ms[name][0], vs[name][0], "adamw_" + name)
        delta[name], new_m[name], new_v[name] = d2.reshape(shp), m2.reshape(shp), v2.reshape(shp)
    vec_names = [n for n, _, _ in SMALL_VECTORS]
    small_names = vec_names + [n for n, _ in SMALL_MATRICES]
    res = adamw_vectors(r_small, [grads[n] for n, _ in SMALL_MATRICES], [weights[n] for n in small_names],
                        [ms[n] for n in small_names], [vs[n] for n in small_names])
    nvec, nall = len(vec_names), len(small_names)
    grads.update(zip(vec_names, res[:nvec]))
    delta.update(zip(small_names, res[nvec:nvec + nall]))
    new_m.update(zip(small_names, res[nvec + nall:nvec + 2 * nall]))
    new_v.update(zip(small_names, res[nvec + 2 * nall:]))

    order = ("norm_g", "mem_norm_g", "w_in", "conv_w", "conv_b", "w_rg", "b_rg", "w_ig", "b_ig", "lru_lambda",
             "q_norm_g", "k_norm_g", "sinks", "w_mem_kv", "xq_norm_g", "xk_norm_g", "out_norm_g", "w_out")
    return (loss, gx[None], *[grads[n] for n in order], *[delta[n] for n in order], *[new_m[n] for n in order],
            *[new_v[n] for n in order])
```

```python
import jax
import jax.numpy as jnp
import numpy as np
from jax import lax
from jax.experimental import pallas as pl
from jax.experimental.pallas import tpu as pltpu

F32 = jnp.float32
_MXU = jnp.bfloat16
_WIRE = jnp.bfloat16

D_MODEL = 1024
MEM_LEN = 256
HEAD = 64
LRU_W = 512
LRU_BLOCKS = 8
CONV_K = 4
LRU_C = 8.0
SWA_W = 256
KV_W = 128
XATT_W = 256
BLOCK = 128
D_IN = 2304
ROPE_THETA = 500000.0
ROPE_DIM = 16
EPS = 1e-6
NEG_INF = -1e30
C_LRUX, C_LRUG, C_SQ, C_SK, C_SV, C_SWAG, C_XQ, C_XG = 0, 512, 1024, 1280, 1408, 1536, 1792, 2048

ADAM_LR, ADAM_B1, ADAM_B2, ADAM_EPS, ADAM_WD, ADAM_STEP = 0.001, 0.9, 0.999, 1e-08, 0.01, 10

N_CHIPS = 4
ROW_TILE = 256
VMEM_LIMIT = 56 * 1024 * 1024
ADAM_BLOCK_BYTES = 1280 * 1024
MESH = pl.DeviceIdType.MESH


def _mm(a, b):
    return jnp.dot(a.astype(_MXU), b.astype(_MXU), preferred_element_type=F32)


def _mm_nt(a, b):
    return lax.dot_general(a.astype(_MXU), b.astype(_MXU), (((1,), (1,)), ((), ())), preferred_element_type=F32)


def _mm_tn(a, b):
    return lax.dot_general(a.astype(_MXU), b.astype(_MXU), (((0,), (0,)), ((), ())), preferred_element_type=F32)


def _group_matrix(width):
    r = lax.shift_right_logical(lax.broadcasted_iota(jnp.int32, (width, width), 0), 6)
    c = lax.shift_right_logical(lax.broadcasted_iota(jnp.int32, (width, width), 1), 6)
    return (r == c).astype(_MXU)


def _seg_mean(x, gm):
    return jnp.dot(x.astype(_MXU), gm, preferred_element_type=F32) * (1.0 / HEAD)


def _row_mean(x):
    return jnp.mean(x, axis=-1, keepdims=True)


def _col_sum(x):
    return jnp.sum(x, axis=0, keepdims=True)


def _sigmoid(x):
    return jax.nn.sigmoid(x)


def _softplus(z):
    e = jnp.exp(-jnp.abs(z))
    u = 1.0 + e
    log1p_e = jnp.where(u == 1.0, e, jnp.log(u) * (e / (u - 1.0)))
    return jnp.maximum(z, 0.0) + log1p_e


def _rope(t, c, s1, s2):
    return t * c + pltpu.roll(t, 120, 1) * s1 + pltpu.roll(t, 8, 1) * s2


def _rope_bwd(d, c, s1, s2):
    return d * c + pltpu.roll(d * s1, 8, 1) + pltpu.roll(d * s2, 120, 1)


def _lane_mask(width, lo, hi):
    lane = lax.broadcasted_iota(jnp.int32, (1, width), 1)
    return ((lane >= lo) & (lane < hi)).astype(F32)


def _swa_mask(first_block):
    qi = lax.broadcasted_iota(jnp.int32, (BLOCK, 2 * BLOCK), 0)
    kj = lax.broadcasted_iota(jnp.int32, (BLOCK, 2 * BLOCK), 1)
    rel = qi + BLOCK - kj
    ok = (rel >= 0) & (rel < BLOCK)
    return ok & (jnp.logical_not(first_block) | (kj >= BLOCK))


def _place_kv(t, scale):
    lo = t * (_lane_mask(KV_W, 0, HEAD) * scale)
    hi = t * (_lane_mask(KV_W, HEAD, KV_W) * scale)
    return [a.astype(_MXU) for a in (lo, pltpu.roll(lo, HEAD, 1), pltpu.roll(hi, HEAD, 1), hi)]


def _unplace_kv(d):
    return (_lane_mask(KV_W, 0, HEAD) * (d[0] + pltpu.roll(d[1], HEAD, 1))
            + _lane_mask(KV_W, HEAD, KV_W) * (d[3] + pltpu.roll(d[2], HEAD, 1)))


def _swa_probs(qh, ka, mask, sink):
    s = _mm_nt(qh, ka)
    s = jnp.where(mask, s, NEG_INF)
    m = jnp.maximum(jnp.max(s, axis=-1, keepdims=True), sink)
    p = jnp.exp(s - m)
    esink = jnp.exp(sink - m)
    inv = 1.0 / (jnp.sum(p, axis=-1, keepdims=True) + esink)
    return p * inv, esink * inv


def _mem_probs(s_all):
    out = []
    for j in range(4):
        s = s_all[:, MEM_LEN * j:MEM_LEN * (j + 1)]
        p = jnp.exp(s - jnp.max(s, axis=-1, keepdims=True))
        out.append(p * (1.0 / jnp.sum(p, axis=-1, keepdims=True)))
    return out


def _head_rows(t, scale):
    return jnp.concatenate([t * (_lane_mask(XATT_W, HEAD * j, HEAD * (j + 1)) * scale) for j in range(4)], axis=0)


def _lru_gates(xc, wg_ref, brg, big, lam):
    p0 = _mm(xc[:, :256], wg_ref[0])
    p1 = _mm(xc[:, 256:], wg_ref[1])
    rg = _sigmoid(jnp.concatenate([p0[:, :256], p1[:, :256]], axis=1) + brg)
    ig = _sigmoid(jnp.concatenate([p0[:, 256:], p1[:, 256:]], axis=1) + big)
    sp = _softplus(-lam)
    la = (-LRU_C) * rg * sp
    a = jnp.exp(la)
    th = jnp.tanh(la)
    one_minus_a2 = (-2.0 * th) / (1.0 - th)
    return rg, ig, sp, a, jnp.sqrt(one_minus_a2)


def _const_spec(shape, single=False):
    zeros = (0,) * len(shape)
    if single:
        return pl.BlockSpec(shape, lambda i: zeros, pipeline_mode=pl.Buffered(1))
    return pl.BlockSpec(shape, lambda i: zeros)


def _chip_of(x, y):
    return 2 * x + y


def _partners(x, y, c):
    north = c == 1
    near = (jnp.where(north, 1 - x, x), jnp.where(north, y, 1 - y))
    far = (jnp.where(north, x, 1 - x), jnp.where(north, 1 - y, y))
    return near, far, (1 - x, 1 - y)


def gather_weights(win_t, wout, wkv, convw):
    arrs = (win_t, wout, wkv)
    n = len(arrs)
    pieces = [(a, 0, arr.shape[0] // 2) for a, arr in enumerate(arrs)]
    npc = len(pieces)

    def body(a0, a1, a2, cw, o0, o1, o2, ocw, send, recv, lsem):
        ins, outs = (a0, a1, a2), (o0, o1, o2)
        x, y, c = lax.axis_index("x"), lax.axis_index("y"), lax.axis_index("c")
        sibling = (x, y, 1 - c)
        near, far, diag = _partners(x, y, c)
        chips = [near, far, diag]
        me = _chip_of(x, y)

        def landed(p, chip, half):
            a, off, rows_ = pieces[p]
            r = ins[a].shape[0]
            return outs[a].at[pl.ds(pl.multiple_of(chip * r + half * (r // 2) + off, 16), rows_)]

        def mine(p):
            a, off, rows_ = pieces[p]
            return ins[a].at[pl.ds(pl.multiple_of(c * (ins[a].shape[0] // 2) + off, 16), rows_)]

        def copy(k, src, dst, to):
            return pltpu.make_async_remote_copy(src_ref=src, dst_ref=dst, send_sem=send.at[k], recv_sem=recv.at[k],
                                                device_id=to, device_id_type=MESH)

        def cw_rows(chip):
            return ocw.at[pl.ds(pl.multiple_of(chip * 8, 8), 8)]

        locals_ = []
        for a in range(n):
            r = ins[a].shape[0]
            locals_.append(pltpu.make_async_copy(ins[a], outs[a].at[pl.ds(pl.multiple_of(me * r, 16), r)], lsem.at[a]))
        locals_.append(pltpu.make_async_copy(cw, cw_rows(me), lsem.at[n]))
        for cp in locals_:
            cp.start()

        sent = []
        for p in range(npc):
            for j in range(2):
                sent.append(copy(p * 6 + j, mine(p), landed(p, me, c), (*chips[j], c)))
        for j, chip in enumerate(chips):
            sent.append(copy(npc * 6 + j, cw, cw_rows(me), (*chip, c)))
        for cp in sent:
            cp.start()
        for p in range(npc):
            for j in range(3):
                got = landed(p, _chip_of(*chips[j]), c)
                copy(p * 6 + j, got, got, sibling).wait_recv()
                if j == 0:
                    sent.append(copy(p * 6 + 2, got, got, (*far, c)))
                    sent[-1].start()
                sent.append(copy(p * 6 + 3 + j, got, got, sibling))
                sent[-1].start()
        for p in range(npc):
            for j in range(3):
                got = landed(p, _chip_of(*chips[(1, 0, 2)[j]]), 1 - c)
                copy(p * 6 + 3 + j, got, got, sibling).wait_recv()
        for j, chip in enumerate(chips):
            got = cw_rows(_chip_of(*chip))
            copy(npc * 6 + j, got, got, (*chip, c)).wait_recv()
        for cp in sent:
            cp.wait_send()
        for cp in locals_:
            cp.wait()

    vm = pl.BlockSpec(memory_space=pltpu.VMEM)
    out_shape = tuple(jax.ShapeDtypeStruct((N_CHIPS * a.shape[0],) + a.shape[1:], a.dtype) for a in arrs) + (
        jax.ShapeDtypeStruct((N_CHIPS * 8, 128), F32),)
    n_rdma = npc * 6 + 3
    return pl.pallas_call(
        body, name="gather_weights", out_shape=out_shape,
        in_specs=[vm] * 4, out_specs=(vm,) * 4,
        scratch_shapes=[pltpu.SemaphoreType.DMA((n_rdma,)), pltpu.SemaphoreType.DMA((n_rdma,)),
                        pltpu.SemaphoreType.DMA((n + 1,))],
        compiler_params=pltpu.CompilerParams(vmem_limit_bytes=VMEM_LIMIT),
    )(win_t, wout, wkv, convw)


def mem_fwd(mem, mem_g, wkv, xk_g):
    def body(mem_ref, g_ref, w_ref, xk_ref, km_ref, vm_ref):
        mem_v = mem_ref[...]
        mn = mem_v * lax.rsqrt(_row_mean(mem_v * mem_v) + EPS) * g_ref[...]
        mkv = _mm(mn, w_ref[...])
        kpre = mkv[:, :XATT_W]
        gm = _group_matrix(XATT_W)
        km = kpre * lax.rsqrt(_seg_mean(kpre * kpre, gm) + EPS) * xk_ref[...]
        km_ref[...] = _head_rows(km, 0.125).astype(km_ref.dtype)
        vm_ref[...] = _head_rows(mkv[:, XATT_W:], 1.0).astype(vm_ref.dtype)

    vm = pl.BlockSpec(memory_space=pltpu.VMEM)
    rows_shape = jax.ShapeDtypeStruct((4 * MEM_LEN, XATT_W), _MXU)
    return pl.pallas_call(
        body, name="mem_fwd", out_shape=(rows_shape, rows_shape), in_specs=[vm] * 4, out_specs=(vm, vm),
    )(mem, mem_g, wkv, xk_g)


def mem_bwd(mem, mem_g, wkv, xk_g, dkm, dvm):
    def body(mem_ref, g_ref, w_ref, xk_ref, dkm_ref, dvm_ref, gw_ref, gg_ref, gxk_ref):
        mem_v = mem_ref[...]
        mh = mem_v * lax.rsqrt(_row_mean(mem_v * mem_v) + EPS)
        mn = mh * g_ref[...]
        mkv = _mm(mn, w_ref[...])
        kpre = mkv[:, :XATT_W]
        gm = _group_matrix(XATT_W)
        rk = lax.rsqrt(_seg_mean(kpre * kpre, gm) + EPS)
        kn = kpre * rk
        dk = jnp.zeros((MEM_LEN, XATT_W), F32)
        dv = jnp.zeros((MEM_LEN, XATT_W), F32)
        for j in range(4):
            mj = _lane_mask(XATT_W, HEAD * j, HEAD * (j + 1))
            dk = dk + dkm_ref[:, MEM_LEN * j:MEM_LEN * (j + 1)].T * (mj * 0.125)
            dv = dv + dvm_ref[:, MEM_LEN * j:MEM_LEN * (j + 1)].T * mj
        gxk_ref[...] = _col_sum(dk * kn)
        dkn = dk * xk_ref[...]
        dkpre = rk * (dkn - kn * _seg_mean(dkn * kn, gm))
        dmkv = jnp.concatenate([dkpre, dv], axis=1)
        gw_ref[...] = _mm_tn(mn, dmkv)
        dmn = _mm_nt(dmkv, w_ref[...])
        gg_ref[...] = _col_sum(dmn * mh)

    vm = pl.BlockSpec(memory_space=pltpu.VMEM)
    return pl.pallas_call(
        body, name="mem_bwd",
        out_shape=(jax.ShapeDtypeStruct((D_MODEL, 2 * XATT_W), F32), jax.ShapeDtypeStruct((1, D_MODEL), F32),
                   jax.ShapeDtypeStruct((1, XATT_W), F32)),
        in_specs=[vm] * 6, out_specs=(vm, vm, vm),
    )(mem, mem_g, wkv, xk_g, dkm, dvm)


def layer_fwd(x, tgt, rc, rs1, rs2, ng, win_t, cw, cb, wg, brg, big, lam, qg, kg, xqg, sinks, km, vm, og, wout):
    seq = x.shape[0]
    tm = min(ROW_TILE, seq)
    nt = seq // tm
    nb = tm // BLOCK

    def body(x_ref, t_ref, c_ref, s1_ref, s2_ref, ng_ref, win_ref, cw_ref, cb_ref, wg_ref, brg_ref, big_ref, lam_ref,
             qg_ref, kg_ref, xqg_ref, sink_ref, km_ref, vm_ref, og_ref, wout_ref,
             proj_ref, ya_ref, yb_ref, yc_ref, ycat_ref, xn_ref, dout_ref, pswa_ref, pmem_ref, psink_ref, gates_ref,
             a_ref, loss_ref,
             ext_ref, b_scr, hc_ref, kp_ref, vp_ref, lacc_ref):
        i = pl.program_id(0)

        @pl.when(i == 0)
        def _():
            ext_ref[0:8, :] = jnp.zeros((8, LRU_W), F32)
            hc_ref[...] = jnp.zeros_like(hc_ref)
            kp_ref[...] = jnp.zeros_like(kp_ref)
            vp_ref[...] = jnp.zeros_like(vp_ref)
            lacc_ref[...] = jnp.zeros_like(lacc_ref)

        xv = x_ref[...]
        xn = (xv * lax.rsqrt(_row_mean(xv * xv) + EPS) * ng_ref[...]).astype(_MXU)
        xn_ref[...] = xn.astype(xn_ref.dtype)
        proj_ref[...] = _mm_nt(xn, win_ref[...])

        u = proj_ref[:, C_LRUX:C_LRUX + LRU_W]
        ext_ref[8:8 + tm, :] = u
        xc = cb_ref[...]
        for k in range(CONV_K):
            xc = xc + cw_ref[k:k + 1, :] * ext_ref[pl.ds(5 + k, tm), :]
        ext_ref[0:8, :] = u[tm - 8:tm, :]
        rg, ig, sp, a, sq = _lru_gates(xc, wg_ref, brg_ref[...], big_ref[...], lam_ref[...])
        for k, t in enumerate((xc, rg, ig, sq)):
            gates_ref[:, LRU_W * k:LRU_W * (k + 1)] = t.astype(gates_ref.dtype)
        a_ref[...] = a
        b_scr[...] = sq * (ig * xc)
        row8 = lax.broadcasted_iota(jnp.int32, (8, LRU_W), 0)

        def scan_step(g, carry):
            r0 = pl.multiple_of(g * 8, 8)
            av = a_ref[pl.ds(r0, 8), :]
            bv = b_scr[pl.ds(r0, 8), :]
            for d in (1, 2, 4):
                a_sh = jnp.where(row8 >= d, pltpu.roll(av, d, 0), 1.0)
                b_sh = jnp.where(row8 >= d, pltpu.roll(bv, d, 0), 0.0)
                bv = bv + av * b_sh
                av = av * a_sh
            hv = bv + av * carry
            ya_ref[pl.ds(r0, 8), :] = hv
            return hv[7:8, :]

        hc_ref[0:1, :] = lax.fori_loop(0, tm // 8, scan_step, hc_ref[0:1, :], unroll=True)

        gm128 = _group_matrix(KV_W)
        cv, s1v, s2v = c_ref[...], s1_ref[...], s2_ref[...]

        def head_norm_rope(t, g):
            n = t * lax.rsqrt(_seg_mean(t * t, gm128) + EPS)
            return _rope(n * g, cv, s1v, s2v)

        qs_ = (head_norm_rope(proj_ref[:, C_SQ:C_SQ + 128], qg_ref[...]).astype(_MXU),
               head_norm_rope(proj_ref[:, C_SQ + 128:C_SQ + 256], qg_ref[...]).astype(_MXU))
        kr = head_norm_rope(proj_ref[:, C_SK:C_SK + KV_W], kg_ref[...])
        sv = proj_ref[:, C_SV:C_SV + KV_W]
        ka = _place_kv(jnp.concatenate([kp_ref[...], kr], axis=0), 0.125)
        va = _place_kv(jnp.concatenate([vp_ref[...], sv], axis=0), 1.0)
        kp_ref[...] = kr[tm - BLOCK:tm, :]
        vp_ref[...] = sv[tm - BLOCK:tm, :]
        lane128 = lax.broadcasted_iota(jnp.int32, (1, 128), 1)
        for b in range(nb):
            mask = _swa_mask((i == 0) & (b == 0)) if b == 0 else _swa_mask(False)
            band = slice(BLOCK * b, BLOCK * b + 2 * BLOCK)
            blk = slice(BLOCK * b, BLOCK * (b + 1))
            psink = jnp.zeros((BLOCK, 128), F32)
            for j in range(4):
                p, pk = _swa_probs(qs_[j // 2][blk], ka[j][band], mask, sink_ref[0, j])
                pswa_ref[blk, 2 * BLOCK * j:2 * BLOCK * (j + 1)] = p.astype(pswa_ref.dtype)
                psink = jnp.where(lane128 == j, pk, psink)
            psink_ref[blk, :] = psink
            for h in range(2):
                yb_ref[blk, KV_W * h:KV_W * (h + 1)] = _mm(
                    pswa_ref[blk, 4 * BLOCK * h:4 * BLOCK * (h + 1)],
                    jnp.concatenate([va[2 * h][band], va[2 * h + 1][band]], axis=0))

        gm256 = _group_matrix(XATT_W)
        xq = proj_ref[:, C_XQ:C_XQ + XATT_W]
        qx = xq * lax.rsqrt(_seg_mean(xq * xq, gm256) + EPS) * xqg_ref[...]
        pm = _mem_probs(_mm_nt(qx, km_ref[...]))
        for j in range(4):
            pmem_ref[:, MEM_LEN * j:MEM_LEN * (j + 1)] = pm[j].astype(pmem_ref.dtype)
        yc = _mm(pmem_ref[...], vm_ref[...])
        yc_ref[...] = yc

        def gated(y, g, gate):
            return y * lax.rsqrt(_row_mean(y * y) + EPS) * g * (gate * _sigmoid(gate))

        ogv = og_ref[...]
        za = gated(ya_ref[...], ogv[:, :512], proj_ref[:, C_LRUG:C_LRUG + LRU_W])
        zb = gated(yb_ref[...], ogv[:, 512:768], proj_ref[:, C_SWAG:C_SWAG + SWA_W])
        zc = gated(yc, ogv[:, 768:], proj_ref[:, C_XG:C_XG + XATT_W])
        ycat_ref[:, 0:512] = za.astype(ycat_ref.dtype)
        ycat_ref[:, 512:768] = zb.astype(ycat_ref.dtype)
        ycat_ref[:, 768:1024] = zc.astype(ycat_ref.dtype)
        out = xv + _mm(ycat_ref[...], wout_ref[...])
        err = out - t_ref[...]
        dout_ref[...] = (err * (1.0 / D_MODEL)).astype(dout_ref.dtype)
        lacc_ref[...] = lacc_ref[...] + (0.5 / D_MODEL) * jnp.sum(err * err)

        @pl.when(i == nt - 1)
        def _():
            loss_ref[...] = lacc_ref[...]

    def rows(ncol):
        return pl.BlockSpec((tm, ncol), lambda i: (i, 0))

    in_specs = [rows(D_MODEL), rows(D_MODEL), rows(128), rows(128), rows(128),
                _const_spec((1, D_MODEL)), _const_spec((D_IN, D_MODEL), True), _const_spec((CONV_K, LRU_W)),
                _const_spec((1, LRU_W)), _const_spec((2, 256, 512), True), _const_spec((1, LRU_W)),
                _const_spec((1, LRU_W)), _const_spec((1, LRU_W)), _const_spec((1, 128)), _const_spec((1, 128)),
                _const_spec((1, XATT_W)), pl.BlockSpec(memory_space=pltpu.SMEM),
                _const_spec((4 * MEM_LEN, XATT_W), True), _const_spec((4 * MEM_LEN, XATT_W), True),
                _const_spec((1, D_MODEL)), _const_spec((D_MODEL, D_MODEL), True)]
    out_shape = (jax.ShapeDtypeStruct((seq, D_IN), F32), jax.ShapeDtypeStruct((seq, LRU_W), F32),
                 jax.ShapeDtypeStruct((seq, SWA_W), F32), jax.ShapeDtypeStruct((seq, XATT_W), F32),
                 jax.ShapeDtypeStruct((seq, D_MODEL), _MXU), jax.ShapeDtypeStruct((seq, D_MODEL), _MXU),
                 jax.ShapeDtypeStruct((seq, D_MODEL), _MXU), jax.ShapeDtypeStruct((seq, 4 * 2 * BLOCK), _MXU),
                 jax.ShapeDtypeStruct((seq, 4 * MEM_LEN), _MXU), jax.ShapeDtypeStruct((seq, 128), F32),
                 jax.ShapeDtypeStruct((seq, 4 * LRU_W), _MXU), jax.ShapeDtypeStruct((seq, LRU_W), F32),
                 jax.ShapeDtypeStruct((8, 128), F32))
    out_specs = (rows(D_IN), rows(LRU_W), rows(SWA_W), rows(XATT_W), rows(D_MODEL), rows(D_MODEL), rows(D_MODEL),
                 rows(4 * 2 * BLOCK), rows(4 * MEM_LEN), rows(128), rows(4 * LRU_W), rows(LRU_W),
                 _const_spec((8, 128)))
    scratch = [pltpu.VMEM((tm + 8, LRU_W), F32), pltpu.VMEM((tm, LRU_W), F32),
               pltpu.VMEM((8, LRU_W), F32), pltpu.VMEM((BLOCK, KV_W), F32), pltpu.VMEM((BLOCK, KV_W), F32),
               pltpu.VMEM((8, 128), F32)]
    return pl.pallas_call(
        body, name="layer_fwd", grid=(nt,), out_shape=out_shape, in_specs=in_specs, out_specs=out_specs,
        scratch_shapes=scratch,
        compiler_params=pltpu.CompilerParams(dimension_semantics=("arbitrary",), vmem_limit_bytes=VMEM_LIMIT),
    )(x, tgt, rc, rs1, rs2, ng, win_t, cw, cb, wg, brg, big, lam, qg, kg, xqg, sinks, km, vm, og, wout)


def wgrad_reduce(lhs, rhs, bigs, g_small, stage_at, name):
    seq, ncol = rhs.shape
    nblk = lhs.shape[1] // 256
    nres = len(bigs) + (g_small is not None)

    def body(l_ref, r_ref, *refs):
        if nres:
            _hosted_reduce(pl.program_id(0), stage_at, refs[:nres], refs[nres + 1:2 * nres + 1], refs[2 * nres + 1:],
                           g_small is not None)
        refs[nres][...] = _mm_tn(l_ref[...], r_ref[...])

    red_shape, scratch = _hosted_reduce_shapes(bigs, g_small) if nres else ([], [])
    vm = pl.BlockSpec(memory_space=pltpu.VMEM)
    hbm = pl.BlockSpec(memory_space=pl.ANY)
    operands = list(bigs) + ([] if g_small is None else [g_small])
    return pl.pallas_call(
        body, name=name, grid=(nblk,),
        out_shape=(jax.ShapeDtypeStruct((lhs.shape[1], ncol), F32), *red_shape),
        in_specs=[pl.BlockSpec((seq, 256), lambda j: (0, j)), _const_spec((seq, ncol), True)] + [hbm] * len(bigs)
        + [vm] * (g_small is not None),
        out_specs=(pl.BlockSpec((256, ncol), lambda j: (j, 0)),) + (hbm,) * len(red_shape),
        scratch_shapes=scratch,
        compiler_params=pltpu.CompilerParams(dimension_semantics=("arbitrary",), vmem_limit_bytes=VMEM_LIMIT),
    )(lhs, rhs, *operands)


def layer_bwd(x, dout, proj, ya, yb, yc, pswa, pmem, psink, gates, a_all, rc, rs1, rs2, ng, win_t, cw, wg, lam, qg, kg,
              xqg, km, vm, og, wout):
    seq = x.shape[0]
    tm = min(ROW_TILE, seq)
    nt = seq // tm
    nb = tm // BLOCK

    def body(x_ref, dout_ref, proj_ref, ya_ref, yb_ref, yc_ref, pswa_ref, pmem_ref, psink_ref, gates_ref, a_ref,
             c_ref, s1_ref, s2_ref,
             yah_ref, kvh_ref, ch_ref, s1h_ref, s2h_ref,
             ng_ref, win_ref, cw_ref, wg_ref, lam_ref, qg_ref, kg_ref, xqg_ref, km_ref, vm_ref, og_ref, wout_ref,
             gx_ref, dproj_ref, gwg_ref, dkm_ref, dvm_ref, gng_ref, gog_ref, gcb_ref, gbrg_ref, gbig_ref, glam_ref,
             gcw_ref, gqn_ref, gkn_ref, gxqn_ref, gsink_ref,
             hext_ref, aext_ref, an_scr, dh_scr, g_scr, dxc_ext, gcar_ref, dkcar_ref, dvcar_ref):
        i = pl.program_id(0)
        tile = nt - 1 - i
        first_tile = tile == 0

        @pl.when(i == 0)
        def _():
            for r in (gwg_ref, dkm_ref, dvm_ref, gng_ref, gog_ref, gcb_ref, gbrg_ref, gbig_ref, glam_ref, gcw_ref,
                      gqn_ref, gkn_ref, gxqn_ref, gsink_ref, gcar_ref, dkcar_ref, dvcar_ref):
                r[...] = jnp.zeros_like(r)
            dxc_ext[tm:tm + 8, :] = jnp.zeros((8, LRU_W), F32)
            aext_ref[tm:tm + 8, :] = jnp.zeros((8, LRU_W), F32)

        xv = x_ref[...]
        dov = dout_ref[...]
        dz = _mm_nt(dov, wout_ref[...])
        ogv = og_ref[...]

        def group_bwd(y, gate, g, dzg):
            r = lax.rsqrt(_row_mean(y * y) + EPS)
            n = y * r
            sg = _sigmoid(gate)
            dgate = dzg * (n * g) * (sg * (1.0 + gate * (1.0 - sg)))
            dng = dzg * (gate * sg)
            dn = dng * g
            return r * (dn - n * _row_mean(dn * n)), dgate, _col_sum(dng * n)

        dya, dga, goa = group_bwd(ya_ref[...], proj_ref[:, C_LRUG:C_LRUG + LRU_W], ogv[:, :512], dz[:, :512])
        dyb, dgb, gob = group_bwd(yb_ref[...], proj_ref[:, C_SWAG:C_SWAG + SWA_W], ogv[:, 512:768], dz[:, 512:768])
        dyc, dgc, goc = group_bwd(yc_ref[...], proj_ref[:, C_XG:C_XG + XATT_W], ogv[:, 768:], dz[:, 768:])
        gog_ref[...] += jnp.concatenate([goa, gob, goc], axis=1)
        dproj_ref[:, C_LRUG:C_LRUG + LRU_W] = dga.astype(dproj_ref.dtype)
        dproj_ref[:, C_SWAG:C_SWAG + SWA_W] = dgb.astype(dproj_ref.dtype)
        dproj_ref[:, C_XG:C_XG + XATT_W] = dgc.astype(dproj_ref.dtype)

        gm256 = _group_matrix(XATT_W)
        xq = proj_ref[:, C_XQ:C_XQ + XATT_W]
        rq = lax.rsqrt(_seg_mean(xq * xq, gm256) + EPS)
        qn = xq * rq
        qx = qn * xqg_ref[...]
        qxb = qx.astype(_MXU)
        dycb = dyc.astype(_MXU)
        dp_all = _mm_nt(dycb, vm_ref[...])
        dsm = []
        for j in range(4):
            pj = pmem_ref[:, MEM_LEN * j:MEM_LEN * (j + 1)].astype(F32)
            dp = dp_all[:, MEM_LEN * j:MEM_LEN * (j + 1)]
            dsm.append((pj * (dp - jnp.sum(pj * dp, axis=-1, keepdims=True))).astype(_MXU))
        ds_all = jnp.concatenate(dsm, axis=1)
        dvm_ref[...] += _mm_tn(dycb, pmem_ref[...])
        dkm_ref[...] += _mm_tn(qxb, ds_all)
        dqx = _mm(ds_all, km_ref[...])
        gxqn_ref[...] += _col_sum(dqx * qn)
        dqn = dqx * xqg_ref[...]
        dproj_ref[:, C_XQ:C_XQ + XATT_W] = (rq * (dqn - qn * _seg_mean(dqn * qn, gm256))).astype(dproj_ref.dtype)

        gm128 = _group_matrix(KV_W)
        cv, s1v, s2v = c_ref[...], s1_ref[...], s2_ref[...]

        def head_norm(t):
            r = lax.rsqrt(_seg_mean(t * t, gm128) + EPS)
            return t * r, r

        qn_, qr_ = zip(head_norm(proj_ref[:, C_SQ:C_SQ + 128]), head_norm(proj_ref[:, C_SQ + 128:C_SQ + 256]))
        qrope = [_rope(qn_[h] * qg_ref[...], cv, s1v, s2v).astype(_MXU) for h in range(2)]
        kn, krr = head_norm(proj_ref[:, C_SK:C_SK + KV_W])
        kr = _rope(kn * kg_ref[...], cv, s1v, s2v)
        khn, _ = head_norm(kvh_ref[:, 0:KV_W])
        khr = _rope(khn * kg_ref[...], ch_ref[...], s1h_ref[...], s2h_ref[...])
        ka = _place_kv(jnp.concatenate([khr, kr], axis=0), 0.125)
        va = _place_kv(jnp.concatenate([kvh_ref[:, KV_W:2 * KV_W], proj_ref[:, C_SV:C_SV + KV_W]], axis=0), 1.0)
        lane128 = lax.broadcasted_iota(jnp.int32, (1, 128), 1)
        gsink = jnp.zeros((1, 128), F32)
        dk_band, dv_band, dq_blk = [], [], []
        for b in range(nb):
            band = slice(BLOCK * b, BLOCK * b + 2 * BLOCK)
            blk = slice(BLOCK * b, BLOCK * (b + 1))
            dka, dva, dsb = [], [], []
            deltas = jnp.zeros((BLOCK, 128), F32)
            for j in range(4):
                qh = qrope[j // 2][blk]
                doh = dyb[blk, KV_W * (j // 2):KV_W * (j // 2 + 1)].astype(_MXU)
                pb = pswa_ref[blk, 2 * BLOCK * j:2 * BLOCK * (j + 1)]
                p = pb.astype(F32)
                dp = _mm_nt(doh, va[j][band])
                delta = jnp.sum(p * dp, axis=-1, keepdims=True)
                ds = (p * (dp - delta)).astype(_MXU)
                deltas = jnp.where(lane128 == j, delta, deltas)
                dva.append(_mm_tn(pb, doh))
                dka.append(_mm_tn(ds, qh))
                dsb.append(ds)
            gsink = gsink - _col_sum(psink_ref[blk, :] * deltas)
            dk_band.append(_unplace_kv(dka) * 0.125)
            dv_band.append(_unplace_kv(dva))
            dq_blk.append([_mm(jnp.concatenate(dsb[2 * h:2 * h + 2], axis=1),
                               jnp.concatenate([ka[2 * h][band], ka[2 * h + 1][band]], axis=0)) for h in range(2)])
        gsink_ref[...] += gsink
        dk_rows = [dk_band[b][BLOCK:] + (dk_band[b + 1][:BLOCK] if b + 1 < nb else dkcar_ref[...]) for b in range(nb)]
        dv_rows = [dv_band[b][BLOCK:] + (dv_band[b + 1][:BLOCK] if b + 1 < nb else dvcar_ref[...]) for b in range(nb)]
        dkcar_ref[...] = dk_band[0][:BLOCK]
        dvcar_ref[...] = dv_band[0][:BLOCK]
        dkg = _rope_bwd(jnp.concatenate(dk_rows, axis=0), cv, s1v, s2v)
        gkn = _col_sum(dkg * kn)
        dkn = dkg * kg_ref[...]
        dproj_ref[:, C_SK:C_SK + KV_W] = (krr * (dkn - kn * _seg_mean(dkn * kn, gm128))).astype(dproj_ref.dtype)
        dproj_ref[:, C_SV:C_SV + KV_W] = jnp.concatenate(dv_rows, axis=0).astype(dproj_ref.dtype)
        gqn = jnp.zeros((1, 128), F32)
        for h in range(2):
            dqg = _rope_bwd(jnp.concatenate([dq_blk[b][h] for b in range(nb)], axis=0), cv, s1v, s2v)
            gqn = gqn + _col_sum(dqg * qn_[h])
            dqn_ = dqg * qg_ref[...]
            dproj_ref[:, C_SQ + 128 * h:C_SQ + 128 * (h + 1)] = (
                qr_[h] * (dqn_ - qn_[h] * _seg_mean(dqn_ * qn_[h], gm128))).astype(dproj_ref.dtype)
        gqn_ref[...] += gqn
        gkn_ref[...] += gkn

        u = proj_ref[:, C_LRUX:C_LRUX + LRU_W]
        xc, rg, ig, sq = (gates_ref[:, LRU_W * k:LRU_W * (k + 1)].astype(F32) for k in range(4))
        a = a_ref[...]
        sp = _softplus(-lam_ref[...])
        hext_ref[0:8, :] = jnp.where(first_tile, 0.0, yah_ref[...])
        hext_ref[8:8 + tm, :] = ya_ref[...]
        hprev = hext_ref[pl.ds(7, tm), :]
        aext_ref[0:tm, :] = a
        an_scr[...] = aext_ref[pl.ds(1, tm), :]
        dh_scr[...] = dya
        dh_scr[tm - 1:tm, :] = dh_scr[tm - 1:tm, :] + gcar_ref[0:1, :]
        row8 = lax.broadcasted_iota(jnp.int32, (8, LRU_W), 0)

        def scan_step(gi, carry):
            r0 = pl.multiple_of((tm // 8 - 1 - gi) * 8, 8)
            av = an_scr[pl.ds(r0, 8), :]
            bv = dh_scr[pl.ds(r0, 8), :]
            for d in (1, 2, 4):
                a_sh = jnp.where(row8 < 8 - d, pltpu.roll(av, 8 - d, 0), 1.0)
                b_sh = jnp.where(row8 < 8 - d, pltpu.roll(bv, 8 - d, 0), 0.0)
                bv = bv + av * b_sh
                av = av * a_sh
            gv = bv + av * carry
            g_scr[pl.ds(r0, 8), :] = gv
            return gv[0:1, :]

        g0 = lax.fori_loop(0, tm // 8, scan_step, jnp.zeros((1, LRU_W), F32), unroll=True)
        gcar_ref[0:1, :] = a[0:1, :] * g0
        gv = g_scr[...]
        da = gv * hprev
        dig = gv * sq * xc
        dxc = gv * sq * ig
        dla = da * a - gv * (ig * xc) * ((a * a) / sq)
        drg = dla * ((-LRU_C) * sp)
        glam_ref[...] += _col_sum(dla * rg)
        dpr = drg * rg * (1.0 - rg)
        dpi = dig * ig * (1.0 - ig)
        gbrg_ref[...] += _col_sum(dpr)
        gbig_ref[...] += _col_sum(dpi)
        dpre0 = jnp.concatenate([dpr[:, :256], dpi[:, :256]], axis=1).astype(_MXU)
        dpre1 = jnp.concatenate([dpr[:, 256:], dpi[:, 256:]], axis=1).astype(_MXU)
        gwg_ref[0] += _mm_tn(xc[:, :256], dpre0)
        gwg_ref[1] += _mm_tn(xc[:, 256:], dpre1)
        dxc = dxc + jnp.concatenate([_mm_nt(dpre0, wg_ref[0]), _mm_nt(dpre1, wg_ref[1])], axis=1)
        gcb_ref[...] += _col_sum(dxc)
        dxc_ext[0:tm, :] = dxc
        du = jnp.zeros((tm, LRU_W), F32)
        for k in range(CONV_K):
            later = dxc_ext[pl.ds(3 - k, tm), :]
            gcw_ref[k:k + 1, :] += _col_sum(later * u)
            du = du + cw_ref[k:k + 1, :] * later
        dxc_ext[tm:tm + 8, :] = dxc[0:8, :]
        dproj_ref[:, C_LRUX:C_LRUX + LRU_W] = du.astype(dproj_ref.dtype)

        dxn = _mm(dproj_ref[...], win_ref[...])
        rx = lax.rsqrt(_row_mean(xv * xv) + EPS)
        xh = xv * rx
        gng_ref[...] += _col_sum(dxn * xh)
        dxh = dxn * ng_ref[...]
        gx_ref[...] = dov.astype(F32) + rx * (dxh - xh * _row_mean(dxh * xh))

        @pl.when(i == nt - 1)
        def _():
            glam_ref[...] = glam_ref[...] * (LRU_C * _sigmoid(-lam_ref[...]))

    def rows(ncol, arr_cols_block=0):
        return pl.BlockSpec((tm, ncol), lambda i: (nt - 1 - i, arr_cols_block))

    def halo(nrow, ncol, colblk=0):
        per = tm // nrow
        return pl.BlockSpec((nrow, ncol), lambda i: (jnp.maximum((nt - 1 - i) * per - 1, 0), colblk))

    in_specs = [rows(D_MODEL), rows(D_MODEL), rows(D_IN), rows(LRU_W), rows(SWA_W), rows(XATT_W),
                rows(4 * 2 * BLOCK), rows(4 * MEM_LEN), rows(128), rows(4 * LRU_W), rows(LRU_W),
                rows(128), rows(128), rows(128),
                halo(8, LRU_W), halo(BLOCK, 2 * KV_W, C_SK // (2 * KV_W)),
                halo(BLOCK, 128), halo(BLOCK, 128), halo(BLOCK, 128),
                _const_spec((1, D_MODEL)), _const_spec((D_IN, D_MODEL), True), _const_spec((CONV_K, LRU_W)),
                _const_spec((2, 256, 512), True), _const_spec((1, LRU_W)), _const_spec((1, 128)), _const_spec((1, 128)),
                _const_spec((1, XATT_W)),
                _const_spec((4 * MEM_LEN, XATT_W), True), _const_spec((4 * MEM_LEN, XATT_W), True),
                _const_spec((1, D_MODEL)), _const_spec((D_MODEL, D_MODEL), True)]
    small = [(2, 256, 512), (XATT_W, 4 * MEM_LEN), (XATT_W, 4 * MEM_LEN), (1, D_MODEL), (1, D_MODEL), (1, LRU_W), (1, LRU_W),
             (1, LRU_W), (1, LRU_W), (CONV_K, LRU_W), (1, 128), (1, 128), (1, XATT_W), (1, 128)]
    out_shape = (jax.ShapeDtypeStruct((seq, D_MODEL), F32), jax.ShapeDtypeStruct((seq, D_IN), _MXU)) + tuple(
        jax.ShapeDtypeStruct(s, F32) for s in small)
    out_specs = (rows(D_MODEL), rows(D_IN)) + tuple(_const_spec(s) for s in small)
    scratch = [pltpu.VMEM((tm + 8, LRU_W), F32), pltpu.VMEM((tm + 8, LRU_W), F32),
               pltpu.VMEM((tm, LRU_W), F32), pltpu.VMEM((tm, LRU_W), F32), pltpu.VMEM((tm, LRU_W), F32),
               pltpu.VMEM((tm + 8, LRU_W), F32),
               pltpu.VMEM((8, LRU_W), F32), pltpu.VMEM((BLOCK, KV_W), F32), pltpu.VMEM((BLOCK, KV_W), F32)]
    return pl.pallas_call(
        body, name="layer_bwd", grid=(nt,), out_shape=out_shape, in_specs=in_specs, out_specs=out_specs,
        scratch_shapes=scratch,
        compiler_params=pltpu.CompilerParams(dimension_semantics=("arbitrary",), vmem_limit_bytes=VMEM_LIMIT),
    )(x, dout, proj, ya, yb, yc, pswa, pmem, psink, gates, a_all, rc, rs1, rs2, ya, proj, rc, rs1, rs2,
      ng, win_t, cw, wg, lam, qg, kg, xqg, km, vm, og, wout)


def _reduce_protocol(big, sm, outs, osm, r1, r1s, wire, r2, r2s, wire2, ps, own, send, recv, lsem):
    nbig = len(big)
    x, y, c = lax.axis_index("x"), lax.axis_index("y"), lax.axis_index("c")
    sibling = (x, y, 1 - c)
    near, far, diag = _partners(x, y, c)
    me, near_id, far_id, diag_id = _chip_of(x, y), _chip_of(*near), _chip_of(*far), _chip_of(*diag)

    def copy(k, src, dst, to):
        return pltpu.make_async_remote_copy(src_ref=src, dst_ref=dst, send_sem=send.at[k], recv_sem=recv.at[k],
                                            device_id=to, device_id_type=MESH)

    def sent(stage, a):
        if a == nbig:
            src, dst, to = ((sm.at[1 - c], r1s, sibling), (r1s, r2s.at[0], (*near, c)), (ps, r2s.at[1], (*far, c)),
                            (osm.at[c], osm.at[c], sibling))[stage]
            return [copy(5 * nbig + stage, src, dst, to)]
        if stage == 0:
            return [copy(5 * a, big[a].at[:, 1 - c], r1[a], sibling)]
        if stage == 1:
            return [copy(5 * a + 1, wire[a].at[near_id], r2[a].at[0], (*near, c)),
                    copy(5 * a + 2, wire[a].at[diag_id], r2[a].at[1], (*near, c))]
        if stage == 2:
            return [copy(5 * a + 3, wire2[a], r2[a].at[2], (*far, c))]
        return [copy(5 * a + 4, outs[a].at[c], outs[a].at[c], sibling)]

    arrays = range(nbig + (sm is not None))

    def start(stage, a):
        for cp in sent(stage, a):
            cp.start()

    def arrived(k, ref):
        copy(k, ref, ref, sibling).wait_recv()

    def loads():
        return [pltpu.make_async_copy(big[a].at[:, c], own[a], lsem.at[a]) for a in range(nbig)]

    def stage0():
        for a in arrays:
            start(0, a)
        for cp in loads():
            cp.start()

    def stage1():
        for a in range(nbig):
            loads()[a].wait()
            arrived(5 * a, r1[a])
            for k in range(N_CHIPS):
                r1[a][k] = own[a][k] + r1[a][k]
                wire[a][k] = r1[a][k].astype(wire[a].dtype)
            start(1, a)
        if sm is not None:
            arrived(5 * nbig, r1s)
            r1s[...] = sm[c] + r1s[...]
            start(1, nbig)

    def stage2():
        for a in range(nbig):
            arrived(5 * a + 1, r2[a].at[0])
            arrived(5 * a + 2, r2[a].at[1])
            r1[a][me] = r1[a][me] + r2[a][0].astype(F32)
            wire2[a][...] = (r1[a][far_id] + r2[a][1].astype(F32)).astype(wire2[a].dtype)
            start(2, a)
        if sm is not None:
            arrived(5 * nbig + 1, r2s.at[0])
            ps[...] = r1s[...] + r2s[0]
            start(2, nbig)

    def stage3():
        for a in range(nbig):
            arrived(5 * a + 3, r2[a].at[2])
            outs[a][c] = r1[a][me] + r2[a][2].astype(F32)
            start(3, a)
        if sm is not None:
            arrived(5 * nbig + 2, r2s.at[1])
            osm[c] = ps[...] + r2s[1]
            start(3, nbig)

    def stage4():
        for a in range(nbig):
            arrived(5 * a + 4, outs[a].at[1 - c])
        if sm is not None:
            arrived(5 * nbig + 3, osm.at[1 - c])
        for stage in range(4):
            for a in arrays:
                for cp in sent(stage, a):
                    cp.wait_send()

    return [stage0, stage1, stage2, stage3, stage4]


def _reduce_buffers(bigs, g_small):
    half = [b.shape[2:] for b in bigs]
    sm_half = None if g_small is None else g_small.shape[1:]
    out_shape = [jax.ShapeDtypeStruct((2,) + h, F32) for h in half]
    small = lambda lead: [] if g_small is None else [pltpu.VMEM(lead + sm_half, F32)]
    if g_small is not None:
        out_shape.append(jax.ShapeDtypeStruct(g_small.shape, F32))
    n_sem = 5 * len(bigs) + 4
    scratch = ([pltpu.VMEM((N_CHIPS,) + h, F32) for h in half] + small(())
               + [pltpu.VMEM((N_CHIPS,) + h, _WIRE) for h in half]
               + [pltpu.VMEM((3,) + h, _WIRE) for h in half] + small((2,))
               + [pltpu.VMEM(h, _WIRE) for h in half] + small(())
               + [pltpu.VMEM((N_CHIPS,) + h, F32) for h in half]
               + [pltpu.SemaphoreType.DMA((n_sem,)), pltpu.SemaphoreType.DMA((n_sem,)),
                  pltpu.SemaphoreType.DMA((len(bigs),))])
    return out_shape, scratch


def _split_reduce_refs(refs, nbig, has_small):
    it = iter(refs)
    take = lambda n: [next(it) for _ in range(n)]
    one = lambda: next(it) if has_small else None
    big, sm = take(nbig), one()
    outs, osm = take(nbig), one()
    r1, r1s, wire, r2, r2s, wire2, ps, own = take(nbig), one(), take(nbig), take(nbig), one(), take(nbig), one(), take(nbig)
    send, recv, lsem = take(3)
    return big, sm, outs, osm, r1, r1s, wire, r2, r2s, wire2, ps, own, send, recv, lsem


def _hosted_reduce_shapes(bigs, g_small):
    red_shape, scratch = _reduce_buffers(bigs, g_small)
    nres = len(red_shape)
    return red_shape, [pltpu.VMEM(r.shape, r.dtype) for r in red_shape] + scratch + [pltpu.SemaphoreType.DMA((nres,))]


def _hosted_reduce(step, stage_at, operands, results, scratch, has_small):
    nres = len(results)
    sums, rest, fsem = scratch[:nres], scratch[nres:-1], scratch[-1]
    refs = tuple(operands) + tuple(sums) + tuple(rest)
    for at, stage in zip(stage_at, _reduce_protocol(*_split_reduce_refs(refs, nres - has_small, has_small))):
        pl.when(step == at)(stage)

    @pl.when(step == stage_at[-1])
    def _():
        out = [pltpu.make_async_copy(sums[k], results[k], fsem.at[k]) for k in range(nres)]
        for cp in out:
            cp.start()
        for cp in out:
            cp.wait()


def reduce_grads(big, name, parts):
    chips, halves, rows_, cols = big.shape
    sub = jax.ShapeDtypeStruct((chips, halves, rows_ // parts, cols), big.dtype)

    def body(b_ref, o_ref, *scratch):
        refs = [b_ref.at[:, :, s] for s in range(parts)] + [o_ref.at[:, s] for s in range(parts)] + list(scratch)
        for stage in _reduce_protocol(*_split_reduce_refs(refs, parts, False)):
            stage()

    _, scratch = _reduce_buffers([sub] * parts, None)
    return pl.pallas_call(
        body, name=name, out_shape=jax.ShapeDtypeStruct((halves, parts, rows_ // parts, cols), F32),
        in_specs=[pl.BlockSpec(memory_space=pl.ANY)], out_specs=pl.BlockSpec(memory_space=pltpu.VMEM),
        scratch_shapes=scratch, compiler_params=pltpu.CompilerParams(vmem_limit_bytes=VMEM_LIMIT),
    )(big.reshape(chips, halves, parts, rows_ // parts, cols))


def adamw(w, g, m, v, name):
    rows_, cols = w.shape
    tr = max(t for t in range(8, rows_ + 1, 8) if rows_ % t == 0 and t * cols * 4 <= ADAM_BLOCK_BYTES)

    def body(w_ref, g_ref, m_ref, v_ref, d_ref, nm_ref, nv_ref):
        d_ref[...], nm_ref[...], nv_ref[...] = _adam_update(w_ref[...], g_ref[...], m_ref[...], v_ref[...])

    spec = pl.BlockSpec((tr, cols), lambda i: (i, 0))
    shp = jax.ShapeDtypeStruct(w.shape, F32)
    return pl.pallas_call(
        body, name=name, grid=(rows_ // tr,), out_shape=(shp, shp, shp), in_specs=[spec] * 4, out_specs=(spec,) * 3,
        compiler_params=pltpu.CompilerParams(dimension_semantics=("arbitrary",)),
    )(w, g, m, v)


def _adam_update(w, g, m, v):
    nm = ADAM_B1 * m + (1.0 - ADAM_B1) * g
    nv = ADAM_B2 * v + (1.0 - ADAM_B2) * (g * g)
    m_hat = nm / (1.0 - ADAM_B1 ** ADAM_STEP)
    v_hat = nv / (1.0 - ADAM_B2 ** ADAM_STEP)
    return (-ADAM_LR) * (m_hat / (jnp.sqrt(v_hat) + ADAM_EPS) + ADAM_WD * w), nm, nv


def adamw_vectors(g_pack, g_mats, ws, ms, vs):
    nvec, nmat = len(SMALL_VECTORS), len(g_mats)
    n = nvec + nmat

    def body(*refs):
        pk = refs[0]
        gm_refs = refs[1:1 + nmat]
        w_refs, m_refs, v_refs = (refs[1 + nmat + k * n:1 + nmat + (k + 1) * n] for k in range(3))
        outs = refs[1 + nmat + 3 * n:]
        g_out, d_out, nm_out, nv_out = outs[:nvec], outs[nvec:nvec + n], outs[nvec + n:nvec + 2 * n], outs[nvec + 2 * n:]
        chip = 2 * lax.axis_index("x") + lax.axis_index("y")
        for k, (name, row, width) in enumerate(SMALL_VECTORS):
            if name == "conv_w":
                g = jnp.concatenate([pk[pl.ds(row + 4 * t + chip, 1), :] for t in range(CONV_K)], axis=0)[None]
            elif width >= 128:
                g = jnp.concatenate([pk[row + r:row + r + 1, :] for r in range(width // 128)], axis=1)
            else:
                g = pk[row:row + 1, 0:width]
            g_out[k][...] = g
            d_out[k][...], nm_out[k][...], nv_out[k][...] = _adam_update(w_refs[k][...], g, m_refs[k][...], v_refs[k][...])
        for k in range(nvec, n):
            d_out[k][...], nm_out[k][...], nv_out[k][...] = _adam_update(
                w_refs[k][...], gm_refs[k - nvec][...], m_refs[k][...], v_refs[k][...])

    vm = pl.BlockSpec(memory_space=pltpu.VMEM)
    like = [jax.ShapeDtypeStruct(w.shape, F32) for w in ws]
    out_shape = like[:nvec] + like * 3
    return pl.pallas_call(
        body, name="adamw_vectors", out_shape=tuple(out_shape), in_specs=[vm] * (1 + nmat + 3 * n),
        out_specs=(vm,) * len(out_shape),
    )(g_pack, *g_mats, *ws, *ms, *vs)


SMALL_VECTORS = (("norm_g", 0, 1024), ("mem_norm_g", 8, 1024), ("conv_w", 16, 512), ("conv_b", 32, 512),
                 ("b_rg", 292, 512), ("b_ig", 552, 512), ("lru_lambda", 556, 512), ("q_norm_g", 560, 64),
                 ("k_norm_g", 561, 64), ("sinks", 562, 4), ("xq_norm_g", 563, 64), ("xk_norm_g", 564, 64),
                 ("out_norm_g", 565, 1024))
SMALL_MATRICES = (("w_rg", 36), ("w_ig", 296))
LOSS_ROW = 573
SMALL_ROWS = 576


def _pack(parts, rows_):
    flat = jnp.concatenate([p.reshape(-1) for p in parts])
    return jnp.pad(flat, (0, rows_ * 128 - flat.shape[0])).reshape(rows_, 128)


def _pad_to(v, n):
    v = v.reshape(-1)
    return jnp.pad(v, (0, n - v.shape[0]))


def _block_diag_gates(w_rg, w_ig):
    eye = jnp.eye(4, dtype=w_rg.dtype)

    def bd(w4):
        return (w4[:, :, None, :] * eye[:, None, :, None]).reshape(256, 256)

    return jnp.stack([jnp.concatenate([bd(w_rg[4 * h:4 * h + 4]), bd(w_ig[4 * h:4 * h + 4])], axis=1) for h in (0, 1)])


def _diag_blocks(g):
    g6 = g.reshape(2, 4, HEAD, 2, 4, HEAD)
    d = (g6 * jnp.eye(4, dtype=g.dtype)[None, :, None, None, :, None]).sum(axis=4)
    return d[:, :, :, 0].reshape(8, HEAD, HEAD), d[:, :, :, 1].reshape(8, HEAD, HEAD)


def _rope_tables(seq):
    pos = np.arange(seq, dtype=np.float32)
    inv_freq = (np.float32(ROPE_THETA) ** (-(np.arange(0, ROPE_DIM, 2, dtype=np.float32) / np.float32(ROPE_DIM)))
                ).astype(np.float32)
    ang = (pos[:, None] * inv_freq[None, :]).astype(np.float32)
    cos, sin = np.cos(ang).astype(np.float32), np.sin(ang).astype(np.float32)
    z = lambda n: np.zeros((seq, n), np.float32)
    c64 = np.concatenate([cos, cos, np.ones((seq, HEAD - ROPE_DIM), np.float32)], axis=1)
    s1_64 = np.concatenate([-sin, z(HEAD - 8)], axis=1)
    s2_64 = np.concatenate([z(8), sin, z(HEAD - ROPE_DIM)], axis=1)
    return tuple(jnp.asarray(np.concatenate([t, t], axis=1)) for t in (c64, s1_64, s2_64))


def kernel(x, mem, norm_g, mem_norm_g, w_in, conv_w, conv_b, w_rg, b_rg, w_ig, b_ig, lru_lambda, q_norm_g, k_norm_g, sinks, w_mem_kv, xq_norm_g, xk_norm_g, out_norm_g, w_out, loss_target, m_norm_g, m_mem_norm_g, m_w_in, m_conv_w, m_conv_b, m_w_rg, m_b_rg, m_w_ig, m_b_ig, m_lru_lambda, m_q_norm_g, m_k_norm_g, m_sinks, m_w_mem_kv, m_xq_norm_g, m_xk_norm_g, m_out_norm_g, m_w_out, v_norm_g, v_mem_norm_g, v_w_in, v_conv_w, v_conv_b, v_w_rg, v_b_rg, v_w_ig, v_b_ig, v_lru_lambda, v_q_norm_g, v_k_norm_g, v_sinks, v_w_mem_kv, v_xq_norm_g, v_xk_norm_g, v_out_norm_g, v_w_out):
    seq = x.shape[1]
    xs, tgt, mems = x[0], loss_target[0], mem[0]

    win_t_sh = w_in[0].T.astype(_MXU)
    cw_sh = jnp.pad(conv_w[0], ((0, 4), (0, 0)))
    win_t, wout, wkv, cw_all = gather_weights(win_t_sh, w_out[0].astype(_MXU), w_mem_kv[0].astype(_MXU), cw_sh)
    cw = cw_all.reshape(N_CHIPS, 8, 128)[:, :CONV_K].transpose(1, 0, 2).reshape(CONV_K, LRU_W)

    rc, rs1, rs2 = _rope_tables(seq)
    wg = _block_diag_gates(w_rg[0], w_ig[0]).astype(_MXU)
    qg = jnp.tile(q_norm_g, (1, 2))
    kg = jnp.tile(k_norm_g, (1, 2))
    xqg = jnp.tile(xq_norm_g, (1, 4))
    xkg = jnp.tile(xk_norm_g, (1, 4))

    km, vm = mem_fwd(mems, mem_norm_g, wkv, xkg)
    proj, ya, yb, yc, ycat, xn, dout, pswa, pmem, psink, gates, a_all, loss8 = layer_fwd(
        xs, tgt, rc, rs1, rs2, norm_g, win_t, cw, conv_b, wg, b_rg, b_ig, lru_lambda, qg, kg, xqg, sinks, km, vm,
        out_norm_g, wout)
    (g_wout,) = wgrad_reduce(ycat, dout, [], None, (), "wgrad_out")
    (gx, dproj, g_wg, dkm, dvm, g_ng, g_og, g_cb, g_brg, g_big, g_lam, g_cw, g_qn, g_kn, g_xqn, g_sink) = layer_bwd(
        xs, dout, proj, ya, yb, yc, pswa, pmem, psink, gates, a_all, rc, rs1, rs2, norm_g, win_t, cw, wg, lru_lambda, qg,
        kg, xqg, km, vm, out_norm_g, wout)
    g_wkv, g_mng, g_xkn = mem_bwd(mems, mem_norm_g, wkv, xkg, dkm, dvm)

    g_wrg, g_wig = _diag_blocks(g_wg)
    fold = lambda v, n: v.reshape(n, HEAD).sum(axis=0)
    small_g = _pack([g_ng, g_mng, g_cw, g_cb, g_wrg, g_brg, g_wig, g_big, g_lam, _pad_to(fold(g_qn, 2), 128),
                     _pad_to(fold(g_kn, 2), 128), g_sink, _pad_to(fold(g_xqn, 4), 128), _pad_to(fold(g_xkn, 4), 128),
                     g_og, loss8[0:1]], SMALL_ROWS)
    early = [g_wout.reshape(N_CHIPS, 2, D_MODEL // 8, D_MODEL), g_wkv.reshape(N_CHIPS, 2, D_MODEL // 8, 2 * XATT_W)]
    g_win_t, r_out, r_kv, r_small = wgrad_reduce(dproj, xn, early, small_g.reshape(2, SMALL_ROWS // 2, 128),
                                                 (0, 1, 6, 8, 8), "wgrad_in")
    r_in = reduce_grads(g_win_t.reshape(N_CHIPS, 2, D_IN // 8, D_MODEL), "reduce_w_in", 3)

    r_small = r_small.reshape(SMALL_ROWS, 128)
    loss = r_small[LOSS_ROW, 0]
    grads = {"w_in": r_in.reshape(D_IN // 4, D_MODEL).T[None], "w_mem_kv": r_kv.reshape(D_MODEL // 4, 2 * XATT_W)[None],
             "w_out": r_out.reshape(D_MODEL // 4, D_MODEL)[None]}
    for name, row in SMALL_MATRICES:
        grads[name] = r_small[row:row + 256].reshape(1, LRU_BLOCKS, HEAD, HEAD)
    weights = dict(norm_g=norm_g, mem_norm_g=mem_norm_g, w_in=w_in, conv_w=conv_w, conv_b=conv_b, w_rg=w_rg, b_rg=b_rg,
                   w_ig=w_ig, b_ig=b_ig, lru_lambda=lru_lambda, q_norm_g=q_norm_g, k_norm_g=k_norm_g, sinks=sinks,
                   w_mem_kv=w_mem_kv, xq_norm_g=xq_norm_g, xk_norm_g=xk_norm_g, out_norm_g=out_norm_g, w_out=w_out)
    ms = dict(norm_g=m_norm_g, mem_norm_g=m_mem_norm_g, w_in=m_w_in, conv_w=m_conv_w, conv_b=m_conv_b, w_rg=m_w_rg,
              b_rg=m_b_rg, w_ig=m_w_ig, b_ig=m_b_ig, lru_lambda=m_lru_lambda, q_norm_g=m_q_norm_g, k_norm_g=m_k_norm_g,
              sinks=m_sinks, w_mem_kv=m_w_mem_kv, xq_norm_g=m_xq_norm_g, xk_norm_g=m_xk_norm_g,
              out_norm_g=m_out_norm_g, w_out=m_w_out)
    vs = dict(norm_g=v_norm_g, mem_norm_g=v_mem_norm_g, w_in=v_w_in, conv_w=v_conv_w, conv_b=v_conv_b, w_rg=v_w_rg,
              b_rg=v_b_rg, w_ig=v_w_ig, b_ig=v_b_ig, lru_lambda=v_lru_lambda, q_norm_g=v_q_norm_g, k_norm_g=v_k_norm_g,
              sinks=v_sinks, w_mem_kv=v_w_mem_kv, xq_norm_g=v_xq_norm_g, xk_norm_g=v_xk_norm_g,
              out_norm_g=v_out_norm_g, w_out=v_w_out)

    delta, new_m, new_v = {}, {}, {}
    d2, m2, v2 = adamw(w_in[0].T, r_in.reshape(D_IN // 4, D_MODEL), m_w_in[0].T, v_w_in[0].T, "adamw_w_in")
    delta["w_in"], new_m["w_in"], new_v["w_in"] = d2.T[None], m2.T[None], v2.T[None]
    for name in ("w_mem_kv", "w_out"):
        shp = weights[name].shape
        d2, m2, v2 = adamw(weights[name][0], grads[name][0], ms[name][0], vs[name][0], "adamw_" + name)
        delta[name], new_m[name], new_v[name] = d2.reshape(shp), m2.reshape(shp), v2.reshape(shp)
    vec_names = [n for n, _, _ in SMALL_VECTORS]
    small_names = vec_names + [n for n, _ in SMALL_MATRICES]
    res = adamw_vectors(r_small, [grads[n] for n, _ in SMALL_MATRICES], [weights[n] for n in small_names],
                        [ms[n] for n in small_names], [vs[n] for n in small_names])
    nvec, nall = len(vec_names), len(small_names)
    grads.update(zip(vec_names, res[:nvec]))
    delta.update(zip(small_names, res[nvec:nvec + nall]))
    new_m.update(zip(small_names, res[nvec + nall:nvec + 2 * nall]))
    new_v.update(zip(small_names, res[nvec + 2 * nall:]))

    order = ("norm_g", "mem_norm_g", "w_in", "conv_w", "conv_b", "w_rg", "b_rg", "w_ig", "b_ig", "lru_lambda",
             "q_norm_g", "k_norm_g", "sinks", "w_mem_kv", "xq_norm_g", "xk_norm_g", "out_norm_g", "w_out")
    return (loss, gx[None], *[grads[n] for n in order], *[delta[n] for n in order], *[new_m[n] for n in order],
            *[new_v[n] for n in order])
```

```python
import jax
import jax.numpy as jnp
import numpy as np
from jax import lax
from jax.experimental import pallas as pl
from jax.experimental.pallas import tpu as pltpu

F32 = jnp.float32
_MXU = jnp.bfloat16
_WIRE = jnp.bfloat16

D_MODEL = 1024
MEM_LEN = 256
HEAD = 64
LRU_W = 512
LRU_BLOCKS = 8
CONV_K = 4
LRU_C = 8.0
SWA_W = 256
KV_W = 128
XATT_W = 256
BLOCK = 128
D_IN = 2304
ROPE_THETA = 500000.0
ROPE_DIM = 16
EPS = 1e-6
NEG_INF = -1e30
C_LRUX, C_LRUG, C_SQ, C_SK, C_SV, C_SWAG, C_XQ, C_XG = 0, 512, 1024, 1280, 1408, 1536, 1792, 2048

ADAM_LR, ADAM_B1, ADAM_B2, ADAM_EPS, ADAM_WD, ADAM_STEP = 0.001, 0.9, 0.999, 1e-08, 0.01, 10

N_CHIPS = 4
ROW_TILE = 256
VMEM_LIMIT = 56 * 1024 * 1024
ADAM_BLOCK_BYTES = 1280 * 1024
MESH = pl.DeviceIdType.MESH


def _mm(a, b):
    return jnp.dot(a.astype(_MXU), b.astype(_MXU), preferred_element_type=F32)


def _mm_nt(a, b):
    return lax.dot_general(a.astype(_MXU), b.astype(_MXU), (((1,), (1,)), ((), ())), preferred_element_type=F32)


def _mm_tn(a, b):
    return lax.dot_general(a.astype(_MXU), b.astype(_MXU), (((0,), (0,)), ((), ())), preferred_element_type=F32)


def _group_matrix(width):
    r = lax.shift_right_logical(lax.broadcasted_iota(jnp.int32, (width, width), 0), 6)
    c = lax.shift_right_logical(lax.broadcasted_iota(jnp.int32, (width, width), 1), 6)
    return (r == c).astype(_MXU)


def _seg_mean(x, gm):
    return jnp.dot(x.astype(_MXU), gm, preferred_element_type=F32) * (1.0 / HEAD)


def _row_mean(x):
    return jnp.mean(x, axis=-1, keepdims=True)


def _col_sum(x):
    return jnp.sum(x, axis=0, keepdims=True)


def _sigmoid(x):
    return jax.nn.sigmoid(x)


def _softplus(z):
    e = jnp.exp(-jnp.abs(z))
    u = 1.0 + e
    log1p_e = jnp.where(u == 1.0, e, jnp.log(u) * (e / (u - 1.0)))
    return jnp.maximum(z, 0.0) + log1p_e


def _rope(t, c, s1, s2):
    return t * c + pltpu.roll(t, 120, 1) * s1 + pltpu.roll(t, 8, 1) * s2


def _rope_bwd(d, c, s1, s2):
    return d * c + pltpu.roll(d * s1, 8, 1) + pltpu.roll(d * s2, 120, 1)


def _lane_mask(width, lo, hi):
    lane = lax.broadcasted_iota(jnp.int32, (1, width), 1)
    return ((lane >= lo) & (lane < hi)).astype(F32)


def _swa_mask(first_block):
    qi = lax.broadcasted_iota(jnp.int32, (BLOCK, 2 * BLOCK), 0)
    kj = lax.broadcasted_iota(jnp.int32, (BLOCK, 2 * BLOCK), 1)
    rel = qi + BLOCK - kj
    ok = (rel >= 0) & (rel < BLOCK)
    return ok & (jnp.logical_not(first_block) | (kj >= BLOCK))


def _place_kv(t, scale):
    lo = t * (_lane_mask(KV_W, 0, HEAD) * scale)
    hi = t * (_lane_mask(KV_W, HEAD, KV_W) * scale)
    return [a.astype(_MXU) for a in (lo, pltpu.roll(lo, HEAD, 1), pltpu.roll(hi, HEAD, 1), hi)]


def _unplace_kv(d):
    return (_lane_mask(KV_W, 0, HEAD) * (d[0] + pltpu.roll(d[1], HEAD, 1))
            + _lane_mask(KV_W, HEAD, KV_W) * (d[3] + pltpu.roll(d[2], HEAD, 1)))


def _swa_probs(qh, ka, mask, sink):
    s = _mm_nt(qh, ka)
    s = jnp.where(mask, s, NEG_INF)
    m = jnp.maximum(jnp.max(s, axis=-1, keepdims=True), sink)
    p = jnp.exp(s - m)
    esink = jnp.exp(sink - m)
    inv = 1.0 / (jnp.sum(p, axis=-1, keepdims=True) + esink)
    return p * inv, esink * inv


def _mem_probs(s_all):
    out = []
    for j in range(4):
        s = s_all[:, MEM_LEN * j:MEM_LEN * (j + 1)]
        p = jnp.exp(s - jnp.max(s, axis=-1, keepdims=True))
        out.append(p * (1.0 / jnp.sum(p, axis=-1, keepdims=True)))
    return out


def _head_rows(t, scale):
    return jnp.concatenate([t * (_lane_mask(XATT_W, HEAD * j, HEAD * (j + 1)) * scale) for j in range(4)], axis=0)


def _lru_gates(xc, wg_ref, brg, big, lam):
    p0 = _mm(xc[:, :256], wg_ref[0])
    p1 = _mm(xc[:, 256:], wg_ref[1])
    rg = _sigmoid(jnp.concatenate([p0[:, :256], p1[:, :256]], axis=1) + brg)
    ig = _sigmoid(jnp.concatenate([p0[:, 256:], p1[:, 256:]], axis=1) + big)
    sp = _softplus(-lam)
    la = (-LRU_C) * rg * sp
    a = jnp.exp(la)
    th = jnp.tanh(la)
    one_minus_a2 = (-2.0 * th) / (1.0 - th)
    return rg, ig, sp, a, jnp.sqrt(one_minus_a2)


def _const_spec(shape, single=False):
    zeros = (0,) * len(shape)
    if single:
        return pl.BlockSpec(shape, lambda i: zeros, pipeline_mode=pl.Buffered(1))
    return pl.BlockSpec(shape, lambda i: zeros)


def _chip_of(x, y):
    return 2 * x + y


def _partners(x, y, c):
    north = c == 1
    near = (jnp.where(north, 1 - x, x), jnp.where(north, y, 1 - y))
    far = (jnp.where(north, x, 1 - x), jnp.where(north, 1 - y, y))
    return near, far, (1 - x, 1 - y)


def gather_weights(win_t, wout, wkv, convw):
    arrs = (win_t, wout, wkv)
    n = len(arrs)
    pieces = [(a, 0, arr.shape[0] // 2) for a, arr in enumerate(arrs)]
    npc = len(pieces)

    def body(a0, a1, a2, cw, o0, o1, o2, ocw, send, recv, lsem):
        ins, outs = (a0, a1, a2), (o0, o1, o2)
        x, y, c = lax.axis_index("x"), lax.axis_index("y"), lax.axis_index("c")
        sibling = (x, y, 1 - c)
        near, far, diag = _partners(x, y, c)
        chips = [near, far, diag]
        me = _chip_of(x, y)

        def landed(p, chip, half):
            a, off, rows_ = pieces[p]
            r = ins[a].shape[0]
            return outs[a].at[pl.ds(pl.multiple_of(chip * r + half * (r // 2) + off, 16), rows_)]

        def mine(p):
            a, off, rows_ = pieces[p]
            return ins[a].at[pl.ds(pl.multiple_of(c * (ins[a].shape[0] // 2) + off, 16), rows_)]

        def copy(k, src, dst, to):
            return pltpu.make_async_remote_copy(src_ref=src, dst_ref=dst, send_sem=send.at[k], recv_sem=recv.at[k],
                                                device_id=to, device_id_type=MESH)

        def cw_rows(chip):
            return ocw.at[pl.ds(pl.multiple_of(chip * 8, 8), 8)]

        locals_ = []
        for a in range(n):
            r = ins[a].shape[0]
            locals_.append(pltpu.make_async_copy(ins[a], outs[a].at[pl.ds(pl.multiple_of(me * r, 16), r)], lsem.at[a]))
        locals_.append(pltpu.make_async_copy(cw, cw_rows(me), lsem.at[n]))
        for cp in locals_:
            cp.start()

        sent = []
        for p in range(npc):
            for j in range(2):
                sent.append(copy(p * 6 + j, mine(p), landed(p, me, c), (*chips[j], c)))
        for j, chip in enumerate(chips):
            sent.append(copy(npc * 6 + j, cw, cw_rows(me), (*chip, c)))
        for cp in sent:
            cp.start()
        for p in range(npc):
            for j in range(3):
                got = landed(p, _chip_of(*chips[j]), c)
                copy(p * 6 + j, got, got, sibling).wait_recv()
                if j == 0:
                    sent.append(copy(p * 6 + 2, got, got, (*far, c)))
                    sent[-1].start()
                sent.append(copy(p * 6 + 3 + j, got, got, sibling))
                sent[-1].start()
        for p in range(npc):
            for j in range(3):
                got = landed(p, _chip_of(*chips[(1, 0, 2)[j]]), 1 - c)
                copy(p * 6 + 3 + j, got, got, sibling).wait_recv()
        for j, chip in enumerate(chips):
            got = cw_rows(_chip_of(*chip))
            copy(npc * 6 + j, got, got, (*chip, c)).wait_recv()
        for cp in sent:
            cp.wait_send()
        for cp in locals_:
            cp.wait()

    vm = pl.BlockSpec(memory_space=pltpu.VMEM)
    out_shape = tuple(jax.ShapeDtypeStruct((N_CHIPS * a.shape[0],) + a.shape[1:], a.dtype) for a in arrs) + (
        jax.ShapeDtypeStruct((N_CHIPS * 8, 128), F32),)
    n_rdma = npc * 6 + 3
    return pl.pallas_call(
        body, name="gather_weights", out_shape=out_shape,
        in_specs=[vm] * 4, out_specs=(vm,) * 4,
        scratch_shapes=[pltpu.SemaphoreType.DMA((n_rdma,)), pltpu.SemaphoreType.DMA((n_rdma,)),
                        pltpu.SemaphoreType.DMA((n + 1,))],
        compiler_params=pltpu.CompilerParams(vmem_limit_bytes=VMEM_LIMIT),
    )(win_t, wout, wkv, convw)


def mem_fwd(mem, mem_g, wkv, xk_g):
    def body(mem_ref, g_ref, w_ref, xk_ref, km_ref, vm_ref):
        mem_v = mem_ref[...]
        mn = mem_v * lax.rsqrt(_row_mean(mem_v * mem_v) + EPS) * g_ref[...]
        mkv = _mm(mn, w_ref[...])
        kpre = mkv[:, :XATT_W]
        gm = _group_matrix(XATT_W)
        km = kpre * lax.rsqrt(_seg_mean(kpre * kpre, gm) + EPS) * xk_ref[...]
        km_ref[...] = _head_rows(km, 0.125).astype(km_ref.dtype)
        vm_ref[...] = _head_rows(mkv[:, XATT_W:], 1.0).astype(vm_ref.dtype)

    vm = pl.BlockSpec(memory_space=pltpu.VMEM)
    rows_shape = jax.ShapeDtypeStruct((4 * MEM_LEN, XATT_W), _MXU)
    return pl.pallas_call(
        body, name="mem_fwd", out_shape=(rows_shape, rows_shape), in_specs=[vm] * 4, out_specs=(vm, vm),
    )(mem, mem_g, wkv, xk_g)


def mem_bwd(mem, mem_g, wkv, xk_g, dkm, dvm):
    def body(mem_ref, g_ref, w_ref, xk_ref, dkm_ref, dvm_ref, gw_ref, gg_ref, gxk_ref):
        mem_v = mem_ref[...]
        mh = mem_v * lax.rsqrt(_row_mean(mem_v * mem_v) + EPS)
        mn = mh * g_ref[...]
        mkv = _mm(mn, w_ref[...])
        kpre = mkv[:, :XATT_W]
        gm = _group_matrix(XATT_W)
        rk = lax.rsqrt(_seg_mean(kpre * kpre, gm) + EPS)
        kn = kpre * rk
        dk = jnp.zeros((MEM_LEN, XATT_W), F32)
        dv = jnp.zeros((MEM_LEN, XATT_W), F32)
        for j in range(4):
            mj = _lane_mask(XATT_W, HEAD * j, HEAD * (j + 1))
            dk = dk + dkm_ref[:, MEM_LEN * j:MEM_LEN * (j + 1)].T * (mj * 0.125)
            dv = dv + dvm_ref[:, MEM_LEN * j:MEM_LEN * (j + 1)].T * mj
        gxk_ref[...] = _col_sum(dk * kn)
        dkn = dk * xk_ref[...]
        dkpre = rk * (dkn - kn * _seg_mean(dkn * kn, gm))
        dmkv = jnp.concatenate([dkpre, dv], axis=1)
        gw_ref[...] = _mm_tn(mn, dmkv)
        dmn = _mm_nt(dmkv, w_ref[...])
        gg_ref[...] = _col_sum(dmn * mh)

    vm = pl.BlockSpec(memory_space=pltpu.VMEM)
    return pl.pallas_call(
        body, name="mem_bwd",
        out_shape=(jax.ShapeDtypeStruct((D_MODEL, 2 * XATT_W), F32), jax.ShapeDtypeStruct((1, D_MODEL), F32),
                   jax.ShapeDtypeStruct((1, XATT_W), F32)),
        in_specs=[vm] * 6, out_specs=(vm, vm, vm),
    )(mem, mem_g, wkv, xk_g, dkm, dvm)


def layer_fwd(x, tgt, rc, rs1, rs2, ng, win_t, cw, cb, wg, brg, big, lam, qg, kg, xqg, sinks, km, vm, og, wout):
    seq = x.shape[0]
    tm = min(ROW_TILE, seq)
    nt = seq // tm
    nb = tm // BLOCK

    def body(x_ref, t_ref, c_ref, s1_ref, s2_ref, ng_ref, win_ref, cw_ref, cb_ref, wg_ref, brg_ref, big_ref, lam_ref,
             qg_ref, kg_ref, xqg_ref, sink_ref, km_ref, vm_ref, og_ref, wout_ref,
             proj_ref, ya_ref, yb_ref, yc_ref, ycat_ref, xn_ref, dout_ref, pswa_ref, pmem_ref, psink_ref, gates_ref,
             a_ref, loss_ref,
             ext_ref, b_scr, hc_ref, kp_ref, vp_ref, lacc_ref):
        i = pl.program_id(0)

        @pl.when(i == 0)
        def _():
            ext_ref[0:8, :] = jnp.zeros((8, LRU_W), F32)
            hc_ref[...] = jnp.zeros_like(hc_ref)
            kp_ref[...] = jnp.zeros_like(kp_ref)
            vp_ref[...] = jnp.zeros_like(vp_ref)
            lacc_ref[...] = jnp.zeros_like(lacc_ref)

        xv = x_ref[...]
        xn = (xv * lax.rsqrt(_row_mean(xv * xv) + EPS) * ng_ref[...]).astype(_MXU)
        xn_ref[...] = xn.astype(xn_ref.dtype)
        proj_ref[...] = _mm_nt(xn, win_ref[...])

        u = proj_ref[:, C_LRUX:C_LRUX + LRU_W]
        ext_ref[8:8 + tm, :] = u
        xc = cb_ref[...]
        for k in range(CONV_K):
            xc = xc + cw_ref[k:k + 1, :] * ext_ref[pl.ds(5 + k, tm), :]
        ext_ref[0:8, :] = u[tm - 8:tm, :]
        rg, ig, sp, a, sq = _lru_gates(xc, wg_ref, brg_ref[...], big_ref[...], lam_ref[...])
        for k, t in enumerate((xc, rg, ig, sq)):
            gates_ref[:, LRU_W * k:LRU_W * (k + 1)] = t.astype(gates_ref.dtype)
        a_ref[...] = a
        b_scr[...] = sq * (ig * xc)
        row8 = lax.broadcasted_iota(jnp.int32, (8, LRU_W), 0)

        def scan_step(g, carry):
            r0 = pl.multiple_of(g * 8, 8)
            av = a_ref[pl.ds(r0, 8), :]
            bv = b_scr[pl.ds(r0, 8), :]
            for d in (1, 2, 4):
                a_sh = jnp.where(row8 >= d, pltpu.roll(av, d, 0), 1.0)
                b_sh = jnp.where(row8 >= d, pltpu.roll(bv, d, 0), 0.0)
                bv = bv + av * b_sh
                av = av * a_sh
            hv = bv + av * carry
            ya_ref[pl.ds(r0, 8), :] = hv
            return hv[7:8, :]

        hc_ref[0:1, :] = lax.fori_loop(0, tm // 8, scan_step, hc_ref[0:1, :], unroll=True)

        gm128 = _group_matrix(KV_W)
        cv, s1v, s2v = c_ref[...], s1_ref[...], s2_ref[...]

        def head_norm_rope(t, g):
            n = t * lax.rsqrt(_seg_mean(t * t, gm128) + EPS)
            return _rope(n * g, cv, s1v, s2v)

        qs_ = (head_norm_rope(proj_ref[:, C_SQ:C_SQ + 128], qg_ref[...]).astype(_MXU),
               head_norm_rope(proj_ref[:, C_SQ + 128:C_SQ + 256], qg_ref[...]).astype(_MXU))
        kr = head_norm_rope(proj_ref[:, C_SK:C_SK + KV_W], kg_ref[...])
        sv = proj_ref[:, C_SV:C_SV + KV_W]
        ka = _place_kv(jnp.concatenate([kp_ref[...], kr], axis=0), 0.125)
        va = _place_kv(jnp.concatenate([vp_ref[...], sv], axis=0), 1.0)
        kp_ref[...] = kr[tm - BLOCK:tm, :]
        vp_ref[...] = sv[tm - BLOCK:tm, :]
        lane128 = lax.broadcasted_iota(jnp.int32, (1, 128), 1)
        for b in range(nb):
            mask = _swa_mask((i == 0) & (b == 0)) if b == 0 else _swa_mask(False)
            band = slice(BLOCK * b, BLOCK * b + 2 * BLOCK)
            blk = slice(BLOCK * b, BLOCK * (b + 1))
            psink = jnp.zeros((BLOCK, 128), F32)
            for j in range(4):
                p, pk = _swa_probs(qs_[j // 2][blk], ka[j][band], mask, sink_ref[0, j])
                pswa_ref[blk, 2 * BLOCK * j:2 * BLOCK * (j + 1)] = p.astype(pswa_ref.dtype)
                psink = jnp.where(lane128 == j, pk, psink)
            psink_ref[blk, :] = psink
            for h in range(2):
                yb_ref[blk, KV_W * h:KV_W * (h + 1)] = _mm(
                    pswa_ref[blk, 4 * BLOCK * h:4 * BLOCK * (h + 1)],
                    jnp.concatenate([va[2 * h][band], va[2 * h + 1][band]], axis=0))

        gm256 = _group_matrix(XATT_W)
        xq = proj_ref[:, C_XQ:C_XQ + XATT_W]
        qx = xq * lax.rsqrt(_seg_mean(xq * xq, gm256) + EPS) * xqg_ref[...]
        pm = _mem_probs(_mm_nt(qx, km_ref[...]))
        for j in range(4):
            pmem_ref[:, MEM_LEN * j:MEM_LEN * (j + 1)] = pm[j].astype(pmem_ref.dtype)
        yc = _mm(pmem_ref[...], vm_ref[...])
        yc_ref[...] = yc

        def gated(y, g, gate):
            return y * lax.rsqrt(_row_mean(y * y) + EPS) * g * (gate * _sigmoid(gate))

        ogv = og_ref[...]
        za = gated(ya_ref[...], ogv[:, :512], proj_ref[:, C_LRUG:C_LRUG + LRU_W])
        zb = gated(yb_ref[...], ogv[:, 512:768], proj_ref[:, C_SWAG:C_SWAG + SWA_W])
        zc = gated(yc, ogv[:, 768:], proj_ref[:, C_XG:C_XG + XATT_W])
        ycat_ref[:, 0:512] = za.astype(ycat_ref.dtype)
        ycat_ref[:, 512:768] = zb.astype(ycat_ref.dtype)
        ycat_ref[:, 768:1024] = zc.astype(ycat_ref.dtype)
        out = xv + _mm(ycat_ref[...], wout_ref[...])
        err = out - t_ref[...]
        dout_ref[...] = (err * (1.0 / D_MODEL)).astype(dout_ref.dtype)
        lacc_ref[...] = lacc_ref[...] + (0.5 / D_MODEL) * jnp.sum(err * err)

        @pl.when(i == nt - 1)
        def _():
            loss_ref[...] = lacc_ref[...]

    def rows(ncol):
        return pl.BlockSpec((tm, ncol), lambda i: (i, 0))

    in_specs = [rows(D_MODEL), rows(D_MODEL), rows(128), rows(128), rows(128),
                _const_spec((1, D_MODEL)), _const_spec((D_IN, D_MODEL), True), _const_spec((CONV_K, LRU_W)),
                _const_spec((1, LRU_W)), _const_spec((2, 256, 512), True), _const_spec((1, LRU_W)),
                _const_spec((1, LRU_W)), _const_spec((1, LRU_W)), _const_spec((1, 128)), _const_spec((1, 128)),
                _const_spec((1, XATT_W)), pl.BlockSpec(memory_space=pltpu.SMEM),
                _const_spec((4 * MEM_LEN, XATT_W), True), _const_spec((4 * MEM_LEN, XATT_W), True),
                _const_spec((1, D_MODEL)), _const_spec((D_MODEL, D_MODEL), True)]
    out_shape = (jax.ShapeDtypeStruct((seq, D_IN), F32), jax.ShapeDtypeStruct((seq, LRU_W), F32),
                 jax.ShapeDtypeStruct((seq, SWA_W), F32), jax.ShapeDtypeStruct((seq, XATT_W), F32),
                 jax.ShapeDtypeStruct((seq, D_MODEL), _MXU), jax.ShapeDtypeStruct((seq, D_MODEL), _MXU),
                 jax.ShapeDtypeStruct((seq, D_MODEL), _MXU), jax.ShapeDtypeStruct((seq, 4 * 2 * BLOCK), _MXU),
                 jax.ShapeDtypeStruct((seq, 4 * MEM_LEN), _MXU), jax.ShapeDtypeStruct((seq, 128), F32),
                 jax.ShapeDtypeStruct((seq, 4 * LRU_W), _MXU), jax.ShapeDtypeStruct((seq, LRU_W), F32),
                 jax.ShapeDtypeStruct((8, 128), F32))
    out_specs = (rows(D_IN), rows(LRU_W), rows(SWA_W), rows(XATT_W), rows(D_MODEL), rows(D_MODEL), rows(D_MODEL),
                 rows(4 * 2 * BLOCK), rows(4 * MEM_LEN), rows(128), rows(4 * LRU_W), rows(LRU_W),
                 _const_spec((8, 128)))
    scratch = [pltpu.VMEM((tm + 8, LRU_W), F32), pltpu.VMEM((tm, LRU_W), F32),
               pltpu.VMEM((8, LRU_W), F32), pltpu.VMEM((BLOCK, KV_W), F32), pltpu.VMEM((BLOCK, KV_W), F32),
               pltpu.VMEM((8, 128), F32)]
    return pl.pallas_call(
        body, name="layer_fwd", grid=(nt,), out_shape=out_shape, in_specs=in_specs, out_specs=out_specs,
        scratch_shapes=scratch,
        compiler_params=pltpu.CompilerParams(dimension_semantics=("arbitrary",), vmem_limit_bytes=VMEM_LIMIT),
    )(x, tgt, rc, rs1, rs2, ng, win_t, cw, cb, wg, brg, big, lam, qg, kg, xqg, sinks, km, vm, og, wout)


def wgrad_reduce(lhs, rhs, bigs, g_small, stage_at, name):
    seq, ncol = rhs.shape
    nblk = lhs.shape[1] // 256
    nres = len(bigs) + (g_small is not None)

    def body(l_ref, r_ref, *refs):
        if nres:
            _hosted_reduce(pl.program_id(0), stage_at, refs[:nres], refs[nres + 1:2 * nres + 1], refs[2 * nres + 1:],
                           g_small is not None)
        refs[nres][...] = _mm_tn(l_ref[...], r_ref[...])

    red_shape, scratch = _hosted_reduce_shapes(bigs, g_small) if nres else ([], [])
    vm = pl.BlockSpec(memory_space=pltpu.VMEM)
    hbm = pl.BlockSpec(memory_space=pl.ANY)
    operands = list(bigs) + ([] if g_small is None else [g_small])
    return pl.pallas_call(
        body, name=name, grid=(nblk,),
        out_shape=(jax.ShapeDtypeStruct((lhs.shape[1], ncol), F32), *red_shape),
        in_specs=[pl.BlockSpec((seq, 256), lambda j: (0, j)), _const_spec((seq, ncol), True)] + [hbm] * len(bigs)
        + [vm] * (g_small is not None),
        out_specs=(pl.BlockSpec((256, ncol), lambda j: (j, 0)),) + (hbm,) * len(red_shape),
        scratch_shapes=scratch,
        compiler_params=pltpu.CompilerParams(dimension_semantics=("arbitrary",), vmem_limit_bytes=VMEM_LIMIT),
    )(lhs, rhs, *operands)


def layer_bwd(x, dout, proj, ya, yb, yc, pswa, pmem, psink, gates, a_all, rc, rs1, rs2, ng, win_t, cw, wg, lam, qg, kg,
              xqg, km, vm, og, wout):
    seq = x.shape[0]
    tm = min(ROW_TILE, seq)
    nt = seq // tm
    nb = tm // BLOCK

    def body(x_ref, dout_ref, proj_ref, ya_ref, yb_ref, yc_ref, pswa_ref, pmem_ref, psink_ref, gates_ref, a_ref,
             c_ref, s1_ref, s2_ref,
             yah_ref, kvh_ref, ch_ref, s1h_ref, s2h_ref,
             ng_ref, win_ref, cw_ref, wg_ref, lam_ref, qg_ref, kg_ref, xqg_ref, km_ref, vm_ref, og_ref, wout_ref,
             gx_ref, dproj_ref, gwg_ref, dkm_ref, dvm_ref, gng_ref, gog_ref, gcb_ref, gbrg_ref, gbig_ref, glam_ref,
             gcw_ref, gqn_ref, gkn_ref, gxqn_ref, gsink_ref,
             hext_ref, aext_ref, an_scr, dh_scr, g_scr, dxc_ext, gcar_ref, dkcar_ref, dvcar_ref):
        i = pl.program_id(0)
        tile = nt - 1 - i
        first_tile = tile == 0

        @pl.when(i == 0)
        def _():
            for r in (gwg_ref, dkm_ref, dvm_ref, gng_ref, gog_ref, gcb_ref, gbrg_ref, gbig_ref, glam_ref, gcw_ref,
                      gqn_ref, gkn_ref, gxqn_ref, gsink_ref, gcar_ref, dkcar_ref, dvcar_ref):
                r[...] = jnp.zeros_like(r)
            dxc_ext[tm:tm + 8, :] = jnp.zeros((8, LRU_W), F32)
            aext_ref[tm:tm + 8, :] = jnp.zeros((8, LRU_W), F32)

        xv = x_ref[...]
        dov = dout_ref[...]
        dz = _mm_nt(dov, wout_ref[...])
        ogv = og_ref[...]

        def group_bwd(y, gate, g, dzg):
            r = lax.rsqrt(_row_mean(y * y) + EPS)
            n = y * r
            sg = _sigmoid(gate)
            dgate = dzg * (n * g) * (sg * (1.0 + gate * (1.0 - sg)))
            dng = dzg * (gate * sg)
            dn = dng * g
            return r * (dn - n * _row_mean(dn * n)), dgate, _col_sum(dng * n)

        dya, dga, goa = group_bwd(ya_ref[...], proj_ref[:, C_LRUG:C_LRUG + LRU_W], ogv[:, :512], dz[:, :512])
        dyb, dgb, gob = group_bwd(yb_ref[...], proj_ref[:, C_SWAG:C_SWAG + SWA_W], ogv[:, 512:768], dz[:, 512:768])
        dyc, dgc, goc = group_bwd(yc_ref[...], proj_ref[:, C_XG:C_XG + XATT_W], ogv[:, 768:], dz[:, 768:])
        gog_ref[...] += jnp.concatenate([goa, gob, goc], axis=1)
        dproj_ref[:, C_LRUG:C_LRUG + LRU_W] = dga.astype(dproj_ref.dtype)
        dproj_ref[:, C_SWAG:C_SWAG + SWA_W] = dgb.astype(dproj_ref.dtype)
        dproj_ref[:, C_XG:C_XG + XATT_W] = dgc.astype(dproj_ref.dtype)

        gm256 = _group_matrix(XATT_W)
        xq = proj_ref[:, C_XQ:C_XQ + XATT_W]
        rq = lax.rsqrt(_seg_mean(xq * xq, gm256) + EPS)
        qn = xq * rq
        qx = qn * xqg_ref[...]
        qxb = qx.astype(_MXU)
        dycb = dyc.astype(_MXU)
        dp_all = _mm_nt(dycb, vm_ref[...])
        dsm = []
        for j in range(4):
            pj = pmem_ref[:, MEM_LEN * j:MEM_LEN * (j + 1)].astype(F32)
            dp = dp_all[:, MEM_LEN * j:MEM_LEN * (j + 1)]
            dsm.append((pj * (dp - jnp.sum(pj * dp, axis=-1, keepdims=True))).astype(_MXU))
        ds_all = jnp.concatenate(dsm, axis=1)
        dvm_ref[...] += _mm_tn(dycb, pmem_ref[...])
        dkm_ref[...] += _mm_tn(qxb, ds_all)
        dqx = _mm(ds_all, km_ref[...])
        gxqn_ref[...] += _col_sum(dqx * qn)
        dqn = dqx * xqg_ref[...]
        dproj_ref[:, C_XQ:C_XQ + XATT_W] = (rq * (dqn - qn * _seg_mean(dqn * qn, gm256))).astype(dproj_ref.dtype)

        gm128 = _group_matrix(KV_W)
        cv, s1v, s2v = c_ref[...], s1_ref[...], s2_ref[...]

        def head_norm(t):
            r = lax.rsqrt(_seg_mean(t * t, gm128) + EPS)
            return t * r, r

        qn_, qr_ = zip(head_norm(proj_ref[:, C_SQ:C_SQ + 128]), head_norm(proj_ref[:, C_SQ + 128:C_SQ + 256]))
        qrope = [_rope(qn_[h] * qg_ref[...], cv, s1v, s2v).astype(_MXU) for h in range(2)]
        kn, krr = head_norm(proj_ref[:, C_SK:C_SK + KV_W])
        kr = _rope(kn * kg_ref[...], cv, s1v, s2v)
        khn, _ = head_norm(kvh_ref[:, 0:KV_W])
        khr = _rope(khn * kg_ref[...], ch_ref[...], s1h_ref[...], s2h_ref[...])
        ka = _place_kv(jnp.concatenate([khr, kr], axis=0), 0.125)
        va = _place_kv(jnp.concatenate([kvh_ref[:, KV_W:2 * KV_W], proj_ref[:, C_SV:C_SV + KV_W]], axis=0), 1.0)
        lane128 = lax.broadcasted_iota(jnp.int32, (1, 128), 1)
        gsink = jnp.zeros((1, 128), F32)
        dk_band, dv_band, dq_blk = [], [], []
        for b in range(nb):
            band = slice(BLOCK * b, BLOCK * b + 2 * BLOCK)
            blk = slice(BLOCK * b, BLOCK * (b + 1))
            dka, dva, dsb = [], [], []
            deltas = jnp.zeros((BLOCK, 128), F32)
            for j in range(4):
                qh = qrope[j // 2][blk]
                doh = dyb[blk, KV_W * (j // 2):KV_W * (j // 2 + 1)].astype(_MXU)
                pb = pswa_ref[blk, 2 * BLOCK * j:2 * BLOCK * (j + 1)]
                p = pb.astype(F32)
                dp = _mm_nt(doh, va[j][band])
                delta = jnp.sum(p * dp, axis=-1, keepdims=True)
                ds = (p * (dp - delta)).astype(_MXU)
                deltas = jnp.where(lane128 == j, delta, deltas)
                dva.append(_mm_tn(pb, doh))
                dka.append(_mm_tn(ds, qh))
                dsb.append(ds)
            gsink = gsink - _col_sum(psink_ref[blk, :] * deltas)
            dk_band.append(_unplace_kv(dka) * 0.125)
            dv_band.append(_unplace_kv(dva))
            dq_blk.append([_mm(jnp.concatenate(dsb[2 * h:2 * h + 2], axis=1),
                               jnp.concatenate([ka[2 * h][band], ka[2 * h + 1][band]], axis=0)) for h in range(2)])
        gsink_ref[...] += gsink
        dk_rows = [dk_band[b][BLOCK:] + (dk_band[b + 1][:BLOCK] if b + 1 < nb else dkcar_ref[...]) for b in range(nb)]
        dv_rows = [dv_band[b][BLOCK:] + (dv_band[b + 1][:BLOCK] if b + 1 < nb else dvcar_ref[...]) for b in range(nb)]
        dkcar_ref[...] = dk_band[0][:BLOCK]
        dvcar_ref[...] = dv_band[0][:BLOCK]
        dkg = _rope_bwd(jnp.concatenate(dk_rows, axis=0), cv, s1v, s2v)
        gkn = _col_sum(dkg * kn)
        dkn = dkg * kg_ref[...]
        dproj_ref[:, C_SK:C_SK + KV_W] = (krr * (dkn - kn * _seg_mean(dkn * kn, gm128))).astype(dproj_ref.dtype)
        dproj_ref[:, C_SV:C_SV + KV_W] = jnp.concatenate(dv_rows, axis=0).astype(dproj_ref.dtype)
        gqn = jnp.zeros((1, 128), F32)
        for h in range(2):
            dqg = _rope_bwd(jnp.concatenate([dq_blk[b][h] for b in range(nb)], axis=0), cv, s1v, s2v)
            gqn = gqn + _col_sum(dqg * qn_[h])
            dqn_ = dqg * qg_ref[...]
            dproj_ref[:, C_SQ + 128 * h:C_SQ + 128 * (h + 1)] = (
                qr_[h] * (dqn_ - qn_[h] * _seg_mean(dqn_ * qn_[h], gm128))).astype(dproj_ref.dtype)
        gqn_ref[...] += gqn
        gkn_ref[...] += gkn

        u = proj_ref[:, C_LRUX:C_LRUX + LRU_W]
        xc, rg, ig, sq = (gates_ref[:, LRU_W * k:LRU_W * (k + 1)].astype(F32) for k in range(4))
        a = a_ref[...]
        sp = _softplus(-lam_ref[...])
        hext_ref[0:8, :] = jnp.where(first_tile, 0.0, yah_ref[...])
        hext_ref[8:8 + tm, :] = ya_ref[...]
        hprev = hext_ref[pl.ds(7, tm), :]
        aext_ref[0:tm, :] = a
        an_scr[...] = aext_ref[pl.ds(1, tm), :]
        dh_scr[...] = dya
        dh_scr[tm - 1:tm, :] = dh_scr[tm - 1:tm, :] + gcar_ref[0:1, :]
        row8 = lax.broadcasted_iota(jnp.int32, (8, LRU_W), 0)

        def scan_step(gi, carry):
            r0 = pl.multiple_of((tm // 8 - 1 - gi) * 8, 8)
            av = an_scr[pl.ds(r0, 8), :]
            bv = dh_scr[pl.ds(r0, 8), :]
            for d in (1, 2, 4):
                a_sh = jnp.where(row8 < 8 - d, pltpu.roll(av, 8 - d, 0), 1.0)
                b_sh = jnp.where(row8 < 8 - d, pltpu.roll(bv, 8 - d, 0), 0.0)
                bv = bv + av * b_sh
                av = av * a_sh
            gv = bv + av * carry
            g_scr[pl.ds(r0, 8), :] = gv
            return gv[0:1, :]

        g0 = lax.fori_loop(0, tm // 8, scan_step, jnp.zeros((1, LRU_W), F32), unroll=True)
        gcar_ref[0:1, :] = a[0:1, :] * g0
        gv = g_scr[...]
        da = gv * hprev
        dig = gv * sq * xc
        dxc = gv * sq * ig
        dla = da * a - gv * (ig * xc) * ((a * a) / sq)
        drg = dla * ((-LRU_C) * sp)
        glam_ref[...] += _col_sum(dla * rg)
        dpr = drg * rg * (1.0 - rg)
        dpi = dig * ig * (1.0 - ig)
        gbrg_ref[...] += _col_sum(dpr)
        gbig_ref[...] += _col_sum(dpi)
        dpre0 = jnp.concatenate([dpr[:, :256], dpi[:, :256]], axis=1).astype(_MXU)
        dpre1 = jnp.concatenate([dpr[:, 256:], dpi[:, 256:]], axis=1).astype(_MXU)
        gwg_ref[0] += _mm_tn(xc[:, :256], dpre0)
        gwg_ref[1] += _mm_tn(xc[:, 256:], dpre1)
        dxc = dxc + jnp.concatenate([_mm_nt(dpre0, wg_ref[0]), _mm_nt(dpre1, wg_ref[1])], axis=1)
        gcb_ref[...] += _col_sum(dxc)
        dxc_ext[0:tm, :] = dxc
        du = jnp.zeros((tm, LRU_W), F32)
        for k in range(CONV_K):
            later = dxc_ext[pl.ds(3 - k, tm), :]
            gcw_ref[k:k + 1, :] += _col_sum(later * u)
            du = du + cw_ref[k:k + 1, :] * later
        dxc_ext[tm:tm + 8, :] = dxc[0:8, :]
        dproj_ref[:, C_LRUX:C_LRUX + LRU_W] = du.astype(dproj_ref.dtype)

        dxn = _mm(dproj_ref[...], win_ref[...])
        rx = lax.rsqrt(_row_mean(xv * xv) + EPS)
        xh = xv * rx
        gng_ref[...] += _col_sum(dxn * xh)
        dxh = dxn * ng_ref[...]
        gx_ref[...] = dov.astype(F32) + rx * (dxh - xh * _row_mean(dxh * xh))

        @pl.when(i == nt - 1)
        def _():
            glam_ref[...] = glam_ref[...] * (LRU_C * _sigmoid(-lam_ref[...]))

    def rows(ncol, arr_cols_block=0):
        return pl.BlockSpec((tm, ncol), lambda i: (nt - 1 - i, arr_cols_block))

    def halo(nrow, ncol, colblk=0):
        per = tm // nrow
        return pl.BlockSpec((nrow, ncol), lambda i: (jnp.maximum((nt - 1 - i) * per - 1, 0), colblk))

    in_specs = [rows(D_MODEL), rows(D_MODEL), rows(D_IN), rows(LRU_W), rows(SWA_W), rows(XATT_W),
                rows(4 * 2 * BLOCK), rows(4 * MEM_LEN), rows(128), rows(4 * LRU_W), rows(LRU_W),
                rows(128), rows(128), rows(128),
                halo(8, LRU_W), halo(BLOCK, 2 * KV_W, C_SK // (2 * KV_W)),
                halo(BLOCK, 128), halo(BLOCK, 128), halo(BLOCK, 128),
                _const_spec((1, D_MODEL)), _const_spec((D_IN, D_MODEL), True), _const_spec((CONV_K, LRU_W)),
                _const_spec((2, 256, 512), True), _const_spec((1, LRU_W)), _const_spec((1, 128)), _const_spec((1, 128)),
                _const_spec((1, XATT_W)),
                _const_spec((4 * MEM_LEN, XATT_W), True), _const_spec((4 * MEM_LEN, XATT_W), True),
                _const_spec((1, D_MODEL)), _const_spec((D_MODEL, D_MODEL), True)]
    small = [(2, 256, 512), (XATT_W, 4 * MEM_LEN), (XATT_W, 4 * MEM_LEN), (1, D_MODEL), (1, D_MODEL), (1, LRU_W), (1, LRU_W),
             (1, LRU_W), (1, LRU_W), (CONV_K, LRU_W), (1, 128), (1, 128), (1, XATT_W), (1, 128)]
    out_shape = (jax.ShapeDtypeStruct((seq, D_MODEL), F32), jax.ShapeDtypeStruct((seq, D_IN), _MXU)) + tuple(
        jax.ShapeDtypeStruct(s, F32) for s in small)
    out_specs = (rows(D_MODEL), rows(D_IN)) + tuple(_const_spec(s) for s in small)
    scratch = [pltpu.VMEM((tm + 8, LRU_W), F32), pltpu.VMEM((tm + 8, LRU_W), F32),
               pltpu.VMEM((tm, LRU_W), F32), pltpu.VMEM((tm, LRU_W), F32), pltpu.VMEM((tm, LRU_W), F32),
               pltpu.VMEM((tm + 8, LRU_W), F32),
               pltpu.VMEM((8, LRU_W), F32), pltpu.VMEM((BLOCK, KV_W), F32), pltpu.VMEM((BLOCK, KV_W), F32)]
    return pl.pallas_call(
        body, name="layer_bwd", grid=(nt,), out_shape=out_shape, in_specs=in_specs, out_specs=out_specs,
        scratch_shapes=scratch,
        compiler_params=pltpu.CompilerParams(dimension_semantics=("arbitrary",), vmem_limit_bytes=VMEM_LIMIT),
    )(x, dout, proj, ya, yb, yc, pswa, pmem, psink, gates, a_all, rc, rs1, rs2, ya, proj, rc, rs1, rs2,
      ng, win_t, cw, wg, lam, qg, kg, xqg, km, vm, og, wout)


def _reduce_protocol(big, sm, outs, osm, r1, r1s, wire, r2, r2s, wire2, ps, own, send, recv, lsem):
    nbig = len(big)
    x, y, c = lax.axis_index("x"), lax.axis_index("y"), lax.axis_index("c")
    sibling = (x, y, 1 - c)
    near, far, diag = _partners(x, y, c)
    me, near_id, far_id, diag_id = _chip_of(x, y), _chip_of(*near), _chip_of(*far), _chip_of(*diag)

    def copy(k, src, dst, to):
        return pltpu.make_async_remote_copy(src_ref=src, dst_ref=dst, send_sem=send.at[k], recv_sem=recv.at[k],
                                            device_id=to, device_id_type=MESH)

    def sent(stage, a):
        if a == nbig:
            src, dst, to = ((sm.at[1 - c], r1s, sibling), (r1s, r2s.at[0], (*near, c)), (ps, r2s.at[1], (*far, c)),
                            (osm.at[c], osm.at[c], sibling))[stage]
            return [copy(5 * nbig + stage, src, dst, to)]
        if stage == 0:
            return [copy(5 * a, big[a].at[:, 1 - c], r1[a], sibling)]
        if stage == 1:
            return [copy(5 * a + 1, wire[a].at[near_id], r2[a].at[0], (*near, c)),
                    copy(5 * a + 2, wire[a].at[diag_id], r2[a].at[1], (*near, c))]
        if stage == 2:
            return [copy(5 * a + 3, wire2[a], r2[a].at[2], (*far, c))]
        return [copy(5 * a + 4, outs[a].at[c], outs[a].at[c], sibling)]

    arrays = range(nbig + (sm is not None))

    def start(stage, a):
        for cp in sent(stage, a):
            cp.start()

    def arrived(k, ref):
        copy(k, ref, ref, sibling).wait_recv()

    def loads():
        return [pltpu.make_async_copy(big[a].at[:, c], own[a], lsem.at[a]) for a in range(nbig)]

    def stage0():
        for a in arrays:
            start(0, a)
        for cp in loads():
            cp.start()

    def stage1():
        for a in range(nbig):
            loads()[a].wait()
            arrived(5 * a, r1[a])
            for k in range(N_CHIPS):
                r1[a][k] = own[a][k] + r1[a][k]
                wire[a][k] = r1[a][k].astype(wire[a].dtype)
            start(1, a)
        if sm is not None:
            arrived(5 * nbig, r1s)
            r1s[...] = sm[c] + r1s[...]
            start(1, nbig)

    def stage2():
        for a in range(nbig):
            arrived(5 * a + 1, r2[a].at[0])
            arrived(5 * a + 2, r2[a].at[1])
            r1[a][me] = r1[a][me] + r2[a][0].astype(F32)
            wire2[a][...] = (r1[a][far_id] + r2[a][1].astype(F32)).astype(wire2[a].dtype)
            start(2, a)
        if sm is not None:
            arrived(5 * nbig + 1, r2s.at[0])
            ps[...] = r1s[...] + r2s[0]
            start(2, nbig)

    def stage3():
        for a in range(nbig):
            arrived(5 * a + 3, r2[a].at[2])
            outs[a][c] = r1[a][me] + r2[a][2].astype(F32)
            start(3, a)
        if sm is not None:
            arrived(5 * nbig + 2, r2s.at[1])
            osm[c] = ps[...] + r2s[1]
            start(3, nbig)

    def stage4():
        for a in range(nbig):
            arrived(5 * a + 4, outs[a].at[1 - c])
        if sm is not None:
            arrived(5 * nbig + 3, osm.at[1 - c])
        for stage in range(4):
            for a in arrays:
                for cp in sent(stage, a):
                    cp.wait_send()

    return [stage0, stage1, stage2, stage3, stage4]


def _reduce_buffers(bigs, g_small):
    half = [b.shape[2:] for b in bigs]
    sm_half = None if g_small is None else g_small.shape[1:]
    out_shape = [jax.ShapeDtypeStruct((2,) + h, F32) for h in half]
    small = lambda lead: [] if g_small is None else [pltpu.VMEM(lead + sm_half, F32)]
    if g_small is not None:
        out_shape.append(jax.ShapeDtypeStruct(g_small.shape, F32))
    n_sem = 5 * len(bigs) + 4
    scratch = ([pltpu.VMEM((N_CHIPS,) + h, F32) for h in half] + small(())
               + [pltpu.VMEM((N_CHIPS,) + h, _WIRE) for h in half]
               + [pltpu.VMEM((3,) + h, _WIRE) for h in half] + small((2,))
               + [pltpu.VMEM(h, _WIRE) for h in half] + small(())
               + [pltpu.VMEM((N_CHIPS,) + h, F32) for h in half]
               + [pltpu.SemaphoreType.DMA((n_sem,)), pltpu.SemaphoreType.DMA((n_sem,)),
                  pltpu.SemaphoreType.DMA((len(bigs),))])
    return out_shape, scratch


def _split_reduce_refs(refs, nbig, has_small):
    it = iter(refs)
    take = lambda n: [next(it) for _ in range(n)]
    one = lambda: next(it) if has_small else None
    big, sm = take(nbig), one()
    outs, osm = take(nbig), one()
    r1, r1s, wire, r2, r2s, wire2, ps, own = take(nbig), one(), take(nbig), take(nbig), one(), take(nbig), one(), take(nbig)
    send, recv, lsem = take(3)
    return big, sm, outs, osm, r1, r1s, wire, r2, r2s, wire2, ps, own, send, recv, lsem


def _hosted_reduce_shapes(bigs, g_small):
    red_shape, scratch = _reduce_buffers(bigs, g_small)
    nres = len(red_shape)
    return red_shape, [pltpu.VMEM(r.shape, r.dtype) for r in red_shape] + scratch + [pltpu.SemaphoreType.DMA((nres,))]


def _hosted_reduce(step, stage_at, operands, results, scratch, has_small):
    nres = len(results)
    sums, rest, fsem = scratch[:nres], scratch[nres:-1], scratch[-1]
    refs = tuple(operands) + tuple(sums) + tuple(rest)
    for at, stage in zip(stage_at, _reduce_protocol(*_split_reduce_refs(refs, nres - has_small, has_small))):
        pl.when(step == at)(stage)

    @pl.when(step == stage_at[-1])
    def _():
        out = [pltpu.make_async_copy(sums[k], results[k], fsem.at[k]) for k in range(nres)]
        for cp in out:
            cp.start()
        for cp in out:
            cp.wait()


def reduce_grads(big, name, parts):
    chips, halves, rows_, cols = big.shape
    sub = jax.ShapeDtypeStruct((chips, halves, rows_ // parts, cols), big.dtype)

    def body(b_ref, o_ref, *scratch):
        refs = [b_ref.at[:, :, s] for s in range(parts)] + [o_ref.at[:, s] for s in range(parts)] + list(scratch)
        for stage in _reduce_protocol(*_split_reduce_refs(refs, parts, False)):
            stage()

    _, scratch = _reduce_buffers([sub] * parts, None)
    return pl.pallas_call(
        body, name=name, out_shape=jax.ShapeDtypeStruct((halves, parts, rows_ // parts, cols), F32),
        in_specs=[pl.BlockSpec(memory_space=pl.ANY)], out_specs=pl.BlockSpec(memory_space=pltpu.VMEM),
        scratch_shapes=scratch, compiler_params=pltpu.CompilerParams(vmem_limit_bytes=VMEM_LIMIT),
    )(big.reshape(chips, halves, parts, rows_ // parts, cols))


def adamw(w, g, m, v, name):
    rows_, cols = w.shape
    tr = max(t for t in range(8, rows_ + 1, 8) if rows_ % t == 0 and t * cols * 4 <= ADAM_BLOCK_BYTES)

    def body(w_ref, g_ref, m_ref, v_ref, d_ref, nm_ref, nv_ref):
        d_ref[...], nm_ref[...], nv_ref[...] = _adam_update(w_ref[...], g_ref[...], m_ref[...], v_ref[...])

    spec = pl.BlockSpec((tr, cols), lambda i: (i, 0))
    shp = jax.ShapeDtypeStruct(w.shape, F32)
    return pl.pallas_call(
        body, name=name, grid=(rows_ // tr,), out_shape=(shp, shp, shp), in_specs=[spec] * 4, out_specs=(spec,) * 3,
        compiler_params=pltpu.CompilerParams(dimension_semantics=("arbitrary",)),
    )(w, g, m, v)


def _adam_update(w, g, m, v):
    nm = ADAM_B1 * m + (1.0 - ADAM_B1) * g
    nv = ADAM_B2 * v + (1.0 - ADAM_B2) * (g * g)
    m_hat = nm / (1.0 - ADAM_B1 ** ADAM_STEP)
    v_hat = nv / (1.0 - ADAM_B2 ** ADAM_STEP)
    return (-ADAM_LR) * (m_hat / (jnp.sqrt(v_hat) + ADAM_EPS) + ADAM_WD * w), nm, nv


def adamw_vectors(g_pack, g_mats, ws, ms, vs):
    nvec, nmat = len(SMALL_VECTORS), len(g_mats)
    n = nvec + nmat

    def body(*refs):
        pk = refs[0]
        gm_refs = refs[1:1 + nmat]
        w_refs, m_refs, v_refs = (refs[1 + nmat + k * n:1 + nmat + (k + 1) * n] for k in range(3))
        outs = refs[1 + nmat + 3 * n:]
        g_out, d_out, nm_out, nv_out = outs[:nvec], outs[nvec:nvec + n], outs[nvec + n:nvec + 2 * n], outs[nvec + 2 * n:]
        chip = 2 * lax.axis_index("x") + lax.axis_index("y")
        for k, (name, row, width) in enumerate(SMALL_VECTORS):
            if name == "conv_w":
                g = jnp.concatenate([pk[pl.ds(row + 4 * t + chip, 1), :] for t in range(CONV_K)], axis=0)[None]
            elif width >= 128:
                g = jnp.concatenate([pk[row + r:row + r + 1, :] for r in range(width // 128)], axis=1)
            else:
                g = pk[row:row + 1, 0:width]
            g_out[k][...] = g
            d_out[k][...], nm_out[k][...], nv_out[k][...] = _adam_update(w_refs[k][...], g, m_refs[k][...], v_refs[k][...])
        for k in range(nvec, n):
            d_out[k][...], nm_out[k][...], nv_out[k][...] = _adam_update(
                w_refs[k][...], gm_refs[k - nvec][...], m_refs[k][...], v_refs[k][...])

    vm = pl.BlockSpec(memory_space=pltpu.VMEM)
    like = [jax.ShapeDtypeStruct(w.shape, F32) for w in ws]
    out_shape = like[:nvec] + like * 3
    return pl.pallas_call(
        body, name="adamw_vectors", out_shape=tuple(out_shape), in_specs=[vm] * (1 + nmat + 3 * n),
        out_specs=(vm,) * len(out_shape),
    )(g_pack, *g_mats, *ws, *ms, *vs)


SMALL_VECTORS = (("norm_g", 0, 1024), ("mem_norm_g", 8, 1024), ("conv_w", 16, 512), ("conv_b", 32, 512),
                 ("b_rg", 292, 512), ("b_ig", 552, 512), ("lru_lambda", 556, 512), ("q_norm_g", 560, 64),
                 ("k_norm_g", 561, 64), ("sinks", 562, 4), ("xq_norm_g", 563, 64), ("xk_norm_g", 564, 64),
                 ("out_norm_g", 565, 1024))
SMALL_MATRICES = (("w_rg", 36), ("w_ig", 296))
LOSS_ROW = 573
SMALL_ROWS = 576


def _pack(parts, rows_):
    flat = jnp.concatenate([p.reshape(-1) for p in parts])
    return jnp.pad(flat, (0, rows_ * 128 - flat.shape[0])).reshape(rows_, 128)


def _pad_to(v, n):
    v = v.reshape(-1)
    return jnp.pad(v, (0, n - v.shape[0]))


def _block_diag_gates(w_rg, w_ig):
    eye = jnp.eye(4, dtype=w_rg.dtype)

    def bd(w4):
        return (w4[:, :, None, :] * eye[:, None, :, None]).reshape(256, 256)

    return jnp.stack([jnp.concatenate([bd(w_rg[4 * h:4 * h + 4]), bd(w_ig[4 * h:4 * h + 4])], axis=1) for h in (0, 1)])


def _diag_blocks(g):
    g6 = g.reshape(2, 4, HEAD, 2, 4, HEAD)
    d = (g6 * jnp.eye(4, dtype=g.dtype)[None, :, None, None, :, None]).sum(axis=4)
    return d[:, :, :, 0].reshape(8, HEAD, HEAD), d[:, :, :, 1].reshape(8, HEAD, HEAD)


def _rope_tables(seq):
    pos = np.arange(seq, dtype=np.float32)
    inv_freq = (np.float32(ROPE_THETA) ** (-(np.arange(0, ROPE_DIM, 2, dtype=np.float32) / np.float32(ROPE_DIM)))
                ).astype(np.float32)
    ang = (pos[:, None] * inv_freq[None, :]).astype(np.float32)
    cos, sin = np.cos(ang).astype(np.float32), np.sin(ang).astype(np.float32)
    z = lambda n: np.zeros((seq, n), np.float32)
    c64 = np.concatenate([cos, cos, np.ones((seq, HEAD - ROPE_DIM), np.float32)], axis=1)
    s1_64 = np.concatenate([-sin, z(HEAD - 8)], axis=1)
    s2_64 = np.concatenate([z(8), sin, z(HEAD - ROPE_DIM)], axis=1)
    return tuple(jnp.asarray(np.concatenate([t, t], axis=1)) for t in (c64, s1_64, s2_64))


def kernel(x, mem, norm_g, mem_norm_g, w_in, conv_w, conv_b, w_rg, b_rg, w_ig, b_ig, lru_lambda, q_norm_g, k_norm_g, sinks, w_mem_kv, xq_norm_g, xk_norm_g, out_norm_g, w_out, loss_target, m_norm_g, m_mem_norm_g, m_w_in, m_conv_w, m_conv_b, m_w_rg, m_b_rg, m_w_ig, m_b_ig, m_lru_lambda, m_q_norm_g, m_k_norm_g, m_sinks, m_w_mem_kv, m_xq_norm_g, m_xk_norm_g, m_out_norm_g, m_w_out, v_norm_g, v_mem_norm_g, v_w_in, v_conv_w, v_conv_b, v_w_rg, v_b_rg, v_w_ig, v_b_ig, v_lru_lambda, v_q_norm_g, v_k_norm_g, v_sinks, v_w_mem_kv, v_xq_norm_g, v_xk_norm_g, v_out_norm_g, v_w_out):
    seq = x.shape[1]
    xs, tgt, mems = x[0], loss_target[0], mem[0]

    win_t_sh = w_in[0].T.astype(_MXU)
    cw_sh = jnp.pad(conv_w[0], ((0, 4), (0, 0)))
    win_t, wout, wkv, cw_all = gather_weights(win_t_sh, w_out[0].astype(_MXU), w_mem_kv[0].astype(_MXU), cw_sh)
    cw = cw_all.reshape(N_CHIPS, 8, 128)[:, :CONV_K].transpose(1, 0, 2).reshape(CONV_K, LRU_W)

    rc, rs1, rs2 = _rope_tables(seq)
    wg = _block_diag_gates(w_rg[0], w_ig[0]).astype(_MXU)
    qg = jnp.tile(q_norm_g, (1, 2))
    kg = jnp.tile(k_norm_g, (1, 2))
    xqg = jnp.tile(xq_norm_g, (1, 4))
    xkg = jnp.tile(xk_norm_g, (1, 4))

    km, vm = mem_fwd(mems, mem_norm_g, wkv, xkg)
    proj, ya, yb, yc, ycat, xn, dout, pswa, pmem, psink, gates, a_all, loss8 = layer_fwd(
        xs, tgt, rc, rs1, rs2, norm_g, win_t, cw, conv_b, wg, b_rg, b_ig, lru_lambda, qg, kg, xqg, sinks, km, vm,
        out_norm_g, wout)
    (g_wout,) = wgrad_reduce(ycat, dout, [], None, (), "wgrad_out")
    (gx, dproj, g_wg, dkm, dvm, g_ng, g_og, g_cb, g_brg, g_big, g_lam, g_cw, g_qn, g_kn, g_xqn, g_sink) = layer_bwd(
        xs, dout, proj, ya, yb, yc, pswa, pmem, psink, gates, a_all, rc, rs1, rs2, norm_g, win_t, cw, wg, lru_lambda, qg,
        kg, xqg, km, vm, out_norm_g, wout)
    g_wkv, g_mng, g_xkn = mem_bwd(mems, mem_norm_g, wkv, xkg, dkm, dvm)

    g_wrg, g_wig = _diag_blocks(g_wg)
    fold = lambda v, n: v.reshape(n, HEAD).sum(axis=0)
    small_g = _pack([g_ng, g_mng, g_cw, g_cb, g_wrg, g_brg, g_wig, g_big, g_lam, _pad_to(fold(g_qn, 2), 128),
                     _pad_to(fold(g_kn, 2), 128), g_sink, _pad_to(fold(g_xqn, 4), 128), _pad_to(fold(g_xkn, 4), 128),
                     g_og, loss8[0:1]], SMALL_ROWS)
    early = [g_wout.reshape(N_CHIPS, 2, D_MODEL // 8, D_MODEL), g_wkv.reshape(N_CHIPS, 2, D_MODEL // 8, 2 * XATT_W)]
    g_win_t, r_out, r_kv, r_small = wgrad_reduce(dproj, xn, early, small_g.reshape(2, SMALL_ROWS // 2, 128),
                                                 (0, 1, 6, 8, 8), "wgrad_in")
    r_in = reduce_grads(g_win_t.reshape(N_CHIPS, 2, D_IN // 8, D_MODEL), "reduce_w_in", 6)

    r_small = r_small.reshape(SMALL_ROWS, 128)
    loss = r_small[LOSS_ROW, 0]
    grads = {"w_in": r_in.reshape(D_IN // 4, D_MODEL).T[None], "w_mem_kv": r_kv.reshape(D_MODEL // 4, 2 * XATT_W)[None],
             "w_out": r_out.reshape(D_MODEL // 4, D_MODEL)[None]}
    for name, row in SMALL_MATRICES:
        grads[name] = r_small[row:row + 256].reshape(1, LRU_BLOCKS, HEAD, HEAD)
    weights = dict(norm_g=norm_g, mem_norm_g=mem_norm_g, w_in=w_in, conv_w=conv_w, conv_b=conv_b, w_rg=w_rg, b_rg=b_rg,
                   w_ig=w_ig, b_ig=b_ig, lru_lambda=lru_lambda, q_norm_g=q_norm_g, k_norm_g=k_norm_g, sinks=sinks,
                   w_mem_kv=w_mem_kv, xq_norm_g=xq_norm_g, xk_norm_g=xk_norm_g, out_norm_g=out_norm_g, w_out=w_out)
    ms = dict(norm_g=m_norm_g, mem_norm_g=m_mem_norm_g, w_in=m_w_in, conv_w=m_conv_w, conv_b=m_conv_b, w_rg=m_w_rg,
              b_rg=m_b_rg, w_ig=m_w_ig, b_ig=m_b_ig, lru_lambda=m_lru_lambda, q_norm_g=m_q_norm_g, k_norm_g=m_k_norm_g,
              sinks=m_sinks, w_mem_kv=m_w_mem_kv, xq_norm_g=m_xq_norm_g, xk_norm_g=m_xk_norm_g,
              out_norm_g=m_out_norm_g, w_out=m_w_out)
    vs = dict(norm_g=v_norm_g, mem_norm_g=v_mem_norm_g, w_in=v_w_in, conv_w=v_conv_w, conv_b=v_conv_b, w_rg=v_w_rg,
              b_rg=v_b_rg, w_ig=v_w_ig, b_ig=v_b_ig, lru_lambda=v_lru_lambda, q_norm_g=v_q_norm_g, k_norm_g=v_k_norm_g,
              sinks=v_sinks, w_mem_kv=v_w_mem_kv, xq_norm_g=v_xq_norm_g, xk_norm_g=v_xk_norm_g,
              out_norm_g=v_out_norm_g, w_out=v_w_out)

    delta, new_m, new_v = {}, {}, {}
    d2, m2, v2 = adamw(w_in[0].T, r_in.reshape(D_IN // 4, D_MODEL), m_w_in[0].T, v_w_in[0].T, "adamw_w_in")
    delta["w_in"], new_m["w_in"], new_v["w_in"] = d2.T[None], m2.T[None], v2.T[None]
    for name in ("w_mem_kv", "w_out"):
        shp = weights[name].shape
        d2, m2, v2 = adamw(weights[name][0], grads[name][0], ms[name][0], vs[name][0], "adamw_" + name)
        delta[name], new_m[name], new_v[name] = d2.reshape(shp), m2.reshape(shp), v2.reshape(shp)
    vec_names = [n for n, _, _ in SMALL_VECTORS]
    small_names = vec_names + [n for n, _ in SMALL_MATRICES]
    res = adamw_vectors(r_small, [grads[n] for n, _ in SMALL_MATRICES], [weights[n] for n in small_names],
                        [ms[n] for n in small_names], [vs[n] for n in small_names])
    nvec, nall = len(vec_names), len(small_names)
    grads.update(zip(vec_names, res[:nvec]))
    delta.update(zip(small_names, res[nvec:nvec + nall]))
    new_m.update(zip(small_names, res[nvec + nall:nvec + 2 * nall]))
    new_v.update(zip(small_names, res[nvec + 2 * nall:]))

    order = ("norm_g", "mem_norm_g", "w_in", "conv_w", "conv_b", "w_rg", "b_rg", "w_ig", "b_ig", "lru_lambda",
             "q_norm_g", "k_norm_g", "sinks", "w_mem_kv", "xq_norm_g", "xk_norm_g", "out_norm_g", "w_out")
    return (loss, gx[None], *[grads[n] for n in order], *[delta[n] for n in order], *[new_m[n] for n in order],
            *[new_v[n] for n in order])
```

```python
import jax
import jax.numpy as jnp
import numpy as np
from jax import lax
from jax.experimental import pallas as pl
from jax.experimental.pallas import tpu as pltpu

F32 = jnp.float32
_MXU = jnp.bfloat16
_WIRE = jnp.bfloat16

D_MODEL = 1024
MEM_LEN = 256
HEAD = 64
LRU_W = 512
LRU_BLOCKS = 8
CONV_K = 4
LRU_C = 8.0
SWA_W = 256
KV_W = 128
XATT_W = 256
BLOCK = 128
D_IN = 2304
ROPE_THETA = 500000.0
ROPE_DIM = 16
EPS = 1e-6
NEG_INF = -1e30
C_LRUX, C_LRUG, C_SQ, C_SK, C_SV, C_SWAG, C_XQ, C_XG = 0, 512, 1024, 1280, 1408, 1536, 1792, 2048

ADAM_LR, ADAM_B1, ADAM_B2, ADAM_EPS, ADAM_WD, ADAM_STEP = 0.001, 0.9, 0.999, 1e-08, 0.01, 10

N_CHIPS = 4
ROW_TILE = 256
VMEM_LIMIT = 56 * 1024 * 1024
ADAM_BLOCK_BYTES = 1280 * 1024
MESH = pl.DeviceIdType.MESH


def _mm(a, b):
    return jnp.dot(a.astype(_MXU), b.astype(_MXU), preferred_element_type=F32)


def _mm_nt(a, b):
    return lax.dot_general(a.astype(_MXU), b.astype(_MXU), (((1,), (1,)), ((), ())), preferred_element_type=F32)


def _mm_tn(a, b):
    return lax.dot_general(a.astype(_MXU), b.astype(_MXU), (((0,), (0,)), ((), ())), preferred_element_type=F32)


def _group_matrix(width):
    r = lax.shift_right_logical(lax.broadcasted_iota(jnp.int32, (width, width), 0), 6)
    c = lax.shift_right_logical(lax.broadcasted_iota(jnp.int32, (width, width), 1), 6)
    return (r == c).astype(_MXU)


def _seg_mean(x, gm):
    return jnp.dot(x.astype(_MXU), gm, preferred_element_type=F32) * (1.0 / HEAD)


def _row_mean(x):
    return jnp.mean(x, axis=-1, keepdims=True)


def _col_sum(x):
    return jnp.sum(x, axis=0, keepdims=True)


def _sigmoid(x):
    return jax.nn.sigmoid(x)


def _softplus(z):
    e = jnp.exp(-jnp.abs(z))
    u = 1.0 + e
    log1p_e = jnp.where(u == 1.0, e, jnp.log(u) * (e / (u - 1.0)))
    return jnp.maximum(z, 0.0) + log1p_e


def _rope(t, c, s1, s2):
    return t * c + pltpu.roll(t, 120, 1) * s1 + pltpu.roll(t, 8, 1) * s2


def _rope_bwd(d, c, s1, s2):
    return d * c + pltpu.roll(d * s1, 8, 1) + pltpu.roll(d * s2, 120, 1)


def _lane_mask(width, lo, hi):
    lane = lax.broadcasted_iota(jnp.int32, (1, width), 1)
    return ((lane >= lo) & (lane < hi)).astype(F32)


def _swa_mask(first_block):
    qi = lax.broadcasted_iota(jnp.int32, (BLOCK, 2 * BLOCK), 0)
    kj = lax.broadcasted_iota(jnp.int32, (BLOCK, 2 * BLOCK), 1)
    rel = qi + BLOCK - kj
    ok = (rel >= 0) & (rel < BLOCK)
    return ok & (jnp.logical_not(first_block) | (kj >= BLOCK))


def _place_kv(t, scale):
    lo = t * (_lane_mask(KV_W, 0, HEAD) * scale)
    hi = t * (_lane_mask(KV_W, HEAD, KV_W) * scale)
    return [a.astype(_MXU) for a in (lo, pltpu.roll(lo, HEAD, 1), pltpu.roll(hi, HEAD, 1), hi)]


def _unplace_kv(d):
    return (_lane_mask(KV_W, 0, HEAD) * (d[0] + pltpu.roll(d[1], HEAD, 1))
            + _lane_mask(KV_W, HEAD, KV_W) * (d[3] + pltpu.roll(d[2], HEAD, 1)))


def _swa_probs(qh, ka, mask, sink):
    s = _mm_nt(qh, ka)
    s = jnp.where(mask, s, NEG_INF)
    m = jnp.maximum(jnp.max(s, axis=-1, keepdims=True), sink)
    p = jnp.exp(s - m)
    esink = jnp.exp(sink - m)
    inv = 1.0 / (jnp.sum(p, axis=-1, keepdims=True) + esink)
    return p * inv, esink * inv


def _mem_probs(s_all):
    out = []
    for j in range(4):
        s = s_all[:, MEM_LEN * j:MEM_LEN * (j + 1)]
        p = jnp.exp(s - jnp.max(s, axis=-1, keepdims=True))
        out.append(p * (1.0 / jnp.sum(p, axis=-1, keepdims=True)))
    return out


def _head_rows(t, scale):
    return jnp.concatenate([t * (_lane_mask(XATT_W, HEAD * j, HEAD * (j + 1)) * scale) for j in range(4)], axis=0)


def _lru_gates(xc, wg_ref, brg, big, lam):
    p0 = _mm(xc[:, :256], wg_ref[0])
    p1 = _mm(xc[:, 256:], wg_ref[1])
    rg = _sigmoid(jnp.concatenate([p0[:, :256], p1[:, :256]], axis=1) + brg)
    ig = _sigmoid(jnp.concatenate([p0[:, 256:], p1[:, 256:]], axis=1) + big)
    sp = _softplus(-lam)
    la = (-LRU_C) * rg * sp
    a = jnp.exp(la)
    th = jnp.tanh(la)
    one_minus_a2 = (-2.0 * th) / (1.0 - th)
    return rg, ig, sp, a, jnp.sqrt(one_minus_a2)


def _const_spec(shape, single=False):
    zeros = (0,) * len(shape)
    if single:
        return pl.BlockSpec(shape, lambda i: zeros, pipeline_mode=pl.Buffered(1))
    return pl.BlockSpec(shape, lambda i: zeros)


def _chip_of(x, y):
    return 2 * x + y


def _partners(x, y, c):
    north = c == 1
    near = (jnp.where(north, 1 - x, x), jnp.where(north, y, 1 - y))
    far = (jnp.where(north, x, 1 - x), jnp.where(north, 1 - y, y))
    return near, far, (1 - x, 1 - y)


def gather_weights(win_t, wout, wkv, convw):
    arrs = (win_t, wout, wkv)
    n = len(arrs)
    pieces = [(a, 0, arr.shape[0] // 2) for a, arr in enumerate(arrs)]
    npc = len(pieces)

    def body(a0, a1, a2, cw, o0, o1, o2, ocw, send, recv, lsem):
        ins, outs = (a0, a1, a2), (o0, o1, o2)
        x, y, c = lax.axis_index("x"), lax.axis_index("y"), lax.axis_index("c")
        sibling = (x, y, 1 - c)
        near, far, diag = _partners(x, y, c)
        chips = [near, far, diag]
        me = _chip_of(x, y)

        def landed(p, chip, half):
            a, off, rows_ = pieces[p]
            r = ins[a].shape[0]
            return outs[a].at[pl.ds(pl.multiple_of(chip * r + half * (r // 2) + off, 16), rows_)]

        def mine(p):
            a, off, rows_ = pieces[p]
            return ins[a].at[pl.ds(pl.multiple_of(c * (ins[a].shape[0] // 2) + off, 16), rows_)]

        def copy(k, src, dst, to):
            return pltpu.make_async_remote_copy(src_ref=src, dst_ref=dst, send_sem=send.at[k], recv_sem=recv.at[k],
                                                device_id=to, device_id_type=MESH)

        def cw_rows(chip):
            return ocw.at[pl.ds(pl.multiple_of(chip * 8, 8), 8)]

        locals_ = []
        for a in range(n):
            r = ins[a].shape[0]
            locals_.append(pltpu.make_async_copy(ins[a], outs[a].at[pl.ds(pl.multiple_of(me * r, 16), r)], lsem.at[a]))
        locals_.append(pltpu.make_async_copy(cw, cw_rows(me), lsem.at[n]))
        for cp in locals_:
            cp.start()

        sent = []
        for p in range(npc):
            for j in range(2):
                sent.append(copy(p * 6 + j, mine(p), landed(p, me, c), (*chips[j], c)))
        for j, chip in enumerate(chips):
            sent.append(copy(npc * 6 + j, cw, cw_rows(me), (*chip, c)))
        for cp in sent:
            cp.start()
        for p in range(npc):
            for j in range(3):
                got = landed(p, _chip_of(*chips[j]), c)
                copy(p * 6 + j, got, got, sibling).wait_recv()
                if j == 0:
                    sent.append(copy(p * 6 + 2, got, got, (*far, c)))
                    sent[-1].start()
                sent.append(copy(p * 6 + 3 + j, got, got, sibling))
                sent[-1].start()
        for p in range(npc):
            for j in range(3):
                got = landed(p, _chip_of(*chips[(1, 0, 2)[j]]), 1 - c)
                copy(p * 6 + 3 + j, got, got, sibling).wait_recv()
        for j, chip in enumerate(chips):
            got = cw_rows(_chip_of(*chip))
            copy(npc * 6 + j, got, got, (*chip, c)).wait_recv()
        for cp in sent:
            cp.wait_send()
        for cp in locals_:
            cp.wait()

    vm = pl.BlockSpec(memory_space=pltpu.VMEM)
    out_shape = tuple(jax.ShapeDtypeStruct((N_CHIPS * a.shape[0],) + a.shape[1:], a.dtype) for a in arrs) + (
        jax.ShapeDtypeStruct((N_CHIPS * 8, 128), F32),)
    n_rdma = npc * 6 + 3
    return pl.pallas_call(
        body, name="gather_weights", out_shape=out_shape,
        in_specs=[vm] * 4, out_specs=(vm,) * 4,
        scratch_shapes=[pltpu.SemaphoreType.DMA((n_rdma,)), pltpu.SemaphoreType.DMA((n_rdma,)),
                        pltpu.SemaphoreType.DMA((n + 1,))],
        compiler_params=pltpu.CompilerParams(vmem_limit_bytes=VMEM_LIMIT),
    )(win_t, wout, wkv, convw)


def mem_fwd(mem, mem_g, wkv, xk_g):
    def body(mem_ref, g_ref, w_ref, xk_ref, km_ref, vm_ref):
        mem_v = mem_ref[...]
        mn = mem_v * lax.rsqrt(_row_mean(mem_v * mem_v) + EPS) * g_ref[...]
        mkv = _mm(mn, w_ref[...])
        kpre = mkv[:, :XATT_W]
        gm = _group_matrix(XATT_W)
        km = kpre * lax.rsqrt(_seg_mean(kpre * kpre, gm) + EPS) * xk_ref[...]
        km_ref[...] = _head_rows(km, 0.125).astype(km_ref.dtype)
        vm_ref[...] = _head_rows(mkv[:, XATT_W:], 1.0).astype(vm_ref.dtype)

    vm = pl.BlockSpec(memory_space=pltpu.VMEM)
    rows_shape = jax.ShapeDtypeStruct((4 * MEM_LEN, XATT_W), _MXU)
    return pl.pallas_call(
        body, name="mem_fwd", out_shape=(rows_shape, rows_shape), in_specs=[vm] * 4, out_specs=(vm, vm),
    )(mem, mem_g, wkv, xk_g)


def mem_bwd(mem, mem_g, wkv, xk_g, dkm, dvm):
    def body(mem_ref, g_ref, w_ref, xk_ref, dkm_ref, dvm_ref, gw_ref, gg_ref, gxk_ref):
        mem_v = mem_ref[...]
        mh = mem_v * lax.rsqrt(_row_mean(mem_v * mem_v) + EPS)
        mn = mh * g_ref[...]
        mkv = _mm(mn, w_ref[...])
        kpre = mkv[:, :XATT_W]
        gm = _group_matrix(XATT_W)
        rk = lax.rsqrt(_seg_mean(kpre * kpre, gm) + EPS)
        kn = kpre * rk
        dk = jnp.zeros((MEM_LEN, XATT_W), F32)
        dv = jnp.zeros((MEM_LEN, XATT_W), F32)
        for j in range(4):
            mj = _lane_mask(XATT_W, HEAD * j, HEAD * (j + 1))
            dk = dk + dkm_ref[:, MEM_LEN * j:MEM_LEN * (j + 1)].T * (mj * 0.125)
            dv = dv + dvm_ref[:, MEM_LEN * j:MEM_LEN * (j + 1)].T * mj
        gxk_ref[...] = _col_sum(dk * kn)
        dkn = dk * xk_ref[...]
        dkpre = rk * (dkn - kn * _seg_mean(dkn * kn, gm))
        dmkv = jnp.concatenate([dkpre, dv], axis=1)
        gw_ref[...] = _mm_tn(mn, dmkv)
        dmn = _mm_nt(dmkv, w_ref[...])
        gg_ref[...] = _col_sum(dmn * mh)

    vm = pl.BlockSpec(memory_space=pltpu.VMEM)
    return pl.pallas_call(
        body, name="mem_bwd",
        out_shape=(jax.ShapeDtypeStruct((D_MODEL, 2 * XATT_W), F32), jax.ShapeDtypeStruct((1, D_MODEL), F32),
                   jax.ShapeDtypeStruct((1, XATT_W), F32)),
        in_specs=[vm] * 6, out_specs=(vm, vm, vm),
    )(mem, mem_g, wkv, xk_g, dkm, dvm)


def layer_fwd(x, tgt, rc, rs1, rs2, ng, win_t, cw, cb, wg, brg, big, lam, qg, kg, xqg, sinks, km, vm, og, wout):
    seq = x.shape[0]
    tm = min(ROW_TILE, seq)
    nt = seq // tm
    nb = tm // BLOCK

    def body(x_ref, t_ref, c_ref, s1_ref, s2_ref, ng_ref, win_ref, cw_ref, cb_ref, wg_ref, brg_ref, big_ref, lam_ref,
             qg_ref, kg_ref, xqg_ref, sink_ref, km_ref, vm_ref, og_ref, wout_ref,
             proj_ref, ya_ref, yb_ref, yc_ref, ycat_ref, xn_ref, dout_ref, pswa_ref, pmem_ref, psink_ref, gates_ref,
             a_ref, loss_ref,
             ext_ref, b_scr, hc_ref, kp_ref, vp_ref, lacc_ref):
        i = pl.program_id(0)

        @pl.when(i == 0)
        def _():
            ext_ref[0:8, :] = jnp.zeros((8, LRU_W), F32)
            hc_ref[...] = jnp.zeros_like(hc_ref)
            kp_ref[...] = jnp.zeros_like(kp_ref)
            vp_ref[...] = jnp.zeros_like(vp_ref)
            lacc_ref[...] = jnp.zeros_like(lacc_ref)

        xv = x_ref[...]
        xn = (xv * lax.rsqrt(_row_mean(xv * xv) + EPS) * ng_ref[...]).astype(_MXU)
        xn_ref[...] = xn.astype(xn_ref.dtype)
        proj_ref[...] = _mm_nt(xn, win_ref[...])

        u = proj_ref[:, C_LRUX:C_LRUX + LRU_W]
        ext_ref[8:8 + tm, :] = u
        xc = cb_ref[...]
        for k in range(CONV_K):
            xc = xc + cw_ref[k:k + 1, :] * ext_ref[pl.ds(5 + k, tm), :]
        ext_ref[0:8, :] = u[tm - 8:tm, :]
        rg, ig, sp, a, sq = _lru_gates(xc, wg_ref, brg_ref[...], big_ref[...], lam_ref[...])
        for k, t in enumerate((xc, rg, ig, sq)):
            gates_ref[:, LRU_W * k:LRU_W * (k + 1)] = t.astype(gates_ref.dtype)
        a_ref[...] = a
        b_scr[...] = sq * (ig * xc)
        row8 = lax.broadcasted_iota(jnp.int32, (8, LRU_W), 0)

        def scan_step(g, carry):
            r0 = pl.multiple_of(g * 8, 8)
            av = a_ref[pl.ds(r0, 8), :]
            bv = b_scr[pl.ds(r0, 8), :]
            for d in (1, 2, 4):
                a_sh = jnp.where(row8 >= d, pltpu.roll(av, d, 0), 1.0)
                b_sh = jnp.where(row8 >= d, pltpu.roll(bv, d, 0), 0.0)
                bv = bv + av * b_sh
                av = av * a_sh
            hv = bv + av * carry
            ya_ref[pl.ds(r0, 8), :] = hv
            return hv[7:8, :]

        hc_ref[0:1, :] = lax.fori_loop(0, tm // 8, scan_step, hc_ref[0:1, :], unroll=True)

        gm128 = _group_matrix(KV_W)
        cv, s1v, s2v = c_ref[...], s1_ref[...], s2_ref[...]

        def head_norm_rope(t, g):
            n = t * lax.rsqrt(_seg_mean(t * t, gm128) + EPS)
            return _rope(n * g, cv, s1v, s2v)

        qs_ = (head_norm_rope(proj_ref[:, C_SQ:C_SQ + 128], qg_ref[...]).astype(_MXU),
               head_norm_rope(proj_ref[:, C_SQ + 128:C_SQ + 256], qg_ref[...]).astype(_MXU))
        kr = head_norm_rope(proj_ref[:, C_SK:C_SK + KV_W], kg_ref[...])
        sv = proj_ref[:, C_SV:C_SV + KV_W]
        ka = _place_kv(jnp.concatenate([kp_ref[...], kr], axis=0), 0.125)
        va = _place_kv(jnp.concatenate([vp_ref[...], sv], axis=0), 1.0)
        kp_ref[...] = kr[tm - BLOCK:tm, :]
        vp_ref[...] = sv[tm - BLOCK:tm, :]
        lane128 = lax.broadcasted_iota(jnp.int32, (1, 128), 1)
        for b in range(nb):
            mask = _swa_mask((i == 0) & (b == 0)) if b == 0 else _swa_mask(False)
            band = slice(BLOCK * b, BLOCK * b + 2 * BLOCK)
            blk = slice(BLOCK * b, BLOCK * (b + 1))
            psink = jnp.zeros((BLOCK, 128), F32)
            for j in range(4):
                p, pk = _swa_probs(qs_[j // 2][blk], ka[j][band], mask, sink_ref[0, j])
                pswa_ref[blk, 2 * BLOCK * j:2 * BLOCK * (j + 1)] = p.astype(pswa_ref.dtype)
                psink = jnp.where(lane128 == j, pk, psink)
            psink_ref[blk, :] = psink
            for h in range(2):
                yb_ref[blk, KV_W * h:KV_W * (h + 1)] = _mm(
                    pswa_ref[blk, 4 * BLOCK * h:4 * BLOCK * (h + 1)],
                    jnp.concatenate([va[2 * h][band], va[2 * h + 1][band]], axis=0))

        gm256 = _group_matrix(XATT_W)
        xq = proj_ref[:, C_XQ:C_XQ + XATT_W]
        qx = xq * lax.rsqrt(_seg_mean(xq * xq, gm256) + EPS) * xqg_ref[...]
        pm = _mem_probs(_mm_nt(qx, km_ref[...]))
        for j in range(4):
            pmem_ref[:, MEM_LEN * j:MEM_LEN * (j + 1)] = pm[j].astype(pmem_ref.dtype)
        yc = _mm(pmem_ref[...], vm_ref[...])
        yc_ref[...] = yc

        def gated(y, g, gate):
            return y * lax.rsqrt(_row_mean(y * y) + EPS) * g * (gate * _sigmoid(gate))

        ogv = og_ref[...]
        za = gated(ya_ref[...], ogv[:, :512], proj_ref[:, C_LRUG:C_LRUG + LRU_W])
        zb = gated(yb_ref[...], ogv[:, 512:768], proj_ref[:, C_SWAG:C_SWAG + SWA_W])
        zc = gated(yc, ogv[:, 768:], proj_ref[:, C_XG:C_XG + XATT_W])
        ycat_ref[:, 0:512] = za.astype(ycat_ref.dtype)
        ycat_ref[:, 512:768] = zb.astype(ycat_ref.dtype)
        ycat_ref[:, 768:1024] = zc.astype(ycat_ref.dtype)
        out = xv + _mm(ycat_ref[...], wout_ref[...])
        err = out - t_ref[...]
        dout_ref[...] = (err * (1.0 / D_MODEL)).astype(dout_ref.dtype)
        lacc_ref[...] = lacc_ref[...] + (0.5 / D_MODEL) * jnp.sum(err * err)

        @pl.when(i == nt - 1)
        def _():
            loss_ref[...] = lacc_ref[...]

    def rows(ncol):
        return pl.BlockSpec((tm, ncol), lambda i: (i, 0))

    in_specs = [rows(D_MODEL), rows(D_MODEL), rows(128), rows(128), rows(128),
                _const_spec((1, D_MODEL)), _const_spec((D_IN, D_MODEL), True), _const_spec((CONV_K, LRU_W)),
                _const_spec((1, LRU_W)), _const_spec((2, 256, 512), True), _const_spec((1, LRU_W)),
                _const_spec((1, LRU_W)), _const_spec((1, LRU_W)), _const_spec((1, 128)), _const_spec((1, 128)),
                _const_spec((1, XATT_W)), pl.BlockSpec(memory_space=pltpu.SMEM),
                _const_spec((4 * MEM_LEN, XATT_W), True), _const_spec((4 * MEM_LEN, XATT_W), True),
                _const_spec((1, D_MODEL)), _const_spec((D_MODEL, D_MODEL), True)]
    out_shape = (jax.ShapeDtypeStruct((seq, D_IN), F32), jax.ShapeDtypeStruct((seq, LRU_W), F32),
                 jax.ShapeDtypeStruct((seq, SWA_W), F32), jax.ShapeDtypeStruct((seq, XATT_W), F32),
                 jax.ShapeDtypeStruct((seq, D_MODEL), _MXU), jax.ShapeDtypeStruct((seq, D_MODEL), _MXU),
                 jax.ShapeDtypeStruct((seq, D_MODEL), _MXU), jax.ShapeDtypeStruct((seq, 4 * 2 * BLOCK), _MXU),
                 jax.ShapeDtypeStruct((seq, 4 * MEM_LEN), _MXU), jax.ShapeDtypeStruct((seq, 128), F32),
                 jax.ShapeDtypeStruct((seq, 4 * LRU_W), _MXU), jax.ShapeDtypeStruct((seq, LRU_W), F32),
                 jax.ShapeDtypeStruct((8, 128), F32))
    out_specs = (rows(D_IN), rows(LRU_W), rows(SWA_W), rows(XATT_W), rows(D_MODEL), rows(D_MODEL), rows(D_MODEL),
                 rows(4 * 2 * BLOCK), rows(4 * MEM_LEN), rows(128), rows(4 * LRU_W), rows(LRU_W),
                 _const_spec((8, 128)))
    scratch = [pltpu.VMEM((tm + 8, LRU_W), F32), pltpu.VMEM((tm, LRU_W), F32),
               pltpu.VMEM((8, LRU_W), F32), pltpu.VMEM((BLOCK, KV_W), F32), pltpu.VMEM((BLOCK, KV_W), F32),
               pltpu.VMEM((8, 128), F32)]
    return pl.pallas_call(
        body, name="layer_fwd", grid=(nt,), out_shape=out_shape, in_specs=in_specs, out_specs=out_specs,
        scratch_shapes=scratch,
        compiler_params=pltpu.CompilerParams(dimension_semantics=("arbitrary",), vmem_limit_bytes=VMEM_LIMIT),
    )(x, tgt, rc, rs1, rs2, ng, win_t, cw, cb, wg, brg, big, lam, qg, kg, xqg, sinks, km, vm, og, wout)


def wgrad_reduce(lhs, rhs, bigs, g_small, stage_at, name):
    seq, ncol = rhs.shape
    nblk = lhs.shape[1] // 256
    nres = len(bigs) + (g_small is not None)

    def body(l_ref, r_ref, *refs):
        if nres:
            _hosted_reduce(pl.program_id(0), stage_at, refs[:nres], refs[nres + 1:2 * nres + 1], refs[2 * nres + 1:],
                           g_small is not None)
        refs[nres][...] = _mm_tn(l_ref[...], r_ref[...])

    red_shape, scratch = _hosted_reduce_shapes(bigs, g_small) if nres else ([], [])
    vm = pl.BlockSpec(memory_space=pltpu.VMEM)
    hbm = pl.BlockSpec(memory_space=pl.ANY)
    operands = list(bigs) + ([] if g_small is None else [g_small])
    return pl.pallas_call(
        body, name=name, grid=(nblk,),
        out_shape=(jax.ShapeDtypeStruct((lhs.shape[1], ncol), F32), *red_shape),
        in_specs=[pl.BlockSpec((seq, 256), lambda j: (0, j)), _const_spec((seq, ncol), True)] + [hbm] * len(bigs)
        + [vm] * (g_small is not None),
        out_specs=(pl.BlockSpec((256, ncol), lambda j: (j, 0)),) + (hbm,) * len(red_shape),
        scratch_shapes=scratch,
        compiler_params=pltpu.CompilerParams(dimension_semantics=("arbitrary",), vmem_limit_bytes=VMEM_LIMIT),
    )(lhs, rhs, *operands)


def layer_bwd(x, dout, proj, ya, yb, yc, pswa, pmem, psink, gates, a_all, rc, rs1, rs2, ng, win_t, cw, wg, lam, qg, kg,
              xqg, km, vm, og, wout):
    seq = x.shape[0]
    tm = min(ROW_TILE, seq)
    nt = seq // tm
    nb = tm // BLOCK

    def body(x_ref, dout_ref, proj_ref, ya_ref, yb_ref, yc_ref, pswa_ref, pmem_ref, psink_ref, gates_ref, a_ref,
             c_ref, s1_ref, s2_ref,
             yah_ref, kvh_ref, ch_ref, s1h_ref, s2h_ref,
             ng_ref, win_ref, cw_ref, wg_ref, lam_ref, qg_ref, kg_ref, xqg_ref, km_ref, vm_ref, og_ref, wout_ref,
             gx_ref, dproj_ref, gwg_ref, dkm_ref, dvm_ref, gng_ref, gog_ref, gcb_ref, gbrg_ref, gbig_ref, glam_ref,
             gcw_ref, gqn_ref, gkn_ref, gxqn_ref, gsink_ref,
             hext_ref, aext_ref, an_scr, dh_scr, g_scr, dxc_ext, gcar_ref, dkcar_ref, dvcar_ref):
        i = pl.program_id(0)
        tile = nt - 1 - i
        first_tile = tile == 0

        @pl.when(i == 0)
        def _():
            for r in (gwg_ref, dkm_ref, dvm_ref, gng_ref, gog_ref, gcb_ref, gbrg_ref, gbig_ref, glam_ref, gcw_ref,
                      gqn_ref, gkn_ref, gxqn_ref, gsink_ref, gcar_ref, dkcar_ref, dvcar_ref):
                r[...] = jnp.zeros_like(r)
            dxc_ext[tm:tm + 8, :] = jnp.zeros((8, LRU_W), F32)
            aext_ref[tm:tm + 8, :] = jnp.zeros((8, LRU_W), F32)

        xv = x_ref[...]
        dov = dout_ref[...]
        dz = _mm_nt(dov, wout_ref[...])
        ogv = og_ref[...]

        def group_bwd(y, gate, g, dzg):
            r = lax.rsqrt(_row_mean(y * y) + EPS)
            n = y * r
            sg = _sigmoid(gate)
            dgate = dzg * (n * g) * (sg * (1.0 + gate * (1.0 - sg)))
            dng = dzg * (gate * sg)
            dn = dng * g
            return r * (dn - n * _row_mean(dn * n)), dgate, _col_sum(dng * n)

        dya, dga, goa = group_bwd(ya_ref[...], proj_ref[:, C_LRUG:C_LRUG + LRU_W], ogv[:, :512], dz[:, :512])
        dyb, dgb, gob = group_bwd(yb_ref[...], proj_ref[:, C_SWAG:C_SWAG + SWA_W], ogv[:, 512:768], dz[:, 512:768])
        dyc, dgc, goc = group_bwd(yc_ref[...], proj_ref[:, C_XG:C_XG + XATT_W], ogv[:, 768:], dz[:, 768:])
        gog_ref[...] += jnp.concatenate([goa, gob, goc], axis=1)
        dproj_ref[:, C_LRUG:C_LRUG + LRU_W] = dga.astype(dproj_ref.dtype)
        dproj_ref[:, C_SWAG:C_SWAG + SWA_W] = dgb.astype(dproj_ref.dtype)
        dproj_ref[:, C_XG:C_XG + XATT_W] = dgc.astype(dproj_ref.dtype)

        gm256 = _group_matrix(XATT_W)
        xq = proj_ref[:, C_XQ:C_XQ + XATT_W]
        rq = lax.rsqrt(_seg_mean(xq * xq, gm256) + EPS)
        qn = xq * rq
        qx = qn * xqg_ref[...]
        qxb = qx.astype(_MXU)
        dycb = dyc.astype(_MXU)
        dp_all = _mm_nt(dycb, vm_ref[...])
        dsm = []
        for j in range(4):
            pj = pmem_ref[:, MEM_LEN * j:MEM_LEN * (j + 1)].astype(F32)
            dp = dp_all[:, MEM_LEN * j:MEM_LEN * (j + 1)]
            dsm.append((pj * (dp - jnp.sum(pj * dp, axis=-1, keepdims=True))).astype(_MXU))
        ds_all = jnp.concatenate(dsm, axis=1)
        dvm_ref[...] += _mm_tn(dycb, pmem_ref[...])
        dkm_ref[...] += _mm_tn(qxb, ds_all)
        dqx = _mm(ds_all, km_ref[...])
        gxqn_ref[...] += _col_sum(dqx * qn)
        dqn = dqx * xqg_ref[...]
        dproj_ref[:, C_XQ:C_XQ + XATT_W] = (rq * (dqn - qn * _seg_mean(dqn * qn, gm256))).astype(dproj_ref.dtype)

        gm128 = _group_matrix(KV_W)
        cv, s1v, s2v = c_ref[...], s1_ref[...], s2_ref[...]

        def head_norm(t):
            r = lax.rsqrt(_seg_mean(t * t, gm128) + EPS)
            return t * r, r

        qn_, qr_ = zip(head_norm(proj_ref[:, C_SQ:C_SQ + 128]), head_norm(proj_ref[:, C_SQ + 128:C_SQ + 256]))
        qrope = [_rope(qn_[h] * qg_ref[...], cv, s1v, s2v).astype(_MXU) for h in range(2)]
        kn, krr = head_norm(proj_ref[:, C_SK:C_SK + KV_W])
        kr = _rope(kn * kg_ref[...], cv, s1v, s2v)
        khn, _ = head_norm(kvh_ref[:, 0:KV_W])
        khr = _rope(khn * kg_ref[...], ch_ref[...], s1h_ref[...], s2h_ref[...])
        ka = _place_kv(jnp.concatenate([khr, kr], axis=0), 0.125)
        va = _place_kv(jnp.concatenate([kvh_ref[:, KV_W:2 * KV_W], proj_ref[:, C_SV:C_SV + KV_W]], axis=0), 1.0)
        lane128 = lax.broadcasted_iota(jnp.int32, (1, 128), 1)
        gsink = jnp.zeros((1, 128), F32)
        dk_band, dv_band, dq_blk = [], [], []
        for b in range(nb):
            band = slice(BLOCK * b, BLOCK * b + 2 * BLOCK)
            blk = slice(BLOCK * b, BLOCK * (b + 1))
            dka, dva, dsb = [], [], []
            deltas = jnp.zeros((BLOCK, 128), F32)
            for j in range(4):
                qh = qrope[j // 2][blk]
                doh = dyb[blk, KV_W * (j // 2):KV_W * (j // 2 + 1)].astype(_MXU)
                pb = pswa_ref[blk, 2 * BLOCK * j:2 * BLOCK * (j + 1)]
                p = pb.astype(F32)
                dp = _mm_nt(doh, va[j][band])
                delta = jnp.sum(p * dp, axis=-1, keepdims=True)
                ds = (p * (dp - delta)).astype(_MXU)
                deltas = jnp.where(lane128 == j, delta, deltas)
                dva.append(_mm_tn(pb, doh))
                dka.append(_mm_tn(ds, qh))
                dsb.append(ds)
            gsink = gsink - _col_sum(psink_ref[blk, :] * deltas)
            dk_band.append(_unplace_kv(dka) * 0.125)
            dv_band.append(_unplace_kv(dva))
            dq_blk.append([_mm(jnp.concatenate(dsb[2 * h:2 * h + 2], axis=1),
                               jnp.concatenate([ka[2 * h][band], ka[2 * h + 1][band]], axis=0)) for h in range(2)])
        gsink_ref[...] += gsink
        dk_rows = [dk_band[b][BLOCK:] + (dk_band[b + 1][:BLOCK] if b + 1 < nb else dkcar_ref[...]) for b in range(nb)]
        dv_rows = [dv_band[b][BLOCK:] + (dv_band[b + 1][:BLOCK] if b + 1 < nb else dvcar_ref[...]) for b in range(nb)]
        dkcar_ref[...] = dk_band[0][:BLOCK]
        dvcar_ref[...] = dv_band[0][:BLOCK]
        dkg = _rope_bwd(jnp.concatenate(dk_rows, axis=0), cv, s1v, s2v)
        gkn = _col_sum(dkg * kn)
        dkn = dkg * kg_ref[...]
        dproj_ref[:, C_SK:C_SK + KV_W] = (krr * (dkn - kn * _seg_mean(dkn * kn, gm128))).astype(dproj_ref.dtype)
        dproj_ref[:, C_SV:C_SV + KV_W] = jnp.concatenate(dv_rows, axis=0).astype(dproj_ref.dtype)
        gqn = jnp.zeros((1, 128), F32)
        for h in range(2):
            dqg = _rope_bwd(jnp.concatenate([dq_blk[b][h] for b in range(nb)], axis=0), cv, s1v, s2v)
            gqn = gqn + _col_sum(dqg * qn_[h])
            dqn_ = dqg * qg_ref[...]
            dproj_ref[:, C_SQ + 128 * h:C_SQ + 128 * (h + 1)] = (
                qr_[h] * (dqn_ - qn_[h] * _seg_mean(dqn_ * qn_[h], gm128))).astype(dproj_ref.dtype)
        gqn_ref[...] += gqn
        gkn_ref[...] += gkn

        u = proj_ref[:, C_LRUX:C_LRUX + LRU_W]
        xc, rg, ig, sq = (gates_ref[:, LRU_W * k:LRU_W * (k + 1)].astype(F32) for k in range(4))
        a = a_ref[...]
        sp = _softplus(-lam_ref[...])
        hext_ref[0:8, :] = jnp.where(first_tile, 0.0, yah_ref[...])
        hext_ref[8:8 + tm, :] = ya_ref[...]
        hprev = hext_ref[pl.ds(7, tm), :]
        aext_ref[0:tm, :] = a
        an_scr[...] = aext_ref[pl.ds(1, tm), :]
        dh_scr[...] = dya
        dh_scr[tm - 1:tm, :] = dh_scr[tm - 1:tm, :] + gcar_ref[0:1, :]
        row8 = lax.broadcasted_iota(jnp.int32, (8, LRU_W), 0)

        def scan_step(gi, carry):
            r0 = pl.multiple_of((tm // 8 - 1 - gi) * 8, 8)
            av = an_scr[pl.ds(r0, 8), :]
            bv = dh_scr[pl.ds(r0, 8), :]
            for d in (1, 2, 4):
                a_sh = jnp.where(row8 < 8 - d, pltpu.roll(av, 8 - d, 0), 1.0)
                b_sh = jnp.where(row8 < 8 - d, pltpu.roll(bv, 8 - d, 0), 0.0)
                bv = bv + av * b_sh
                av = av * a_sh
            gv = bv + av * carry
            g_scr[pl.ds(r0, 8), :] = gv
            return gv[0:1, :]

        g0 = lax.fori_loop(0, tm // 8, scan_step, jnp.zeros((1, LRU_W), F32), unroll=True)
        gcar_ref[0:1, :] = a[0:1, :] * g0
        gv = g_scr[...]
        da = gv * hprev
        dig = gv * sq * xc
        dxc = gv * sq * ig
        dla = da * a - gv * (ig * xc) * ((a * a) / sq)
        drg = dla * ((-LRU_C) * sp)
        glam_ref[...] += _col_sum(dla * rg)
        dpr = drg * rg * (1.0 - rg)
        dpi = dig * ig * (1.0 - ig)
        gbrg_ref[...] += _col_sum(dpr)
        gbig_ref[...] += _col_sum(dpi)
        dpre0 = jnp.concatenate([dpr[:, :256], dpi[:, :256]], axis=1).astype(_MXU)
        dpre1 = jnp.concatenate([dpr[:, 256:], dpi[:, 256:]], axis=1).astype(_MXU)
        gwg_ref[0] += _mm_tn(xc[:, :256], dpre0)
        gwg_ref[1] += _mm_tn(xc[:, 256:], dpre1)
        dxc = dxc + jnp.concatenate([_mm_nt(dpre0, wg_ref[0]), _mm_nt(dpre1, wg_ref[1])], axis=1)
        gcb_ref[...] += _col_sum(dxc)
        dxc_ext[0:tm, :] = dxc
        du = jnp.zeros((tm, LRU_W), F32)
        for k in range(CONV_K):
            later = dxc_ext[pl.ds(3 - k, tm), :]
            gcw_ref[k:k + 1, :] += _col_sum(later * u)
            du = du + cw_ref[k:k + 1, :] * later
        dxc_ext[tm:tm + 8, :] = dxc[0:8, :]
        dproj_ref[:, C_LRUX:C_LRUX + LRU_W] = du.astype(dproj_ref.dtype)

        dxn = _mm(dproj_ref[...], win_ref[...])
        rx = lax.rsqrt(_row_mean(xv * xv) + EPS)
        xh = xv * rx
        gng_ref[...] += _col_sum(dxn * xh)
        dxh = dxn * ng_ref[...]
        gx_ref[...] = dov.astype(F32) + rx * (dxh - xh * _row_mean(dxh * xh))

        @pl.when(i == nt - 1)
        def _():
            glam_ref[...] = glam_ref[...] * (LRU_C * _sigmoid(-lam_ref[...]))

    def rows(ncol, arr_cols_block=0):
        return pl.BlockSpec((tm, ncol), lambda i: (nt - 1 - i, arr_cols_block))

    def halo(nrow, ncol, colblk=0):
        per = tm // nrow
        return pl.BlockSpec((nrow, ncol), lambda i: (jnp.maximum((nt - 1 - i) * per - 1, 0), colblk))

    in_specs = [rows(D_MODEL), rows(D_MODEL), rows(D_IN), rows(LRU_W), rows(SWA_W), rows(XATT_W),
                rows(4 * 2 * BLOCK), rows(4 * MEM_LEN), rows(128), rows(4 * LRU_W), rows(LRU_W),
                rows(128), rows(128), rows(128),
                halo(8, LRU_W), halo(BLOCK, 2 * KV_W, C_SK // (2 * KV_W)),
                halo(BLOCK, 128), halo(BLOCK, 128), halo(BLOCK, 128),
                _const_spec((1, D_MODEL)), _const_spec((D_IN, D_MODEL), True), _const_spec((CONV_K, LRU_W)),
                _const_spec((2, 256, 512), True), _const_spec((1, LRU_W)), _const_spec((1, 128)), _const_spec((1, 128)),
                _const_spec((1, XATT_W)),
                _const_spec((4 * MEM_LEN, XATT_W), True), _const_spec((4 * MEM_LEN, XATT_W), True),
                _const_spec((1, D_MODEL)), _const_spec((D_MODEL, D_MODEL), True)]
    small = [(2, 256, 512), (XATT_W, 4 * MEM_LEN), (XATT_W, 4 * MEM_LEN), (1, D_MODEL), (1, D_MODEL), (1, LRU_W), (1, LRU_W),
             (1, LRU_W), (1, LRU_W), (CONV_K, LRU_W), (1, 128), (1, 128), (1, XATT_W), (1, 128)]
    out_shape = (jax.ShapeDtypeStruct((seq, D_MODEL), F32), jax.ShapeDtypeStruct((seq, D_IN), _MXU)) + tuple(
        jax.ShapeDtypeStruct(s, F32) for s in small)
    out_specs = (rows(D_MODEL), rows(D_IN)) + tuple(_const_spec(s) for s in small)
    scratch = [pltpu.VMEM((tm + 8, LRU_W), F32), pltpu.VMEM((tm + 8, LRU_W), F32),
               pltpu.VMEM((tm, LRU_W), F32), pltpu.VMEM((tm, LRU_W), F32), pltpu.VMEM((tm, LRU_W), F32),
               pltpu.VMEM((tm + 8, LRU_W), F32),
               pltpu.VMEM((8, LRU_W), F32), pltpu.VMEM((BLOCK, KV_W), F32), pltpu.VMEM((BLOCK, KV_W), F32)]
    return pl.pallas_call(
        body, name="layer_bwd", grid=(nt,), out_shape=out_shape, in_specs=in_specs, out_specs=out_specs,
        scratch_shapes=scratch,
        compiler_params=pltpu.CompilerParams(dimension_semantics=("arbitrary",), vmem_limit_bytes=VMEM_LIMIT),
    )(x, dout, proj, ya, yb, yc, pswa, pmem, psink, gates, a_all, rc, rs1, rs2, ya, proj, rc, rs1, rs2,
      ng, win_t, cw, wg, lam, qg, kg, xqg, km, vm, og, wout)


def _reduce_protocol(big, sm, outs, osm, r1, r1s, wire, r2, r2s, wire2, ps, own, send, recv, lsem):
    nbig = len(big)
    x, y, c = lax.axis_index("x"), lax.axis_index("y"), lax.axis_index("c")
    sibling = (x, y, 1 - c)
    near, far, diag = _partners(x, y, c)
    me, near_id, far_id, diag_id = _chip_of(x, y), _chip_of(*near), _chip_of(*far), _chip_of(*diag)

    def copy(k, src, dst, to):
        return pltpu.make_async_remote_copy(src_ref=src, dst_ref=dst, send_sem=send.at[k], recv_sem=recv.at[k],
                                            device_id=to, device_id_type=MESH)

    def sent(stage, a):
        if a == nbig:
            src, dst, to = ((sm.at[1 - c], r1s, sibling), (r1s, r2s.at[0], (*near, c)), (ps, r2s.at[1], (*far, c)),
                            (osm.at[c], osm.at[c], sibling))[stage]
            return [copy(5 * nbig + stage, src, dst, to)]
        if stage == 0:
            return [copy(5 * a, big[a].at[:, 1 - c], r1[a], sibling)]
        if stage == 1:
            return [copy(5 * a + 1, wire[a].at[near_id], r2[a].at[0], (*near, c)),
                    copy(5 * a + 2, wire[a].at[diag_id], r2[a].at[1], (*near, c))]
        if stage == 2:
            return [copy(5 * a + 3, wire2[a], r2[a].at[2], (*far, c))]
        return [copy(5 * a + 4, outs[a].at[c], outs[a].at[c], sibling)]

    arrays = range(nbig + (sm is not None))

    def start(stage, a):
        for cp in sent(stage, a):
            cp.start()

    def arrived(k, ref):
        copy(k, ref, ref, sibling).wait_recv()

    def loads():
        return [pltpu.make_async_copy(big[a].at[:, c], own[a], lsem.at[a]) for a in range(nbig)]

    def stage0():
        for a in arrays:
            start(0, a)
        for cp in loads():
            cp.start()

    def stage1():
        for a in range(nbig):
            loads()[a].wait()
            arrived(5 * a, r1[a])
            for k in range(N_CHIPS):
                r1[a][k] = own[a][k] + r1[a][k]
                wire[a][k] = r1[a][k].astype(wire[a].dtype)
            start(1, a)
        if sm is not None:
            arrived(5 * nbig, r1s)
            r1s[...] = sm[c] + r1s[...]
            start(1, nbig)

    def stage2():
        for a in range(nbig):
            arrived(5 * a + 1, r2[a].at[0])
            arrived(5 * a + 2, r2[a].at[1])
            r1[a][me] = r1[a][me] + r2[a][0].astype(F32)
            wire2[a][...] = (r1[a][far_id] + r2[a][1].astype(F32)).astype(wire2[a].dtype)
            start(2, a)
        if sm is not None:
            arrived(5 * nbig + 1, r2s.at[0])
            ps[...] = r1s[...] + r2s[0]
            start(2, nbig)

    def stage3():
        for a in range(nbig):
            arrived(5 * a + 3, r2[a].at[2])
            outs[a][c] = r1[a][me] + r2[a][2].astype(F32)
            start(3, a)
        if sm is not None:
            arrived(5 * nbig + 2, r2s.at[1])
            osm[c] = ps[...] + r2s[1]
            start(3, nbig)

    def stage4():
        for a in range(nbig):
            arrived(5 * a + 4, outs[a].at[1 - c])
        if sm is not None:
            arrived(5 * nbig + 3, osm.at[1 - c])
        for stage in range(4):
            for a in arrays:
                for cp in sent(stage, a):
                    cp.wait_send()

    return [stage0, stage1, stage2, stage3, stage4]


def _reduce_buffers(bigs, g_small):
    half = [b.shape[2:] for b in bigs]
    sm_half = None if g_small is None else g_small.shape[1:]
    out_shape = [jax.ShapeDtypeStruct((2,) + h, F32) for h in half]
    small = lambda lead: [] if g_small is None else [pltpu.VMEM(lead + sm_half, F32)]
    if g_small is not None:
        out_shape.append(jax.ShapeDtypeStruct(g_small.shape, F32))
    n_sem = 5 * len(bigs) + 4
    scratch = ([pltpu.VMEM((N_CHIPS,) + h, F32) for h in half] + small(())
               + [pltpu.VMEM((N_CHIPS,) + h, _WIRE) for h in half]
               + [pltpu.VMEM((3,) + h, _WIRE) for h in half] + small((2,))
               + [pltpu.VMEM(h, _WIRE) for h in half] + small(())
               + [pltpu.VMEM((N_CHIPS,) + h, F32) for h in half]
               + [pltpu.SemaphoreType.DMA((n_sem,)), pltpu.SemaphoreType.DMA((n_sem,)),
                  pltpu.SemaphoreType.DMA((len(bigs),))])
    return out_shape, scratch


def _split_reduce_refs(refs, nbig, has_small):
    it = iter(refs)
    take = lambda n: [next(it) for _ in range(n)]
    one = lambda: next(it) if has_small else None
    big, sm = take(nbig), one()
    outs, osm = take(nbig), one()
    r1, r1s, wire, r2, r2s, wire2, ps, own = take(nbig), one(), take(nbig), take(nbig), one(), take(nbig), one(), take(nbig)
    send, recv, lsem = take(3)
    return big, sm, outs, osm, r1, r1s, wire, r2, r2s, wire2, ps, own, send, recv, lsem


def _hosted_reduce_shapes(bigs, g_small):
    red_shape, scratch = _reduce_buffers(bigs, g_small)
    nres = len(red_shape)
    return red_shape, [pltpu.VMEM(r.shape, r.dtype) for r in red_shape] + scratch + [pltpu.SemaphoreType.DMA((nres,))]


def _hosted_reduce(step, stage_at, operands, results, scratch, has_small):
    nres = len(results)
    sums, rest, fsem = scratch[:nres], scratch[nres:-1], scratch[-1]
    refs = tuple(operands) + tuple(sums) + tuple(rest)
    for at, stage in zip(stage_at, _reduce_protocol(*_split_reduce_refs(refs, nres - has_small, has_small))):
        pl.when(step == at)(stage)

    @pl.when(step == stage_at[-1])
    def _():
        out = [pltpu.make_async_copy(sums[k], results[k], fsem.at[k]) for k in range(nres)]
        for cp in out:
            cp.start()
        for cp in out:
            cp.wait()


def reduce_grads(big, name, parts):
    chips, halves, rows_, cols = big.shape
    sub = jax.ShapeDtypeStruct((chips, halves, rows_ // parts, cols), big.dtype)

    def body(b_ref, o_ref, *scratch):
        refs = [b_ref.at[:, :, s] for s in range(parts)] + [o_ref.at[:, s] for s in range(parts)] + list(scratch)
        for stage in _reduce_protocol(*_split_reduce_refs(refs, parts, False)):
            stage()

    _, scratch = _reduce_buffers([sub] * parts, None)
    return pl.pallas_call(
        body, name=name, out_shape=jax.ShapeDtypeStruct((halves, parts, rows_ // parts, cols), F32),
        in_specs=[pl.BlockSpec(memory_space=pl.ANY)], out_specs=pl.BlockSpec(memory_space=pltpu.VMEM),
        scratch_shapes=scratch, compiler_params=pltpu.CompilerParams(vmem_limit_bytes=VMEM_LIMIT),
    )(big.reshape(chips, halves, parts, rows_ // parts, cols))


def adamw(w, g, m, v, name):
    rows_, cols = w.shape
    tr = max(t for t in range(8, rows_ + 1, 8) if rows_ % t == 0 and t * cols * 4 <= ADAM_BLOCK_BYTES)

    def body(w_ref, g_ref, m_ref, v_ref, d_ref, nm_ref, nv_ref):
        d_ref[...], nm_ref[...], nv_ref[...] = _adam_update(w_ref[...], g_ref[...], m_ref[...], v_ref[...])

    spec = pl.BlockSpec((tr, cols), lambda i: (i, 0))
    shp = jax.ShapeDtypeStruct(w.shape, F32)
    return pl.pallas_call(
        body, name=name, grid=(rows_ // tr,), out_shape=(shp, shp, shp), in_specs=[spec] * 4, out_specs=(spec,) * 3,
        compiler_params=pltpu.CompilerParams(dimension_semantics=("arbitrary",)),
    )(w, g, m, v)


def _adam_update(w, g, m, v):
    nm = ADAM_B1 * m + (1.0 - ADAM_B1) * g
    nv = ADAM_B2 * v + (1.0 - ADAM_B2) * (g * g)
    m_hat = nm / (1.0 - ADAM_B1 ** ADAM_STEP)
    v_hat = nv / (1.0 - ADAM_B2 ** ADAM_STEP)
    return (-ADAM_LR) * (m_hat / (jnp.sqrt(v_hat) + ADAM_EPS) + ADAM_WD * w), nm, nv


def adamw_vectors(g_pack, g_mats, ws, ms, vs):
    nvec, nmat = len(SMALL_VECTORS), len(g_mats)
    n = nvec + nmat

    def body(*refs):
        pk = refs[0]
        gm_refs = refs[1:1 + nmat]
        w_refs, m_refs, v_refs = (refs[1 + nmat + k * n:1 + nmat + (k + 1) * n] for k in range(3))
        outs = refs[1 + nmat + 3 * n:]
        g_out, d_out, nm_out, nv_out = outs[:nvec], outs[nvec:nvec + n], outs[nvec + n:nvec + 2 * n], outs[nvec + 2 * n:]
        chip = 2 * lax.axis_index("x") + lax.axis_index("y")
        for k, (name, row, width) in enumerate(SMALL_VECTORS):
            if name == "conv_w":
                g = jnp.concatenate([pk[pl.ds(row + 4 * t + chip, 1), :] for t in range(CONV_K)], axis=0)[None]
            elif width >= 128:
                g = jnp.concatenate([pk[row + r:row + r + 1, :] for r in range(width // 128)], axis=1)
            else:
                g = pk[row:row + 1, 0:width]
            g_out[k][...] = g
            d_out[k][...], nm_out[k][...], nv_out[k][...] = _adam_update(w_refs[k][...], g, m_refs[k][...], v_refs[k][...])
        for k in range(nvec, n):
            d_out[k][...], nm_out[k][...], nv_out[k][...] = _adam_update(
                w_refs[k][...], gm_refs[k - nvec][...], m_refs[k][...], v_refs[k][...])

    vm = pl.BlockSpec(memory_space=pltpu.VMEM)
    like = [jax.ShapeDtypeStruct(w.shape, F32) for w in ws]
    out_shape = like[:nvec] + like * 3
    return pl.pallas_call(
        body, name="adamw_vectors", out_shape=tuple(out_shape), in_specs=[vm] * (1 + nmat + 3 * n),
        out_specs=(vm,) * len(out_shape),
    )(g_pack, *g_mats, *ws, *ms, *vs)


SMALL_VECTORS = (("norm_g", 0, 1024), ("mem_norm_g", 8, 1024), ("conv_w", 16, 512), ("conv_b", 32, 512),
                 ("b_rg", 292, 512), ("b_ig", 552, 512), ("lru_lambda", 556, 512), ("q_norm_g", 560, 64),
                 ("k_norm_g", 561, 64), ("sinks", 562, 4), ("xq_norm_g", 563, 64), ("xk_norm_g", 564, 64),
                 ("out_norm_g", 565, 1024))
SMALL_MATRICES = (("w_rg", 36), ("w_ig", 296))
LOSS_ROW = 573
SMALL_ROWS = 576


def _pack(parts, rows_):
    flat = jnp.concatenate([p.reshape(-1) for p in parts])
    return jnp.pad(flat, (0, rows_ * 128 - flat.shape[0])).reshape(rows_, 128)


def _pad_to(v, n):
    v = v.reshape(-1)
    return jnp.pad(v, (0, n - v.shape[0]))


def _block_diag_gates(w_rg, w_ig):
    eye = jnp.eye(4, dtype=w_rg.dtype)

    def bd(w4):
        return (w4[:, :, None, :] * eye[:, None, :, None]).reshape(256, 256)

    return jnp.stack([jnp.concatenate([bd(w_rg[4 * h:4 * h + 4]), bd(w_ig[4 * h:4 * h + 4])], axis=1) for h in (0, 1)])


def _diag_blocks(g):
    g6 = g.reshape(2, 4, HEAD, 2, 4, HEAD)
    d = (g6 * jnp.eye(4, dtype=g.dtype)[None, :, None, None, :, None]).sum(axis=4)
    return d[:, :, :, 0].reshape(8, HEAD, HEAD), d[:, :, :, 1].reshape(8, HEAD, HEAD)


def _rope_tables(seq):
    pos = np.arange(seq, dtype=np.float32)
    inv_freq = (np.float32(ROPE_THETA) ** (-(np.arange(0, ROPE_DIM, 2, dtype=np.float32) / np.float32(ROPE_DIM)))
                ).astype(np.float32)
    ang = (pos[:, None] * inv_freq[None, :]).astype(np.float32)
    cos, sin = np.cos(ang).astype(np.float32), np.sin(ang).astype(np.float32)
    z = lambda n: np.zeros((seq, n), np.float32)
    c64 = np.concatenate([cos, cos, np.ones((seq, HEAD - ROPE_DIM), np.float32)], axis=1)
    s1_64 = np.concatenate([-sin, z(HEAD - 8)], axis=1)
    s2_64 = np.concatenate([z(8), sin, z(HEAD - ROPE_DIM)], axis=1)
    return tuple(jnp.asarray(np.concatenate([t, t], axis=1)) for t in (c64, s1_64, s2_64))


def kernel(x, mem, norm_g, mem_norm_g, w_in, conv_w, conv_b, w_rg, b_rg, w_ig, b_ig, lru_lambda, q_norm_g, k_norm_g, sinks, w_mem_kv, xq_norm_g, xk_norm_g, out_norm_g, w_out, loss_target, m_norm_g, m_mem_norm_g, m_w_in, m_conv_w, m_conv_b, m_w_rg, m_b_rg, m_w_ig, m_b_ig, m_lru_lambda, m_q_norm_g, m_k_norm_g, m_sinks, m_w_mem_kv, m_xq_norm_g, m_xk_norm_g, m_out_norm_g, m_w_out, v_norm_g, v_mem_norm_g, v_w_in, v_conv_w, v_conv_b, v_w_rg, v_b_rg, v_w_ig, v_b_ig, v_lru_lambda, v_q_norm_g, v_k_norm_g, v_sinks, v_w_mem_kv, v_xq_norm_g, v_xk_norm_g, v_out_norm_g, v_w_out):
    seq = x.shape[1]
    xs, tgt, mems = x[0], loss_target[0], mem[0]

    win_t_sh = w_in[0].T.astype(_MXU)
    cw_sh = jnp.pad(conv_w[0], ((0, 4), (0, 0)))
    win_t, wout, wkv, cw_all = gather_weights(win_t_sh, w_out[0].astype(_MXU), w_mem_kv[0].astype(_MXU), cw_sh)
    cw = cw_all.reshape(N_CHIPS, 8, 128)[:, :CONV_K].transpose(1, 0, 2).reshape(CONV_K, LRU_W)

    rc, rs1, rs2 = _rope_tables(seq)
    wg = _block_diag_gates(w_rg[0], w_ig[0]).astype(_MXU)
    qg = jnp.tile(q_norm_g, (1, 2))
    kg = jnp.tile(k_norm_g, (1, 2))
    xqg = jnp.tile(xq_norm_g, (1, 4))
    xkg = jnp.tile(xk_norm_g, (1, 4))

    km, vm = mem_fwd(mems, mem_norm_g, wkv, xkg)
    proj, ya, yb, yc, ycat, xn, dout, pswa, pmem, psink, gates, a_all, loss8 = layer_fwd(
        xs, tgt, rc, rs1, rs2, norm_g, win_t, cw, conv_b, wg, b_rg, b_ig, lru_lambda, qg, kg, xqg, sinks, km, vm,
        out_norm_g, wout)
    (g_wout,) = wgrad_reduce(ycat, dout, [], None, (), "wgrad_out")
    (gx, dproj, g_wg, dkm, dvm, g_ng, g_og, g_cb, g_brg, g_big, g_lam, g_cw, g_qn, g_kn, g_xqn, g_sink) = layer_bwd(
        xs, dout, proj, ya, yb, yc, pswa, pmem, psink, gates, a_all, rc, rs1, rs2, norm_g, win_t, cw, wg, lru_lambda, qg,
        kg, xqg, km, vm, out_norm_g, wout)
    g_wkv, g_mng, g_xkn = mem_bwd(mems, mem_norm_g, wkv, xkg, dkm, dvm)

    g_wrg, g_wig = _diag_blocks(g_wg)
    fold = lambda v, n: v.reshape(n, HEAD).sum(axis=0)
    small_g = _pack([g_ng, g_mng, g_cw, g_cb, g_wrg, g_brg, g_wig, g_big, g_lam, _pad_to(fold(g_qn, 2), 128),
                     _pad_to(fold(g_kn, 2), 128), g_sink, _pad_to(fold(g_xqn, 4), 128), _pad_to(fold(g_xkn, 4), 128),
                     g_og, loss8[0:1]], SMALL_ROWS)
    early = [g_wout.reshape(N_CHIPS, 2, D_MODEL // 8, D_MODEL), g_wkv.reshape(N_CHIPS, 2, D_MODEL // 8, 2 * XATT_W)]
    g_win_t, r_out, r_kv, r_small = wgrad_reduce(dproj, xn, early, small_g.reshape(2, SMALL_ROWS // 2, 128),
                                                 (0, 2, 5, 7, 8), "wgrad_in")
    r_in = reduce_grads(g_win_t.reshape(N_CHIPS, 2, D_IN // 8, D_MODEL), "reduce_w_in", 6)

    r_small = r_small.reshape(SMALL_ROWS, 128)
    loss = r_small[LOSS_ROW, 0]
    grads = {"w_in": r_in.reshape(D_IN // 4, D_MODEL).T[None], "w_mem_kv": r_kv.reshape(D_MODEL // 4, 2 * XATT_W)[None],
             "w_out": r_out.reshape(D_MODEL // 4, D_MODEL)[None]}
    for name, row in SMALL_MATRICES:
        grads[name] = r_small[row:row + 256].reshape(1, LRU_BLOCKS, HEAD, HEAD)
    weights = dict(norm_g=norm_g, mem_norm_g=mem_norm_g, w_in=w_in, conv_w=conv_w, conv_b=conv_b, w_rg=w_rg, b_rg=b_rg,
                   w_ig=w_ig, b_ig=b_ig, lru_lambda=lru_lambda, q_norm_g=q_norm_g, k_norm_g=k_norm_g, sinks=sinks,
                   w_mem_kv=w_mem_kv, xq_norm_g=xq_norm_g, xk_norm_g=xk_norm_g, out_norm_g=out_norm_g, w_out=w_out)
    ms = dict(norm_g=m_norm_g, mem_norm_g=m_mem_norm_g, w_in=m_w_in, conv_w=m_conv_w, conv_b=m_conv_b, w_rg=m_w_rg,
              b_rg=m_b_rg, w_ig=m_w_ig, b_ig=m_b_ig, lru_lambda=m_lru_lambda, q_norm_g=m_q_norm_g, k_norm_g=m_k_norm_g,
              sinks=m_sinks, w_mem_kv=m_w_mem_kv, xq_norm_g=m_xq_norm_g, xk_norm_g=m_xk_norm_g,
              out_norm_g=m_out_norm_g, w_out=m_w_out)
    vs = dict(norm_g=v_norm_g, mem_norm_g=v_mem_norm_g, w_in=v_w_in, conv_w=v_conv_w, conv_b=v_conv_b, w_rg=v_w_rg,
              b_rg=v_b_rg, w_ig=v_w_ig, b_ig=v_b_ig, lru_lambda=v_lru_lambda, q_norm_g=v_q_norm_g, k_norm_g=v_k_norm_g,
              sinks=v_sinks, w_mem_kv=v_w_mem_kv, xq_norm_g=v_xq_norm_g, xk_norm_g=v_xk_norm_g,
              out_norm_g=v_out_norm_g, w_out=v_w_out)

    delta, new_m, new_v = {}, {}, {}
    d2, m2, v2 = adamw(w_in[0].T, r_in.reshape(D_IN // 4, D_MODEL), m_w_in[0].T, v_w_in[0].T, "adamw_w_in")
    delta["w_in"], new_m["w_in"], new_v["w_in"] = d2.T[None], m2.T[None], v2.T[None]
    for name in ("w_mem_kv", "w_out"):
        shp = weights[name].shape
        d2, m2, v2 = adamw(weights[name][0], grads[name][0], ms[name][0], vs[name][0], "adamw_" + name)
        delta[name], new_m[name], new_v[name] = d2.reshape(shp), m2.reshape(shp), v2.reshape(shp)
    vec_names = [n for n, _, _ in SMALL_VECTORS]
    small_names = vec_names + [n for n, _ in SMALL_MATRICES]
    res = adamw_vectors(r_small, [grads[n] for n, _ in SMALL_MATRICES], [weights[n] for n in small_names],
                        [ms[n] for n in small_names], [vs[n] for n in small_names])
    nvec, nall = len(vec_names), len(small_names)
    grads.update(zip(vec_names, res[:nvec]))
    delta.update(zip(small_names, res[nvec:nvec + nall]))
    new_m.update(zip(small_names, res[nvec + nall:nvec + 2 * nall]))
    new_v.update(zip(small_names, res[nvec + 2 * nall:]))

    order = ("norm_g", "mem_norm_g", "w_in", "conv_w", "conv_b", "w_rg", "b_rg", "w_ig", "b_ig", "lru_lambda",
             "q_norm_g", "k_norm_g", "sinks", "w_mem_kv", "xq_norm_g", "xk_norm_g", "out_norm_g", "w_out")
    return (loss, gx[None], *[grads[n] for n in order], *[delta[n] for n in order], *[new_m[n] for n in order],
            *[new_v[n] for n in order])
```

```python
import jax
import jax.numpy as jnp
import numpy as np
from jax import lax
from jax.experimental import pallas as pl
from jax.experimental.pallas import tpu as pltpu

F32 = jnp.float32
_MXU = jnp.bfloat16
_WIRE = jnp.bfloat16

D_MODEL = 1024
MEM_LEN = 256
HEAD = 64
LRU_W = 512
LRU_BLOCKS = 8
CONV_K = 4
LRU_C = 8.0
SWA_W = 256
KV_W = 128
XATT_W = 256
BLOCK = 128
D_IN = 2304
ROPE_THETA = 500000.0
ROPE_DIM = 16
EPS = 1e-6
NEG_INF = -1e30
C_LRUX, C_LRUG, C_SQ, C_SK, C_SV, C_SWAG, C_XQ, C_XG = 0, 512, 1024, 1280, 1408, 1536, 1792, 2048

ADAM_LR, ADAM_B1, ADAM_B2, ADAM_EPS, ADAM_WD, ADAM_STEP = 0.001, 0.9, 0.999, 1e-08, 0.01, 10

N_CHIPS = 4
ROW_TILE = 256
VMEM_LIMIT = 56 * 1024 * 1024
ADAM_BLOCK_BYTES = 1280 * 1024
MESH = pl.DeviceIdType.MESH


def _mm(a, b):
    return jnp.dot(a.astype(_MXU), b.astype(_MXU), preferred_element_type=F32)


def _mm_nt(a, b):
    return lax.dot_general(a.astype(_MXU), b.astype(_MXU), (((1,), (1,)), ((), ())), preferred_element_type=F32)


def _mm_tn(a, b):
    return lax.dot_general(a.astype(_MXU), b.astype(_MXU), (((0,), (0,)), ((), ())), preferred_element_type=F32)


def _group_matrix(width):
    r = lax.shift_right_logical(lax.broadcasted_iota(jnp.int32, (width, width), 0), 6)
    c = lax.shift_right_logical(lax.broadcasted_iota(jnp.int32, (width, width), 1), 6)
    return (r == c).astype(_MXU)


def _seg_mean(x, gm):
    return jnp.dot(x.astype(_MXU), gm, preferred_element_type=F32) * (1.0 / HEAD)


def _row_mean(x):
    return jnp.mean(x, axis=-1, keepdims=True)


def _col_sum(x):
    return jnp.sum(x, axis=0, keepdims=True)


def _sigmoid(x):
    return jax.nn.sigmoid(x)


def _softplus(z):
    e = jnp.exp(-jnp.abs(z))
    u = 1.0 + e
    log1p_e = jnp.where(u == 1.0, e, jnp.log(u) * (e / (u - 1.0)))
    return jnp.maximum(z, 0.0) + log1p_e


def _rope(t, c, s1, s2):
    return t * c + pltpu.roll(t, 120, 1) * s1 + pltpu.roll(t, 8, 1) * s2


def _rope_bwd(d, c, s1, s2):
    return d * c + pltpu.roll(d * s1, 8, 1) + pltpu.roll(d * s2, 120, 1)


def _lane_mask(width, lo, hi):
    lane = lax.broadcasted_iota(jnp.int32, (1, width), 1)
    return ((lane >= lo) & (lane < hi)).astype(F32)


def _swa_mask(first_block):
    qi = lax.broadcasted_iota(jnp.int32, (BLOCK, 2 * BLOCK), 0)
    kj = lax.broadcasted_iota(jnp.int32, (BLOCK, 2 * BLOCK), 1)
    rel = qi + BLOCK - kj
    ok = (rel >= 0) & (rel < BLOCK)
    return ok & (jnp.logical_not(first_block) | (kj >= BLOCK))


def _place_kv(t, scale):
    lo = t * (_lane_mask(KV_W, 0, HEAD) * scale)
    hi = t * (_lane_mask(KV_W, HEAD, KV_W) * scale)
    return [a.astype(_MXU) for a in (lo, pltpu.roll(lo, HEAD, 1), pltpu.roll(hi, HEAD, 1), hi)]


def _unplace_kv(d):
    return (_lane_mask(KV_W, 0, HEAD) * (d[0] + pltpu.roll(d[1], HEAD, 1))
            + _lane_mask(KV_W, HEAD, KV_W) * (d[3] + pltpu.roll(d[2], HEAD, 1)))


def _swa_probs(qh, ka, mask, sink):
    s = _mm_nt(qh, ka)
    s = jnp.where(mask, s, NEG_INF)
    m = jnp.maximum(jnp.max(s, axis=-1, keepdims=True), sink)
    p = jnp.exp(s - m)
    esink = jnp.exp(sink - m)
    inv = 1.0 / (jnp.sum(p, axis=-1, keepdims=True) + esink)
    return p * inv, esink * inv


def _mem_probs(s_all):
    out = []
    for j in range(4):
        s = s_all[:, MEM_LEN * j:MEM_LEN * (j + 1)]
        p = jnp.exp(s - jnp.max(s, axis=-1, keepdims=True))
        out.append(p * (1.0 / jnp.sum(p, axis=-1, keepdims=True)))
    return out


def _head_rows(t, scale):
    return jnp.concatenate([t * (_lane_mask(XATT_W, HEAD * j, HEAD * (j + 1)) * scale) for j in range(4)], axis=0)


def _lru_gates(xc, wg_ref, brg, big, lam):
    p0 = _mm(xc[:, :256], wg_ref[0])
    p1 = _mm(xc[:, 256:], wg_ref[1])
    rg = _sigmoid(jnp.concatenate([p0[:, :256], p1[:, :256]], axis=1) + brg)
    ig = _sigmoid(jnp.concatenate([p0[:, 256:], p1[:, 256:]], axis=1) + big)
    sp = _softplus(-lam)
    la = (-LRU_C) * rg * sp
    a = jnp.exp(la)
    th = jnp.tanh(la)
    one_minus_a2 = (-2.0 * th) / (1.0 - th)
    return rg, ig, sp, a, jnp.sqrt(one_minus_a2)


def _const_spec(shape, single=False):
    zeros = (0,) * len(shape)
    if single:
        return pl.BlockSpec(shape, lambda i: zeros, pipeline_mode=pl.Buffered(1))
    return pl.BlockSpec(shape, lambda i: zeros)


def _chip_of(x, y):
    return 2 * x + y


def _partners(x, y, c):
    north = c == 1
    near = (jnp.where(north, 1 - x, x), jnp.where(north, y, 1 - y))
    far = (jnp.where(north, x, 1 - x), jnp.where(north, 1 - y, y))
    return near, far, (1 - x, 1 - y)


def gather_weights(win_t, wout, wkv, convw):
    arrs = (win_t, wout, wkv)
    n = len(arrs)
    pieces = [(a, 0, arr.shape[0] // 2) for a, arr in enumerate(arrs)]
    npc = len(pieces)

    def body(a0, a1, a2, cw, o0, o1, o2, ocw, s0, s1, s2, send, recv, lsem):
        ins, outs = (s0, s1, s2), (o0, o1, o2)
        for src, dst in zip((a0, a1, a2), ins):
            dst[...] = src[...].astype(dst.dtype)
        x, y, c = lax.axis_index("x"), lax.axis_index("y"), lax.axis_index("c")
        sibling = (x, y, 1 - c)
        near, far, diag = _partners(x, y, c)
        chips = [near, far, diag]
        me = _chip_of(x, y)

        def landed(p, chip, half):
            a, off, rows_ = pieces[p]
            r = ins[a].shape[0]
            return outs[a].at[pl.ds(pl.multiple_of(chip * r + half * (r // 2) + off, 16), rows_)]

        def mine(p):
            a, off, rows_ = pieces[p]
            return ins[a].at[pl.ds(pl.multiple_of(c * (ins[a].shape[0] // 2) + off, 16), rows_)]

        def copy(k, src, dst, to):
            return pltpu.make_async_remote_copy(src_ref=src, dst_ref=dst, send_sem=send.at[k], recv_sem=recv.at[k],
                                                device_id=to, device_id_type=MESH)

        def cw_rows(chip):
            return ocw.at[pl.ds(pl.multiple_of(chip * 8, 8), 8)]

        locals_ = []
        for a in range(n):
            r = ins[a].shape[0]
            locals_.append(pltpu.make_async_copy(ins[a], outs[a].at[pl.ds(pl.multiple_of(me * r, 16), r)], lsem.at[a]))
        locals_.append(pltpu.make_async_copy(cw, cw_rows(me), lsem.at[n]))
        for cp in locals_:
            cp.start()

        sent = []
        for p in range(npc):
            for j in range(2):
                sent.append(copy(p * 6 + j, mine(p), landed(p, me, c), (*chips[j], c)))
        for j, chip in enumerate(chips):
            sent.append(copy(npc * 6 + j, cw, cw_rows(me), (*chip, c)))
        for cp in sent:
            cp.start()
        for p in range(npc):
            for j in range(3):
                got = landed(p, _chip_of(*chips[j]), c)
                copy(p * 6 + j, got, got, sibling).wait_recv()
                if j == 0:
                    sent.append(copy(p * 6 + 2, got, got, (*far, c)))
                    sent[-1].start()
                sent.append(copy(p * 6 + 3 + j, got, got, sibling))
                sent[-1].start()
        for p in range(npc):
            for j in range(3):
                got = landed(p, _chip_of(*chips[(1, 0, 2)[j]]), 1 - c)
                copy(p * 6 + 3 + j, got, got, sibling).wait_recv()
        for j, chip in enumerate(chips):
            got = cw_rows(_chip_of(*chip))
            copy(npc * 6 + j, got, got, (*chip, c)).wait_recv()
        for cp in sent:
            cp.wait_send()
        for cp in locals_:
            cp.wait()

    vm = pl.BlockSpec(memory_space=pltpu.VMEM)
    out_shape = tuple(jax.ShapeDtypeStruct((N_CHIPS * a.shape[0],) + a.shape[1:], _MXU) for a in arrs) + (
        jax.ShapeDtypeStruct((N_CHIPS * 8, 128), F32),)
    n_rdma = npc * 6 + 3
    return pl.pallas_call(
        body, name="gather_weights", out_shape=out_shape,
        in_specs=[vm] * 4, out_specs=(pl.BlockSpec(memory_space=pl.ANY),) * n + (vm,),
        scratch_shapes=[pltpu.VMEM(a.shape, _MXU) for a in arrs] + [pltpu.SemaphoreType.DMA((n_rdma,)), pltpu.SemaphoreType.DMA((n_rdma,)),
                        pltpu.SemaphoreType.DMA((n + 1,))],
        compiler_params=pltpu.CompilerParams(vmem_limit_bytes=VMEM_LIMIT),
    )(win_t, wout, wkv, convw)


def mem_fwd(mem, mem_g, wkv, xk_g):
    def body(mem_ref, g_ref, w_ref, xk_ref, km_ref, vm_ref):
        mem_v = mem_ref[...]
        mn = mem_v * lax.rsqrt(_row_mean(mem_v * mem_v) + EPS) * g_ref[...]
        mkv = _mm(mn, w_ref[...])
        kpre = mkv[:, :XATT_W]
        gm = _group_matrix(XATT_W)
        km = kpre * lax.rsqrt(_seg_mean(kpre * kpre, gm) + EPS) * xk_ref[...]
        km_ref[...] = _head_rows(km, 0.125).astype(km_ref.dtype)
        vm_ref[...] = _head_rows(mkv[:, XATT_W:], 1.0).astype(vm_ref.dtype)

    vm = pl.BlockSpec(memory_space=pltpu.VMEM)
    rows_shape = jax.ShapeDtypeStruct((4 * MEM_LEN, XATT_W), _MXU)
    return pl.pallas_call(
        body, name="mem_fwd", out_shape=(rows_shape, rows_shape), in_specs=[vm] * 4, out_specs=(vm, vm),
    )(mem, mem_g, wkv, xk_g)


def mem_bwd(mem, mem_g, wkv, xk_g, dkm, dvm):
    def body(mem_ref, g_ref, w_ref, xk_ref, dkm_ref, dvm_ref, gw_ref, gg_ref, gxk_ref):
        mem_v = mem_ref[...]
        mh = mem_v * lax.rsqrt(_row_mean(mem_v * mem_v) + EPS)
        mn = mh * g_ref[...]
        mkv = _mm(mn, w_ref[...])
        kpre = mkv[:, :XATT_W]
        gm = _group_matrix(XATT_W)
        rk = lax.rsqrt(_seg_mean(kpre * kpre, gm) + EPS)
        kn = kpre * rk
        dk = jnp.zeros((MEM_LEN, XATT_W), F32)
        dv = jnp.zeros((MEM_LEN, XATT_W), F32)
        for j in range(4):
            mj = _lane_mask(XATT_W, HEAD * j, HEAD * (j + 1))
            dk = dk + dkm_ref[:, MEM_LEN * j:MEM_LEN * (j + 1)].T * (mj * 0.125)
            dv = dv + dvm_ref[:, MEM_LEN * j:MEM_LEN * (j + 1)].T * mj
        gxk_ref[...] = _col_sum(dk * kn)
        dkn = dk * xk_ref[...]
        dkpre = rk * (dkn - kn * _seg_mean(dkn * kn, gm))
        dmkv = jnp.concatenate([dkpre, dv], axis=1)
        gw_ref[...] = _mm_tn(mn, dmkv)
        dmn = _mm_nt(dmkv, w_ref[...])
        gg_ref[...] = _col_sum(dmn * mh)

    vm = pl.BlockSpec(memory_space=pltpu.VMEM)
    return pl.pallas_call(
        body, name="mem_bwd",
        out_shape=(jax.ShapeDtypeStruct((D_MODEL, 2 * XATT_W), F32), jax.ShapeDtypeStruct((1, D_MODEL), F32),
                   jax.ShapeDtypeStruct((1, XATT_W), F32)),
        in_specs=[vm] * 6, out_specs=(vm, vm, vm),
    )(mem, mem_g, wkv, xk_g, dkm, dvm)


def layer_fwd(x, tgt, rc, rs1, rs2, ng, win_t, cw, cb, wg, brg, big, lam, qg, kg, xqg, sinks, km, vm, og, wout):
    seq = x.shape[0]
    tm = min(ROW_TILE, seq)
    nt = seq // tm
    nb = tm // BLOCK

    def body(x_ref, t_ref, c_ref, s1_ref, s2_ref, ng_ref, win_ref, cw_ref, cb_ref, wg_ref, brg_ref, big_ref, lam_ref,
             qg_ref, kg_ref, xqg_ref, sink_ref, km_ref, vm_ref, og_ref, wout_ref,
             proj_ref, ya_ref, yb_ref, yc_ref, ycat_ref, xn_ref, dout_ref, pswa_ref, pmem_ref, psink_ref, gates_ref,
             a_ref, loss_ref,
             ext_ref, b_scr, hc_ref, kp_ref, vp_ref, lacc_ref):
        i = pl.program_id(0)

        @pl.when(i == 0)
        def _():
            ext_ref[0:8, :] = jnp.zeros((8, LRU_W), F32)
            hc_ref[...] = jnp.zeros_like(hc_ref)
            kp_ref[...] = jnp.zeros_like(kp_ref)
            vp_ref[...] = jnp.zeros_like(vp_ref)
            lacc_ref[...] = jnp.zeros_like(lacc_ref)

        xv = x_ref[...]
        xn = (xv * lax.rsqrt(_row_mean(xv * xv) + EPS) * ng_ref[...]).astype(_MXU)
        xn_ref[...] = xn.astype(xn_ref.dtype)
        proj_ref[...] = _mm_nt(xn, win_ref[...])

        u = proj_ref[:, C_LRUX:C_LRUX + LRU_W]
        ext_ref[8:8 + tm, :] = u
        xc = cb_ref[...]
        for k in range(CONV_K):
            xc = xc + cw_ref[k:k + 1, :] * ext_ref[pl.ds(5 + k, tm), :]
        ext_ref[0:8, :] = u[tm - 8:tm, :]
        rg, ig, sp, a, sq = _lru_gates(xc, wg_ref, brg_ref[...], big_ref[...], lam_ref[...])
        for k, t in enumerate((xc, rg, ig, sq)):
            gates_ref[:, LRU_W * k:LRU_W * (k + 1)] = t.astype(gates_ref.dtype)
        a_ref[...] = a
        b_scr[...] = sq * (ig * xc)
        row8 = lax.broadcasted_iota(jnp.int32, (8, LRU_W), 0)

        def scan_step(g, carry):
            r0 = pl.multiple_of(g * 8, 8)
            av = a_ref[pl.ds(r0, 8), :]
            bv = b_scr[pl.ds(r0, 8), :]
            for d in (1, 2, 4):
                a_sh = jnp.where(row8 >= d, pltpu.roll(av, d, 0), 1.0)
                b_sh = jnp.where(row8 >= d, pltpu.roll(bv, d, 0), 0.0)
                bv = bv + av * b_sh
                av = av * a_sh
            hv = bv + av * carry
            ya_ref[pl.ds(r0, 8), :] = hv
            return hv[7:8, :]

        hc_ref[0:1, :] = lax.fori_loop(0, tm // 8, scan_step, hc_ref[0:1, :], unroll=True)

        gm128 = _group_matrix(KV_W)
        cv, s1v, s2v = c_ref[...], s1_ref[...], s2_ref[...]

        def head_norm_rope(t, g):
            n = t * lax.rsqrt(_seg_mean(t * t, gm128) + EPS)
            return _rope(n * g, cv, s1v, s2v)

        qs_ = (head_norm_rope(proj_ref[:, C_SQ:C_SQ + 128], qg_ref[...]).astype(_MXU),
               head_norm_rope(proj_ref[:, C_SQ + 128:C_SQ + 256], qg_ref[...]).astype(_MXU))
        kr = head_norm_rope(proj_ref[:, C_SK:C_SK + KV_W], kg_ref[...])
        sv = proj_ref[:, C_SV:C_SV + KV_W]
        ka = _place_kv(jnp.concatenate([kp_ref[...], kr], axis=0), 0.125)
        va = _place_kv(jnp.concatenate([vp_ref[...], sv], axis=0), 1.0)
        kp_ref[...] = kr[tm - BLOCK:tm, :]
        vp_ref[...] = sv[tm - BLOCK:tm, :]
        lane128 = lax.broadcasted_iota(jnp.int32, (1, 128), 1)
        for b in range(nb):
            mask = _swa_mask((i == 0) & (b == 0)) if b == 0 else _swa_mask(False)
            band = slice(BLOCK * b, BLOCK * b + 2 * BLOCK)
            blk = slice(BLOCK * b, BLOCK * (b + 1))
            psink = jnp.zeros((BLOCK, 128), F32)
            for j in range(4):
                p, pk = _swa_probs(qs_[j // 2][blk], ka[j][band], mask, sink_ref[0, j])
                pswa_ref[blk, 2 * BLOCK * j:2 * BLOCK * (j + 1)] = p.astype(pswa_ref.dtype)
                psink = jnp.where(lane128 == j, pk, psink)
            psink_ref[blk, :] = psink
            for h in range(2):
                yb_ref[blk, KV_W * h:KV_W * (h + 1)] = _mm(
                    pswa_ref[blk, 4 * BLOCK * h:4 * BLOCK * (h + 1)],
                    jnp.concatenate([va[2 * h][band], va[2 * h + 1][band]], axis=0))

        gm256 = _group_matrix(XATT_W)
        xq = proj_ref[:, C_XQ:C_XQ + XATT_W]
        qx = xq * lax.rsqrt(_seg_mean(xq * xq, gm256) + EPS) * xqg_ref[...]
        pm = _mem_probs(_mm_nt(qx, km_ref[...]))
        for j in range(4):
            pmem_ref[:, MEM_LEN * j:MEM_LEN * (j + 1)] = pm[j].astype(pmem_ref.dtype)
        yc = _mm(pmem_ref[...], vm_ref[...])
        yc_ref[...] = yc

        def gated(y, g, gate):
            return y * lax.rsqrt(_row_mean(y * y) + EPS) * g * (gate * _sigmoid(gate))

        ogv = og_ref[...]
        za = gated(ya_ref[...], ogv[:, :512], proj_ref[:, C_LRUG:C_LRUG + LRU_W])
        zb = gated(yb_ref[...], ogv[:, 512:768], proj_ref[:, C_SWAG:C_SWAG + SWA_W])
        zc = gated(yc, ogv[:, 768:], proj_ref[:, C_XG:C_XG + XATT_W])
        ycat_ref[:, 0:512] = za.astype(ycat_ref.dtype)
        ycat_ref[:, 512:768] = zb.astype(ycat_ref.dtype)
        ycat_ref[:, 768:1024] = zc.astype(ycat_ref.dtype)
        out = xv + _mm(ycat_ref[...], wout_ref[...])
        err = out - t_ref[...]
        dout_ref[...] = (err * (1.0 / D_MODEL)).astype(dout_ref.dtype)
        lacc_ref[...] = lacc_ref[...] + (0.5 / D_MODEL) * jnp.sum(err * err)

        @pl.when(i == nt - 1)
        def _():
            loss_ref[...] = lacc_ref[...]

    def rows(ncol):
        return pl.BlockSpec((tm, ncol), lambda i: (i, 0))

    in_specs = [rows(D_MODEL), rows(D_MODEL), rows(128), rows(128), rows(128),
                _const_spec((1, D_MODEL)), _const_spec((D_IN, D_MODEL), True), _const_spec((CONV_K, LRU_W)),
                _const_spec((1, LRU_W)), _const_spec((2, 256, 512), True), _const_spec((1, LRU_W)),
                _const_spec((1, LRU_W)), _const_spec((1, LRU_W)), _const_spec((1, 128)), _const_spec((1, 128)),
                _const_spec((1, XATT_W)), pl.BlockSpec(memory_space=pltpu.SMEM),
                _const_spec((4 * MEM_LEN, XATT_W), True), _const_spec((4 * MEM_LEN, XATT_W), True),
                _const_spec((1, D_MODEL)), _const_spec((D_MODEL, D_MODEL), True)]
    out_shape = (jax.ShapeDtypeStruct((seq, D_IN), F32), jax.ShapeDtypeStruct((seq, LRU_W), F32),
                 jax.ShapeDtypeStruct((seq, SWA_W), F32), jax.ShapeDtypeStruct((seq, XATT_W), F32),
                 jax.ShapeDtypeStruct((seq, D_MODEL), _MXU), jax.ShapeDtypeStruct((seq, D_MODEL), _MXU),
                 jax.ShapeDtypeStruct((seq, D_MODEL), _MXU), jax.ShapeDtypeStruct((seq, 4 * 2 * BLOCK), _MXU),
                 jax.ShapeDtypeStruct((seq, 4 * MEM_LEN), _MXU), jax.ShapeDtypeStruct((seq, 128), F32),
                 jax.ShapeDtypeStruct((seq, 4 * LRU_W), _MXU), jax.ShapeDtypeStruct((seq, LRU_W), F32),
                 jax.ShapeDtypeStruct((8, 128), F32))
    out_specs = (rows(D_IN), rows(LRU_W), rows(SWA_W), rows(XATT_W), rows(D_MODEL), rows(D_MODEL), rows(D_MODEL),
                 rows(4 * 2 * BLOCK), rows(4 * MEM_LEN), rows(128), rows(4 * LRU_W), rows(LRU_W),
                 _const_spec((8, 128)))
    scratch = [pltpu.VMEM((tm + 8, LRU_W), F32), pltpu.VMEM((tm, LRU_W), F32),
               pltpu.VMEM((8, LRU_W), F32), pltpu.VMEM((BLOCK, KV_W), F32), pltpu.VMEM((BLOCK, KV_W), F32),
               pltpu.VMEM((8, 128), F32)]
    return pl.pallas_call(
        body, name="layer_fwd", grid=(nt,), out_shape=out_shape, in_specs=in_specs, out_specs=out_specs,
        scratch_shapes=scratch,
        compiler_params=pltpu.CompilerParams(dimension_semantics=("arbitrary",), vmem_limit_bytes=VMEM_LIMIT),
    )(x, tgt, rc, rs1, rs2, ng, win_t, cw, cb, wg, brg, big, lam, qg, kg, xqg, sinks, km, vm, og, wout)


def wgrad_reduce(lhs, rhs, bigs, g_small, stage_at, name):
    seq, ncol = rhs.shape
    nblk = lhs.shape[1] // 256
    nres = len(bigs) + (g_small is not None)

    def body(l_ref, r_ref, *refs):
        if nres:
            _hosted_reduce(pl.program_id(0), stage_at, refs[:nres], refs[nres + 1:2 * nres + 1], refs[2 * nres + 1:],
                           g_small is not None)
        refs[nres][...] = _mm_tn(l_ref[...], r_ref[...])

    red_shape, scratch = _hosted_reduce_shapes(bigs, g_small) if nres else ([], [])
    vm = pl.BlockSpec(memory_space=pltpu.VMEM)
    hbm = pl.BlockSpec(memory_space=pl.ANY)
    operands = list(bigs) + ([] if g_small is None else [g_small])
    return pl.pallas_call(
        body, name=name, grid=(nblk,),
        out_shape=(jax.ShapeDtypeStruct((lhs.shape[1], ncol), F32), *red_shape),
        in_specs=[pl.BlockSpec((seq, 256), lambda j: (0, j)), _const_spec((seq, ncol), True)] + [hbm] * len(bigs)
        + [vm] * (g_small is not None),
        out_specs=(pl.BlockSpec((256, ncol), lambda j: (j, 0)),) + (hbm,) * len(red_shape),
        scratch_shapes=scratch,
        compiler_params=pltpu.CompilerParams(dimension_semantics=("arbitrary",), vmem_limit_bytes=VMEM_LIMIT),
    )(lhs, rhs, *operands)


def layer_bwd(x, dout, proj, ya, yb, yc, pswa, pmem, psink, gates, a_all, rc, rs1, rs2, ng, win_t, cw, wg, lam, qg, kg,
              xqg, km, vm, og, wout):
    seq = x.shape[0]
    tm = min(ROW_TILE, seq)
    nt = seq // tm
    nb = tm // BLOCK

    def body(x_ref, dout_ref, proj_ref, ya_ref, yb_ref, yc_ref, pswa_ref, pmem_ref, psink_ref, gates_ref, a_ref,
             c_ref, s1_ref, s2_ref,
             yah_ref, kvh_ref, ch_ref, s1h_ref, s2h_ref,
             ng_ref, win_ref, cw_ref, wg_ref, lam_ref, qg_ref, kg_ref, xqg_ref, km_ref, vm_ref, og_ref, wout_ref,
             gx_ref, dproj_ref, gwg_ref, dkm_ref, dvm_ref, gng_ref, gog_ref, gcb_ref, gbrg_ref, gbig_ref, glam_ref,
             gcw_ref, gqn_ref, gkn_ref, gxqn_ref, gsink_ref,
             hext_ref, aext_ref, an_scr, dh_scr, g_scr, dxc_ext, gcar_ref, dkcar_ref, dvcar_ref):
        i = pl.program_id(0)
        tile = nt - 1 - i
        first_tile = tile == 0

        @pl.when(i == 0)
        def _():
            for r in (gwg_ref, dkm_ref, dvm_ref, gng_ref, gog_ref, gcb_ref, gbrg_ref, gbig_ref, glam_ref, gcw_ref,
                      gqn_ref, gkn_ref, gxqn_ref, gsink_ref, gcar_ref, dkcar_ref, dvcar_ref):
                r[...] = jnp.zeros_like(r)
            dxc_ext[tm:tm + 8, :] = jnp.zeros((8, LRU_W), F32)
            aext_ref[tm:tm + 8, :] = jnp.zeros((8, LRU_W), F32)

        xv = x_ref[...]
        dov = dout_ref[...]
        dz = _mm_nt(dov, wout_ref[...])
        ogv = og_ref[...]

        def group_bwd(y, gate, g, dzg):
            r = lax.rsqrt(_row_mean(y * y) + EPS)
            n = y * r
            sg = _sigmoid(gate)
            dgate = dzg * (n * g) * (sg * (1.0 + gate * (1.0 - sg)))
            dng = dzg * (gate * sg)
            dn = dng * g
            return r * (dn - n * _row_mean(dn * n)), dgate, _col_sum(dng * n)

        dya, dga, goa = group_bwd(ya_ref[...], proj_ref[:, C_LRUG:C_LRUG + LRU_W], ogv[:, :512], dz[:, :512])
        dyb, dgb, gob = group_bwd(yb_ref[...], proj_ref[:, C_SWAG:C_SWAG + SWA_W], ogv[:, 512:768], dz[:, 512:768])
        dyc, dgc, goc = group_bwd(yc_ref[...], proj_ref[:, C_XG:C_XG + XATT_W], ogv[:, 768:], dz[:, 768:])
        gog_ref[...] += jnp.concatenate([goa, gob, goc], axis=1)
        dproj_ref[:, C_LRUG:C_LRUG + LRU_W] = dga.astype(dproj_ref.dtype)
        dproj_ref[:, C_SWAG:C_SWAG + SWA_W] = dgb.astype(dproj_ref.dtype)
        dproj_ref[:, C_XG:C_XG + XATT_W] = dgc.astype(dproj_ref.dtype)

        gm256 = _group_matrix(XATT_W)
        xq = proj_ref[:, C_XQ:C_XQ + XATT_W]
        rq = lax.rsqrt(_seg_mean(xq * xq, gm256) + EPS)
        qn = xq * rq
        qx = qn * xqg_ref[...]
        qxb = qx.astype(_MXU)
        dycb = dyc.astype(_MXU)
        dp_all = _mm_nt(dycb, vm_ref[...])
        dsm = []
        for j in range(4):
            pj = pmem_ref[:, MEM_LEN * j:MEM_LEN * (j + 1)].astype(F32)
            dp = dp_all[:, MEM_LEN * j:MEM_LEN * (j + 1)]
            dsm.append((pj * (dp - jnp.sum(pj * dp, axis=-1, keepdims=True))).astype(_MXU))
        ds_all = jnp.concatenate(dsm, axis=1)
        dvm_ref[...] += _mm_tn(dycb, pmem_ref[...])
        dkm_ref[...] += _mm_tn(qxb, ds_all)
        dqx = _mm(ds_all, km_ref[...])
        gxqn_ref[...] += _col_sum(dqx * qn)
        dqn = dqx * xqg_ref[...]
        dproj_ref[:, C_XQ:C_XQ + XATT_W] = (rq * (dqn - qn * _seg_mean(dqn * qn, gm256))).astype(dproj_ref.dtype)

        gm128 = _group_matrix(KV_W)
        cv, s1v, s2v = c_ref[...], s1_ref[...], s2_ref[...]

        def head_norm(t):
            r = lax.rsqrt(_seg_mean(t * t, gm128) + EPS)
            return t * r, r

        qn_, qr_ = zip(head_norm(proj_ref[:, C_SQ:C_SQ + 128]), head_norm(proj_ref[:, C_SQ + 128:C_SQ + 256]))
        qrope = [_rope(qn_[h] * qg_ref[...], cv, s1v, s2v).astype(_MXU) for h in range(2)]
        kn, krr = head_norm(proj_ref[:, C_SK:C_SK + KV_W])
        kr = _rope(kn * kg_ref[...], cv, s1v, s2v)
        khn, _ = head_norm(kvh_ref[:, 0:KV_W])
        khr = _rope(khn * kg_ref[...], ch_ref[...], s1h_ref[...], s2h_ref[...])
        ka = _place_kv(jnp.concatenate([khr, kr], axis=0), 0.125)
        va = _place_kv(jnp.concatenate([kvh_ref[:, KV_W:2 * KV_W], proj_ref[:, C_SV:C_SV + KV_W]], axis=0), 1.0)
        lane128 = lax.broadcasted_iota(jnp.int32, (1, 128), 1)
        gsink = jnp.zeros((1, 128), F32)
        dk_band, dv_band, dq_blk = [], [], []
        for b in range(nb):
            band = slice(BLOCK * b, BLOCK * b + 2 * BLOCK)
            blk = slice(BLOCK * b, BLOCK * (b + 1))
            dka, dva, dsb = [], [], []
            deltas = jnp.zeros((BLOCK, 128), F32)
            for j in range(4):
                qh = qrope[j // 2][blk]
                doh = dyb[blk, KV_W * (j // 2):KV_W * (j // 2 + 1)].astype(_MXU)
                pb = pswa_ref[blk, 2 * BLOCK * j:2 * BLOCK * (j + 1)]
                p = pb.astype(F32)
                dp = _mm_nt(doh, va[j][band])
                delta = jnp.sum(p * dp, axis=-1, keepdims=True)
                ds = (p * (dp - delta)).astype(_MXU)
                deltas = jnp.where(lane128 == j, delta, deltas)
                dva.append(_mm_tn(pb, doh))
                dka.append(_mm_tn(ds, qh))
                dsb.append(ds)
            gsink = gsink - _col_sum(psink_ref[blk, :] * deltas)
            dk_band.append(_unplace_kv(dka) * 0.125)
            dv_band.append(_unplace_kv(dva))
            dq_blk.append([_mm(jnp.concatenate(dsb[2 * h:2 * h + 2], axis=1),
                               jnp.concatenate([ka[2 * h][band], ka[2 * h + 1][band]], axis=0)) for h in range(2)])
        gsink_ref[...] += gsink
        dk_rows = [dk_band[b][BLOCK:] + (dk_band[b + 1][:BLOCK] if b + 1 < nb else dkcar_ref[...]) for b in range(nb)]
        dv_rows = [dv_band[b][BLOCK:] + (dv_band[b + 1][:BLOCK] if b + 1 < nb else dvcar_ref[...]) for b in range(nb)]
        dkcar_ref[...] = dk_band[0][:BLOCK]
        dvcar_ref[...] = dv_band[0][:BLOCK]
        dkg = _rope_bwd(jnp.concatenate(dk_rows, axis=0), cv, s1v, s2v)
        gkn = _col_sum(dkg * kn)
        dkn = dkg * kg_ref[...]
        dproj_ref[:, C_SK:C_SK + KV_W] = (krr * (dkn - kn * _seg_mean(dkn * kn, gm128))).astype(dproj_ref.dtype)
        dproj_ref[:, C_SV:C_SV + KV_W] = jnp.concatenate(dv_rows, axis=0).astype(dproj_ref.dtype)
        gqn = jnp.zeros((1, 128), F32)
        for h in range(2):
            dqg = _rope_bwd(jnp.concatenate([dq_blk[b][h] for b in range(nb)], axis=0), cv, s1v, s2v)
            gqn = gqn + _col_sum(dqg * qn_[h])
            dqn_ = dqg * qg_ref[...]
            dproj_ref[:, C_SQ + 128 * h:C_SQ + 128 * (h + 1)] = (
                qr_[h] * (dqn_ - qn_[h] * _seg_mean(dqn_ * qn_[h], gm128))).astype(dproj_ref.dtype)
        gqn_ref[...] += gqn
        gkn_ref[...] += gkn

        u = proj_ref[:, C_LRUX:C_LRUX + LRU_W]
        xc, rg, ig, sq = (gates_ref[:, LRU_W * k:LRU_W * (k + 1)].astype(F32) for k in range(4))
        a = a_ref[...]
        sp = _softplus(-lam_ref[...])
        hext_ref[0:8, :] = jnp.where(first_tile, 0.0, yah_ref[...])
        hext_ref[8:8 + tm, :] = ya_ref[...]
        hprev = hext_ref[pl.ds(7, tm), :]
        aext_ref[0:tm, :] = a
        an_scr[...] = aext_ref[pl.ds(1, tm), :]
        dh_scr[...] = dya
        dh_scr[tm - 1:tm, :] = dh_scr[tm - 1:tm, :] + gcar_ref[0:1, :]
        row8 = lax.broadcasted_iota(jnp.int32, (8, LRU_W), 0)

        def scan_step(gi, carry):
            r0 = pl.multiple_of((tm // 8 - 1 - gi) * 8, 8)
            av = an_scr[pl.ds(r0, 8), :]
            bv = dh_scr[pl.ds(r0, 8), :]
            for d in (1, 2, 4):
                a_sh = jnp.where(row8 < 8 - d, pltpu.roll(av, 8 - d, 0), 1.0)
                b_sh = jnp.where(row8 < 8 - d, pltpu.roll(bv, 8 - d, 0), 0.0)
                bv = bv + av * b_sh
                av = av * a_sh
            gv = bv + av * carry
            g_scr[pl.ds(r0, 8), :] = gv
            return gv[0:1, :]

        g0 = lax.fori_loop(0, tm // 8, scan_step, jnp.zeros((1, LRU_W), F32), unroll=True)
        gcar_ref[0:1, :] = a[0:1, :] * g0
        gv = g_scr[...]
        da = gv * hprev
        dig = gv * sq * xc
        dxc = gv * sq * ig
        dla = da * a - gv * (ig * xc) * ((a * a) / sq)
        drg = dla * ((-LRU_C) * sp)
        glam_ref[...] += _col_sum(dla * rg)
        dpr = drg * rg * (1.0 - rg)
        dpi = dig * ig * (1.0 - ig)
        gbrg_ref[...] += _col_sum(dpr)
        gbig_ref[...] += _col_sum(dpi)
        dpre0 = jnp.concatenate([dpr[:, :256], dpi[:, :256]], axis=1).astype(_MXU)
        dpre1 = jnp.concatenate([dpr[:, 256:], dpi[:, 256:]], axis=1).astype(_MXU)
        gwg_ref[0] += _mm_tn(xc[:, :256], dpre0)
        gwg_ref[1] += _mm_tn(xc[:, 256:], dpre1)
        dxc = dxc + jnp.concatenate([_mm_nt(dpre0, wg_ref[0]), _mm_nt(dpre1, wg_ref[1])], axis=1)
        gcb_ref[...] += _col_sum(dxc)
        dxc_ext[0:tm, :] = dxc
        du = jnp.zeros((tm, LRU_W), F32)
        for k in range(CONV_K):
            later = dxc_ext[pl.ds(3 - k, tm), :]
            gcw_ref[k:k + 1, :] += _col_sum(later * u)
            du = du + cw_ref[k:k + 1, :] * later
        dxc_ext[tm:tm + 8, :] = dxc[0:8, :]
        dproj_ref[:, C_LRUX:C_LRUX + LRU_W] = du.astype(dproj_ref.dtype)

        dxn = _mm(dproj_ref[...], win_ref[...])
        rx = lax.rsqrt(_row_mean(xv * xv) + EPS)
        xh = xv * rx
        gng_ref[...] += _col_sum(dxn * xh)
        dxh = dxn * ng_ref[...]
        gx_ref[...] = dov.astype(F32) + rx * (dxh - xh * _row_mean(dxh * xh))

        @pl.when(i == nt - 1)
        def _():
            glam_ref[...] = glam_ref[...] * (LRU_C * _sigmoid(-lam_ref[...]))

    def rows(ncol, arr_cols_block=0):
        return pl.BlockSpec((tm, ncol), lambda i: (nt - 1 - i, arr_cols_block))

    def halo(nrow, ncol, colblk=0):
        per = tm // nrow
        return pl.BlockSpec((nrow, ncol), lambda i: (jnp.maximum((nt - 1 - i) * per - 1, 0), colblk))

    in_specs = [rows(D_MODEL), rows(D_MODEL), rows(D_IN), rows(LRU_W), rows(SWA_W), rows(XATT_W),
                rows(4 * 2 * BLOCK), rows(4 * MEM_LEN), rows(128), rows(4 * LRU_W), rows(LRU_W),
                rows(128), rows(128), rows(128),
                halo(8, LRU_W), halo(BLOCK, 2 * KV_W, C_SK // (2 * KV_W)),
                halo(BLOCK, 128), halo(BLOCK, 128), halo(BLOCK, 128),
                _const_spec((1, D_MODEL)), _const_spec((D_IN, D_MODEL), True), _const_spec((CONV_K, LRU_W)),
                _const_spec((2, 256, 512), True), _const_spec((1, LRU_W)), _const_spec((1, 128)), _const_spec((1, 128)),
                _const_spec((1, XATT_W)),
                _const_spec((4 * MEM_LEN, XATT_W), True), _const_spec((4 * MEM_LEN, XATT_W), True),
                _const_spec((1, D_MODEL)), _const_spec((D_MODEL, D_MODEL), True)]
    small = [(2, 256, 512), (XATT_W, 4 * MEM_LEN), (XATT_W, 4 * MEM_LEN), (1, D_MODEL), (1, D_MODEL), (1, LRU_W), (1, LRU_W),
             (1, LRU_W), (1, LRU_W), (CONV_K, LRU_W), (1, 128), (1, 128), (1, XATT_W), (1, 128)]
    out_shape = (jax.ShapeDtypeStruct((seq, D_MODEL), F32), jax.ShapeDtypeStruct((seq, D_IN), _MXU)) + tuple(
        jax.ShapeDtypeStruct(s, F32) for s in small)
    out_specs = (rows(D_MODEL), rows(D_IN)) + tuple(_const_spec(s) for s in small)
    scratch = [pltpu.VMEM((tm + 8, LRU_W), F32), pltpu.VMEM((tm + 8, LRU_W), F32),
               pltpu.VMEM((tm, LRU_W), F32), pltpu.VMEM((tm, LRU_W), F32), pltpu.VMEM((tm, LRU_W), F32),
               pltpu.VMEM((tm + 8, LRU_W), F32),
               pltpu.VMEM((8, LRU_W), F32), pltpu.VMEM((BLOCK, KV_W), F32), pltpu.VMEM((BLOCK, KV_W), F32)]
    return pl.pallas_call(
        body, name="layer_bwd", grid=(nt,), out_shape=out_shape, in_specs=in_specs, out_specs=out_specs,
        scratch_shapes=scratch,
        compiler_params=pltpu.CompilerParams(dimension_semantics=("arbitrary",), vmem_limit_bytes=VMEM_LIMIT),
    )(x, dout, proj, ya, yb, yc, pswa, pmem, psink, gates, a_all, rc, rs1, rs2, ya, proj, rc, rs1, rs2,
      ng, win_t, cw, wg, lam, qg, kg, xqg, km, vm, og, wout)


def _reduce_protocol(big, sm, outs, osm, r1, r1s, wire, r2, r2s, wire2, ps, own, send, recv, lsem):
    nbig = len(big)
    x, y, c = lax.axis_index("x"), lax.axis_index("y"), lax.axis_index("c")
    sibling = (x, y, 1 - c)
    near, far, diag = _partners(x, y, c)
    me, near_id, far_id, diag_id = _chip_of(x, y), _chip_of(*near), _chip_of(*far), _chip_of(*diag)

    def copy(k, src, dst, to):
        return pltpu.make_async_remote_copy(src_ref=src, dst_ref=dst, send_sem=send.at[k], recv_sem=recv.at[k],
                                            device_id=to, device_id_type=MESH)

    def sent(stage, a):
        if a == nbig:
            src, dst, to = ((sm.at[1 - c], r1s, sibling), (r1s, r2s.at[0], (*near, c)), (ps, r2s.at[1], (*far, c)),
                            (osm.at[c], osm.at[c], sibling))[stage]
            return [copy(5 * nbig + stage, src, dst, to)]
        if stage == 0:
            return [copy(5 * a, big[a].at[:, 1 - c], r1[a], sibling)]
        if stage == 1:
            return [copy(5 * a + 1, wire[a].at[near_id], r2[a].at[0], (*near, c)),
                    copy(5 * a + 2, wire[a].at[diag_id], r2[a].at[1], (*near, c))]
        if stage == 2:
            return [copy(5 * a + 3, wire2[a], r2[a].at[2], (*far, c))]
        return [copy(5 * a + 4, outs[a].at[c], outs[a].at[c], sibling)]

    arrays = range(nbig + (sm is not None))

    def start(stage, a):
        for cp in sent(stage, a):
            cp.start()

    def arrived(k, ref):
        copy(k, ref, ref, sibling).wait_recv()

    def loads():
        return [pltpu.make_async_copy(big[a].at[:, c], own[a], lsem.at[a]) for a in range(nbig)]

    def stage0():
        for a in arrays:
            start(0, a)
        for cp in loads():
            cp.start()

    def stage1():
        for a in range(nbig):
            loads()[a].wait()
            arrived(5 * a, r1[a])
            for k in range(N_CHIPS):
                r1[a][k] = own[a][k] + r1[a][k]
                wire[a][k] = r1[a][k].astype(wire[a].dtype)
            start(1, a)
        if sm is not None:
            arrived(5 * nbig, r1s)
            r1s[...] = sm[c] + r1s[...]
            start(1, nbig)

    def stage2():
        for a in range(nbig):
            arrived(5 * a + 1, r2[a].at[0])
            arrived(5 * a + 2, r2[a].at[1])
            r1[a][me] = r1[a][me] + r2[a][0].astype(F32)
            wire2[a][...] = (r1[a][far_id] + r2[a][1].astype(F32)).astype(wire2[a].dtype)
            start(2, a)
        if sm is not None:
            arrived(5 * nbig + 1, r2s.at[0])
            ps[...] = r1s[...] + r2s[0]
            start(2, nbig)

    def stage3():
        for a in range(nbig):
            arrived(5 * a + 3, r2[a].at[2])
            outs[a][c] = r1[a][me] + r2[a][2].astype(F32)
            start(3, a)
        if sm is not None:
            arrived(5 * nbig + 2, r2s.at[1])
            osm[c] = ps[...] + r2s[1]
            start(3, nbig)

    def stage4():
        for a in range(nbig):
            arrived(5 * a + 4, outs[a].at[1 - c])
        if sm is not None:
            arrived(5 * nbig + 3, osm.at[1 - c])
        for stage in range(4):
            for a in arrays:
                for cp in sent(stage, a):
                    cp.wait_send()

    return [stage0, stage1, stage2, stage3, stage4]


def _reduce_buffers(bigs, g_small):
    half = [b.shape[2:] for b in bigs]
    sm_half = None if g_small is None else g_small.shape[1:]
    out_shape = [jax.ShapeDtypeStruct((2,) + h, F32) for h in half]
    small = lambda lead: [] if g_small is None else [pltpu.VMEM(lead + sm_half, F32)]
    if g_small is not None:
        out_shape.append(jax.ShapeDtypeStruct(g_small.shape, F32))
    n_sem = 5 * len(bigs) + 4
    scratch = ([pltpu.VMEM((N_CHIPS,) + h, F32) for h in half] + small(())
               + [pltpu.VMEM((N_CHIPS,) + h, _WIRE) for h in half]
               + [pltpu.VMEM((3,) + h, _WIRE) for h in half] + small((2,))
               + [pltpu.VMEM(h, _WIRE) for h in half] + small(())
               + [pltpu.VMEM((N_CHIPS,) + h, F32) for h in half]
               + [pltpu.SemaphoreType.DMA((n_sem,)), pltpu.SemaphoreType.DMA((n_sem,)),
                  pltpu.SemaphoreType.DMA((len(bigs),))])
    return out_shape, scratch


def _split_reduce_refs(refs, nbig, has_small):
    it = iter(refs)
    take = lambda n: [next(it) for _ in range(n)]
    one = lambda: next(it) if has_small else None
    big, sm = take(nbig), one()
    outs, osm = take(nbig), one()
    r1, r1s, wire, r2, r2s, wire2, ps, own = take(nbig), one(), take(nbig), take(nbig), one(), take(nbig), one(), take(nbig)
    send, recv, lsem = take(3)
    return big, sm, outs, osm, r1, r1s, wire, r2, r2s, wire2, ps, own, send, recv, lsem


def _hosted_reduce_shapes(bigs, g_small):
    red_shape, scratch = _reduce_buffers(bigs, g_small)
    nres = len(red_shape)
    return red_shape, [pltpu.VMEM(r.shape, r.dtype) for r in red_shape] + scratch + [pltpu.SemaphoreType.DMA((nres,))]


def _hosted_reduce(step, stage_at, operands, results, scratch, has_small):
    nres = len(results)
    sums, rest, fsem = scratch[:nres], scratch[nres:-1], scratch[-1]
    refs = tuple(operands) + tuple(sums) + tuple(rest)
    for at, stage in zip(stage_at, _reduce_protocol(*_split_reduce_refs(refs, nres - has_small, has_small))):
        pl.when(step == at)(stage)

    @pl.when(step == stage_at[-1])
    def _():
        out = [pltpu.make_async_copy(sums[k], results[k], fsem.at[k]) for k in range(nres)]
        for cp in out:
            cp.start()
        for cp in out:
            cp.wait()


def reduce_grads(big, name, parts):
    chips, halves, rows_, cols = big.shape
    sub = jax.ShapeDtypeStruct((chips, halves, rows_ // parts, cols), big.dtype)

    def body(b_ref, o_ref, *scratch):
        refs = [b_ref.at[:, :, s] for s in range(parts)] + [o_ref.at[:, s] for s in range(parts)] + list(scratch)
        for stage in _reduce_protocol(*_split_reduce_refs(refs, parts, False)):
            stage()

    _, scratch = _reduce_buffers([sub] * parts, None)
    return pl.pallas_call(
        body, name=name, out_shape=jax.ShapeDtypeStruct((halves, parts, rows_ // parts, cols), F32),
        in_specs=[pl.BlockSpec(memory_space=pl.ANY)], out_specs=pl.BlockSpec(memory_space=pltpu.VMEM),
        scratch_shapes=scratch, compiler_params=pltpu.CompilerParams(vmem_limit_bytes=VMEM_LIMIT),
    )(big.reshape(chips, halves, parts, rows_ // parts, cols))


def adamw(w, g, m, v, name):
    rows_, cols = w.shape
    tr = max(t for t in range(8, rows_ + 1, 8) if rows_ % t == 0 and t * cols * 4 <= ADAM_BLOCK_BYTES)

    def body(w_ref, g_ref, m_ref, v_ref, d_ref, nm_ref, nv_ref):
        d_ref[...], nm_ref[...], nv_ref[...] = _adam_update(w_ref[...], g_ref[...], m_ref[...], v_ref[...])

    spec = pl.BlockSpec((tr, cols), lambda i: (i, 0))
    shp = jax.ShapeDtypeStruct(w.shape, F32)
    return pl.pallas_call(
        body, name=name, grid=(rows_ // tr,), out_shape=(shp, shp, shp), in_specs=[spec] * 4, out_specs=(spec,) * 3,
        compiler_params=pltpu.CompilerParams(dimension_semantics=("arbitrary",)),
    )(w, g, m, v)


def _adam_update(w, g, m, v):
    nm = ADAM_B1 * m + (1.0 - ADAM_B1) * g
    nv = ADAM_B2 * v + (1.0 - ADAM_B2) * (g * g)
    m_hat = nm / (1.0 - ADAM_B1 ** ADAM_STEP)
    v_hat = nv / (1.0 - ADAM_B2 ** ADAM_STEP)
    return (-ADAM_LR) * (m_hat / (jnp.sqrt(v_hat) + ADAM_EPS) + ADAM_WD * w), nm, nv


def adamw_vectors(g_pack, g_mats, ws, ms, vs):
    nvec, nmat = len(SMALL_VECTORS), len(g_mats)
    n = nvec + nmat

    def body(*refs):
        pk = refs[0]
        gm_refs = refs[1:1 + nmat]
        w_refs, m_refs, v_refs = (refs[1 + nmat + k * n:1 + nmat + (k + 1) * n] for k in range(3))
        outs = refs[1 + nmat + 3 * n:]
        g_out, d_out, nm_out, nv_out = outs[:nvec], outs[nvec:nvec + n], outs[nvec + n:nvec + 2 * n], outs[nvec + 2 * n:]
        chip = 2 * lax.axis_index("x") + lax.axis_index("y")
        for k, (name, row, width) in enumerate(SMALL_VECTORS):
            if name == "conv_w":
                g = jnp.concatenate([pk[pl.ds(row + 4 * t + chip, 1), :] for t in range(CONV_K)], axis=0)[None]
            elif width >= 128:
                g = jnp.concatenate([pk[row + r:row + r + 1, :] for r in range(width // 128)], axis=1)
            else:
                g = pk[row:row + 1, 0:width]
            g_out[k][...] = g
            d_out[k][...], nm_out[k][...], nv_out[k][...] = _adam_update(w_refs[k][...], g, m_refs[k][...], v_refs[k][...])
        for k in range(nvec, n):
            d_out[k][...], nm_out[k][...], nv_out[k][...] = _adam_update(
                w_refs[k][...], gm_refs[k - nvec][...], m_refs[k][...], v_refs[k][...])

    vm = pl.BlockSpec(memory_space=pltpu.VMEM)
    like = [jax.ShapeDtypeStruct(w.shape, F32) for w in ws]
    out_shape = like[:nvec] + like * 3
    return pl.pallas_call(
        body, name="adamw_vectors", out_shape=tuple(out_shape), in_specs=[vm] * (1 + nmat + 3 * n),
        out_specs=(vm,) * len(out_shape),
    )(g_pack, *g_mats, *ws, *ms, *vs)


SMALL_VECTORS = (("norm_g", 0, 1024), ("mem_norm_g", 8, 1024), ("conv_w", 16, 512), ("conv_b", 32, 512),
                 ("b_rg", 292, 512), ("b_ig", 552, 512), ("lru_lambda", 556, 512), ("q_norm_g", 560, 64),
                 ("k_norm_g", 561, 64), ("sinks", 562, 4), ("xq_norm_g", 563, 64), ("xk_norm_g", 564, 64),
                 ("out_norm_g", 565, 1024))
SMALL_MATRICES = (("w_rg", 36), ("w_ig", 296))
LOSS_ROW = 573
SMALL_ROWS = 576


def _pack(parts, rows_):
    flat = jnp.concatenate([p.reshape(-1) for p in parts])
    return jnp.pad(flat, (0, rows_ * 128 - flat.shape[0])).reshape(rows_, 128)


def _pad_to(v, n):
    v = v.reshape(-1)
    return jnp.pad(v, (0, n - v.shape[0]))


def _block_diag_gates(w_rg, w_ig):
    eye = jnp.eye(4, dtype=w_rg.dtype)

    def bd(w4):
        return (w4[:, :, None, :] * eye[:, None, :, None]).reshape(256, 256)

    return jnp.stack([jnp.concatenate([bd(w_rg[4 * h:4 * h + 4]), bd(w_ig[4 * h:4 * h + 4])], axis=1) for h in (0, 1)])


def _diag_blocks(g):
    g6 = g.reshape(2, 4, HEAD, 2, 4, HEAD)
    d = (g6 * jnp.eye(4, dtype=g.dtype)[None, :, None, None, :, None]).sum(axis=4)
    return d[:, :, :, 0].reshape(8, HEAD, HEAD), d[:, :, :, 1].reshape(8, HEAD, HEAD)


def _rope_tables(seq):
    pos = np.arange(seq, dtype=np.float32)
    inv_freq = (np.float32(ROPE_THETA) ** (-(np.arange(0, ROPE_DIM, 2, dtype=np.float32) / np.float32(ROPE_DIM)))
                ).astype(np.float32)
    ang = (pos[:, None] * inv_freq[None, :]).astype(np.float32)
    cos, sin = np.cos(ang).astype(np.float32), np.sin(ang).astype(np.float32)
    z = lambda n: np.zeros((seq, n), np.float32)
    c64 = np.concatenate([cos, cos, np.ones((seq, HEAD - ROPE_DIM), np.float32)], axis=1)
    s1_64 = np.concatenate([-sin, z(HEAD - 8)], axis=1)
    s2_64 = np.concatenate([z(8), sin, z(HEAD - ROPE_DIM)], axis=1)
    return tuple(jnp.asarray(np.concatenate([t, t], axis=1)) for t in (c64, s1_64, s2_64))


def kernel(x, mem, norm_g, mem_norm_g, w_in, conv_w, conv_b, w_rg, b_rg, w_ig, b_ig, lru_lambda, q_norm_g, k_norm_g, sinks, w_mem_kv, xq_norm_g, xk_norm_g, out_norm_g, w_out, loss_target, m_norm_g, m_mem_norm_g, m_w_in, m_conv_w, m_conv_b, m_w_rg, m_b_rg, m_w_ig, m_b_ig, m_lru_lambda, m_q_norm_g, m_k_norm_g, m_sinks, m_w_mem_kv, m_xq_norm_g, m_xk_norm_g, m_out_norm_g, m_w_out, v_norm_g, v_mem_norm_g, v_w_in, v_conv_w, v_conv_b, v_w_rg, v_b_rg, v_w_ig, v_b_ig, v_lru_lambda, v_q_norm_g, v_k_norm_g, v_sinks, v_w_mem_kv, v_xq_norm_g, v_xk_norm_g, v_out_norm_g, v_w_out):
    seq = x.shape[1]
    xs, tgt, mems = x[0], loss_target[0], mem[0]

    cw_sh = jnp.pad(conv_w[0], ((0, 4), (0, 0)))
    win_t, wout, wkv, cw_all = gather_weights(w_in[0].T, w_out[0], w_mem_kv[0], cw_sh)
    cw = cw_all.reshape(N_CHIPS, 8, 128)[:, :CONV_K].transpose(1, 0, 2).reshape(CONV_K, LRU_W)

    rc, rs1, rs2 = _rope_tables(seq)
    wg = _block_diag_gates(w_rg[0], w_ig[0]).astype(_MXU)
    qg = jnp.tile(q_norm_g, (1, 2))
    kg = jnp.tile(k_norm_g, (1, 2))
    xqg = jnp.tile(xq_norm_g, (1, 4))
    xkg = jnp.tile(xk_norm_g, (1, 4))

    km, vm = mem_fwd(mems, mem_norm_g, wkv, xkg)
    proj, ya, yb, yc, ycat, xn, dout, pswa, pmem, psink, gates, a_all, loss8 = layer_fwd(
        xs, tgt, rc, rs1, rs2, norm_g, win_t, cw, conv_b, wg, b_rg, b_ig, lru_lambda, qg, kg, xqg, sinks, km, vm,
        out_norm_g, wout)
    (g_wout,) = wgrad_reduce(ycat, dout, [], None, (), "wgrad_out")
    (gx, dproj, g_wg, dkm, dvm, g_ng, g_og, g_cb, g_brg, g_big, g_lam, g_cw, g_qn, g_kn, g_xqn, g_sink) = layer_bwd(
        xs, dout, proj, ya, yb, yc, pswa, pmem, psink, gates, a_all, rc, rs1, rs2, norm_g, win_t, cw, wg, lru_lambda, qg,
        kg, xqg, km, vm, out_norm_g, wout)
    g_wkv, g_mng, g_xkn = mem_bwd(mems, mem_norm_g, wkv, xkg, dkm, dvm)

    g_wrg, g_wig = _diag_blocks(g_wg)
    fold = lambda v, n: v.reshape(n, HEAD).sum(axis=0)
    small_g = _pack([g_ng, g_mng, g_cw, g_cb, g_wrg, g_brg, g_wig, g_big, g_lam, _pad_to(fold(g_qn, 2), 128),
                     _pad_to(fold(g_kn, 2), 128), g_sink, _pad_to(fold(g_xqn, 4), 128), _pad_to(fold(g_xkn, 4), 128),
                     g_og, loss8[0:1]], SMALL_ROWS)
    early = [g_wout.reshape(N_CHIPS, 2, D_MODEL // 8, D_MODEL), g_wkv.reshape(N_CHIPS, 2, D_MODEL // 8, 2 * XATT_W)]
    g_win_t, r_out, r_kv, r_small = wgrad_reduce(dproj, xn, early, small_g.reshape(2, SMALL_ROWS // 2, 128),
                                                 (0, 2, 5, 7, 8), "wgrad_in")
    r_in = reduce_grads(g_win_t.reshape(N_CHIPS, 2, D_IN // 8, D_MODEL), "reduce_w_in", 6)

    r_small = r_small.reshape(SMALL_ROWS, 128)
    loss = r_small[LOSS_ROW, 0]
    grads = {"w_in": r_in.reshape(D_IN // 4, D_MODEL).T[None], "w_mem_kv": r_kv.reshape(D_MODEL // 4, 2 * XATT_W)[None],
             "w_out": r_out.reshape(D_MODEL // 4, D_MODEL)[None]}
    for name, row in SMALL_MATRICES:
        grads[name] = r_small[row:row + 256].reshape(1, LRU_BLOCKS, HEAD, HEAD)
    weights = dict(norm_g=norm_g, mem_norm_g=mem_norm_g, w_in=w_in, conv_w=conv_w, conv_b=conv_b, w_rg=w_rg, b_rg=b_rg,
                   w_ig=w_ig, b_ig=b_ig, lru_lambda=lru_lambda, q_norm_g=q_norm_g, k_norm_g=k_norm_g, sinks=sinks,
                   w_mem_kv=w_mem_kv, xq_norm_g=xq_norm_g, xk_norm_g=xk_norm_g, out_norm_g=out_norm_g, w_out=w_out)
    ms = dict(norm_g=m_norm_g, mem_norm_g=m_mem_norm_g, w_in=m_w_in, conv_w=m_conv_w, conv_b=m_conv_b, w_rg=m_w_rg,
              b_rg=m_b_rg, w_ig=m_w_ig, b_ig=m_b_ig, lru_lambda=m_lru_lambda, q_norm_g=m_q_norm_g, k_norm_g=m_k_norm_g,
              sinks=m_sinks, w_mem_kv=m_w_mem_kv, xq_norm_g=m_xq_norm_g, xk_norm_g=m_xk_norm_g,
              out_norm_g=m_out_norm_g, w_out=m_w_out)
    vs = dict(norm_g=v_norm_g, mem_norm_g=v_mem_norm_g, w_in=v_w_in, conv_w=v_conv_w, conv_b=v_conv_b, w_rg=v_w_rg,
              b_rg=v_b_rg, w_ig=v_w_ig, b_ig=v_b_ig, lru_lambda=v_lru_lambda, q_norm_g=v_q_norm_g, k_norm_g=v_k_norm_g,
              sinks=v_sinks, w_mem_kv=v_w_mem_kv, xq_norm_g=v_xq_norm_g, xk_norm_g=v_xk_norm_g,
              out_norm_g=v_out_norm_g, w_out=v_w_out)

    delta, new_m, new_v = {}, {}, {}
    d2, m2, v2 = adamw(w_in[0].T, r_in.reshape(D_IN // 4, D_MODEL), m_w_in[0].T, v_w_in[0].T, "adamw_w_in")
    delta["w_in"], new_m["w_in"], new_v["w_in"] = d2.T[None], m2.T[None], v2.T[None]
    for name in ("w_mem_kv", "w_out"):
        shp = weights[name].shape
        d2, m2, v2 = adamw(weights[name][0], grads[name][0], ms[name][0], vs[name][0], "adamw_" + name)
        delta[name], new_m[name], new_v[name] = d2.reshape(shp), m2.reshape(shp), v2.reshape(shp)
    vec_names = [n for n, _, _ in SMALL_VECTORS]
    small_names = vec_names + [n for n, _ in SMALL_MATRICES]
    res = adamw_vectors(r_small, [grads[n] for n, _ in SMALL_MATRICES], [weights[n] for n in small_names],
                        [ms[n] for n in small_names], [vs[n] for n in small_names])
    nvec, nall = len(vec_names), len(small_names)
    grads.update(zip(vec_names, res[:nvec]))
    delta.update(zip(small_names, res[nvec:nvec + nall]))
    new_m.update(zip(small_names, res[nvec + nall:nvec + 2 * nall]))
    new_v.update(zip(small_names, res[nvec + 2 * nall:]))

    order = ("norm_g", "mem_norm_g", "w_in", "conv_w", "conv_b", "w_rg", "b_rg", "w_ig", "b_ig", "lru_lambda",
             "q_norm_g", "k_norm_g", "sinks", "w_mem_kv", "xq_norm_g", "xk_norm_g", "out_norm_g", "w_out")
    return (loss, gx[None], *[grads[n] for n in order], *[delta[n] for n in order], *[new_m[n] for n in order],
            *[new_v[n] for n in order])
```

```python
import jax
import jax.numpy as jnp
import numpy as np
from jax import lax
from jax.experimental import pallas as pl
from jax.experimental.pallas import tpu as pltpu

F32 = jnp.float32
_MXU = jnp.bfloat16
_WIRE = jnp.bfloat16

D_MODEL = 1024
MEM_LEN = 256
HEAD = 64
LRU_W = 512
LRU_BLOCKS = 8
CONV_K = 4
LRU_C = 8.0
SWA_W = 256
KV_W = 128
XATT_W = 256
BLOCK = 128
D_IN = 2304
ROPE_THETA = 500000.0
ROPE_DIM = 16
EPS = 1e-6
NEG_INF = -1e30
C_LRUX, C_LRUG, C_SQ, C_SK, C_SV, C_SWAG, C_XQ, C_XG = 0, 512, 1024, 1280, 1408, 1536, 1792, 2048

ADAM_LR, ADAM_B1, ADAM_B2, ADAM_EPS, ADAM_WD, ADAM_STEP = 0.001, 0.9, 0.999, 1e-08, 0.01, 10

N_CHIPS = 4
ROW_TILE = 256
VMEM_LIMIT = 56 * 1024 * 1024
ADAM_BLOCK_BYTES = 1280 * 1024
MESH = pl.DeviceIdType.MESH


def _mm(a, b):
    return jnp.dot(a.astype(_MXU), b.astype(_MXU), preferred_element_type=F32)


def _mm_nt(a, b):
    return lax.dot_general(a.astype(_MXU), b.astype(_MXU), (((1,), (1,)), ((), ())), preferred_element_type=F32)


def _mm_tn(a, b):
    return lax.dot_general(a.astype(_MXU), b.astype(_MXU), (((0,), (0,)), ((), ())), preferred_element_type=F32)


def _group_matrix(width):
    r = lax.shift_right_logical(lax.broadcasted_iota(jnp.int32, (width, width), 0), 6)
    c = lax.shift_right_logical(lax.broadcasted_iota(jnp.int32, (width, width), 1), 6)
    return (r == c).astype(_MXU)


def _seg_mean(x, gm):
    return jnp.dot(x.astype(_MXU), gm, preferred_element_type=F32) * (1.0 / HEAD)


def _row_mean(x):
    return jnp.mean(x, axis=-1, keepdims=True)


def _col_sum(x):
    return jnp.sum(x, axis=0, keepdims=True)


def _sigmoid(x):
    return jax.nn.sigmoid(x)


def _softplus(z):
    e = jnp.exp(-jnp.abs(z))
    u = 1.0 + e
    log1p_e = jnp.where(u == 1.0, e, jnp.log(u) * (e / (u - 1.0)))
    return jnp.maximum(z, 0.0) + log1p_e


def _rope(t, c, s1, s2):
    return t * c + pltpu.roll(t, 120, 1) * s1 + pltpu.roll(t, 8, 1) * s2


def _rope_bwd(d, c, s1, s2):
    return d * c + pltpu.roll(d * s1, 8, 1) + pltpu.roll(d * s2, 120, 1)


def _lane_mask(width, lo, hi):
    lane = lax.broadcasted_iota(jnp.int32, (1, width), 1)
    return ((lane >= lo) & (lane < hi)).astype(F32)


def _swa_mask(first_block):
    qi = lax.broadcasted_iota(jnp.int32, (BLOCK, 2 * BLOCK), 0)
    kj = lax.broadcasted_iota(jnp.int32, (BLOCK, 2 * BLOCK), 1)
    rel = qi + BLOCK - kj
    ok = (rel >= 0) & (rel < BLOCK)
    return ok & (jnp.logical_not(first_block) | (kj >= BLOCK))


def _place_kv(t, scale):
    lo = t * (_lane_mask(KV_W, 0, HEAD) * scale)
    hi = t * (_lane_mask(KV_W, HEAD, KV_W) * scale)
    return [a.astype(_MXU) for a in (lo, pltpu.roll(lo, HEAD, 1), pltpu.roll(hi, HEAD, 1), hi)]


def _unplace_kv(d):
    return (_lane_mask(KV_W, 0, HEAD) * (d[0] + pltpu.roll(d[1], HEAD, 1))
            + _lane_mask(KV_W, HEAD, KV_W) * (d[3] + pltpu.roll(d[2], HEAD, 1)))


def _swa_probs(qh, ka, mask, sink):
    s = _mm_nt(qh, ka)
    s = jnp.where(mask, s, NEG_INF)
    m = jnp.maximum(jnp.max(s, axis=-1, keepdims=True), sink)
    p = jnp.exp(s - m)
    esink = jnp.exp(sink - m)
    inv = 1.0 / (jnp.sum(p, axis=-1, keepdims=True) + esink)
    return p * inv, esink * inv


def _mem_probs(s_all):
    out = []
    for j in range(4):
        s = s_all[:, MEM_LEN * j:MEM_LEN * (j + 1)]
        p = jnp.exp(s - jnp.max(s, axis=-1, keepdims=True))
        out.append(p * (1.0 / jnp.sum(p, axis=-1, keepdims=True)))
    return out


def _head_rows(t, scale):
    return jnp.concatenate([t * (_lane_mask(XATT_W, HEAD * j, HEAD * (j + 1)) * scale) for j in range(4)], axis=0)


def _lru_gates(xc, wg_ref, brg, big, lam):
    p0 = _mm(xc[:, :256], wg_ref[0])
    p1 = _mm(xc[:, 256:], wg_ref[1])
    rg = _sigmoid(jnp.concatenate([p0[:, :256], p1[:, :256]], axis=1) + brg)
    ig = _sigmoid(jnp.concatenate([p0[:, 256:], p1[:, 256:]], axis=1) + big)
    sp = _softplus(-lam)
    la = (-LRU_C) * rg * sp
    a = jnp.exp(la)
    th = jnp.tanh(la)
    one_minus_a2 = (-2.0 * th) / (1.0 - th)
    return rg, ig, sp, a, jnp.sqrt(one_minus_a2)


def _const_spec(shape, single=False):
    zeros = (0,) * len(shape)
    if single:
        return pl.BlockSpec(shape, lambda i: zeros, pipeline_mode=pl.Buffered(1))
    return pl.BlockSpec(shape, lambda i: zeros)


def _chip_of(x, y):
    return 2 * x + y


def _partners(x, y, c):
    north = c == 1
    near = (jnp.where(north, 1 - x, x), jnp.where(north, y, 1 - y))
    far = (jnp.where(north, x, 1 - x), jnp.where(north, 1 - y, y))
    return near, far, (1 - x, 1 - y)


def gather_weights(win_t, wout, wkv, convw):
    arrs = (win_t, wout, wkv)
    n = len(arrs)
    pieces = [(a, 0, arr.shape[0] // 2) for a, arr in enumerate(arrs)]
    npc = len(pieces)

    def body(a0, a1, a2, cw, o0, o1, o2, ocw, s0, s1, s2, send, recv, lsem):
        ins, outs = (s0, s1, s2), (o0, o1, o2)
        for src, dst in zip((a0, a1, a2), ins):
            dst[...] = src[...].astype(dst.dtype)
        x, y, c = lax.axis_index("x"), lax.axis_index("y"), lax.axis_index("c")
        sibling = (x, y, 1 - c)
        near, far, diag = _partners(x, y, c)
        chips = [near, far, diag]
        me = _chip_of(x, y)

        def landed(p, chip, half):
            a, off, rows_ = pieces[p]
            r = ins[a].shape[0]
            return outs[a].at[pl.ds(pl.multiple_of(chip * r + half * (r // 2) + off, 16), rows_)]

        def mine(p):
            a, off, rows_ = pieces[p]
            return ins[a].at[pl.ds(pl.multiple_of(c * (ins[a].shape[0] // 2) + off, 16), rows_)]

        def copy(k, src, dst, to):
            return pltpu.make_async_remote_copy(src_ref=src, dst_ref=dst, send_sem=send.at[k], recv_sem=recv.at[k],
                                                device_id=to, device_id_type=MESH)

        def cw_rows(chip):
            return ocw.at[pl.ds(pl.multiple_of(chip * 8, 8), 8)]

        locals_ = []
        for a in range(n):
            r = ins[a].shape[0]
            locals_.append(pltpu.make_async_copy(ins[a], outs[a].at[pl.ds(pl.multiple_of(me * r, 16), r)], lsem.at[a]))
        locals_.append(pltpu.make_async_copy(cw, cw_rows(me), lsem.at[n]))
        for cp in locals_:
            cp.start()

        sent = []
        for p in range(npc):
            for j in range(2):
                sent.append(copy(p * 6 + j, mine(p), landed(p, me, c), (*chips[j], c)))
        for j, chip in enumerate(chips):
            sent.append(copy(npc * 6 + j, cw, cw_rows(me), (*chip, c)))
        for cp in sent:
            cp.start()
        for j in range(3):
            for p in range(npc):
                got = landed(p, _chip_of(*chips[j]), c)
                copy(p * 6 + j, got, got, sibling).wait_recv()
                if j == 0:
                    sent.append(copy(p * 6 + 2, got, got, (*far, c)))
                    sent[-1].start()
                sent.append(copy(p * 6 + 3 + j, got, got, sibling))
                sent[-1].start()
        for p in range(npc):
            for j in range(3):
                got = landed(p, _chip_of(*chips[(1, 0, 2)[j]]), 1 - c)
                copy(p * 6 + 3 + j, got, got, sibling).wait_recv()
        for j, chip in enumerate(chips):
            got = cw_rows(_chip_of(*chip))
            copy(npc * 6 + j, got, got, (*chip, c)).wait_recv()
        for cp in sent:
            cp.wait_send()
        for cp in locals_:
            cp.wait()

    vm = pl.BlockSpec(memory_space=pltpu.VMEM)
    out_shape = tuple(jax.ShapeDtypeStruct((N_CHIPS * a.shape[0],) + a.shape[1:], _MXU) for a in arrs) + (
        jax.ShapeDtypeStruct((N_CHIPS * 8, 128), F32),)
    n_rdma = npc * 6 + 3
    return pl.pallas_call(
        body, name="gather_weights", out_shape=out_shape,
        in_specs=[vm] * 4, out_specs=(pl.BlockSpec(memory_space=pl.ANY),) * n + (vm,),
        scratch_shapes=[pltpu.VMEM(a.shape, _MXU) for a in arrs] + [pltpu.SemaphoreType.DMA((n_rdma,)), pltpu.SemaphoreType.DMA((n_rdma,)),
                        pltpu.SemaphoreType.DMA((n + 1,))],
        compiler_params=pltpu.CompilerParams(vmem_limit_bytes=VMEM_LIMIT),
    )(win_t, wout, wkv, convw)


def mem_fwd(mem, mem_g, wkv, xk_g):
    def body(mem_ref, g_ref, w_ref, xk_ref, km_ref, vm_ref):
        mem_v = mem_ref[...]
        mn = mem_v * lax.rsqrt(_row_mean(mem_v * mem_v) + EPS) * g_ref[...]
        mkv = _mm(mn, w_ref[...])
        kpre = mkv[:, :XATT_W]
        gm = _group_matrix(XATT_W)
        km = kpre * lax.rsqrt(_seg_mean(kpre * kpre, gm) + EPS) * xk_ref[...]
        km_ref[...] = _head_rows(km, 0.125).astype(km_ref.dtype)
        vm_ref[...] = _head_rows(mkv[:, XATT_W:], 1.0).astype(vm_ref.dtype)

    vm = pl.BlockSpec(memory_space=pltpu.VMEM)
    rows_shape = jax.ShapeDtypeStruct((4 * MEM_LEN, XATT_W), _MXU)
    return pl.pallas_call(
        body, name="mem_fwd", out_shape=(rows_shape, rows_shape), in_specs=[vm] * 4, out_specs=(vm, vm),
    )(mem, mem_g, wkv, xk_g)


def mem_bwd(mem, mem_g, wkv, xk_g, dkm, dvm):
    def body(mem_ref, g_ref, w_ref, xk_ref, dkm_ref, dvm_ref, gw_ref, gg_ref, gxk_ref):
        mem_v = mem_ref[...]
        mh = mem_v * lax.rsqrt(_row_mean(mem_v * mem_v) + EPS)
        mn = mh * g_ref[...]
        mkv = _mm(mn, w_ref[...])
        kpre = mkv[:, :XATT_W]
        gm = _group_matrix(XATT_W)
        rk = lax.rsqrt(_seg_mean(kpre * kpre, gm) + EPS)
        kn = kpre * rk
        dk = jnp.zeros((MEM_LEN, XATT_W), F32)
        dv = jnp.zeros((MEM_LEN, XATT_W), F32)
        for j in range(4):
            mj = _lane_mask(XATT_W, HEAD * j, HEAD * (j + 1))
            dk = dk + dkm_ref[:, MEM_LEN * j:MEM_LEN * (j + 1)].T * (mj * 0.125)
            dv = dv + dvm_ref[:, MEM_LEN * j:MEM_LEN * (j + 1)].T * mj
        gxk_ref[...] = _col_sum(dk * kn)
        dkn = dk * xk_ref[...]
        dkpre = rk * (dkn - kn * _seg_mean(dkn * kn, gm))
        dmkv = jnp.concatenate([dkpre, dv], axis=1)
        gw_ref[...] = _mm_tn(mn, dmkv)
        dmn = _mm_nt(dmkv, w_ref[...])
        gg_ref[...] = _col_sum(dmn * mh)

    vm = pl.BlockSpec(memory_space=pltpu.VMEM)
    return pl.pallas_call(
        body, name="mem_bwd",
        out_shape=(jax.ShapeDtypeStruct((D_MODEL, 2 * XATT_W), F32), jax.ShapeDtypeStruct((1, D_MODEL), F32),
                   jax.ShapeDtypeStruct((1, XATT_W), F32)),
        in_specs=[vm] * 6, out_specs=(vm, vm, vm),
    )(mem, mem_g, wkv, xk_g, dkm, dvm)


def layer_fwd(x, tgt, rc, rs1, rs2, ng, win_t, cw, cb, wg, brg, big, lam, qg, kg, xqg, sinks, km, vm, og, wout):
    seq = x.shape[0]
    tm = min(ROW_TILE, seq)
    nt = seq // tm
    nb = tm // BLOCK

    def body(x_ref, t_ref, c_ref, s1_ref, s2_ref, ng_ref, win_ref, cw_ref, cb_ref, wg_ref, brg_ref, big_ref, lam_ref,
             qg_ref, kg_ref, xqg_ref, sink_ref, km_ref, vm_ref, og_ref, wout_ref,
             proj_ref, ya_ref, yb_ref, yc_ref, ycat_ref, xn_ref, dout_ref, pswa_ref, pmem_ref, psink_ref, gates_ref,
             a_ref, loss_ref,
             ext_ref, b_scr, hc_ref, kp_ref, vp_ref, lacc_ref):
        i = pl.program_id(0)

        @pl.when(i == 0)
        def _():
            ext_ref[0:8, :] = jnp.zeros((8, LRU_W), F32)
            hc_ref[...] = jnp.zeros_like(hc_ref)
            kp_ref[...] = jnp.zeros_like(kp_ref)
            vp_ref[...] = jnp.zeros_like(vp_ref)
            lacc_ref[...] = jnp.zeros_like(lacc_ref)

        xv = x_ref[...]
        xn = (xv * lax.rsqrt(_row_mean(xv * xv) + EPS) * ng_ref[...]).astype(_MXU)
        xn_ref[...] = xn.astype(xn_ref.dtype)
        proj_ref[...] = _mm_nt(xn, win_ref[...])

        u = proj_ref[:, C_LRUX:C_LRUX + LRU_W]
        ext_ref[8:8 + tm, :] = u
        xc = cb_ref[...]
        for k in range(CONV_K):
            xc = xc + cw_ref[k:k + 1, :] * ext_ref[pl.ds(5 + k, tm), :]
        ext_ref[0:8, :] = u[tm - 8:tm, :]
        rg, ig, sp, a, sq = _lru_gates(xc, wg_ref, brg_ref[...], big_ref[...], lam_ref[...])
        for k, t in enumerate((xc, rg, ig, sq)):
            gates_ref[:, LRU_W * k:LRU_W * (k + 1)] = t.astype(gates_ref.dtype)
        a_ref[...] = a
        b_scr[...] = sq * (ig * xc)
        row8 = lax.broadcasted_iota(jnp.int32, (8, LRU_W), 0)

        def scan_step(g, carry):
            r0 = pl.multiple_of(g * 8, 8)
            av = a_ref[pl.ds(r0, 8), :]
            bv = b_scr[pl.ds(r0, 8), :]
            for d in (1, 2, 4):
                a_sh = jnp.where(row8 >= d, pltpu.roll(av, d, 0), 1.0)
                b_sh = jnp.where(row8 >= d, pltpu.roll(bv, d, 0), 0.0)
                bv = bv + av * b_sh
                av = av * a_sh
            hv = bv + av * carry
            ya_ref[pl.ds(r0, 8), :] = hv
            return hv[7:8, :]

        hc_ref[0:1, :] = lax.fori_loop(0, tm // 8, scan_step, hc_ref[0:1, :], unroll=True)

        gm128 = _group_matrix(KV_W)
        cv, s1v, s2v = c_ref[...], s1_ref[...], s2_ref[...]

        def head_norm_rope(t, g):
            n = t * lax.rsqrt(_seg_mean(t * t, gm128) + EPS)
            return _rope(n * g, cv, s1v, s2v)

        qs_ = (head_norm_rope(proj_ref[:, C_SQ:C_SQ + 128], qg_ref[...]).astype(_MXU),
               head_norm_rope(proj_ref[:, C_SQ + 128:C_SQ + 256], qg_ref[...]).astype(_MXU))
        kr = head_norm_rope(proj_ref[:, C_SK:C_SK + KV_W], kg_ref[...])
        sv = proj_ref[:, C_SV:C_SV + KV_W]
        ka = _place_kv(jnp.concatenate([kp_ref[...], kr], axis=0), 0.125)
        va = _place_kv(jnp.concatenate([vp_ref[...], sv], axis=0), 1.0)
        kp_ref[...] = kr[tm - BLOCK:tm, :]
        vp_ref[...] = sv[tm - BLOCK:tm, :]
        lane128 = lax.broadcasted_iota(jnp.int32, (1, 128), 1)
        for b in range(nb):
            mask = _swa_mask((i == 0) & (b == 0)) if b == 0 else _swa_mask(False)
            band = slice(BLOCK * b, BLOCK * b + 2 * BLOCK)
            blk = slice(BLOCK * b, BLOCK * (b + 1))
            psink = jnp.zeros((BLOCK, 128), F32)
            for j in range(4):
                p, pk = _swa_probs(qs_[j // 2][blk], ka[j][band], mask, sink_ref[0, j])
                pswa_ref[blk, 2 * BLOCK * j:2 * BLOCK * (j + 1)] = p.astype(pswa_ref.dtype)
                psink = jnp.where(lane128 == j, pk, psink)
            psink_ref[blk, :] = psink
            for h in range(2):
                yb_ref[blk, KV_W * h:KV_W * (h + 1)] = _mm(
                    pswa_ref[blk, 4 * BLOCK * h:4 * BLOCK * (h + 1)],
                    jnp.concatenate([va[2 * h][band], va[2 * h + 1][band]], axis=0))

        gm256 = _group_matrix(XATT_W)
        xq = proj_ref[:, C_XQ:C_XQ + XATT_W]
        qx = xq * lax.rsqrt(_seg_mean(xq * xq, gm256) + EPS) * xqg_ref[...]
        pm = _mem_probs(_mm_nt(qx, km_ref[...]))
        for j in range(4):
            pmem_ref[:, MEM_LEN * j:MEM_LEN * (j + 1)] = pm[j].astype(pmem_ref.dtype)
        yc = _mm(pmem_ref[...], vm_ref[...])
        yc_ref[...] = yc

        def gated(y, g, gate):
            return y * lax.rsqrt(_row_mean(y * y) + EPS) * g * (gate * _sigmoid(gate))

        ogv = og_ref[...]
        za = gated(ya_ref[...], ogv[:, :512], proj_ref[:, C_LRUG:C_LRUG + LRU_W])
        zb = gated(yb_ref[...], ogv[:, 512:768], proj_ref[:, C_SWAG:C_SWAG + SWA_W])
        zc = gated(yc, ogv[:, 768:], proj_ref[:, C_XG:C_XG + XATT_W])
        ycat_ref[:, 0:512] = za.astype(ycat_ref.dtype)
        ycat_ref[:, 512:768] = zb.astype(ycat_ref.dtype)
        ycat_ref[:, 768:1024] = zc.astype(ycat_ref.dtype)
        out = xv + _mm(ycat_ref[...], wout_ref[...])
        err = out - t_ref[...]
        dout_ref[...] = (err * (1.0 / D_MODEL)).astype(dout_ref.dtype)
        lacc_ref[...] = lacc_ref[...] + (0.5 / D_MODEL) * jnp.sum(err * err)

        @pl.when(i == nt - 1)
        def _():
            loss_ref[...] = lacc_ref[...]

    def rows(ncol):
        return pl.BlockSpec((tm, ncol), lambda i: (i, 0))

    in_specs = [rows(D_MODEL), rows(D_MODEL), rows(128), rows(128), rows(128),
                _const_spec((1, D_MODEL)), _const_spec((D_IN, D_MODEL), True), _const_spec((CONV_K, LRU_W)),
                _const_spec((1, LRU_W)), _const_spec((2, 256, 512), True), _const_spec((1, LRU_W)),
                _const_spec((1, LRU_W)), _const_spec((1, LRU_W)), _const_spec((1, 128)), _const_spec((1, 128)),
                _const_spec((1, XATT_W)), pl.BlockSpec(memory_space=pltpu.SMEM),
                _const_spec((4 * MEM_LEN, XATT_W), True), _const_spec((4 * MEM_LEN, XATT_W), True),
                _const_spec((1, D_MODEL)), _const_spec((D_MODEL, D_MODEL), True)]
    out_shape = (jax.ShapeDtypeStruct((seq, D_IN), F32), jax.ShapeDtypeStruct((seq, LRU_W), F32),
                 jax.ShapeDtypeStruct((seq, SWA_W), F32), jax.ShapeDtypeStruct((seq, XATT_W), F32),
                 jax.ShapeDtypeStruct((seq, D_MODEL), _MXU), jax.ShapeDtypeStruct((seq, D_MODEL), _MXU),
                 jax.ShapeDtypeStruct((seq, D_MODEL), _MXU), jax.ShapeDtypeStruct((seq, 4 * 2 * BLOCK), _MXU),
                 jax.ShapeDtypeStruct((seq, 4 * MEM_LEN), _MXU), jax.ShapeDtypeStruct((seq, 128), F32),
                 jax.ShapeDtypeStruct((seq, 4 * LRU_W), _MXU), jax.ShapeDtypeStruct((seq, LRU_W), F32),
                 jax.ShapeDtypeStruct((8, 128), F32))
    out_specs = (rows(D_IN), rows(LRU_W), rows(SWA_W), rows(XATT_W), rows(D_MODEL), rows(D_MODEL), rows(D_MODEL),
                 rows(4 * 2 * BLOCK), rows(4 * MEM_LEN), rows(128), rows(4 * LRU_W), rows(LRU_W),
                 _const_spec((8, 128)))
    scratch = [pltpu.VMEM((tm + 8, LRU_W), F32), pltpu.VMEM((tm, LRU_W), F32),
               pltpu.VMEM((8, LRU_W), F32), pltpu.VMEM((BLOCK, KV_W), F32), pltpu.VMEM((BLOCK, KV_W), F32),
               pltpu.VMEM((8, 128), F32)]
    return pl.pallas_call(
        body, name="layer_fwd", grid=(nt,), out_shape=out_shape, in_specs=in_specs, out_specs=out_specs,
        scratch_shapes=scratch,
        compiler_params=pltpu.CompilerParams(dimension_semantics=("arbitrary",), vmem_limit_bytes=VMEM_LIMIT),
    )(x, tgt, rc, rs1, rs2, ng, win_t, cw, cb, wg, brg, big, lam, qg, kg, xqg, sinks, km, vm, og, wout)


def wgrad_reduce(lhs, rhs, bigs, g_small, stage_at, name):
    seq, ncol = rhs.shape
    nblk = lhs.shape[1] // 256
    nres = len(bigs) + (g_small is not None)

    def body(l_ref, r_ref, *refs):
        if nres:
            _hosted_reduce(pl.program_id(0), stage_at, refs[:nres], refs[nres + 1:2 * nres + 1], refs[2 * nres + 1:],
                           g_small is not None)
        refs[nres][...] = _mm_tn(l_ref[...], r_ref[...])

    red_shape, scratch = _hosted_reduce_shapes(bigs, g_small) if nres else ([], [])
    vm = pl.BlockSpec(memory_space=pltpu.VMEM)
    hbm = pl.BlockSpec(memory_space=pl.ANY)
    operands = list(bigs) + ([] if g_small is None else [g_small])
    return pl.pallas_call(
        body, name=name, grid=(nblk,),
        out_shape=(jax.ShapeDtypeStruct((lhs.shape[1], ncol), F32), *red_shape),
        in_specs=[pl.BlockSpec((seq, 256), lambda j: (0, j)), _const_spec((seq, ncol), True)] + [hbm] * len(bigs)
        + [vm] * (g_small is not None),
        out_specs=(pl.BlockSpec((256, ncol), lambda j: (j, 0)),) + (hbm,) * len(red_shape),
        scratch_shapes=scratch,
        compiler_params=pltpu.CompilerParams(dimension_semantics=("arbitrary",), vmem_limit_bytes=VMEM_LIMIT),
    )(lhs, rhs, *operands)


def layer_bwd(x, dout, proj, ya, yb, yc, pswa, pmem, psink, gates, a_all, rc, rs1, rs2, ng, win_t, cw, wg, lam, qg, kg,
              xqg, km, vm, og, wout):
    seq = x.shape[0]
    tm = min(ROW_TILE, seq)
    nt = seq // tm
    nb = tm // BLOCK

    def body(x_ref, dout_ref, proj_ref, ya_ref, yb_ref, yc_ref, pswa_ref, pmem_ref, psink_ref, gates_ref, a_ref,
             c_ref, s1_ref, s2_ref,
             yah_ref, kvh_ref, ch_ref, s1h_ref, s2h_ref,
             ng_ref, win_ref, cw_ref, wg_ref, lam_ref, qg_ref, kg_ref, xqg_ref, km_ref, vm_ref, og_ref, wout_ref,
             gx_ref, dproj_ref, gwg_ref, dkm_ref, dvm_ref, gng_ref, gog_ref, gcb_ref, gbrg_ref, gbig_ref, glam_ref,
             gcw_ref, gqn_ref, gkn_ref, gxqn_ref, gsink_ref,
             hext_ref, aext_ref, an_scr, dh_scr, g_scr, dxc_ext, gcar_ref, dkcar_ref, dvcar_ref):
        i = pl.program_id(0)
        tile = nt - 1 - i
        first_tile = tile == 0

        @pl.when(i == 0)
        def _():
            for r in (gwg_ref, dkm_ref, dvm_ref, gng_ref, gog_ref, gcb_ref, gbrg_ref, gbig_ref, glam_ref, gcw_ref,
                      gqn_ref, gkn_ref, gxqn_ref, gsink_ref, gcar_ref, dkcar_ref, dvcar_ref):
                r[...] = jnp.zeros_like(r)
            dxc_ext[tm:tm + 8, :] = jnp.zeros((8, LRU_W), F32)
            aext_ref[tm:tm + 8, :] = jnp.zeros((8, LRU_W), F32)

        xv = x_ref[...]
        dov = dout_ref[...]
        dz = _mm_nt(dov, wout_ref[...])
        ogv = og_ref[...]

        def group_bwd(y, gate, g, dzg):
            r = lax.rsqrt(_row_mean(y * y) + EPS)
            n = y * r
            sg = _sigmoid(gate)
            dgate = dzg * (n * g) * (sg * (1.0 + gate * (1.0 - sg)))
            dng = dzg * (gate * sg)
            dn = dng * g
            return r * (dn - n * _row_mean(dn * n)), dgate, _col_sum(dng * n)

        dya, dga, goa = group_bwd(ya_ref[...], proj_ref[:, C_LRUG:C_LRUG + LRU_W], ogv[:, :512], dz[:, :512])
        dyb, dgb, gob = group_bwd(yb_ref[...], proj_ref[:, C_SWAG:C_SWAG + SWA_W], ogv[:, 512:768], dz[:, 512:768])
        dyc, dgc, goc = group_bwd(yc_ref[...], proj_ref[:, C_XG:C_XG + XATT_W], ogv[:, 768:], dz[:, 768:])
        gog_ref[...] += jnp.concatenate([goa, gob, goc], axis=1)
        dproj_ref[:, C_LRUG:C_LRUG + LRU_W] = dga.astype(dproj_ref.dtype)
        dproj_ref[:, C_SWAG:C_SWAG + SWA_W] = dgb.astype(dproj_ref.dtype)
        dproj_ref[:, C_XG:C_XG + XATT_W] = dgc.astype(dproj_ref.dtype)

        gm256 = _group_matrix(XATT_W)
        xq = proj_ref[:, C_XQ:C_XQ + XATT_W]
        rq = lax.rsqrt(_seg_mean(xq * xq, gm256) + EPS)
        qn = xq * rq
        qx = qn * xqg_ref[...]
        qxb = qx.astype(_MXU)
        dycb = dyc.astype(_MXU)
        dp_all = _mm_nt(dycb, vm_ref[...])
        dsm = []
        for j in range(4):
            pj = pmem_ref[:, MEM_LEN * j:MEM_LEN * (j + 1)].astype(F32)
            dp = dp_all[:, MEM_LEN * j:MEM_LEN * (j + 1)]
            dsm.append((pj * (dp - jnp.sum(pj * dp, axis=-1, keepdims=True))).astype(_MXU))
        ds_all = jnp.concatenate(dsm, axis=1)
        dvm_ref[...] += _mm_tn(dycb, pmem_ref[...])
        dkm_ref[...] += _mm_tn(qxb, ds_all)
        dqx = _mm(ds_all, km_ref[...])
        gxqn_ref[...] += _col_sum(dqx * qn)
        dqn = dqx * xqg_ref[...]
        dproj_ref[:, C_XQ:C_XQ + XATT_W] = (rq * (dqn - qn * _seg_mean(dqn * qn, gm256))).astype(dproj_ref.dtype)

        gm128 = _group_matrix(KV_W)
        cv, s1v, s2v = c_ref[...], s1_ref[...], s2_ref[...]

        def head_norm(t):
            r = lax.rsqrt(_seg_mean(t * t, gm128) + EPS)
            return t * r, r

        qn_, qr_ = zip(head_norm(proj_ref[:, C_SQ:C_SQ + 128]), head_norm(proj_ref[:, C_SQ + 128:C_SQ + 256]))
        qrope = [_rope(qn_[h] * qg_ref[...], cv, s1v, s2v).astype(_MXU) for h in range(2)]
        kn, krr = head_norm(proj_ref[:, C_SK:C_SK + KV_W])
        kr = _rope(kn * kg_ref[...], cv, s1v, s2v)
        khn, _ = head_norm(kvh_ref[:, 0:KV_W])
        khr = _rope(khn * kg_ref[...], ch_ref[...], s1h_ref[...], s2h_ref[...])
        ka = _place_kv(jnp.concatenate([khr, kr], axis=0), 0.125)
        va = _place_kv(jnp.concatenate([kvh_ref[:, KV_W:2 * KV_W], proj_ref[:, C_SV:C_SV + KV_W]], axis=0), 1.0)
        lane128 = lax.broadcasted_iota(jnp.int32, (1, 128), 1)
        gsink = jnp.zeros((1, 128), F32)
        dk_band, dv_band, dq_blk = [], [], []
        for b in range(nb):
            band = slice(BLOCK * b, BLOCK * b + 2 * BLOCK)
            blk = slice(BLOCK * b, BLOCK * (b + 1))
            dka, dva, dsb = [], [], []
            deltas = jnp.zeros((BLOCK, 128), F32)
            for j in range(4):
                qh = qrope[j // 2][blk]
                doh = dyb[blk, KV_W * (j // 2):KV_W * (j // 2 + 1)].astype(_MXU)
                pb = pswa_ref[blk, 2 * BLOCK * j:2 * BLOCK * (j + 1)]
                p = pb.astype(F32)
                dp = _mm_nt(doh, va[j][band])
                delta = jnp.sum(p * dp, axis=-1, keepdims=True)
                ds = (p * (dp - delta)).astype(_MXU)
                deltas = jnp.where(lane128 == j, delta, deltas)
                dva.append(_mm_tn(pb, doh))
                dka.append(_mm_tn(ds, qh))
                dsb.append(ds)
            gsink = gsink - _col_sum(psink_ref[blk, :] * deltas)
            dk_band.append(_unplace_kv(dka) * 0.125)
            dv_band.append(_unplace_kv(dva))
            dq_blk.append([_mm(jnp.concatenate(dsb[2 * h:2 * h + 2], axis=1),
                               jnp.concatenate([ka[2 * h][band], ka[2 * h + 1][band]], axis=0)) for h in range(2)])
        gsink_ref[...] += gsink
        dk_rows = [dk_band[b][BLOCK:] + (dk_band[b + 1][:BLOCK] if b + 1 < nb else dkcar_ref[...]) for b in range(nb)]
        dv_rows = [dv_band[b][BLOCK:] + (dv_band[b + 1][:BLOCK] if b + 1 < nb else dvcar_ref[...]) for b in range(nb)]
        dkcar_ref[...] = dk_band[0][:BLOCK]
        dvcar_ref[...] = dv_band[0][:BLOCK]
        dkg = _rope_bwd(jnp.concatenate(dk_rows, axis=0), cv, s1v, s2v)
        gkn = _col_sum(dkg * kn)
        dkn = dkg * kg_ref[...]
        dproj_ref[:, C_SK:C_SK + KV_W] = (krr * (dkn - kn * _seg_mean(dkn * kn, gm128))).astype(dproj_ref.dtype)
        dproj_ref[:, C_SV:C_SV + KV_W] = jnp.concatenate(dv_rows, axis=0).astype(dproj_ref.dtype)
        gqn = jnp.zeros((1, 128), F32)
        for h in range(2):
            dqg = _rope_bwd(jnp.concatenate([dq_blk[b][h] for b in range(nb)], axis=0), cv, s1v, s2v)
            gqn = gqn + _col_sum(dqg * qn_[h])
            dqn_ = dqg * qg_ref[...]
            dproj_ref[:, C_SQ + 128 * h:C_SQ + 128 * (h + 1)] = (
                qr_[h] * (dqn_ - qn_[h] * _seg_mean(dqn_ * qn_[h], gm128))).astype(dproj_ref.dtype)
        gqn_ref[...] += gqn
        gkn_ref[...] += gkn

        u = proj_ref[:, C_LRUX:C_LRUX + LRU_W]
        xc, rg, ig, sq = (gates_ref[:, LRU_W * k:LRU_W * (k + 1)].astype(F32) for k in range(4))
        a = a_ref[...]
        sp = _softplus(-lam_ref[...])
        hext_ref[0:8, :] = jnp.where(first_tile, 0.0, yah_ref[...])
        hext_ref[8:8 + tm, :] = ya_ref[...]
        hprev = hext_ref[pl.ds(7, tm), :]
        aext_ref[0:tm, :] = a
        an_scr[...] = aext_ref[pl.ds(1, tm), :]
        dh_scr[...] = dya
        dh_scr[tm - 1:tm, :] = dh_scr[tm - 1:tm, :] + gcar_ref[0:1, :]
        row8 = lax.broadcasted_iota(jnp.int32, (8, LRU_W), 0)

        def scan_step(gi, carry):
            r0 = pl.multiple_of((tm // 8 - 1 - gi) * 8, 8)
            av = an_scr[pl.ds(r0, 8), :]
            bv = dh_scr[pl.ds(r0, 8), :]
            for d in (1, 2, 4):
                a_sh = jnp.where(row8 < 8 - d, pltpu.roll(av, 8 - d, 0), 1.0)
                b_sh = jnp.where(row8 < 8 - d, pltpu.roll(bv, 8 - d, 0), 0.0)
                bv = bv + av * b_sh
                av = av * a_sh
            gv = bv + av * carry
            g_scr[pl.ds(r0, 8), :] = gv
            return gv[0:1, :]

        g0 = lax.fori_loop(0, tm // 8, scan_step, jnp.zeros((1, LRU_W), F32), unroll=True)
        gcar_ref[0:1, :] = a[0:1, :] * g0
        gv = g_scr[...]
        da = gv * hprev
        dig = gv * sq * xc
        dxc = gv * sq * ig
        dla = da * a - gv * (ig * xc) * ((a * a) / sq)
        drg = dla * ((-LRU_C) * sp)
        glam_ref[...] += _col_sum(dla * rg)
        dpr = drg * rg * (1.0 - rg)
        dpi = dig * ig * (1.0 - ig)
        gbrg_ref[...] += _col_sum(dpr)
        gbig_ref[...] += _col_sum(dpi)
        dpre0 = jnp.concatenate([dpr[:, :256], dpi[:, :256]], axis=1).astype(_MXU)
        dpre1 = jnp.concatenate([dpr[:, 256:], dpi[:, 256:]], axis=1).astype(_MXU)
        gwg_ref[0] += _mm_tn(xc[:, :256], dpre0)
        gwg_ref[1] += _mm_tn(xc[:, 256:], dpre1)
        dxc = dxc + jnp.concatenate([_mm_nt(dpre0, wg_ref[0]), _mm_nt(dpre1, wg_ref[1])], axis=1)
        gcb_ref[...] += _col_sum(dxc)
        dxc_ext[0:tm, :] = dxc
        du = jnp.zeros((tm, LRU_W), F32)
        for k in range(CONV_K):
            later = dxc_ext[pl.ds(3 - k, tm), :]
            gcw_ref[k:k + 1, :] += _col_sum(later * u)
            du = du + cw_ref[k:k + 1, :] * later
        dxc_ext[tm:tm + 8, :] = dxc[0:8, :]
        dproj_ref[:, C_LRUX:C_LRUX + LRU_W] = du.astype(dproj_ref.dtype)

        dxn = _mm(dproj_ref[...], win_ref[...])
        rx = lax.rsqrt(_row_mean(xv * xv) + EPS)
        xh = xv * rx
        gng_ref[...] += _col_sum(dxn * xh)
        dxh = dxn * ng_ref[...]
        gx_ref[...] = dov.astype(F32) + rx * (dxh - xh * _row_mean(dxh * xh))

        @pl.when(i == nt - 1)
        def _():
            glam_ref[...] = glam_ref[...] * (LRU_C * _sigmoid(-lam_ref[...]))

    def rows(ncol, arr_cols_block=0):
        return pl.BlockSpec((tm, ncol), lambda i: (nt - 1 - i, arr_cols_block))

    def halo(nrow, ncol, colblk=0):
        per = tm // nrow
        return pl.BlockSpec((nrow, ncol), lambda i: (jnp.maximum((nt - 1 - i) * per - 1, 0), colblk))

    in_specs = [rows(D_MODEL), rows(D_MODEL), rows(D_IN), rows(LRU_W), rows(SWA_W), rows(XATT_W),
                rows(4 * 2 * BLOCK), rows(4 * MEM_LEN), rows(128), rows(4 * LRU_W), rows(LRU_W),
                rows(128), rows(128), rows(128),
                halo(8, LRU_W), halo(BLOCK, 2 * KV_W, C_SK // (2 * KV_W)),
                halo(BLOCK, 128), halo(BLOCK, 128), halo(BLOCK, 128),
                _const_spec((1, D_MODEL)), _const_spec((D_IN, D_MODEL), True), _const_spec((CONV_K, LRU_W)),
                _const_spec((2, 256, 512), True), _const_spec((1, LRU_W)), _const_spec((1, 128)), _const_spec((1, 128)),
                _const_spec((1, XATT_W)),
                _const_spec((4 * MEM_LEN, XATT_W), True), _const_spec((4 * MEM_LEN, XATT_W), True),
                _const_spec((1, D_MODEL)), _const_spec((D_MODEL, D_MODEL), True)]
    small = [(2, 256, 512), (XATT_W, 4 * MEM_LEN), (XATT_W, 4 * MEM_LEN), (1, D_MODEL), (1, D_MODEL), (1, LRU_W), (1, LRU_W),
             (1, LRU_W), (1, LRU_W), (CONV_K, LRU_W), (1, 128), (1, 128), (1, XATT_W), (1, 128)]
    out_shape = (jax.ShapeDtypeStruct((seq, D_MODEL), F32), jax.ShapeDtypeStruct((seq, D_IN), _MXU)) + tuple(
        jax.ShapeDtypeStruct(s, F32) for s in small)
    out_specs = (rows(D_MODEL), rows(D_IN)) + tuple(_const_spec(s) for s in small)
    scratch = [pltpu.VMEM((tm + 8, LRU_W), F32), pltpu.VMEM((tm + 8, LRU_W), F32),
               pltpu.VMEM((tm, LRU_W), F32), pltpu.VMEM((tm, LRU_W), F32), pltpu.VMEM((tm, LRU_W), F32),
               pltpu.VMEM((tm + 8, LRU_W), F32),
               pltpu.VMEM((8, LRU_W), F32), pltpu.VMEM((BLOCK, KV_W), F32), pltpu.VMEM((BLOCK, KV_W), F32)]
    return pl.pallas_call(
        body, name="layer_bwd", grid=(nt,), out_shape=out_shape, in_specs=in_specs, out_specs=out_specs,
        scratch_shapes=scratch,
        compiler_params=pltpu.CompilerParams(dimension_semantics=("arbitrary",), vmem_limit_bytes=VMEM_LIMIT),
    )(x, dout, proj, ya, yb, yc, pswa, pmem, psink, gates, a_all, rc, rs1, rs2, ya, proj, rc, rs1, rs2,
      ng, win_t, cw, wg, lam, qg, kg, xqg, km, vm, og, wout)


def _reduce_protocol(big, sm, outs, osm, r1, r1s, wire, r2, r2s, wire2, ps, own, send, recv, lsem):
    nbig = len(big)
    x, y, c = lax.axis_index("x"), lax.axis_index("y"), lax.axis_index("c")
    sibling = (x, y, 1 - c)
    near, far, diag = _partners(x, y, c)
    me, near_id, far_id, diag_id = _chip_of(x, y), _chip_of(*near), _chip_of(*far), _chip_of(*diag)

    def copy(k, src, dst, to):
        return pltpu.make_async_remote_copy(src_ref=src, dst_ref=dst, send_sem=send.at[k], recv_sem=recv.at[k],
                                            device_id=to, device_id_type=MESH)

    def sent(stage, a):
        if a == nbig:
            src, dst, to = ((sm.at[1 - c], r1s, sibling), (r1s, r2s.at[0], (*near, c)), (ps, r2s.at[1], (*far, c)),
                            (osm.at[c], osm.at[c], sibling))[stage]
            return [copy(5 * nbig + stage, src, dst, to)]
        if stage == 0:
            return [copy(5 * a, big[a].at[:, 1 - c], r1[a], sibling)]
        if stage == 1:
            return [copy(5 * a + 1, wire[a].at[near_id], r2[a].at[0], (*near, c)),
                    copy(5 * a + 2, wire[a].at[diag_id], r2[a].at[1], (*near, c))]
        if stage == 2:
            return [copy(5 * a + 3, wire2[a], r2[a].at[2], (*far, c))]
        return [copy(5 * a + 4, outs[a].at[c], outs[a].at[c], sibling)]

    arrays = range(nbig + (sm is not None))

    def start(stage, a):
        for cp in sent(stage, a):
            cp.start()

    def arrived(k, ref):
        copy(k, ref, ref, sibling).wait_recv()

    def loads():
        return [pltpu.make_async_copy(big[a].at[:, c], own[a], lsem.at[a]) for a in range(nbig)]

    def stage0():
        for a in arrays:
            start(0, a)
        for cp in loads():
            cp.start()

    def stage1():
        for a in range(nbig):
            loads()[a].wait()
            arrived(5 * a, r1[a])
            for k in range(N_CHIPS):
                r1[a][k] = own[a][k] + r1[a][k]
                wire[a][k] = r1[a][k].astype(wire[a].dtype)
            start(1, a)
        if sm is not None:
            arrived(5 * nbig, r1s)
            r1s[...] = sm[c] + r1s[...]
            start(1, nbig)

    def stage2():
        for a in range(nbig):
            arrived(5 * a + 1, r2[a].at[0])
            arrived(5 * a + 2, r2[a].at[1])
            r1[a][me] = r1[a][me] + r2[a][0].astype(F32)
            wire2[a][...] = (r1[a][far_id] + r2[a][1].astype(F32)).astype(wire2[a].dtype)
            start(2, a)
        if sm is not None:
            arrived(5 * nbig + 1, r2s.at[0])
            ps[...] = r1s[...] + r2s[0]
            start(2, nbig)

    def stage3():
        for a in range(nbig):
            arrived(5 * a + 3, r2[a].at[2])
            outs[a][c] = r1[a][me] + r2[a][2].astype(F32)
            start(3, a)
        if sm is not None:
            arrived(5 * nbig + 2, r2s.at[1])
            osm[c] = ps[...] + r2s[1]
            start(3, nbig)

    def stage4():
        for a in range(nbig):
            arrived(5 * a + 4, outs[a].at[1 - c])
        if sm is not None:
            arrived(5 * nbig + 3, osm.at[1 - c])
        for stage in range(4):
            for a in arrays:
                for cp in sent(stage, a):
                    cp.wait_send()

    return [stage0, stage1, stage2, stage3, stage4]


def _reduce_buffers(bigs, g_small):
    half = [b.shape[2:] for b in bigs]
    sm_half = None if g_small is None else g_small.shape[1:]
    out_shape = [jax.ShapeDtypeStruct((2,) + h, F32) for h in half]
    small = lambda lead: [] if g_small is None else [pltpu.VMEM(lead + sm_half, F32)]
    if g_small is not None:
        out_shape.append(jax.ShapeDtypeStruct(g_small.shape, F32))
    n_sem = 5 * len(bigs) + 4
    scratch = ([pltpu.VMEM((N_CHIPS,) + h, F32) for h in half] + small(())
               + [pltpu.VMEM((N_CHIPS,) + h, _WIRE) for h in half]
               + [pltpu.VMEM((3,) + h, _WIRE) for h in half] + small((2,))
               + [pltpu.VMEM(h, _WIRE) for h in half] + small(())
               + [pltpu.VMEM((N_CHIPS,) + h, F32) for h in half]
               + [pltpu.SemaphoreType.DMA((n_sem,)), pltpu.SemaphoreType.DMA((n_sem,)),
                  pltpu.SemaphoreType.DMA((len(bigs),))])
    return out_shape, scratch


def _split_reduce_refs(refs, nbig, has_small):
    it = iter(refs)
    take = lambda n: [next(it) for _ in range(n)]
    one = lambda: next(it) if has_small else None
    big, sm = take(nbig), one()
    outs, osm = take(nbig), one()
    r1, r1s, wire, r2, r2s, wire2, ps, own = take(nbig), one(), take(nbig), take(nbig), one(), take(nbig), one(), take(nbig)
    send, recv, lsem = take(3)
    return big, sm, outs, osm, r1, r1s, wire, r2, r2s, wire2, ps, own, send, recv, lsem


def _hosted_reduce_shapes(bigs, g_small):
    red_shape, scratch = _reduce_buffers(bigs, g_small)
    nres = len(red_shape)
    return red_shape, [pltpu.VMEM(r.shape, r.dtype) for r in red_shape] + scratch + [pltpu.SemaphoreType.DMA((nres,))]


def _hosted_reduce(step, stage_at, operands, results, scratch, has_small):
    nres = len(results)
    sums, rest, fsem = scratch[:nres], scratch[nres:-1], scratch[-1]
    refs = tuple(operands) + tuple(sums) + tuple(rest)
    for at, stage in zip(stage_at, _reduce_protocol(*_split_reduce_refs(refs, nres - has_small, has_small))):
        pl.when(step == at)(stage)

    @pl.when(step == stage_at[-1])
    def _():
        out = [pltpu.make_async_copy(sums[k], results[k], fsem.at[k]) for k in range(nres)]
        for cp in out:
            cp.start()
        for cp in out:
            cp.wait()


def reduce_grads(big, name, parts):
    chips, halves, rows_, cols = big.shape
    sub = jax.ShapeDtypeStruct((chips, halves, rows_ // parts, cols), big.dtype)

    def body(b_ref, o_ref, *scratch):
        refs = [b_ref.at[:, :, s] for s in range(parts)] + [o_ref.at[:, s] for s in range(parts)] + list(scratch)
        for stage in _reduce_protocol(*_split_reduce_refs(refs, parts, False)):
            stage()

    _, scratch = _reduce_buffers([sub] * parts, None)
    return pl.pallas_call(
        body, name=name, out_shape=jax.ShapeDtypeStruct((halves, parts, rows_ // parts, cols), F32),
        in_specs=[pl.BlockSpec(memory_space=pl.ANY)], out_specs=pl.BlockSpec(memory_space=pltpu.VMEM),
        scratch_shapes=scratch, compiler_params=pltpu.CompilerParams(vmem_limit_bytes=VMEM_LIMIT),
    )(big.reshape(chips, halves, parts, rows_ // parts, cols))


def adamw(w, g, m, v, name):
    rows_, cols = w.shape
    tr = max(t for t in range(8, rows_ + 1, 8) if rows_ % t == 0 and t * cols * 4 <= ADAM_BLOCK_BYTES)

    def body(w_ref, g_ref, m_ref, v_ref, d_ref, nm_ref, nv_ref):
        d_ref[...], nm_ref[...], nv_ref[...] = _adam_update(w_ref[...], g_ref[...], m_ref[...], v_ref[...])

    spec = pl.BlockSpec((tr, cols), lambda i: (i, 0))
    shp = jax.ShapeDtypeStruct(w.shape, F32)
    return pl.pallas_call(
        body, name=name, grid=(rows_ // tr,), out_shape=(shp, shp, shp), in_specs=[spec] * 4, out_specs=(spec,) * 3,
        compiler_params=pltpu.CompilerParams(dimension_semantics=("arbitrary",)),
    )(w, g, m, v)


def _adam_update(w, g, m, v):
    nm = ADAM_B1 * m + (1.0 - ADAM_B1) * g
    nv = ADAM_B2 * v + (1.0 - ADAM_B2) * (g * g)
    m_hat = nm / (1.0 - ADAM_B1 ** ADAM_STEP)
    v_hat = nv / (1.0 - ADAM_B2 ** ADAM_STEP)
    return (-ADAM_LR) * (m_hat / (jnp.sqrt(v_hat) + ADAM_EPS) + ADAM_WD * w), nm, nv


def adamw_vectors(g_pack, g_mats, ws, ms, vs):
    nvec, nmat = len(SMALL_VECTORS), len(g_mats)
    n = nvec + nmat

    def body(*refs):
        pk = refs[0]
        gm_refs = refs[1:1 + nmat]
        w_refs, m_refs, v_refs = (refs[1 + nmat + k * n:1 + nmat + (k + 1) * n] for k in range(3))
        outs = refs[1 + nmat + 3 * n:]
        g_out, d_out, nm_out, nv_out = outs[:nvec], outs[nvec:nvec + n], outs[nvec + n:nvec + 2 * n], outs[nvec + 2 * n:]
        chip = 2 * lax.axis_index("x") + lax.axis_index("y")
        for k, (name, row, width) in enumerate(SMALL_VECTORS):
            if name == "conv_w":
                g = jnp.concatenate([pk[pl.ds(row + 4 * t + chip, 1), :] for t in range(CONV_K)], axis=0)[None]
            elif width >= 128:
                g = jnp.concatenate([pk[row + r:row + r + 1, :] for r in range(width // 128)], axis=1)
            else:
                g = pk[row:row + 1, 0:width]
            g_out[k][...] = g
            d_out[k][...], nm_out[k][...], nv_out[k][...] = _adam_update(w_refs[k][...], g, m_refs[k][...], v_refs[k][...])
        for k in range(nvec, n):
            d_out[k][...], nm_out[k][...], nv_out[k][...] = _adam_update(
                w_refs[k][...], gm_refs[k - nvec][...], m_refs[k][...], v_refs[k][...])

    vm = pl.BlockSpec(memory_space=pltpu.VMEM)
    like = [jax.ShapeDtypeStruct(w.shape, F32) for w in ws]
    out_shape = like[:nvec] + like * 3
    return pl.pallas_call(
        body, name="adamw_vectors", out_shape=tuple(out_shape), in_specs=[vm] * (1 + nmat + 3 * n),
        out_specs=(vm,) * len(out_shape),
    )(g_pack, *g_mats, *ws, *ms, *vs)


SMALL_VECTORS = (("norm_g", 0, 1024), ("mem_norm_g", 8, 1024), ("conv_w", 16, 512), ("conv_b", 32, 512),
                 ("b_rg", 292, 512), ("b_ig", 552, 512), ("lru_lambda", 556, 512), ("q_norm_g", 560, 64),
                 ("k_norm_g", 561, 64), ("sinks", 562, 4), ("xq_norm_g", 563, 64), ("xk_norm_g", 564, 64),
                 ("out_norm_g", 565, 1024))
SMALL_MATRICES = (("w_rg", 36), ("w_ig", 296))
LOSS_ROW = 573
SMALL_ROWS = 576


def _pack(parts, rows_):
    flat = jnp.concatenate([p.reshape(-1) for p in parts])
    return jnp.pad(flat, (0, rows_ * 128 - flat.shape[0])).reshape(rows_, 128)


def _pad_to(v, n):
    v = v.reshape(-1)
    return jnp.pad(v, (0, n - v.shape[0]))


def _block_diag_gates(w_rg, w_ig):
    eye = jnp.eye(4, dtype=w_rg.dtype)

    def bd(w4):
        return (w4[:, :, None, :] * eye[:, None, :, None]).reshape(256, 256)

    return jnp.stack([jnp.concatenate([bd(w_rg[4 * h:4 * h + 4]), bd(w_ig[4 * h:4 * h + 4])], axis=1) for h in (0, 1)])


def _diag_blocks(g):
    g6 = g.reshape(2, 4, HEAD, 2, 4, HEAD)
    d = (g6 * jnp.eye(4, dtype=g.dtype)[None, :, None, None, :, None]).sum(axis=4)
    return d[:, :, :, 0].reshape(8, HEAD, HEAD), d[:, :, :, 1].reshape(8, HEAD, HEAD)


def _rope_tables(seq):
    pos = np.arange(seq, dtype=np.float32)
    inv_freq = (np.float32(ROPE_THETA) ** (-(np.arange(0, ROPE_DIM, 2, dtype=np.float32) / np.float32(ROPE_DIM)))
                ).astype(np.float32)
    ang = (pos[:, None] * inv_freq[None, :]).astype(np.float32)
    cos, sin = np.cos(ang).astype(np.float32), np.sin(ang).astype(np.float32)
    z = lambda n: np.zeros((seq, n), np.float32)
    c64 = np.concatenate([cos, cos, np.ones((seq, HEAD - ROPE_DIM), np.float32)], axis=1)
    s1_64 = np.concatenate([-sin, z(HEAD - 8)], axis=1)
    s2_64 = np.concatenate([z(8), sin, z(HEAD - ROPE_DIM)], axis=1)
    return tuple(jnp.asarray(np.concatenate([t, t], axis=1)) for t in (c64, s1_64, s2_64))


def kernel(x, mem, norm_g, mem_norm_g, w_in, conv_w, conv_b, w_rg, b_rg, w_ig, b_ig, lru_lambda, q_norm_g, k_norm_g, sinks, w_mem_kv, xq_norm_g, xk_norm_g, out_norm_g, w_out, loss_target, m_norm_g, m_mem_norm_g, m_w_in, m_conv_w, m_conv_b, m_w_rg, m_b_rg, m_w_ig, m_b_ig, m_lru_lambda, m_q_norm_g, m_k_norm_g, m_sinks, m_w_mem_kv, m_xq_norm_g, m_xk_norm_g, m_out_norm_g, m_w_out, v_norm_g, v_mem_norm_g, v_w_in, v_conv_w, v_conv_b, v_w_rg, v_b_rg, v_w_ig, v_b_ig, v_lru_lambda, v_q_norm_g, v_k_norm_g, v_sinks, v_w_mem_kv, v_xq_norm_g, v_xk_norm_g, v_out_norm_g, v_w_out):
    seq = x.shape[1]
    xs, tgt, mems = x[0], loss_target[0], mem[0]

    cw_sh = jnp.pad(conv_w[0], ((0, 4), (0, 0)))
    win_t, wout, wkv, cw_all = gather_weights(w_in[0].T, w_out[0], w_mem_kv[0], cw_sh)
    cw = cw_all.reshape(N_CHIPS, 8, 128)[:, :CONV_K].transpose(1, 0, 2).reshape(CONV_K, LRU_W)

    rc, rs1, rs2 = _rope_tables(seq)
    wg = _block_diag_gates(w_rg[0], w_ig[0]).astype(_MXU)
    qg = jnp.tile(q_norm_g, (1, 2))
    kg = jnp.tile(k_norm_g, (1, 2))
    xqg = jnp.tile(xq_norm_g, (1, 4))
    xkg = jnp.tile(xk_norm_g, (1, 4))

    km, vm = mem_fwd(mems, mem_norm_g, wkv, xkg)
    proj, ya, yb, yc, ycat, xn, dout, pswa, pmem, psink, gates, a_all, loss8 = layer_fwd(
        xs, tgt, rc, rs1, rs2, norm_g, win_t, cw, conv_b, wg, b_rg, b_ig, lru_lambda, qg, kg, xqg, sinks, km, vm,
        out_norm_g, wout)
    (g_wout,) = wgrad_reduce(ycat, dout, [], None, (), "wgrad_out")
    (gx, dproj, g_wg, dkm, dvm, g_ng, g_og, g_cb, g_brg, g_big, g_lam, g_cw, g_qn, g_kn, g_xqn, g_sink) = layer_bwd(
        xs, dout, proj, ya, yb, yc, pswa, pmem, psink, gates, a_all, rc, rs1, rs2, norm_g, win_t, cw, wg, lru_lambda, qg,
        kg, xqg, km, vm, out_norm_g, wout)
    g_wkv, g_mng, g_xkn = mem_bwd(mems, mem_norm_g, wkv, xkg, dkm, dvm)

    g_wrg, g_wig = _diag_blocks(g_wg)
    fold = lambda v, n: v.reshape(n, HEAD).sum(axis=0)
    small_g = _pack([g_ng, g_mng, g_cw, g_cb, g_wrg, g_brg, g_wig, g_big, g_lam, _pad_to(fold(g_qn, 2), 128),
                     _pad_to(fold(g_kn, 2), 128), g_sink, _pad_to(fold(g_xqn, 4), 128), _pad_to(fold(g_xkn, 4), 128),
                     g_og, loss8[0:1]], SMALL_ROWS)
    early = [g_wout.reshape(N_CHIPS, 2, D_MODEL // 8, D_MODEL), g_wkv.reshape(N_CHIPS, 2, D_MODEL // 8, 2 * XATT_W)]
    g_win_t, r_out, r_kv, r_small = wgrad_reduce(dproj, xn, early, small_g.reshape(2, SMALL_ROWS // 2, 128),
                                                 (0, 2, 5, 7, 8), "wgrad_in")
    r_in = reduce_grads(g_win_t.reshape(N_CHIPS, 2, D_IN // 8, D_MODEL), "reduce_w_in", 6)

    r_small = r_small.reshape(SMALL_ROWS, 128)
    loss = r_small[LOSS_ROW, 0]
    grads = {"w_in": r_in.reshape(D_IN // 4, D_MODEL).T[None], "w_mem_kv": r_kv.reshape(D_MODEL // 4, 2 * XATT_W)[None],
             "w_out": r_out.reshape(D_MODEL // 4, D_MODEL)[None]}
    for name, row in SMALL_MATRICES:
        grads[name] = r_small[row:row + 256].reshape(1, LRU_BLOCKS, HEAD, HEAD)
    weights = dict(norm_g=norm_g, mem_norm_g=mem_norm_g, w_in=w_in, conv_w=conv_w, conv_b=conv_b, w_rg=w_rg, b_rg=b_rg,
                   w_ig=w_ig, b_ig=b_ig, lru_lambda=lru_lambda, q_norm_g=q_norm_g, k_norm_g=k_norm_g, sinks=sinks,
                   w_mem_kv=w_mem_kv, xq_norm_g=xq_norm_g, xk_norm_g=xk_norm_g, out_norm_g=out_norm_g, w_out=w_out)
    ms = dict(norm_g=m_norm_g, mem_norm_g=m_mem_norm_g, w_in=m_w_in, conv_w=m_conv_w, conv_b=m_conv_b, w_rg=m_w_rg,
              b_rg=m_b_rg, w_ig=m_w_ig, b_ig=m_b_ig, lru_lambda=m_lru_lambda, q_norm_g=m_q_norm_g, k_norm_g=m_k_norm_g,
              sinks=m_sinks, w_mem_kv=m_w_mem_kv, xq_norm_g=m_xq_norm_g, xk_norm_g=m_xk_norm_g,
              out_norm_g=m_out_norm_g, w_out=m_w_out)
    vs = dict(norm_g=v_norm_g, mem_norm_g=v_mem_norm_g, w_in=v_w_in, conv_w=v_conv_w, conv_b=v_conv_b, w_rg=v_w_rg,
              b_rg=v_b_rg, w_ig=v_w_ig, b_ig=v_b_ig, lru_lambda=v_lru_lambda, q_norm_g=v_q_norm_g, k_norm_g=v_k_norm_g,
              sinks=v_sinks, w_mem_kv=v_w_mem_kv, xq_norm_g=v_xq_norm_g, xk_norm_g=v_xk_norm_g,
              out_norm_g=v_out_norm_g, w_out=v_w_out)

    delta, new_m, new_v = {}, {}, {}
    d2, m2, v2 = adamw(w_in[0].T, r_in.reshape(D_IN // 4, D_MODEL), m_w_in[0].T, v_w_in[0].T, "adamw_w_in")
    delta["w_in"], new_m["w_in"], new_v["w_in"] = d2.T[None], m2.T[None], v2.T[None]
    for name in ("w_mem_kv", "w_out"):
        shp = weights[name].shape
        d2, m2, v2 = adamw(weights[name][0], grads[name][0], ms[name][0], vs[name][0], "adamw_" + name)
        delta[name], new_m[name], new_v[name] = d2.reshape(shp), m2.reshape(shp), v2.reshape(shp)
    vec_names = [n for n, _, _ in SMALL_VECTORS]
    small_names = vec_names + [n for n, _ in SMALL_MATRICES]
    res = adamw_vectors(r_small, [grads[n] for n, _ in SMALL_MATRICES], [weights[n] for n in small_names],
                        [ms[n] for n in small_names], [vs[n] for n in small_names])
    nvec, nall = len(vec_names), len(small_names)
    grads.update(zip(vec_names, res[:nvec]))
    delta.update(zip(small_names, res[nvec:nvec + nall]))
    new_m.update(zip(small_names, res[nvec + nall:nvec + 2 * nall]))
    new_v.update(zip(small_names, res[nvec + 2 * nall:]))

    order = ("norm_g", "mem_norm_g", "w_in", "conv_w", "conv_b", "w_rg", "b_rg", "w_ig", "b_ig", "lru_lambda",
             "q_norm_g", "k_norm_g", "sinks", "w_mem_kv", "xq_norm_g", "xk_norm_g", "out_norm_g", "w_out")
    return (loss, gx[None], *[grads[n] for n in order], *[delta[n] for n in order], *[new_m[n] for n in order],
            *[new_v[n] for n in order])
```

```python
import jax
import jax.numpy as jnp
import numpy as np
from jax import lax
from jax.experimental import pallas as pl
from jax.experimental.pallas import tpu as pltpu

F32 = jnp.float32
_MXU = jnp.bfloat16
_WIRE = jnp.bfloat16

D_MODEL = 1024
MEM_LEN = 256
HEAD = 64
LRU_W = 512
LRU_BLOCKS = 8
CONV_K = 4
LRU_C = 8.0
SWA_W = 256
KV_W = 128
XATT_W = 256
BLOCK = 128
D_IN = 2304
ROPE_THETA = 500000.0
ROPE_DIM = 16
EPS = 1e-6
NEG_INF = -1e30
C_LRUX, C_LRUG, C_SQ, C_SK, C_SV, C_SWAG, C_XQ, C_XG = 0, 512, 1024, 1280, 1408, 1536, 1792, 2048

ADAM_LR, ADAM_B1, ADAM_B2, ADAM_EPS, ADAM_WD, ADAM_STEP = 0.001, 0.9, 0.999, 1e-08, 0.01, 10

N_CHIPS = 4
ROW_TILE = 256
VMEM_LIMIT = 56 * 1024 * 1024
ADAM_BLOCK_BYTES = 1280 * 1024
MESH = pl.DeviceIdType.MESH


def _mm(a, b):
    return jnp.dot(a.astype(_MXU), b.astype(_MXU), preferred_element_type=F32)


def _mm_nt(a, b):
    return lax.dot_general(a.astype(_MXU), b.astype(_MXU), (((1,), (1,)), ((), ())), preferred_element_type=F32)


def _mm_tn(a, b):
    return lax.dot_general(a.astype(_MXU), b.astype(_MXU), (((0,), (0,)), ((), ())), preferred_element_type=F32)


def _group_matrix(width):
    r = lax.shift_right_logical(lax.broadcasted_iota(jnp.int32, (width, width), 0), 6)
    c = lax.shift_right_logical(lax.broadcasted_iota(jnp.int32, (width, width), 1), 6)
    return (r == c).astype(_MXU)


def _seg_mean(x, gm):
    return jnp.dot(x.astype(_MXU), gm, preferred_element_type=F32) * (1.0 / HEAD)


def _row_mean(x):
    return jnp.mean(x, axis=-1, keepdims=True)


def _col_sum(x):
    return jnp.sum(x, axis=0, keepdims=True)


def _sigmoid(x):
    return jax.nn.sigmoid(x)


def _softplus(z):
    e = jnp.exp(-jnp.abs(z))
    u = 1.0 + e
    log1p_e = jnp.where(u == 1.0, e, jnp.log(u) * (e / (u - 1.0)))
    return jnp.maximum(z, 0.0) + log1p_e


def _rope(t, c, s1, s2):
    return t * c + pltpu.roll(t, 120, 1) * s1 + pltpu.roll(t, 8, 1) * s2


def _rope_bwd(d, c, s1, s2):
    return d * c + pltpu.roll(d * s1, 8, 1) + pltpu.roll(d * s2, 120, 1)


def _fold_heads(v):
    out = v
    for k in range(1, v.shape[1] // HEAD):
        out = out + pltpu.roll(v, HEAD * k, 1)
    return out


def _lane_mask(width, lo, hi):
    lane = lax.broadcasted_iota(jnp.int32, (1, width), 1)
    return ((lane >= lo) & (lane < hi)).astype(F32)


def _swa_mask(first_block):
    qi = lax.broadcasted_iota(jnp.int32, (BLOCK, 2 * BLOCK), 0)
    kj = lax.broadcasted_iota(jnp.int32, (BLOCK, 2 * BLOCK), 1)
    rel = qi + BLOCK - kj
    ok = (rel >= 0) & (rel < BLOCK)
    return ok & (jnp.logical_not(first_block) | (kj >= BLOCK))


def _place_kv(t, scale):
    lo = t * (_lane_mask(KV_W, 0, HEAD) * scale)
    hi = t * (_lane_mask(KV_W, HEAD, KV_W) * scale)
    return [a.astype(_MXU) for a in (lo, pltpu.roll(lo, HEAD, 1), pltpu.roll(hi, HEAD, 1), hi)]


def _unplace_kv(d):
    return (_lane_mask(KV_W, 0, HEAD) * (d[0] + pltpu.roll(d[1], HEAD, 1))
            + _lane_mask(KV_W, HEAD, KV_W) * (d[3] + pltpu.roll(d[2], HEAD, 1)))


def _swa_probs(qh, ka, mask, sink):
    s = _mm_nt(qh, ka)
    s = jnp.where(mask, s, NEG_INF)
    m = jnp.maximum(jnp.max(s, axis=-1, keepdims=True), sink)
    p = jnp.exp(s - m)
    esink = jnp.exp(sink - m)
    inv = 1.0 / (jnp.sum(p, axis=-1, keepdims=True) + esink)
    return p * inv, esink * inv


def _mem_probs(s_all):
    out = []
    for j in range(4):
        s = s_all[:, MEM_LEN * j:MEM_LEN * (j + 1)]
        p = jnp.exp(s - jnp.max(s, axis=-1, keepdims=True))
        out.append(p * (1.0 / jnp.sum(p, axis=-1, keepdims=True)))
    return out


def _head_rows(t, scale):
    return jnp.concatenate([t * (_lane_mask(XATT_W, HEAD * j, HEAD * (j + 1)) * scale) for j in range(4)], axis=0)


def _lru_gates(xc, wg_ref, brg, big, lam):
    p0 = _mm(xc[:, :256], wg_ref[0])
    p1 = _mm(xc[:, 256:], wg_ref[1])
    rg = _sigmoid(jnp.concatenate([p0[:, :256], p1[:, :256]], axis=1) + brg)
    ig = _sigmoid(jnp.concatenate([p0[:, 256:], p1[:, 256:]], axis=1) + big)
    sp = _softplus(-lam)
    la = (-LRU_C) * rg * sp
    a = jnp.exp(la)
    th = jnp.tanh(la)
    one_minus_a2 = (-2.0 * th) / (1.0 - th)
    return rg, ig, sp, a, jnp.sqrt(one_minus_a2)


def _const_spec(shape, single=False):
    zeros = (0,) * len(shape)
    if single:
        return pl.BlockSpec(shape, lambda i: zeros, pipeline_mode=pl.Buffered(1))
    return pl.BlockSpec(shape, lambda i: zeros)


def _chip_of(x, y):
    return 2 * x + y


def _partners(x, y, c):
    north = c == 1
    near = (jnp.where(north, 1 - x, x), jnp.where(north, y, 1 - y))
    far = (jnp.where(north, x, 1 - x), jnp.where(north, 1 - y, y))
    return near, far, (1 - x, 1 - y)


def gather_weights(win_t, wout, wkv, convw):
    arrs = (win_t, wout, wkv)
    n = len(arrs)
    pieces = [(a, 0, arr.shape[0] // 2) for a, arr in enumerate(arrs)]
    npc = len(pieces)

    def body(a0, a1, a2, cw, o0, o1, o2, ocw, s0, s1, s2, send, recv, lsem):
        ins, outs = (s0, s1, s2), (o0, o1, o2)
        for src, dst in zip((a0, a1, a2), ins):
            dst[...] = src[...].astype(dst.dtype)
        x, y, c = lax.axis_index("x"), lax.axis_index("y"), lax.axis_index("c")
        sibling = (x, y, 1 - c)
        near, far, diag = _partners(x, y, c)
        chips = [near, far, diag]
        me = _chip_of(x, y)

        def landed(p, chip, half):
            a, off, rows_ = pieces[p]
            r = ins[a].shape[0]
            return outs[a].at[pl.ds(pl.multiple_of(chip * r + half * (r // 2) + off, 16), rows_)]

        def mine(p):
            a, off, rows_ = pieces[p]
            return ins[a].at[pl.ds(pl.multiple_of(c * (ins[a].shape[0] // 2) + off, 16), rows_)]

        def copy(k, src, dst, to):
            return pltpu.make_async_remote_copy(src_ref=src, dst_ref=dst, send_sem=send.at[k], recv_sem=recv.at[k],
                                                device_id=to, device_id_type=MESH)

        def cw_rows(chip):
            return ocw.at[pl.ds(pl.multiple_of(chip * 8, 8), 8)]

        locals_ = []
        for a in range(n):
            r = ins[a].shape[0]
            locals_.append(pltpu.make_async_copy(ins[a], outs[a].at[pl.ds(pl.multiple_of(me * r, 16), r)], lsem.at[a]))
        locals_.append(pltpu.make_async_copy(cw, cw_rows(me), lsem.at[n]))
        for cp in locals_:
            cp.start()

        sent = []
        for p in range(npc):
            for j in range(2):
                sent.append(copy(p * 6 + j, mine(p), landed(p, me, c), (*chips[j], c)))
        for j, chip in enumerate(chips):
            sent.append(copy(npc * 6 + j, cw, cw_rows(me), (*chip, c)))
        for cp in sent:
            cp.start()
        for j in range(3):
            for p in range(npc):
                got = landed(p, _chip_of(*chips[j]), c)
                copy(p * 6 + j, got, got, sibling).wait_recv()
                if j == 0:
                    sent.append(copy(p * 6 + 2, got, got, (*far, c)))
                    sent[-1].start()
                sent.append(copy(p * 6 + 3 + j, got, got, sibling))
                sent[-1].start()
        for p in range(npc):
            for j in range(3):
                got = landed(p, _chip_of(*chips[(1, 0, 2)[j]]), 1 - c)
                copy(p * 6 + 3 + j, got, got, sibling).wait_recv()
        for j, chip in enumerate(chips):
            got = cw_rows(_chip_of(*chip))
            copy(npc * 6 + j, got, got, (*chip, c)).wait_recv()
        for cp in sent:
            cp.wait_send()
        for cp in locals_:
            cp.wait()

    vm = pl.BlockSpec(memory_space=pltpu.VMEM)
    out_shape = tuple(jax.ShapeDtypeStruct((N_CHIPS * a.shape[0],) + a.shape[1:], _MXU) for a in arrs) + (
        jax.ShapeDtypeStruct((N_CHIPS * 8, 128), F32),)
    n_rdma = npc * 6 + 3
    return pl.pallas_call(
        body, name="gather_weights", out_shape=out_shape,
        in_specs=[vm] * 4, out_specs=(pl.BlockSpec(memory_space=pl.ANY),) * n + (vm,),
        scratch_shapes=[pltpu.VMEM(a.shape, _MXU) for a in arrs] + [pltpu.SemaphoreType.DMA((n_rdma,)), pltpu.SemaphoreType.DMA((n_rdma,)),
                        pltpu.SemaphoreType.DMA((n + 1,))],
        compiler_params=pltpu.CompilerParams(vmem_limit_bytes=VMEM_LIMIT),
    )(win_t, wout, wkv, convw)


def mem_fwd(mem, mem_g, wkv, xk_g):
    def body(mem_ref, g_ref, w_ref, xk_ref, km_ref, vm_ref):
        mem_v = mem_ref[...]
        mn = mem_v * lax.rsqrt(_row_mean(mem_v * mem_v) + EPS) * g_ref[...]
        mkv = _mm(mn, w_ref[...])
        kpre = mkv[:, :XATT_W]
        gm = _group_matrix(XATT_W)
        km = kpre * lax.rsqrt(_seg_mean(kpre * kpre, gm) + EPS) * xk_ref[...]
        km_ref[...] = _head_rows(km, 0.125).astype(km_ref.dtype)
        vm_ref[...] = _head_rows(mkv[:, XATT_W:], 1.0).astype(vm_ref.dtype)

    vm = pl.BlockSpec(memory_space=pltpu.VMEM)
    rows_shape = jax.ShapeDtypeStruct((4 * MEM_LEN, XATT_W), _MXU)
    return pl.pallas_call(
        body, name="mem_fwd", out_shape=(rows_shape, rows_shape), in_specs=[vm] * 4, out_specs=(vm, vm),
    )(mem, mem_g, wkv, xk_g)


def mem_bwd(mem, mem_g, wkv, xk_g, dkm, dvm):
    def body(mem_ref, g_ref, w_ref, xk_ref, dkm_ref, dvm_ref, gw_ref, gg_ref, gxk_ref):
        mem_v = mem_ref[...]
        mh = mem_v * lax.rsqrt(_row_mean(mem_v * mem_v) + EPS)
        mn = mh * g_ref[...]
        mkv = _mm(mn, w_ref[...])
        kpre = mkv[:, :XATT_W]
        gm = _group_matrix(XATT_W)
        rk = lax.rsqrt(_seg_mean(kpre * kpre, gm) + EPS)
        kn = kpre * rk
        dk = jnp.zeros((MEM_LEN, XATT_W), F32)
        dv = jnp.zeros((MEM_LEN, XATT_W), F32)
        for j in range(4):
            mj = _lane_mask(XATT_W, HEAD * j, HEAD * (j + 1))
            dk = dk + dkm_ref[:, MEM_LEN * j:MEM_LEN * (j + 1)].T * (mj * 0.125)
            dv = dv + dvm_ref[:, MEM_LEN * j:MEM_LEN * (j + 1)].T * mj
        gxk_ref[...] = _fold_heads(_col_sum(dk * kn))
        dkn = dk * xk_ref[...]
        dkpre = rk * (dkn - kn * _seg_mean(dkn * kn, gm))
        dmkv = jnp.concatenate([dkpre, dv], axis=1)
        gw_ref[...] = _mm_tn(mn, dmkv)
        dmn = _mm_nt(dmkv, w_ref[...])
        gg_ref[...] = _col_sum(dmn * mh)

    vm = pl.BlockSpec(memory_space=pltpu.VMEM)
    return pl.pallas_call(
        body, name="mem_bwd",
        out_shape=(jax.ShapeDtypeStruct((D_MODEL, 2 * XATT_W), F32), jax.ShapeDtypeStruct((1, D_MODEL), F32),
                   jax.ShapeDtypeStruct((1, XATT_W), F32)),
        in_specs=[vm] * 6, out_specs=(vm, vm, vm),
    )(mem, mem_g, wkv, xk_g, dkm, dvm)


def layer_fwd(x, tgt, rc, rs1, rs2, ng, win_t, cw, cb, wg, brg, big, lam, qg, kg, xqg, sinks, km, vm, og, wout):
    seq = x.shape[0]
    tm = min(ROW_TILE, seq)
    nt = seq // tm
    nb = tm // BLOCK

    def body(x_ref, t_ref, c_ref, s1_ref, s2_ref, ng_ref, win_ref, cw_ref, cb_ref, wg_ref, brg_ref, big_ref, lam_ref,
             qg_ref, kg_ref, xqg_ref, sink_ref, km_ref, vm_ref, og_ref, wout_ref,
             proj_ref, ya_ref, yb_ref, yc_ref, ycat_ref, xn_ref, dout_ref, pswa_ref, pmem_ref, psink_ref, gates_ref,
             a_ref, loss_ref,
             ext_ref, b_scr, hc_ref, kp_ref, vp_ref, lacc_ref):
        i = pl.program_id(0)

        @pl.when(i == 0)
        def _():
            ext_ref[0:8, :] = jnp.zeros((8, LRU_W), F32)
            hc_ref[...] = jnp.zeros_like(hc_ref)
            kp_ref[...] = jnp.zeros_like(kp_ref)
            vp_ref[...] = jnp.zeros_like(vp_ref)
            lacc_ref[...] = jnp.zeros_like(lacc_ref)

        xv = x_ref[...]
        xn = (xv * lax.rsqrt(_row_mean(xv * xv) + EPS) * ng_ref[...]).astype(_MXU)
        xn_ref[...] = xn.astype(xn_ref.dtype)
        proj_ref[...] = _mm_nt(xn, win_ref[...])

        u = proj_ref[:, C_LRUX:C_LRUX + LRU_W]
        ext_ref[8:8 + tm, :] = u
        xc = cb_ref[...]
        for k in range(CONV_K):
            xc = xc + cw_ref[k:k + 1, :] * ext_ref[pl.ds(5 + k, tm), :]
        ext_ref[0:8, :] = u[tm - 8:tm, :]
        rg, ig, sp, a, sq = _lru_gates(xc, wg_ref, brg_ref[...], big_ref[...], lam_ref[...])
        for k, t in enumerate((xc, rg, ig, sq)):
            gates_ref[:, LRU_W * k:LRU_W * (k + 1)] = t.astype(gates_ref.dtype)
        a_ref[...] = a
        b_scr[...] = sq * (ig * xc)
        row8 = lax.broadcasted_iota(jnp.int32, (8, LRU_W), 0)

        def scan_step(g, carry):
            r0 = pl.multiple_of(g * 8, 8)
            av = a_ref[pl.ds(r0, 8), :]
            bv = b_scr[pl.ds(r0, 8), :]
            for d in (1, 2, 4):
                a_sh = jnp.where(row8 >= d, pltpu.roll(av, d, 0), 1.0)
                b_sh = jnp.where(row8 >= d, pltpu.roll(bv, d, 0), 0.0)
                bv = bv + av * b_sh
                av = av * a_sh
            hv = bv + av * carry
            ya_ref[pl.ds(r0, 8), :] = hv
            return hv[7:8, :]

        hc_ref[0:1, :] = lax.fori_loop(0, tm // 8, scan_step, hc_ref[0:1, :], unroll=True)

        gm128 = _group_matrix(KV_W)
        cv, s1v, s2v = c_ref[...], s1_ref[...], s2_ref[...]

        def head_norm_rope(t, g):
            n = t * lax.rsqrt(_seg_mean(t * t, gm128) + EPS)
            return _rope(n * g, cv, s1v, s2v)

        qs_ = (head_norm_rope(proj_ref[:, C_SQ:C_SQ + 128], qg_ref[...]).astype(_MXU),
               head_norm_rope(proj_ref[:, C_SQ + 128:C_SQ + 256], qg_ref[...]).astype(_MXU))
        kr = head_norm_rope(proj_ref[:, C_SK:C_SK + KV_W], kg_ref[...])
        sv = proj_ref[:, C_SV:C_SV + KV_W]
        ka = _place_kv(jnp.concatenate([kp_ref[...], kr], axis=0), 0.125)
        va = _place_kv(jnp.concatenate([vp_ref[...], sv], axis=0), 1.0)
        kp_ref[...] = kr[tm - BLOCK:tm, :]
        vp_ref[...] = sv[tm - BLOCK:tm, :]
        lane128 = lax.broadcasted_iota(jnp.int32, (1, 128), 1)
        for b in range(nb):
            mask = _swa_mask((i == 0) & (b == 0)) if b == 0 else _swa_mask(False)
            band = slice(BLOCK * b, BLOCK * b + 2 * BLOCK)
            blk = slice(BLOCK * b, BLOCK * (b + 1))
            psink = jnp.zeros((BLOCK, 128), F32)
            for j in range(4):
                p, pk = _swa_probs(qs_[j // 2][blk], ka[j][band], mask, sink_ref[0, j])
                pswa_ref[blk, 2 * BLOCK * j:2 * BLOCK * (j + 1)] = p.astype(pswa_ref.dtype)
                psink = jnp.where(lane128 == j, pk, psink)
            psink_ref[blk, :] = psink
            for h in range(2):
                yb_ref[blk, KV_W * h:KV_W * (h + 1)] = _mm(
                    pswa_ref[blk, 4 * BLOCK * h:4 * BLOCK * (h + 1)],
                    jnp.concatenate([va[2 * h][band], va[2 * h + 1][band]], axis=0))

        gm256 = _group_matrix(XATT_W)
        xq = proj_ref[:, C_XQ:C_XQ + XATT_W]
        qx = xq * lax.rsqrt(_seg_mean(xq * xq, gm256) + EPS) * xqg_ref[...]
        pm = _mem_probs(_mm_nt(qx, km_ref[...]))
        for j in range(4):
            pmem_ref[:, MEM_LEN * j:MEM_LEN * (j + 1)] = pm[j].astype(pmem_ref.dtype)
        yc = _mm(pmem_ref[...], vm_ref[...])
        yc_ref[...] = yc

        def gated(y, g, gate):
            return y * lax.rsqrt(_row_mean(y * y) + EPS) * g * (gate * _sigmoid(gate))

        ogv = og_ref[...]
        za = gated(ya_ref[...], ogv[:, :512], proj_ref[:, C_LRUG:C_LRUG + LRU_W])
        zb = gated(yb_ref[...], ogv[:, 512:768], proj_ref[:, C_SWAG:C_SWAG + SWA_W])
        zc = gated(yc, ogv[:, 768:], proj_ref[:, C_XG:C_XG + XATT_W])
        ycat_ref[:, 0:512] = za.astype(ycat_ref.dtype)
        ycat_ref[:, 512:768] = zb.astype(ycat_ref.dtype)
        ycat_ref[:, 768:1024] = zc.astype(ycat_ref.dtype)
        out = xv + _mm(ycat_ref[...], wout_ref[...])
        err = out - t_ref[...]
        dout_ref[...] = (err * (1.0 / D_MODEL)).astype(dout_ref.dtype)
        lacc_ref[...] = lacc_ref[...] + (0.5 / D_MODEL) * jnp.sum(err * err)

        @pl.when(i == nt - 1)
        def _():
            loss_ref[...] = lacc_ref[...]

    def rows(ncol):
        return pl.BlockSpec((tm, ncol), lambda i: (i, 0))

    in_specs = [rows(D_MODEL), rows(D_MODEL), rows(128), rows(128), rows(128),
                _const_spec((1, D_MODEL)), _const_spec((D_IN, D_MODEL), True), _const_spec((CONV_K, LRU_W)),
                _const_spec((1, LRU_W)), _const_spec((2, 256, 512), True), _const_spec((1, LRU_W)),
                _const_spec((1, LRU_W)), _const_spec((1, LRU_W)), _const_spec((1, 128)), _const_spec((1, 128)),
                _const_spec((1, XATT_W)), pl.BlockSpec(memory_space=pltpu.SMEM),
                _const_spec((4 * MEM_LEN, XATT_W), True), _const_spec((4 * MEM_LEN, XATT_W), True),
                _const_spec((1, D_MODEL)), _const_spec((D_MODEL, D_MODEL), True)]
    out_shape = (jax.ShapeDtypeStruct((seq, D_IN), F32), jax.ShapeDtypeStruct((seq, LRU_W), F32),
                 jax.ShapeDtypeStruct((seq, SWA_W), F32), jax.ShapeDtypeStruct((seq, XATT_W), F32),
                 jax.ShapeDtypeStruct((seq, D_MODEL), _MXU), jax.ShapeDtypeStruct((seq, D_MODEL), _MXU),
                 jax.ShapeDtypeStruct((seq, D_MODEL), _MXU), jax.ShapeDtypeStruct((seq, 4 * 2 * BLOCK), _MXU),
                 jax.ShapeDtypeStruct((seq, 4 * MEM_LEN), _MXU), jax.ShapeDtypeStruct((seq, 128), F32),
                 jax.ShapeDtypeStruct((seq, 4 * LRU_W), _MXU), jax.ShapeDtypeStruct((seq, LRU_W), F32),
                 jax.ShapeDtypeStruct((8, 128), F32))
    out_specs = (rows(D_IN), rows(LRU_W), rows(SWA_W), rows(XATT_W), rows(D_MODEL), rows(D_MODEL), rows(D_MODEL),
                 rows(4 * 2 * BLOCK), rows(4 * MEM_LEN), rows(128), rows(4 * LRU_W), rows(LRU_W),
                 _const_spec((8, 128)))
    scratch = [pltpu.VMEM((tm + 8, LRU_W), F32), pltpu.VMEM((tm, LRU_W), F32),
               pltpu.VMEM((8, LRU_W), F32), pltpu.VMEM((BLOCK, KV_W), F32), pltpu.VMEM((BLOCK, KV_W), F32),
               pltpu.VMEM((8, 128), F32)]
    return pl.pallas_call(
        body, name="layer_fwd", grid=(nt,), out_shape=out_shape, in_specs=in_specs, out_specs=out_specs,
        scratch_shapes=scratch,
        compiler_params=pltpu.CompilerParams(dimension_semantics=("arbitrary",), vmem_limit_bytes=VMEM_LIMIT),
    )(x, tgt, rc, rs1, rs2, ng, win_t, cw, cb, wg, brg, big, lam, qg, kg, xqg, sinks, km, vm, og, wout)


def wgrad_reduce(lhs, rhs, bigs, g_small, stage_at, name):
    seq, ncol = rhs.shape
    nblk = lhs.shape[1] // 256
    nres = len(bigs) + (g_small is not None)

    def body(l_ref, r_ref, *refs):
        if nres:
            _hosted_reduce(pl.program_id(0), stage_at, refs[:nres], refs[nres + 1:2 * nres + 1], refs[2 * nres + 1:],
                           g_small is not None)
        refs[nres][...] = _mm_tn(l_ref[...], r_ref[...])

    red_shape, scratch = _hosted_reduce_shapes(bigs, g_small) if nres else ([], [])
    vm = pl.BlockSpec(memory_space=pltpu.VMEM)
    hbm = pl.BlockSpec(memory_space=pl.ANY)
    operands = list(bigs) + ([] if g_small is None else [g_small])
    return pl.pallas_call(
        body, name=name, grid=(nblk,),
        out_shape=(jax.ShapeDtypeStruct((lhs.shape[1], ncol), F32), *red_shape),
        in_specs=[pl.BlockSpec((seq, 256), lambda j: (0, j)), _const_spec((seq, ncol), True)] + [hbm] * len(bigs)
        + [vm] * (g_small is not None),
        out_specs=(pl.BlockSpec((256, ncol), lambda j: (j, 0)),) + (hbm,) * len(red_shape),
        scratch_shapes=scratch,
        compiler_params=pltpu.CompilerParams(dimension_semantics=("arbitrary",), vmem_limit_bytes=VMEM_LIMIT),
    )(lhs, rhs, *operands)


def layer_bwd(x, dout, proj, ya, yb, yc, pswa, pmem, psink, gates, a_all, rc, rs1, rs2, ng, win_t, cw, wg, lam, qg, kg,
              xqg, km, vm, og, wout):
    seq = x.shape[0]
    tm = min(ROW_TILE, seq)
    nt = seq // tm
    nb = tm // BLOCK

    def body(x_ref, dout_ref, proj_ref, ya_ref, yb_ref, yc_ref, pswa_ref, pmem_ref, psink_ref, gates_ref, a_ref,
             c_ref, s1_ref, s2_ref,
             yah_ref, kvh_ref, ch_ref, s1h_ref, s2h_ref,
             ng_ref, win_ref, cw_ref, wg_ref, lam_ref, qg_ref, kg_ref, xqg_ref, km_ref, vm_ref, og_ref, wout_ref,
             gx_ref, dproj_ref, gwg_ref, dkm_ref, dvm_ref, gng_ref, gog_ref, gcb_ref, gbrg_ref, gbig_ref, glam_ref,
             gcw_ref, gqn_ref, gkn_ref, gxqn_ref, gsink_ref,
             hext_ref, aext_ref, an_scr, dh_scr, g_scr, dxc_ext, gcar_ref, dkcar_ref, dvcar_ref):
        i = pl.program_id(0)
        tile = nt - 1 - i
        first_tile = tile == 0

        @pl.when(i == 0)
        def _():
            for r in (gwg_ref, dkm_ref, dvm_ref, gng_ref, gog_ref, gcb_ref, gbrg_ref, gbig_ref, glam_ref, gcw_ref,
                      gqn_ref, gkn_ref, gxqn_ref, gsink_ref, gcar_ref, dkcar_ref, dvcar_ref):
                r[...] = jnp.zeros_like(r)
            dxc_ext[tm:tm + 8, :] = jnp.zeros((8, LRU_W), F32)
            aext_ref[tm:tm + 8, :] = jnp.zeros((8, LRU_W), F32)

        xv = x_ref[...]
        dov = dout_ref[...]
        dz = _mm_nt(dov, wout_ref[...])
        ogv = og_ref[...]

        def group_bwd(y, gate, g, dzg):
            r = lax.rsqrt(_row_mean(y * y) + EPS)
            n = y * r
            sg = _sigmoid(gate)
            dgate = dzg * (n * g) * (sg * (1.0 + gate * (1.0 - sg)))
            dng = dzg * (gate * sg)
            dn = dng * g
            return r * (dn - n * _row_mean(dn * n)), dgate, _col_sum(dng * n)

        dya, dga, goa = group_bwd(ya_ref[...], proj_ref[:, C_LRUG:C_LRUG + LRU_W], ogv[:, :512], dz[:, :512])
        dyb, dgb, gob = group_bwd(yb_ref[...], proj_ref[:, C_SWAG:C_SWAG + SWA_W], ogv[:, 512:768], dz[:, 512:768])
        dyc, dgc, goc = group_bwd(yc_ref[...], proj_ref[:, C_XG:C_XG + XATT_W], ogv[:, 768:], dz[:, 768:])
        gog_ref[...] += jnp.concatenate([goa, gob, goc], axis=1)
        dproj_ref[:, C_LRUG:C_LRUG + LRU_W] = dga.astype(dproj_ref.dtype)
        dproj_ref[:, C_SWAG:C_SWAG + SWA_W] = dgb.astype(dproj_ref.dtype)
        dproj_ref[:, C_XG:C_XG + XATT_W] = dgc.astype(dproj_ref.dtype)

        gm256 = _group_matrix(XATT_W)
        xq = proj_ref[:, C_XQ:C_XQ + XATT_W]
        rq = lax.rsqrt(_seg_mean(xq * xq, gm256) + EPS)
        qn = xq * rq
        qx = qn * xqg_ref[...]
        qxb = qx.astype(_MXU)
        dycb = dyc.astype(_MXU)
        dp_all = _mm_nt(dycb, vm_ref[...])
        dsm = []
        for j in range(4):
            pj = pmem_ref[:, MEM_LEN * j:MEM_LEN * (j + 1)].astype(F32)
            dp = dp_all[:, MEM_LEN * j:MEM_LEN * (j + 1)]
            dsm.append((pj * (dp - jnp.sum(pj * dp, axis=-1, keepdims=True))).astype(_MXU))
        ds_all = jnp.concatenate(dsm, axis=1)
        dvm_ref[...] += _mm_tn(dycb, pmem_ref[...])
        dkm_ref[...] += _mm_tn(qxb, ds_all)
        dqx = _mm(ds_all, km_ref[...])
        gxqn_ref[...] += _col_sum(dqx * qn)
        dqn = dqx * xqg_ref[...]
        dproj_ref[:, C_XQ:C_XQ + XATT_W] = (rq * (dqn - qn * _seg_mean(dqn * qn, gm256))).astype(dproj_ref.dtype)

        gm128 = _group_matrix(KV_W)
        cv, s1v, s2v = c_ref[...], s1_ref[...], s2_ref[...]

        def head_norm(t):
            r = lax.rsqrt(_seg_mean(t * t, gm128) + EPS)
            return t * r, r

        qn_, qr_ = zip(head_norm(proj_ref[:, C_SQ:C_SQ + 128]), head_norm(proj_ref[:, C_SQ + 128:C_SQ + 256]))
        qrope = [_rope(qn_[h] * qg_ref[...], cv, s1v, s2v).astype(_MXU) for h in range(2)]
        kn, krr = head_norm(proj_ref[:, C_SK:C_SK + KV_W])
        kr = _rope(kn * kg_ref[...], cv, s1v, s2v)
        khn, _ = head_norm(kvh_ref[:, 0:KV_W])
        khr = _rope(khn * kg_ref[...], ch_ref[...], s1h_ref[...], s2h_ref[...])
        ka = _place_kv(jnp.concatenate([khr, kr], axis=0), 0.125)
        va = _place_kv(jnp.concatenate([kvh_ref[:, KV_W:2 * KV_W], proj_ref[:, C_SV:C_SV + KV_W]], axis=0), 1.0)
        lane128 = lax.broadcasted_iota(jnp.int32, (1, 128), 1)
        gsink = jnp.zeros((1, 128), F32)
        dk_band, dv_band, dq_blk = [], [], []
        for b in range(nb):
            band = slice(BLOCK * b, BLOCK * b + 2 * BLOCK)
            blk = slice(BLOCK * b, BLOCK * (b + 1))
            dka, dva, dsb = [], [], []
            deltas = jnp.zeros((BLOCK, 128), F32)
            for j in range(4):
                qh = qrope[j // 2][blk]
                doh = dyb[blk, KV_W * (j // 2):KV_W * (j // 2 + 1)].astype(_MXU)
                pb = pswa_ref[blk, 2 * BLOCK * j:2 * BLOCK * (j + 1)]
                p = pb.astype(F32)
                dp = _mm_nt(doh, va[j][band])
                delta = jnp.sum(p * dp, axis=-1, keepdims=True)
                ds = (p * (dp - delta)).astype(_MXU)
                deltas = jnp.where(lane128 == j, delta, deltas)
                dva.append(_mm_tn(pb, doh))
                dka.append(_mm_tn(ds, qh))
                dsb.append(ds)
            gsink = gsink - _col_sum(psink_ref[blk, :] * deltas)
            dk_band.append(_unplace_kv(dka) * 0.125)
            dv_band.append(_unplace_kv(dva))
            dq_blk.append([_mm(jnp.concatenate(dsb[2 * h:2 * h + 2], axis=1),
                               jnp.concatenate([ka[2 * h][band], ka[2 * h + 1][band]], axis=0)) for h in range(2)])
        gsink_ref[...] += gsink
        dk_rows = [dk_band[b][BLOCK:] + (dk_band[b + 1][:BLOCK] if b + 1 < nb else dkcar_ref[...]) for b in range(nb)]
        dv_rows = [dv_band[b][BLOCK:] + (dv_band[b + 1][:BLOCK] if b + 1 < nb else dvcar_ref[...]) for b in range(nb)]
        dkcar_ref[...] = dk_band[0][:BLOCK]
        dvcar_ref[...] = dv_band[0][:BLOCK]
        dkg = _rope_bwd(jnp.concatenate(dk_rows, axis=0), cv, s1v, s2v)
        gkn = _col_sum(dkg * kn)
        dkn = dkg * kg_ref[...]
        dproj_ref[:, C_SK:C_SK + KV_W] = (krr * (dkn - kn * _seg_mean(dkn * kn, gm128))).astype(dproj_ref.dtype)
        dproj_ref[:, C_SV:C_SV + KV_W] = jnp.concatenate(dv_rows, axis=0).astype(dproj_ref.dtype)
        gqn = jnp.zeros((1, 128), F32)
        for h in range(2):
            dqg = _rope_bwd(jnp.concatenate([dq_blk[b][h] for b in range(nb)], axis=0), cv, s1v, s2v)
            gqn = gqn + _col_sum(dqg * qn_[h])
            dqn_ = dqg * qg_ref[...]
            dproj_ref[:, C_SQ + 128 * h:C_SQ + 128 * (h + 1)] = (
                qr_[h] * (dqn_ - qn_[h] * _seg_mean(dqn_ * qn_[h], gm128))).astype(dproj_ref.dtype)
        gqn_ref[...] += gqn
        gkn_ref[...] += gkn

        u = proj_ref[:, C_LRUX:C_LRUX + LRU_W]
        xc, rg, ig, sq = (gates_ref[:, LRU_W * k:LRU_W * (k + 1)].astype(F32) for k in range(4))
        a = a_ref[...]
        sp = _softplus(-lam_ref[...])
        hext_ref[0:8, :] = jnp.where(first_tile, 0.0, yah_ref[...])
        hext_ref[8:8 + tm, :] = ya_ref[...]
        hprev = hext_ref[pl.ds(7, tm), :]
        aext_ref[0:tm, :] = a
        an_scr[...] = aext_ref[pl.ds(1, tm), :]
        dh_scr[...] = dya
        dh_scr[tm - 1:tm, :] = dh_scr[tm - 1:tm, :] + gcar_ref[0:1, :]
        row8 = lax.broadcasted_iota(jnp.int32, (8, LRU_W), 0)

        def scan_step(gi, carry):
            r0 = pl.multiple_of((tm // 8 - 1 - gi) * 8, 8)
            av = an_scr[pl.ds(r0, 8), :]
            bv = dh_scr[pl.ds(r0, 8), :]
            for d in (1, 2, 4):
                a_sh = jnp.where(row8 < 8 - d, pltpu.roll(av, 8 - d, 0), 1.0)
                b_sh = jnp.where(row8 < 8 - d, pltpu.roll(bv, 8 - d, 0), 0.0)
                bv = bv + av * b_sh
                av = av * a_sh
            gv = bv + av * carry
            g_scr[pl.ds(r0, 8), :] = gv
            return gv[0:1, :]

        g0 = lax.fori_loop(0, tm // 8, scan_step, jnp.zeros((1, LRU_W), F32), unroll=True)
        gcar_ref[0:1, :] = a[0:1, :] * g0
        gv = g_scr[...]
        da = gv * hprev
        dig = gv * sq * xc
        dxc = gv * sq * ig
        dla = da * a - gv * (ig * xc) * ((a * a) / sq)
        drg = dla * ((-LRU_C) * sp)
        glam_ref[...] += _col_sum(dla * rg)
        dpr = drg * rg * (1.0 - rg)
        dpi = dig * ig * (1.0 - ig)
        gbrg_ref[...] += _col_sum(dpr)
        gbig_ref[...] += _col_sum(dpi)
        dpre0 = jnp.concatenate([dpr[:, :256], dpi[:, :256]], axis=1).astype(_MXU)
        dpre1 = jnp.concatenate([dpr[:, 256:], dpi[:, 256:]], axis=1).astype(_MXU)
        gwg_ref[0] += _mm_tn(xc[:, :256], dpre0)
        gwg_ref[1] += _mm_tn(xc[:, 256:], dpre1)
        dxc = dxc + jnp.concatenate([_mm_nt(dpre0, wg_ref[0]), _mm_nt(dpre1, wg_ref[1])], axis=1)
        gcb_ref[...] += _col_sum(dxc)
        dxc_ext[0:tm, :] = dxc
        du = jnp.zeros((tm, LRU_W), F32)
        for k in range(CONV_K):
            later = dxc_ext[pl.ds(3 - k, tm), :]
            gcw_ref[k:k + 1, :] += _col_sum(later * u)
            du = du + cw_ref[k:k + 1, :] * later
        dxc_ext[tm:tm + 8, :] = dxc[0:8, :]
        dproj_ref[:, C_LRUX:C_LRUX + LRU_W] = du.astype(dproj_ref.dtype)

        dxn = _mm(dproj_ref[...], win_ref[...])
        rx = lax.rsqrt(_row_mean(xv * xv) + EPS)
        xh = xv * rx
        gng_ref[...] += _col_sum(dxn * xh)
        dxh = dxn * ng_ref[...]
        gx_ref[...] = dov.astype(F32) + rx * (dxh - xh * _row_mean(dxh * xh))

        @pl.when(i == nt - 1)
        def _():
            glam_ref[...] = glam_ref[...] * (LRU_C * _sigmoid(-lam_ref[...]))
            for r in (gqn_ref, gkn_ref, gxqn_ref):
                r[...] = _fold_heads(r[...])

    def rows(ncol, arr_cols_block=0):
        return pl.BlockSpec((tm, ncol), lambda i: (nt - 1 - i, arr_cols_block))

    def halo(nrow, ncol, colblk=0):
        per = tm // nrow
        return pl.BlockSpec((nrow, ncol), lambda i: (jnp.maximum((nt - 1 - i) * per - 1, 0), colblk))

    in_specs = [rows(D_MODEL), rows(D_MODEL), rows(D_IN), rows(LRU_W), rows(SWA_W), rows(XATT_W),
                rows(4 * 2 * BLOCK), rows(4 * MEM_LEN), rows(128), rows(4 * LRU_W), rows(LRU_W),
                rows(128), rows(128), rows(128),
                halo(8, LRU_W), halo(BLOCK, 2 * KV_W, C_SK // (2 * KV_W)),
                halo(BLOCK, 128), halo(BLOCK, 128), halo(BLOCK, 128),
                _const_spec((1, D_MODEL)), _const_spec((D_IN, D_MODEL), True), _const_spec((CONV_K, LRU_W)),
                _const_spec((2, 256, 512), True), _const_spec((1, LRU_W)), _const_spec((1, 128)), _const_spec((1, 128)),
                _const_spec((1, XATT_W)),
                _const_spec((4 * MEM_LEN, XATT_W), True), _const_spec((4 * MEM_LEN, XATT_W), True),
                _const_spec((1, D_MODEL)), _const_spec((D_MODEL, D_MODEL), True)]
    small = [(2, 256, 512), (XATT_W, 4 * MEM_LEN), (XATT_W, 4 * MEM_LEN), (1, D_MODEL), (1, D_MODEL), (1, LRU_W), (1, LRU_W),
             (1, LRU_W), (1, LRU_W), (CONV_K, LRU_W), (1, 128), (1, 128), (1, XATT_W), (1, 128)]
    out_shape = (jax.ShapeDtypeStruct((seq, D_MODEL), F32), jax.ShapeDtypeStruct((seq, D_IN), _MXU)) + tuple(
        jax.ShapeDtypeStruct(s, F32) for s in small)
    out_specs = (rows(D_MODEL), rows(D_IN)) + tuple(_const_spec(s) for s in small)
    scratch = [pltpu.VMEM((tm + 8, LRU_W), F32), pltpu.VMEM((tm + 8, LRU_W), F32),
               pltpu.VMEM((tm, LRU_W), F32), pltpu.VMEM((tm, LRU_W), F32), pltpu.VMEM((tm, LRU_W), F32),
               pltpu.VMEM((tm + 8, LRU_W), F32),
               pltpu.VMEM((8, LRU_W), F32), pltpu.VMEM((BLOCK, KV_W), F32), pltpu.VMEM((BLOCK, KV_W), F32)]
    return pl.pallas_call(
        body, name="layer_bwd", grid=(nt,), out_shape=out_shape, in_specs=in_specs, out_specs=out_specs,
        scratch_shapes=scratch,
        compiler_params=pltpu.CompilerParams(dimension_semantics=("arbitrary",), vmem_limit_bytes=VMEM_LIMIT),
    )(x, dout, proj, ya, yb, yc, pswa, pmem, psink, gates, a_all, rc, rs1, rs2, ya, proj, rc, rs1, rs2,
      ng, win_t, cw, wg, lam, qg, kg, xqg, km, vm, og, wout)


def _reduce_protocol(big, sm, outs, osm, r1, r1s, wire, r2, r2s, wire2, ps, own, send, recv, lsem):
    nbig = len(big)
    x, y, c = lax.axis_index("x"), lax.axis_index("y"), lax.axis_index("c")
    sibling = (x, y, 1 - c)
    near, far, diag = _partners(x, y, c)
    me, near_id, far_id, diag_id = _chip_of(x, y), _chip_of(*near), _chip_of(*far), _chip_of(*diag)

    def copy(k, src, dst, to):
        return pltpu.make_async_remote_copy(src_ref=src, dst_ref=dst, send_sem=send.at[k], recv_sem=recv.at[k],
                                            device_id=to, device_id_type=MESH)

    def sent(stage, a):
        if a == nbig:
            src, dst, to = ((sm.at[1 - c], r1s, sibling), (r1s, r2s.at[0], (*near, c)), (ps, r2s.at[1], (*far, c)),
                            (osm.at[c], osm.at[c], sibling))[stage]
            return [copy(5 * nbig + stage, src, dst, to)]
        if stage == 0:
            return [copy(5 * a, big[a].at[:, 1 - c], r1[a], sibling)]
        if stage == 1:
            return [copy(5 * a + 1, wire[a].at[near_id], r2[a].at[0], (*near, c)),
                    copy(5 * a + 2, wire[a].at[diag_id], r2[a].at[1], (*near, c))]
        if stage == 2:
            return [copy(5 * a + 3, wire2[a], r2[a].at[2], (*far, c))]
        return [copy(5 * a + 4, outs[a].at[c], outs[a].at[c], sibling)]

    arrays = range(nbig + (sm is not None))

    def start(stage, a):
        for cp in sent(stage, a):
            cp.start()

    def arrived(k, ref):
        copy(k, ref, ref, sibling).wait_recv()

    def loads():
        return [pltpu.make_async_copy(big[a].at[:, c], own[a], lsem.at[a]) for a in range(nbig)]

    def stage0():
        for a in arrays:
            start(0, a)
        for cp in loads():
            cp.start()

    def stage1():
        for a in range(nbig):
            loads()[a].wait()
            arrived(5 * a, r1[a])
            for k in range(N_CHIPS):
                r1[a][k] = own[a][k] + r1[a][k]
                wire[a][k] = r1[a][k].astype(wire[a].dtype)
            start(1, a)
        if sm is not None:
            arrived(5 * nbig, r1s)
            r1s[...] = sm[c] + r1s[...]
            start(1, nbig)

    def stage2():
        for a in range(nbig):
            arrived(5 * a + 1, r2[a].at[0])
            arrived(5 * a + 2, r2[a].at[1])
            r1[a][me] = r1[a][me] + r2[a][0].astype(F32)
            wire2[a][...] = (r1[a][far_id] + r2[a][1].astype(F32)).astype(wire2[a].dtype)
            start(2, a)
        if sm is not None:
            arrived(5 * nbig + 1, r2s.at[0])
            ps[...] = r1s[...] + r2s[0]
            start(2, nbig)

    def stage3():
        for a in range(nbig):
            arrived(5 * a + 3, r2[a].at[2])
            outs[a][c] = r1[a][me] + r2[a][2].astype(F32)
            start(3, a)
        if sm is not None:
            arrived(5 * nbig + 2, r2s.at[1])
            osm[c] = ps[...] + r2s[1]
            start(3, nbig)

    def stage4():
        for a in range(nbig):
            arrived(5 * a + 4, outs[a].at[1 - c])
        if sm is not None:
            arrived(5 * nbig + 3, osm.at[1 - c])
        for stage in range(4):
            for a in arrays:
                for cp in sent(stage, a):
                    cp.wait_send()

    return [stage0, stage1, stage2, stage3, stage4]


def _reduce_buffers(bigs, g_small):
    half = [b.shape[2:] for b in bigs]
    sm_half = None if g_small is None else g_small.shape[1:]
    out_shape = [jax.ShapeDtypeStruct((2,) + h, F32) for h in half]
    small = lambda lead: [] if g_small is None else [pltpu.VMEM(lead + sm_half, F32)]
    if g_small is not None:
        out_shape.append(jax.ShapeDtypeStruct(g_small.shape, F32))
    n_sem = 5 * len(bigs) + 4
    scratch = ([pltpu.VMEM((N_CHIPS,) + h, F32) for h in half] + small(())
               + [pltpu.VMEM((N_CHIPS,) + h, _WIRE) for h in half]
               + [pltpu.VMEM((3,) + h, _WIRE) for h in half] + small((2,))
               + [pltpu.VMEM(h, _WIRE) for h in half] + small(())
               + [pltpu.VMEM((N_CHIPS,) + h, F32) for h in half]
               + [pltpu.SemaphoreType.DMA((n_sem,)), pltpu.SemaphoreType.DMA((n_sem,)),
                  pltpu.SemaphoreType.DMA((len(bigs),))])
    return out_shape, scratch


def _split_reduce_refs(refs, nbig, has_small):
    it = iter(refs)
    take = lambda n: [next(it) for _ in range(n)]
    one = lambda: next(it) if has_small else None
    big, sm = take(nbig), one()
    outs, osm = take(nbig), one()
    r1, r1s, wire, r2, r2s, wire2, ps, own = take(nbig), one(), take(nbig), take(nbig), one(), take(nbig), one(), take(nbig)
    send, recv, lsem = take(3)
    return big, sm, outs, osm, r1, r1s, wire, r2, r2s, wire2, ps, own, send, recv, lsem


def _hosted_reduce_shapes(bigs, g_small):
    red_shape, scratch = _reduce_buffers(bigs, g_small)
    nres = len(red_shape)
    return red_shape, [pltpu.VMEM(r.shape, r.dtype) for r in red_shape] + scratch + [pltpu.SemaphoreType.DMA((nres,))]


def _hosted_reduce(step, stage_at, operands, results, scratch, has_small):
    nres = len(results)
    sums, rest, fsem = scratch[:nres], scratch[nres:-1], scratch[-1]
    refs = tuple(operands) + tuple(sums) + tuple(rest)
    for at, stage in zip(stage_at, _reduce_protocol(*_split_reduce_refs(refs, nres - has_small, has_small))):
        pl.when(step == at)(stage)

    @pl.when(step == stage_at[-1])
    def _():
        out = [pltpu.make_async_copy(sums[k], results[k], fsem.at[k]) for k in range(nres)]
        for cp in out:
            cp.start()
        for cp in out:
            cp.wait()


def reduce_grads(big, name, parts):
    chips, halves, rows_, cols = big.shape
    sub = jax.ShapeDtypeStruct((chips, halves, rows_ // parts, cols), big.dtype)

    def body(b_ref, o_ref, *scratch):
        refs = [b_ref.at[:, :, s] for s in range(parts)] + [o_ref.at[:, s] for s in range(parts)] + list(scratch)
        for stage in _reduce_protocol(*_split_reduce_refs(refs, parts, False)):
            stage()

    _, scratch = _reduce_buffers([sub] * parts, None)
    return pl.pallas_call(
        body, name=name, out_shape=jax.ShapeDtypeStruct((halves, parts, rows_ // parts, cols), F32),
        in_specs=[pl.BlockSpec(memory_space=pl.ANY)], out_specs=pl.BlockSpec(memory_space=pltpu.VMEM),
        scratch_shapes=scratch, compiler_params=pltpu.CompilerParams(vmem_limit_bytes=VMEM_LIMIT),
    )(big.reshape(chips, halves, parts, rows_ // parts, cols))


def adamw(w, g, m, v, name):
    rows_, cols = w.shape
    tr = max(t for t in range(8, rows_ + 1, 8) if rows_ % t == 0 and t * cols * 4 <= ADAM_BLOCK_BYTES)

    def body(w_ref, g_ref, m_ref, v_ref, go_ref, d_ref, nm_ref, nv_ref):
        gv = g_ref[...]
        go_ref[...] = gv
        d_ref[...], nm_ref[...], nv_ref[...] = _adam_update(w_ref[...], gv, m_ref[...], v_ref[...])

    spec = pl.BlockSpec((tr, cols), lambda i: (i, 0))
    shp = jax.ShapeDtypeStruct(w.shape, F32)
    return pl.pallas_call(
        body, name=name, grid=(rows_ // tr,), out_shape=(shp,) * 4, in_specs=[spec] * 4, out_specs=(spec,) * 4,
        compiler_params=pltpu.CompilerParams(dimension_semantics=("arbitrary",)),
    )(w, g, m, v)


def _adam_update(w, g, m, v):
    nm = ADAM_B1 * m + (1.0 - ADAM_B1) * g
    nv = ADAM_B2 * v + (1.0 - ADAM_B2) * (g * g)
    m_hat = nm / (1.0 - ADAM_B1 ** ADAM_STEP)
    v_hat = nv / (1.0 - ADAM_B2 ** ADAM_STEP)
    return (-ADAM_LR) * (m_hat / (jnp.sqrt(v_hat) + ADAM_EPS) + ADAM_WD * w), nm, nv


def adamw_vectors(g_pack, g_mats, ws, ms, vs):
    nvec, nmat = len(SMALL_VECTORS), len(g_mats)
    n = nvec + nmat

    def body(*refs):
        pk = refs[0]
        gm_refs = refs[1:1 + nmat]
        w_refs, m_refs, v_refs = (refs[1 + nmat + k * n:1 + nmat + (k + 1) * n] for k in range(3))
        outs = refs[1 + nmat + 3 * n:]
        g_out, d_out, nm_out, nv_out = outs[:nvec], outs[nvec:nvec + n], outs[nvec + n:nvec + 2 * n], outs[nvec + 2 * n:]
        chip = 2 * lax.axis_index("x") + lax.axis_index("y")
        for k, (name, row, width) in enumerate(SMALL_VECTORS):
            if name == "conv_w":
                g = jnp.concatenate([pk[pl.ds(row + 4 * t + chip, 1), :] for t in range(CONV_K)], axis=0)[None]
            elif width >= 128:
                g = jnp.concatenate([pk[row + r:row + r + 1, :] for r in range(width // 128)], axis=1)
            else:
                g = pk[row:row + 1, 0:width]
            g_out[k][...] = g
            d_out[k][...], nm_out[k][...], nv_out[k][...] = _adam_update(w_refs[k][...], g, m_refs[k][...], v_refs[k][...])
        for k in range(nvec, n):
            d_out[k][...], nm_out[k][...], nv_out[k][...] = _adam_update(
                w_refs[k][...], gm_refs[k - nvec][...], m_refs[k][...], v_refs[k][...])

    vm = pl.BlockSpec(memory_space=pltpu.VMEM)
    like = [jax.ShapeDtypeStruct(w.shape, F32) for w in ws]
    out_shape = like[:nvec] + like * 3
    return pl.pallas_call(
        body, name="adamw_vectors", out_shape=tuple(out_shape), in_specs=[vm] * (1 + nmat + 3 * n),
        out_specs=(vm,) * len(out_shape),
    )(g_pack, *g_mats, *ws, *ms, *vs)


SMALL_VECTORS = (("norm_g", 0, 1024), ("mem_norm_g", 8, 1024), ("conv_w", 16, 512), ("conv_b", 32, 512),
                 ("b_rg", 292, 512), ("b_ig", 552, 512), ("lru_lambda", 556, 512), ("q_norm_g", 560, 64),
                 ("k_norm_g", 561, 64), ("sinks", 562, 4), ("xq_norm_g", 563, 64), ("xk_norm_g", 564, 64),
                 ("out_norm_g", 565, 1024))
SMALL_MATRICES = (("w_rg", 36), ("w_ig", 296))
LOSS_ROW = 573
SMALL_ROWS = 576


def _pack(parts, rows_):
    flat = jnp.concatenate([p.reshape(-1) for p in parts])
    return jnp.pad(flat, (0, rows_ * 128 - flat.shape[0])).reshape(rows_, 128)


def _block_diag_gates(w_rg, w_ig):
    eye = jnp.eye(4, dtype=w_rg.dtype)

    def bd(w4):
        return (w4[:, :, None, :] * eye[:, None, :, None]).reshape(256, 256)

    return jnp.stack([jnp.concatenate([bd(w_rg[4 * h:4 * h + 4]), bd(w_ig[4 * h:4 * h + 4])], axis=1) for h in (0, 1)])


def _diag_blocks(g):
    g6 = g.reshape(2, 4, HEAD, 2, 4, HEAD)
    d = (g6 * jnp.eye(4, dtype=g.dtype)[None, :, None, None, :, None]).sum(axis=4)
    return d[:, :, :, 0].reshape(8, HEAD, HEAD), d[:, :, :, 1].reshape(8, HEAD, HEAD)


def _rope_tables(seq):
    pos = np.arange(seq, dtype=np.float32)
    inv_freq = (np.float32(ROPE_THETA) ** (-(np.arange(0, ROPE_DIM, 2, dtype=np.float32) / np.float32(ROPE_DIM)))
                ).astype(np.float32)
    ang = (pos[:, None] * inv_freq[None, :]).astype(np.float32)
    cos, sin = np.cos(ang).astype(np.float32), np.sin(ang).astype(np.float32)
    z = lambda n: np.zeros((seq, n), np.float32)
    c64 = np.concatenate([cos, cos, np.ones((seq, HEAD - ROPE_DIM), np.float32)], axis=1)
    s1_64 = np.concatenate([-sin, z(HEAD - 8)], axis=1)
    s2_64 = np.concatenate([z(8), sin, z(HEAD - ROPE_DIM)], axis=1)
    return tuple(jnp.asarray(np.concatenate([t, t], axis=1)) for t in (c64, s1_64, s2_64))


def kernel(x, mem, norm_g, mem_norm_g, w_in, conv_w, conv_b, w_rg, b_rg, w_ig, b_ig, lru_lambda, q_norm_g, k_norm_g, sinks, w_mem_kv, xq_norm_g, xk_norm_g, out_norm_g, w_out, loss_target, m_norm_g, m_mem_norm_g, m_w_in, m_conv_w, m_conv_b, m_w_rg, m_b_rg, m_w_ig, m_b_ig, m_lru_lambda, m_q_norm_g, m_k_norm_g, m_sinks, m_w_mem_kv, m_xq_norm_g, m_xk_norm_g, m_out_norm_g, m_w_out, v_norm_g, v_mem_norm_g, v_w_in, v_conv_w, v_conv_b, v_w_rg, v_b_rg, v_w_ig, v_b_ig, v_lru_lambda, v_q_norm_g, v_k_norm_g, v_sinks, v_w_mem_kv, v_xq_norm_g, v_xk_norm_g, v_out_norm_g, v_w_out):
    seq = x.shape[1]
    xs, tgt, mems = x[0], loss_target[0], mem[0]

    cw_sh = jnp.pad(conv_w[0], ((0, 4), (0, 0)))
    win_t, wout, wkv, cw_all = gather_weights(w_in[0].T, w_out[0], w_mem_kv[0], cw_sh)
    cw = cw_all.reshape(N_CHIPS, 8, 128)[:, :CONV_K].transpose(1, 0, 2).reshape(CONV_K, LRU_W)

    rc, rs1, rs2 = _rope_tables(seq)
    wg = _block_diag_gates(w_rg[0], w_ig[0]).astype(_MXU)
    qg = jnp.tile(q_norm_g, (1, 2))
    kg = jnp.tile(k_norm_g, (1, 2))
    xqg = jnp.tile(xq_norm_g, (1, 4))
    xkg = jnp.tile(xk_norm_g, (1, 4))

    km, vm = mem_fwd(mems, mem_norm_g, wkv, xkg)
    proj, ya, yb, yc, ycat, xn, dout, pswa, pmem, psink, gates, a_all, loss8 = layer_fwd(
        xs, tgt, rc, rs1, rs2, norm_g, win_t, cw, conv_b, wg, b_rg, b_ig, lru_lambda, qg, kg, xqg, sinks, km, vm,
        out_norm_g, wout)
    (g_wout,) = wgrad_reduce(ycat, dout, [], None, (), "wgrad_out")
    (gx, dproj, g_wg, dkm, dvm, g_ng, g_og, g_cb, g_brg, g_big, g_lam, g_cw, g_qn, g_kn, g_xqn, g_sink) = layer_bwd(
        xs, dout, proj, ya, yb, yc, pswa, pmem, psink, gates, a_all, rc, rs1, rs2, norm_g, win_t, cw, wg, lru_lambda, qg,
        kg, xqg, km, vm, out_norm_g, wout)
    g_wkv, g_mng, g_xkn = mem_bwd(mems, mem_norm_g, wkv, xkg, dkm, dvm)

    g_wrg, g_wig = _diag_blocks(g_wg)
    small_g = _pack([g_ng, g_mng, g_cw, g_cb, g_wrg, g_brg, g_wig, g_big, g_lam, g_qn, g_kn, g_sink, g_xqn[:, :128],
                     g_xkn[:, :128], g_og, loss8[0:1]], SMALL_ROWS)
    early = [g_wout.reshape(N_CHIPS, 2, D_MODEL // 8, D_MODEL), g_wkv.reshape(N_CHIPS, 2, D_MODEL // 8, 2 * XATT_W)]
    g_win_t, r_out, r_kv, r_small = wgrad_reduce(dproj, xn, early, small_g.reshape(2, SMALL_ROWS // 2, 128),
                                                 (0, 2, 5, 7, 8), "wgrad_in")
    r_in = reduce_grads(g_win_t.reshape(N_CHIPS, 2, D_IN // 8, D_MODEL), "reduce_w_in", 6)

    r_small = r_small.reshape(SMALL_ROWS, 128)
    loss = r_small[LOSS_ROW, 0]
    grads = {}
    for name, row in SMALL_MATRICES:
        grads[name] = r_small[row:row + 256].reshape(1, LRU_BLOCKS, HEAD, HEAD)
    weights = dict(norm_g=norm_g, mem_norm_g=mem_norm_g, w_in=w_in, conv_w=conv_w, conv_b=conv_b, w_rg=w_rg, b_rg=b_rg,
                   w_ig=w_ig, b_ig=b_ig, lru_lambda=lru_lambda, q_norm_g=q_norm_g, k_norm_g=k_norm_g, sinks=sinks,
                   w_mem_kv=w_mem_kv, xq_norm_g=xq_norm_g, xk_norm_g=xk_norm_g, out_norm_g=out_norm_g, w_out=w_out)
    ms = dict(norm_g=m_norm_g, mem_norm_g=m_mem_norm_g, w_in=m_w_in, conv_w=m_conv_w, conv_b=m_conv_b, w_rg=m_w_rg,
              b_rg=m_b_rg, w_ig=m_w_ig, b_ig=m_b_ig, lru_lambda=m_lru_lambda, q_norm_g=m_q_norm_g, k_norm_g=m_k_norm_g,
              sinks=m_sinks, w_mem_kv=m_w_mem_kv, xq_norm_g=m_xq_norm_g, xk_norm_g=m_xk_norm_g,
              out_norm_g=m_out_norm_g, w_out=m_w_out)
    vs = dict(norm_g=v_norm_g, mem_norm_g=v_mem_norm_g, w_in=v_w_in, conv_w=v_conv_w, conv_b=v_conv_b, w_rg=v_w_rg,
              b_rg=v_b_rg, w_ig=v_w_ig, b_ig=v_b_ig, lru_lambda=v_lru_lambda, q_norm_g=v_q_norm_g, k_norm_g=v_k_norm_g,
              sinks=v_sinks, w_mem_kv=v_w_mem_kv, xq_norm_g=v_xq_norm_g, xk_norm_g=v_xk_norm_g,
              out_norm_g=v_out_norm_g, w_out=v_w_out)

    delta, new_m, new_v = {}, {}, {}
    g2, d2, m2, v2 = adamw(w_in[0].T, r_in.reshape(D_IN // 4, D_MODEL), m_w_in[0].T, v_w_in[0].T, "adamw_w_in")
    grads["w_in"], delta["w_in"], new_m["w_in"], new_v["w_in"] = g2.T[None], d2.T[None], m2.T[None], v2.T[None]
    for name, summed in (("w_mem_kv", r_kv.reshape(D_MODEL // 4, 2 * XATT_W)), ("w_out", r_out.reshape(D_MODEL // 4, D_MODEL))):
        res = adamw(weights[name][0], summed, ms[name][0], vs[name][0], "adamw_" + name)
        grads[name], delta[name], new_m[name], new_v[name] = (r[None] for r in res)
    vec_names = [n for n, _, _ in SMALL_VECTORS]
    small_names = vec_names + [n for n, _ in SMALL_MATRICES]
    res = adamw_vectors(r_small, [grads[n] for n, _ in SMALL_MATRICES], [weights[n] for n in small_names],
                        [ms[n] for n in small_names], [vs[n] for n in small_names])
    nvec, nall = len(vec_names), len(small_names)
    grads.update(zip(vec_names, res[:nvec]))
    delta.update(zip(small_names, res[nvec:nvec + nall]))
    new_m.update(zip(small_names, res[nvec + nall:nvec + 2 * nall]))
    new_v.update(zip(small_names, res[nvec + 2 * nall:]))

    order = ("norm_g", "mem_norm_g", "w_in", "conv_w", "conv_b", "w_rg", "b_rg", "w_ig", "b_ig", "lru_lambda",
             "q_norm_g", "k_norm_g", "sinks", "w_mem_kv", "xq_norm_g", "xk_norm_g", "out_norm_g", "w_out")
    return (loss, gx[None], *[grads[n] for n in order], *[delta[n] for n in order], *[new_m[n] for n in order],
            *[new_v[n] for n in order])
```

```python
import jax
import jax.numpy as jnp
import numpy as np
from jax import lax
from jax.experimental import pallas as pl
from jax.experimental.pallas import tpu as pltpu

F32 = jnp.float32
_MXU = jnp.bfloat16
_WIRE = jnp.bfloat16

D_MODEL = 1024
MEM_LEN = 256
HEAD = 64
LRU_W = 512
LRU_BLOCKS = 8
CONV_K = 4
LRU_C = 8.0
SWA_W = 256
KV_W = 128
XATT_W = 256
BLOCK = 128
D_IN = 2304
ROPE_THETA = 500000.0
ROPE_DIM = 16
EPS = 1e-6
NEG_INF = -1e30
C_LRUX, C_LRUG, C_SQ, C_SK, C_SV, C_SWAG, C_XQ, C_XG = 0, 512, 1024, 1280, 1408, 1536, 1792, 2048

ADAM_LR, ADAM_B1, ADAM_B2, ADAM_EPS, ADAM_WD, ADAM_STEP = 0.001, 0.9, 0.999, 1e-08, 0.01, 10

N_CHIPS = 4
ROW_TILE = 256
VMEM_LIMIT = 56 * 1024 * 1024
ADAM_BLOCK_BYTES = 1280 * 1024
MESH = pl.DeviceIdType.MESH


def _mm(a, b):
    return jnp.dot(a.astype(_MXU), b.astype(_MXU), preferred_element_type=F32)


def _mm_nt(a, b):
    return lax.dot_general(a.astype(_MXU), b.astype(_MXU), (((1,), (1,)), ((), ())), preferred_element_type=F32)


def _mm_tn(a, b):
    return lax.dot_general(a.astype(_MXU), b.astype(_MXU), (((0,), (0,)), ((), ())), preferred_element_type=F32)


def _group_matrix(width):
    r = lax.shift_right_logical(lax.broadcasted_iota(jnp.int32, (width, width), 0), 6)
    c = lax.shift_right_logical(lax.broadcasted_iota(jnp.int32, (width, width), 1), 6)
    return (r == c).astype(_MXU)


def _seg_mean(x, gm):
    return jnp.dot(x.astype(_MXU), gm, preferred_element_type=F32) * (1.0 / HEAD)


def _row_mean(x):
    return jnp.mean(x, axis=-1, keepdims=True)


def _col_sum(x):
    return jnp.sum(x, axis=0, keepdims=True)


def _sigmoid(x):
    return jax.nn.sigmoid(x)


def _softplus(z):
    e = jnp.exp(-jnp.abs(z))
    u = 1.0 + e
    log1p_e = jnp.where(u == 1.0, e, jnp.log(u) * (e / (u - 1.0)))
    return jnp.maximum(z, 0.0) + log1p_e


def _rope(t, c, s1, s2):
    return t * c + pltpu.roll(t, 120, 1) * s1 + pltpu.roll(t, 8, 1) * s2


def _rope_bwd(d, c, s1, s2):
    return d * c + pltpu.roll(d * s1, 8, 1) + pltpu.roll(d * s2, 120, 1)


def _fold_heads(v):
    out = v
    for k in range(1, v.shape[1] // HEAD):
        out = out + pltpu.roll(v, HEAD * k, 1)
    return out


def _lane_mask(width, lo, hi):
    lane = lax.broadcasted_iota(jnp.int32, (1, width), 1)
    return ((lane >= lo) & (lane < hi)).astype(F32)


def _swa_mask(first_block):
    qi = lax.broadcasted_iota(jnp.int32, (BLOCK, 2 * BLOCK), 0)
    kj = lax.broadcasted_iota(jnp.int32, (BLOCK, 2 * BLOCK), 1)
    rel = qi + BLOCK - kj
    ok = (rel >= 0) & (rel < BLOCK)
    return ok & (jnp.logical_not(first_block) | (kj >= BLOCK))


def _place_kv(t, scale):
    lo = t * (_lane_mask(KV_W, 0, HEAD) * scale)
    hi = t * (_lane_mask(KV_W, HEAD, KV_W) * scale)
    return [a.astype(_MXU) for a in (lo, pltpu.roll(lo, HEAD, 1), pltpu.roll(hi, HEAD, 1), hi)]


def _unplace_kv(d):
    return (_lane_mask(KV_W, 0, HEAD) * (d[0] + pltpu.roll(d[1], HEAD, 1))
            + _lane_mask(KV_W, HEAD, KV_W) * (d[3] + pltpu.roll(d[2], HEAD, 1)))


def _swa_probs(qh, ka, mask, sink):
    s = _mm_nt(qh, ka)
    s = jnp.where(mask, s, NEG_INF)
    m = jnp.maximum(jnp.max(s, axis=-1, keepdims=True), sink)
    p = jnp.exp(s - m)
    esink = jnp.exp(sink - m)
    inv = 1.0 / (jnp.sum(p, axis=-1, keepdims=True) + esink)
    return p * inv, esink * inv


def _mem_probs(s_all):
    out = []
    for j in range(4):
        s = s_all[:, MEM_LEN * j:MEM_LEN * (j + 1)]
        p = jnp.exp(s - jnp.max(s, axis=-1, keepdims=True))
        out.append(p * (1.0 / jnp.sum(p, axis=-1, keepdims=True)))
    return out


def _head_rows(t, scale):
    return jnp.concatenate([t * (_lane_mask(XATT_W, HEAD * j, HEAD * (j + 1)) * scale) for j in range(4)], axis=0)


def _lru_gates(xc, wg_ref, brg, big, lam):
    p0 = _mm(xc[:, :256], wg_ref[0])
    p1 = _mm(xc[:, 256:], wg_ref[1])
    rg = _sigmoid(jnp.concatenate([p0[:, :256], p1[:, :256]], axis=1) + brg)
    ig = _sigmoid(jnp.concatenate([p0[:, 256:], p1[:, 256:]], axis=1) + big)
    sp = _softplus(-lam)
    la = (-LRU_C) * rg * sp
    a = jnp.exp(la)
    th = jnp.tanh(la)
    one_minus_a2 = (-2.0 * th) / (1.0 - th)
    return rg, ig, sp, a, jnp.sqrt(one_minus_a2)


def _const_spec(shape, single=False):
    zeros = (0,) * len(shape)
    if single:
        return pl.BlockSpec(shape, lambda i: zeros, pipeline_mode=pl.Buffered(1))
    return pl.BlockSpec(shape, lambda i: zeros)


def _chip_of(x, y):
    return 2 * x + y


def _partners(x, y, c):
    north = c == 1
    near = (jnp.where(north, 1 - x, x), jnp.where(north, y, 1 - y))
    far = (jnp.where(north, x, 1 - x), jnp.where(north, 1 - y, y))
    return near, far, (1 - x, 1 - y)


def gather_weights(win_t, wout, wkv, convw):
    arrs = (win_t, wout, wkv)
    n = len(arrs)
    pieces = [(a, 0, arr.shape[0] // 2) for a, arr in enumerate(arrs)]
    npc = len(pieces)

    def body(a0, a1, a2, cw, o0, o1, o2, ocw, s0, s1, s2, send, recv, lsem):
        ins, outs = (s0, s1, s2), (o0, o1, o2)
        for src, dst in zip((a0, a1, a2), ins):
            dst[...] = src[...].astype(dst.dtype)
        x, y, c = lax.axis_index("x"), lax.axis_index("y"), lax.axis_index("c")
        sibling = (x, y, 1 - c)
        near, far, diag = _partners(x, y, c)
        chips = [near, far, diag]
        me = _chip_of(x, y)

        def landed(p, chip, half):
            a, off, rows_ = pieces[p]
            r = ins[a].shape[0]
            return outs[a].at[pl.ds(pl.multiple_of(chip * r + half * (r // 2) + off, 16), rows_)]

        def mine(p):
            a, off, rows_ = pieces[p]
            return ins[a].at[pl.ds(pl.multiple_of(c * (ins[a].shape[0] // 2) + off, 16), rows_)]

        def copy(k, src, dst, to):
            return pltpu.make_async_remote_copy(src_ref=src, dst_ref=dst, send_sem=send.at[k], recv_sem=recv.at[k],
                                                device_id=to, device_id_type=MESH)

        def cw_rows(chip):
            return ocw.at[pl.ds(pl.multiple_of(chip * 8, 8), 8)]

        locals_ = []
        for a in range(n):
            r = ins[a].shape[0]
            locals_.append(pltpu.make_async_copy(ins[a], outs[a].at[pl.ds(pl.multiple_of(me * r, 16), r)], lsem.at[a]))
        locals_.append(pltpu.make_async_copy(cw, cw_rows(me), lsem.at[n]))
        for cp in locals_:
            cp.start()

        sent = []
        for p in range(npc):
            for j in range(2):
                sent.append(copy(p * 6 + j, mine(p), landed(p, me, c), (*chips[j], c)))
        for j, chip in enumerate(chips):
            sent.append(copy(npc * 6 + j, cw, cw_rows(me), (*chip, c)))
        for cp in sent:
            cp.start()
        for j in range(3):
            for p in range(npc):
                got = landed(p, _chip_of(*chips[j]), c)
                copy(p * 6 + j, got, got, sibling).wait_recv()
                if j == 0:
                    sent.append(copy(p * 6 + 2, got, got, (*far, c)))
                    sent[-1].start()
                sent.append(copy(p * 6 + 3 + j, got, got, sibling))
                sent[-1].start()
        for p in range(npc):
            for j in range(3):
                got = landed(p, _chip_of(*chips[(1, 0, 2)[j]]), 1 - c)
                copy(p * 6 + 3 + j, got, got, sibling).wait_recv()
        for j, chip in enumerate(chips):
            got = cw_rows(_chip_of(*chip))
            copy(npc * 6 + j, got, got, (*chip, c)).wait_recv()
        for cp in sent:
            cp.wait_send()
        for cp in locals_:
            cp.wait()

    vm = pl.BlockSpec(memory_space=pltpu.VMEM)
    out_shape = tuple(jax.ShapeDtypeStruct((N_CHIPS * a.shape[0],) + a.shape[1:], _MXU) for a in arrs) + (
        jax.ShapeDtypeStruct((N_CHIPS * 8, 128), F32),)
    n_rdma = npc * 6 + 3
    return pl.pallas_call(
        body, name="gather_weights", out_shape=out_shape,
        in_specs=[vm] * 4, out_specs=(pl.BlockSpec(memory_space=pl.ANY),) * n + (vm,),
        scratch_shapes=[pltpu.VMEM(a.shape, _MXU) for a in arrs] + [pltpu.SemaphoreType.DMA((n_rdma,)), pltpu.SemaphoreType.DMA((n_rdma,)),
                        pltpu.SemaphoreType.DMA((n + 1,))],
        compiler_params=pltpu.CompilerParams(vmem_limit_bytes=VMEM_LIMIT),
    )(win_t, wout, wkv, convw)


def mem_fwd(mem, mem_g, wkv, xk_g):
    def body(mem_ref, g_ref, w_ref, xk_ref, km_ref, vm_ref):
        mem_v = mem_ref[...]
        mn = mem_v * lax.rsqrt(_row_mean(mem_v * mem_v) + EPS) * g_ref[...]
        mkv = _mm(mn, w_ref[...])
        kpre = mkv[:, :XATT_W]
        gm = _group_matrix(XATT_W)
        km = kpre * lax.rsqrt(_seg_mean(kpre * kpre, gm) + EPS) * xk_ref[...]
        km_ref[...] = _head_rows(km, 0.125).astype(km_ref.dtype)
        vm_ref[...] = _head_rows(mkv[:, XATT_W:], 1.0).astype(vm_ref.dtype)

    vm = pl.BlockSpec(memory_space=pltpu.VMEM)
    rows_shape = jax.ShapeDtypeStruct((4 * MEM_LEN, XATT_W), _MXU)
    return pl.pallas_call(
        body, name="mem_fwd", out_shape=(rows_shape, rows_shape), in_specs=[vm] * 4, out_specs=(vm, vm),
    )(mem, mem_g, wkv, xk_g)


def mem_bwd(mem, mem_g, wkv, xk_g, dkm, dvm):
    def body(mem_ref, g_ref, w_ref, xk_ref, dkm_ref, dvm_ref, gw_ref, gg_ref, gxk_ref):
        mem_v = mem_ref[...]
        mh = mem_v * lax.rsqrt(_row_mean(mem_v * mem_v) + EPS)
        mn = mh * g_ref[...]
        mkv = _mm(mn, w_ref[...])
        kpre = mkv[:, :XATT_W]
        gm = _group_matrix(XATT_W)
        rk = lax.rsqrt(_seg_mean(kpre * kpre, gm) + EPS)
        kn = kpre * rk
        dk = jnp.zeros((MEM_LEN, XATT_W), F32)
        dv = jnp.zeros((MEM_LEN, XATT_W), F32)
        for j in range(4):
            mj = _lane_mask(XATT_W, HEAD * j, HEAD * (j + 1))
            dk = dk + dkm_ref[:, MEM_LEN * j:MEM_LEN * (j + 1)].T * (mj * 0.125)
            dv = dv + dvm_ref[:, MEM_LEN * j:MEM_LEN * (j + 1)].T * mj
        gxk_ref[...] = _fold_heads(_col_sum(dk * kn))
        dkn = dk * xk_ref[...]
        dkpre = rk * (dkn - kn * _seg_mean(dkn * kn, gm))
        dmkv = jnp.concatenate([dkpre, dv], axis=1)
        gw_ref[...] = _mm_tn(mn, dmkv)
        dmn = _mm_nt(dmkv, w_ref[...])
        gg_ref[...] = _col_sum(dmn * mh)

    vm = pl.BlockSpec(memory_space=pltpu.VMEM)
    return pl.pallas_call(
        body, name="mem_bwd",
        out_shape=(jax.ShapeDtypeStruct((D_MODEL, 2 * XATT_W), F32), jax.ShapeDtypeStruct((1, D_MODEL), F32),
                   jax.ShapeDtypeStruct((1, XATT_W), F32)),
        in_specs=[vm] * 6, out_specs=(vm, vm, vm),
    )(mem, mem_g, wkv, xk_g, dkm, dvm)


def layer_fwd(x, tgt, rc, rs1, rs2, ng, win_t, cw, cb, wg, brg, big, lam, qg, kg, xqg, sinks, km, vm, og, wout):
    seq = x.shape[0]
    tm = min(ROW_TILE, seq)
    nt = seq // tm
    nb = tm // BLOCK

    def body(x_ref, t_ref, c_ref, s1_ref, s2_ref, ng_ref, win_ref, cw_ref, cb_ref, wg_ref, brg_ref, big_ref, lam_ref,
             qg_ref, kg_ref, xqg_ref, sink_ref, km_ref, vm_ref, og_ref, wout_ref,
             proj_ref, ya_ref, yb_ref, yc_ref, ycat_ref, xn_ref, dout_ref, pswa_ref, pmem_ref, psink_ref, gates_ref,
             a_ref, loss_ref,
             ext_ref, b_scr, hc_ref, kp_ref, vp_ref, lacc_ref):
        i = pl.program_id(0)

        @pl.when(i == 0)
        def _():
            ext_ref[0:8, :] = jnp.zeros((8, LRU_W), F32)
            hc_ref[...] = jnp.zeros_like(hc_ref)
            kp_ref[...] = jnp.zeros_like(kp_ref)
            vp_ref[...] = jnp.zeros_like(vp_ref)
            lacc_ref[...] = jnp.zeros_like(lacc_ref)

        xv = x_ref[...]
        xn = (xv * lax.rsqrt(_row_mean(xv * xv) + EPS) * ng_ref[...]).astype(_MXU)
        xn_ref[...] = xn.astype(xn_ref.dtype)
        proj_ref[...] = _mm_nt(xn, win_ref[...])

        u = proj_ref[:, C_LRUX:C_LRUX + LRU_W]
        ext_ref[8:8 + tm, :] = u
        xc = cb_ref[...]
        for k in range(CONV_K):
            xc = xc + cw_ref[k:k + 1, :] * ext_ref[pl.ds(5 + k, tm), :]
        ext_ref[0:8, :] = u[tm - 8:tm, :]
        rg, ig, sp, a, sq = _lru_gates(xc, wg_ref, brg_ref[...], big_ref[...], lam_ref[...])
        for k, t in enumerate((xc, rg, ig, sq)):
            gates_ref[:, LRU_W * k:LRU_W * (k + 1)] = t.astype(gates_ref.dtype)
        a_ref[...] = a
        b_scr[...] = sq * (ig * xc)
        row8 = lax.broadcasted_iota(jnp.int32, (8, LRU_W), 0)

        def scan_step(g, carry):
            r0 = pl.multiple_of(g * 8, 8)
            av = a_ref[pl.ds(r0, 8), :]
            bv = b_scr[pl.ds(r0, 8), :]
            for d in (1, 2, 4):
                a_sh = jnp.where(row8 >= d, pltpu.roll(av, d, 0), 1.0)
                b_sh = jnp.where(row8 >= d, pltpu.roll(bv, d, 0), 0.0)
                bv = bv + av * b_sh
                av = av * a_sh
            hv = bv + av * carry
            ya_ref[pl.ds(r0, 8), :] = hv
            return hv[7:8, :]

        hc_ref[0:1, :] = lax.fori_loop(0, tm // 8, scan_step, hc_ref[0:1, :], unroll=True)

        gm128 = _group_matrix(KV_W)
        cv, s1v, s2v = c_ref[...], s1_ref[...], s2_ref[...]

        def head_norm_rope(t, g):
            n = t * lax.rsqrt(_seg_mean(t * t, gm128) + EPS)
            return _rope(n * g, cv, s1v, s2v)

        qs_ = (head_norm_rope(proj_ref[:, C_SQ:C_SQ + 128], qg_ref[...]).astype(_MXU),
               head_norm_rope(proj_ref[:, C_SQ + 128:C_SQ + 256], qg_ref[...]).astype(_MXU))
        kr = head_norm_rope(proj_ref[:, C_SK:C_SK + KV_W], kg_ref[...])
        sv = proj_ref[:, C_SV:C_SV + KV_W]
        ka = _place_kv(jnp.concatenate([kp_ref[...], kr], axis=0), 0.125)
        va = _place_kv(jnp.concatenate([vp_ref[...], sv], axis=0), 1.0)
        kp_ref[...] = kr[tm - BLOCK:tm, :]
        vp_ref[...] = sv[tm - BLOCK:tm, :]
        lane128 = lax.broadcasted_iota(jnp.int32, (1, 128), 1)
        for b in range(nb):
            mask = _swa_mask((i == 0) & (b == 0)) if b == 0 else _swa_mask(False)
            band = slice(BLOCK * b, BLOCK * b + 2 * BLOCK)
            blk = slice(BLOCK * b, BLOCK * (b + 1))
            psink = jnp.zeros((BLOCK, 128), F32)
            for j in range(4):
                p, pk = _swa_probs(qs_[j // 2][blk], ka[j][band], mask, sink_ref[0, j])
                pswa_ref[blk, 2 * BLOCK * j:2 * BLOCK * (j + 1)] = p.astype(pswa_ref.dtype)
                psink = jnp.where(lane128 == j, pk, psink)
            psink_ref[blk, :] = psink
            for h in range(2):
                yb_ref[blk, KV_W * h:KV_W * (h + 1)] = _mm(
                    pswa_ref[blk, 4 * BLOCK * h:4 * BLOCK * (h + 1)],
                    jnp.concatenate([va[2 * h][band], va[2 * h + 1][band]], axis=0))

        gm256 = _group_matrix(XATT_W)
        xq = proj_ref[:, C_XQ:C_XQ + XATT_W]
        qx = xq * lax.rsqrt(_seg_mean(xq * xq, gm256) + EPS) * xqg_ref[...]
        pm = _mem_probs(_mm_nt(qx, km_ref[...]))
        for j in range(4):
            pmem_ref[:, MEM_LEN * j:MEM_LEN * (j + 1)] = pm[j].astype(pmem_ref.dtype)
        yc = _mm(pmem_ref[...], vm_ref[...])
        yc_ref[...] = yc

        def gated(y, g, gate):
            return y * lax.rsqrt(_row_mean(y * y) + EPS) * g * (gate * _sigmoid(gate))

        ogv = og_ref[...]
        za = gated(ya_ref[...], ogv[:, :512], proj_ref[:, C_LRUG:C_LRUG + LRU_W])
        zb = gated(yb_ref[...], ogv[:, 512:768], proj_ref[:, C_SWAG:C_SWAG + SWA_W])
        zc = gated(yc, ogv[:, 768:], proj_ref[:, C_XG:C_XG + XATT_W])
        ycat_ref[:, 0:512] = za.astype(ycat_ref.dtype)
        ycat_ref[:, 512:768] = zb.astype(ycat_ref.dtype)
        ycat_ref[:, 768:1024] = zc.astype(ycat_ref.dtype)
        out = xv + _mm(ycat_ref[...], wout_ref[...])
        err = out - t_ref[...]
        dout_ref[...] = (err * (1.0 / D_MODEL)).astype(dout_ref.dtype)
        lacc_ref[...] = lacc_ref[...] + (0.5 / D_MODEL) * jnp.sum(err * err)

        @pl.when(i == nt - 1)
        def _():
            loss_ref[...] = lacc_ref[...]

    def rows(ncol):
        return pl.BlockSpec((tm, ncol), lambda i: (i, 0))

    in_specs = [rows(D_MODEL), rows(D_MODEL), rows(128), rows(128), rows(128),
                _const_spec((1, D_MODEL)), _const_spec((D_IN, D_MODEL), True), _const_spec((CONV_K, LRU_W)),
                _const_spec((1, LRU_W)), _const_spec((2, 256, 512), True), _const_spec((1, LRU_W)),
                _const_spec((1, LRU_W)), _const_spec((1, LRU_W)), _const_spec((1, 128)), _const_spec((1, 128)),
                _const_spec((1, XATT_W)), pl.BlockSpec(memory_space=pltpu.SMEM),
                _const_spec((4 * MEM_LEN, XATT_W), True), _const_spec((4 * MEM_LEN, XATT_W), True),
                _const_spec((1, D_MODEL)), _const_spec((D_MODEL, D_MODEL), True)]
    out_shape = (jax.ShapeDtypeStruct((seq, D_IN), F32), jax.ShapeDtypeStruct((seq, LRU_W), F32),
                 jax.ShapeDtypeStruct((seq, SWA_W), F32), jax.ShapeDtypeStruct((seq, XATT_W), F32),
                 jax.ShapeDtypeStruct((seq, D_MODEL), _MXU), jax.ShapeDtypeStruct((seq, D_MODEL), _MXU),
                 jax.ShapeDtypeStruct((seq, D_MODEL), _MXU), jax.ShapeDtypeStruct((seq, 4 * 2 * BLOCK), _MXU),
                 jax.ShapeDtypeStruct((seq, 4 * MEM_LEN), _MXU), jax.ShapeDtypeStruct((seq, 128), F32),
                 jax.ShapeDtypeStruct((seq, 4 * LRU_W), _MXU), jax.ShapeDtypeStruct((seq, LRU_W), F32),
                 jax.ShapeDtypeStruct((8, 128), F32))
    out_specs = (rows(D_IN), rows(LRU_W), rows(SWA_W), rows(XATT_W), rows(D_MODEL), rows(D_MODEL), rows(D_MODEL),
                 rows(4 * 2 * BLOCK), rows(4 * MEM_LEN), rows(128), rows(4 * LRU_W), rows(LRU_W),
                 _const_spec((8, 128)))
    scratch = [pltpu.VMEM((tm + 8, LRU_W), F32), pltpu.VMEM((tm, LRU_W), F32),
               pltpu.VMEM((8, LRU_W), F32), pltpu.VMEM((BLOCK, KV_W), F32), pltpu.VMEM((BLOCK, KV_W), F32),
               pltpu.VMEM((8, 128), F32)]
    return pl.pallas_call(
        body, name="layer_fwd", grid=(nt,), out_shape=out_shape, in_specs=in_specs, out_specs=out_specs,
        scratch_shapes=scratch,
        compiler_params=pltpu.CompilerParams(dimension_semantics=("arbitrary",), vmem_limit_bytes=VMEM_LIMIT),
    )(x, tgt, rc, rs1, rs2, ng, win_t, cw, cb, wg, brg, big, lam, qg, kg, xqg, sinks, km, vm, og, wout)


def wgrad_reduce(lhs, rhs, blk, bigs, g_small, stage_at, name):
    seq, ncol = rhs.shape
    nblk = lhs.shape[1] // blk
    nres = len(bigs) + (g_small is not None)

    def body(l_ref, r_ref, *refs):
        if nres:
            _hosted_reduce(pl.program_id(0), stage_at, refs[:nres], refs[nres + 1:2 * nres + 1], refs[2 * nres + 1:],
                           g_small is not None)
        refs[nres][...] = _mm_tn(l_ref[...], r_ref[...])

    red_shape, scratch = _hosted_reduce_shapes(bigs, g_small) if nres else ([], [])
    vm = pl.BlockSpec(memory_space=pltpu.VMEM)
    hbm = pl.BlockSpec(memory_space=pl.ANY)
    operands = list(bigs) + ([] if g_small is None else [g_small])
    return pl.pallas_call(
        body, name=name, grid=(nblk,),
        out_shape=(jax.ShapeDtypeStruct((lhs.shape[1], ncol), F32), *red_shape),
        in_specs=[pl.BlockSpec((seq, blk), lambda j: (0, j)), _const_spec((seq, ncol), True)] + [hbm] * len(bigs)
        + [vm] * (g_small is not None),
        out_specs=(pl.BlockSpec((blk, ncol), lambda j: (j, 0)),) + (hbm,) * len(red_shape),
        scratch_shapes=scratch,
        compiler_params=pltpu.CompilerParams(dimension_semantics=("arbitrary",), vmem_limit_bytes=VMEM_LIMIT),
    )(lhs, rhs, *operands)


def layer_bwd(x, dout, proj, ya, yb, yc, pswa, pmem, psink, gates, a_all, rc, rs1, rs2, ng, win_t, cw, wg, lam, qg, kg,
              xqg, km, vm, og, wout):
    seq = x.shape[0]
    tm = min(ROW_TILE, seq)
    nt = seq // tm
    nb = tm // BLOCK

    def body(x_ref, dout_ref, proj_ref, ya_ref, yb_ref, yc_ref, pswa_ref, pmem_ref, psink_ref, gates_ref, a_ref,
             c_ref, s1_ref, s2_ref,
             yah_ref, kvh_ref, ch_ref, s1h_ref, s2h_ref,
             ng_ref, win_ref, cw_ref, wg_ref, lam_ref, qg_ref, kg_ref, xqg_ref, km_ref, vm_ref, og_ref, wout_ref,
             gx_ref, dproj_ref, gwg_ref, dkm_ref, dvm_ref, gng_ref, gog_ref, gcb_ref, gbrg_ref, gbig_ref, glam_ref,
             gcw_ref, gqn_ref, gkn_ref, gxqn_ref, gsink_ref,
             hext_ref, aext_ref, an_scr, dh_scr, g_scr, dxc_ext, gcar_ref, dkcar_ref, dvcar_ref):
        i = pl.program_id(0)
        tile = nt - 1 - i
        first_tile = tile == 0

        @pl.when(i == 0)
        def _():
            for r in (gwg_ref, dkm_ref, dvm_ref, gng_ref, gog_ref, gcb_ref, gbrg_ref, gbig_ref, glam_ref, gcw_ref,
                      gqn_ref, gkn_ref, gxqn_ref, gsink_ref, gcar_ref, dkcar_ref, dvcar_ref):
                r[...] = jnp.zeros_like(r)
            dxc_ext[tm:tm + 8, :] = jnp.zeros((8, LRU_W), F32)
            aext_ref[tm:tm + 8, :] = jnp.zeros((8, LRU_W), F32)

        xv = x_ref[...]
        dov = dout_ref[...]
        dz = _mm_nt(dov, wout_ref[...])
        ogv = og_ref[...]

        def group_bwd(y, gate, g, dzg):
            r = lax.rsqrt(_row_mean(y * y) + EPS)
            n = y * r
            sg = _sigmoid(gate)
            dgate = dzg * (n * g) * (sg * (1.0 + gate * (1.0 - sg)))
            dng = dzg * (gate * sg)
            dn = dng * g
            return r * (dn - n * _row_mean(dn * n)), dgate, _col_sum(dng * n)

        dya, dga, goa = group_bwd(ya_ref[...], proj_ref[:, C_LRUG:C_LRUG + LRU_W], ogv[:, :512], dz[:, :512])
        dyb, dgb, gob = group_bwd(yb_ref[...], proj_ref[:, C_SWAG:C_SWAG + SWA_W], ogv[:, 512:768], dz[:, 512:768])
        dyc, dgc, goc = group_bwd(yc_ref[...], proj_ref[:, C_XG:C_XG + XATT_W], ogv[:, 768:], dz[:, 768:])
        gog_ref[...] += jnp.concatenate([goa, gob, goc], axis=1)
        dproj_ref[:, C_LRUG:C_LRUG + LRU_W] = dga.astype(dproj_ref.dtype)
        dproj_ref[:, C_SWAG:C_SWAG + SWA_W] = dgb.astype(dproj_ref.dtype)
        dproj_ref[:, C_XG:C_XG + XATT_W] = dgc.astype(dproj_ref.dtype)

        gm256 = _group_matrix(XATT_W)
        xq = proj_ref[:, C_XQ:C_XQ + XATT_W]
        rq = lax.rsqrt(_seg_mean(xq * xq, gm256) + EPS)
        qn = xq * rq
        qx = qn * xqg_ref[...]
        qxb = qx.astype(_MXU)
        dycb = dyc.astype(_MXU)
        dp_all = _mm_nt(dycb, vm_ref[...])
        dsm = []
        for j in range(4):
            pj = pmem_ref[:, MEM_LEN * j:MEM_LEN * (j + 1)].astype(F32)
            dp = dp_all[:, MEM_LEN * j:MEM_LEN * (j + 1)]
            dsm.append((pj * (dp - jnp.sum(pj * dp, axis=-1, keepdims=True))).astype(_MXU))
        ds_all = jnp.concatenate(dsm, axis=1)
        dvm_ref[...] += _mm_tn(dycb, pmem_ref[...])
        dkm_ref[...] += _mm_tn(qxb, ds_all)
        dqx = _mm(ds_all, km_ref[...])
        gxqn_ref[...] += _col_sum(dqx * qn)
        dqn = dqx * xqg_ref[...]
        dproj_ref[:, C_XQ:C_XQ + XATT_W] = (rq * (dqn - qn * _seg_mean(dqn * qn, gm256))).astype(dproj_ref.dtype)

        gm128 = _group_matrix(KV_W)
        cv, s1v, s2v = c_ref[...], s1_ref[...], s2_ref[...]

        def head_norm(t):
            r = lax.rsqrt(_seg_mean(t * t, gm128) + EPS)
            return t * r, r

        qn_, qr_ = zip(head_norm(proj_ref[:, C_SQ:C_SQ + 128]), head_norm(proj_ref[:, C_SQ + 128:C_SQ + 256]))
        qrope = [_rope(qn_[h] * qg_ref[...], cv, s1v, s2v).astype(_MXU) for h in range(2)]
        kn, krr = head_norm(proj_ref[:, C_SK:C_SK + KV_W])
        kr = _rope(kn * kg_ref[...], cv, s1v, s2v)
        khn, _ = head_norm(kvh_ref[:, 0:KV_W])
        khr = _rope(khn * kg_ref[...], ch_ref[...], s1h_ref[...], s2h_ref[...])
        ka = _place_kv(jnp.concatenate([khr, kr], axis=0), 0.125)
        va = _place_kv(jnp.concatenate([kvh_ref[:, KV_W:2 * KV_W], proj_ref[:, C_SV:C_SV + KV_W]], axis=0), 1.0)
        lane128 = lax.broadcasted_iota(jnp.int32, (1, 128), 1)
        gsink = jnp.zeros((1, 128), F32)
        dk_band, dv_band, dq_blk = [], [], []
        for b in range(nb):
            band = slice(BLOCK * b, BLOCK * b + 2 * BLOCK)
            blk = slice(BLOCK * b, BLOCK * (b + 1))
            dka, dva, dsb = [], [], []
            deltas = jnp.zeros((BLOCK, 128), F32)
            for j in range(4):
                qh = qrope[j // 2][blk]
                doh = dyb[blk, KV_W * (j // 2):KV_W * (j // 2 + 1)].astype(_MXU)
                pb = pswa_ref[blk, 2 * BLOCK * j:2 * BLOCK * (j + 1)]
                p = pb.astype(F32)
                dp = _mm_nt(doh, va[j][band])
                delta = jnp.sum(p * dp, axis=-1, keepdims=True)
                ds = (p * (dp - delta)).astype(_MXU)
                deltas = jnp.where(lane128 == j, delta, deltas)
                dva.append(_mm_tn(pb, doh))
                dka.append(_mm_tn(ds, qh))
                dsb.append(ds)
            gsink = gsink - _col_sum(psink_ref[blk, :] * deltas)
            dk_band.append(_unplace_kv(dka) * 0.125)
            dv_band.append(_unplace_kv(dva))
            dq_blk.append([_mm(jnp.concatenate(dsb[2 * h:2 * h + 2], axis=1),
                               jnp.concatenate([ka[2 * h][band], ka[2 * h + 1][band]], axis=0)) for h in range(2)])
        gsink_ref[...] += gsink
        dk_rows = [dk_band[b][BLOCK:] + (dk_band[b + 1][:BLOCK] if b + 1 < nb else dkcar_ref[...]) for b in range(nb)]
        dv_rows = [dv_band[b][BLOCK:] + (dv_band[b + 1][:BLOCK] if b + 1 < nb else dvcar_ref[...]) for b in range(nb)]
        dkcar_ref[...] = dk_band[0][:BLOCK]
        dvcar_ref[...] = dv_band[0][:BLOCK]
        dkg = _rope_bwd(jnp.concatenate(dk_rows, axis=0), cv, s1v, s2v)
        gkn = _col_sum(dkg * kn)
        dkn = dkg * kg_ref[...]
        dproj_ref[:, C_SK:C_SK + KV_W] = (krr * (dkn - kn * _seg_mean(dkn * kn, gm128))).astype(dproj_ref.dtype)
        dproj_ref[:, C_SV:C_SV + KV_W] = jnp.concatenate(dv_rows, axis=0).astype(dproj_ref.dtype)
        gqn = jnp.zeros((1, 128), F32)
        for h in range(2):
            dqg = _rope_bwd(jnp.concatenate([dq_blk[b][h] for b in range(nb)], axis=0), cv, s1v, s2v)
            gqn = gqn + _col_sum(dqg * qn_[h])
            dqn_ = dqg * qg_ref[...]
            dproj_ref[:, C_SQ + 128 * h:C_SQ + 128 * (h + 1)] = (
                qr_[h] * (dqn_ - qn_[h] * _seg_mean(dqn_ * qn_[h], gm128))).astype(dproj_ref.dtype)
        gqn_ref[...] += gqn
        gkn_ref[...] += gkn

        u = proj_ref[:, C_LRUX:C_LRUX + LRU_W]
        xc, rg, ig, sq = (gates_ref[:, LRU_W * k:LRU_W * (k + 1)].astype(F32) for k in range(4))
        a = a_ref[...]
        sp = _softplus(-lam_ref[...])
        hext_ref[0:8, :] = jnp.where(first_tile, 0.0, yah_ref[...])
        hext_ref[8:8 + tm, :] = ya_ref[...]
        hprev = hext_ref[pl.ds(7, tm), :]
        aext_ref[0:tm, :] = a
        an_scr[...] = aext_ref[pl.ds(1, tm), :]
        dh_scr[...] = dya
        dh_scr[tm - 1:tm, :] = dh_scr[tm - 1:tm, :] + gcar_ref[0:1, :]
        row8 = lax.broadcasted_iota(jnp.int32, (8, LRU_W), 0)

        def scan_step(gi, carry):
            r0 = pl.multiple_of((tm // 8 - 1 - gi) * 8, 8)
            av = an_scr[pl.ds(r0, 8), :]
            bv = dh_scr[pl.ds(r0, 8), :]
            for d in (1, 2, 4):
                a_sh = jnp.where(row8 < 8 - d, pltpu.roll(av, 8 - d, 0), 1.0)
                b_sh = jnp.where(row8 < 8 - d, pltpu.roll(bv, 8 - d, 0), 0.0)
                bv = bv + av * b_sh
                av = av * a_sh
            gv = bv + av * carry
            g_scr[pl.ds(r0, 8), :] = gv
            return gv[0:1, :]

        g0 = lax.fori_loop(0, tm // 8, scan_step, jnp.zeros((1, LRU_W), F32), unroll=True)
        gcar_ref[0:1, :] = a[0:1, :] * g0
        gv = g_scr[...]
        da = gv * hprev
        dig = gv * sq * xc
        dxc = gv * sq * ig
        dla = da * a - gv * (ig * xc) * ((a * a) / sq)
        drg = dla * ((-LRU_C) * sp)
        glam_ref[...] += _col_sum(dla * rg)
        dpr = drg * rg * (1.0 - rg)
        dpi = dig * ig * (1.0 - ig)
        gbrg_ref[...] += _col_sum(dpr)
        gbig_ref[...] += _col_sum(dpi)
        dpre0 = jnp.concatenate([dpr[:, :256], dpi[:, :256]], axis=1).astype(_MXU)
        dpre1 = jnp.concatenate([dpr[:, 256:], dpi[:, 256:]], axis=1).astype(_MXU)
        gwg_ref[0] += _mm_tn(xc[:, :256], dpre0)
        gwg_ref[1] += _mm_tn(xc[:, 256:], dpre1)
        dxc = dxc + jnp.concatenate([_mm_nt(dpre0, wg_ref[0]), _mm_nt(dpre1, wg_ref[1])], axis=1)
        gcb_ref[...] += _col_sum(dxc)
        dxc_ext[0:tm, :] = dxc
        du = jnp.zeros((tm, LRU_W), F32)
        for k in range(CONV_K):
            later = dxc_ext[pl.ds(3 - k, tm), :]
            gcw_ref[k:k + 1, :] += _col_sum(later * u)
            du = du + cw_ref[k:k + 1, :] * later
        dxc_ext[tm:tm + 8, :] = dxc[0:8, :]
        dproj_ref[:, C_LRUX:C_LRUX + LRU_W] = du.astype(dproj_ref.dtype)

        dxn = _mm(dproj_ref[...], win_ref[...])
        rx = lax.rsqrt(_row_mean(xv * xv) + EPS)
        xh = xv * rx
        gng_ref[...] += _col_sum(dxn * xh)
        dxh = dxn * ng_ref[...]
        gx_ref[...] = dov.astype(F32) + rx * (dxh - xh * _row_mean(dxh * xh))

        @pl.when(i == nt - 1)
        def _():
            glam_ref[...] = glam_ref[...] * (LRU_C * _sigmoid(-lam_ref[...]))
            for r in (gqn_ref, gkn_ref, gxqn_ref):
                r[...] = _fold_heads(r[...])

    def rows(ncol, arr_cols_block=0):
        return pl.BlockSpec((tm, ncol), lambda i: (nt - 1 - i, arr_cols_block))

    def halo(nrow, ncol, colblk=0):
        per = tm // nrow
        return pl.BlockSpec((nrow, ncol), lambda i: (jnp.maximum((nt - 1 - i) * per - 1, 0), colblk))

    in_specs = [rows(D_MODEL), rows(D_MODEL), rows(D_IN), rows(LRU_W), rows(SWA_W), rows(XATT_W),
                rows(4 * 2 * BLOCK), rows(4 * MEM_LEN), rows(128), rows(4 * LRU_W), rows(LRU_W),
                rows(128), rows(128), rows(128),
                halo(8, LRU_W), halo(BLOCK, 2 * KV_W, C_SK // (2 * KV_W)),
                halo(BLOCK, 128), halo(BLOCK, 128), halo(BLOCK, 128),
                _const_spec((1, D_MODEL)), _const_spec((D_IN, D_MODEL), True), _const_spec((CONV_K, LRU_W)),
                _const_spec((2, 256, 512), True), _const_spec((1, LRU_W)), _const_spec((1, 128)), _const_spec((1, 128)),
                _const_spec((1, XATT_W)),
                _const_spec((4 * MEM_LEN, XATT_W), True), _const_spec((4 * MEM_LEN, XATT_W), True),
                _const_spec((1, D_MODEL)), _const_spec((D_MODEL, D_MODEL), True)]
    small = [(2, 256, 512), (XATT_W, 4 * MEM_LEN), (XATT_W, 4 * MEM_LEN), (1, D_MODEL), (1, D_MODEL), (1, LRU_W), (1, LRU_W),
             (1, LRU_W), (1, LRU_W), (CONV_K, LRU_W), (1, 128), (1, 128), (1, XATT_W), (1, 128)]
    out_shape = (jax.ShapeDtypeStruct((seq, D_MODEL), F32), jax.ShapeDtypeStruct((seq, D_IN), _MXU)) + tuple(
        jax.ShapeDtypeStruct(s, F32) for s in small)
    out_specs = (rows(D_MODEL), rows(D_IN)) + tuple(_const_spec(s) for s in small)
    scratch = [pltpu.VMEM((tm + 8, LRU_W), F32), pltpu.VMEM((tm + 8, LRU_W), F32),
               pltpu.VMEM((tm, LRU_W), F32), pltpu.VMEM((tm, LRU_W), F32), pltpu.VMEM((tm, LRU_W), F32),
               pltpu.VMEM((tm + 8, LRU_W), F32),
               pltpu.VMEM((8, LRU_W), F32), pltpu.VMEM((BLOCK, KV_W), F32), pltpu.VMEM((BLOCK, KV_W), F32)]
    return pl.pallas_call(
        body, name="layer_bwd", grid=(nt,), out_shape=out_shape, in_specs=in_specs, out_specs=out_specs,
        scratch_shapes=scratch,
        compiler_params=pltpu.CompilerParams(dimension_semantics=("arbitrary",), vmem_limit_bytes=VMEM_LIMIT),
    )(x, dout, proj, ya, yb, yc, pswa, pmem, psink, gates, a_all, rc, rs1, rs2, ya, proj, rc, rs1, rs2,
      ng, win_t, cw, wg, lam, qg, kg, xqg, km, vm, og, wout)


def _reduce_protocol(big, sm, outs, osm, r1, r1s, wire, r2, r2s, wire2, ps, own, send, recv, lsem):
    nbig = len(big)
    x, y, c = lax.axis_index("x"), lax.axis_index("y"), lax.axis_index("c")
    sibling = (x, y, 1 - c)
    near, far, diag = _partners(x, y, c)
    me, near_id, far_id, diag_id = _chip_of(x, y), _chip_of(*near), _chip_of(*far), _chip_of(*diag)

    def copy(k, src, dst, to):
        return pltpu.make_async_remote_copy(src_ref=src, dst_ref=dst, send_sem=send.at[k], recv_sem=recv.at[k],
                                            device_id=to, device_id_type=MESH)

    def sent(stage, a):
        if a == nbig:
            src, dst, to = ((sm.at[1 - c], r1s, sibling), (r1s, r2s.at[0], (*near, c)), (ps, r2s.at[1], (*far, c)),
                            (osm.at[c], osm.at[c], sibling))[stage]
            return [copy(5 * nbig + stage, src, dst, to)]
        if stage == 0:
            return [copy(5 * a, big[a].at[:, 1 - c], r1[a], sibling)]
        if stage == 1:
            return [copy(5 * a + 1, wire[a].at[near_id], r2[a].at[0], (*near, c)),
                    copy(5 * a + 2, wire[a].at[diag_id], r2[a].at[1], (*near, c))]
        if stage == 2:
            return [copy(5 * a + 3, wire2[a], r2[a].at[2], (*far, c))]
        return [copy(5 * a + 4, outs[a].at[c], outs[a].at[c], sibling)]

    arrays = range(nbig + (sm is not None))

    def start(stage, a):
        for cp in sent(stage, a):
            cp.start()

    def arrived(k, ref):
        copy(k, ref, ref, sibling).wait_recv()

    def loads():
        return [pltpu.make_async_copy(big[a].at[:, c], own[a], lsem.at[a]) for a in range(nbig)]

    def stage0():
        for a in arrays:
            start(0, a)
        for cp in loads():
            cp.start()

    def stage1():
        for a in range(nbig):
            loads()[a].wait()
            arrived(5 * a, r1[a])
            for k in range(N_CHIPS):
                r1[a][k] = own[a][k] + r1[a][k]
                wire[a][k] = r1[a][k].astype(wire[a].dtype)
            start(1, a)
        if sm is not None:
            arrived(5 * nbig, r1s)
            r1s[...] = sm[c] + r1s[...]
            start(1, nbig)

    def stage2():
        for a in range(nbig):
            arrived(5 * a + 1, r2[a].at[0])
            arrived(5 * a + 2, r2[a].at[1])
            r1[a][me] = r1[a][me] + r2[a][0].astype(F32)
            wire2[a][...] = (r1[a][far_id] + r2[a][1].astype(F32)).astype(wire2[a].dtype)
            start(2, a)
        if sm is not None:
            arrived(5 * nbig + 1, r2s.at[0])
            ps[...] = r1s[...] + r2s[0]
            start(2, nbig)

    def stage3():
        for a in range(nbig):
            arrived(5 * a + 3, r2[a].at[2])
            outs[a][c] = r1[a][me] + r2[a][2].astype(F32)
            start(3, a)
        if sm is not None:
            arrived(5 * nbig + 2, r2s.at[1])
            osm[c] = ps[...] + r2s[1]
            start(3, nbig)

    def stage4():
        for a in range(nbig):
            arrived(5 * a + 4, outs[a].at[1 - c])
        if sm is not None:
            arrived(5 * nbig + 3, osm.at[1 - c])
        for stage in range(4):
            for a in arrays:
                for cp in sent(stage, a):
                    cp.wait_send()

    return [stage0, stage1, stage2, stage3, stage4]


def _reduce_buffers(bigs, g_small):
    half = [b.shape[2:] for b in bigs]
    sm_half = None if g_small is None else g_small.shape[1:]
    out_shape = [jax.ShapeDtypeStruct((2,) + h, F32) for h in half]
    small = lambda lead: [] if g_small is None else [pltpu.VMEM(lead + sm_half, F32)]
    if g_small is not None:
        out_shape.append(jax.ShapeDtypeStruct(g_small.shape, F32))
    n_sem = 5 * len(bigs) + 4
    scratch = ([pltpu.VMEM((N_CHIPS,) + h, F32) for h in half] + small(())
               + [pltpu.VMEM((N_CHIPS,) + h, _WIRE) for h in half]
               + [pltpu.VMEM((3,) + h, _WIRE) for h in half] + small((2,))
               + [pltpu.VMEM(h, _WIRE) for h in half] + small(())
               + [pltpu.VMEM((N_CHIPS,) + h, F32) for h in half]
               + [pltpu.SemaphoreType.DMA((n_sem,)), pltpu.SemaphoreType.DMA((n_sem,)),
                  pltpu.SemaphoreType.DMA((len(bigs),))])
    return out_shape, scratch


def _split_reduce_refs(refs, nbig, has_small):
    it = iter(refs)
    take = lambda n: [next(it) for _ in range(n)]
    one = lambda: next(it) if has_small else None
    big, sm = take(nbig), one()
    outs, osm = take(nbig), one()
    r1, r1s, wire, r2, r2s, wire2, ps, own = take(nbig), one(), take(nbig), take(nbig), one(), take(nbig), one(), take(nbig)
    send, recv, lsem = take(3)
    return big, sm, outs, osm, r1, r1s, wire, r2, r2s, wire2, ps, own, send, recv, lsem


def _hosted_reduce_shapes(bigs, g_small):
    red_shape, scratch = _reduce_buffers(bigs, g_small)
    nres = len(red_shape)
    return red_shape, [pltpu.VMEM(r.shape, r.dtype) for r in red_shape] + scratch + [pltpu.SemaphoreType.DMA((nres,))]


def _hosted_reduce(step, stage_at, operands, results, scratch, has_small):
    nres = len(results)
    sums, rest, fsem = scratch[:nres], scratch[nres:-1], scratch[-1]
    refs = tuple(operands) + tuple(sums) + tuple(rest)
    for at, stage in zip(stage_at, _reduce_protocol(*_split_reduce_refs(refs, nres - has_small, has_small))):
        pl.when(step == at)(stage)

    @pl.when(step == stage_at[-1])
    def _():
        out = [pltpu.make_async_copy(sums[k], results[k], fsem.at[k]) for k in range(nres)]
        for cp in out:
            cp.start()
        for cp in out:
            cp.wait()


def reduce_grads(big, name, parts):
    chips, halves, rows_, cols = big.shape
    sub = jax.ShapeDtypeStruct((chips, halves, rows_ // parts, cols), big.dtype)

    def body(b_ref, o_ref, *scratch):
        refs = [b_ref.at[:, :, s] for s in range(parts)] + [o_ref.at[:, s] for s in range(parts)] + list(scratch)
        for stage in _reduce_protocol(*_split_reduce_refs(refs, parts, False)):
            stage()

    _, scratch = _reduce_buffers([sub] * parts, None)
    return pl.pallas_call(
        body, name=name, out_shape=jax.ShapeDtypeStruct((halves, parts, rows_ // parts, cols), F32),
        in_specs=[pl.BlockSpec(memory_space=pl.ANY)], out_specs=pl.BlockSpec(memory_space=pltpu.VMEM),
        scratch_shapes=scratch, compiler_params=pltpu.CompilerParams(vmem_limit_bytes=VMEM_LIMIT),
    )(big.reshape(chips, halves, parts, rows_ // parts, cols))


def adamw(w, g, m, v, name):
    rows_, cols = w.shape
    tr = max(t for t in range(8, rows_ + 1, 8) if rows_ % t == 0 and t * cols * 4 <= ADAM_BLOCK_BYTES)

    def body(w_ref, g_ref, m_ref, v_ref, go_ref, d_ref, nm_ref, nv_ref):
        gv = g_ref[...]
        go_ref[...] = gv
        d_ref[...], nm_ref[...], nv_ref[...] = _adam_update(w_ref[...], gv, m_ref[...], v_ref[...])

    spec = pl.BlockSpec((tr, cols), lambda i: (i, 0))
    shp = jax.ShapeDtypeStruct(w.shape, F32)
    return pl.pallas_call(
        body, name=name, grid=(rows_ // tr,), out_shape=(shp,) * 4, in_specs=[spec] * 4, out_specs=(spec,) * 4,
        compiler_params=pltpu.CompilerParams(dimension_semantics=("arbitrary",)),
    )(w, g, m, v)


def _adam_update(w, g, m, v):
    nm = ADAM_B1 * m + (1.0 - ADAM_B1) * g
    nv = ADAM_B2 * v + (1.0 - ADAM_B2) * (g * g)
    m_hat = nm / (1.0 - ADAM_B1 ** ADAM_STEP)
    v_hat = nv / (1.0 - ADAM_B2 ** ADAM_STEP)
    return (-ADAM_LR) * (m_hat / (jnp.sqrt(v_hat) + ADAM_EPS) + ADAM_WD * w), nm, nv


def adamw_vectors(g_pack, g_mats, ws, ms, vs):
    nvec, nmat = len(SMALL_VECTORS), len(g_mats)
    n = nvec + nmat

    def body(*refs):
        pk = refs[0]
        gm_refs = refs[1:1 + nmat]
        w_refs, m_refs, v_refs = (refs[1 + nmat + k * n:1 + nmat + (k + 1) * n] for k in range(3))
        outs = refs[1 + nmat + 3 * n:]
        g_out, d_out, nm_out, nv_out = outs[:nvec], outs[nvec:nvec + n], outs[nvec + n:nvec + 2 * n], outs[nvec + 2 * n:]
        chip = 2 * lax.axis_index("x") + lax.axis_index("y")
        for k, (name, row, width) in enumerate(SMALL_VECTORS):
            if name == "conv_w":
                g = jnp.concatenate([pk[pl.ds(row + 4 * t + chip, 1), :] for t in range(CONV_K)], axis=0)[None]
            elif width >= 128:
                g = jnp.concatenate([pk[row + r:row + r + 1, :] for r in range(width // 128)], axis=1)
            else:
                g = pk[row:row + 1, 0:width]
            g_out[k][...] = g
            d_out[k][...], nm_out[k][...], nv_out[k][...] = _adam_update(w_refs[k][...], g, m_refs[k][...], v_refs[k][...])
        for k in range(nvec, n):
            d_out[k][...], nm_out[k][...], nv_out[k][...] = _adam_update(
                w_refs[k][...], gm_refs[k - nvec][...], m_refs[k][...], v_refs[k][...])

    vm = pl.BlockSpec(memory_space=pltpu.VMEM)
    like = [jax.ShapeDtypeStruct(w.shape, F32) for w in ws]
    out_shape = like[:nvec] + like * 3
    return pl.pallas_call(
        body, name="adamw_vectors", out_shape=tuple(out_shape), in_specs=[vm] * (1 + nmat + 3 * n),
        out_specs=(vm,) * len(out_shape),
    )(g_pack, *g_mats, *ws, *ms, *vs)


SMALL_VECTORS = (("norm_g", 0, 1024), ("mem_norm_g", 8, 1024), ("conv_w", 16, 512), ("conv_b", 32, 512),
                 ("b_rg", 292, 512), ("b_ig", 552, 512), ("lru_lambda", 556, 512), ("q_norm_g", 560, 64),
                 ("k_norm_g", 561, 64), ("sinks", 562, 4), ("xq_norm_g", 563, 64), ("xk_norm_g", 564, 64),
                 ("out_norm_g", 565, 1024))
SMALL_MATRICES = (("w_rg", 36), ("w_ig", 296))
LOSS_ROW = 573
SMALL_ROWS = 576


def _pack(parts, rows_):
    flat = jnp.concatenate([p.reshape(-1) for p in parts])
    return jnp.pad(flat, (0, rows_ * 128 - flat.shape[0])).reshape(rows_, 128)


def _block_diag_gates(w_rg, w_ig):
    eye = jnp.eye(4, dtype=w_rg.dtype)

    def bd(w4):
        return (w4[:, :, None, :] * eye[:, None, :, None]).reshape(256, 256)

    return jnp.stack([jnp.concatenate([bd(w_rg[4 * h:4 * h + 4]), bd(w_ig[4 * h:4 * h + 4])], axis=1) for h in (0, 1)])


def _diag_blocks(g):
    g6 = g.reshape(2, 4, HEAD, 2, 4, HEAD)
    d = (g6 * jnp.eye(4, dtype=g.dtype)[None, :, None, None, :, None]).sum(axis=4)
    return d[:, :, :, 0].reshape(8, HEAD, HEAD), d[:, :, :, 1].reshape(8, HEAD, HEAD)


def _rope_tables(seq):
    pos = np.arange(seq, dtype=np.float32)
    inv_freq = (np.float32(ROPE_THETA) ** (-(np.arange(0, ROPE_DIM, 2, dtype=np.float32) / np.float32(ROPE_DIM)))
                ).astype(np.float32)
    ang = (pos[:, None] * inv_freq[None, :]).astype(np.float32)
    cos, sin = np.cos(ang).astype(np.float32), np.sin(ang).astype(np.float32)
    z = lambda n: np.zeros((seq, n), np.float32)
    c64 = np.concatenate([cos, cos, np.ones((seq, HEAD - ROPE_DIM), np.float32)], axis=1)
    s1_64 = np.concatenate([-sin, z(HEAD - 8)], axis=1)
    s2_64 = np.concatenate([z(8), sin, z(HEAD - ROPE_DIM)], axis=1)
    return tuple(jnp.asarray(np.concatenate([t, t], axis=1)) for t in (c64, s1_64, s2_64))


def kernel(x, mem, norm_g, mem_norm_g, w_in, conv_w, conv_b, w_rg, b_rg, w_ig, b_ig, lru_lambda, q_norm_g, k_norm_g, sinks, w_mem_kv, xq_norm_g, xk_norm_g, out_norm_g, w_out, loss_target, m_norm_g, m_mem_norm_g, m_w_in, m_conv_w, m_conv_b, m_w_rg, m_b_rg, m_w_ig, m_b_ig, m_lru_lambda, m_q_norm_g, m_k_norm_g, m_sinks, m_w_mem_kv, m_xq_norm_g, m_xk_norm_g, m_out_norm_g, m_w_out, v_norm_g, v_mem_norm_g, v_w_in, v_conv_w, v_conv_b, v_w_rg, v_b_rg, v_w_ig, v_b_ig, v_lru_lambda, v_q_norm_g, v_k_norm_g, v_sinks, v_w_mem_kv, v_xq_norm_g, v_xk_norm_g, v_out_norm_g, v_w_out):
    seq = x.shape[1]
    xs, tgt, mems = x[0], loss_target[0], mem[0]

    cw_sh = jnp.pad(conv_w[0], ((0, 4), (0, 0)))
    win_t, wout, wkv, cw_all = gather_weights(w_in[0].T, w_out[0], w_mem_kv[0], cw_sh)
    cw = cw_all.reshape(N_CHIPS, 8, 128)[:, :CONV_K].transpose(1, 0, 2).reshape(CONV_K, LRU_W)

    rc, rs1, rs2 = _rope_tables(seq)
    wg = _block_diag_gates(w_rg[0], w_ig[0]).astype(_MXU)
    qg = jnp.tile(q_norm_g, (1, 2))
    kg = jnp.tile(k_norm_g, (1, 2))
    xqg = jnp.tile(xq_norm_g, (1, 4))
    xkg = jnp.tile(xk_norm_g, (1, 4))

    km, vm = mem_fwd(mems, mem_norm_g, wkv, xkg)
    proj, ya, yb, yc, ycat, xn, dout, pswa, pmem, psink, gates, a_all, loss8 = layer_fwd(
        xs, tgt, rc, rs1, rs2, norm_g, win_t, cw, conv_b, wg, b_rg, b_ig, lru_lambda, qg, kg, xqg, sinks, km, vm,
        out_norm_g, wout)
    (g_wout,) = wgrad_reduce(ycat, dout, 512, [], None, (), "wgrad_out")
    (gx, dproj, g_wg, dkm, dvm, g_ng, g_og, g_cb, g_brg, g_big, g_lam, g_cw, g_qn, g_kn, g_xqn, g_sink) = layer_bwd(
        xs, dout, proj, ya, yb, yc, pswa, pmem, psink, gates, a_all, rc, rs1, rs2, norm_g, win_t, cw, wg, lru_lambda, qg,
        kg, xqg, km, vm, out_norm_g, wout)
    g_wkv, g_mng, g_xkn = mem_bwd(mems, mem_norm_g, wkv, xkg, dkm, dvm)

    g_wrg, g_wig = _diag_blocks(g_wg)
    small_g = _pack([g_ng, g_mng, g_cw, g_cb, g_wrg, g_brg, g_wig, g_big, g_lam, g_qn, g_kn, g_sink, g_xqn[:, :128],
                     g_xkn[:, :128], g_og, loss8[0:1]], SMALL_ROWS)
    early = [g_wout.reshape(N_CHIPS, 2, D_MODEL // 8, D_MODEL), g_wkv.reshape(N_CHIPS, 2, D_MODEL // 8, 2 * XATT_W)]
    g_win_t, r_out, r_kv, r_small = wgrad_reduce(dproj, xn, 256, early, small_g.reshape(2, SMALL_ROWS // 2, 128),
                                                 (0, 2, 5, 7, 8), "wgrad_in")
    r_in = reduce_grads(g_win_t.reshape(N_CHIPS, 2, D_IN // 8, D_MODEL), "reduce_w_in", 6)

    r_small = r_small.reshape(SMALL_ROWS, 128)
    loss = r_small[LOSS_ROW, 0]
    grads = {}
    for name, row in SMALL_MATRICES:
        grads[name] = r_small[row:row + 256].reshape(1, LRU_BLOCKS, HEAD, HEAD)
    weights = dict(norm_g=norm_g, mem_norm_g=mem_norm_g, w_in=w_in, conv_w=conv_w, conv_b=conv_b, w_rg=w_rg, b_rg=b_rg,
                   w_ig=w_ig, b_ig=b_ig, lru_lambda=lru_lambda, q_norm_g=q_norm_g, k_norm_g=k_norm_g, sinks=sinks,
                   w_mem_kv=w_mem_kv, xq_norm_g=xq_norm_g, xk_norm_g=xk_norm_g, out_norm_g=out_norm_g, w_out=w_out)
    ms = dict(norm_g=m_norm_g, mem_norm_g=m_mem_norm_g, w_in=m_w_in, conv_w=m_conv_w, conv_b=m_conv_b, w_rg=m_w_rg,
              b_rg=m_b_rg, w_ig=m_w_ig, b_ig=m_b_ig, lru_lambda=m_lru_lambda, q_norm_g=m_q_norm_g, k_norm_g=m_k_norm_g,
              sinks=m_sinks, w_mem_kv=m_w_mem_kv, xq_norm_g=m_xq_norm_g, xk_norm_g=m_xk_norm_g,
              out_norm_g=m_out_norm_g, w_out=m_w_out)
    vs = dict(norm_g=v_norm_g, mem_norm_g=v_mem_norm_g, w_in=v_w_in, conv_w=v_conv_w, conv_b=v_conv_b, w_rg=v_w_rg,
              b_rg=v_b_rg, w_ig=v_w_ig, b_ig=v_b_ig, lru_lambda=v_lru_lambda, q_norm_g=v_q_norm_g, k_norm_g=v_k_norm_g,
              sinks=v_sinks, w_mem_kv=v_w_mem_kv, xq_norm_g=v_xq_norm_g, xk_norm_g=v_xk_norm_g,
              out_norm_g=v_out_norm_g, w_out=v_w_out)

    delta, new_m, new_v = {}, {}, {}
    g2, d2, m2, v2 = adamw(w_in[0].T, r_in.reshape(D_IN // 4, D_MODEL), m_w_in[0].T, v_w_in[0].T, "adamw_w_in")
    grads["w_in"], delta["w_in"], new_m["w_in"], new_v["w_in"] = g2.T[None], d2.T[None], m2.T[None], v2.T[None]
    for name, summed in (("w_mem_kv", r_kv.reshape(D_MODEL // 4, 2 * XATT_W)), ("w_out", r_out.reshape(D_MODEL // 4, D_MODEL))):
        res = adamw(weights[name][0], summed, ms[name][0], vs[name][0], "adamw_" + name)
        grads[name], delta[name], new_m[name], new_v[name] = (r[None] for r in res)
    vec_names = [n for n, _, _ in SMALL_VECTORS]
    small_names = vec_names + [n for n, _ in SMALL_MATRICES]
    res = adamw_vectors(r_small, [grads[n] for n, _ in SMALL_MATRICES], [weights[n] for n in small_names],
                        [ms[n] for n in small_names], [vs[n] for n in small_names])
    nvec, nall = len(vec_names), len(small_names)
    grads.update(zip(vec_names, res[:nvec]))
    delta.update(zip(small_names, res[nvec:nvec + nall]))
    new_m.update(zip(small_names, res[nvec + nall:nvec + 2 * nall]))
    new_v.update(zip(small_names, res[nvec + 2 * nall:]))

    order = ("norm_g", "mem_norm_g", "w_in", "conv_w", "conv_b", "w_rg", "b_rg", "w_ig", "b_ig", "lru_lambda",
             "q_norm_g", "k_norm_g", "sinks", "w_mem_kv", "xq_norm_g", "xk_norm_g", "out_norm_g", "w_out")
    return (loss, gx[None], *[grads[n] for n in order], *[delta[n] for n in order], *[new_m[n] for n in order],
            *[new_v[n] for n in order])
```

```python
import jax
import jax.numpy as jnp
import numpy as np
from jax import lax
from jax.experimental import pallas as pl
from jax.experimental.pallas import tpu as pltpu

F32 = jnp.float32
_MXU = jnp.bfloat16
_WIRE = jnp.bfloat16

D_MODEL = 1024
MEM_LEN = 256
HEAD = 64
LRU_W = 512
LRU_BLOCKS = 8
CONV_K = 4
LRU_C = 8.0
SWA_W = 256
KV_W = 128
XATT_W = 256
BLOCK = 128
D_IN = 2304
ROPE_THETA = 500000.0
ROPE_DIM = 16
EPS = 1e-6
NEG_INF = -1e30
C_LRUX, C_LRUG, C_SQ, C_SK, C_SV, C_SWAG, C_XQ, C_XG = 0, 512, 1024, 1280, 1408, 1536, 1792, 2048
G_Q, G_K, G_XQ, G_XK, GAINS_W = 0, 128, 256, 512, 768

ADAM_LR, ADAM_B1, ADAM_B2, ADAM_EPS, ADAM_WD, ADAM_STEP = 0.001, 0.9, 0.999, 1e-08, 0.01, 10

N_CHIPS = 4
ROW_TILE = 256
VMEM_LIMIT = 56 * 1024 * 1024
ADAM_BLOCK_BYTES = 1280 * 1024
MESH = pl.DeviceIdType.MESH


def _mm(a, b):
    return jnp.dot(a.astype(_MXU), b.astype(_MXU), preferred_element_type=F32)


def _mm_nt(a, b):
    return lax.dot_general(a.astype(_MXU), b.astype(_MXU), (((1,), (1,)), ((), ())), preferred_element_type=F32)


def _mm_tn(a, b):
    return lax.dot_general(a.astype(_MXU), b.astype(_MXU), (((0,), (0,)), ((), ())), preferred_element_type=F32)


def _group_matrix(width):
    r = lax.shift_right_logical(lax.broadcasted_iota(jnp.int32, (width, width), 0), 6)
    c = lax.shift_right_logical(lax.broadcasted_iota(jnp.int32, (width, width), 1), 6)
    return (r == c).astype(_MXU)


def _seg_mean(x, gm):
    return jnp.dot(x.astype(_MXU), gm, preferred_element_type=F32) * (1.0 / HEAD)


def _row_mean(x):
    return jnp.mean(x, axis=-1, keepdims=True)


def _col_sum(x):
    return jnp.sum(x, axis=0, keepdims=True)


def _sigmoid(x):
    return jax.nn.sigmoid(x)


def _softplus(z):
    e = jnp.exp(-jnp.abs(z))
    u = 1.0 + e
    log1p_e = jnp.where(u == 1.0, e, jnp.log(u) * (e / (u - 1.0)))
    return jnp.maximum(z, 0.0) + log1p_e


def _rope(t, c, s1, s2):
    return t * c + pltpu.roll(t, 120, 1) * s1 + pltpu.roll(t, 8, 1) * s2


def _rope_bwd(d, c, s1, s2):
    return d * c + pltpu.roll(d * s1, 8, 1) + pltpu.roll(d * s2, 120, 1)


def _fold_heads(v):
    out = v
    for k in range(1, v.shape[1] // HEAD):
        out = out + pltpu.roll(v, HEAD * k, 1)
    return out


def _lane_mask(width, lo, hi):
    lane = lax.broadcasted_iota(jnp.int32, (1, width), 1)
    return ((lane >= lo) & (lane < hi)).astype(F32)


def _swa_mask(first_block):
    qi = lax.broadcasted_iota(jnp.int32, (BLOCK, 2 * BLOCK), 0)
    kj = lax.broadcasted_iota(jnp.int32, (BLOCK, 2 * BLOCK), 1)
    rel = qi + BLOCK - kj
    ok = (rel >= 0) & (rel < BLOCK)
    return ok & (jnp.logical_not(first_block) | (kj >= BLOCK))


def _place_kv(t, scale):
    lo = t * (_lane_mask(KV_W, 0, HEAD) * scale)
    hi = t * (_lane_mask(KV_W, HEAD, KV_W) * scale)
    return [a.astype(_MXU) for a in (lo, pltpu.roll(lo, HEAD, 1), pltpu.roll(hi, HEAD, 1), hi)]


def _unplace_kv(d):
    return (_lane_mask(KV_W, 0, HEAD) * (d[0] + pltpu.roll(d[1], HEAD, 1))
            + _lane_mask(KV_W, HEAD, KV_W) * (d[3] + pltpu.roll(d[2], HEAD, 1)))


def _swa_probs(qh, ka, mask, sink):
    s = _mm_nt(qh, ka)
    s = jnp.where(mask, s, NEG_INF)
    m = jnp.maximum(jnp.max(s, axis=-1, keepdims=True), sink)
    p = jnp.exp(s - m)
    esink = jnp.exp(sink - m)
    inv = 1.0 / (jnp.sum(p, axis=-1, keepdims=True) + esink)
    return p * inv, esink * inv


def _mem_probs(s_all):
    out = []
    for j in range(4):
        s = s_all[:, MEM_LEN * j:MEM_LEN * (j + 1)]
        p = jnp.exp(s - jnp.max(s, axis=-1, keepdims=True))
        out.append(p * (1.0 / jnp.sum(p, axis=-1, keepdims=True)))
    return out


def _head_rows(t, scale):
    return jnp.concatenate([t * (_lane_mask(XATT_W, HEAD * j, HEAD * (j + 1)) * scale) for j in range(4)], axis=0)


def _lru_gates(xc, wg_ref, brg, big, lam):
    p0 = _mm(xc[:, :256], wg_ref[0])
    p1 = _mm(xc[:, 256:], wg_ref[1])
    rg = _sigmoid(jnp.concatenate([p0[:, :256], p1[:, :256]], axis=1) + brg)
    ig = _sigmoid(jnp.concatenate([p0[:, 256:], p1[:, 256:]], axis=1) + big)
    sp = _softplus(-lam)
    la = (-LRU_C) * rg * sp
    a = jnp.exp(la)
    th = jnp.tanh(la)
    one_minus_a2 = (-2.0 * th) / (1.0 - th)
    return rg, ig, sp, a, jnp.sqrt(one_minus_a2)


def _const_spec(shape, single=False):
    zeros = (0,) * len(shape)
    if single:
        return pl.BlockSpec(shape, lambda i: zeros, pipeline_mode=pl.Buffered(1))
    return pl.BlockSpec(shape, lambda i: zeros)


def _chip_of(x, y):
    return 2 * x + y


def _partners(x, y, c):
    north = c == 1
    near = (jnp.where(north, 1 - x, x), jnp.where(north, y, 1 - y))
    far = (jnp.where(north, x, 1 - x), jnp.where(north, 1 - y, y))
    return near, far, (1 - x, 1 - y)


def gather_weights(win_t, wout, wkv, conv_w, w_rg, w_ig, head_gains):
    arrs = (win_t, wout, wkv)
    n = len(arrs)
    pieces = [(a, 0, arr.shape[0] // 2) for a, arr in enumerate(arrs)]
    npc = len(pieces)

    def body(a0, a1, a2, cw_in, wrg_ref, wig_ref, q_ref, k_ref, xq_ref, xk_ref, o0, o1, o2, cw_out, wg_ref, gn_ref,
             s0, s1, s2, cw, ocw, send, recv, lsem):
        ins, outs = (s0, s1, s2), (o0, o1, o2)
        for src, dst in zip((a0, a1, a2), ins):
            dst[...] = src[...].astype(dst.dtype)
        cw[...] = jnp.zeros(cw.shape, F32)
        cw[0:CONV_K, :] = cw_in[0]
        x, y, c = lax.axis_index("x"), lax.axis_index("y"), lax.axis_index("c")
        sibling = (x, y, 1 - c)
        near, far, diag = _partners(x, y, c)
        chips = [near, far, diag]
        me = _chip_of(x, y)

        def landed(p, chip, half):
            a, off, rows_ = pieces[p]
            r = ins[a].shape[0]
            return outs[a].at[pl.ds(pl.multiple_of(chip * r + half * (r // 2) + off, 16), rows_)]

        def mine(p):
            a, off, rows_ = pieces[p]
            return ins[a].at[pl.ds(pl.multiple_of(c * (ins[a].shape[0] // 2) + off, 16), rows_)]

        def copy(k, src, dst, to):
            return pltpu.make_async_remote_copy(src_ref=src, dst_ref=dst, send_sem=send.at[k], recv_sem=recv.at[k],
                                                device_id=to, device_id_type=MESH)

        def cw_rows(chip):
            return ocw.at[pl.ds(pl.multiple_of(chip * 8, 8), 8)]

        locals_ = []
        for a in range(n):
            r = ins[a].shape[0]
            locals_.append(pltpu.make_async_copy(ins[a], outs[a].at[pl.ds(pl.multiple_of(me * r, 16), r)], lsem.at[a]))
        locals_.append(pltpu.make_async_copy(cw, cw_rows(me), lsem.at[n]))
        for cp in locals_:
            cp.start()

        sent = []
        for p in range(npc):
            for j in range(2):
                sent.append(copy(p * 6 + j, mine(p), landed(p, me, c), (*chips[j], c)))
        for j, chip in enumerate(chips):
            sent.append(copy(npc * 6 + j, cw, cw_rows(me), (*chip, c)))
        for cp in sent:
            cp.start()

        gn_ref[...] = jnp.concatenate([q_ref[...]] * 2 + [k_ref[...]] * 2 + [xq_ref[...]] * 4 + [xk_ref[...]] * 4,
                                      axis=1)
        zeros = lambda lanes: [jnp.zeros((HEAD, lanes), F32)] if lanes else []
        for h in range(2):
            for b in range(4):
                row = []
                for w_ref in (wrg_ref, wig_ref):
                    row += zeros(HEAD * b) + [w_ref[0, 4 * h + b]] + zeros(HEAD * (3 - b))
                wg_ref[h, HEAD * b:HEAD * (b + 1), :] = jnp.concatenate(row, axis=1).astype(wg_ref.dtype)

        for j in range(3):
            for p in range(npc):
                got = landed(p, _chip_of(*chips[j]), c)
                copy(p * 6 + j, got, got, sibling).wait_recv()
                if j == 0:
                    sent.append(copy(p * 6 + 2, got, got, (*far, c)))
                    sent[-1].start()
                sent.append(copy(p * 6 + 3 + j, got, got, sibling))
                sent[-1].start()
        for p in range(npc):
            for j in range(3):
                got = landed(p, _chip_of(*chips[(1, 0, 2)[j]]), 1 - c)
                copy(p * 6 + 3 + j, got, got, sibling).wait_recv()
        for j, chip in enumerate(chips):
            got = cw_rows(_chip_of(*chip))
            copy(npc * 6 + j, got, got, (*chip, c)).wait_recv()
        for cp in sent:
            cp.wait_send()
        for cp in locals_:
            cp.wait()
        for chip in range(N_CHIPS):
            cw_out[:, 128 * chip:128 * (chip + 1)] = ocw[8 * chip:8 * chip + CONV_K, :]

    vm = pl.BlockSpec(memory_space=pltpu.VMEM)
    out_shape = tuple(jax.ShapeDtypeStruct((N_CHIPS * a.shape[0],) + a.shape[1:], _MXU) for a in arrs) + (
        jax.ShapeDtypeStruct((CONV_K, LRU_W), F32), jax.ShapeDtypeStruct((2, 256, 512), _MXU),
        jax.ShapeDtypeStruct((1, GAINS_W), F32))
    n_rdma = npc * 6 + 3
    return pl.pallas_call(
        body, name="gather_weights", out_shape=out_shape,
        in_specs=[vm] * 10, out_specs=(pl.BlockSpec(memory_space=pl.ANY),) * n + (vm, vm, vm),
        scratch_shapes=[pltpu.VMEM(a.shape, _MXU) for a in arrs] + [
            pltpu.VMEM((8, 128), F32), pltpu.VMEM((N_CHIPS * 8, 128), F32),
            pltpu.SemaphoreType.DMA((n_rdma,)), pltpu.SemaphoreType.DMA((n_rdma,)), pltpu.SemaphoreType.DMA((n + 1,))],
        compiler_params=pltpu.CompilerParams(vmem_limit_bytes=VMEM_LIMIT),
    )(win_t, wout, wkv, conv_w, w_rg, w_ig, *head_gains)


def mem_fwd(mem, mem_g, wkv, gains):
    def body(mem_ref, g_ref, w_ref, gn_ref, km_ref, vm_ref):
        mem_v = mem_ref[...]
        mn = mem_v * lax.rsqrt(_row_mean(mem_v * mem_v) + EPS) * g_ref[...]
        mkv = _mm(mn, w_ref[...])
        kpre = mkv[:, :XATT_W]
        gm = _group_matrix(XATT_W)
        km = kpre * lax.rsqrt(_seg_mean(kpre * kpre, gm) + EPS) * gn_ref[:, G_XK:GAINS_W]
        km_ref[...] = _head_rows(km, 0.125).astype(km_ref.dtype)
        vm_ref[...] = _head_rows(mkv[:, XATT_W:], 1.0).astype(vm_ref.dtype)

    vm = pl.BlockSpec(memory_space=pltpu.VMEM)
    rows_shape = jax.ShapeDtypeStruct((4 * MEM_LEN, XATT_W), _MXU)
    return pl.pallas_call(
        body, name="mem_fwd", out_shape=(rows_shape, rows_shape), in_specs=[vm] * 4, out_specs=(vm, vm),
    )(mem, mem_g, wkv, gains)


def mem_bwd(mem, mem_g, wkv, gains, dkm, dvm, g_gates, loss8, vectors):
    names = tuple(vectors)
    first_row = {name: (row, width) for name, row, width in SMALL_VECTORS}

    def body(mem_ref, g_ref, w_ref, gn_ref, dkm_ref, dvm_ref, gg_ref, loss_ref, *rest):
        vec_refs, (gw_ref, pk_ref) = rest[:len(names)], rest[len(names):]
        pk_ref[...] = jnp.zeros(pk_ref.shape, F32)

        def put(name, src):
            row, width = first_row[name]
            per_row = 1 if width < 128 else src.shape[1] // 128
            for t in range(src.shape[0]):
                for r in range(per_row):
                    at = row + per_row * t + r
                    pk_ref[at:at + 1, :] = src[t:t + 1, 128 * r:128 * (r + 1)]

        for name, ref in zip(names, vec_refs):
            put(name, ref)
        pk_ref[LOSS_ROW:LOSS_ROW + 1, :] = loss_ref[0:1, :]
        upper = lax.broadcasted_iota(jnp.int32, (HEAD, 128), 1) >= HEAD
        for h in range(2):
            for b in range(4):
                rg = gg_ref[h, HEAD * b:HEAD * (b + 1), 128 * (b // 2):128 * (b // 2 + 1)]
                ig = gg_ref[h, HEAD * b:HEAD * (b + 1), 256 + 128 * (b // 2):256 + 128 * (b // 2 + 1)]
                if b % 2:
                    rg = pltpu.roll(rg, HEAD, axis=1)
                else:
                    ig = pltpu.roll(ig, HEAD, axis=1)
                at = GATES_ROW + HEAD * (4 * h + b)
                pk_ref[at:at + HEAD, :] = jnp.where(upper, ig, rg)

        mem_v = mem_ref[...]
        mh = mem_v * lax.rsqrt(_row_mean(mem_v * mem_v) + EPS)
        mn = mh * g_ref[...]
        mkv = _mm(mn, w_ref[...])
        kpre = mkv[:, :XATT_W]
        gm = _group_matrix(XATT_W)
        rk = lax.rsqrt(_seg_mean(kpre * kpre, gm) + EPS)
        kn = kpre * rk
        dk = jnp.zeros((MEM_LEN, XATT_W), F32)
        dv = jnp.zeros((MEM_LEN, XATT_W), F32)
        for j in range(4):
            mj = _lane_mask(XATT_W, HEAD * j, HEAD * (j + 1))
            dk = dk + dkm_ref[:, MEM_LEN * j:MEM_LEN * (j + 1)].T * (mj * 0.125)
            dv = dv + dvm_ref[:, MEM_LEN * j:MEM_LEN * (j + 1)].T * mj
        put("xk_norm_g", _fold_heads(_col_sum(dk * kn)))
        dkn = dk * gn_ref[:, G_XK:GAINS_W]
        dkpre = rk * (dkn - kn * _seg_mean(dkn * kn, gm))
        dmkv = jnp.concatenate([dkpre, dv], axis=1)
        gw_ref[...] = _mm_tn(mn, dmkv)
        dmn = _mm_nt(dmkv, w_ref[...])
        put("mem_norm_g", _col_sum(dmn * mh))

    vm = pl.BlockSpec(memory_space=pltpu.VMEM)
    return pl.pallas_call(
        body, name="mem_bwd",
        out_shape=(jax.ShapeDtypeStruct((D_MODEL, 2 * XATT_W), F32), jax.ShapeDtypeStruct((SMALL_ROWS, 128), F32)),
        in_specs=[vm] * (8 + len(names)), out_specs=(vm, vm),
    )(mem, mem_g, wkv, gains, dkm, dvm, g_gates, loss8, *vectors.values())


def layer_fwd(x, tgt, rc, rs1, rs2, ng, win_t, cw, cb, wg, brg, big, lam, gains, sinks, km, vm, og, wout):
    seq = x.shape[0]
    tm = min(ROW_TILE, seq)
    nt = seq // tm
    nb = tm // BLOCK

    def body(x_ref, t_ref, c_ref, s1_ref, s2_ref, ng_ref, win_ref, cw_ref, cb_ref, wg_ref, brg_ref, big_ref, lam_ref,
             gn_ref, sink_ref, km_ref, vm_ref, og_ref, wout_ref,
             proj_ref, ya_ref, yb_ref, yc_ref, ycat_ref, xn_ref, dout_ref, pswa_ref, pmem_ref, psink_ref, gates_ref,
             a_ref, loss_ref,
             ext_ref, b_scr, hc_ref, kp_ref, vp_ref, lacc_ref):
        i = pl.program_id(0)

        @pl.when(i == 0)
        def _():
            ext_ref[0:8, :] = jnp.zeros((8, LRU_W), F32)
            hc_ref[...] = jnp.zeros_like(hc_ref)
            kp_ref[...] = jnp.zeros_like(kp_ref)
            vp_ref[...] = jnp.zeros_like(vp_ref)
            lacc_ref[...] = jnp.zeros_like(lacc_ref)

        xv = x_ref[...]
        xn = (xv * lax.rsqrt(_row_mean(xv * xv) + EPS) * ng_ref[...]).astype(_MXU)
        xn_ref[...] = xn.astype(xn_ref.dtype)
        proj_ref[...] = _mm_nt(xn, win_ref[...])

        u = proj_ref[:, C_LRUX:C_LRUX + LRU_W]
        ext_ref[8:8 + tm, :] = u
        xc = cb_ref[...]
        for k in range(CONV_K):
            xc = xc + cw_ref[k:k + 1, :] * ext_ref[pl.ds(5 + k, tm), :]
        ext_ref[0:8, :] = u[tm - 8:tm, :]
        rg, ig, sp, a, sq = _lru_gates(xc, wg_ref, brg_ref[...], big_ref[...], lam_ref[...])
        for k, t in enumerate((xc, rg, ig, sq)):
            gates_ref[:, LRU_W * k:LRU_W * (k + 1)] = t.astype(gates_ref.dtype)
        a_ref[...] = a
        b_scr[...] = sq * (ig * xc)
        row8 = lax.broadcasted_iota(jnp.int32, (8, LRU_W), 0)

        def scan_step(g, carry):
            r0 = pl.multiple_of(g * 8, 8)
            av = a_ref[pl.ds(r0, 8), :]
            bv = b_scr[pl.ds(r0, 8), :]
            for d in (1, 2, 4):
                a_sh = jnp.where(row8 >= d, pltpu.roll(av, d, 0), 1.0)
                b_sh = jnp.where(row8 >= d, pltpu.roll(bv, d, 0), 0.0)
                bv = bv + av * b_sh
                av = av * a_sh
            hv = bv + av * carry
            ya_ref[pl.ds(r0, 8), :] = hv
            return hv[7:8, :]

        hc_ref[0:1, :] = lax.fori_loop(0, tm // 8, scan_step, hc_ref[0:1, :], unroll=True)

        gm128 = _group_matrix(KV_W)
        cv, s1v, s2v = c_ref[...], s1_ref[...], s2_ref[...]

        def head_norm_rope(t, g):
            n = t * lax.rsqrt(_seg_mean(t * t, gm128) + EPS)
            return _rope(n * g, cv, s1v, s2v)

        qs_ = (head_norm_rope(proj_ref[:, C_SQ:C_SQ + 128], gn_ref[:, G_Q:G_K]).astype(_MXU),
               head_norm_rope(proj_ref[:, C_SQ + 128:C_SQ + 256], gn_ref[:, G_Q:G_K]).astype(_MXU))
        kr = head_norm_rope(proj_ref[:, C_SK:C_SK + KV_W], gn_ref[:, G_K:G_XQ])
        sv = proj_ref[:, C_SV:C_SV + KV_W]
        ka = _place_kv(jnp.concatenate([kp_ref[...], kr], axis=0), 0.125)
        va = _place_kv(jnp.concatenate([vp_ref[...], sv], axis=0), 1.0)
        kp_ref[...] = kr[tm - BLOCK:tm, :]
        vp_ref[...] = sv[tm - BLOCK:tm, :]
        lane128 = lax.broadcasted_iota(jnp.int32, (1, 128), 1)
        for b in range(nb):
            mask = _swa_mask((i == 0) & (b == 0)) if b == 0 else _swa_mask(False)
            band = slice(BLOCK * b, BLOCK * b + 2 * BLOCK)
            blk = slice(BLOCK * b, BLOCK * (b + 1))
            psink = jnp.zeros((BLOCK, 128), F32)
            for j in range(4):
                p, pk = _swa_probs(qs_[j // 2][blk], ka[j][band], mask, sink_ref[0, j])
                pswa_ref[blk, 2 * BLOCK * j:2 * BLOCK * (j + 1)] = p.astype(pswa_ref.dtype)
                psink = jnp.where(lane128 == j, pk, psink)
            psink_ref[blk, :] = psink
            for h in range(2):
                yb_ref[blk, KV_W * h:KV_W * (h + 1)] = _mm(
                    pswa_ref[blk, 4 * BLOCK * h:4 * BLOCK * (h + 1)],
                    jnp.concatenate([va[2 * h][band], va[2 * h + 1][band]], axis=0))

        gm256 = _group_matrix(XATT_W)
        xq = proj_ref[:, C_XQ:C_XQ + XATT_W]
        qx = xq * lax.rsqrt(_seg_mean(xq * xq, gm256) + EPS) * gn_ref[:, G_XQ:G_XK]
        pm = _mem_probs(_mm_nt(qx, km_ref[...]))
        for j in range(4):
            pmem_ref[:, MEM_LEN * j:MEM_LEN * (j + 1)] = pm[j].astype(pmem_ref.dtype)
        yc = _mm(pmem_ref[...], vm_ref[...])
        yc_ref[...] = yc

        def gated(y, g, gate):
            return y * lax.rsqrt(_row_mean(y * y) + EPS) * g * (gate * _sigmoid(gate))

        ogv = og_ref[...]
        za = gated(ya_ref[...], ogv[:, :512], proj_ref[:, C_LRUG:C_LRUG + LRU_W])
        zb = gated(yb_ref[...], ogv[:, 512:768], proj_ref[:, C_SWAG:C_SWAG + SWA_W])
        zc = gated(yc, ogv[:, 768:], proj_ref[:, C_XG:C_XG + XATT_W])
        ycat_ref[:, 0:512] = za.astype(ycat_ref.dtype)
        ycat_ref[:, 512:768] = zb.astype(ycat_ref.dtype)
        ycat_ref[:, 768:1024] = zc.astype(ycat_ref.dtype)
        out = xv + _mm(ycat_ref[...], wout_ref[...])
        err = out - t_ref[...]
        dout_ref[...] = (err * (1.0 / D_MODEL)).astype(dout_ref.dtype)
        lacc_ref[...] = lacc_ref[...] + (0.5 / D_MODEL) * jnp.sum(err * err)

        @pl.when(i == nt - 1)
        def _():
            loss_ref[...] = lacc_ref[...]

    def rows(ncol):
        return pl.BlockSpec((tm, ncol), lambda i: (i, 0))

    in_specs = [rows(D_MODEL), rows(D_MODEL), rows(128), rows(128), rows(128),
                _const_spec((1, D_MODEL)), _const_spec((D_IN, D_MODEL), True), _const_spec((CONV_K, LRU_W)),
                _const_spec((1, LRU_W)), _const_spec((2, 256, 512), True), _const_spec((1, LRU_W)),
                _const_spec((1, LRU_W)), _const_spec((1, LRU_W)), _const_spec((1, GAINS_W)), pl.BlockSpec(memory_space=pltpu.SMEM),
                _const_spec((4 * MEM_LEN, XATT_W), True), _const_spec((4 * MEM_LEN, XATT_W), True),
                _const_spec((1, D_MODEL)), _const_spec((D_MODEL, D_MODEL), True)]
    out_shape = (jax.ShapeDtypeStruct((seq, D_IN), F32), jax.ShapeDtypeStruct((seq, LRU_W), F32),
                 jax.ShapeDtypeStruct((seq, SWA_W), F32), jax.ShapeDtypeStruct((seq, XATT_W), F32),
                 jax.ShapeDtypeStruct((seq, D_MODEL), _MXU), jax.ShapeDtypeStruct((seq, D_MODEL), _MXU),
                 jax.ShapeDtypeStruct((seq, D_MODEL), _MXU), jax.ShapeDtypeStruct((seq, 4 * 2 * BLOCK), _MXU),
                 jax.ShapeDtypeStruct((seq, 4 * MEM_LEN), _MXU), jax.ShapeDtypeStruct((seq, 128), F32),
                 jax.ShapeDtypeStruct((seq, 4 * LRU_W), _MXU), jax.ShapeDtypeStruct((seq, LRU_W), F32),
                 jax.ShapeDtypeStruct((8, 128), F32))
    out_specs = (rows(D_IN), rows(LRU_W), rows(SWA_W), rows(XATT_W), rows(D_MODEL), rows(D_MODEL), rows(D_MODEL),
                 rows(4 * 2 * BLOCK), rows(4 * MEM_LEN), rows(128), rows(4 * LRU_W), rows(LRU_W),
                 _const_spec((8, 128)))
    scratch = [pltpu.VMEM((tm + 8, LRU_W), F32), pltpu.VMEM((tm, LRU_W), F32),
               pltpu.VMEM((8, LRU_W), F32), pltpu.VMEM((BLOCK, KV_W), F32), pltpu.VMEM((BLOCK, KV_W), F32),
               pltpu.VMEM((8, 128), F32)]
    return pl.pallas_call(
        body, name="layer_fwd", grid=(nt,), out_shape=out_shape, in_specs=in_specs, out_specs=out_specs,
        scratch_shapes=scratch,
        compiler_params=pltpu.CompilerParams(dimension_semantics=("arbitrary",), vmem_limit_bytes=VMEM_LIMIT),
    )(x, tgt, rc, rs1, rs2, ng, win_t, cw, cb, wg, brg, big, lam, gains, sinks, km, vm, og, wout)


def wgrad_reduce(lhs, rhs, bigs, g_small, stage_at, name):
    seq, ncol = rhs.shape
    blk = 256
    nblk = lhs.shape[1] // blk
    nres = len(bigs) + (g_small is not None)

    def body(l_ref, r_ref, *refs):
        if nres:
            _hosted_reduce(pl.program_id(0), stage_at, refs[:nres], refs[nres + 1:2 * nres + 1], refs[2 * nres + 1:],
                           g_small is not None)
        refs[nres][...] = _mm_tn(l_ref[...], r_ref[...])

    red_shape, scratch = _hosted_reduce_shapes(bigs, g_small) if nres else ([], [])
    vm = pl.BlockSpec(memory_space=pltpu.VMEM)
    hbm = pl.BlockSpec(memory_space=pl.ANY)
    operands = list(bigs) + ([] if g_small is None else [g_small])
    return pl.pallas_call(
        body, name=name, grid=(nblk,),
        out_shape=(jax.ShapeDtypeStruct((lhs.shape[1], ncol), F32), *red_shape),
        in_specs=[pl.BlockSpec((seq, blk), lambda j: (0, j)), _const_spec((seq, ncol), True)] + [hbm] * len(bigs)
        + [vm] * (g_small is not None),
        out_specs=(pl.BlockSpec((blk, ncol), lambda j: (j, 0)),) + (hbm,) * len(red_shape),
        scratch_shapes=scratch,
        compiler_params=pltpu.CompilerParams(dimension_semantics=("arbitrary",), vmem_limit_bytes=VMEM_LIMIT),
    )(lhs, rhs, *operands)


def layer_bwd(x, dout, proj, ya, yb, yc, pswa, pmem, psink, gates, a_all, rc, rs1, rs2, ng, win_t, cw, wg, lam, gains,
              km, vm, og, wout):
    seq = x.shape[0]
    tm = min(ROW_TILE, seq)
    nt = seq // tm
    nb = tm // BLOCK

    def body(x_ref, dout_ref, proj_ref, ya_ref, yb_ref, yc_ref, pswa_ref, pmem_ref, psink_ref, gates_ref, a_ref,
             c_ref, s1_ref, s2_ref,
             yah_ref, kvh_ref, ch_ref, s1h_ref, s2h_ref,
             ng_ref, win_ref, cw_ref, wg_ref, lam_ref, gn_ref, km_ref, vm_ref, og_ref, wout_ref,
             gx_ref, dproj_ref, gwg_ref, dkm_ref, dvm_ref, gng_ref, gog_ref, gcb_ref, gbrg_ref, gbig_ref, glam_ref,
             gcw_ref, gqn_ref, gkn_ref, gxqn_ref, gsink_ref,
             hext_ref, aext_ref, an_scr, dh_scr, g_scr, dxc_ext, gcar_ref, dkcar_ref, dvcar_ref):
        i = pl.program_id(0)
        tile = nt - 1 - i
        first_tile = tile == 0

        @pl.when(i == 0)
        def _():
            for r in (gwg_ref, dkm_ref, dvm_ref, gng_ref, gog_ref, gcb_ref, gbrg_ref, gbig_ref, glam_ref, gcw_ref,
                      gqn_ref, gkn_ref, gxqn_ref, gsink_ref, gcar_ref, dkcar_ref, dvcar_ref):
                r[...] = jnp.zeros_like(r)
            dxc_ext[tm:tm + 8, :] = jnp.zeros((8, LRU_W), F32)
            aext_ref[tm:tm + 8, :] = jnp.zeros((8, LRU_W), F32)

        xv = x_ref[...]
        dov = dout_ref[...]
        dz = _mm_nt(dov, wout_ref[...])
        ogv = og_ref[...]

        def group_bwd(y, gate, g, dzg):
            r = lax.rsqrt(_row_mean(y * y) + EPS)
            n = y * r
            sg = _sigmoid(gate)
            dgate = dzg * (n * g) * (sg * (1.0 + gate * (1.0 - sg)))
            dng = dzg * (gate * sg)
            dn = dng * g
            return r * (dn - n * _row_mean(dn * n)), dgate, _col_sum(dng * n)

        dya, dga, goa = group_bwd(ya_ref[...], proj_ref[:, C_LRUG:C_LRUG + LRU_W], ogv[:, :512], dz[:, :512])
        dyb, dgb, gob = group_bwd(yb_ref[...], proj_ref[:, C_SWAG:C_SWAG + SWA_W], ogv[:, 512:768], dz[:, 512:768])
        dyc, dgc, goc = group_bwd(yc_ref[...], proj_ref[:, C_XG:C_XG + XATT_W], ogv[:, 768:], dz[:, 768:])
        gog_ref[...] += jnp.concatenate([goa, gob, goc], axis=1)
        dproj_ref[:, C_LRUG:C_LRUG + LRU_W] = dga.astype(dproj_ref.dtype)
        dproj_ref[:, C_SWAG:C_SWAG + SWA_W] = dgb.astype(dproj_ref.dtype)
        dproj_ref[:, C_XG:C_XG + XATT_W] = dgc.astype(dproj_ref.dtype)

        gm256 = _group_matrix(XATT_W)
        xq = proj_ref[:, C_XQ:C_XQ + XATT_W]
        rq = lax.rsqrt(_seg_mean(xq * xq, gm256) + EPS)
        qn = xq * rq
        qx = qn * gn_ref[:, G_XQ:G_XK]
        qxb = qx.astype(_MXU)
        dycb = dyc.astype(_MXU)
        dp_all = _mm_nt(dycb, vm_ref[...])
        dsm = []
        for j in range(4):
            pj = pmem_ref[:, MEM_LEN * j:MEM_LEN * (j + 1)].astype(F32)
            dp = dp_all[:, MEM_LEN * j:MEM_LEN * (j + 1)]
            dsm.append((pj * (dp - jnp.sum(pj * dp, axis=-1, keepdims=True))).astype(_MXU))
        ds_all = jnp.concatenate(dsm, axis=1)
        dvm_ref[...] += _mm_tn(dycb, pmem_ref[...])
        dkm_ref[...] += _mm_tn(qxb, ds_all)
        dqx = _mm(ds_all, km_ref[...])
        gxqn_ref[...] += _col_sum(dqx * qn)
        dqn = dqx * gn_ref[:, G_XQ:G_XK]
        dproj_ref[:, C_XQ:C_XQ + XATT_W] = (rq * (dqn - qn * _seg_mean(dqn * qn, gm256))).astype(dproj_ref.dtype)

        gm128 = _group_matrix(KV_W)
        cv, s1v, s2v = c_ref[...], s1_ref[...], s2_ref[...]

        def head_norm(t):
            r = lax.rsqrt(_seg_mean(t * t, gm128) + EPS)
            return t * r, r

        qn_, qr_ = zip(head_norm(proj_ref[:, C_SQ:C_SQ + 128]), head_norm(proj_ref[:, C_SQ + 128:C_SQ + 256]))
        qrope = [_rope(qn_[h] * gn_ref[:, G_Q:G_K], cv, s1v, s2v).astype(_MXU) for h in range(2)]
        kn, krr = head_norm(proj_ref[:, C_SK:C_SK + KV_W])
        kr = _rope(kn * gn_ref[:, G_K:G_XQ], cv, s1v, s2v)
        khn, _ = head_norm(kvh_ref[:, 0:KV_W])
        khr = _rope(khn * gn_ref[:, G_K:G_XQ], ch_ref[...], s1h_ref[...], s2h_ref[...])
        ka = _place_kv(jnp.concatenate([khr, kr], axis=0), 0.125)
        va = _place_kv(jnp.concatenate([kvh_ref[:, KV_W:2 * KV_W], proj_ref[:, C_SV:C_SV + KV_W]], axis=0), 1.0)
        lane128 = lax.broadcasted_iota(jnp.int32, (1, 128), 1)
        gsink = jnp.zeros((1, 128), F32)
        dk_band, dv_band, dq_blk = [], [], []
        for b in range(nb):
            band = slice(BLOCK * b, BLOCK * b + 2 * BLOCK)
            blk = slice(BLOCK * b, BLOCK * (b + 1))
            dka, dva, dsb = [], [], []
            deltas = jnp.zeros((BLOCK, 128), F32)
            for j in range(4):
                qh = qrope[j // 2][blk]
                doh = dyb[blk, KV_W * (j // 2):KV_W * (j // 2 + 1)].astype(_MXU)
                pb = pswa_ref[blk, 2 * BLOCK * j:2 * BLOCK * (j + 1)]
                p = pb.astype(F32)
                dp = _mm_nt(doh, va[j][band])
                delta = jnp.sum(p * dp, axis=-1, keepdims=True)
                ds = (p * (dp - delta)).astype(_MXU)
                deltas = jnp.where(lane128 == j, delta, deltas)
                dva.append(_mm_tn(pb, doh))
                dka.append(_mm_tn(ds, qh))
                dsb.append(ds)
            gsink = gsink - _col_sum(psink_ref[blk, :] * deltas)
            dk_band.append(_unplace_kv(dka) * 0.125)
            dv_band.append(_unplace_kv(dva))
            dq_blk.append([_mm(jnp.concatenate(dsb[2 * h:2 * h + 2], axis=1),
                               jnp.concatenate([ka[2 * h][band], ka[2 * h + 1][band]], axis=0)) for h in range(2)])
        gsink_ref[...] += gsink
        dk_rows = [dk_band[b][BLOCK:] + (dk_band[b + 1][:BLOCK] if b + 1 < nb else dkcar_ref[...]) for b in range(nb)]
        dv_rows = [dv_band[b][BLOCK:] + (dv_band[b + 1][:BLOCK] if b + 1 < nb else dvcar_ref[...]) for b in range(nb)]
        dkcar_ref[...] = dk_band[0][:BLOCK]
        dvcar_ref[...] = dv_band[0][:BLOCK]
        dkg = _rope_bwd(jnp.concatenate(dk_rows, axis=0), cv, s1v, s2v)
        gkn = _col_sum(dkg * kn)
        dkn = dkg * gn_ref[:, G_K:G_XQ]
        dproj_ref[:, C_SK:C_SK + KV_W] = (krr * (dkn - kn * _seg_mean(dkn * kn, gm128))).astype(dproj_ref.dtype)
        dproj_ref[:, C_SV:C_SV + KV_W] = jnp.concatenate(dv_rows, axis=0).astype(dproj_ref.dtype)
        gqn = jnp.zeros((1, 128), F32)
        for h in range(2):
            dqg = _rope_bwd(jnp.concatenate([dq_blk[b][h] for b in range(nb)], axis=0), cv, s1v, s2v)
            gqn = gqn + _col_sum(dqg * qn_[h])
            dqn_ = dqg * gn_ref[:, G_Q:G_K]
            dproj_ref[:, C_SQ + 128 * h:C_SQ + 128 * (h + 1)] = (
                qr_[h] * (dqn_ - qn_[h] * _seg_mean(dqn_ * qn_[h], gm128))).astype(dproj_ref.dtype)
        gqn_ref[...] += gqn
        gkn_ref[...] += gkn

        u = proj_ref[:, C_LRUX:C_LRUX + LRU_W]
        xc, rg, ig, sq = (gates_ref[:, LRU_W * k:LRU_W * (k + 1)].astype(F32) for k in range(4))
        a = a_ref[...]
        sp = _softplus(-lam_ref[...])
        hext_ref[0:8, :] = jnp.where(first_tile, 0.0, yah_ref[...])
        hext_ref[8:8 + tm, :] = ya_ref[...]
        hprev = hext_ref[pl.ds(7, tm), :]
        aext_ref[0:tm, :] = a
        an_scr[...] = aext_ref[pl.ds(1, tm), :]
        dh_scr[...] = dya
        dh_scr[tm - 1:tm, :] = dh_scr[tm - 1:tm, :] + gcar_ref[0:1, :]
        row8 = lax.broadcasted_iota(jnp.int32, (8, LRU_W), 0)

        def scan_step(gi, carry):
            r0 = pl.multiple_of((tm // 8 - 1 - gi) * 8, 8)
            av = an_scr[pl.ds(r0, 8), :]
            bv = dh_scr[pl.ds(r0, 8), :]
            for d in (1, 2, 4):
                a_sh = jnp.where(row8 < 8 - d, pltpu.roll(av, 8 - d, 0), 1.0)
                b_sh = jnp.where(row8 < 8 - d, pltpu.roll(bv, 8 - d, 0), 0.0)
                bv = bv + av * b_sh
                av = av * a_sh
            gv = bv + av * carry
            g_scr[pl.ds(r0, 8), :] = gv
            return gv[0:1, :]

        g0 = lax.fori_loop(0, tm // 8, scan_step, jnp.zeros((1, LRU_W), F32), unroll=True)
        gcar_ref[0:1, :] = a[0:1, :] * g0
        gv = g_scr[...]
        da = gv * hprev
        dig = gv * sq * xc
        dxc = gv * sq * ig
        dla = da * a - gv * (ig * xc) * ((a * a) / sq)
        drg = dla * ((-LRU_C) * sp)
        glam_ref[...] += _col_sum(dla * rg)
        dpr = drg * rg * (1.0 - rg)
        dpi = dig * ig * (1.0 - ig)
        gbrg_ref[...] += _col_sum(dpr)
        gbig_ref[...] += _col_sum(dpi)
        dpre0 = jnp.concatenate([dpr[:, :256], dpi[:, :256]], axis=1).astype(_MXU)
        dpre1 = jnp.concatenate([dpr[:, 256:], dpi[:, 256:]], axis=1).astype(_MXU)
        gwg_ref[0] += _mm_tn(xc[:, :256], dpre0)
        gwg_ref[1] += _mm_tn(xc[:, 256:], dpre1)
        dxc = dxc + jnp.concatenate([_mm_nt(dpre0, wg_ref[0]), _mm_nt(dpre1, wg_ref[1])], axis=1)
        gcb_ref[...] += _col_sum(dxc)
        dxc_ext[0:tm, :] = dxc
        du = jnp.zeros((tm, LRU_W), F32)
        for k in range(CONV_K):
            later = dxc_ext[pl.ds(3 - k, tm), :]
            gcw_ref[k:k + 1, :] += _col_sum(later * u)
            du = du + cw_ref[k:k + 1, :] * later
        dxc_ext[tm:tm + 8, :] = dxc[0:8, :]
        dproj_ref[:, C_LRUX:C_LRUX + LRU_W] = du.astype(dproj_ref.dtype)

        dxn = _mm(dproj_ref[...], win_ref[...])
        rx = lax.rsqrt(_row_mean(xv * xv) + EPS)
        xh = xv * rx
        gng_ref[...] += _col_sum(dxn * xh)
        dxh = dxn * ng_ref[...]
        gx_ref[...] = dov.astype(F32) + rx * (dxh - xh * _row_mean(dxh * xh))

        @pl.when(i == nt - 1)
        def _():
            glam_ref[...] = glam_ref[...] * (LRU_C * _sigmoid(-lam_ref[...]))
            for r in (gqn_ref, gkn_ref, gxqn_ref):
                r[...] = _fold_heads(r[...])

    def rows(ncol, arr_cols_block=0):
        return pl.BlockSpec((tm, ncol), lambda i: (nt - 1 - i, arr_cols_block))

    def halo(nrow, ncol, colblk=0):
        per = tm // nrow
        return pl.BlockSpec((nrow, ncol), lambda i: (jnp.maximum((nt - 1 - i) * per - 1, 0), colblk))

    in_specs = [rows(D_MODEL), rows(D_MODEL), rows(D_IN), rows(LRU_W), rows(SWA_W), rows(XATT_W),
                rows(4 * 2 * BLOCK), rows(4 * MEM_LEN), rows(128), rows(4 * LRU_W), rows(LRU_W),
                rows(128), rows(128), rows(128),
                halo(8, LRU_W), halo(BLOCK, 2 * KV_W, C_SK // (2 * KV_W)),
                halo(BLOCK, 128), halo(BLOCK, 128), halo(BLOCK, 128),
                _const_spec((1, D_MODEL)), _const_spec((D_IN, D_MODEL), True), _const_spec((CONV_K, LRU_W)),
                _const_spec((2, 256, 512), True), _const_spec((1, LRU_W)), _const_spec((1, GAINS_W)),
                _const_spec((4 * MEM_LEN, XATT_W), True), _const_spec((4 * MEM_LEN, XATT_W), True),
                _const_spec((1, D_MODEL)), _const_spec((D_MODEL, D_MODEL), True)]
    small = [(2, 256, 512), (XATT_W, 4 * MEM_LEN), (XATT_W, 4 * MEM_LEN), (1, D_MODEL), (1, D_MODEL), (1, LRU_W), (1, LRU_W),
             (1, LRU_W), (1, LRU_W), (CONV_K, LRU_W), (1, 128), (1, 128), (1, XATT_W), (1, 128)]
    out_shape = (jax.ShapeDtypeStruct((seq, D_MODEL), F32), jax.ShapeDtypeStruct((seq, D_IN), _MXU)) + tuple(
        jax.ShapeDtypeStruct(s, F32) for s in small)
    out_specs = (rows(D_MODEL), rows(D_IN)) + tuple(_const_spec(s) for s in small)
    scratch = [pltpu.VMEM((tm + 8, LRU_W), F32), pltpu.VMEM((tm + 8, LRU_W), F32),
               pltpu.VMEM((tm, LRU_W), F32), pltpu.VMEM((tm, LRU_W), F32), pltpu.VMEM((tm, LRU_W), F32),
               pltpu.VMEM((tm + 8, LRU_W), F32),
               pltpu.VMEM((8, LRU_W), F32), pltpu.VMEM((BLOCK, KV_W), F32), pltpu.VMEM((BLOCK, KV_W), F32)]
    return pl.pallas_call(
        body, name="layer_bwd", grid=(nt,), out_shape=out_shape, in_specs=in_specs, out_specs=out_specs,
        scratch_shapes=scratch,
        compiler_params=pltpu.CompilerParams(dimension_semantics=("arbitrary",), vmem_limit_bytes=VMEM_LIMIT),
    )(x, dout, proj, ya, yb, yc, pswa, pmem, psink, gates, a_all, rc, rs1, rs2, ya, proj, rc, rs1, rs2,
      ng, win_t, cw, wg, lam, gains, km, vm, og, wout)


def _reduce_protocol(big, sm, outs, osm, r1, r1s, wire, r2, r2s, wire2, ps, own, send, recv, lsem):
    nbig = len(big)
    x, y, c = lax.axis_index("x"), lax.axis_index("y"), lax.axis_index("c")
    sibling = (x, y, 1 - c)
    near, far, diag = _partners(x, y, c)
    me, near_id, far_id, diag_id = _chip_of(x, y), _chip_of(*near), _chip_of(*far), _chip_of(*diag)

    def copy(k, src, dst, to):
        return pltpu.make_async_remote_copy(src_ref=src, dst_ref=dst, send_sem=send.at[k], recv_sem=recv.at[k],
                                            device_id=to, device_id_type=MESH)

    def sent(stage, a):
        if a == nbig:
            src, dst, to = ((sm.at[1 - c], r1s, sibling), (r1s, r2s.at[0], (*near, c)), (ps, r2s.at[1], (*far, c)),
                            (osm.at[c], osm.at[c], sibling))[stage]
            return [copy(5 * nbig + stage, src, dst, to)]
        if stage == 0:
            return [copy(5 * a, big[a].at[:, 1 - c], r1[a], sibling)]
        if stage == 1:
            return [copy(5 * a + 1, wire[a].at[near_id], r2[a].at[0], (*near, c)),
                    copy(5 * a + 2, wire[a].at[diag_id], r2[a].at[1], (*near, c))]
        if stage == 2:
            return [copy(5 * a + 3, wire2[a], r2[a].at[2], (*far, c))]
        return [copy(5 * a + 4, outs[a].at[c], outs[a].at[c], sibling)]

    arrays = range(nbig + (sm is not None))

    def start(stage, a):
        for cp in sent(stage, a):
            cp.start()

    def arrived(k, ref):
        copy(k, ref, ref, sibling).wait_recv()

    def loads():
        return [pltpu.make_async_copy(big[a].at[:, c], own[a], lsem.at[a]) for a in range(nbig)]

    def stage0():
        for a in arrays:
            start(0, a)
        for cp in loads():
            cp.start()

    def stage1():
        for a in range(nbig):
            loads()[a].wait()
            arrived(5 * a, r1[a])
            for k in range(N_CHIPS):
                r1[a][k] = own[a][k] + r1[a][k]
                wire[a][k] = r1[a][k].astype(wire[a].dtype)
            start(1, a)
        if sm is not None:
            arrived(5 * nbig, r1s)
            r1s[...] = sm[c] + r1s[...]
            start(1, nbig)

    def stage2():
        for a in range(nbig):
            arrived(5 * a + 1, r2[a].at[0])
            arrived(5 * a + 2, r2[a].at[1])
            r1[a][me] = r1[a][me] + r2[a][0].astype(F32)
            wire2[a][...] = (r1[a][far_id] + r2[a][1].astype(F32)).astype(wire2[a].dtype)
            start(2, a)
        if sm is not None:
            arrived(5 * nbig + 1, r2s.at[0])
            ps[...] = r1s[...] + r2s[0]
            start(2, nbig)

    def stage3():
        for a in range(nbig):
            arrived(5 * a + 3, r2[a].at[2])
            outs[a][c] = r1[a][me] + r2[a][2].astype(F32)
            start(3, a)
        if sm is not None:
            arrived(5 * nbig + 2, r2s.at[1])
            osm[c] = ps[...] + r2s[1]
            start(3, nbig)

    def stage4():
        for a in range(nbig):
            arrived(5 * a + 4, outs[a].at[1 - c])
        if sm is not None:
            arrived(5 * nbig + 3, osm.at[1 - c])
        for stage in range(4):
            for a in arrays:
                for cp in sent(stage, a):
                    cp.wait_send()

    return [stage0, stage1, stage2, stage3, stage4]


def _reduce_buffers(bigs, g_small):
    half = [b.shape[2:] for b in bigs]
    sm_half = None if g_small is None else g_small.shape[1:]
    out_shape = [jax.ShapeDtypeStruct((2,) + h, F32) for h in half]
    small = lambda lead: [] if g_small is None else [pltpu.VMEM(lead + sm_half, F32)]
    if g_small is not None:
        out_shape.append(jax.ShapeDtypeStruct(g_small.shape, F32))
    n_sem = 5 * len(bigs) + 4
    scratch = ([pltpu.VMEM((N_CHIPS,) + h, F32) for h in half] + small(())
               + [pltpu.VMEM((N_CHIPS,) + h, _WIRE) for h in half]
               + [pltpu.VMEM((3,) + h, _WIRE) for h in half] + small((2,))
               + [pltpu.VMEM(h, _WIRE) for h in half] + small(())
               + [pltpu.VMEM((N_CHIPS,) + h, F32) for h in half]
               + [pltpu.SemaphoreType.DMA((n_sem,)), pltpu.SemaphoreType.DMA((n_sem,)),
                  pltpu.SemaphoreType.DMA((len(bigs),))])
    return out_shape, scratch


def _split_reduce_refs(refs, nbig, has_small):
    it = iter(refs)
    take = lambda n: [next(it) for _ in range(n)]
    one = lambda: next(it) if has_small else None
    big, sm = take(nbig), one()
    outs, osm = take(nbig), one()
    r1, r1s, wire, r2, r2s, wire2, ps, own = take(nbig), one(), take(nbig), take(nbig), one(), take(nbig), one(), take(nbig)
    send, recv, lsem = take(3)
    return big, sm, outs, osm, r1, r1s, wire, r2, r2s, wire2, ps, own, send, recv, lsem


def _hosted_reduce_shapes(bigs, g_small):
    red_shape, scratch = _reduce_buffers(bigs, g_small)
    nres = len(red_shape)
    return red_shape, [pltpu.VMEM(r.shape, r.dtype) for r in red_shape] + scratch + [pltpu.SemaphoreType.DMA((nres,))]


def _hosted_reduce(step, stage_at, operands, results, scratch, has_small):
    nres = len(results)
    sums, rest, fsem = scratch[:nres], scratch[nres:-1], scratch[-1]
    refs = tuple(operands) + tuple(sums) + tuple(rest)
    for at, stage in zip(stage_at, _reduce_protocol(*_split_reduce_refs(refs, nres - has_small, has_small))):
        pl.when(step == at)(stage)

    @pl.when(step == stage_at[-1])
    def _():
        out = [pltpu.make_async_copy(sums[k], results[k], fsem.at[k]) for k in range(nres)]
        for cp in out:
            cp.start()
        for cp in out:
            cp.wait()


def reduce_grads(big, name, parts):
    chips, halves, rows_, cols = big.shape
    sub = jax.ShapeDtypeStruct((chips, halves, rows_ // parts, cols), big.dtype)

    def body(b_ref, o_ref, *scratch):
        refs = [b_ref.at[:, :, s] for s in range(parts)] + [o_ref.at[:, s] for s in range(parts)] + list(scratch)
        for stage in _reduce_protocol(*_split_reduce_refs(refs, parts, False)):
            stage()

    _, scratch = _reduce_buffers([sub] * parts, None)
    return pl.pallas_call(
        body, name=name, out_shape=jax.ShapeDtypeStruct((halves, parts, rows_ // parts, cols), F32),
        in_specs=[pl.BlockSpec(memory_space=pl.ANY)], out_specs=pl.BlockSpec(memory_space=pltpu.VMEM),
        scratch_shapes=scratch, compiler_params=pltpu.CompilerParams(vmem_limit_bytes=VMEM_LIMIT),
    )(big.reshape(chips, halves, parts, rows_ // parts, cols))


def adamw(w, g, m, v, name):
    rows_, cols = w.shape
    tr = max(t for t in range(8, rows_ + 1, 8) if rows_ % t == 0 and t * cols * 4 <= ADAM_BLOCK_BYTES)

    def body(w_ref, g_ref, m_ref, v_ref, go_ref, d_ref, nm_ref, nv_ref):
        gv = g_ref[...]
        go_ref[...] = gv
        d_ref[...], nm_ref[...], nv_ref[...] = _adam_update(w_ref[...], gv, m_ref[...], v_ref[...])

    spec = pl.BlockSpec((tr, cols), lambda i: (i, 0))
    shp = jax.ShapeDtypeStruct(w.shape, F32)
    return pl.pallas_call(
        body, name=name, grid=(rows_ // tr,), out_shape=(shp,) * 4, in_specs=[spec] * 4, out_specs=(spec,) * 4,
        compiler_params=pltpu.CompilerParams(dimension_semantics=("arbitrary",)),
    )(w, g, m, v)


def _adam_update(w, g, m, v):
    nm = ADAM_B1 * m + (1.0 - ADAM_B1) * g
    nv = ADAM_B2 * v + (1.0 - ADAM_B2) * (g * g)
    m_hat = nm / (1.0 - ADAM_B1 ** ADAM_STEP)
    v_hat = nv / (1.0 - ADAM_B2 ** ADAM_STEP)
    return (-ADAM_LR) * (m_hat / (jnp.sqrt(v_hat) + ADAM_EPS) + ADAM_WD * w), nm, nv


def adamw_vectors(g_pack, ws, ms, vs):
    nvec = len(SMALL_VECTORS)
    n = nvec + len(SMALL_MATRICES)

    def body(*refs):
        pk = refs[0]
        w_refs, m_refs, v_refs = (refs[1 + k * n:1 + (k + 1) * n] for k in range(3))
        g_out, d_out, nm_out, nv_out = (refs[1 + (3 + k) * n:1 + (4 + k) * n] for k in range(4))
        chip = 2 * lax.axis_index("x") + lax.axis_index("y")
        for k, (name, row, width) in enumerate(SMALL_VECTORS):
            if name == "conv_w":
                g = jnp.concatenate([pk[pl.ds(row + 4 * t + chip, 1), :] for t in range(CONV_K)], axis=0)[None]
            elif width >= 128:
                g = jnp.concatenate([pk[row + r:row + r + 1, :] for r in range(width // 128)], axis=1)
            else:
                g = pk[row:row + 1, 0:width]
            g_out[k][...] = g
            d_out[k][...], nm_out[k][...], nv_out[k][...] = _adam_update(w_refs[k][...], g, m_refs[k][...], v_refs[k][...])
        for k in range(nvec, n):
            for b in range(LRU_BLOCKS):
                rows_ = pk[GATES_ROW + HEAD * b:GATES_ROW + HEAD * (b + 1), :]
                g = (pltpu.roll(rows_, HEAD, axis=1) if k > nvec else rows_)[:, 0:HEAD]
                g_out[k][0, b] = g
                d_out[k][0, b], nm_out[k][0, b], nv_out[k][0, b] = _adam_update(
                    w_refs[k][0, b], g, m_refs[k][0, b], v_refs[k][0, b])

    vm = pl.BlockSpec(memory_space=pltpu.VMEM)
    like = [jax.ShapeDtypeStruct(w.shape, F32) for w in ws]
    return pl.pallas_call(
        body, name="adamw_vectors", out_shape=tuple(like * 4), in_specs=[vm] * (1 + 3 * n), out_specs=(vm,) * (4 * n),
    )(g_pack, *ws, *ms, *vs)


SMALL_VECTORS = (("norm_g", 0, 1024), ("mem_norm_g", 8, 1024), ("conv_w", 16, 512), ("conv_b", 32, 512),
                 ("b_rg", 36, 512), ("b_ig", 40, 512), ("lru_lambda", 44, 512), ("q_norm_g", 48, 64),
                 ("k_norm_g", 49, 64), ("sinks", 50, 4), ("xq_norm_g", 51, 64), ("xk_norm_g", 52, 64),
                 ("out_norm_g", 53, 1024))
LOSS_ROW = 61
SMALL_MATRICES = ("w_rg", "w_ig")
GATES_ROW = 64
SMALL_ROWS = GATES_ROW + LRU_BLOCKS * HEAD


def _rope_tables(seq):
    pos = np.arange(seq, dtype=np.float32)
    inv_freq = (np.float32(ROPE_THETA) ** (-(np.arange(0, ROPE_DIM, 2, dtype=np.float32) / np.float32(ROPE_DIM)))
                ).astype(np.float32)
    ang = (pos[:, None] * inv_freq[None, :]).astype(np.float32)
    cos, sin = np.cos(ang).astype(np.float32), np.sin(ang).astype(np.float32)
    z = lambda n: np.zeros((seq, n), np.float32)
    c64 = np.concatenate([cos, cos, np.ones((seq, HEAD - ROPE_DIM), np.float32)], axis=1)
    s1_64 = np.concatenate([-sin, z(HEAD - 8)], axis=1)
    s2_64 = np.concatenate([z(8), sin, z(HEAD - ROPE_DIM)], axis=1)
    return tuple(jnp.asarray(np.concatenate([t, t], axis=1)) for t in (c64, s1_64, s2_64))


def kernel(x, mem, norm_g, mem_norm_g, w_in, conv_w, conv_b, w_rg, b_rg, w_ig, b_ig, lru_lambda, q_norm_g, k_norm_g, sinks, w_mem_kv, xq_norm_g, xk_norm_g, out_norm_g, w_out, loss_target, m_norm_g, m_mem_norm_g, m_w_in, m_conv_w, m_conv_b, m_w_rg, m_b_rg, m_w_ig, m_b_ig, m_lru_lambda, m_q_norm_g, m_k_norm_g, m_sinks, m_w_mem_kv, m_xq_norm_g, m_xk_norm_g, m_out_norm_g, m_w_out, v_norm_g, v_mem_norm_g, v_w_in, v_conv_w, v_conv_b, v_w_rg, v_b_rg, v_w_ig, v_b_ig, v_lru_lambda, v_q_norm_g, v_k_norm_g, v_sinks, v_w_mem_kv, v_xq_norm_g, v_xk_norm_g, v_out_norm_g, v_w_out):
    seq = x.shape[1]
    xs, tgt, mems = x[0], loss_target[0], mem[0]

    win_t, wout, wkv, cw, wg, gains = gather_weights(w_in[0].T, w_out[0], w_mem_kv[0], conv_w, w_rg, w_ig,
                                                     (q_norm_g, k_norm_g, xq_norm_g, xk_norm_g))
    rc, rs1, rs2 = _rope_tables(seq)

    km, vm = mem_fwd(mems, mem_norm_g, wkv, gains)
    proj, ya, yb, yc, ycat, xn, dout, pswa, pmem, psink, gates, a_all, loss8 = layer_fwd(
        xs, tgt, rc, rs1, rs2, norm_g, win_t, cw, conv_b, wg, b_rg, b_ig, lru_lambda, gains, sinks, km, vm,
        out_norm_g, wout)
    (g_wout,) = wgrad_reduce(ycat, dout, [], None, (), "wgrad_out")
    (gx, dproj, g_wg, dkm, dvm, g_ng, g_og, g_cb, g_brg, g_big, g_lam, g_cw, g_qn, g_kn, g_xqn, g_sink) = layer_bwd(
        xs, dout, proj, ya, yb, yc, pswa, pmem, psink, gates, a_all, rc, rs1, rs2, norm_g, win_t, cw, wg, lru_lambda,
        gains, km, vm, out_norm_g, wout)
    g_wkv, small_g = mem_bwd(mems, mem_norm_g, wkv, gains, dkm, dvm, g_wg, loss8, dict(
        norm_g=g_ng, conv_w=g_cw, conv_b=g_cb, b_rg=g_brg, b_ig=g_big, lru_lambda=g_lam, q_norm_g=g_qn, k_norm_g=g_kn,
        sinks=g_sink, xq_norm_g=g_xqn, out_norm_g=g_og))
    early = [g_wout.reshape(N_CHIPS, 2, D_MODEL // 8, D_MODEL), g_wkv.reshape(N_CHIPS, 2, D_MODEL // 8, 2 * XATT_W)]
    g_win_t, r_out, r_kv, r_small = wgrad_reduce(dproj, xn, early, small_g.reshape(2, SMALL_ROWS // 2, 128),
                                                 (0, 2, 5, 7, 8), "wgrad_in")
    r_in = reduce_grads(g_win_t.reshape(N_CHIPS, 2, D_IN // 8, D_MODEL), "reduce_w_in", 6)

    r_small = r_small.reshape(SMALL_ROWS, 128)
    loss = r_small[LOSS_ROW, 0]
    grads = {}
    weights = dict(norm_g=norm_g, mem_norm_g=mem_norm_g, w_in=w_in, conv_w=conv_w, conv_b=conv_b, w_rg=w_rg, b_rg=b_rg,
                   w_ig=w_ig, b_ig=b_ig, lru_lambda=lru_lambda, q_norm_g=q_norm_g, k_norm_g=k_norm_g, sinks=sinks,
                   w_mem_kv=w_mem_kv, xq_norm_g=xq_norm_g, xk_norm_g=xk_norm_g, out_norm_g=out_norm_g, w_out=w_out)
    ms = dict(norm_g=m_norm_g, mem_norm_g=m_mem_norm_g, w_in=m_w_in, conv_w=m_conv_w, conv_b=m_conv_b, w_rg=m_w_rg,
              b_rg=m_b_rg, w_ig=m_w_ig, b_ig=m_b_ig, lru_lambda=m_lru_lambda, q_norm_g=m_q_norm_g, k_norm_g=m_k_norm_g,
              sinks=m_sinks, w_mem_kv=m_w_mem_kv, xq_norm_g=m_xq_norm_g, xk_norm_g=m_xk_norm_g,
              out_norm_g=m_out_norm_g, w_out=m_w_out)
    vs = dict(norm_g=v_norm_g, mem_norm_g=v_mem_norm_g, w_in=v_w_in, conv_w=v_conv_w, conv_b=v_conv_b, w_rg=v_w_rg,
              b_rg=v_b_rg, w_ig=v_w_ig, b_ig=v_b_ig, lru_lambda=v_lru_lambda, q_norm_g=v_q_norm_g, k_norm_g=v_k_norm_g,
              sinks=v_sinks, w_mem_kv=v_w_mem_kv, xq_norm_g=v_xq_norm_g, xk_norm_g=v_xk_norm_g,
              out_norm_g=v_out_norm_g, w_out=v_w_out)

    delta, new_m, new_v = {}, {}, {}
    g2, d2, m2, v2 = adamw(w_in[0].T, r_in.reshape(D_IN // 4, D_MODEL), m_w_in[0].T, v_w_in[0].T, "adamw_w_in")
    grads["w_in"], delta["w_in"], new_m["w_in"], new_v["w_in"] = g2.T[None], d2.T[None], m2.T[None], v2.T[None]
    for name, summed in (("w_mem_kv", r_kv.reshape(D_MODEL // 4, 2 * XATT_W)), ("w_out", r_out.reshape(D_MODEL // 4, D_MODEL))):
        res = adamw(weights[name][0], summed, ms[name][0], vs[name][0], "adamw_" + name)
        grads[name], delta[name], new_m[name], new_v[name] = (r[None] for r in res)
    small_names = [n for n, _, _ in SMALL_VECTORS] + list(SMALL_MATRICES)
    res = adamw_vectors(r_small, [weights[n] for n in small_names], [ms[n] for n in small_names],
                        [vs[n] for n in small_names])
    nall = len(small_names)
    for k, into in enumerate((grads, delta, new_m, new_v)):
        into.update(zip(small_names, res[k * nall:(k + 1) * nall]))

    order = ("norm_g", "mem_norm_g", "w_in", "conv_w", "conv_b", "w_rg", "b_rg", "w_ig", "b_ig", "lru_lambda",
             "q_norm_g", "k_norm_g", "sinks", "w_mem_kv", "xq_norm_g", "xk_norm_g", "out_norm_g", "w_out")
    return (loss, gx[None], *[grads[n] for n in order], *[delta[n] for n in order], *[new_m[n] for n in order],
            *[new_v[n] for n in order])
```

```python
import jax
import jax.numpy as jnp
import numpy as np
from jax import lax
from jax.experimental import pallas as pl
from jax.experimental.pallas import tpu as pltpu

F32 = jnp.float32
_MXU = jnp.bfloat16
_WIRE = jnp.bfloat16

D_MODEL = 1024
MEM_LEN = 256
HEAD = 64
LRU_W = 512
LRU_BLOCKS = 8
CONV_K = 4
LRU_C = 8.0
SWA_W = 256
KV_W = 128
XATT_W = 256
BLOCK = 128
D_IN = 2304
ROPE_THETA = 500000.0
ROPE_DIM = 16
EPS = 1e-6
NEG_INF = -1e30
C_LRUX, C_LRUG, C_SQ, C_SK, C_SV, C_SWAG, C_XQ, C_XG = 0, 512, 1024, 1280, 1408, 1536, 1792, 2048
G_Q, G_K, G_XQ, G_XK, GAINS_W = 0, 128, 256, 512, 768

ADAM_LR, ADAM_B1, ADAM_B2, ADAM_EPS, ADAM_WD, ADAM_STEP = 0.001, 0.9, 0.999, 1e-08, 0.01, 10

N_CHIPS = 4
ROW_TILE = 256
VMEM_LIMIT = 56 * 1024 * 1024
ADAM_BLOCK_BYTES = 640 * 1024
MESH = pl.DeviceIdType.MESH


def _mm(a, b):
    return jnp.dot(a.astype(_MXU), b.astype(_MXU), preferred_element_type=F32)


def _mm_nt(a, b):
    return lax.dot_general(a.astype(_MXU), b.astype(_MXU), (((1,), (1,)), ((), ())), preferred_element_type=F32)


def _mm_tn(a, b):
    return lax.dot_general(a.astype(_MXU), b.astype(_MXU), (((0,), (0,)), ((), ())), preferred_element_type=F32)


def _group_matrix(width):
    r = lax.shift_right_logical(lax.broadcasted_iota(jnp.int32, (width, width), 0), 6)
    c = lax.shift_right_logical(lax.broadcasted_iota(jnp.int32, (width, width), 1), 6)
    return (r == c).astype(_MXU)


def _seg_mean(x, gm):
    return jnp.dot(x.astype(_MXU), gm, preferred_element_type=F32) * (1.0 / HEAD)


def _row_mean(x):
    return jnp.mean(x, axis=-1, keepdims=True)


def _col_sum(x):
    return jnp.sum(x, axis=0, keepdims=True)


def _sigmoid(x):
    return jax.nn.sigmoid(x)


def _softplus(z):
    e = jnp.exp(-jnp.abs(z))
    u = 1.0 + e
    log1p_e = jnp.where(u == 1.0, e, jnp.log(u) * (e / (u - 1.0)))
    return jnp.maximum(z, 0.0) + log1p_e


def _rope(t, c, s1, s2):
    return t * c + pltpu.roll(t, 120, 1) * s1 + pltpu.roll(t, 8, 1) * s2


def _rope_bwd(d, c, s1, s2):
    return d * c + pltpu.roll(d * s1, 8, 1) + pltpu.roll(d * s2, 120, 1)


def _fold_heads(v):
    out = v
    for k in range(1, v.shape[1] // HEAD):
        out = out + pltpu.roll(v, HEAD * k, 1)
    return out


def _lane_mask(width, lo, hi):
    lane = lax.broadcasted_iota(jnp.int32, (1, width), 1)
    return ((lane >= lo) & (lane < hi)).astype(F32)


def _swa_mask(first_block):
    qi = lax.broadcasted_iota(jnp.int32, (BLOCK, 2 * BLOCK), 0)
    kj = lax.broadcasted_iota(jnp.int32, (BLOCK, 2 * BLOCK), 1)
    rel = qi + BLOCK - kj
    ok = (rel >= 0) & (rel < BLOCK)
    return ok & (jnp.logical_not(first_block) | (kj >= BLOCK))


def _place_kv(t, scale):
    lo = t * (_lane_mask(KV_W, 0, HEAD) * scale)
    hi = t * (_lane_mask(KV_W, HEAD, KV_W) * scale)
    return [a.astype(_MXU) for a in (lo, pltpu.roll(lo, HEAD, 1), pltpu.roll(hi, HEAD, 1), hi)]


def _unplace_kv(d):
    return (_lane_mask(KV_W, 0, HEAD) * (d[0] + pltpu.roll(d[1], HEAD, 1))
            + _lane_mask(KV_W, HEAD, KV_W) * (d[3] + pltpu.roll(d[2], HEAD, 1)))


def _swa_probs(qh, ka, mask, sink):
    s = _mm_nt(qh, ka)
    s = jnp.where(mask, s, NEG_INF)
    m = jnp.maximum(jnp.max(s, axis=-1, keepdims=True), sink)
    p = jnp.exp(s - m)
    esink = jnp.exp(sink - m)
    inv = 1.0 / (jnp.sum(p, axis=-1, keepdims=True) + esink)
    return p * inv, esink * inv


def _mem_probs(s_all):
    out = []
    for j in range(4):
        s = s_all[:, MEM_LEN * j:MEM_LEN * (j + 1)]
        p = jnp.exp(s - jnp.max(s, axis=-1, keepdims=True))
        out.append(p * (1.0 / jnp.sum(p, axis=-1, keepdims=True)))
    return out


def _head_rows(t, scale):
    return jnp.concatenate([t * (_lane_mask(XATT_W, HEAD * j, HEAD * (j + 1)) * scale) for j in range(4)], axis=0)


def _lru_gates(xc, wg_ref, brg, big, lam):
    p0 = _mm(xc[:, :256], wg_ref[0])
    p1 = _mm(xc[:, 256:], wg_ref[1])
    rg = _sigmoid(jnp.concatenate([p0[:, :256], p1[:, :256]], axis=1) + brg)
    ig = _sigmoid(jnp.concatenate([p0[:, 256:], p1[:, 256:]], axis=1) + big)
    sp = _softplus(-lam)
    la = (-LRU_C) * rg * sp
    a = jnp.exp(la)
    th = jnp.tanh(la)
    one_minus_a2 = (-2.0 * th) / (1.0 - th)
    return rg, ig, sp, a, jnp.sqrt(one_minus_a2)


def _const_spec(shape, single=False):
    zeros = (0,) * len(shape)
    if single:
        return pl.BlockSpec(shape, lambda i: zeros, pipeline_mode=pl.Buffered(1))
    return pl.BlockSpec(shape, lambda i: zeros)


def _chip_of(x, y):
    return 2 * x + y


def _partners(x, y, c):
    north = c == 1
    near = (jnp.where(north, 1 - x, x), jnp.where(north, y, 1 - y))
    far = (jnp.where(north, x, 1 - x), jnp.where(north, 1 - y, y))
    return near, far, (1 - x, 1 - y)


def gather_weights(win_t, wout, wkv, conv_w, w_rg, w_ig, head_gains):
    arrs = (win_t, wout, wkv)
    n = len(arrs)
    pieces = [(a, k * (arr.shape[0] // (2 * cut)), arr.shape[0] // (2 * cut))
              for a, (arr, cut) in enumerate(zip(arrs, (2, 1, 1))) for k in range(cut)]
    npc = len(pieces)

    def body(a0, a1, a2, cw_in, wrg_ref, wig_ref, q_ref, k_ref, xq_ref, xk_ref, o0, o1, o2, cw_out, wg_ref, gn_ref,
             s0, s1, s2, cw, ocw, send, recv, lsem):
        ins, outs = (s0, s1, s2), (o0, o1, o2)
        for src, dst in zip((a0, a1, a2), ins):
            dst[...] = src[...].astype(dst.dtype)
        cw[...] = jnp.zeros(cw.shape, F32)
        cw[0:CONV_K, :] = cw_in[0]
        x, y, c = lax.axis_index("x"), lax.axis_index("y"), lax.axis_index("c")
        sibling = (x, y, 1 - c)
        near, far, diag = _partners(x, y, c)
        chips = [near, far, diag]
        me = _chip_of(x, y)

        def landed(p, chip, half):
            a, off, rows_ = pieces[p]
            r = ins[a].shape[0]
            return outs[a].at[pl.ds(pl.multiple_of(chip * r + half * (r // 2) + off, 16), rows_)]

        def mine(p):
            a, off, rows_ = pieces[p]
            return ins[a].at[pl.ds(pl.multiple_of(c * (ins[a].shape[0] // 2) + off, 16), rows_)]

        def copy(k, src, dst, to):
            return pltpu.make_async_remote_copy(src_ref=src, dst_ref=dst, send_sem=send.at[k], recv_sem=recv.at[k],
                                                device_id=to, device_id_type=MESH)

        def cw_rows(chip):
            return ocw.at[pl.ds(pl.multiple_of(chip * 8, 8), 8)]

        locals_ = []
        for a in range(n):
            r = ins[a].shape[0]
            locals_.append(pltpu.make_async_copy(ins[a], outs[a].at[pl.ds(pl.multiple_of(me * r, 16), r)], lsem.at[a]))
        locals_.append(pltpu.make_async_copy(cw, cw_rows(me), lsem.at[n]))
        for cp in locals_:
            cp.start()

        sent = []
        for p in range(npc):
            for j in range(2):
                sent.append(copy(p * 6 + j, mine(p), landed(p, me, c), (*chips[j], c)))
        for j, chip in enumerate(chips):
            sent.append(copy(npc * 6 + j, cw, cw_rows(me), (*chip, c)))
        for cp in sent:
            cp.start()

        gn_ref[...] = jnp.concatenate([q_ref[...]] * 2 + [k_ref[...]] * 2 + [xq_ref[...]] * 4 + [xk_ref[...]] * 4,
                                      axis=1)
        zeros = lambda lanes: [jnp.zeros((HEAD, lanes), F32)] if lanes else []
        for h in range(2):
            for b in range(4):
                row = []
                for w_ref in (wrg_ref, wig_ref):
                    row += zeros(HEAD * b) + [w_ref[0, 4 * h + b]] + zeros(HEAD * (3 - b))
                wg_ref[h, HEAD * b:HEAD * (b + 1), :] = jnp.concatenate(row, axis=1).astype(wg_ref.dtype)

        for j in range(3):
            for p in range(npc):
                got = landed(p, _chip_of(*chips[j]), c)
                copy(p * 6 + j, got, got, sibling).wait_recv()
                if j == 0:
                    sent.append(copy(p * 6 + 2, got, got, (*far, c)))
                    sent[-1].start()
                sent.append(copy(p * 6 + 3 + j, got, got, sibling))
                sent[-1].start()
        for p in range(npc):
            for j in range(3):
                got = landed(p, _chip_of(*chips[(1, 0, 2)[j]]), 1 - c)
                copy(p * 6 + 3 + j, got, got, sibling).wait_recv()
        for j, chip in enumerate(chips):
            got = cw_rows(_chip_of(*chip))
            copy(npc * 6 + j, got, got, (*chip, c)).wait_recv()
        for cp in sent:
            cp.wait_send()
        for cp in locals_:
            cp.wait()
        for chip in range(N_CHIPS):
            cw_out[:, 128 * chip:128 * (chip + 1)] = ocw[8 * chip:8 * chip + CONV_K, :]

    vm = pl.BlockSpec(memory_space=pltpu.VMEM)
    out_shape = tuple(jax.ShapeDtypeStruct((N_CHIPS * a.shape[0],) + a.shape[1:], _MXU) for a in arrs) + (
        jax.ShapeDtypeStruct((CONV_K, LRU_W), F32), jax.ShapeDtypeStruct((2, 256, 512), _MXU),
        jax.ShapeDtypeStruct((1, GAINS_W), F32))
    n_rdma = npc * 6 + 3
    return pl.pallas_call(
        body, name="gather_weights", out_shape=out_shape,
        in_specs=[vm] * 10, out_specs=(pl.BlockSpec(memory_space=pl.ANY),) * n + (vm, vm, vm),
        scratch_shapes=[pltpu.VMEM(a.shape, _MXU) for a in arrs] + [
            pltpu.VMEM((8, 128), F32), pltpu.VMEM((N_CHIPS * 8, 128), F32),
            pltpu.SemaphoreType.DMA((n_rdma,)), pltpu.SemaphoreType.DMA((n_rdma,)), pltpu.SemaphoreType.DMA((n + 1,))],
        compiler_params=pltpu.CompilerParams(vmem_limit_bytes=VMEM_LIMIT),
    )(win_t, wout, wkv, conv_w, w_rg, w_ig, *head_gains)


def mem_fwd(mem, mem_g, wkv, gains):
    def body(mem_ref, g_ref, w_ref, gn_ref, km_ref, vm_ref):
        mem_v = mem_ref[...]
        mn = mem_v * lax.rsqrt(_row_mean(mem_v * mem_v) + EPS) * g_ref[...]
        mkv = _mm(mn, w_ref[...])
        kpre = mkv[:, :XATT_W]
        gm = _group_matrix(XATT_W)
        km = kpre * lax.rsqrt(_seg_mean(kpre * kpre, gm) + EPS) * gn_ref[:, G_XK:GAINS_W]
        km_ref[...] = _head_rows(km, 0.125).astype(km_ref.dtype)
        vm_ref[...] = _head_rows(mkv[:, XATT_W:], 1.0).astype(vm_ref.dtype)

    vm = pl.BlockSpec(memory_space=pltpu.VMEM)
    rows_shape = jax.ShapeDtypeStruct((4 * MEM_LEN, XATT_W), _MXU)
    return pl.pallas_call(
        body, name="mem_fwd", out_shape=(rows_shape, rows_shape), in_specs=[vm] * 4, out_specs=(vm, vm),
    )(mem, mem_g, wkv, gains)


def mem_bwd(mem, mem_g, wkv, gains, dkm, dvm, g_gates, loss8, vectors):
    names = tuple(vectors)
    first_row = {name: (row, width) for name, row, width in SMALL_VECTORS}

    def body(mem_ref, g_ref, w_ref, gn_ref, dkm_ref, dvm_ref, gg_ref, loss_ref, *rest):
        vec_refs, (gw_ref, pk_ref) = rest[:len(names)], rest[len(names):]
        pk_ref[...] = jnp.zeros(pk_ref.shape, F32)

        def put(name, src):
            row, width = first_row[name]
            per_row = 1 if width < 128 else src.shape[1] // 128
            for t in range(src.shape[0]):
                for r in range(per_row):
                    at = row + per_row * t + r
                    pk_ref[at:at + 1, :] = src[t:t + 1, 128 * r:128 * (r + 1)]

        for name, ref in zip(names, vec_refs):
            put(name, ref)
        pk_ref[LOSS_ROW:LOSS_ROW + 1, :] = loss_ref[0:1, :]
        upper = lax.broadcasted_iota(jnp.int32, (HEAD, 128), 1) >= HEAD
        for h in range(2):
            for b in range(4):
                rg = gg_ref[h, HEAD * b:HEAD * (b + 1), 128 * (b // 2):128 * (b // 2 + 1)]
                ig = gg_ref[h, HEAD * b:HEAD * (b + 1), 256 + 128 * (b // 2):256 + 128 * (b // 2 + 1)]
                if b % 2:
                    rg = pltpu.roll(rg, HEAD, axis=1)
                else:
                    ig = pltpu.roll(ig, HEAD, axis=1)
                at = GATES_ROW + HEAD * (4 * h + b)
                pk_ref[at:at + HEAD, :] = jnp.where(upper, ig, rg)

        mem_v = mem_ref[...]
        mh = mem_v * lax.rsqrt(_row_mean(mem_v * mem_v) + EPS)
        mn = mh * g_ref[...]
        mkv = _mm(mn, w_ref[...])
        kpre = mkv[:, :XATT_W]
        gm = _group_matrix(XATT_W)
        rk = lax.rsqrt(_seg_mean(kpre * kpre, gm) + EPS)
        kn = kpre * rk
        dk = jnp.zeros((MEM_LEN, XATT_W), F32)
        dv = jnp.zeros((MEM_LEN, XATT_W), F32)
        for j in range(4):
            mj = _lane_mask(XATT_W, HEAD * j, HEAD * (j + 1))
            dk = dk + dkm_ref[:, MEM_LEN * j:MEM_LEN * (j + 1)].T * (mj * 0.125)
            dv = dv + dvm_ref[:, MEM_LEN * j:MEM_LEN * (j + 1)].T * mj
        put("xk_norm_g", _fold_heads(_col_sum(dk * kn)))
        dkn = dk * gn_ref[:, G_XK:GAINS_W]
        dkpre = rk * (dkn - kn * _seg_mean(dkn * kn, gm))
        dmkv = jnp.concatenate([dkpre, dv], axis=1)
        gw_ref[...] = _mm_tn(mn, dmkv)
        dmn = _mm_nt(dmkv, w_ref[...])
        put("mem_norm_g", _col_sum(dmn * mh))

    vm = pl.BlockSpec(memory_space=pltpu.VMEM)
    return pl.pallas_call(
        body, name="mem_bwd",
        out_shape=(jax.ShapeDtypeStruct((D_MODEL, 2 * XATT_W), F32), jax.ShapeDtypeStruct((SMALL_ROWS, 128), F32)),
        in_specs=[vm] * (8 + len(names)), out_specs=(vm, vm),
    )(mem, mem_g, wkv, gains, dkm, dvm, g_gates, loss8, *vectors.values())


def layer_fwd(x, tgt, rc, rs1, rs2, ng, win_t, cw, cb, wg, brg, big, lam, gains, sinks, km, vm, og, wout):
    seq = x.shape[0]
    tm = min(ROW_TILE, seq)
    nt = seq // tm
    nb = tm // BLOCK

    def body(x_ref, t_ref, c_ref, s1_ref, s2_ref, ng_ref, win_ref, cw_ref, cb_ref, wg_ref, brg_ref, big_ref, lam_ref,
             gn_ref, sink_ref, km_ref, vm_ref, og_ref, wout_ref,
             proj_ref, ya_ref, yb_ref, yc_ref, ycat_ref, xn_ref, dout_ref, pswa_ref, pmem_ref, psink_ref, gates_ref,
             a_ref, loss_ref,
             ext_ref, b_scr, hc_ref, kp_ref, vp_ref, lacc_ref):
        i = pl.program_id(0)

        @pl.when(i == 0)
        def _():
            ext_ref[0:8, :] = jnp.zeros((8, LRU_W), F32)
            hc_ref[...] = jnp.zeros_like(hc_ref)
            kp_ref[...] = jnp.zeros_like(kp_ref)
            vp_ref[...] = jnp.zeros_like(vp_ref)
            lacc_ref[...] = jnp.zeros_like(lacc_ref)

        xv = x_ref[...]
        xn = (xv * lax.rsqrt(_row_mean(xv * xv) + EPS) * ng_ref[...]).astype(_MXU)
        xn_ref[...] = xn.astype(xn_ref.dtype)
        proj_ref[...] = _mm_nt(xn, win_ref[...])

        u = proj_ref[:, C_LRUX:C_LRUX + LRU_W]
        ext_ref[8:8 + tm, :] = u
        xc = cb_ref[...]
        for k in range(CONV_K):
            xc = xc + cw_ref[k:k + 1, :] * ext_ref[pl.ds(5 + k, tm), :]
        ext_ref[0:8, :] = u[tm - 8:tm, :]
        rg, ig, sp, a, sq = _lru_gates(xc, wg_ref, brg_ref[...], big_ref[...], lam_ref[...])
        for k, t in enumerate((xc, rg, ig, sq)):
            gates_ref[:, LRU_W * k:LRU_W * (k + 1)] = t.astype(gates_ref.dtype)
        a_ref[...] = a
        b_scr[...] = sq * (ig * xc)
        row8 = lax.broadcasted_iota(jnp.int32, (8, LRU_W), 0)

        def scan_step(g, carry):
            r0 = pl.multiple_of(g * 8, 8)
            av = a_ref[pl.ds(r0, 8), :]
            bv = b_scr[pl.ds(r0, 8), :]
            for d in (1, 2, 4):
                a_sh = jnp.where(row8 >= d, pltpu.roll(av, d, 0), 1.0)
                b_sh = jnp.where(row8 >= d, pltpu.roll(bv, d, 0), 0.0)
                bv = bv + av * b_sh
                av = av * a_sh
            hv = bv + av * carry
            ya_ref[pl.ds(r0, 8), :] = hv
            return hv[7:8, :]

        hc_ref[0:1, :] = lax.fori_loop(0, tm // 8, scan_step, hc_ref[0:1, :], unroll=True)

        gm128 = _group_matrix(KV_W)
        cv, s1v, s2v = c_ref[...], s1_ref[...], s2_ref[...]

        def head_norm_rope(t, g):
            n = t * lax.rsqrt(_seg_mean(t * t, gm128) + EPS)
            return _rope(n * g, cv, s1v, s2v)

        qs_ = (head_norm_rope(proj_ref[:, C_SQ:C_SQ + 128], gn_ref[:, G_Q:G_K]).astype(_MXU),
               head_norm_rope(proj_ref[:, C_SQ + 128:C_SQ + 256], gn_ref[:, G_Q:G_K]).astype(_MXU))
        kr = head_norm_rope(proj_ref[:, C_SK:C_SK + KV_W], gn_ref[:, G_K:G_XQ])
        sv = proj_ref[:, C_SV:C_SV + KV_W]
        ka = _place_kv(jnp.concatenate([kp_ref[...], kr], axis=0), 0.125)
        va = _place_kv(jnp.concatenate([vp_ref[...], sv], axis=0), 1.0)
        kp_ref[...] = kr[tm - BLOCK:tm, :]
        vp_ref[...] = sv[tm - BLOCK:tm, :]
        lane128 = lax.broadcasted_iota(jnp.int32, (1, 128), 1)
        for b in range(nb):
            mask = _swa_mask((i == 0) & (b == 0)) if b == 0 else _swa_mask(False)
            band = slice(BLOCK * b, BLOCK * b + 2 * BLOCK)
            blk = slice(BLOCK * b, BLOCK * (b + 1))
            psink = jnp.zeros((BLOCK, 128), F32)
            for j in range(4):
                p, pk = _swa_probs(qs_[j // 2][blk], ka[j][band], mask, sink_ref[0, j])
                pswa_ref[blk, 2 * BLOCK * j:2 * BLOCK * (j + 1)] = p.astype(pswa_ref.dtype)
                psink = jnp.where(lane128 == j, pk, psink)
            psink_ref[blk, :] = psink
            for h in range(2):
                yb_ref[blk, KV_W * h:KV_W * (h + 1)] = _mm(
                    pswa_ref[blk, 4 * BLOCK * h:4 * BLOCK * (h + 1)],
                    jnp.concatenate([va[2 * h][band], va[2 * h + 1][band]], axis=0))

        gm256 = _group_matrix(XATT_W)
        xq = proj_ref[:, C_XQ:C_XQ + XATT_W]
        qx = xq * lax.rsqrt(_seg_mean(xq * xq, gm256) + EPS) * gn_ref[:, G_XQ:G_XK]
        pm = _mem_probs(_mm_nt(qx, km_ref[...]))
        for j in range(4):
            pmem_ref[:, MEM_LEN * j:MEM_LEN * (j + 1)] = pm[j].astype(pmem_ref.dtype)
        yc = _mm(pmem_ref[...], vm_ref[...])
        yc_ref[...] = yc

        def gated(y, g, gate):
            return y * lax.rsqrt(_row_mean(y * y) + EPS) * g * (gate * _sigmoid(gate))

        ogv = og_ref[...]
        za = gated(ya_ref[...], ogv[:, :512], proj_ref[:, C_LRUG:C_LRUG + LRU_W])
        zb = gated(yb_ref[...], ogv[:, 512:768], proj_ref[:, C_SWAG:C_SWAG + SWA_W])
        zc = gated(yc, ogv[:, 768:], proj_ref[:, C_XG:C_XG + XATT_W])
        ycat_ref[:, 0:512] = za.astype(ycat_ref.dtype)
        ycat_ref[:, 512:768] = zb.astype(ycat_ref.dtype)
        ycat_ref[:, 768:1024] = zc.astype(ycat_ref.dtype)
        out = xv + _mm(ycat_ref[...], wout_ref[...])
        err = out - t_ref[...]
        dout_ref[...] = (err * (1.0 / D_MODEL)).astype(dout_ref.dtype)
        lacc_ref[...] = lacc_ref[...] + (0.5 / D_MODEL) * jnp.sum(err * err)

        @pl.when(i == nt - 1)
        def _():
            loss_ref[...] = lacc_ref[...]

    def rows(ncol):
        return pl.BlockSpec((tm, ncol), lambda i: (i, 0))

    in_specs = [rows(D_MODEL), rows(D_MODEL), rows(128), rows(128), rows(128),
                _const_spec((1, D_MODEL)), _const_spec((D_IN, D_MODEL), True), _const_spec((CONV_K, LRU_W)),
                _const_spec((1, LRU_W)), _const_spec((2, 256, 512), True), _const_spec((1, LRU_W)),
                _const_spec((1, LRU_W)), _const_spec((1, LRU_W)), _const_spec((1, GAINS_W)), pl.BlockSpec(memory_space=pltpu.SMEM),
                _const_spec((4 * MEM_LEN, XATT_W), True), _const_spec((4 * MEM_LEN, XATT_W), True),
                _const_spec((1, D_MODEL)), _const_spec((D_MODEL, D_MODEL), True)]
    out_shape = (jax.ShapeDtypeStruct((seq, D_IN), F32), jax.ShapeDtypeStruct((seq, LRU_W), F32),
                 jax.ShapeDtypeStruct((seq, SWA_W), F32), jax.ShapeDtypeStruct((seq, XATT_W), F32),
                 jax.ShapeDtypeStruct((seq, D_MODEL), _MXU), jax.ShapeDtypeStruct((seq, D_MODEL), _MXU),
                 jax.ShapeDtypeStruct((seq, D_MODEL), _MXU), jax.ShapeDtypeStruct((seq, 4 * 2 * BLOCK), _MXU),
                 jax.ShapeDtypeStruct((seq, 4 * MEM_LEN), _MXU), jax.ShapeDtypeStruct((seq, 128), F32),
                 jax.ShapeDtypeStruct((seq, 4 * LRU_W), _MXU), jax.ShapeDtypeStruct((seq, LRU_W), F32),
                 jax.ShapeDtypeStruct((8, 128), F32))
    out_specs = (rows(D_IN), rows(LRU_W), rows(SWA_W), rows(XATT_W), rows(D_MODEL), rows(D_MODEL), rows(D_MODEL),
                 rows(4 * 2 * BLOCK), rows(4 * MEM_LEN), rows(128), rows(4 * LRU_W), rows(LRU_W),
                 _const_spec((8, 128)))
    scratch = [pltpu.VMEM((tm + 8, LRU_W), F32), pltpu.VMEM((tm, LRU_W), F32),
               pltpu.VMEM((8, LRU_W), F32), pltpu.VMEM((BLOCK, KV_W), F32), pltpu.VMEM((BLOCK, KV_W), F32),
               pltpu.VMEM((8, 128), F32)]
    return pl.pallas_call(
        body, name="layer_fwd", grid=(nt,), out_shape=out_shape, in_specs=in_specs, out_specs=out_specs,
        scratch_shapes=scratch,
        compiler_params=pltpu.CompilerParams(dimension_semantics=("arbitrary",), vmem_limit_bytes=VMEM_LIMIT),
    )(x, tgt, rc, rs1, rs2, ng, win_t, cw, cb, wg, brg, big, lam, gains, sinks, km, vm, og, wout)


def wgrad_reduce(lhs, rhs, bigs, g_small, stage_at, name):
    seq, ncol = rhs.shape
    blk = 256
    nblk = lhs.shape[1] // blk
    nres = len(bigs) + (g_small is not None)

    def body(l_ref, r_ref, *refs):
        if nres:
            _hosted_reduce(pl.program_id(0), stage_at, refs[:nres], refs[nres + 1:2 * nres + 1], refs[2 * nres + 1:],
                           g_small is not None)
        refs[nres][...] = _mm_tn(l_ref[...], r_ref[...])

    red_shape, scratch = _hosted_reduce_shapes(bigs, g_small) if nres else ([], [])
    vm = pl.BlockSpec(memory_space=pltpu.VMEM)
    hbm = pl.BlockSpec(memory_space=pl.ANY)
    operands = list(bigs) + ([] if g_small is None else [g_small])
    return pl.pallas_call(
        body, name=name, grid=(nblk,),
        out_shape=(jax.ShapeDtypeStruct((lhs.shape[1], ncol), F32), *red_shape),
        in_specs=[pl.BlockSpec((seq, blk), lambda j: (0, j)), _const_spec((seq, ncol), True)] + [hbm] * len(bigs)
        + [vm] * (g_small is not None),
        out_specs=(pl.BlockSpec((blk, ncol), lambda j: (j, 0)),) + (hbm,) * len(red_shape),
        scratch_shapes=scratch,
        compiler_params=pltpu.CompilerParams(dimension_semantics=("arbitrary",), vmem_limit_bytes=VMEM_LIMIT),
    )(lhs, rhs, *operands)


def layer_bwd(x, dout, proj, ya, yb, yc, pswa, pmem, psink, gates, a_all, rc, rs1, rs2, ng, win_t, cw, wg, lam, gains,
              km, vm, og, wout):
    seq = x.shape[0]
    tm = min(ROW_TILE, seq)
    nt = seq // tm
    nb = tm // BLOCK

    def body(x_ref, dout_ref, proj_ref, ya_ref, yb_ref, yc_ref, pswa_ref, pmem_ref, psink_ref, gates_ref, a_ref,
             c_ref, s1_ref, s2_ref,
             yah_ref, kvh_ref, ch_ref, s1h_ref, s2h_ref,
             ng_ref, win_ref, cw_ref, wg_ref, lam_ref, gn_ref, km_ref, vm_ref, og_ref, wout_ref,
             gx_ref, dproj_ref, gwg_ref, dkm_ref, dvm_ref, gng_ref, gog_ref, gcb_ref, gbrg_ref, gbig_ref, glam_ref,
             gcw_ref, gqn_ref, gkn_ref, gxqn_ref, gsink_ref,
             hext_ref, aext_ref, an_scr, dh_scr, g_scr, dxc_ext, gcar_ref, dkcar_ref, dvcar_ref):
        i = pl.program_id(0)
        tile = nt - 1 - i
        first_tile = tile == 0

        @pl.when(i == 0)
        def _():
            for r in (gwg_ref, dkm_ref, dvm_ref, gng_ref, gog_ref, gcb_ref, gbrg_ref, gbig_ref, glam_ref, gcw_ref,
                      gqn_ref, gkn_ref, gxqn_ref, gsink_ref, gcar_ref, dkcar_ref, dvcar_ref):
                r[...] = jnp.zeros_like(r)
            dxc_ext[tm:tm + 8, :] = jnp.zeros((8, LRU_W), F32)
            aext_ref[tm:tm + 8, :] = jnp.zeros((8, LRU_W), F32)

        xv = x_ref[...]
        dov = dout_ref[...]
        dz = _mm_nt(dov, wout_ref[...])
        ogv = og_ref[...]

        def group_bwd(y, gate, g, dzg):
            r = lax.rsqrt(_row_mean(y * y) + EPS)
            n = y * r
            sg = _sigmoid(gate)
            dgate = dzg * (n * g) * (sg * (1.0 + gate * (1.0 - sg)))
            dng = dzg * (gate * sg)
            dn = dng * g
            return r * (dn - n * _row_mean(dn * n)), dgate, _col_sum(dng * n)

        dya, dga, goa = group_bwd(ya_ref[...], proj_ref[:, C_LRUG:C_LRUG + LRU_W], ogv[:, :512], dz[:, :512])
        dyb, dgb, gob = group_bwd(yb_ref[...], proj_ref[:, C_SWAG:C_SWAG + SWA_W], ogv[:, 512:768], dz[:, 512:768])
        dyc, dgc, goc = group_bwd(yc_ref[...], proj_ref[:, C_XG:C_XG + XATT_W], ogv[:, 768:], dz[:, 768:])
        gog_ref[...] += jnp.concatenate([goa, gob, goc], axis=1)
        dproj_ref[:, C_LRUG:C_LRUG + LRU_W] = dga.astype(dproj_ref.dtype)
        dproj_ref[:, C_SWAG:C_SWAG + SWA_W] = dgb.astype(dproj_ref.dtype)
        dproj_ref[:, C_XG:C_XG + XATT_W] = dgc.astype(dproj_ref.dtype)

        gm256 = _group_matrix(XATT_W)
        xq = proj_ref[:, C_XQ:C_XQ + XATT_W]
        rq = lax.rsqrt(_seg_mean(xq * xq, gm256) + EPS)
        qn = xq * rq
        qx = qn * gn_ref[:, G_XQ:G_XK]
        qxb = qx.astype(_MXU)
        dycb = dyc.astype(_MXU)
        dp_all = _mm_nt(dycb, vm_ref[...])
        dsm = []
        for j in range(4):
            pj = pmem_ref[:, MEM_LEN * j:MEM_LEN * (j + 1)].astype(F32)
            dp = dp_all[:, MEM_LEN * j:MEM_LEN * (j + 1)]
            dsm.append((pj * (dp - jnp.sum(pj * dp, axis=-1, keepdims=True))).astype(_MXU))
        ds_all = jnp.concatenate(dsm, axis=1)
        dvm_ref[...] += _mm_tn(dycb, pmem_ref[...])
        dkm_ref[...] += _mm_tn(qxb, ds_all)
        dqx = _mm(ds_all, km_ref[...])
        gxqn_ref[...] += _col_sum(dqx * qn)
        dqn = dqx * gn_ref[:, G_XQ:G_XK]
        dproj_ref[:, C_XQ:C_XQ + XATT_W] = (rq * (dqn - qn * _seg_mean(dqn * qn, gm256))).astype(dproj_ref.dtype)

        gm128 = _group_matrix(KV_W)
        cv, s1v, s2v = c_ref[...], s1_ref[...], s2_ref[...]

        def head_norm(t):
            r = lax.rsqrt(_seg_mean(t * t, gm128) + EPS)
            return t * r, r

        qn_, qr_ = zip(head_norm(proj_ref[:, C_SQ:C_SQ + 128]), head_norm(proj_ref[:, C_SQ + 128:C_SQ + 256]))
        qrope = [_rope(qn_[h] * gn_ref[:, G_Q:G_K], cv, s1v, s2v).astype(_MXU) for h in range(2)]
        kn, krr = head_norm(proj_ref[:, C_SK:C_SK + KV_W])
        kr = _rope(kn * gn_ref[:, G_K:G_XQ], cv, s1v, s2v)
        khn, _ = head_norm(kvh_ref[:, 0:KV_W])
        khr = _rope(khn * gn_ref[:, G_K:G_XQ], ch_ref[...], s1h_ref[...], s2h_ref[...])
        ka = _place_kv(jnp.concatenate([khr, kr], axis=0), 0.125)
        va = _place_kv(jnp.concatenate([kvh_ref[:, KV_W:2 * KV_W], proj_ref[:, C_SV:C_SV + KV_W]], axis=0), 1.0)
        lane128 = lax.broadcasted_iota(jnp.int32, (1, 128), 1)
        gsink = jnp.zeros((1, 128), F32)
        dk_band, dv_band, dq_blk = [], [], []
        for b in range(nb):
            band = slice(BLOCK * b, BLOCK * b + 2 * BLOCK)
            blk = slice(BLOCK * b, BLOCK * (b + 1))
            dka, dva, dsb = [], [], []
            deltas = jnp.zeros((BLOCK, 128), F32)
            for j in range(4):
                qh = qrope[j // 2][blk]
                doh = dyb[blk, KV_W * (j // 2):KV_W * (j // 2 + 1)].astype(_MXU)
                pb = pswa_ref[blk, 2 * BLOCK * j:2 * BLOCK * (j + 1)]
                p = pb.astype(F32)
                dp = _mm_nt(doh, va[j][band])
                delta = jnp.sum(p * dp, axis=-1, keepdims=True)
                ds = (p * (dp - delta)).astype(_MXU)
                deltas = jnp.where(lane128 == j, delta, deltas)
                dva.append(_mm_tn(pb, doh))
                dka.append(_mm_tn(ds, qh))
                dsb.append(ds)
            gsink = gsink - _col_sum(psink_ref[blk, :] * deltas)
            dk_band.append(_unplace_kv(dka) * 0.125)
            dv_band.append(_unplace_kv(dva))
            dq_blk.append([_mm(jnp.concatenate(dsb[2 * h:2 * h + 2], axis=1),
                               jnp.concatenate([ka[2 * h][band], ka[2 * h + 1][band]], axis=0)) for h in range(2)])
        gsink_ref[...] += gsink
        dk_rows = [dk_band[b][BLOCK:] + (dk_band[b + 1][:BLOCK] if b + 1 < nb else dkcar_ref[...]) for b in range(nb)]
        dv_rows = [dv_band[b][BLOCK:] + (dv_band[b + 1][:BLOCK] if b + 1 < nb else dvcar_ref[...]) for b in range(nb)]
        dkcar_ref[...] = dk_band[0][:BLOCK]
        dvcar_ref[...] = dv_band[0][:BLOCK]
        dkg = _rope_bwd(jnp.concatenate(dk_rows, axis=0), cv, s1v, s2v)
        gkn = _col_sum(dkg * kn)
        dkn = dkg * gn_ref[:, G_K:G_XQ]
        dproj_ref[:, C_SK:C_SK + KV_W] = (krr * (dkn - kn * _seg_mean(dkn * kn, gm128))).astype(dproj_ref.dtype)
        dproj_ref[:, C_SV:C_SV + KV_W] = jnp.concatenate(dv_rows, axis=0).astype(dproj_ref.dtype)
        gqn = jnp.zeros((1, 128), F32)
        for h in range(2):
            dqg = _rope_bwd(jnp.concatenate([dq_blk[b][h] for b in range(nb)], axis=0), cv, s1v, s2v)
            gqn = gqn + _col_sum(dqg * qn_[h])
            dqn_ = dqg * gn_ref[:, G_Q:G_K]
            dproj_ref[:, C_SQ + 128 * h:C_SQ + 128 * (h + 1)] = (
                qr_[h] * (dqn_ - qn_[h] * _seg_mean(dqn_ * qn_[h], gm128))).astype(dproj_ref.dtype)
        gqn_ref[...] += gqn
        gkn_ref[...] += gkn

        u = proj_ref[:, C_LRUX:C_LRUX + LRU_W]
        xc, rg, ig, sq = (gates_ref[:, LRU_W * k:LRU_W * (k + 1)].astype(F32) for k in range(4))
        a = a_ref[...]
        sp = _softplus(-lam_ref[...])
        hext_ref[0:8, :] = jnp.where(first_tile, 0.0, yah_ref[...])
        hext_ref[8:8 + tm, :] = ya_ref[...]
        hprev = hext_ref[pl.ds(7, tm), :]
        aext_ref[0:tm, :] = a
        an_scr[...] = aext_ref[pl.ds(1, tm), :]
        dh_scr[...] = dya
        dh_scr[tm - 1:tm, :] = dh_scr[tm - 1:tm, :] + gcar_ref[0:1, :]
        row8 = lax.broadcasted_iota(jnp.int32, (8, LRU_W), 0)

        def scan_step(gi, carry):
            r0 = pl.multiple_of((tm // 8 - 1 - gi) * 8, 8)
            av = an_scr[pl.ds(r0, 8), :]
            bv = dh_scr[pl.ds(r0, 8), :]
            for d in (1, 2, 4):
                a_sh = jnp.where(row8 < 8 - d, pltpu.roll(av, 8 - d, 0), 1.0)
                b_sh = jnp.where(row8 < 8 - d, pltpu.roll(bv, 8 - d, 0), 0.0)
                bv = bv + av * b_sh
                av = av * a_sh
            gv = bv + av * carry
            g_scr[pl.ds(r0, 8), :] = gv
            return gv[0:1, :]

        g0 = lax.fori_loop(0, tm // 8, scan_step, jnp.zeros((1, LRU_W), F32), unroll=True)
        gcar_ref[0:1, :] = a[0:1, :] * g0
        gv = g_scr[...]
        da = gv * hprev
        dig = gv * sq * xc
        dxc = gv * sq * ig
        dla = da * a - gv * (ig * xc) * ((a * a) / sq)
        drg = dla * ((-LRU_C) * sp)
        glam_ref[...] += _col_sum(dla * rg)
        dpr = drg * rg * (1.0 - rg)
        dpi = dig * ig * (1.0 - ig)
        gbrg_ref[...] += _col_sum(dpr)
        gbig_ref[...] += _col_sum(dpi)
        dpre0 = jnp.concatenate([dpr[:, :256], dpi[:, :256]], axis=1).astype(_MXU)
        dpre1 = jnp.concatenate([dpr[:, 256:], dpi[:, 256:]], axis=1).astype(_MXU)
        gwg_ref[0] += _mm_tn(xc[:, :256], dpre0)
        gwg_ref[1] += _mm_tn(xc[:, 256:], dpre1)
        dxc = dxc + jnp.concatenate([_mm_nt(dpre0, wg_ref[0]), _mm_nt(dpre1, wg_ref[1])], axis=1)
        gcb_ref[...] += _col_sum(dxc)
        dxc_ext[0:tm, :] = dxc
        du = jnp.zeros((tm, LRU_W), F32)
        for k in range(CONV_K):
            later = dxc_ext[pl.ds(3 - k, tm), :]
            gcw_ref[k:k + 1, :] += _col_sum(later * u)
            du = du + cw_ref[k:k + 1, :] * later
        dxc_ext[tm:tm + 8, :] = dxc[0:8, :]
        dproj_ref[:, C_LRUX:C_LRUX + LRU_W] = du.astype(dproj_ref.dtype)

        dxn = _mm(dproj_ref[...], win_ref[...])
        rx = lax.rsqrt(_row_mean(xv * xv) + EPS)
        xh = xv * rx
        gng_ref[...] += _col_sum(dxn * xh)
        dxh = dxn * ng_ref[...]
        gx_ref[...] = dov.astype(F32) + rx * (dxh - xh * _row_mean(dxh * xh))

        @pl.when(i == nt - 1)
        def _():
            glam_ref[...] = glam_ref[...] * (LRU_C * _sigmoid(-lam_ref[...]))
            for r in (gqn_ref, gkn_ref, gxqn_ref):
                r[...] = _fold_heads(r[...])

    def rows(ncol, arr_cols_block=0):
        return pl.BlockSpec((tm, ncol), lambda i: (nt - 1 - i, arr_cols_block))

    def halo(nrow, ncol, colblk=0):
        per = tm // nrow
        return pl.BlockSpec((nrow, ncol), lambda i: (jnp.maximum((nt - 1 - i) * per - 1, 0), colblk))

    in_specs = [rows(D_MODEL), rows(D_MODEL), rows(D_IN), rows(LRU_W), rows(SWA_W), rows(XATT_W),
                rows(4 * 2 * BLOCK), rows(4 * MEM_LEN), rows(128), rows(4 * LRU_W), rows(LRU_W),
                rows(128), rows(128), rows(128),
                halo(8, LRU_W), halo(BLOCK, 2 * KV_W, C_SK // (2 * KV_W)),
                halo(BLOCK, 128), halo(BLOCK, 128), halo(BLOCK, 128),
                _const_spec((1, D_MODEL)), _const_spec((D_IN, D_MODEL), True), _const_spec((CONV_K, LRU_W)),
                _const_spec((2, 256, 512), True), _const_spec((1, LRU_W)), _const_spec((1, GAINS_W)),
                _const_spec((4 * MEM_LEN, XATT_W), True), _const_spec((4 * MEM_LEN, XATT_W), True),
                _const_spec((1, D_MODEL)), _const_spec((D_MODEL, D_MODEL), True)]
    small = [(2, 256, 512), (XATT_W, 4 * MEM_LEN), (XATT_W, 4 * MEM_LEN), (1, D_MODEL), (1, D_MODEL), (1, LRU_W), (1, LRU_W),
             (1, LRU_W), (1, LRU_W), (CONV_K, LRU_W), (1, 128), (1, 128), (1, XATT_W), (1, 128)]
    out_shape = (jax.ShapeDtypeStruct((seq, D_MODEL), F32), jax.ShapeDtypeStruct((seq, D_IN), _MXU)) + tuple(
        jax.ShapeDtypeStruct(s, F32) for s in small)
    out_specs = (rows(D_MODEL), rows(D_IN)) + tuple(_const_spec(s) for s in small)
    scratch = [pltpu.VMEM((tm + 8, LRU_W), F32), pltpu.VMEM((tm + 8, LRU_W), F32),
               pltpu.VMEM((tm, LRU_W), F32), pltpu.VMEM((tm, LRU_W), F32), pltpu.VMEM((tm, LRU_W), F32),
               pltpu.VMEM((tm + 8, LRU_W), F32),
               pltpu.VMEM((8, LRU_W), F32), pltpu.VMEM((BLOCK, KV_W), F32), pltpu.VMEM((BLOCK, KV_W), F32)]
    return pl.pallas_call(
        body, name="layer_bwd", grid=(nt,), out_shape=out_shape, in_specs=in_specs, out_specs=out_specs,
        scratch_shapes=scratch,
        compiler_params=pltpu.CompilerParams(dimension_semantics=("arbitrary",), vmem_limit_bytes=VMEM_LIMIT),
    )(x, dout, proj, ya, yb, yc, pswa, pmem, psink, gates, a_all, rc, rs1, rs2, ya, proj, rc, rs1, rs2,
      ng, win_t, cw, wg, lam, gains, km, vm, og, wout)


def _reduce_protocol(big, sm, outs, osm, r1, r1s, wire, r2, r2s, wire2, ps, own, send, recv, lsem):
    nbig = len(big)
    x, y, c = lax.axis_index("x"), lax.axis_index("y"), lax.axis_index("c")
    sibling = (x, y, 1 - c)
    near, far, diag = _partners(x, y, c)
    me, near_id, far_id, diag_id = _chip_of(x, y), _chip_of(*near), _chip_of(*far), _chip_of(*diag)

    def copy(k, src, dst, to):
        return pltpu.make_async_remote_copy(src_ref=src, dst_ref=dst, send_sem=send.at[k], recv_sem=recv.at[k],
                                            device_id=to, device_id_type=MESH)

    def sent(stage, a):
        if a == nbig:
            src, dst, to = ((sm.at[1 - c], r1s, sibling), (r1s, r2s.at[0], (*near, c)), (ps, r2s.at[1], (*far, c)),
                            (osm.at[c], osm.at[c], sibling))[stage]
            return [copy(5 * nbig + stage, src, dst, to)]
        if stage == 0:
            return [copy(5 * a, big[a].at[:, 1 - c], r1[a], sibling)]
        if stage == 1:
            return [copy(5 * a + 1, wire[a].at[near_id], r2[a].at[0], (*near, c)),
                    copy(5 * a + 2, wire[a].at[diag_id], r2[a].at[1], (*near, c))]
        if stage == 2:
            return [copy(5 * a + 3, wire2[a], r2[a].at[2], (*far, c))]
        return [copy(5 * a + 4, outs[a].at[c], outs[a].at[c], sibling)]

    arrays = range(nbig + (sm is not None))

    def start(stage, a):
        for cp in sent(stage, a):
            cp.start()

    def arrived(k, ref):
        copy(k, ref, ref, sibling).wait_recv()

    def loads():
        return [pltpu.make_async_copy(big[a].at[:, c], own[a], lsem.at[a]) for a in range(nbig)]

    def stage0():
        for a in arrays:
            start(0, a)
        for cp in loads():
            cp.start()

    def stage1():
        for a in range(nbig):
            loads()[a].wait()
            arrived(5 * a, r1[a])
            for k in range(N_CHIPS):
                r1[a][k] = own[a][k] + r1[a][k]
                wire[a][k] = r1[a][k].astype(wire[a].dtype)
            start(1, a)
        if sm is not None:
            arrived(5 * nbig, r1s)
            r1s[...] = sm[c] + r1s[...]
            start(1, nbig)

    def stage2():
        for a in range(nbig):
            arrived(5 * a + 1, r2[a].at[0])
            arrived(5 * a + 2, r2[a].at[1])
            r1[a][me] = r1[a][me] + r2[a][0].astype(F32)
            wire2[a][...] = (r1[a][far_id] + r2[a][1].astype(F32)).astype(wire2[a].dtype)
            start(2, a)
        if sm is not None:
            arrived(5 * nbig + 1, r2s.at[0])
            ps[...] = r1s[...] + r2s[0]
            start(2, nbig)

    def stage3():
        for a in range(nbig):
            arrived(5 * a + 3, r2[a].at[2])
            outs[a][c] = r1[a][me] + r2[a][2].astype(F32)
            start(3, a)
        if sm is not None:
            arrived(5 * nbig + 2, r2s.at[1])
            osm[c] = ps[...] + r2s[1]
            start(3, nbig)

    def stage4():
        for a in range(nbig):
            arrived(5 * a + 4, outs[a].at[1 - c])
        if sm is not None:
            arrived(5 * nbig + 3, osm.at[1 - c])
        for stage in range(4):
            for a in arrays:
                for cp in sent(stage, a):
                    cp.wait_send()

    return [stage0, stage1, stage2, stage3, stage4]


def _reduce_buffers(bigs, g_small):
    half = [b.shape[2:] for b in bigs]
    sm_half = None if g_small is None else g_small.shape[1:]
    out_shape = [jax.ShapeDtypeStruct((2,) + h, F32) for h in half]
    small = lambda lead: [] if g_small is None else [pltpu.VMEM(lead + sm_half, F32)]
    if g_small is not None:
        out_shape.append(jax.ShapeDtypeStruct(g_small.shape, F32))
    n_sem = 5 * len(bigs) + 4
    scratch = ([pltpu.VMEM((N_CHIPS,) + h, F32) for h in half] + small(())
               + [pltpu.VMEM((N_CHIPS,) + h, _WIRE) for h in half]
               + [pltpu.VMEM((3,) + h, _WIRE) for h in half] + small((2,))
               + [pltpu.VMEM(h, _WIRE) for h in half] + small(())
               + [pltpu.VMEM((N_CHIPS,) + h, F32) for h in half]
               + [pltpu.SemaphoreType.DMA((n_sem,)), pltpu.SemaphoreType.DMA((n_sem,)),
                  pltpu.SemaphoreType.DMA((len(bigs),))])
    return out_shape, scratch


def _split_reduce_refs(refs, nbig, has_small):
    it = iter(refs)
    take = lambda n: [next(it) for _ in range(n)]
    one = lambda: next(it) if has_small else None
    big, sm = take(nbig), one()
    outs, osm = take(nbig), one()
    r1, r1s, wire, r2, r2s, wire2, ps, own = take(nbig), one(), take(nbig), take(nbig), one(), take(nbig), one(), take(nbig)
    send, recv, lsem = take(3)
    return big, sm, outs, osm, r1, r1s, wire, r2, r2s, wire2, ps, own, send, recv, lsem


def _hosted_reduce_shapes(bigs, g_small):
    red_shape, scratch = _reduce_buffers(bigs, g_small)
    nres = len(red_shape)
    return red_shape, [pltpu.VMEM(r.shape, r.dtype) for r in red_shape] + scratch + [pltpu.SemaphoreType.DMA((nres,))]


def _hosted_reduce(step, stage_at, operands, results, scratch, has_small):
    nres = len(results)
    sums, rest, fsem = scratch[:nres], scratch[nres:-1], scratch[-1]
    refs = tuple(operands) + tuple(sums) + tuple(rest)
    for at, stage in zip(stage_at, _reduce_protocol(*_split_reduce_refs(refs, nres - has_small, has_small))):
        pl.when(step == at)(stage)

    @pl.when(step == stage_at[-1])
    def _():
        out = [pltpu.make_async_copy(sums[k], results[k], fsem.at[k]) for k in range(nres)]
        for cp in out:
            cp.start()
        for cp in out:
            cp.wait()


def reduce_grads(big, name, parts):
    chips, halves, rows_, cols = big.shape
    sub = jax.ShapeDtypeStruct((chips, halves, rows_ // parts, cols), big.dtype)

    def body(b_ref, o_ref, *scratch):
        refs = [b_ref.at[:, :, s] for s in range(parts)] + [o_ref.at[:, s] for s in range(parts)] + list(scratch)
        for stage in _reduce_protocol(*_split_reduce_refs(refs, parts, False)):
            stage()

    _, scratch = _reduce_buffers([sub] * parts, None)
    return pl.pallas_call(
        body, name=name, out_shape=jax.ShapeDtypeStruct((halves, parts, rows_ // parts, cols), F32),
        in_specs=[pl.BlockSpec(memory_space=pl.ANY)], out_specs=pl.BlockSpec(memory_space=pltpu.VMEM),
        scratch_shapes=scratch, compiler_params=pltpu.CompilerParams(vmem_limit_bytes=VMEM_LIMIT),
    )(big.reshape(chips, halves, parts, rows_ // parts, cols))


def adamw_matrices(items):
    plan, total = [], 0
    for w, _, _, _ in items:
        rows_, cols = w.shape
        tr = max(t for t in range(8, rows_ + 1, 8) if rows_ % t == 0 and t * cols * 4 <= ADAM_BLOCK_BYTES)
        plan.append((total, rows_ // tr, tr, cols))
        total += rows_ // tr
    nin = 4 * len(items)

    def body(*refs):
        i = pl.program_id(0)
        for k, (first, steps, _, _) in enumerate(plan):
            w_ref, g_ref, m_ref, v_ref = refs[4 * k:4 * k + 4]
            go_ref, d_ref, nm_ref, nv_ref = refs[nin + 4 * k:nin + 4 * k + 4]

            @pl.when((i >= first) & (i < first + steps))
            def _():
                gv = g_ref[...]
                go_ref[...] = gv
                d_ref[...], nm_ref[...], nv_ref[...] = _adam_update(w_ref[...], gv, m_ref[...], v_ref[...])

    specs, shapes = [], []
    for (first, steps, tr, cols), (w, _, _, _) in zip(plan, items):
        spec = pl.BlockSpec((tr, cols), lambda i, first=first, steps=steps: (jnp.clip(i - first, 0, steps - 1), 0))
        specs += [spec] * 4
        shapes += [jax.ShapeDtypeStruct(w.shape, F32)] * 4
    res = pl.pallas_call(
        body, name="adamw_matrices", grid=(total,), out_shape=tuple(shapes), in_specs=specs, out_specs=tuple(specs),
        compiler_params=pltpu.CompilerParams(dimension_semantics=("arbitrary",)),
    )(*[a for item in items for a in item])
    return [res[4 * k:4 * k + 4] for k in range(len(items))]


def _adam_update(w, g, m, v):
    nm = ADAM_B1 * m + (1.0 - ADAM_B1) * g
    nv = ADAM_B2 * v + (1.0 - ADAM_B2) * (g * g)
    m_hat = nm / (1.0 - ADAM_B1 ** ADAM_STEP)
    v_hat = nv / (1.0 - ADAM_B2 ** ADAM_STEP)
    return (-ADAM_LR) * (m_hat / (jnp.sqrt(v_hat) + ADAM_EPS) + ADAM_WD * w), nm, nv


def adamw_vectors(g_pack, ws, ms, vs):
    nvec = len(SMALL_VECTORS)
    n = nvec + len(SMALL_MATRICES)

    def body(*refs):
        pk = refs[0]
        w_refs, m_refs, v_refs = (refs[1 + k * n:1 + (k + 1) * n] for k in range(3))
        g_out, d_out, nm_out, nv_out = (refs[1 + (3 + k) * n:1 + (4 + k) * n] for k in range(4))
        refs[-1][...] = pk[LOSS_ROW:LOSS_ROW + 1, 0:1]
        chip = 2 * lax.axis_index("x") + lax.axis_index("y")
        for k, (name, row, width) in enumerate(SMALL_VECTORS):
            if name == "conv_w":
                g = jnp.concatenate([pk[pl.ds(row + 4 * t + chip, 1), :] for t in range(CONV_K)], axis=0)[None]
            elif width >= 128:
                g = jnp.concatenate([pk[row + r:row + r + 1, :] for r in range(width // 128)], axis=1)
            else:
                g = pk[row:row + 1, 0:width]
            g_out[k][...] = g
            d_out[k][...], nm_out[k][...], nv_out[k][...] = _adam_update(w_refs[k][...], g, m_refs[k][...], v_refs[k][...])
        for k in range(nvec, n):
            for b in range(LRU_BLOCKS):
                rows_ = pk[GATES_ROW + HEAD * b:GATES_ROW + HEAD * (b + 1), :]
                g = (pltpu.roll(rows_, HEAD, axis=1) if k > nvec else rows_)[:, 0:HEAD]
                g_out[k][0, b] = g
                d_out[k][0, b], nm_out[k][0, b], nv_out[k][0, b] = _adam_update(
                    w_refs[k][0, b], g, m_refs[k][0, b], v_refs[k][0, b])

    vm = pl.BlockSpec(memory_space=pltpu.VMEM)
    like = [jax.ShapeDtypeStruct(w.shape, F32) for w in ws]
    return pl.pallas_call(
        body, name="adamw_vectors", out_shape=(*like * 4, jax.ShapeDtypeStruct((1, 1), F32)),
        in_specs=[vm] * (1 + 3 * n), out_specs=(vm,) * (4 * n + 1),
    )(g_pack, *ws, *ms, *vs)


SMALL_VECTORS = (("norm_g", 0, 1024), ("mem_norm_g", 8, 1024), ("conv_w", 16, 512), ("conv_b", 32, 512),
                 ("b_rg", 36, 512), ("b_ig", 40, 512), ("lru_lambda", 44, 512), ("q_norm_g", 48, 64),
                 ("k_norm_g", 49, 64), ("sinks", 50, 4), ("xq_norm_g", 51, 64), ("xk_norm_g", 52, 64),
                 ("out_norm_g", 53, 1024))
LOSS_ROW = 61
SMALL_MATRICES = ("w_rg", "w_ig")
GATES_ROW = 64
SMALL_ROWS = GATES_ROW + LRU_BLOCKS * HEAD


def _rope_tables(seq):
    pos = np.arange(seq, dtype=np.float32)
    inv_freq = (np.float32(ROPE_THETA) ** (-(np.arange(0, ROPE_DIM, 2, dtype=np.float32) / np.float32(ROPE_DIM)))
                ).astype(np.float32)
    ang = (pos[:, None] * inv_freq[None, :]).astype(np.float32)
    cos, sin = np.cos(ang).astype(np.float32), np.sin(ang).astype(np.float32)
    z = lambda n: np.zeros((seq, n), np.float32)
    c64 = np.concatenate([cos, cos, np.ones((seq, HEAD - ROPE_DIM), np.float32)], axis=1)
    s1_64 = np.concatenate([-sin, z(HEAD - 8)], axis=1)
    s2_64 = np.concatenate([z(8), sin, z(HEAD - ROPE_DIM)], axis=1)
    return tuple(jnp.asarray(np.concatenate([t, t], axis=1)) for t in (c64, s1_64, s2_64))


def kernel(x, mem, norm_g, mem_norm_g, w_in, conv_w, conv_b, w_rg, b_rg, w_ig, b_ig, lru_lambda, q_norm_g, k_norm_g, sinks, w_mem_kv, xq_norm_g, xk_norm_g, out_norm_g, w_out, loss_target, m_norm_g, m_mem_norm_g, m_w_in, m_conv_w, m_conv_b, m_w_rg, m_b_rg, m_w_ig, m_b_ig, m_lru_lambda, m_q_norm_g, m_k_norm_g, m_sinks, m_w_mem_kv, m_xq_norm_g, m_xk_norm_g, m_out_norm_g, m_w_out, v_norm_g, v_mem_norm_g, v_w_in, v_conv_w, v_conv_b, v_w_rg, v_b_rg, v_w_ig, v_b_ig, v_lru_lambda, v_q_norm_g, v_k_norm_g, v_sinks, v_w_mem_kv, v_xq_norm_g, v_xk_norm_g, v_out_norm_g, v_w_out):
    seq = x.shape[1]
    xs, tgt, mems = x[0], loss_target[0], mem[0]

    win_t, wout, wkv, cw, wg, gains = gather_weights(w_in[0].T, w_out[0], w_mem_kv[0], conv_w, w_rg, w_ig,
                                                     (q_norm_g, k_norm_g, xq_norm_g, xk_norm_g))
    rc, rs1, rs2 = _rope_tables(seq)

    km, vm = mem_fwd(mems, mem_norm_g, wkv, gains)
    proj, ya, yb, yc, ycat, xn, dout, pswa, pmem, psink, gates, a_all, loss8 = layer_fwd(
        xs, tgt, rc, rs1, rs2, norm_g, win_t, cw, conv_b, wg, b_rg, b_ig, lru_lambda, gains, sinks, km, vm,
        out_norm_g, wout)
    (g_wout,) = wgrad_reduce(ycat, dout, [], None, (), "wgrad_out")
    (gx, dproj, g_wg, dkm, dvm, g_ng, g_og, g_cb, g_brg, g_big, g_lam, g_cw, g_qn, g_kn, g_xqn, g_sink) = layer_bwd(
        xs, dout, proj, ya, yb, yc, pswa, pmem, psink, gates, a_all, rc, rs1, rs2, norm_g, win_t, cw, wg, lru_lambda,
        gains, km, vm, out_norm_g, wout)
    g_wkv, small_g = mem_bwd(mems, mem_norm_g, wkv, gains, dkm, dvm, g_wg, loss8, dict(
        norm_g=g_ng, conv_w=g_cw, conv_b=g_cb, b_rg=g_brg, b_ig=g_big, lru_lambda=g_lam, q_norm_g=g_qn, k_norm_g=g_kn,
        sinks=g_sink, xq_norm_g=g_xqn, out_norm_g=g_og))
    early = [g_wout.reshape(N_CHIPS, 2, D_MODEL // 8, D_MODEL), g_wkv.reshape(N_CHIPS, 2, D_MODEL // 8, 2 * XATT_W)]
    g_win_t, r_out, r_kv, r_small = wgrad_reduce(dproj, xn, early, small_g.reshape(2, SMALL_ROWS // 2, 128),
                                                 (0, 2, 5, 7, 8), "wgrad_in")
    r_in = reduce_grads(g_win_t.reshape(N_CHIPS, 2, D_IN // 8, D_MODEL), "reduce_w_in", 6)

    r_small = r_small.reshape(SMALL_ROWS, 128)
    grads = {}
    weights = dict(norm_g=norm_g, mem_norm_g=mem_norm_g, w_in=w_in, conv_w=conv_w, conv_b=conv_b, w_rg=w_rg, b_rg=b_rg,
                   w_ig=w_ig, b_ig=b_ig, lru_lambda=lru_lambda, q_norm_g=q_norm_g, k_norm_g=k_norm_g, sinks=sinks,
                   w_mem_kv=w_mem_kv, xq_norm_g=xq_norm_g, xk_norm_g=xk_norm_g, out_norm_g=out_norm_g, w_out=w_out)
    ms = dict(norm_g=m_norm_g, mem_norm_g=m_mem_norm_g, w_in=m_w_in, conv_w=m_conv_w, conv_b=m_conv_b, w_rg=m_w_rg,
              b_rg=m_b_rg, w_ig=m_w_ig, b_ig=m_b_ig, lru_lambda=m_lru_lambda, q_norm_g=m_q_norm_g, k_norm_g=m_k_norm_g,
              sinks=m_sinks, w_mem_kv=m_w_mem_kv, xq_norm_g=m_xq_norm_g, xk_norm_g=m_xk_norm_g,
              out_norm_g=m_out_norm_g, w_out=m_w_out)
    vs = dict(norm_g=v_norm_g, mem_norm_g=v_mem_norm_g, w_in=v_w_in, conv_w=v_conv_w, conv_b=v_conv_b, w_rg=v_w_rg,
              b_rg=v_b_rg, w_ig=v_w_ig, b_ig=v_b_ig, lru_lambda=v_lru_lambda, q_norm_g=v_q_norm_g, k_norm_g=v_k_norm_g,
              sinks=v_sinks, w_mem_kv=v_w_mem_kv, xq_norm_g=v_xq_norm_g, xk_norm_g=v_xk_norm_g,
              out_norm_g=v_out_norm_g, w_out=v_w_out)

    delta, new_m, new_v = {}, {}, {}
    res_in, res_out, res_kv = adamw_matrices([
        (w_in[0].T, r_in.reshape(D_IN // 4, D_MODEL), m_w_in[0].T, v_w_in[0].T),
        (w_out[0], r_out.reshape(D_MODEL // 4, D_MODEL), m_w_out[0], v_w_out[0]),
        (w_mem_kv[0], r_kv.reshape(D_MODEL // 4, 2 * XATT_W), m_w_mem_kv[0], v_w_mem_kv[0])])
    grads["w_in"], delta["w_in"], new_m["w_in"], new_v["w_in"] = (r.T[None] for r in res_in)
    grads["w_out"], delta["w_out"], new_m["w_out"], new_v["w_out"] = (r[None] for r in res_out)
    grads["w_mem_kv"], delta["w_mem_kv"], new_m["w_mem_kv"], new_v["w_mem_kv"] = (r[None] for r in res_kv)
    small_names = [n for n, _, _ in SMALL_VECTORS] + list(SMALL_MATRICES)
    res = adamw_vectors(r_small, [weights[n] for n in small_names], [ms[n] for n in small_names],
                        [vs[n] for n in small_names])
    nall = len(small_names)
    for k, into in enumerate((grads, delta, new_m, new_v)):
        into.update(zip(small_names, res[k * nall:(k + 1) * nall]))
    loss = res[-1].reshape(())

    order = ("norm_g", "mem_norm_g", "w_in", "conv_w", "conv_b", "w_rg", "b_rg", "w_ig", "b_ig", "lru_lambda",
             "q_norm_g", "k_norm_g", "sinks", "w_mem_kv", "xq_norm_g", "xk_norm_g", "out_norm_g", "w_out")
    return (loss, gx[None], *[grads[n] for n in order], *[delta[n] for n in order], *[new_m[n] for n in order],
            *[new_v[n] for n in order])
```

```python
import jax
import jax.numpy as jnp
import numpy as np
from jax import lax
from jax.experimental import pallas as pl
from jax.experimental.pallas import tpu as pltpu

F32 = jnp.float32
_MXU = jnp.bfloat16
_WIRE = jnp.bfloat16

D_MODEL = 1024
MEM_LEN = 256
HEAD = 64
LRU_W = 512
LRU_BLOCKS = 8
CONV_K = 4
LRU_C = 8.0
SWA_W = 256
KV_W = 128
XATT_W = 256
BLOCK = 128
D_IN = 2304
ROPE_THETA = 500000.0
ROPE_DIM = 16
EPS = 1e-6
NEG_INF = -1e30
C_LRUX, C_LRUG, C_SQ, C_SK, C_SV, C_SWAG, C_XQ, C_XG = 0, 512, 1024, 1280, 1408, 1536, 1792, 2048
G_Q, G_K, G_XQ, G_XK, GAINS_W = 0, 128, 256, 512, 768

ADAM_LR, ADAM_B1, ADAM_B2, ADAM_EPS, ADAM_WD, ADAM_STEP = 0.001, 0.9, 0.999, 1e-08, 0.01, 10

N_CHIPS = 4
ROW_TILE = 256
VMEM_LIMIT = 56 * 1024 * 1024
ADAM_BLOCK_BYTES = 640 * 1024
MESH = pl.DeviceIdType.MESH


def _mm(a, b):
    return jnp.dot(a.astype(_MXU), b.astype(_MXU), preferred_element_type=F32)


def _mm_nt(a, b):
    return lax.dot_general(a.astype(_MXU), b.astype(_MXU), (((1,), (1,)), ((), ())), preferred_element_type=F32)


def _mm_tn(a, b):
    return lax.dot_general(a.astype(_MXU), b.astype(_MXU), (((0,), (0,)), ((), ())), preferred_element_type=F32)


def _group_matrix(width):
    r = lax.shift_right_logical(lax.broadcasted_iota(jnp.int32, (width, width), 0), 6)
    c = lax.shift_right_logical(lax.broadcasted_iota(jnp.int32, (width, width), 1), 6)
    return (r == c).astype(_MXU)


def _seg_mean(x, gm):
    return jnp.dot(x.astype(_MXU), gm, preferred_element_type=F32) * (1.0 / HEAD)


def _row_mean(x):
    return jnp.mean(x, axis=-1, keepdims=True)


def _col_sum(x):
    return jnp.sum(x, axis=0, keepdims=True)


def _sigmoid(x):
    return jax.nn.sigmoid(x)


def _softplus(z):
    e = jnp.exp(-jnp.abs(z))
    u = 1.0 + e
    log1p_e = jnp.where(u == 1.0, e, jnp.log(u) * (e / (u - 1.0)))
    return jnp.maximum(z, 0.0) + log1p_e


def _rope(t, c, s1, s2):
    return t * c + pltpu.roll(t, 120, 1) * s1 + pltpu.roll(t, 8, 1) * s2


def _rope_bwd(d, c, s1, s2):
    return d * c + pltpu.roll(d * s1, 8, 1) + pltpu.roll(d * s2, 120, 1)


def _fold_heads(v):
    out = v
    for k in range(1, v.shape[1] // HEAD):
        out = out + pltpu.roll(v, HEAD * k, 1)
    return out


def _lane_mask(width, lo, hi):
    lane = lax.broadcasted_iota(jnp.int32, (1, width), 1)
    return ((lane >= lo) & (lane < hi)).astype(F32)


def _swa_mask(first_block):
    qi = lax.broadcasted_iota(jnp.int32, (BLOCK, 2 * BLOCK), 0)
    kj = lax.broadcasted_iota(jnp.int32, (BLOCK, 2 * BLOCK), 1)
    rel = qi + BLOCK - kj
    ok = (rel >= 0) & (rel < BLOCK)
    return ok & (jnp.logical_not(first_block) | (kj >= BLOCK))


def _place_kv(t, scale):
    lo = t * (_lane_mask(KV_W, 0, HEAD) * scale)
    hi = t * (_lane_mask(KV_W, HEAD, KV_W) * scale)
    return [a.astype(_MXU) for a in (lo, pltpu.roll(lo, HEAD, 1), pltpu.roll(hi, HEAD, 1), hi)]


def _unplace_kv(d):
    return (_lane_mask(KV_W, 0, HEAD) * (d[0] + pltpu.roll(d[1], HEAD, 1))
            + _lane_mask(KV_W, HEAD, KV_W) * (d[3] + pltpu.roll(d[2], HEAD, 1)))


def _swa_probs(qh, ka, mask, sink):
    s = _mm_nt(qh, ka)
    s = jnp.where(mask, s, NEG_INF)
    m = jnp.maximum(jnp.max(s, axis=-1, keepdims=True), sink)
    p = jnp.exp(s - m)
    esink = jnp.exp(sink - m)
    inv = 1.0 / (jnp.sum(p, axis=-1, keepdims=True) + esink)
    return p * inv, esink * inv


def _mem_probs(s_all):
    out = []
    for j in range(4):
        s = s_all[:, MEM_LEN * j:MEM_LEN * (j + 1)]
        p = jnp.exp(s - jnp.max(s, axis=-1, keepdims=True))
        out.append(p * (1.0 / jnp.sum(p, axis=-1, keepdims=True)))
    return out


def _head_rows(t, scale):
    return jnp.concatenate([t * (_lane_mask(XATT_W, HEAD * j, HEAD * (j + 1)) * scale) for j in range(4)], axis=0)


def _lru_gates(xc, wg_ref, brg, big, lam):
    p0 = _mm(xc[:, :256], wg_ref[0])
    p1 = _mm(xc[:, 256:], wg_ref[1])
    rg = _sigmoid(jnp.concatenate([p0[:, :256], p1[:, :256]], axis=1) + brg)
    ig = _sigmoid(jnp.concatenate([p0[:, 256:], p1[:, 256:]], axis=1) + big)
    sp = _softplus(-lam)
    la = (-LRU_C) * rg * sp
    a = jnp.exp(la)
    th = jnp.tanh(la)
    one_minus_a2 = (-2.0 * th) / (1.0 - th)
    return rg, ig, sp, a, jnp.sqrt(one_minus_a2)


def _const_spec(shape, single=False):
    zeros = (0,) * len(shape)
    if single:
        return pl.BlockSpec(shape, lambda i: zeros, pipeline_mode=pl.Buffered(1))
    return pl.BlockSpec(shape, lambda i: zeros)


def _chip_of(x, y):
    return 2 * x + y


def _partners(x, y, c):
    north = c == 1
    near = (jnp.where(north, 1 - x, x), jnp.where(north, y, 1 - y))
    far = (jnp.where(north, x, 1 - x), jnp.where(north, 1 - y, y))
    return near, far, (1 - x, 1 - y)


def gather_weights(win_t, wout, wkv, conv_w, w_rg, w_ig, head_gains):
    arrs = (win_t, wout, wkv)
    n = len(arrs)
    pieces = [(a, k * (arr.shape[0] // (2 * cut)), arr.shape[0] // (2 * cut))
              for a, (arr, cut) in enumerate(zip(arrs, (2, 1, 1))) for k in range(cut)]
    npc = len(pieces)

    def body(a0, a1, a2, cw_in, wrg_ref, wig_ref, q_ref, k_ref, xq_ref, xk_ref, o0, o1, o2, cw_out, wg_ref, gn_ref,
             s0, s1, s2, cw, ocw, send, recv, lsem):
        ins, outs = (s0, s1, s2), (o0, o1, o2)
        for src, dst in zip((a0, a1, a2), ins):
            dst[...] = src[...].astype(dst.dtype)
        cw[...] = jnp.zeros(cw.shape, F32)
        cw[0:CONV_K, :] = cw_in[0]
        x, y, c = lax.axis_index("x"), lax.axis_index("y"), lax.axis_index("c")
        sibling = (x, y, 1 - c)
        near, far, diag = _partners(x, y, c)
        chips = [near, far, diag]
        me = _chip_of(x, y)

        def landed(p, chip, half):
            a, off, rows_ = pieces[p]
            r = ins[a].shape[0]
            return outs[a].at[pl.ds(pl.multiple_of(chip * r + half * (r // 2) + off, 16), rows_)]

        def mine(p):
            a, off, rows_ = pieces[p]
            return ins[a].at[pl.ds(pl.multiple_of(c * (ins[a].shape[0] // 2) + off, 16), rows_)]

        def copy(k, src, dst, to):
            return pltpu.make_async_remote_copy(src_ref=src, dst_ref=dst, send_sem=send.at[k], recv_sem=recv.at[k],
                                                device_id=to, device_id_type=MESH)

        def cw_rows(chip):
            return ocw.at[pl.ds(pl.multiple_of(chip * 8, 8), 8)]

        locals_ = []
        for a in range(n):
            r = ins[a].shape[0]
            locals_.append(pltpu.make_async_copy(ins[a], outs[a].at[pl.ds(pl.multiple_of(me * r, 16), r)], lsem.at[a]))
        locals_.append(pltpu.make_async_copy(cw, cw_rows(me), lsem.at[n]))
        for cp in locals_:
            cp.start()

        sent = []
        for p in range(npc):
            for j in range(2):
                sent.append(copy(p * 6 + j, mine(p), landed(p, me, c), (*chips[j], c)))
        for j, chip in enumerate(chips):
            sent.append(copy(npc * 6 + j, cw, cw_rows(me), (*chip, c)))
        for cp in sent:
            cp.start()

        gn_ref[...] = jnp.concatenate([q_ref[...]] * 2 + [k_ref[...]] * 2 + [xq_ref[...]] * 4 + [xk_ref[...]] * 4,
                                      axis=1)
        zeros = lambda lanes: [jnp.zeros((HEAD, lanes), F32)] if lanes else []
        for h in range(2):
            for b in range(4):
                row = []
                for w_ref in (wrg_ref, wig_ref):
                    row += zeros(HEAD * b) + [w_ref[0, 4 * h + b]] + zeros(HEAD * (3 - b))
                wg_ref[h, HEAD * b:HEAD * (b + 1), :] = jnp.concatenate(row, axis=1).astype(wg_ref.dtype)

        for j in range(3):
            for p in range(npc):
                got = landed(p, _chip_of(*chips[j]), c)
                copy(p * 6 + j, got, got, sibling).wait_recv()
                if j == 0:
                    sent.append(copy(p * 6 + 2, got, got, (*far, c)))
                    sent[-1].start()
                sent.append(copy(p * 6 + 3 + j, got, got, sibling))
                sent[-1].start()
        for p in range(npc):
            for j in range(3):
                got = landed(p, _chip_of(*chips[(1, 0, 2)[j]]), 1 - c)
                copy(p * 6 + 3 + j, got, got, sibling).wait_recv()
        for j, chip in enumerate(chips):
            got = cw_rows(_chip_of(*chip))
            copy(npc * 6 + j, got, got, (*chip, c)).wait_recv()
        for cp in sent:
            cp.wait_send()
        for cp in locals_:
            cp.wait()
        for chip in range(N_CHIPS):
            cw_out[:, 128 * chip:128 * (chip + 1)] = ocw[8 * chip:8 * chip + CONV_K, :]

    vm = pl.BlockSpec(memory_space=pltpu.VMEM)
    out_shape = tuple(jax.ShapeDtypeStruct((N_CHIPS * a.shape[0],) + a.shape[1:], _MXU) for a in arrs) + (
        jax.ShapeDtypeStruct((CONV_K, LRU_W), F32), jax.ShapeDtypeStruct((2, 256, 512), _MXU),
        jax.ShapeDtypeStruct((1, GAINS_W), F32))
    n_rdma = npc * 6 + 3
    return pl.pallas_call(
        body, name="gather_weights", out_shape=out_shape,
        in_specs=[vm] * 10, out_specs=(pl.BlockSpec(memory_space=pl.ANY),) * n + (vm, vm, vm),
        scratch_shapes=[pltpu.VMEM(a.shape, _MXU) for a in arrs] + [
            pltpu.VMEM((8, 128), F32), pltpu.VMEM((N_CHIPS * 8, 128), F32),
            pltpu.SemaphoreType.DMA((n_rdma,)), pltpu.SemaphoreType.DMA((n_rdma,)), pltpu.SemaphoreType.DMA((n + 1,))],
        compiler_params=pltpu.CompilerParams(vmem_limit_bytes=VMEM_LIMIT),
    )(win_t, wout, wkv, conv_w, w_rg, w_ig, *head_gains)


def mem_fwd(mem, mem_g, wkv, gains):
    def body(mem_ref, g_ref, w_ref, gn_ref, km_ref, vm_ref):
        mem_v = mem_ref[...]
        mn = mem_v * lax.rsqrt(_row_mean(mem_v * mem_v) + EPS) * g_ref[...]
        mkv = _mm(mn, w_ref[...])
        kpre = mkv[:, :XATT_W]
        gm = _group_matrix(XATT_W)
        km = kpre * lax.rsqrt(_seg_mean(kpre * kpre, gm) + EPS) * gn_ref[:, G_XK:GAINS_W]
        km_ref[...] = _head_rows(km, 0.125).astype(km_ref.dtype)
        vm_ref[...] = _head_rows(mkv[:, XATT_W:], 1.0).astype(vm_ref.dtype)

    vm = pl.BlockSpec(memory_space=pltpu.VMEM)
    rows_shape = jax.ShapeDtypeStruct((4 * MEM_LEN, XATT_W), _MXU)
    return pl.pallas_call(
        body, name="mem_fwd", out_shape=(rows_shape, rows_shape), in_specs=[vm] * 4, out_specs=(vm, vm),
    )(mem, mem_g, wkv, gains)


def mem_bwd(mem, mem_g, wkv, gains, dkm, dvm, g_gates, loss8, vectors):
    names = tuple(vectors)
    first_row = {name: (row, width) for name, row, width in SMALL_VECTORS}

    def body(mem_ref, g_ref, w_ref, gn_ref, dkm_ref, dvm_ref, gg_ref, loss_ref, *rest):
        vec_refs, (gw_ref, pk_ref) = rest[:len(names)], rest[len(names):]
        pk_ref[...] = jnp.zeros(pk_ref.shape, F32)

        def put(name, src):
            row, width = first_row[name]
            per_row = 1 if width < 128 else src.shape[1] // 128
            for t in range(src.shape[0]):
                for r in range(per_row):
                    at = row + per_row * t + r
                    pk_ref[at:at + 1, :] = src[t:t + 1, 128 * r:128 * (r + 1)]

        for name, ref in zip(names, vec_refs):
            put(name, ref)
        pk_ref[LOSS_ROW:LOSS_ROW + 1, :] = loss_ref[0:1, :]
        upper = lax.broadcasted_iota(jnp.int32, (HEAD, 128), 1) >= HEAD
        for h in range(2):
            for b in range(4):
                rg = gg_ref[h, HEAD * b:HEAD * (b + 1), 128 * (b // 2):128 * (b // 2 + 1)]
                ig = gg_ref[h, HEAD * b:HEAD * (b + 1), 256 + 128 * (b // 2):256 + 128 * (b // 2 + 1)]
                if b % 2:
                    rg = pltpu.roll(rg, HEAD, axis=1)
                else:
                    ig = pltpu.roll(ig, HEAD, axis=1)
                at = GATES_ROW + HEAD * (4 * h + b)
                pk_ref[at:at + HEAD, :] = jnp.where(upper, ig, rg)

        mem_v = mem_ref[...]
        mh = mem_v * lax.rsqrt(_row_mean(mem_v * mem_v) + EPS)
        mn = mh * g_ref[...]
        mkv = _mm(mn, w_ref[...])
        kpre = mkv[:, :XATT_W]
        gm = _group_matrix(XATT_W)
        rk = lax.rsqrt(_seg_mean(kpre * kpre, gm) + EPS)
        kn = kpre * rk
        dk = jnp.zeros((MEM_LEN, XATT_W), F32)
        dv = jnp.zeros((MEM_LEN, XATT_W), F32)
        for j in range(4):
            mj = _lane_mask(XATT_W, HEAD * j, HEAD * (j + 1))
            dk = dk + dkm_ref[:, MEM_LEN * j:MEM_LEN * (j + 1)].T * (mj * 0.125)
            dv = dv + dvm_ref[:, MEM_LEN * j:MEM_LEN * (j + 1)].T * mj
        put("xk_norm_g", _fold_heads(_col_sum(dk * kn)))
        dkn = dk * gn_ref[:, G_XK:GAINS_W]
        dkpre = rk * (dkn - kn * _seg_mean(dkn * kn, gm))
        dmkv = jnp.concatenate([dkpre, dv], axis=1)
        gw_ref[...] = _mm_tn(mn, dmkv)
        dmn = _mm_nt(dmkv, w_ref[...])
        put("mem_norm_g", _col_sum(dmn * mh))

    vm = pl.BlockSpec(memory_space=pltpu.VMEM)
    return pl.pallas_call(
        body, name="mem_bwd",
        out_shape=(jax.ShapeDtypeStruct((D_MODEL, 2 * XATT_W), F32), jax.ShapeDtypeStruct((SMALL_ROWS, 128), F32)),
        in_specs=[vm] * (8 + len(names)), out_specs=(vm, vm),
    )(mem, mem_g, wkv, gains, dkm, dvm, g_gates, loss8, *vectors.values())


def layer_fwd(x, tgt, rc, rs1, rs2, ng, win_t, cw, cb, wg, brg, big, lam, gains, sinks, km, vm, og, wout):
    seq = x.shape[0]
    tm = min(ROW_TILE, seq)
    nt = seq // tm
    nb = tm // BLOCK

    def body(x_ref, t_ref, c_ref, s1_ref, s2_ref, ng_ref, win_ref, cw_ref, cb_ref, wg_ref, brg_ref, big_ref, lam_ref,
             gn_ref, sink_ref, km_ref, vm_ref, og_ref, wout_ref,
             proj_ref, ya_ref, yb_ref, yc_ref, ycat_ref, xn_ref, dout_ref, pswa_ref, pmem_ref, psink_ref, gates_ref,
             a_ref, loss_ref,
             ext_ref, b_scr, hc_ref, kp_ref, vp_ref, lacc_ref):
        i = pl.program_id(0)

        @pl.when(i == 0)
        def _():
            ext_ref[0:8, :] = jnp.zeros((8, LRU_W), F32)
            hc_ref[...] = jnp.zeros_like(hc_ref)
            kp_ref[...] = jnp.zeros_like(kp_ref)
            vp_ref[...] = jnp.zeros_like(vp_ref)
            lacc_ref[...] = jnp.zeros_like(lacc_ref)

        xv = x_ref[...]
        xn = (xv * lax.rsqrt(_row_mean(xv * xv) + EPS) * ng_ref[...]).astype(_MXU)
        xn_ref[...] = xn.astype(xn_ref.dtype)
        proj_ref[...] = _mm_nt(xn, win_ref[...])

        u = proj_ref[:, C_LRUX:C_LRUX + LRU_W]
        ext_ref[8:8 + tm, :] = u
        xc = cb_ref[...]
        for k in range(CONV_K):
            xc = xc + cw_ref[k:k + 1, :] * ext_ref[pl.ds(5 + k, tm), :]
        ext_ref[0:8, :] = u[tm - 8:tm, :]
        rg, ig, sp, a, sq = _lru_gates(xc, wg_ref, brg_ref[...], big_ref[...], lam_ref[...])
        for k, t in enumerate((xc, rg, ig, sq)):
            gates_ref[:, LRU_W * k:LRU_W * (k + 1)] = t.astype(gates_ref.dtype)
        a_ref[...] = a
        b_scr[...] = sq * (ig * xc)
        row8 = lax.broadcasted_iota(jnp.int32, (8, LRU_W), 0)

        def scan_step(g, carry):
            r0 = pl.multiple_of(g * 8, 8)
            av = a_ref[pl.ds(r0, 8), :]
            bv = b_scr[pl.ds(r0, 8), :]
            for d in (1, 2, 4):
                a_sh = jnp.where(row8 >= d, pltpu.roll(av, d, 0), 1.0)
                b_sh = jnp.where(row8 >= d, pltpu.roll(bv, d, 0), 0.0)
                bv = bv + av * b_sh
                av = av * a_sh
            hv = bv + av * carry
            ya_ref[pl.ds(r0, 8), :] = hv
            return hv[7:8, :]

        hc_ref[0:1, :] = lax.fori_loop(0, tm // 8, scan_step, hc_ref[0:1, :], unroll=True)

        gm128 = _group_matrix(KV_W)
        cv, s1v, s2v = c_ref[...], s1_ref[...], s2_ref[...]

        def head_norm_rope(t, g):
            n = t * lax.rsqrt(_seg_mean(t * t, gm128) + EPS)
            return _rope(n * g, cv, s1v, s2v)

        qs_ = (head_norm_rope(proj_ref[:, C_SQ:C_SQ + 128], gn_ref[:, G_Q:G_K]).astype(_MXU),
               head_norm_rope(proj_ref[:, C_SQ + 128:C_SQ + 256], gn_ref[:, G_Q:G_K]).astype(_MXU))
        kr = head_norm_rope(proj_ref[:, C_SK:C_SK + KV_W], gn_ref[:, G_K:G_XQ])
        sv = proj_ref[:, C_SV:C_SV + KV_W]
        ka = _place_kv(jnp.concatenate([kp_ref[...], kr], axis=0), 0.125)
        va = _place_kv(jnp.concatenate([vp_ref[...], sv], axis=0), 1.0)
        kp_ref[...] = kr[tm - BLOCK:tm, :]
        vp_ref[...] = sv[tm - BLOCK:tm, :]
        lane128 = lax.broadcasted_iota(jnp.int32, (1, 128), 1)
        for b in range(nb):
            mask = _swa_mask((i == 0) & (b == 0)) if b == 0 else _swa_mask(False)
            band = slice(BLOCK * b, BLOCK * b + 2 * BLOCK)
            blk = slice(BLOCK * b, BLOCK * (b + 1))
            psink = jnp.zeros((BLOCK, 128), F32)
            for j in range(4):
                p, pk = _swa_probs(qs_[j // 2][blk], ka[j][band], mask, sink_ref[0, j])
                pswa_ref[blk, 2 * BLOCK * j:2 * BLOCK * (j + 1)] = p.astype(pswa_ref.dtype)
                psink = jnp.where(lane128 == j, pk, psink)
            psink_ref[blk, :] = psink
            for h in range(2):
                yb_ref[blk, KV_W * h:KV_W * (h + 1)] = _mm(
                    pswa_ref[blk, 4 * BLOCK * h:4 * BLOCK * (h + 1)],
                    jnp.concatenate([va[2 * h][band], va[2 * h + 1][band]], axis=0))

        gm256 = _group_matrix(XATT_W)
        xq = proj_ref[:, C_XQ:C_XQ + XATT_W]
        qx = xq * lax.rsqrt(_seg_mean(xq * xq, gm256) + EPS) * gn_ref[:, G_XQ:G_XK]
        pm = _mem_probs(_mm_nt(qx, km_ref[...]))
        for j in range(4):
            pmem_ref[:, MEM_LEN * j:MEM_LEN * (j + 1)] = pm[j].astype(pmem_ref.dtype)
        yc = _mm(pmem_ref[...], vm_ref[...])
        yc_ref[...] = yc

        def gated(y, g, gate):
            return y * lax.rsqrt(_row_mean(y * y) + EPS) * g * (gate * _sigmoid(gate))

        ogv = og_ref[...]
        za = gated(ya_ref[...], ogv[:, :512], proj_ref[:, C_LRUG:C_LRUG + LRU_W])
        zb = gated(yb_ref[...], ogv[:, 512:768], proj_ref[:, C_SWAG:C_SWAG + SWA_W])
        zc = gated(yc, ogv[:, 768:], proj_ref[:, C_XG:C_XG + XATT_W])
        ycat_ref[:, 0:512] = za.astype(ycat_ref.dtype)
        ycat_ref[:, 512:768] = zb.astype(ycat_ref.dtype)
        ycat_ref[:, 768:1024] = zc.astype(ycat_ref.dtype)
        out = xv + _mm(ycat_ref[...], wout_ref[...])
        err = out - t_ref[...]
        dout_ref[...] = (err * (1.0 / D_MODEL)).astype(dout_ref.dtype)
        lacc_ref[...] = lacc_ref[...] + (0.5 / D_MODEL) * jnp.sum(err * err)

        @pl.when(i == nt - 1)
        def _():
            loss_ref[...] = lacc_ref[...]

    def rows(ncol):
        return pl.BlockSpec((tm, ncol), lambda i: (i, 0))

    in_specs = [rows(D_MODEL), rows(D_MODEL), rows(128), rows(128), rows(128),
                _const_spec((1, D_MODEL)), _const_spec((D_IN, D_MODEL), True), _const_spec((CONV_K, LRU_W)),
                _const_spec((1, LRU_W)), _const_spec((2, 256, 512), True), _const_spec((1, LRU_W)),
                _const_spec((1, LRU_W)), _const_spec((1, LRU_W)), _const_spec((1, GAINS_W)), pl.BlockSpec(memory_space=pltpu.SMEM),
                _const_spec((4 * MEM_LEN, XATT_W), True), _const_spec((4 * MEM_LEN, XATT_W), True),
                _const_spec((1, D_MODEL)), _const_spec((D_MODEL, D_MODEL), True)]
    out_shape = (jax.ShapeDtypeStruct((seq, D_IN), F32), jax.ShapeDtypeStruct((seq, LRU_W), F32),
                 jax.ShapeDtypeStruct((seq, SWA_W), F32), jax.ShapeDtypeStruct((seq, XATT_W), F32),
                 jax.ShapeDtypeStruct((seq, D_MODEL), _MXU), jax.ShapeDtypeStruct((seq, D_MODEL), _MXU),
                 jax.ShapeDtypeStruct((seq, D_MODEL), _MXU), jax.ShapeDtypeStruct((seq, 4 * 2 * BLOCK), _MXU),
                 jax.ShapeDtypeStruct((seq, 4 * MEM_LEN), _MXU), jax.ShapeDtypeStruct((seq, 128), F32),
                 jax.ShapeDtypeStruct((seq, 4 * LRU_W), _MXU), jax.ShapeDtypeStruct((seq, LRU_W), F32),
                 jax.ShapeDtypeStruct((8, 128), F32))
    out_specs = (rows(D_IN), rows(LRU_W), rows(SWA_W), rows(XATT_W), rows(D_MODEL), rows(D_MODEL), rows(D_MODEL),
                 rows(4 * 2 * BLOCK), rows(4 * MEM_LEN), rows(128), rows(4 * LRU_W), rows(LRU_W),
                 _const_spec((8, 128)))
    scratch = [pltpu.VMEM((tm + 8, LRU_W), F32), pltpu.VMEM((tm, LRU_W), F32),
               pltpu.VMEM((8, LRU_W), F32), pltpu.VMEM((BLOCK, KV_W), F32), pltpu.VMEM((BLOCK, KV_W), F32),
               pltpu.VMEM((8, 128), F32)]
    return pl.pallas_call(
        body, name="layer_fwd", grid=(nt,), out_shape=out_shape, in_specs=in_specs, out_specs=out_specs,
        scratch_shapes=scratch,
        compiler_params=pltpu.CompilerParams(dimension_semantics=("arbitrary",), vmem_limit_bytes=VMEM_LIMIT),
    )(x, tgt, rc, rs1, rs2, ng, win_t, cw, cb, wg, brg, big, lam, gains, sinks, km, vm, og, wout)


def weight_grads(ycat, dout, dproj, xn, g_kv, g_small, early_at, late_at):
    seq, ncol = xn.shape
    blk = 256
    n_out, n_in = ycat.shape[1] // blk, dproj.shape[1] // blk
    assert n_out == N_CHIPS and late_at[0] >= n_out
    g_out = jax.ShapeDtypeStruct((N_CHIPS, 2, blk // 2, dout.shape[1]), F32)
    shape_e, scratch_e = _hosted_reduce_shapes([g_kv], g_small)
    shape_l, scratch_l = _hosted_reduce_shapes([g_out], None)

    def body(l1_ref, r1_ref, l2_ref, r2_ref, kv_ref, sm_ref, o_ref, sum_out, sum_kv, sum_sm, gout_scr, *scratch):
        j = pl.program_id(0)
        _hosted_reduce(j, early_at, (kv_ref, sm_ref), (sum_kv, sum_sm), scratch[:len(scratch_e)], True)
        _hosted_reduce(j, late_at, (gout_scr,), (sum_out,), scratch[len(scratch_e):], False)

        @pl.when(j < n_out)
        def _():
            gout_scr[j] = _mm_tn(l1_ref[...], r1_ref[...]).reshape(g_out.shape[1:])

        @pl.when(j >= n_out)
        def _():
            o_ref[...] = _mm_tn(l2_ref[...], r2_ref[...])

    vm = pl.BlockSpec(memory_space=pltpu.VMEM)
    hbm = pl.BlockSpec(memory_space=pl.ANY)
    return pl.pallas_call(
        body, name="weight_grads", grid=(n_out + n_in,),
        out_shape=(jax.ShapeDtypeStruct((dproj.shape[1], ncol), F32), *shape_l, *shape_e),
        in_specs=[pl.BlockSpec((seq, blk), lambda j: (0, jnp.minimum(j, n_out - 1))), _const_spec(dout.shape, True),
                  pl.BlockSpec((seq, blk), lambda j: (0, jnp.maximum(j - n_out, 0))), _const_spec(xn.shape, True),
                  hbm, vm],
        out_specs=(pl.BlockSpec((blk, ncol), lambda j: (jnp.maximum(j - n_out, 0), 0)), hbm, hbm, hbm),
        scratch_shapes=[pltpu.VMEM(g_out.shape, F32)] + scratch_e + scratch_l,
        compiler_params=pltpu.CompilerParams(dimension_semantics=("arbitrary",), vmem_limit_bytes=VMEM_LIMIT),
    )(ycat, dout, dproj, xn, g_kv, g_small)


def layer_bwd(x, dout, proj, ya, yb, yc, pswa, pmem, psink, gates, a_all, rc, rs1, rs2, ng, win_t, cw, wg, lam, gains,
              km, vm, og, wout):
    seq = x.shape[0]
    tm = min(ROW_TILE, seq)
    nt = seq // tm
    nb = tm // BLOCK

    def body(x_ref, dout_ref, proj_ref, ya_ref, yb_ref, yc_ref, pswa_ref, pmem_ref, psink_ref, gates_ref, a_ref,
             c_ref, s1_ref, s2_ref,
             yah_ref, kvh_ref, ch_ref, s1h_ref, s2h_ref,
             ng_ref, win_ref, cw_ref, wg_ref, lam_ref, gn_ref, km_ref, vm_ref, og_ref, wout_ref,
             gx_ref, dproj_ref, gwg_ref, dkm_ref, dvm_ref, gng_ref, gog_ref, gcb_ref, gbrg_ref, gbig_ref, glam_ref,
             gcw_ref, gqn_ref, gkn_ref, gxqn_ref, gsink_ref,
             hext_ref, aext_ref, an_scr, dh_scr, g_scr, dxc_ext, gcar_ref, dkcar_ref, dvcar_ref):
        i = pl.program_id(0)
        tile = nt - 1 - i
        first_tile = tile == 0

        @pl.when(i == 0)
        def _():
            for r in (gwg_ref, dkm_ref, dvm_ref, gng_ref, gog_ref, gcb_ref, gbrg_ref, gbig_ref, glam_ref, gcw_ref,
                      gqn_ref, gkn_ref, gxqn_ref, gsink_ref, gcar_ref, dkcar_ref, dvcar_ref):
                r[...] = jnp.zeros_like(r)
            dxc_ext[tm:tm + 8, :] = jnp.zeros((8, LRU_W), F32)
            aext_ref[tm:tm + 8, :] = jnp.zeros((8, LRU_W), F32)

        xv = x_ref[...]
        dov = dout_ref[...]
        dz = _mm_nt(dov, wout_ref[...])
        ogv = og_ref[...]

        def group_bwd(y, gate, g, dzg):
            r = lax.rsqrt(_row_mean(y * y) + EPS)
            n = y * r
            sg = _sigmoid(gate)
            dgate = dzg * (n * g) * (sg * (1.0 + gate * (1.0 - sg)))
            dng = dzg * (gate * sg)
            dn = dng * g
            return r * (dn - n * _row_mean(dn * n)), dgate, _col_sum(dng * n)

        dya, dga, goa = group_bwd(ya_ref[...], proj_ref[:, C_LRUG:C_LRUG + LRU_W], ogv[:, :512], dz[:, :512])
        dyb, dgb, gob = group_bwd(yb_ref[...], proj_ref[:, C_SWAG:C_SWAG + SWA_W], ogv[:, 512:768], dz[:, 512:768])
        dyc, dgc, goc = group_bwd(yc_ref[...], proj_ref[:, C_XG:C_XG + XATT_W], ogv[:, 768:], dz[:, 768:])
        gog_ref[...] += jnp.concatenate([goa, gob, goc], axis=1)
        dproj_ref[:, C_LRUG:C_LRUG + LRU_W] = dga.astype(dproj_ref.dtype)
        dproj_ref[:, C_SWAG:C_SWAG + SWA_W] = dgb.astype(dproj_ref.dtype)
        dproj_ref[:, C_XG:C_XG + XATT_W] = dgc.astype(dproj_ref.dtype)

        gm256 = _group_matrix(XATT_W)
        xq = proj_ref[:, C_XQ:C_XQ + XATT_W]
        rq = lax.rsqrt(_seg_mean(xq * xq, gm256) + EPS)
        qn = xq * rq
        qx = qn * gn_ref[:, G_XQ:G_XK]
        qxb = qx.astype(_MXU)
        dycb = dyc.astype(_MXU)
        dp_all = _mm_nt(dycb, vm_ref[...])
        dsm = []
        for j in range(4):
            pj = pmem_ref[:, MEM_LEN * j:MEM_LEN * (j + 1)].astype(F32)
            dp = dp_all[:, MEM_LEN * j:MEM_LEN * (j + 1)]
            dsm.append((pj * (dp - jnp.sum(pj * dp, axis=-1, keepdims=True))).astype(_MXU))
        ds_all = jnp.concatenate(dsm, axis=1)
        dvm_ref[...] += _mm_tn(dycb, pmem_ref[...])
        dkm_ref[...] += _mm_tn(qxb, ds_all)
        dqx = _mm(ds_all, km_ref[...])
        gxqn_ref[...] += _col_sum(dqx * qn)
        dqn = dqx * gn_ref[:, G_XQ:G_XK]
        dproj_ref[:, C_XQ:C_XQ + XATT_W] = (rq * (dqn - qn * _seg_mean(dqn * qn, gm256))).astype(dproj_ref.dtype)

        gm128 = _group_matrix(KV_W)
        cv, s1v, s2v = c_ref[...], s1_ref[...], s2_ref[...]

        def head_norm(t):
            r = lax.rsqrt(_seg_mean(t * t, gm128) + EPS)
            return t * r, r

        qn_, qr_ = zip(head_norm(proj_ref[:, C_SQ:C_SQ + 128]), head_norm(proj_ref[:, C_SQ + 128:C_SQ + 256]))
        qrope = [_rope(qn_[h] * gn_ref[:, G_Q:G_K], cv, s1v, s2v).astype(_MXU) for h in range(2)]
        kn, krr = head_norm(proj_ref[:, C_SK:C_SK + KV_W])
        kr = _rope(kn * gn_ref[:, G_K:G_XQ], cv, s1v, s2v)
        khn, _ = head_norm(kvh_ref[:, 0:KV_W])
        khr = _rope(khn * gn_ref[:, G_K:G_XQ], ch_ref[...], s1h_ref[...], s2h_ref[...])
        ka = _place_kv(jnp.concatenate([khr, kr], axis=0), 0.125)
        va = _place_kv(jnp.concatenate([kvh_ref[:, KV_W:2 * KV_W], proj_ref[:, C_SV:C_SV + KV_W]], axis=0), 1.0)
        lane128 = lax.broadcasted_iota(jnp.int32, (1, 128), 1)
        gsink = jnp.zeros((1, 128), F32)
        dk_band, dv_band, dq_blk = [], [], []
        for b in range(nb):
            band = slice(BLOCK * b, BLOCK * b + 2 * BLOCK)
            blk = slice(BLOCK * b, BLOCK * (b + 1))
            dka, dva, dsb = [], [], []
            deltas = jnp.zeros((BLOCK, 128), F32)
            for j in range(4):
                qh = qrope[j // 2][blk]
                doh = dyb[blk, KV_W * (j // 2):KV_W * (j // 2 + 1)].astype(_MXU)
                pb = pswa_ref[blk, 2 * BLOCK * j:2 * BLOCK * (j + 1)]
                p = pb.astype(F32)
                dp = _mm_nt(doh, va[j][band])
                delta = jnp.sum(p * dp, axis=-1, keepdims=True)
                ds = (p * (dp - delta)).astype(_MXU)
                deltas = jnp.where(lane128 == j, delta, deltas)
                dva.append(_mm_tn(pb, doh))
                dka.append(_mm_tn(ds, qh))
                dsb.append(ds)
            gsink = gsink - _col_sum(psink_ref[blk, :] * deltas)
            dk_band.append(_unplace_kv(dka) * 0.125)
            dv_band.append(_unplace_kv(dva))
            dq_blk.append([_mm(jnp.concatenate(dsb[2 * h:2 * h + 2], axis=1),
                               jnp.concatenate([ka[2 * h][band], ka[2 * h + 1][band]], axis=0)) for h in range(2)])
        gsink_ref[...] += gsink
        dk_rows = [dk_band[b][BLOCK:] + (dk_band[b + 1][:BLOCK] if b + 1 < nb else dkcar_ref[...]) for b in range(nb)]
        dv_rows = [dv_band[b][BLOCK:] + (dv_band[b + 1][:BLOCK] if b + 1 < nb else dvcar_ref[...]) for b in range(nb)]
        dkcar_ref[...] = dk_band[0][:BLOCK]
        dvcar_ref[...] = dv_band[0][:BLOCK]
        dkg = _rope_bwd(jnp.concatenate(dk_rows, axis=0), cv, s1v, s2v)
        gkn = _col_sum(dkg * kn)
        dkn = dkg * gn_ref[:, G_K:G_XQ]
        dproj_ref[:, C_SK:C_SK + KV_W] = (krr * (dkn - kn * _seg_mean(dkn * kn, gm128))).astype(dproj_ref.dtype)
        dproj_ref[:, C_SV:C_SV + KV_W] = jnp.concatenate(dv_rows, axis=0).astype(dproj_ref.dtype)
        gqn = jnp.zeros((1, 128), F32)
        for h in range(2):
            dqg = _rope_bwd(jnp.concatenate([dq_blk[b][h] for b in range(nb)], axis=0), cv, s1v, s2v)
            gqn = gqn + _col_sum(dqg * qn_[h])
            dqn_ = dqg * gn_ref[:, G_Q:G_K]
            dproj_ref[:, C_SQ + 128 * h:C_SQ + 128 * (h + 1)] = (
                qr_[h] * (dqn_ - qn_[h] * _seg_mean(dqn_ * qn_[h], gm128))).astype(dproj_ref.dtype)
        gqn_ref[...] += gqn
        gkn_ref[...] += gkn

        u = proj_ref[:, C_LRUX:C_LRUX + LRU_W]
        xc, rg, ig, sq = (gates_ref[:, LRU_W * k:LRU_W * (k + 1)].astype(F32) for k in range(4))
        a = a_ref[...]
        sp = _softplus(-lam_ref[...])
        hext_ref[0:8, :] = jnp.where(first_tile, 0.0, yah_ref[...])
        hext_ref[8:8 + tm, :] = ya_ref[...]
        hprev = hext_ref[pl.ds(7, tm), :]
        aext_ref[0:tm, :] = a
        an_scr[...] = aext_ref[pl.ds(1, tm), :]
        dh_scr[...] = dya
        dh_scr[tm - 1:tm, :] = dh_scr[tm - 1:tm, :] + gcar_ref[0:1, :]
        row8 = lax.broadcasted_iota(jnp.int32, (8, LRU_W), 0)

        def scan_step(gi, carry):
            r0 = pl.multiple_of((tm // 8 - 1 - gi) * 8, 8)
            av = an_scr[pl.ds(r0, 8), :]
            bv = dh_scr[pl.ds(r0, 8), :]
            for d in (1, 2, 4):
                a_sh = jnp.where(row8 < 8 - d, pltpu.roll(av, 8 - d, 0), 1.0)
                b_sh = jnp.where(row8 < 8 - d, pltpu.roll(bv, 8 - d, 0), 0.0)
                bv = bv + av * b_sh
                av = av * a_sh
            gv = bv + av * carry
            g_scr[pl.ds(r0, 8), :] = gv
            return gv[0:1, :]

        g0 = lax.fori_loop(0, tm // 8, scan_step, jnp.zeros((1, LRU_W), F32), unroll=True)
        gcar_ref[0:1, :] = a[0:1, :] * g0
        gv = g_scr[...]
        da = gv * hprev
        dig = gv * sq * xc
        dxc = gv * sq * ig
        dla = da * a - gv * (ig * xc) * ((a * a) / sq)
        drg = dla * ((-LRU_C) * sp)
        glam_ref[...] += _col_sum(dla * rg)
        dpr = drg * rg * (1.0 - rg)
        dpi = dig * ig * (1.0 - ig)
        gbrg_ref[...] += _col_sum(dpr)
        gbig_ref[...] += _col_sum(dpi)
        dpre0 = jnp.concatenate([dpr[:, :256], dpi[:, :256]], axis=1).astype(_MXU)
        dpre1 = jnp.concatenate([dpr[:, 256:], dpi[:, 256:]], axis=1).astype(_MXU)
        gwg_ref[0] += _mm_tn(xc[:, :256], dpre0)
        gwg_ref[1] += _mm_tn(xc[:, 256:], dpre1)
        dxc = dxc + jnp.concatenate([_mm_nt(dpre0, wg_ref[0]), _mm_nt(dpre1, wg_ref[1])], axis=1)
        gcb_ref[...] += _col_sum(dxc)
        dxc_ext[0:tm, :] = dxc
        du = jnp.zeros((tm, LRU_W), F32)
        for k in range(CONV_K):
            later = dxc_ext[pl.ds(3 - k, tm), :]
            gcw_ref[k:k + 1, :] += _col_sum(later * u)
            du = du + cw_ref[k:k + 1, :] * later
        dxc_ext[tm:tm + 8, :] = dxc[0:8, :]
        dproj_ref[:, C_LRUX:C_LRUX + LRU_W] = du.astype(dproj_ref.dtype)

        dxn = _mm(dproj_ref[...], win_ref[...])
        rx = lax.rsqrt(_row_mean(xv * xv) + EPS)
        xh = xv * rx
        gng_ref[...] += _col_sum(dxn * xh)
        dxh = dxn * ng_ref[...]
        gx_ref[...] = dov.astype(F32) + rx * (dxh - xh * _row_mean(dxh * xh))

        @pl.when(i == nt - 1)
        def _():
            glam_ref[...] = glam_ref[...] * (LRU_C * _sigmoid(-lam_ref[...]))
            for r in (gqn_ref, gkn_ref, gxqn_ref):
                r[...] = _fold_heads(r[...])

    def rows(ncol, arr_cols_block=0):
        return pl.BlockSpec((tm, ncol), lambda i: (nt - 1 - i, arr_cols_block))

    def halo(nrow, ncol, colblk=0):
        per = tm // nrow
        return pl.BlockSpec((nrow, ncol), lambda i: (jnp.maximum((nt - 1 - i) * per - 1, 0), colblk))

    in_specs = [rows(D_MODEL), rows(D_MODEL), rows(D_IN), rows(LRU_W), rows(SWA_W), rows(XATT_W),
                rows(4 * 2 * BLOCK), rows(4 * MEM_LEN), rows(128), rows(4 * LRU_W), rows(LRU_W),
                rows(128), rows(128), rows(128),
                halo(8, LRU_W), halo(BLOCK, 2 * KV_W, C_SK // (2 * KV_W)),
                halo(BLOCK, 128), halo(BLOCK, 128), halo(BLOCK, 128),
                _const_spec((1, D_MODEL)), _const_spec((D_IN, D_MODEL), True), _const_spec((CONV_K, LRU_W)),
                _const_spec((2, 256, 512), True), _const_spec((1, LRU_W)), _const_spec((1, GAINS_W)),
                _const_spec((4 * MEM_LEN, XATT_W), True), _const_spec((4 * MEM_LEN, XATT_W), True),
                _const_spec((1, D_MODEL)), _const_spec((D_MODEL, D_MODEL), True)]
    small = [(2, 256, 512), (XATT_W, 4 * MEM_LEN), (XATT_W, 4 * MEM_LEN), (1, D_MODEL), (1, D_MODEL), (1, LRU_W), (1, LRU_W),
             (1, LRU_W), (1, LRU_W), (CONV_K, LRU_W), (1, 128), (1, 128), (1, XATT_W), (1, 128)]
    out_shape = (jax.ShapeDtypeStruct((seq, D_MODEL), F32), jax.ShapeDtypeStruct((seq, D_IN), _MXU)) + tuple(
        jax.ShapeDtypeStruct(s, F32) for s in small)
    out_specs = (rows(D_MODEL), rows(D_IN)) + tuple(_const_spec(s) for s in small)
    scratch = [pltpu.VMEM((tm + 8, LRU_W), F32), pltpu.VMEM((tm + 8, LRU_W), F32),
               pltpu.VMEM((tm, LRU_W), F32), pltpu.VMEM((tm, LRU_W), F32), pltpu.VMEM((tm, LRU_W), F32),
               pltpu.VMEM((tm + 8, LRU_W), F32),
               pltpu.VMEM((8, LRU_W), F32), pltpu.VMEM((BLOCK, KV_W), F32), pltpu.VMEM((BLOCK, KV_W), F32)]
    return pl.pallas_call(
        body, name="layer_bwd", grid=(nt,), out_shape=out_shape, in_specs=in_specs, out_specs=out_specs,
        scratch_shapes=scratch,
        compiler_params=pltpu.CompilerParams(dimension_semantics=("arbitrary",), vmem_limit_bytes=VMEM_LIMIT),
    )(x, dout, proj, ya, yb, yc, pswa, pmem, psink, gates, a_all, rc, rs1, rs2, ya, proj, rc, rs1, rs2,
      ng, win_t, cw, wg, lam, gains, km, vm, og, wout)


def _reduce_protocol(big, sm, outs, osm, r1, r1s, wire, r2, r2s, wire2, ps, own, send, recv, lsem):
    nbig = len(big)
    x, y, c = lax.axis_index("x"), lax.axis_index("y"), lax.axis_index("c")
    sibling = (x, y, 1 - c)
    near, far, diag = _partners(x, y, c)
    me, near_id, far_id, diag_id = _chip_of(x, y), _chip_of(*near), _chip_of(*far), _chip_of(*diag)

    def copy(k, src, dst, to):
        return pltpu.make_async_remote_copy(src_ref=src, dst_ref=dst, send_sem=send.at[k], recv_sem=recv.at[k],
                                            device_id=to, device_id_type=MESH)

    def sent(stage, a):
        if a == nbig:
            src, dst, to = ((sm.at[1 - c], r1s, sibling), (r1s, r2s.at[0], (*near, c)), (ps, r2s.at[1], (*far, c)),
                            (osm.at[c], osm.at[c], sibling))[stage]
            return [copy(5 * nbig + stage, src, dst, to)]
        if stage == 0:
            return [copy(5 * a, big[a].at[:, 1 - c], r1[a], sibling)]
        if stage == 1:
            return [copy(5 * a + 1, wire[a].at[near_id], r2[a].at[0], (*near, c)),
                    copy(5 * a + 2, wire[a].at[diag_id], r2[a].at[1], (*near, c))]
        if stage == 2:
            return [copy(5 * a + 3, wire2[a], r2[a].at[2], (*far, c))]
        return [copy(5 * a + 4, outs[a].at[c], outs[a].at[c], sibling)]

    arrays = range(nbig + (sm is not None))

    def start(stage, a):
        for cp in sent(stage, a):
            cp.start()

    def arrived(k, ref):
        copy(k, ref, ref, sibling).wait_recv()

    def loads():
        return [pltpu.make_async_copy(big[a].at[:, c], own[a], lsem.at[a]) for a in range(nbig)]

    def stage0():
        for a in arrays:
            start(0, a)
        for cp in loads():
            cp.start()

    def stage1():
        for a in range(nbig):
            loads()[a].wait()
            arrived(5 * a, r1[a])
            for k in range(N_CHIPS):
                r1[a][k] = own[a][k] + r1[a][k]
                wire[a][k] = r1[a][k].astype(wire[a].dtype)
            start(1, a)
        if sm is not None:
            arrived(5 * nbig, r1s)
            r1s[...] = sm[c] + r1s[...]
            start(1, nbig)

    def stage2():
        for a in range(nbig):
            arrived(5 * a + 1, r2[a].at[0])
            arrived(5 * a + 2, r2[a].at[1])
            r1[a][me] = r1[a][me] + r2[a][0].astype(F32)
            wire2[a][...] = (r1[a][far_id] + r2[a][1].astype(F32)).astype(wire2[a].dtype)
            start(2, a)
        if sm is not None:
            arrived(5 * nbig + 1, r2s.at[0])
            ps[...] = r1s[...] + r2s[0]
            start(2, nbig)

    def stage3():
        for a in range(nbig):
            arrived(5 * a + 3, r2[a].at[2])
            outs[a][c] = r1[a][me] + r2[a][2].astype(F32)
            start(3, a)
        if sm is not None:
            arrived(5 * nbig + 2, r2s.at[1])
            osm[c] = ps[...] + r2s[1]
            start(3, nbig)

    def stage4():
        for a in range(nbig):
            arrived(5 * a + 4, outs[a].at[1 - c])
        if sm is not None:
            arrived(5 * nbig + 3, osm.at[1 - c])
        for stage in range(4):
            for a in arrays:
                for cp in sent(stage, a):
                    cp.wait_send()

    return [stage0, stage1, stage2, stage3, stage4]


def _reduce_buffers(bigs, g_small):
    half = [b.shape[2:] for b in bigs]
    sm_half = None if g_small is None else g_small.shape[1:]
    out_shape = [jax.ShapeDtypeStruct((2,) + h, F32) for h in half]
    small = lambda lead: [] if g_small is None else [pltpu.VMEM(lead + sm_half, F32)]
    if g_small is not None:
        out_shape.append(jax.ShapeDtypeStruct(g_small.shape, F32))
    n_sem = 5 * len(bigs) + 4
    scratch = ([pltpu.VMEM((N_CHIPS,) + h, F32) for h in half] + small(())
               + [pltpu.VMEM((N_CHIPS,) + h, _WIRE) for h in half]
               + [pltpu.VMEM((3,) + h, _WIRE) for h in half] + small((2,))
               + [pltpu.VMEM(h, _WIRE) for h in half] + small(())
               + [pltpu.VMEM((N_CHIPS,) + h, F32) for h in half]
               + [pltpu.SemaphoreType.DMA((n_sem,)), pltpu.SemaphoreType.DMA((n_sem,)),
                  pltpu.SemaphoreType.DMA((len(bigs),))])
    return out_shape, scratch


def _split_reduce_refs(refs, nbig, has_small):
    it = iter(refs)
    take = lambda n: [next(it) for _ in range(n)]
    one = lambda: next(it) if has_small else None
    big, sm = take(nbig), one()
    outs, osm = take(nbig), one()
    r1, r1s, wire, r2, r2s, wire2, ps, own = take(nbig), one(), take(nbig), take(nbig), one(), take(nbig), one(), take(nbig)
    send, recv, lsem = take(3)
    return big, sm, outs, osm, r1, r1s, wire, r2, r2s, wire2, ps, own, send, recv, lsem


def _hosted_reduce_shapes(bigs, g_small):
    red_shape, scratch = _reduce_buffers(bigs, g_small)
    nres = len(red_shape)
    return red_shape, [pltpu.VMEM(r.shape, r.dtype) for r in red_shape] + scratch + [pltpu.SemaphoreType.DMA((nres,))]


def _hosted_reduce(step, stage_at, operands, results, scratch, has_small):
    nres = len(results)
    sums, rest, fsem = scratch[:nres], scratch[nres:-1], scratch[-1]
    refs = tuple(operands) + tuple(sums) + tuple(rest)
    for at, stage in zip(stage_at, _reduce_protocol(*_split_reduce_refs(refs, nres - has_small, has_small))):
        pl.when(step == at)(stage)

    @pl.when(step == stage_at[-1])
    def _():
        out = [pltpu.make_async_copy(sums[k], results[k], fsem.at[k]) for k in range(nres)]
        for cp in out:
            cp.start()
        for cp in out:
            cp.wait()


def reduce_grads(big, name, parts):
    chips, halves, rows_, cols = big.shape
    sub = jax.ShapeDtypeStruct((chips, halves, rows_ // parts, cols), big.dtype)

    def body(b_ref, o_ref, *scratch):
        refs = [b_ref.at[:, :, s] for s in range(parts)] + [o_ref.at[:, s] for s in range(parts)] + list(scratch)
        for stage in _reduce_protocol(*_split_reduce_refs(refs, parts, False)):
            stage()

    _, scratch = _reduce_buffers([sub] * parts, None)
    return pl.pallas_call(
        body, name=name, out_shape=jax.ShapeDtypeStruct((halves, parts, rows_ // parts, cols), F32),
        in_specs=[pl.BlockSpec(memory_space=pl.ANY)], out_specs=pl.BlockSpec(memory_space=pltpu.VMEM),
        scratch_shapes=scratch, compiler_params=pltpu.CompilerParams(vmem_limit_bytes=VMEM_LIMIT),
    )(big.reshape(chips, halves, parts, rows_ // parts, cols))


def adamw_matrices(items):
    plan, total = [], 0
    for w, _, _, _ in items:
        rows_, cols = w.shape
        tr = max(t for t in range(8, rows_ + 1, 8) if rows_ % t == 0 and t * cols * 4 <= ADAM_BLOCK_BYTES)
        plan.append((total, rows_ // tr, tr, cols))
        total += rows_ // tr
    nin = 4 * len(items)

    def body(*refs):
        i = pl.program_id(0)
        for k, (first, steps, _, _) in enumerate(plan):
            w_ref, g_ref, m_ref, v_ref = refs[4 * k:4 * k + 4]
            go_ref, d_ref, nm_ref, nv_ref = refs[nin + 4 * k:nin + 4 * k + 4]

            @pl.when((i >= first) & (i < first + steps))
            def _():
                gv = g_ref[...]
                go_ref[...] = gv
                d_ref[...], nm_ref[...], nv_ref[...] = _adam_update(w_ref[...], gv, m_ref[...], v_ref[...])

    specs, shapes = [], []
    for (first, steps, tr, cols), (w, _, _, _) in zip(plan, items):
        spec = pl.BlockSpec((tr, cols), lambda i, first=first, steps=steps: (jnp.clip(i - first, 0, steps - 1), 0))
        specs += [spec] * 4
        shapes += [jax.ShapeDtypeStruct(w.shape, F32)] * 4
    res = pl.pallas_call(
        body, name="adamw_matrices", grid=(total,), out_shape=tuple(shapes), in_specs=specs, out_specs=tuple(specs),
        compiler_params=pltpu.CompilerParams(dimension_semantics=("arbitrary",)),
    )(*[a for item in items for a in item])
    return [res[4 * k:4 * k + 4] for k in range(len(items))]


def _adam_update(w, g, m, v):
    nm = ADAM_B1 * m + (1.0 - ADAM_B1) * g
    nv = ADAM_B2 * v + (1.0 - ADAM_B2) * (g * g)
    m_hat = nm / (1.0 - ADAM_B1 ** ADAM_STEP)
    v_hat = nv / (1.0 - ADAM_B2 ** ADAM_STEP)
    return (-ADAM_LR) * (m_hat / (jnp.sqrt(v_hat) + ADAM_EPS) + ADAM_WD * w), nm, nv


def adamw_vectors(g_pack, ws, ms, vs):
    nvec = len(SMALL_VECTORS)
    n = nvec + len(SMALL_MATRICES)

    def body(*refs):
        pk = refs[0]
        w_refs, m_refs, v_refs = (refs[1 + k * n:1 + (k + 1) * n] for k in range(3))
        g_out, d_out, nm_out, nv_out = (refs[1 + (3 + k) * n:1 + (4 + k) * n] for k in range(4))
        refs[-1][...] = pk[LOSS_ROW:LOSS_ROW + 1, 0:1]
        chip = 2 * lax.axis_index("x") + lax.axis_index("y")
        for k, (name, row, width) in enumerate(SMALL_VECTORS):
            if name == "conv_w":
                g = jnp.concatenate([pk[pl.ds(row + 4 * t + chip, 1), :] for t in range(CONV_K)], axis=0)[None]
            elif width >= 128:
                g = jnp.concatenate([pk[row + r:row + r + 1, :] for r in range(width // 128)], axis=1)
            else:
                g = pk[row:row + 1, 0:width]
            g_out[k][...] = g
            d_out[k][...], nm_out[k][...], nv_out[k][...] = _adam_update(w_refs[k][...], g, m_refs[k][...], v_refs[k][...])
        for k in range(nvec, n):
            for b in range(LRU_BLOCKS):
                rows_ = pk[GATES_ROW + HEAD * b:GATES_ROW + HEAD * (b + 1), :]
                g = (pltpu.roll(rows_, HEAD, axis=1) if k > nvec else rows_)[:, 0:HEAD]
                g_out[k][0, b] = g
                d_out[k][0, b], nm_out[k][0, b], nv_out[k][0, b] = _adam_update(
                    w_refs[k][0, b], g, m_refs[k][0, b], v_refs[k][0, b])

    vm = pl.BlockSpec(memory_space=pltpu.VMEM)
    like = [jax.ShapeDtypeStruct(w.shape, F32) for w in ws]
    return pl.pallas_call(
        body, name="adamw_vectors", out_shape=(*like * 4, jax.ShapeDtypeStruct((1, 1), F32)),
        in_specs=[vm] * (1 + 3 * n), out_specs=(vm,) * (4 * n + 1),
    )(g_pack, *ws, *ms, *vs)


SMALL_VECTORS = (("norm_g", 0, 1024), ("mem_norm_g", 8, 1024), ("conv_w", 16, 512), ("conv_b", 32, 512),
                 ("b_rg", 36, 512), ("b_ig", 40, 512), ("lru_lambda", 44, 512), ("q_norm_g", 48, 64),
                 ("k_norm_g", 49, 64), ("sinks", 50, 4), ("xq_norm_g", 51, 64), ("xk_norm_g", 52, 64),
                 ("out_norm_g", 53, 1024))
LOSS_ROW = 61
SMALL_MATRICES = ("w_rg", "w_ig")
GATES_ROW = 64
SMALL_ROWS = GATES_ROW + LRU_BLOCKS * HEAD


def _rope_tables(seq):
    pos = np.arange(seq, dtype=np.float32)
    inv_freq = (np.float32(ROPE_THETA) ** (-(np.arange(0, ROPE_DIM, 2, dtype=np.float32) / np.float32(ROPE_DIM)))
                ).astype(np.float32)
    ang = (pos[:, None] * inv_freq[None, :]).astype(np.float32)
    cos, sin = np.cos(ang).astype(np.float32), np.sin(ang).astype(np.float32)
    z = lambda n: np.zeros((seq, n), np.float32)
    c64 = np.concatenate([cos, cos, np.ones((seq, HEAD - ROPE_DIM), np.float32)], axis=1)
    s1_64 = np.concatenate([-sin, z(HEAD - 8)], axis=1)
    s2_64 = np.concatenate([z(8), sin, z(HEAD - ROPE_DIM)], axis=1)
    return tuple(jnp.asarray(np.concatenate([t, t], axis=1)) for t in (c64, s1_64, s2_64))


def kernel(x, mem, norm_g, mem_norm_g, w_in, conv_w, conv_b, w_rg, b_rg, w_ig, b_ig, lru_lambda, q_norm_g, k_norm_g, sinks, w_mem_kv, xq_norm_g, xk_norm_g, out_norm_g, w_out, loss_target, m_norm_g, m_mem_norm_g, m_w_in, m_conv_w, m_conv_b, m_w_rg, m_b_rg, m_w_ig, m_b_ig, m_lru_lambda, m_q_norm_g, m_k_norm_g, m_sinks, m_w_mem_kv, m_xq_norm_g, m_xk_norm_g, m_out_norm_g, m_w_out, v_norm_g, v_mem_norm_g, v_w_in, v_conv_w, v_conv_b, v_w_rg, v_b_rg, v_w_ig, v_b_ig, v_lru_lambda, v_q_norm_g, v_k_norm_g, v_sinks, v_w_mem_kv, v_xq_norm_g, v_xk_norm_g, v_out_norm_g, v_w_out):
    seq = x.shape[1]
    xs, tgt, mems = x[0], loss_target[0], mem[0]

    win_t, wout, wkv, cw, wg, gains = gather_weights(w_in[0].T, w_out[0], w_mem_kv[0], conv_w, w_rg, w_ig,
                                                     (q_norm_g, k_norm_g, xq_norm_g, xk_norm_g))
    rc, rs1, rs2 = _rope_tables(seq)

    km, vm = mem_fwd(mems, mem_norm_g, wkv, gains)
    proj, ya, yb, yc, ycat, xn, dout, pswa, pmem, psink, gates, a_all, loss8 = layer_fwd(
        xs, tgt, rc, rs1, rs2, norm_g, win_t, cw, conv_b, wg, b_rg, b_ig, lru_lambda, gains, sinks, km, vm,
        out_norm_g, wout)
    (gx, dproj, g_wg, dkm, dvm, g_ng, g_og, g_cb, g_brg, g_big, g_lam, g_cw, g_qn, g_kn, g_xqn, g_sink) = layer_bwd(
        xs, dout, proj, ya, yb, yc, pswa, pmem, psink, gates, a_all, rc, rs1, rs2, norm_g, win_t, cw, wg, lru_lambda,
        gains, km, vm, out_norm_g, wout)
    g_wkv, small_g = mem_bwd(mems, mem_norm_g, wkv, gains, dkm, dvm, g_wg, loss8, dict(
        norm_g=g_ng, conv_w=g_cw, conv_b=g_cb, b_rg=g_brg, b_ig=g_big, lru_lambda=g_lam, q_norm_g=g_qn, k_norm_g=g_kn,
        sinks=g_sink, xq_norm_g=g_xqn, out_norm_g=g_og))
    g_win_t, r_out, r_kv, r_small = weight_grads(
        ycat, dout, dproj, xn, g_wkv.reshape(N_CHIPS, 2, D_MODEL // 8, 2 * XATT_W),
        small_g.reshape(2, SMALL_ROWS // 2, 128), (0, 1, 3, 5, 6), (4, 5, 9, 11, 12))
    r_in = reduce_grads(g_win_t.reshape(N_CHIPS, 2, D_IN // 8, D_MODEL), "reduce_w_in", 6)

    r_small = r_small.reshape(SMALL_ROWS, 128)
    grads = {}
    weights = dict(norm_g=norm_g, mem_norm_g=mem_norm_g, w_in=w_in, conv_w=conv_w, conv_b=conv_b, w_rg=w_rg, b_rg=b_rg,
                   w_ig=w_ig, b_ig=b_ig, lru_lambda=lru_lambda, q_norm_g=q_norm_g, k_norm_g=k_norm_g, sinks=sinks,
                   w_mem_kv=w_mem_kv, xq_norm_g=xq_norm_g, xk_norm_g=xk_norm_g, out_norm_g=out_norm_g, w_out=w_out)
    ms = dict(norm_g=m_norm_g, mem_norm_g=m_mem_norm_g, w_in=m_w_in, conv_w=m_conv_w, conv_b=m_conv_b, w_rg=m_w_rg,
              b_rg=m_b_rg, w_ig=m_w_ig, b_ig=m_b_ig, lru_lambda=m_lru_lambda, q_norm_g=m_q_norm_g, k_norm_g=m_k_norm_g,
              sinks=m_sinks, w_mem_kv=m_w_mem_kv, xq_norm_g=m_xq_norm_g, xk_norm_g=m_xk_norm_g,
              out_norm_g=m_out_norm_g, w_out=m_w_out)
    vs = dict(norm_g=v_norm_g, mem_norm_g=v_mem_norm_g, w_in=v_w_in, conv_w=v_conv_w, conv_b=v_conv_b, w_rg=v_w_rg,
              b_rg=v_b_rg, w_ig=v_w_ig, b_ig=v_b_ig, lru_lambda=v_lru_lambda, q_norm_g=v_q_norm_g, k_norm_g=v_k_norm_g,
              sinks=v_sinks, w_mem_kv=v_w_mem_kv, xq_norm_g=v_xq_norm_g, xk_norm_g=v_xk_norm_g,
              out_norm_g=v_out_norm_g, w_out=v_w_out)

    delta, new_m, new_v = {}, {}, {}
    res_in, res_out, res_kv = adamw_matrices([
        (w_in[0].T, r_in.reshape(D_IN // 4, D_MODEL), m_w_in[0].T, v_w_in[0].T),
        (w_out[0], r_out.reshape(D_MODEL // 4, D_MODEL), m_w_out[0], v_w_out[0]),
        (w_mem_kv[0], r_kv.reshape(D_MODEL // 4, 2 * XATT_W), m_w_mem_kv[0], v_w_mem_kv[0])])
    grads["w_in"], delta["w_in"], new_m["w_in"], new_v["w_in"] = (r.T[None] for r in res_in)
    grads["w_out"], delta["w_out"], new_m["w_out"], new_v["w_out"] = (r[None] for r in res_out)
    grads["w_mem_kv"], delta["w_mem_kv"], new_m["w_mem_kv"], new_v["w_mem_kv"] = (r[None] for r in res_kv)
    small_names = [n for n, _, _ in SMALL_VECTORS] + list(SMALL_MATRICES)
    res = adamw_vectors(r_small, [weights[n] for n in small_names], [ms[n] for n in small_names],
                        [vs[n] for n in small_names])
    nall = len(small_names)
    for k, into in enumerate((grads, delta, new_m, new_v)):
        into.update(zip(small_names, res[k * nall:(k + 1) * nall]))
    loss = res[-1].reshape(())

    order = ("norm_g", "mem_norm_g", "w_in", "conv_w", "conv_b", "w_rg", "b_rg", "w_ig", "b_ig", "lru_lambda",
             "q_norm_g", "k_norm_g", "sinks", "w_mem_kv", "xq_norm_g", "xk_norm_g", "out_norm_g", "w_out")
    return (loss, gx[None], *[grads[n] for n in order], *[delta[n] for n in order], *[new_m[n] for n in order],
            *[new_v[n] for n in order])
```

```python
import jax
import jax.numpy as jnp
import numpy as np
from jax import lax
from jax.experimental import pallas as pl
from jax.experimental.pallas import tpu as pltpu

F32 = jnp.float32
_MXU = jnp.bfloat16
_WIRE = jnp.bfloat16

D_MODEL = 1024
MEM_LEN = 256
HEAD = 64
LRU_W = 512
LRU_BLOCKS = 8
CONV_K = 4
LRU_C = 8.0
SWA_W = 256
KV_W = 128
XATT_W = 256
BLOCK = 128
D_IN = 2304
ROPE_THETA = 500000.0
ROPE_DIM = 16
EPS = 1e-6
NEG_INF = -1e30
C_LRUX, C_LRUG, C_SQ, C_SK, C_SV, C_SWAG, C_XQ, C_XG = 0, 512, 1024, 1280, 1408, 1536, 1792, 2048
G_Q, G_K, G_XQ, G_XK, GAINS_W = 0, 128, 256, 512, 768

ADAM_LR, ADAM_B1, ADAM_B2, ADAM_EPS, ADAM_WD, ADAM_STEP = 0.001, 0.9, 0.999, 1e-08, 0.01, 10

N_CHIPS = 4
ROW_TILE = 256
VMEM_LIMIT = 56 * 1024 * 1024
ADAM_BLOCK_BYTES = 640 * 1024
MESH = pl.DeviceIdType.MESH


def _mm(a, b):
    return jnp.dot(a.astype(_MXU), b.astype(_MXU), preferred_element_type=F32)


def _mm_nt(a, b):
    return lax.dot_general(a.astype(_MXU), b.astype(_MXU), (((1,), (1,)), ((), ())), preferred_element_type=F32)


def _mm_tn(a, b):
    return lax.dot_general(a.astype(_MXU), b.astype(_MXU), (((0,), (0,)), ((), ())), preferred_element_type=F32)


def _group_matrix(width):
    r = lax.shift_right_logical(lax.broadcasted_iota(jnp.int32, (width, width), 0), 6)
    c = lax.shift_right_logical(lax.broadcasted_iota(jnp.int32, (width, width), 1), 6)
    return (r == c).astype(_MXU)


def _seg_mean(x, gm):
    return jnp.dot(x.astype(_MXU), gm, preferred_element_type=F32) * (1.0 / HEAD)


def _row_mean(x):
    return jnp.mean(x, axis=-1, keepdims=True)


def _col_sum(x):
    return jnp.sum(x, axis=0, keepdims=True)


def _sigmoid(x):
    return jax.nn.sigmoid(x)


def _softplus(z):
    e = jnp.exp(-jnp.abs(z))
    u = 1.0 + e
    log1p_e = jnp.where(u == 1.0, e, jnp.log(u) * (e / (u - 1.0)))
    return jnp.maximum(z, 0.0) + log1p_e


def _rope(t, c, s1, s2):
    return t * c + pltpu.roll(t, 120, 1) * s1 + pltpu.roll(t, 8, 1) * s2


def _rope_bwd(d, c, s1, s2):
    return d * c + pltpu.roll(d * s1, 8, 1) + pltpu.roll(d * s2, 120, 1)


def _fold_heads(v):
    out = v
    for k in range(1, v.shape[1] // HEAD):
        out = out + pltpu.roll(v, HEAD * k, 1)
    return out


def _lane_mask(width, lo, hi):
    lane = lax.broadcasted_iota(jnp.int32, (1, width), 1)
    return ((lane >= lo) & (lane < hi)).astype(F32)


def _swa_mask(first_block):
    qi = lax.broadcasted_iota(jnp.int32, (BLOCK, 2 * BLOCK), 0)
    kj = lax.broadcasted_iota(jnp.int32, (BLOCK, 2 * BLOCK), 1)
    rel = qi + BLOCK - kj
    ok = (rel >= 0) & (rel < BLOCK)
    return ok & (jnp.logical_not(first_block) | (kj >= BLOCK))


def _place_kv(t, scale):
    lo = t * (_lane_mask(KV_W, 0, HEAD) * scale)
    hi = t * (_lane_mask(KV_W, HEAD, KV_W) * scale)
    return [a.astype(_MXU) for a in (lo, pltpu.roll(lo, HEAD, 1), pltpu.roll(hi, HEAD, 1), hi)]


def _unplace_kv(d):
    return (_lane_mask(KV_W, 0, HEAD) * (d[0] + pltpu.roll(d[1], HEAD, 1))
            + _lane_mask(KV_W, HEAD, KV_W) * (d[3] + pltpu.roll(d[2], HEAD, 1)))


def _swa_probs(qh, ka, mask, sink):
    s = _mm_nt(qh, ka)
    s = jnp.where(mask, s, NEG_INF)
    m = jnp.maximum(jnp.max(s, axis=-1, keepdims=True), sink)
    p = jnp.exp(s - m)
    esink = jnp.exp(sink - m)
    inv = 1.0 / (jnp.sum(p, axis=-1, keepdims=True) + esink)
    return p * inv, esink * inv


def _mem_probs(s_all):
    out = []
    for j in range(4):
        s = s_all[:, MEM_LEN * j:MEM_LEN * (j + 1)]
        p = jnp.exp(s - jnp.max(s, axis=-1, keepdims=True))
        out.append(p * (1.0 / jnp.sum(p, axis=-1, keepdims=True)))
    return out


def _head_rows(t, scale):
    return jnp.concatenate([t * (_lane_mask(XATT_W, HEAD * j, HEAD * (j + 1)) * scale) for j in range(4)], axis=0)


def _lru_gates(xc, wg_ref, brg, big, lam):
    p0 = _mm(xc[:, :256], wg_ref[0])
    p1 = _mm(xc[:, 256:], wg_ref[1])
    rg = _sigmoid(jnp.concatenate([p0[:, :256], p1[:, :256]], axis=1) + brg)
    ig = _sigmoid(jnp.concatenate([p0[:, 256:], p1[:, 256:]], axis=1) + big)
    sp = _softplus(-lam)
    la = (-LRU_C) * rg * sp
    a = jnp.exp(la)
    th = jnp.tanh(la)
    one_minus_a2 = (-2.0 * th) / (1.0 - th)
    return rg, ig, sp, a, jnp.sqrt(one_minus_a2)


def _const_spec(shape, single=False):
    zeros = (0,) * len(shape)
    if single:
        return pl.BlockSpec(shape, lambda i: zeros, pipeline_mode=pl.Buffered(1))
    return pl.BlockSpec(shape, lambda i: zeros)


def _chip_of(x, y):
    return 2 * x + y


def _partners(x, y, c):
    north = c == 1
    near = (jnp.where(north, 1 - x, x), jnp.where(north, y, 1 - y))
    far = (jnp.where(north, x, 1 - x), jnp.where(north, 1 - y, y))
    return near, far, (1 - x, 1 - y)


def gather_weights(win_t, wout, wkv, conv_w, w_rg, w_ig, head_gains):
    arrs = (win_t, wout, wkv)
    n = len(arrs)
    pieces = [(a, k * (arr.shape[0] // (2 * cut)), arr.shape[0] // (2 * cut))
              for a, (arr, cut) in enumerate(zip(arrs, (2, 1, 1))) for k in range(cut)]
    npc = len(pieces)

    def body(a0, a1, a2, cw_in, wrg_ref, wig_ref, q_ref, k_ref, xq_ref, xk_ref, o0, o1, o2, cw_out, wg_ref, gn_ref,
             s0, s1, s2, cw, ocw, send, recv, lsem):
        ins, outs = (s0, s1, s2), (o0, o1, o2)
        for src, dst in zip((a0, a1, a2), ins):
            dst[...] = src[...].astype(dst.dtype)
        cw[...] = jnp.zeros(cw.shape, F32)
        cw[0:CONV_K, :] = cw_in[0]
        x, y, c = lax.axis_index("x"), lax.axis_index("y"), lax.axis_index("c")
        sibling = (x, y, 1 - c)
        near, far, diag = _partners(x, y, c)
        chips = [near, far, diag]
        me = _chip_of(x, y)

        def landed(p, chip, half):
            a, off, rows_ = pieces[p]
            r = ins[a].shape[0]
            return outs[a].at[pl.ds(pl.multiple_of(chip * r + half * (r // 2) + off, 16), rows_)]

        def mine(p):
            a, off, rows_ = pieces[p]
            return ins[a].at[pl.ds(pl.multiple_of(c * (ins[a].shape[0] // 2) + off, 16), rows_)]

        def copy(k, src, dst, to):
            return pltpu.make_async_remote_copy(src_ref=src, dst_ref=dst, send_sem=send.at[k], recv_sem=recv.at[k],
                                                device_id=to, device_id_type=MESH)

        def cw_rows(chip):
            return ocw.at[pl.ds(pl.multiple_of(chip * 8, 8), 8)]

        locals_ = []
        for a in range(n):
            r = ins[a].shape[0]
            locals_.append(pltpu.make_async_copy(ins[a], outs[a].at[pl.ds(pl.multiple_of(me * r, 16), r)], lsem.at[a]))
        locals_.append(pltpu.make_async_copy(cw, cw_rows(me), lsem.at[n]))
        for cp in locals_:
            cp.start()

        sent = []
        for p in range(npc):
            for j in range(2):
                sent.append(copy(p * 6 + j, mine(p), landed(p, me, c), (*chips[j], c)))
        for j, chip in enumerate(chips):
            sent.append(copy(npc * 6 + j, cw, cw_rows(me), (*chip, c)))
        for cp in sent:
            cp.start()

        gn_ref[...] = jnp.concatenate([q_ref[...]] * 2 + [k_ref[...]] * 2 + [xq_ref[...]] * 4 + [xk_ref[...]] * 4,
                                      axis=1)
        zeros = lambda lanes: [jnp.zeros((HEAD, lanes), F32)] if lanes else []
        for h in range(2):
            for b in range(4):
                row = []
                for w_ref in (wrg_ref, wig_ref):
                    row += zeros(HEAD * b) + [w_ref[0, 4 * h + b]] + zeros(HEAD * (3 - b))
                wg_ref[h, HEAD * b:HEAD * (b + 1), :] = jnp.concatenate(row, axis=1).astype(wg_ref.dtype)

        for j in range(3):
            for p in range(npc):
                got = landed(p, _chip_of(*chips[j]), c)
                copy(p * 6 + j, got, got, sibling).wait_recv()
                if j == 0:
                    sent.append(copy(p * 6 + 2, got, got, (*far, c)))
                    sent[-1].start()
                sent.append(copy(p * 6 + 3 + j, got, got, sibling))
                sent[-1].start()
        for p in range(npc):
            for j in range(3):
                got = landed(p, _chip_of(*chips[(1, 0, 2)[j]]), 1 - c)
                copy(p * 6 + 3 + j, got, got, sibling).wait_recv()
        for j, chip in enumerate(chips):
            got = cw_rows(_chip_of(*chip))
            copy(npc * 6 + j, got, got, (*chip, c)).wait_recv()
        for cp in sent:
            cp.wait_send()
        for cp in locals_:
            cp.wait()
        for chip in range(N_CHIPS):
            cw_out[:, 128 * chip:128 * (chip + 1)] = ocw[8 * chip:8 * chip + CONV_K, :]

    vm = pl.BlockSpec(memory_space=pltpu.VMEM)
    out_shape = tuple(jax.ShapeDtypeStruct((N_CHIPS * a.shape[0],) + a.shape[1:], _MXU) for a in arrs) + (
        jax.ShapeDtypeStruct((CONV_K, LRU_W), F32), jax.ShapeDtypeStruct((2, 256, 512), _MXU),
        jax.ShapeDtypeStruct((1, GAINS_W), F32))
    n_rdma = npc * 6 + 3
    return pl.pallas_call(
        body, name="gather_weights", out_shape=out_shape,
        in_specs=[vm] * 10, out_specs=(pl.BlockSpec(memory_space=pl.ANY),) * n + (vm, vm, vm),
        scratch_shapes=[pltpu.VMEM(a.shape, _MXU) for a in arrs] + [
            pltpu.VMEM((8, 128), F32), pltpu.VMEM((N_CHIPS * 8, 128), F32),
            pltpu.SemaphoreType.DMA((n_rdma,)), pltpu.SemaphoreType.DMA((n_rdma,)), pltpu.SemaphoreType.DMA((n + 1,))],
        compiler_params=pltpu.CompilerParams(vmem_limit_bytes=VMEM_LIMIT),
    )(win_t, wout, wkv, conv_w, w_rg, w_ig, *head_gains)


def mem_fwd(mem, mem_g, wkv, gains):
    def body(mem_ref, g_ref, w_ref, gn_ref, km_ref, vm_ref):
        mem_v = mem_ref[...]
        mn = mem_v * lax.rsqrt(_row_mean(mem_v * mem_v) + EPS) * g_ref[...]
        mkv = _mm(mn, w_ref[...])
        kpre = mkv[:, :XATT_W]
        gm = _group_matrix(XATT_W)
        km = kpre * lax.rsqrt(_seg_mean(kpre * kpre, gm) + EPS) * gn_ref[:, G_XK:GAINS_W]
        km_ref[...] = _head_rows(km, 0.125).astype(km_ref.dtype)
        vm_ref[...] = _head_rows(mkv[:, XATT_W:], 1.0).astype(vm_ref.dtype)

    vm = pl.BlockSpec(memory_space=pltpu.VMEM)
    rows_shape = jax.ShapeDtypeStruct((4 * MEM_LEN, XATT_W), _MXU)
    return pl.pallas_call(
        body, name="mem_fwd", out_shape=(rows_shape, rows_shape), in_specs=[vm] * 4, out_specs=(vm, vm),
    )(mem, mem_g, wkv, gains)


def mem_bwd(mem, mem_g, wkv, gains, dkm, dvm, g_gates, loss8, vectors):
    names = tuple(vectors)
    first_row = {name: (row, width) for name, row, width in SMALL_VECTORS}

    def body(mem_ref, g_ref, w_ref, gn_ref, dkm_ref, dvm_ref, gg_ref, loss_ref, *rest):
        vec_refs, (gw_ref, pk_ref) = rest[:len(names)], rest[len(names):]
        pk_ref[...] = jnp.zeros(pk_ref.shape, F32)

        def put(name, src):
            row, width = first_row[name]
            per_row = 1 if width < 128 else src.shape[1] // 128
            for t in range(src.shape[0]):
                for r in range(per_row):
                    at = row + per_row * t + r
                    pk_ref[at:at + 1, :] = src[t:t + 1, 128 * r:128 * (r + 1)]

        for name, ref in zip(names, vec_refs):
            put(name, ref)
        pk_ref[LOSS_ROW:LOSS_ROW + 1, :] = loss_ref[0:1, :]
        upper = lax.broadcasted_iota(jnp.int32, (HEAD, 128), 1) >= HEAD
        for h in range(2):
            for b in range(4):
                rg = gg_ref[h, HEAD * b:HEAD * (b + 1), 128 * (b // 2):128 * (b // 2 + 1)]
                ig = gg_ref[h, HEAD * b:HEAD * (b + 1), 256 + 128 * (b // 2):256 + 128 * (b // 2 + 1)]
                if b % 2:
                    rg = pltpu.roll(rg, HEAD, axis=1)
                else:
                    ig = pltpu.roll(ig, HEAD, axis=1)
                at = GATES_ROW + HEAD * (4 * h + b)
                pk_ref[at:at + HEAD, :] = jnp.where(upper, ig, rg)

        mem_v = mem_ref[...]
        mh = mem_v * lax.rsqrt(_row_mean(mem_v * mem_v) + EPS)
        mn = mh * g_ref[...]
        mkv = _mm(mn, w_ref[...])
        kpre = mkv[:, :XATT_W]
        gm = _group_matrix(XATT_W)
        rk = lax.rsqrt(_seg_mean(kpre * kpre, gm) + EPS)
        kn = kpre * rk
        dk = jnp.zeros((MEM_LEN, XATT_W), F32)
        dv = jnp.zeros((MEM_LEN, XATT_W), F32)
        for j in range(4):
            mj = _lane_mask(XATT_W, HEAD * j, HEAD * (j + 1))
            dk = dk + dkm_ref[:, MEM_LEN * j:MEM_LEN * (j + 1)].T * (mj * 0.125)
            dv = dv + dvm_ref[:, MEM_LEN * j:MEM_LEN * (j + 1)].T * mj
        put("xk_norm_g", _fold_heads(_col_sum(dk * kn)))
        dkn = dk * gn_ref[:, G_XK:GAINS_W]
        dkpre = rk * (dkn - kn * _seg_mean(dkn * kn, gm))
        dmkv = jnp.concatenate([dkpre, dv], axis=1)
        gw_ref[...] = _mm_tn(mn, dmkv)
        dmn = _mm_nt(dmkv, w_ref[...])
        put("mem_norm_g", _col_sum(dmn * mh))

    vm = pl.BlockSpec(memory_space=pltpu.VMEM)
    return pl.pallas_call(
        body, name="mem_bwd",
        out_shape=(jax.ShapeDtypeStruct((D_MODEL, 2 * XATT_W), F32), jax.ShapeDtypeStruct((SMALL_ROWS, 128), F32)),
        in_specs=[vm] * (8 + len(names)), out_specs=(vm, vm),
    )(mem, mem_g, wkv, gains, dkm, dvm, g_gates, loss8, *vectors.values())


def layer_fwd(x, tgt, rc, rs1, rs2, ng, win_t, cw, cb, wg, brg, big, lam, gains, sinks, km, vm, og, wout):
    seq = x.shape[0]
    tm = min(ROW_TILE, seq)
    nt = seq // tm
    nb = tm // BLOCK

    def body(x_ref, t_ref, c_ref, s1_ref, s2_ref, ng_ref, win_ref, cw_ref, cb_ref, wg_ref, brg_ref, big_ref, lam_ref,
             gn_ref, sink_ref, km_ref, vm_ref, og_ref, wout_ref,
             proj_ref, ya_ref, yb_ref, yc_ref, ycat_ref, xn_ref, dout_ref, pswa_ref, pmem_ref, psink_ref, gates_ref,
             a_ref, loss_ref,
             ext_ref, b_scr, hc_ref, kp_ref, vp_ref, lacc_ref):
        i = pl.program_id(0)

        @pl.when(i == 0)
        def _():
            ext_ref[0:8, :] = jnp.zeros((8, LRU_W), F32)
            hc_ref[...] = jnp.zeros_like(hc_ref)
            kp_ref[...] = jnp.zeros_like(kp_ref)
            vp_ref[...] = jnp.zeros_like(vp_ref)
            lacc_ref[...] = jnp.zeros_like(lacc_ref)

        xv = x_ref[...]
        xn = (xv * lax.rsqrt(_row_mean(xv * xv) + EPS) * ng_ref[...]).astype(_MXU)
        xn_ref[...] = xn.astype(xn_ref.dtype)
        proj_ref[...] = _mm_nt(xn, win_ref[...])

        u = proj_ref[:, C_LRUX:C_LRUX + LRU_W]
        ext_ref[8:8 + tm, :] = u
        xc = cb_ref[...]
        for k in range(CONV_K):
            xc = xc + cw_ref[k:k + 1, :] * ext_ref[pl.ds(5 + k, tm), :]
        ext_ref[0:8, :] = u[tm - 8:tm, :]
        rg, ig, sp, a, sq = _lru_gates(xc, wg_ref, brg_ref[...], big_ref[...], lam_ref[...])
        for k, t in enumerate((xc, rg, ig, sq)):
            gates_ref[:, LRU_W * k:LRU_W * (k + 1)] = t.astype(gates_ref.dtype)
        a_ref[...] = a
        b_scr[...] = sq * (ig * xc)
        row8 = lax.broadcasted_iota(jnp.int32, (8, LRU_W), 0)

        def scan_step(g, carry):
            r0 = pl.multiple_of(g * 8, 8)
            av = a_ref[pl.ds(r0, 8), :]
            bv = b_scr[pl.ds(r0, 8), :]
            for d in (1, 2, 4):
                a_sh = jnp.where(row8 >= d, pltpu.roll(av, d, 0), 1.0)
                b_sh = jnp.where(row8 >= d, pltpu.roll(bv, d, 0), 0.0)
                bv = bv + av * b_sh
                av = av * a_sh
            hv = bv + av * carry
            ya_ref[pl.ds(r0, 8), :] = hv
            return hv[7:8, :]

        hc_ref[0:1, :] = lax.fori_loop(0, tm // 8, scan_step, hc_ref[0:1, :], unroll=True)

        gm128 = _group_matrix(KV_W)
        cv, s1v, s2v = c_ref[...], s1_ref[...], s2_ref[...]

        def head_norm_rope(t, g):
            n = t * lax.rsqrt(_seg_mean(t * t, gm128) + EPS)
            return _rope(n * g, cv, s1v, s2v)

        qs_ = (head_norm_rope(proj_ref[:, C_SQ:C_SQ + 128], gn_ref[:, G_Q:G_K]).astype(_MXU),
               head_norm_rope(proj_ref[:, C_SQ + 128:C_SQ + 256], gn_ref[:, G_Q:G_K]).astype(_MXU))
        kr = head_norm_rope(proj_ref[:, C_SK:C_SK + KV_W], gn_ref[:, G_K:G_XQ])
        sv = proj_ref[:, C_SV:C_SV + KV_W]
        ka = _place_kv(jnp.concatenate([kp_ref[...], kr], axis=0), 0.125)
        va = _place_kv(jnp.concatenate([vp_ref[...], sv], axis=0), 1.0)
        kp_ref[...] = kr[tm - BLOCK:tm, :]
        vp_ref[...] = sv[tm - BLOCK:tm, :]
        lane128 = lax.broadcasted_iota(jnp.int32, (1, 128), 1)
        for b in range(nb):
            mask = _swa_mask((i == 0) & (b == 0)) if b == 0 else _swa_mask(False)
            band = slice(BLOCK * b, BLOCK * b + 2 * BLOCK)
            blk = slice(BLOCK * b, BLOCK * (b + 1))
            psink = jnp.zeros((BLOCK, 128), F32)
            for j in range(4):
                p, pk = _swa_probs(qs_[j // 2][blk], ka[j][band], mask, sink_ref[0, j])
                pswa_ref[blk, 2 * BLOCK * j:2 * BLOCK * (j + 1)] = p.astype(pswa_ref.dtype)
                psink = jnp.where(lane128 == j, pk, psink)
            psink_ref[blk, :] = psink
            for h in range(2):
                yb_ref[blk, KV_W * h:KV_W * (h + 1)] = _mm(
                    pswa_ref[blk, 4 * BLOCK * h:4 * BLOCK * (h + 1)],
                    jnp.concatenate([va[2 * h][band], va[2 * h + 1][band]], axis=0))

        gm256 = _group_matrix(XATT_W)
        xq = proj_ref[:, C_XQ:C_XQ + XATT_W]
        qx = xq * lax.rsqrt(_seg_mean(xq * xq, gm256) + EPS) * gn_ref[:, G_XQ:G_XK]
        pm = _mem_probs(_mm_nt(qx, km_ref[...]))
        for j in range(4):
            pmem_ref[:, MEM_LEN * j:MEM_LEN * (j + 1)] = pm[j].astype(pmem_ref.dtype)
        yc = _mm(pmem_ref[...], vm_ref[...])
        yc_ref[...] = yc

        def gated(y, g, gate):
            return y * lax.rsqrt(_row_mean(y * y) + EPS) * g * (gate * _sigmoid(gate))

        ogv = og_ref[...]
        za = gated(ya_ref[...], ogv[:, :512], proj_ref[:, C_LRUG:C_LRUG + LRU_W])
        zb = gated(yb_ref[...], ogv[:, 512:768], proj_ref[:, C_SWAG:C_SWAG + SWA_W])
        zc = gated(yc, ogv[:, 768:], proj_ref[:, C_XG:C_XG + XATT_W])
        ycat_ref[:, 0:512] = za.astype(ycat_ref.dtype)
        ycat_ref[:, 512:768] = zb.astype(ycat_ref.dtype)
        ycat_ref[:, 768:1024] = zc.astype(ycat_ref.dtype)
        out = xv + _mm(ycat_ref[...], wout_ref[...])
        err = out - t_ref[...]
        dout_ref[...] = (err * (1.0 / D_MODEL)).astype(dout_ref.dtype)
        lacc_ref[...] = lacc_ref[...] + (0.5 / D_MODEL) * jnp.sum(err * err)

        @pl.when(i == nt - 1)
        def _():
            loss_ref[...] = lacc_ref[...]

    def rows(ncol):
        return pl.BlockSpec((tm, ncol), lambda i: (i, 0))

    in_specs = [rows(D_MODEL), rows(D_MODEL), rows(128), rows(128), rows(128),
                _const_spec((1, D_MODEL)), _const_spec((D_IN, D_MODEL), True), _const_spec((CONV_K, LRU_W)),
                _const_spec((1, LRU_W)), _const_spec((2, 256, 512), True), _const_spec((1, LRU_W)),
                _const_spec((1, LRU_W)), _const_spec((1, LRU_W)), _const_spec((1, GAINS_W)), pl.BlockSpec(memory_space=pltpu.SMEM),
                _const_spec((4 * MEM_LEN, XATT_W), True), _const_spec((4 * MEM_LEN, XATT_W), True),
                _const_spec((1, D_MODEL)), _const_spec((D_MODEL, D_MODEL), True)]
    out_shape = (jax.ShapeDtypeStruct((seq, D_IN), F32), jax.ShapeDtypeStruct((seq, LRU_W), F32),
                 jax.ShapeDtypeStruct((seq, SWA_W), F32), jax.ShapeDtypeStruct((seq, XATT_W), F32),
                 jax.ShapeDtypeStruct((seq, D_MODEL), _MXU), jax.ShapeDtypeStruct((seq, D_MODEL), _MXU),
                 jax.ShapeDtypeStruct((seq, D_MODEL), _MXU), jax.ShapeDtypeStruct((seq, 4 * 2 * BLOCK), _MXU),
                 jax.ShapeDtypeStruct((seq, 4 * MEM_LEN), _MXU), jax.ShapeDtypeStruct((seq, 128), F32),
                 jax.ShapeDtypeStruct((seq, 4 * LRU_W), _MXU), jax.ShapeDtypeStruct((seq, LRU_W), F32),
                 jax.ShapeDtypeStruct((8, 128), F32))
    out_specs = (rows(D_IN), rows(LRU_W), rows(SWA_W), rows(XATT_W), rows(D_MODEL), rows(D_MODEL), rows(D_MODEL),
                 rows(4 * 2 * BLOCK), rows(4 * MEM_LEN), rows(128), rows(4 * LRU_W), rows(LRU_W),
                 _const_spec((8, 128)))
    scratch = [pltpu.VMEM((tm + 8, LRU_W), F32), pltpu.VMEM((tm, LRU_W), F32),
               pltpu.VMEM((8, LRU_W), F32), pltpu.VMEM((BLOCK, KV_W), F32), pltpu.VMEM((BLOCK, KV_W), F32),
               pltpu.VMEM((8, 128), F32)]
    return pl.pallas_call(
        body, name="layer_fwd", grid=(nt,), out_shape=out_shape, in_specs=in_specs, out_specs=out_specs,
        scratch_shapes=scratch,
        compiler_params=pltpu.CompilerParams(dimension_semantics=("arbitrary",), vmem_limit_bytes=VMEM_LIMIT),
    )(x, tgt, rc, rs1, rs2, ng, win_t, cw, cb, wg, brg, big, lam, gains, sinks, km, vm, og, wout)


def weight_grads(ycat, dout, dproj, xn, g_kv, g_small, early_at, late_at):
    seq, ncol = xn.shape
    blk = 256
    n_out, n_in = ycat.shape[1] // blk, dproj.shape[1] // blk
    assert n_out == N_CHIPS and late_at[0] >= n_out
    g_out = jax.ShapeDtypeStruct((N_CHIPS, 2, blk // 2, dout.shape[1]), F32)
    shape_e, scratch_e = _hosted_reduce_shapes([g_kv], g_small)
    shape_l, scratch_l = _hosted_reduce_shapes([g_out], None)

    def body(l1_ref, r1_ref, l2_ref, r2_ref, kv_ref, sm_ref, o_ref, sum_out, sum_kv, sum_sm, gout_scr, *scratch):
        j = pl.program_id(0)

        def reduce_stages(closing):
            _hosted_reduce(j, n_out + n_in, closing, early_at, (kv_ref, sm_ref), (sum_kv, sum_sm),
                           scratch[:len(scratch_e)], True)
            _hosted_reduce(j, n_out + n_in, closing, late_at, (gout_scr,), (sum_out,), scratch[len(scratch_e):], False)

        reduce_stages(False)

        @pl.when(j < n_out)
        def _():
            gout_scr[j] = _mm_tn(l1_ref[...], r1_ref[...]).reshape(g_out.shape[1:])

        @pl.when(j >= n_out)
        def _():
            o_ref[...] = _mm_tn(l2_ref[...], r2_ref[...])

        reduce_stages(True)

    vm = pl.BlockSpec(memory_space=pltpu.VMEM)
    hbm = pl.BlockSpec(memory_space=pl.ANY)
    return pl.pallas_call(
        body, name="weight_grads", grid=(n_out + n_in,),
        out_shape=(jax.ShapeDtypeStruct((dproj.shape[1], ncol), F32), *shape_l, *shape_e),
        in_specs=[pl.BlockSpec((seq, blk), lambda j: (0, jnp.minimum(j, n_out - 1))), _const_spec(dout.shape, True),
                  pl.BlockSpec((seq, blk), lambda j: (0, jnp.maximum(j - n_out, 0))), _const_spec(xn.shape, True),
                  hbm, vm],
        out_specs=(pl.BlockSpec((blk, ncol), lambda j: (jnp.maximum(j - n_out, 0), 0)), hbm, hbm, hbm),
        scratch_shapes=[pltpu.VMEM(g_out.shape, F32)] + scratch_e + scratch_l,
        compiler_params=pltpu.CompilerParams(dimension_semantics=("arbitrary",), vmem_limit_bytes=VMEM_LIMIT),
    )(ycat, dout, dproj, xn, g_kv, g_small)


def layer_bwd(x, dout, proj, ya, yb, yc, pswa, pmem, psink, gates, a_all, rc, rs1, rs2, ng, win_t, cw, wg, lam, gains,
              km, vm, og, wout):
    seq = x.shape[0]
    tm = min(ROW_TILE, seq)
    nt = seq // tm
    nb = tm // BLOCK

    def body(x_ref, dout_ref, proj_ref, ya_ref, yb_ref, yc_ref, pswa_ref, pmem_ref, psink_ref, gates_ref, a_ref,
             c_ref, s1_ref, s2_ref,
             yah_ref, kvh_ref, ch_ref, s1h_ref, s2h_ref,
             ng_ref, win_ref, cw_ref, wg_ref, lam_ref, gn_ref, km_ref, vm_ref, og_ref, wout_ref,
             gx_ref, dproj_ref, gwg_ref, dkm_ref, dvm_ref, gng_ref, gog_ref, gcb_ref, gbrg_ref, gbig_ref, glam_ref,
             gcw_ref, gqn_ref, gkn_ref, gxqn_ref, gsink_ref,
             hext_ref, aext_ref, an_scr, dh_scr, g_scr, dxc_ext, gcar_ref, dkcar_ref, dvcar_ref):
        i = pl.program_id(0)
        tile = nt - 1 - i
        first_tile = tile == 0

        @pl.when(i == 0)
        def _():
            for r in (gwg_ref, dkm_ref, dvm_ref, gng_ref, gog_ref, gcb_ref, gbrg_ref, gbig_ref, glam_ref, gcw_ref,
                      gqn_ref, gkn_ref, gxqn_ref, gsink_ref, gcar_ref, dkcar_ref, dvcar_ref):
                r[...] = jnp.zeros_like(r)
            dxc_ext[tm:tm + 8, :] = jnp.zeros((8, LRU_W), F32)
            aext_ref[tm:tm + 8, :] = jnp.zeros((8, LRU_W), F32)

        xv = x_ref[...]
        dov = dout_ref[...]
        dz = _mm_nt(dov, wout_ref[...])
        ogv = og_ref[...]

        def group_bwd(y, gate, g, dzg):
            r = lax.rsqrt(_row_mean(y * y) + EPS)
            n = y * r
            sg = _sigmoid(gate)
            dgate = dzg * (n * g) * (sg * (1.0 + gate * (1.0 - sg)))
            dng = dzg * (gate * sg)
            dn = dng * g
            return r * (dn - n * _row_mean(dn * n)), dgate, _col_sum(dng * n)

        dya, dga, goa = group_bwd(ya_ref[...], proj_ref[:, C_LRUG:C_LRUG + LRU_W], ogv[:, :512], dz[:, :512])
        dyb, dgb, gob = group_bwd(yb_ref[...], proj_ref[:, C_SWAG:C_SWAG + SWA_W], ogv[:, 512:768], dz[:, 512:768])
        dyc, dgc, goc = group_bwd(yc_ref[...], proj_ref[:, C_XG:C_XG + XATT_W], ogv[:, 768:], dz[:, 768:])
        gog_ref[...] += jnp.concatenate([goa, gob, goc], axis=1)
        dproj_ref[:, C_LRUG:C_LRUG + LRU_W] = dga.astype(dproj_ref.dtype)
        dproj_ref[:, C_SWAG:C_SWAG + SWA_W] = dgb.astype(dproj_ref.dtype)
        dproj_ref[:, C_XG:C_XG + XATT_W] = dgc.astype(dproj_ref.dtype)

        gm256 = _group_matrix(XATT_W)
        xq = proj_ref[:, C_XQ:C_XQ + XATT_W]
        rq = lax.rsqrt(_seg_mean(xq * xq, gm256) + EPS)
        qn = xq * rq
        qx = qn * gn_ref[:, G_XQ:G_XK]
        qxb = qx.astype(_MXU)
        dycb = dyc.astype(_MXU)
        dp_all = _mm_nt(dycb, vm_ref[...])
        dsm = []
        for j in range(4):
            pj = pmem_ref[:, MEM_LEN * j:MEM_LEN * (j + 1)].astype(F32)
            dp = dp_all[:, MEM_LEN * j:MEM_LEN * (j + 1)]
            dsm.append((pj * (dp - jnp.sum(pj * dp, axis=-1, keepdims=True))).astype(_MXU))
        ds_all = jnp.concatenate(dsm, axis=1)
        dvm_ref[...] += _mm_tn(dycb, pmem_ref[...])
        dkm_ref[...] += _mm_tn(qxb, ds_all)
        dqx = _mm(ds_all, km_ref[...])
        gxqn_ref[...] += _col_sum(dqx * qn)
        dqn = dqx * gn_ref[:, G_XQ:G_XK]
        dproj_ref[:, C_XQ:C_XQ + XATT_W] = (rq * (dqn - qn * _seg_mean(dqn * qn, gm256))).astype(dproj_ref.dtype)

        gm128 = _group_matrix(KV_W)
        cv, s1v, s2v = c_ref[...], s1_ref[...], s2_ref[...]

        def head_norm(t):
            r = lax.rsqrt(_seg_mean(t * t, gm128) + EPS)
            return t * r, r

        qn_, qr_ = zip(head_norm(proj_ref[:, C_SQ:C_SQ + 128]), head_norm(proj_ref[:, C_SQ + 128:C_SQ + 256]))
        qrope = [_rope(qn_[h] * gn_ref[:, G_Q:G_K], cv, s1v, s2v).astype(_MXU) for h in range(2)]
        kn, krr = head_norm(proj_ref[:, C_SK:C_SK + KV_W])
        kr = _rope(kn * gn_ref[:, G_K:G_XQ], cv, s1v, s2v)
        khn, _ = head_norm(kvh_ref[:, 0:KV_W])
        khr = _rope(khn * gn_ref[:, G_K:G_XQ], ch_ref[...], s1h_ref[...], s2h_ref[...])
        ka = _place_kv(jnp.concatenate([khr, kr], axis=0), 0.125)
        va = _place_kv(jnp.concatenate([kvh_ref[:, KV_W:2 * KV_W], proj_ref[:, C_SV:C_SV + KV_W]], axis=0), 1.0)
        lane128 = lax.broadcasted_iota(jnp.int32, (1, 128), 1)
        gsink = jnp.zeros((1, 128), F32)
        dk_band, dv_band, dq_blk = [], [], []
        for b in range(nb):
            band = slice(BLOCK * b, BLOCK * b + 2 * BLOCK)
            blk = slice(BLOCK * b, BLOCK * (b + 1))
            dka, dva, dsb = [], [], []
            deltas = jnp.zeros((BLOCK, 128), F32)
            for j in range(4):
                qh = qrope[j // 2][blk]
                doh = dyb[blk, KV_W * (j // 2):KV_W * (j // 2 + 1)].astype(_MXU)
                pb = pswa_ref[blk, 2 * BLOCK * j:2 * BLOCK * (j + 1)]
                p = pb.astype(F32)
                dp = _mm_nt(doh, va[j][band])
                delta = jnp.sum(p * dp, axis=-1, keepdims=True)
                ds = (p * (dp - delta)).astype(_MXU)
                deltas = jnp.where(lane128 == j, delta, deltas)
                dva.append(_mm_tn(pb, doh))
                dka.append(_mm_tn(ds, qh))
                dsb.append(ds)
            gsink = gsink - _col_sum(psink_ref[blk, :] * deltas)
            dk_band.append(_unplace_kv(dka) * 0.125)
            dv_band.append(_unplace_kv(dva))
            dq_blk.append([_mm(jnp.concatenate(dsb[2 * h:2 * h + 2], axis=1),
                               jnp.concatenate([ka[2 * h][band], ka[2 * h + 1][band]], axis=0)) for h in range(2)])
        gsink_ref[...] += gsink
        dk_rows = [dk_band[b][BLOCK:] + (dk_band[b + 1][:BLOCK] if b + 1 < nb else dkcar_ref[...]) for b in range(nb)]
        dv_rows = [dv_band[b][BLOCK:] + (dv_band[b + 1][:BLOCK] if b + 1 < nb else dvcar_ref[...]) for b in range(nb)]
        dkcar_ref[...] = dk_band[0][:BLOCK]
        dvcar_ref[...] = dv_band[0][:BLOCK]
        dkg = _rope_bwd(jnp.concatenate(dk_rows, axis=0), cv, s1v, s2v)
        gkn = _col_sum(dkg * kn)
        dkn = dkg * gn_ref[:, G_K:G_XQ]
        dproj_ref[:, C_SK:C_SK + KV_W] = (krr * (dkn - kn * _seg_mean(dkn * kn, gm128))).astype(dproj_ref.dtype)
        dproj_ref[:, C_SV:C_SV + KV_W] = jnp.concatenate(dv_rows, axis=0).astype(dproj_ref.dtype)
        gqn = jnp.zeros((1, 128), F32)
        for h in range(2):
            dqg = _rope_bwd(jnp.concatenate([dq_blk[b][h] for b in range(nb)], axis=0), cv, s1v, s2v)
            gqn = gqn + _col_sum(dqg * qn_[h])
            dqn_ = dqg * gn_ref[:, G_Q:G_K]
            dproj_ref[:, C_SQ + 128 * h:C_SQ + 128 * (h + 1)] = (
                qr_[h] * (dqn_ - qn_[h] * _seg_mean(dqn_ * qn_[h], gm128))).astype(dproj_ref.dtype)
        gqn_ref[...] += gqn
        gkn_ref[...] += gkn

        u = proj_ref[:, C_LRUX:C_LRUX + LRU_W]
        xc, rg, ig, sq = (gates_ref[:, LRU_W * k:LRU_W * (k + 1)].astype(F32) for k in range(4))
        a = a_ref[...]
        sp = _softplus(-lam_ref[...])
        hext_ref[0:8, :] = jnp.where(first_tile, 0.0, yah_ref[...])
        hext_ref[8:8 + tm, :] = ya_ref[...]
        hprev = hext_ref[pl.ds(7, tm), :]
        aext_ref[0:tm, :] = a
        an_scr[...] = aext_ref[pl.ds(1, tm), :]
        dh_scr[...] = dya
        dh_scr[tm - 1:tm, :] = dh_scr[tm - 1:tm, :] + gcar_ref[0:1, :]
        row8 = lax.broadcasted_iota(jnp.int32, (8, LRU_W), 0)

        def scan_step(gi, carry):
            r0 = pl.multiple_of((tm // 8 - 1 - gi) * 8, 8)
            av = an_scr[pl.ds(r0, 8), :]
            bv = dh_scr[pl.ds(r0, 8), :]
            for d in (1, 2, 4):
                a_sh = jnp.where(row8 < 8 - d, pltpu.roll(av, 8 - d, 0), 1.0)
                b_sh = jnp.where(row8 < 8 - d, pltpu.roll(bv, 8 - d, 0), 0.0)
                bv = bv + av * b_sh
                av = av * a_sh
            gv = bv + av * carry
            g_scr[pl.ds(r0, 8), :] = gv
            return gv[0:1, :]

        g0 = lax.fori_loop(0, tm // 8, scan_step, jnp.zeros((1, LRU_W), F32), unroll=True)
        gcar_ref[0:1, :] = a[0:1, :] * g0
        gv = g_scr[...]
        da = gv * hprev
        dig = gv * sq * xc
        dxc = gv * sq * ig
        dla = da * a - gv * (ig * xc) * ((a * a) / sq)
        drg = dla * ((-LRU_C) * sp)
        glam_ref[...] += _col_sum(dla * rg)
        dpr = drg * rg * (1.0 - rg)
        dpi = dig * ig * (1.0 - ig)
        gbrg_ref[...] += _col_sum(dpr)
        gbig_ref[...] += _col_sum(dpi)
        dpre0 = jnp.concatenate([dpr[:, :256], dpi[:, :256]], axis=1).astype(_MXU)
        dpre1 = jnp.concatenate([dpr[:, 256:], dpi[:, 256:]], axis=1).astype(_MXU)
        gwg_ref[0] += _mm_tn(xc[:, :256], dpre0)
        gwg_ref[1] += _mm_tn(xc[:, 256:], dpre1)
        dxc = dxc + jnp.concatenate([_mm_nt(dpre0, wg_ref[0]), _mm_nt(dpre1, wg_ref[1])], axis=1)
        gcb_ref[...] += _col_sum(dxc)
        dxc_ext[0:tm, :] = dxc
        du = jnp.zeros((tm, LRU_W), F32)
        for k in range(CONV_K):
            later = dxc_ext[pl.ds(3 - k, tm), :]
            gcw_ref[k:k + 1, :] += _col_sum(later * u)
            du = du + cw_ref[k:k + 1, :] * later
        dxc_ext[tm:tm + 8, :] = dxc[0:8, :]
        dproj_ref[:, C_LRUX:C_LRUX + LRU_W] = du.astype(dproj_ref.dtype)

        dxn = _mm(dproj_ref[...], win_ref[...])
        rx = lax.rsqrt(_row_mean(xv * xv) + EPS)
        xh = xv * rx
        gng_ref[...] += _col_sum(dxn * xh)
        dxh = dxn * ng_ref[...]
        gx_ref[...] = dov.astype(F32) + rx * (dxh - xh * _row_mean(dxh * xh))

        @pl.when(i == nt - 1)
        def _():
            glam_ref[...] = glam_ref[...] * (LRU_C * _sigmoid(-lam_ref[...]))
            for r in (gqn_ref, gkn_ref, gxqn_ref):
                r[...] = _fold_heads(r[...])

    def rows(ncol, arr_cols_block=0):
        return pl.BlockSpec((tm, ncol), lambda i: (nt - 1 - i, arr_cols_block))

    def halo(nrow, ncol, colblk=0):
        per = tm // nrow
        return pl.BlockSpec((nrow, ncol), lambda i: (jnp.maximum((nt - 1 - i) * per - 1, 0), colblk))

    in_specs = [rows(D_MODEL), rows(D_MODEL), rows(D_IN), rows(LRU_W), rows(SWA_W), rows(XATT_W),
                rows(4 * 2 * BLOCK), rows(4 * MEM_LEN), rows(128), rows(4 * LRU_W), rows(LRU_W),
                rows(128), rows(128), rows(128),
                halo(8, LRU_W), halo(BLOCK, 2 * KV_W, C_SK // (2 * KV_W)),
                halo(BLOCK, 128), halo(BLOCK, 128), halo(BLOCK, 128),
                _const_spec((1, D_MODEL)), _const_spec((D_IN, D_MODEL), True), _const_spec((CONV_K, LRU_W)),
                _const_spec((2, 256, 512), True), _const_spec((1, LRU_W)), _const_spec((1, GAINS_W)),
                _const_spec((4 * MEM_LEN, XATT_W), True), _const_spec((4 * MEM_LEN, XATT_W), True),
                _const_spec((1, D_MODEL)), _const_spec((D_MODEL, D_MODEL), True)]
    small = [(2, 256, 512), (XATT_W, 4 * MEM_LEN), (XATT_W, 4 * MEM_LEN), (1, D_MODEL), (1, D_MODEL), (1, LRU_W), (1, LRU_W),
             (1, LRU_W), (1, LRU_W), (CONV_K, LRU_W), (1, 128), (1, 128), (1, XATT_W), (1, 128)]
    out_shape = (jax.ShapeDtypeStruct((seq, D_MODEL), F32), jax.ShapeDtypeStruct((seq, D_IN), _MXU)) + tuple(
        jax.ShapeDtypeStruct(s, F32) for s in small)
    out_specs = (rows(D_MODEL), rows(D_IN)) + tuple(_const_spec(s) for s in small)
    scratch = [pltpu.VMEM((tm + 8, LRU_W), F32), pltpu.VMEM((tm + 8, LRU_W), F32),
               pltpu.VMEM((tm, LRU_W), F32), pltpu.VMEM((tm, LRU_W), F32), pltpu.VMEM((tm, LRU_W), F32),
               pltpu.VMEM((tm + 8, LRU_W), F32),
               pltpu.VMEM((8, LRU_W), F32), pltpu.VMEM((BLOCK, KV_W), F32), pltpu.VMEM((BLOCK, KV_W), F32)]
    return pl.pallas_call(
        body, name="layer_bwd", grid=(nt,), out_shape=out_shape, in_specs=in_specs, out_specs=out_specs,
        scratch_shapes=scratch,
        compiler_params=pltpu.CompilerParams(dimension_semantics=("arbitrary",), vmem_limit_bytes=VMEM_LIMIT),
    )(x, dout, proj, ya, yb, yc, pswa, pmem, psink, gates, a_all, rc, rs1, rs2, ya, proj, rc, rs1, rs2,
      ng, win_t, cw, wg, lam, gains, km, vm, og, wout)


def _reduce_protocol(big, sm, outs, osm, r1, r1s, wire, r2, r2s, wire2, ps, own, send, recv, lsem):
    nbig = len(big)
    x, y, c = lax.axis_index("x"), lax.axis_index("y"), lax.axis_index("c")
    sibling = (x, y, 1 - c)
    near, far, diag = _partners(x, y, c)
    me, near_id, far_id, diag_id = _chip_of(x, y), _chip_of(*near), _chip_of(*far), _chip_of(*diag)

    def copy(k, src, dst, to):
        return pltpu.make_async_remote_copy(src_ref=src, dst_ref=dst, send_sem=send.at[k], recv_sem=recv.at[k],
                                            device_id=to, device_id_type=MESH)

    def sent(stage, a):
        if a == nbig:
            src, dst, to = ((sm.at[1 - c], r1s, sibling), (r1s, r2s.at[0], (*near, c)), (ps, r2s.at[1], (*far, c)),
                            (osm.at[c], osm.at[c], sibling))[stage]
            return [copy(5 * nbig + stage, src, dst, to)]
        if stage == 0:
            return [copy(5 * a, big[a].at[:, 1 - c], r1[a], sibling)]
        if stage == 1:
            return [copy(5 * a + 1, wire[a].at[near_id], r2[a].at[0], (*near, c)),
                    copy(5 * a + 2, wire[a].at[diag_id], r2[a].at[1], (*near, c))]
        if stage == 2:
            return [copy(5 * a + 3, wire2[a], r2[a].at[2], (*far, c))]
        return [copy(5 * a + 4, outs[a].at[c], outs[a].at[c], sibling)]

    arrays = range(nbig + (sm is not None))

    def start(stage, a):
        for cp in sent(stage, a):
            cp.start()

    def arrived(k, ref):
        copy(k, ref, ref, sibling).wait_recv()

    def loads():
        return [pltpu.make_async_copy(big[a].at[:, c], own[a], lsem.at[a]) for a in range(nbig)]

    def stage0():
        for a in arrays:
            start(0, a)
        for cp in loads():
            cp.start()

    def stage1():
        for a in range(nbig):
            loads()[a].wait()
            arrived(5 * a, r1[a])
            for k in range(N_CHIPS):
                r1[a][k] = own[a][k] + r1[a][k]
                wire[a][k] = r1[a][k].astype(wire[a].dtype)
            start(1, a)
        if sm is not None:
            arrived(5 * nbig, r1s)
            r1s[...] = sm[c] + r1s[...]
            start(1, nbig)

    def stage2():
        for a in range(nbig):
            arrived(5 * a + 1, r2[a].at[0])
            arrived(5 * a + 2, r2[a].at[1])
            r1[a][me] = r1[a][me] + r2[a][0].astype(F32)
            wire2[a][...] = (r1[a][far_id] + r2[a][1].astype(F32)).astype(wire2[a].dtype)
            start(2, a)
        if sm is not None:
            arrived(5 * nbig + 1, r2s.at[0])
            ps[...] = r1s[...] + r2s[0]
            start(2, nbig)

    def stage3():
        for a in range(nbig):
            arrived(5 * a + 3, r2[a].at[2])
            outs[a][c] = r1[a][me] + r2[a][2].astype(F32)
            start(3, a)
        if sm is not None:
            arrived(5 * nbig + 2, r2s.at[1])
            osm[c] = ps[...] + r2s[1]
            start(3, nbig)

    def stage4():
        for a in range(nbig):
            arrived(5 * a + 4, outs[a].at[1 - c])
        if sm is not None:
            arrived(5 * nbig + 3, osm.at[1 - c])
        for stage in range(4):
            for a in arrays:
                for cp in sent(stage, a):
                    cp.wait_send()

    return [stage0, stage1, stage2, stage3, stage4]


def _reduce_buffers(bigs, g_small):
    half = [b.shape[2:] for b in bigs]
    sm_half = None if g_small is None else g_small.shape[1:]
    out_shape = [jax.ShapeDtypeStruct((2,) + h, F32) for h in half]
    small = lambda lead: [] if g_small is None else [pltpu.VMEM(lead + sm_half, F32)]
    if g_small is not None:
        out_shape.append(jax.ShapeDtypeStruct(g_small.shape, F32))
    n_sem = 5 * len(bigs) + 4
    scratch = ([pltpu.VMEM((N_CHIPS,) + h, F32) for h in half] + small(())
               + [pltpu.VMEM((N_CHIPS,) + h, _WIRE) for h in half]
               + [pltpu.VMEM((3,) + h, _WIRE) for h in half] + small((2,))
               + [pltpu.VMEM(h, _WIRE) for h in half] + small(())
               + [pltpu.VMEM((N_CHIPS,) + h, F32) for h in half]
               + [pltpu.SemaphoreType.DMA((n_sem,)), pltpu.SemaphoreType.DMA((n_sem,)),
                  pltpu.SemaphoreType.DMA((len(bigs),))])
    return out_shape, scratch


def _split_reduce_refs(refs, nbig, has_small):
    it = iter(refs)
    take = lambda n: [next(it) for _ in range(n)]
    one = lambda: next(it) if has_small else None
    big, sm = take(nbig), one()
    outs, osm = take(nbig), one()
    r1, r1s, wire, r2, r2s, wire2, ps, own = take(nbig), one(), take(nbig), take(nbig), one(), take(nbig), one(), take(nbig)
    send, recv, lsem = take(3)
    return big, sm, outs, osm, r1, r1s, wire, r2, r2s, wire2, ps, own, send, recv, lsem


def _hosted_reduce_shapes(bigs, g_small):
    red_shape, scratch = _reduce_buffers(bigs, g_small)
    nres = len(red_shape)
    return red_shape, [pltpu.VMEM(r.shape, r.dtype) for r in red_shape] + scratch + [pltpu.SemaphoreType.DMA((nres,))]


def _hosted_reduce(step, n_steps, closing, stage_at, operands, results, scratch, has_small):
    nres = len(results)
    sums, rest, fsem = scratch[:nres], scratch[nres:-1], scratch[-1]
    refs = tuple(operands) + tuple(sums) + tuple(rest)

    def to_results():
        out = [pltpu.make_async_copy(sums[k], results[k], fsem.at[k]) for k in range(nres)]
        for cp in out:
            cp.start()
        for cp in out:
            cp.wait()

    stages = _reduce_protocol(*_split_reduce_refs(refs, nres - has_small, has_small))

    def last_stage():
        stages[-1]()
        to_results()

    for at, stage in zip(stage_at, stages[:-1] + [last_stage]):
        if closing == (at == n_steps):
            pl.when(step == min(at, n_steps - 1))(stage)


def reduce_grads(big, name, parts):
    chips, halves, rows_, cols = big.shape
    sub = jax.ShapeDtypeStruct((chips, halves, rows_ // parts, cols), big.dtype)

    def body(b_ref, o_ref, *scratch):
        refs = [b_ref.at[:, :, s] for s in range(parts)] + [o_ref.at[:, s] for s in range(parts)] + list(scratch)
        for stage in _reduce_protocol(*_split_reduce_refs(refs, parts, False)):
            stage()

    _, scratch = _reduce_buffers([sub] * parts, None)
    return pl.pallas_call(
        body, name=name, out_shape=jax.ShapeDtypeStruct((halves, parts, rows_ // parts, cols), F32),
        in_specs=[pl.BlockSpec(memory_space=pl.ANY)], out_specs=pl.BlockSpec(memory_space=pltpu.VMEM),
        scratch_shapes=scratch, compiler_params=pltpu.CompilerParams(vmem_limit_bytes=VMEM_LIMIT),
    )(big.reshape(chips, halves, parts, rows_ // parts, cols))


def adamw_matrices(items):
    plan, total = [], 0
    for w, _, _, _ in items:
        rows_, cols = w.shape
        tr = max(t for t in range(8, rows_ + 1, 8) if rows_ % t == 0 and t * cols * 4 <= ADAM_BLOCK_BYTES)
        plan.append((total, rows_ // tr, tr, cols))
        total += rows_ // tr
    nin = 4 * len(items)

    def body(*refs):
        i = pl.program_id(0)
        for k, (first, steps, _, _) in enumerate(plan):
            w_ref, g_ref, m_ref, v_ref = refs[4 * k:4 * k + 4]
            go_ref, d_ref, nm_ref, nv_ref = refs[nin + 4 * k:nin + 4 * k + 4]

            @pl.when((i >= first) & (i < first + steps))
            def _():
                gv = g_ref[...]
                go_ref[...] = gv
                d_ref[...], nm_ref[...], nv_ref[...] = _adam_update(w_ref[...], gv, m_ref[...], v_ref[...])

    specs, shapes = [], []
    for (first, steps, tr, cols), (w, _, _, _) in zip(plan, items):
        spec = pl.BlockSpec((tr, cols), lambda i, first=first, steps=steps: (jnp.clip(i - first, 0, steps - 1), 0))
        specs += [spec] * 4
        shapes += [jax.ShapeDtypeStruct(w.shape, F32)] * 4
    res = pl.pallas_call(
        body, name="adamw_matrices", grid=(total,), out_shape=tuple(shapes), in_specs=specs, out_specs=tuple(specs),
        compiler_params=pltpu.CompilerParams(dimension_semantics=("arbitrary",)),
    )(*[a for item in items for a in item])
    return [res[4 * k:4 * k + 4] for k in range(len(items))]


def _adam_update(w, g, m, v):
    nm = ADAM_B1 * m + (1.0 - ADAM_B1) * g
    nv = ADAM_B2 * v + (1.0 - ADAM_B2) * (g * g)
    m_hat = nm / (1.0 - ADAM_B1 ** ADAM_STEP)
    v_hat = nv / (1.0 - ADAM_B2 ** ADAM_STEP)
    return (-ADAM_LR) * (m_hat / (jnp.sqrt(v_hat) + ADAM_EPS) + ADAM_WD * w), nm, nv


def adamw_vectors(g_pack, ws, ms, vs):
    nvec = len(SMALL_VECTORS)
    n = nvec + len(SMALL_MATRICES)

    def body(*refs):
        pk = refs[0]
        w_refs, m_refs, v_refs = (refs[1 + k * n:1 + (k + 1) * n] for k in range(3))
        g_out, d_out, nm_out, nv_out = (refs[1 + (3 + k) * n:1 + (4 + k) * n] for k in range(4))
        refs[-1][...] = pk[LOSS_ROW:LOSS_ROW + 1, 0:1]
        chip = 2 * lax.axis_index("x") + lax.axis_index("y")
        for k, (name, row, width) in enumerate(SMALL_VECTORS):
            if name == "conv_w":
                g = jnp.concatenate([pk[pl.ds(row + 4 * t + chip, 1), :] for t in range(CONV_K)], axis=0)[None]
            elif width >= 128:
                g = jnp.concatenate([pk[row + r:row + r + 1, :] for r in range(width // 128)], axis=1)
            else:
                g = pk[row:row + 1, 0:width]
            g_out[k][...] = g
            d_out[k][...], nm_out[k][...], nv_out[k][...] = _adam_update(w_refs[k][...], g, m_refs[k][...], v_refs[k][...])
        for k in range(nvec, n):
            for b in range(LRU_BLOCKS):
                rows_ = pk[GATES_ROW + HEAD * b:GATES_ROW + HEAD * (b + 1), :]
                g = (pltpu.roll(rows_, HEAD, axis=1) if k > nvec else rows_)[:, 0:HEAD]
                g_out[k][0, b] = g
                d_out[k][0, b], nm_out[k][0, b], nv_out[k][0, b] = _adam_update(
                    w_refs[k][0, b], g, m_refs[k][0, b], v_refs[k][0, b])

    vm = pl.BlockSpec(memory_space=pltpu.VMEM)
    like = [jax.ShapeDtypeStruct(w.shape, F32) for w in ws]
    return pl.pallas_call(
        body, name="adamw_vectors", out_shape=(*like * 4, jax.ShapeDtypeStruct((1, 1), F32)),
        in_specs=[vm] * (1 + 3 * n), out_specs=(vm,) * (4 * n + 1),
    )(g_pack, *ws, *ms, *vs)


SMALL_VECTORS = (("norm_g", 0, 1024), ("mem_norm_g", 8, 1024), ("conv_w", 16, 512), ("conv_b", 32, 512),
                 ("b_rg", 36, 512), ("b_ig", 40, 512), ("lru_lambda", 44, 512), ("q_norm_g", 48, 64),
                 ("k_norm_g", 49, 64), ("sinks", 50, 4), ("xq_norm_g", 51, 64), ("xk_norm_g", 52, 64),
                 ("out_norm_g", 53, 1024))
LOSS_ROW = 61
SMALL_MATRICES = ("w_rg", "w_ig")
GATES_ROW = 64
SMALL_ROWS = GATES_ROW + LRU_BLOCKS * HEAD


def _rope_tables(seq):
    pos = np.arange(seq, dtype=np.float32)
    inv_freq = (np.float32(ROPE_THETA) ** (-(np.arange(0, ROPE_DIM, 2, dtype=np.float32) / np.float32(ROPE_DIM)))
                ).astype(np.float32)
    ang = (pos[:, None] * inv_freq[None, :]).astype(np.float32)
    cos, sin = np.cos(ang).astype(np.float32), np.sin(ang).astype(np.float32)
    z = lambda n: np.zeros((seq, n), np.float32)
    c64 = np.concatenate([cos, cos, np.ones((seq, HEAD - ROPE_DIM), np.float32)], axis=1)
    s1_64 = np.concatenate([-sin, z(HEAD - 8)], axis=1)
    s2_64 = np.concatenate([z(8), sin, z(HEAD - ROPE_DIM)], axis=1)
    return tuple(jnp.asarray(np.concatenate([t, t], axis=1)) for t in (c64, s1_64, s2_64))


def kernel(x, mem, norm_g, mem_norm_g, w_in, conv_w, conv_b, w_rg, b_rg, w_ig, b_ig, lru_lambda, q_norm_g, k_norm_g, sinks, w_mem_kv, xq_norm_g, xk_norm_g, out_norm_g, w_out, loss_target, m_norm_g, m_mem_norm_g, m_w_in, m_conv_w, m_conv_b, m_w_rg, m_b_rg, m_w_ig, m_b_ig, m_lru_lambda, m_q_norm_g, m_k_norm_g, m_sinks, m_w_mem_kv, m_xq_norm_g, m_xk_norm_g, m_out_norm_g, m_w_out, v_norm_g, v_mem_norm_g, v_w_in, v_conv_w, v_conv_b, v_w_rg, v_b_rg, v_w_ig, v_b_ig, v_lru_lambda, v_q_norm_g, v_k_norm_g, v_sinks, v_w_mem_kv, v_xq_norm_g, v_xk_norm_g, v_out_norm_g, v_w_out):
    seq = x.shape[1]
    xs, tgt, mems = x[0], loss_target[0], mem[0]

    win_t, wout, wkv, cw, wg, gains = gather_weights(w_in[0].T, w_out[0], w_mem_kv[0], conv_w, w_rg, w_ig,
                                                     (q_norm_g, k_norm_g, xq_norm_g, xk_norm_g))
    rc, rs1, rs2 = _rope_tables(seq)

    km, vm = mem_fwd(mems, mem_norm_g, wkv, gains)
    proj, ya, yb, yc, ycat, xn, dout, pswa, pmem, psink, gates, a_all, loss8 = layer_fwd(
        xs, tgt, rc, rs1, rs2, norm_g, win_t, cw, conv_b, wg, b_rg, b_ig, lru_lambda, gains, sinks, km, vm,
        out_norm_g, wout)
    (gx, dproj, g_wg, dkm, dvm, g_ng, g_og, g_cb, g_brg, g_big, g_lam, g_cw, g_qn, g_kn, g_xqn, g_sink) = layer_bwd(
        xs, dout, proj, ya, yb, yc, pswa, pmem, psink, gates, a_all, rc, rs1, rs2, norm_g, win_t, cw, wg, lru_lambda,
        gains, km, vm, out_norm_g, wout)
    g_wkv, small_g = mem_bwd(mems, mem_norm_g, wkv, gains, dkm, dvm, g_wg, loss8, dict(
        norm_g=g_ng, conv_w=g_cw, conv_b=g_cb, b_rg=g_brg, b_ig=g_big, lru_lambda=g_lam, q_norm_g=g_qn, k_norm_g=g_kn,
        sinks=g_sink, xq_norm_g=g_xqn, out_norm_g=g_og))
    g_win_t, r_out, r_kv, r_small = weight_grads(
        ycat, dout, dproj, xn, g_wkv.reshape(N_CHIPS, 2, D_MODEL // 8, 2 * XATT_W),
        small_g.reshape(2, SMALL_ROWS // 2, 128), (0, 1, 4, 7, 8), (4, 5, 10, 12, 13))
    r_in = reduce_grads(g_win_t.reshape(N_CHIPS, 2, D_IN // 8, D_MODEL), "reduce_w_in", 6)

    r_small = r_small.reshape(SMALL_ROWS, 128)
    grads = {}
    weights = dict(norm_g=norm_g, mem_norm_g=mem_norm_g, w_in=w_in, conv_w=conv_w, conv_b=conv_b, w_rg=w_rg, b_rg=b_rg,
                   w_ig=w_ig, b_ig=b_ig, lru_lambda=lru_lambda, q_norm_g=q_norm_g, k_norm_g=k_norm_g, sinks=sinks,
                   w_mem_kv=w_mem_kv, xq_norm_g=xq_norm_g, xk_norm_g=xk_norm_g, out_norm_g=out_norm_g, w_out=w_out)
    ms = dict(norm_g=m_norm_g, mem_norm_g=m_mem_norm_g, w_in=m_w_in, conv_w=m_conv_w, conv_b=m_conv_b, w_rg=m_w_rg,
              b_rg=m_b_rg, w_ig=m_w_ig, b_ig=m_b_ig, lru_lambda=m_lru_lambda, q_norm_g=m_q_norm_g, k_norm_g=m_k_norm_g,
              sinks=m_sinks, w_mem_kv=m_w_mem_kv, xq_norm_g=m_xq_norm_g, xk_norm_g=m_xk_norm_g,
              out_norm_g=m_out_norm_g, w_out=m_w_out)
    vs = dict(norm_g=v_norm_g, mem_norm_g=v_mem_norm_g, w_in=v_w_in, conv_w=v_conv_w, conv_b=v_conv_b, w_rg=v_w_rg,
              b_rg=v_b_rg, w_ig=v_w_ig, b_ig=v_b_ig, lru_lambda=v_lru_lambda, q_norm_g=v_q_norm_g, k_norm_g=v_k_norm_g,
              sinks=v_sinks, w_mem_kv=v_w_mem_kv, xq_norm_g=v_xq_norm_g, xk_norm_g=v_xk_norm_g,
              out_norm_g=v_out_norm_g, w_out=v_w_out)

    delta, new_m, new_v = {}, {}, {}
    res_in, res_out, res_kv = adamw_matrices([
        (w_in[0].T, r_in.reshape(D_IN // 4, D_MODEL), m_w_in[0].T, v_w_in[0].T),
        (w_out[0], r_out.reshape(D_MODEL // 4, D_MODEL), m_w_out[0], v_w_out[0]),
        (w_mem_kv[0], r_kv.reshape(D_MODEL // 4, 2 * XATT_W), m_w_mem_kv[0], v_w_mem_kv[0])])
    grads["w_in"], delta["w_in"], new_m["w_in"], new_v["w_in"] = (r.T[None] for r in res_in)
    grads["w_out"], delta["w_out"], new_m["w_out"], new_v["w_out"] = (r[None] for r in res_out)
    grads["w_mem_kv"], delta["w_mem_kv"], new_m["w_mem_kv"], new_v["w_mem_kv"] = (r[None] for r in res_kv)
    small_names = [n for n, _, _ in SMALL_VECTORS] + list(SMALL_MATRICES)
    res = adamw_vectors(r_small, [weights[n] for n in small_names], [ms[n] for n in small_names],
                        [vs[n] for n in small_names])
    nall = len(small_names)
    for k, into in enumerate((grads, delta, new_m, new_v)):
        into.update(zip(small_names, res[k * nall:(k + 1) * nall]))
    loss = res[-1].reshape(())

    order = ("norm_g", "mem_norm_g", "w_in", "conv_w", "conv_b", "w_rg", "b_rg", "w_ig", "b_ig", "lru_lambda",
             "q_norm_g", "k_norm_g", "sinks", "w_mem_kv", "xq_norm_g", "xk_norm_g", "out_norm_g", "w_out")
    return (loss, gx[None], *[grads[n] for n in order], *[delta[n] for n in order], *[new_m[n] for n in order],
            *[new_v[n] for n in order])
```

```python
import jax
import jax.numpy as jnp
import numpy as np
from jax import lax
from jax.experimental import pallas as pl
from jax.experimental.pallas import tpu as pltpu

F32 = jnp.float32
_MXU = jnp.bfloat16
_WIRE = jnp.bfloat16

D_MODEL = 1024
MEM_LEN = 256
HEAD = 64
LRU_W = 512
LRU_BLOCKS = 8
CONV_K = 4
LRU_C = 8.0
SWA_W = 256
KV_W = 128
XATT_W = 256
BLOCK = 128
D_IN = 2304
ROPE_THETA = 500000.0
ROPE_DIM = 16
EPS = 1e-6
NEG_INF = -1e30
C_LRUX, C_LRUG, C_SQ, C_SK, C_SV, C_SWAG, C_XQ, C_XG = 0, 512, 1024, 1280, 1408, 1536, 1792, 2048
G_Q, G_K, G_XQ, G_XK, GAINS_W = 0, 128, 256, 512, 768

ADAM_LR, ADAM_B1, ADAM_B2, ADAM_EPS, ADAM_WD, ADAM_STEP = 0.001, 0.9, 0.999, 1e-08, 0.01, 10

N_CHIPS = 4
ROW_TILE = 256
VMEM_LIMIT = 56 * 1024 * 1024
ADAM_BLOCK_BYTES = 640 * 1024
MESH = pl.DeviceIdType.MESH


def _mm(a, b):
    return jnp.dot(a.astype(_MXU), b.astype(_MXU), preferred_element_type=F32)


def _mm_nt(a, b):
    return lax.dot_general(a.astype(_MXU), b.astype(_MXU), (((1,), (1,)), ((), ())), preferred_element_type=F32)


def _mm_tn(a, b):
    return lax.dot_general(a.astype(_MXU), b.astype(_MXU), (((0,), (0,)), ((), ())), preferred_element_type=F32)


def _group_matrix(width):
    r = lax.shift_right_logical(lax.broadcasted_iota(jnp.int32, (width, width), 0), 6)
    c = lax.shift_right_logical(lax.broadcasted_iota(jnp.int32, (width, width), 1), 6)
    return (r == c).astype(_MXU)


def _seg_mean(x, gm):
    return jnp.dot(x.astype(_MXU), gm, preferred_element_type=F32) * (1.0 / HEAD)


def _row_mean(x):
    return jnp.mean(x, axis=-1, keepdims=True)


def _col_sum(x):
    return jnp.sum(x, axis=0, keepdims=True)


def _sigmoid(x):
    return jax.nn.sigmoid(x)


def _softplus(z):
    e = jnp.exp(-jnp.abs(z))
    u = 1.0 + e
    log1p_e = jnp.where(u == 1.0, e, jnp.log(u) * (e / (u - 1.0)))
    return jnp.maximum(z, 0.0) + log1p_e


def _rope(t, c, s1, s2):
    return t * c + pltpu.roll(t, 120, 1) * s1 + pltpu.roll(t, 8, 1) * s2


def _rope_bwd(d, c, s1, s2):
    return d * c + pltpu.roll(d * s1, 8, 1) + pltpu.roll(d * s2, 120, 1)


def _fold_heads(v):
    out = v
    for k in range(1, v.shape[1] // HEAD):
        out = out + pltpu.roll(v, HEAD * k, 1)
    return out


def _lane_mask(width, lo, hi):
    lane = lax.broadcasted_iota(jnp.int32, (1, width), 1)
    return ((lane >= lo) & (lane < hi)).astype(F32)


def _swa_mask(first_block):
    qi = lax.broadcasted_iota(jnp.int32, (BLOCK, 2 * BLOCK), 0)
    kj = lax.broadcasted_iota(jnp.int32, (BLOCK, 2 * BLOCK), 1)
    rel = qi + BLOCK - kj
    ok = (rel >= 0) & (rel < BLOCK)
    return ok & (jnp.logical_not(first_block) | (kj >= BLOCK))


def _place_kv(t, scale):
    lo = t * (_lane_mask(KV_W, 0, HEAD) * scale)
    hi = t * (_lane_mask(KV_W, HEAD, KV_W) * scale)
    return [a.astype(_MXU) for a in (lo, pltpu.roll(lo, HEAD, 1), pltpu.roll(hi, HEAD, 1), hi)]


def _unplace_kv(d):
    return (_lane_mask(KV_W, 0, HEAD) * (d[0] + pltpu.roll(d[1], HEAD, 1))
            + _lane_mask(KV_W, HEAD, KV_W) * (d[3] + pltpu.roll(d[2], HEAD, 1)))


def _swa_probs(qh, ka, mask, sink):
    s = _mm_nt(qh, ka)
    s = jnp.where(mask, s, NEG_INF)
    m = jnp.maximum(jnp.max(s, axis=-1, keepdims=True), sink)
    p = jnp.exp(s - m)
    esink = jnp.exp(sink - m)
    inv = 1.0 / (jnp.sum(p, axis=-1, keepdims=True) + esink)
    return p * inv, esink * inv


def _mem_probs(s_all):
    out = []
    for j in range(4):
        s = s_all[:, MEM_LEN * j:MEM_LEN * (j + 1)]
        p = jnp.exp(s - jnp.max(s, axis=-1, keepdims=True))
        out.append(p * (1.0 / jnp.sum(p, axis=-1, keepdims=True)))
    return out


def _head_rows(t, scale):
    return jnp.concatenate([t * (_lane_mask(XATT_W, HEAD * j, HEAD * (j + 1)) * scale) for j in range(4)], axis=0)


def _lru_gates(xc, wg_ref, brg, big, lam):
    p0 = _mm(xc[:, :256], wg_ref[0])
    p1 = _mm(xc[:, 256:], wg_ref[1])
    rg = _sigmoid(jnp.concatenate([p0[:, :256], p1[:, :256]], axis=1) + brg)
    ig = _sigmoid(jnp.concatenate([p0[:, 256:], p1[:, 256:]], axis=1) + big)
    sp = _softplus(-lam)
    la = (-LRU_C) * rg * sp
    a = jnp.exp(la)
    th = jnp.tanh(la)
    one_minus_a2 = (-2.0 * th) / (1.0 - th)
    return rg, ig, sp, a, jnp.sqrt(one_minus_a2)


def _const_spec(shape, single=False):
    zeros = (0,) * len(shape)
    if single:
        return pl.BlockSpec(shape, lambda i: zeros, pipeline_mode=pl.Buffered(1))
    return pl.BlockSpec(shape, lambda i: zeros)


def _chip_of(x, y):
    return 2 * x + y


def _partners(x, y, c):
    north = c == 1
    near = (jnp.where(north, 1 - x, x), jnp.where(north, y, 1 - y))
    far = (jnp.where(north, x, 1 - x), jnp.where(north, 1 - y, y))
    return near, far, (1 - x, 1 - y)


def gather_weights(win_t, wout, wkv, conv_w, w_rg, w_ig, head_gains):
    arrs = (win_t, wout, wkv)
    n = len(arrs)
    pieces = [(a, k * (arr.shape[0] // (2 * cut)), arr.shape[0] // (2 * cut))
              for a, (arr, cut) in enumerate(zip(arrs, (2, 1, 1))) for k in range(cut)]
    npc = len(pieces)

    def body(a0, a1, a2, cw_in, wrg_ref, wig_ref, q_ref, k_ref, xq_ref, xk_ref, o0, o1, o2, cw_out, wg_ref, gn_ref,
             s0, s1, s2, cw, ocw, send, recv, lsem):
        ins, outs = (s0, s1, s2), (o0, o1, o2)
        for src, dst in zip((a0, a1, a2), ins):
            dst[...] = src[...].astype(dst.dtype)
        cw[...] = jnp.zeros(cw.shape, F32)
        cw[0:CONV_K, :] = cw_in[0]
        x, y, c = lax.axis_index("x"), lax.axis_index("y"), lax.axis_index("c")
        sibling = (x, y, 1 - c)
        near, far, diag = _partners(x, y, c)
        chips = [near, far, diag]
        me = _chip_of(x, y)

        def landed(p, chip, half):
            a, off, rows_ = pieces[p]
            r = ins[a].shape[0]
            return outs[a].at[pl.ds(pl.multiple_of(chip * r + half * (r // 2) + off, 16), rows_)]

        def mine(p):
            a, off, rows_ = pieces[p]
            return ins[a].at[pl.ds(pl.multiple_of(c * (ins[a].shape[0] // 2) + off, 16), rows_)]

        def copy(k, src, dst, to):
            return pltpu.make_async_remote_copy(src_ref=src, dst_ref=dst, send_sem=send.at[k], recv_sem=recv.at[k],
                                                device_id=to, device_id_type=MESH)

        def cw_rows(chip):
            return ocw.at[pl.ds(pl.multiple_of(chip * 8, 8), 8)]

        locals_ = []
        for a in range(n):
            r = ins[a].shape[0]
            locals_.append(pltpu.make_async_copy(ins[a], outs[a].at[pl.ds(pl.multiple_of(me * r, 16), r)], lsem.at[a]))
        locals_.append(pltpu.make_async_copy(cw, cw_rows(me), lsem.at[n]))
        for cp in locals_:
            cp.start()

        sent = []
        for p in range(npc):
            for j in range(2):
                sent.append(copy(p * 6 + j, mine(p), landed(p, me, c), (*chips[j], c)))
        for j, chip in enumerate(chips):
            sent.append(copy(npc * 6 + j, cw, cw_rows(me), (*chip, c)))
        for cp in sent:
            cp.start()

        gn_ref[...] = jnp.concatenate([q_ref[...]] * 2 + [k_ref[...]] * 2 + [xq_ref[...]] * 4 + [xk_ref[...]] * 4,
                                      axis=1)
        zeros = lambda lanes: [jnp.zeros((HEAD, lanes), F32)] if lanes else []
        for h in range(2):
            for b in range(4):
                row = []
                for w_ref in (wrg_ref, wig_ref):
                    row += zeros(HEAD * b) + [w_ref[0, 4 * h + b]] + zeros(HEAD * (3 - b))
                wg_ref[h, HEAD * b:HEAD * (b + 1), :] = jnp.concatenate(row, axis=1).astype(wg_ref.dtype)

        for j in range(3):
            for p in range(npc):
                got = landed(p, _chip_of(*chips[j]), c)
                copy(p * 6 + j, got, got, sibling).wait_recv()
                if j == 0:
                    sent.append(copy(p * 6 + 2, got, got, (*far, c)))
                    sent[-1].start()
                sent.append(copy(p * 6 + 3 + j, got, got, sibling))
                sent[-1].start()
        for p in range(npc):
            for j in range(3):
                got = landed(p, _chip_of(*chips[(1, 0, 2)[j]]), 1 - c)
                copy(p * 6 + 3 + j, got, got, sibling).wait_recv()
        for j, chip in enumerate(chips):
            got = cw_rows(_chip_of(*chip))
            copy(npc * 6 + j, got, got, (*chip, c)).wait_recv()
        for cp in sent:
            cp.wait_send()
        for cp in locals_:
            cp.wait()
        for chip in range(N_CHIPS):
            cw_out[:, 128 * chip:128 * (chip + 1)] = ocw[8 * chip:8 * chip + CONV_K, :]

    vm = pl.BlockSpec(memory_space=pltpu.VMEM)
    out_shape = tuple(jax.ShapeDtypeStruct((N_CHIPS * a.shape[0],) + a.shape[1:], _MXU) for a in arrs) + (
        jax.ShapeDtypeStruct((CONV_K, LRU_W), F32), jax.ShapeDtypeStruct((2, 256, 512), _MXU),
        jax.ShapeDtypeStruct((1, GAINS_W), F32))
    n_rdma = npc * 6 + 3
    return pl.pallas_call(
        body, name="gather_weights", out_shape=out_shape,
        in_specs=[vm] * 10, out_specs=(pl.BlockSpec(memory_space=pl.ANY),) * n + (vm, vm, vm),
        scratch_shapes=[pltpu.VMEM(a.shape, _MXU) for a in arrs] + [
            pltpu.VMEM((8, 128), F32), pltpu.VMEM((N_CHIPS * 8, 128), F32),
            pltpu.SemaphoreType.DMA((n_rdma,)), pltpu.SemaphoreType.DMA((n_rdma,)), pltpu.SemaphoreType.DMA((n + 1,))],
        compiler_params=pltpu.CompilerParams(vmem_limit_bytes=VMEM_LIMIT),
    )(win_t, wout, wkv, conv_w, w_rg, w_ig, *head_gains)


def mem_fwd(mem, mem_g, wkv, gains):
    def body(mem_ref, g_ref, w_ref, gn_ref, km_ref, vm_ref):
        mem_v = mem_ref[...]
        mn = mem_v * lax.rsqrt(_row_mean(mem_v * mem_v) + EPS) * g_ref[...]
        mkv = _mm(mn, w_ref[...])
        kpre = mkv[:, :XATT_W]
        gm = _group_matrix(XATT_W)
        km = kpre * lax.rsqrt(_seg_mean(kpre * kpre, gm) + EPS) * gn_ref[:, G_XK:GAINS_W]
        km_ref[...] = _head_rows(km, 0.125).astype(km_ref.dtype)
        vm_ref[...] = _head_rows(mkv[:, XATT_W:], 1.0).astype(vm_ref.dtype)

    vm = pl.BlockSpec(memory_space=pltpu.VMEM)
    rows_shape = jax.ShapeDtypeStruct((4 * MEM_LEN, XATT_W), _MXU)
    return pl.pallas_call(
        body, name="mem_fwd", out_shape=(rows_shape, rows_shape), in_specs=[vm] * 4, out_specs=(vm, vm),
    )(mem, mem_g, wkv, gains)


def mem_bwd(mem, mem_g, wkv, gains, dkm, dvm, g_gates, loss8, vectors):
    names = tuple(vectors)
    first_row = {name: (row, width) for name, row, width in SMALL_VECTORS}

    def body(mem_ref, g_ref, w_ref, gn_ref, dkm_ref, dvm_ref, gg_ref, loss_ref, *rest):
        vec_refs, (gw_ref, pk_ref) = rest[:len(names)], rest[len(names):]
        pk_ref[...] = jnp.zeros(pk_ref.shape, F32)

        def put(name, src):
            row, width = first_row[name]
            per_row = 1 if width < 128 else src.shape[1] // 128
            for t in range(src.shape[0]):
                for r in range(per_row):
                    at = row + per_row * t + r
                    pk_ref[at:at + 1, :] = src[t:t + 1, 128 * r:128 * (r + 1)]

        for name, ref in zip(names, vec_refs):
            put(name, ref)
        pk_ref[LOSS_ROW:LOSS_ROW + 1, :] = loss_ref[0:1, :]
        upper = lax.broadcasted_iota(jnp.int32, (HEAD, 128), 1) >= HEAD
        for h in range(2):
            for b in range(4):
                rg = gg_ref[h, HEAD * b:HEAD * (b + 1), 128 * (b // 2):128 * (b // 2 + 1)]
                ig = gg_ref[h, HEAD * b:HEAD * (b + 1), 256 + 128 * (b // 2):256 + 128 * (b // 2 + 1)]
                if b % 2:
                    rg = pltpu.roll(rg, HEAD, axis=1)
                else:
                    ig = pltpu.roll(ig, HEAD, axis=1)
                at = GATES_ROW + HEAD * (4 * h + b)
                pk_ref[at:at + HEAD, :] = jnp.where(upper, ig, rg)

        mem_v = mem_ref[...]
        mh = mem_v * lax.rsqrt(_row_mean(mem_v * mem_v) + EPS)
        mn = mh * g_ref[...]
        mkv = _mm(mn, w_ref[...])
        kpre = mkv[:, :XATT_W]
        gm = _group_matrix(XATT_W)
        rk = lax.rsqrt(_seg_mean(kpre * kpre, gm) + EPS)
        kn = kpre * rk
        dk = jnp.zeros((MEM_LEN, XATT_W), F32)
        dv = jnp.zeros((MEM_LEN, XATT_W), F32)
        for j in range(4):
            mj = _lane_mask(XATT_W, HEAD * j, HEAD * (j + 1))
            dk = dk + dkm_ref[:, MEM_LEN * j:MEM_LEN * (j + 1)].T * (mj * 0.125)
            dv = dv + dvm_ref[:, MEM_LEN * j:MEM_LEN * (j + 1)].T * mj
        put("xk_norm_g", _fold_heads(_col_sum(dk * kn)))
        dkn = dk * gn_ref[:, G_XK:GAINS_W]
        dkpre = rk * (dkn - kn * _seg_mean(dkn * kn, gm))
        dmkv = jnp.concatenate([dkpre, dv], axis=1)
        gw_ref[...] = _mm_tn(mn, dmkv)
        dmn = _mm_nt(dmkv, w_ref[...])
        put("mem_norm_g", _col_sum(dmn * mh))

    vm = pl.BlockSpec(memory_space=pltpu.VMEM)
    return pl.pallas_call(
        body, name="mem_bwd",
        out_shape=(jax.ShapeDtypeStruct((D_MODEL, 2 * XATT_W), F32), jax.ShapeDtypeStruct((SMALL_ROWS, 128), F32)),
        in_specs=[vm] * (8 + len(names)), out_specs=(vm, vm),
    )(mem, mem_g, wkv, gains, dkm, dvm, g_gates, loss8, *vectors.values())


def layer_fwd(x, tgt, rc, rs1, rs2, ng, win_t, cw, cb, wg, brg, big, lam, gains, sinks, km, vm, og, wout):
    seq = x.shape[0]
    tm = min(ROW_TILE, seq)
    nt = seq // tm
    nb = tm // BLOCK

    def body(x_ref, t_ref, c_ref, s1_ref, s2_ref, ng_ref, win_ref, cw_ref, cb_ref, wg_ref, brg_ref, big_ref, lam_ref,
             gn_ref, sink_ref, km_ref, vm_ref, og_ref, wout_ref,
             proj_ref, ya_ref, yb_ref, yc_ref, ycat_ref, xn_ref, dout_ref, pswa_ref, pmem_ref, psink_ref, gates_ref,
             a_ref, loss_ref,
             ext_ref, b_scr, hc_ref, kp_ref, vp_ref, lacc_ref):
        i = pl.program_id(0)

        @pl.when(i == 0)
        def _():
            ext_ref[0:8, :] = jnp.zeros((8, LRU_W), F32)
            hc_ref[...] = jnp.zeros_like(hc_ref)
            kp_ref[...] = jnp.zeros_like(kp_ref)
            vp_ref[...] = jnp.zeros_like(vp_ref)
            lacc_ref[...] = jnp.zeros_like(lacc_ref)

        xv = x_ref[...]
        xn = (xv * lax.rsqrt(_row_mean(xv * xv) + EPS) * ng_ref[...]).astype(_MXU)
        xn_ref[...] = xn.astype(xn_ref.dtype)
        proj_ref[...] = _mm_nt(xn, win_ref[...])

        u = proj_ref[:, C_LRUX:C_LRUX + LRU_W]
        ext_ref[8:8 + tm, :] = u
        xc = cb_ref[...]
        for k in range(CONV_K):
            xc = xc + cw_ref[k:k + 1, :] * ext_ref[pl.ds(5 + k, tm), :]
        ext_ref[0:8, :] = u[tm - 8:tm, :]
        rg, ig, sp, a, sq = _lru_gates(xc, wg_ref, brg_ref[...], big_ref[...], lam_ref[...])
        for k, t in enumerate((xc, rg, ig, sq)):
            gates_ref[:, LRU_W * k:LRU_W * (k + 1)] = t.astype(gates_ref.dtype)
        a_ref[...] = a
        b_scr[...] = sq * (ig * xc)
        row8 = lax.broadcasted_iota(jnp.int32, (8, LRU_W), 0)

        def scan_step(g, carry):
            r0 = pl.multiple_of(g * 8, 8)
            av = a_ref[pl.ds(r0, 8), :]
            bv = b_scr[pl.ds(r0, 8), :]
            for d in (1, 2, 4):
                a_sh = jnp.where(row8 >= d, pltpu.roll(av, d, 0), 1.0)
                b_sh = jnp.where(row8 >= d, pltpu.roll(bv, d, 0), 0.0)
                bv = bv + av * b_sh
                av = av * a_sh
            hv = bv + av * carry
            ya_ref[pl.ds(r0, 8), :] = hv
            return hv[7:8, :]

        hc_ref[0:1, :] = lax.fori_loop(0, tm // 8, scan_step, hc_ref[0:1, :], unroll=True)

        gm128 = _group_matrix(KV_W)
        cv, s1v, s2v = c_ref[...], s1_ref[...], s2_ref[...]

        def head_norm_rope(t, g):
            n = t * lax.rsqrt(_seg_mean(t * t, gm128) + EPS)
            return _rope(n * g, cv, s1v, s2v)

        qs_ = (head_norm_rope(proj_ref[:, C_SQ:C_SQ + 128], gn_ref[:, G_Q:G_K]).astype(_MXU),
               head_norm_rope(proj_ref[:, C_SQ + 128:C_SQ + 256], gn_ref[:, G_Q:G_K]).astype(_MXU))
        kr = head_norm_rope(proj_ref[:, C_SK:C_SK + KV_W], gn_ref[:, G_K:G_XQ])
        sv = proj_ref[:, C_SV:C_SV + KV_W]
        ka = _place_kv(jnp.concatenate([kp_ref[...], kr], axis=0), 0.125)
        va = _place_kv(jnp.concatenate([vp_ref[...], sv], axis=0), 1.0)
        kp_ref[...] = kr[tm - BLOCK:tm, :]
        vp_ref[...] = sv[tm - BLOCK:tm, :]
        lane128 = lax.broadcasted_iota(jnp.int32, (1, 128), 1)
        for b in range(nb):
            mask = _swa_mask((i == 0) & (b == 0)) if b == 0 else _swa_mask(False)
            band = slice(BLOCK * b, BLOCK * b + 2 * BLOCK)
            blk = slice(BLOCK * b, BLOCK * (b + 1))
            psink = jnp.zeros((BLOCK, 128), F32)
            for j in range(4):
                p, pk = _swa_probs(qs_[j // 2][blk], ka[j][band], mask, sink_ref[0, j])
                pswa_ref[blk, 2 * BLOCK * j:2 * BLOCK * (j + 1)] = p.astype(pswa_ref.dtype)
                psink = jnp.where(lane128 == j, pk, psink)
            psink_ref[blk, :] = psink
            for h in range(2):
                yb_ref[blk, KV_W * h:KV_W * (h + 1)] = _mm(
                    pswa_ref[blk, 4 * BLOCK * h:4 * BLOCK * (h + 1)],
                    jnp.concatenate([va[2 * h][band], va[2 * h + 1][band]], axis=0))

        gm256 = _group_matrix(XATT_W)
        xq = proj_ref[:, C_XQ:C_XQ + XATT_W]
        qx = xq * lax.rsqrt(_seg_mean(xq * xq, gm256) + EPS) * gn_ref[:, G_XQ:G_XK]
        pm = _mem_probs(_mm_nt(qx, km_ref[...]))
        for j in range(4):
            pmem_ref[:, MEM_LEN * j:MEM_LEN * (j + 1)] = pm[j].astype(pmem_ref.dtype)
        yc = _mm(pmem_ref[...], vm_ref[...])
        yc_ref[...] = yc

        def gated(y, g, gate):
            return y * lax.rsqrt(_row_mean(y * y) + EPS) * g * (gate * _sigmoid(gate))

        ogv = og_ref[...]
        za = gated(ya_ref[...], ogv[:, :512], proj_ref[:, C_LRUG:C_LRUG + LRU_W])
        zb = gated(yb_ref[...], ogv[:, 512:768], proj_ref[:, C_SWAG:C_SWAG + SWA_W])
        zc = gated(yc, ogv[:, 768:], proj_ref[:, C_XG:C_XG + XATT_W])
        ycat_ref[:, 0:512] = za.astype(ycat_ref.dtype)
        ycat_ref[:, 512:768] = zb.astype(ycat_ref.dtype)
        ycat_ref[:, 768:1024] = zc.astype(ycat_ref.dtype)
        out = xv + _mm(ycat_ref[...], wout_ref[...])
        err = out - t_ref[...]
        dout_ref[...] = (err * (1.0 / D_MODEL)).astype(dout_ref.dtype)
        lacc_ref[...] = lacc_ref[...] + (0.5 / D_MODEL) * jnp.sum(err * err)

        @pl.when(i == nt - 1)
        def _():
            loss_ref[...] = lacc_ref[...]

    def rows(ncol):
        return pl.BlockSpec((tm, ncol), lambda i: (i, 0))

    in_specs = [rows(D_MODEL), rows(D_MODEL), rows(128), rows(128), rows(128),
                _const_spec((1, D_MODEL)), _const_spec((D_IN, D_MODEL), True), _const_spec((CONV_K, LRU_W)),
                _const_spec((1, LRU_W)), _const_spec((2, 256, 512), True), _const_spec((1, LRU_W)),
                _const_spec((1, LRU_W)), _const_spec((1, LRU_W)), _const_spec((1, GAINS_W)), pl.BlockSpec(memory_space=pltpu.SMEM),
                _const_spec((4 * MEM_LEN, XATT_W), True), _const_spec((4 * MEM_LEN, XATT_W), True),
                _const_spec((1, D_MODEL)), _const_spec((D_MODEL, D_MODEL), True)]
    out_shape = (jax.ShapeDtypeStruct((seq, D_IN), F32), jax.ShapeDtypeStruct((seq, LRU_W), F32),
                 jax.ShapeDtypeStruct((seq, SWA_W), F32), jax.ShapeDtypeStruct((seq, XATT_W), F32),
                 jax.ShapeDtypeStruct((seq, D_MODEL), _MXU), jax.ShapeDtypeStruct((seq, D_MODEL), _MXU),
                 jax.ShapeDtypeStruct((seq, D_MODEL), _MXU), jax.ShapeDtypeStruct((seq, 4 * 2 * BLOCK), _MXU),
                 jax.ShapeDtypeStruct((seq, 4 * MEM_LEN), _MXU), jax.ShapeDtypeStruct((seq, 128), F32),
                 jax.ShapeDtypeStruct((seq, 4 * LRU_W), _MXU), jax.ShapeDtypeStruct((seq, LRU_W), F32),
                 jax.ShapeDtypeStruct((8, 128), F32))
    out_specs = (rows(D_IN), rows(LRU_W), rows(SWA_W), rows(XATT_W), rows(D_MODEL), rows(D_MODEL), rows(D_MODEL),
                 rows(4 * 2 * BLOCK), rows(4 * MEM_LEN), rows(128), rows(4 * LRU_W), rows(LRU_W),
                 _const_spec((8, 128)))
    scratch = [pltpu.VMEM((tm + 8, LRU_W), F32), pltpu.VMEM((tm, LRU_W), F32),
               pltpu.VMEM((8, LRU_W), F32), pltpu.VMEM((BLOCK, KV_W), F32), pltpu.VMEM((BLOCK, KV_W), F32),
               pltpu.VMEM((8, 128), F32)]
    return pl.pallas_call(
        body, name="layer_fwd", grid=(nt,), out_shape=out_shape, in_specs=in_specs, out_specs=out_specs,
        scratch_shapes=scratch,
        compiler_params=pltpu.CompilerParams(dimension_semantics=("arbitrary",), vmem_limit_bytes=VMEM_LIMIT),
    )(x, tgt, rc, rs1, rs2, ng, win_t, cw, cb, wg, brg, big, lam, gains, sinks, km, vm, og, wout)


def weight_grads(ycat, dout, dproj, xn, g_kv, g_small, early_at, late_at):
    seq, ncol = xn.shape
    blk = 256
    n_out, n_in = ycat.shape[1] // blk, dproj.shape[1] // blk
    assert n_out == N_CHIPS and late_at[0] >= n_out
    g_out = jax.ShapeDtypeStruct((N_CHIPS, 2, blk // 2, dout.shape[1]), F32)
    shape_e, scratch_e = _hosted_reduce_shapes([g_kv], g_small)
    shape_l, scratch_l = _hosted_reduce_shapes([g_out], None)

    def body(l1_ref, r1_ref, l2_ref, r2_hbm, kv_ref, sm_ref, o_ref, sum_out, sum_kv, sum_sm, gout_scr, r2_ref, r2_sem,
             *scratch):
        j = pl.program_id(0)
        r2_copy = pltpu.make_async_copy(r2_hbm, r2_ref, r2_sem.at[0])
        pl.when(j == 0)(r2_copy.start)

        def reduce_stages(closing):
            _hosted_reduce(j, n_out + n_in, closing, early_at, (kv_ref, sm_ref), (sum_kv, sum_sm),
                           scratch[:len(scratch_e)], True)
            _hosted_reduce(j, n_out + n_in, closing, late_at, (gout_scr,), (sum_out,), scratch[len(scratch_e):], False)

        reduce_stages(False)

        @pl.when(j < n_out)
        def _():
            gout_scr[j] = _mm_tn(l1_ref[...], r1_ref[...]).reshape(g_out.shape[1:])

        pl.when(j == n_out)(r2_copy.wait)

        @pl.when(j >= n_out)
        def _():
            o_ref[...] = _mm_tn(l2_ref[...], r2_ref[...])

        reduce_stages(True)

    vm = pl.BlockSpec(memory_space=pltpu.VMEM)
    hbm = pl.BlockSpec(memory_space=pl.ANY)
    return pl.pallas_call(
        body, name="weight_grads", grid=(n_out + n_in,),
        out_shape=(jax.ShapeDtypeStruct((dproj.shape[1], ncol), F32), *shape_l, *shape_e),
        in_specs=[pl.BlockSpec((seq, blk), lambda j: (0, jnp.minimum(j, n_out - 1))), _const_spec(dout.shape, True),
                  pl.BlockSpec((seq, blk), lambda j: (0, jnp.maximum(j - n_out, 0))), hbm, hbm, vm],
        out_specs=(pl.BlockSpec((blk, ncol), lambda j: (jnp.maximum(j - n_out, 0), 0)), hbm, hbm, hbm),
        scratch_shapes=[pltpu.VMEM(g_out.shape, F32), pltpu.VMEM(xn.shape, xn.dtype), pltpu.SemaphoreType.DMA((1,))]
        + scratch_e + scratch_l,
        compiler_params=pltpu.CompilerParams(dimension_semantics=("arbitrary",), vmem_limit_bytes=VMEM_LIMIT),
    )(ycat, dout, dproj, xn, g_kv, g_small)


def layer_bwd(x, dout, proj, ya, yb, yc, pswa, pmem, psink, gates, a_all, rc, rs1, rs2, ng, win_t, cw, wg, lam, gains,
              km, vm, og, wout):
    seq = x.shape[0]
    tm = min(ROW_TILE, seq)
    nt = seq // tm
    nb = tm // BLOCK

    def body(x_ref, dout_ref, proj_ref, ya_ref, yb_ref, yc_ref, pswa_ref, pmem_ref, psink_ref, gates_ref, a_ref,
             c_ref, s1_ref, s2_ref,
             yah_ref, kvh_ref, ch_ref, s1h_ref, s2h_ref,
             ng_ref, win_hbm, cw_ref, wg_ref, lam_ref, gn_ref, km_ref, vm_ref, og_ref, wout_ref,
             gx_ref, dproj_ref, gwg_ref, dkm_ref, dvm_ref, gng_ref, gog_ref, gcb_ref, gbrg_ref, gbig_ref, glam_ref,
             gcw_ref, gqn_ref, gkn_ref, gxqn_ref, gsink_ref,
             hext_ref, aext_ref, an_scr, dh_scr, g_scr, dxc_ext, gcar_ref, dkcar_ref, dvcar_ref, win_ref, win_sem):
        i = pl.program_id(0)
        tile = nt - 1 - i
        first_tile = tile == 0
        win_copy = pltpu.make_async_copy(win_hbm, win_ref, win_sem.at[0])

        @pl.when(i == 0)
        def _():
            win_copy.start()
            for r in (gwg_ref, dkm_ref, dvm_ref, gng_ref, gog_ref, gcb_ref, gbrg_ref, gbig_ref, glam_ref, gcw_ref,
                      gqn_ref, gkn_ref, gxqn_ref, gsink_ref, gcar_ref, dkcar_ref, dvcar_ref):
                r[...] = jnp.zeros_like(r)
            dxc_ext[tm:tm + 8, :] = jnp.zeros((8, LRU_W), F32)
            aext_ref[tm:tm + 8, :] = jnp.zeros((8, LRU_W), F32)

        xv = x_ref[...]
        dov = dout_ref[...]
        dz = _mm_nt(dov, wout_ref[...])
        ogv = og_ref[...]

        def group_bwd(y, gate, g, dzg):
            r = lax.rsqrt(_row_mean(y * y) + EPS)
            n = y * r
            sg = _sigmoid(gate)
            dgate = dzg * (n * g) * (sg * (1.0 + gate * (1.0 - sg)))
            dng = dzg * (gate * sg)
            dn = dng * g
            return r * (dn - n * _row_mean(dn * n)), dgate, _col_sum(dng * n)

        dya, dga, goa = group_bwd(ya_ref[...], proj_ref[:, C_LRUG:C_LRUG + LRU_W], ogv[:, :512], dz[:, :512])
        dyb, dgb, gob = group_bwd(yb_ref[...], proj_ref[:, C_SWAG:C_SWAG + SWA_W], ogv[:, 512:768], dz[:, 512:768])
        dyc, dgc, goc = group_bwd(yc_ref[...], proj_ref[:, C_XG:C_XG + XATT_W], ogv[:, 768:], dz[:, 768:])
        gog_ref[...] += jnp.concatenate([goa, gob, goc], axis=1)
        dproj_ref[:, C_LRUG:C_LRUG + LRU_W] = dga.astype(dproj_ref.dtype)
        dproj_ref[:, C_SWAG:C_SWAG + SWA_W] = dgb.astype(dproj_ref.dtype)
        dproj_ref[:, C_XG:C_XG + XATT_W] = dgc.astype(dproj_ref.dtype)

        gm256 = _group_matrix(XATT_W)
        xq = proj_ref[:, C_XQ:C_XQ + XATT_W]
        rq = lax.rsqrt(_seg_mean(xq * xq, gm256) + EPS)
        qn = xq * rq
        qx = qn * gn_ref[:, G_XQ:G_XK]
        qxb = qx.astype(_MXU)
        dycb = dyc.astype(_MXU)
        dp_all = _mm_nt(dycb, vm_ref[...])
        dsm = []
        for j in range(4):
            pj = pmem_ref[:, MEM_LEN * j:MEM_LEN * (j + 1)].astype(F32)
            dp = dp_all[:, MEM_LEN * j:MEM_LEN * (j + 1)]
            dsm.append((pj * (dp - jnp.sum(pj * dp, axis=-1, keepdims=True))).astype(_MXU))
        ds_all = jnp.concatenate(dsm, axis=1)
        dvm_ref[...] += _mm_tn(dycb, pmem_ref[...])
        dkm_ref[...] += _mm_tn(qxb, ds_all)
        dqx = _mm(ds_all, km_ref[...])
        gxqn_ref[...] += _col_sum(dqx * qn)
        dqn = dqx * gn_ref[:, G_XQ:G_XK]
        dproj_ref[:, C_XQ:C_XQ + XATT_W] = (rq * (dqn - qn * _seg_mean(dqn * qn, gm256))).astype(dproj_ref.dtype)

        gm128 = _group_matrix(KV_W)
        cv, s1v, s2v = c_ref[...], s1_ref[...], s2_ref[...]

        def head_norm(t):
            r = lax.rsqrt(_seg_mean(t * t, gm128) + EPS)
            return t * r, r

        qn_, qr_ = zip(head_norm(proj_ref[:, C_SQ:C_SQ + 128]), head_norm(proj_ref[:, C_SQ + 128:C_SQ + 256]))
        qrope = [_rope(qn_[h] * gn_ref[:, G_Q:G_K], cv, s1v, s2v).astype(_MXU) for h in range(2)]
        kn, krr = head_norm(proj_ref[:, C_SK:C_SK + KV_W])
        kr = _rope(kn * gn_ref[:, G_K:G_XQ], cv, s1v, s2v)
        khn, _ = head_norm(kvh_ref[:, 0:KV_W])
        khr = _rope(khn * gn_ref[:, G_K:G_XQ], ch_ref[...], s1h_ref[...], s2h_ref[...])
        ka = _place_kv(jnp.concatenate([khr, kr], axis=0), 0.125)
        va = _place_kv(jnp.concatenate([kvh_ref[:, KV_W:2 * KV_W], proj_ref[:, C_SV:C_SV + KV_W]], axis=0), 1.0)
        lane128 = lax.broadcasted_iota(jnp.int32, (1, 128), 1)
        gsink = jnp.zeros((1, 128), F32)
        dk_band, dv_band, dq_blk = [], [], []
        for b in range(nb):
            band = slice(BLOCK * b, BLOCK * b + 2 * BLOCK)
            blk = slice(BLOCK * b, BLOCK * (b + 1))
            dka, dva, dsb = [], [], []
            deltas = jnp.zeros((BLOCK, 128), F32)
            for j in range(4):
                qh = qrope[j // 2][blk]
                doh = dyb[blk, KV_W * (j // 2):KV_W * (j // 2 + 1)].astype(_MXU)
                pb = pswa_ref[blk, 2 * BLOCK * j:2 * BLOCK * (j + 1)]
                p = pb.astype(F32)
                dp = _mm_nt(doh, va[j][band])
                delta = jnp.sum(p * dp, axis=-1, keepdims=True)
                ds = (p * (dp - delta)).astype(_MXU)
                deltas = jnp.where(lane128 == j, delta, deltas)
                dva.append(_mm_tn(pb, doh))
                dka.append(_mm_tn(ds, qh))
                dsb.append(ds)
            gsink = gsink - _col_sum(psink_ref[blk, :] * deltas)
            dk_band.append(_unplace_kv(dka) * 0.125)
            dv_band.append(_unplace_kv(dva))
            dq_blk.append([_mm(jnp.concatenate(dsb[2 * h:2 * h + 2], axis=1),
                               jnp.concatenate([ka[2 * h][band], ka[2 * h + 1][band]], axis=0)) for h in range(2)])
        gsink_ref[...] += gsink
        dk_rows = [dk_band[b][BLOCK:] + (dk_band[b + 1][:BLOCK] if b + 1 < nb else dkcar_ref[...]) for b in range(nb)]
        dv_rows = [dv_band[b][BLOCK:] + (dv_band[b + 1][:BLOCK] if b + 1 < nb else dvcar_ref[...]) for b in range(nb)]
        dkcar_ref[...] = dk_band[0][:BLOCK]
        dvcar_ref[...] = dv_band[0][:BLOCK]
        dkg = _rope_bwd(jnp.concatenate(dk_rows, axis=0), cv, s1v, s2v)
        gkn = _col_sum(dkg * kn)
        dkn = dkg * gn_ref[:, G_K:G_XQ]
        dproj_ref[:, C_SK:C_SK + KV_W] = (krr * (dkn - kn * _seg_mean(dkn * kn, gm128))).astype(dproj_ref.dtype)
        dproj_ref[:, C_SV:C_SV + KV_W] = jnp.concatenate(dv_rows, axis=0).astype(dproj_ref.dtype)
        gqn = jnp.zeros((1, 128), F32)
        for h in range(2):
            dqg = _rope_bwd(jnp.concatenate([dq_blk[b][h] for b in range(nb)], axis=0), cv, s1v, s2v)
            gqn = gqn + _col_sum(dqg * qn_[h])
            dqn_ = dqg * gn_ref[:, G_Q:G_K]
            dproj_ref[:, C_SQ + 128 * h:C_SQ + 128 * (h + 1)] = (
                qr_[h] * (dqn_ - qn_[h] * _seg_mean(dqn_ * qn_[h], gm128))).astype(dproj_ref.dtype)
        gqn_ref[...] += gqn
        gkn_ref[...] += gkn

        u = proj_ref[:, C_LRUX:C_LRUX + LRU_W]
        xc, rg, ig, sq = (gates_ref[:, LRU_W * k:LRU_W * (k + 1)].astype(F32) for k in range(4))
        a = a_ref[...]
        sp = _softplus(-lam_ref[...])
        hext_ref[0:8, :] = jnp.where(first_tile, 0.0, yah_ref[...])
        hext_ref[8:8 + tm, :] = ya_ref[...]
        hprev = hext_ref[pl.ds(7, tm), :]
        aext_ref[0:tm, :] = a
        an_scr[...] = aext_ref[pl.ds(1, tm), :]
        dh_scr[...] = dya
        dh_scr[tm - 1:tm, :] = dh_scr[tm - 1:tm, :] + gcar_ref[0:1, :]
        row8 = lax.broadcasted_iota(jnp.int32, (8, LRU_W), 0)

        def scan_step(gi, carry):
            r0 = pl.multiple_of((tm // 8 - 1 - gi) * 8, 8)
            av = an_scr[pl.ds(r0, 8), :]
            bv = dh_scr[pl.ds(r0, 8), :]
            for d in (1, 2, 4):
                a_sh = jnp.where(row8 < 8 - d, pltpu.roll(av, 8 - d, 0), 1.0)
                b_sh = jnp.where(row8 < 8 - d, pltpu.roll(bv, 8 - d, 0), 0.0)
                bv = bv + av * b_sh
                av = av * a_sh
            gv = bv + av * carry
            g_scr[pl.ds(r0, 8), :] = gv
            return gv[0:1, :]

        g0 = lax.fori_loop(0, tm // 8, scan_step, jnp.zeros((1, LRU_W), F32), unroll=True)
        gcar_ref[0:1, :] = a[0:1, :] * g0
        gv = g_scr[...]
        da = gv * hprev
        dig = gv * sq * xc
        dxc = gv * sq * ig
        dla = da * a - gv * (ig * xc) * ((a * a) / sq)
        drg = dla * ((-LRU_C) * sp)
        glam_ref[...] += _col_sum(dla * rg)
        dpr = drg * rg * (1.0 - rg)
        dpi = dig * ig * (1.0 - ig)
        gbrg_ref[...] += _col_sum(dpr)
        gbig_ref[...] += _col_sum(dpi)
        dpre0 = jnp.concatenate([dpr[:, :256], dpi[:, :256]], axis=1).astype(_MXU)
        dpre1 = jnp.concatenate([dpr[:, 256:], dpi[:, 256:]], axis=1).astype(_MXU)
        gwg_ref[0] += _mm_tn(xc[:, :256], dpre0)
        gwg_ref[1] += _mm_tn(xc[:, 256:], dpre1)
        dxc = dxc + jnp.concatenate([_mm_nt(dpre0, wg_ref[0]), _mm_nt(dpre1, wg_ref[1])], axis=1)
        gcb_ref[...] += _col_sum(dxc)
        dxc_ext[0:tm, :] = dxc
        du = jnp.zeros((tm, LRU_W), F32)
        for k in range(CONV_K):
            later = dxc_ext[pl.ds(3 - k, tm), :]
            gcw_ref[k:k + 1, :] += _col_sum(later * u)
            du = du + cw_ref[k:k + 1, :] * later
        dxc_ext[tm:tm + 8, :] = dxc[0:8, :]
        dproj_ref[:, C_LRUX:C_LRUX + LRU_W] = du.astype(dproj_ref.dtype)

        pl.when(i == 0)(win_copy.wait)
        dxn = _mm(dproj_ref[...], win_ref[...])
        rx = lax.rsqrt(_row_mean(xv * xv) + EPS)
        xh = xv * rx
        gng_ref[...] += _col_sum(dxn * xh)
        dxh = dxn * ng_ref[...]
        gx_ref[...] = dov.astype(F32) + rx * (dxh - xh * _row_mean(dxh * xh))

        @pl.when(i == nt - 1)
        def _():
            glam_ref[...] = glam_ref[...] * (LRU_C * _sigmoid(-lam_ref[...]))
            for r in (gqn_ref, gkn_ref, gxqn_ref):
                r[...] = _fold_heads(r[...])

    def rows(ncol, arr_cols_block=0):
        return pl.BlockSpec((tm, ncol), lambda i: (nt - 1 - i, arr_cols_block))

    def halo(nrow, ncol, colblk=0):
        per = tm // nrow
        return pl.BlockSpec((nrow, ncol), lambda i: (jnp.maximum((nt - 1 - i) * per - 1, 0), colblk))

    in_specs = [rows(D_MODEL), rows(D_MODEL), rows(D_IN), rows(LRU_W), rows(SWA_W), rows(XATT_W),
                rows(4 * 2 * BLOCK), rows(4 * MEM_LEN), rows(128), rows(4 * LRU_W), rows(LRU_W),
                rows(128), rows(128), rows(128),
                halo(8, LRU_W), halo(BLOCK, 2 * KV_W, C_SK // (2 * KV_W)),
                halo(BLOCK, 128), halo(BLOCK, 128), halo(BLOCK, 128),
                _const_spec((1, D_MODEL)), pl.BlockSpec(memory_space=pl.ANY), _const_spec((CONV_K, LRU_W)),
                _const_spec((2, 256, 512), True), _const_spec((1, LRU_W)), _const_spec((1, GAINS_W)),
                _const_spec((4 * MEM_LEN, XATT_W), True), _const_spec((4 * MEM_LEN, XATT_W), True),
                _const_spec((1, D_MODEL)), _const_spec((D_MODEL, D_MODEL), True)]
    small = [(2, 256, 512), (XATT_W, 4 * MEM_LEN), (XATT_W, 4 * MEM_LEN), (1, D_MODEL), (1, D_MODEL), (1, LRU_W), (1, LRU_W),
             (1, LRU_W), (1, LRU_W), (CONV_K, LRU_W), (1, 128), (1, 128), (1, XATT_W), (1, 128)]
    out_shape = (jax.ShapeDtypeStruct((seq, D_MODEL), F32), jax.ShapeDtypeStruct((seq, D_IN), _MXU)) + tuple(
        jax.ShapeDtypeStruct(s, F32) for s in small)
    out_specs = (rows(D_MODEL), rows(D_IN)) + tuple(_const_spec(s) for s in small)
    scratch = [pltpu.VMEM((tm + 8, LRU_W), F32), pltpu.VMEM((tm + 8, LRU_W), F32),
               pltpu.VMEM((tm, LRU_W), F32), pltpu.VMEM((tm, LRU_W), F32), pltpu.VMEM((tm, LRU_W), F32),
               pltpu.VMEM((tm + 8, LRU_W), F32),
               pltpu.VMEM((8, LRU_W), F32), pltpu.VMEM((BLOCK, KV_W), F32), pltpu.VMEM((BLOCK, KV_W), F32),
               pltpu.VMEM((D_IN, D_MODEL), _MXU), pltpu.SemaphoreType.DMA((1,))]
    return pl.pallas_call(
        body, name="layer_bwd", grid=(nt,), out_shape=out_shape, in_specs=in_specs, out_specs=out_specs,
        scratch_shapes=scratch,
        compiler_params=pltpu.CompilerParams(dimension_semantics=("arbitrary",), vmem_limit_bytes=VMEM_LIMIT),
    )(x, dout, proj, ya, yb, yc, pswa, pmem, psink, gates, a_all, rc, rs1, rs2, ya, proj, rc, rs1, rs2,
      ng, win_t, cw, wg, lam, gains, km, vm, og, wout)


def _reduce_protocol(big, sm, outs, osm, r1, r1s, wire, r2, r2s, wire2, ps, own, send, recv, lsem):
    nbig = len(big)
    x, y, c = lax.axis_index("x"), lax.axis_index("y"), lax.axis_index("c")
    sibling = (x, y, 1 - c)
    near, far, diag = _partners(x, y, c)
    me, near_id, far_id, diag_id = _chip_of(x, y), _chip_of(*near), _chip_of(*far), _chip_of(*diag)

    def copy(k, src, dst, to):
        return pltpu.make_async_remote_copy(src_ref=src, dst_ref=dst, send_sem=send.at[k], recv_sem=recv.at[k],
                                            device_id=to, device_id_type=MESH)

    def sent(stage, a):
        if a == nbig:
            src, dst, to = ((sm.at[1 - c], r1s, sibling), (r1s, r2s.at[0], (*near, c)), (ps, r2s.at[1], (*far, c)),
                            (osm.at[c], osm.at[c], sibling))[stage]
            return [copy(5 * nbig + stage, src, dst, to)]
        if stage == 0:
            return [copy(5 * a, big[a].at[:, 1 - c], r1[a], sibling)]
        if stage == 1:
            return [copy(5 * a + 1, wire[a].at[near_id], r2[a].at[0], (*near, c)),
                    copy(5 * a + 2, wire[a].at[diag_id], r2[a].at[1], (*near, c))]
        if stage == 2:
            return [copy(5 * a + 3, wire2[a], r2[a].at[2], (*far, c))]
        return [copy(5 * a + 4, outs[a].at[c], outs[a].at[c], sibling)]

    arrays = range(nbig + (sm is not None))

    def start(stage, a):
        for cp in sent(stage, a):
            cp.start()

    def arrived(k, ref):
        copy(k, ref, ref, sibling).wait_recv()

    def loads():
        return [pltpu.make_async_copy(big[a].at[:, c], own[a], lsem.at[a]) for a in range(nbig)]

    def stage0():
        for a in arrays:
            start(0, a)
        for cp in loads():
            cp.start()

    def stage1():
        for a in range(nbig):
            loads()[a].wait()
            arrived(5 * a, r1[a])
            for k in range(N_CHIPS):
                r1[a][k] = own[a][k] + r1[a][k]
                wire[a][k] = r1[a][k].astype(wire[a].dtype)
            start(1, a)
        if sm is not None:
            arrived(5 * nbig, r1s)
            r1s[...] = sm[c] + r1s[...]
            start(1, nbig)

    def stage2():
        for a in range(nbig):
            arrived(5 * a + 1, r2[a].at[0])
            arrived(5 * a + 2, r2[a].at[1])
            r1[a][me] = r1[a][me] + r2[a][0].astype(F32)
            wire2[a][...] = (r1[a][far_id] + r2[a][1].astype(F32)).astype(wire2[a].dtype)
            start(2, a)
        if sm is not None:
            arrived(5 * nbig + 1, r2s.at[0])
            ps[...] = r1s[...] + r2s[0]
            start(2, nbig)

    def stage3():
        for a in range(nbig):
            arrived(5 * a + 3, r2[a].at[2])
            outs[a][c] = r1[a][me] + r2[a][2].astype(F32)
            start(3, a)
        if sm is not None:
            arrived(5 * nbig + 2, r2s.at[1])
            osm[c] = ps[...] + r2s[1]
            start(3, nbig)

    def stage4():
        for a in range(nbig):
            arrived(5 * a + 4, outs[a].at[1 - c])
        if sm is not None:
            arrived(5 * nbig + 3, osm.at[1 - c])
        for stage in range(4):
            for a in arrays:
                for cp in sent(stage, a):
                    cp.wait_send()

    return [stage0, stage1, stage2, stage3, stage4]


def _reduce_buffers(bigs, g_small):
    half = [b.shape[2:] for b in bigs]
    sm_half = None if g_small is None else g_small.shape[1:]
    out_shape = [jax.ShapeDtypeStruct((2,) + h, F32) for h in half]
    small = lambda lead: [] if g_small is None else [pltpu.VMEM(lead + sm_half, F32)]
    if g_small is not None:
        out_shape.append(jax.ShapeDtypeStruct(g_small.shape, F32))
    n_sem = 5 * len(bigs) + 4
    scratch = ([pltpu.VMEM((N_CHIPS,) + h, F32) for h in half] + small(())
               + [pltpu.VMEM((N_CHIPS,) + h, _WIRE) for h in half]
               + [pltpu.VMEM((3,) + h, _WIRE) for h in half] + small((2,))
               + [pltpu.VMEM(h, _WIRE) for h in half] + small(())
               + [pltpu.VMEM((N_CHIPS,) + h, F32) for h in half]
               + [pltpu.SemaphoreType.DMA((n_sem,)), pltpu.SemaphoreType.DMA((n_sem,)),
                  pltpu.SemaphoreType.DMA((len(bigs),))])
    return out_shape, scratch


def _split_reduce_refs(refs, nbig, has_small):
    it = iter(refs)
    take = lambda n: [next(it) for _ in range(n)]
    one = lambda: next(it) if has_small else None
    big, sm = take(nbig), one()
    outs, osm = take(nbig), one()
    r1, r1s, wire, r2, r2s, wire2, ps, own = take(nbig), one(), take(nbig), take(nbig), one(), take(nbig), one(), take(nbig)
    send, recv, lsem = take(3)
    return big, sm, outs, osm, r1, r1s, wire, r2, r2s, wire2, ps, own, send, recv, lsem


def _hosted_reduce_shapes(bigs, g_small):
    red_shape, scratch = _reduce_buffers(bigs, g_small)
    nres = len(red_shape)
    return red_shape, [pltpu.VMEM(r.shape, r.dtype) for r in red_shape] + scratch + [pltpu.SemaphoreType.DMA((nres,))]


def _hosted_reduce(step, n_steps, closing, stage_at, operands, results, scratch, has_small):
    nres = len(results)
    sums, rest, fsem = scratch[:nres], scratch[nres:-1], scratch[-1]
    refs = tuple(operands) + tuple(sums) + tuple(rest)

    def to_results():
        out = [pltpu.make_async_copy(sums[k], results[k], fsem.at[k]) for k in range(nres)]
        for cp in out:
            cp.start()
        for cp in out:
            cp.wait()

    stages = _reduce_protocol(*_split_reduce_refs(refs, nres - has_small, has_small))

    def last_stage():
        stages[-1]()
        to_results()

    for at, stage in zip(stage_at, stages[:-1] + [last_stage]):
        if closing == (at == n_steps):
            pl.when(step == min(at, n_steps - 1))(stage)


def reduce_grads(big, name, parts):
    chips, halves, rows_, cols = big.shape
    sub = jax.ShapeDtypeStruct((chips, halves, rows_ // parts, cols), big.dtype)

    def body(b_ref, o_ref, *scratch):
        refs = [b_ref.at[:, :, s] for s in range(parts)] + [o_ref.at[:, s] for s in range(parts)] + list(scratch)
        for stage in _reduce_protocol(*_split_reduce_refs(refs, parts, False)):
            stage()

    _, scratch = _reduce_buffers([sub] * parts, None)
    return pl.pallas_call(
        body, name=name, out_shape=jax.ShapeDtypeStruct((halves, parts, rows_ // parts, cols), F32),
        in_specs=[pl.BlockSpec(memory_space=pl.ANY)], out_specs=pl.BlockSpec(memory_space=pltpu.VMEM),
        scratch_shapes=scratch, compiler_params=pltpu.CompilerParams(vmem_limit_bytes=VMEM_LIMIT),
    )(big.reshape(chips, halves, parts, rows_ // parts, cols))


def adamw_matrices(items):
    plan, total = [], 0
    for w, _, _, _ in items:
        rows_, cols = w.shape
        tr = max(t for t in range(8, rows_ + 1, 8) if rows_ % t == 0 and t * cols * 4 <= ADAM_BLOCK_BYTES)
        plan.append((total, rows_ // tr, tr, cols))
        total += rows_ // tr
    nin = 4 * len(items)

    def body(*refs):
        i = pl.program_id(0)
        for k, (first, steps, _, _) in enumerate(plan):
            w_ref, g_ref, m_ref, v_ref = refs[4 * k:4 * k + 4]
            go_ref, d_ref, nm_ref, nv_ref = refs[nin + 4 * k:nin + 4 * k + 4]

            @pl.when((i >= first) & (i < first + steps))
            def _():
                gv = g_ref[...]
                go_ref[...] = gv
                d_ref[...], nm_ref[...], nv_ref[...] = _adam_update(w_ref[...], gv, m_ref[...], v_ref[...])

    specs, shapes = [], []
    for (first, steps, tr, cols), (w, _, _, _) in zip(plan, items):
        spec = pl.BlockSpec((tr, cols), lambda i, first=first, steps=steps: (jnp.clip(i - first, 0, steps - 1), 0))
        specs += [spec] * 4
        shapes += [jax.ShapeDtypeStruct(w.shape, F32)] * 4
    res = pl.pallas_call(
        body, name="adamw_matrices", grid=(total,), out_shape=tuple(shapes), in_specs=specs, out_specs=tuple(specs),
        compiler_params=pltpu.CompilerParams(dimension_semantics=("arbitrary",)),
    )(*[a for item in items for a in item])
    return [res[4 * k:4 * k + 4] for k in range(len(items))]


def _adam_update(w, g, m, v):
    nm = ADAM_B1 * m + (1.0 - ADAM_B1) * g
    nv = ADAM_B2 * v + (1.0 - ADAM_B2) * (g * g)
    m_hat = nm / (1.0 - ADAM_B1 ** ADAM_STEP)
    v_hat = nv / (1.0 - ADAM_B2 ** ADAM_STEP)
    return (-ADAM_LR) * (m_hat / (jnp.sqrt(v_hat) + ADAM_EPS) + ADAM_WD * w), nm, nv


def adamw_vectors(g_pack, ws, ms, vs):
    nvec = len(SMALL_VECTORS)
    n = nvec + len(SMALL_MATRICES)

    def body(*refs):
        pk = refs[0]
        w_refs, m_refs, v_refs = (refs[1 + k * n:1 + (k + 1) * n] for k in range(3))
        g_out, d_out, nm_out, nv_out = (refs[1 + (3 + k) * n:1 + (4 + k) * n] for k in range(4))
        refs[-1][...] = pk[LOSS_ROW:LOSS_ROW + 1, 0:1]
        chip = 2 * lax.axis_index("x") + lax.axis_index("y")
        for k, (name, row, width) in enumerate(SMALL_VECTORS):
            if name == "conv_w":
                g = jnp.concatenate([pk[pl.ds(row + 4 * t + chip, 1), :] for t in range(CONV_K)], axis=0)[None]
            elif width >= 128:
                g = jnp.concatenate([pk[row + r:row + r + 1, :] for r in range(width // 128)], axis=1)
            else:
                g = pk[row:row + 1, 0:width]
            g_out[k][...] = g
            d_out[k][...], nm_out[k][...], nv_out[k][...] = _adam_update(w_refs[k][...], g, m_refs[k][...], v_refs[k][...])
        for k in range(nvec, n):
            for b in range(LRU_BLOCKS):
                rows_ = pk[GATES_ROW + HEAD * b:GATES_ROW + HEAD * (b + 1), :]
                g = (pltpu.roll(rows_, HEAD, axis=1) if k > nvec else rows_)[:, 0:HEAD]
                g_out[k][0, b] = g
                d_out[k][0, b], nm_out[k][0, b], nv_out[k][0, b] = _adam_update(
                    w_refs[k][0, b], g, m_refs[k][0, b], v_refs[k][0, b])

    vm = pl.BlockSpec(memory_space=pltpu.VMEM)
    like = [jax.ShapeDtypeStruct(w.shape, F32) for w in ws]
    return pl.pallas_call(
        body, name="adamw_vectors", out_shape=(*like * 4, jax.ShapeDtypeStruct((1, 1), F32)),
        in_specs=[vm] * (1 + 3 * n), out_specs=(vm,) * (4 * n + 1),
    )(g_pack, *ws, *ms, *vs)


SMALL_VECTORS = (("norm_g", 0, 1024), ("mem_norm_g", 8, 1024), ("conv_w", 16, 512), ("conv_b", 32, 512),
                 ("b_rg", 36, 512), ("b_ig", 40, 512), ("lru_lambda", 44, 512), ("q_norm_g", 48, 64),
                 ("k_norm_g", 49, 64), ("sinks", 50, 4), ("xq_norm_g", 51, 64), ("xk_norm_g", 52, 64),
                 ("out_norm_g", 53, 1024))
LOSS_ROW = 61
SMALL_MATRICES = ("w_rg", "w_ig")
GATES_ROW = 64
SMALL_ROWS = GATES_ROW + LRU_BLOCKS * HEAD


def _rope_tables(seq):
    pos = np.arange(seq, dtype=np.float32)
    inv_freq = (np.float32(ROPE_THETA) ** (-(np.arange(0, ROPE_DIM, 2, dtype=np.float32) / np.float32(ROPE_DIM)))
                ).astype(np.float32)
    ang = (pos[:, None] * inv_freq[None, :]).astype(np.float32)
    cos, sin = np.cos(ang).astype(np.float32), np.sin(ang).astype(np.float32)
    z = lambda n: np.zeros((seq, n), np.float32)
    c64 = np.concatenate([cos, cos, np.ones((seq, HEAD - ROPE_DIM), np.float32)], axis=1)
    s1_64 = np.concatenate([-sin, z(HEAD - 8)], axis=1)
    s2_64 = np.concatenate([z(8), sin, z(HEAD - ROPE_DIM)], axis=1)
    return tuple(jnp.asarray(np.concatenate([t, t], axis=1)) for t in (c64, s1_64, s2_64))


def kernel(x, mem, norm_g, mem_norm_g, w_in, conv_w, conv_b, w_rg, b_rg, w_ig, b_ig, lru_lambda, q_norm_g, k_norm_g, sinks, w_mem_kv, xq_norm_g, xk_norm_g, out_norm_g, w_out, loss_target, m_norm_g, m_mem_norm_g, m_w_in, m_conv_w, m_conv_b, m_w_rg, m_b_rg, m_w_ig, m_b_ig, m_lru_lambda, m_q_norm_g, m_k_norm_g, m_sinks, m_w_mem_kv, m_xq_norm_g, m_xk_norm_g, m_out_norm_g, m_w_out, v_norm_g, v_mem_norm_g, v_w_in, v_conv_w, v_conv_b, v_w_rg, v_b_rg, v_w_ig, v_b_ig, v_lru_lambda, v_q_norm_g, v_k_norm_g, v_sinks, v_w_mem_kv, v_xq_norm_g, v_xk_norm_g, v_out_norm_g, v_w_out):
    seq = x.shape[1]
    xs, tgt, mems = x[0], loss_target[0], mem[0]

    win_t, wout, wkv, cw, wg, gains = gather_weights(w_in[0].T, w_out[0], w_mem_kv[0], conv_w, w_rg, w_ig,
                                                     (q_norm_g, k_norm_g, xq_norm_g, xk_norm_g))
    rc, rs1, rs2 = _rope_tables(seq)

    km, vm = mem_fwd(mems, mem_norm_g, wkv, gains)
    proj, ya, yb, yc, ycat, xn, dout, pswa, pmem, psink, gates, a_all, loss8 = layer_fwd(
        xs, tgt, rc, rs1, rs2, norm_g, win_t, cw, conv_b, wg, b_rg, b_ig, lru_lambda, gains, sinks, km, vm,
        out_norm_g, wout)
    (gx, dproj, g_wg, dkm, dvm, g_ng, g_og, g_cb, g_brg, g_big, g_lam, g_cw, g_qn, g_kn, g_xqn, g_sink) = layer_bwd(
        xs, dout, proj, ya, yb, yc, pswa, pmem, psink, gates, a_all, rc, rs1, rs2, norm_g, win_t, cw, wg, lru_lambda,
        gains, km, vm, out_norm_g, wout)
    g_wkv, small_g = mem_bwd(mems, mem_norm_g, wkv, gains, dkm, dvm, g_wg, loss8, dict(
        norm_g=g_ng, conv_w=g_cw, conv_b=g_cb, b_rg=g_brg, b_ig=g_big, lru_lambda=g_lam, q_norm_g=g_qn, k_norm_g=g_kn,
        sinks=g_sink, xq_norm_g=g_xqn, out_norm_g=g_og))
    g_win_t, r_out, r_kv, r_small = weight_grads(
        ycat, dout, dproj, xn, g_wkv.reshape(N_CHIPS, 2, D_MODEL // 8, 2 * XATT_W),
        small_g.reshape(2, SMALL_ROWS // 2, 128), (0, 1, 4, 7, 8), (4, 5, 10, 12, 13))
    r_in = reduce_grads(g_win_t.reshape(N_CHIPS, 2, D_IN // 8, D_MODEL), "reduce_w_in", 6)

    r_small = r_small.reshape(SMALL_ROWS, 128)
    grads = {}
    weights = dict(norm_g=norm_g, mem_norm_g=mem_norm_g, w_in=w_in, conv_w=conv_w, conv_b=conv_b, w_rg=w_rg, b_rg=b_rg,
                   w_ig=w_ig, b_ig=b_ig, lru_lambda=lru_lambda, q_norm_g=q_norm_g, k_norm_g=k_norm_g, sinks=sinks,
                   w_mem_kv=w_mem_kv, xq_norm_g=xq_norm_g, xk_norm_g=xk_norm_g, out_norm_g=out_norm_g, w_out=w_out)
    ms = dict(norm_g=m_norm_g, mem_norm_g=m_mem_norm_g, w_in=m_w_in, conv_w=m_conv_w, conv_b=m_conv_b, w_rg=m_w_rg,
              b_rg=m_b_rg, w_ig=m_w_ig, b_ig=m_b_ig, lru_lambda=m_lru_lambda, q_norm_g=m_q_norm_g, k_norm_g=m_k_norm_g,
              sinks=m_sinks, w_mem_kv=m_w_mem_kv, xq_norm_g=m_xq_norm_g, xk_norm_g=m_xk_norm_g,
              out_norm_g=m_out_norm_g, w_out=m_w_out)
    vs = dict(norm_g=v_norm_g, mem_norm_g=v_mem_norm_g, w_in=v_w_in, conv_w=v_conv_w, conv_b=v_conv_b, w_rg=v_w_rg,
              b_rg=v_b_rg, w_ig=v_w_ig, b_ig=v_b_ig, lru_lambda=v_lru_lambda, q_norm_g=v_q_norm_g, k_norm_g=v_k_norm_g,
              sinks=v_sinks, w_mem_kv=v_w_mem_kv, xq_norm_g=v_xq_norm_g, xk_norm_g=v_xk_norm_g,
              out_norm_g=v_out_norm_g, w_out=v_w_out)

    delta, new_m, new_v = {}, {}, {}
    res_in, res_out, res_kv = adamw_matrices([
        (w_in[0].T, r_in.reshape(D_IN // 4, D_MODEL), m_w_in[0].T, v_w_in[0].T),
        (w_out[0], r_out.reshape(D_MODEL // 4, D_MODEL), m_w_out[0], v_w_out[0]),
        (w_mem_kv[0], r_kv.reshape(D_MODEL // 4, 2 * XATT_W), m_w_mem_kv[0], v_w_mem_kv[0])])
    grads["w_in"], delta["w_in"], new_m["w_in"], new_v["w_in"] = (r.T[None] for r in res_in)
    grads["w_out"], delta["w_out"], new_m["w_out"], new_v["w_out"] = (r[None] for r in res_out)
    grads["w_mem_kv"], delta["w_mem_kv"], new_m["w_mem_kv"], new_v["w_mem_kv"] = (r[None] for r in res_kv)
    small_names = [n for n, _, _ in SMALL_VECTORS] + list(SMALL_MATRICES)
    res = adamw_vectors(r_small, [weights[n] for n in small_names], [ms[n] for n in small_names],
                        [vs[n] for n in small_names])
    nall = len(small_names)
    for k, into in enumerate((grads, delta, new_m, new_v)):
        into.update(zip(small_names, res[k * nall:(k + 1) * nall]))
    loss = res[-1].reshape(())

    order = ("norm_g", "mem_norm_g", "w_in", "conv_w", "conv_b", "w_rg", "b_rg", "w_ig", "b_ig", "lru_lambda",
             "q_norm_g", "k_norm_g", "sinks", "w_mem_kv", "xq_norm_g", "xk_norm_g", "out_norm_g", "w_out")
    return (loss, gx[None], *[grads[n] for n in order], *[delta[n] for n in order], *[new_m[n] for n in order],
            *[new_v[n] for n in order])
```

```python
import jax
import jax.numpy as jnp
import numpy as np
from jax import lax
from jax.experimental import pallas as pl
from jax.experimental.pallas import tpu as pltpu

F32 = jnp.float32
_MXU = jnp.bfloat16
_WIRE = jnp.bfloat16

D_MODEL = 1024
MEM_LEN = 256
HEAD = 64
LRU_W = 512
LRU_BLOCKS = 8
CONV_K = 4
LRU_C = 8.0
SWA_W = 256
KV_W = 128
XATT_W = 256
BLOCK = 128
D_IN = 2304
ROPE_THETA = 500000.0
ROPE_DIM = 16
EPS = 1e-6
NEG_INF = -1e30
C_LRUX, C_LRUG, C_SQ, C_SK, C_SV, C_SWAG, C_XQ, C_XG = 0, 512, 1024, 1280, 1408, 1536, 1792, 2048
G_Q, G_K, G_XQ, G_XK, GAINS_W = 0, 128, 256, 512, 768

ADAM_LR, ADAM_B1, ADAM_B2, ADAM_EPS, ADAM_WD, ADAM_STEP = 0.001, 0.9, 0.999, 1e-08, 0.01, 10

N_CHIPS = 4
ROW_TILE = 256
VMEM_LIMIT = 56 * 1024 * 1024
ADAM_BLOCK_BYTES = 640 * 1024
MESH = pl.DeviceIdType.MESH


def _mm(a, b):
    return jnp.dot(a.astype(_MXU), b.astype(_MXU), preferred_element_type=F32)


def _mm_nt(a, b):
    return lax.dot_general(a.astype(_MXU), b.astype(_MXU), (((1,), (1,)), ((), ())), preferred_element_type=F32)


def _mm_tn(a, b):
    return lax.dot_general(a.astype(_MXU), b.astype(_MXU), (((0,), (0,)), ((), ())), preferred_element_type=F32)


def _group_matrix(width):
    r = lax.shift_right_logical(lax.broadcasted_iota(jnp.int32, (width, width), 0), 6)
    c = lax.shift_right_logical(lax.broadcasted_iota(jnp.int32, (width, width), 1), 6)
    return (r == c).astype(_MXU)


def _seg_mean(x, gm):
    return jnp.dot(x.astype(_MXU), gm, preferred_element_type=F32) * (1.0 / HEAD)


def _row_mean(x):
    return jnp.mean(x, axis=-1, keepdims=True)


def _col_sum(x):
    return jnp.sum(x, axis=0, keepdims=True)


def _sigmoid(x):
    return jax.nn.sigmoid(x)


def _softplus(z):
    e = jnp.exp(-jnp.abs(z))
    u = 1.0 + e
    log1p_e = jnp.where(u == 1.0, e, jnp.log(u) * (e / (u - 1.0)))
    return jnp.maximum(z, 0.0) + log1p_e


def _rope(t, c, s1, s2):
    return t * c + pltpu.roll(t, 120, 1) * s1 + pltpu.roll(t, 8, 1) * s2


def _rope_bwd(d, c, s1, s2):
    return d * c + pltpu.roll(d * s1, 8, 1) + pltpu.roll(d * s2, 120, 1)


def _fold_heads(v):
    out = v
    for k in range(1, v.shape[1] // HEAD):
        out = out + pltpu.roll(v, HEAD * k, 1)
    return out


def _lane_mask(width, lo, hi):
    lane = lax.broadcasted_iota(jnp.int32, (1, width), 1)
    return ((lane >= lo) & (lane < hi)).astype(F32)


def _swa_mask(first_block):
    qi = lax.broadcasted_iota(jnp.int32, (BLOCK, 2 * BLOCK), 0)
    kj = lax.broadcasted_iota(jnp.int32, (BLOCK, 2 * BLOCK), 1)
    rel = qi + BLOCK - kj
    ok = (rel >= 0) & (rel < BLOCK)
    return ok & (jnp.logical_not(first_block) | (kj >= BLOCK))


def _place_kv(t, scale):
    lo = t * (_lane_mask(KV_W, 0, HEAD) * scale)
    hi = t * (_lane_mask(KV_W, HEAD, KV_W) * scale)
    return [a.astype(_MXU) for a in (lo, pltpu.roll(lo, HEAD, 1), pltpu.roll(hi, HEAD, 1), hi)]


def _unplace_kv(d):
    return (_lane_mask(KV_W, 0, HEAD) * (d[0] + pltpu.roll(d[1], HEAD, 1))
            + _lane_mask(KV_W, HEAD, KV_W) * (d[3] + pltpu.roll(d[2], HEAD, 1)))


def _swa_probs(qh, ka, mask, sink):
    s = _mm_nt(qh, ka)
    s = jnp.where(mask, s, NEG_INF)
    m = jnp.maximum(jnp.max(s, axis=-1, keepdims=True), sink)
    p = jnp.exp(s - m)
    esink = jnp.exp(sink - m)
    inv = 1.0 / (jnp.sum(p, axis=-1, keepdims=True) + esink)
    return p * inv, esink * inv


def _mem_probs(s_all):
    out = []
    for j in range(4):
        s = s_all[:, MEM_LEN * j:MEM_LEN * (j + 1)]
        p = jnp.exp(s - jnp.max(s, axis=-1, keepdims=True))
        out.append(p * (1.0 / jnp.sum(p, axis=-1, keepdims=True)))
    return out


def _head_rows(t, scale):
    return jnp.concatenate([t * (_lane_mask(XATT_W, HEAD * j, HEAD * (j + 1)) * scale) for j in range(4)], axis=0)


def _lru_gates(xc, wg_ref, brg, big, lam):
    p0 = _mm(xc[:, :256], wg_ref[0])
    p1 = _mm(xc[:, 256:], wg_ref[1])
    rg = _sigmoid(jnp.concatenate([p0[:, :256], p1[:, :256]], axis=1) + brg)
    ig = _sigmoid(jnp.concatenate([p0[:, 256:], p1[:, 256:]], axis=1) + big)
    sp = _softplus(-lam)
    la = (-LRU_C) * rg * sp
    a = jnp.exp(la)
    th = jnp.tanh(la)
    one_minus_a2 = (-2.0 * th) / (1.0 - th)
    return rg, ig, sp, a, jnp.sqrt(one_minus_a2)


def _const_spec(shape, single=False):
    zeros = (0,) * len(shape)
    if single:
        return pl.BlockSpec(shape, lambda i: zeros, pipeline_mode=pl.Buffered(1))
    return pl.BlockSpec(shape, lambda i: zeros)


def _chip_of(x, y):
    return 2 * x + y


def _partners(x, y, c):
    north = c == 1
    near = (jnp.where(north, 1 - x, x), jnp.where(north, y, 1 - y))
    far = (jnp.where(north, x, 1 - x), jnp.where(north, 1 - y, y))
    return near, far, (1 - x, 1 - y)


def gather_weights(win_t, wout, wkv, conv_w, w_rg, w_ig, head_gains):
    arrs = (win_t, wout, wkv)
    n = len(arrs)
    pieces = [(a, k * (arr.shape[0] // (2 * cut)), arr.shape[0] // (2 * cut))
              for a, (arr, cut) in enumerate(zip(arrs, (2, 1, 1))) for k in range(cut)]
    npc = len(pieces)

    def body(a0, a1, a2, cw_in, wrg_ref, wig_ref, q_ref, k_ref, xq_ref, xk_ref, o0, o1, o2, cw_out, wg_ref, gn_ref,
             s0, s1, s2, cw, ocw, send, recv, lsem):
        ins, outs = (s0, s1, s2), (o0, o1, o2)
        for src, dst in zip((a0, a1, a2), ins):
            dst[...] = src[...].astype(dst.dtype)
        cw[...] = jnp.zeros(cw.shape, F32)
        cw[0:CONV_K, :] = cw_in[0]
        x, y, c = lax.axis_index("x"), lax.axis_index("y"), lax.axis_index("c")
        sibling = (x, y, 1 - c)
        near, far, diag = _partners(x, y, c)
        chips = [near, far, diag]
        me = _chip_of(x, y)

        def landed(p, chip, half):
            a, off, rows_ = pieces[p]
            r = ins[a].shape[0]
            return outs[a].at[pl.ds(pl.multiple_of(chip * r + half * (r // 2) + off, 16), rows_)]

        def mine(p):
            a, off, rows_ = pieces[p]
            return ins[a].at[pl.ds(pl.multiple_of(c * (ins[a].shape[0] // 2) + off, 16), rows_)]

        def copy(k, src, dst, to):
            return pltpu.make_async_remote_copy(src_ref=src, dst_ref=dst, send_sem=send.at[k], recv_sem=recv.at[k],
                                                device_id=to, device_id_type=MESH)

        def cw_rows(chip):
            return ocw.at[pl.ds(pl.multiple_of(chip * 8, 8), 8)]

        locals_ = []
        for a in range(n):
            r = ins[a].shape[0]
            locals_.append(pltpu.make_async_copy(ins[a], outs[a].at[pl.ds(pl.multiple_of(me * r, 16), r)], lsem.at[a]))
        locals_.append(pltpu.make_async_copy(cw, cw_rows(me), lsem.at[n]))
        for cp in locals_:
            cp.start()

        sent = []
        for p in range(npc):
            for j in range(2):
                sent.append(copy(p * 6 + j, mine(p), landed(p, me, c), (*chips[j], c)))
        for j, chip in enumerate(chips):
            sent.append(copy(npc * 6 + j, cw, cw_rows(me), (*chip, c)))
        for cp in sent:
            cp.start()

        gn_ref[...] = jnp.concatenate([q_ref[...]] * 2 + [k_ref[...]] * 2 + [xq_ref[...]] * 4 + [xk_ref[...]] * 4,
                                      axis=1)
        zeros = lambda lanes: [jnp.zeros((HEAD, lanes), F32)] if lanes else []
        for h in range(2):
            for b in range(4):
                row = []
                for w_ref in (wrg_ref, wig_ref):
                    row += zeros(HEAD * b) + [w_ref[0, 4 * h + b]] + zeros(HEAD * (3 - b))
                wg_ref[h, HEAD * b:HEAD * (b + 1), :] = jnp.concatenate(row, axis=1).astype(wg_ref.dtype)

        for j in range(3):
            for p in range(npc):
                got = landed(p, _chip_of(*chips[j]), c)
                copy(p * 6 + j, got, got, sibling).wait_recv()
                if j == 0:
                    sent.append(copy(p * 6 + 2, got, got, (*far, c)))
                    sent[-1].start()
                sent.append(copy(p * 6 + 3 + j, got, got, sibling))
                sent[-1].start()
        for p in range(npc):
            for j in range(3):
                got = landed(p, _chip_of(*chips[(1, 0, 2)[j]]), 1 - c)
                copy(p * 6 + 3 + j, got, got, sibling).wait_recv()
        for j, chip in enumerate(chips):
            got = cw_rows(_chip_of(*chip))
            copy(npc * 6 + j, got, got, (*chip, c)).wait_recv()
        for cp in sent:
            cp.wait_send()
        for cp in locals_:
            cp.wait()
        for chip in range(N_CHIPS):
            cw_out[:, 128 * chip:128 * (chip + 1)] = ocw[8 * chip:8 * chip + CONV_K, :]

    vm = pl.BlockSpec(memory_space=pltpu.VMEM)
    out_shape = tuple(jax.ShapeDtypeStruct((N_CHIPS * a.shape[0],) + a.shape[1:], _MXU) for a in arrs) + (
        jax.ShapeDtypeStruct((CONV_K, LRU_W), F32), jax.ShapeDtypeStruct((2, 256, 512), _MXU),
        jax.ShapeDtypeStruct((1, GAINS_W), F32))
    n_rdma = npc * 6 + 3
    return pl.pallas_call(
        body, name="gather_weights", out_shape=out_shape,
        in_specs=[vm] * 10, out_specs=(pl.BlockSpec(memory_space=pl.ANY),) * n + (vm, vm, vm),
        scratch_shapes=[pltpu.VMEM(a.shape, _MXU) for a in arrs] + [
            pltpu.VMEM((8, 128), F32), pltpu.VMEM((N_CHIPS * 8, 128), F32),
            pltpu.SemaphoreType.DMA((n_rdma,)), pltpu.SemaphoreType.DMA((n_rdma,)), pltpu.SemaphoreType.DMA((n + 1,))],
        compiler_params=pltpu.CompilerParams(vmem_limit_bytes=VMEM_LIMIT),
    )(win_t, wout, wkv, conv_w, w_rg, w_ig, *head_gains)


def mem_fwd(mem, mem_g, wkv, gains):
    def body(mem_ref, g_ref, w_ref, gn_ref, km_ref, vm_ref):
        mem_v = mem_ref[...]
        mn = mem_v * lax.rsqrt(_row_mean(mem_v * mem_v) + EPS) * g_ref[...]
        mkv = _mm(mn, w_ref[...])
        kpre = mkv[:, :XATT_W]
        gm = _group_matrix(XATT_W)
        km = kpre * lax.rsqrt(_seg_mean(kpre * kpre, gm) + EPS) * gn_ref[:, G_XK:GAINS_W]
        km_ref[...] = _head_rows(km, 0.125).astype(km_ref.dtype)
        vm_ref[...] = _head_rows(mkv[:, XATT_W:], 1.0).astype(vm_ref.dtype)

    vm = pl.BlockSpec(memory_space=pltpu.VMEM)
    rows_shape = jax.ShapeDtypeStruct((4 * MEM_LEN, XATT_W), _MXU)
    return pl.pallas_call(
        body, name="mem_fwd", out_shape=(rows_shape, rows_shape), in_specs=[vm] * 4, out_specs=(vm, vm),
    )(mem, mem_g, wkv, gains)


def mem_bwd(mem, mem_g, wkv, gains, dkm, dvm, g_gates, loss8, vectors):
    names = tuple(vectors)
    first_row = {name: (row, width) for name, row, width in SMALL_VECTORS}

    def body(mem_ref, g_ref, w_ref, gn_ref, dkm_ref, dvm_ref, gg_ref, loss_ref, *rest):
        vec_refs, (gw_ref, pk_ref) = rest[:len(names)], rest[len(names):]
        pk_ref[...] = jnp.zeros(pk_ref.shape, F32)

        def put(name, src):
            row, width = first_row[name]
            per_row = 1 if width < 128 else src.shape[1] // 128
            for t in range(src.shape[0]):
                for r in range(per_row):
                    at = row + per_row * t + r
                    pk_ref[at:at + 1, :] = src[t:t + 1, 128 * r:128 * (r + 1)]

        for name, ref in zip(names, vec_refs):
            put(name, ref)
        pk_ref[LOSS_ROW:LOSS_ROW + 1, :] = loss_ref[0:1, :]
        upper = lax.broadcasted_iota(jnp.int32, (HEAD, 128), 1) >= HEAD
        for h in range(2):
            for b in range(4):
                rg = gg_ref[h, HEAD * b:HEAD * (b + 1), 128 * (b // 2):128 * (b // 2 + 1)]
                ig = gg_ref[h, HEAD * b:HEAD * (b + 1), 256 + 128 * (b // 2):256 + 128 * (b // 2 + 1)]
                if b % 2:
                    rg = pltpu.roll(rg, HEAD, axis=1)
                else:
                    ig = pltpu.roll(ig, HEAD, axis=1)
                at = GATES_ROW + HEAD * (4 * h + b)
                pk_ref[at:at + HEAD, :] = jnp.where(upper, ig, rg)

        mem_v = mem_ref[...]
        mh = mem_v * lax.rsqrt(_row_mean(mem_v * mem_v) + EPS)
        mn = mh * g_ref[...]
        mkv = _mm(mn, w_ref[...])
        kpre = mkv[:, :XATT_W]
        gm = _group_matrix(XATT_W)
        rk = lax.rsqrt(_seg_mean(kpre * kpre, gm) + EPS)
        kn = kpre * rk
        dk = jnp.zeros((MEM_LEN, XATT_W), F32)
        dv = jnp.zeros((MEM_LEN, XATT_W), F32)
        for j in range(4):
            mj = _lane_mask(XATT_W, HEAD * j, HEAD * (j + 1))
            dk = dk + dkm_ref[:, MEM_LEN * j:MEM_LEN * (j + 1)].T * (mj * 0.125)
            dv = dv + dvm_ref[:, MEM_LEN * j:MEM_LEN * (j + 1)].T * mj
        put("xk_norm_g", _fold_heads(_col_sum(dk * kn)))
        dkn = dk * gn_ref[:, G_XK:GAINS_W]
        dkpre = rk * (dkn - kn * _seg_mean(dkn * kn, gm))
        dmkv = jnp.concatenate([dkpre, dv], axis=1)
        gw_ref[...] = _mm_tn(mn, dmkv)
        dmn = _mm_nt(dmkv, w_ref[...])
        put("mem_norm_g", _col_sum(dmn * mh))

    vm = pl.BlockSpec(memory_space=pltpu.VMEM)
    return pl.pallas_call(
        body, name="mem_bwd",
        out_shape=(jax.ShapeDtypeStruct((D_MODEL, 2 * XATT_W), F32), jax.ShapeDtypeStruct((SMALL_ROWS, 128), F32)),
        in_specs=[vm] * (8 + len(names)), out_specs=(vm, vm),
    )(mem, mem_g, wkv, gains, dkm, dvm, g_gates, loss8, *vectors.values())


def layer_fwd(x, tgt, rc, rs1, rs2, ng, win_t, cw, cb, wg, brg, big, lam, gains, sinks, km, vm, og, wout):
    seq = x.shape[0]
    tm = min(ROW_TILE, seq)
    nt = seq // tm
    nb = tm // BLOCK

    def body(x_ref, t_ref, c_ref, s1_ref, s2_ref, ng_ref, win_ref, cw_ref, cb_ref, wg_ref, brg_ref, big_ref, lam_ref,
             gn_ref, sink_ref, km_ref, vm_ref, og_ref, wout_ref,
             proj_ref, ya_ref, yb_ref, yc_ref, ycat_ref, xn_ref, dout_ref, pswa_ref, pmem_ref, psink_ref, gates_ref,
             a_ref, loss_ref,
             ext_ref, b_scr, hc_ref, kp_ref, vp_ref, lacc_ref):
        i = pl.program_id(0)

        @pl.when(i == 0)
        def _():
            ext_ref[0:8, :] = jnp.zeros((8, LRU_W), F32)
            hc_ref[...] = jnp.zeros_like(hc_ref)
            kp_ref[...] = jnp.zeros_like(kp_ref)
            vp_ref[...] = jnp.zeros_like(vp_ref)
            lacc_ref[...] = jnp.zeros_like(lacc_ref)

        xv = x_ref[...]
        xn = (xv * lax.rsqrt(_row_mean(xv * xv) + EPS) * ng_ref[...]).astype(_MXU)
        xn_ref[...] = xn.astype(xn_ref.dtype)
        proj_ref[...] = _mm_nt(xn, win_ref[...])

        u = proj_ref[:, C_LRUX:C_LRUX + LRU_W]
        ext_ref[8:8 + tm, :] = u
        xc = cb_ref[...]
        for k in range(CONV_K):
            xc = xc + cw_ref[k:k + 1, :] * ext_ref[pl.ds(5 + k, tm), :]
        ext_ref[0:8, :] = u[tm - 8:tm, :]
        rg, ig, sp, a, sq = _lru_gates(xc, wg_ref, brg_ref[...], big_ref[...], lam_ref[...])
        for k, t in enumerate((xc, rg, ig, sq)):
            gates_ref[:, LRU_W * k:LRU_W * (k + 1)] = t
        a_ref[...] = a
        b_scr[...] = sq * (ig * xc)
        row8 = lax.broadcasted_iota(jnp.int32, (8, LRU_W), 0)

        def scan_step(g, carry):
            r0 = pl.multiple_of(g * 8, 8)
            av = a_ref[pl.ds(r0, 8), :]
            bv = b_scr[pl.ds(r0, 8), :]
            for d in (1, 2, 4):
                a_sh = jnp.where(row8 >= d, pltpu.roll(av, d, 0), 1.0)
                b_sh = jnp.where(row8 >= d, pltpu.roll(bv, d, 0), 0.0)
                bv = bv + av * b_sh
                av = av * a_sh
            hv = bv + av * carry
            ya_ref[pl.ds(r0, 8), :] = hv
            return hv[7:8, :]

        hc_ref[0:1, :] = lax.fori_loop(0, tm // 8, scan_step, hc_ref[0:1, :], unroll=True)

        gm128 = _group_matrix(KV_W)
        cv, s1v, s2v = c_ref[...], s1_ref[...], s2_ref[...]

        def head_norm_rope(t, g):
            n = t * lax.rsqrt(_seg_mean(t * t, gm128) + EPS)
            return _rope(n * g, cv, s1v, s2v)

        qs_ = (head_norm_rope(proj_ref[:, C_SQ:C_SQ + 128], gn_ref[:, G_Q:G_K]).astype(_MXU),
               head_norm_rope(proj_ref[:, C_SQ + 128:C_SQ + 256], gn_ref[:, G_Q:G_K]).astype(_MXU))
        kr = head_norm_rope(proj_ref[:, C_SK:C_SK + KV_W], gn_ref[:, G_K:G_XQ])
        sv = proj_ref[:, C_SV:C_SV + KV_W]
        ka = _place_kv(jnp.concatenate([kp_ref[...], kr], axis=0), 0.125)
        va = _place_kv(jnp.concatenate([vp_ref[...], sv], axis=0), 1.0)
        kp_ref[...] = kr[tm - BLOCK:tm, :]
        vp_ref[...] = sv[tm - BLOCK:tm, :]
        lane128 = lax.broadcasted_iota(jnp.int32, (1, 128), 1)
        for b in range(nb):
            mask = _swa_mask((i == 0) & (b == 0)) if b == 0 else _swa_mask(False)
            band = slice(BLOCK * b, BLOCK * b + 2 * BLOCK)
            blk = slice(BLOCK * b, BLOCK * (b + 1))
            psink = jnp.zeros((BLOCK, 128), F32)
            for j in range(4):
                p, pk = _swa_probs(qs_[j // 2][blk], ka[j][band], mask, sink_ref[0, j])
                pswa_ref[blk, 2 * BLOCK * j:2 * BLOCK * (j + 1)] = p.astype(pswa_ref.dtype)
                psink = jnp.where(lane128 == j, pk, psink)
            psink_ref[blk, :] = psink
            for h in range(2):
                yb_ref[blk, KV_W * h:KV_W * (h + 1)] = _mm(
                    pswa_ref[blk, 4 * BLOCK * h:4 * BLOCK * (h + 1)],
                    jnp.concatenate([va[2 * h][band], va[2 * h + 1][band]], axis=0))

        gm256 = _group_matrix(XATT_W)
        xq = proj_ref[:, C_XQ:C_XQ + XATT_W]
        qx = xq * lax.rsqrt(_seg_mean(xq * xq, gm256) + EPS) * gn_ref[:, G_XQ:G_XK]
        pm = _mem_probs(_mm_nt(qx, km_ref[...]))
        for j in range(4):
            pmem_ref[:, MEM_LEN * j:MEM_LEN * (j + 1)] = pm[j].astype(pmem_ref.dtype)
        yc = _mm(pmem_ref[...], vm_ref[...])
        yc_ref[...] = yc

        def gated(y, g, gate):
            return y * lax.rsqrt(_row_mean(y * y) + EPS) * g * (gate * _sigmoid(gate))

        ogv = og_ref[...]
        za = gated(ya_ref[...], ogv[:, :512], proj_ref[:, C_LRUG:C_LRUG + LRU_W])
        zb = gated(yb_ref[...], ogv[:, 512:768], proj_ref[:, C_SWAG:C_SWAG + SWA_W])
        zc = gated(yc, ogv[:, 768:], proj_ref[:, C_XG:C_XG + XATT_W])
        ycat_ref[:, 0:512] = za.astype(ycat_ref.dtype)
        ycat_ref[:, 512:768] = zb.astype(ycat_ref.dtype)
        ycat_ref[:, 768:1024] = zc.astype(ycat_ref.dtype)
        out = xv + _mm(ycat_ref[...], wout_ref[...])
        err = out - t_ref[...]
        dout_ref[...] = (err * (1.0 / D_MODEL)).astype(dout_ref.dtype)
        lacc_ref[...] = lacc_ref[...] + (0.5 / D_MODEL) * jnp.sum(err * err)

        @pl.when(i == nt - 1)
        def _():
            loss_ref[...] = lacc_ref[...]

    def rows(ncol):
        return pl.BlockSpec((tm, ncol), lambda i: (i, 0))

    in_specs = [rows(D_MODEL), rows(D_MODEL), rows(128), rows(128), rows(128),
                _const_spec((1, D_MODEL)), _const_spec((D_IN, D_MODEL), True), _const_spec((CONV_K, LRU_W)),
                _const_spec((1, LRU_W)), _const_spec((2, 256, 512), True), _const_spec((1, LRU_W)),
                _const_spec((1, LRU_W)), _const_spec((1, LRU_W)), _const_spec((1, GAINS_W)), pl.BlockSpec(memory_space=pltpu.SMEM),
                _const_spec((4 * MEM_LEN, XATT_W), True), _const_spec((4 * MEM_LEN, XATT_W), True),
                _const_spec((1, D_MODEL)), _const_spec((D_MODEL, D_MODEL), True)]
    out_shape = (jax.ShapeDtypeStruct((seq, D_IN), F32), jax.ShapeDtypeStruct((seq, LRU_W), F32),
                 jax.ShapeDtypeStruct((seq, SWA_W), F32), jax.ShapeDtypeStruct((seq, XATT_W), F32),
                 jax.ShapeDtypeStruct((seq, D_MODEL), _MXU), jax.ShapeDtypeStruct((seq, D_MODEL), _MXU),
                 jax.ShapeDtypeStruct((seq, D_MODEL), _MXU), jax.ShapeDtypeStruct((seq, 4 * 2 * BLOCK), _MXU),
                 jax.ShapeDtypeStruct((seq, 4 * MEM_LEN), _MXU), jax.ShapeDtypeStruct((seq, 128), F32),
                 jax.ShapeDtypeStruct((seq, 4 * LRU_W), F32), jax.ShapeDtypeStruct((seq, LRU_W), F32),
                 jax.ShapeDtypeStruct((8, 128), F32))
    out_specs = (rows(D_IN), rows(LRU_W), rows(SWA_W), rows(XATT_W), rows(D_MODEL), rows(D_MODEL), rows(D_MODEL),
                 rows(4 * 2 * BLOCK), rows(4 * MEM_LEN), rows(128), rows(4 * LRU_W), rows(LRU_W),
                 _const_spec((8, 128)))
    scratch = [pltpu.VMEM((tm + 8, LRU_W), F32), pltpu.VMEM((tm, LRU_W), F32),
               pltpu.VMEM((8, LRU_W), F32), pltpu.VMEM((BLOCK, KV_W), F32), pltpu.VMEM((BLOCK, KV_W), F32),
               pltpu.VMEM((8, 128), F32)]
    return pl.pallas_call(
        body, name="layer_fwd", grid=(nt,), out_shape=out_shape, in_specs=in_specs, out_specs=out_specs,
        scratch_shapes=scratch,
        compiler_params=pltpu.CompilerParams(dimension_semantics=("arbitrary",), vmem_limit_bytes=VMEM_LIMIT),
    )(x, tgt, rc, rs1, rs2, ng, win_t, cw, cb, wg, brg, big, lam, gains, sinks, km, vm, og, wout)


def weight_grads(ycat, dout, dproj, xn, g_kv, g_small, early_at, late_at):
    seq, ncol = xn.shape
    blk = 256
    n_out, n_in = ycat.shape[1] // blk, dproj.shape[1] // blk
    assert n_out == N_CHIPS and late_at[0] >= n_out
    g_out = jax.ShapeDtypeStruct((N_CHIPS, 2, blk // 2, dout.shape[1]), F32)
    shape_e, scratch_e = _hosted_reduce_shapes([g_kv], g_small)
    shape_l, scratch_l = _hosted_reduce_shapes([g_out], None)

    def body(l1_ref, r1_ref, l2_ref, r2_ref, kv_ref, sm_ref, o_ref, sum_out, sum_kv, sum_sm, gout_scr, *scratch):
        j = pl.program_id(0)

        def reduce_stages(closing):
            _hosted_reduce(j, n_out + n_in, closing, early_at, (kv_ref, sm_ref), (sum_kv, sum_sm),
                           scratch[:len(scratch_e)], True)
            _hosted_reduce(j, n_out + n_in, closing, late_at, (gout_scr,), (sum_out,), scratch[len(scratch_e):], False)

        reduce_stages(False)

        @pl.when(j < n_out)
        def _():
            gout_scr[j] = _mm_tn(l1_ref[...], r1_ref[...]).reshape(g_out.shape[1:])

        @pl.when(j >= n_out)
        def _():
            o_ref[...] = _mm_tn(l2_ref[...], r2_ref[...])

        reduce_stages(True)

    vm = pl.BlockSpec(memory_space=pltpu.VMEM)
    hbm = pl.BlockSpec(memory_space=pl.ANY)
    return pl.pallas_call(
        body, name="weight_grads", grid=(n_out + n_in,),
        out_shape=(jax.ShapeDtypeStruct((dproj.shape[1], ncol), F32), *shape_l, *shape_e),
        in_specs=[pl.BlockSpec((seq, blk), lambda j: (0, jnp.minimum(j, n_out - 1))), _const_spec(dout.shape, True),
                  pl.BlockSpec((seq, blk), lambda j: (0, jnp.maximum(j - n_out, 0))), _const_spec(xn.shape, True),
                  hbm, vm],
        out_specs=(pl.BlockSpec((blk, ncol), lambda j: (jnp.maximum(j - n_out, 0), 0)), hbm, hbm, hbm),
        scratch_shapes=[pltpu.VMEM(g_out.shape, F32)] + scratch_e + scratch_l,
        compiler_params=pltpu.CompilerParams(dimension_semantics=("arbitrary",), vmem_limit_bytes=VMEM_LIMIT),
    )(ycat, dout, dproj, xn, g_kv, g_small)


def layer_bwd(x, dout, proj, ya, yb, yc, pswa, pmem, psink, gates, a_all, rc, rs1, rs2, ng, win_t, cw, wg, lam, gains,
              km, vm, og, wout):
    seq = x.shape[0]
    tm = min(ROW_TILE, seq)
    nt = seq // tm
    nb = tm // BLOCK

    def body(x_ref, dout_ref, proj_ref, ya_ref, yb_ref, yc_ref, pswa_ref, pmem_ref, psink_ref, gates_ref, a_ref,
             c_ref, s1_ref, s2_ref,
             yah_ref, kvh_ref, ch_ref, s1h_ref, s2h_ref,
             ng_ref, win_ref, cw_ref, wg_ref, lam_ref, gn_ref, km_ref, vm_ref, og_ref, wout_ref,
             gx_ref, dproj_ref, gwg_ref, dkm_ref, dvm_ref, gng_ref, gog_ref, gcb_ref, gbrg_ref, gbig_ref, glam_ref,
             gcw_ref, gqn_ref, gkn_ref, gxqn_ref, gsink_ref,
             hext_ref, aext_ref, an_scr, dh_scr, g_scr, dxc_ext, gcar_ref, dkcar_ref, dvcar_ref):
        i = pl.program_id(0)
        tile = nt - 1 - i
        first_tile = tile == 0

        @pl.when(i == 0)
        def _():
            for r in (gwg_ref, dkm_ref, dvm_ref, gng_ref, gog_ref, gcb_ref, gbrg_ref, gbig_ref, glam_ref, gcw_ref,
                      gqn_ref, gkn_ref, gxqn_ref, gsink_ref, gcar_ref, dkcar_ref, dvcar_ref):
                r[...] = jnp.zeros_like(r)
            dxc_ext[tm:tm + 8, :] = jnp.zeros((8, LRU_W), F32)
            aext_ref[tm:tm + 8, :] = jnp.zeros((8, LRU_W), F32)

        xv = x_ref[...]
        dov = dout_ref[...]
        dz = _mm_nt(dov, wout_ref[...])
        ogv = og_ref[...]

        def group_bwd(y, gate, g, dzg):
            r = lax.rsqrt(_row_mean(y * y) + EPS)
            n = y * r
            sg = _sigmoid(gate)
            dgate = dzg * (n * g) * (sg * (1.0 + gate * (1.0 - sg)))
            dng = dzg * (gate * sg)
            dn = dng * g
            return r * (dn - n * _row_mean(dn * n)), dgate, _col_sum(dng * n)

        dya, dga, goa = group_bwd(ya_ref[...], proj_ref[:, C_LRUG:C_LRUG + LRU_W], ogv[:, :512], dz[:, :512])
        dyb, dgb, gob = group_bwd(yb_ref[...], proj_ref[:, C_SWAG:C_SWAG + SWA_W], ogv[:, 512:768], dz[:, 512:768])
        dyc, dgc, goc = group_bwd(yc_ref[...], proj_ref[:, C_XG:C_XG + XATT_W], ogv[:, 768:], dz[:, 768:])
        gog_ref[...] += jnp.concatenate([goa, gob, goc], axis=1)
        dproj_ref[:, C_LRUG:C_LRUG + LRU_W] = dga.astype(dproj_ref.dtype)
        dproj_ref[:, C_SWAG:C_SWAG + SWA_W] = dgb.astype(dproj_ref.dtype)
        dproj_ref[:, C_XG:C_XG + XATT_W] = dgc.astype(dproj_ref.dtype)

        gm256 = _group_matrix(XATT_W)
        xq = proj_ref[:, C_XQ:C_XQ + XATT_W]
        rq = lax.rsqrt(_seg_mean(xq * xq, gm256) + EPS)
        qn = xq * rq
        qx = qn * gn_ref[:, G_XQ:G_XK]
        qxb = qx.astype(_MXU)
        dycb = dyc.astype(_MXU)
        dp_all = _mm_nt(dycb, vm_ref[...])
        dsm = []
        for j in range(4):
            pj = pmem_ref[:, MEM_LEN * j:MEM_LEN * (j + 1)].astype(F32)
            dp = dp_all[:, MEM_LEN * j:MEM_LEN * (j + 1)]
            dsm.append((pj * (dp - jnp.sum(pj * dp, axis=-1, keepdims=True))).astype(_MXU))
        ds_all = jnp.concatenate(dsm, axis=1)
        dvm_ref[...] += _mm_tn(dycb, pmem_ref[...])
        dkm_ref[...] += _mm_tn(qxb, ds_all)
        dqx = _mm(ds_all, km_ref[...])
        gxqn_ref[...] += _col_sum(dqx * qn)
        dqn = dqx * gn_ref[:, G_XQ:G_XK]
        dproj_ref[:, C_XQ:C_XQ + XATT_W] = (rq * (dqn - qn * _seg_mean(dqn * qn, gm256))).astype(dproj_ref.dtype)

        gm128 = _group_matrix(KV_W)
        cv, s1v, s2v = c_ref[...], s1_ref[...], s2_ref[...]

        def head_norm(t):
            r = lax.rsqrt(_seg_mean(t * t, gm128) + EPS)
            return t * r, r

        qn_, qr_ = zip(head_norm(proj_ref[:, C_SQ:C_SQ + 128]), head_norm(proj_ref[:, C_SQ + 128:C_SQ + 256]))
        qrope = [_rope(qn_[h] * gn_ref[:, G_Q:G_K], cv, s1v, s2v).astype(_MXU) for h in range(2)]
        kn, krr = head_norm(proj_ref[:, C_SK:C_SK + KV_W])
        kr = _rope(kn * gn_ref[:, G_K:G_XQ], cv, s1v, s2v)
        khn, _ = head_norm(kvh_ref[:, 0:KV_W])
        khr = _rope(khn * gn_ref[:, G_K:G_XQ], ch_ref[...], s1h_ref[...], s2h_ref[...])
        ka = _place_kv(jnp.concatenate([khr, kr], axis=0), 0.125)
        va = _place_kv(jnp.concatenate([kvh_ref[:, KV_W:2 * KV_W], proj_ref[:, C_SV:C_SV + KV_W]], axis=0), 1.0)
        lane128 = lax.broadcasted_iota(jnp.int32, (1, 128), 1)
        gsink = jnp.zeros((1, 128), F32)
        dk_band, dv_band, dq_blk = [], [], []
        for b in range(nb):
            band = slice(BLOCK * b, BLOCK * b + 2 * BLOCK)
            blk = slice(BLOCK * b, BLOCK * (b + 1))
            dka, dva, dsb = [], [], []
            deltas = jnp.zeros((BLOCK, 128), F32)
            for j in range(4):
                qh = qrope[j // 2][blk]
                doh = dyb[blk, KV_W * (j // 2):KV_W * (j // 2 + 1)].astype(_MXU)
                pb = pswa_ref[blk, 2 * BLOCK * j:2 * BLOCK * (j + 1)]
                p = pb.astype(F32)
                dp = _mm_nt(doh, va[j][band])
                delta = jnp.sum(p * dp, axis=-1, keepdims=True)
                ds = (p * (dp - delta)).astype(_MXU)
                deltas = jnp.where(lane128 == j, delta, deltas)
                dva.append(_mm_tn(pb, doh))
                dka.append(_mm_tn(ds, qh))
                dsb.append(ds)
            gsink = gsink - _col_sum(psink_ref[blk, :] * deltas)
            dk_band.append(_unplace_kv(dka) * 0.125)
            dv_band.append(_unplace_kv(dva))
            dq_blk.append([_mm(jnp.concatenate(dsb[2 * h:2 * h + 2], axis=1),
                               jnp.concatenate([ka[2 * h][band], ka[2 * h + 1][band]], axis=0)) for h in range(2)])
        gsink_ref[...] += gsink
        dk_rows = [dk_band[b][BLOCK:] + (dk_band[b + 1][:BLOCK] if b + 1 < nb else dkcar_ref[...]) for b in range(nb)]
        dv_rows = [dv_band[b][BLOCK:] + (dv_band[b + 1][:BLOCK] if b + 1 < nb else dvcar_ref[...]) for b in range(nb)]
        dkcar_ref[...] = dk_band[0][:BLOCK]
        dvcar_ref[...] = dv_band[0][:BLOCK]
        dkg = _rope_bwd(jnp.concatenate(dk_rows, axis=0), cv, s1v, s2v)
        gkn = _col_sum(dkg * kn)
        dkn = dkg * gn_ref[:, G_K:G_XQ]
        dproj_ref[:, C_SK:C_SK + KV_W] = (krr * (dkn - kn * _seg_mean(dkn * kn, gm128))).astype(dproj_ref.dtype)
        dproj_ref[:, C_SV:C_SV + KV_W] = jnp.concatenate(dv_rows, axis=0).astype(dproj_ref.dtype)
        gqn = jnp.zeros((1, 128), F32)
        for h in range(2):
            dqg = _rope_bwd(jnp.concatenate([dq_blk[b][h] for b in range(nb)], axis=0), cv, s1v, s2v)
            gqn = gqn + _col_sum(dqg * qn_[h])
            dqn_ = dqg * gn_ref[:, G_Q:G_K]
            dproj_ref[:, C_SQ + 128 * h:C_SQ + 128 * (h + 1)] = (
                qr_[h] * (dqn_ - qn_[h] * _seg_mean(dqn_ * qn_[h], gm128))).astype(dproj_ref.dtype)
        gqn_ref[...] += gqn
        gkn_ref[...] += gkn

        u = proj_ref[:, C_LRUX:C_LRUX + LRU_W]
        xc, rg, ig, sq = (gates_ref[:, LRU_W * k:LRU_W * (k + 1)] for k in range(4))
        a = a_ref[...]
        sp = _softplus(-lam_ref[...])
        hext_ref[0:8, :] = jnp.where(first_tile, 0.0, yah_ref[...])
        hext_ref[8:8 + tm, :] = ya_ref[...]
        hprev = hext_ref[pl.ds(7, tm), :]
        aext_ref[0:tm, :] = a
        an_scr[...] = aext_ref[pl.ds(1, tm), :]
        dh_scr[...] = dya
        dh_scr[tm - 1:tm, :] = dh_scr[tm - 1:tm, :] + gcar_ref[0:1, :]
        row8 = lax.broadcasted_iota(jnp.int32, (8, LRU_W), 0)

        def scan_step(gi, carry):
            r0 = pl.multiple_of((tm // 8 - 1 - gi) * 8, 8)
            av = an_scr[pl.ds(r0, 8), :]
            bv = dh_scr[pl.ds(r0, 8), :]
            for d in (1, 2, 4):
                a_sh = jnp.where(row8 < 8 - d, pltpu.roll(av, 8 - d, 0), 1.0)
                b_sh = jnp.where(row8 < 8 - d, pltpu.roll(bv, 8 - d, 0), 0.0)
                bv = bv + av * b_sh
                av = av * a_sh
            gv = bv + av * carry
            g_scr[pl.ds(r0, 8), :] = gv
            return gv[0:1, :]

        g0 = lax.fori_loop(0, tm // 8, scan_step, jnp.zeros((1, LRU_W), F32), unroll=True)
        gcar_ref[0:1, :] = a[0:1, :] * g0
        gv = g_scr[...]
        da = gv * hprev
        dig = gv * sq * xc
        dxc = gv * sq * ig
        dla = da * a - gv * (ig * xc) * ((a * a) / sq)
        drg = dla * ((-LRU_C) * sp)
        glam_ref[...] += _col_sum(dla * rg)
        dpr = drg * rg * (1.0 - rg)
        dpi = dig * ig * (1.0 - ig)
        gbrg_ref[...] += _col_sum(dpr)
        gbig_ref[...] += _col_sum(dpi)
        dpre0 = jnp.concatenate([dpr[:, :256], dpi[:, :256]], axis=1).astype(_MXU)
        dpre1 = jnp.concatenate([dpr[:, 256:], dpi[:, 256:]], axis=1).astype(_MXU)
        gwg_ref[0] += _mm_tn(xc[:, :256], dpre0)
        gwg_ref[1] += _mm_tn(xc[:, 256:], dpre1)
        dxc = dxc + jnp.concatenate([_mm_nt(dpre0, wg_ref[0]), _mm_nt(dpre1, wg_ref[1])], axis=1)
        gcb_ref[...] += _col_sum(dxc)
        dxc_ext[0:tm, :] = dxc
        du = jnp.zeros((tm, LRU_W), F32)
        for k in range(CONV_K):
            later = dxc_ext[pl.ds(3 - k, tm), :]
            gcw_ref[k:k + 1, :] += _col_sum(later * u)
            du = du + cw_ref[k:k + 1, :] * later
        dxc_ext[tm:tm + 8, :] = dxc[0:8, :]
        dproj_ref[:, C_LRUX:C_LRUX + LRU_W] = du.astype(dproj_ref.dtype)

        dxn = _mm(dproj_ref[...], win_ref[...])
        rx = lax.rsqrt(_row_mean(xv * xv) + EPS)
        xh = xv * rx
        gng_ref[...] += _col_sum(dxn * xh)
        dxh = dxn * ng_ref[...]
        gx_ref[...] = dov.astype(F32) + rx * (dxh - xh * _row_mean(dxh * xh))

        @pl.when(i == nt - 1)
        def _():
            glam_ref[...] = glam_ref[...] * (LRU_C * _sigmoid(-lam_ref[...]))
            for r in (gqn_ref, gkn_ref, gxqn_ref):
                r[...] = _fold_heads(r[...])

    def rows(ncol, arr_cols_block=0):
        return pl.BlockSpec((tm, ncol), lambda i: (nt - 1 - i, arr_cols_block))

    def halo(nrow, ncol, colblk=0):
        per = tm // nrow
        return pl.BlockSpec((nrow, ncol), lambda i: (jnp.maximum((nt - 1 - i) * per - 1, 0), colblk))

    in_specs = [rows(D_MODEL), rows(D_MODEL), rows(D_IN), rows(LRU_W), rows(SWA_W), rows(XATT_W),
                rows(4 * 2 * BLOCK), rows(4 * MEM_LEN), rows(128), rows(4 * LRU_W), rows(LRU_W),
                rows(128), rows(128), rows(128),
                halo(8, LRU_W), halo(BLOCK, 2 * KV_W, C_SK // (2 * KV_W)),
                halo(BLOCK, 128), halo(BLOCK, 128), halo(BLOCK, 128),
                _const_spec((1, D_MODEL)), _const_spec((D_IN, D_MODEL), True), _const_spec((CONV_K, LRU_W)),
                _const_spec((2, 256, 512), True), _const_spec((1, LRU_W)), _const_spec((1, GAINS_W)),
                _const_spec((4 * MEM_LEN, XATT_W), True), _const_spec((4 * MEM_LEN, XATT_W), True),
                _const_spec((1, D_MODEL)), _const_spec((D_MODEL, D_MODEL), True)]
    small = [(2, 256, 512), (XATT_W, 4 * MEM_LEN), (XATT_W, 4 * MEM_LEN), (1, D_MODEL), (1, D_MODEL), (1, LRU_W), (1, LRU_W),
             (1, LRU_W), (1, LRU_W), (CONV_K, LRU_W), (1, 128), (1, 128), (1, XATT_W), (1, 128)]
    out_shape = (jax.ShapeDtypeStruct((seq, D_MODEL), F32), jax.ShapeDtypeStruct((seq, D_IN), _MXU)) + tuple(
        jax.ShapeDtypeStruct(s, F32) for s in small)
    out_specs = (rows(D_MODEL), rows(D_IN)) + tuple(_const_spec(s) for s in small)
    scratch = [pltpu.VMEM((tm + 8, LRU_W), F32), pltpu.VMEM((tm + 8, LRU_W), F32),
               pltpu.VMEM((tm, LRU_W), F32), pltpu.VMEM((tm, LRU_W), F32), pltpu.VMEM((tm, LRU_W), F32),
               pltpu.VMEM((tm + 8, LRU_W), F32),
               pltpu.VMEM((8, LRU_W), F32), pltpu.VMEM((BLOCK, KV_W), F32), pltpu.VMEM((BLOCK, KV_W), F32)]
    return pl.pallas_call(
        body, name="layer_bwd", grid=(nt,), out_shape=out_shape, in_specs=in_specs, out_specs=out_specs,
        scratch_shapes=scratch,
        compiler_params=pltpu.CompilerParams(dimension_semantics=("arbitrary",), vmem_limit_bytes=VMEM_LIMIT),
    )(x, dout, proj, ya, yb, yc, pswa, pmem, psink, gates, a_all, rc, rs1, rs2, ya, proj, rc, rs1, rs2,
      ng, win_t, cw, wg, lam, gains, km, vm, og, wout)


def _reduce_protocol(big, sm, outs, osm, r1, r1s, wire, r2, r2s, wire2, ps, own, send, recv, lsem):
    nbig = len(big)
    x, y, c = lax.axis_index("x"), lax.axis_index("y"), lax.axis_index("c")
    sibling = (x, y, 1 - c)
    near, far, diag = _partners(x, y, c)
    me, near_id, far_id, diag_id = _chip_of(x, y), _chip_of(*near), _chip_of(*far), _chip_of(*diag)

    def copy(k, src, dst, to):
        return pltpu.make_async_remote_copy(src_ref=src, dst_ref=dst, send_sem=send.at[k], recv_sem=recv.at[k],
                                            device_id=to, device_id_type=MESH)

    def sent(stage, a):
        if a == nbig:
            src, dst, to = ((sm.at[1 - c], r1s, sibling), (r1s, r2s.at[0], (*near, c)), (ps, r2s.at[1], (*far, c)),
                            (osm.at[c], osm.at[c], sibling))[stage]
            return [copy(5 * nbig + stage, src, dst, to)]
        if stage == 0:
            return [copy(5 * a, big[a].at[:, 1 - c], r1[a], sibling)]
        if stage == 1:
            return [copy(5 * a + 1, wire[a].at[near_id], r2[a].at[0], (*near, c)),
                    copy(5 * a + 2, wire[a].at[diag_id], r2[a].at[1], (*near, c))]
        if stage == 2:
            return [copy(5 * a + 3, wire2[a], r2[a].at[2], (*far, c))]
        return [copy(5 * a + 4, outs[a].at[c], outs[a].at[c], sibling)]

    arrays = range(nbig + (sm is not None))

    def start(stage, a):
        for cp in sent(stage, a):
            cp.start()

    def arrived(k, ref):
        copy(k, ref, ref, sibling).wait_recv()

    def loads():
        return [pltpu.make_async_copy(big[a].at[:, c], own[a], lsem.at[a]) for a in range(nbig)]

    def stage0():
        for a in arrays:
            start(0, a)
        for cp in loads():
            cp.start()

    def stage1():
        for a in range(nbig):
            loads()[a].wait()
            arrived(5 * a, r1[a])
            for k in range(N_CHIPS):
                r1[a][k] = own[a][k] + r1[a][k]
                wire[a][k] = r1[a][k].astype(wire[a].dtype)
            start(1, a)
        if sm is not None:
            arrived(5 * nbig, r1s)
            r1s[...] = sm[c] + r1s[...]
            start(1, nbig)

    def stage2():
        for a in range(nbig):
            arrived(5 * a + 1, r2[a].at[0])
            arrived(5 * a + 2, r2[a].at[1])
            r1[a][me] = r1[a][me] + r2[a][0].astype(F32)
            wire2[a][...] = (r1[a][far_id] + r2[a][1].astype(F32)).astype(wire2[a].dtype)
            start(2, a)
        if sm is not None:
            arrived(5 * nbig + 1, r2s.at[0])
            ps[...] = r1s[...] + r2s[0]
            start(2, nbig)

    def stage3():
        for a in range(nbig):
            arrived(5 * a + 3, r2[a].at[2])
            outs[a][c] = r1[a][me] + r2[a][2].astype(F32)
            start(3, a)
        if sm is not None:
            arrived(5 * nbig + 2, r2s.at[1])
            osm[c] = ps[...] + r2s[1]
            start(3, nbig)

    def stage4():
        for a in range(nbig):
            arrived(5 * a + 4, outs[a].at[1 - c])
        if sm is not None:
            arrived(5 * nbig + 3, osm.at[1 - c])
        for stage in range(4):
            for a in arrays:
                for cp in sent(stage, a):
                    cp.wait_send()

    return [stage0, stage1, stage2, stage3, stage4]


def _reduce_buffers(bigs, g_small):
    half = [b.shape[2:] for b in bigs]
    sm_half = None if g_small is None else g_small.shape[1:]
    out_shape = [jax.ShapeDtypeStruct((2,) + h, F32) for h in half]
    small = lambda lead: [] if g_small is None else [pltpu.VMEM(lead + sm_half, F32)]
    if g_small is not None:
        out_shape.append(jax.ShapeDtypeStruct(g_small.shape, F32))
    n_sem = 5 * len(bigs) + 4
    scratch = ([pltpu.VMEM((N_CHIPS,) + h, F32) for h in half] + small(())
               + [pltpu.VMEM((N_CHIPS,) + h, _WIRE) for h in half]
               + [pltpu.VMEM((3,) + h, _WIRE) for h in half] + small((2,))
               + [pltpu.VMEM(h, _WIRE) for h in half] + small(())
               + [pltpu.VMEM((N_CHIPS,) + h, F32) for h in half]
               + [pltpu.SemaphoreType.DMA((n_sem,)), pltpu.SemaphoreType.DMA((n_sem,)),
                  pltpu.SemaphoreType.DMA((len(bigs),))])
    return out_shape, scratch


def _split_reduce_refs(refs, nbig, has_small):
    it = iter(refs)
    take = lambda n: [next(it) for _ in range(n)]
    one = lambda: next(it) if has_small else None
    big, sm = take(nbig), one()
    outs, osm = take(nbig), one()
    r1, r1s, wire, r2, r2s, wire2, ps, own = take(nbig), one(), take(nbig), take(nbig), one(), take(nbig), one(), take(nbig)
    send, recv, lsem = take(3)
    return big, sm, outs, osm, r1, r1s, wire, r2, r2s, wire2, ps, own, send, recv, lsem


def _hosted_reduce_shapes(bigs, g_small):
    red_shape, scratch = _reduce_buffers(bigs, g_small)
    nres = len(red_shape)
    return red_shape, [pltpu.VMEM(r.shape, r.dtype) for r in red_shape] + scratch + [pltpu.SemaphoreType.DMA((nres,))]


def _hosted_reduce(step, n_steps, closing, stage_at, operands, results, scratch, has_small):
    nres = len(results)
    sums, rest, fsem = scratch[:nres], scratch[nres:-1], scratch[-1]
    refs = tuple(operands) + tuple(sums) + tuple(rest)

    def to_results():
        out = [pltpu.make_async_copy(sums[k], results[k], fsem.at[k]) for k in range(nres)]
        for cp in out:
            cp.start()
        for cp in out:
            cp.wait()

    stages = _reduce_protocol(*_split_reduce_refs(refs, nres - has_small, has_small))

    def last_stage():
        stages[-1]()
        to_results()

    for at, stage in zip(stage_at, stages[:-1] + [last_stage]):
        if closing == (at == n_steps):
            pl.when(step == min(at, n_steps - 1))(stage)


def reduce_grads(big, name, parts):
    chips, halves, rows_, cols = big.shape
    sub = jax.ShapeDtypeStruct((chips, halves, rows_ // parts, cols), big.dtype)

    def body(b_ref, o_ref, *scratch):
        refs = [b_ref.at[:, :, s] for s in range(parts)] + [o_ref.at[:, s] for s in range(parts)] + list(scratch)
        for stage in _reduce_protocol(*_split_reduce_refs(refs, parts, False)):
            stage()

    _, scratch = _reduce_buffers([sub] * parts, None)
    return pl.pallas_call(
        body, name=name, out_shape=jax.ShapeDtypeStruct((halves, parts, rows_ // parts, cols), F32),
        in_specs=[pl.BlockSpec(memory_space=pl.ANY)], out_specs=pl.BlockSpec(memory_space=pltpu.VMEM),
        scratch_shapes=scratch, compiler_params=pltpu.CompilerParams(vmem_limit_bytes=VMEM_LIMIT),
    )(big.reshape(chips, halves, parts, rows_ // parts, cols))


def adamw_matrices(items):
    plan, total = [], 0
    for w, _, _, _ in items:
        rows_, cols = w.shape
        tr = max(t for t in range(8, rows_ + 1, 8) if rows_ % t == 0 and t * cols * 4 <= ADAM_BLOCK_BYTES)
        plan.append((total, rows_ // tr, tr, cols))
        total += rows_ // tr
    nin = 4 * len(items)

    def body(*refs):
        i = pl.program_id(0)
        for k, (first, steps, _, _) in enumerate(plan):
            w_ref, g_ref, m_ref, v_ref = refs[4 * k:4 * k + 4]
            go_ref, d_ref, nm_ref, nv_ref = refs[nin + 4 * k:nin + 4 * k + 4]

            @pl.when((i >= first) & (i < first + steps))
            def _():
                gv = g_ref[...]
                go_ref[...] = gv
                d_ref[...], nm_ref[...], nv_ref[...] = _adam_update(w_ref[...], gv, m_ref[...], v_ref[...])

    specs, shapes = [], []
    for (first, steps, tr, cols), (w, _, _, _) in zip(plan, items):
        spec = pl.BlockSpec((tr, cols), lambda i, first=first, steps=steps: (jnp.clip(i - first, 0, steps - 1), 0))
        specs += [spec] * 4
        shapes += [jax.ShapeDtypeStruct(w.shape, F32)] * 4
    res = pl.pallas_call(
        body, name="adamw_matrices", grid=(total,), out_shape=tuple(shapes), in_specs=specs, out_specs=tuple(specs),
        compiler_params=pltpu.CompilerParams(dimension_semantics=("arbitrary",)),
    )(*[a for item in items for a in item])
    return [res[4 * k:4 * k + 4] for k in range(len(items))]


def _adam_update(w, g, m, v):
    nm = ADAM_B1 * m + (1.0 - ADAM_B1) * g
    nv = ADAM_B2 * v + (1.0 - ADAM_B2) * (g * g)
    m_hat = nm / (1.0 - ADAM_B1 ** ADAM_STEP)
    v_hat = nv / (1.0 - ADAM_B2 ** ADAM_STEP)
    return (-ADAM_LR) * (m_hat / (jnp.sqrt(v_hat) + ADAM_EPS) + ADAM_WD * w), nm, nv


def adamw_vectors(g_pack, ws, ms, vs):
    nvec = len(SMALL_VECTORS)
    n = nvec + len(SMALL_MATRICES)

    def body(*refs):
        pk = refs[0]
        w_refs, m_refs, v_refs = (refs[1 + k * n:1 + (k + 1) * n] for k in range(3))
        g_out, d_out, nm_out, nv_out = (refs[1 + (3 + k) * n:1 + (4 + k) * n] for k in range(4))
        refs[-1][...] = pk[LOSS_ROW:LOSS_ROW + 1, 0:1]
        chip = 2 * lax.axis_index("x") + lax.axis_index("y")
        for k, (name, row, width) in enumerate(SMALL_VECTORS):
            if name == "conv_w":
                g = jnp.concatenate([pk[pl.ds(row + 4 * t + chip, 1), :] for t in range(CONV_K)], axis=0)[None]
            elif width >= 128:
                g = jnp.concatenate([pk[row + r:row + r + 1, :] for r in range(width // 128)], axis=1)
            else:
                g = pk[row:row + 1, 0:width]
            g_out[k][...] = g
            d_out[k][...], nm_out[k][...], nv_out[k][...] = _adam_update(w_refs[k][...], g, m_refs[k][...], v_refs[k][...])
        for k in range(nvec, n):
            for b in range(LRU_BLOCKS):
                rows_ = pk[GATES_ROW + HEAD * b:GATES_ROW + HEAD * (b + 1), :]
                g = (pltpu.roll(rows_, HEAD, axis=1) if k > nvec else rows_)[:, 0:HEAD]
                g_out[k][0, b] = g
                d_out[k][0, b], nm_out[k][0, b], nv_out[k][0, b] = _adam_update(
                    w_refs[k][0, b], g, m_refs[k][0, b], v_refs[k][0, b])

    vm = pl.BlockSpec(memory_space=pltpu.VMEM)
    like = [jax.ShapeDtypeStruct(w.shape, F32) for w in ws]
    return pl.pallas_call(
        body, name="adamw_vectors", out_shape=(*like * 4, jax.ShapeDtypeStruct((1, 1), F32)),
        in_specs=[vm] * (1 + 3 * n), out_specs=(vm,) * (4 * n + 1),
    )(g_pack, *ws, *ms, *vs)


SMALL_VECTORS = (("norm_g", 0, 1024), ("mem_norm_g", 8, 1024), ("conv_w", 16, 512), ("conv_b", 32, 512),
                 ("b_rg", 36, 512), ("b_ig", 40, 512), ("lru_lambda", 44, 512), ("q_norm_g", 48, 64),
                 ("k_norm_g", 49, 64), ("sinks", 50, 4), ("xq_norm_g", 51, 64), ("xk_norm_g", 52, 64),
                 ("out_norm_g", 53, 1024))
LOSS_ROW = 61
SMALL_MATRICES = ("w_rg", "w_ig")
GATES_ROW = 64
SMALL_ROWS = GATES_ROW + LRU_BLOCKS * HEAD


def _rope_tables(seq):
    pos = np.arange(seq, dtype=np.float32)
    inv_freq = (np.float32(ROPE_THETA) ** (-(np.arange(0, ROPE_DIM, 2, dtype=np.float32) / np.float32(ROPE_DIM)))
                ).astype(np.float32)
    ang = (pos[:, None] * inv_freq[None, :]).astype(np.float32)
    cos, sin = np.cos(ang).astype(np.float32), np.sin(ang).astype(np.float32)
    z = lambda n: np.zeros((seq, n), np.float32)
    c64 = np.concatenate([cos, cos, np.ones((seq, HEAD - ROPE_DIM), np.float32)], axis=1)
    s1_64 = np.concatenate([-sin, z(HEAD - 8)], axis=1)
    s2_64 = np.concatenate([z(8), sin, z(HEAD - ROPE_DIM)], axis=1)
    return tuple(jnp.asarray(np.concatenate([t, t], axis=1)) for t in (c64, s1_64, s2_64))


def kernel(x, mem, norm_g, mem_norm_g, w_in, conv_w, conv_b, w_rg, b_rg, w_ig, b_ig, lru_lambda, q_norm_g, k_norm_g, sinks, w_mem_kv, xq_norm_g, xk_norm_g, out_norm_g, w_out, loss_target, m_norm_g, m_mem_norm_g, m_w_in, m_conv_w, m_conv_b, m_w_rg, m_b_rg, m_w_ig, m_b_ig, m_lru_lambda, m_q_norm_g, m_k_norm_g, m_sinks, m_w_mem_kv, m_xq_norm_g, m_xk_norm_g, m_out_norm_g, m_w_out, v_norm_g, v_mem_norm_g, v_w_in, v_conv_w, v_conv_b, v_w_rg, v_b_rg, v_w_ig, v_b_ig, v_lru_lambda, v_q_norm_g, v_k_norm_g, v_sinks, v_w_mem_kv, v_xq_norm_g, v_xk_norm_g, v_out_norm_g, v_w_out):
    seq = x.shape[1]
    xs, tgt, mems = x[0], loss_target[0], mem[0]

    win_t, wout, wkv, cw, wg, gains = gather_weights(w_in[0].T, w_out[0], w_mem_kv[0], conv_w, w_rg, w_ig,
                                                     (q_norm_g, k_norm_g, xq_norm_g, xk_norm_g))
    rc, rs1, rs2 = _rope_tables(seq)

    km, vm = mem_fwd(mems, mem_norm_g, wkv, gains)
    proj, ya, yb, yc, ycat, xn, dout, pswa, pmem, psink, gates, a_all, loss8 = layer_fwd(
        xs, tgt, rc, rs1, rs2, norm_g, win_t, cw, conv_b, wg, b_rg, b_ig, lru_lambda, gains, sinks, km, vm,
        out_norm_g, wout)
    (gx, dproj, g_wg, dkm, dvm, g_ng, g_og, g_cb, g_brg, g_big, g_lam, g_cw, g_qn, g_kn, g_xqn, g_sink) = layer_bwd(
        xs, dout, proj, ya, yb, yc, pswa, pmem, psink, gates, a_all, rc, rs1, rs2, norm_g, win_t, cw, wg, lru_lambda,
        gains, km, vm, out_norm_g, wout)
    g_wkv, small_g = mem_bwd(mems, mem_norm_g, wkv, gains, dkm, dvm, g_wg, loss8, dict(
        norm_g=g_ng, conv_w=g_cw, conv_b=g_cb, b_rg=g_brg, b_ig=g_big, lru_lambda=g_lam, q_norm_g=g_qn, k_norm_g=g_kn,
        sinks=g_sink, xq_norm_g=g_xqn, out_norm_g=g_og))
    g_win_t, r_out, r_kv, r_small = weight_grads(
        ycat, dout, dproj, xn, g_wkv.reshape(N_CHIPS, 2, D_MODEL // 8, 2 * XATT_W),
        small_g.reshape(2, SMALL_ROWS // 2, 128), (0, 1, 4, 7, 8), (4, 5, 10, 12, 13))
    r_in = reduce_grads(g_win_t.reshape(N_CHIPS, 2, D_IN // 8, D_MODEL), "reduce_w_in", 6)

    r_small = r_small.reshape(SMALL_ROWS, 128)
    grads = {}
    weights = dict(norm_g=norm_g, mem_norm_g=mem_norm_g, w_in=w_in, conv_w=conv_w, conv_b=conv_b, w_rg=w_rg, b_rg=b_rg,
                   w_ig=w_ig, b_ig=b_ig, lru_lambda=lru_lambda, q_norm_g=q_norm_g, k_norm_g=k_norm_g, sinks=sinks,
                   w_mem_kv=w_mem_kv, xq_norm_g=xq_norm_g, xk_norm_g=xk_norm_g, out_norm_g=out_norm_g, w_out=w_out)
    ms = dict(norm_g=m_norm_g, mem_norm_g=m_mem_norm_g, w_in=m_w_in, conv_w=m_conv_w, conv_b=m_conv_b, w_rg=m_w_rg,
              b_rg=m_b_rg, w_ig=m_w_ig, b_ig=m_b_ig, lru_lambda=m_lru_lambda, q_norm_g=m_q_norm_g, k_norm_g=m_k_norm_g,
              sinks=m_sinks, w_mem_kv=m_w_mem_kv, xq_norm_g=m_xq_norm_g, xk_norm_g=m_xk_norm_g,
              out_norm_g=m_out_norm_g, w_out=m_w_out)
    vs = dict(norm_g=v_norm_g, mem_norm_g=v_mem_norm_g, w_in=v_w_in, conv_w=v_conv_w, conv_b=v_conv_b, w_rg=v_w_rg,
              b_rg=v_b_rg, w_ig=v_w_ig, b_ig=v_b_ig, lru_lambda=v_lru_lambda, q_norm_g=v_q_norm_g, k_norm_g=v_k_norm_g,
              sinks=v_sinks, w_mem_kv=v_w_mem_kv, xq_norm_g=v_xq_norm_g, xk_norm_g=v_xk_norm_g,
              out_norm_g=v_out_norm_g, w_out=v_w_out)

    delta, new_m, new_v = {}, {}, {}
    res_in, res_out, res_kv = adamw_matrices([
        (w_in[0].T, r_in.reshape(D_IN // 4, D_MODEL), m_w_in[0].T, v_w_in[0].T),
        (w_out[0], r_out.reshape(D_MODEL // 4, D_MODEL), m_w_out[0], v_w_out[0]),
        (w_mem_kv[0], r_kv.reshape(D_MODEL // 4, 2 * XATT_W), m_w_mem_kv[0], v_w_mem_kv[0])])
    grads["w_in"], delta["w_in"], new_m["w_in"], new_v["w_in"] = (r.T[None] for r in res_in)
    grads["w_out"], delta["w_out"], new_m["w_out"], new_v["w_out"] = (r[None] for r in res_out)
    grads["w_mem_kv"], delta["w_mem_kv"], new_m["w_mem_kv"], new_v["w_mem_kv"] = (r[None] for r in res_kv)
    small_names = [n for n, _, _ in SMALL_VECTORS] + list(SMALL_MATRICES)
    res = adamw_vectors(r_small, [weights[n] for n in small_names], [ms[n] for n in small_names],
                        [vs[n] for n in small_names])
    nall = len(small_names)
    for k, into in enumerate((grads, delta, new_m, new_v)):
        into.update(zip(small_names, res[k * nall:(k + 1) * nall]))
    loss = res[-1].reshape(())

    order = ("norm_g", "mem_norm_g", "w_in", "conv_w", "conv_b", "w_rg", "b_rg", "w_ig", "b_ig", "lru_lambda",
             "q_norm_g", "k_norm_g", "sinks", "w_mem_kv", "xq_norm_g", "xk_norm_g", "out_norm_g", "w_out")
    return (loss, gx[None], *[grads[n] for n in order], *[delta[n] for n in order], *[new_m[n] for n in order],
            *[new_v[n] for n in order])
```

```python
import jax
import jax.numpy as jnp
import numpy as np
from jax import lax
from jax.experimental import pallas as pl
from jax.experimental.pallas import tpu as pltpu

F32 = jnp.float32
_MXU = jnp.bfloat16
_WIRE = jnp.bfloat16

D_MODEL = 1024
MEM_LEN = 256
HEAD = 64
LRU_W = 512
LRU_BLOCKS = 8
CONV_K = 4
LRU_C = 8.0
SWA_W = 256
KV_W = 128
XATT_W = 256
BLOCK = 128
D_IN = 2304
ROPE_THETA = 500000.0
ROPE_DIM = 16
EPS = 1e-6
NEG_INF = -1e30
C_LRUX, C_LRUG, C_SQ, C_SK, C_SV, C_SWAG, C_XQ, C_XG = 0, 512, 1024, 1280, 1408, 1536, 1792, 2048
G_Q, G_K, G_XQ, G_XK, GAINS_W = 0, 128, 256, 512, 768

ADAM_LR, ADAM_B1, ADAM_B2, ADAM_EPS, ADAM_WD, ADAM_STEP = 0.001, 0.9, 0.999, 1e-08, 0.01, 10

N_CHIPS = 4
ROW_TILE = 256
VMEM_LIMIT = 56 * 1024 * 1024
ADAM_BLOCK_BYTES = 640 * 1024
MESH = pl.DeviceIdType.MESH


def _mm(a, b):
    return jnp.dot(a.astype(_MXU), b.astype(_MXU), preferred_element_type=F32)


def _mm_nt(a, b):
    return lax.dot_general(a.astype(_MXU), b.astype(_MXU), (((1,), (1,)), ((), ())), preferred_element_type=F32)


def _mm_tn(a, b):
    return lax.dot_general(a.astype(_MXU), b.astype(_MXU), (((0,), (0,)), ((), ())), preferred_element_type=F32)


def _group_matrix(width):
    r = lax.shift_right_logical(lax.broadcasted_iota(jnp.int32, (width, width), 0), 6)
    c = lax.shift_right_logical(lax.broadcasted_iota(jnp.int32, (width, width), 1), 6)
    return (r == c).astype(_MXU)


def _seg_mean(x, gm):
    return jnp.dot(x.astype(_MXU), gm, preferred_element_type=F32) * (1.0 / HEAD)


def _row_mean(x):
    return jnp.mean(x, axis=-1, keepdims=True)


def _col_sum(x):
    return jnp.sum(x, axis=0, keepdims=True)


def _sigmoid(x):
    return jax.nn.sigmoid(x)


def _softplus(z):
    e = jnp.exp(-jnp.abs(z))
    u = 1.0 + e
    log1p_e = jnp.where(u == 1.0, e, jnp.log(u) * (e / (u - 1.0)))
    return jnp.maximum(z, 0.0) + log1p_e


def _rope(t, c, s1, s2):
    return t * c + pltpu.roll(t, 120, 1) * s1 + pltpu.roll(t, 8, 1) * s2


def _rope_bwd(d, c, s1, s2):
    return d * c + pltpu.roll(d * s1, 8, 1) + pltpu.roll(d * s2, 120, 1)


def _fold_heads(v):
    out = v
    for k in range(1, v.shape[1] // HEAD):
        out = out + pltpu.roll(v, HEAD * k, 1)
    return out


def _lane_mask(width, lo, hi):
    lane = lax.broadcasted_iota(jnp.int32, (1, width), 1)
    return ((lane >= lo) & (lane < hi)).astype(F32)


def _swa_mask(first_block):
    qi = lax.broadcasted_iota(jnp.int32, (BLOCK, 2 * BLOCK), 0)
    kj = lax.broadcasted_iota(jnp.int32, (BLOCK, 2 * BLOCK), 1)
    rel = qi + BLOCK - kj
    ok = (rel >= 0) & (rel < BLOCK)
    return ok & (jnp.logical_not(first_block) | (kj >= BLOCK))


def _place_kv(t, scale):
    lo = t * (_lane_mask(KV_W, 0, HEAD) * scale)
    hi = t * (_lane_mask(KV_W, HEAD, KV_W) * scale)
    return [a.astype(_MXU) for a in (lo, pltpu.roll(lo, HEAD, 1), pltpu.roll(hi, HEAD, 1), hi)]


def _unplace_kv(d):
    return (_lane_mask(KV_W, 0, HEAD) * (d[0] + pltpu.roll(d[1], HEAD, 1))
            + _lane_mask(KV_W, HEAD, KV_W) * (d[3] + pltpu.roll(d[2], HEAD, 1)))


def _swa_probs(qh, ka, mask, sink):
    s = _mm_nt(qh, ka)
    s = jnp.where(mask, s, NEG_INF)
    m = jnp.maximum(jnp.max(s, axis=-1, keepdims=True), sink)
    p = jnp.exp(s - m)
    esink = jnp.exp(sink - m)
    inv = 1.0 / (jnp.sum(p, axis=-1, keepdims=True) + esink)
    return p * inv, esink * inv


def _mem_probs(s_all):
    out = []
    for j in range(4):
        s = s_all[:, MEM_LEN * j:MEM_LEN * (j + 1)]
        p = jnp.exp(s - jnp.max(s, axis=-1, keepdims=True))
        out.append(p * (1.0 / jnp.sum(p, axis=-1, keepdims=True)))
    return out


def _head_rows(t, scale):
    return jnp.concatenate([t * (_lane_mask(XATT_W, HEAD * j, HEAD * (j + 1)) * scale) for j in range(4)], axis=0)


def _lru_gates(xc, wg_ref, brg, big, lam):
    p0 = _mm(xc[:, :256], wg_ref[0])
    p1 = _mm(xc[:, 256:], wg_ref[1])
    rg = _sigmoid(jnp.concatenate([p0[:, :256], p1[:, :256]], axis=1) + brg)
    ig = _sigmoid(jnp.concatenate([p0[:, 256:], p1[:, 256:]], axis=1) + big)
    sp = _softplus(-lam)
    la = (-LRU_C) * rg * sp
    a = jnp.exp(la)
    th = jnp.tanh(la)
    one_minus_a2 = (-2.0 * th) / (1.0 - th)
    return rg, ig, sp, a, jnp.sqrt(one_minus_a2)


def _const_spec(shape, single=False):
    zeros = (0,) * len(shape)
    if single:
        return pl.BlockSpec(shape, lambda i: zeros, pipeline_mode=pl.Buffered(1))
    return pl.BlockSpec(shape, lambda i: zeros)


def _chip_of(x, y):
    return 2 * x + y


def _partners(x, y, c):
    north = c == 1
    near = (jnp.where(north, 1 - x, x), jnp.where(north, y, 1 - y))
    far = (jnp.where(north, x, 1 - x), jnp.where(north, 1 - y, y))
    return near, far, (1 - x, 1 - y)


def gather_weights(win_t, wout, wkv, conv_w, w_rg, w_ig, head_gains, mem, mem_g):
    arrs = (win_t, wout, wkv)
    n = len(arrs)
    pieces = [(a, k * (arr.shape[0] // (2 * cut)), arr.shape[0] // (2 * cut))
              for a, (arr, cut) in enumerate(zip(arrs, (2, 1, 1))) for k in range(cut)]
    npc = len(pieces)

    def body(a0, a1, a2, cw_in, wrg_ref, wig_ref, q_ref, k_ref, xq_ref, xk_ref, mem_ref, mg_ref,
             o0, o1, o2, cw_out, wg_ref, gn_ref, km_ref, vm_ref, s0, s1, s2, cw, ocw, send, recv, lsem):
        ins, outs = (s0, s1, s2), (o0, o1, o2)
        for src, dst in zip((a0, a1, a2), ins):
            dst[...] = src[...].astype(dst.dtype)
        cw[...] = jnp.zeros(cw.shape, F32)
        cw[0:CONV_K, :] = cw_in[0]
        x, y, c = lax.axis_index("x"), lax.axis_index("y"), lax.axis_index("c")
        sibling = (x, y, 1 - c)
        near, far, diag = _partners(x, y, c)
        chips = [near, far, diag]
        me = _chip_of(x, y)

        def landed(p, chip, half):
            a, off, rows_ = pieces[p]
            r = ins[a].shape[0]
            return outs[a].at[pl.ds(pl.multiple_of(chip * r + half * (r // 2) + off, 16), rows_)]

        def mine(p):
            a, off, rows_ = pieces[p]
            return ins[a].at[pl.ds(pl.multiple_of(c * (ins[a].shape[0] // 2) + off, 16), rows_)]

        def copy(k, src, dst, to):
            return pltpu.make_async_remote_copy(src_ref=src, dst_ref=dst, send_sem=send.at[k], recv_sem=recv.at[k],
                                                device_id=to, device_id_type=MESH)

        def cw_rows(chip):
            return ocw.at[pl.ds(pl.multiple_of(chip * 8, 8), 8)]

        locals_ = []
        for a in range(n):
            r = ins[a].shape[0]
            locals_.append(pltpu.make_async_copy(ins[a], outs[a].at[pl.ds(pl.multiple_of(me * r, 16), r)], lsem.at[a]))
        locals_.append(pltpu.make_async_copy(cw, cw_rows(me), lsem.at[n]))
        for cp in locals_:
            cp.start()

        sent = []
        for p in range(npc):
            for j in range(2):
                sent.append(copy(p * 6 + j, mine(p), landed(p, me, c), (*chips[j], c)))
        for j, chip in enumerate(chips):
            sent.append(copy(npc * 6 + j, cw, cw_rows(me), (*chip, c)))
        for cp in sent:
            cp.start()

        gn_ref[...] = jnp.concatenate([q_ref[...]] * 2 + [k_ref[...]] * 2 + [xq_ref[...]] * 4 + [xk_ref[...]] * 4,
                                      axis=1)
        zeros = lambda lanes: [jnp.zeros((HEAD, lanes), F32)] if lanes else []
        for h in range(2):
            for b in range(4):
                row = []
                for w_ref in (wrg_ref, wig_ref):
                    row += zeros(HEAD * b) + [w_ref[0, 4 * h + b]] + zeros(HEAD * (3 - b))
                wg_ref[h, HEAD * b:HEAD * (b + 1), :] = jnp.concatenate(row, axis=1).astype(wg_ref.dtype)

        for j in range(3):
            for p in range(npc):
                got = landed(p, _chip_of(*chips[j]), c)
                copy(p * 6 + j, got, got, sibling).wait_recv()
                if j == 0:
                    sent.append(copy(p * 6 + 2, got, got, (*far, c)))
                    sent[-1].start()
                sent.append(copy(p * 6 + 3 + j, got, got, sibling))
                sent[-1].start()
        for p in range(npc):
            for j in range(3):
                got = landed(p, _chip_of(*chips[(1, 0, 2)[j]]), 1 - c)
                copy(p * 6 + 3 + j, got, got, sibling).wait_recv()
        for j, chip in enumerate(chips):
            got = cw_rows(_chip_of(*chip))
            copy(npc * 6 + j, got, got, (*chip, c)).wait_recv()
        for cp in sent:
            cp.wait_send()
        for cp in locals_:
            cp.wait()
        for chip in range(N_CHIPS):
            cw_out[:, 128 * chip:128 * (chip + 1)] = ocw[8 * chip:8 * chip + CONV_K, :]

        mem_v = mem_ref[...]
        mn = mem_v * lax.rsqrt(_row_mean(mem_v * mem_v) + EPS) * mg_ref[...]
        mkv = _mm(mn, o2[...])
        kpre = mkv[:, :XATT_W]
        km = kpre * lax.rsqrt(_seg_mean(kpre * kpre, _group_matrix(XATT_W)) + EPS) * gn_ref[:, G_XK:GAINS_W]
        km_ref[...] = _head_rows(km, 0.125).astype(km_ref.dtype)
        vm_ref[...] = _head_rows(mkv[:, XATT_W:], 1.0).astype(vm_ref.dtype)

    vm = pl.BlockSpec(memory_space=pltpu.VMEM)
    hbm = pl.BlockSpec(memory_space=pl.ANY)
    head_rows = jax.ShapeDtypeStruct((4 * MEM_LEN, XATT_W), _MXU)
    out_shape = tuple(jax.ShapeDtypeStruct((N_CHIPS * a.shape[0],) + a.shape[1:], _MXU) for a in arrs) + (
        jax.ShapeDtypeStruct((CONV_K, LRU_W), F32), jax.ShapeDtypeStruct((2, 256, 512), _MXU),
        jax.ShapeDtypeStruct((1, GAINS_W), F32), head_rows, head_rows)
    n_rdma = npc * 6 + 3
    return pl.pallas_call(
        body, name="gather_weights", out_shape=out_shape,
        in_specs=[vm] * 12, out_specs=(hbm, hbm, vm, vm, vm, vm, vm, vm),
        scratch_shapes=[pltpu.VMEM(a.shape, _MXU) for a in arrs] + [
            pltpu.VMEM((8, 128), F32), pltpu.VMEM((N_CHIPS * 8, 128), F32),
            pltpu.SemaphoreType.DMA((n_rdma,)), pltpu.SemaphoreType.DMA((n_rdma,)), pltpu.SemaphoreType.DMA((n + 1,))],
        compiler_params=pltpu.CompilerParams(vmem_limit_bytes=VMEM_LIMIT),
    )(win_t, wout, wkv, conv_w, w_rg, w_ig, *head_gains, mem, mem_g)


def mem_bwd(mem, mem_g, wkv, gains, dkm, dvm, g_gates, loss8, vectors):
    names = tuple(vectors)
    first_row = {name: (row, width) for name, row, width in SMALL_VECTORS}

    def body(mem_ref, g_ref, w_ref, gn_ref, dkm_ref, dvm_ref, gg_ref, loss_ref, *rest):
        vec_refs, (gw_ref, pk_ref) = rest[:len(names)], rest[len(names):]
        pk_ref[...] = jnp.zeros(pk_ref.shape, F32)

        def put(name, src):
            row, width = first_row[name]
            per_row = 1 if width < 128 else src.shape[1] // 128
            for t in range(src.shape[0]):
                for r in range(per_row):
                    at = row + per_row * t + r
                    pk_ref[at:at + 1, :] = src[t:t + 1, 128 * r:128 * (r + 1)]

        for name, ref in zip(names, vec_refs):
            put(name, ref)
        pk_ref[LOSS_ROW:LOSS_ROW + 1, :] = loss_ref[0:1, :]
        upper = lax.broadcasted_iota(jnp.int32, (HEAD, 128), 1) >= HEAD
        for h in range(2):
            for b in range(4):
                rg = gg_ref[h, HEAD * b:HEAD * (b + 1), 128 * (b // 2):128 * (b // 2 + 1)]
                ig = gg_ref[h, HEAD * b:HEAD * (b + 1), 256 + 128 * (b // 2):256 + 128 * (b // 2 + 1)]
                if b % 2:
                    rg = pltpu.roll(rg, HEAD, axis=1)
                else:
                    ig = pltpu.roll(ig, HEAD, axis=1)
                at = GATES_ROW + HEAD * (4 * h + b)
                pk_ref[at:at + HEAD, :] = jnp.where(upper, ig, rg)

        mem_v = mem_ref[...]
        mh = mem_v * lax.rsqrt(_row_mean(mem_v * mem_v) + EPS)
        mn = mh * g_ref[...]
        mkv = _mm(mn, w_ref[...])
        kpre = mkv[:, :XATT_W]
        gm = _group_matrix(XATT_W)
        rk = lax.rsqrt(_seg_mean(kpre * kpre, gm) + EPS)
        kn = kpre * rk
        dk = jnp.zeros((MEM_LEN, XATT_W), F32)
        dv = jnp.zeros((MEM_LEN, XATT_W), F32)
        for j in range(4):
            mj = _lane_mask(XATT_W, HEAD * j, HEAD * (j + 1))
            dk = dk + dkm_ref[:, MEM_LEN * j:MEM_LEN * (j + 1)].T * (mj * 0.125)
            dv = dv + dvm_ref[:, MEM_LEN * j:MEM_LEN * (j + 1)].T * mj
        put("xk_norm_g", _fold_heads(_col_sum(dk * kn)))
        dkn = dk * gn_ref[:, G_XK:GAINS_W]
        dkpre = rk * (dkn - kn * _seg_mean(dkn * kn, gm))
        dmkv = jnp.concatenate([dkpre, dv], axis=1)
        gw_ref[...] = _mm_tn(mn, dmkv)
        dmn = _mm_nt(dmkv, w_ref[...])
        put("mem_norm_g", _col_sum(dmn * mh))

    vm = pl.BlockSpec(memory_space=pltpu.VMEM)
    return pl.pallas_call(
        body, name="mem_bwd",
        out_shape=(jax.ShapeDtypeStruct((D_MODEL, 2 * XATT_W), F32), jax.ShapeDtypeStruct((SMALL_ROWS, 128), F32)),
        in_specs=[vm] * (8 + len(names)), out_specs=(vm, vm),
    )(mem, mem_g, wkv, gains, dkm, dvm, g_gates, loss8, *vectors.values())


def layer_fwd(x, tgt, rc, rs1, rs2, ng, win_t, cw, cb, wg, brg, big, lam, gains, sinks, km, vm, og, wout):
    seq = x.shape[0]
    tm = min(ROW_TILE, seq)
    nt = seq // tm
    nb = tm // BLOCK

    def body(x_ref, t_ref, c_ref, s1_ref, s2_ref, ng_ref, win_ref, cw_ref, cb_ref, wg_ref, brg_ref, big_ref, lam_ref,
             gn_ref, sink_ref, km_ref, vm_ref, og_ref, wout_ref,
             proj_ref, ya_ref, yb_ref, yc_ref, ycat_ref, xn_ref, dout_ref, pswa_ref, pmem_ref, psink_ref, gates_ref,
             a_ref, loss_ref,
             ext_ref, b_scr, hc_ref, kp_ref, vp_ref, lacc_ref):
        i = pl.program_id(0)

        @pl.when(i == 0)
        def _():
            ext_ref[0:8, :] = jnp.zeros((8, LRU_W), F32)
            hc_ref[...] = jnp.zeros_like(hc_ref)
            kp_ref[...] = jnp.zeros_like(kp_ref)
            vp_ref[...] = jnp.zeros_like(vp_ref)
            lacc_ref[...] = jnp.zeros_like(lacc_ref)

        xv = x_ref[...]
        xn = (xv * lax.rsqrt(_row_mean(xv * xv) + EPS) * ng_ref[...]).astype(_MXU)
        xn_ref[...] = xn.astype(xn_ref.dtype)
        proj_ref[...] = _mm_nt(xn, win_ref[...])

        u = proj_ref[:, C_LRUX:C_LRUX + LRU_W]
        ext_ref[8:8 + tm, :] = u
        xc = cb_ref[...]
        for k in range(CONV_K):
            xc = xc + cw_ref[k:k + 1, :] * ext_ref[pl.ds(5 + k, tm), :]
        ext_ref[0:8, :] = u[tm - 8:tm, :]
        rg, ig, sp, a, sq = _lru_gates(xc, wg_ref, brg_ref[...], big_ref[...], lam_ref[...])
        for k, t in enumerate((xc, rg, ig, sq)):
            gates_ref[:, LRU_W * k:LRU_W * (k + 1)] = t.astype(gates_ref.dtype)
        a_ref[...] = a
        b_scr[...] = sq * (ig * xc)
        row8 = lax.broadcasted_iota(jnp.int32, (8, LRU_W), 0)

        def scan_step(g, carry):
            r0 = pl.multiple_of(g * 8, 8)
            av = a_ref[pl.ds(r0, 8), :]
            bv = b_scr[pl.ds(r0, 8), :]
            for d in (1, 2, 4):
                a_sh = jnp.where(row8 >= d, pltpu.roll(av, d, 0), 1.0)
                b_sh = jnp.where(row8 >= d, pltpu.roll(bv, d, 0), 0.0)
                bv = bv + av * b_sh
                av = av * a_sh
            hv = bv + av * carry
            ya_ref[pl.ds(r0, 8), :] = hv
            return hv[7:8, :]

        hc_ref[0:1, :] = lax.fori_loop(0, tm // 8, scan_step, hc_ref[0:1, :], unroll=True)

        gm128 = _group_matrix(KV_W)
        cv, s1v, s2v = c_ref[...], s1_ref[...], s2_ref[...]

        def head_norm_rope(t, g):
            n = t * lax.rsqrt(_seg_mean(t * t, gm128) + EPS)
            return _rope(n * g, cv, s1v, s2v)

        qs_ = (head_norm_rope(proj_ref[:, C_SQ:C_SQ + 128], gn_ref[:, G_Q:G_K]).astype(_MXU),
               head_norm_rope(proj_ref[:, C_SQ + 128:C_SQ + 256], gn_ref[:, G_Q:G_K]).astype(_MXU))
        kr = head_norm_rope(proj_ref[:, C_SK:C_SK + KV_W], gn_ref[:, G_K:G_XQ])
        sv = proj_ref[:, C_SV:C_SV + KV_W]
        ka = _place_kv(jnp.concatenate([kp_ref[...], kr], axis=0), 0.125)
        va = _place_kv(jnp.concatenate([vp_ref[...], sv], axis=0), 1.0)
        kp_ref[...] = kr[tm - BLOCK:tm, :]
        vp_ref[...] = sv[tm - BLOCK:tm, :]
        lane128 = lax.broadcasted_iota(jnp.int32, (1, 128), 1)
        for b in range(nb):
            mask = _swa_mask((i == 0) & (b == 0)) if b == 0 else _swa_mask(False)
            band = slice(BLOCK * b, BLOCK * b + 2 * BLOCK)
            blk = slice(BLOCK * b, BLOCK * (b + 1))
            psink = jnp.zeros((BLOCK, 128), F32)
            for j in range(4):
                p, pk = _swa_probs(qs_[j // 2][blk], ka[j][band], mask, sink_ref[0, j])
                pswa_ref[blk, 2 * BLOCK * j:2 * BLOCK * (j + 1)] = p.astype(pswa_ref.dtype)
                psink = jnp.where(lane128 == j, pk, psink)
            psink_ref[blk, :] = psink
            for h in range(2):
                yb_ref[blk, KV_W * h:KV_W * (h + 1)] = _mm(
                    pswa_ref[blk, 4 * BLOCK * h:4 * BLOCK * (h + 1)],
                    jnp.concatenate([va[2 * h][band], va[2 * h + 1][band]], axis=0))

        gm256 = _group_matrix(XATT_W)
        xq = proj_ref[:, C_XQ:C_XQ + XATT_W]
        qx = xq * lax.rsqrt(_seg_mean(xq * xq, gm256) + EPS) * gn_ref[:, G_XQ:G_XK]
        pm = _mem_probs(_mm_nt(qx, km_ref[...]))
        for j in range(4):
            pmem_ref[:, MEM_LEN * j:MEM_LEN * (j + 1)] = pm[j].astype(pmem_ref.dtype)
        yc = _mm(pmem_ref[...], vm_ref[...])
        yc_ref[...] = yc

        def gated(y, g, gate):
            return y * lax.rsqrt(_row_mean(y * y) + EPS) * g * (gate * _sigmoid(gate))

        ogv = og_ref[...]
        za = gated(ya_ref[...], ogv[:, :512], proj_ref[:, C_LRUG:C_LRUG + LRU_W])
        zb = gated(yb_ref[...], ogv[:, 512:768], proj_ref[:, C_SWAG:C_SWAG + SWA_W])
        zc = gated(yc, ogv[:, 768:], proj_ref[:, C_XG:C_XG + XATT_W])
        ycat_ref[:, 0:512] = za.astype(ycat_ref.dtype)
        ycat_ref[:, 512:768] = zb.astype(ycat_ref.dtype)
        ycat_ref[:, 768:1024] = zc.astype(ycat_ref.dtype)
        out = xv + _mm(ycat_ref[...], wout_ref[...])
        err = out - t_ref[...]
        dout_ref[...] = (err * (1.0 / D_MODEL)).astype(dout_ref.dtype)
        lacc_ref[...] = lacc_ref[...] + (0.5 / D_MODEL) * jnp.sum(err * err)

        @pl.when(i == nt - 1)
        def _():
            loss_ref[...] = lacc_ref[...]

    def rows(ncol):
        return pl.BlockSpec((tm, ncol), lambda i: (i, 0))

    in_specs = [rows(D_MODEL), rows(D_MODEL), rows(128), rows(128), rows(128),
                _const_spec((1, D_MODEL)), _const_spec((D_IN, D_MODEL), True), _const_spec((CONV_K, LRU_W)),
                _const_spec((1, LRU_W)), _const_spec((2, 256, 512), True), _const_spec((1, LRU_W)),
                _const_spec((1, LRU_W)), _const_spec((1, LRU_W)), _const_spec((1, GAINS_W)), pl.BlockSpec(memory_space=pltpu.SMEM),
                _const_spec((4 * MEM_LEN, XATT_W), True), _const_spec((4 * MEM_LEN, XATT_W), True),
                _const_spec((1, D_MODEL)), _const_spec((D_MODEL, D_MODEL), True)]
    out_shape = (jax.ShapeDtypeStruct((seq, D_IN), F32), jax.ShapeDtypeStruct((seq, LRU_W), F32),
                 jax.ShapeDtypeStruct((seq, SWA_W), F32), jax.ShapeDtypeStruct((seq, XATT_W), F32),
                 jax.ShapeDtypeStruct((seq, D_MODEL), _MXU), jax.ShapeDtypeStruct((seq, D_MODEL), _MXU),
                 jax.ShapeDtypeStruct((seq, D_MODEL), _MXU), jax.ShapeDtypeStruct((seq, 4 * 2 * BLOCK), _MXU),
                 jax.ShapeDtypeStruct((seq, 4 * MEM_LEN), _MXU), jax.ShapeDtypeStruct((seq, 128), F32),
                 jax.ShapeDtypeStruct((seq, 4 * LRU_W), _MXU), jax.ShapeDtypeStruct((seq, LRU_W), F32),
                 jax.ShapeDtypeStruct((8, 128), F32))
    out_specs = (rows(D_IN), rows(LRU_W), rows(SWA_W), rows(XATT_W), rows(D_MODEL), rows(D_MODEL), rows(D_MODEL),
                 rows(4 * 2 * BLOCK), rows(4 * MEM_LEN), rows(128), rows(4 * LRU_W), rows(LRU_W),
                 _const_spec((8, 128)))
    scratch = [pltpu.VMEM((tm + 8, LRU_W), F32), pltpu.VMEM((tm, LRU_W), F32),
               pltpu.VMEM((8, LRU_W), F32), pltpu.VMEM((BLOCK, KV_W), F32), pltpu.VMEM((BLOCK, KV_W), F32),
               pltpu.VMEM((8, 128), F32)]
    return pl.pallas_call(
        body, name="layer_fwd", grid=(nt,), out_shape=out_shape, in_specs=in_specs, out_specs=out_specs,
        scratch_shapes=scratch,
        compiler_params=pltpu.CompilerParams(dimension_semantics=("arbitrary",), vmem_limit_bytes=VMEM_LIMIT),
    )(x, tgt, rc, rs1, rs2, ng, win_t, cw, cb, wg, brg, big, lam, gains, sinks, km, vm, og, wout)


def weight_grads(ycat, dout, dproj, xn, g_kv, g_small, early_at, late_at):
    seq, ncol = xn.shape
    blk = 256
    n_out, n_in = ycat.shape[1] // blk, dproj.shape[1] // blk
    assert n_out == N_CHIPS and late_at[0] >= n_out
    g_out = jax.ShapeDtypeStruct((N_CHIPS, 2, blk // 2, dout.shape[1]), F32)
    shape_e, scratch_e = _hosted_reduce_shapes([g_kv], g_small)
    shape_l, scratch_l = _hosted_reduce_shapes([g_out], None)

    def body(l1_ref, r1_ref, l2_ref, r2_ref, kv_ref, sm_ref, o_ref, sum_out, sum_kv, sum_sm, gout_scr, *scratch):
        j = pl.program_id(0)

        def reduce_stages(closing):
            _hosted_reduce(j, n_out + n_in, closing, early_at, (kv_ref, sm_ref), (sum_kv, sum_sm),
                           scratch[:len(scratch_e)], True)
            _hosted_reduce(j, n_out + n_in, closing, late_at, (gout_scr,), (sum_out,), scratch[len(scratch_e):], False)

        reduce_stages(False)

        @pl.when(j < n_out)
        def _():
            gout_scr[j] = _mm_tn(l1_ref[...], r1_ref[...]).reshape(g_out.shape[1:])

        @pl.when(j >= n_out)
        def _():
            o_ref[...] = _mm_tn(l2_ref[...], r2_ref[...])

        reduce_stages(True)

    vm = pl.BlockSpec(memory_space=pltpu.VMEM)
    hbm = pl.BlockSpec(memory_space=pl.ANY)
    return pl.pallas_call(
        body, name="weight_grads", grid=(n_out + n_in,),
        out_shape=(jax.ShapeDtypeStruct((dproj.shape[1], ncol), F32), *shape_l, *shape_e),
        in_specs=[pl.BlockSpec((seq, blk), lambda j: (0, jnp.minimum(j, n_out - 1))), _const_spec(dout.shape, True),
                  pl.BlockSpec((seq, blk), lambda j: (0, jnp.maximum(j - n_out, 0))), _const_spec(xn.shape, True),
                  hbm, vm],
        out_specs=(pl.BlockSpec((blk, ncol), lambda j: (jnp.maximum(j - n_out, 0), 0)), hbm, hbm, hbm),
        scratch_shapes=[pltpu.VMEM(g_out.shape, F32)] + scratch_e + scratch_l,
        compiler_params=pltpu.CompilerParams(dimension_semantics=("arbitrary",), vmem_limit_bytes=VMEM_LIMIT),
    )(ycat, dout, dproj, xn, g_kv, g_small)


def layer_bwd(x, dout, proj, ya, yb, yc, pswa, pmem, psink, gates, a_all, rc, rs1, rs2, ng, win_t, cw, wg, lam, gains,
              km, vm, og, wout):
    seq = x.shape[0]
    tm = min(ROW_TILE, seq)
    nt = seq // tm
    nb = tm // BLOCK

    def body(x_ref, dout_ref, proj_ref, ya_ref, yb_ref, yc_ref, pswa_ref, pmem_ref, psink_ref, gates_ref, a_ref,
             c_ref, s1_ref, s2_ref,
             yah_ref, kvh_ref, ch_ref, s1h_ref, s2h_ref,
             ng_ref, win_ref, cw_ref, wg_ref, lam_ref, gn_ref, km_ref, vm_ref, og_ref, wout_ref,
             gx_ref, dproj_ref, gwg_ref, dkm_ref, dvm_ref, gng_ref, gog_ref, gcb_ref, gbrg_ref, gbig_ref, glam_ref,
             gcw_ref, gqn_ref, gkn_ref, gxqn_ref, gsink_ref,
             hext_ref, aext_ref, an_scr, dh_scr, g_scr, dxc_ext, gcar_ref, dkcar_ref, dvcar_ref):
        i = pl.program_id(0)
        tile = nt - 1 - i
        first_tile = tile == 0

        @pl.when(i == 0)
        def _():
            for r in (gwg_ref, dkm_ref, dvm_ref, gng_ref, gog_ref, gcb_ref, gbrg_ref, gbig_ref, glam_ref, gcw_ref,
                      gqn_ref, gkn_ref, gxqn_ref, gsink_ref, gcar_ref, dkcar_ref, dvcar_ref):
                r[...] = jnp.zeros_like(r)
            dxc_ext[tm:tm + 8, :] = jnp.zeros((8, LRU_W), F32)
            aext_ref[tm:tm + 8, :] = jnp.zeros((8, LRU_W), F32)

        xv = x_ref[...]
        dov = dout_ref[...]
        dz = _mm_nt(dov, wout_ref[...])
        ogv = og_ref[...]

        def group_bwd(y, gate, g, dzg):
            r = lax.rsqrt(_row_mean(y * y) + EPS)
            n = y * r
            sg = _sigmoid(gate)
            dgate = dzg * (n * g) * (sg * (1.0 + gate * (1.0 - sg)))
            dng = dzg * (gate * sg)
            dn = dng * g
            return r * (dn - n * _row_mean(dn * n)), dgate, _col_sum(dng * n)

        dya, dga, goa = group_bwd(ya_ref[...], proj_ref[:, C_LRUG:C_LRUG + LRU_W], ogv[:, :512], dz[:, :512])
        dyb, dgb, gob = group_bwd(yb_ref[...], proj_ref[:, C_SWAG:C_SWAG + SWA_W], ogv[:, 512:768], dz[:, 512:768])
        dyc, dgc, goc = group_bwd(yc_ref[...], proj_ref[:, C_XG:C_XG + XATT_W], ogv[:, 768:], dz[:, 768:])
        gog_ref[...] += jnp.concatenate([goa, gob, goc], axis=1)
        dproj_ref[:, C_LRUG:C_LRUG + LRU_W] = dga.astype(dproj_ref.dtype)
        dproj_ref[:, C_SWAG:C_SWAG + SWA_W] = dgb.astype(dproj_ref.dtype)
        dproj_ref[:, C_XG:C_XG + XATT_W] = dgc.astype(dproj_ref.dtype)

        gm256 = _group_matrix(XATT_W)
        xq = proj_ref[:, C_XQ:C_XQ + XATT_W]
        rq = lax.rsqrt(_seg_mean(xq * xq, gm256) + EPS)
        qn = xq * rq
        qx = qn * gn_ref[:, G_XQ:G_XK]
        qxb = qx.astype(_MXU)
        dycb = dyc.astype(_MXU)
        dp_all = _mm_nt(dycb, vm_ref[...])
        dsm = []
        for j in range(4):
            pj = pmem_ref[:, MEM_LEN * j:MEM_LEN * (j + 1)].astype(F32)
            dp = dp_all[:, MEM_LEN * j:MEM_LEN * (j + 1)]
            dsm.append((pj * (dp - jnp.sum(pj * dp, axis=-1, keepdims=True))).astype(_MXU))
        ds_all = jnp.concatenate(dsm, axis=1)
        dvm_ref[...] += _mm_tn(dycb, pmem_ref[...])
        dkm_ref[...] += _mm_tn(qxb, ds_all)
        dqx = _mm(ds_all, km_ref[...])
        gxqn_ref[...] += _col_sum(dqx * qn)
        dqn = dqx * gn_ref[:, G_XQ:G_XK]
        dproj_ref[:, C_XQ:C_XQ + XATT_W] = (rq * (dqn - qn * _seg_mean(dqn * qn, gm256))).astype(dproj_ref.dtype)

        gm128 = _group_matrix(KV_W)
        cv, s1v, s2v = c_ref[...], s1_ref[...], s2_ref[...]

        def head_norm(t):
            r = lax.rsqrt(_seg_mean(t * t, gm128) + EPS)
            return t * r, r

        qn_, qr_ = zip(head_norm(proj_ref[:, C_SQ:C_SQ + 128]), head_norm(proj_ref[:, C_SQ + 128:C_SQ + 256]))
        qrope = [_rope(qn_[h] * gn_ref[:, G_Q:G_K], cv, s1v, s2v).astype(_MXU) for h in range(2)]
        kn, krr = head_norm(proj_ref[:, C_SK:C_SK + KV_W])
        kr = _rope(kn * gn_ref[:, G_K:G_XQ], cv, s1v, s2v)
        khn, _ = head_norm(kvh_ref[:, 0:KV_W])
        khr = _rope(khn * gn_ref[:, G_K:G_XQ], ch_ref[...], s1h_ref[...], s2h_ref[...])
        ka = _place_kv(jnp.concatenate([khr, kr], axis=0), 0.125)
        va = _place_kv(jnp.concatenate([kvh_ref[:, KV_W:2 * KV_W], proj_ref[:, C_SV:C_SV + KV_W]], axis=0), 1.0)
        lane128 = lax.broadcasted_iota(jnp.int32, (1, 128), 1)
        gsink = jnp.zeros((1, 128), F32)
        dk_band, dv_band, dq_blk = [], [], []
        for b in range(nb):
            band = slice(BLOCK * b, BLOCK * b + 2 * BLOCK)
            blk = slice(BLOCK * b, BLOCK * (b + 1))
            dka, dva, dsb = [], [], []
            deltas = jnp.zeros((BLOCK, 128), F32)
            for j in range(4):
                qh = qrope[j // 2][blk]
                doh = dyb[blk, KV_W * (j // 2):KV_W * (j // 2 + 1)].astype(_MXU)
                pb = pswa_ref[blk, 2 * BLOCK * j:2 * BLOCK * (j + 1)]
                p = pb.astype(F32)
                dp = _mm_nt(doh, va[j][band])
                delta = jnp.sum(p * dp, axis=-1, keepdims=True)
                ds = (p * (dp - delta)).astype(_MXU)
                deltas = jnp.where(lane128 == j, delta, deltas)
                dva.append(_mm_tn(pb, doh))
                dka.append(_mm_tn(ds, qh))
                dsb.append(ds)
            gsink = gsink - _col_sum(psink_ref[blk, :] * deltas)
            dk_band.append(_unplace_kv(dka) * 0.125)
            dv_band.append(_unplace_kv(dva))
            dq_blk.append([_mm(jnp.concatenate(dsb[2 * h:2 * h + 2], axis=1),
                               jnp.concatenate([ka[2 * h][band], ka[2 * h + 1][band]], axis=0)) for h in range(2)])
        gsink_ref[...] += gsink
        dk_rows = [dk_band[b][BLOCK:] + (dk_band[b + 1][:BLOCK] if b + 1 < nb else dkcar_ref[...]) for b in range(nb)]
        dv_rows = [dv_band[b][BLOCK:] + (dv_band[b + 1][:BLOCK] if b + 1 < nb else dvcar_ref[...]) for b in range(nb)]
        dkcar_ref[...] = dk_band[0][:BLOCK]
        dvcar_ref[...] = dv_band[0][:BLOCK]
        dkg = _rope_bwd(jnp.concatenate(dk_rows, axis=0), cv, s1v, s2v)
        gkn = _col_sum(dkg * kn)
        dkn = dkg * gn_ref[:, G_K:G_XQ]
        dproj_ref[:, C_SK:C_SK + KV_W] = (krr * (dkn - kn * _seg_mean(dkn * kn, gm128))).astype(dproj_ref.dtype)
        dproj_ref[:, C_SV:C_SV + KV_W] = jnp.concatenate(dv_rows, axis=0).astype(dproj_ref.dtype)
        gqn = jnp.zeros((1, 128), F32)
        for h in range(2):
            dqg = _rope_bwd(jnp.concatenate([dq_blk[b][h] for b in range(nb)], axis=0), cv, s1v, s2v)
            gqn = gqn + _col_sum(dqg * qn_[h])
            dqn_ = dqg * gn_ref[:, G_Q:G_K]
            dproj_ref[:, C_SQ + 128 * h:C_SQ + 128 * (h + 1)] = (
                qr_[h] * (dqn_ - qn_[h] * _seg_mean(dqn_ * qn_[h], gm128))).astype(dproj_ref.dtype)
        gqn_ref[...] += gqn
        gkn_ref[...] += gkn

        u = proj_ref[:, C_LRUX:C_LRUX + LRU_W]
        xc, rg, ig, sq = (gates_ref[:, LRU_W * k:LRU_W * (k + 1)].astype(F32) for k in range(4))
        a = a_ref[...]
        sp = _softplus(-lam_ref[...])
        hext_ref[0:8, :] = jnp.where(first_tile, 0.0, yah_ref[...])
        hext_ref[8:8 + tm, :] = ya_ref[...]
        hprev = hext_ref[pl.ds(7, tm), :]
        aext_ref[0:tm, :] = a
        an_scr[...] = aext_ref[pl.ds(1, tm), :]
        dh_scr[...] = dya
        dh_scr[tm - 1:tm, :] = dh_scr[tm - 1:tm, :] + gcar_ref[0:1, :]
        row8 = lax.broadcasted_iota(jnp.int32, (8, LRU_W), 0)

        def scan_step(gi, carry):
            r0 = pl.multiple_of((tm // 8 - 1 - gi) * 8, 8)
            av = an_scr[pl.ds(r0, 8), :]
            bv = dh_scr[pl.ds(r0, 8), :]
            for d in (1, 2, 4):
                a_sh = jnp.where(row8 < 8 - d, pltpu.roll(av, 8 - d, 0), 1.0)
                b_sh = jnp.where(row8 < 8 - d, pltpu.roll(bv, 8 - d, 0), 0.0)
                bv = bv + av * b_sh
                av = av * a_sh
            gv = bv + av * carry
            g_scr[pl.ds(r0, 8), :] = gv
            return gv[0:1, :]

        g0 = lax.fori_loop(0, tm // 8, scan_step, jnp.zeros((1, LRU_W), F32), unroll=True)
        gcar_ref[0:1, :] = a[0:1, :] * g0
        gv = g_scr[...]
        da = gv * hprev
        dig = gv * sq * xc
        dxc = gv * sq * ig
        dla = da * a - gv * (ig * xc) * ((a * a) / sq)
        drg = dla * ((-LRU_C) * sp)
        glam_ref[...] += _col_sum(dla * rg)
        dpr = drg * rg * (1.0 - rg)
        dpi = dig * ig * (1.0 - ig)
        gbrg_ref[...] += _col_sum(dpr)
        gbig_ref[...] += _col_sum(dpi)
        dpre0 = jnp.concatenate([dpr[:, :256], dpi[:, :256]], axis=1).astype(_MXU)
        dpre1 = jnp.concatenate([dpr[:, 256:], dpi[:, 256:]], axis=1).astype(_MXU)
        gwg_ref[0] += _mm_tn(xc[:, :256], dpre0)
        gwg_ref[1] += _mm_tn(xc[:, 256:], dpre1)
        dxc = dxc + jnp.concatenate([_mm_nt(dpre0, wg_ref[0]), _mm_nt(dpre1, wg_ref[1])], axis=1)
        gcb_ref[...] += _col_sum(dxc)
        dxc_ext[0:tm, :] = dxc
        du = jnp.zeros((tm, LRU_W), F32)
        for k in range(CONV_K):
            later = dxc_ext[pl.ds(3 - k, tm), :]
            gcw_ref[k:k + 1, :] += _col_sum(later * u)
            du = du + cw_ref[k:k + 1, :] * later
        dxc_ext[tm:tm + 8, :] = dxc[0:8, :]
        dproj_ref[:, C_LRUX:C_LRUX + LRU_W] = du.astype(dproj_ref.dtype)

        dxn = _mm(dproj_ref[...], win_ref[...])
        rx = lax.rsqrt(_row_mean(xv * xv) + EPS)
        xh = xv * rx
        gng_ref[...] += _col_sum(dxn * xh)
        dxh = dxn * ng_ref[...]
        gx_ref[...] = dov.astype(F32) + rx * (dxh - xh * _row_mean(dxh * xh))

        @pl.when(i == nt - 1)
        def _():
            glam_ref[...] = glam_ref[...] * (LRU_C * _sigmoid(-lam_ref[...]))
            for r in (gqn_ref, gkn_ref, gxqn_ref):
                r[...] = _fold_heads(r[...])

    def rows(ncol, arr_cols_block=0):
        return pl.BlockSpec((tm, ncol), lambda i: (nt - 1 - i, arr_cols_block))

    def halo(nrow, ncol, colblk=0):
        per = tm // nrow
        return pl.BlockSpec((nrow, ncol), lambda i: (jnp.maximum((nt - 1 - i) * per - 1, 0), colblk))

    in_specs = [rows(D_MODEL), rows(D_MODEL), rows(D_IN), rows(LRU_W), rows(SWA_W), rows(XATT_W),
                rows(4 * 2 * BLOCK), rows(4 * MEM_LEN), rows(128), rows(4 * LRU_W), rows(LRU_W),
                rows(128), rows(128), rows(128),
                halo(8, LRU_W), halo(BLOCK, 2 * KV_W, C_SK // (2 * KV_W)),
                halo(BLOCK, 128), halo(BLOCK, 128), halo(BLOCK, 128),
                _const_spec((1, D_MODEL)), _const_spec((D_IN, D_MODEL), True), _const_spec((CONV_K, LRU_W)),
                _const_spec((2, 256, 512), True), _const_spec((1, LRU_W)), _const_spec((1, GAINS_W)),
                _const_spec((4 * MEM_LEN, XATT_W), True), _const_spec((4 * MEM_LEN, XATT_W), True),
                _const_spec((1, D_MODEL)), _const_spec((D_MODEL, D_MODEL), True)]
    small = [(2, 256, 512), (XATT_W, 4 * MEM_LEN), (XATT_W, 4 * MEM_LEN), (1, D_MODEL), (1, D_MODEL), (1, LRU_W), (1, LRU_W),
             (1, LRU_W), (1, LRU_W), (CONV_K, LRU_W), (1, 128), (1, 128), (1, XATT_W), (1, 128)]
    out_shape = (jax.ShapeDtypeStruct((seq, D_MODEL), F32), jax.ShapeDtypeStruct((seq, D_IN), _MXU)) + tuple(
        jax.ShapeDtypeStruct(s, F32) for s in small)
    out_specs = (rows(D_MODEL), rows(D_IN)) + tuple(_const_spec(s) for s in small)
    scratch = [pltpu.VMEM((tm + 8, LRU_W), F32), pltpu.VMEM((tm + 8, LRU_W), F32),
               pltpu.VMEM((tm, LRU_W), F32), pltpu.VMEM((tm, LRU_W), F32), pltpu.VMEM((tm, LRU_W), F32),
               pltpu.VMEM((tm + 8, LRU_W), F32),
               pltpu.VMEM((8, LRU_W), F32), pltpu.VMEM((BLOCK, KV_W), F32), pltpu.VMEM((BLOCK, KV_W), F32)]
    return pl.pallas_call(
        body, name="layer_bwd", grid=(nt,), out_shape=out_shape, in_specs=in_specs, out_specs=out_specs,
        scratch_shapes=scratch,
        compiler_params=pltpu.CompilerParams(dimension_semantics=("arbitrary",), vmem_limit_bytes=VMEM_LIMIT),
    )(x, dout, proj, ya, yb, yc, pswa, pmem, psink, gates, a_all, rc, rs1, rs2, ya, proj, rc, rs1, rs2,
      ng, win_t, cw, wg, lam, gains, km, vm, og, wout)


def _reduce_protocol(big, sm, outs, osm, r1, r1s, wire, r2, r2s, wire2, ps, own, send, recv, lsem):
    nbig = len(big)
    x, y, c = lax.axis_index("x"), lax.axis_index("y"), lax.axis_index("c")
    sibling = (x, y, 1 - c)
    near, far, diag = _partners(x, y, c)
    me, near_id, far_id, diag_id = _chip_of(x, y), _chip_of(*near), _chip_of(*far), _chip_of(*diag)

    def copy(k, src, dst, to):
        return pltpu.make_async_remote_copy(src_ref=src, dst_ref=dst, send_sem=send.at[k], recv_sem=recv.at[k],
                                            device_id=to, device_id_type=MESH)

    def sent(stage, a):
        if a == nbig:
            src, dst, to = ((sm.at[1 - c], r1s, sibling), (r1s, r2s.at[0], (*near, c)), (ps, r2s.at[1], (*far, c)),
                            (osm.at[c], osm.at[c], sibling))[stage]
            return [copy(5 * nbig + stage, src, dst, to)]
        if stage == 0:
            return [copy(5 * a, big[a].at[:, 1 - c], r1[a], sibling)]
        if stage == 1:
            return [copy(5 * a + 1, wire[a].at[near_id], r2[a].at[0], (*near, c)),
                    copy(5 * a + 2, wire[a].at[diag_id], r2[a].at[1], (*near, c))]
        if stage == 2:
            return [copy(5 * a + 3, wire2[a], r2[a].at[2], (*far, c))]
        return [copy(5 * a + 4, outs[a].at[c], outs[a].at[c], sibling)]

    arrays = range(nbig + (sm is not None))

    def start(stage, a):
        for cp in sent(stage, a):
            cp.start()

    def arrived(k, ref):
        copy(k, ref, ref, sibling).wait_recv()

    def loads():
        return [pltpu.make_async_copy(big[a].at[:, c], own[a], lsem.at[a]) for a in range(nbig)]

    def stage0():
        for a in arrays:
            start(0, a)
        for cp in loads():
            cp.start()

    def stage1():
        for a in range(nbig):
            loads()[a].wait()
            arrived(5 * a, r1[a])
            for k in range(N_CHIPS):
                r1[a][k] = own[a][k] + r1[a][k]
                wire[a][k] = r1[a][k].astype(wire[a].dtype)
            start(1, a)
        if sm is not None:
            arrived(5 * nbig, r1s)
            r1s[...] = sm[c] + r1s[...]
            start(1, nbig)

    def stage2():
        for a in range(nbig):
            arrived(5 * a + 1, r2[a].at[0])
            arrived(5 * a + 2, r2[a].at[1])
            r1[a][me] = r1[a][me] + r2[a][0].astype(F32)
            wire2[a][...] = (r1[a][far_id] + r2[a][1].astype(F32)).astype(wire2[a].dtype)
            start(2, a)
        if sm is not None:
            arrived(5 * nbig + 1, r2s.at[0])
            ps[...] = r1s[...] + r2s[0]
            start(2, nbig)

    def stage3():
        for a in range(nbig):
            arrived(5 * a + 3, r2[a].at[2])
            outs[a][c] = r1[a][me] + r2[a][2].astype(F32)
            start(3, a)
        if sm is not None:
            arrived(5 * nbig + 2, r2s.at[1])
            osm[c] = ps[...] + r2s[1]
            start(3, nbig)

    def stage4():
        for a in range(nbig):
            arrived(5 * a + 4, outs[a].at[1 - c])
        if sm is not None:
            arrived(5 * nbig + 3, osm.at[1 - c])
        for stage in range(4):
            for a in arrays:
                for cp in sent(stage, a):
                    cp.wait_send()

    return [stage0, stage1, stage2, stage3, stage4]


def _reduce_buffers(bigs, g_small):
    half = [b.shape[2:] for b in bigs]
    sm_half = None if g_small is None else g_small.shape[1:]
    out_shape = [jax.ShapeDtypeStruct((2,) + h, F32) for h in half]
    small = lambda lead: [] if g_small is None else [pltpu.VMEM(lead + sm_half, F32)]
    if g_small is not None:
        out_shape.append(jax.ShapeDtypeStruct(g_small.shape, F32))
    n_sem = 5 * len(bigs) + 4
    scratch = ([pltpu.VMEM((N_CHIPS,) + h, F32) for h in half] + small(())
               + [pltpu.VMEM((N_CHIPS,) + h, _WIRE) for h in half]
               + [pltpu.VMEM((3,) + h, _WIRE) for h in half] + small((2,))
               + [pltpu.VMEM(h, _WIRE) for h in half] + small(())
               + [pltpu.VMEM((N_CHIPS,) + h, F32) for h in half]
               + [pltpu.SemaphoreType.DMA((n_sem,)), pltpu.SemaphoreType.DMA((n_sem,)),
                  pltpu.SemaphoreType.DMA((len(bigs),))])
    return out_shape, scratch


def _split_reduce_refs(refs, nbig, has_small):
    it = iter(refs)
    take = lambda n: [next(it) for _ in range(n)]
    one = lambda: next(it) if has_small else None
    big, sm = take(nbig), one()
    outs, osm = take(nbig), one()
    r1, r1s, wire, r2, r2s, wire2, ps, own = take(nbig), one(), take(nbig), take(nbig), one(), take(nbig), one(), take(nbig)
    send, recv, lsem = take(3)
    return big, sm, outs, osm, r1, r1s, wire, r2, r2s, wire2, ps, own, send, recv, lsem


def _hosted_reduce_shapes(bigs, g_small):
    red_shape, scratch = _reduce_buffers(bigs, g_small)
    nres = len(red_shape)
    return red_shape, [pltpu.VMEM(r.shape, r.dtype) for r in red_shape] + scratch + [pltpu.SemaphoreType.DMA((nres,))]


def _hosted_reduce(step, n_steps, closing, stage_at, operands, results, scratch, has_small):
    nres = len(results)
    sums, rest, fsem = scratch[:nres], scratch[nres:-1], scratch[-1]
    refs = tuple(operands) + tuple(sums) + tuple(rest)

    def to_results():
        out = [pltpu.make_async_copy(sums[k], results[k], fsem.at[k]) for k in range(nres)]
        for cp in out:
            cp.start()
        for cp in out:
            cp.wait()

    stages = _reduce_protocol(*_split_reduce_refs(refs, nres - has_small, has_small))

    def last_stage():
        stages[-1]()
        to_results()

    for at, stage in zip(stage_at, stages[:-1] + [last_stage]):
        if closing == (at == n_steps):
            pl.when(step == min(at, n_steps - 1))(stage)


def reduce_grads(big, name, parts):
    chips, halves, rows_, cols = big.shape
    sub = jax.ShapeDtypeStruct((chips, halves, rows_ // parts, cols), big.dtype)

    def body(b_ref, o_ref, *scratch):
        refs = [b_ref.at[:, :, s] for s in range(parts)] + [o_ref.at[:, s] for s in range(parts)] + list(scratch)
        for stage in _reduce_protocol(*_split_reduce_refs(refs, parts, False)):
            stage()

    _, scratch = _reduce_buffers([sub] * parts, None)
    return pl.pallas_call(
        body, name=name, out_shape=jax.ShapeDtypeStruct((halves, parts, rows_ // parts, cols), F32),
        in_specs=[pl.BlockSpec(memory_space=pl.ANY)], out_specs=pl.BlockSpec(memory_space=pltpu.VMEM),
        scratch_shapes=scratch, compiler_params=pltpu.CompilerParams(vmem_limit_bytes=VMEM_LIMIT),
    )(big.reshape(chips, halves, parts, rows_ // parts, cols))


def adamw_matrices(items):
    plan, total = [], 0
    for w, _, _, _ in items:
        rows_, cols = w.shape
        tr = max(t for t in range(8, rows_ + 1, 8) if rows_ % t == 0 and t * cols * 4 <= ADAM_BLOCK_BYTES)
        plan.append((total, rows_ // tr, tr, cols))
        total += rows_ // tr
    nin = 4 * len(items)

    def body(*refs):
        i = pl.program_id(0)
        for k, (first, steps, _, _) in enumerate(plan):
            w_ref, g_ref, m_ref, v_ref = refs[4 * k:4 * k + 4]
            go_ref, d_ref, nm_ref, nv_ref = refs[nin + 4 * k:nin + 4 * k + 4]

            @pl.when((i >= first) & (i < first + steps))
            def _():
                gv = g_ref[...]
                go_ref[...] = gv
                d_ref[...], nm_ref[...], nv_ref[...] = _adam_update(w_ref[...], gv, m_ref[...], v_ref[...])

    specs, shapes = [], []
    for (first, steps, tr, cols), (w, _, _, _) in zip(plan, items):
        spec = pl.BlockSpec((tr, cols), lambda i, first=first, steps=steps: (jnp.clip(i - first, 0, steps - 1), 0))
        specs += [spec] * 4
        shapes += [jax.ShapeDtypeStruct(w.shape, F32)] * 4
    res = pl.pallas_call(
        body, name="adamw_matrices", grid=(total,), out_shape=tuple(shapes), in_specs=specs, out_specs=tuple(specs),
        compiler_params=pltpu.CompilerParams(dimension_semantics=("arbitrary",)),
    )(*[a for item in items for a in item])
    return [res[4 * k:4 * k + 4] for k in range(len(items))]


def _adam_update(w, g, m, v):
    nm = ADAM_B1 * m + (1.0 - ADAM_B1) * g
    nv = ADAM_B2 * v + (1.0 - ADAM_B2) * (g * g)
    m_hat = nm / (1.0 - ADAM_B1 ** ADAM_STEP)
    v_hat = nv / (1.0 - ADAM_B2 ** ADAM_STEP)
    return (-ADAM_LR) * (m_hat / (jnp.sqrt(v_hat) + ADAM_EPS) + ADAM_WD * w), nm, nv


def adamw_vectors(g_pack, ws, ms, vs):
    nvec = len(SMALL_VECTORS)
    n = nvec + len(SMALL_MATRICES)

    def body(*refs):
        pk = refs[0]
        w_refs, m_refs, v_refs = (refs[1 + k * n:1 + (k + 1) * n] for k in range(3))
        g_out, d_out, nm_out, nv_out = (refs[1 + (3 + k) * n:1 + (4 + k) * n] for k in range(4))
        refs[-1][...] = pk[LOSS_ROW:LOSS_ROW + 1, 0:1]
        chip = 2 * lax.axis_index("x") + lax.axis_index("y")
        for k, (name, row, width) in enumerate(SMALL_VECTORS):
            if name == "conv_w":
                g = jnp.concatenate([pk[pl.ds(row + 4 * t + chip, 1), :] for t in range(CONV_K)], axis=0)[None]
            elif width >= 128:
                g = jnp.concatenate([pk[row + r:row + r + 1, :] for r in range(width // 128)], axis=1)
            else:
                g = pk[row:row + 1, 0:width]
            g_out[k][...] = g
            d_out[k][...], nm_out[k][...], nv_out[k][...] = _adam_update(w_refs[k][...], g, m_refs[k][...], v_refs[k][...])
        for k in range(nvec, n):
            for b in range(LRU_BLOCKS):
                rows_ = pk[GATES_ROW + HEAD * b:GATES_ROW + HEAD * (b + 1), :]
                g = (pltpu.roll(rows_, HEAD, axis=1) if k > nvec else rows_)[:, 0:HEAD]
                g_out[k][0, b] = g
                d_out[k][0, b], nm_out[k][0, b], nv_out[k][0, b] = _adam_update(
                    w_refs[k][0, b], g, m_refs[k][0, b], v_refs[k][0, b])

    vm = pl.BlockSpec(memory_space=pltpu.VMEM)
    like = [jax.ShapeDtypeStruct(w.shape, F32) for w in ws]
    return pl.pallas_call(
        body, name="adamw_vectors", out_shape=(*like * 4, jax.ShapeDtypeStruct((1, 1), F32)),
        in_specs=[vm] * (1 + 3 * n), out_specs=(vm,) * (4 * n + 1),
    )(g_pack, *ws, *ms, *vs)


SMALL_VECTORS = (("norm_g", 0, 1024), ("mem_norm_g", 8, 1024), ("conv_w", 16, 512), ("conv_b", 32, 512),
                 ("b_rg", 36, 512), ("b_ig", 40, 512), ("lru_lambda", 44, 512), ("q_norm_g", 48, 64),
                 ("k_norm_g", 49, 64), ("sinks", 50, 4), ("xq_norm_g", 51, 64), ("xk_norm_g", 52, 64),
                 ("out_norm_g", 53, 1024))
LOSS_ROW = 61
SMALL_MATRICES = ("w_rg", "w_ig")
GATES_ROW = 64
SMALL_ROWS = GATES_ROW + LRU_BLOCKS * HEAD


def _rope_tables(seq):
    pos = np.arange(seq, dtype=np.float32)
    inv_freq = (np.float32(ROPE_THETA) ** (-(np.arange(0, ROPE_DIM, 2, dtype=np.float32) / np.float32(ROPE_DIM)))
                ).astype(np.float32)
    ang = (pos[:, None] * inv_freq[None, :]).astype(np.float32)
    cos, sin = np.cos(ang).astype(np.float32), np.sin(ang).astype(np.float32)
    z = lambda n: np.zeros((seq, n), np.float32)
    c64 = np.concatenate([cos, cos, np.ones((seq, HEAD - ROPE_DIM), np.float32)], axis=1)
    s1_64 = np.concatenate([-sin, z(HEAD - 8)], axis=1)
    s2_64 = np.concatenate([z(8), sin, z(HEAD - ROPE_DIM)], axis=1)
    return tuple(jnp.asarray(np.concatenate([t, t], axis=1)) for t in (c64, s1_64, s2_64))


def kernel(x, mem, norm_g, mem_norm_g, w_in, conv_w, conv_b, w_rg, b_rg, w_ig, b_ig, lru_lambda, q_norm_g, k_norm_g, sinks, w_mem_kv, xq_norm_g, xk_norm_g, out_norm_g, w_out, loss_target, m_norm_g, m_mem_norm_g, m_w_in, m_conv_w, m_conv_b, m_w_rg, m_b_rg, m_w_ig, m_b_ig, m_lru_lambda, m_q_norm_g, m_k_norm_g, m_sinks, m_w_mem_kv, m_xq_norm_g, m_xk_norm_g, m_out_norm_g, m_w_out, v_norm_g, v_mem_norm_g, v_w_in, v_conv_w, v_conv_b, v_w_rg, v_b_rg, v_w_ig, v_b_ig, v_lru_lambda, v_q_norm_g, v_k_norm_g, v_sinks, v_w_mem_kv, v_xq_norm_g, v_xk_norm_g, v_out_norm_g, v_w_out):
    seq = x.shape[1]
    xs, tgt, mems = x[0], loss_target[0], mem[0]

    win_t, wout, wkv, cw, wg, gains, km, vm = gather_weights(
        w_in[0].T, w_out[0], w_mem_kv[0], conv_w, w_rg, w_ig, (q_norm_g, k_norm_g, xq_norm_g, xk_norm_g), mems,
        mem_norm_g)
    rc, rs1, rs2 = _rope_tables(seq)
    proj, ya, yb, yc, ycat, xn, dout, pswa, pmem, psink, gates, a_all, loss8 = layer_fwd(
        xs, tgt, rc, rs1, rs2, norm_g, win_t, cw, conv_b, wg, b_rg, b_ig, lru_lambda, gains, sinks, km, vm,
        out_norm_g, wout)
    (gx, dproj, g_wg, dkm, dvm, g_ng, g_og, g_cb, g_brg, g_big, g_lam, g_cw, g_qn, g_kn, g_xqn, g_sink) = layer_bwd(
        xs, dout, proj, ya, yb, yc, pswa, pmem, psink, gates, a_all, rc, rs1, rs2, norm_g, win_t, cw, wg, lru_lambda,
        gains, km, vm, out_norm_g, wout)
    g_wkv, small_g = mem_bwd(mems, mem_norm_g, wkv, gains, dkm, dvm, g_wg, loss8, dict(
        norm_g=g_ng, conv_w=g_cw, conv_b=g_cb, b_rg=g_brg, b_ig=g_big, lru_lambda=g_lam, q_norm_g=g_qn, k_norm_g=g_kn,
        sinks=g_sink, xq_norm_g=g_xqn, out_norm_g=g_og))
    g_win_t, r_out, r_kv, r_small = weight_grads(
        ycat, dout, dproj, xn, g_wkv.reshape(N_CHIPS, 2, D_MODEL // 8, 2 * XATT_W),
        small_g.reshape(2, SMALL_ROWS // 2, 128), (0, 1, 4, 7, 8), (4, 5, 10, 12, 13))
    r_in = reduce_grads(g_win_t.reshape(N_CHIPS, 2, D_IN // 8, D_MODEL), "reduce_w_in", 6)

    r_small = r_small.reshape(SMALL_ROWS, 128)
    grads = {}
    weights = dict(norm_g=norm_g, mem_norm_g=mem_norm_g, w_in=w_in, conv_w=conv_w, conv_b=conv_b, w_rg=w_rg, b_rg=b_rg,
                   w_ig=w_ig, b_ig=b_ig, lru_lambda=lru_lambda, q_norm_g=q_norm_g, k_norm_g=k_norm_g, sinks=sinks,
                   w_mem_kv=w_mem_kv, xq_norm_g=xq_norm_g, xk_norm_g=xk_norm_g, out_norm_g=out_norm_g, w_out=w_out)
    ms = dict(norm_g=m_norm_g, mem_norm_g=m_mem_norm_g, w_in=m_w_in, conv_w=m_conv_w, conv_b=m_conv_b, w_rg=m_w_rg,
              b_rg=m_b_rg, w_ig=m_w_ig, b_ig=m_b_ig, lru_lambda=m_lru_lambda, q_norm_g=m_q_norm_g, k_norm_g=m_k_norm_g,
              sinks=m_sinks, w_mem_kv=m_w_mem_kv, xq_norm_g=m_xq_norm_g, xk_norm_g=m_xk_norm_g,
              out_norm_g=m_out_norm_g, w_out=m_w_out)
    vs = dict(norm_g=v_norm_g, mem_norm_g=v_mem_norm_g, w_in=v_w_in, conv_w=v_conv_w, conv_b=v_conv_b, w_rg=v_w_rg,
              b_rg=v_b_rg, w_ig=v_w_ig, b_ig=v_b_ig, lru_lambda=v_lru_lambda, q_norm_g=v_q_norm_g, k_norm_g=v_k_norm_g,
              sinks=v_sinks, w_mem_kv=v_w_mem_kv, xq_norm_g=v_xq_norm_g, xk_norm_g=v_xk_norm_g,
              out_norm_g=v_out_norm_g, w_out=v_w_out)

    delta, new_m, new_v = {}, {}, {}
    res_in, res_out, res_kv = adamw_matrices([
        (w_in[0].T, r_in.reshape(D_IN // 4, D_MODEL), m_w_in[0].T, v_w_in[0].T),
        (w_out[0], r_out.reshape(D_MODEL // 4, D_MODEL), m_w_out[0], v_w_out[0]),
        (w_mem_kv[0], r_kv.reshape(D_MODEL // 4, 2 * XATT_W), m_w_mem_kv[0], v_w_mem_kv[0])])
    grads["w_in"], delta["w_in"], new_m["w_in"], new_v["w_in"] = (r.T[None] for r in res_in)
    grads["w_out"], delta["w_out"], new_m["w_out"], new_v["w_out"] = (r[None] for r in res_out)
    grads["w_mem_kv"], delta["w_mem_kv"], new_m["w_mem_kv"], new_v["w_mem_kv"] = (r[None] for r in res_kv)
    small_names = [n for n, _, _ in SMALL_VECTORS] + list(SMALL_MATRICES)
    res = adamw_vectors(r_small, [weights[n] for n in small_names], [ms[n] for n in small_names],
                        [vs[n] for n in small_names])
    nall = len(small_names)
    for k, into in enumerate((grads, delta, new_m, new_v)):
        into.update(zip(small_names, res[k * nall:(k + 1) * nall]))
    loss = res[-1].reshape(())

    order = ("norm_g", "mem_norm_g", "w_in", "conv_w", "conv_b", "w_rg", "b_rg", "w_ig", "b_ig", "lru_lambda",
             "q_norm_g", "k_norm_g", "sinks", "w_mem_kv", "xq_norm_g", "xk_norm_g", "out_norm_g", "w_out")
    return (loss, gx[None], *[grads[n] for n in order], *[delta[n] for n in order], *[new_m[n] for n in order],
            *[new_v[n] for n in order])
```

```python
import jax
import jax.numpy as jnp
import numpy as np
from jax import lax
from jax.experimental import pallas as pl
from jax.experimental.pallas import tpu as pltpu

F32 = jnp.float32
_MXU = jnp.bfloat16
_WIRE = jnp.bfloat16

D_MODEL = 1024
MEM_LEN = 256
HEAD = 64
LRU_W = 512
LRU_BLOCKS = 8
CONV_K = 4
LRU_C = 8.0
SWA_W = 256
KV_W = 128
XATT_W = 256
BLOCK = 128
D_IN = 2304
ROPE_THETA = 500000.0
ROPE_DIM = 16
EPS = 1e-6
NEG_INF = -1e30
C_LRUX, C_LRUG, C_SQ, C_SK, C_SV, C_SWAG, C_XQ, C_XG = 0, 512, 1024, 1280, 1408, 1536, 1792, 2048
G_Q, G_K, G_XQ, G_XK, GAINS_W = 0, 128, 256, 512, 768

ADAM_LR, ADAM_B1, ADAM_B2, ADAM_EPS, ADAM_WD, ADAM_STEP = 0.001, 0.9, 0.999, 1e-08, 0.01, 10

N_CHIPS = 4
ROW_TILE = 256
VMEM_LIMIT = 56 * 1024 * 1024
ADAM_BLOCK_BYTES = 640 * 1024
MESH = pl.DeviceIdType.MESH


def _mm(a, b):
    return jnp.dot(a.astype(_MXU), b.astype(_MXU), preferred_element_type=F32)


def _mm_nt(a, b):
    return lax.dot_general(a.astype(_MXU), b.astype(_MXU), (((1,), (1,)), ((), ())), preferred_element_type=F32)


def _mm_tn(a, b):
    return lax.dot_general(a.astype(_MXU), b.astype(_MXU), (((0,), (0,)), ((), ())), preferred_element_type=F32)


def _group_matrix(width):
    r = lax.shift_right_logical(lax.broadcasted_iota(jnp.int32, (width, width), 0), 6)
    c = lax.shift_right_logical(lax.broadcasted_iota(jnp.int32, (width, width), 1), 6)
    return (r == c).astype(_MXU)


def _seg_mean(x, gm):
    return jnp.dot(x.astype(_MXU), gm, preferred_element_type=F32) * (1.0 / HEAD)


def _row_mean(x):
    return jnp.mean(x, axis=-1, keepdims=True)


def _col_sum(x):
    return jnp.sum(x, axis=0, keepdims=True)


def _sigmoid(x):
    return jax.nn.sigmoid(x)


def _softplus(z):
    e = jnp.exp(-jnp.abs(z))
    u = 1.0 + e
    log1p_e = jnp.where(u == 1.0, e, jnp.log(u) * (e / (u - 1.0)))
    return jnp.maximum(z, 0.0) + log1p_e


def _rope(t, c, s1, s2):
    return t * c + pltpu.roll(t, 120, 1) * s1 + pltpu.roll(t, 8, 1) * s2


def _rope_bwd(d, c, s1, s2):
    return d * c + pltpu.roll(d * s1, 8, 1) + pltpu.roll(d * s2, 120, 1)


def _fold_heads(v):
    out = v
    for k in range(1, v.shape[1] // HEAD):
        out = out + pltpu.roll(v, HEAD * k, 1)
    return out


def _lane_mask(width, lo, hi):
    lane = lax.broadcasted_iota(jnp.int32, (1, width), 1)
    return ((lane >= lo) & (lane < hi)).astype(F32)


def _swa_mask(first_block):
    qi = lax.broadcasted_iota(jnp.int32, (BLOCK, 2 * BLOCK), 0)
    kj = lax.broadcasted_iota(jnp.int32, (BLOCK, 2 * BLOCK), 1)
    rel = qi + BLOCK - kj
    ok = (rel >= 0) & (rel < BLOCK)
    return ok & (jnp.logical_not(first_block) | (kj >= BLOCK))


def _place_kv(t, scale):
    lo = t * (_lane_mask(KV_W, 0, HEAD) * scale)
    hi = t * (_lane_mask(KV_W, HEAD, KV_W) * scale)
    return [a.astype(_MXU) for a in (lo, pltpu.roll(lo, HEAD, 1), pltpu.roll(hi, HEAD, 1), hi)]


def _unplace_kv(d):
    return (_lane_mask(KV_W, 0, HEAD) * (d[0] + pltpu.roll(d[1], HEAD, 1))
            + _lane_mask(KV_W, HEAD, KV_W) * (d[3] + pltpu.roll(d[2], HEAD, 1)))


def _swa_probs(qh, ka, mask, sink):
    s = _mm_nt(qh, ka)
    s = jnp.where(mask, s, NEG_INF)
    m = jnp.maximum(jnp.max(s, axis=-1, keepdims=True), sink)
    p = jnp.exp(s - m)
    esink = jnp.exp(sink - m)
    inv = 1.0 / (jnp.sum(p, axis=-1, keepdims=True) + esink)
    return p * inv, esink * inv


def _mem_probs(s_all):
    out = []
    for j in range(4):
        s = s_all[:, MEM_LEN * j:MEM_LEN * (j + 1)]
        p = jnp.exp(s - jnp.max(s, axis=-1, keepdims=True))
        out.append(p * (1.0 / jnp.sum(p, axis=-1, keepdims=True)))
    return out


def _head_rows(t, scale):
    return jnp.concatenate([t * (_lane_mask(XATT_W, HEAD * j, HEAD * (j + 1)) * scale) for j in range(4)], axis=0)


def _lru_gates(xc, wg_ref, brg, big, lam):
    p0 = _mm(xc[:, :256], wg_ref[0])
    p1 = _mm(xc[:, 256:], wg_ref[1])
    rg = _sigmoid(jnp.concatenate([p0[:, :256], p1[:, :256]], axis=1) + brg)
    ig = _sigmoid(jnp.concatenate([p0[:, 256:], p1[:, 256:]], axis=1) + big)
    sp = _softplus(-lam)
    la = (-LRU_C) * rg * sp
    a = jnp.exp(la)
    th = jnp.tanh(la)
    one_minus_a2 = (-2.0 * th) / (1.0 - th)
    return rg, ig, sp, a, jnp.sqrt(one_minus_a2)


def _const_spec(shape, single=False):
    zeros = (0,) * len(shape)
    if single:
        return pl.BlockSpec(shape, lambda i: zeros, pipeline_mode=pl.Buffered(1))
    return pl.BlockSpec(shape, lambda i: zeros)


def _chip_of(x, y):
    return 2 * x + y


def _partners(x, y, c):
    north = c == 1
    near = (jnp.where(north, 1 - x, x), jnp.where(north, y, 1 - y))
    far = (jnp.where(north, x, 1 - x), jnp.where(north, 1 - y, y))
    return near, far, (1 - x, 1 - y)


def gather_weights(win_t, wout, wkv, conv_w, w_rg, w_ig, head_gains, mem, mem_g):
    arrs = (win_t, wout, wkv)
    n = len(arrs)
    pieces = [(a, k * (arr.shape[0] // (2 * cut)), arr.shape[0] // (2 * cut))
              for a, (arr, cut) in enumerate(zip(arrs, (2, 1, 1))) for k in range(cut)]
    npc = len(pieces)

    def body(a0, a1, a2, cw_in, wrg_ref, wig_ref, q_ref, k_ref, xq_ref, xk_ref, mem_ref, mg_ref,
             o0, o1, o2, cw_out, wg_ref, gn_ref, km_ref, vm_ref, s0, s1, s2, cw, ocw, send, recv, lsem):
        ins, outs = (s0, s1, s2), (o0, o1, o2)
        for src, dst in zip((a0, a1, a2), ins):
            dst[...] = src[...].astype(dst.dtype)
        cw[...] = jnp.zeros(cw.shape, F32)
        cw[0:CONV_K, :] = cw_in[0]
        x, y, c = lax.axis_index("x"), lax.axis_index("y"), lax.axis_index("c")
        sibling = (x, y, 1 - c)
        near, far, diag = _partners(x, y, c)
        chips = [near, far, diag]
        me = _chip_of(x, y)

        def landed(p, chip, half):
            a, off, rows_ = pieces[p]
            r = ins[a].shape[0]
            return outs[a].at[pl.ds(pl.multiple_of(chip * r + half * (r // 2) + off, 16), rows_)]

        def mine(p):
            a, off, rows_ = pieces[p]
            return ins[a].at[pl.ds(pl.multiple_of(c * (ins[a].shape[0] // 2) + off, 16), rows_)]

        def copy(k, src, dst, to):
            return pltpu.make_async_remote_copy(src_ref=src, dst_ref=dst, send_sem=send.at[k], recv_sem=recv.at[k],
                                                device_id=to, device_id_type=MESH)

        def cw_rows(chip):
            return ocw.at[pl.ds(pl.multiple_of(chip * 8, 8), 8)]

        locals_ = []
        for a in range(n):
            r = ins[a].shape[0]
            locals_.append(pltpu.make_async_copy(ins[a], outs[a].at[pl.ds(pl.multiple_of(me * r, 16), r)], lsem.at[a]))
        locals_.append(pltpu.make_async_copy(cw, cw_rows(me), lsem.at[n]))
        for cp in locals_:
            cp.start()

        sent = []
        for p in range(npc):
            for j in range(2):
                sent.append(copy(p * 6 + j, mine(p), landed(p, me, c), (*chips[j], c)))
        for j, chip in enumerate(chips):
            sent.append(copy(npc * 6 + j, cw, cw_rows(me), (*chip, c)))
        for cp in sent:
            cp.start()

        gn_ref[...] = jnp.concatenate([q_ref[...]] * 2 + [k_ref[...]] * 2 + [xq_ref[...]] * 4 + [xk_ref[...]] * 4,
                                      axis=1)
        zeros = lambda lanes: [jnp.zeros((HEAD, lanes), F32)] if lanes else []
        for h in range(2):
            for b in range(4):
                row = []
                for w_ref in (wrg_ref, wig_ref):
                    row += zeros(HEAD * b) + [w_ref[0, 4 * h + b]] + zeros(HEAD * (3 - b))
                wg_ref[h, HEAD * b:HEAD * (b + 1), :] = jnp.concatenate(row, axis=1).astype(wg_ref.dtype)

        for j in range(3):
            for p in range(npc):
                got = landed(p, _chip_of(*chips[j]), c)
                copy(p * 6 + j, got, got, sibling).wait_recv()
                if j == 0:
                    sent.append(copy(p * 6 + 2, got, got, (*far, c)))
                    sent[-1].start()
                sent.append(copy(p * 6 + 3 + j, got, got, sibling))
                sent[-1].start()
        for p in range(npc):
            for j in range(3):
                got = landed(p, _chip_of(*chips[(1, 0, 2)[j]]), 1 - c)
                copy(p * 6 + 3 + j, got, got, sibling).wait_recv()
        for j, chip in enumerate(chips):
            got = cw_rows(_chip_of(*chip))
            copy(npc * 6 + j, got, got, (*chip, c)).wait_recv()
        for cp in sent:
            cp.wait_send()
        for cp in locals_:
            cp.wait()
        for chip in range(N_CHIPS):
            cw_out[:, 128 * chip:128 * (chip + 1)] = ocw[8 * chip:8 * chip + CONV_K, :]

        mem_v = mem_ref[...]
        mn = mem_v * lax.rsqrt(_row_mean(mem_v * mem_v) + EPS) * mg_ref[...]
        mkv = _mm(mn, o2[...])
        kpre = mkv[:, :XATT_W]
        km = kpre * lax.rsqrt(_seg_mean(kpre * kpre, _group_matrix(XATT_W)) + EPS) * gn_ref[:, G_XK:GAINS_W]
        km_ref[...] = _head_rows(km, 0.125).astype(km_ref.dtype)
        vm_ref[...] = _head_rows(mkv[:, XATT_W:], 1.0).astype(vm_ref.dtype)

    vm = pl.BlockSpec(memory_space=pltpu.VMEM)
    hbm = pl.BlockSpec(memory_space=pl.ANY)
    head_rows = jax.ShapeDtypeStruct((4 * MEM_LEN, XATT_W), _MXU)
    out_shape = tuple(jax.ShapeDtypeStruct((N_CHIPS * a.shape[0],) + a.shape[1:], _MXU) for a in arrs) + (
        jax.ShapeDtypeStruct((CONV_K, LRU_W), F32), jax.ShapeDtypeStruct((2, 256, 512), _MXU),
        jax.ShapeDtypeStruct((1, GAINS_W), F32), head_rows, head_rows)
    n_rdma = npc * 6 + 3
    return pl.pallas_call(
        body, name="gather_weights", out_shape=out_shape,
        in_specs=[vm] * 12, out_specs=(hbm, hbm, vm, vm, vm, vm, vm, vm),
        scratch_shapes=[pltpu.VMEM(a.shape, _MXU) for a in arrs] + [
            pltpu.VMEM((8, 128), F32), pltpu.VMEM((N_CHIPS * 8, 128), F32),
            pltpu.SemaphoreType.DMA((n_rdma,)), pltpu.SemaphoreType.DMA((n_rdma,)), pltpu.SemaphoreType.DMA((n + 1,))],
        compiler_params=pltpu.CompilerParams(vmem_limit_bytes=VMEM_LIMIT),
    )(win_t, wout, wkv, conv_w, w_rg, w_ig, *head_gains, mem, mem_g)


def _mem_bwd_and_pack(mem_ref, g_ref, w_ref, gn_ref, dkm_ref, dvm_ref, gg_ref, loss_ref, vectors, gw_ref, pk_ref):
    first_row = {name: (row, width) for name, row, width in SMALL_VECTORS}
    half_rows = SMALL_ROWS // 2
    pk_ref[...] = jnp.zeros(pk_ref.shape, F32)

    def rows_at(at, n):
        assert at // half_rows == (at + n - 1) // half_rows
        return at // half_rows, slice(at % half_rows, at % half_rows + n), slice(None)

    def put(name, src):
        row, width = first_row[name]
        per_row = 1 if width < 128 else src.shape[1] // 128
        for t in range(src.shape[0]):
            for r in range(per_row):
                pk_ref[rows_at(row + per_row * t + r, 1)] = src[t:t + 1, 128 * r:128 * (r + 1)]

    for name, ref in vectors.items():
        put(name, ref)
    pk_ref[rows_at(LOSS_ROW, 1)] = loss_ref[0:1, :]
    upper = lax.broadcasted_iota(jnp.int32, (HEAD, 128), 1) >= HEAD
    for h in range(2):
        for b in range(4):
            rg = gg_ref[h, HEAD * b:HEAD * (b + 1), 128 * (b // 2):128 * (b // 2 + 1)]
            ig = gg_ref[h, HEAD * b:HEAD * (b + 1), 256 + 128 * (b // 2):256 + 128 * (b // 2 + 1)]
            if b % 2:
                rg = pltpu.roll(rg, HEAD, axis=1)
            else:
                ig = pltpu.roll(ig, HEAD, axis=1)
            pk_ref[rows_at(GATES_ROW + HEAD * (4 * h + b), HEAD)] = jnp.where(upper, ig, rg)

    mem_v = mem_ref[...]
    mh = mem_v * lax.rsqrt(_row_mean(mem_v * mem_v) + EPS)
    mn = mh * g_ref[...]
    mkv = _mm(mn, w_ref[...])
    kpre = mkv[:, :XATT_W]
    gm = _group_matrix(XATT_W)
    rk = lax.rsqrt(_seg_mean(kpre * kpre, gm) + EPS)
    kn = kpre * rk
    dk = jnp.zeros((MEM_LEN, XATT_W), F32)
    dv = jnp.zeros((MEM_LEN, XATT_W), F32)
    for j in range(4):
        mj = _lane_mask(XATT_W, HEAD * j, HEAD * (j + 1))
        dk = dk + dkm_ref[:, MEM_LEN * j:MEM_LEN * (j + 1)].T * (mj * 0.125)
        dv = dv + dvm_ref[:, MEM_LEN * j:MEM_LEN * (j + 1)].T * mj
    put("xk_norm_g", _fold_heads(_col_sum(dk * kn)))
    dkn = dk * gn_ref[:, G_XK:GAINS_W]
    dkpre = rk * (dkn - kn * _seg_mean(dkn * kn, gm))
    dmkv = jnp.concatenate([dkpre, dv], axis=1)
    gw_ref[...] = _mm_tn(mn, dmkv).reshape(gw_ref.shape)
    dmn = _mm_nt(dmkv, w_ref[...])
    put("mem_norm_g", _col_sum(dmn * mh))


def layer_fwd(x, tgt, rc, rs1, rs2, ng, win_t, cw, cb, wg, brg, big, lam, gains, sinks, km, vm, og, wout):
    seq = x.shape[0]
    tm = min(ROW_TILE, seq)
    nt = seq // tm
    nb = tm // BLOCK

    def body(x_ref, t_ref, c_ref, s1_ref, s2_ref, ng_ref, win_ref, cw_ref, cb_ref, wg_ref, brg_ref, big_ref, lam_ref,
             gn_ref, sink_ref, km_ref, vm_ref, og_ref, wout_ref,
             proj_ref, ya_ref, yb_ref, yc_ref, ycat_ref, xn_ref, dout_ref, pswa_ref, pmem_ref, psink_ref, gates_ref,
             a_ref, loss_ref,
             ext_ref, b_scr, hc_ref, kp_ref, vp_ref, lacc_ref):
        i = pl.program_id(0)

        @pl.when(i == 0)
        def _():
            ext_ref[0:8, :] = jnp.zeros((8, LRU_W), F32)
            hc_ref[...] = jnp.zeros_like(hc_ref)
            kp_ref[...] = jnp.zeros_like(kp_ref)
            vp_ref[...] = jnp.zeros_like(vp_ref)
            lacc_ref[...] = jnp.zeros_like(lacc_ref)

        xv = x_ref[...]
        xn = (xv * lax.rsqrt(_row_mean(xv * xv) + EPS) * ng_ref[...]).astype(_MXU)
        xn_ref[...] = xn.astype(xn_ref.dtype)
        proj_ref[...] = _mm_nt(xn, win_ref[...])

        u = proj_ref[:, C_LRUX:C_LRUX + LRU_W]
        ext_ref[8:8 + tm, :] = u
        xc = cb_ref[...]
        for k in range(CONV_K):
            xc = xc + cw_ref[k:k + 1, :] * ext_ref[pl.ds(5 + k, tm), :]
        ext_ref[0:8, :] = u[tm - 8:tm, :]
        rg, ig, sp, a, sq = _lru_gates(xc, wg_ref, brg_ref[...], big_ref[...], lam_ref[...])
        for k, t in enumerate((xc, rg, ig, sq)):
            gates_ref[:, LRU_W * k:LRU_W * (k + 1)] = t.astype(gates_ref.dtype)
        a_ref[...] = a
        b_scr[...] = sq * (ig * xc)
        row8 = lax.broadcasted_iota(jnp.int32, (8, LRU_W), 0)

        def scan_step(g, carry):
            r0 = pl.multiple_of(g * 8, 8)
            av = a_ref[pl.ds(r0, 8), :]
            bv = b_scr[pl.ds(r0, 8), :]
            for d in (1, 2, 4):
                a_sh = jnp.where(row8 >= d, pltpu.roll(av, d, 0), 1.0)
                b_sh = jnp.where(row8 >= d, pltpu.roll(bv, d, 0), 0.0)
                bv = bv + av * b_sh
                av = av * a_sh
            hv = bv + av * carry
            ya_ref[pl.ds(r0, 8), :] = hv
            return hv[7:8, :]

        hc_ref[0:1, :] = lax.fori_loop(0, tm // 8, scan_step, hc_ref[0:1, :], unroll=True)

        gm128 = _group_matrix(KV_W)
        cv, s1v, s2v = c_ref[...], s1_ref[...], s2_ref[...]

        def head_norm_rope(t, g):
            n = t * lax.rsqrt(_seg_mean(t * t, gm128) + EPS)
            return _rope(n * g, cv, s1v, s2v)

        qs_ = (head_norm_rope(proj_ref[:, C_SQ:C_SQ + 128], gn_ref[:, G_Q:G_K]).astype(_MXU),
               head_norm_rope(proj_ref[:, C_SQ + 128:C_SQ + 256], gn_ref[:, G_Q:G_K]).astype(_MXU))
        kr = head_norm_rope(proj_ref[:, C_SK:C_SK + KV_W], gn_ref[:, G_K:G_XQ])
        sv = proj_ref[:, C_SV:C_SV + KV_W]
        ka = _place_kv(jnp.concatenate([kp_ref[...], kr], axis=0), 0.125)
        va = _place_kv(jnp.concatenate([vp_ref[...], sv], axis=0), 1.0)
        kp_ref[...] = kr[tm - BLOCK:tm, :]
        vp_ref[...] = sv[tm - BLOCK:tm, :]
        lane128 = lax.broadcasted_iota(jnp.int32, (1, 128), 1)
        for b in range(nb):
            mask = _swa_mask((i == 0) & (b == 0)) if b == 0 else _swa_mask(False)
            band = slice(BLOCK * b, BLOCK * b + 2 * BLOCK)
            blk = slice(BLOCK * b, BLOCK * (b + 1))
            psink = jnp.zeros((BLOCK, 128), F32)
            for j in range(4):
                p, pk = _swa_probs(qs_[j // 2][blk], ka[j][band], mask, sink_ref[0, j])
                pswa_ref[blk, 2 * BLOCK * j:2 * BLOCK * (j + 1)] = p.astype(pswa_ref.dtype)
                psink = jnp.where(lane128 == j, pk, psink)
            psink_ref[blk, :] = psink
            for h in range(2):
                yb_ref[blk, KV_W * h:KV_W * (h + 1)] = _mm(
                    pswa_ref[blk, 4 * BLOCK * h:4 * BLOCK * (h + 1)],
                    jnp.concatenate([va[2 * h][band], va[2 * h + 1][band]], axis=0))

        gm256 = _group_matrix(XATT_W)
        xq = proj_ref[:, C_XQ:C_XQ + XATT_W]
        qx = xq * lax.rsqrt(_seg_mean(xq * xq, gm256) + EPS) * gn_ref[:, G_XQ:G_XK]
        pm = _mem_probs(_mm_nt(qx, km_ref[...]))
        for j in range(4):
            pmem_ref[:, MEM_LEN * j:MEM_LEN * (j + 1)] = pm[j].astype(pmem_ref.dtype)
        yc = _mm(pmem_ref[...], vm_ref[...])
        yc_ref[...] = yc

        def gated(y, g, gate):
            return y * lax.rsqrt(_row_mean(y * y) + EPS) * g * (gate * _sigmoid(gate))

        ogv = og_ref[...]
        za = gated(ya_ref[...], ogv[:, :512], proj_ref[:, C_LRUG:C_LRUG + LRU_W])
        zb = gated(yb_ref[...], ogv[:, 512:768], proj_ref[:, C_SWAG:C_SWAG + SWA_W])
        zc = gated(yc, ogv[:, 768:], proj_ref[:, C_XG:C_XG + XATT_W])
        ycat_ref[:, 0:512] = za.astype(ycat_ref.dtype)
        ycat_ref[:, 512:768] = zb.astype(ycat_ref.dtype)
        ycat_ref[:, 768:1024] = zc.astype(ycat_ref.dtype)
        out = xv + _mm(ycat_ref[...], wout_ref[...])
        err = out - t_ref[...]
        dout_ref[...] = (err * (1.0 / D_MODEL)).astype(dout_ref.dtype)
        lacc_ref[...] = lacc_ref[...] + (0.5 / D_MODEL) * jnp.sum(err * err)

        @pl.when(i == nt - 1)
        def _():
            loss_ref[...] = lacc_ref[...]

    def rows(ncol):
        return pl.BlockSpec((tm, ncol), lambda i: (i, 0))

    in_specs = [rows(D_MODEL), rows(D_MODEL), rows(128), rows(128), rows(128),
                _const_spec((1, D_MODEL)), _const_spec((D_IN, D_MODEL), True), _const_spec((CONV_K, LRU_W)),
                _const_spec((1, LRU_W)), _const_spec((2, 256, 512), True), _const_spec((1, LRU_W)),
                _const_spec((1, LRU_W)), _const_spec((1, LRU_W)), _const_spec((1, GAINS_W)), pl.BlockSpec(memory_space=pltpu.SMEM),
                _const_spec((4 * MEM_LEN, XATT_W), True), _const_spec((4 * MEM_LEN, XATT_W), True),
                _const_spec((1, D_MODEL)), _const_spec((D_MODEL, D_MODEL), True)]
    out_shape = (jax.ShapeDtypeStruct((seq, D_IN), F32), jax.ShapeDtypeStruct((seq, LRU_W), F32),
                 jax.ShapeDtypeStruct((seq, SWA_W), F32), jax.ShapeDtypeStruct((seq, XATT_W), F32),
                 jax.ShapeDtypeStruct((seq, D_MODEL), _MXU), jax.ShapeDtypeStruct((seq, D_MODEL), _MXU),
                 jax.ShapeDtypeStruct((seq, D_MODEL), _MXU), jax.ShapeDtypeStruct((seq, 4 * 2 * BLOCK), _MXU),
                 jax.ShapeDtypeStruct((seq, 4 * MEM_LEN), _MXU), jax.ShapeDtypeStruct((seq, 128), F32),
                 jax.ShapeDtypeStruct((seq, 4 * LRU_W), _MXU), jax.ShapeDtypeStruct((seq, LRU_W), F32),
                 jax.ShapeDtypeStruct((8, 128), F32))
    out_specs = (rows(D_IN), rows(LRU_W), rows(SWA_W), rows(XATT_W), rows(D_MODEL), rows(D_MODEL), rows(D_MODEL),
                 rows(4 * 2 * BLOCK), rows(4 * MEM_LEN), rows(128), rows(4 * LRU_W), rows(LRU_W),
                 _const_spec((8, 128)))
    scratch = [pltpu.VMEM((tm + 8, LRU_W), F32), pltpu.VMEM((tm, LRU_W), F32),
               pltpu.VMEM((8, LRU_W), F32), pltpu.VMEM((BLOCK, KV_W), F32), pltpu.VMEM((BLOCK, KV_W), F32),
               pltpu.VMEM((8, 128), F32)]
    return pl.pallas_call(
        body, name="layer_fwd", grid=(nt,), out_shape=out_shape, in_specs=in_specs, out_specs=out_specs,
        scratch_shapes=scratch,
        compiler_params=pltpu.CompilerParams(dimension_semantics=("arbitrary",), vmem_limit_bytes=VMEM_LIMIT),
    )(x, tgt, rc, rs1, rs2, ng, win_t, cw, cb, wg, brg, big, lam, gains, sinks, km, vm, og, wout)


def weight_grads(ycat, dout, dproj, xn, mem_operands, vectors, early_at, late_at):
    seq, ncol = xn.shape
    blk = 256
    n_out, n_in = ycat.shape[1] // blk, dproj.shape[1] // blk
    assert n_out == N_CHIPS and late_at[0] >= n_out
    g_out = jax.ShapeDtypeStruct((N_CHIPS, 2, blk // 2, dout.shape[1]), F32)
    g_kv = jax.ShapeDtypeStruct((N_CHIPS, 2, D_MODEL // 8, 2 * XATT_W), F32)
    g_small = jax.ShapeDtypeStruct((2, SMALL_ROWS // 2, 128), F32)
    shape_e, scratch_e = _hosted_reduce_shapes([g_kv], g_small)
    shape_l, scratch_l = _hosted_reduce_shapes([g_out], None)
    n_mem, names = len(mem_operands), tuple(vectors)

    def body(l1_ref, r1_ref, l2_ref, r2_ref, *refs):
        mem_refs, vec_refs = refs[:n_mem], refs[n_mem:n_mem + len(names)]
        o_ref, sum_out, sum_kv, sum_sm, gout_scr, gkv_scr, pack_scr, *scratch = refs[n_mem + len(names):]
        j = pl.program_id(0)

        @pl.when(j == 0)
        def _():
            _mem_bwd_and_pack(*mem_refs, dict(zip(names, vec_refs)), gkv_scr, pack_scr)

        def reduce_stages(closing):
            _hosted_reduce(j, n_out + n_in, closing, early_at, (gkv_scr, pack_scr), (sum_kv, sum_sm),
                           scratch[:len(scratch_e)], True)
            _hosted_reduce(j, n_out + n_in, closing, late_at, (gout_scr,), (sum_out,), scratch[len(scratch_e):], False)

        reduce_stages(False)

        @pl.when(j < n_out)
        def _():
            gout_scr[j] = _mm_tn(l1_ref[...], r1_ref[...]).reshape(g_out.shape[1:])

        @pl.when(j >= n_out)
        def _():
            o_ref[...] = _mm_tn(l2_ref[...], r2_ref[...])

        reduce_stages(True)

    vm = pl.BlockSpec(memory_space=pltpu.VMEM)
    hbm = pl.BlockSpec(memory_space=pl.ANY)
    return pl.pallas_call(
        body, name="weight_grads", grid=(n_out + n_in,),
        out_shape=(jax.ShapeDtypeStruct((dproj.shape[1], ncol), F32), *shape_l, *shape_e),
        in_specs=[pl.BlockSpec((seq, blk), lambda j: (0, jnp.minimum(j, n_out - 1))), _const_spec(dout.shape, True),
                  pl.BlockSpec((seq, blk), lambda j: (0, jnp.maximum(j - n_out, 0))), _const_spec(xn.shape, True)]
        + [vm] * (n_mem + len(names)),
        out_specs=(pl.BlockSpec((blk, ncol), lambda j: (jnp.maximum(j - n_out, 0), 0)), hbm, hbm, hbm),
        scratch_shapes=[pltpu.VMEM(s.shape, F32) for s in (g_out, g_kv, g_small)] + scratch_e + scratch_l,
        compiler_params=pltpu.CompilerParams(dimension_semantics=("arbitrary",), vmem_limit_bytes=VMEM_LIMIT),
    )(ycat, dout, dproj, xn, *mem_operands, *vectors.values())


def layer_bwd(x, dout, proj, ya, yb, yc, pswa, pmem, psink, gates, a_all, rc, rs1, rs2, ng, win_t, cw, wg, lam, gains,
              km, vm, og, wout):
    seq = x.shape[0]
    tm = min(ROW_TILE, seq)
    nt = seq // tm
    nb = tm // BLOCK

    def body(x_ref, dout_ref, proj_ref, ya_ref, yb_ref, yc_ref, pswa_ref, pmem_ref, psink_ref, gates_ref, a_ref,
             c_ref, s1_ref, s2_ref,
             yah_ref, kvh_ref, ch_ref, s1h_ref, s2h_ref,
             ng_ref, win_ref, cw_ref, wg_ref, lam_ref, gn_ref, km_ref, vm_ref, og_ref, wout_ref,
             gx_ref, dproj_ref, gwg_ref, dkm_ref, dvm_ref, gng_ref, gog_ref, gcb_ref, gbrg_ref, gbig_ref, glam_ref,
             gcw_ref, gqn_ref, gkn_ref, gxqn_ref, gsink_ref,
             hext_ref, aext_ref, an_scr, dh_scr, g_scr, dxc_ext, gcar_ref, dkcar_ref, dvcar_ref):
        i = pl.program_id(0)
        tile = nt - 1 - i
        first_tile = tile == 0

        @pl.when(i == 0)
        def _():
            for r in (gwg_ref, dkm_ref, dvm_ref, gng_ref, gog_ref, gcb_ref, gbrg_ref, gbig_ref, glam_ref, gcw_ref,
                      gqn_ref, gkn_ref, gxqn_ref, gsink_ref, gcar_ref, dkcar_ref, dvcar_ref):
                r[...] = jnp.zeros_like(r)
            dxc_ext[tm:tm + 8, :] = jnp.zeros((8, LRU_W), F32)
            aext_ref[tm:tm + 8, :] = jnp.zeros((8, LRU_W), F32)

        xv = x_ref[...]
        dov = dout_ref[...]
        dz = _mm_nt(dov, wout_ref[...])
        ogv = og_ref[...]

        def group_bwd(y, gate, g, dzg):
            r = lax.rsqrt(_row_mean(y * y) + EPS)
            n = y * r
            sg = _sigmoid(gate)
            dgate = dzg * (n * g) * (sg * (1.0 + gate * (1.0 - sg)))
            dng = dzg * (gate * sg)
            dn = dng * g
            return r * (dn - n * _row_mean(dn * n)), dgate, _col_sum(dng * n)

        dya, dga, goa = group_bwd(ya_ref[...], proj_ref[:, C_LRUG:C_LRUG + LRU_W], ogv[:, :512], dz[:, :512])
        dyb, dgb, gob = group_bwd(yb_ref[...], proj_ref[:, C_SWAG:C_SWAG + SWA_W], ogv[:, 512:768], dz[:, 512:768])
        dyc, dgc, goc = group_bwd(yc_ref[...], proj_ref[:, C_XG:C_XG + XATT_W], ogv[:, 768:], dz[:, 768:])
        gog_ref[...] += jnp.concatenate([goa, gob, goc], axis=1)
        dproj_ref[:, C_LRUG:C_LRUG + LRU_W] = dga.astype(dproj_ref.dtype)
        dproj_ref[:, C_SWAG:C_SWAG + SWA_W] = dgb.astype(dproj_ref.dtype)
        dproj_ref[:, C_XG:C_XG + XATT_W] = dgc.astype(dproj_ref.dtype)

        gm256 = _group_matrix(XATT_W)
        xq = proj_ref[:, C_XQ:C_XQ + XATT_W]
        rq = lax.rsqrt(_seg_mean(xq * xq, gm256) + EPS)
        qn = xq * rq
        qx = qn * gn_ref[:, G_XQ:G_XK]
        qxb = qx.astype(_MXU)
        dycb = dyc.astype(_MXU)
        dp_all = _mm_nt(dycb, vm_ref[...])
        dsm = []
        for j in range(4):
            pj = pmem_ref[:, MEM_LEN * j:MEM_LEN * (j + 1)].astype(F32)
            dp = dp_all[:, MEM_LEN * j:MEM_LEN * (j + 1)]
            dsm.append((pj * (dp - jnp.sum(pj * dp, axis=-1, keepdims=True))).astype(_MXU))
        ds_all = jnp.concatenate(dsm, axis=1)
        dvm_ref[...] += _mm_tn(dycb, pmem_ref[...])
        dkm_ref[...] += _mm_tn(qxb, ds_all)
        dqx = _mm(ds_all, km_ref[...])
        gxqn_ref[...] += _col_sum(dqx * qn)
        dqn = dqx * gn_ref[:, G_XQ:G_XK]
        dproj_ref[:, C_XQ:C_XQ + XATT_W] = (rq * (dqn - qn * _seg_mean(dqn * qn, gm256))).astype(dproj_ref.dtype)

        gm128 = _group_matrix(KV_W)
        cv, s1v, s2v = c_ref[...], s1_ref[...], s2_ref[...]

        def head_norm(t):
            r = lax.rsqrt(_seg_mean(t * t, gm128) + EPS)
            return t * r, r

        qn_, qr_ = zip(head_norm(proj_ref[:, C_SQ:C_SQ + 128]), head_norm(proj_ref[:, C_SQ + 128:C_SQ + 256]))
        qrope = [_rope(qn_[h] * gn_ref[:, G_Q:G_K], cv, s1v, s2v).astype(_MXU) for h in range(2)]
        kn, krr = head_norm(proj_ref[:, C_SK:C_SK + KV_W])
        kr = _rope(kn * gn_ref[:, G_K:G_XQ], cv, s1v, s2v)
        khn, _ = head_norm(kvh_ref[:, 0:KV_W])
        khr = _rope(khn * gn_ref[:, G_K:G_XQ], ch_ref[...], s1h_ref[...], s2h_ref[...])
        ka = _place_kv(jnp.concatenate([khr, kr], axis=0), 0.125)
        va = _place_kv(jnp.concatenate([kvh_ref[:, KV_W:2 * KV_W], proj_ref[:, C_SV:C_SV + KV_W]], axis=0), 1.0)
        lane128 = lax.broadcasted_iota(jnp.int32, (1, 128), 1)
        gsink = jnp.zeros((1, 128), F32)
        dk_band, dv_band, dq_blk = [], [], []
        for b in range(nb):
            band = slice(BLOCK * b, BLOCK * b + 2 * BLOCK)
            blk = slice(BLOCK * b, BLOCK * (b + 1))
            dka, dva, dsb = [], [], []
            deltas = jnp.zeros((BLOCK, 128), F32)
            for j in range(4):
                qh = qrope[j // 2][blk]
                doh = dyb[blk, KV_W * (j // 2):KV_W * (j // 2 + 1)].astype(_MXU)
                pb = pswa_ref[blk, 2 * BLOCK * j:2 * BLOCK * (j + 1)]
                p = pb.astype(F32)
                dp = _mm_nt(doh, va[j][band])
                delta = jnp.sum(p * dp, axis=-1, keepdims=True)
                ds = (p * (dp - delta)).astype(_MXU)
                deltas = jnp.where(lane128 == j, delta, deltas)
                dva.append(_mm_tn(pb, doh))
                dka.append(_mm_tn(ds, qh))
                dsb.append(ds)
            gsink = gsink - _col_sum(psink_ref[blk, :] * deltas)
            dk_band.append(_unplace_kv(dka) * 0.125)
            dv_band.append(_unplace_kv(dva))
            dq_blk.append([_mm(jnp.concatenate(dsb[2 * h:2 * h + 2], axis=1),
                               jnp.concatenate([ka[2 * h][band], ka[2 * h + 1][band]], axis=0)) for h in range(2)])
        gsink_ref[...] += gsink
        dk_rows = [dk_band[b][BLOCK:] + (dk_band[b + 1][:BLOCK] if b + 1 < nb else dkcar_ref[...]) for b in range(nb)]
        dv_rows = [dv_band[b][BLOCK:] + (dv_band[b + 1][:BLOCK] if b + 1 < nb else dvcar_ref[...]) for b in range(nb)]
        dkcar_ref[...] = dk_band[0][:BLOCK]
        dvcar_ref[...] = dv_band[0][:BLOCK]
        dkg = _rope_bwd(jnp.concatenate(dk_rows, axis=0), cv, s1v, s2v)
        gkn = _col_sum(dkg * kn)
        dkn = dkg * gn_ref[:, G_K:G_XQ]
        dproj_ref[:, C_SK:C_SK + KV_W] = (krr * (dkn - kn * _seg_mean(dkn * kn, gm128))).astype(dproj_ref.dtype)
        dproj_ref[:, C_SV:C_SV + KV_W] = jnp.concatenate(dv_rows, axis=0).astype(dproj_ref.dtype)
        gqn = jnp.zeros((1, 128), F32)
        for h in range(2):
            dqg = _rope_bwd(jnp.concatenate([dq_blk[b][h] for b in range(nb)], axis=0), cv, s1v, s2v)
            gqn = gqn + _col_sum(dqg * qn_[h])
            dqn_ = dqg * gn_ref[:, G_Q:G_K]
            dproj_ref[:, C_SQ + 128 * h:C_SQ + 128 * (h + 1)] = (
                qr_[h] * (dqn_ - qn_[h] * _seg_mean(dqn_ * qn_[h], gm128))).astype(dproj_ref.dtype)
        gqn_ref[...] += gqn
        gkn_ref[...] += gkn

        u = proj_ref[:, C_LRUX:C_LRUX + LRU_W]
        xc, rg, ig, sq = (gates_ref[:, LRU_W * k:LRU_W * (k + 1)].astype(F32) for k in range(4))
        a = a_ref[...]
        sp = _softplus(-lam_ref[...])
        hext_ref[0:8, :] = jnp.where(first_tile, 0.0, yah_ref[...])
        hext_ref[8:8 + tm, :] = ya_ref[...]
        hprev = hext_ref[pl.ds(7, tm), :]
        aext_ref[0:tm, :] = a
        an_scr[...] = aext_ref[pl.ds(1, tm), :]
        dh_scr[...] = dya
        dh_scr[tm - 1:tm, :] = dh_scr[tm - 1:tm, :] + gcar_ref[0:1, :]
        row8 = lax.broadcasted_iota(jnp.int32, (8, LRU_W), 0)

        def scan_step(gi, carry):
            r0 = pl.multiple_of((tm // 8 - 1 - gi) * 8, 8)
            av = an_scr[pl.ds(r0, 8), :]
            bv = dh_scr[pl.ds(r0, 8), :]
            for d in (1, 2, 4):
                a_sh = jnp.where(row8 < 8 - d, pltpu.roll(av, 8 - d, 0), 1.0)
                b_sh = jnp.where(row8 < 8 - d, pltpu.roll(bv, 8 - d, 0), 0.0)
                bv = bv + av * b_sh
                av = av * a_sh
            gv = bv + av * carry
            g_scr[pl.ds(r0, 8), :] = gv
            return gv[0:1, :]

        g0 = lax.fori_loop(0, tm // 8, scan_step, jnp.zeros((1, LRU_W), F32), unroll=True)
        gcar_ref[0:1, :] = a[0:1, :] * g0
        gv = g_scr[...]
        da = gv * hprev
        dig = gv * sq * xc
        dxc = gv * sq * ig
        dla = da * a - gv * (ig * xc) * ((a * a) / sq)
        drg = dla * ((-LRU_C) * sp)
        glam_ref[...] += _col_sum(dla * rg)
        dpr = drg * rg * (1.0 - rg)
        dpi = dig * ig * (1.0 - ig)
        gbrg_ref[...] += _col_sum(dpr)
        gbig_ref[...] += _col_sum(dpi)
        dpre0 = jnp.concatenate([dpr[:, :256], dpi[:, :256]], axis=1).astype(_MXU)
        dpre1 = jnp.concatenate([dpr[:, 256:], dpi[:, 256:]], axis=1).astype(_MXU)
        gwg_ref[0] += _mm_tn(xc[:, :256], dpre0)
        gwg_ref[1] += _mm_tn(xc[:, 256:], dpre1)
        dxc = dxc + jnp.concatenate([_mm_nt(dpre0, wg_ref[0]), _mm_nt(dpre1, wg_ref[1])], axis=1)
        gcb_ref[...] += _col_sum(dxc)
        dxc_ext[0:tm, :] = dxc
        du = jnp.zeros((tm, LRU_W), F32)
        for k in range(CONV_K):
            later = dxc_ext[pl.ds(3 - k, tm), :]
            gcw_ref[k:k + 1, :] += _col_sum(later * u)
            du = du + cw_ref[k:k + 1, :] * later
        dxc_ext[tm:tm + 8, :] = dxc[0:8, :]
        dproj_ref[:, C_LRUX:C_LRUX + LRU_W] = du.astype(dproj_ref.dtype)

        dxn = _mm(dproj_ref[...], win_ref[...])
        rx = lax.rsqrt(_row_mean(xv * xv) + EPS)
        xh = xv * rx
        gng_ref[...] += _col_sum(dxn * xh)
        dxh = dxn * ng_ref[...]
        gx_ref[...] = dov.astype(F32) + rx * (dxh - xh * _row_mean(dxh * xh))

        @pl.when(i == nt - 1)
        def _():
            glam_ref[...] = glam_ref[...] * (LRU_C * _sigmoid(-lam_ref[...]))
            for r in (gqn_ref, gkn_ref, gxqn_ref):
                r[...] = _fold_heads(r[...])

    def rows(ncol, arr_cols_block=0):
        return pl.BlockSpec((tm, ncol), lambda i: (nt - 1 - i, arr_cols_block))

    def halo(nrow, ncol, colblk=0):
        per = tm // nrow
        return pl.BlockSpec((nrow, ncol), lambda i: (jnp.maximum((nt - 1 - i) * per - 1, 0), colblk))

    in_specs = [rows(D_MODEL), rows(D_MODEL), rows(D_IN), rows(LRU_W), rows(SWA_W), rows(XATT_W),
                rows(4 * 2 * BLOCK), rows(4 * MEM_LEN), rows(128), rows(4 * LRU_W), rows(LRU_W),
                rows(128), rows(128), rows(128),
                halo(8, LRU_W), halo(BLOCK, 2 * KV_W, C_SK // (2 * KV_W)),
                halo(BLOCK, 128), halo(BLOCK, 128), halo(BLOCK, 128),
                _const_spec((1, D_MODEL)), _const_spec((D_IN, D_MODEL), True), _const_spec((CONV_K, LRU_W)),
                _const_spec((2, 256, 512), True), _const_spec((1, LRU_W)), _const_spec((1, GAINS_W)),
                _const_spec((4 * MEM_LEN, XATT_W), True), _const_spec((4 * MEM_LEN, XATT_W), True),
                _const_spec((1, D_MODEL)), _const_spec((D_MODEL, D_MODEL), True)]
    small = [(2, 256, 512), (XATT_W, 4 * MEM_LEN), (XATT_W, 4 * MEM_LEN), (1, D_MODEL), (1, D_MODEL), (1, LRU_W), (1, LRU_W),
             (1, LRU_W), (1, LRU_W), (CONV_K, LRU_W), (1, 128), (1, 128), (1, XATT_W), (1, 128)]
    out_shape = (jax.ShapeDtypeStruct((seq, D_MODEL), F32), jax.ShapeDtypeStruct((seq, D_IN), _MXU)) + tuple(
        jax.ShapeDtypeStruct(s, F32) for s in small)
    out_specs = (rows(D_MODEL), rows(D_IN)) + tuple(_const_spec(s) for s in small)
    scratch = [pltpu.VMEM((tm + 8, LRU_W), F32), pltpu.VMEM((tm + 8, LRU_W), F32),
               pltpu.VMEM((tm, LRU_W), F32), pltpu.VMEM((tm, LRU_W), F32), pltpu.VMEM((tm, LRU_W), F32),
               pltpu.VMEM((tm + 8, LRU_W), F32),
               pltpu.VMEM((8, LRU_W), F32), pltpu.VMEM((BLOCK, KV_W), F32), pltpu.VMEM((BLOCK, KV_W), F32)]
    return pl.pallas_call(
        body, name="layer_bwd", grid=(nt,), out_shape=out_shape, in_specs=in_specs, out_specs=out_specs,
        scratch_shapes=scratch,
        compiler_params=pltpu.CompilerParams(dimension_semantics=("arbitrary",), vmem_limit_bytes=VMEM_LIMIT),
    )(x, dout, proj, ya, yb, yc, pswa, pmem, psink, gates, a_all, rc, rs1, rs2, ya, proj, rc, rs1, rs2,
      ng, win_t, cw, wg, lam, gains, km, vm, og, wout)


def _reduce_protocol(big, sm, outs, osm, r1, r1s, wire, r2, r2s, wire2, ps, own, send, recv, lsem):
    nbig = len(big)
    x, y, c = lax.axis_index("x"), lax.axis_index("y"), lax.axis_index("c")
    sibling = (x, y, 1 - c)
    near, far, diag = _partners(x, y, c)
    me, near_id, far_id, diag_id = _chip_of(x, y), _chip_of(*near), _chip_of(*far), _chip_of(*diag)

    def copy(k, src, dst, to):
        return pltpu.make_async_remote_copy(src_ref=src, dst_ref=dst, send_sem=send.at[k], recv_sem=recv.at[k],
                                            device_id=to, device_id_type=MESH)

    def sent(stage, a):
        if a == nbig:
            src, dst, to = ((sm.at[1 - c], r1s, sibling), (r1s, r2s.at[0], (*near, c)), (ps, r2s.at[1], (*far, c)),
                            (osm.at[c], osm.at[c], sibling))[stage]
            return [copy(5 * nbig + stage, src, dst, to)]
        if stage == 0:
            return [copy(5 * a, big[a].at[:, 1 - c], r1[a], sibling)]
        if stage == 1:
            return [copy(5 * a + 1, wire[a].at[near_id], r2[a].at[0], (*near, c)),
                    copy(5 * a + 2, wire[a].at[diag_id], r2[a].at[1], (*near, c))]
        if stage == 2:
            return [copy(5 * a + 3, wire2[a], r2[a].at[2], (*far, c))]
        return [copy(5 * a + 4, outs[a].at[c], outs[a].at[c], sibling)]

    arrays = range(nbig + (sm is not None))

    def start(stage, a):
        for cp in sent(stage, a):
            cp.start()

    def arrived(k, ref):
        copy(k, ref, ref, sibling).wait_recv()

    def loads():
        return [pltpu.make_async_copy(big[a].at[:, c], own[a], lsem.at[a]) for a in range(nbig)]

    def stage0():
        for a in arrays:
            start(0, a)
        for cp in loads():
            cp.start()

    def stage1():
        for a in range(nbig):
            loads()[a].wait()
            arrived(5 * a, r1[a])
            for k in range(N_CHIPS):
                r1[a][k] = own[a][k] + r1[a][k]
                wire[a][k] = r1[a][k].astype(wire[a].dtype)
            start(1, a)
        if sm is not None:
            arrived(5 * nbig, r1s)
            r1s[...] = sm[c] + r1s[...]
            start(1, nbig)

    def stage2():
        for a in range(nbig):
            arrived(5 * a + 1, r2[a].at[0])
            arrived(5 * a + 2, r2[a].at[1])
            r1[a][me] = r1[a][me] + r2[a][0].astype(F32)
            wire2[a][...] = (r1[a][far_id] + r2[a][1].astype(F32)).astype(wire2[a].dtype)
            start(2, a)
        if sm is not None:
            arrived(5 * nbig + 1, r2s.at[0])
            ps[...] = r1s[...] + r2s[0]
            start(2, nbig)

    def stage3():
        for a in range(nbig):
            arrived(5 * a + 3, r2[a].at[2])
            outs[a][c] = r1[a][me] + r2[a][2].astype(F32)
            start(3, a)
        if sm is not None:
            arrived(5 * nbig + 2, r2s.at[1])
            osm[c] = ps[...] + r2s[1]
            start(3, nbig)

    def stage4():
        for a in range(nbig):
            arrived(5 * a + 4, outs[a].at[1 - c])
        if sm is not None:
            arrived(5 * nbig + 3, osm.at[1 - c])
        for stage in range(4):
            for a in arrays:
                for cp in sent(stage, a):
                    cp.wait_send()

    return [stage0, stage1, stage2, stage3, stage4]


def _reduce_buffers(bigs, g_small):
    half = [b.shape[2:] for b in bigs]
    sm_half = None if g_small is None else g_small.shape[1:]
    out_shape = [jax.ShapeDtypeStruct((2,) + h, F32) for h in half]
    small = lambda lead: [] if g_small is None else [pltpu.VMEM(lead + sm_half, F32)]
    if g_small is not None:
        out_shape.append(jax.ShapeDtypeStruct(g_small.shape, F32))
    n_sem = 5 * len(bigs) + 4
    scratch = ([pltpu.VMEM((N_CHIPS,) + h, F32) for h in half] + small(())
               + [pltpu.VMEM((N_CHIPS,) + h, _WIRE) for h in half]
               + [pltpu.VMEM((3,) + h, _WIRE) for h in half] + small((2,))
               + [pltpu.VMEM(h, _WIRE) for h in half] + small(())
               + [pltpu.VMEM((N_CHIPS,) + h, F32) for h in half]
               + [pltpu.SemaphoreType.DMA((n_sem,)), pltpu.SemaphoreType.DMA((n_sem,)),
                  pltpu.SemaphoreType.DMA((len(bigs),))])
    return out_shape, scratch


def _split_reduce_refs(refs, nbig, has_small):
    it = iter(refs)
    take = lambda n: [next(it) for _ in range(n)]
    one = lambda: next(it) if has_small else None
    big, sm = take(nbig), one()
    outs, osm = take(nbig), one()
    r1, r1s, wire, r2, r2s, wire2, ps, own = take(nbig), one(), take(nbig), take(nbig), one(), take(nbig), one(), take(nbig)
    send, recv, lsem = take(3)
    return big, sm, outs, osm, r1, r1s, wire, r2, r2s, wire2, ps, own, send, recv, lsem


def _hosted_reduce_shapes(bigs, g_small):
    red_shape, scratch = _reduce_buffers(bigs, g_small)
    nres = len(red_shape)
    return red_shape, [pltpu.VMEM(r.shape, r.dtype) for r in red_shape] + scratch + [pltpu.SemaphoreType.DMA((nres,))]


def _hosted_reduce(step, n_steps, closing, stage_at, operands, results, scratch, has_small):
    nres = len(results)
    sums, rest, fsem = scratch[:nres], scratch[nres:-1], scratch[-1]
    refs = tuple(operands) + tuple(sums) + tuple(rest)

    def to_results():
        out = [pltpu.make_async_copy(sums[k], results[k], fsem.at[k]) for k in range(nres)]
        for cp in out:
            cp.start()
        for cp in out:
            cp.wait()

    stages = _reduce_protocol(*_split_reduce_refs(refs, nres - has_small, has_small))

    def last_stage():
        stages[-1]()
        to_results()

    for at, stage in zip(stage_at, stages[:-1] + [last_stage]):
        if closing == (at == n_steps):
            pl.when(step == min(at, n_steps - 1))(stage)


def reduce_grads(big, name, parts):
    chips, halves, rows_, cols = big.shape
    sub = jax.ShapeDtypeStruct((chips, halves, rows_ // parts, cols), big.dtype)

    def body(b_ref, o_ref, *scratch):
        refs = [b_ref.at[:, :, s] for s in range(parts)] + [o_ref.at[:, s] for s in range(parts)] + list(scratch)
        for stage in _reduce_protocol(*_split_reduce_refs(refs, parts, False)):
            stage()

    _, scratch = _reduce_buffers([sub] * parts, None)
    return pl.pallas_call(
        body, name=name, out_shape=jax.ShapeDtypeStruct((halves, parts, rows_ // parts, cols), F32),
        in_specs=[pl.BlockSpec(memory_space=pl.ANY)], out_specs=pl.BlockSpec(memory_space=pltpu.VMEM),
        scratch_shapes=scratch, compiler_params=pltpu.CompilerParams(vmem_limit_bytes=VMEM_LIMIT),
    )(big.reshape(chips, halves, parts, rows_ // parts, cols))


def adamw_matrices(items):
    plan, total = [], 0
    for w, _, _, _ in items:
        rows_, cols = w.shape
        tr = max(t for t in range(8, rows_ + 1, 8) if rows_ % t == 0 and t * cols * 4 <= ADAM_BLOCK_BYTES)
        plan.append((total, rows_ // tr, tr, cols))
        total += rows_ // tr
    nin = 4 * len(items)

    def body(*refs):
        i = pl.program_id(0)
        for k, (first, steps, _, _) in enumerate(plan):
            w_ref, g_ref, m_ref, v_ref = refs[4 * k:4 * k + 4]
            go_ref, d_ref, nm_ref, nv_ref = refs[nin + 4 * k:nin + 4 * k + 4]

            @pl.when((i >= first) & (i < first + steps))
            def _():
                gv = g_ref[...]
                go_ref[...] = gv
                d_ref[...], nm_ref[...], nv_ref[...] = _adam_update(w_ref[...], gv, m_ref[...], v_ref[...])

    specs, shapes = [], []
    for (first, steps, tr, cols), (w, _, _, _) in zip(plan, items):
        spec = pl.BlockSpec((tr, cols), lambda i, first=first, steps=steps: (jnp.clip(i - first, 0, steps - 1), 0))
        specs += [spec] * 4
        shapes += [jax.ShapeDtypeStruct(w.shape, F32)] * 4
    res = pl.pallas_call(
        body, name="adamw_matrices", grid=(total,), out_shape=tuple(shapes), in_specs=specs, out_specs=tuple(specs),
        compiler_params=pltpu.CompilerParams(dimension_semantics=("arbitrary",)),
    )(*[a for item in items for a in item])
    return [res[4 * k:4 * k + 4] for k in range(len(items))]


def _adam_update(w, g, m, v):
    nm = ADAM_B1 * m + (1.0 - ADAM_B1) * g
    nv = ADAM_B2 * v + (1.0 - ADAM_B2) * (g * g)
    m_hat = nm / (1.0 - ADAM_B1 ** ADAM_STEP)
    v_hat = nv / (1.0 - ADAM_B2 ** ADAM_STEP)
    return (-ADAM_LR) * (m_hat / (jnp.sqrt(v_hat) + ADAM_EPS) + ADAM_WD * w), nm, nv


def adamw_vectors(g_pack, ws, ms, vs):
    nvec = len(SMALL_VECTORS)
    n = nvec + len(SMALL_MATRICES)

    def body(*refs):
        pk = refs[0]
        w_refs, m_refs, v_refs = (refs[1 + k * n:1 + (k + 1) * n] for k in range(3))
        g_out, d_out, nm_out, nv_out = (refs[1 + (3 + k) * n:1 + (4 + k) * n] for k in range(4))
        refs[-1][...] = pk[LOSS_ROW:LOSS_ROW + 1, 0:1]
        chip = 2 * lax.axis_index("x") + lax.axis_index("y")
        for k, (name, row, width) in enumerate(SMALL_VECTORS):
            if name == "conv_w":
                g = jnp.concatenate([pk[pl.ds(row + 4 * t + chip, 1), :] for t in range(CONV_K)], axis=0)[None]
            elif width >= 128:
                g = jnp.concatenate([pk[row + r:row + r + 1, :] for r in range(width // 128)], axis=1)
            else:
                g = pk[row:row + 1, 0:width]
            g_out[k][...] = g
            d_out[k][...], nm_out[k][...], nv_out[k][...] = _adam_update(w_refs[k][...], g, m_refs[k][...], v_refs[k][...])
        for k in range(nvec, n):
            for b in range(LRU_BLOCKS):
                rows_ = pk[GATES_ROW + HEAD * b:GATES_ROW + HEAD * (b + 1), :]
                g = (pltpu.roll(rows_, HEAD, axis=1) if k > nvec else rows_)[:, 0:HEAD]
                g_out[k][0, b] = g
                d_out[k][0, b], nm_out[k][0, b], nv_out[k][0, b] = _adam_update(
                    w_refs[k][0, b], g, m_refs[k][0, b], v_refs[k][0, b])

    vm = pl.BlockSpec(memory_space=pltpu.VMEM)
    like = [jax.ShapeDtypeStruct(w.shape, F32) for w in ws]
    return pl.pallas_call(
        body, name="adamw_vectors", out_shape=(*like * 4, jax.ShapeDtypeStruct((1, 1), F32)),
        in_specs=[vm] * (1 + 3 * n), out_specs=(vm,) * (4 * n + 1),
    )(g_pack, *ws, *ms, *vs)


SMALL_VECTORS = (("norm_g", 512, 1024), ("mem_norm_g", 520, 1024), ("conv_w", 528, 512), ("conv_b", 544, 512),
                 ("b_rg", 548, 512), ("b_ig", 552, 512), ("lru_lambda", 556, 512), ("q_norm_g", 560, 64),
                 ("k_norm_g", 561, 64), ("sinks", 562, 4), ("xq_norm_g", 563, 64), ("xk_norm_g", 564, 64),
                 ("out_norm_g", 565, 1024))
LOSS_ROW = 573
SMALL_MATRICES = ("w_rg", "w_ig")
GATES_ROW = 0
SMALL_ROWS = 640


def _rope_tables(seq):
    pos = np.arange(seq, dtype=np.float32)
    inv_freq = (np.float32(ROPE_THETA) ** (-(np.arange(0, ROPE_DIM, 2, dtype=np.float32) / np.float32(ROPE_DIM)))
                ).astype(np.float32)
    ang = (pos[:, None] * inv_freq[None, :]).astype(np.float32)
    cos, sin = np.cos(ang).astype(np.float32), np.sin(ang).astype(np.float32)
    z = lambda n: np.zeros((seq, n), np.float32)
    c64 = np.concatenate([cos, cos, np.ones((seq, HEAD - ROPE_DIM), np.float32)], axis=1)
    s1_64 = np.concatenate([-sin, z(HEAD - 8)], axis=1)
    s2_64 = np.concatenate([z(8), sin, z(HEAD - ROPE_DIM)], axis=1)
    return tuple(jnp.asarray(np.concatenate([t, t], axis=1)) for t in (c64, s1_64, s2_64))


def kernel(x, mem, norm_g, mem_norm_g, w_in, conv_w, conv_b, w_rg, b_rg, w_ig, b_ig, lru_lambda, q_norm_g, k_norm_g, sinks, w_mem_kv, xq_norm_g, xk_norm_g, out_norm_g, w_out, loss_target, m_norm_g, m_mem_norm_g, m_w_in, m_conv_w, m_conv_b, m_w_rg, m_b_rg, m_w_ig, m_b_ig, m_lru_lambda, m_q_norm_g, m_k_norm_g, m_sinks, m_w_mem_kv, m_xq_norm_g, m_xk_norm_g, m_out_norm_g, m_w_out, v_norm_g, v_mem_norm_g, v_w_in, v_conv_w, v_conv_b, v_w_rg, v_b_rg, v_w_ig, v_b_ig, v_lru_lambda, v_q_norm_g, v_k_norm_g, v_sinks, v_w_mem_kv, v_xq_norm_g, v_xk_norm_g, v_out_norm_g, v_w_out):
    seq = x.shape[1]
    xs, tgt, mems = x[0], loss_target[0], mem[0]

    win_t, wout, wkv, cw, wg, gains, km, vm = gather_weights(
        w_in[0].T, w_out[0], w_mem_kv[0], conv_w, w_rg, w_ig, (q_norm_g, k_norm_g, xq_norm_g, xk_norm_g), mems,
        mem_norm_g)
    rc, rs1, rs2 = _rope_tables(seq)
    proj, ya, yb, yc, ycat, xn, dout, pswa, pmem, psink, gates, a_all, loss8 = layer_fwd(
        xs, tgt, rc, rs1, rs2, norm_g, win_t, cw, conv_b, wg, b_rg, b_ig, lru_lambda, gains, sinks, km, vm,
        out_norm_g, wout)
    (gx, dproj, g_wg, dkm, dvm, g_ng, g_og, g_cb, g_brg, g_big, g_lam, g_cw, g_qn, g_kn, g_xqn, g_sink) = layer_bwd(
        xs, dout, proj, ya, yb, yc, pswa, pmem, psink, gates, a_all, rc, rs1, rs2, norm_g, win_t, cw, wg, lru_lambda,
        gains, km, vm, out_norm_g, wout)
    g_win_t, r_out, r_kv, r_small = weight_grads(
        ycat, dout, dproj, xn, (mems, mem_norm_g, wkv, gains, dkm, dvm, g_wg, loss8), dict(
            norm_g=g_ng, conv_w=g_cw, conv_b=g_cb, b_rg=g_brg, b_ig=g_big, lru_lambda=g_lam, q_norm_g=g_qn,
            k_norm_g=g_kn, sinks=g_sink, xq_norm_g=g_xqn, out_norm_g=g_og), (0, 1, 4, 7, 8), (4, 5, 10, 12, 13))
    r_in = reduce_grads(g_win_t.reshape(N_CHIPS, 2, D_IN // 8, D_MODEL), "reduce_w_in", 6)

    r_small = r_small.reshape(SMALL_ROWS, 128)
    grads = {}
    weights = dict(norm_g=norm_g, mem_norm_g=mem_norm_g, w_in=w_in, conv_w=conv_w, conv_b=conv_b, w_rg=w_rg, b_rg=b_rg,
                   w_ig=w_ig, b_ig=b_ig, lru_lambda=lru_lambda, q_norm_g=q_norm_g, k_norm_g=k_norm_g, sinks=sinks,
                   w_mem_kv=w_mem_kv, xq_norm_g=xq_norm_g, xk_norm_g=xk_norm_g, out_norm_g=out_norm_g, w_out=w_out)
    ms = dict(norm_g=m_norm_g, mem_norm_g=m_mem_norm_g, w_in=m_w_in, conv_w=m_conv_w, conv_b=m_conv_b, w_rg=m_w_rg,
              b_rg=m_b_rg, w_ig=m_w_ig, b_ig=m_b_ig, lru_lambda=m_lru_lambda, q_norm_g=m_q_norm_g, k_norm_g=m_k_norm_g,
              sinks=m_sinks, w_mem_kv=m_w_mem_kv, xq_norm_g=m_xq_norm_g, xk_norm_g=m_xk_norm_g,
              out_norm_g=m_out_norm_g, w_out=m_w_out)
    vs = dict(norm_g=v_norm_g, mem_norm_g=v_mem_norm_g, w_in=v_w_in, conv_w=v_conv_w, conv_b=v_conv_b, w_rg=v_w_rg,
              b_rg=v_b_rg, w_ig=v_w_ig, b_ig=v_b_ig, lru_lambda=v_lru_lambda, q_norm_g=v_q_norm_g, k_norm_g=v_k_norm_g,
              sinks=v_sinks, w_mem_kv=v_w_mem_kv, xq_norm_g=v_xq_norm_g, xk_norm_g=v_xk_norm_g,
              out_norm_g=v_out_norm_g, w_out=v_w_out)

    delta, new_m, new_v = {}, {}, {}
    res_in, res_out, res_kv = adamw_matrices([
        (w_in[0].T, r_in.reshape(D_IN // 4, D_MODEL), m_w_in[0].T, v_w_in[0].T),
        (w_out[0], r_out.reshape(D_MODEL // 4, D_MODEL), m_w_out[0], v_w_out[0]),
        (w_mem_kv[0], r_kv.reshape(D_MODEL // 4, 2 * XATT_W), m_w_mem_kv[0], v_w_mem_kv[0])])
    grads["w_in"], delta["w_in"], new_m["w_in"], new_v["w_in"] = (r.T[None] for r in res_in)
    grads["w_out"], delta["w_out"], new_m["w_out"], new_v["w_out"] = (r[None] for r in res_out)
    grads["w_mem_kv"], delta["w_mem_kv"], new_m["w_mem_kv"], new_v["w_mem_kv"] = (r[None] for r in res_kv)
    small_names = [n for n, _, _ in SMALL_VECTORS] + list(SMALL_MATRICES)
    res = adamw_vectors(r_small, [weights[n] for n in small_names], [ms[n] for n in small_names],
                        [vs[n] for n in small_names])
    nall = len(small_names)
    for k, into in enumerate((grads, delta, new_m, new_v)):
        into.update(zip(small_names, res[k * nall:(k + 1) * nall]))
    loss = res[-1].reshape(())

    order = ("norm_g", "mem_norm_g", "w_in", "conv_w", "conv_b", "w_rg", "b_rg", "w_ig", "b_ig", "lru_lambda",
             "q_norm_g", "k_norm_g", "sinks", "w_mem_kv", "xq_norm_g", "xk_norm_g", "out_norm_g", "w_out")
    return (loss, gx[None], *[grads[n] for n in order], *[delta[n] for n in order], *[new_m[n] for n in order],
            *[new_v[n] for n in order])
```

```python
import jax
import jax.numpy as jnp
import numpy as np
from jax import lax
from jax.experimental import pallas as pl
from jax.experimental.pallas import tpu as pltpu

F32 = jnp.float32
_MXU = jnp.bfloat16
_WIRE = jnp.bfloat16

D_MODEL = 1024
MEM_LEN = 256
HEAD = 64
LRU_W = 512
LRU_BLOCKS = 8
CONV_K = 4
LRU_C = 8.0
SWA_W = 256
KV_W = 128
XATT_W = 256
BLOCK = 128
D_IN = 2304
ROPE_THETA = 500000.0
ROPE_DIM = 16
EPS = 1e-6
NEG_INF = -1e30
C_LRUX, C_LRUG, C_SQ, C_SK, C_SV, C_SWAG, C_XQ, C_XG = 0, 512, 1024, 1280, 1408, 1536, 1792, 2048
G_Q, G_K, G_XQ, G_XK, GAINS_W = 0, 128, 256, 512, 768

ADAM_LR, ADAM_B1, ADAM_B2, ADAM_EPS, ADAM_WD, ADAM_STEP = 0.001, 0.9, 0.999, 1e-08, 0.01, 10

N_CHIPS = 4
ROW_TILE = 256
VMEM_LIMIT = 56 * 1024 * 1024
ADAM_BLOCK_BYTES = 640 * 1024
MESH = pl.DeviceIdType.MESH


def _mm(a, b):
    return jnp.dot(a.astype(_MXU), b.astype(_MXU), preferred_element_type=F32)


def _mm_nt(a, b):
    return lax.dot_general(a.astype(_MXU), b.astype(_MXU), (((1,), (1,)), ((), ())), preferred_element_type=F32)


def _mm_tn(a, b):
    return lax.dot_general(a.astype(_MXU), b.astype(_MXU), (((0,), (0,)), ((), ())), preferred_element_type=F32)


def _group_matrix(width):
    r = lax.shift_right_logical(lax.broadcasted_iota(jnp.int32, (width, width), 0), 6)
    c = lax.shift_right_logical(lax.broadcasted_iota(jnp.int32, (width, width), 1), 6)
    return (r == c).astype(_MXU)


def _seg_mean(x, gm):
    return jnp.dot(x.astype(_MXU), gm, preferred_element_type=F32) * (1.0 / HEAD)


def _seg_sum(x, gm):
    hi = x.astype(_MXU)
    lo = (x - hi.astype(F32)).astype(_MXU)
    return jnp.dot(hi, gm, preferred_element_type=F32) + jnp.dot(lo, gm, preferred_element_type=F32)


def _row_mean(x):
    return jnp.mean(x, axis=-1, keepdims=True)


def _col_sum(x):
    return jnp.sum(x, axis=0, keepdims=True)


def _sigmoid(x):
    return jax.nn.sigmoid(x)


def _softplus(z):
    e = jnp.exp(-jnp.abs(z))
    u = 1.0 + e
    log1p_e = jnp.where(u == 1.0, e, jnp.log(u) * (e / (u - 1.0)))
    return jnp.maximum(z, 0.0) + log1p_e


def _rope(t, c, s1, s2):
    return t * c + pltpu.roll(t, 120, 1) * s1 + pltpu.roll(t, 8, 1) * s2


def _rope_bwd(d, c, s1, s2):
    return d * c + pltpu.roll(d * s1, 8, 1) + pltpu.roll(d * s2, 120, 1)


def _fold_heads(v):
    out = v
    for k in range(1, v.shape[1] // HEAD):
        out = out + pltpu.roll(v, HEAD * k, 1)
    return out


def _lane_mask(width, lo, hi):
    lane = lax.broadcasted_iota(jnp.int32, (1, width), 1)
    return ((lane >= lo) & (lane < hi)).astype(F32)


def _swa_mask(first_block):
    qi = lax.broadcasted_iota(jnp.int32, (BLOCK, 2 * BLOCK), 0)
    kj = lax.broadcasted_iota(jnp.int32, (BLOCK, 2 * BLOCK), 1)
    rel = qi + BLOCK - kj
    ok = (rel >= 0) & (rel < BLOCK)
    return ok & (jnp.logical_not(first_block) | (kj >= BLOCK))


def _place_kv(t, scale):
    lo = t * (_lane_mask(KV_W, 0, HEAD) * scale)
    hi = t * (_lane_mask(KV_W, HEAD, KV_W) * scale)
    return [a.astype(_MXU) for a in (lo, pltpu.roll(lo, HEAD, 1), pltpu.roll(hi, HEAD, 1), hi)]


def _unplace_kv(d):
    return (_lane_mask(KV_W, 0, HEAD) * (d[0] + pltpu.roll(d[1], HEAD, 1))
            + _lane_mask(KV_W, HEAD, KV_W) * (d[3] + pltpu.roll(d[2], HEAD, 1)))


def _swa_probs(qh, ka, mask, sink):
    s = _mm_nt(qh, ka)
    s = jnp.where(mask, s, NEG_INF)
    m = jnp.maximum(jnp.max(s, axis=-1, keepdims=True), sink)
    p = jnp.exp(s - m)
    esink = jnp.exp(sink - m)
    inv = 1.0 / (jnp.sum(p, axis=-1, keepdims=True) + esink)
    return p * inv, esink * inv


def _mem_probs(s_all):
    out = []
    for j in range(4):
        s = s_all[:, MEM_LEN * j:MEM_LEN * (j + 1)]
        p = jnp.exp(s - jnp.max(s, axis=-1, keepdims=True))
        out.append(p * (1.0 / jnp.sum(p, axis=-1, keepdims=True)))
    return out


def _head_rows(t, scale):
    return jnp.concatenate([t * (_lane_mask(XATT_W, HEAD * j, HEAD * (j + 1)) * scale) for j in range(4)], axis=0)


def _lru_gates(xc, wg_ref, brg, big, lam):
    p0 = _mm(xc[:, :256], wg_ref[0])
    p1 = _mm(xc[:, 256:], wg_ref[1])
    rg = _sigmoid(jnp.concatenate([p0[:, :256], p1[:, :256]], axis=1) + brg)
    ig = _sigmoid(jnp.concatenate([p0[:, 256:], p1[:, 256:]], axis=1) + big)
    sp = _softplus(-lam)
    la = (-LRU_C) * rg * sp
    a = jnp.exp(la)
    th = jnp.tanh(la)
    one_minus_a2 = (-2.0 * th) / (1.0 - th)
    return rg, ig, sp, a, jnp.sqrt(one_minus_a2)


def _const_spec(shape, single=False):
    zeros = (0,) * len(shape)
    if single:
        return pl.BlockSpec(shape, lambda i: zeros, pipeline_mode=pl.Buffered(1))
    return pl.BlockSpec(shape, lambda i: zeros)


def _chip_of(x, y):
    return 2 * x + y


def _partners(x, y, c):
    north = c == 1
    near = (jnp.where(north, 1 - x, x), jnp.where(north, y, 1 - y))
    far = (jnp.where(north, x, 1 - x), jnp.where(north, 1 - y, y))
    return near, far, (1 - x, 1 - y)


def gather_weights(win_t, wout, wkv, conv_w, w_rg, w_ig, head_gains, mem, mem_g):
    arrs = (win_t, wout, wkv)
    n = len(arrs)
    pieces = [(a, k * (arr.shape[0] // (2 * cut)), arr.shape[0] // (2 * cut))
              for a, (arr, cut) in enumerate(zip(arrs, (2, 1, 1))) for k in range(cut)]
    npc = len(pieces)

    def body(a0, a1, a2, cw_in, wrg_ref, wig_ref, q_ref, k_ref, xq_ref, xk_ref, mem_ref, mg_ref,
             o0, o1, o2, cw_out, wg_ref, gn_ref, km_ref, vm_ref, s0, s1, s2, cw, ocw, send, recv, lsem):
        ins, outs = (s0, s1, s2), (o0, o1, o2)
        for src, dst in zip((a0, a1, a2), ins):
            dst[...] = src[...].astype(dst.dtype)
        cw[...] = jnp.zeros(cw.shape, F32)
        cw[0:CONV_K, :] = cw_in[0]
        x, y, c = lax.axis_index("x"), lax.axis_index("y"), lax.axis_index("c")
        sibling = (x, y, 1 - c)
        near, far, diag = _partners(x, y, c)
        chips = [near, far, diag]
        me = _chip_of(x, y)

        def landed(p, chip, half):
            a, off, rows_ = pieces[p]
            r = ins[a].shape[0]
            return outs[a].at[pl.ds(pl.multiple_of(chip * r + half * (r // 2) + off, 16), rows_)]

        def mine(p):
            a, off, rows_ = pieces[p]
            return ins[a].at[pl.ds(pl.multiple_of(c * (ins[a].shape[0] // 2) + off, 16), rows_)]

        def copy(k, src, dst, to):
            return pltpu.make_async_remote_copy(src_ref=src, dst_ref=dst, send_sem=send.at[k], recv_sem=recv.at[k],
                                                device_id=to, device_id_type=MESH)

        def cw_rows(chip):
            return ocw.at[pl.ds(pl.multiple_of(chip * 8, 8), 8)]

        locals_ = []
        for a in range(n):
            r = ins[a].shape[0]
            locals_.append(pltpu.make_async_copy(ins[a], outs[a].at[pl.ds(pl.multiple_of(me * r, 16), r)], lsem.at[a]))
        locals_.append(pltpu.make_async_copy(cw, cw_rows(me), lsem.at[n]))
        for cp in locals_:
            cp.start()

        sent = []
        for p in range(npc):
            for j in range(2):
                sent.append(copy(p * 6 + j, mine(p), landed(p, me, c), (*chips[j], c)))
        for j, chip in enumerate(chips):
            sent.append(copy(npc * 6 + j, cw, cw_rows(me), (*chip, c)))
        for cp in sent:
            cp.start()

        gn_ref[...] = jnp.concatenate([q_ref[...]] * 2 + [k_ref[...]] * 2 + [xq_ref[...]] * 4 + [xk_ref[...]] * 4,
                                      axis=1)
        zeros = lambda lanes: [jnp.zeros((HEAD, lanes), F32)] if lanes else []
        for h in range(2):
            for b in range(4):
                row = []
                for w_ref in (wrg_ref, wig_ref):
                    row += zeros(HEAD * b) + [w_ref[0, 4 * h + b]] + zeros(HEAD * (3 - b))
                wg_ref[h, HEAD * b:HEAD * (b + 1), :] = jnp.concatenate(row, axis=1).astype(wg_ref.dtype)

        for j in range(3):
            for p in range(npc):
                got = landed(p, _chip_of(*chips[j]), c)
                copy(p * 6 + j, got, got, sibling).wait_recv()
                if j == 0:
                    sent.append(copy(p * 6 + 2, got, got, (*far, c)))
                    sent[-1].start()
                sent.append(copy(p * 6 + 3 + j, got, got, sibling))
                sent[-1].start()
        for p in range(npc):
            for j in range(3):
                got = landed(p, _chip_of(*chips[(1, 0, 2)[j]]), 1 - c)
                copy(p * 6 + 3 + j, got, got, sibling).wait_recv()
        for j, chip in enumerate(chips):
            got = cw_rows(_chip_of(*chip))
            copy(npc * 6 + j, got, got, (*chip, c)).wait_recv()
        for cp in sent:
            cp.wait_send()
        for cp in locals_:
            cp.wait()
        for chip in range(N_CHIPS):
            cw_out[:, 128 * chip:128 * (chip + 1)] = ocw[8 * chip:8 * chip + CONV_K, :]

        mem_v = mem_ref[...]
        mn = mem_v * lax.rsqrt(_row_mean(mem_v * mem_v) + EPS) * mg_ref[...]
        mkv = _mm(mn, o2[...])
        kpre = mkv[:, :XATT_W]
        km = kpre * lax.rsqrt(_seg_mean(kpre * kpre, _group_matrix(XATT_W)) + EPS) * gn_ref[:, G_XK:GAINS_W]
        km_ref[...] = _head_rows(km, 0.125).astype(km_ref.dtype)
        vm_ref[...] = _head_rows(mkv[:, XATT_W:], 1.0).astype(vm_ref.dtype)

    vm = pl.BlockSpec(memory_space=pltpu.VMEM)
    hbm = pl.BlockSpec(memory_space=pl.ANY)
    head_rows = jax.ShapeDtypeStruct((4 * MEM_LEN, XATT_W), _MXU)
    out_shape = tuple(jax.ShapeDtypeStruct((N_CHIPS * a.shape[0],) + a.shape[1:], _MXU) for a in arrs) + (
        jax.ShapeDtypeStruct((CONV_K, LRU_W), F32), jax.ShapeDtypeStruct((2, 256, 512), _MXU),
        jax.ShapeDtypeStruct((1, GAINS_W), F32), head_rows, head_rows)
    n_rdma = npc * 6 + 3
    return pl.pallas_call(
        body, name="gather_weights", out_shape=out_shape,
        in_specs=[vm] * 12, out_specs=(hbm, hbm, vm, vm, vm, vm, vm, vm),
        scratch_shapes=[pltpu.VMEM(a.shape, _MXU) for a in arrs] + [
            pltpu.VMEM((8, 128), F32), pltpu.VMEM((N_CHIPS * 8, 128), F32),
            pltpu.SemaphoreType.DMA((n_rdma,)), pltpu.SemaphoreType.DMA((n_rdma,)), pltpu.SemaphoreType.DMA((n + 1,))],
        compiler_params=pltpu.CompilerParams(vmem_limit_bytes=VMEM_LIMIT),
    )(win_t, wout, wkv, conv_w, w_rg, w_ig, *head_gains, mem, mem_g)


def _mem_bwd_and_pack(mem_ref, g_ref, w_ref, gn_ref, dkm_ref, dvm_ref, gg_ref, loss_ref, vectors, gw_ref, pk_ref):
    first_row = {name: (row, width) for name, row, width in SMALL_VECTORS}
    half_rows = SMALL_ROWS // 2
    pk_ref[...] = jnp.zeros(pk_ref.shape, F32)

    def rows_at(at, n):
        assert at // half_rows == (at + n - 1) // half_rows
        return at // half_rows, slice(at % half_rows, at % half_rows + n), slice(None)

    def put(name, src):
        row, width = first_row[name]
        per_row = 1 if width < 128 else src.shape[1] // 128
        for t in range(src.shape[0]):
            for r in range(per_row):
                pk_ref[rows_at(row + per_row * t + r, 1)] = src[t:t + 1, 128 * r:128 * (r + 1)]

    for name, ref in vectors.items():
        put(name, ref)
    pk_ref[rows_at(LOSS_ROW, 1)] = loss_ref[0:1, :]
    upper = lax.broadcasted_iota(jnp.int32, (HEAD, 128), 1) >= HEAD
    for h in range(2):
        for b in range(4):
            rg = gg_ref[h, HEAD * b:HEAD * (b + 1), 128 * (b // 2):128 * (b // 2 + 1)]
            ig = gg_ref[h, HEAD * b:HEAD * (b + 1), 256 + 128 * (b // 2):256 + 128 * (b // 2 + 1)]
            if b % 2:
                rg = pltpu.roll(rg, HEAD, axis=1)
            else:
                ig = pltpu.roll(ig, HEAD, axis=1)
            pk_ref[rows_at(GATES_ROW + HEAD * (4 * h + b), HEAD)] = jnp.where(upper, ig, rg)

    mem_v = mem_ref[...]
    mh = mem_v * lax.rsqrt(_row_mean(mem_v * mem_v) + EPS)
    mn = mh * g_ref[...]
    mkv = _mm(mn, w_ref[...])
    kpre = mkv[:, :XATT_W]
    gm = _group_matrix(XATT_W)
    rk = lax.rsqrt(_seg_mean(kpre * kpre, gm) + EPS)
    kn = kpre * rk
    dk = jnp.zeros((MEM_LEN, XATT_W), F32)
    dv = jnp.zeros((MEM_LEN, XATT_W), F32)
    for j in range(4):
        mj = _lane_mask(XATT_W, HEAD * j, HEAD * (j + 1))
        dk = dk + dkm_ref[:, MEM_LEN * j:MEM_LEN * (j + 1)].T * (mj * 0.125)
        dv = dv + dvm_ref[:, MEM_LEN * j:MEM_LEN * (j + 1)].T * mj
    put("xk_norm_g", _fold_heads(_col_sum(dk * kn)))
    dkn = dk * gn_ref[:, G_XK:GAINS_W]
    dkpre = rk * (dkn - kn * _seg_mean(dkn * kn, gm))
    dmkv = jnp.concatenate([dkpre, dv], axis=1)
    gw_ref[...] = _mm_tn(mn, dmkv).reshape(gw_ref.shape)
    dmn = _mm_nt(dmkv, w_ref[...])
    put("mem_norm_g", _col_sum(dmn * mh))


def layer_fwd(x, tgt, rc, rs1, rs2, ng, win_t, cw, cb, wg, brg, big, lam, gains, sinks, km, vm, og, wout):
    seq = x.shape[0]
    tm = min(ROW_TILE, seq)
    nt = seq // tm
    nb = tm // BLOCK

    def body(x_ref, t_ref, c_ref, s1_ref, s2_ref, ng_ref, win_ref, cw_ref, cb_ref, wg_ref, brg_ref, big_ref, lam_ref,
             gn_ref, sink_ref, km_ref, vm_ref, og_ref, wout_ref,
             proj_ref, ya_ref, yb_ref, yc_ref, ycat_ref, xn_ref, dout_ref, pswa_ref, pmem_ref, psink_ref, gates_ref,
             a_ref, loss_ref,
             ext_ref, b_scr, hc_ref, kp_ref, vp_ref, lacc_ref):
        i = pl.program_id(0)

        @pl.when(i == 0)
        def _():
            ext_ref[0:8, :] = jnp.zeros((8, LRU_W), F32)
            hc_ref[...] = jnp.zeros_like(hc_ref)
            kp_ref[...] = jnp.zeros_like(kp_ref)
            vp_ref[...] = jnp.zeros_like(vp_ref)
            lacc_ref[...] = jnp.zeros_like(lacc_ref)

        xv = x_ref[...]
        xn = (xv * lax.rsqrt(_row_mean(xv * xv) + EPS) * ng_ref[...]).astype(_MXU)
        xn_ref[...] = xn.astype(xn_ref.dtype)
        proj_ref[...] = _mm_nt(xn, win_ref[...])

        u = proj_ref[:, C_LRUX:C_LRUX + LRU_W]
        ext_ref[8:8 + tm, :] = u
        xc = cb_ref[...]
        for k in range(CONV_K):
            xc = xc + cw_ref[k:k + 1, :] * ext_ref[pl.ds(5 + k, tm), :]
        ext_ref[0:8, :] = u[tm - 8:tm, :]
        rg, ig, sp, a, sq = _lru_gates(xc, wg_ref, brg_ref[...], big_ref[...], lam_ref[...])
        for k, t in enumerate((xc, rg, ig, sq)):
            gates_ref[:, LRU_W * k:LRU_W * (k + 1)] = t.astype(gates_ref.dtype)
        a_ref[...] = a
        b_scr[...] = sq * (ig * xc)
        row8 = lax.broadcasted_iota(jnp.int32, (8, LRU_W), 0)

        def scan_step(g, carry):
            r0 = pl.multiple_of(g * 8, 8)
            av = a_ref[pl.ds(r0, 8), :]
            bv = b_scr[pl.ds(r0, 8), :]
            for d in (1, 2, 4):
                a_sh = jnp.where(row8 >= d, pltpu.roll(av, d, 0), 1.0)
                b_sh = jnp.where(row8 >= d, pltpu.roll(bv, d, 0), 0.0)
                bv = bv + av * b_sh
                av = av * a_sh
            hv = bv + av * carry
            ya_ref[pl.ds(r0, 8), :] = hv
            return hv[7:8, :]

        hc_ref[0:1, :] = lax.fori_loop(0, tm // 8, scan_step, hc_ref[0:1, :], unroll=True)

        gm128 = _group_matrix(KV_W)
        cv, s1v, s2v = c_ref[...], s1_ref[...], s2_ref[...]

        def head_norm_rope(t, g):
            n = t * lax.rsqrt(_seg_mean(t * t, gm128) + EPS)
            return _rope(n * g, cv, s1v, s2v)

        qs_ = (head_norm_rope(proj_ref[:, C_SQ:C_SQ + 128], gn_ref[:, G_Q:G_K]).astype(_MXU),
               head_norm_rope(proj_ref[:, C_SQ + 128:C_SQ + 256], gn_ref[:, G_Q:G_K]).astype(_MXU))
        kr = head_norm_rope(proj_ref[:, C_SK:C_SK + KV_W], gn_ref[:, G_K:G_XQ])
        sv = proj_ref[:, C_SV:C_SV + KV_W]
        ka = _place_kv(jnp.concatenate([kp_ref[...], kr], axis=0), 0.125)
        va = _place_kv(jnp.concatenate([vp_ref[...], sv], axis=0), 1.0)
        kp_ref[...] = kr[tm - BLOCK:tm, :]
        vp_ref[...] = sv[tm - BLOCK:tm, :]
        lane128 = lax.broadcasted_iota(jnp.int32, (1, 128), 1)
        for b in range(nb):
            mask = _swa_mask((i == 0) & (b == 0)) if b == 0 else _swa_mask(False)
            band = slice(BLOCK * b, BLOCK * b + 2 * BLOCK)
            blk = slice(BLOCK * b, BLOCK * (b + 1))
            psink = jnp.zeros((BLOCK, 128), F32)
            for j in range(4):
                p, pk = _swa_probs(qs_[j // 2][blk], ka[j][band], mask, sink_ref[0, j])
                pswa_ref[blk, 2 * BLOCK * j:2 * BLOCK * (j + 1)] = p.astype(pswa_ref.dtype)
                psink = jnp.where(lane128 == j, pk, psink)
            psink_ref[blk, :] = psink
            for h in range(2):
                yb_ref[blk, KV_W * h:KV_W * (h + 1)] = _mm(
                    pswa_ref[blk, 4 * BLOCK * h:4 * BLOCK * (h + 1)],
                    jnp.concatenate([va[2 * h][band], va[2 * h + 1][band]], axis=0))

        gm256 = _group_matrix(XATT_W)
        xq = proj_ref[:, C_XQ:C_XQ + XATT_W]
        qx = xq * lax.rsqrt(_seg_mean(xq * xq, gm256) + EPS) * gn_ref[:, G_XQ:G_XK]
        pm = _mem_probs(_mm_nt(qx, km_ref[...]))
        for j in range(4):
            pmem_ref[:, MEM_LEN * j:MEM_LEN * (j + 1)] = pm[j].astype(pmem_ref.dtype)
        yc = _mm(pmem_ref[...], vm_ref[...])
        yc_ref[...] = yc

        def gated(y, g, gate):
            return y * lax.rsqrt(_row_mean(y * y) + EPS) * g * (gate * _sigmoid(gate))

        ogv = og_ref[...]
        za = gated(ya_ref[...], ogv[:, :512], proj_ref[:, C_LRUG:C_LRUG + LRU_W])
        zb = gated(yb_ref[...], ogv[:, 512:768], proj_ref[:, C_SWAG:C_SWAG + SWA_W])
        zc = gated(yc, ogv[:, 768:], proj_ref[:, C_XG:C_XG + XATT_W])
        ycat_ref[:, 0:512] = za.astype(ycat_ref.dtype)
        ycat_ref[:, 512:768] = zb.astype(ycat_ref.dtype)
        ycat_ref[:, 768:1024] = zc.astype(ycat_ref.dtype)
        out = xv + _mm(ycat_ref[...], wout_ref[...])
        err = out - t_ref[...]
        dout_ref[...] = (err * (1.0 / D_MODEL)).astype(dout_ref.dtype)
        lacc_ref[...] = lacc_ref[...] + (0.5 / D_MODEL) * jnp.sum(err * err)

        @pl.when(i == nt - 1)
        def _():
            loss_ref[...] = lacc_ref[...]

    def rows(ncol):
        return pl.BlockSpec((tm, ncol), lambda i: (i, 0))

    in_specs = [rows(D_MODEL), rows(D_MODEL), rows(128), rows(128), rows(128),
                _const_spec((1, D_MODEL)), _const_spec((D_IN, D_MODEL), True), _const_spec((CONV_K, LRU_W)),
                _const_spec((1, LRU_W)), _const_spec((2, 256, 512), True), _const_spec((1, LRU_W)),
                _const_spec((1, LRU_W)), _const_spec((1, LRU_W)), _const_spec((1, GAINS_W)), pl.BlockSpec(memory_space=pltpu.SMEM),
                _const_spec((4 * MEM_LEN, XATT_W), True), _const_spec((4 * MEM_LEN, XATT_W), True),
                _const_spec((1, D_MODEL)), _const_spec((D_MODEL, D_MODEL), True)]
    out_shape = (jax.ShapeDtypeStruct((seq, D_IN), F32), jax.ShapeDtypeStruct((seq, LRU_W), F32),
                 jax.ShapeDtypeStruct((seq, SWA_W), F32), jax.ShapeDtypeStruct((seq, XATT_W), F32),
                 jax.ShapeDtypeStruct((seq, D_MODEL), _MXU), jax.ShapeDtypeStruct((seq, D_MODEL), _MXU),
                 jax.ShapeDtypeStruct((seq, D_MODEL), _MXU), jax.ShapeDtypeStruct((seq, 4 * 2 * BLOCK), _MXU),
                 jax.ShapeDtypeStruct((seq, 4 * MEM_LEN), _MXU), jax.ShapeDtypeStruct((seq, 128), F32),
                 jax.ShapeDtypeStruct((seq, 4 * LRU_W), _MXU), jax.ShapeDtypeStruct((seq, LRU_W), F32),
                 jax.ShapeDtypeStruct((8, 128), F32))
    out_specs = (rows(D_IN), rows(LRU_W), rows(SWA_W), rows(XATT_W), rows(D_MODEL), rows(D_MODEL), rows(D_MODEL),
                 rows(4 * 2 * BLOCK), rows(4 * MEM_LEN), rows(128), rows(4 * LRU_W), rows(LRU_W),
                 _const_spec((8, 128)))
    scratch = [pltpu.VMEM((tm + 8, LRU_W), F32), pltpu.VMEM((tm, LRU_W), F32),
               pltpu.VMEM((8, LRU_W), F32), pltpu.VMEM((BLOCK, KV_W), F32), pltpu.VMEM((BLOCK, KV_W), F32),
               pltpu.VMEM((8, 128), F32)]
    return pl.pallas_call(
        body, name="layer_fwd", grid=(nt,), out_shape=out_shape, in_specs=in_specs, out_specs=out_specs,
        scratch_shapes=scratch,
        compiler_params=pltpu.CompilerParams(dimension_semantics=("arbitrary",), vmem_limit_bytes=VMEM_LIMIT),
    )(x, tgt, rc, rs1, rs2, ng, win_t, cw, cb, wg, brg, big, lam, gains, sinks, km, vm, og, wout)


def weight_grads(ycat, dout, dproj, xn, mem_operands, vectors, early_at, late_at):
    seq, ncol = xn.shape
    blk = 256
    n_out, n_in = ycat.shape[1] // blk, dproj.shape[1] // blk
    assert n_out == N_CHIPS and late_at[0] >= n_out
    g_out = jax.ShapeDtypeStruct((N_CHIPS, 2, blk // 2, dout.shape[1]), F32)
    g_kv = jax.ShapeDtypeStruct((N_CHIPS, 2, D_MODEL // 8, 2 * XATT_W), F32)
    g_small = jax.ShapeDtypeStruct((2, SMALL_ROWS // 2, 128), F32)
    shape_e, scratch_e = _hosted_reduce_shapes([g_kv], g_small)
    shape_l, scratch_l = _hosted_reduce_shapes([g_out], None)
    n_mem, names = len(mem_operands), tuple(vectors)

    def body(l1_ref, r1_ref, l2_ref, r2_ref, *refs):
        mem_refs, vec_refs = refs[:n_mem], refs[n_mem:n_mem + len(names)]
        o_ref, sum_out, sum_kv, sum_sm, gout_scr, gkv_scr, pack_scr, *scratch = refs[n_mem + len(names):]
        j = pl.program_id(0)

        @pl.when(j == 0)
        def _():
            _mem_bwd_and_pack(*mem_refs, dict(zip(names, vec_refs)), gkv_scr, pack_scr)

        def reduce_stages(closing):
            _hosted_reduce(j, n_out + n_in, closing, early_at, (gkv_scr, pack_scr), (sum_kv, sum_sm),
                           scratch[:len(scratch_e)], True)
            _hosted_reduce(j, n_out + n_in, closing, late_at, (gout_scr,), (sum_out,), scratch[len(scratch_e):], False)

        reduce_stages(False)

        @pl.when(j < n_out)
        def _():
            gout_scr[j] = _mm_tn(l1_ref[...], r1_ref[...]).reshape(g_out.shape[1:])

        @pl.when(j >= n_out)
        def _():
            o_ref[...] = _mm_tn(l2_ref[...], r2_ref[...])

        reduce_stages(True)

    vm = pl.BlockSpec(memory_space=pltpu.VMEM)
    hbm = pl.BlockSpec(memory_space=pl.ANY)
    return pl.pallas_call(
        body, name="weight_grads", grid=(n_out + n_in,),
        out_shape=(jax.ShapeDtypeStruct((dproj.shape[1], ncol), F32), *shape_l, *shape_e),
        in_specs=[pl.BlockSpec((seq, blk), lambda j: (0, jnp.minimum(j, n_out - 1))), _const_spec(dout.shape, True),
                  pl.BlockSpec((seq, blk), lambda j: (0, jnp.maximum(j - n_out, 0))), _const_spec(xn.shape, True)]
        + [vm] * (n_mem + len(names)),
        out_specs=(pl.BlockSpec((blk, ncol), lambda j: (jnp.maximum(j - n_out, 0), 0)), hbm, hbm, hbm),
        scratch_shapes=[pltpu.VMEM(s.shape, F32) for s in (g_out, g_kv, g_small)] + scratch_e + scratch_l,
        compiler_params=pltpu.CompilerParams(dimension_semantics=("arbitrary",), vmem_limit_bytes=VMEM_LIMIT),
    )(ycat, dout, dproj, xn, *mem_operands, *vectors.values())


def layer_bwd(x, dout, proj, ya, yb, yc, pswa, pmem, psink, gates, a_all, rc, rs1, rs2, ng, win_t, cw, wg, lam, gains,
              km, vm, og, wout):
    seq = x.shape[0]
    tm = min(ROW_TILE, seq)
    nt = seq // tm
    nb = tm // BLOCK

    def body(x_ref, dout_ref, proj_ref, ya_ref, yb_ref, yc_ref, pswa_ref, pmem_ref, psink_ref, gates_ref, a_ref,
             c_ref, s1_ref, s2_ref,
             yah_ref, kvh_ref, ch_ref, s1h_ref, s2h_ref,
             ng_ref, win_ref, cw_ref, wg_ref, lam_ref, gn_ref, km_ref, vm_ref, og_ref, wout_ref,
             gx_ref, dproj_ref, gwg_ref, dkm_ref, dvm_ref, gng_ref, gog_ref, gcb_ref, gbrg_ref, gbig_ref, glam_ref,
             gcw_ref, gqn_ref, gkn_ref, gxqn_ref, gsink_ref,
             hext_ref, aext_ref, an_scr, dh_scr, g_scr, dxc_ext, gcar_ref, dkcar_ref, dvcar_ref):
        i = pl.program_id(0)
        tile = nt - 1 - i
        first_tile = tile == 0

        @pl.when(i == 0)
        def _():
            for r in (gwg_ref, dkm_ref, dvm_ref, gng_ref, gog_ref, gcb_ref, gbrg_ref, gbig_ref, glam_ref, gcw_ref,
                      gqn_ref, gkn_ref, gxqn_ref, gsink_ref, gcar_ref, dkcar_ref, dvcar_ref):
                r[...] = jnp.zeros_like(r)
            dxc_ext[tm:tm + 8, :] = jnp.zeros((8, LRU_W), F32)
            aext_ref[tm:tm + 8, :] = jnp.zeros((8, LRU_W), F32)

        xv = x_ref[...]
        dov = dout_ref[...]
        dz = _mm_nt(dov, wout_ref[...])
        ogv = og_ref[...]

        def group_bwd(y, gate, g, dzg):
            r = lax.rsqrt(_row_mean(y * y) + EPS)
            n = y * r
            sg = _sigmoid(gate)
            dgate = dzg * (n * g) * (sg * (1.0 + gate * (1.0 - sg)))
            dng = dzg * (gate * sg)
            dn = dng * g
            return r * (dn - n * _row_mean(dn * n)), dgate, _col_sum(dng * n)

        dya, dga, goa = group_bwd(ya_ref[...], proj_ref[:, C_LRUG:C_LRUG + LRU_W], ogv[:, :512], dz[:, :512])
        dyb, dgb, gob = group_bwd(yb_ref[...], proj_ref[:, C_SWAG:C_SWAG + SWA_W], ogv[:, 512:768], dz[:, 512:768])
        dyc, dgc, goc = group_bwd(yc_ref[...], proj_ref[:, C_XG:C_XG + XATT_W], ogv[:, 768:], dz[:, 768:])
        gog_ref[...] += jnp.concatenate([goa, gob, goc], axis=1)
        dproj_ref[:, C_LRUG:C_LRUG + LRU_W] = dga.astype(dproj_ref.dtype)
        dproj_ref[:, C_SWAG:C_SWAG + SWA_W] = dgb.astype(dproj_ref.dtype)
        dproj_ref[:, C_XG:C_XG + XATT_W] = dgc.astype(dproj_ref.dtype)

        gm256 = _group_matrix(XATT_W)
        xq = proj_ref[:, C_XQ:C_XQ + XATT_W]
        rq = lax.rsqrt(_seg_mean(xq * xq, gm256) + EPS)
        qn = xq * rq
        qx = qn * gn_ref[:, G_XQ:G_XK]
        qxb = qx.astype(_MXU)
        dycb = dyc.astype(_MXU)
        dp_all = _mm_nt(dycb, vm_ref[...])
        delta_c = _seg_sum(dyc * yc_ref[...], gm256)
        dsm = []
        for j in range(4):
            dp = dp_all[:, MEM_LEN * j:MEM_LEN * (j + 1)]
            dsm.append(pmem_ref[:, MEM_LEN * j:MEM_LEN * (j + 1)] * (dp - delta_c[:, HEAD * j:HEAD * j + 1]).astype(_MXU))
        ds_all = jnp.concatenate(dsm, axis=1)
        dvm_ref[...] += _mm_tn(dycb, pmem_ref[...])
        dkm_ref[...] += _mm_tn(qxb, ds_all)
        dqx = _mm(ds_all, km_ref[...])
        gxqn_ref[...] += _col_sum(dqx * qn)
        dqn = dqx * gn_ref[:, G_XQ:G_XK]
        dproj_ref[:, C_XQ:C_XQ + XATT_W] = (rq * (dqn - qn * _seg_mean(dqn * qn, gm256))).astype(dproj_ref.dtype)

        gm128 = _group_matrix(KV_W)
        cv, s1v, s2v = c_ref[...], s1_ref[...], s2_ref[...]

        def head_norm(t):
            r = lax.rsqrt(_seg_mean(t * t, gm128) + EPS)
            return t * r, r

        qn_, qr_ = zip(head_norm(proj_ref[:, C_SQ:C_SQ + 128]), head_norm(proj_ref[:, C_SQ + 128:C_SQ + 256]))
        qrope = [_rope(qn_[h] * gn_ref[:, G_Q:G_K], cv, s1v, s2v).astype(_MXU) for h in range(2)]
        kn, krr = head_norm(proj_ref[:, C_SK:C_SK + KV_W])
        kr = _rope(kn * gn_ref[:, G_K:G_XQ], cv, s1v, s2v)
        khn, _ = head_norm(kvh_ref[:, 0:KV_W])
        khr = _rope(khn * gn_ref[:, G_K:G_XQ], ch_ref[...], s1h_ref[...], s2h_ref[...])
        ka = _place_kv(jnp.concatenate([khr, kr], axis=0), 0.125)
        va = _place_kv(jnp.concatenate([kvh_ref[:, KV_W:2 * KV_W], proj_ref[:, C_SV:C_SV + KV_W]], axis=0), 1.0)
        lane128 = lax.broadcasted_iota(jnp.int32, (1, 128), 1)
        delta_b = _seg_sum(dyb * yb_ref[...], gm256)
        gsink = jnp.zeros((1, 128), F32)
        dk_band, dv_band, dq_blk = [], [], []
        for b in range(nb):
            band = slice(BLOCK * b, BLOCK * b + 2 * BLOCK)
            blk = slice(BLOCK * b, BLOCK * (b + 1))
            dka, dva, dsb = [], [], []
            deltas = jnp.zeros((BLOCK, 128), F32)
            for j in range(4):
                qh = qrope[j // 2][blk]
                doh = dyb[blk, KV_W * (j // 2):KV_W * (j // 2 + 1)].astype(_MXU)
                pb = pswa_ref[blk, 2 * BLOCK * j:2 * BLOCK * (j + 1)]
                dp = _mm_nt(doh, va[j][band])
                delta = delta_b[blk, HEAD * j:HEAD * j + 1]
                ds = pb * (dp - delta).astype(_MXU)
                deltas = jnp.where(lane128 == j, delta, deltas)
                dva.append(_mm_tn(pb, doh))
                dka.append(_mm_tn(ds, qh))
                dsb.append(ds)
            gsink = gsink - _col_sum(psink_ref[blk, :] * deltas)
            dk_band.append(_unplace_kv(dka) * 0.125)
            dv_band.append(_unplace_kv(dva))
            dq_blk.append([_mm(jnp.concatenate(dsb[2 * h:2 * h + 2], axis=1),
                               jnp.concatenate([ka[2 * h][band], ka[2 * h + 1][band]], axis=0)) for h in range(2)])
        gsink_ref[...] += gsink
        dk_rows = [dk_band[b][BLOCK:] + (dk_band[b + 1][:BLOCK] if b + 1 < nb else dkcar_ref[...]) for b in range(nb)]
        dv_rows = [dv_band[b][BLOCK:] + (dv_band[b + 1][:BLOCK] if b + 1 < nb else dvcar_ref[...]) for b in range(nb)]
        dkcar_ref[...] = dk_band[0][:BLOCK]
        dvcar_ref[...] = dv_band[0][:BLOCK]
        dkg = _rope_bwd(jnp.concatenate(dk_rows, axis=0), cv, s1v, s2v)
        gkn = _col_sum(dkg * kn)
        dkn = dkg * gn_ref[:, G_K:G_XQ]
        dproj_ref[:, C_SK:C_SK + KV_W] = (krr * (dkn - kn * _seg_mean(dkn * kn, gm128))).astype(dproj_ref.dtype)
        dproj_ref[:, C_SV:C_SV + KV_W] = jnp.concatenate(dv_rows, axis=0).astype(dproj_ref.dtype)
        gqn = jnp.zeros((1, 128), F32)
        for h in range(2):
            dqg = _rope_bwd(jnp.concatenate([dq_blk[b][h] for b in range(nb)], axis=0), cv, s1v, s2v)
            gqn = gqn + _col_sum(dqg * qn_[h])
            dqn_ = dqg * gn_ref[:, G_Q:G_K]
            dproj_ref[:, C_SQ + 128 * h:C_SQ + 128 * (h + 1)] = (
                qr_[h] * (dqn_ - qn_[h] * _seg_mean(dqn_ * qn_[h], gm128))).astype(dproj_ref.dtype)
        gqn_ref[...] += gqn
        gkn_ref[...] += gkn

        u = proj_ref[:, C_LRUX:C_LRUX + LRU_W]
        xc, rg, ig, sq = (gates_ref[:, LRU_W * k:LRU_W * (k + 1)].astype(F32) for k in range(4))
        a = a_ref[...]
        sp = _softplus(-lam_ref[...])
        hext_ref[0:8, :] = jnp.where(first_tile, 0.0, yah_ref[...])
        hext_ref[8:8 + tm, :] = ya_ref[...]
        hprev = hext_ref[pl.ds(7, tm), :]
        aext_ref[0:tm, :] = a
        an_scr[...] = aext_ref[pl.ds(1, tm), :]
        dh_scr[...] = dya
        dh_scr[tm - 1:tm, :] = dh_scr[tm - 1:tm, :] + gcar_ref[0:1, :]
        row8 = lax.broadcasted_iota(jnp.int32, (8, LRU_W), 0)

        def scan_step(gi, carry):
            r0 = pl.multiple_of((tm // 8 - 1 - gi) * 8, 8)
            av = an_scr[pl.ds(r0, 8), :]
            bv = dh_scr[pl.ds(r0, 8), :]
            for d in (1, 2, 4):
                a_sh = jnp.where(row8 < 8 - d, pltpu.roll(av, 8 - d, 0), 1.0)
                b_sh = jnp.where(row8 < 8 - d, pltpu.roll(bv, 8 - d, 0), 0.0)
                bv = bv + av * b_sh
                av = av * a_sh
            gv = bv + av * carry
            g_scr[pl.ds(r0, 8), :] = gv
            return gv[0:1, :]

        g0 = lax.fori_loop(0, tm // 8, scan_step, jnp.zeros((1, LRU_W), F32), unroll=True)
        gcar_ref[0:1, :] = a[0:1, :] * g0
        gv = g_scr[...]
        da = gv * hprev
        dig = gv * sq * xc
        dxc = gv * sq * ig
        dla = da * a - gv * (ig * xc) * ((a * a) / sq)
        drg = dla * ((-LRU_C) * sp)
        glam_ref[...] += _col_sum(dla * rg)
        dpr = drg * rg * (1.0 - rg)
        dpi = dig * ig * (1.0 - ig)
        gbrg_ref[...] += _col_sum(dpr)
        gbig_ref[...] += _col_sum(dpi)
        dpre0 = jnp.concatenate([dpr[:, :256], dpi[:, :256]], axis=1).astype(_MXU)
        dpre1 = jnp.concatenate([dpr[:, 256:], dpi[:, 256:]], axis=1).astype(_MXU)
        gwg_ref[0] += _mm_tn(xc[:, :256], dpre0)
        gwg_ref[1] += _mm_tn(xc[:, 256:], dpre1)
        dxc = dxc + jnp.concatenate([_mm_nt(dpre0, wg_ref[0]), _mm_nt(dpre1, wg_ref[1])], axis=1)
        gcb_ref[...] += _col_sum(dxc)
        dxc_ext[0:tm, :] = dxc
        du = jnp.zeros((tm, LRU_W), F32)
        for k in range(CONV_K):
            later = dxc_ext[pl.ds(3 - k, tm), :]
            gcw_ref[k:k + 1, :] += _col_sum(later * u)
            du = du + cw_ref[k:k + 1, :] * later
        dxc_ext[tm:tm + 8, :] = dxc[0:8, :]
        dproj_ref[:, C_LRUX:C_LRUX + LRU_W] = du.astype(dproj_ref.dtype)

        dxn = _mm(dproj_ref[...], win_ref[...])
        rx = lax.rsqrt(_row_mean(xv * xv) + EPS)
        xh = xv * rx
        gng_ref[...] += _col_sum(dxn * xh)
        dxh = dxn * ng_ref[...]
        gx_ref[...] = dov.astype(F32) + rx * (dxh - xh * _row_mean(dxh * xh))

        @pl.when(i == nt - 1)
        def _():
            glam_ref[...] = glam_ref[...] * (LRU_C * _sigmoid(-lam_ref[...]))
            for r in (gqn_ref, gkn_ref, gxqn_ref):
                r[...] = _fold_heads(r[...])

    def rows(ncol, arr_cols_block=0):
        return pl.BlockSpec((tm, ncol), lambda i: (nt - 1 - i, arr_cols_block))

    def halo(nrow, ncol, colblk=0):
        per = tm // nrow
        return pl.BlockSpec((nrow, ncol), lambda i: (jnp.maximum((nt - 1 - i) * per - 1, 0), colblk))

    in_specs = [rows(D_MODEL), rows(D_MODEL), rows(D_IN), rows(LRU_W), rows(SWA_W), rows(XATT_W),
                rows(4 * 2 * BLOCK), rows(4 * MEM_LEN), rows(128), rows(4 * LRU_W), rows(LRU_W),
                rows(128), rows(128), rows(128),
                halo(8, LRU_W), halo(BLOCK, 2 * KV_W, C_SK // (2 * KV_W)),
                halo(BLOCK, 128), halo(BLOCK, 128), halo(BLOCK, 128),
                _const_spec((1, D_MODEL)), _const_spec((D_IN, D_MODEL), True), _const_spec((CONV_K, LRU_W)),
                _const_spec((2, 256, 512), True), _const_spec((1, LRU_W)), _const_spec((1, GAINS_W)),
                _const_spec((4 * MEM_LEN, XATT_W), True), _const_spec((4 * MEM_LEN, XATT_W), True),
                _const_spec((1, D_MODEL)), _const_spec((D_MODEL, D_MODEL), True)]
    small = [(2, 256, 512), (XATT_W, 4 * MEM_LEN), (XATT_W, 4 * MEM_LEN), (1, D_MODEL), (1, D_MODEL), (1, LRU_W), (1, LRU_W),
             (1, LRU_W), (1, LRU_W), (CONV_K, LRU_W), (1, 128), (1, 128), (1, XATT_W), (1, 128)]
    out_shape = (jax.ShapeDtypeStruct((seq, D_MODEL), F32), jax.ShapeDtypeStruct((seq, D_IN), _MXU)) + tuple(
        jax.ShapeDtypeStruct(s, F32) for s in small)
    out_specs = (rows(D_MODEL), rows(D_IN)) + tuple(_const_spec(s) for s in small)
    scratch = [pltpu.VMEM((tm + 8, LRU_W), F32), pltpu.VMEM((tm + 8, LRU_W), F32),
               pltpu.VMEM((tm, LRU_W), F32), pltpu.VMEM((tm, LRU_W), F32), pltpu.VMEM((tm, LRU_W), F32),
               pltpu.VMEM((tm + 8, LRU_W), F32),
               pltpu.VMEM((8, LRU_W), F32), pltpu.VMEM((BLOCK, KV_W), F32), pltpu.VMEM((BLOCK, KV_W), F32)]
    return pl.pallas_call(
        body, name="layer_bwd", grid=(nt,), out_shape=out_shape, in_specs=in_specs, out_specs=out_specs,
        scratch_shapes=scratch,
        compiler_params=pltpu.CompilerParams(dimension_semantics=("arbitrary",), vmem_limit_bytes=VMEM_LIMIT),
    )(x, dout, proj, ya, yb, yc, pswa, pmem, psink, gates, a_all, rc, rs1, rs2, ya, proj, rc, rs1, rs2,
      ng, win_t, cw, wg, lam, gains, km, vm, og, wout)


def _reduce_protocol(big, sm, outs, osm, r1, r1s, wire, r2, r2s, wire2, ps, own, send, recv, lsem):
    nbig = len(big)
    x, y, c = lax.axis_index("x"), lax.axis_index("y"), lax.axis_index("c")
    sibling = (x, y, 1 - c)
    near, far, diag = _partners(x, y, c)
    me, near_id, far_id, diag_id = _chip_of(x, y), _chip_of(*near), _chip_of(*far), _chip_of(*diag)

    def copy(k, src, dst, to):
        return pltpu.make_async_remote_copy(src_ref=src, dst_ref=dst, send_sem=send.at[k], recv_sem=recv.at[k],
                                            device_id=to, device_id_type=MESH)

    def sent(stage, a):
        if a == nbig:
            src, dst, to = ((sm.at[1 - c], r1s, sibling), (r1s, r2s.at[0], (*near, c)), (ps, r2s.at[1], (*far, c)),
                            (osm.at[c], osm.at[c], sibling))[stage]
            return [copy(5 * nbig + stage, src, dst, to)]
        if stage == 0:
            return [copy(5 * a, big[a].at[:, 1 - c], r1[a], sibling)]
        if stage == 1:
            return [copy(5 * a + 1, wire[a].at[near_id], r2[a].at[0], (*near, c)),
                    copy(5 * a + 2, wire[a].at[diag_id], r2[a].at[1], (*near, c))]
        if stage == 2:
            return [copy(5 * a + 3, wire2[a], r2[a].at[2], (*far, c))]
        return [copy(5 * a + 4, outs[a].at[c], outs[a].at[c], sibling)]

    arrays = range(nbig + (sm is not None))

    def start(stage, a):
        for cp in sent(stage, a):
            cp.start()

    def arrived(k, ref):
        copy(k, ref, ref, sibling).wait_recv()

    def loads():
        return [pltpu.make_async_copy(big[a].at[:, c], own[a], lsem.at[a]) for a in range(nbig)]

    def stage0():
        for a in arrays:
            start(0, a)
        for cp in loads():
            cp.start()

    def stage1():
        for a in range(nbig):
            loads()[a].wait()
            arrived(5 * a, r1[a])
            for k in range(N_CHIPS):
                r1[a][k] = own[a][k] + r1[a][k]
                wire[a][k] = r1[a][k].astype(wire[a].dtype)
            start(1, a)
        if sm is not None:
            arrived(5 * nbig, r1s)
            r1s[...] = sm[c] + r1s[...]
            start(1, nbig)

    def stage2():
        for a in range(nbig):
            arrived(5 * a + 1, r2[a].at[0])
            arrived(5 * a + 2, r2[a].at[1])
            r1[a][me] = r1[a][me] + r2[a][0].astype(F32)
            wire2[a][...] = (r1[a][far_id] + r2[a][1].astype(F32)).astype(wire2[a].dtype)
            start(2, a)
        if sm is not None:
            arrived(5 * nbig + 1, r2s.at[0])
            ps[...] = r1s[...] + r2s[0]
            start(2, nbig)

    def stage3():
        for a in range(nbig):
            arrived(5 * a + 3, r2[a].at[2])
            outs[a][c] = r1[a][me] + r2[a][2].astype(F32)
            start(3, a)
        if sm is not None:
            arrived(5 * nbig + 2, r2s.at[1])
            osm[c] = ps[...] + r2s[1]
            start(3, nbig)

    def stage4():
        for a in range(nbig):
            arrived(5 * a + 4, outs[a].at[1 - c])
        if sm is not None:
            arrived(5 * nbig + 3, osm.at[1 - c])
        for stage in range(4):
            for a in arrays:
                for cp in sent(stage, a):
                    cp.wait_send()

    return [stage0, stage1, stage2, stage3, stage4]


def _reduce_buffers(bigs, g_small):
    half = [b.shape[2:] for b in bigs]
    sm_half = None if g_small is None else g_small.shape[1:]
    out_shape = [jax.ShapeDtypeStruct((2,) + h, F32) for h in half]
    small = lambda lead: [] if g_small is None else [pltpu.VMEM(lead + sm_half, F32)]
    if g_small is not None:
        out_shape.append(jax.ShapeDtypeStruct(g_small.shape, F32))
    n_sem = 5 * len(bigs) + 4
    scratch = ([pltpu.VMEM((N_CHIPS,) + h, F32) for h in half] + small(())
               + [pltpu.VMEM((N_CHIPS,) + h, _WIRE) for h in half]
               + [pltpu.VMEM((3,) + h, _WIRE) for h in half] + small((2,))
               + [pltpu.VMEM(h, _WIRE) for h in half] + small(())
               + [pltpu.VMEM((N_CHIPS,) + h, F32) for h in half]
               + [pltpu.SemaphoreType.DMA((n_sem,)), pltpu.SemaphoreType.DMA((n_sem,)),
                  pltpu.SemaphoreType.DMA((len(bigs),))])
    return out_shape, scratch


def _split_reduce_refs(refs, nbig, has_small):
    it = iter(refs)
    take = lambda n: [next(it) for _ in range(n)]
    one = lambda: next(it) if has_small else None
    big, sm = take(nbig), one()
    outs, osm = take(nbig), one()
    r1, r1s, wire, r2, r2s, wire2, ps, own = take(nbig), one(), take(nbig), take(nbig), one(), take(nbig), one(), take(nbig)
    send, recv, lsem = take(3)
    return big, sm, outs, osm, r1, r1s, wire, r2, r2s, wire2, ps, own, send, recv, lsem


def _hosted_reduce_shapes(bigs, g_small):
    red_shape, scratch = _reduce_buffers(bigs, g_small)
    nres = len(red_shape)
    return red_shape, [pltpu.VMEM(r.shape, r.dtype) for r in red_shape] + scratch + [pltpu.SemaphoreType.DMA((nres,))]


def _hosted_reduce(step, n_steps, closing, stage_at, operands, results, scratch, has_small):
    nres = len(results)
    sums, rest, fsem = scratch[:nres], scratch[nres:-1], scratch[-1]
    refs = tuple(operands) + tuple(sums) + tuple(rest)

    def to_results():
        out = [pltpu.make_async_copy(sums[k], results[k], fsem.at[k]) for k in range(nres)]
        for cp in out:
            cp.start()
        for cp in out:
            cp.wait()

    stages = _reduce_protocol(*_split_reduce_refs(refs, nres - has_small, has_small))

    def last_stage():
        stages[-1]()
        to_results()

    for at, stage in zip(stage_at, stages[:-1] + [last_stage]):
        if closing == (at == n_steps):
            pl.when(step == min(at, n_steps - 1))(stage)


def reduce_grads(big, name, parts):
    chips, halves, rows_, cols = big.shape
    sub = jax.ShapeDtypeStruct((chips, halves, rows_ // parts, cols), big.dtype)

    def body(b_ref, o_ref, *scratch):
        refs = [b_ref.at[:, :, s] for s in range(parts)] + [o_ref.at[:, s] for s in range(parts)] + list(scratch)
        for stage in _reduce_protocol(*_split_reduce_refs(refs, parts, False)):
            stage()

    _, scratch = _reduce_buffers([sub] * parts, None)
    return pl.pallas_call(
        body, name=name, out_shape=jax.ShapeDtypeStruct((halves, parts, rows_ // parts, cols), F32),
        in_specs=[pl.BlockSpec(memory_space=pl.ANY)], out_specs=pl.BlockSpec(memory_space=pltpu.VMEM),
        scratch_shapes=scratch, compiler_params=pltpu.CompilerParams(vmem_limit_bytes=VMEM_LIMIT),
    )(big.reshape(chips, halves, parts, rows_ // parts, cols))


def adamw_matrices(items):
    plan, total = [], 0
    for w, _, _, _ in items:
        rows_, cols = w.shape
        tr = max(t for t in range(8, rows_ + 1, 8) if rows_ % t == 0 and t * cols * 4 <= ADAM_BLOCK_BYTES)
        plan.append((total, rows_ // tr, tr, cols))
        total += rows_ // tr
    nin = 4 * len(items)

    def body(*refs):
        i = pl.program_id(0)
        for k, (first, steps, _, _) in enumerate(plan):
            w_ref, g_ref, m_ref, v_ref = refs[4 * k:4 * k + 4]
            go_ref, d_ref, nm_ref, nv_ref = refs[nin + 4 * k:nin + 4 * k + 4]

            @pl.when((i >= first) & (i < first + steps))
            def _():
                gv = g_ref[...]
                go_ref[...] = gv
                d_ref[...], nm_ref[...], nv_ref[...] = _adam_update(w_ref[...], gv, m_ref[...], v_ref[...])

    specs, shapes = [], []
    for (first, steps, tr, cols), (w, _, _, _) in zip(plan, items):
        spec = pl.BlockSpec((tr, cols), lambda i, first=first, steps=steps: (jnp.clip(i - first, 0, steps - 1), 0))
        specs += [spec] * 4
        shapes += [jax.ShapeDtypeStruct(w.shape, F32)] * 4
    res = pl.pallas_call(
        body, name="adamw_matrices", grid=(total,), out_shape=tuple(shapes), in_specs=specs, out_specs=tuple(specs),
        compiler_params=pltpu.CompilerParams(dimension_semantics=("arbitrary",)),
    )(*[a for item in items for a in item])
    return [res[4 * k:4 * k + 4] for k in range(len(items))]


def _adam_update(w, g, m, v):
    nm = ADAM_B1 * m + (1.0 - ADAM_B1) * g
    nv = ADAM_B2 * v + (1.0 - ADAM_B2) * (g * g)
    m_hat = nm / (1.0 - ADAM_B1 ** ADAM_STEP)
    v_hat = nv / (1.0 - ADAM_B2 ** ADAM_STEP)
    return (-ADAM_LR) * (m_hat / (jnp.sqrt(v_hat) + ADAM_EPS) + ADAM_WD * w), nm, nv


def adamw_vectors(g_pack, ws, ms, vs):
    nvec = len(SMALL_VECTORS)
    n = nvec + len(SMALL_MATRICES)

    def body(*refs):
        pk = refs[0]
        w_refs, m_refs, v_refs = (refs[1 + k * n:1 + (k + 1) * n] for k in range(3))
        g_out, d_out, nm_out, nv_out = (refs[1 + (3 + k) * n:1 + (4 + k) * n] for k in range(4))
        refs[-1][...] = pk[LOSS_ROW:LOSS_ROW + 1, 0:1]
        chip = 2 * lax.axis_index("x") + lax.axis_index("y")
        for k, (name, row, width) in enumerate(SMALL_VECTORS):
            if name == "conv_w":
                g = jnp.concatenate([pk[pl.ds(row + 4 * t + chip, 1), :] for t in range(CONV_K)], axis=0)[None]
            elif width >= 128:
                g = jnp.concatenate([pk[row + r:row + r + 1, :] for r in range(width // 128)], axis=1)
            else:
                g = pk[row:row + 1, 0:width]
            g_out[k][...] = g
            d_out[k][...], nm_out[k][...], nv_out[k][...] = _adam_update(w_refs[k][...], g, m_refs[k][...], v_refs[k][...])
        for k in range(nvec, n):
            for b in range(LRU_BLOCKS):
                rows_ = pk[GATES_ROW + HEAD * b:GATES_ROW + HEAD * (b + 1), :]
                g = (pltpu.roll(rows_, HEAD, axis=1) if k > nvec else rows_)[:, 0:HEAD]
                g_out[k][0, b] = g
                d_out[k][0, b], nm_out[k][0, b], nv_out[k][0, b] = _adam_update(
                    w_refs[k][0, b], g, m_refs[k][0, b], v_refs[k][0, b])

    vm = pl.BlockSpec(memory_space=pltpu.VMEM)
    like = [jax.ShapeDtypeStruct(w.shape, F32) for w in ws]
    return pl.pallas_call(
        body, name="adamw_vectors", out_shape=(*like * 4, jax.ShapeDtypeStruct((1, 1), F32)),
        in_specs=[vm] * (1 + 3 * n), out_specs=(vm,) * (4 * n + 1),
    )(g_pack, *ws, *ms, *vs)


SMALL_VECTORS = (("norm_g", 512, 1024), ("mem_norm_g", 520, 1024), ("conv_w", 528, 512), ("conv_b", 544, 512),
                 ("b_rg", 548, 512), ("b_ig", 552, 512), ("lru_lambda", 556, 512), ("q_norm_g", 560, 64),
                 ("k_norm_g", 561, 64), ("sinks", 562, 4), ("xq_norm_g", 563, 64), ("xk_norm_g", 564, 64),
                 ("out_norm_g", 565, 1024))
LOSS_ROW = 573
SMALL_MATRICES = ("w_rg", "w_ig")
GATES_ROW = 0
SMALL_ROWS = 640


def _rope_tables(seq):
    pos = np.arange(seq, dtype=np.float32)
    inv_freq = (np.float32(ROPE_THETA) ** (-(np.arange(0, ROPE_DIM, 2, dtype=np.float32) / np.float32(ROPE_DIM)))
                ).astype(np.float32)
    ang = (pos[:, None] * inv_freq[None, :]).astype(np.float32)
    cos, sin = np.cos(ang).astype(np.float32), np.sin(ang).astype(np.float32)
    z = lambda n: np.zeros((seq, n), np.float32)
    c64 = np.concatenate([cos, cos, np.ones((seq, HEAD - ROPE_DIM), np.float32)], axis=1)
    s1_64 = np.concatenate([-sin, z(HEAD - 8)], axis=1)
    s2_64 = np.concatenate([z(8), sin, z(HEAD - ROPE_DIM)], axis=1)
    return tuple(jnp.asarray(np.concatenate([t, t], axis=1)) for t in (c64, s1_64, s2_64))


def kernel(x, mem, norm_g, mem_norm_g, w_in, conv_w, conv_b, w_rg, b_rg, w_ig, b_ig, lru_lambda, q_norm_g, k_norm_g, sinks, w_mem_kv, xq_norm_g, xk_norm_g, out_norm_g, w_out, loss_target, m_norm_g, m_mem_norm_g, m_w_in, m_conv_w, m_conv_b, m_w_rg, m_b_rg, m_w_ig, m_b_ig, m_lru_lambda, m_q_norm_g, m_k_norm_g, m_sinks, m_w_mem_kv, m_xq_norm_g, m_xk_norm_g, m_out_norm_g, m_w_out, v_norm_g, v_mem_norm_g, v_w_in, v_conv_w, v_conv_b, v_w_rg, v_b_rg, v_w_ig, v_b_ig, v_lru_lambda, v_q_norm_g, v_k_norm_g, v_sinks, v_w_mem_kv, v_xq_norm_g, v_xk_norm_g, v_out_norm_g, v_w_out):
    seq = x.shape[1]
    xs, tgt, mems = x[0], loss_target[0], mem[0]

    win_t, wout, wkv, cw, wg, gains, km, vm = gather_weights(
        w_in[0].T, w_out[0], w_mem_kv[0], conv_w, w_rg, w_ig, (q_norm_g, k_norm_g, xq_norm_g, xk_norm_g), mems,
        mem_norm_g)
    rc, rs1, rs2 = _rope_tables(seq)
    proj, ya, yb, yc, ycat, xn, dout, pswa, pmem, psink, gates, a_all, loss8 = layer_fwd(
        xs, tgt, rc, rs1, rs2, norm_g, win_t, cw, conv_b, wg, b_rg, b_ig, lru_lambda, gains, sinks, km, vm,
        out_norm_g, wout)
    (gx, dproj, g_wg, dkm, dvm, g_ng, g_og, g_cb, g_brg, g_big, g_lam, g_cw, g_qn, g_kn, g_xqn, g_sink) = layer_bwd(
        xs, dout, proj, ya, yb, yc, pswa, pmem, psink, gates, a_all, rc, rs1, rs2, norm_g, win_t, cw, wg, lru_lambda,
        gains, km, vm, out_norm_g, wout)
    g_win_t, r_out, r_kv, r_small = weight_grads(
        ycat, dout, dproj, xn, (mems, mem_norm_g, wkv, gains, dkm, dvm, g_wg, loss8), dict(
            norm_g=g_ng, conv_w=g_cw, conv_b=g_cb, b_rg=g_brg, b_ig=g_big, lru_lambda=g_lam, q_norm_g=g_qn,
            k_norm_g=g_kn, sinks=g_sink, xq_norm_g=g_xqn, out_norm_g=g_og), (0, 1, 4, 7, 8), (4, 5, 10, 12, 13))
    r_in = reduce_grads(g_win_t.reshape(N_CHIPS, 2, D_IN // 8, D_MODEL), "reduce_w_in", 6)

    r_small = r_small.reshape(SMALL_ROWS, 128)
    grads = {}
    weights = dict(norm_g=norm_g, mem_norm_g=mem_norm_g, w_in=w_in, conv_w=conv_w, conv_b=conv_b, w_rg=w_rg, b_rg=b_rg,
                   w_ig=w_ig, b_ig=b_ig, lru_lambda=lru_lambda, q_norm_g=q_norm_g, k_norm_g=k_norm_g, sinks=sinks,
                   w_mem_kv=w_mem_kv, xq_norm_g=xq_norm_g, xk_norm_g=xk_norm_g, out_norm_g=out_norm_g, w_out=w_out)
    ms = dict(norm_g=m_norm_g, mem_norm_g=m_mem_norm_g, w_in=m_w_in, conv_w=m_conv_w, conv_b=m_conv_b, w_rg=m_w_rg,
              b_rg=m_b_rg, w_ig=m_w_ig, b_ig=m_b_ig, lru_lambda=m_lru_lambda, q_norm_g=m_q_norm_g, k_norm_g=m_k_norm_g,
              sinks=m_sinks, w_mem_kv=m_w_mem_kv, xq_norm_g=m_xq_norm_g, xk_norm_g=m_xk_norm_g,
              out_norm_g=m_out_norm_g, w_out=m_w_out)
    vs = dict(norm_g=v_norm_g, mem_norm_g=v_mem_norm_g, w_in=v_w_in, conv_w=v_conv_w, conv_b=v_conv_b, w_rg=v_w_rg,
              b_rg=v_b_rg, w_ig=v_w_ig, b_ig=v_b_ig, lru_lambda=v_lru_lambda, q_norm_g=v_q_norm_g, k_norm_g=v_k_norm_g,
              sinks=v_sinks, w_mem_kv=v_w_mem_kv, xq_norm_g=v_xq_norm_g, xk_norm_g=v_xk_norm_g,
              out_norm_g=v_out_norm_g, w_out=v_w_out)

    delta, new_m, new_v = {}, {}, {}
    res_in, res_out, res_kv = adamw_matrices([
        (w_in[0].T, r_in.reshape(D_IN // 4, D_MODEL), m_w_in[0].T, v_w_in[0].T),
        (w_out[0], r_out.reshape(D_MODEL // 4, D_MODEL), m_w_out[0], v_w_out[0]),
        (w_mem_kv[0], r_kv.reshape(D_MODEL // 4, 2 * XATT_W), m_w_mem_kv[0], v_w_mem_kv[0])])
    grads["w_in"], delta["w_in"], new_m["w_in"], new_v["w_in"] = (r.T[None] for r in res_in)
    grads["w_out"], delta["w_out"], new_m["w_out"], new_v["w_out"] = (r[None] for r in res_out)
    grads["w_mem_kv"], delta["w_mem_kv"], new_m["w_mem_kv"], new_v["w_mem_kv"] = (r[None] for r in res_kv)
    small_names = [n for n, _, _ in SMALL_VECTORS] + list(SMALL_MATRICES)
    res = adamw_vectors(r_small, [weights[n] for n in small_names], [ms[n] for n in small_names],
                        [vs[n] for n in small_names])
    nall = len(small_names)
    for k, into in enumerate((grads, delta, new_m, new_v)):
        into.update(zip(small_names, res[k * nall:(k + 1) * nall]))
    loss = res[-1].reshape(())

    order = ("norm_g", "mem_norm_g", "w_in", "conv_w", "conv_b", "w_rg", "b_rg", "w_ig", "b_ig", "lru_lambda",
             "q_norm_g", "k_norm_g", "sinks", "w_mem_kv", "xq_norm_g", "xk_norm_g", "out_norm_g", "w_out")
    return (loss, gx[None], *[grads[n] for n in order], *[delta[n] for n in order], *[new_m[n] for n in order],
            *[new_v[n] for n in order])
```

```python
import jax
import jax.numpy as jnp
import numpy as np
from jax import lax
from jax.experimental import pallas as pl
from jax.experimental.pallas import tpu as pltpu

F32 = jnp.float32
_MXU = jnp.bfloat16
_WIRE = jnp.bfloat16

D_MODEL = 1024
MEM_LEN = 256
HEAD = 64
LRU_W = 512
LRU_BLOCKS = 8
CONV_K = 4
LRU_C = 8.0
SWA_W = 256
KV_W = 128
XATT_W = 256
BLOCK = 128
D_IN = 2304
ROPE_THETA = 500000.0
ROPE_DIM = 16
EPS = 1e-6
NEG_INF = -1e30
C_LRUX, C_LRUG, C_SQ, C_SK, C_SV, C_SWAG, C_XQ, C_XG = 0, 512, 1024, 1280, 1408, 1536, 1792, 2048
G_Q, G_K, G_XQ, G_XK, GAINS_W = 0, 128, 256, 512, 768

ADAM_LR, ADAM_B1, ADAM_B2, ADAM_EPS, ADAM_WD, ADAM_STEP = 0.001, 0.9, 0.999, 1e-08, 0.01, 10

N_CHIPS = 4
ROW_TILE = 256
VMEM_LIMIT = 56 * 1024 * 1024
ADAM_BLOCK_BYTES = 640 * 1024
MESH = pl.DeviceIdType.MESH


def _mm(a, b):
    return jnp.dot(a.astype(_MXU), b.astype(_MXU), preferred_element_type=F32)


def _mm_nt(a, b):
    return lax.dot_general(a.astype(_MXU), b.astype(_MXU), (((1,), (1,)), ((), ())), preferred_element_type=F32)


def _mm_tn(a, b):
    return lax.dot_general(a.astype(_MXU), b.astype(_MXU), (((0,), (0,)), ((), ())), preferred_element_type=F32)


def _group_matrix(width):
    r = lax.shift_right_logical(lax.broadcasted_iota(jnp.int32, (width, width), 0), 6)
    c = lax.shift_right_logical(lax.broadcasted_iota(jnp.int32, (width, width), 1), 6)
    return (r == c).astype(_MXU)


def _seg_mean(x, gm):
    return jnp.dot(x.astype(_MXU), gm, preferred_element_type=F32) * (1.0 / HEAD)


def _row_mean(x):
    return jnp.mean(x, axis=-1, keepdims=True)


def _col_sum(x):
    return jnp.sum(x, axis=0, keepdims=True)


def _sigmoid(x):
    return jax.nn.sigmoid(x)


def _softplus(z):
    e = jnp.exp(-jnp.abs(z))
    u = 1.0 + e
    log1p_e = jnp.where(u == 1.0, e, jnp.log(u) * (e / (u - 1.0)))
    return jnp.maximum(z, 0.0) + log1p_e


def _rope(t, c, s1, s2):
    return t * c + pltpu.roll(t, 120, 1) * s1 + pltpu.roll(t, 8, 1) * s2


def _rope_bwd(d, c, s1, s2):
    return d * c + pltpu.roll(d * s1, 8, 1) + pltpu.roll(d * s2, 120, 1)


def _fold_heads(v):
    out = v
    for k in range(1, v.shape[1] // HEAD):
        out = out + pltpu.roll(v, HEAD * k, 1)
    return out


def _lane_mask(width, lo, hi):
    lane = lax.broadcasted_iota(jnp.int32, (1, width), 1)
    return ((lane >= lo) & (lane < hi)).astype(F32)


def _swa_mask(first_block):
    qi = lax.broadcasted_iota(jnp.int32, (BLOCK, 2 * BLOCK), 0)
    kj = lax.broadcasted_iota(jnp.int32, (BLOCK, 2 * BLOCK), 1)
    rel = qi + BLOCK - kj
    ok = (rel >= 0) & (rel < BLOCK)
    return ok & (jnp.logical_not(first_block) | (kj >= BLOCK))


def _place_kv(t, scale):
    lo = t * (_lane_mask(KV_W, 0, HEAD) * scale)
    hi = t * (_lane_mask(KV_W, HEAD, KV_W) * scale)
    return [a.astype(_MXU) for a in (lo, pltpu.roll(lo, HEAD, 1), pltpu.roll(hi, HEAD, 1), hi)]


def _unplace_kv(d):
    return (_lane_mask(KV_W, 0, HEAD) * (d[0] + pltpu.roll(d[1], HEAD, 1))
            + _lane_mask(KV_W, HEAD, KV_W) * (d[3] + pltpu.roll(d[2], HEAD, 1)))


def _swa_probs(qh, ka, mask, sink):
    s = _mm_nt(qh, ka)
    s = jnp.where(mask, s, NEG_INF)
    m = jnp.maximum(jnp.max(s, axis=-1, keepdims=True), sink)
    p = jnp.exp(s - m)
    esink = jnp.exp(sink - m)
    inv = 1.0 / (jnp.sum(p, axis=-1, keepdims=True) + esink)
    return p * inv, esink * inv


def _mem_probs(s_all):
    out = []
    for j in range(4):
        s = s_all[:, MEM_LEN * j:MEM_LEN * (j + 1)]
        p = jnp.exp(s - jnp.max(s, axis=-1, keepdims=True))
        out.append(p * (1.0 / jnp.sum(p, axis=-1, keepdims=True)))
    return out


def _head_rows(t, scale):
    return jnp.concatenate([t * (_lane_mask(XATT_W, HEAD * j, HEAD * (j + 1)) * scale) for j in range(4)], axis=0)


def _lru_gates(xc, wg_ref, brg, big, lam):
    p0 = _mm(xc[:, :256], wg_ref[0])
    p1 = _mm(xc[:, 256:], wg_ref[1])
    rg = _sigmoid(jnp.concatenate([p0[:, :256], p1[:, :256]], axis=1) + brg)
    ig = _sigmoid(jnp.concatenate([p0[:, 256:], p1[:, 256:]], axis=1) + big)
    sp = _softplus(-lam)
    la = (-LRU_C) * rg * sp
    a = jnp.exp(la)
    th = jnp.tanh(la)
    one_minus_a2 = (-2.0 * th) / (1.0 - th)
    return rg, ig, sp, a, jnp.sqrt(one_minus_a2)


def _const_spec(shape, single=False):
    zeros = (0,) * len(shape)
    if single:
        return pl.BlockSpec(shape, lambda i: zeros, pipeline_mode=pl.Buffered(1))
    return pl.BlockSpec(shape, lambda i: zeros)


def _chip_of(x, y):
    return 2 * x + y


def _partners(x, y, c):
    north = c == 1
    near = (jnp.where(north, 1 - x, x), jnp.where(north, y, 1 - y))
    far = (jnp.where(north, x, 1 - x), jnp.where(north, 1 - y, y))
    return near, far, (1 - x, 1 - y)


def gather_weights(win_t, wout, wkv, conv_w, w_rg, w_ig, head_gains, mem, mem_g):
    arrs = (win_t, wout, wkv)
    n = len(arrs)
    pieces = [(a, 0, arr.shape[0] // 2) for a, arr in enumerate(arrs)]
    npc = len(pieces)

    def body(a0, a1, a2, cw_in, wrg_ref, wig_ref, q_ref, k_ref, xq_ref, xk_ref, mem_ref, mg_ref,
             o0, o1, o2, cw_out, wg_ref, gn_ref, km_ref, vm_ref, s0, s1, s2, cw, ocw, send, recv, lsem):
        ins, outs = (s0, s1, s2), (o0, o1, o2)
        for src, dst in zip((a0, a1, a2), ins):
            dst[...] = src[...].astype(dst.dtype)
        cw[...] = jnp.zeros(cw.shape, F32)
        cw[0:CONV_K, :] = cw_in[0]
        x, y, c = lax.axis_index("x"), lax.axis_index("y"), lax.axis_index("c")
        sibling = (x, y, 1 - c)
        near, far, diag = _partners(x, y, c)
        chips = [near, far, diag]
        me = _chip_of(x, y)

        def landed(p, chip, half):
            a, off, rows_ = pieces[p]
            r = ins[a].shape[0]
            return outs[a].at[pl.ds(pl.multiple_of(chip * r + half * (r // 2) + off, 16), rows_)]

        def mine(p):
            a, off, rows_ = pieces[p]
            return ins[a].at[pl.ds(pl.multiple_of(c * (ins[a].shape[0] // 2) + off, 16), rows_)]

        def copy(k, src, dst, to):
            return pltpu.make_async_remote_copy(src_ref=src, dst_ref=dst, send_sem=send.at[k], recv_sem=recv.at[k],
                                                device_id=to, device_id_type=MESH)

        def cw_rows(chip):
            return ocw.at[pl.ds(pl.multiple_of(chip * 8, 8), 8)]

        locals_ = []
        for a in range(n):
            r = ins[a].shape[0]
            locals_.append(pltpu.make_async_copy(ins[a], outs[a].at[pl.ds(pl.multiple_of(me * r, 16), r)], lsem.at[a]))
        locals_.append(pltpu.make_async_copy(cw, cw_rows(me), lsem.at[n]))
        for cp in locals_:
            cp.start()

        sent = []
        for p in range(npc):
            for j in range(2):
                sent.append(copy(p * 6 + j, mine(p), landed(p, me, c), (*chips[j], c)))
        for j, chip in enumerate(chips):
            sent.append(copy(npc * 6 + j, cw, cw_rows(me), (*chip, c)))
        for cp in sent:
            cp.start()

        gn_ref[...] = jnp.concatenate([q_ref[...]] * 2 + [k_ref[...]] * 2 + [xq_ref[...]] * 4 + [xk_ref[...]] * 4,
                                      axis=1)
        zeros = lambda lanes: [jnp.zeros((HEAD, lanes), F32)] if lanes else []
        for h in range(2):
            for b in range(4):
                row = []
                for w_ref in (wrg_ref, wig_ref):
                    row += zeros(HEAD * b) + [w_ref[0, 4 * h + b]] + zeros(HEAD * (3 - b))
                wg_ref[h, HEAD * b:HEAD * (b + 1), :] = jnp.concatenate(row, axis=1).astype(wg_ref.dtype)

        for j in range(3):
            for p in range(npc):
                got = landed(p, _chip_of(*chips[j]), c)
                copy(p * 6 + j, got, got, sibling).wait_recv()
                if j == 0:
                    sent.append(copy(p * 6 + 2, got, got, (*far, c)))
                    sent[-1].start()
                sent.append(copy(p * 6 + 3 + j, got, got, sibling))
                sent[-1].start()
        for p in range(npc):
            for j in range(3):
                got = landed(p, _chip_of(*chips[(1, 0, 2)[j]]), 1 - c)
                copy(p * 6 + 3 + j, got, got, sibling).wait_recv()
        for j, chip in enumerate(chips):
            got = cw_rows(_chip_of(*chip))
            copy(npc * 6 + j, got, got, (*chip, c)).wait_recv()
        for cp in sent:
            cp.wait_send()
        for cp in locals_:
            cp.wait()
        for chip in range(N_CHIPS):
            cw_out[:, 128 * chip:128 * (chip + 1)] = ocw[8 * chip:8 * chip + CONV_K, :]

        mem_v = mem_ref[...]
        mn = mem_v * lax.rsqrt(_row_mean(mem_v * mem_v) + EPS) * mg_ref[...]
        mkv = _mm(mn, o2[...])
        kpre = mkv[:, :XATT_W]
        km = kpre * lax.rsqrt(_seg_mean(kpre * kpre, _group_matrix(XATT_W)) + EPS) * gn_ref[:, G_XK:GAINS_W]
        km_ref[...] = _head_rows(km, 0.125).astype(km_ref.dtype)
        vm_ref[...] = _head_rows(mkv[:, XATT_W:], 1.0).astype(vm_ref.dtype)

    vm = pl.BlockSpec(memory_space=pltpu.VMEM)
    hbm = pl.BlockSpec(memory_space=pl.ANY)
    head_rows = jax.ShapeDtypeStruct((4 * MEM_LEN, XATT_W), _MXU)
    out_shape = tuple(jax.ShapeDtypeStruct((N_CHIPS * a.shape[0],) + a.shape[1:], _MXU) for a in arrs) + (
        jax.ShapeDtypeStruct((CONV_K, LRU_W), F32), jax.ShapeDtypeStruct((2, 256, 512), _MXU),
        jax.ShapeDtypeStruct((1, GAINS_W), F32), head_rows, head_rows)
    n_rdma = npc * 6 + 3
    return pl.pallas_call(
        body, name="gather_weights", out_shape=out_shape,
        in_specs=[vm] * 12, out_specs=(hbm, hbm, vm, vm, vm, vm, vm, vm),
        scratch_shapes=[pltpu.VMEM(a.shape, _MXU) for a in arrs] + [
            pltpu.VMEM((8, 128), F32), pltpu.VMEM((N_CHIPS * 8, 128), F32),
            pltpu.SemaphoreType.DMA((n_rdma,)), pltpu.SemaphoreType.DMA((n_rdma,)), pltpu.SemaphoreType.DMA((n + 1,))],
        compiler_params=pltpu.CompilerParams(vmem_limit_bytes=VMEM_LIMIT),
    )(win_t, wout, wkv, conv_w, w_rg, w_ig, *head_gains, mem, mem_g)


def _mem_bwd_and_pack(mem_ref, g_ref, w_ref, gn_ref, dkm_ref, dvm_ref, gg_ref, loss_ref, vectors, gw_ref, pk_ref):
    first_row = {name: (row, width) for name, row, width in SMALL_VECTORS}
    half_rows = SMALL_ROWS // 2
    pk_ref[...] = jnp.zeros(pk_ref.shape, F32)

    def rows_at(at, n):
        assert at // half_rows == (at + n - 1) // half_rows
        return at // half_rows, slice(at % half_rows, at % half_rows + n), slice(None)

    def put(name, src):
        row, width = first_row[name]
        per_row = 1 if width < 128 else src.shape[1] // 128
        for t in range(src.shape[0]):
            for r in range(per_row):
                pk_ref[rows_at(row + per_row * t + r, 1)] = src[t:t + 1, 128 * r:128 * (r + 1)]

    for name, ref in vectors.items():
        put(name, ref)
    pk_ref[rows_at(LOSS_ROW, 1)] = loss_ref[0:1, :]
    upper = lax.broadcasted_iota(jnp.int32, (HEAD, 128), 1) >= HEAD
    for h in range(2):
        for b in range(4):
            rg = gg_ref[h, HEAD * b:HEAD * (b + 1), 128 * (b // 2):128 * (b // 2 + 1)]
            ig = gg_ref[h, HEAD * b:HEAD * (b + 1), 256 + 128 * (b // 2):256 + 128 * (b // 2 + 1)]
            if b % 2:
                rg = pltpu.roll(rg, HEAD, axis=1)
            else:
                ig = pltpu.roll(ig, HEAD, axis=1)
            pk_ref[rows_at(GATES_ROW + HEAD * (4 * h + b), HEAD)] = jnp.where(upper, ig, rg)

    mem_v = mem_ref[...]
    mh = mem_v * lax.rsqrt(_row_mean(mem_v * mem_v) + EPS)
    mn = mh * g_ref[...]
    mkv = _mm(mn, w_ref[...])
    kpre = mkv[:, :XATT_W]
    gm = _group_matrix(XATT_W)
    rk = lax.rsqrt(_seg_mean(kpre * kpre, gm) + EPS)
    kn = kpre * rk
    dk = jnp.zeros((MEM_LEN, XATT_W), F32)
    dv = jnp.zeros((MEM_LEN, XATT_W), F32)
    for j in range(4):
        mj = _lane_mask(XATT_W, HEAD * j, HEAD * (j + 1))
        dk = dk + dkm_ref[:, MEM_LEN * j:MEM_LEN * (j + 1)].T * (mj * 0.125)
        dv = dv + dvm_ref[:, MEM_LEN * j:MEM_LEN * (j + 1)].T * mj
    put("xk_norm_g", _fold_heads(_col_sum(dk * kn)))
    dkn = dk * gn_ref[:, G_XK:GAINS_W]
    dkpre = rk * (dkn - kn * _seg_mean(dkn * kn, gm))
    dmkv = jnp.concatenate([dkpre, dv], axis=1)
    gw_ref[...] = _mm_tn(mn, dmkv).reshape(gw_ref.shape)
    dmn = _mm_nt(dmkv, w_ref[...])
    put("mem_norm_g", _col_sum(dmn * mh))


def layer_fwd(x, tgt, rc, rs1, rs2, ng, win_t, cw, cb, wg, brg, big, lam, gains, sinks, km, vm, og, wout):
    seq = x.shape[0]
    tm = min(ROW_TILE, seq)
    nt = seq // tm
    nb = tm // BLOCK

    def body(x_ref, t_ref, c_ref, s1_ref, s2_ref, ng_ref, win_ref, cw_ref, cb_ref, wg_ref, brg_ref, big_ref, lam_ref,
             gn_ref, sink_ref, km_ref, vm_ref, og_ref, wout_ref,
             proj_ref, ya_ref, yb_ref, yc_ref, ycat_ref, xn_ref, dout_ref, pswa_ref, pmem_ref, psink_ref, gates_ref,
             a_ref, loss_ref,
             ext_ref, b_scr, hc_ref, kp_ref, vp_ref, lacc_ref):
        i = pl.program_id(0)

        @pl.when(i == 0)
        def _():
            ext_ref[0:8, :] = jnp.zeros((8, LRU_W), F32)
            hc_ref[...] = jnp.zeros_like(hc_ref)
            kp_ref[...] = jnp.zeros_like(kp_ref)
            vp_ref[...] = jnp.zeros_like(vp_ref)
            lacc_ref[...] = jnp.zeros_like(lacc_ref)

        xv = x_ref[...]
        xn = (xv * lax.rsqrt(_row_mean(xv * xv) + EPS) * ng_ref[...]).astype(_MXU)
        xn_ref[...] = xn.astype(xn_ref.dtype)
        proj_ref[...] = _mm_nt(xn, win_ref[...])

        u = proj_ref[:, C_LRUX:C_LRUX + LRU_W]
        ext_ref[8:8 + tm, :] = u
        xc = cb_ref[...]
        for k in range(CONV_K):
            xc = xc + cw_ref[k:k + 1, :] * ext_ref[pl.ds(5 + k, tm), :]
        ext_ref[0:8, :] = u[tm - 8:tm, :]
        rg, ig, sp, a, sq = _lru_gates(xc, wg_ref, brg_ref[...], big_ref[...], lam_ref[...])
        for k, t in enumerate((xc, rg, ig, sq)):
            gates_ref[:, LRU_W * k:LRU_W * (k + 1)] = t.astype(gates_ref.dtype)
        a_ref[...] = a
        b_scr[...] = sq * (ig * xc)
        row8 = lax.broadcasted_iota(jnp.int32, (8, LRU_W), 0)

        def scan_step(g, carry):
            r0 = pl.multiple_of(g * 8, 8)
            av = a_ref[pl.ds(r0, 8), :]
            bv = b_scr[pl.ds(r0, 8), :]
            for d in (1, 2, 4):
                a_sh = jnp.where(row8 >= d, pltpu.roll(av, d, 0), 1.0)
                b_sh = jnp.where(row8 >= d, pltpu.roll(bv, d, 0), 0.0)
                bv = bv + av * b_sh
                av = av * a_sh
            hv = bv + av * carry
            ya_ref[pl.ds(r0, 8), :] = hv
            return hv[7:8, :]

        hc_ref[0:1, :] = lax.fori_loop(0, tm // 8, scan_step, hc_ref[0:1, :], unroll=True)

        gm128 = _group_matrix(KV_W)
        cv, s1v, s2v = c_ref[...], s1_ref[...], s2_ref[...]

        def head_norm_rope(t, g):
            n = t * lax.rsqrt(_seg_mean(t * t, gm128) + EPS)
            return _rope(n * g, cv, s1v, s2v)

        qs_ = (head_norm_rope(proj_ref[:, C_SQ:C_SQ + 128], gn_ref[:, G_Q:G_K]).astype(_MXU),
               head_norm_rope(proj_ref[:, C_SQ + 128:C_SQ + 256], gn_ref[:, G_Q:G_K]).astype(_MXU))
        kr = head_norm_rope(proj_ref[:, C_SK:C_SK + KV_W], gn_ref[:, G_K:G_XQ])
        sv = proj_ref[:, C_SV:C_SV + KV_W]
        ka = _place_kv(jnp.concatenate([kp_ref[...], kr], axis=0), 0.125)
        va = _place_kv(jnp.concatenate([vp_ref[...], sv], axis=0), 1.0)
        kp_ref[...] = kr[tm - BLOCK:tm, :]
        vp_ref[...] = sv[tm - BLOCK:tm, :]
        lane128 = lax.broadcasted_iota(jnp.int32, (1, 128), 1)
        for b in range(nb):
            mask = _swa_mask((i == 0) & (b == 0)) if b == 0 else _swa_mask(False)
            band = slice(BLOCK * b, BLOCK * b + 2 * BLOCK)
            blk = slice(BLOCK * b, BLOCK * (b + 1))
            psink = jnp.zeros((BLOCK, 128), F32)
            for j in range(4):
                p, pk = _swa_probs(qs_[j // 2][blk], ka[j][band], mask, sink_ref[0, j])
                pswa_ref[blk, 2 * BLOCK * j:2 * BLOCK * (j + 1)] = p.astype(pswa_ref.dtype)
                psink = jnp.where(lane128 == j, pk, psink)
            psink_ref[blk, :] = psink
            for h in range(2):
                yb_ref[blk, KV_W * h:KV_W * (h + 1)] = _mm(
                    pswa_ref[blk, 4 * BLOCK * h:4 * BLOCK * (h + 1)],
                    jnp.concatenate([va[2 * h][band], va[2 * h + 1][band]], axis=0))

        gm256 = _group_matrix(XATT_W)
        xq = proj_ref[:, C_XQ:C_XQ + XATT_W]
        qx = xq * lax.rsqrt(_seg_mean(xq * xq, gm256) + EPS) * gn_ref[:, G_XQ:G_XK]
        pm = _mem_probs(_mm_nt(qx, km_ref[...]))
        for j in range(4):
            pmem_ref[:, MEM_LEN * j:MEM_LEN * (j + 1)] = pm[j].astype(pmem_ref.dtype)
        yc = _mm(pmem_ref[...], vm_ref[...])
        yc_ref[...] = yc

        def gated(y, g, gate):
            return y * lax.rsqrt(_row_mean(y * y) + EPS) * g * (gate * _sigmoid(gate))

        ogv = og_ref[...]
        za = gated(ya_ref[...], ogv[:, :512], proj_ref[:, C_LRUG:C_LRUG + LRU_W])
        zb = gated(yb_ref[...], ogv[:, 512:768], proj_ref[:, C_SWAG:C_SWAG + SWA_W])
        zc = gated(yc, ogv[:, 768:], proj_ref[:, C_XG:C_XG + XATT_W])
        ycat_ref[:, 0:512] = za.astype(ycat_ref.dtype)
        ycat_ref[:, 512:768] = zb.astype(ycat_ref.dtype)
        ycat_ref[:, 768:1024] = zc.astype(ycat_ref.dtype)
        out = xv + _mm(ycat_ref[...], wout_ref[...])
        err = out - t_ref[...]
        dout_ref[...] = (err * (1.0 / D_MODEL)).astype(dout_ref.dtype)
        lacc_ref[...] = lacc_ref[...] + (0.5 / D_MODEL) * jnp.sum(err * err)

        @pl.when(i == nt - 1)
        def _():
            loss_ref[...] = lacc_ref[...]

    def rows(ncol):
        return pl.BlockSpec((tm, ncol), lambda i: (i, 0))

    in_specs = [rows(D_MODEL), rows(D_MODEL), rows(128), rows(128), rows(128),
                _const_spec((1, D_MODEL)), _const_spec((D_IN, D_MODEL), True), _const_spec((CONV_K, LRU_W)),
                _const_spec((1, LRU_W)), _const_spec((2, 256, 512), True), _const_spec((1, LRU_W)),
                _const_spec((1, LRU_W)), _const_spec((1, LRU_W)), _const_spec((1, GAINS_W)), pl.BlockSpec(memory_space=pltpu.SMEM),
                _const_spec((4 * MEM_LEN, XATT_W), True), _const_spec((4 * MEM_LEN, XATT_W), True),
                _const_spec((1, D_MODEL)), _const_spec((D_MODEL, D_MODEL), True)]
    out_shape = (jax.ShapeDtypeStruct((seq, D_IN), F32), jax.ShapeDtypeStruct((seq, LRU_W), F32),
                 jax.ShapeDtypeStruct((seq, SWA_W), F32), jax.ShapeDtypeStruct((seq, XATT_W), F32),
                 jax.ShapeDtypeStruct((seq, D_MODEL), _MXU), jax.ShapeDtypeStruct((seq, D_MODEL), _MXU),
                 jax.ShapeDtypeStruct((seq, D_MODEL), _MXU), jax.ShapeDtypeStruct((seq, 4 * 2 * BLOCK), _MXU),
                 jax.ShapeDtypeStruct((seq, 4 * MEM_LEN), _MXU), jax.ShapeDtypeStruct((seq, 128), F32),
                 jax.ShapeDtypeStruct((seq, 4 * LRU_W), _MXU), jax.ShapeDtypeStruct((seq, LRU_W), F32),
                 jax.ShapeDtypeStruct((8, 128), F32))
    out_specs = (rows(D_IN), rows(LRU_W), rows(SWA_W), rows(XATT_W), rows(D_MODEL), rows(D_MODEL), rows(D_MODEL),
                 rows(4 * 2 * BLOCK), rows(4 * MEM_LEN), rows(128), rows(4 * LRU_W), rows(LRU_W),
                 _const_spec((8, 128)))
    scratch = [pltpu.VMEM((tm + 8, LRU_W), F32), pltpu.VMEM((tm, LRU_W), F32),
               pltpu.VMEM((8, LRU_W), F32), pltpu.VMEM((BLOCK, KV_W), F32), pltpu.VMEM((BLOCK, KV_W), F32),
               pltpu.VMEM((8, 128), F32)]
    return pl.pallas_call(
        body, name="layer_fwd", grid=(nt,), out_shape=out_shape, in_specs=in_specs, out_specs=out_specs,
        scratch_shapes=scratch,
        compiler_params=pltpu.CompilerParams(dimension_semantics=("arbitrary",), vmem_limit_bytes=VMEM_LIMIT),
    )(x, tgt, rc, rs1, rs2, ng, win_t, cw, cb, wg, brg, big, lam, gains, sinks, km, vm, og, wout)


def weight_grads(ycat, dout, dproj, xn, mem_operands, vectors, early_at, late_at):
    seq, ncol = xn.shape
    blk = 256
    n_out, n_in = ycat.shape[1] // blk, dproj.shape[1] // blk
    assert n_out == N_CHIPS and late_at[0] >= n_out
    g_out = jax.ShapeDtypeStruct((N_CHIPS, 2, blk // 2, dout.shape[1]), F32)
    g_kv = jax.ShapeDtypeStruct((N_CHIPS, 2, D_MODEL // 8, 2 * XATT_W), F32)
    g_small = jax.ShapeDtypeStruct((2, SMALL_ROWS // 2, 128), F32)
    shape_e, scratch_e = _hosted_reduce_shapes([g_kv], g_small)
    shape_l, scratch_l = _hosted_reduce_shapes([g_out], None)
    n_mem, names = len(mem_operands), tuple(vectors)

    def body(l1_ref, r1_ref, l2_ref, r2_ref, *refs):
        mem_refs, vec_refs = refs[:n_mem], refs[n_mem:n_mem + len(names)]
        o_ref, sum_out, sum_kv, sum_sm, gout_scr, gkv_scr, pack_scr, *scratch = refs[n_mem + len(names):]
        j = pl.program_id(0)

        @pl.when(j == 0)
        def _():
            _mem_bwd_and_pack(*mem_refs, dict(zip(names, vec_refs)), gkv_scr, pack_scr)

        def reduce_stages(closing):
            _hosted_reduce(j, n_out + n_in, closing, early_at, (gkv_scr, pack_scr), (sum_kv, sum_sm),
                           scratch[:len(scratch_e)], True)
            _hosted_reduce(j, n_out + n_in, closing, late_at, (gout_scr,), (sum_out,), scratch[len(scratch_e):], False)

        reduce_stages(False)

        @pl.when(j < n_out)
        def _():
            gout_scr[j] = _mm_tn(l1_ref[...], r1_ref[...]).reshape(g_out.shape[1:])

        @pl.when(j >= n_out)
        def _():
            o_ref[...] = _mm_tn(l2_ref[...], r2_ref[...])

        reduce_stages(True)

    vm = pl.BlockSpec(memory_space=pltpu.VMEM)
    hbm = pl.BlockSpec(memory_space=pl.ANY)
    return pl.pallas_call(
        body, name="weight_grads", grid=(n_out + n_in,),
        out_shape=(jax.ShapeDtypeStruct((dproj.shape[1], ncol), F32), *shape_l, *shape_e),
        in_specs=[pl.BlockSpec((seq, blk), lambda j: (0, jnp.minimum(j, n_out - 1))), _const_spec(dout.shape, True),
                  pl.BlockSpec((seq, blk), lambda j: (0, jnp.maximum(j - n_out, 0))), _const_spec(xn.shape, True)]
        + [vm] * (n_mem + len(names)),
        out_specs=(pl.BlockSpec((blk, ncol), lambda j: (jnp.maximum(j - n_out, 0), 0)), hbm, hbm, hbm),
        scratch_shapes=[pltpu.VMEM(s.shape, F32) for s in (g_out, g_kv, g_small)] + scratch_e + scratch_l,
        compiler_params=pltpu.CompilerParams(dimension_semantics=("arbitrary",), vmem_limit_bytes=VMEM_LIMIT),
    )(ycat, dout, dproj, xn, *mem_operands, *vectors.values())


def layer_bwd(x, dout, proj, ya, yb, yc, pswa, pmem, psink, gates, a_all, rc, rs1, rs2, ng, win_t, cw, wg, lam, gains,
              km, vm, og, wout):
    seq = x.shape[0]
    tm = min(ROW_TILE, seq)
    nt = seq // tm
    nb = tm // BLOCK

    def body(x_ref, dout_ref, proj_ref, ya_ref, yb_ref, yc_ref, pswa_ref, pmem_ref, psink_ref, gates_ref, a_ref,
             c_ref, s1_ref, s2_ref,
             yah_ref, kvh_ref, ch_ref, s1h_ref, s2h_ref,
             ng_ref, win_ref, cw_ref, wg_ref, lam_ref, gn_ref, km_ref, vm_ref, og_ref, wout_ref,
             gx_ref, dproj_ref, gwg_ref, dkm_ref, dvm_ref, gng_ref, gog_ref, gcb_ref, gbrg_ref, gbig_ref, glam_ref,
             gcw_ref, gqn_ref, gkn_ref, gxqn_ref, gsink_ref,
             hext_ref, aext_ref, an_scr, dh_scr, g_scr, dxc_ext, gcar_ref, dkcar_ref, dvcar_ref):
        i = pl.program_id(0)
        tile = nt - 1 - i
        first_tile = tile == 0

        @pl.when(i == 0)
        def _():
            for r in (gwg_ref, dkm_ref, dvm_ref, gng_ref, gog_ref, gcb_ref, gbrg_ref, gbig_ref, glam_ref, gcw_ref,
                      gqn_ref, gkn_ref, gxqn_ref, gsink_ref, gcar_ref, dkcar_ref, dvcar_ref):
                r[...] = jnp.zeros_like(r)
            dxc_ext[tm:tm + 8, :] = jnp.zeros((8, LRU_W), F32)
            aext_ref[tm:tm + 8, :] = jnp.zeros((8, LRU_W), F32)

        xv = x_ref[...]
        dov = dout_ref[...]
        dz = _mm_nt(dov, wout_ref[...])
        ogv = og_ref[...]

        def group_bwd(y, gate, g, dzg):
            r = lax.rsqrt(_row_mean(y * y) + EPS)
            n = y * r
            sg = _sigmoid(gate)
            dgate = dzg * (n * g) * (sg * (1.0 + gate * (1.0 - sg)))
            dng = dzg * (gate * sg)
            dn = dng * g
            return r * (dn - n * _row_mean(dn * n)), dgate, _col_sum(dng * n)

        dya, dga, goa = group_bwd(ya_ref[...], proj_ref[:, C_LRUG:C_LRUG + LRU_W], ogv[:, :512], dz[:, :512])
        dyb, dgb, gob = group_bwd(yb_ref[...], proj_ref[:, C_SWAG:C_SWAG + SWA_W], ogv[:, 512:768], dz[:, 512:768])
        dyc, dgc, goc = group_bwd(yc_ref[...], proj_ref[:, C_XG:C_XG + XATT_W], ogv[:, 768:], dz[:, 768:])
        gog_ref[...] += jnp.concatenate([goa, gob, goc], axis=1)
        dproj_ref[:, C_LRUG:C_LRUG + LRU_W] = dga.astype(dproj_ref.dtype)
        dproj_ref[:, C_SWAG:C_SWAG + SWA_W] = dgb.astype(dproj_ref.dtype)
        dproj_ref[:, C_XG:C_XG + XATT_W] = dgc.astype(dproj_ref.dtype)

        gm256 = _group_matrix(XATT_W)
        xq = proj_ref[:, C_XQ:C_XQ + XATT_W]
        rq = lax.rsqrt(_seg_mean(xq * xq, gm256) + EPS)
        qn = xq * rq
        qx = qn * gn_ref[:, G_XQ:G_XK]
        qxb = qx.astype(_MXU)
        dycb = dyc.astype(_MXU)
        dp_all = _mm_nt(dycb, vm_ref[...])
        dsm = []
        for j in range(4):
            pj = pmem_ref[:, MEM_LEN * j:MEM_LEN * (j + 1)].astype(F32)
            dp = dp_all[:, MEM_LEN * j:MEM_LEN * (j + 1)]
            dsm.append((pj * (dp - jnp.sum(pj * dp, axis=-1, keepdims=True))).astype(_MXU))
        ds_all = jnp.concatenate(dsm, axis=1)
        dvm_ref[...] += _mm_tn(dycb, pmem_ref[...])
        dkm_ref[...] += _mm_tn(qxb, ds_all)
        dqx = _mm(ds_all, km_ref[...])
        gxqn_ref[...] += _col_sum(dqx * qn)
        dqn = dqx * gn_ref[:, G_XQ:G_XK]
        dproj_ref[:, C_XQ:C_XQ + XATT_W] = (rq * (dqn - qn * _seg_mean(dqn * qn, gm256))).astype(dproj_ref.dtype)

        gm128 = _group_matrix(KV_W)
        cv, s1v, s2v = c_ref[...], s1_ref[...], s2_ref[...]

        def head_norm(t):
            r = lax.rsqrt(_seg_mean(t * t, gm128) + EPS)
            return t * r, r

        qn_, qr_ = zip(head_norm(proj_ref[:, C_SQ:C_SQ + 128]), head_norm(proj_ref[:, C_SQ + 128:C_SQ + 256]))
        qrope = [_rope(qn_[h] * gn_ref[:, G_Q:G_K], cv, s1v, s2v).astype(_MXU) for h in range(2)]
        kn, krr = head_norm(proj_ref[:, C_SK:C_SK + KV_W])
        kr = _rope(kn * gn_ref[:, G_K:G_XQ], cv, s1v, s2v)
        khn, _ = head_norm(kvh_ref[:, 0:KV_W])
        khr = _rope(khn * gn_ref[:, G_K:G_XQ], ch_ref[...], s1h_ref[...], s2h_ref[...])
        ka = _place_kv(jnp.concatenate([khr, kr], axis=0), 0.125)
        va = _place_kv(jnp.concatenate([kvh_ref[:, KV_W:2 * KV_W], proj_ref[:, C_SV:C_SV + KV_W]], axis=0), 1.0)
        lane128 = lax.broadcasted_iota(jnp.int32, (1, 128), 1)
        gsink = jnp.zeros((1, 128), F32)
        dk_band, dv_band, dq_blk = [], [], []
        for b in range(nb):
            band = slice(BLOCK * b, BLOCK * b + 2 * BLOCK)
            blk = slice(BLOCK * b, BLOCK * (b + 1))
            dka, dva, dsb = [], [], []
            deltas = jnp.zeros((BLOCK, 128), F32)
            for j in range(4):
                qh = qrope[j // 2][blk]
                doh = dyb[blk, KV_W * (j // 2):KV_W * (j // 2 + 1)].astype(_MXU)
                pb = pswa_ref[blk, 2 * BLOCK * j:2 * BLOCK * (j + 1)]
                p = pb.astype(F32)
                dp = _mm_nt(doh, va[j][band])
                delta = jnp.sum(p * dp, axis=-1, keepdims=True)
                ds = (p * (dp - delta)).astype(_MXU)
                deltas = jnp.where(lane128 == j, delta, deltas)
                dva.append(_mm_tn(pb, doh))
                dka.append(_mm_tn(ds, qh))
                dsb.append(ds)
            gsink = gsink - _col_sum(psink_ref[blk, :] * deltas)
            dk_band.append(_unplace_kv(dka) * 0.125)
            dv_band.append(_unplace_kv(dva))
            dq_blk.append([_mm(jnp.concatenate(dsb[2 * h:2 * h + 2], axis=1),
                               jnp.concatenate([ka[2 * h][band], ka[2 * h + 1][band]], axis=0)) for h in range(2)])
        gsink_ref[...] += gsink
        dk_rows = [dk_band[b][BLOCK:] + (dk_band[b + 1][:BLOCK] if b + 1 < nb else dkcar_ref[...]) for b in range(nb)]
        dv_rows = [dv_band[b][BLOCK:] + (dv_band[b + 1][:BLOCK] if b + 1 < nb else dvcar_ref[...]) for b in range(nb)]
        dkcar_ref[...] = dk_band[0][:BLOCK]
        dvcar_ref[...] = dv_band[0][:BLOCK]
        dkg = _rope_bwd(jnp.concatenate(dk_rows, axis=0), cv, s1v, s2v)
        gkn = _col_sum(dkg * kn)
        dkn = dkg * gn_ref[:, G_K:G_XQ]
        dproj_ref[:, C_SK:C_SK + KV_W] = (krr * (dkn - kn * _seg_mean(dkn * kn, gm128))).astype(dproj_ref.dtype)
        dproj_ref[:, C_SV:C_SV + KV_W] = jnp.concatenate(dv_rows, axis=0).astype(dproj_ref.dtype)
        gqn = jnp.zeros((1, 128), F32)
        for h in range(2):
            dqg = _rope_bwd(jnp.concatenate([dq_blk[b][h] for b in range(nb)], axis=0), cv, s1v, s2v)
            gqn = gqn + _col_sum(dqg * qn_[h])
            dqn_ = dqg * gn_ref[:, G_Q:G_K]
            dproj_ref[:, C_SQ + 128 * h:C_SQ + 128 * (h + 1)] = (
                qr_[h] * (dqn_ - qn_[h] * _seg_mean(dqn_ * qn_[h], gm128))).astype(dproj_ref.dtype)
        gqn_ref[...] += gqn
        gkn_ref[...] += gkn

        u = proj_ref[:, C_LRUX:C_LRUX + LRU_W]
        xc, rg, ig, sq = (gates_ref[:, LRU_W * k:LRU_W * (k + 1)].astype(F32) for k in range(4))
        a = a_ref[...]
        sp = _softplus(-lam_ref[...])
        hext_ref[0:8, :] = jnp.where(first_tile, 0.0, yah_ref[...])
        hext_ref[8:8 + tm, :] = ya_ref[...]
        hprev = hext_ref[pl.ds(7, tm), :]
        aext_ref[0:tm, :] = a
        an_scr[...] = aext_ref[pl.ds(1, tm), :]
        dh_scr[...] = dya
        dh_scr[tm - 1:tm, :] = dh_scr[tm - 1:tm, :] + gcar_ref[0:1, :]
        row8 = lax.broadcasted_iota(jnp.int32, (8, LRU_W), 0)

        def scan_step(gi, carry):
            r0 = pl.multiple_of((tm // 8 - 1 - gi) * 8, 8)
            av = an_scr[pl.ds(r0, 8), :]
            bv = dh_scr[pl.ds(r0, 8), :]
            for d in (1, 2, 4):
                a_sh = jnp.where(row8 < 8 - d, pltpu.roll(av, 8 - d, 0), 1.0)
                b_sh = jnp.where(row8 < 8 - d, pltpu.roll(bv, 8 - d, 0), 0.0)
                bv = bv + av * b_sh
                av = av * a_sh
            gv = bv + av * carry
            g_scr[pl.ds(r0, 8), :] = gv
            return gv[0:1, :]

        g0 = lax.fori_loop(0, tm // 8, scan_step, jnp.zeros((1, LRU_W), F32), unroll=True)
        gcar_ref[0:1, :] = a[0:1, :] * g0
        gv = g_scr[...]
        da = gv * hprev
        dig = gv * sq * xc
        dxc = gv * sq * ig
        dla = da * a - gv * (ig * xc) * ((a * a) / sq)
        drg = dla * ((-LRU_C) * sp)
        glam_ref[...] += _col_sum(dla * rg)
        dpr = drg * rg * (1.0 - rg)
        dpi = dig * ig * (1.0 - ig)
        gbrg_ref[...] += _col_sum(dpr)
        gbig_ref[...] += _col_sum(dpi)
        dpre0 = jnp.concatenate([dpr[:, :256], dpi[:, :256]], axis=1).astype(_MXU)
        dpre1 = jnp.concatenate([dpr[:, 256:], dpi[:, 256:]], axis=1).astype(_MXU)
        gwg_ref[0] += _mm_tn(xc[:, :256], dpre0)
        gwg_ref[1] += _mm_tn(xc[:, 256:], dpre1)
        dxc = dxc + jnp.concatenate([_mm_nt(dpre0, wg_ref[0]), _mm_nt(dpre1, wg_ref[1])], axis=1)
        gcb_ref[...] += _col_sum(dxc)
        dxc_ext[0:tm, :] = dxc
        du = jnp.zeros((tm, LRU_W), F32)
        for k in range(CONV_K):
            later = dxc_ext[pl.ds(3 - k, tm), :]
            gcw_ref[k:k + 1, :] += _col_sum(later * u)
            du = du + cw_ref[k:k + 1, :] * later
        dxc_ext[tm:tm + 8, :] = dxc[0:8, :]
        dproj_ref[:, C_LRUX:C_LRUX + LRU_W] = du.astype(dproj_ref.dtype)

        dxn = _mm(dproj_ref[...], win_ref[...])
        rx = lax.rsqrt(_row_mean(xv * xv) + EPS)
        xh = xv * rx
        gng_ref[...] += _col_sum(dxn * xh)
        dxh = dxn * ng_ref[...]
        gx_ref[...] = dov.astype(F32) + rx * (dxh - xh * _row_mean(dxh * xh))

        @pl.when(i == nt - 1)
        def _():
            glam_ref[...] = glam_ref[...] * (LRU_C * _sigmoid(-lam_ref[...]))
            for r in (gqn_ref, gkn_ref, gxqn_ref):
                r[...] = _fold_heads(r[...])

    def rows(ncol, arr_cols_block=0):
        return pl.BlockSpec((tm, ncol), lambda i: (nt - 1 - i, arr_cols_block))

    def halo(nrow, ncol, colblk=0):
        per = tm // nrow
        return pl.BlockSpec((nrow, ncol), lambda i: (jnp.maximum((nt - 1 - i) * per - 1, 0), colblk))

    in_specs = [rows(D_MODEL), rows(D_MODEL), rows(D_IN), rows(LRU_W), rows(SWA_W), rows(XATT_W),
                rows(4 * 2 * BLOCK), rows(4 * MEM_LEN), rows(128), rows(4 * LRU_W), rows(LRU_W),
                rows(128), rows(128), rows(128),
                halo(8, LRU_W), halo(BLOCK, 2 * KV_W, C_SK // (2 * KV_W)),
                halo(BLOCK, 128), halo(BLOCK, 128), halo(BLOCK, 128),
                _const_spec((1, D_MODEL)), _const_spec((D_IN, D_MODEL), True), _const_spec((CONV_K, LRU_W)),
                _const_spec((2, 256, 512), True), _const_spec((1, LRU_W)), _const_spec((1, GAINS_W)),
                _const_spec((4 * MEM_LEN, XATT_W), True), _const_spec((4 * MEM_LEN, XATT_W), True),
                _const_spec((1, D_MODEL)), _const_spec((D_MODEL, D_MODEL), True)]
    small = [(2, 256, 512), (XATT_W, 4 * MEM_LEN), (XATT_W, 4 * MEM_LEN), (1, D_MODEL), (1, D_MODEL), (1, LRU_W), (1, LRU_W),
             (1, LRU_W), (1, LRU_W), (CONV_K, LRU_W), (1, 128), (1, 128), (1, XATT_W), (1, 128)]
    out_shape = (jax.ShapeDtypeStruct((seq, D_MODEL), F32), jax.ShapeDtypeStruct((seq, D_IN), _MXU)) + tuple(
        jax.ShapeDtypeStruct(s, F32) for s in small)
    out_specs = (rows(D_MODEL), rows(D_IN)) + tuple(_const_spec(s) for s in small)
    scratch = [pltpu.VMEM((tm + 8, LRU_W), F32), pltpu.VMEM((tm + 8, LRU_W), F32),
               pltpu.VMEM((tm, LRU_W), F32), pltpu.VMEM((tm, LRU_W), F32), pltpu.VMEM((tm, LRU_W), F32),
               pltpu.VMEM((tm + 8, LRU_W), F32),
               pltpu.VMEM((8, LRU_W), F32), pltpu.VMEM((BLOCK, KV_W), F32), pltpu.VMEM((BLOCK, KV_W), F32)]
    return pl.pallas_call(
        body, name="layer_bwd", grid=(nt,), out_shape=out_shape, in_specs=in_specs, out_specs=out_specs,
        scratch_shapes=scratch,
        compiler_params=pltpu.CompilerParams(dimension_semantics=("arbitrary",), vmem_limit_bytes=VMEM_LIMIT),
    )(x, dout, proj, ya, yb, yc, pswa, pmem, psink, gates, a_all, rc, rs1, rs2, ya, proj, rc, rs1, rs2,
      ng, win_t, cw, wg, lam, gains, km, vm, og, wout)


def _reduce_protocol(big, sm, outs, osm, r1, r1s, wire, r2, r2s, wire2, ps, own, send, recv, lsem):
    nbig = len(big)
    x, y, c = lax.axis_index("x"), lax.axis_index("y"), lax.axis_index("c")
    sibling = (x, y, 1 - c)
    near, far, diag = _partners(x, y, c)
    me, near_id, far_id, diag_id = _chip_of(x, y), _chip_of(*near), _chip_of(*far), _chip_of(*diag)

    def copy(k, src, dst, to):
        return pltpu.make_async_remote_copy(src_ref=src, dst_ref=dst, send_sem=send.at[k], recv_sem=recv.at[k],
                                            device_id=to, device_id_type=MESH)

    def sent(stage, a):
        if a == nbig:
            src, dst, to = ((sm.at[1 - c], r1s, sibling), (r1s, r2s.at[0], (*near, c)), (ps, r2s.at[1], (*far, c)),
                            (osm.at[c], osm.at[c], sibling))[stage]
            return [copy(5 * nbig + stage, src, dst, to)]
        if stage == 0:
            return [copy(5 * a, big[a].at[:, 1 - c], r1[a], sibling)]
        if stage == 1:
            return [copy(5 * a + 1, wire[a].at[near_id], r2[a].at[0], (*near, c)),
                    copy(5 * a + 2, wire[a].at[diag_id], r2[a].at[1], (*near, c))]
        if stage == 2:
            return [copy(5 * a + 3, wire2[a], r2[a].at[2], (*far, c))]
        return [copy(5 * a + 4, outs[a].at[c], outs[a].at[c], sibling)]

    arrays = range(nbig + (sm is not None))

    def start(stage, a):
        for cp in sent(stage, a):
            cp.start()

    def arrived(k, ref):
        copy(k, ref, ref, sibling).wait_recv()

    def loads():
        return [pltpu.make_async_copy(big[a].at[:, c], own[a], lsem.at[a]) for a in range(nbig)]

    def stage0():
        for a in arrays:
            start(0, a)
        for cp in loads():
            cp.start()

    def stage1():
        for a in range(nbig):
            loads()[a].wait()
            arrived(5 * a, r1[a])
            for k in range(N_CHIPS):
                r1[a][k] = own[a][k] + r1[a][k]
                wire[a][k] = r1[a][k].astype(wire[a].dtype)
            start(1, a)
        if sm is not None:
            arrived(5 * nbig, r1s)
            r1s[...] = sm[c] + r1s[...]
            start(1, nbig)

    def stage2():
        for a in range(nbig):
            arrived(5 * a + 1, r2[a].at[0])
            arrived(5 * a + 2, r2[a].at[1])
            r1[a][me] = r1[a][me] + r2[a][0].astype(F32)
            wire2[a][...] = (r1[a][far_id] + r2[a][1].astype(F32)).astype(wire2[a].dtype)
            start(2, a)
        if sm is not None:
            arrived(5 * nbig + 1, r2s.at[0])
            ps[...] = r1s[...] + r2s[0]
            start(2, nbig)

    def stage3():
        for a in range(nbig):
            arrived(5 * a + 3, r2[a].at[2])
            outs[a][c] = r1[a][me] + r2[a][2].astype(F32)
            start(3, a)
        if sm is not None:
            arrived(5 * nbig + 2, r2s.at[1])
            osm[c] = ps[...] + r2s[1]
            start(3, nbig)

    def stage4():
        for a in range(nbig):
            arrived(5 * a + 4, outs[a].at[1 - c])
        if sm is not None:
            arrived(5 * nbig + 3, osm.at[1 - c])
        for stage in range(4):
            for a in arrays:
                for cp in sent(stage, a):
                    cp.wait_send()

    return [stage0, stage1, stage2, stage3, stage4]


def _reduce_buffers(bigs, g_small):
    half = [b.shape[2:] for b in bigs]
    sm_half = None if g_small is None else g_small.shape[1:]
    out_shape = [jax.ShapeDtypeStruct((2,) + h, F32) for h in half]
    small = lambda lead: [] if g_small is None else [pltpu.VMEM(lead + sm_half, F32)]
    if g_small is not None:
        out_shape.append(jax.ShapeDtypeStruct(g_small.shape, F32))
    n_sem = 5 * len(bigs) + 4
    scratch = ([pltpu.VMEM((N_CHIPS,) + h, F32) for h in half] + small(())
               + [pltpu.VMEM((N_CHIPS,) + h, _WIRE) for h in half]
               + [pltpu.VMEM((3,) + h, _WIRE) for h in half] + small((2,))
               + [pltpu.VMEM(h, _WIRE) for h in half] + small(())
               + [pltpu.VMEM((N_CHIPS,) + h, F32) for h in half]
               + [pltpu.SemaphoreType.DMA((n_sem,)), pltpu.SemaphoreType.DMA((n_sem,)),
                  pltpu.SemaphoreType.DMA((len(bigs),))])
    return out_shape, scratch


def _split_reduce_refs(refs, nbig, has_small):
    it = iter(refs)
    take = lambda n: [next(it) for _ in range(n)]
    one = lambda: next(it) if has_small else None
    big, sm = take(nbig), one()
    outs, osm = take(nbig), one()
    r1, r1s, wire, r2, r2s, wire2, ps, own = take(nbig), one(), take(nbig), take(nbig), one(), take(nbig), one(), take(nbig)
    send, recv, lsem = take(3)
    return big, sm, outs, osm, r1, r1s, wire, r2, r2s, wire2, ps, own, send, recv, lsem


def _hosted_reduce_shapes(bigs, g_small):
    red_shape, scratch = _reduce_buffers(bigs, g_small)
    nres = len(red_shape)
    return red_shape, [pltpu.VMEM(r.shape, r.dtype) for r in red_shape] + scratch + [pltpu.SemaphoreType.DMA((nres,))]


def _hosted_reduce(step, n_steps, closing, stage_at, operands, results, scratch, has_small):
    nres = len(results)
    sums, rest, fsem = scratch[:nres], scratch[nres:-1], scratch[-1]
    refs = tuple(operands) + tuple(sums) + tuple(rest)

    def to_results():
        out = [pltpu.make_async_copy(sums[k], results[k], fsem.at[k]) for k in range(nres)]
        for cp in out:
            cp.start()
        for cp in out:
            cp.wait()

    stages = _reduce_protocol(*_split_reduce_refs(refs, nres - has_small, has_small))

    def last_stage():
        stages[-1]()
        to_results()

    for at, stage in zip(stage_at, stages[:-1] + [last_stage]):
        if closing == (at == n_steps):
            pl.when(step == min(at, n_steps - 1))(stage)


def reduce_grads(big, name, parts):
    chips, halves, rows_, cols = big.shape
    sub = jax.ShapeDtypeStruct((chips, halves, rows_ // parts, cols), big.dtype)

    def body(b_ref, o_ref, *scratch):
        refs = [b_ref.at[:, :, s] for s in range(parts)] + [o_ref.at[:, s] for s in range(parts)] + list(scratch)
        for stage in _reduce_protocol(*_split_reduce_refs(refs, parts, False)):
            stage()

    _, scratch = _reduce_buffers([sub] * parts, None)
    return pl.pallas_call(
        body, name=name, out_shape=jax.ShapeDtypeStruct((halves, parts, rows_ // parts, cols), F32),
        in_specs=[pl.BlockSpec(memory_space=pl.ANY)], out_specs=pl.BlockSpec(memory_space=pltpu.VMEM),
        scratch_shapes=scratch, compiler_params=pltpu.CompilerParams(vmem_limit_bytes=VMEM_LIMIT),
    )(big.reshape(chips, halves, parts, rows_ // parts, cols))


def adamw(items, g_pack, ws, ms, vs):
    plan, total = [], 0
    for w, _, _, _ in items:
        rows_, cols = w.shape
        tr = max(t for t in range(8, rows_ + 1, 8) if rows_ % t == 0 and t * cols * 4 <= ADAM_BLOCK_BYTES)
        plan.append((total, rows_ // tr, tr, cols))
        total += rows_ // tr
    nin, n = 4 * len(items), len(ws)

    def body(*refs):
        i = pl.program_id(0)
        small_in = refs[nin:nin + 1 + 3 * n]
        outs = refs[nin + 1 + 3 * n:]

        @pl.when(i == 0)
        def _():
            _adamw_small(small_in[0], *(small_in[1 + k * n:1 + (k + 1) * n] for k in range(3)),
                         *(outs[nin + k * n:nin + (k + 1) * n] for k in range(4)), outs[-1])

        for k, (first, steps, _, _) in enumerate(plan):
            w_ref, g_ref, m_ref, v_ref = refs[4 * k:4 * k + 4]
            go_ref, d_ref, nm_ref, nv_ref = outs[4 * k:4 * k + 4]

            @pl.when((i >= first) & (i < first + steps))
            def _():
                gv = g_ref[...]
                go_ref[...] = gv
                d_ref[...], nm_ref[...], nv_ref[...] = _adam_update(w_ref[...], gv, m_ref[...], v_ref[...])

    specs, shapes = [], []
    for (first, steps, tr, cols), (w, _, _, _) in zip(plan, items):
        spec = pl.BlockSpec((tr, cols), lambda i, first=first, steps=steps: (jnp.clip(i - first, 0, steps - 1), 0))
        specs += [spec] * 4
        shapes += [jax.ShapeDtypeStruct(w.shape, F32)] * 4
    vm = pl.BlockSpec(memory_space=pltpu.VMEM)
    like = [jax.ShapeDtypeStruct(w.shape, F32) for w in ws]
    res = pl.pallas_call(
        body, name="adamw", grid=(total,), out_shape=(*shapes, *like * 4, jax.ShapeDtypeStruct((1, 1), F32)),
        in_specs=specs + [vm] * (1 + 3 * n), out_specs=(*specs, *[vm] * (4 * n + 1)),
        compiler_params=pltpu.CompilerParams(dimension_semantics=("arbitrary",)),
    )(*[a for item in items for a in item], g_pack, *ws, *ms, *vs)
    return [res[4 * k:4 * k + 4] for k in range(len(items))], res[nin:]


def _adam_update(w, g, m, v):
    nm = ADAM_B1 * m + (1.0 - ADAM_B1) * g
    nv = ADAM_B2 * v + (1.0 - ADAM_B2) * (g * g)
    m_hat = nm / (1.0 - ADAM_B1 ** ADAM_STEP)
    v_hat = nv / (1.0 - ADAM_B2 ** ADAM_STEP)
    return (-ADAM_LR) * (m_hat / (jnp.sqrt(v_hat) + ADAM_EPS) + ADAM_WD * w), nm, nv


def _adamw_small(pk, w_refs, m_refs, v_refs, g_out, d_out, nm_out, nv_out, loss_ref):
    nvec = len(SMALL_VECTORS)
    loss_ref[...] = pk[LOSS_ROW:LOSS_ROW + 1, 0:1]
    chip = 2 * lax.axis_index("x") + lax.axis_index("y")
    for k, (name, row, width) in enumerate(SMALL_VECTORS):
        if name == "conv_w":
            g = jnp.concatenate([pk[pl.ds(row + 4 * t + chip, 1), :] for t in range(CONV_K)], axis=0)[None]
        elif width >= 128:
            g = jnp.concatenate([pk[row + r:row + r + 1, :] for r in range(width // 128)], axis=1)
        else:
            g = pk[row:row + 1, 0:width]
        g_out[k][...] = g
        d_out[k][...], nm_out[k][...], nv_out[k][...] = _adam_update(w_refs[k][...], g, m_refs[k][...], v_refs[k][...])
    for k in range(nvec, nvec + len(SMALL_MATRICES)):
        for b in range(LRU_BLOCKS):
            rows_ = pk[GATES_ROW + HEAD * b:GATES_ROW + HEAD * (b + 1), :]
            g = (pltpu.roll(rows_, HEAD, axis=1) if k > nvec else rows_)[:, 0:HEAD]
            g_out[k][0, b] = g
            d_out[k][0, b], nm_out[k][0, b], nv_out[k][0, b] = _adam_update(
                w_refs[k][0, b], g, m_refs[k][0, b], v_refs[k][0, b])


SMALL_VECTORS = (("norm_g", 512, 1024), ("mem_norm_g", 520, 1024), ("conv_w", 528, 512), ("conv_b", 544, 512),
                 ("b_rg", 548, 512), ("b_ig", 552, 512), ("lru_lambda", 556, 512), ("q_norm_g", 560, 64),
                 ("k_norm_g", 561, 64), ("sinks", 562, 4), ("xq_norm_g", 563, 64), ("xk_norm_g", 564, 64),
                 ("out_norm_g", 565, 1024))
LOSS_ROW = 573
SMALL_MATRICES = ("w_rg", "w_ig")
GATES_ROW = 0
SMALL_ROWS = 640


def _rope_tables(seq):
    pos = np.arange(seq, dtype=np.float32)
    inv_freq = (np.float32(ROPE_THETA) ** (-(np.arange(0, ROPE_DIM, 2, dtype=np.float32) / np.float32(ROPE_DIM)))
                ).astype(np.float32)
    ang = (pos[:, None] * inv_freq[None, :]).astype(np.float32)
    cos, sin = np.cos(ang).astype(np.float32), np.sin(ang).astype(np.float32)
    z = lambda n: np.zeros((seq, n), np.float32)
    c64 = np.concatenate([cos, cos, np.ones((seq, HEAD - ROPE_DIM), np.float32)], axis=1)
    s1_64 = np.concatenate([-sin, z(HEAD - 8)], axis=1)
    s2_64 = np.concatenate([z(8), sin, z(HEAD - ROPE_DIM)], axis=1)
    return tuple(jnp.asarray(np.concatenate([t, t], axis=1)) for t in (c64, s1_64, s2_64))


def kernel(x, mem, norm_g, mem_norm_g, w_in, conv_w, conv_b, w_rg, b_rg, w_ig, b_ig, lru_lambda, q_norm_g, k_norm_g, sinks, w_mem_kv, xq_norm_g, xk_norm_g, out_norm_g, w_out, loss_target, m_norm_g, m_mem_norm_g, m_w_in, m_conv_w, m_conv_b, m_w_rg, m_b_rg, m_w_ig, m_b_ig, m_lru_lambda, m_q_norm_g, m_k_norm_g, m_sinks, m_w_mem_kv, m_xq_norm_g, m_xk_norm_g, m_out_norm_g, m_w_out, v_norm_g, v_mem_norm_g, v_w_in, v_conv_w, v_conv_b, v_w_rg, v_b_rg, v_w_ig, v_b_ig, v_lru_lambda, v_q_norm_g, v_k_norm_g, v_sinks, v_w_mem_kv, v_xq_norm_g, v_xk_norm_g, v_out_norm_g, v_w_out):
    seq = x.shape[1]
    xs, tgt, mems = x[0], loss_target[0], mem[0]

    win_t, wout, wkv, cw, wg, gains, km, vm = gather_weights(
        w_in[0].T, w_out[0], w_mem_kv[0], conv_w, w_rg, w_ig, (q_norm_g, k_norm_g, xq_norm_g, xk_norm_g), mems,
        mem_norm_g)
    rc, rs1, rs2 = _rope_tables(seq)
    proj, ya, yb, yc, ycat, xn, dout, pswa, pmem, psink, gates, a_all, loss8 = layer_fwd(
        xs, tgt, rc, rs1, rs2, norm_g, win_t, cw, conv_b, wg, b_rg, b_ig, lru_lambda, gains, sinks, km, vm,
        out_norm_g, wout)
    (gx, dproj, g_wg, dkm, dvm, g_ng, g_og, g_cb, g_brg, g_big, g_lam, g_cw, g_qn, g_kn, g_xqn, g_sink) = layer_bwd(
        xs, dout, proj, ya, yb, yc, pswa, pmem, psink, gates, a_all, rc, rs1, rs2, norm_g, win_t, cw, wg, lru_lambda,
        gains, km, vm, out_norm_g, wout)
    g_win_t, r_out, r_kv, r_small = weight_grads(
        ycat, dout, dproj, xn, (mems, mem_norm_g, wkv, gains, dkm, dvm, g_wg, loss8), dict(
            norm_g=g_ng, conv_w=g_cw, conv_b=g_cb, b_rg=g_brg, b_ig=g_big, lru_lambda=g_lam, q_norm_g=g_qn,
            k_norm_g=g_kn, sinks=g_sink, xq_norm_g=g_xqn, out_norm_g=g_og), (0, 1, 4, 7, 8), (4, 5, 10, 12, 13))
    r_in = reduce_grads(g_win_t.reshape(N_CHIPS, 2, D_IN // 8, D_MODEL), "reduce_w_in", 6)

    r_small = r_small.reshape(SMALL_ROWS, 128)
    grads = {}
    weights = dict(norm_g=norm_g, mem_norm_g=mem_norm_g, w_in=w_in, conv_w=conv_w, conv_b=conv_b, w_rg=w_rg, b_rg=b_rg,
                   w_ig=w_ig, b_ig=b_ig, lru_lambda=lru_lambda, q_norm_g=q_norm_g, k_norm_g=k_norm_g, sinks=sinks,
                   w_mem_kv=w_mem_kv, xq_norm_g=xq_norm_g, xk_norm_g=xk_norm_g, out_norm_g=out_norm_g, w_out=w_out)
    ms = dict(norm_g=m_norm_g, mem_norm_g=m_mem_norm_g, w_in=m_w_in, conv_w=m_conv_w, conv_b=m_conv_b, w_rg=m_w_rg,
              b_rg=m_b_rg, w_ig=m_w_ig, b_ig=m_b_ig, lru_lambda=m_lru_lambda, q_norm_g=m_q_norm_g, k_norm_g=m_k_norm_g,
              sinks=m_sinks, w_mem_kv=m_w_mem_kv, xq_norm_g=m_xq_norm_g, xk_norm_g=m_xk_norm_g,
              out_norm_g=m_out_norm_g, w_out=m_w_out)
    vs = dict(norm_g=v_norm_g, mem_norm_g=v_mem_norm_g, w_in=v_w_in, conv_w=v_conv_w, conv_b=v_conv_b, w_rg=v_w_rg,
              b_rg=v_b_rg, w_ig=v_w_ig, b_ig=v_b_ig, lru_lambda=v_lru_lambda, q_norm_g=v_q_norm_g, k_norm_g=v_k_norm_g,
              sinks=v_sinks, w_mem_kv=v_w_mem_kv, xq_norm_g=v_xq_norm_g, xk_norm_g=v_xk_norm_g,
              out_norm_g=v_out_norm_g, w_out=v_w_out)

    delta, new_m, new_v = {}, {}, {}
    small_names = [n for n, _, _ in SMALL_VECTORS] + list(SMALL_MATRICES)
    (res_in, res_out, res_kv), res = adamw(
        [(w_in[0].T, r_in.reshape(D_IN // 4, D_MODEL), m_w_in[0].T, v_w_in[0].T),
         (w_out[0], r_out.reshape(D_MODEL // 4, D_MODEL), m_w_out[0], v_w_out[0]),
         (w_mem_kv[0], r_kv.reshape(D_MODEL // 4, 2 * XATT_W), m_w_mem_kv[0], v_w_mem_kv[0])],
        r_small, [weights[n] for n in small_names], [ms[n] for n in small_names], [vs[n] for n in small_names])
    grads["w_in"], delta["w_in"], new_m["w_in"], new_v["w_in"] = (r.T[None] for r in res_in)
    grads["w_out"], delta["w_out"], new_m["w_out"], new_v["w_out"] = (r[None] for r in res_out)
    grads["w_mem_kv"], delta["w_mem_kv"], new_m["w_mem_kv"], new_v["w_mem_kv"] = (r[None] for r in res_kv)
    nall = len(small_names)
    for k, into in enumerate((grads, delta, new_m, new_v)):
        into.update(zip(small_names, res[k * nall:(k + 1) * nall]))
    loss = res[-1].reshape(())

    order = ("norm_g", "mem_norm_g", "w_in", "conv_w", "conv_b", "w_rg", "b_rg", "w_ig", "b_ig", "lru_lambda",
             "q_norm_g", "k_norm_g", "sinks", "w_mem_kv", "xq_norm_g", "xk_norm_g", "out_norm_g", "w_out")
    return (loss, gx[None], *[grads[n] for n in order], *[delta[n] for n in order], *[new_m[n] for n in order],
            *[new_v[n] for n in order])
```

```python
import jax
import jax.numpy as jnp
import numpy as np
from jax import lax
from jax.experimental import pallas as pl
from jax.experimental.pallas import tpu as pltpu

F32 = jnp.float32
_MXU = jnp.bfloat16
_WIRE = jnp.bfloat16

D_MODEL = 1024
MEM_LEN = 256
HEAD = 64
LRU_W = 512
LRU_BLOCKS = 8
CONV_K = 4
LRU_C = 8.0
SWA_W = 256
KV_W = 128
XATT_W = 256
BLOCK = 128
D_IN = 2304
ROPE_THETA = 500000.0
ROPE_DIM = 16
EPS = 1e-6
NEG_INF = -1e30
C_LRUX, C_LRUG, C_SQ, C_SK, C_SV, C_SWAG, C_XQ, C_XG = 0, 512, 1024, 1280, 1408, 1536, 1792, 2048
G_Q, G_K, G_XQ, G_XK, GAINS_W = 0, 128, 256, 512, 768

ADAM_LR, ADAM_B1, ADAM_B2, ADAM_EPS, ADAM_WD, ADAM_STEP = 0.001, 0.9, 0.999, 1e-08, 0.01, 10

N_CHIPS = 4
ROW_TILE = 256
VMEM_LIMIT = 56 * 1024 * 1024
ADAM_BLOCK_BYTES = 640 * 1024
MESH = pl.DeviceIdType.MESH


def _mm(a, b):
    return jnp.dot(a.astype(_MXU), b.astype(_MXU), preferred_element_type=F32)


def _mm_nt(a, b):
    return lax.dot_general(a.astype(_MXU), b.astype(_MXU), (((1,), (1,)), ((), ())), preferred_element_type=F32)


def _mm_tn(a, b):
    return lax.dot_general(a.astype(_MXU), b.astype(_MXU), (((0,), (0,)), ((), ())), preferred_element_type=F32)


def _group_matrix(width):
    r = lax.shift_right_logical(lax.broadcasted_iota(jnp.int32, (width, width), 0), 6)
    c = lax.shift_right_logical(lax.broadcasted_iota(jnp.int32, (width, width), 1), 6)
    return (r == c).astype(_MXU)


def _seg_mean(x, gm):
    return jnp.dot(x.astype(_MXU), gm, preferred_element_type=F32) * (1.0 / HEAD)


def _row_mean(x):
    return jnp.mean(x, axis=-1, keepdims=True)


def _col_sum(x):
    return jnp.sum(x, axis=0, keepdims=True)


def _sigmoid(x):
    return jax.nn.sigmoid(x)


def _softplus(z):
    e = jnp.exp(-jnp.abs(z))
    u = 1.0 + e
    log1p_e = jnp.where(u == 1.0, e, jnp.log(u) * (e / (u - 1.0)))
    return jnp.maximum(z, 0.0) + log1p_e


def _rope(t, c, s1, s2):
    return t * c + pltpu.roll(t, 120, 1) * s1 + pltpu.roll(t, 8, 1) * s2


def _rope_bwd(d, c, s1, s2):
    return d * c + pltpu.roll(d * s1, 8, 1) + pltpu.roll(d * s2, 120, 1)


def _fold_heads(v):
    out = v
    for k in range(1, v.shape[1] // HEAD):
        out = out + pltpu.roll(v, HEAD * k, 1)
    return out


def _lane_mask(width, lo, hi):
    lane = lax.broadcasted_iota(jnp.int32, (1, width), 1)
    return ((lane >= lo) & (lane < hi)).astype(F32)


def _swa_mask(first_block):
    qi = lax.broadcasted_iota(jnp.int32, (BLOCK, 2 * BLOCK), 0)
    kj = lax.broadcasted_iota(jnp.int32, (BLOCK, 2 * BLOCK), 1)
    rel = qi + BLOCK - kj
    ok = (rel >= 0) & (rel < BLOCK)
    return ok & (jnp.logical_not(first_block) | (kj >= BLOCK))


def _place_kv(t, scale):
    lo = t * (_lane_mask(KV_W, 0, HEAD) * scale)
    hi = t * (_lane_mask(KV_W, HEAD, KV_W) * scale)
    return [a.astype(_MXU) for a in (lo, pltpu.roll(lo, HEAD, 1), pltpu.roll(hi, HEAD, 1), hi)]


def _unplace_kv(d):
    return (_lane_mask(KV_W, 0, HEAD) * (d[0] + pltpu.roll(d[1], HEAD, 1))
            + _lane_mask(KV_W, HEAD, KV_W) * (d[3] + pltpu.roll(d[2], HEAD, 1)))


def _swa_probs(qh, ka, mask, sink):
    s = _mm_nt(qh, ka)
    s = jnp.where(mask, s, NEG_INF)
    m = jnp.maximum(jnp.max(s, axis=-1, keepdims=True), sink)
    p = jnp.exp(s - m)
    esink = jnp.exp(sink - m)
    inv = 1.0 / (jnp.sum(p, axis=-1, keepdims=True) + esink)
    return p * inv, esink * inv


def _mem_probs(s_all):
    out = []
    for j in range(4):
        s = s_all[:, MEM_LEN * j:MEM_LEN * (j + 1)]
        p = jnp.exp(s - jnp.max(s, axis=-1, keepdims=True))
        out.append(p * (1.0 / jnp.sum(p, axis=-1, keepdims=True)))
    return out


def _head_rows(t, scale):
    return jnp.concatenate([t * (_lane_mask(XATT_W, HEAD * j, HEAD * (j + 1)) * scale) for j in range(4)], axis=0)


def _lru_gates(xc, wg_ref, brg, big, lam):
    p0 = _mm(xc[:, :256], wg_ref[0])
    p1 = _mm(xc[:, 256:], wg_ref[1])
    rg = _sigmoid(jnp.concatenate([p0[:, :256], p1[:, :256]], axis=1) + brg)
    ig = _sigmoid(jnp.concatenate([p0[:, 256:], p1[:, 256:]], axis=1) + big)
    sp = _softplus(-lam)
    la = (-LRU_C) * rg * sp
    a = jnp.exp(la)
    th = jnp.tanh(la)
    one_minus_a2 = (-2.0 * th) / (1.0 - th)
    return rg, ig, sp, a, jnp.sqrt(one_minus_a2)


def _const_spec(shape, single=False):
    zeros = (0,) * len(shape)
    if single:
        return pl.BlockSpec(shape, lambda i: zeros, pipeline_mode=pl.Buffered(1))
    return pl.BlockSpec(shape, lambda i: zeros)


def _chip_of(x, y):
    return 2 * x + y


def _partners(x, y, c):
    north = c == 1
    near = (jnp.where(north, 1 - x, x), jnp.where(north, y, 1 - y))
    far = (jnp.where(north, x, 1 - x), jnp.where(north, 1 - y, y))
    return near, far, (1 - x, 1 - y)


def gather_weights(win_t, wout, wkv, conv_w, w_rg, w_ig, head_gains, mem, mem_g):
    arrs = (win_t, wout, wkv)
    n = len(arrs)
    pieces = [(a, 0, arr.shape[0] // 2) for a, arr in enumerate(arrs)]
    npc = len(pieces)

    def body(a0, a1, a2, cw_in, wrg_ref, wig_ref, q_ref, k_ref, xq_ref, xk_ref, mem_ref, mg_ref,
             o0, o1, o2, cw_out, wg_ref, gn_ref, km_ref, vm_ref, s0, s1, s2, cw, ocw, send, recv, lsem):
        ins, outs = (s0, s1, s2), (o0, o1, o2)
        for src, dst in zip((a0, a1, a2), ins):
            dst[...] = src[...].astype(dst.dtype)
        cw[...] = jnp.zeros(cw.shape, F32)
        cw[0:CONV_K, :] = cw_in[0]
        x, y, c = lax.axis_index("x"), lax.axis_index("y"), lax.axis_index("c")
        sibling = (x, y, 1 - c)
        near, far, diag = _partners(x, y, c)
        chips = [near, far, diag]
        me = _chip_of(x, y)

        def landed(p, chip, half):
            a, off, rows_ = pieces[p]
            r = ins[a].shape[0]
            return outs[a].at[pl.ds(pl.multiple_of(chip * r + half * (r // 2) + off, 16), rows_)]

        def mine(p):
            a, off, rows_ = pieces[p]
            return ins[a].at[pl.ds(pl.multiple_of(c * (ins[a].shape[0] // 2) + off, 16), rows_)]

        def copy(k, src, dst, to):
            return pltpu.make_async_remote_copy(src_ref=src, dst_ref=dst, send_sem=send.at[k], recv_sem=recv.at[k],
                                                device_id=to, device_id_type=MESH)

        def cw_rows(chip):
            return ocw.at[pl.ds(pl.multiple_of(chip * 8, 8), 8)]

        locals_ = []
        for a in range(n):
            r = ins[a].shape[0]
            locals_.append(pltpu.make_async_copy(ins[a], outs[a].at[pl.ds(pl.multiple_of(me * r, 16), r)], lsem.at[a]))
        locals_.append(pltpu.make_async_copy(cw, cw_rows(me), lsem.at[n]))
        for cp in locals_:
            cp.start()

        sent = []
        for p in range(npc):
            for j in range(2):
                sent.append(copy(p * 6 + j, mine(p), landed(p, me, c), (*chips[j], c)))
        for j, chip in enumerate(chips):
            sent.append(copy(npc * 6 + j, cw, cw_rows(me), (*chip, c)))
        for cp in sent:
            cp.start()

        gn_ref[...] = jnp.concatenate([q_ref[...]] * 2 + [k_ref[...]] * 2 + [xq_ref[...]] * 4 + [xk_ref[...]] * 4,
                                      axis=1)
        zeros = lambda lanes: [jnp.zeros((HEAD, lanes), F32)] if lanes else []
        for h in range(2):
            for b in range(4):
                row = []
                for w_ref in (wrg_ref, wig_ref):
                    row += zeros(HEAD * b) + [w_ref[0, 4 * h + b]] + zeros(HEAD * (3 - b))
                wg_ref[h, HEAD * b:HEAD * (b + 1), :] = jnp.concatenate(row, axis=1).astype(wg_ref.dtype)

        for j in range(3):
            for p in range(npc):
                got = landed(p, _chip_of(*chips[j]), c)
                copy(p * 6 + j, got, got, sibling).wait_recv()
                if j == 0:
                    sent.append(copy(p * 6 + 2, got, got, (*far, c)))
                    sent[-1].start()
                sent.append(copy(p * 6 + 3 + j, got, got, sibling))
                sent[-1].start()
        for p in range(npc):
            for j in range(3):
                got = landed(p, _chip_of(*chips[(1, 0, 2)[j]]), 1 - c)
                copy(p * 6 + 3 + j, got, got, sibling).wait_recv()
        for j, chip in enumerate(chips):
            got = cw_rows(_chip_of(*chip))
            copy(npc * 6 + j, got, got, (*chip, c)).wait_recv()
        for cp in sent:
            cp.wait_send()
        for cp in locals_:
            cp.wait()
        for chip in range(N_CHIPS):
            cw_out[:, 128 * chip:128 * (chip + 1)] = ocw[8 * chip:8 * chip + CONV_K, :]

        mem_v = mem_ref[...]
        mn = mem_v * lax.rsqrt(_row_mean(mem_v * mem_v) + EPS) * mg_ref[...]
        mkv = _mm(mn, o2[...])
        kpre = mkv[:, :XATT_W]
        km = kpre * lax.rsqrt(_seg_mean(kpre * kpre, _group_matrix(XATT_W)) + EPS) * gn_ref[:, G_XK:GAINS_W]
        km_ref[...] = _head_rows(km, 0.125).astype(km_ref.dtype)
        vm_ref[...] = _head_rows(mkv[:, XATT_W:], 1.0).astype(vm_ref.dtype)

    vm = pl.BlockSpec(memory_space=pltpu.VMEM)
    hbm = pl.BlockSpec(memory_space=pl.ANY)
    head_rows = jax.ShapeDtypeStruct((4 * MEM_LEN, XATT_W), _MXU)
    out_shape = tuple(jax.ShapeDtypeStruct((N_CHIPS * a.shape[0],) + a.shape[1:], _MXU) for a in arrs) + (
        jax.ShapeDtypeStruct((CONV_K, LRU_W), F32), jax.ShapeDtypeStruct((2, 256, 512), _MXU),
        jax.ShapeDtypeStruct((1, GAINS_W), F32), head_rows, head_rows)
    n_rdma = npc * 6 + 3
    return pl.pallas_call(
        body, name="gather_weights", out_shape=out_shape,
        in_specs=[vm] * 12, out_specs=(hbm, hbm, vm, vm, vm, vm, vm, vm),
        scratch_shapes=[pltpu.VMEM(a.shape, _MXU) for a in arrs] + [
            pltpu.VMEM((8, 128), F32), pltpu.VMEM((N_CHIPS * 8, 128), F32),
            pltpu.SemaphoreType.DMA((n_rdma,)), pltpu.SemaphoreType.DMA((n_rdma,)), pltpu.SemaphoreType.DMA((n + 1,))],
        compiler_params=pltpu.CompilerParams(vmem_limit_bytes=VMEM_LIMIT),
    )(win_t, wout, wkv, conv_w, w_rg, w_ig, *head_gains, mem, mem_g)


def _mem_bwd_and_pack(mem_ref, g_ref, w_ref, gn_ref, dkm_ref, dvm_ref, gg_ref, loss_ref, vectors, gw_ref, pk_ref):
    first_row = {name: (row, width) for name, row, width in SMALL_VECTORS}
    half_rows = SMALL_ROWS // 2
    pk_ref[...] = jnp.zeros(pk_ref.shape, F32)

    def rows_at(at, n):
        assert at // half_rows == (at + n - 1) // half_rows
        return at // half_rows, slice(at % half_rows, at % half_rows + n), slice(None)

    def put(name, src):
        row, width = first_row[name]
        per_row = 1 if width < 128 else src.shape[1] // 128
        for t in range(src.shape[0]):
            for r in range(per_row):
                pk_ref[rows_at(row + per_row * t + r, 1)] = src[t:t + 1, 128 * r:128 * (r + 1)]

    for name, ref in vectors.items():
        put(name, ref)
    pk_ref[rows_at(LOSS_ROW, 1)] = loss_ref[0:1, :]
    upper = lax.broadcasted_iota(jnp.int32, (HEAD, 128), 1) >= HEAD
    for h in range(2):
        for b in range(4):
            rg = gg_ref[h, HEAD * b:HEAD * (b + 1), 128 * (b // 2):128 * (b // 2 + 1)]
            ig = gg_ref[h, HEAD * b:HEAD * (b + 1), 256 + 128 * (b // 2):256 + 128 * (b // 2 + 1)]
            if b % 2:
                rg = pltpu.roll(rg, HEAD, axis=1)
            else:
                ig = pltpu.roll(ig, HEAD, axis=1)
            pk_ref[rows_at(GATES_ROW + HEAD * (4 * h + b), HEAD)] = jnp.where(upper, ig, rg)

    mem_v = mem_ref[...]
    mh = mem_v * lax.rsqrt(_row_mean(mem_v * mem_v) + EPS)
    mn = mh * g_ref[...]
    mkv = _mm(mn, w_ref[...])
    kpre = mkv[:, :XATT_W]
    gm = _group_matrix(XATT_W)
    rk = lax.rsqrt(_seg_mean(kpre * kpre, gm) + EPS)
    kn = kpre * rk
    dk = jnp.zeros((MEM_LEN, XATT_W), F32)
    dv = jnp.zeros((MEM_LEN, XATT_W), F32)
    for j in range(4):
        mj = _lane_mask(XATT_W, HEAD * j, HEAD * (j + 1))
        dk = dk + dkm_ref[:, MEM_LEN * j:MEM_LEN * (j + 1)].T * (mj * 0.125)
        dv = dv + dvm_ref[:, MEM_LEN * j:MEM_LEN * (j + 1)].T * mj
    put("xk_norm_g", _fold_heads(_col_sum(dk * kn)))
    dkn = dk * gn_ref[:, G_XK:GAINS_W]
    dkpre = rk * (dkn - kn * _seg_mean(dkn * kn, gm))
    dmkv = jnp.concatenate([dkpre, dv], axis=1)
    gw_ref[...] = _mm_tn(mn, dmkv).reshape(gw_ref.shape)
    dmn = _mm_nt(dmkv, w_ref[...])
    put("mem_norm_g", _col_sum(dmn * mh))


def layer_fwd(x, tgt, rc, rs1, rs2, ng, win_t, cw, cb, wg, brg, big, lam, gains, sinks, km, vm, og, wout):
    seq = x.shape[0]
    tm = min(ROW_TILE, seq)
    nt = seq // tm
    nb = tm // BLOCK

    def body(x_ref, t_ref, c_ref, s1_ref, s2_ref, ng_ref, win_ref, cw_ref, cb_ref, wg_ref, brg_ref, big_ref, lam_ref,
             gn_ref, sink_ref, km_ref, vm_ref, og_ref, wout_ref,
             proj_ref, ya_ref, yb_ref, yc_ref, ycat_ref, xn_ref, dout_ref, pswa_ref, pmem_ref, psink_ref, gates_ref,
             a_ref, loss_ref,
             ext_ref, b_scr, hc_ref, kp_ref, vp_ref, lacc_ref):
        i = pl.program_id(0)

        @pl.when(i == 0)
        def _():
            ext_ref[0:8, :] = jnp.zeros((8, LRU_W), F32)
            hc_ref[...] = jnp.zeros_like(hc_ref)
            kp_ref[...] = jnp.zeros_like(kp_ref)
            vp_ref[...] = jnp.zeros_like(vp_ref)
            lacc_ref[...] = jnp.zeros_like(lacc_ref)

        xv = x_ref[...]
        xn = (xv * lax.rsqrt(_row_mean(xv * xv) + EPS) * ng_ref[...]).astype(_MXU)
        xn_ref[...] = xn.astype(xn_ref.dtype)
        proj_ref[...] = _mm_nt(xn, win_ref[...])

        u = proj_ref[:, C_LRUX:C_LRUX + LRU_W]
        ext_ref[8:8 + tm, :] = u
        xc = cb_ref[...]
        for k in range(CONV_K):
            xc = xc + cw_ref[k:k + 1, :] * ext_ref[pl.ds(5 + k, tm), :]
        ext_ref[0:8, :] = u[tm - 8:tm, :]
        rg, ig, sp, a, sq = _lru_gates(xc, wg_ref, brg_ref[...], big_ref[...], lam_ref[...])
        for k, t in enumerate((xc, rg, ig, sq)):
            gates_ref[:, LRU_W * k:LRU_W * (k + 1)] = t.astype(gates_ref.dtype)
        a_ref[...] = a
        b_scr[...] = sq * (ig * xc)
        row8 = lax.broadcasted_iota(jnp.int32, (8, LRU_W), 0)

        def scan_step(g, carry):
            r0 = pl.multiple_of(g * 8, 8)
            av = a_ref[pl.ds(r0, 8), :]
            bv = b_scr[pl.ds(r0, 8), :]
            for d in (1, 2, 4):
                a_sh = jnp.where(row8 >= d, pltpu.roll(av, d, 0), 1.0)
                b_sh = jnp.where(row8 >= d, pltpu.roll(bv, d, 0), 0.0)
                bv = bv + av * b_sh
                av = av * a_sh
            hv = bv + av * carry
            ya_ref[pl.ds(r0, 8), :] = hv
            return hv[7:8, :]

        hc_ref[0:1, :] = lax.fori_loop(0, tm // 8, scan_step, hc_ref[0:1, :], unroll=True)

        gm128 = _group_matrix(KV_W)
        cv, s1v, s2v = c_ref[...], s1_ref[...], s2_ref[...]

        def head_norm_rope(t, g):
            n = t * lax.rsqrt(_seg_mean(t * t, gm128) + EPS)
            return _rope(n * g, cv, s1v, s2v)

        qs_ = (head_norm_rope(proj_ref[:, C_SQ:C_SQ + 128], gn_ref[:, G_Q:G_K]).astype(_MXU),
               head_norm_rope(proj_ref[:, C_SQ + 128:C_SQ + 256], gn_ref[:, G_Q:G_K]).astype(_MXU))
        kr = head_norm_rope(proj_ref[:, C_SK:C_SK + KV_W], gn_ref[:, G_K:G_XQ])
        sv = proj_ref[:, C_SV:C_SV + KV_W]
        ka = _place_kv(jnp.concatenate([kp_ref[...], kr], axis=0), 0.125)
        va = _place_kv(jnp.concatenate([vp_ref[...], sv], axis=0), 1.0)
        kp_ref[...] = kr[tm - BLOCK:tm, :]
        vp_ref[...] = sv[tm - BLOCK:tm, :]
        lane128 = lax.broadcasted_iota(jnp.int32, (1, 128), 1)
        for b in range(nb):
            mask = _swa_mask((i == 0) & (b == 0)) if b == 0 else _swa_mask(False)
            band = slice(BLOCK * b, BLOCK * b + 2 * BLOCK)
            blk = slice(BLOCK * b, BLOCK * (b + 1))
            psink = jnp.zeros((BLOCK, 128), F32)
            for j in range(4):
                p, pk = _swa_probs(qs_[j // 2][blk], ka[j][band], mask, sink_ref[0, j])
                pswa_ref[blk, 2 * BLOCK * j:2 * BLOCK * (j + 1)] = p.astype(pswa_ref.dtype)
                psink = jnp.where(lane128 == j, pk, psink)
            psink_ref[blk, :] = psink
            for h in range(2):
                yb_ref[blk, KV_W * h:KV_W * (h + 1)] = _mm(
                    pswa_ref[blk, 4 * BLOCK * h:4 * BLOCK * (h + 1)],
                    jnp.concatenate([va[2 * h][band], va[2 * h + 1][band]], axis=0))

        gm256 = _group_matrix(XATT_W)
        xq = proj_ref[:, C_XQ:C_XQ + XATT_W]
        qx = xq * lax.rsqrt(_seg_mean(xq * xq, gm256) + EPS) * gn_ref[:, G_XQ:G_XK]
        pm = _mem_probs(_mm_nt(qx, km_ref[...]))
        for j in range(4):
            pmem_ref[:, MEM_LEN * j:MEM_LEN * (j + 1)] = pm[j].astype(pmem_ref.dtype)
        yc = _mm(pmem_ref[...], vm_ref[...])
        yc_ref[...] = yc

        def gated(y, g, gate):
            return y * lax.rsqrt(_row_mean(y * y) + EPS) * g * (gate * _sigmoid(gate))

        ogv = og_ref[...]
        za = gated(ya_ref[...], ogv[:, :512], proj_ref[:, C_LRUG:C_LRUG + LRU_W])
        zb = gated(yb_ref[...], ogv[:, 512:768], proj_ref[:, C_SWAG:C_SWAG + SWA_W])
        zc = gated(yc, ogv[:, 768:], proj_ref[:, C_XG:C_XG + XATT_W])
        ycat_ref[:, 0:512] = za.astype(ycat_ref.dtype)
        ycat_ref[:, 512:768] = zb.astype(ycat_ref.dtype)
        ycat_ref[:, 768:1024] = zc.astype(ycat_ref.dtype)
        out = xv + _mm(ycat_ref[...], wout_ref[...])
        err = out - t_ref[...]
        dout_ref[...] = (err * (1.0 / D_MODEL)).astype(dout_ref.dtype)
        lacc_ref[...] = lacc_ref[...] + (0.5 / D_MODEL) * jnp.sum(err * err)

        @pl.when(i == nt - 1)
        def _():
            loss_ref[...] = lacc_ref[...]

    def rows(ncol):
        return pl.BlockSpec((tm, ncol), lambda i: (i, 0))

    in_specs = [rows(D_MODEL), rows(D_MODEL), rows(128), rows(128), rows(128),
                _const_spec((1, D_MODEL)), _const_spec((D_IN, D_MODEL), True), _const_spec((CONV_K, LRU_W)),
                _const_spec((1, LRU_W)), _const_spec((2, 256, 512), True), _const_spec((1, LRU_W)),
                _const_spec((1, LRU_W)), _const_spec((1, LRU_W)), _const_spec((1, GAINS_W)), pl.BlockSpec(memory_space=pltpu.SMEM),
                _const_spec((4 * MEM_LEN, XATT_W), True), _const_spec((4 * MEM_LEN, XATT_W), True),
                _const_spec((1, D_MODEL)), _const_spec((D_MODEL, D_MODEL), True)]
    out_shape = (jax.ShapeDtypeStruct((seq, D_IN), F32), jax.ShapeDtypeStruct((seq, LRU_W), F32),
                 jax.ShapeDtypeStruct((seq, SWA_W), F32), jax.ShapeDtypeStruct((seq, XATT_W), F32),
                 jax.ShapeDtypeStruct((seq, D_MODEL), _MXU), jax.ShapeDtypeStruct((seq, D_MODEL), _MXU),
                 jax.ShapeDtypeStruct((seq, D_MODEL), _MXU), jax.ShapeDtypeStruct((seq, 4 * 2 * BLOCK), _MXU),
                 jax.ShapeDtypeStruct((seq, 4 * MEM_LEN), _MXU), jax.ShapeDtypeStruct((seq, 128), F32),
                 jax.ShapeDtypeStruct((seq, 4 * LRU_W), _MXU), jax.ShapeDtypeStruct((seq, LRU_W), F32),
                 jax.ShapeDtypeStruct((8, 128), F32))
    out_specs = (rows(D_IN), rows(LRU_W), rows(SWA_W), rows(XATT_W), rows(D_MODEL), rows(D_MODEL), rows(D_MODEL),
                 rows(4 * 2 * BLOCK), rows(4 * MEM_LEN), rows(128), rows(4 * LRU_W), rows(LRU_W),
                 _const_spec((8, 128)))
    scratch = [pltpu.VMEM((tm + 8, LRU_W), F32), pltpu.VMEM((tm, LRU_W), F32),
               pltpu.VMEM((8, LRU_W), F32), pltpu.VMEM((BLOCK, KV_W), F32), pltpu.VMEM((BLOCK, KV_W), F32),
               pltpu.VMEM((8, 128), F32)]
    return pl.pallas_call(
        body, name="layer_fwd", grid=(nt,), out_shape=out_shape, in_specs=in_specs, out_specs=out_specs,
        scratch_shapes=scratch,
        compiler_params=pltpu.CompilerParams(dimension_semantics=("arbitrary",), vmem_limit_bytes=VMEM_LIMIT),
    )(x, tgt, rc, rs1, rs2, ng, win_t, cw, cb, wg, brg, big, lam, gains, sinks, km, vm, og, wout)


def weight_grads(ycat, dout, dproj, xn, mem_operands, vectors, early_at, late_at):
    seq, ncol = xn.shape
    blk = 256
    n_out, n_in = ycat.shape[1] // blk, dproj.shape[1] // blk
    assert n_out == N_CHIPS and late_at[0] >= n_out
    g_out = jax.ShapeDtypeStruct((N_CHIPS, 2, blk // 2, dout.shape[1]), F32)
    g_kv = jax.ShapeDtypeStruct((N_CHIPS, 2, D_MODEL // 8, 2 * XATT_W), F32)
    g_small = jax.ShapeDtypeStruct((2, SMALL_ROWS // 2, 128), F32)
    shape_e, scratch_e = _hosted_reduce_shapes([g_kv], g_small)
    shape_l, scratch_l = _hosted_reduce_shapes([g_out], None)
    n_mem, names = len(mem_operands), tuple(vectors)

    def body(l1_ref, r1_ref, l2_ref, r2_ref, *refs):
        mem_refs, vec_refs = refs[:n_mem], refs[n_mem:n_mem + len(names)]
        o_ref, sum_out, sum_kv, sum_sm, gout_scr, gkv_scr, pack_scr, *scratch = refs[n_mem + len(names):]
        j = pl.program_id(0)

        @pl.when(j == 0)
        def _():
            _mem_bwd_and_pack(*mem_refs, dict(zip(names, vec_refs)), gkv_scr, pack_scr)

        def reduce_stages(closing):
            _hosted_reduce(j, n_out + n_in, closing, early_at, (gkv_scr, pack_scr), (sum_kv, sum_sm),
                           scratch[:len(scratch_e)], True)
            _hosted_reduce(j, n_out + n_in, closing, late_at, (gout_scr,), (sum_out,), scratch[len(scratch_e):], False)

        reduce_stages(False)

        @pl.when(j < n_out)
        def _():
            gout_scr[j] = _mm_tn(l1_ref[...], r1_ref[...]).reshape(g_out.shape[1:])

        @pl.when(j >= n_out)
        def _():
            o_ref[...] = _mm_tn(l2_ref[...], r2_ref[...])

        reduce_stages(True)

    vm = pl.BlockSpec(memory_space=pltpu.VMEM)
    hbm = pl.BlockSpec(memory_space=pl.ANY)
    return pl.pallas_call(
        body, name="weight_grads", grid=(n_out + n_in,),
        out_shape=(jax.ShapeDtypeStruct((dproj.shape[1], ncol), F32), *shape_l, *shape_e),
        in_specs=[pl.BlockSpec((seq, blk), lambda j: (0, jnp.minimum(j, n_out - 1))), _const_spec(dout.shape, True),
                  pl.BlockSpec((seq, blk), lambda j: (0, jnp.maximum(j - n_out, 0))), _const_spec(xn.shape, True)]
        + [vm] * (n_mem + len(names)),
        out_specs=(pl.BlockSpec((blk, ncol), lambda j: (jnp.maximum(j - n_out, 0), 0)), hbm, hbm, hbm),
        scratch_shapes=[pltpu.VMEM(s.shape, F32) for s in (g_out, g_kv, g_small)] + scratch_e + scratch_l,
        compiler_params=pltpu.CompilerParams(dimension_semantics=("arbitrary",), vmem_limit_bytes=VMEM_LIMIT),
    )(ycat, dout, dproj, xn, *mem_operands, *vectors.values())


def layer_bwd(x, dout, proj, ya, yb, yc, pswa, pmem, psink, gates, a_all, rc, rs1, rs2, ng, win_t, cw, wg, lam, gains,
              km, vm, og, wout):
    seq = x.shape[0]
    tm = min(ROW_TILE, seq)
    nt = seq // tm
    nb = tm // BLOCK

    def body(x_ref, dout_ref, proj_ref, ya_ref, yb_ref, yc_ref, pswa_ref, pmem_ref, psink_ref, gates_ref, a_ref,
             c_ref, s1_ref, s2_ref,
             yah_ref, kvh_ref, ch_ref, s1h_ref, s2h_ref,
             ng_ref, win_ref, cw_ref, wg_ref, lam_ref, gn_ref, km_ref, vm_ref, og_ref, wout_ref,
             gx_ref, dproj_ref, gwg_ref, dkm_ref, dvm_ref, gng_ref, gog_ref, gcb_ref, gbrg_ref, gbig_ref, glam_ref,
             gcw_ref, gqn_ref, gkn_ref, gxqn_ref, gsink_ref,
             hext_ref, aext_ref, an_scr, dh_scr, g_scr, dxc_ext, gcar_ref, dkcar_ref, dvcar_ref):
        i = pl.program_id(0)
        tile = nt - 1 - i
        first_tile = tile == 0

        @pl.when(i == 0)
        def _():
            for r in (gwg_ref, dkm_ref, dvm_ref, gng_ref, gog_ref, gcb_ref, gbrg_ref, gbig_ref, glam_ref, gcw_ref,
                      gqn_ref, gkn_ref, gxqn_ref, gsink_ref, gcar_ref, dkcar_ref, dvcar_ref):
                r[...] = jnp.zeros_like(r)
            dxc_ext[tm:tm + 8, :] = jnp.zeros((8, LRU_W), F32)
            aext_ref[tm:tm + 8, :] = jnp.zeros((8, LRU_W), F32)

        xv = x_ref[...]
        dov = dout_ref[...]
        dz = _mm_nt(dov, wout_ref[...])
        ogv = og_ref[...]

        def group_bwd(y, gate, g, dzg):
            r = lax.rsqrt(_row_mean(y * y) + EPS)
            n = y * r
            sg = _sigmoid(gate)
            dgate = dzg * (n * g) * (sg * (1.0 + gate * (1.0 - sg)))
            dng = dzg * (gate * sg)
            dn = dng * g
            return r * (dn - n * _row_mean(dn * n)), dgate, _col_sum(dng * n)

        dya, dga, goa = group_bwd(ya_ref[...], proj_ref[:, C_LRUG:C_LRUG + LRU_W], ogv[:, :512], dz[:, :512])
        dyb, dgb, gob = group_bwd(yb_ref[...], proj_ref[:, C_SWAG:C_SWAG + SWA_W], ogv[:, 512:768], dz[:, 512:768])
        dyc, dgc, goc = group_bwd(yc_ref[...], proj_ref[:, C_XG:C_XG + XATT_W], ogv[:, 768:], dz[:, 768:])
        gog_ref[...] += jnp.concatenate([goa, gob, goc], axis=1)
        dproj_ref[:, C_LRUG:C_LRUG + LRU_W] = dga.astype(dproj_ref.dtype)
        dproj_ref[:, C_SWAG:C_SWAG + SWA_W] = dgb.astype(dproj_ref.dtype)
        dproj_ref[:, C_XG:C_XG + XATT_W] = dgc.astype(dproj_ref.dtype)

        gm256 = _group_matrix(XATT_W)
        xq = proj_ref[:, C_XQ:C_XQ + XATT_W]
        rq = lax.rsqrt(_seg_mean(xq * xq, gm256) + EPS)
        qn = xq * rq
        qx = qn * gn_ref[:, G_XQ:G_XK]
        qxb = qx.astype(_MXU)
        dycb = dyc.astype(_MXU)
        dp_all = _mm_nt(dycb, vm_ref[...])
        dsm = []
        for j in range(4):
            pj = pmem_ref[:, MEM_LEN * j:MEM_LEN * (j + 1)].astype(F32)
            dp = dp_all[:, MEM_LEN * j:MEM_LEN * (j + 1)]
            dsm.append((pj * (dp - jnp.sum(pj * dp, axis=-1, keepdims=True))).astype(_MXU))
        ds_all = jnp.concatenate(dsm, axis=1)
        dvm_ref[...] += _mm_tn(dycb, pmem_ref[...])
        dkm_ref[...] += _mm_tn(qxb, ds_all)
        dqx = _mm(ds_all, km_ref[...])
        gxqn_ref[...] += _col_sum(dqx * qn)
        dqn = dqx * gn_ref[:, G_XQ:G_XK]
        dproj_ref[:, C_XQ:C_XQ + XATT_W] = (rq * (dqn - qn * _seg_mean(dqn * qn, gm256))).astype(dproj_ref.dtype)

        gm128 = _group_matrix(KV_W)
        cv, s1v, s2v = c_ref[...], s1_ref[...], s2_ref[...]

        def head_norm(t):
            r = lax.rsqrt(_seg_mean(t * t, gm128) + EPS)
            return t * r, r

        qn_, qr_ = zip(head_norm(proj_ref[:, C_SQ:C_SQ + 128]), head_norm(proj_ref[:, C_SQ + 128:C_SQ + 256]))
        qrope = [_rope(qn_[h] * gn_ref[:, G_Q:G_K], cv, s1v, s2v).astype(_MXU) for h in range(2)]
        kn, krr = head_norm(proj_ref[:, C_SK:C_SK + KV_W])
        kr = _rope(kn * gn_ref[:, G_K:G_XQ], cv, s1v, s2v)
        khn, _ = head_norm(kvh_ref[:, 0:KV_W])
        khr = _rope(khn * gn_ref[:, G_K:G_XQ], ch_ref[...], s1h_ref[...], s2h_ref[...])
        ka = _place_kv(jnp.concatenate([khr, kr], axis=0), 0.125)
        va = _place_kv(jnp.concatenate([kvh_ref[:, KV_W:2 * KV_W], proj_ref[:, C_SV:C_SV + KV_W]], axis=0), 1.0)
        lane128 = lax.broadcasted_iota(jnp.int32, (1, 128), 1)
        gsink = jnp.zeros((1, 128), F32)
        dk_band, dv_band, dq_blk = [], [], []
        for b in range(nb):
            band = slice(BLOCK * b, BLOCK * b + 2 * BLOCK)
            blk = slice(BLOCK * b, BLOCK * (b + 1))
            dka, dva, dsb = [], [], []
            deltas = jnp.zeros((BLOCK, 128), F32)
            for j in range(4):
                qh = qrope[j // 2][blk]
                doh = dyb[blk, KV_W * (j // 2):KV_W * (j // 2 + 1)].astype(_MXU)
                pb = pswa_ref[blk, 2 * BLOCK * j:2 * BLOCK * (j + 1)]
                p = pb.astype(F32)
                dp = _mm_nt(doh, va[j][band])
                delta = jnp.sum(p * dp, axis=-1, keepdims=True)
                ds = (p * (dp - delta)).astype(_MXU)
                deltas = jnp.where(lane128 == j, delta, deltas)
                dva.append(_mm_tn(pb, doh))
                dka.append(_mm_tn(ds, qh))
                dsb.append(ds)
            gsink = gsink - _col_sum(psink_ref[blk, :] * deltas)
            dk_band.append(_unplace_kv(dka) * 0.125)
            dv_band.append(_unplace_kv(dva))
            dq_blk.append([_mm(jnp.concatenate(dsb[2 * h:2 * h + 2], axis=1),
                               jnp.concatenate([ka[2 * h][band], ka[2 * h + 1][band]], axis=0)) for h in range(2)])
        gsink_ref[...] += gsink
        dk_rows = [dk_band[b][BLOCK:] + (dk_band[b + 1][:BLOCK] if b + 1 < nb else dkcar_ref[...]) for b in range(nb)]
        dv_rows = [dv_band[b][BLOCK:] + (dv_band[b + 1][:BLOCK] if b + 1 < nb else dvcar_ref[...]) for b in range(nb)]
        dkcar_ref[...] = dk_band[0][:BLOCK]
        dvcar_ref[...] = dv_band[0][:BLOCK]
        dkg = _rope_bwd(jnp.concatenate(dk_rows, axis=0), cv, s1v, s2v)
        gkn = _col_sum(dkg * kn)
        dkn = dkg * gn_ref[:, G_K:G_XQ]
        dproj_ref[:, C_SK:C_SK + KV_W] = (krr * (dkn - kn * _seg_mean(dkn * kn, gm128))).astype(dproj_ref.dtype)
        dproj_ref[:, C_SV:C_SV + KV_W] = jnp.concatenate(dv_rows, axis=0).astype(dproj_ref.dtype)
        gqn = jnp.zeros((1, 128), F32)
        for h in range(2):
            dqg = _rope_bwd(jnp.concatenate([dq_blk[b][h] for b in range(nb)], axis=0), cv, s1v, s2v)
            gqn = gqn + _col_sum(dqg * qn_[h])
            dqn_ = dqg * gn_ref[:, G_Q:G_K]
            dproj_ref[:, C_SQ + 128 * h:C_SQ + 128 * (h + 1)] = (
                qr_[h] * (dqn_ - qn_[h] * _seg_mean(dqn_ * qn_[h], gm128))).astype(dproj_ref.dtype)
        gqn_ref[...] += gqn
        gkn_ref[...] += gkn

        u = proj_ref[:, C_LRUX:C_LRUX + LRU_W]
        xc, rg, ig, sq = (gates_ref[:, LRU_W * k:LRU_W * (k + 1)].astype(F32) for k in range(4))
        a = a_ref[...]
        sp = _softplus(-lam_ref[...])
        hext_ref[0:8, :] = jnp.where(first_tile, 0.0, yah_ref[...])
        hext_ref[8:8 + tm, :] = ya_ref[...]
        hprev = hext_ref[pl.ds(7, tm), :]
        aext_ref[0:tm, :] = a
        an_scr[...] = aext_ref[pl.ds(1, tm), :]
        dh_scr[...] = dya
        dh_scr[tm - 1:tm, :] = dh_scr[tm - 1:tm, :] + gcar_ref[0:1, :]
        row8 = lax.broadcasted_iota(jnp.int32, (8, LRU_W), 0)

        def scan_step(gi, carry):
            r0 = pl.multiple_of((tm // 8 - 1 - gi) * 8, 8)
            av = an_scr[pl.ds(r0, 8), :]
            bv = dh_scr[pl.ds(r0, 8), :]
            for d in (1, 2, 4):
                a_sh = jnp.where(row8 < 8 - d, pltpu.roll(av, 8 - d, 0), 1.0)
                b_sh = jnp.where(row8 < 8 - d, pltpu.roll(bv, 8 - d, 0), 0.0)
                bv = bv + av * b_sh
                av = av * a_sh
            gv = bv + av * carry
            g_scr[pl.ds(r0, 8), :] = gv
            return gv[0:1, :]

        g0 = lax.fori_loop(0, tm // 8, scan_step, jnp.zeros((1, LRU_W), F32), unroll=True)
        gcar_ref[0:1, :] = a[0:1, :] * g0
        gv = g_scr[...]
        da = gv * hprev
        dig = gv * sq * xc
        dxc = gv * sq * ig
        dla = da * a - gv * (ig * xc) * ((a * a) / sq)
        drg = dla * ((-LRU_C) * sp)
        glam_ref[...] += _col_sum(dla * rg)
        dpr = drg * rg * (1.0 - rg)
        dpi = dig * ig * (1.0 - ig)
        gbrg_ref[...] += _col_sum(dpr)
        gbig_ref[...] += _col_sum(dpi)
        dpre0 = jnp.concatenate([dpr[:, :256], dpi[:, :256]], axis=1).astype(_MXU)
        dpre1 = jnp.concatenate([dpr[:, 256:], dpi[:, 256:]], axis=1).astype(_MXU)
        gwg_ref[0] += _mm_tn(xc[:, :256], dpre0)
        gwg_ref[1] += _mm_tn(xc[:, 256:], dpre1)
        dxc = dxc + jnp.concatenate([_mm_nt(dpre0, wg_ref[0]), _mm_nt(dpre1, wg_ref[1])], axis=1)
        gcb_ref[...] += _col_sum(dxc)
        dxc_ext[0:tm, :] = dxc
        du = jnp.zeros((tm, LRU_W), F32)
        for k in range(CONV_K):
            later = dxc_ext[pl.ds(3 - k, tm), :]
            gcw_ref[k:k + 1, :] += _col_sum(later * u)
            du = du + cw_ref[k:k + 1, :] * later
        dxc_ext[tm:tm + 8, :] = dxc[0:8, :]
        dproj_ref[:, C_LRUX:C_LRUX + LRU_W] = du.astype(dproj_ref.dtype)

        dxn = _mm(dproj_ref[...], win_ref[...])
        rx = lax.rsqrt(_row_mean(xv * xv) + EPS)
        xh = xv * rx
        gng_ref[...] += _col_sum(dxn * xh)
        dxh = dxn * ng_ref[...]
        gx_ref[...] = dov.astype(F32) + rx * (dxh - xh * _row_mean(dxh * xh))

        @pl.when(i == nt - 1)
        def _():
            glam_ref[...] = glam_ref[...] * (LRU_C * _sigmoid(-lam_ref[...]))
            for r in (gqn_ref, gkn_ref, gxqn_ref):
                r[...] = _fold_heads(r[...])

    def rows(ncol, arr_cols_block=0):
        return pl.BlockSpec((tm, ncol), lambda i: (nt - 1 - i, arr_cols_block))

    def halo(nrow, ncol, colblk=0):
        per = tm // nrow
        return pl.BlockSpec((nrow, ncol), lambda i: (jnp.maximum((nt - 1 - i) * per - 1, 0), colblk))

    in_specs = [rows(D_MODEL), rows(D_MODEL), rows(D_IN), rows(LRU_W), rows(SWA_W), rows(XATT_W),
                rows(4 * 2 * BLOCK), rows(4 * MEM_LEN), rows(128), rows(4 * LRU_W), rows(LRU_W),
                rows(128), rows(128), rows(128),
                halo(8, LRU_W), halo(BLOCK, 2 * KV_W, C_SK // (2 * KV_W)),
                halo(BLOCK, 128), halo(BLOCK, 128), halo(BLOCK, 128),
                _const_spec((1, D_MODEL)), _const_spec((D_IN, D_MODEL), True), _const_spec((CONV_K, LRU_W)),
                _const_spec((2, 256, 512), True), _const_spec((1, LRU_W)), _const_spec((1, GAINS_W)),
                _const_spec((4 * MEM_LEN, XATT_W), True), _const_spec((4 * MEM_LEN, XATT_W), True),
                _const_spec((1, D_MODEL)), _const_spec((D_MODEL, D_MODEL), True)]
    small = [(2, 256, 512), (XATT_W, 4 * MEM_LEN), (XATT_W, 4 * MEM_LEN), (1, D_MODEL), (1, D_MODEL), (1, LRU_W), (1, LRU_W),
             (1, LRU_W), (1, LRU_W), (CONV_K, LRU_W), (1, 128), (1, 128), (1, XATT_W), (1, 128)]
    out_shape = (jax.ShapeDtypeStruct((seq, D_MODEL), F32), jax.ShapeDtypeStruct((seq, D_IN), _MXU)) + tuple(
        jax.ShapeDtypeStruct(s, F32) for s in small)
    out_specs = (rows(D_MODEL), rows(D_IN)) + tuple(_const_spec(s) for s in small)
    scratch = [pltpu.VMEM((tm + 8, LRU_W), F32), pltpu.VMEM((tm + 8, LRU_W), F32),
               pltpu.VMEM((tm, LRU_W), F32), pltpu.VMEM((tm, LRU_W), F32), pltpu.VMEM((tm, LRU_W), F32),
               pltpu.VMEM((tm + 8, LRU_W), F32),
               pltpu.VMEM((8, LRU_W), F32), pltpu.VMEM((BLOCK, KV_W), F32), pltpu.VMEM((BLOCK, KV_W), F32)]
    return pl.pallas_call(
        body, name="layer_bwd", grid=(nt,), out_shape=out_shape, in_specs=in_specs, out_specs=out_specs,
        scratch_shapes=scratch,
        compiler_params=pltpu.CompilerParams(dimension_semantics=("arbitrary",), vmem_limit_bytes=VMEM_LIMIT),
    )(x, dout, proj, ya, yb, yc, pswa, pmem, psink, gates, a_all, rc, rs1, rs2, ya, proj, rc, rs1, rs2,
      ng, win_t, cw, wg, lam, gains, km, vm, og, wout)


def _reduce_protocol(big, sm, outs, osm, r1, r1s, wire, r2, r2s, wire2, ps, own, send, recv, lsem):
    nbig = len(big)
    x, y, c = lax.axis_index("x"), lax.axis_index("y"), lax.axis_index("c")
    sibling = (x, y, 1 - c)
    near, far, diag = _partners(x, y, c)
    me, near_id, far_id, diag_id = _chip_of(x, y), _chip_of(*near), _chip_of(*far), _chip_of(*diag)

    def copy(k, src, dst, to):
        return pltpu.make_async_remote_copy(src_ref=src, dst_ref=dst, send_sem=send.at[k], recv_sem=recv.at[k],
                                            device_id=to, device_id_type=MESH)

    def sent(stage, a):
        if a == nbig:
            src, dst, to = ((sm.at[1 - c], r1s, sibling), (r1s, r2s.at[0], (*near, c)), (ps, r2s.at[1], (*far, c)),
                            (osm.at[c], osm.at[c], sibling))[stage]
            return [copy(5 * nbig + stage, src, dst, to)]
        if stage == 0:
            return [copy(5 * a, big[a].at[:, 1 - c], r1[a], sibling)]
        if stage == 1:
            return [copy(5 * a + 1, wire[a].at[near_id], r2[a].at[0], (*near, c)),
                    copy(5 * a + 2, wire[a].at[diag_id], r2[a].at[1], (*near, c))]
        if stage == 2:
            return [copy(5 * a + 3, wire2[a], r2[a].at[2], (*far, c))]
        return [copy(5 * a + 4, outs[a].at[c], outs[a].at[c], sibling)]

    arrays = range(nbig + (sm is not None))

    def start(stage, a):
        for cp in sent(stage, a):
            cp.start()

    def arrived(k, ref):
        copy(k, ref, ref, sibling).wait_recv()

    def loads():
        return [pltpu.make_async_copy(big[a].at[:, c], own[a], lsem.at[a]) for a in range(nbig)]

    def stage0():
        for a in arrays:
            start(0, a)
        for cp in loads():
            cp.start()

    def stage1():
        for a in range(nbig):
            loads()[a].wait()
            arrived(5 * a, r1[a])
            for k in range(N_CHIPS):
                r1[a][k] = own[a][k] + r1[a][k]
                wire[a][k] = r1[a][k].astype(wire[a].dtype)
            start(1, a)
        if sm is not None:
            arrived(5 * nbig, r1s)
            r1s[...] = sm[c] + r1s[...]
            start(1, nbig)

    def stage2():
        for a in range(nbig):
            arrived(5 * a + 1, r2[a].at[0])
            arrived(5 * a + 2, r2[a].at[1])
            r1[a][me] = r1[a][me] + r2[a][0].astype(F32)
            wire2[a][...] = (r1[a][far_id] + r2[a][1].astype(F32)).astype(wire2[a].dtype)
            start(2, a)
        if sm is not None:
            arrived(5 * nbig + 1, r2s.at[0])
            ps[...] = r1s[...] + r2s[0]
            start(2, nbig)

    def stage3():
        for a in range(nbig):
            arrived(5 * a + 3, r2[a].at[2])
            outs[a][c] = r1[a][me] + r2[a][2].astype(F32)
            start(3, a)
        if sm is not None:
            arrived(5 * nbig + 2, r2s.at[1])
            osm[c] = ps[...] + r2s[1]
            start(3, nbig)

    def stage4():
        for a in range(nbig):
            arrived(5 * a + 4, outs[a].at[1 - c])
        if sm is not None:
            arrived(5 * nbig + 3, osm.at[1 - c])
        for stage in range(4):
            for a in arrays:
                for cp in sent(stage, a):
                    cp.wait_send()

    return [stage0, stage1, stage2, stage3, stage4]


def _reduce_buffers(bigs, g_small):
    half = [b.shape[2:] for b in bigs]
    sm_half = None if g_small is None else g_small.shape[1:]
    out_shape = [jax.ShapeDtypeStruct((2,) + h, F32) for h in half]
    small = lambda lead: [] if g_small is None else [pltpu.VMEM(lead + sm_half, F32)]
    if g_small is not None:
        out_shape.append(jax.ShapeDtypeStruct(g_small.shape, F32))
    n_sem = 5 * len(bigs) + 4
    scratch = ([pltpu.VMEM((N_CHIPS,) + h, F32) for h in half] + small(())
               + [pltpu.VMEM((N_CHIPS,) + h, _WIRE) for h in half]
               + [pltpu.VMEM((3,) + h, _WIRE) for h in half] + small((2,))
               + [pltpu.VMEM(h, _WIRE) for h in half] + small(())
               + [pltpu.VMEM((N_CHIPS,) + h, F32) for h in half]
               + [pltpu.SemaphoreType.DMA((n_sem,)), pltpu.SemaphoreType.DMA((n_sem,)),
                  pltpu.SemaphoreType.DMA((len(bigs),))])
    return out_shape, scratch


def _split_reduce_refs(refs, nbig, has_small):
    it = iter(refs)
    take = lambda n: [next(it) for _ in range(n)]
    one = lambda: next(it) if has_small else None
    big, sm = take(nbig), one()
    outs, osm = take(nbig), one()
    r1, r1s, wire, r2, r2s, wire2, ps, own = take(nbig), one(), take(nbig), take(nbig), one(), take(nbig), one(), take(nbig)
    send, recv, lsem = take(3)
    return big, sm, outs, osm, r1, r1s, wire, r2, r2s, wire2, ps, own, send, recv, lsem


def _hosted_reduce_shapes(bigs, g_small):
    red_shape, scratch = _reduce_buffers(bigs, g_small)
    nres = len(red_shape)
    return red_shape, [pltpu.VMEM(r.shape, r.dtype) for r in red_shape] + scratch + [pltpu.SemaphoreType.DMA((nres,))]


def _hosted_reduce(step, n_steps, closing, stage_at, operands, results, scratch, has_small):
    nres = len(results)
    sums, rest, fsem = scratch[:nres], scratch[nres:-1], scratch[-1]
    refs = tuple(operands) + tuple(sums) + tuple(rest)

    def to_results():
        out = [pltpu.make_async_copy(sums[k], results[k], fsem.at[k]) for k in range(nres)]
        for cp in out:
            cp.start()
        for cp in out:
            cp.wait()

    stages = _reduce_protocol(*_split_reduce_refs(refs, nres - has_small, has_small))

    def last_stage():
        stages[-1]()
        to_results()

    for at, stage in zip(stage_at, stages[:-1] + [last_stage]):
        if closing == (at == n_steps):
            pl.when(step == min(at, n_steps - 1))(stage)


def reduce_grads(big, name, parts):
    chips, halves, rows_, cols = big.shape
    sub = jax.ShapeDtypeStruct((chips, halves, rows_ // parts, cols), big.dtype)

    def body(b_ref, o_ref, *scratch):
        refs = [b_ref.at[:, :, s] for s in range(parts)] + [o_ref.at[:, s] for s in range(parts)] + list(scratch)
        for stage in _reduce_protocol(*_split_reduce_refs(refs, parts, False)):
            stage()

    _, scratch = _reduce_buffers([sub] * parts, None)
    return pl.pallas_call(
        body, name=name, out_shape=jax.ShapeDtypeStruct((halves, parts, rows_ // parts, cols), F32),
        in_specs=[pl.BlockSpec(memory_space=pl.ANY)], out_specs=pl.BlockSpec(memory_space=pltpu.VMEM),
        scratch_shapes=scratch, compiler_params=pltpu.CompilerParams(vmem_limit_bytes=VMEM_LIMIT),
    )(big.reshape(chips, halves, parts, rows_ // parts, cols))


def adamw(items, g_pack, ws, ms, vs):
    plan, total = [], 0
    for w, _, _, _ in items:
        rows_, cols = w.shape
        tr = max(t for t in range(8, rows_ + 1, 8) if rows_ % t == 0 and t * cols * 4 <= ADAM_BLOCK_BYTES)
        plan.append((total, rows_ // tr, tr, cols))
        total += rows_ // tr
    nin, n = 4 * len(items), len(ws)

    def body(*refs):
        i = pl.program_id(0)
        small_in = refs[nin:nin + 1 + 3 * n]
        outs = refs[nin + 1 + 3 * n:]

        @pl.when(i == 0)
        def _():
            _adamw_small(small_in[0], *(small_in[1 + k * n:1 + (k + 1) * n] for k in range(3)),
                         *(outs[nin + k * n:nin + (k + 1) * n] for k in range(4)), outs[-1])

        for k, (first, steps, _, _) in enumerate(plan):
            w_ref, g_ref, m_ref, v_ref = refs[4 * k:4 * k + 4]
            go_ref, d_ref, nm_ref, nv_ref = outs[4 * k:4 * k + 4]

            @pl.when((i >= first) & (i < first + steps))
            def _():
                gv = g_ref[...]
                go_ref[...] = gv
                d_ref[...], nm_ref[...], nv_ref[...] = _adam_update(w_ref[...], gv, m_ref[...], v_ref[...])

    specs, shapes = [], []
    for (first, steps, tr, cols), (w, _, _, _) in zip(plan, items):
        spec = pl.BlockSpec((tr, cols), lambda i, first=first, steps=steps: (jnp.clip(i - first, 0, steps - 1), 0))
        specs += [spec] * 4
        shapes += [jax.ShapeDtypeStruct(w.shape, F32)] * 4
    vm = pl.BlockSpec(memory_space=pltpu.VMEM)
    like = [jax.ShapeDtypeStruct(w.shape, F32) for w in ws]
    res = pl.pallas_call(
        body, name="adamw", grid=(total,), out_shape=(*shapes, *like * 4, jax.ShapeDtypeStruct((1, 1), F32)),
        in_specs=specs + [vm] * (1 + 3 * n), out_specs=(*specs, *[vm] * (4 * n + 1)),
        compiler_params=pltpu.CompilerParams(dimension_semantics=("arbitrary",)),
    )(*[a for item in items for a in item], g_pack, *ws, *ms, *vs)
    return [res[4 * k:4 * k + 4] for k in range(len(items))], res[nin:]


def _adam_update(w, g, m, v):
    nm = ADAM_B1 * m + (1.0 - ADAM_B1) * g
    nv = ADAM_B2 * v + (1.0 - ADAM_B2) * (g * g)
    m_hat = nm / (1.0 - ADAM_B1 ** ADAM_STEP)
    v_hat = nv / (1.0 - ADAM_B2 ** ADAM_STEP)
    return (-ADAM_LR) * (m_hat / (jnp.sqrt(v_hat) + ADAM_EPS) + ADAM_WD * w), nm, nv


def _adamw_small(pk, w_refs, m_refs, v_refs, g_out, d_out, nm_out, nv_out, loss_ref):
    nvec = len(SMALL_VECTORS)
    loss_ref[...] = pk[LOSS_ROW:LOSS_ROW + 1, 0:1]
    chip = 2 * lax.axis_index("x") + lax.axis_index("y")
    for k, (name, row, width) in enumerate(SMALL_VECTORS):
        if name == "conv_w":
            g = jnp.concatenate([pk[pl.ds(row + 4 * t + chip, 1), :] for t in range(CONV_K)], axis=0)[None]
        elif width >= 128:
            g = jnp.concatenate([pk[row + r:row + r + 1, :] for r in range(width // 128)], axis=1)
        else:
            g = pk[row:row + 1, 0:width]
        g_out[k][...] = g
        d_out[k][...], nm_out[k][...], nv_out[k][...] = _adam_update(w_refs[k][...], g, m_refs[k][...], v_refs[k][...])
    for k in range(nvec, nvec + len(SMALL_MATRICES)):
        for b in range(LRU_BLOCKS):
            rows_ = pk[GATES_ROW + HEAD * b:GATES_ROW + HEAD * (b + 1), :]
            g = (pltpu.roll(rows_, HEAD, axis=1) if k > nvec else rows_)[:, 0:HEAD]
            g_out[k][0, b] = g
            d_out[k][0, b], nm_out[k][0, b], nv_out[k][0, b] = _adam_update(
                w_refs[k][0, b], g, m_refs[k][0, b], v_refs[k][0, b])


SMALL_VECTORS = (("norm_g", 512, 1024), ("mem_norm_g", 520, 1024), ("conv_w", 528, 512), ("conv_b", 544, 512),
                 ("b_rg", 548, 512), ("b_ig", 552, 512), ("lru_lambda", 556, 512), ("q_norm_g", 560, 64),
                 ("k_norm_g", 561, 64), ("sinks", 562, 4), ("xq_norm_g", 563, 64), ("xk_norm_g", 564, 64),
                 ("out_norm_g", 565, 1024))
LOSS_ROW = 573
SMALL_MATRICES = ("w_rg", "w_ig")
GATES_ROW = 0
SMALL_ROWS = 640


def _rope_tables(seq):
    pos = np.arange(seq, dtype=np.float32)
    inv_freq = (np.float32(ROPE_THETA) ** (-(np.arange(0, ROPE_DIM, 2, dtype=np.float32) / np.float32(ROPE_DIM)))
                ).astype(np.float32)
    ang = (pos[:, None] * inv_freq[None, :]).astype(np.float32)
    cos, sin = np.cos(ang).astype(np.float32), np.sin(ang).astype(np.float32)
    z = lambda n: np.zeros((seq, n), np.float32)
    c64 = np.concatenate([cos, cos, np.ones((seq, HEAD - ROPE_DIM), np.float32)], axis=1)
    s1_64 = np.concatenate([-sin, z(HEAD - 8)], axis=1)
    s2_64 = np.concatenate([z(8), sin, z(HEAD - ROPE_DIM)], axis=1)
    return tuple(jnp.asarray(np.concatenate([t, t], axis=1)) for t in (c64, s1_64, s2_64))


def kernel(x, mem, norm_g, mem_norm_g, w_in, conv_w, conv_b, w_rg, b_rg, w_ig, b_ig, lru_lambda, q_norm_g, k_norm_g, sinks, w_mem_kv, xq_norm_g, xk_norm_g, out_norm_g, w_out, loss_target, m_norm_g, m_mem_norm_g, m_w_in, m_conv_w, m_conv_b, m_w_rg, m_b_rg, m_w_ig, m_b_ig, m_lru_lambda, m_q_norm_g, m_k_norm_g, m_sinks, m_w_mem_kv, m_xq_norm_g, m_xk_norm_g, m_out_norm_g, m_w_out, v_norm_g, v_mem_norm_g, v_w_in, v_conv_w, v_conv_b, v_w_rg, v_b_rg, v_w_ig, v_b_ig, v_lru_lambda, v_q_norm_g, v_k_norm_g, v_sinks, v_w_mem_kv, v_xq_norm_g, v_xk_norm_g, v_out_norm_g, v_w_out):
    seq = x.shape[1]
    xs, tgt, mems = x[0], loss_target[0], mem[0]

    win_t, wout, wkv, cw, wg, gains, km, vm = gather_weights(
        w_in[0].T, w_out[0], w_mem_kv[0], conv_w, w_rg, w_ig, (q_norm_g, k_norm_g, xq_norm_g, xk_norm_g), mems,
        mem_norm_g)
    rc, rs1, rs2 = _rope_tables(seq)
    proj, ya, yb, yc, ycat, xn, dout, pswa, pmem, psink, gates, a_all, loss8 = layer_fwd(
        xs, tgt, rc, rs1, rs2, norm_g, win_t, cw, conv_b, wg, b_rg, b_ig, lru_lambda, gains, sinks, km, vm,
        out_norm_g, wout)
    (gx, dproj, g_wg, dkm, dvm, g_ng, g_og, g_cb, g_brg, g_big, g_lam, g_cw, g_qn, g_kn, g_xqn, g_sink) = layer_bwd(
        xs, dout, proj, ya, yb, yc, pswa, pmem, psink, gates, a_all, rc, rs1, rs2, norm_g, win_t, cw, wg, lru_lambda,
        gains, km, vm, out_norm_g, wout)
    g_win_t, r_out, r_kv, r_small = weight_grads(
        ycat, dout, dproj, xn, (mems, mem_norm_g, wkv, gains, dkm, dvm, g_wg, loss8), dict(
            norm_g=g_ng, conv_w=g_cw, conv_b=g_cb, b_rg=g_brg, b_ig=g_big, lru_lambda=g_lam, q_norm_g=g_qn,
            k_norm_g=g_kn, sinks=g_sink, xq_norm_g=g_xqn, out_norm_g=g_og), (0, 1, 4, 7, 8), (4, 5, 9, 11, 13))
    r_in = reduce_grads(g_win_t.reshape(N_CHIPS, 2, D_IN // 8, D_MODEL), "reduce_w_in", 6)

    r_small = r_small.reshape(SMALL_ROWS, 128)
    grads = {}
    weights = dict(norm_g=norm_g, mem_norm_g=mem_norm_g, w_in=w_in, conv_w=conv_w, conv_b=conv_b, w_rg=w_rg, b_rg=b_rg,
                   w_ig=w_ig, b_ig=b_ig, lru_lambda=lru_lambda, q_norm_g=q_norm_g, k_norm_g=k_norm_g, sinks=sinks,
                   w_mem_kv=w_mem_kv, xq_norm_g=xq_norm_g, xk_norm_g=xk_norm_g, out_norm_g=out_norm_g, w_out=w_out)
    ms = dict(norm_g=m_norm_g, mem_norm_g=m_mem_norm_g, w_in=m_w_in, conv_w=m_conv_w, conv_b=m_conv_b, w_rg=m_w_rg,
              b_rg=m_b_rg, w_ig=m_w_ig, b_ig=m_b_ig, lru_lambda=m_lru_lambda, q_norm_g=m_q_norm_g, k_norm_g=m_k_norm_g,
              sinks=m_sinks, w_mem_kv=m_w_mem_kv, xq_norm_g=m_xq_norm_g, xk_norm_g=m_xk_norm_g,
              out_norm_g=m_out_norm_g, w_out=m_w_out)
    vs = dict(norm_g=v_norm_g, mem_norm_g=v_mem_norm_g, w_in=v_w_in, conv_w=v_conv_w, conv_b=v_conv_b, w_rg=v_w_rg,
              b_rg=v_b_rg, w_ig=v_w_ig, b_ig=v_b_ig, lru_lambda=v_lru_lambda, q_norm_g=v_q_norm_g, k_norm_g=v_k_norm_g,
              sinks=v_sinks, w_mem_kv=v_w_mem_kv, xq_norm_g=v_xq_norm_g, xk_norm_g=v_xk_norm_g,
              out_norm_g=v_out_norm_g, w_out=v_w_out)

    delta, new_m, new_v = {}, {}, {}
    small_names = [n for n, _, _ in SMALL_VECTORS] + list(SMALL_MATRICES)
    (res_in, res_out, res_kv), res = adamw(
        [(w_in[0].T, r_in.reshape(D_IN // 4, D_MODEL), m_w_in[0].T, v_w_in[0].T),
         (w_out[0], r_out.reshape(D_MODEL // 4, D_MODEL), m_w_out[0], v_w_out[0]),
         (w_mem_kv[0], r_kv.reshape(D_MODEL // 4, 2 * XATT_W), m_w_mem_kv[0], v_w_mem_kv[0])],
        r_small, [weights[n] for n in small_names], [ms[n] for n in small_names], [vs[n] for n in small_names])
    grads["w_in"], delta["w_in"], new_m["w_in"], new_v["w_in"] = (r.T[None] for r in res_in)
    grads["w_out"], delta["w_out"], new_m["w_out"], new_v["w_out"] = (r[None] for r in res_out)
    grads["w_mem_kv"], delta["w_mem_kv"], new_m["w_mem_kv"], new_v["w_mem_kv"] = (r[None] for r in res_kv)
    nall = len(small_names)
    for k, into in enumerate((grads, delta, new_m, new_v)):
        into.update(zip(small_names, res[k * nall:(k + 1) * nall]))
    loss = res[-1].reshape(())

    order = ("norm_g", "mem_norm_g", "w_in", "conv_w", "conv_b", "w_rg", "b_rg", "w_ig", "b_ig", "lru_lambda",
             "q_norm_g", "k_norm_g", "sinks", "w_mem_kv", "xq_norm_g", "xk_norm_g", "out_norm_g", "w_out")
    return (loss, gx[None], *[grads[n] for n in order], *[delta[n] for n in order], *[new_m[n] for n in order],
            *[new_v[n] for n in order])
```

```python
import jax
import jax.numpy as jnp
import numpy as np
from jax import lax
from jax.experimental import pallas as pl
from jax.experimental.pallas import tpu as pltpu

F32 = jnp.float32
_MXU = jnp.bfloat16
_WIRE = jnp.bfloat16

D_MODEL = 1024
MEM_LEN = 256
HEAD = 64
LRU_W = 512
LRU_BLOCKS = 8
CONV_K = 4
LRU_C = 8.0
SWA_W = 256
KV_W = 128
XATT_W = 256
BLOCK = 128
D_IN = 2304
ROPE_THETA = 500000.0
ROPE_DIM = 16
EPS = 1e-6
NEG_INF = -1e30
C_LRUX, C_LRUG, C_SQ, C_SK, C_SV, C_SWAG, C_XQ, C_XG = 0, 512, 1024, 1280, 1408, 1536, 1792, 2048
G_Q, G_K, G_XQ, G_XK, GAINS_W = 0, 128, 256, 512, 768

ADAM_LR, ADAM_B1, ADAM_B2, ADAM_EPS, ADAM_WD, ADAM_STEP = 0.001, 0.9, 0.999, 1e-08, 0.01, 10

N_CHIPS = 4
ROW_TILE = 256
VMEM_LIMIT = 56 * 1024 * 1024
ADAM_BLOCK_BYTES = 640 * 1024
MESH = pl.DeviceIdType.MESH


def _mm(a, b):
    return jnp.dot(a.astype(_MXU), b.astype(_MXU), preferred_element_type=F32)


def _mm_nt(a, b):
    return lax.dot_general(a.astype(_MXU), b.astype(_MXU), (((1,), (1,)), ((), ())), preferred_element_type=F32)


def _mm_tn(a, b):
    return lax.dot_general(a.astype(_MXU), b.astype(_MXU), (((0,), (0,)), ((), ())), preferred_element_type=F32)


def _group_matrix(width):
    r = lax.shift_right_logical(lax.broadcasted_iota(jnp.int32, (width, width), 0), 6)
    c = lax.shift_right_logical(lax.broadcasted_iota(jnp.int32, (width, width), 1), 6)
    return (r == c).astype(_MXU)


def _seg_mean(x, gm):
    return jnp.dot(x.astype(_MXU), gm, preferred_element_type=F32) * (1.0 / HEAD)


def _row_mean(x):
    return jnp.mean(x, axis=-1, keepdims=True)


def _col_sum(x):
    return jnp.sum(x, axis=0, keepdims=True)


def _sigmoid(x):
    return jax.nn.sigmoid(x)


def _softplus(z):
    e = jnp.exp(-jnp.abs(z))
    u = 1.0 + e
    log1p_e = jnp.where(u == 1.0, e, jnp.log(u) * (e / (u - 1.0)))
    return jnp.maximum(z, 0.0) + log1p_e


def _rope(t, c, s1, s2):
    return t * c + pltpu.roll(t, 120, 1) * s1 + pltpu.roll(t, 8, 1) * s2


def _rope_bwd(d, c, s1, s2):
    return d * c + pltpu.roll(d * s1, 8, 1) + pltpu.roll(d * s2, 120, 1)


def _fold_heads(v):
    out = v
    for k in range(1, v.shape[1] // HEAD):
        out = out + pltpu.roll(v, HEAD * k, 1)
    return out


def _lane_mask(width, lo, hi):
    lane = lax.broadcasted_iota(jnp.int32, (1, width), 1)
    return ((lane >= lo) & (lane < hi)).astype(F32)


def _swa_mask(first_block):
    qi = lax.broadcasted_iota(jnp.int32, (BLOCK, 2 * BLOCK), 0)
    kj = lax.broadcasted_iota(jnp.int32, (BLOCK, 2 * BLOCK), 1)
    rel = qi + BLOCK - kj
    ok = (rel >= 0) & (rel < BLOCK)
    return ok & (jnp.logical_not(first_block) | (kj >= BLOCK))


def _place_kv(t, scale):
    lo = t * (_lane_mask(KV_W, 0, HEAD) * scale)
    hi = t * (_lane_mask(KV_W, HEAD, KV_W) * scale)
    return [a.astype(_MXU) for a in (lo, pltpu.roll(lo, HEAD, 1), pltpu.roll(hi, HEAD, 1), hi)]


def _unplace_kv(d):
    return (_lane_mask(KV_W, 0, HEAD) * (d[0] + pltpu.roll(d[1], HEAD, 1))
            + _lane_mask(KV_W, HEAD, KV_W) * (d[3] + pltpu.roll(d[2], HEAD, 1)))


def _swa_probs(qh, ka, mask, sink):
    s = _mm_nt(qh, ka)
    s = jnp.where(mask, s, NEG_INF)
    m = jnp.maximum(jnp.max(s, axis=-1, keepdims=True), sink)
    p = jnp.exp(s - m)
    esink = jnp.exp(sink - m)
    inv = 1.0 / (jnp.sum(p, axis=-1, keepdims=True) + esink)
    return p * inv, esink * inv


def _mem_probs(s_all):
    out = []
    for j in range(4):
        s = s_all[:, MEM_LEN * j:MEM_LEN * (j + 1)]
        p = jnp.exp(s - jnp.max(s, axis=-1, keepdims=True))
        out.append(p * (1.0 / jnp.sum(p, axis=-1, keepdims=True)))
    return out


def _head_rows(t, scale):
    return jnp.concatenate([t * (_lane_mask(XATT_W, HEAD * j, HEAD * (j + 1)) * scale) for j in range(4)], axis=0)


def _lru_gates(xc, wg_ref, brg, big, lam):
    p0 = _mm(xc[:, :256], wg_ref[0])
    p1 = _mm(xc[:, 256:], wg_ref[1])
    rg = _sigmoid(jnp.concatenate([p0[:, :256], p1[:, :256]], axis=1) + brg)
    ig = _sigmoid(jnp.concatenate([p0[:, 256:], p1[:, 256:]], axis=1) + big)
    sp = _softplus(-lam)
    la = (-LRU_C) * rg * sp
    a = jnp.exp(la)
    th = jnp.tanh(la)
    one_minus_a2 = (-2.0 * th) / (1.0 - th)
    return rg, ig, sp, a, jnp.sqrt(one_minus_a2)


def _const_spec(shape, single=False):
    zeros = (0,) * len(shape)
    if single:
        return pl.BlockSpec(shape, lambda i: zeros, pipeline_mode=pl.Buffered(1))
    return pl.BlockSpec(shape, lambda i: zeros)


def _chip_of(x, y):
    return 2 * x + y


def _partners(x, y, c):
    north = c == 1
    near = (jnp.where(north, 1 - x, x), jnp.where(north, y, 1 - y))
    far = (jnp.where(north, x, 1 - x), jnp.where(north, 1 - y, y))
    return near, far, (1 - x, 1 - y)


def gather_weights(win_t, wout, wkv, conv_w, w_rg, w_ig, head_gains, mem, mem_g):
    arrs = (win_t, wout, wkv)
    n = len(arrs)
    pieces = [(a, 0, arr.shape[0] // 2) for a, arr in enumerate(arrs)]
    npc = len(pieces)

    def body(a0, a1, a2, cw_in, wrg_ref, wig_ref, q_ref, k_ref, xq_ref, xk_ref, mem_ref, mg_ref,
             o0, o1, o2, cw_out, wg_ref, gn_ref, km_ref, vm_ref, s0, s1, s2, cw, ocw, send, recv, lsem):
        ins, outs = (s0, s1, s2), (o0, o1, o2)
        for src, dst in zip((a0, a1, a2), ins):
            dst[...] = src[...].astype(dst.dtype)
        cw[...] = jnp.zeros(cw.shape, F32)
        cw[0:CONV_K, :] = cw_in[0]
        x, y, c = lax.axis_index("x"), lax.axis_index("y"), lax.axis_index("c")
        sibling = (x, y, 1 - c)
        near, far, diag = _partners(x, y, c)
        chips = [near, far, diag]
        me = _chip_of(x, y)

        def landed(p, chip, half):
            a, off, rows_ = pieces[p]
            r = ins[a].shape[0]
            return outs[a].at[pl.ds(pl.multiple_of(chip * r + half * (r // 2) + off, 16), rows_)]

        def mine(p):
            a, off, rows_ = pieces[p]
            return ins[a].at[pl.ds(pl.multiple_of(c * (ins[a].shape[0] // 2) + off, 16), rows_)]

        def copy(k, src, dst, to):
            return pltpu.make_async_remote_copy(src_ref=src, dst_ref=dst, send_sem=send.at[k], recv_sem=recv.at[k],
                                                device_id=to, device_id_type=MESH)

        def cw_rows(chip):
            return ocw.at[pl.ds(pl.multiple_of(chip * 8, 8), 8)]

        locals_ = []
        for a in range(n):
            r = ins[a].shape[0]
            locals_.append(pltpu.make_async_copy(ins[a], outs[a].at[pl.ds(pl.multiple_of(me * r, 16), r)], lsem.at[a]))
        locals_.append(pltpu.make_async_copy(cw, cw_rows(me), lsem.at[n]))
        for cp in locals_:
            cp.start()

        sent = []
        for p in range(npc):
            for j in range(2):
                sent.append(copy(p * 6 + j, mine(p), landed(p, me, c), (*chips[j], c)))
        for j, chip in enumerate(chips):
            sent.append(copy(npc * 6 + j, cw, cw_rows(me), (*chip, c)))
        for cp in sent:
            cp.start()

        gn_ref[...] = jnp.concatenate([q_ref[...]] * 2 + [k_ref[...]] * 2 + [xq_ref[...]] * 4 + [xk_ref[...]] * 4,
                                      axis=1)
        zeros = lambda lanes: [jnp.zeros((HEAD, lanes), F32)] if lanes else []
        for h in range(2):
            for b in range(4):
                row = []
                for w_ref in (wrg_ref, wig_ref):
                    row += zeros(HEAD * b) + [w_ref[0, 4 * h + b]] + zeros(HEAD * (3 - b))
                wg_ref[h, HEAD * b:HEAD * (b + 1), :] = jnp.concatenate(row, axis=1).astype(wg_ref.dtype)

        for j in range(3):
            for p in range(npc):
                got = landed(p, _chip_of(*chips[j]), c)
                copy(p * 6 + j, got, got, sibling).wait_recv()
                if j == 0:
                    sent.append(copy(p * 6 + 2, got, got, (*far, c)))
                    sent[-1].start()
                sent.append(copy(p * 6 + 3 + j, got, got, sibling))
                sent[-1].start()
        for p in range(npc):
            for j in range(3):
                got = landed(p, _chip_of(*chips[(1, 0, 2)[j]]), 1 - c)
                copy(p * 6 + 3 + j, got, got, sibling).wait_recv()
        for j, chip in enumerate(chips):
            got = cw_rows(_chip_of(*chip))
            copy(npc * 6 + j, got, got, (*chip, c)).wait_recv()
        for cp in sent:
            cp.wait_send()
        for cp in locals_:
            cp.wait()
        for chip in range(N_CHIPS):
            cw_out[:, 128 * chip:128 * (chip + 1)] = ocw[8 * chip:8 * chip + CONV_K, :]

        mem_v = mem_ref[...]
        mn = mem_v * lax.rsqrt(_row_mean(mem_v * mem_v) + EPS) * mg_ref[...]
        mkv = _mm(mn, o2[...])
        kpre = mkv[:, :XATT_W]
        km = kpre * lax.rsqrt(_seg_mean(kpre * kpre, _group_matrix(XATT_W)) + EPS) * gn_ref[:, G_XK:GAINS_W]
        km_ref[...] = _head_rows(km, 0.125).astype(km_ref.dtype)
        vm_ref[...] = _head_rows(mkv[:, XATT_W:], 1.0).astype(vm_ref.dtype)

    vm = pl.BlockSpec(memory_space=pltpu.VMEM)
    hbm = pl.BlockSpec(memory_space=pl.ANY)
    head_rows = jax.ShapeDtypeStruct((4 * MEM_LEN, XATT_W), _MXU)
    out_shape = tuple(jax.ShapeDtypeStruct((N_CHIPS * a.shape[0],) + a.shape[1:], _MXU) for a in arrs) + (
        jax.ShapeDtypeStruct((CONV_K, LRU_W), F32), jax.ShapeDtypeStruct((2, 256, 512), _MXU),
        jax.ShapeDtypeStruct((1, GAINS_W), F32), head_rows, head_rows)
    n_rdma = npc * 6 + 3
    return pl.pallas_call(
        body, name="gather_weights", out_shape=out_shape,
        in_specs=[vm] * 12, out_specs=(hbm, hbm, vm, vm, vm, vm, vm, vm),
        scratch_shapes=[pltpu.VMEM(a.shape, _MXU) for a in arrs] + [
            pltpu.VMEM((8, 128), F32), pltpu.VMEM((N_CHIPS * 8, 128), F32),
            pltpu.SemaphoreType.DMA((n_rdma,)), pltpu.SemaphoreType.DMA((n_rdma,)), pltpu.SemaphoreType.DMA((n + 1,))],
        compiler_params=pltpu.CompilerParams(vmem_limit_bytes=VMEM_LIMIT),
    )(win_t, wout, wkv, conv_w, w_rg, w_ig, *head_gains, mem, mem_g)


def _mem_bwd_and_pack(mem_ref, g_ref, w_ref, gn_ref, dkm_ref, dvm_ref, gg_ref, loss_ref, vectors, gw_ref, pk_ref):
    first_row = {name: (row, width) for name, row, width in SMALL_VECTORS}
    half_rows = SMALL_ROWS // 2
    pk_ref[...] = jnp.zeros(pk_ref.shape, F32)

    def rows_at(at, n):
        assert at // half_rows == (at + n - 1) // half_rows
        return at // half_rows, slice(at % half_rows, at % half_rows + n), slice(None)

    def put(name, src):
        row, width = first_row[name]
        per_row = 1 if width < 128 else src.shape[1] // 128
        for t in range(src.shape[0]):
            for r in range(per_row):
                pk_ref[rows_at(row + per_row * t + r, 1)] = src[t:t + 1, 128 * r:128 * (r + 1)]

    for name, ref in vectors.items():
        put(name, ref)
    pk_ref[rows_at(LOSS_ROW, 1)] = loss_ref[0:1, :]
    upper = lax.broadcasted_iota(jnp.int32, (HEAD, 128), 1) >= HEAD
    for h in range(2):
        for b in range(4):
            rg = gg_ref[h, HEAD * b:HEAD * (b + 1), 128 * (b // 2):128 * (b // 2 + 1)]
            ig = gg_ref[h, HEAD * b:HEAD * (b + 1), 256 + 128 * (b // 2):256 + 128 * (b // 2 + 1)]
            if b % 2:
                rg = pltpu.roll(rg, HEAD, axis=1)
            else:
                ig = pltpu.roll(ig, HEAD, axis=1)
            pk_ref[rows_at(GATES_ROW + HEAD * (4 * h + b), HEAD)] = jnp.where(upper, ig, rg)

    mem_v = mem_ref[...]
    mh = mem_v * lax.rsqrt(_row_mean(mem_v * mem_v) + EPS)
    mn = mh * g_ref[...]
    mkv = _mm(mn, w_ref[...])
    kpre = mkv[:, :XATT_W]
    gm = _group_matrix(XATT_W)
    rk = lax.rsqrt(_seg_mean(kpre * kpre, gm) + EPS)
    kn = kpre * rk
    dk = jnp.zeros((MEM_LEN, XATT_W), F32)
    dv = jnp.zeros((MEM_LEN, XATT_W), F32)
    for j in range(4):
        mj = _lane_mask(XATT_W, HEAD * j, HEAD * (j + 1))
        dk = dk + dkm_ref[:, MEM_LEN * j:MEM_LEN * (j + 1)].T * (mj * 0.125)
        dv = dv + dvm_ref[:, MEM_LEN * j:MEM_LEN * (j + 1)].T * mj
    put("xk_norm_g", _fold_heads(_col_sum(dk * kn)))
    dkn = dk * gn_ref[:, G_XK:GAINS_W]
    dkpre = rk * (dkn - kn * _seg_mean(dkn * kn, gm))
    dmkv = jnp.concatenate([dkpre, dv], axis=1)
    gw_ref[...] = _mm_tn(mn, dmkv).reshape(gw_ref.shape)
    dmn = _mm_nt(dmkv, w_ref[...])
    put("mem_norm_g", _col_sum(dmn * mh))


def layer_fwd(x, tgt, rc, rs1, rs2, ng, win_t, cw, cb, wg, brg, big, lam, gains, sinks, km, vm, og, wout):
    seq = x.shape[0]
    tm = min(ROW_TILE, seq)
    nt = seq // tm
    nb = tm // BLOCK

    def body(x_ref, t_ref, c_ref, s1_ref, s2_ref, ng_ref, win_ref, cw_ref, cb_ref, wg_ref, brg_ref, big_ref, lam_ref,
             gn_ref, sink_ref, km_ref, vm_ref, og_ref, wout_ref,
             proj_ref, ya_ref, yb_ref, yc_ref, ycat_ref, xn_ref, dout_ref, pswa_ref, pmem_ref, psink_ref, gates_ref,
             a_ref, loss_ref,
             ext_ref, b_scr, hc_ref, kp_ref, vp_ref, lacc_ref):
        i = pl.program_id(0)

        @pl.when(i == 0)
        def _():
            ext_ref[0:8, :] = jnp.zeros((8, LRU_W), F32)
            hc_ref[...] = jnp.zeros_like(hc_ref)
            kp_ref[...] = jnp.zeros_like(kp_ref)
            vp_ref[...] = jnp.zeros_like(vp_ref)
            lacc_ref[...] = jnp.zeros_like(lacc_ref)

        xv = x_ref[...]
        xn = (xv * lax.rsqrt(_row_mean(xv * xv) + EPS) * ng_ref[...]).astype(_MXU)
        xn_ref[...] = xn.astype(xn_ref.dtype)
        proj_ref[...] = _mm_nt(xn, win_ref[...])

        u = proj_ref[:, C_LRUX:C_LRUX + LRU_W]
        ext_ref[8:8 + tm, :] = u
        xc = cb_ref[...]
        for k in range(CONV_K):
            xc = xc + cw_ref[k:k + 1, :] * ext_ref[pl.ds(5 + k, tm), :]
        ext_ref[0:8, :] = u[tm - 8:tm, :]
        rg, ig, sp, a, sq = _lru_gates(xc, wg_ref, brg_ref[...], big_ref[...], lam_ref[...])
        for k, t in enumerate((xc, rg, ig, sq)):
            gates_ref[:, LRU_W * k:LRU_W * (k + 1)] = t.astype(gates_ref.dtype)
        a_ref[...] = a
        b_scr[...] = sq * (ig * xc)
        row8 = lax.broadcasted_iota(jnp.int32, (8, LRU_W), 0)

        def scan_step(g, carry):
            r0 = pl.multiple_of(g * 8, 8)
            av = a_ref[pl.ds(r0, 8), :]
            bv = b_scr[pl.ds(r0, 8), :]
            for d in (1, 2, 4):
                a_sh = jnp.where(row8 >= d, pltpu.roll(av, d, 0), 1.0)
                b_sh = jnp.where(row8 >= d, pltpu.roll(bv, d, 0), 0.0)
                bv = bv + av * b_sh
                av = av * a_sh
            hv = bv + av * carry
            ya_ref[pl.ds(r0, 8), :] = hv
            return hv[7:8, :]

        hc_ref[0:1, :] = lax.fori_loop(0, tm // 8, scan_step, hc_ref[0:1, :], unroll=True)

        gm128 = _group_matrix(KV_W)
        cv, s1v, s2v = c_ref[...], s1_ref[...], s2_ref[...]

        def head_norm_rope(t, g):
            n = t * lax.rsqrt(_seg_mean(t * t, gm128) + EPS)
            return _rope(n * g, cv, s1v, s2v)

        qs_ = (head_norm_rope(proj_ref[:, C_SQ:C_SQ + 128], gn_ref[:, G_Q:G_K]).astype(_MXU),
               head_norm_rope(proj_ref[:, C_SQ + 128:C_SQ + 256], gn_ref[:, G_Q:G_K]).astype(_MXU))
        kr = head_norm_rope(proj_ref[:, C_SK:C_SK + KV_W], gn_ref[:, G_K:G_XQ])
        sv = proj_ref[:, C_SV:C_SV + KV_W]
        ka = _place_kv(jnp.concatenate([kp_ref[...], kr], axis=0), 0.125)
        va = _place_kv(jnp.concatenate([vp_ref[...], sv], axis=0), 1.0)
        kp_ref[...] = kr[tm - BLOCK:tm, :]
        vp_ref[...] = sv[tm - BLOCK:tm, :]
        lane128 = lax.broadcasted_iota(jnp.int32, (1, 128), 1)
        for b in range(nb):
            mask = _swa_mask((i == 0) & (b == 0)) if b == 0 else _swa_mask(False)
            band = slice(BLOCK * b, BLOCK * b + 2 * BLOCK)
            blk = slice(BLOCK * b, BLOCK * (b + 1))
            psink = jnp.zeros((BLOCK, 128), F32)
            for j in range(4):
                p, pk = _swa_probs(qs_[j // 2][blk], ka[j][band], mask, sink_ref[0, j])
                pswa_ref[blk, 2 * BLOCK * j:2 * BLOCK * (j + 1)] = p.astype(pswa_ref.dtype)
                psink = jnp.where(lane128 == j, pk, psink)
            psink_ref[blk, :] = psink
            for h in range(2):
                yb_ref[blk, KV_W * h:KV_W * (h + 1)] = _mm(
                    pswa_ref[blk, 4 * BLOCK * h:4 * BLOCK * (h + 1)],
                    jnp.concatenate([va[2 * h][band], va[2 * h + 1][band]], axis=0))

        gm256 = _group_matrix(XATT_W)
        xq = proj_ref[:, C_XQ:C_XQ + XATT_W]
        qx = xq * lax.rsqrt(_seg_mean(xq * xq, gm256) + EPS) * gn_ref[:, G_XQ:G_XK]
        pm = _mem_probs(_mm_nt(qx, km_ref[...]))
        for j in range(4):
            pmem_ref[:, MEM_LEN * j:MEM_LEN * (j + 1)] = pm[j].astype(pmem_ref.dtype)
        yc = _mm(pmem_ref[...], vm_ref[...])
        yc_ref[...] = yc

        def gated(y, g, gate):
            return y * lax.rsqrt(_row_mean(y * y) + EPS) * g * (gate * _sigmoid(gate))

        ogv = og_ref[...]
        za = gated(ya_ref[...], ogv[:, :512], proj_ref[:, C_LRUG:C_LRUG + LRU_W])
        zb = gated(yb_ref[...], ogv[:, 512:768], proj_ref[:, C_SWAG:C_SWAG + SWA_W])
        zc = gated(yc, ogv[:, 768:], proj_ref[:, C_XG:C_XG + XATT_W])
        ycat_ref[:, 0:512] = za.astype(ycat_ref.dtype)
        ycat_ref[:, 512:768] = zb.astype(ycat_ref.dtype)
        ycat_ref[:, 768:1024] = zc.astype(ycat_ref.dtype)
        out = xv + _mm(ycat_ref[...], wout_ref[...])
        err = out - t_ref[...]
        dout_ref[...] = (err * (1.0 / D_MODEL)).astype(dout_ref.dtype)
        lacc_ref[...] = lacc_ref[...] + (0.5 / D_MODEL) * jnp.sum(err * err)

        @pl.when(i == nt - 1)
        def _():
            loss_ref[...] = lacc_ref[...]

    def rows(ncol):
        return pl.BlockSpec((tm, ncol), lambda i: (i, 0))

    in_specs = [rows(D_MODEL), rows(D_MODEL), rows(128), rows(128), rows(128),
                _const_spec((1, D_MODEL)), _const_spec((D_IN, D_MODEL), True), _const_spec((CONV_K, LRU_W)),
                _const_spec((1, LRU_W)), _const_spec((2, 256, 512), True), _const_spec((1, LRU_W)),
                _const_spec((1, LRU_W)), _const_spec((1, LRU_W)), _const_spec((1, GAINS_W)), pl.BlockSpec(memory_space=pltpu.SMEM),
                _const_spec((4 * MEM_LEN, XATT_W), True), _const_spec((4 * MEM_LEN, XATT_W), True),
                _const_spec((1, D_MODEL)), _const_spec((D_MODEL, D_MODEL), True)]
    out_shape = (jax.ShapeDtypeStruct((seq, D_IN), F32), jax.ShapeDtypeStruct((seq, LRU_W), F32),
                 jax.ShapeDtypeStruct((seq, SWA_W), F32), jax.ShapeDtypeStruct((seq, XATT_W), F32),
                 jax.ShapeDtypeStruct((seq, D_MODEL), _MXU), jax.ShapeDtypeStruct((seq, D_MODEL), _MXU),
                 jax.ShapeDtypeStruct((seq, D_MODEL), _MXU), jax.ShapeDtypeStruct((seq, 4 * 2 * BLOCK), _MXU),
                 jax.ShapeDtypeStruct((seq, 4 * MEM_LEN), _MXU), jax.ShapeDtypeStruct((seq, 128), F32),
                 jax.ShapeDtypeStruct((seq, 4 * LRU_W), _MXU), jax.ShapeDtypeStruct((seq, LRU_W), F32),
                 jax.ShapeDtypeStruct((8, 128), F32))
    out_specs = (rows(D_IN), rows(LRU_W), rows(SWA_W), rows(XATT_W), rows(D_MODEL), rows(D_MODEL), rows(D_MODEL),
                 rows(4 * 2 * BLOCK), rows(4 * MEM_LEN), rows(128), rows(4 * LRU_W), rows(LRU_W),
                 _const_spec((8, 128)))
    scratch = [pltpu.VMEM((tm + 8, LRU_W), F32), pltpu.VMEM((tm, LRU_W), F32),
               pltpu.VMEM((8, LRU_W), F32), pltpu.VMEM((BLOCK, KV_W), F32), pltpu.VMEM((BLOCK, KV_W), F32),
               pltpu.VMEM((8, 128), F32)]
    return pl.pallas_call(
        body, name="layer_fwd", grid=(nt,), out_shape=out_shape, in_specs=in_specs, out_specs=out_specs,
        scratch_shapes=scratch,
        compiler_params=pltpu.CompilerParams(dimension_semantics=("arbitrary",), vmem_limit_bytes=VMEM_LIMIT),
    )(x, tgt, rc, rs1, rs2, ng, win_t, cw, cb, wg, brg, big, lam, gains, sinks, km, vm, og, wout)


def weight_grads(ycat, dout, dproj, xn, mem_operands, vectors, early_at, late_at):
    seq, ncol = xn.shape
    blk = 256
    n_out, n_in = ycat.shape[1] // blk, dproj.shape[1] // blk
    assert n_out == N_CHIPS and late_at[0] >= n_out
    g_out = jax.ShapeDtypeStruct((N_CHIPS, 2, blk // 2, dout.shape[1]), F32)
    g_kv = jax.ShapeDtypeStruct((N_CHIPS, 2, D_MODEL // 8, 2 * XATT_W), F32)
    g_small = jax.ShapeDtypeStruct((2, SMALL_ROWS // 2, 128), F32)
    shape_e, scratch_e = _hosted_reduce_shapes([g_kv], g_small)
    shape_l, scratch_l = _hosted_reduce_shapes([g_out], None)
    n_mem, names = len(mem_operands), tuple(vectors)

    def body(l1_ref, r1_ref, l2_ref, r2_ref, *refs):
        mem_refs, vec_refs = refs[:n_mem], refs[n_mem:n_mem + len(names)]
        o_ref, sum_out, sum_kv, sum_sm, gout_scr, gkv_scr, pack_scr, *scratch = refs[n_mem + len(names):]
        j = pl.program_id(0)

        @pl.when(j == 0)
        def _():
            _mem_bwd_and_pack(*mem_refs, dict(zip(names, vec_refs)), gkv_scr, pack_scr)

        def reduce_stages(closing):
            _hosted_reduce(j, n_out + n_in, closing, early_at, (gkv_scr, pack_scr), (sum_kv, sum_sm),
                           scratch[:len(scratch_e)], True)
            _hosted_reduce(j, n_out + n_in, closing, late_at, (gout_scr,), (sum_out,), scratch[len(scratch_e):], False)

        reduce_stages(False)

        @pl.when(j < n_out)
        def _():
            gout_scr[j] = _mm_tn(l1_ref[...], r1_ref[...]).reshape(g_out.shape[1:])

        @pl.when(j >= n_out)
        def _():
            o_ref[...] = _mm_tn(l2_ref[...], r2_ref[...])

        reduce_stages(True)

    vm = pl.BlockSpec(memory_space=pltpu.VMEM)
    hbm = pl.BlockSpec(memory_space=pl.ANY)
    return pl.pallas_call(
        body, name="weight_grads", grid=(n_out + n_in,),
        out_shape=(jax.ShapeDtypeStruct((dproj.shape[1], ncol), F32), *shape_l, *shape_e),
        in_specs=[pl.BlockSpec((seq, blk), lambda j: (0, jnp.minimum(j, n_out - 1))), _const_spec(dout.shape, True),
                  pl.BlockSpec((seq, blk), lambda j: (0, jnp.maximum(j - n_out, 0))), _const_spec(xn.shape, True)]
        + [vm] * (n_mem + len(names)),
        out_specs=(pl.BlockSpec((blk, ncol), lambda j: (jnp.maximum(j - n_out, 0), 0)), hbm, hbm, hbm),
        scratch_shapes=[pltpu.VMEM(s.shape, F32) for s in (g_out, g_kv, g_small)] + scratch_e + scratch_l,
        compiler_params=pltpu.CompilerParams(dimension_semantics=("arbitrary",), vmem_limit_bytes=VMEM_LIMIT),
    )(ycat, dout, dproj, xn, *mem_operands, *vectors.values())


def layer_bwd(x, dout, proj, ya, yb, yc, pswa, pmem, psink, gates, a_all, rc, rs1, rs2, ng, win_t, cw, wg, lam, gains,
              km, vm, og, wout):
    seq = x.shape[0]
    tm = min(ROW_TILE, seq)
    nt = seq // tm
    nb = tm // BLOCK

    def body(x_ref, dout_ref, proj_ref, ya_ref, yb_ref, yc_ref, pswa_ref, pmem_ref, psink_ref, gates_ref, a_ref,
             c_ref, s1_ref, s2_ref,
             yah_ref, kvh_ref, ch_ref, s1h_ref, s2h_ref,
             ng_ref, win_ref, cw_ref, wg_ref, lam_ref, gn_ref, km_ref, vm_ref, og_ref, wout_ref,
             gx_ref, dproj_ref, gwg_ref, dkm_ref, dvm_ref, gng_ref, gog_ref, gcb_ref, gbrg_ref, gbig_ref, glam_ref,
             gcw_ref, gqn_ref, gkn_ref, gxqn_ref, gsink_ref,
             hext_ref, aext_ref, an_scr, dh_scr, g_scr, dxc_ext, gcar_ref, dkcar_ref, dvcar_ref):
        i = pl.program_id(0)
        tile = nt - 1 - i
        first_tile = tile == 0

        @pl.when(i == 0)
        def _():
            for r in (gwg_ref, dkm_ref, dvm_ref, gng_ref, gog_ref, gcb_ref, gbrg_ref, gbig_ref, glam_ref, gcw_ref,
                      gqn_ref, gkn_ref, gxqn_ref, gsink_ref, gcar_ref, dkcar_ref, dvcar_ref):
                r[...] = jnp.zeros_like(r)
            dxc_ext[tm:tm + 8, :] = jnp.zeros((8, LRU_W), F32)
            aext_ref[tm:tm + 8, :] = jnp.zeros((8, LRU_W), F32)

        xv = x_ref[...]
        dov = dout_ref[...]
        dz = _mm_nt(dov, wout_ref[...])
        ogv = og_ref[...]

        def group_bwd(y, gate, g, dzg):
            r = lax.rsqrt(_row_mean(y * y) + EPS)
            n = y * r
            sg = _sigmoid(gate)
            dgate = dzg * (n * g) * (sg * (1.0 + gate * (1.0 - sg)))
            dng = dzg * (gate * sg)
            dn = dng * g
            return r * (dn - n * _row_mean(dn * n)), dgate, _col_sum(dng * n)

        dya, dga, goa = group_bwd(ya_ref[...], proj_ref[:, C_LRUG:C_LRUG + LRU_W], ogv[:, :512], dz[:, :512])
        dyb, dgb, gob = group_bwd(yb_ref[...], proj_ref[:, C_SWAG:C_SWAG + SWA_W], ogv[:, 512:768], dz[:, 512:768])
        dyc, dgc, goc = group_bwd(yc_ref[...], proj_ref[:, C_XG:C_XG + XATT_W], ogv[:, 768:], dz[:, 768:])
        gog_ref[...] += jnp.concatenate([goa, gob, goc], axis=1)
        dproj_ref[:, C_LRUG:C_LRUG + LRU_W] = dga.astype(dproj_ref.dtype)
        dproj_ref[:, C_SWAG:C_SWAG + SWA_W] = dgb.astype(dproj_ref.dtype)
        dproj_ref[:, C_XG:C_XG + XATT_W] = dgc.astype(dproj_ref.dtype)

        gm256 = _group_matrix(XATT_W)
        xq = proj_ref[:, C_XQ:C_XQ + XATT_W]
        rq = lax.rsqrt(_seg_mean(xq * xq, gm256) + EPS)
        qn = xq * rq
        qx = qn * gn_ref[:, G_XQ:G_XK]
        qxb = qx.astype(_MXU)
        dycb = dyc.astype(_MXU)
        dp_all = _mm_nt(dycb, vm_ref[...])
        dsm = []
        for j in range(4):
            pj = pmem_ref[:, MEM_LEN * j:MEM_LEN * (j + 1)].astype(F32)
            dp = dp_all[:, MEM_LEN * j:MEM_LEN * (j + 1)]
            dsm.append((pj * (dp - jnp.sum(pj * dp, axis=-1, keepdims=True))).astype(_MXU))
        ds_all = jnp.concatenate(dsm, axis=1)
        dvm_ref[...] += _mm_tn(dycb, pmem_ref[...])
        dkm_ref[...] += _mm_tn(qxb, ds_all)
        dqx = _mm(ds_all, km_ref[...])
        gxqn_ref[...] += _col_sum(dqx * qn)
        dqn = dqx * gn_ref[:, G_XQ:G_XK]
        dproj_ref[:, C_XQ:C_XQ + XATT_W] = (rq * (dqn - qn * _seg_mean(dqn * qn, gm256))).astype(dproj_ref.dtype)

        gm128 = _group_matrix(KV_W)
        cv, s1v, s2v = c_ref[...], s1_ref[...], s2_ref[...]

        def head_norm(t):
            r = lax.rsqrt(_seg_mean(t * t, gm128) + EPS)
            return t * r, r

        qn_, qr_ = zip(head_norm(proj_ref[:, C_SQ:C_SQ + 128]), head_norm(proj_ref[:, C_SQ + 128:C_SQ + 256]))
        qrope = [_rope(qn_[h] * gn_ref[:, G_Q:G_K], cv, s1v, s2v).astype(_MXU) for h in range(2)]
        kn, krr = head_norm(proj_ref[:, C_SK:C_SK + KV_W])
        kr = _rope(kn * gn_ref[:, G_K:G_XQ], cv, s1v, s2v)
        khn, _ = head_norm(kvh_ref[:, 0:KV_W])
        khr = _rope(khn * gn_ref[:, G_K:G_XQ], ch_ref[...], s1h_ref[...], s2h_ref[...])
        ka = _place_kv(jnp.concatenate([khr, kr], axis=0), 0.125)
        va = _place_kv(jnp.concatenate([kvh_ref[:, KV_W:2 * KV_W], proj_ref[:, C_SV:C_SV + KV_W]], axis=0), 1.0)
        lane128 = lax.broadcasted_iota(jnp.int32, (1, 128), 1)
        gsink = jnp.zeros((1, 128), F32)
        dk_band, dv_band, dq_blk = [], [], []
        for b in range(nb):
            band = slice(BLOCK * b, BLOCK * b + 2 * BLOCK)
            blk = slice(BLOCK * b, BLOCK * (b + 1))
            dka, dva, dsb = [], [], []
            deltas = jnp.zeros((BLOCK, 128), F32)
            for j in range(4):
                qh = qrope[j // 2][blk]
                doh = dyb[blk, KV_W * (j // 2):KV_W * (j // 2 + 1)].astype(_MXU)
                pb = pswa_ref[blk, 2 * BLOCK * j:2 * BLOCK * (j + 1)]
                p = pb.astype(F32)
                dp = _mm_nt(doh, va[j][band])
                delta = jnp.sum(p * dp, axis=-1, keepdims=True)
                ds = (p * (dp - delta)).astype(_MXU)
                deltas = jnp.where(lane128 == j, delta, deltas)
                dva.append(_mm_tn(pb, doh))
                dka.append(_mm_tn(ds, qh))
                dsb.append(ds)
            gsink = gsink - _col_sum(psink_ref[blk, :] * deltas)
            dk_band.append(_unplace_kv(dka) * 0.125)
            dv_band.append(_unplace_kv(dva))
            dq_blk.append([_mm(jnp.concatenate(dsb[2 * h:2 * h + 2], axis=1),
                               jnp.concatenate([ka[2 * h][band], ka[2 * h + 1][band]], axis=0)) for h in range(2)])
        gsink_ref[...] += gsink
        dk_rows = [dk_band[b][BLOCK:] + (dk_band[b + 1][:BLOCK] if b + 1 < nb else dkcar_ref[...]) for b in range(nb)]
        dv_rows = [dv_band[b][BLOCK:] + (dv_band[b + 1][:BLOCK] if b + 1 < nb else dvcar_ref[...]) for b in range(nb)]
        dkcar_ref[...] = dk_band[0][:BLOCK]
        dvcar_ref[...] = dv_band[0][:BLOCK]
        dkg = _rope_bwd(jnp.concatenate(dk_rows, axis=0), cv, s1v, s2v)
        gkn = _col_sum(dkg * kn)
        dkn = dkg * gn_ref[:, G_K:G_XQ]
        dproj_ref[:, C_SK:C_SK + KV_W] = (krr * (dkn - kn * _seg_mean(dkn * kn, gm128))).astype(dproj_ref.dtype)
        dproj_ref[:, C_SV:C_SV + KV_W] = jnp.concatenate(dv_rows, axis=0).astype(dproj_ref.dtype)
        gqn = jnp.zeros((1, 128), F32)
        for h in range(2):
            dqg = _rope_bwd(jnp.concatenate([dq_blk[b][h] for b in range(nb)], axis=0), cv, s1v, s2v)
            gqn = gqn + _col_sum(dqg * qn_[h])
            dqn_ = dqg * gn_ref[:, G_Q:G_K]
            dproj_ref[:, C_SQ + 128 * h:C_SQ + 128 * (h + 1)] = (
                qr_[h] * (dqn_ - qn_[h] * _seg_mean(dqn_ * qn_[h], gm128))).astype(dproj_ref.dtype)
        gqn_ref[...] += gqn
        gkn_ref[...] += gkn

        u = proj_ref[:, C_LRUX:C_LRUX + LRU_W]
        xc, rg, ig, sq = (gates_ref[:, LRU_W * k:LRU_W * (k + 1)].astype(F32) for k in range(4))
        a = a_ref[...]
        sp = _softplus(-lam_ref[...])
        hext_ref[0:8, :] = jnp.where(first_tile, 0.0, yah_ref[...])
        hext_ref[8:8 + tm, :] = ya_ref[...]
        hprev = hext_ref[pl.ds(7, tm), :]
        aext_ref[0:tm, :] = a
        an_scr[...] = aext_ref[pl.ds(1, tm), :]
        dh_scr[...] = dya
        dh_scr[tm - 1:tm, :] = dh_scr[tm - 1:tm, :] + gcar_ref[0:1, :]
        row8 = lax.broadcasted_iota(jnp.int32, (8, LRU_W), 0)

        def scan_step(gi, carry):
            r0 = pl.multiple_of((tm // 8 - 1 - gi) * 8, 8)
            av = an_scr[pl.ds(r0, 8), :]
            bv = dh_scr[pl.ds(r0, 8), :]
            for d in (1, 2, 4):
                a_sh = jnp.where(row8 < 8 - d, pltpu.roll(av, 8 - d, 0), 1.0)
                b_sh = jnp.where(row8 < 8 - d, pltpu.roll(bv, 8 - d, 0), 0.0)
                bv = bv + av * b_sh
                av = av * a_sh
            gv = bv + av * carry
            g_scr[pl.ds(r0, 8), :] = gv
            return gv[0:1, :]

        g0 = lax.fori_loop(0, tm // 8, scan_step, jnp.zeros((1, LRU_W), F32), unroll=True)
        gcar_ref[0:1, :] = a[0:1, :] * g0
        gv = g_scr[...]
        da = gv * hprev
        dig = gv * sq * xc
        dxc = gv * sq * ig
        dla = da * a - gv * (ig * xc) * ((a * a) / sq)
        drg = dla * ((-LRU_C) * sp)
        glam_ref[...] += _col_sum(dla * rg)
        dpr = drg * rg * (1.0 - rg)
        dpi = dig * ig * (1.0 - ig)
        gbrg_ref[...] += _col_sum(dpr)
        gbig_ref[...] += _col_sum(dpi)
        dpre0 = jnp.concatenate([dpr[:, :256], dpi[:, :256]], axis=1).astype(_MXU)
        dpre1 = jnp.concatenate([dpr[:, 256:], dpi[:, 256:]], axis=1).astype(_MXU)
        gwg_ref[0] += _mm_tn(xc[:, :256], dpre0)
        gwg_ref[1] += _mm_tn(xc[:, 256:], dpre1)
        dxc = dxc + jnp.concatenate([_mm_nt(dpre0, wg_ref[0]), _mm_nt(dpre1, wg_ref[1])], axis=1)
        gcb_ref[...] += _col_sum(dxc)
        dxc_ext[0:tm, :] = dxc
        du = jnp.zeros((tm, LRU_W), F32)
        for k in range(CONV_K):
            later = dxc_ext[pl.ds(3 - k, tm), :]
            gcw_ref[k:k + 1, :] += _col_sum(later * u)
            du = du + cw_ref[k:k + 1, :] * later
        dxc_ext[tm:tm + 8, :] = dxc[0:8, :]
        dproj_ref[:, C_LRUX:C_LRUX + LRU_W] = du.astype(dproj_ref.dtype)

        dxn = _mm(dproj_ref[...], win_ref[...])
        rx = lax.rsqrt(_row_mean(xv * xv) + EPS)
        xh = xv * rx
        gng_ref[...] += _col_sum(dxn * xh)
        dxh = dxn * ng_ref[...]
        gx_ref[...] = dov.astype(F32) + rx * (dxh - xh * _row_mean(dxh * xh))

        @pl.when(i == nt - 1)
        def _():
            glam_ref[...] = glam_ref[...] * (LRU_C * _sigmoid(-lam_ref[...]))
            for r in (gqn_ref, gkn_ref, gxqn_ref):
                r[...] = _fold_heads(r[...])

    def rows(ncol, arr_cols_block=0):
        return pl.BlockSpec((tm, ncol), lambda i: (nt - 1 - i, arr_cols_block))

    def halo(nrow, ncol, colblk=0):
        per = tm // nrow
        return pl.BlockSpec((nrow, ncol), lambda i: (jnp.maximum((nt - 1 - i) * per - 1, 0), colblk))

    in_specs = [rows(D_MODEL), rows(D_MODEL), rows(D_IN), rows(LRU_W), rows(SWA_W), rows(XATT_W),
                rows(4 * 2 * BLOCK), rows(4 * MEM_LEN), rows(128), rows(4 * LRU_W), rows(LRU_W),
                rows(128), rows(128), rows(128),
                halo(8, LRU_W), halo(BLOCK, 2 * KV_W, C_SK // (2 * KV_W)),
                halo(BLOCK, 128), halo(BLOCK, 128), halo(BLOCK, 128),
                _const_spec((1, D_MODEL)), _const_spec((D_IN, D_MODEL), True), _const_spec((CONV_K, LRU_W)),
                _const_spec((2, 256, 512), True), _const_spec((1, LRU_W)), _const_spec((1, GAINS_W)),
                _const_spec((4 * MEM_LEN, XATT_W), True), _const_spec((4 * MEM_LEN, XATT_W), True),
                _const_spec((1, D_MODEL)), _const_spec((D_MODEL, D_MODEL), True)]
    small = [(2, 256, 512), (XATT_W, 4 * MEM_LEN), (XATT_W, 4 * MEM_LEN), (1, D_MODEL), (1, D_MODEL), (1, LRU_W), (1, LRU_W),
             (1, LRU_W), (1, LRU_W), (CONV_K, LRU_W), (1, 128), (1, 128), (1, XATT_W), (1, 128)]
    out_shape = (jax.ShapeDtypeStruct((seq, D_MODEL), F32), jax.ShapeDtypeStruct((seq, D_IN), _MXU)) + tuple(
        jax.ShapeDtypeStruct(s, F32) for s in small)
    out_specs = (rows(D_MODEL), rows(D_IN)) + tuple(_const_spec(s) for s in small)
    scratch = [pltpu.VMEM((tm + 8, LRU_W), F32), pltpu.VMEM((tm + 8, LRU_W), F32),
               pltpu.VMEM((tm, LRU_W), F32), pltpu.VMEM((tm, LRU_W), F32), pltpu.VMEM((tm, LRU_W), F32),
               pltpu.VMEM((tm + 8, LRU_W), F32),
               pltpu.VMEM((8, LRU_W), F32), pltpu.VMEM((BLOCK, KV_W), F32), pltpu.VMEM((BLOCK, KV_W), F32)]
    return pl.pallas_call(
        body, name="layer_bwd", grid=(nt,), out_shape=out_shape, in_specs=in_specs, out_specs=out_specs,
        scratch_shapes=scratch,
        compiler_params=pltpu.CompilerParams(dimension_semantics=("arbitrary",), vmem_limit_bytes=VMEM_LIMIT),
    )(x, dout, proj, ya, yb, yc, pswa, pmem, psink, gates, a_all, rc, rs1, rs2, ya, proj, rc, rs1, rs2,
      ng, win_t, cw, wg, lam, gains, km, vm, og, wout)


def _reduce_protocol(big, sm, outs, osm, r1, r1s, wire, r2, r2s, wire2, ps, own, send, recv, lsem):
    nbig = len(big)
    x, y, c = lax.axis_index("x"), lax.axis_index("y"), lax.axis_index("c")
    sibling = (x, y, 1 - c)
    near, far, diag = _partners(x, y, c)
    me, near_id, far_id, diag_id = _chip_of(x, y), _chip_of(*near), _chip_of(*far), _chip_of(*diag)

    def copy(k, src, dst, to):
        return pltpu.make_async_remote_copy(src_ref=src, dst_ref=dst, send_sem=send.at[k], recv_sem=recv.at[k],
                                            device_id=to, device_id_type=MESH)

    def sent(stage, a):
        if a == nbig:
            src, dst, to = ((sm.at[1 - c], r1s, sibling), (r1s, r2s.at[0], (*near, c)), (ps, r2s.at[1], (*far, c)),
                            (osm.at[c], osm.at[c], sibling))[stage]
            return [copy(5 * nbig + stage, src, dst, to)]
        if stage == 0:
            return [copy(5 * a, big[a].at[:, 1 - c], r1[a], sibling)]
        if stage == 1:
            return [copy(5 * a + 1, wire[a].at[near_id], r2[a].at[0], (*near, c)),
                    copy(5 * a + 2, wire[a].at[diag_id], r2[a].at[1], (*near, c))]
        if stage == 2:
            return [copy(5 * a + 3, wire2[a], r2[a].at[2], (*far, c))]
        return [copy(5 * a + 4, outs[a].at[c], outs[a].at[c], sibling)]

    arrays = range(nbig + (sm is not None))

    def start(stage, a):
        for cp in sent(stage, a):
            cp.start()

    def arrived(k, ref):
        copy(k, ref, ref, sibling).wait_recv()

    def loads():
        return [pltpu.make_async_copy(big[a].at[:, c], own[a], lsem.at[a]) for a in range(nbig)]

    def stage0():
        for a in arrays:
            start(0, a)
        for cp in loads():
            cp.start()

    def stage1():
        for a in range(nbig):
            loads()[a].wait()
            arrived(5 * a, r1[a])
            for k in range(N_CHIPS):
                r1[a][k] = own[a][k] + r1[a][k]
                wire[a][k] = r1[a][k].astype(wire[a].dtype)
            start(1, a)
        if sm is not None:
            arrived(5 * nbig, r1s)
            r1s[...] = sm[c] + r1s[...]
            start(1, nbig)

    def stage2():
        for a in range(nbig):
            arrived(5 * a + 1, r2[a].at[0])
            arrived(5 * a + 2, r2[a].at[1])
            r1[a][me] = r1[a][me] + r2[a][0].astype(F32)
            wire2[a][...] = (r1[a][far_id] + r2[a][1].astype(F32)).astype(wire2[a].dtype)
            start(2, a)
        if sm is not None:
            arrived(5 * nbig + 1, r2s.at[0])
            ps[...] = r1s[...] + r2s[0]
            start(2, nbig)

    def stage3():
        for a in range(nbig):
            arrived(5 * a + 3, r2[a].at[2])
            outs[a][c] = r1[a][me] + r2[a][2].astype(F32)
            start(3, a)
        if sm is not None:
            arrived(5 * nbig + 2, r2s.at[1])
            osm[c] = ps[...] + r2s[1]
            start(3, nbig)

    def stage4():
        for a in range(nbig):
            arrived(5 * a + 4, outs[a].at[1 - c])
        if sm is not None:
            arrived(5 * nbig + 3, osm.at[1 - c])
        for stage in range(4):
            for a in arrays:
                for cp in sent(stage, a):
                    cp.wait_send()

    return [stage0, stage1, stage2, stage3, stage4]


def _reduce_buffers(bigs, g_small):
    half = [b.shape[2:] for b in bigs]
    sm_half = None if g_small is None else g_small.shape[1:]
    out_shape = [jax.ShapeDtypeStruct((2,) + h, F32) for h in half]
    small = lambda lead: [] if g_small is None else [pltpu.VMEM(lead + sm_half, F32)]
    if g_small is not None:
        out_shape.append(jax.ShapeDtypeStruct(g_small.shape, F32))
    n_sem = 5 * len(bigs) + 4
    scratch = ([pltpu.VMEM((N_CHIPS,) + h, F32) for h in half] + small(())
               + [pltpu.VMEM((N_CHIPS,) + h, _WIRE) for h in half]
               + [pltpu.VMEM((3,) + h, _WIRE) for h in half] + small((2,))
               + [pltpu.VMEM(h, _WIRE) for h in half] + small(())
               + [pltpu.VMEM((N_CHIPS,) + h, F32) for h in half]
               + [pltpu.SemaphoreType.DMA((n_sem,)), pltpu.SemaphoreType.DMA((n_sem,)),
                  pltpu.SemaphoreType.DMA((len(bigs),))])
    return out_shape, scratch


def _split_reduce_refs(refs, nbig, has_small):
    it = iter(refs)
    take = lambda n: [next(it) for _ in range(n)]
    one = lambda: next(it) if has_small else None
    big, sm = take(nbig), one()
    outs, osm = take(nbig), one()
    r1, r1s, wire, r2, r2s, wire2, ps, own = take(nbig), one(), take(nbig), take(nbig), one(), take(nbig), one(), take(nbig)
    send, recv, lsem = take(3)
    return big, sm, outs, osm, r1, r1s, wire, r2, r2s, wire2, ps, own, send, recv, lsem


def _hosted_reduce_shapes(bigs, g_small):
    red_shape, scratch = _reduce_buffers(bigs, g_small)
    nres = len(red_shape)
    return red_shape, [pltpu.VMEM(r.shape, r.dtype) for r in red_shape] + scratch + [pltpu.SemaphoreType.DMA((nres,))]


def _hosted_reduce(step, n_steps, closing, stage_at, operands, results, scratch, has_small):
    nres = len(results)
    sums, rest, fsem = scratch[:nres], scratch[nres:-1], scratch[-1]
    refs = tuple(operands) + tuple(sums) + tuple(rest)

    def to_results():
        out = [pltpu.make_async_copy(sums[k], results[k], fsem.at[k]) for k in range(nres)]
        for cp in out:
            cp.start()
        for cp in out:
            cp.wait()

    stages = _reduce_protocol(*_split_reduce_refs(refs, nres - has_small, has_small))

    def last_stage():
        stages[-1]()
        to_results()

    for at, stage in zip(stage_at, stages[:-1] + [last_stage]):
        if closing == (at == n_steps):
            pl.when(step == min(at, n_steps - 1))(stage)


def reduce_grads(big, name, parts):
    chips, halves, rows_, cols = big.shape
    sub = jax.ShapeDtypeStruct((chips, halves, rows_ // parts, cols), big.dtype)

    def body(b_ref, o_ref, *scratch):
        refs = [b_ref.at[:, :, s] for s in range(parts)] + [o_ref.at[:, s] for s in range(parts)] + list(scratch)
        for stage in _reduce_protocol(*_split_reduce_refs(refs, parts, False)):
            stage()

    _, scratch = _reduce_buffers([sub] * parts, None)
    return pl.pallas_call(
        body, name=name, out_shape=jax.ShapeDtypeStruct((halves, parts, rows_ // parts, cols), F32),
        in_specs=[pl.BlockSpec(memory_space=pl.ANY)], out_specs=pl.BlockSpec(memory_space=pltpu.VMEM),
        scratch_shapes=scratch, compiler_params=pltpu.CompilerParams(vmem_limit_bytes=VMEM_LIMIT),
    )(big.reshape(chips, halves, parts, rows_ // parts, cols))


def adamw(items, g_pack, ws, ms, vs):
    plan, total = [], 0
    for w, _, _, _ in items:
        rows_, cols = w.shape
        tr = max(t for t in range(8, rows_ + 1, 8) if rows_ % t == 0 and t * cols * 4 <= ADAM_BLOCK_BYTES)
        plan.append((total, rows_ // tr, tr, cols))
        total += rows_ // tr
    nin, n = 4 * len(items), len(ws)

    def body(*refs):
        i = pl.program_id(0)
        small_in = refs[nin:nin + 1 + 3 * n]
        outs = refs[nin + 1 + 3 * n:]

        @pl.when(i == 0)
        def _():
            _adamw_small(small_in[0], *(small_in[1 + k * n:1 + (k + 1) * n] for k in range(3)),
                         *(outs[nin + k * n:nin + (k + 1) * n] for k in range(4)), outs[-1])

        for k, (first, steps, _, _) in enumerate(plan):
            w_ref, g_ref, m_ref, v_ref = refs[4 * k:4 * k + 4]
            go_ref, d_ref, nm_ref, nv_ref = outs[4 * k:4 * k + 4]

            @pl.when((i >= first) & (i < first + steps))
            def _():
                gv = g_ref[...]
                go_ref[...] = gv
                d_ref[...], nm_ref[...], nv_ref[...] = _adam_update(w_ref[...], gv, m_ref[...], v_ref[...])

    specs, shapes = [], []
    for (first, steps, tr, cols), (w, _, _, _) in zip(plan, items):
        spec = pl.BlockSpec((tr, cols), lambda i, first=first, steps=steps: (jnp.clip(i - first, 0, steps - 1), 0))
        specs += [spec] * 4
        shapes += [jax.ShapeDtypeStruct(w.shape, F32)] * 4
    vm = pl.BlockSpec(memory_space=pltpu.VMEM)
    like = [jax.ShapeDtypeStruct(w.shape, F32) for w in ws]
    res = pl.pallas_call(
        body, name="adamw", grid=(total,), out_shape=(*shapes, *like * 4, jax.ShapeDtypeStruct((1, 1), F32)),
        in_specs=specs + [vm] * (1 + 3 * n), out_specs=(*specs, *[vm] * (4 * n + 1)),
        compiler_params=pltpu.CompilerParams(dimension_semantics=("arbitrary",)),
    )(*[a for item in items for a in item], g_pack, *ws, *ms, *vs)
    return [res[4 * k:4 * k + 4] for k in range(len(items))], res[nin:]


def _adam_update(w, g, m, v):
    nm = ADAM_B1 * m + (1.0 - ADAM_B1) * g
    nv = ADAM_B2 * v + (1.0 - ADAM_B2) * (g * g)
    m_hat = nm / (1.0 - ADAM_B1 ** ADAM_STEP)
    v_hat = nv / (1.0 - ADAM_B2 ** ADAM_STEP)
    return (-ADAM_LR) * (m_hat / (jnp.sqrt(v_hat) + ADAM_EPS) + ADAM_WD * w), nm, nv


def _adamw_small(pk, w_refs, m_refs, v_refs, g_out, d_out, nm_out, nv_out, loss_ref):
    nvec = len(SMALL_VECTORS)
    loss_ref[...] = pk[LOSS_ROW:LOSS_ROW + 1, 0:1]
    chip = 2 * lax.axis_index("x") + lax.axis_index("y")
    for k, (name, row, width) in enumerate(SMALL_VECTORS):
        if name == "conv_w":
            g = jnp.concatenate([pk[pl.ds(row + 4 * t + chip, 1), :] for t in range(CONV_K)], axis=0)[None]
        elif width >= 128:
            g = jnp.concatenate([pk[row + r:row + r + 1, :] for r in range(width // 128)], axis=1)
        else:
            g = pk[row:row + 1, 0:width]
        g_out[k][...] = g
        d_out[k][...], nm_out[k][...], nv_out[k][...] = _adam_update(w_refs[k][...], g, m_refs[k][...], v_refs[k][...])
    for k in range(nvec, nvec + len(SMALL_MATRICES)):
        for b in range(LRU_BLOCKS):
            rows_ = pk[GATES_ROW + HEAD * b:GATES_ROW + HEAD * (b + 1), :]
            g = (pltpu.roll(rows_, HEAD, axis=1) if k > nvec else rows_)[:, 0:HEAD]
            g_out[k][0, b] = g
            d_out[k][0, b], nm_out[k][0, b], nv_out[k][0, b] = _adam_update(
                w_refs[k][0, b], g, m_refs[k][0, b], v_refs[k][0, b])


SMALL_VECTORS = (("norm_g", 512, 1024), ("mem_norm_g", 520, 1024), ("conv_w", 528, 512), ("conv_b", 544, 512),
                 ("b_rg", 548, 512), ("b_ig", 552, 512), ("lru_lambda", 556, 512), ("q_norm_g", 560, 64),
                 ("k_norm_g", 561, 64), ("sinks", 562, 4), ("xq_norm_g", 563, 64), ("xk_norm_g", 564, 64),
                 ("out_norm_g", 565, 1024))
LOSS_ROW = 573
SMALL_MATRICES = ("w_rg", "w_ig")
GATES_ROW = 0
SMALL_ROWS = 640


def _rope_tables(seq):
    pos = np.arange(seq, dtype=np.float32)
    inv_freq = (np.float32(ROPE_THETA) ** (-(np.arange(0, ROPE_DIM, 2, dtype=np.float32) / np.float32(ROPE_DIM)))
                ).astype(np.float32)
    ang = (pos[:, None] * inv_freq[None, :]).astype(np.float32)
    cos, sin = np.cos(ang).astype(np.float32), np.sin(ang).astype(np.float32)
    z = lambda n: np.zeros((seq, n), np.float32)
    c64 = np.concatenate([cos, cos, np.ones((seq, HEAD - ROPE_DIM), np.float32)], axis=1)
    s1_64 = np.concatenate([-sin, z(HEAD - 8)], axis=1)
    s2_64 = np.concatenate([z(8), sin, z(HEAD - ROPE_DIM)], axis=1)
    return tuple(jnp.asarray(np.concatenate([t, t], axis=1)) for t in (c64, s1_64, s2_64))


def kernel(x, mem, norm_g, mem_norm_g, w_in, conv_w, conv_b, w_rg, b_rg, w_ig, b_ig, lru_lambda, q_norm_g, k_norm_g, sinks, w_mem_kv, xq_norm_g, xk_norm_g, out_norm_g, w_out, loss_target, m_norm_g, m_mem_norm_g, m_w_in, m_conv_w, m_conv_b, m_w_rg, m_b_rg, m_w_ig, m_b_ig, m_lru_lambda, m_q_norm_g, m_k_norm_g, m_sinks, m_w_mem_kv, m_xq_norm_g, m_xk_norm_g, m_out_norm_g, m_w_out, v_norm_g, v_mem_norm_g, v_w_in, v_conv_w, v_conv_b, v_w_rg, v_b_rg, v_w_ig, v_b_ig, v_lru_lambda, v_q_norm_g, v_k_norm_g, v_sinks, v_w_mem_kv, v_xq_norm_g, v_xk_norm_g, v_out_norm_g, v_w_out):
    seq = x.shape[1]
    xs, tgt, mems = x[0], loss_target[0], mem[0]

    win_t, wout, wkv, cw, wg, gains, km, vm = gather_weights(
        w_in[0].T, w_out[0], w_mem_kv[0], conv_w, w_rg, w_ig, (q_norm_g, k_norm_g, xq_norm_g, xk_norm_g), mems,
        mem_norm_g)
    rc, rs1, rs2 = _rope_tables(seq)
    proj, ya, yb, yc, ycat, xn, dout, pswa, pmem, psink, gates, a_all, loss8 = layer_fwd(
        xs, tgt, rc, rs1, rs2, norm_g, win_t, cw, conv_b, wg, b_rg, b_ig, lru_lambda, gains, sinks, km, vm,
        out_norm_g, wout)
    (gx, dproj, g_wg, dkm, dvm, g_ng, g_og, g_cb, g_brg, g_big, g_lam, g_cw, g_qn, g_kn, g_xqn, g_sink) = layer_bwd(
        xs, dout, proj, ya, yb, yc, pswa, pmem, psink, gates, a_all, rc, rs1, rs2, norm_g, win_t, cw, wg, lru_lambda,
        gains, km, vm, out_norm_g, wout)
    g_win_t, r_out, r_kv, r_small = weight_grads(
        ycat, dout, dproj, xn, (mems, mem_norm_g, wkv, gains, dkm, dvm, g_wg, loss8), dict(
            norm_g=g_ng, conv_w=g_cw, conv_b=g_cb, b_rg=g_brg, b_ig=g_big, lru_lambda=g_lam, q_norm_g=g_qn,
            k_norm_g=g_kn, sinks=g_sink, xq_norm_g=g_xqn, out_norm_g=g_og), (0, 1, 4, 7, 8), (4, 5, 8, 10, 12))
    r_in = reduce_grads(g_win_t.reshape(N_CHIPS, 2, D_IN // 8, D_MODEL), "reduce_w_in", 6)

    r_small = r_small.reshape(SMALL_ROWS, 128)
    grads = {}
    weights = dict(norm_g=norm_g, mem_norm_g=mem_norm_g, w_in=w_in, conv_w=conv_w, conv_b=conv_b, w_rg=w_rg, b_rg=b_rg,
                   w_ig=w_ig, b_ig=b_ig, lru_lambda=lru_lambda, q_norm_g=q_norm_g, k_norm_g=k_norm_g, sinks=sinks,
                   w_mem_kv=w_mem_kv, xq_norm_g=xq_norm_g, xk_norm_g=xk_norm_g, out_norm_g=out_norm_g, w_out=w_out)
    ms = dict(norm_g=m_norm_g, mem_norm_g=m_mem_norm_g, w_in=m_w_in, conv_w=m_conv_w, conv_b=m_conv_b, w_rg=m_w_rg,
              b_rg=m_b_rg, w_ig=m_w_ig, b_ig=m_b_ig, lru_lambda=m_lru_lambda, q_norm_g=m_q_norm_g, k_norm_g=m_k_norm_g,
              sinks=m_sinks, w_mem_kv=m_w_mem_kv, xq_norm_g=m_xq_norm_g, xk_norm_g=m_xk_norm_g,
              out_norm_g=m_out_norm_g, w_out=m_w_out)
    vs = dict(norm_g=v_norm_g, mem_norm_g=v_mem_norm_g, w_in=v_w_in, conv_w=v_conv_w, conv_b=v_conv_b, w_rg=v_w_rg,
              b_rg=v_b_rg, w_ig=v_w_ig, b_ig=v_b_ig, lru_lambda=v_lru_lambda, q_norm_g=v_q_norm_g, k_norm_g=v_k_norm_g,
              sinks=v_sinks, w_mem_kv=v_w_mem_kv, xq_norm_g=v_xq_norm_g, xk_norm_g=v_xk_norm_g,
              out_norm_g=v_out_norm_g, w_out=v_w_out)

    delta, new_m, new_v = {}, {}, {}
    small_names = [n for n, _, _ in SMALL_VECTORS] + list(SMALL_MATRICES)
    (res_in, res_out, res_kv), res = adamw(
        [(w_in[0].T, r_in.reshape(D_IN // 4, D_MODEL), m_w_in[0].T, v_w_in[0].T),
         (w_out[0], r_out.reshape(D_MODEL // 4, D_MODEL), m_w_out[0], v_w_out[0]),
         (w_mem_kv[0], r_kv.reshape(D_MODEL // 4, 2 * XATT_W), m_w_mem_kv[0], v_w_mem_kv[0])],
        r_small, [weights[n] for n in small_names], [ms[n] for n in small_names], [vs[n] for n in small_names])
    grads["w_in"], delta["w_in"], new_m["w_in"], new_v["w_in"] = (r.T[None] for r in res_in)
    grads["w_out"], delta["w_out"], new_m["w_out"], new_v["w_out"] = (r[None] for r in res_out)
    grads["w_mem_kv"], delta["w_mem_kv"], new_m["w_mem_kv"], new_v["w_mem_kv"] = (r[None] for r in res_kv)
    nall = len(small_names)
    for k, into in enumerate((grads, delta, new_m, new_v)):
        into.update(zip(small_names, res[k * nall:(k + 1) * nall]))
    loss = res[-1].reshape(())

    order = ("norm_g", "mem_norm_g", "w_in", "conv_w", "conv_b", "w_rg", "b_rg", "w_ig", "b_ig", "lru_lambda",
             "q_norm_g", "k_norm_g", "sinks", "w_mem_kv", "xq_norm_g", "xk_norm_g", "out_norm_g", "w_out")
    return (loss, gx[None], *[grads[n] for n in order], *[delta[n] for n in order], *[new_m[n] for n in order],
            *[new_v[n] for n in order])
```

```python
import jax
import jax.numpy as jnp
import numpy as np
from jax import lax
from jax.experimental import pallas as pl
from jax.experimental.pallas import tpu as pltpu

F32 = jnp.float32
_MXU = jnp.bfloat16
_WIRE = jnp.bfloat16

D_MODEL = 1024
MEM_LEN = 256
HEAD = 64
LRU_W = 512
LRU_BLOCKS = 8
CONV_K = 4
LRU_C = 8.0
SWA_W = 256
KV_W = 128
XATT_W = 256
BLOCK = 128
D_IN = 2304
ROPE_THETA = 500000.0
ROPE_DIM = 16
EPS = 1e-6
NEG_INF = -1e30
C_LRUX, C_LRUG, C_SQ, C_SK, C_SV, C_SWAG, C_XQ, C_XG = 0, 512, 1024, 1280, 1408, 1536, 1792, 2048
G_Q, G_K, G_XQ, G_XK, GAINS_W = 0, 128, 256, 512, 768

ADAM_LR, ADAM_B1, ADAM_B2, ADAM_EPS, ADAM_WD, ADAM_STEP = 0.001, 0.9, 0.999, 1e-08, 0.01, 10

N_CHIPS = 4
ROW_TILE = 256
VMEM_LIMIT = 56 * 1024 * 1024
ADAM_BLOCK_BYTES = 640 * 1024
MESH = pl.DeviceIdType.MESH


def _mm(a, b):
    return jnp.dot(a.astype(_MXU), b.astype(_MXU), preferred_element_type=F32)


def _mm_nt(a, b):
    return lax.dot_general(a.astype(_MXU), b.astype(_MXU), (((1,), (1,)), ((), ())), preferred_element_type=F32)


def _mm_tn(a, b):
    return lax.dot_general(a.astype(_MXU), b.astype(_MXU), (((0,), (0,)), ((), ())), preferred_element_type=F32)


def _group_matrix(width):
    r = lax.shift_right_logical(lax.broadcasted_iota(jnp.int32, (width, width), 0), 6)
    c = lax.shift_right_logical(lax.broadcasted_iota(jnp.int32, (width, width), 1), 6)
    return (r == c).astype(_MXU)


def _seg_mean(x, gm):
    return jnp.dot(x.astype(_MXU), gm, preferred_element_type=F32) * (1.0 / HEAD)


def _row_mean(x):
    return jnp.mean(x, axis=-1, keepdims=True)


def _col_sum(x):
    return jnp.sum(x, axis=0, keepdims=True)


def _sigmoid(x):
    return jax.nn.sigmoid(x)


def _softplus(z):
    e = jnp.exp(-jnp.abs(z))
    u = 1.0 + e
    log1p_e = jnp.where(u == 1.0, e, jnp.log(u) * (e / (u - 1.0)))
    return jnp.maximum(z, 0.0) + log1p_e


def _rope(t, c, s1, s2):
    return t * c + pltpu.roll(t, 120, 1) * s1 + pltpu.roll(t, 8, 1) * s2


def _rope_bwd(d, c, s1, s2):
    return d * c + pltpu.roll(d * s1, 8, 1) + pltpu.roll(d * s2, 120, 1)


def _fold_heads(v):
    out = v
    for k in range(1, v.shape[1] // HEAD):
        out = out + pltpu.roll(v, HEAD * k, 1)
    return out


def _lane_mask(width, lo, hi):
    lane = lax.broadcasted_iota(jnp.int32, (1, width), 1)
    return ((lane >= lo) & (lane < hi)).astype(F32)


def _swa_mask(first_block):
    qi = lax.broadcasted_iota(jnp.int32, (BLOCK, 2 * BLOCK), 0)
    kj = lax.broadcasted_iota(jnp.int32, (BLOCK, 2 * BLOCK), 1)
    rel = qi + BLOCK - kj
    ok = (rel >= 0) & (rel < BLOCK)
    return ok & (jnp.logical_not(first_block) | (kj >= BLOCK))


def _place_kv(t, scale):
    lo = t * (_lane_mask(KV_W, 0, HEAD) * scale)
    hi = t * (_lane_mask(KV_W, HEAD, KV_W) * scale)
    return [a.astype(_MXU) for a in (lo, pltpu.roll(lo, HEAD, 1), pltpu.roll(hi, HEAD, 1), hi)]


def _unplace_kv(d):
    return (_lane_mask(KV_W, 0, HEAD) * (d[0] + pltpu.roll(d[1], HEAD, 1))
            + _lane_mask(KV_W, HEAD, KV_W) * (d[3] + pltpu.roll(d[2], HEAD, 1)))


def _swa_probs(qh, ka, mask, sink):
    s = _mm_nt(qh, ka)
    s = jnp.where(mask, s, NEG_INF)
    m = jnp.maximum(jnp.max(s, axis=-1, keepdims=True), sink)
    p = jnp.exp(s - m)
    esink = jnp.exp(sink - m)
    inv = 1.0 / (jnp.sum(p, axis=-1, keepdims=True) + esink)
    return p * inv, esink * inv


def _mem_probs(s_all):
    out = []
    for j in range(4):
        s = s_all[:, MEM_LEN * j:MEM_LEN * (j + 1)]
        p = jnp.exp(s - jnp.max(s, axis=-1, keepdims=True))
        out.append(p * (1.0 / jnp.sum(p, axis=-1, keepdims=True)))
    return out


def _head_rows(t, scale):
    return jnp.concatenate([t * (_lane_mask(XATT_W, HEAD * j, HEAD * (j + 1)) * scale) for j in range(4)], axis=0)


def _lru_gates(xc, wg_ref, brg, big, lam):
    p0 = _mm(xc[:, :256], wg_ref[0])
    p1 = _mm(xc[:, 256:], wg_ref[1])
    rg = _sigmoid(jnp.concatenate([p0[:, :256], p1[:, :256]], axis=1) + brg)
    ig = _sigmoid(jnp.concatenate([p0[:, 256:], p1[:, 256:]], axis=1) + big)
    sp = _softplus(-lam)
    la = (-LRU_C) * rg * sp
    a = jnp.exp(la)
    th = jnp.tanh(la)
    one_minus_a2 = (-2.0 * th) / (1.0 - th)
    return rg, ig, sp, a, jnp.sqrt(one_minus_a2)


def _const_spec(shape, single=False):
    zeros = (0,) * len(shape)
    if single:
        return pl.BlockSpec(shape, lambda i: zeros, pipeline_mode=pl.Buffered(1))
    return pl.BlockSpec(shape, lambda i: zeros)


def _chip_of(x, y):
    return 2 * x + y


def _partners(x, y, c):
    north = c == 1
    near = (jnp.where(north, 1 - x, x), jnp.where(north, y, 1 - y))
    far = (jnp.where(north, x, 1 - x), jnp.where(north, 1 - y, y))
    return near, far, (1 - x, 1 - y)


def gather_weights(win_t, wout, wkv, conv_w, w_rg, w_ig, head_gains, mem, mem_g):
    arrs = (win_t, wout, wkv)
    n = len(arrs)
    pieces = [(a, 0, arr.shape[0] // 2) for a, arr in enumerate(arrs)]
    npc = len(pieces)

    def body(a0, a1, a2, cw_in, wrg_ref, wig_ref, q_ref, k_ref, xq_ref, xk_ref, mem_ref, mg_ref,
             o0, o1, o2, cw_out, wg_ref, gn_ref, km_ref, vm_ref, s0, s1, s2, cw, ocw, send, recv, lsem):
        ins, outs = (s0, s1, s2), (o0, o1, o2)
        for src, dst in zip((a0, a1, a2), ins):
            dst[...] = src[...].astype(dst.dtype)
        cw[...] = jnp.zeros(cw.shape, F32)
        cw[0:CONV_K, :] = cw_in[0]
        x, y, c = lax.axis_index("x"), lax.axis_index("y"), lax.axis_index("c")
        sibling = (x, y, 1 - c)
        near, far, diag = _partners(x, y, c)
        chips = [near, far, diag]
        me = _chip_of(x, y)

        def landed(p, chip, half):
            a, off, rows_ = pieces[p]
            r = ins[a].shape[0]
            return outs[a].at[pl.ds(pl.multiple_of(chip * r + half * (r // 2) + off, 16), rows_)]

        def mine(p):
            a, off, rows_ = pieces[p]
            return ins[a].at[pl.ds(pl.multiple_of(c * (ins[a].shape[0] // 2) + off, 16), rows_)]

        def copy(k, src, dst, to):
            return pltpu.make_async_remote_copy(src_ref=src, dst_ref=dst, send_sem=send.at[k], recv_sem=recv.at[k],
                                                device_id=to, device_id_type=MESH)

        def cw_rows(chip):
            return ocw.at[pl.ds(pl.multiple_of(chip * 8, 8), 8)]

        locals_ = []
        for a in range(n):
            r = ins[a].shape[0]
            locals_.append(pltpu.make_async_copy(ins[a], outs[a].at[pl.ds(pl.multiple_of(me * r, 16), r)], lsem.at[a]))
        locals_.append(pltpu.make_async_copy(cw, cw_rows(me), lsem.at[n]))
        for cp in locals_:
            cp.start()

        sent = []
        for p in range(npc):
            for j in range(2):
                sent.append(copy(p * 6 + j, mine(p), landed(p, me, c), (*chips[j], c)))
        for j, chip in enumerate(chips):
            sent.append(copy(npc * 6 + j, cw, cw_rows(me), (*chip, c)))
        for cp in sent:
            cp.start()

        gn_ref[...] = jnp.concatenate([q_ref[...]] * 2 + [k_ref[...]] * 2 + [xq_ref[...]] * 4 + [xk_ref[...]] * 4,
                                      axis=1)
        zeros = lambda lanes: [jnp.zeros((HEAD, lanes), F32)] if lanes else []
        for h in range(2):
            for b in range(4):
                row = []
                for w_ref in (wrg_ref, wig_ref):
                    row += zeros(HEAD * b) + [w_ref[0, 4 * h + b]] + zeros(HEAD * (3 - b))
                wg_ref[h, HEAD * b:HEAD * (b + 1), :] = jnp.concatenate(row, axis=1).astype(wg_ref.dtype)

        for j in range(3):
            for p in range(npc):
                got = landed(p, _chip_of(*chips[j]), c)
                copy(p * 6 + j, got, got, sibling).wait_recv()
                if j == 0:
                    sent.append(copy(p * 6 + 2, got, got, (*far, c)))
                    sent[-1].start()
                sent.append(copy(p * 6 + 3 + j, got, got, sibling))
                sent[-1].start()
        for p in range(npc):
            for j in range(3):
                got = landed(p, _chip_of(*chips[(1, 0, 2)[j]]), 1 - c)
                copy(p * 6 + 3 + j, got, got, sibling).wait_recv()
        for j, chip in enumerate(chips):
            got = cw_rows(_chip_of(*chip))
            copy(npc * 6 + j, got, got, (*chip, c)).wait_recv()
        for cp in sent:
            cp.wait_send()
        for cp in locals_:
            cp.wait()
        for chip in range(N_CHIPS):
            cw_out[:, 128 * chip:128 * (chip + 1)] = ocw[8 * chip:8 * chip + CONV_K, :]

        mem_v = mem_ref[...]
        mn = mem_v * lax.rsqrt(_row_mean(mem_v * mem_v) + EPS) * mg_ref[...]
        mkv = _mm(mn, o2[...])
        kpre = mkv[:, :XATT_W]
        km = kpre * lax.rsqrt(_seg_mean(kpre * kpre, _group_matrix(XATT_W)) + EPS) * gn_ref[:, G_XK:GAINS_W]
        km_ref[...] = _head_rows(km, 0.125).astype(km_ref.dtype)
        vm_ref[...] = _head_rows(mkv[:, XATT_W:], 1.0).astype(vm_ref.dtype)

    vm = pl.BlockSpec(memory_space=pltpu.VMEM)
    hbm = pl.BlockSpec(memory_space=pl.ANY)
    head_rows = jax.ShapeDtypeStruct((4 * MEM_LEN, XATT_W), _MXU)
    out_shape = tuple(jax.ShapeDtypeStruct((N_CHIPS * a.shape[0],) + a.shape[1:], _MXU) for a in arrs) + (
        jax.ShapeDtypeStruct((CONV_K, LRU_W), F32), jax.ShapeDtypeStruct((2, 256, 512), _MXU),
        jax.ShapeDtypeStruct((1, GAINS_W), F32), head_rows, head_rows)
    n_rdma = npc * 6 + 3
    return pl.pallas_call(
        body, name="gather_weights", out_shape=out_shape,
        in_specs=[vm] * 12, out_specs=(hbm, hbm, vm, vm, vm, vm, vm, vm),
        scratch_shapes=[pltpu.VMEM(a.shape, _MXU) for a in arrs] + [
            pltpu.VMEM((8, 128), F32), pltpu.VMEM((N_CHIPS * 8, 128), F32),
            pltpu.SemaphoreType.DMA((n_rdma,)), pltpu.SemaphoreType.DMA((n_rdma,)), pltpu.SemaphoreType.DMA((n + 1,))],
        compiler_params=pltpu.CompilerParams(vmem_limit_bytes=VMEM_LIMIT),
    )(win_t, wout, wkv, conv_w, w_rg, w_ig, *head_gains, mem, mem_g)


def _mem_bwd_and_pack(mem_ref, g_ref, w_ref, gn_ref, dkm_ref, dvm_ref, gg_ref, loss_ref, vectors, gw_ref, pk_ref):
    first_row = {name: (row, width) for name, row, width in SMALL_VECTORS}
    half_rows = SMALL_ROWS // 2
    pk_ref[...] = jnp.zeros(pk_ref.shape, F32)

    def rows_at(at, n):
        assert at // half_rows == (at + n - 1) // half_rows
        return at // half_rows, slice(at % half_rows, at % half_rows + n), slice(None)

    def put(name, src):
        row, width = first_row[name]
        per_row = 1 if width < 128 else src.shape[1] // 128
        for t in range(src.shape[0]):
            for r in range(per_row):
                pk_ref[rows_at(row + per_row * t + r, 1)] = src[t:t + 1, 128 * r:128 * (r + 1)]

    for name, ref in vectors.items():
        put(name, ref)
    pk_ref[rows_at(LOSS_ROW, 1)] = loss_ref[0:1, :]
    upper = lax.broadcasted_iota(jnp.int32, (HEAD, 128), 1) >= HEAD
    for h in range(2):
        for b in range(4):
            rg = gg_ref[h, HEAD * b:HEAD * (b + 1), 128 * (b // 2):128 * (b // 2 + 1)]
            ig = gg_ref[h, HEAD * b:HEAD * (b + 1), 256 + 128 * (b // 2):256 + 128 * (b // 2 + 1)]
            if b % 2:
                rg = pltpu.roll(rg, HEAD, axis=1)
            else:
                ig = pltpu.roll(ig, HEAD, axis=1)
            pk_ref[rows_at(GATES_ROW + HEAD * (4 * h + b), HEAD)] = jnp.where(upper, ig, rg)

    mem_v = mem_ref[...]
    mh = mem_v * lax.rsqrt(_row_mean(mem_v * mem_v) + EPS)
    mn = mh * g_ref[...]
    mkv = _mm(mn, w_ref[...])
    kpre = mkv[:, :XATT_W]
    gm = _group_matrix(XATT_W)
    rk = lax.rsqrt(_seg_mean(kpre * kpre, gm) + EPS)
    kn = kpre * rk
    dk = jnp.zeros((MEM_LEN, XATT_W), F32)
    dv = jnp.zeros((MEM_LEN, XATT_W), F32)
    for j in range(4):
        mj = _lane_mask(XATT_W, HEAD * j, HEAD * (j + 1))
        dk = dk + dkm_ref[:, MEM_LEN * j:MEM_LEN * (j + 1)].T * (mj * 0.125)
        dv = dv + dvm_ref[:, MEM_LEN * j:MEM_LEN * (j + 1)].T * mj
    put("xk_norm_g", _fold_heads(_col_sum(dk * kn)))
    dkn = dk * gn_ref[:, G_XK:GAINS_W]
    dkpre = rk * (dkn - kn * _seg_mean(dkn * kn, gm))
    dmkv = jnp.concatenate([dkpre, dv], axis=1)
    gw_ref[...] = _mm_tn(mn, dmkv).reshape(gw_ref.shape)
    dmn = _mm_nt(dmkv, w_ref[...])
    put("mem_norm_g", _col_sum(dmn * mh))


def layer_fwd(x, tgt, rc, rs1, rs2, ng, win_t, cw, cb, wg, brg, big, lam, gains, sinks, km, vm, og, wout):
    seq = x.shape[0]
    tm = min(ROW_TILE, seq)
    nt = seq // tm
    nb = tm // BLOCK

    def body(x_ref, t_ref, c_ref, s1_ref, s2_ref, ng_ref, win_ref, cw_ref, cb_ref, wg_ref, brg_ref, big_ref, lam_ref,
             gn_ref, sink_ref, km_ref, vm_ref, og_ref, wout_ref,
             proj_ref, ya_ref, yb_ref, yc_ref, ycat_ref, xn_ref, dout_ref, pswa_ref, pmem_ref, psink_ref, gates_ref,
             a_ref, loss_ref,
             ext_ref, b_scr, hc_ref, kp_ref, vp_ref, lacc_ref):
        i = pl.program_id(0)

        @pl.when(i == 0)
        def _():
            ext_ref[0:8, :] = jnp.zeros((8, LRU_W), F32)
            hc_ref[...] = jnp.zeros_like(hc_ref)
            kp_ref[...] = jnp.zeros_like(kp_ref)
            vp_ref[...] = jnp.zeros_like(vp_ref)
            lacc_ref[...] = jnp.zeros_like(lacc_ref)

        xv = x_ref[...]
        xn = (xv * lax.rsqrt(_row_mean(xv * xv) + EPS) * ng_ref[...]).astype(_MXU)
        xn_ref[...] = xn.astype(xn_ref.dtype)
        proj_ref[...] = _mm_nt(xn, win_ref[...])

        u = proj_ref[:, C_LRUX:C_LRUX + LRU_W]
        ext_ref[8:8 + tm, :] = u
        xc = cb_ref[...]
        for k in range(CONV_K):
            xc = xc + cw_ref[k:k + 1, :] * ext_ref[pl.ds(5 + k, tm), :]
        ext_ref[0:8, :] = u[tm - 8:tm, :]
        rg, ig, sp, a, sq = _lru_gates(xc, wg_ref, brg_ref[...], big_ref[...], lam_ref[...])
        for k, t in enumerate((xc, rg, ig, sq)):
            gates_ref[:, LRU_W * k:LRU_W * (k + 1)] = t.astype(gates_ref.dtype)
        a_ref[...] = a
        b_scr[...] = sq * (ig * xc)
        row8 = lax.broadcasted_iota(jnp.int32, (8, LRU_W), 0)

        def scan_step(g, carry):
            r0 = pl.multiple_of(g * 8, 8)
            av = a_ref[pl.ds(r0, 8), :]
            bv = b_scr[pl.ds(r0, 8), :]
            for d in (1, 2, 4):
                a_sh = jnp.where(row8 >= d, pltpu.roll(av, d, 0), 1.0)
                b_sh = jnp.where(row8 >= d, pltpu.roll(bv, d, 0), 0.0)
                bv = bv + av * b_sh
                av = av * a_sh
            hv = bv + av * carry
            ya_ref[pl.ds(r0, 8), :] = hv
            return hv[7:8, :]

        hc_ref[0:1, :] = lax.fori_loop(0, tm // 8, scan_step, hc_ref[0:1, :], unroll=True)

        gm128 = _group_matrix(KV_W)
        cv, s1v, s2v = c_ref[...], s1_ref[...], s2_ref[...]

        def head_norm_rope(t, g):
            n = t * lax.rsqrt(_seg_mean(t * t, gm128) + EPS)
            return _rope(n * g, cv, s1v, s2v)

        qs_ = (head_norm_rope(proj_ref[:, C_SQ:C_SQ + 128], gn_ref[:, G_Q:G_K]).astype(_MXU),
               head_norm_rope(proj_ref[:, C_SQ + 128:C_SQ + 256], gn_ref[:, G_Q:G_K]).astype(_MXU))
        kr = head_norm_rope(proj_ref[:, C_SK:C_SK + KV_W], gn_ref[:, G_K:G_XQ])
        sv = proj_ref[:, C_SV:C_SV + KV_W]
        ka = _place_kv(jnp.concatenate([kp_ref[...], kr], axis=0), 0.125)
        va = _place_kv(jnp.concatenate([vp_ref[...], sv], axis=0), 1.0)
        kp_ref[...] = kr[tm - BLOCK:tm, :]
        vp_ref[...] = sv[tm - BLOCK:tm, :]
        lane128 = lax.broadcasted_iota(jnp.int32, (1, 128), 1)
        for b in range(nb):
            mask = _swa_mask((i == 0) & (b == 0)) if b == 0 else _swa_mask(False)
            band = slice(BLOCK * b, BLOCK * b + 2 * BLOCK)
            blk = slice(BLOCK * b, BLOCK * (b + 1))
            psink = jnp.zeros((BLOCK, 128), F32)
            for j in range(4):
                p, pk = _swa_probs(qs_[j // 2][blk], ka[j][band], mask, sink_ref[0, j])
                pswa_ref[blk, 2 * BLOCK * j:2 * BLOCK * (j + 1)] = p.astype(pswa_ref.dtype)
                psink = jnp.where(lane128 == j, pk, psink)
            psink_ref[blk, :] = psink
            for h in range(2):
                yb_ref[blk, KV_W * h:KV_W * (h + 1)] = _mm(
                    pswa_ref[blk, 4 * BLOCK * h:4 * BLOCK * (h + 1)],
                    jnp.concatenate([va[2 * h][band], va[2 * h + 1][band]], axis=0))

        gm256 = _group_matrix(XATT_W)
        xq = proj_ref[:, C_XQ:C_XQ + XATT_W]
        qx = xq * lax.rsqrt(_seg_mean(xq * xq, gm256) + EPS) * gn_ref[:, G_XQ:G_XK]
        pm = _mem_probs(_mm_nt(qx, km_ref[...]))
        for j in range(4):
            pmem_ref[:, MEM_LEN * j:MEM_LEN * (j + 1)] = pm[j].astype(pmem_ref.dtype)
        yc = _mm(pmem_ref[...], vm_ref[...])
        yc_ref[...] = yc

        def gated(y, g, gate):
            return y * lax.rsqrt(_row_mean(y * y) + EPS) * g * (gate * _sigmoid(gate))

        ogv = og_ref[...]
        za = gated(ya_ref[...], ogv[:, :512], proj_ref[:, C_LRUG:C_LRUG + LRU_W])
        zb = gated(yb_ref[...], ogv[:, 512:768], proj_ref[:, C_SWAG:C_SWAG + SWA_W])
        zc = gated(yc, ogv[:, 768:], proj_ref[:, C_XG:C_XG + XATT_W])
        ycat_ref[:, 0:512] = za.astype(ycat_ref.dtype)
        ycat_ref[:, 512:768] = zb.astype(ycat_ref.dtype)
        ycat_ref[:, 768:1024] = zc.astype(ycat_ref.dtype)
        out = xv + _mm(ycat_ref[...], wout_ref[...])
        err = out - t_ref[...]
        dout_ref[...] = (err * (1.0 / D_MODEL)).astype(dout_ref.dtype)
        lacc_ref[...] = lacc_ref[...] + (0.5 / D_MODEL) * jnp.sum(err * err)

        @pl.when(i == nt - 1)
        def _():
            loss_ref[...] = lacc_ref[...]

    def rows(ncol):
        return pl.BlockSpec((tm, ncol), lambda i: (i, 0))

    in_specs = [rows(D_MODEL), rows(D_MODEL), rows(128), rows(128), rows(128),
                _const_spec((1, D_MODEL)), _const_spec((D_IN, D_MODEL), True), _const_spec((CONV_K, LRU_W)),
                _const_spec((1, LRU_W)), _const_spec((2, 256, 512), True), _const_spec((1, LRU_W)),
                _const_spec((1, LRU_W)), _const_spec((1, LRU_W)), _const_spec((1, GAINS_W)), pl.BlockSpec(memory_space=pltpu.SMEM),
                _const_spec((4 * MEM_LEN, XATT_W), True), _const_spec((4 * MEM_LEN, XATT_W), True),
                _const_spec((1, D_MODEL)), _const_spec((D_MODEL, D_MODEL), True)]
    out_shape = (jax.ShapeDtypeStruct((seq, D_IN), F32), jax.ShapeDtypeStruct((seq, LRU_W), F32),
                 jax.ShapeDtypeStruct((seq, SWA_W), F32), jax.ShapeDtypeStruct((seq, XATT_W), F32),
                 jax.ShapeDtypeStruct((seq, D_MODEL), _MXU), jax.ShapeDtypeStruct((seq, D_MODEL), _MXU),
                 jax.ShapeDtypeStruct((seq, D_MODEL), _MXU), jax.ShapeDtypeStruct((seq, 4 * 2 * BLOCK), _MXU),
                 jax.ShapeDtypeStruct((seq, 4 * MEM_LEN), _MXU), jax.ShapeDtypeStruct((seq, 128), F32),
                 jax.ShapeDtypeStruct((seq, 4 * LRU_W), _MXU), jax.ShapeDtypeStruct((seq, LRU_W), F32),
                 jax.ShapeDtypeStruct((8, 128), F32))
    out_specs = (rows(D_IN), rows(LRU_W), rows(SWA_W), rows(XATT_W), rows(D_MODEL), rows(D_MODEL), rows(D_MODEL),
                 rows(4 * 2 * BLOCK), rows(4 * MEM_LEN), rows(128), rows(4 * LRU_W), rows(LRU_W),
                 _const_spec((8, 128)))
    scratch = [pltpu.VMEM((tm + 8, LRU_W), F32), pltpu.VMEM((tm, LRU_W), F32),
               pltpu.VMEM((8, LRU_W), F32), pltpu.VMEM((BLOCK, KV_W), F32), pltpu.VMEM((BLOCK, KV_W), F32),
               pltpu.VMEM((8, 128), F32)]
    return pl.pallas_call(
        body, name="layer_fwd", grid=(nt,), out_shape=out_shape, in_specs=in_specs, out_specs=out_specs,
        scratch_shapes=scratch,
        compiler_params=pltpu.CompilerParams(dimension_semantics=("arbitrary",), vmem_limit_bytes=VMEM_LIMIT),
    )(x, tgt, rc, rs1, rs2, ng, win_t, cw, cb, wg, brg, big, lam, gains, sinks, km, vm, og, wout)


def weight_grads(ycat, dout, dproj, xn, mem_operands, vectors, early_at, late_at):
    seq, ncol = xn.shape
    blk = 256
    n_out, n_in = ycat.shape[1] // blk, dproj.shape[1] // blk
    assert n_out == N_CHIPS and late_at[0] >= n_out
    g_out = jax.ShapeDtypeStruct((N_CHIPS, 2, blk // 2, dout.shape[1]), F32)
    g_kv = jax.ShapeDtypeStruct((N_CHIPS, 2, D_MODEL // 8, 2 * XATT_W), F32)
    g_small = jax.ShapeDtypeStruct((2, SMALL_ROWS // 2, 128), F32)
    shape_e, scratch_e = _hosted_reduce_shapes([g_kv], g_small)
    shape_l, scratch_l = _hosted_reduce_shapes([g_out], None)
    n_mem, names = len(mem_operands), tuple(vectors)

    def body(l1_ref, r1_ref, l2_ref, r2_ref, *refs):
        mem_refs, vec_refs = refs[:n_mem], refs[n_mem:n_mem + len(names)]
        o_ref, sum_out, sum_kv, sum_sm, gout_scr, gkv_scr, pack_scr, *scratch = refs[n_mem + len(names):]
        j = pl.program_id(0)

        @pl.when(j == 0)
        def _():
            _mem_bwd_and_pack(*mem_refs, dict(zip(names, vec_refs)), gkv_scr, pack_scr)

        def reduce_stages(closing):
            _hosted_reduce(j, n_out + n_in, closing, early_at, (gkv_scr, pack_scr), (sum_kv, sum_sm),
                           scratch[:len(scratch_e)], True)
            _hosted_reduce(j, n_out + n_in, closing, late_at, (gout_scr,), (sum_out,), scratch[len(scratch_e):], False)

        reduce_stages(False)

        @pl.when(j < n_out)
        def _():
            gout_scr[j] = _mm_tn(l1_ref[...], r1_ref[...]).reshape(g_out.shape[1:])

        @pl.when(j >= n_out)
        def _():
            o_ref[...] = _mm_tn(l2_ref[...], r2_ref[...])

        reduce_stages(True)

    vm = pl.BlockSpec(memory_space=pltpu.VMEM)
    hbm = pl.BlockSpec(memory_space=pl.ANY)
    return pl.pallas_call(
        body, name="weight_grads", grid=(n_out + n_in,),
        out_shape=(jax.ShapeDtypeStruct((dproj.shape[1], ncol), F32), *shape_l, *shape_e),
        in_specs=[pl.BlockSpec((seq, blk), lambda j: (0, jnp.minimum(j, n_out - 1))), _const_spec(dout.shape, True),
                  pl.BlockSpec((seq, blk), lambda j: (0, jnp.maximum(j - n_out, 0))), _const_spec(xn.shape, True)]
        + [vm] * (n_mem + len(names)),
        out_specs=(pl.BlockSpec((blk, ncol), lambda j: (jnp.maximum(j - n_out, 0), 0)), hbm, hbm, hbm),
        scratch_shapes=[pltpu.VMEM(s.shape, F32) for s in (g_out, g_kv, g_small)] + scratch_e + scratch_l,
        compiler_params=pltpu.CompilerParams(dimension_semantics=("arbitrary",), vmem_limit_bytes=VMEM_LIMIT),
    )(ycat, dout, dproj, xn, *mem_operands, *vectors.values())


def layer_bwd(x, dout, proj, ya, yb, yc, pswa, pmem, psink, gates, a_all, rc, rs1, rs2, ng, win_t, cw, wg, lam, gains,
              km, vm, og, wout):
    seq = x.shape[0]
    tm = min(ROW_TILE, seq)
    nt = seq // tm
    nb = tm // BLOCK

    def body(x_ref, dout_ref, proj_ref, ya_ref, yb_ref, yc_ref, pswa_ref, pmem_ref, psink_ref, gates_ref, a_ref,
             c_ref, s1_ref, s2_ref,
             yah_ref, kvh_ref, ch_ref, s1h_ref, s2h_ref,
             ng_ref, win_ref, cw_ref, wg_ref, lam_ref, gn_ref, km_ref, vm_ref, og_ref, wout_ref,
             gx_ref, dproj_ref, gwg_ref, dkm_ref, dvm_ref, gng_ref, gog_ref, gcb_ref, gbrg_ref, gbig_ref, glam_ref,
             gcw_ref, gqn_ref, gkn_ref, gxqn_ref, gsink_ref,
             hext_ref, aext_ref, an_scr, dh_scr, g_scr, dxc_ext, gcar_ref, dkcar_ref, dvcar_ref):
        i = pl.program_id(0)
        tile = nt - 1 - i
        first_tile = tile == 0

        @pl.when(i == 0)
        def _():
            for r in (gwg_ref, dkm_ref, dvm_ref, gng_ref, gog_ref, gcb_ref, gbrg_ref, gbig_ref, glam_ref, gcw_ref,
                      gqn_ref, gkn_ref, gxqn_ref, gsink_ref, gcar_ref, dkcar_ref, dvcar_ref):
                r[...] = jnp.zeros_like(r)
            dxc_ext[tm:tm + 8, :] = jnp.zeros((8, LRU_W), F32)
            aext_ref[tm:tm + 8, :] = jnp.zeros((8, LRU_W), F32)

        xv = x_ref[...]
        dov = dout_ref[...]
        dz = _mm_nt(dov, wout_ref[...])
        ogv = og_ref[...]

        def group_bwd(y, gate, g, dzg):
            r = lax.rsqrt(_row_mean(y * y) + EPS)
            n = y * r
            sg = _sigmoid(gate)
            dgate = dzg * (n * g) * (sg * (1.0 + gate * (1.0 - sg)))
            dng = dzg * (gate * sg)
            dn = dng * g
            return r * (dn - n * _row_mean(dn * n)), dgate, _col_sum(dng * n)

        dya, dga, goa = group_bwd(ya_ref[...], proj_ref[:, C_LRUG:C_LRUG + LRU_W], ogv[:, :512], dz[:, :512])
        dyb, dgb, gob = group_bwd(yb_ref[...], proj_ref[:, C_SWAG:C_SWAG + SWA_W], ogv[:, 512:768], dz[:, 512:768])
        dyc, dgc, goc = group_bwd(yc_ref[...], proj_ref[:, C_XG:C_XG + XATT_W], ogv[:, 768:], dz[:, 768:])
        gog_ref[...] += jnp.concatenate([goa, gob, goc], axis=1)
        dproj_ref[:, C_LRUG:C_LRUG + LRU_W] = dga.astype(dproj_ref.dtype)
        dproj_ref[:, C_SWAG:C_SWAG + SWA_W] = dgb.astype(dproj_ref.dtype)
        dproj_ref[:, C_XG:C_XG + XATT_W] = dgc.astype(dproj_ref.dtype)

        gm256 = _group_matrix(XATT_W)
        xq = proj_ref[:, C_XQ:C_XQ + XATT_W]
        rq = lax.rsqrt(_seg_mean(xq * xq, gm256) + EPS)
        qn = xq * rq
        qx = qn * gn_ref[:, G_XQ:G_XK]
        qxb = qx.astype(_MXU)
        dycb = dyc.astype(_MXU)
        dp_all = _mm_nt(dycb, vm_ref[...])
        dsm = []
        for j in range(4):
            pj = pmem_ref[:, MEM_LEN * j:MEM_LEN * (j + 1)].astype(F32)
            dp = dp_all[:, MEM_LEN * j:MEM_LEN * (j + 1)]
            dsm.append((pj * (dp - jnp.sum(pj * dp, axis=-1, keepdims=True))).astype(_MXU))
        ds_all = jnp.concatenate(dsm, axis=1)
        dvm_ref[...] += _mm_tn(dycb, pmem_ref[...])
        dkm_ref[...] += _mm_tn(qxb, ds_all)
        dqx = _mm(ds_all, km_ref[...])
        gxqn_ref[...] += _col_sum(dqx * qn)
        dqn = dqx * gn_ref[:, G_XQ:G_XK]
        dproj_ref[:, C_XQ:C_XQ + XATT_W] = (rq * (dqn - qn * _seg_mean(dqn * qn, gm256))).astype(dproj_ref.dtype)

        gm128 = _group_matrix(KV_W)
        cv, s1v, s2v = c_ref[...], s1_ref[...], s2_ref[...]

        def head_norm(t):
            r = lax.rsqrt(_seg_mean(t * t, gm128) + EPS)
            return t * r, r

        qn_, qr_ = zip(head_norm(proj_ref[:, C_SQ:C_SQ + 128]), head_norm(proj_ref[:, C_SQ + 128:C_SQ + 256]))
        qrope = [_rope(qn_[h] * gn_ref[:, G_Q:G_K], cv, s1v, s2v).astype(_MXU) for h in range(2)]
        kn, krr = head_norm(proj_ref[:, C_SK:C_SK + KV_W])
        kr = _rope(kn * gn_ref[:, G_K:G_XQ], cv, s1v, s2v)
        khn, _ = head_norm(kvh_ref[:, 0:KV_W])
        khr = _rope(khn * gn_ref[:, G_K:G_XQ], ch_ref[...], s1h_ref[...], s2h_ref[...])
        ka = _place_kv(jnp.concatenate([khr, kr], axis=0), 0.125)
        va = _place_kv(jnp.concatenate([kvh_ref[:, KV_W:2 * KV_W], proj_ref[:, C_SV:C_SV + KV_W]], axis=0), 1.0)
        lane128 = lax.broadcasted_iota(jnp.int32, (1, 128), 1)
        gsink = jnp.zeros((1, 128), F32)
        dk_band, dv_band, dq_blk = [], [], []
        for b in range(nb):
            band = slice(BLOCK * b, BLOCK * b + 2 * BLOCK)
            blk = slice(BLOCK * b, BLOCK * (b + 1))
            dka, dva, dsb = [], [], []
            deltas = jnp.zeros((BLOCK, 128), F32)
            for j in range(4):
                qh = qrope[j // 2][blk]
                doh = dyb[blk, KV_W * (j // 2):KV_W * (j // 2 + 1)].astype(_MXU)
                pb = pswa_ref[blk, 2 * BLOCK * j:2 * BLOCK * (j + 1)]
                p = pb.astype(F32)
                dp = _mm_nt(doh, va[j][band])
                delta = jnp.sum(p * dp, axis=-1, keepdims=True)
                ds = (p * (dp - delta)).astype(_MXU)
                deltas = jnp.where(lane128 == j, delta, deltas)
                dva.append(_mm_tn(pb, doh))
                dka.append(_mm_tn(ds, qh))
                dsb.append(ds)
            gsink = gsink - _col_sum(psink_ref[blk, :] * deltas)
            dk_band.append(_unplace_kv(dka) * 0.125)
            dv_band.append(_unplace_kv(dva))
            dq_blk.append([_mm(jnp.concatenate(dsb[2 * h:2 * h + 2], axis=1),
                               jnp.concatenate([ka[2 * h][band], ka[2 * h + 1][band]], axis=0)) for h in range(2)])
        gsink_ref[...] += gsink
        dk_rows = [dk_band[b][BLOCK:] + (dk_band[b + 1][:BLOCK] if b + 1 < nb else dkcar_ref[...]) for b in range(nb)]
        dv_rows = [dv_band[b][BLOCK:] + (dv_band[b + 1][:BLOCK] if b + 1 < nb else dvcar_ref[...]) for b in range(nb)]
        dkcar_ref[...] = dk_band[0][:BLOCK]
        dvcar_ref[...] = dv_band[0][:BLOCK]
        dkg = _rope_bwd(jnp.concatenate(dk_rows, axis=0), cv, s1v, s2v)
        gkn = _col_sum(dkg * kn)
        dkn = dkg * gn_ref[:, G_K:G_XQ]
        dproj_ref[:, C_SK:C_SK + KV_W] = (krr * (dkn - kn * _seg_mean(dkn * kn, gm128))).astype(dproj_ref.dtype)
        dproj_ref[:, C_SV:C_SV + KV_W] = jnp.concatenate(dv_rows, axis=0).astype(dproj_ref.dtype)
        gqn = jnp.zeros((1, 128), F32)
        for h in range(2):
            dqg = _rope_bwd(jnp.concatenate([dq_blk[b][h] for b in range(nb)], axis=0), cv, s1v, s2v)
            gqn = gqn + _col_sum(dqg * qn_[h])
            dqn_ = dqg * gn_ref[:, G_Q:G_K]
            dproj_ref[:, C_SQ + 128 * h:C_SQ + 128 * (h + 1)] = (
                qr_[h] * (dqn_ - qn_[h] * _seg_mean(dqn_ * qn_[h], gm128))).astype(dproj_ref.dtype)
        gqn_ref[...] += gqn
        gkn_ref[...] += gkn

        u = proj_ref[:, C_LRUX:C_LRUX + LRU_W]
        xc, rg, ig, sq = (gates_ref[:, LRU_W * k:LRU_W * (k + 1)].astype(F32) for k in range(4))
        a = a_ref[...]
        sp = _softplus(-lam_ref[...])
        hext_ref[0:8, :] = jnp.where(first_tile, 0.0, yah_ref[...])
        hext_ref[8:8 + tm, :] = ya_ref[...]
        hprev = hext_ref[pl.ds(7, tm), :]
        aext_ref[0:tm, :] = a
        an_scr[...] = aext_ref[pl.ds(1, tm), :]
        dh_scr[...] = dya
        dh_scr[tm - 1:tm, :] = dh_scr[tm - 1:tm, :] + gcar_ref[0:1, :]
        row8 = lax.broadcasted_iota(jnp.int32, (8, LRU_W), 0)

        def scan_step(gi, carry):
            r0 = pl.multiple_of((tm // 8 - 1 - gi) * 8, 8)
            av = an_scr[pl.ds(r0, 8), :]
            bv = dh_scr[pl.ds(r0, 8), :]
            for d in (1, 2, 4):
                a_sh = jnp.where(row8 < 8 - d, pltpu.roll(av, 8 - d, 0), 1.0)
                b_sh = jnp.where(row8 < 8 - d, pltpu.roll(bv, 8 - d, 0), 0.0)
                bv = bv + av * b_sh
                av = av * a_sh
            gv = bv + av * carry
            g_scr[pl.ds(r0, 8), :] = gv
            return gv[0:1, :]

        g0 = lax.fori_loop(0, tm // 8, scan_step, jnp.zeros((1, LRU_W), F32), unroll=True)
        gcar_ref[0:1, :] = a[0:1, :] * g0
        gv = g_scr[...]
        da = gv * hprev
        dig = gv * sq * xc
        dxc = gv * sq * ig
        dla = da * a - gv * (ig * xc) * ((a * a) / sq)
        drg = dla * ((-LRU_C) * sp)
        glam_ref[...] += _col_sum(dla * rg)
        dpr = drg * rg * (1.0 - rg)
        dpi = dig * ig * (1.0 - ig)
        gbrg_ref[...] += _col_sum(dpr)
        gbig_ref[...] += _col_sum(dpi)
        dpre0 = jnp.concatenate([dpr[:, :256], dpi[:, :256]], axis=1).astype(_MXU)
        dpre1 = jnp.concatenate([dpr[:, 256:], dpi[:, 256:]], axis=1).astype(_MXU)
        gwg_ref[0] += _mm_tn(xc[:, :256], dpre0)
        gwg_ref[1] += _mm_tn(xc[:, 256:], dpre1)
        dxc = dxc + jnp.concatenate([_mm_nt(dpre0, wg_ref[0]), _mm_nt(dpre1, wg_ref[1])], axis=1)
        gcb_ref[...] += _col_sum(dxc)
        dxc_ext[0:tm, :] = dxc
        du = jnp.zeros((tm, LRU_W), F32)
        for k in range(CONV_K):
            later = dxc_ext[pl.ds(3 - k, tm), :]
            gcw_ref[k:k + 1, :] += _col_sum(later * u)
            du = du + cw_ref[k:k + 1, :] * later
        dxc_ext[tm:tm + 8, :] = dxc[0:8, :]
        dproj_ref[:, C_LRUX:C_LRUX + LRU_W] = du.astype(dproj_ref.dtype)

        dxn = _mm(dproj_ref[...], win_ref[...])
        rx = lax.rsqrt(_row_mean(xv * xv) + EPS)
        xh = xv * rx
        gng_ref[...] += _col_sum(dxn * xh)
        dxh = dxn * ng_ref[...]
        gx_ref[...] = dov.astype(F32) + rx * (dxh - xh * _row_mean(dxh * xh))

        @pl.when(i == nt - 1)
        def _():
            glam_ref[...] = glam_ref[...] * (LRU_C * _sigmoid(-lam_ref[...]))
            for r in (gqn_ref, gkn_ref, gxqn_ref):
                r[...] = _fold_heads(r[...])

    def rows(ncol, arr_cols_block=0):
        return pl.BlockSpec((tm, ncol), lambda i: (nt - 1 - i, arr_cols_block))

    def halo(nrow, ncol, colblk=0):
        per = tm // nrow
        return pl.BlockSpec((nrow, ncol), lambda i: (jnp.maximum((nt - 1 - i) * per - 1, 0), colblk))

    in_specs = [rows(D_MODEL), rows(D_MODEL), rows(D_IN), rows(LRU_W), rows(SWA_W), rows(XATT_W),
                rows(4 * 2 * BLOCK), rows(4 * MEM_LEN), rows(128), rows(4 * LRU_W), rows(LRU_W),
                rows(128), rows(128), rows(128),
                halo(8, LRU_W), halo(BLOCK, 2 * KV_W, C_SK // (2 * KV_W)),
                halo(BLOCK, 128), halo(BLOCK, 128), halo(BLOCK, 128),
                _const_spec((1, D_MODEL)), _const_spec((D_IN, D_MODEL), True), _const_spec((CONV_K, LRU_W)),
                _const_spec((2, 256, 512), True), _const_spec((1, LRU_W)), _const_spec((1, GAINS_W)),
                _const_spec((4 * MEM_LEN, XATT_W), True), _const_spec((4 * MEM_LEN, XATT_W), True),
                _const_spec((1, D_MODEL)), _const_spec((D_MODEL, D_MODEL), True)]
    small = [(2, 256, 512), (XATT_W, 4 * MEM_LEN), (XATT_W, 4 * MEM_LEN), (1, D_MODEL), (1, D_MODEL), (1, LRU_W), (1, LRU_W),
             (1, LRU_W), (1, LRU_W), (CONV_K, LRU_W), (1, 128), (1, 128), (1, XATT_W), (1, 128)]
    out_shape = (jax.ShapeDtypeStruct((seq, D_MODEL), F32), jax.ShapeDtypeStruct((seq, D_IN), _MXU)) + tuple(
        jax.ShapeDtypeStruct(s, F32) for s in small)
    out_specs = (rows(D_MODEL), rows(D_IN)) + tuple(_const_spec(s) for s in small)
    scratch = [pltpu.VMEM((tm + 8, LRU_W), F32), pltpu.VMEM((tm + 8, LRU_W), F32),
               pltpu.VMEM((tm, LRU_W), F32), pltpu.VMEM((tm, LRU_W), F32), pltpu.VMEM((tm, LRU_W), F32),
               pltpu.VMEM((tm + 8, LRU_W), F32),
               pltpu.VMEM((8, LRU_W), F32), pltpu.VMEM((BLOCK, KV_W), F32), pltpu.VMEM((BLOCK, KV_W), F32)]
    return pl.pallas_call(
        body, name="layer_bwd", grid=(nt,), out_shape=out_shape, in_specs=in_specs, out_specs=out_specs,
        scratch_shapes=scratch,
        compiler_params=pltpu.CompilerParams(dimension_semantics=("arbitrary",), vmem_limit_bytes=VMEM_LIMIT),
    )(x, dout, proj, ya, yb, yc, pswa, pmem, psink, gates, a_all, rc, rs1, rs2, ya, proj, rc, rs1, rs2,
      ng, win_t, cw, wg, lam, gains, km, vm, og, wout)


def _reduce_protocol(big, sm, outs, osm, r1, r1s, wire, r2, r2s, wire2, ps, own, send, recv, lsem):
    nbig = len(big)
    x, y, c = lax.axis_index("x"), lax.axis_index("y"), lax.axis_index("c")
    sibling = (x, y, 1 - c)
    near, far, diag = _partners(x, y, c)
    me, near_id, far_id, diag_id = _chip_of(x, y), _chip_of(*near), _chip_of(*far), _chip_of(*diag)

    def copy(k, src, dst, to):
        return pltpu.make_async_remote_copy(src_ref=src, dst_ref=dst, send_sem=send.at[k], recv_sem=recv.at[k],
                                            device_id=to, device_id_type=MESH)

    def sent(stage, a):
        if a == nbig:
            src, dst, to = ((sm.at[1 - c], r1s, sibling), (r1s, r2s.at[0], (*near, c)), (ps, r2s.at[1], (*far, c)),
                            (osm.at[c], osm.at[c], sibling))[stage]
            return [copy(5 * nbig + stage, src, dst, to)]
        if stage == 0:
            return [copy(5 * a, big[a].at[:, 1 - c], r1[a], sibling)]
        if stage == 1:
            return [copy(5 * a + 1, wire[a].at[near_id], r2[a].at[0], (*near, c)),
                    copy(5 * a + 2, wire[a].at[diag_id], r2[a].at[1], (*near, c))]
        if stage == 2:
            return [copy(5 * a + 3, wire2[a], r2[a].at[2], (*far, c))]
        return [copy(5 * a + 4, outs[a].at[c], outs[a].at[c], sibling)]

    arrays = range(nbig + (sm is not None))

    def start(stage, a):
        for cp in sent(stage, a):
            cp.start()

    def arrived(k, ref):
        copy(k, ref, ref, sibling).wait_recv()

    def loads():
        return [pltpu.make_async_copy(big[a].at[:, c], own[a], lsem.at[a]) for a in range(nbig)]

    def stage0():
        for a in arrays:
            start(0, a)
        for cp in loads():
            cp.start()

    def stage1():
        for a in range(nbig):
            loads()[a].wait()
            arrived(5 * a, r1[a])
            for k in range(N_CHIPS):
                r1[a][k] = own[a][k] + r1[a][k]
                wire[a][k] = r1[a][k].astype(wire[a].dtype)
            start(1, a)
        if sm is not None:
            arrived(5 * nbig, r1s)
            r1s[...] = sm[c] + r1s[...]
            start(1, nbig)

    def stage2():
        for a in range(nbig):
            arrived(5 * a + 1, r2[a].at[0])
            arrived(5 * a + 2, r2[a].at[1])
            r1[a][me] = r1[a][me] + r2[a][0].astype(F32)
            wire2[a][...] = (r1[a][far_id] + r2[a][1].astype(F32)).astype(wire2[a].dtype)
            start(2, a)
        if sm is not None:
            arrived(5 * nbig + 1, r2s.at[0])
            ps[...] = r1s[...] + r2s[0]
            start(2, nbig)

    def stage3():
        for a in range(nbig):
            arrived(5 * a + 3, r2[a].at[2])
            outs[a][c] = r1[a][me] + r2[a][2].astype(F32)
            start(3, a)
        if sm is not None:
            arrived(5 * nbig + 2, r2s.at[1])
            osm[c] = ps[...] + r2s[1]
            start(3, nbig)

    def stage4():
        for a in range(nbig):
            arrived(5 * a + 4, outs[a].at[1 - c])
        if sm is not None:
            arrived(5 * nbig + 3, osm.at[1 - c])
        for stage in range(4):
            for a in arrays:
                for cp in sent(stage, a):
                    cp.wait_send()

    return [stage0, stage1, stage2, stage3, stage4]


def _reduce_buffers(bigs, g_small):
    half = [b.shape[2:] for b in bigs]
    sm_half = None if g_small is None else g_small.shape[1:]
    out_shape = [jax.ShapeDtypeStruct((2,) + h, F32) for h in half]
    small = lambda lead: [] if g_small is None else [pltpu.VMEM(lead + sm_half, F32)]
    if g_small is not None:
        out_shape.append(jax.ShapeDtypeStruct(g_small.shape, F32))
    n_sem = 5 * len(bigs) + 4
    scratch = ([pltpu.VMEM((N_CHIPS,) + h, F32) for h in half] + small(())
               + [pltpu.VMEM((N_CHIPS,) + h, _WIRE) for h in half]
               + [pltpu.VMEM((3,) + h, _WIRE) for h in half] + small((2,))
               + [pltpu.VMEM(h, _WIRE) for h in half] + small(())
               + [pltpu.VMEM((N_CHIPS,) + h, F32) for h in half]
               + [pltpu.SemaphoreType.DMA((n_sem,)), pltpu.SemaphoreType.DMA((n_sem,)),
                  pltpu.SemaphoreType.DMA((len(bigs),))])
    return out_shape, scratch


def _split_reduce_refs(refs, nbig, has_small):
    it = iter(refs)
    take = lambda n: [next(it) for _ in range(n)]
    one = lambda: next(it) if has_small else None
    big, sm = take(nbig), one()
    outs, osm = take(nbig), one()
    r1, r1s, wire, r2, r2s, wire2, ps, own = take(nbig), one(), take(nbig), take(nbig), one(), take(nbig), one(), take(nbig)
    send, recv, lsem = take(3)
    return big, sm, outs, osm, r1, r1s, wire, r2, r2s, wire2, ps, own, send, recv, lsem


def _hosted_reduce_shapes(bigs, g_small):
    red_shape, scratch = _reduce_buffers(bigs, g_small)
    nres = len(red_shape)
    return red_shape, [pltpu.VMEM(r.shape, r.dtype) for r in red_shape] + scratch + [pltpu.SemaphoreType.DMA((nres,))]


def _hosted_reduce(step, n_steps, closing, stage_at, operands, results, scratch, has_small):
    nres = len(results)
    sums, rest, fsem = scratch[:nres], scratch[nres:-1], scratch[-1]
    refs = tuple(operands) + tuple(sums) + tuple(rest)

    def to_results():
        out = [pltpu.make_async_copy(sums[k], results[k], fsem.at[k]) for k in range(nres)]
        for cp in out:
            cp.start()
        for cp in out:
            cp.wait()

    stages = _reduce_protocol(*_split_reduce_refs(refs, nres - has_small, has_small))

    def last_stage():
        stages[-1]()
        to_results()

    for at, stage in zip(stage_at, stages[:-1] + [last_stage]):
        if closing == (at == n_steps):
            pl.when(step == min(at, n_steps - 1))(stage)


def reduce_grads(big, name, parts):
    chips, halves, rows_, cols = big.shape
    sub = jax.ShapeDtypeStruct((chips, halves, rows_ // parts, cols), big.dtype)

    def body(b_ref, o_ref, *scratch):
        refs = [b_ref.at[:, :, s] for s in range(parts)] + [o_ref.at[:, s] for s in range(parts)] + list(scratch)
        for stage in _reduce_protocol(*_split_reduce_refs(refs, parts, False)):
            stage()

    _, scratch = _reduce_buffers([sub] * parts, None)
    return pl.pallas_call(
        body, name=name, out_shape=jax.ShapeDtypeStruct((halves, parts, rows_ // parts, cols), F32),
        in_specs=[pl.BlockSpec(memory_space=pl.ANY)], out_specs=pl.BlockSpec(memory_space=pltpu.VMEM),
        scratch_shapes=scratch, compiler_params=pltpu.CompilerParams(vmem_limit_bytes=VMEM_LIMIT),
    )(big.reshape(chips, halves, parts, rows_ // parts, cols))


def adamw(items, g_pack, ws, ms, vs):
    plan, total = [], 0
    for w, _, _, _ in items:
        rows_, cols = w.shape
        tr = max(t for t in range(8, rows_ + 1, 8) if rows_ % t == 0 and t * cols * 4 <= ADAM_BLOCK_BYTES)
        plan.append((total, rows_ // tr, tr, cols))
        total += rows_ // tr
    nin, n = 4 * len(items), len(ws)

    def body(*refs):
        i = pl.program_id(0)
        small_in = refs[nin:nin + 1 + 3 * n]
        outs = refs[nin + 1 + 3 * n:]

        @pl.when(i == 0)
        def _():
            _adamw_small(small_in[0], *(small_in[1 + k * n:1 + (k + 1) * n] for k in range(3)),
                         *(outs[nin + k * n:nin + (k + 1) * n] for k in range(4)), outs[-1])

        for k, (first, steps, _, _) in enumerate(plan):
            w_ref, g_ref, m_ref, v_ref = refs[4 * k:4 * k + 4]
            go_ref, d_ref, nm_ref, nv_ref = outs[4 * k:4 * k + 4]

            @pl.when((i >= first) & (i < first + steps))
            def _():
                gv = g_ref[...]
                go_ref[...] = gv
                d_ref[...], nm_ref[...], nv_ref[...] = _adam_update(w_ref[...], gv, m_ref[...], v_ref[...])

    specs, shapes = [], []
    for (first, steps, tr, cols), (w, _, _, _) in zip(plan, items):
        spec = pl.BlockSpec((tr, cols), lambda i, first=first, steps=steps: (jnp.clip(i - first, 0, steps - 1), 0))
        specs += [spec] * 4
        shapes += [jax.ShapeDtypeStruct(w.shape, F32)] * 4
    vm = pl.BlockSpec(memory_space=pltpu.VMEM)
    like = [jax.ShapeDtypeStruct(w.shape, F32) for w in ws]
    res = pl.pallas_call(
        body, name="adamw", grid=(total,), out_shape=(*shapes, *like * 4, jax.ShapeDtypeStruct((1, 1), F32)),
        in_specs=specs + [vm] * (1 + 3 * n), out_specs=(*specs, *[vm] * (4 * n + 1)),
        compiler_params=pltpu.CompilerParams(dimension_semantics=("arbitrary",)),
    )(*[a for item in items for a in item], g_pack, *ws, *ms, *vs)
    return [res[4 * k:4 * k + 4] for k in range(len(items))], res[nin:]


def _adam_update(w, g, m, v):
    nm = ADAM_B1 * m + (1.0 - ADAM_B1) * g
    nv = ADAM_B2 * v + (1.0 - ADAM_B2) * (g * g)
    m_hat = nm / (1.0 - ADAM_B1 ** ADAM_STEP)
    v_hat = nv / (1.0 - ADAM_B2 ** ADAM_STEP)
    return (-ADAM_LR) * (m_hat / (jnp.sqrt(v_hat) + ADAM_EPS) + ADAM_WD * w), nm, nv


def _adamw_small(pk, w_refs, m_refs, v_refs, g_out, d_out, nm_out, nv_out, loss_ref):
    nvec = len(SMALL_VECTORS)
    loss_ref[...] = pk[LOSS_ROW:LOSS_ROW + 1, 0:1]
    chip = 2 * lax.axis_index("x") + lax.axis_index("y")
    for k, (name, row, width) in enumerate(SMALL_VECTORS):
        if name == "conv_w":
            g = jnp.concatenate([pk[pl.ds(row + 4 * t + chip, 1), :] for t in range(CONV_K)], axis=0)[None]
        elif width >= 128:
            g = jnp.concatenate([pk[row + r:row + r + 1, :] for r in range(width // 128)], axis=1)
        else:
            g = pk[row:row + 1, 0:width]
        g_out[k][...] = g
        d_out[k][...], nm_out[k][...], nv_out[k][...] = _adam_update(w_refs[k][...], g, m_refs[k][...], v_refs[k][...])
    for k in range(nvec, nvec + len(SMALL_MATRICES)):
        for b in range(LRU_BLOCKS):
            rows_ = pk[GATES_ROW + HEAD * b:GATES_ROW + HEAD * (b + 1), :]
            g = (pltpu.roll(rows_, HEAD, axis=1) if k > nvec else rows_)[:, 0:HEAD]
            g_out[k][0, b] = g
            d_out[k][0, b], nm_out[k][0, b], nv_out[k][0, b] = _adam_update(
                w_refs[k][0, b], g, m_refs[k][0, b], v_refs[k][0, b])


SMALL_VECTORS = (("norm_g", 512, 1024), ("mem_norm_g", 520, 1024), ("conv_w", 528, 512), ("conv_b", 544, 512),
                 ("b_rg", 548, 512), ("b_ig", 552, 512), ("lru_lambda", 556, 512), ("q_norm_g", 560, 64),
                 ("k_norm_g", 561, 64), ("sinks", 562, 4), ("xq_norm_g", 563, 64), ("xk_norm_g", 564, 64),
                 ("out_norm_g", 565, 1024))
LOSS_ROW = 573
SMALL_MATRICES = ("w_rg", "w_ig")
GATES_ROW = 0
SMALL_ROWS = 640


def _rope_tables(seq):
    pos = np.arange(seq, dtype=np.float32)
    inv_freq = (np.float32(ROPE_THETA) ** (-(np.arange(0, ROPE_DIM, 2, dtype=np.float32) / np.float32(ROPE_DIM)))
                ).astype(np.float32)
    ang = (pos[:, None] * inv_freq[None, :]).astype(np.float32)
    cos, sin = np.cos(ang).astype(np.float32), np.sin(ang).astype(np.float32)
    z = lambda n: np.zeros((seq, n), np.float32)
    c64 = np.concatenate([cos, cos, np.ones((seq, HEAD - ROPE_DIM), np.float32)], axis=1)
    s1_64 = np.concatenate([-sin, z(HEAD - 8)], axis=1)
    s2_64 = np.concatenate([z(8), sin, z(HEAD - ROPE_DIM)], axis=1)
    return tuple(jnp.asarray(np.concatenate([t, t], axis=1)) for t in (c64, s1_64, s2_64))


def kernel(x, mem, norm_g, mem_norm_g, w_in, conv_w, conv_b, w_rg, b_rg, w_ig, b_ig, lru_lambda, q_norm_g, k_norm_g, sinks, w_mem_kv, xq_norm_g, xk_norm_g, out_norm_g, w_out, loss_target, m_norm_g, m_mem_norm_g, m_w_in, m_conv_w, m_conv_b, m_w_rg, m_b_rg, m_w_ig, m_b_ig, m_lru_lambda, m_q_norm_g, m_k_norm_g, m_sinks, m_w_mem_kv, m_xq_norm_g, m_xk_norm_g, m_out_norm_g, m_w_out, v_norm_g, v_mem_norm_g, v_w_in, v_conv_w, v_conv_b, v_w_rg, v_b_rg, v_w_ig, v_b_ig, v_lru_lambda, v_q_norm_g, v_k_norm_g, v_sinks, v_w_mem_kv, v_xq_norm_g, v_xk_norm_g, v_out_norm_g, v_w_out):
    seq = x.shape[1]
    xs, tgt, mems = x[0], loss_target[0], mem[0]

    win_t, wout, wkv, cw, wg, gains, km, vm = gather_weights(
        w_in[0].T, w_out[0], w_mem_kv[0], conv_w, w_rg, w_ig, (q_norm_g, k_norm_g, xq_norm_g, xk_norm_g), mems,
        mem_norm_g)
    rc, rs1, rs2 = _rope_tables(seq)
    proj, ya, yb, yc, ycat, xn, dout, pswa, pmem, psink, gates, a_all, loss8 = layer_fwd(
        xs, tgt, rc, rs1, rs2, norm_g, win_t, cw, conv_b, wg, b_rg, b_ig, lru_lambda, gains, sinks, km, vm,
        out_norm_g, wout)
    (gx, dproj, g_wg, dkm, dvm, g_ng, g_og, g_cb, g_brg, g_big, g_lam, g_cw, g_qn, g_kn, g_xqn, g_sink) = layer_bwd(
        xs, dout, proj, ya, yb, yc, pswa, pmem, psink, gates, a_all, rc, rs1, rs2, norm_g, win_t, cw, wg, lru_lambda,
        gains, km, vm, out_norm_g, wout)
    g_win_t, r_out, r_kv, r_small = weight_grads(
        ycat, dout, dproj, xn, (mems, mem_norm_g, wkv, gains, dkm, dvm, g_wg, loss8), dict(
            norm_g=g_ng, conv_w=g_cw, conv_b=g_cb, b_rg=g_brg, b_ig=g_big, lru_lambda=g_lam, q_norm_g=g_qn,
            k_norm_g=g_kn, sinks=g_sink, xq_norm_g=g_xqn, out_norm_g=g_og), (0, 1, 3, 5, 6), (4, 5, 9, 11, 13))
    r_in = reduce_grads(g_win_t.reshape(N_CHIPS, 2, D_IN // 8, D_MODEL), "reduce_w_in", 6)

    r_small = r_small.reshape(SMALL_ROWS, 128)
    grads = {}
    weights = dict(norm_g=norm_g, mem_norm_g=mem_norm_g, w_in=w_in, conv_w=conv_w, conv_b=conv_b, w_rg=w_rg, b_rg=b_rg,
                   w_ig=w_ig, b_ig=b_ig, lru_lambda=lru_lambda, q_norm_g=q_norm_g, k_norm_g=k_norm_g, sinks=sinks,
                   w_mem_kv=w_mem_kv, xq_norm_g=xq_norm_g, xk_norm_g=xk_norm_g, out_norm_g=out_norm_g, w_out=w_out)
    ms = dict(norm_g=m_norm_g, mem_norm_g=m_mem_norm_g, w_in=m_w_in, conv_w=m_conv_w, conv_b=m_conv_b, w_rg=m_w_rg,
              b_rg=m_b_rg, w_ig=m_w_ig, b_ig=m_b_ig, lru_lambda=m_lru_lambda, q_norm_g=m_q_norm_g, k_norm_g=m_k_norm_g,
              sinks=m_sinks, w_mem_kv=m_w_mem_kv, xq_norm_g=m_xq_norm_g, xk_norm_g=m_xk_norm_g,
              out_norm_g=m_out_norm_g, w_out=m_w_out)
    vs = dict(norm_g=v_norm_g, mem_norm_g=v_mem_norm_g, w_in=v_w_in, conv_w=v_conv_w, conv_b=v_conv_b, w_rg=v_w_rg,
              b_rg=v_b_rg, w_ig=v_w_ig, b_ig=v_b_ig, lru_lambda=v_lru_lambda, q_norm_g=v_q_norm_g, k_norm_g=v_k_norm_g,
              sinks=v_sinks, w_mem_kv=v_w_mem_kv, xq_norm_g=v_xq_norm_g, xk_norm_g=v_xk_norm_g,
              out_norm_g=v_out_norm_g, w_out=v_w_out)

    delta, new_m, new_v = {}, {}, {}
    small_names = [n for n, _, _ in SMALL_VECTORS] + list(SMALL_MATRICES)
    (res_in, res_out, res_kv), res = adamw(
        [(w_in[0].T, r_in.reshape(D_IN // 4, D_MODEL), m_w_in[0].T, v_w_in[0].T),
         (w_out[0], r_out.reshape(D_MODEL // 4, D_MODEL), m_w_out[0], v_w_out[0]),
         (w_mem_kv[0], r_kv.reshape(D_MODEL // 4, 2 * XATT_W), m_w_mem_kv[0], v_w_mem_kv[0])],
        r_small, [weights[n] for n in small_names], [ms[n] for n in small_names], [vs[n] for n in small_names])
    grads["w_in"], delta["w_in"], new_m["w_in"], new_v["w_in"] = (r.T[None] for r in res_in)
    grads["w_out"], delta["w_out"], new_m["w_out"], new_v["w_out"] = (r[None] for r in res_out)
    grads["w_mem_kv"], delta["w_mem_kv"], new_m["w_mem_kv"], new_v["w_mem_kv"] = (r[None] for r in res_kv)
    nall = len(small_names)
    for k, into in enumerate((grads, delta, new_m, new_v)):
        into.update(zip(small_names, res[k * nall:(k + 1) * nall]))
    loss = res[-1].reshape(())

    order = ("norm_g", "mem_norm_g", "w_in", "conv_w", "conv_b", "w_rg", "b_rg", "w_ig", "b_ig", "lru_lambda",
             "q_norm_g", "k_norm_g", "sinks", "w_mem_kv", "xq_norm_g", "xk_norm_g", "out_norm_g", "w_out")
    return (loss, gx[None], *[grads[n] for n in order], *[delta[n] for n in order], *[new_m[n] for n in order],
            *[new_v[n] for n in order])
```

```python
import jax
import jax.numpy as jnp
import numpy as np
from jax import lax
from jax.experimental import pallas as pl
from jax.experimental.pallas import tpu as pltpu

F32 = jnp.float32
_MXU = jnp.bfloat16
_WIRE = jnp.bfloat16

D_MODEL = 1024
MEM_LEN = 256
HEAD = 64
LRU_W = 512
LRU_BLOCKS = 8
CONV_K = 4
LRU_C = 8.0
SWA_W = 256
KV_W = 128
XATT_W = 256
BLOCK = 128
D_IN = 2304
ROPE_THETA = 500000.0
ROPE_DIM = 16
EPS = 1e-6
NEG_INF = -1e30
C_LRUX, C_LRUG, C_SQ, C_SK, C_SV, C_SWAG, C_XQ, C_XG = 0, 512, 1024, 1280, 1408, 1536, 1792, 2048
G_Q, G_K, G_XQ, G_XK, GAINS_W = 0, 128, 256, 512, 768

ADAM_LR, ADAM_B1, ADAM_B2, ADAM_EPS, ADAM_WD, ADAM_STEP = 0.001, 0.9, 0.999, 1e-08, 0.01, 10

N_CHIPS = 4
ROW_TILE = 256
VMEM_LIMIT = 56 * 1024 * 1024
ADAM_BLOCK_BYTES = 640 * 1024
MESH = pl.DeviceIdType.MESH


def _mm(a, b):
    return jnp.dot(a.astype(_MXU), b.astype(_MXU), preferred_element_type=F32)


def _mm_nt(a, b):
    return lax.dot_general(a.astype(_MXU), b.astype(_MXU), (((1,), (1,)), ((), ())), preferred_element_type=F32)


def _mm_tn(a, b):
    return lax.dot_general(a.astype(_MXU), b.astype(_MXU), (((0,), (0,)), ((), ())), preferred_element_type=F32)


def _group_matrix(width):
    r = lax.shift_right_logical(lax.broadcasted_iota(jnp.int32, (width, width), 0), 6)
    c = lax.shift_right_logical(lax.broadcasted_iota(jnp.int32, (width, width), 1), 6)
    return (r == c).astype(_MXU)


def _seg_mean(x, gm):
    return jnp.dot(x.astype(_MXU), gm, preferred_element_type=F32) * (1.0 / HEAD)


def _row_mean(x):
    return jnp.mean(x, axis=-1, keepdims=True)


def _col_sum(x):
    return jnp.sum(x, axis=0, keepdims=True)


def _sigmoid(x):
    return jax.nn.sigmoid(x)


def _softplus(z):
    e = jnp.exp(-jnp.abs(z))
    u = 1.0 + e
    log1p_e = jnp.where(u == 1.0, e, jnp.log(u) * (e / (u - 1.0)))
    return jnp.maximum(z, 0.0) + log1p_e


def _rope(t, c, s1, s2):
    return t * c + pltpu.roll(t, 120, 1) * s1 + pltpu.roll(t, 8, 1) * s2


def _rope_bwd(d, c, s1, s2):
    return d * c + pltpu.roll(d * s1, 8, 1) + pltpu.roll(d * s2, 120, 1)


def _fold_heads(v):
    out = v
    for k in range(1, v.shape[1] // HEAD):
        out = out + pltpu.roll(v, HEAD * k, 1)
    return out


def _lane_mask(width, lo, hi):
    lane = lax.broadcasted_iota(jnp.int32, (1, width), 1)
    return ((lane >= lo) & (lane < hi)).astype(F32)


def _swa_mask(first_block):
    qi = lax.broadcasted_iota(jnp.int32, (BLOCK, 2 * BLOCK), 0)
    kj = lax.broadcasted_iota(jnp.int32, (BLOCK, 2 * BLOCK), 1)
    rel = qi + BLOCK - kj
    ok = (rel >= 0) & (rel < BLOCK)
    return ok & (jnp.logical_not(first_block) | (kj >= BLOCK))


def _place_kv(t, scale):
    lo = t * (_lane_mask(KV_W, 0, HEAD) * scale)
    hi = t * (_lane_mask(KV_W, HEAD, KV_W) * scale)
    return [a.astype(_MXU) for a in (lo, pltpu.roll(lo, HEAD, 1), pltpu.roll(hi, HEAD, 1), hi)]


def _unplace_kv(d):
    return (_lane_mask(KV_W, 0, HEAD) * (d[0] + pltpu.roll(d[1], HEAD, 1))
            + _lane_mask(KV_W, HEAD, KV_W) * (d[3] + pltpu.roll(d[2], HEAD, 1)))


def _swa_probs(qh, ka, mask, sink):
    s = _mm_nt(qh, ka)
    s = jnp.where(mask, s, NEG_INF)
    m = jnp.maximum(jnp.max(s, axis=-1, keepdims=True), sink)
    p = jnp.exp(s - m)
    esink = jnp.exp(sink - m)
    inv = 1.0 / (jnp.sum(p, axis=-1, keepdims=True) + esink)
    return p * inv, esink * inv


def _mem_probs(s_all):
    out = []
    for j in range(4):
        s = s_all[:, MEM_LEN * j:MEM_LEN * (j + 1)]
        p = jnp.exp(s - jnp.max(s, axis=-1, keepdims=True))
        out.append(p * (1.0 / jnp.sum(p, axis=-1, keepdims=True)))
    return out


def _head_rows(t, scale):
    return jnp.concatenate([t * (_lane_mask(XATT_W, HEAD * j, HEAD * (j + 1)) * scale) for j in range(4)], axis=0)


def _lru_gates(xc, wg_ref, brg, big, lam):
    p0 = _mm(xc[:, :256], wg_ref[0])
    p1 = _mm(xc[:, 256:], wg_ref[1])
    rg = _sigmoid(jnp.concatenate([p0[:, :256], p1[:, :256]], axis=1) + brg)
    ig = _sigmoid(jnp.concatenate([p0[:, 256:], p1[:, 256:]], axis=1) + big)
    sp = _softplus(-lam)
    la = (-LRU_C) * rg * sp
    a = jnp.exp(la)
    th = jnp.tanh(la)
    one_minus_a2 = (-2.0 * th) / (1.0 - th)
    return rg, ig, sp, a, jnp.sqrt(one_minus_a2)


def _const_spec(shape, single=False):
    zeros = (0,) * len(shape)
    if single:
        return pl.BlockSpec(shape, lambda i: zeros, pipeline_mode=pl.Buffered(1))
    return pl.BlockSpec(shape, lambda i: zeros)


def _chip_of(x, y):
    return 2 * x + y


def _partners(x, y, c):
    north = c == 1
    near = (jnp.where(north, 1 - x, x), jnp.where(north, y, 1 - y))
    far = (jnp.where(north, x, 1 - x), jnp.where(north, 1 - y, y))
    return near, far, (1 - x, 1 - y)


def gather_weights(win_t, wout, wkv, conv_w, w_rg, w_ig, head_gains, mem, mem_g):
    arrs = (win_t, wout, wkv)
    n = len(arrs)
    pieces = [(a, 0, arr.shape[0] // 2) for a, arr in enumerate(arrs)]
    npc = len(pieces)

    def body(a0, a1, a2, cw_in, wrg_ref, wig_ref, q_ref, k_ref, xq_ref, xk_ref, mem_ref, mg_ref,
             o0, o1, o2, cw_out, wg_ref, gn_ref, km_ref, vm_ref, s0, s1, s2, cw, ocw, send, recv, lsem):
        ins, outs = (s0, s1, s2), (o0, o1, o2)
        for src, dst in zip((a0, a1, a2), ins):
            dst[...] = src[...].astype(dst.dtype)
        cw[...] = jnp.zeros(cw.shape, F32)
        cw[0:CONV_K, :] = cw_in[0]
        x, y, c = lax.axis_index("x"), lax.axis_index("y"), lax.axis_index("c")
        sibling = (x, y, 1 - c)
        near, far, diag = _partners(x, y, c)
        chips = [near, far, diag]
        me = _chip_of(x, y)

        def landed(p, chip, half):
            a, off, rows_ = pieces[p]
            r = ins[a].shape[0]
            return outs[a].at[pl.ds(pl.multiple_of(chip * r + half * (r // 2) + off, 16), rows_)]

        def mine(p):
            a, off, rows_ = pieces[p]
            return ins[a].at[pl.ds(pl.multiple_of(c * (ins[a].shape[0] // 2) + off, 16), rows_)]

        def copy(k, src, dst, to):
            return pltpu.make_async_remote_copy(src_ref=src, dst_ref=dst, send_sem=send.at[k], recv_sem=recv.at[k],
                                                device_id=to, device_id_type=MESH)

        def cw_rows(chip):
            return ocw.at[pl.ds(pl.multiple_of(chip * 8, 8), 8)]

        locals_ = []
        for a in range(n):
            r = ins[a].shape[0]
            locals_.append(pltpu.make_async_copy(ins[a], outs[a].at[pl.ds(pl.multiple_of(me * r, 16), r)], lsem.at[a]))
        locals_.append(pltpu.make_async_copy(cw, cw_rows(me), lsem.at[n]))
        for cp in locals_:
            cp.start()

        sent = []
        for p in range(npc):
            for j in range(2):
                sent.append(copy(p * 6 + j, mine(p), landed(p, me, c), (*chips[j], c)))
        for j, chip in enumerate(chips):
            sent.append(copy(npc * 6 + j, cw, cw_rows(me), (*chip, c)))
        for cp in sent:
            cp.start()

        gn_ref[...] = jnp.concatenate([q_ref[...]] * 2 + [k_ref[...]] * 2 + [xq_ref[...]] * 4 + [xk_ref[...]] * 4,
                                      axis=1)
        zeros = lambda lanes: [jnp.zeros((HEAD, lanes), F32)] if lanes else []
        for h in range(2):
            for b in range(4):
                row = []
                for w_ref in (wrg_ref, wig_ref):
                    row += zeros(HEAD * b) + [w_ref[0, 4 * h + b]] + zeros(HEAD * (3 - b))
                wg_ref[h, HEAD * b:HEAD * (b + 1), :] = jnp.concatenate(row, axis=1).astype(wg_ref.dtype)

        for j in range(3):
            for p in range(npc):
                got = landed(p, _chip_of(*chips[j]), c)
                copy(p * 6 + j, got, got, sibling).wait_recv()
                if j == 0:
                    sent.append(copy(p * 6 + 2, got, got, (*far, c)))
                    sent[-1].start()
                sent.append(copy(p * 6 + 3 + j, got, got, sibling))
                sent[-1].start()
        for p in range(npc):
            for j in range(3):
                got = landed(p, _chip_of(*chips[(1, 0, 2)[j]]), 1 - c)
                copy(p * 6 + 3 + j, got, got, sibling).wait_recv()
        for j, chip in enumerate(chips):
            got = cw_rows(_chip_of(*chip))
            copy(npc * 6 + j, got, got, (*chip, c)).wait_recv()
        for cp in sent:
            cp.wait_send()
        for cp in locals_:
            cp.wait()
        for chip in range(N_CHIPS):
            cw_out[:, 128 * chip:128 * (chip + 1)] = ocw[8 * chip:8 * chip + CONV_K, :]

        mem_v = mem_ref[...]
        mn = mem_v * lax.rsqrt(_row_mean(mem_v * mem_v) + EPS) * mg_ref[...]
        mkv = _mm(mn, o2[...])
        kpre = mkv[:, :XATT_W]
        km = kpre * lax.rsqrt(_seg_mean(kpre * kpre, _group_matrix(XATT_W)) + EPS) * gn_ref[:, G_XK:GAINS_W]
        km_ref[...] = _head_rows(km, 0.125).astype(km_ref.dtype)
        vm_ref[...] = _head_rows(mkv[:, XATT_W:], 1.0).astype(vm_ref.dtype)

    vm = pl.BlockSpec(memory_space=pltpu.VMEM)
    hbm = pl.BlockSpec(memory_space=pl.ANY)
    head_rows = jax.ShapeDtypeStruct((4 * MEM_LEN, XATT_W), _MXU)
    out_shape = tuple(jax.ShapeDtypeStruct((N_CHIPS * a.shape[0],) + a.shape[1:], _MXU) for a in arrs) + (
        jax.ShapeDtypeStruct((CONV_K, LRU_W), F32), jax.ShapeDtypeStruct((2, 256, 512), _MXU),
        jax.ShapeDtypeStruct((1, GAINS_W), F32), head_rows, head_rows)
    n_rdma = npc * 6 + 3
    return pl.pallas_call(
        body, name="gather_weights", out_shape=out_shape,
        in_specs=[vm] * 12, out_specs=(hbm, hbm, vm, vm, vm, vm, vm, vm),
        scratch_shapes=[pltpu.VMEM(a.shape, _MXU) for a in arrs] + [
            pltpu.VMEM((8, 128), F32), pltpu.VMEM((N_CHIPS * 8, 128), F32),
            pltpu.SemaphoreType.DMA((n_rdma,)), pltpu.SemaphoreType.DMA((n_rdma,)), pltpu.SemaphoreType.DMA((n + 1,))],
        compiler_params=pltpu.CompilerParams(vmem_limit_bytes=VMEM_LIMIT),
    )(win_t, wout, wkv, conv_w, w_rg, w_ig, *head_gains, mem, mem_g)


def _mem_bwd_and_pack(mem_ref, g_ref, w_ref, gn_ref, dkm_ref, dvm_ref, gg_ref, loss_ref, vectors, gw_ref, pk_ref):
    first_row = {name: (row, width) for name, row, width in SMALL_VECTORS}
    half_rows = SMALL_ROWS // 2
    pk_ref[...] = jnp.zeros(pk_ref.shape, F32)

    def rows_at(at, n):
        assert at // half_rows == (at + n - 1) // half_rows
        return at // half_rows, slice(at % half_rows, at % half_rows + n), slice(None)

    def put(name, src):
        row, width = first_row[name]
        per_row = 1 if width < 128 else src.shape[1] // 128
        for t in range(src.shape[0]):
            for r in range(per_row):
                pk_ref[rows_at(row + per_row * t + r, 1)] = src[t:t + 1, 128 * r:128 * (r + 1)]

    for name, ref in vectors.items():
        put(name, ref)
    pk_ref[rows_at(LOSS_ROW, 1)] = loss_ref[0:1, :]
    upper = lax.broadcasted_iota(jnp.int32, (HEAD, 128), 1) >= HEAD
    for h in range(2):
        for b in range(4):
            rg = gg_ref[h, HEAD * b:HEAD * (b + 1), 128 * (b // 2):128 * (b // 2 + 1)]
            ig = gg_ref[h, HEAD * b:HEAD * (b + 1), 256 + 128 * (b // 2):256 + 128 * (b // 2 + 1)]
            if b % 2:
                rg = pltpu.roll(rg, HEAD, axis=1)
            else:
                ig = pltpu.roll(ig, HEAD, axis=1)
            pk_ref[rows_at(GATES_ROW + HEAD * (4 * h + b), HEAD)] = jnp.where(upper, ig, rg)

    mem_v = mem_ref[...]
    mh = mem_v * lax.rsqrt(_row_mean(mem_v * mem_v) + EPS)
    mn = mh * g_ref[...]
    mkv = _mm(mn, w_ref[...])
    kpre = mkv[:, :XATT_W]
    gm = _group_matrix(XATT_W)
    rk = lax.rsqrt(_seg_mean(kpre * kpre, gm) + EPS)
    kn = kpre * rk
    dk = jnp.zeros((MEM_LEN, XATT_W), F32)
    dv = jnp.zeros((MEM_LEN, XATT_W), F32)
    for j in range(4):
        mj = _lane_mask(XATT_W, HEAD * j, HEAD * (j + 1))
        dk = dk + dkm_ref[:, MEM_LEN * j:MEM_LEN * (j + 1)].T * (mj * 0.125)
        dv = dv + dvm_ref[:, MEM_LEN * j:MEM_LEN * (j + 1)].T * mj
    put("xk_norm_g", _fold_heads(_col_sum(dk * kn)))
    dkn = dk * gn_ref[:, G_XK:GAINS_W]
    dkpre = rk * (dkn - kn * _seg_mean(dkn * kn, gm))
    dmkv = jnp.concatenate([dkpre, dv], axis=1)
    gw_ref[...] = _mm_tn(mn, dmkv).reshape(gw_ref.shape)
    dmn = _mm_nt(dmkv, w_ref[...])
    put("mem_norm_g", _col_sum(dmn * mh))


def layer_fwd(x, tgt, rc, rs1, rs2, ng, win_t, cw, cb, wg, brg, big, lam, gains, sinks, km, vm, og, wout):
    seq = x.shape[0]
    tm = min(ROW_TILE, seq)
    nt = seq // tm
    nb = tm // BLOCK

    def body(x_ref, t_ref, c_ref, s1_ref, s2_ref, ng_ref, win_ref, cw_ref, cb_ref, wg_ref, brg_ref, big_ref, lam_ref,
             gn_ref, sink_ref, km_ref, vm_ref, og_ref, wout_ref,
             proj_ref, ya_ref, yb_ref, yc_ref, ycat_ref, xn_ref, dout_ref, pswa_ref, pmem_ref, psink_ref, gates_ref,
             a_ref, loss_ref,
             ext_ref, b_scr, hc_ref, kp_ref, vp_ref, lacc_ref):
        i = pl.program_id(0)

        @pl.when(i == 0)
        def _():
            ext_ref[0:8, :] = jnp.zeros((8, LRU_W), F32)
            hc_ref[...] = jnp.zeros_like(hc_ref)
            kp_ref[...] = jnp.zeros_like(kp_ref)
            vp_ref[...] = jnp.zeros_like(vp_ref)
            lacc_ref[...] = jnp.zeros_like(lacc_ref)

        xv = x_ref[...]
        xn = (xv * lax.rsqrt(_row_mean(xv * xv) + EPS) * ng_ref[...]).astype(_MXU)
        xn_ref[...] = xn.astype(xn_ref.dtype)
        proj_ref[...] = _mm_nt(xn, win_ref[...])

        u = proj_ref[:, C_LRUX:C_LRUX + LRU_W]
        ext_ref[8:8 + tm, :] = u
        xc = cb_ref[...]
        for k in range(CONV_K):
            xc = xc + cw_ref[k:k + 1, :] * ext_ref[pl.ds(5 + k, tm), :]
        ext_ref[0:8, :] = u[tm - 8:tm, :]
        rg, ig, sp, a, sq = _lru_gates(xc, wg_ref, brg_ref[...], big_ref[...], lam_ref[...])
        for k, t in enumerate((xc, rg, ig, sq)):
            gates_ref[:, LRU_W * k:LRU_W * (k + 1)] = t.astype(gates_ref.dtype)
        a_ref[...] = a
        b_scr[...] = sq * (ig * xc)
        row8 = lax.broadcasted_iota(jnp.int32, (8, LRU_W), 0)

        def scan_step(g, carry):
            r0 = pl.multiple_of(g * 8, 8)
            av = a_ref[pl.ds(r0, 8), :]
            bv = b_scr[pl.ds(r0, 8), :]
            for d in (1, 2, 4):
                a_sh = jnp.where(row8 >= d, pltpu.roll(av, d, 0), 1.0)
                b_sh = jnp.where(row8 >= d, pltpu.roll(bv, d, 0), 0.0)
                bv = bv + av * b_sh
                av = av * a_sh
            hv = bv + av * carry
            ya_ref[pl.ds(r0, 8), :] = hv
            return hv[7:8, :]

        hc_ref[0:1, :] = lax.fori_loop(0, tm // 8, scan_step, hc_ref[0:1, :], unroll=True)

        gm128 = _group_matrix(KV_W)
        cv, s1v, s2v = c_ref[...], s1_ref[...], s2_ref[...]

        def head_norm_rope(t, g):
            n = t * lax.rsqrt(_seg_mean(t * t, gm128) + EPS)
            return _rope(n * g, cv, s1v, s2v)

        qs_ = (head_norm_rope(proj_ref[:, C_SQ:C_SQ + 128], gn_ref[:, G_Q:G_K]).astype(_MXU),
               head_norm_rope(proj_ref[:, C_SQ + 128:C_SQ + 256], gn_ref[:, G_Q:G_K]).astype(_MXU))
        kr = head_norm_rope(proj_ref[:, C_SK:C_SK + KV_W], gn_ref[:, G_K:G_XQ])
        sv = proj_ref[:, C_SV:C_SV + KV_W]
        ka = _place_kv(jnp.concatenate([kp_ref[...], kr], axis=0), 0.125)
        va = _place_kv(jnp.concatenate([vp_ref[...], sv], axis=0), 1.0)
        kp_ref[...] = kr[tm - BLOCK:tm, :]
        vp_ref[...] = sv[tm - BLOCK:tm, :]
        lane128 = lax.broadcasted_iota(jnp.int32, (1, 128), 1)
        for b in range(nb):
            mask = _swa_mask((i == 0) & (b == 0)) if b == 0 else _swa_mask(False)
            band = slice(BLOCK * b, BLOCK * b + 2 * BLOCK)
            blk = slice(BLOCK * b, BLOCK * (b + 1))
            psink = jnp.zeros((BLOCK, 128), F32)
            for j in range(4):
                p, pk = _swa_probs(qs_[j // 2][blk], ka[j][band], mask, sink_ref[0, j])
                pswa_ref[blk, 2 * BLOCK * j:2 * BLOCK * (j + 1)] = p.astype(pswa_ref.dtype)
                psink = jnp.where(lane128 == j, pk, psink)
            psink_ref[blk, :] = psink
            for h in range(2):
                yb_ref[blk, KV_W * h:KV_W * (h + 1)] = _mm(
                    pswa_ref[blk, 4 * BLOCK * h:4 * BLOCK * (h + 1)],
                    jnp.concatenate([va[2 * h][band], va[2 * h + 1][band]], axis=0))

        gm256 = _group_matrix(XATT_W)
        xq = proj_ref[:, C_XQ:C_XQ + XATT_W]
        qx = xq * lax.rsqrt(_seg_mean(xq * xq, gm256) + EPS) * gn_ref[:, G_XQ:G_XK]
        pm = _mem_probs(_mm_nt(qx, km_ref[...]))
        for j in range(4):
            pmem_ref[:, MEM_LEN * j:MEM_LEN * (j + 1)] = pm[j].astype(pmem_ref.dtype)
        yc = _mm(pmem_ref[...], vm_ref[...])
        yc_ref[...] = yc

        def gated(y, g, gate):
            return y * lax.rsqrt(_row_mean(y * y) + EPS) * g * (gate * _sigmoid(gate))

        ogv = og_ref[...]
        za = gated(ya_ref[...], ogv[:, :512], proj_ref[:, C_LRUG:C_LRUG + LRU_W])
        zb = gated(yb_ref[...], ogv[:, 512:768], proj_ref[:, C_SWAG:C_SWAG + SWA_W])
        zc = gated(yc, ogv[:, 768:], proj_ref[:, C_XG:C_XG + XATT_W])
        ycat_ref[:, 0:512] = za.astype(ycat_ref.dtype)
        ycat_ref[:, 512:768] = zb.astype(ycat_ref.dtype)
        ycat_ref[:, 768:1024] = zc.astype(ycat_ref.dtype)
        out = xv + _mm(ycat_ref[...], wout_ref[...])
        err = out - t_ref[...]
        dout_ref[...] = (err * (1.0 / D_MODEL)).astype(dout_ref.dtype)
        lacc_ref[...] = lacc_ref[...] + (0.5 / D_MODEL) * jnp.sum(err * err)

        @pl.when(i == nt - 1)
        def _():
            loss_ref[...] = lacc_ref[...]

    def rows(ncol):
        return pl.BlockSpec((tm, ncol), lambda i: (i, 0))

    in_specs = [rows(D_MODEL), rows(D_MODEL), rows(128), rows(128), rows(128),
                _const_spec((1, D_MODEL)), _const_spec((D_IN, D_MODEL), True), _const_spec((CONV_K, LRU_W)),
                _const_spec((1, LRU_W)), _const_spec((2, 256, 512), True), _const_spec((1, LRU_W)),
                _const_spec((1, LRU_W)), _const_spec((1, LRU_W)), _const_spec((1, GAINS_W)), pl.BlockSpec(memory_space=pltpu.SMEM),
                _const_spec((4 * MEM_LEN, XATT_W), True), _const_spec((4 * MEM_LEN, XATT_W), True),
                _const_spec((1, D_MODEL)), _const_spec((D_MODEL, D_MODEL), True)]
    out_shape = (jax.ShapeDtypeStruct((seq, D_IN), F32), jax.ShapeDtypeStruct((seq, LRU_W), F32),
                 jax.ShapeDtypeStruct((seq, SWA_W), F32), jax.ShapeDtypeStruct((seq, XATT_W), F32),
                 jax.ShapeDtypeStruct((seq, D_MODEL), _MXU), jax.ShapeDtypeStruct((seq, D_MODEL), _MXU),
                 jax.ShapeDtypeStruct((seq, D_MODEL), _MXU), jax.ShapeDtypeStruct((seq, 4 * 2 * BLOCK), _MXU),
                 jax.ShapeDtypeStruct((seq, 4 * MEM_LEN), _MXU), jax.ShapeDtypeStruct((seq, 128), F32),
                 jax.ShapeDtypeStruct((seq, 4 * LRU_W), _MXU), jax.ShapeDtypeStruct((seq, LRU_W), F32),
                 jax.ShapeDtypeStruct((8, 128), F32))
    out_specs = (rows(D_IN), rows(LRU_W), rows(SWA_W), rows(XATT_W), rows(D_MODEL), rows(D_MODEL), rows(D_MODEL),
                 rows(4 * 2 * BLOCK), rows(4 * MEM_LEN), rows(128), rows(4 * LRU_W), rows(LRU_W),
                 _const_spec((8, 128)))
    scratch = [pltpu.VMEM((tm + 8, LRU_W), F32), pltpu.VMEM((tm, LRU_W), F32),
               pltpu.VMEM((8, LRU_W), F32), pltpu.VMEM((BLOCK, KV_W), F32), pltpu.VMEM((BLOCK, KV_W), F32),
               pltpu.VMEM((8, 128), F32)]
    return pl.pallas_call(
        body, name="layer_fwd", grid=(nt,), out_shape=out_shape, in_specs=in_specs, out_specs=out_specs,
        scratch_shapes=scratch,
        compiler_params=pltpu.CompilerParams(dimension_semantics=("arbitrary",), vmem_limit_bytes=VMEM_LIMIT),
    )(x, tgt, rc, rs1, rs2, ng, win_t, cw, cb, wg, brg, big, lam, gains, sinks, km, vm, og, wout)


def weight_grads(ycat, dout, dproj, xn, mem_operands, vectors, early_at, late_at):
    seq, ncol = xn.shape
    blk = 256
    n_out, n_in = ycat.shape[1] // blk, dproj.shape[1] // blk
    assert n_out == N_CHIPS and late_at[0] >= n_out
    g_out = jax.ShapeDtypeStruct((N_CHIPS, 2, blk // 2, dout.shape[1]), F32)
    g_kv = jax.ShapeDtypeStruct((N_CHIPS, 2, D_MODEL // 8, 2 * XATT_W), F32)
    g_small = jax.ShapeDtypeStruct((2, SMALL_ROWS // 2, 128), F32)
    shape_e, scratch_e = _hosted_reduce_shapes([g_kv], g_small)
    shape_l, scratch_l = _hosted_reduce_shapes([g_out], None)
    n_mem, names = len(mem_operands), tuple(vectors)

    def body(l1_ref, r1_ref, l2_ref, r2_ref, *refs):
        mem_refs, vec_refs = refs[:n_mem], refs[n_mem:n_mem + len(names)]
        o_ref, sum_out, sum_kv, sum_sm, gout_scr, gkv_scr, pack_scr, *scratch = refs[n_mem + len(names):]
        j = pl.program_id(0)

        @pl.when(j == 0)
        def _():
            _mem_bwd_and_pack(*mem_refs, dict(zip(names, vec_refs)), gkv_scr, pack_scr)

        def reduce_stages(closing):
            _hosted_reduce(j, n_out + n_in, closing, late_at, (gout_scr,), (sum_out,), scratch[len(scratch_e):], False)
            _hosted_reduce(j, n_out + n_in, closing, early_at, (gkv_scr, pack_scr), (sum_kv, sum_sm),
                           scratch[:len(scratch_e)], True)

        reduce_stages(False)

        @pl.when(j < n_out)
        def _():
            gout_scr[j] = _mm_tn(l1_ref[...], r1_ref[...]).reshape(g_out.shape[1:])

        @pl.when(j >= n_out)
        def _():
            o_ref[...] = _mm_tn(l2_ref[...], r2_ref[...])

        reduce_stages(True)

    vm = pl.BlockSpec(memory_space=pltpu.VMEM)
    hbm = pl.BlockSpec(memory_space=pl.ANY)
    return pl.pallas_call(
        body, name="weight_grads", grid=(n_out + n_in,),
        out_shape=(jax.ShapeDtypeStruct((dproj.shape[1], ncol), F32), *shape_l, *shape_e),
        in_specs=[pl.BlockSpec((seq, blk), lambda j: (0, jnp.minimum(j, n_out - 1))), _const_spec(dout.shape, True),
                  pl.BlockSpec((seq, blk), lambda j: (0, jnp.maximum(j - n_out, 0))), _const_spec(xn.shape, True)]
        + [vm] * (n_mem + len(names)),
        out_specs=(pl.BlockSpec((blk, ncol), lambda j: (jnp.maximum(j - n_out, 0), 0)), hbm, hbm, hbm),
        scratch_shapes=[pltpu.VMEM(s.shape, F32) for s in (g_out, g_kv, g_small)] + scratch_e + scratch_l,
        compiler_params=pltpu.CompilerParams(dimension_semantics=("arbitrary",), vmem_limit_bytes=VMEM_LIMIT),
    )(ycat, dout, dproj, xn, *mem_operands, *vectors.values())


def layer_bwd(x, dout, proj, ya, yb, yc, pswa, pmem, psink, gates, a_all, rc, rs1, rs2, ng, win_t, cw, wg, lam, gains,
              km, vm, og, wout):
    seq = x.shape[0]
    tm = min(ROW_TILE, seq)
    nt = seq // tm
    nb = tm // BLOCK

    def body(x_ref, dout_ref, proj_ref, ya_ref, yb_ref, yc_ref, pswa_ref, pmem_ref, psink_ref, gates_ref, a_ref,
             c_ref, s1_ref, s2_ref,
             yah_ref, kvh_ref, ch_ref, s1h_ref, s2h_ref,
             ng_ref, win_ref, cw_ref, wg_ref, lam_ref, gn_ref, km_ref, vm_ref, og_ref, wout_ref,
             gx_ref, dproj_ref, gwg_ref, dkm_ref, dvm_ref, gng_ref, gog_ref, gcb_ref, gbrg_ref, gbig_ref, glam_ref,
             gcw_ref, gqn_ref, gkn_ref, gxqn_ref, gsink_ref,
             hext_ref, aext_ref, an_scr, dh_scr, g_scr, dxc_ext, gcar_ref, dkcar_ref, dvcar_ref):
        i = pl.program_id(0)
        tile = nt - 1 - i
        first_tile = tile == 0

        @pl.when(i == 0)
        def _():
            for r in (gwg_ref, dkm_ref, dvm_ref, gng_ref, gog_ref, gcb_ref, gbrg_ref, gbig_ref, glam_ref, gcw_ref,
                      gqn_ref, gkn_ref, gxqn_ref, gsink_ref, gcar_ref, dkcar_ref, dvcar_ref):
                r[...] = jnp.zeros_like(r)
            dxc_ext[tm:tm + 8, :] = jnp.zeros((8, LRU_W), F32)
            aext_ref[tm:tm + 8, :] = jnp.zeros((8, LRU_W), F32)

        xv = x_ref[...]
        dov = dout_ref[...]
        dz = _mm_nt(dov, wout_ref[...])
        ogv = og_ref[...]

        def group_bwd(y, gate, g, dzg):
            r = lax.rsqrt(_row_mean(y * y) + EPS)
            n = y * r
            sg = _sigmoid(gate)
            dgate = dzg * (n * g) * (sg * (1.0 + gate * (1.0 - sg)))
            dng = dzg * (gate * sg)
            dn = dng * g
            return r * (dn - n * _row_mean(dn * n)), dgate, _col_sum(dng * n)

        dya, dga, goa = group_bwd(ya_ref[...], proj_ref[:, C_LRUG:C_LRUG + LRU_W], ogv[:, :512], dz[:, :512])
        dyb, dgb, gob = group_bwd(yb_ref[...], proj_ref[:, C_SWAG:C_SWAG + SWA_W], ogv[:, 512:768], dz[:, 512:768])
        dyc, dgc, goc = group_bwd(yc_ref[...], proj_ref[:, C_XG:C_XG + XATT_W], ogv[:, 768:], dz[:, 768:])
        gog_ref[...] += jnp.concatenate([goa, gob, goc], axis=1)
        dproj_ref[:, C_LRUG:C_LRUG + LRU_W] = dga.astype(dproj_ref.dtype)
        dproj_ref[:, C_SWAG:C_SWAG + SWA_W] = dgb.astype(dproj_ref.dtype)
        dproj_ref[:, C_XG:C_XG + XATT_W] = dgc.astype(dproj_ref.dtype)

        gm256 = _group_matrix(XATT_W)
        xq = proj_ref[:, C_XQ:C_XQ + XATT_W]
        rq = lax.rsqrt(_seg_mean(xq * xq, gm256) + EPS)
        qn = xq * rq
        qx = qn * gn_ref[:, G_XQ:G_XK]
        qxb = qx.astype(_MXU)
        dycb = dyc.astype(_MXU)
        dp_all = _mm_nt(dycb, vm_ref[...])
        dsm = []
        for j in range(4):
            pj = pmem_ref[:, MEM_LEN * j:MEM_LEN * (j + 1)].astype(F32)
            dp = dp_all[:, MEM_LEN * j:MEM_LEN * (j + 1)]
            dsm.append((pj * (dp - jnp.sum(pj * dp, axis=-1, keepdims=True))).astype(_MXU))
        ds_all = jnp.concatenate(dsm, axis=1)
        dvm_ref[...] += _mm_tn(dycb, pmem_ref[...])
        dkm_ref[...] += _mm_tn(qxb, ds_all)
        dqx = _mm(ds_all, km_ref[...])
        gxqn_ref[...] += _col_sum(dqx * qn)
        dqn = dqx * gn_ref[:, G_XQ:G_XK]
        dproj_ref[:, C_XQ:C_XQ + XATT_W] = (rq * (dqn - qn * _seg_mean(dqn * qn, gm256))).astype(dproj_ref.dtype)

        gm128 = _group_matrix(KV_W)
        cv, s1v, s2v = c_ref[...], s1_ref[...], s2_ref[...]

        def head_norm(t):
            r = lax.rsqrt(_seg_mean(t * t, gm128) + EPS)
            return t * r, r

        qn_, qr_ = zip(head_norm(proj_ref[:, C_SQ:C_SQ + 128]), head_norm(proj_ref[:, C_SQ + 128:C_SQ + 256]))
        qrope = [_rope(qn_[h] * gn_ref[:, G_Q:G_K], cv, s1v, s2v).astype(_MXU) for h in range(2)]
        kn, krr = head_norm(proj_ref[:, C_SK:C_SK + KV_W])
        kr = _rope(kn * gn_ref[:, G_K:G_XQ], cv, s1v, s2v)
        khn, _ = head_norm(kvh_ref[:, 0:KV_W])
        khr = _rope(khn * gn_ref[:, G_K:G_XQ], ch_ref[...], s1h_ref[...], s2h_ref[...])
        ka = _place_kv(jnp.concatenate([khr, kr], axis=0), 0.125)
        va = _place_kv(jnp.concatenate([kvh_ref[:, KV_W:2 * KV_W], proj_ref[:, C_SV:C_SV + KV_W]], axis=0), 1.0)
        lane128 = lax.broadcasted_iota(jnp.int32, (1, 128), 1)
        gsink = jnp.zeros((1, 128), F32)
        dk_band, dv_band, dq_blk = [], [], []
        for b in range(nb):
            band = slice(BLOCK * b, BLOCK * b + 2 * BLOCK)
            blk = slice(BLOCK * b, BLOCK * (b + 1))
            dka, dva, dsb = [], [], []
            deltas = jnp.zeros((BLOCK, 128), F32)
            for j in range(4):
                qh = qrope[j // 2][blk]
                doh = dyb[blk, KV_W * (j // 2):KV_W * (j // 2 + 1)].astype(_MXU)
                pb = pswa_ref[blk, 2 * BLOCK * j:2 * BLOCK * (j + 1)]
                p = pb.astype(F32)
                dp = _mm_nt(doh, va[j][band])
                delta = jnp.sum(p * dp, axis=-1, keepdims=True)
                ds = (p * (dp - delta)).astype(_MXU)
                deltas = jnp.where(lane128 == j, delta, deltas)
                dva.append(_mm_tn(pb, doh))
                dka.append(_mm_tn(ds, qh))
                dsb.append(ds)
            gsink = gsink - _col_sum(psink_ref[blk, :] * deltas)
            dk_band.append(_unplace_kv(dka) * 0.125)
            dv_band.append(_unplace_kv(dva))
            dq_blk.append([_mm(jnp.concatenate(dsb[2 * h:2 * h + 2], axis=1),
                               jnp.concatenate([ka[2 * h][band], ka[2 * h + 1][band]], axis=0)) for h in range(2)])
        gsink_ref[...] += gsink
        dk_rows = [dk_band[b][BLOCK:] + (dk_band[b + 1][:BLOCK] if b + 1 < nb else dkcar_ref[...]) for b in range(nb)]
        dv_rows = [dv_band[b][BLOCK:] + (dv_band[b + 1][:BLOCK] if b + 1 < nb else dvcar_ref[...]) for b in range(nb)]
        dkcar_ref[...] = dk_band[0][:BLOCK]
        dvcar_ref[...] = dv_band[0][:BLOCK]
        dkg = _rope_bwd(jnp.concatenate(dk_rows, axis=0), cv, s1v, s2v)
        gkn = _col_sum(dkg * kn)
        dkn = dkg * gn_ref[:, G_K:G_XQ]
        dproj_ref[:, C_SK:C_SK + KV_W] = (krr * (dkn - kn * _seg_mean(dkn * kn, gm128))).astype(dproj_ref.dtype)
        dproj_ref[:, C_SV:C_SV + KV_W] = jnp.concatenate(dv_rows, axis=0).astype(dproj_ref.dtype)
        gqn = jnp.zeros((1, 128), F32)
        for h in range(2):
            dqg = _rope_bwd(jnp.concatenate([dq_blk[b][h] for b in range(nb)], axis=0), cv, s1v, s2v)
            gqn = gqn + _col_sum(dqg * qn_[h])
            dqn_ = dqg * gn_ref[:, G_Q:G_K]
            dproj_ref[:, C_SQ + 128 * h:C_SQ + 128 * (h + 1)] = (
                qr_[h] * (dqn_ - qn_[h] * _seg_mean(dqn_ * qn_[h], gm128))).astype(dproj_ref.dtype)
        gqn_ref[...] += gqn
        gkn_ref[...] += gkn

        u = proj_ref[:, C_LRUX:C_LRUX + LRU_W]
        xc, rg, ig, sq = (gates_ref[:, LRU_W * k:LRU_W * (k + 1)].astype(F32) for k in range(4))
        a = a_ref[...]
        sp = _softplus(-lam_ref[...])
        hext_ref[0:8, :] = jnp.where(first_tile, 0.0, yah_ref[...])
        hext_ref[8:8 + tm, :] = ya_ref[...]
        hprev = hext_ref[pl.ds(7, tm), :]
        aext_ref[0:tm, :] = a
        an_scr[...] = aext_ref[pl.ds(1, tm), :]
        dh_scr[...] = dya
        dh_scr[tm - 1:tm, :] = dh_scr[tm - 1:tm, :] + gcar_ref[0:1, :]
        row8 = lax.broadcasted_iota(jnp.int32, (8, LRU_W), 0)

        def scan_step(gi, carry):
            r0 = pl.multiple_of((tm // 8 - 1 - gi) * 8, 8)
            av = an_scr[pl.ds(r0, 8), :]
            bv = dh_scr[pl.ds(r0, 8), :]
            for d in (1, 2, 4):
                a_sh = jnp.where(row8 < 8 - d, pltpu.roll(av, 8 - d, 0), 1.0)
                b_sh = jnp.where(row8 < 8 - d, pltpu.roll(bv, 8 - d, 0), 0.0)
                bv = bv + av * b_sh
                av = av * a_sh
            gv = bv + av * carry
            g_scr[pl.ds(r0, 8), :] = gv
            return gv[0:1, :]

        g0 = lax.fori_loop(0, tm // 8, scan_step, jnp.zeros((1, LRU_W), F32), unroll=True)
        gcar_ref[0:1, :] = a[0:1, :] * g0
        gv = g_scr[...]
        da = gv * hprev
        dig = gv * sq * xc
        dxc = gv * sq * ig
        dla = da * a - gv * (ig * xc) * ((a * a) / sq)
        drg = dla * ((-LRU_C) * sp)
        glam_ref[...] += _col_sum(dla * rg)
        dpr = drg * rg * (1.0 - rg)
        dpi = dig * ig * (1.0 - ig)
        gbrg_ref[...] += _col_sum(dpr)
        gbig_ref[...] += _col_sum(dpi)
        dpre0 = jnp.concatenate([dpr[:, :256], dpi[:, :256]], axis=1).astype(_MXU)
        dpre1 = jnp.concatenate([dpr[:, 256:], dpi[:, 256:]], axis=1).astype(_MXU)
        gwg_ref[0] += _mm_tn(xc[:, :256], dpre0)
        gwg_ref[1] += _mm_tn(xc[:, 256:], dpre1)
        dxc = dxc + jnp.concatenate([_mm_nt(dpre0, wg_ref[0]), _mm_nt(dpre1, wg_ref[1])], axis=1)
        gcb_ref[...] += _col_sum(dxc)
        dxc_ext[0:tm, :] = dxc
        du = jnp.zeros((tm, LRU_W), F32)
        for k in range(CONV_K):
            later = dxc_ext[pl.ds(3 - k, tm), :]
            gcw_ref[k:k + 1, :] += _col_sum(later * u)
            du = du + cw_ref[k:k + 1, :] * later
        dxc_ext[tm:tm + 8, :] = dxc[0:8, :]
        dproj_ref[:, C_LRUX:C_LRUX + LRU_W] = du.astype(dproj_ref.dtype)

        dxn = _mm(dproj_ref[...], win_ref[...])
        rx = lax.rsqrt(_row_mean(xv * xv) + EPS)
        xh = xv * rx
        gng_ref[...] += _col_sum(dxn * xh)
        dxh = dxn * ng_ref[...]
        gx_ref[...] = dov.astype(F32) + rx * (dxh - xh * _row_mean(dxh * xh))

        @pl.when(i == nt - 1)
        def _():
            glam_ref[...] = glam_ref[...] * (LRU_C * _sigmoid(-lam_ref[...]))
            for r in (gqn_ref, gkn_ref, gxqn_ref):
                r[...] = _fold_heads(r[...])

    def rows(ncol, arr_cols_block=0):
        return pl.BlockSpec((tm, ncol), lambda i: (nt - 1 - i, arr_cols_block))

    def halo(nrow, ncol, colblk=0):
        per = tm // nrow
        return pl.BlockSpec((nrow, ncol), lambda i: (jnp.maximum((nt - 1 - i) * per - 1, 0), colblk))

    in_specs = [rows(D_MODEL), rows(D_MODEL), rows(D_IN), rows(LRU_W), rows(SWA_W), rows(XATT_W),
                rows(4 * 2 * BLOCK), rows(4 * MEM_LEN), rows(128), rows(4 * LRU_W), rows(LRU_W),
                rows(128), rows(128), rows(128),
                halo(8, LRU_W), halo(BLOCK, 2 * KV_W, C_SK // (2 * KV_W)),
                halo(BLOCK, 128), halo(BLOCK, 128), halo(BLOCK, 128),
                _const_spec((1, D_MODEL)), _const_spec((D_IN, D_MODEL), True), _const_spec((CONV_K, LRU_W)),
                _const_spec((2, 256, 512), True), _const_spec((1, LRU_W)), _const_spec((1, GAINS_W)),
                _const_spec((4 * MEM_LEN, XATT_W), True), _const_spec((4 * MEM_LEN, XATT_W), True),
                _const_spec((1, D_MODEL)), _const_spec((D_MODEL, D_MODEL), True)]
    small = [(2, 256, 512), (XATT_W, 4 * MEM_LEN), (XATT_W, 4 * MEM_LEN), (1, D_MODEL), (1, D_MODEL), (1, LRU_W), (1, LRU_W),
             (1, LRU_W), (1, LRU_W), (CONV_K, LRU_W), (1, 128), (1, 128), (1, XATT_W), (1, 128)]
    out_shape = (jax.ShapeDtypeStruct((seq, D_MODEL), F32), jax.ShapeDtypeStruct((seq, D_IN), _MXU)) + tuple(
        jax.ShapeDtypeStruct(s, F32) for s in small)
    out_specs = (rows(D_MODEL), rows(D_IN)) + tuple(_const_spec(s) for s in small)
    scratch = [pltpu.VMEM((tm + 8, LRU_W), F32), pltpu.VMEM((tm + 8, LRU_W), F32),
               pltpu.VMEM((tm, LRU_W), F32), pltpu.VMEM((tm, LRU_W), F32), pltpu.VMEM((tm, LRU_W), F32),
               pltpu.VMEM((tm + 8, LRU_W), F32),
               pltpu.VMEM((8, LRU_W), F32), pltpu.VMEM((BLOCK, KV_W), F32), pltpu.VMEM((BLOCK, KV_W), F32)]
    return pl.pallas_call(
        body, name="layer_bwd", grid=(nt,), out_shape=out_shape, in_specs=in_specs, out_specs=out_specs,
        scratch_shapes=scratch,
        compiler_params=pltpu.CompilerParams(dimension_semantics=("arbitrary",), vmem_limit_bytes=VMEM_LIMIT),
    )(x, dout, proj, ya, yb, yc, pswa, pmem, psink, gates, a_all, rc, rs1, rs2, ya, proj, rc, rs1, rs2,
      ng, win_t, cw, wg, lam, gains, km, vm, og, wout)


def _reduce_protocol(big, sm, outs, osm, r1, r1s, wire, r2, r2s, wire2, ps, own, send, recv, lsem):
    nbig = len(big)
    x, y, c = lax.axis_index("x"), lax.axis_index("y"), lax.axis_index("c")
    sibling = (x, y, 1 - c)
    near, far, diag = _partners(x, y, c)
    me, near_id, far_id, diag_id = _chip_of(x, y), _chip_of(*near), _chip_of(*far), _chip_of(*diag)

    def copy(k, src, dst, to):
        return pltpu.make_async_remote_copy(src_ref=src, dst_ref=dst, send_sem=send.at[k], recv_sem=recv.at[k],
                                            device_id=to, device_id_type=MESH)

    def sent(stage, a):
        if a == nbig:
            src, dst, to = ((sm.at[1 - c], r1s, sibling), (r1s, r2s.at[0], (*near, c)), (ps, r2s.at[1], (*far, c)),
                            (osm.at[c], osm.at[c], sibling))[stage]
            return [copy(5 * nbig + stage, src, dst, to)]
        if stage == 0:
            return [copy(5 * a, big[a].at[:, 1 - c], r1[a], sibling)]
        if stage == 1:
            return [copy(5 * a + 1, wire[a].at[near_id], r2[a].at[0], (*near, c)),
                    copy(5 * a + 2, wire[a].at[diag_id], r2[a].at[1], (*near, c))]
        if stage == 2:
            return [copy(5 * a + 3, wire2[a], r2[a].at[2], (*far, c))]
        return [copy(5 * a + 4, outs[a].at[c], outs[a].at[c], sibling)]

    arrays = range(nbig + (sm is not None))

    def start(stage, a):
        for cp in sent(stage, a):
            cp.start()

    def arrived(k, ref):
        copy(k, ref, ref, sibling).wait_recv()

    def loads():
        return [pltpu.make_async_copy(big[a].at[:, c], own[a], lsem.at[a]) for a in range(nbig)]

    def stage0():
        for a in arrays:
            start(0, a)
        for cp in loads():
            cp.start()

    def stage1():
        for a in range(nbig):
            loads()[a].wait()
            arrived(5 * a, r1[a])
            for k in range(N_CHIPS):
                r1[a][k] = own[a][k] + r1[a][k]
                wire[a][k] = r1[a][k].astype(wire[a].dtype)
            start(1, a)
        if sm is not None:
            arrived(5 * nbig, r1s)
            r1s[...] = sm[c] + r1s[...]
            start(1, nbig)

    def stage2():
        for a in range(nbig):
            arrived(5 * a + 1, r2[a].at[0])
            arrived(5 * a + 2, r2[a].at[1])
            r1[a][me] = r1[a][me] + r2[a][0].astype(F32)
            wire2[a][...] = (r1[a][far_id] + r2[a][1].astype(F32)).astype(wire2[a].dtype)
            start(2, a)
        if sm is not None:
            arrived(5 * nbig + 1, r2s.at[0])
            ps[...] = r1s[...] + r2s[0]
            start(2, nbig)

    def stage3():
        for a in range(nbig):
            arrived(5 * a + 3, r2[a].at[2])
            outs[a][c] = r1[a][me] + r2[a][2].astype(F32)
            start(3, a)
        if sm is not None:
            arrived(5 * nbig + 2, r2s.at[1])
            osm[c] = ps[...] + r2s[1]
            start(3, nbig)

    def stage4():
        for a in range(nbig):
            arrived(5 * a + 4, outs[a].at[1 - c])
        if sm is not None:
            arrived(5 * nbig + 3, osm.at[1 - c])
        for stage in range(4):
            for a in arrays:
                for cp in sent(stage, a):
                    cp.wait_send()

    return [stage0, stage1, stage2, stage3, stage4]


def _reduce_buffers(bigs, g_small):
    half = [b.shape[2:] for b in bigs]
    sm_half = None if g_small is None else g_small.shape[1:]
    out_shape = [jax.ShapeDtypeStruct((2,) + h, F32) for h in half]
    small = lambda lead: [] if g_small is None else [pltpu.VMEM(lead + sm_half, F32)]
    if g_small is not None:
        out_shape.append(jax.ShapeDtypeStruct(g_small.shape, F32))
    n_sem = 5 * len(bigs) + 4
    scratch = ([pltpu.VMEM((N_CHIPS,) + h, F32) for h in half] + small(())
               + [pltpu.VMEM((N_CHIPS,) + h, _WIRE) for h in half]
               + [pltpu.VMEM((3,) + h, _WIRE) for h in half] + small((2,))
               + [pltpu.VMEM(h, _WIRE) for h in half] + small(())
               + [pltpu.VMEM((N_CHIPS,) + h, F32) for h in half]
               + [pltpu.SemaphoreType.DMA((n_sem,)), pltpu.SemaphoreType.DMA((n_sem,)),
                  pltpu.SemaphoreType.DMA((len(bigs),))])
    return out_shape, scratch


def _split_reduce_refs(refs, nbig, has_small):
    it = iter(refs)
    take = lambda n: [next(it) for _ in range(n)]
    one = lambda: next(it) if has_small else None
    big, sm = take(nbig), one()
    outs, osm = take(nbig), one()
    r1, r1s, wire, r2, r2s, wire2, ps, own = take(nbig), one(), take(nbig), take(nbig), one(), take(nbig), one(), take(nbig)
    send, recv, lsem = take(3)
    return big, sm, outs, osm, r1, r1s, wire, r2, r2s, wire2, ps, own, send, recv, lsem


def _hosted_reduce_shapes(bigs, g_small):
    red_shape, scratch = _reduce_buffers(bigs, g_small)
    nres = len(red_shape)
    return red_shape, [pltpu.VMEM(r.shape, r.dtype) for r in red_shape] + scratch + [pltpu.SemaphoreType.DMA((nres,))]


def _hosted_reduce(step, n_steps, closing, stage_at, operands, results, scratch, has_small):
    nres = len(results)
    sums, rest, fsem = scratch[:nres], scratch[nres:-1], scratch[-1]
    refs = tuple(operands) + tuple(sums) + tuple(rest)

    def to_results():
        out = [pltpu.make_async_copy(sums[k], results[k], fsem.at[k]) for k in range(nres)]
        for cp in out:
            cp.start()
        for cp in out:
            cp.wait()

    stages = _reduce_protocol(*_split_reduce_refs(refs, nres - has_small, has_small))

    def last_stage():
        stages[-1]()
        to_results()

    for at, stage in zip(stage_at, stages[:-1] + [last_stage]):
        if closing == (at == n_steps):
            pl.when(step == min(at, n_steps - 1))(stage)


def reduce_grads(big, name, parts):
    chips, halves, rows_, cols = big.shape
    sub = jax.ShapeDtypeStruct((chips, halves, rows_ // parts, cols), big.dtype)

    def body(b_ref, o_ref, *scratch):
        refs = [b_ref.at[:, :, s] for s in range(parts)] + [o_ref.at[:, s] for s in range(parts)] + list(scratch)
        for stage in _reduce_protocol(*_split_reduce_refs(refs, parts, False)):
            stage()

    _, scratch = _reduce_buffers([sub] * parts, None)
    return pl.pallas_call(
        body, name=name, out_shape=jax.ShapeDtypeStruct((halves, parts, rows_ // parts, cols), F32),
        in_specs=[pl.BlockSpec(memory_space=pl.ANY)], out_specs=pl.BlockSpec(memory_space=pltpu.VMEM),
        scratch_shapes=scratch, compiler_params=pltpu.CompilerParams(vmem_limit_bytes=VMEM_LIMIT),
    )(big.reshape(chips, halves, parts, rows_ // parts, cols))


def adamw(items, g_pack, ws, ms, vs):
    plan, total = [], 0
    for w, _, _, _ in items:
        rows_, cols = w.shape
        tr = max(t for t in range(8, rows_ + 1, 8) if rows_ % t == 0 and t * cols * 4 <= ADAM_BLOCK_BYTES)
        plan.append((total, rows_ // tr, tr, cols))
        total += rows_ // tr
    nin, n = 4 * len(items), len(ws)

    def body(*refs):
        i = pl.program_id(0)
        small_in = refs[nin:nin + 1 + 3 * n]
        outs = refs[nin + 1 + 3 * n:]

        @pl.when(i == 0)
        def _():
            _adamw_small(small_in[0], *(small_in[1 + k * n:1 + (k + 1) * n] for k in range(3)),
                         *(outs[nin + k * n:nin + (k + 1) * n] for k in range(4)), outs[-1])

        for k, (first, steps, _, _) in enumerate(plan):
            w_ref, g_ref, m_ref, v_ref = refs[4 * k:4 * k + 4]
            go_ref, d_ref, nm_ref, nv_ref = outs[4 * k:4 * k + 4]

            @pl.when((i >= first) & (i < first + steps))
            def _():
                gv = g_ref[...]
                go_ref[...] = gv
                d_ref[...], nm_ref[...], nv_ref[...] = _adam_update(w_ref[...], gv, m_ref[...], v_ref[...])

    specs, shapes = [], []
    for (first, steps, tr, cols), (w, _, _, _) in zip(plan, items):
        spec = pl.BlockSpec((tr, cols), lambda i, first=first, steps=steps: (jnp.clip(i - first, 0, steps - 1), 0))
        specs += [spec] * 4
        shapes += [jax.ShapeDtypeStruct(w.shape, F32)] * 4
    vm = pl.BlockSpec(memory_space=pltpu.VMEM)
    like = [jax.ShapeDtypeStruct(w.shape, F32) for w in ws]
    res = pl.pallas_call(
        body, name="adamw", grid=(total,), out_shape=(*shapes, *like * 4, jax.ShapeDtypeStruct((1, 1), F32)),
        in_specs=specs + [vm] * (1 + 3 * n), out_specs=(*specs, *[vm] * (4 * n + 1)),
        compiler_params=pltpu.CompilerParams(dimension_semantics=("arbitrary",)),
    )(*[a for item in items for a in item], g_pack, *ws, *ms, *vs)
    return [res[4 * k:4 * k + 4] for k in range(len(items))], res[nin:]


def _adam_update(w, g, m, v):
    nm = ADAM_B1 * m + (1.0 - ADAM_B1) * g
    nv = ADAM_B2 * v + (1.0 - ADAM_B2) * (g * g)
    m_hat = nm / (1.0 - ADAM_B1 ** ADAM_STEP)
    v_hat = nv / (1.0 - ADAM_B2 ** ADAM_STEP)
    return (-ADAM_LR) * (m_hat / (jnp.sqrt(v_hat) + ADAM_EPS) + ADAM_WD * w), nm, nv


def _adamw_small(pk, w_refs, m_refs, v_refs, g_out, d_out, nm_out, nv_out, loss_ref):
    nvec = len(SMALL_VECTORS)
    loss_ref[...] = pk[LOSS_ROW:LOSS_ROW + 1, 0:1]
    chip = 2 * lax.axis_index("x") + lax.axis_index("y")
    for k, (name, row, width) in enumerate(SMALL_VECTORS):
        if name == "conv_w":
            g = jnp.concatenate([pk[pl.ds(row + 4 * t + chip, 1), :] for t in range(CONV_K)], axis=0)[None]
        elif width >= 128:
            g = jnp.concatenate([pk[row + r:row + r + 1, :] for r in range(width // 128)], axis=1)
        else:
            g = pk[row:row + 1, 0:width]
        g_out[k][...] = g
        d_out[k][...], nm_out[k][...], nv_out[k][...] = _adam_update(w_refs[k][...], g, m_refs[k][...], v_refs[k][...])
    for k in range(nvec, nvec + len(SMALL_MATRICES)):
        for b in range(LRU_BLOCKS):
            rows_ = pk[GATES_ROW + HEAD * b:GATES_ROW + HEAD * (b + 1), :]
            g = (pltpu.roll(rows_, HEAD, axis=1) if k > nvec else rows_)[:, 0:HEAD]
            g_out[k][0, b] = g
            d_out[k][0, b], nm_out[k][0, b], nv_out[k][0, b] = _adam_update(
                w_refs[k][0, b], g, m_refs[k][0, b], v_refs[k][0, b])


SMALL_VECTORS = (("norm_g", 512, 1024), ("mem_norm_g", 520, 1024), ("conv_w", 528, 512), ("conv_b", 544, 512),
                 ("b_rg", 548, 512), ("b_ig", 552, 512), ("lru_lambda", 556, 512), ("q_norm_g", 560, 64),
                 ("k_norm_g", 561, 64), ("sinks", 562, 4), ("xq_norm_g", 563, 64), ("xk_norm_g", 564, 64),
                 ("out_norm_g", 565, 1024))
LOSS_ROW = 573
SMALL_MATRICES = ("w_rg", "w_ig")
GATES_ROW = 0
SMALL_ROWS = 640


def _rope_tables(seq):
    pos = np.arange(seq, dtype=np.float32)
    inv_freq = (np.float32(ROPE_THETA) ** (-(np.arange(0, ROPE_DIM, 2, dtype=np.float32) / np.float32(ROPE_DIM)))
                ).astype(np.float32)
    ang = (pos[:, None] * inv_freq[None, :]).astype(np.float32)
    cos, sin = np.cos(ang).astype(np.float32), np.sin(ang).astype(np.float32)
    z = lambda n: np.zeros((seq, n), np.float32)
    c64 = np.concatenate([cos, cos, np.ones((seq, HEAD - ROPE_DIM), np.float32)], axis=1)
    s1_64 = np.concatenate([-sin, z(HEAD - 8)], axis=1)
    s2_64 = np.concatenate([z(8), sin, z(HEAD - ROPE_DIM)], axis=1)
    return tuple(jnp.asarray(np.concatenate([t, t], axis=1)) for t in (c64, s1_64, s2_64))


def kernel(x, mem, norm_g, mem_norm_g, w_in, conv_w, conv_b, w_rg, b_rg, w_ig, b_ig, lru_lambda, q_norm_g, k_norm_g, sinks, w_mem_kv, xq_norm_g, xk_norm_g, out_norm_g, w_out, loss_target, m_norm_g, m_mem_norm_g, m_w_in, m_conv_w, m_conv_b, m_w_rg, m_b_rg, m_w_ig, m_b_ig, m_lru_lambda, m_q_norm_g, m_k_norm_g, m_sinks, m_w_mem_kv, m_xq_norm_g, m_xk_norm_g, m_out_norm_g, m_w_out, v_norm_g, v_mem_norm_g, v_w_in, v_conv_w, v_conv_b, v_w_rg, v_b_rg, v_w_ig, v_b_ig, v_lru_lambda, v_q_norm_g, v_k_norm_g, v_sinks, v_w_mem_kv, v_xq_norm_g, v_xk_norm_g, v_out_norm_g, v_w_out):
    seq = x.shape[1]
    xs, tgt, mems = x[0], loss_target[0], mem[0]

    win_t, wout, wkv, cw, wg, gains, km, vm = gather_weights(
        w_in[0].T, w_out[0], w_mem_kv[0], conv_w, w_rg, w_ig, (q_norm_g, k_norm_g, xq_norm_g, xk_norm_g), mems,
        mem_norm_g)
    rc, rs1, rs2 = _rope_tables(seq)
    proj, ya, yb, yc, ycat, xn, dout, pswa, pmem, psink, gates, a_all, loss8 = layer_fwd(
        xs, tgt, rc, rs1, rs2, norm_g, win_t, cw, conv_b, wg, b_rg, b_ig, lru_lambda, gains, sinks, km, vm,
        out_norm_g, wout)
    (gx, dproj, g_wg, dkm, dvm, g_ng, g_og, g_cb, g_brg, g_big, g_lam, g_cw, g_qn, g_kn, g_xqn, g_sink) = layer_bwd(
        xs, dout, proj, ya, yb, yc, pswa, pmem, psink, gates, a_all, rc, rs1, rs2, norm_g, win_t, cw, wg, lru_lambda,
        gains, km, vm, out_norm_g, wout)
    g_win_t, r_out, r_kv, r_small = weight_grads(
        ycat, dout, dproj, xn, (mems, mem_norm_g, wkv, gains, dkm, dvm, g_wg, loss8), dict(
            norm_g=g_ng, conv_w=g_cw, conv_b=g_cb, b_rg=g_brg, b_ig=g_big, lru_lambda=g_lam, q_norm_g=g_qn,
            k_norm_g=g_kn, sinks=g_sink, xq_norm_g=g_xqn, out_norm_g=g_og), (0, 1, 4, 7, 8), (4, 5, 9, 11, 13))
    r_in = reduce_grads(g_win_t.reshape(N_CHIPS, 2, D_IN // 8, D_MODEL), "reduce_w_in", 6)

    r_small = r_small.reshape(SMALL_ROWS, 128)
    grads = {}
    weights = dict(norm_g=norm_g, mem_norm_g=mem_norm_g, w_in=w_in, conv_w=conv_w, conv_b=conv_b, w_rg=w_rg, b_rg=b_rg,
                   w_ig=w_ig, b_ig=b_ig, lru_lambda=lru_lambda, q_norm_g=q_norm_g, k_norm_g=k_norm_g, sinks=sinks,
                   w_mem_kv=w_mem_kv, xq_norm_g=xq_norm_g, xk_norm_g=xk_norm_g, out_norm_g=out_norm_g, w_out=w_out)
    ms = dict(norm_g=m_norm_g, mem_norm_g=m_mem_norm_g, w_in=m_w_in, conv_w=m_conv_w, conv_b=m_conv_b, w_rg=m_w_rg,
              b_rg=m_b_rg, w_ig=m_w_ig, b_ig=m_b_ig, lru_lambda=m_lru_lambda, q_norm_g=m_q_norm_g, k_norm_g=m_k_norm_g,
              sinks=m_sinks, w_mem_kv=m_w_mem_kv, xq_norm_g=m_xq_norm_g, xk_norm_g=m_xk_norm_g,
              out_norm_g=m_out_norm_g, w_out=m_w_out)
    vs = dict(norm_g=v_norm_g, mem_norm_g=v_mem_norm_g, w_in=v_w_in, conv_w=v_conv_w, conv_b=v_conv_b, w_rg=v_w_rg,
              b_rg=v_b_rg, w_ig=v_w_ig, b_ig=v_b_ig, lru_lambda=v_lru_lambda, q_norm_g=v_q_norm_g, k_norm_g=v_k_norm_g,
              sinks=v_sinks, w_mem_kv=v_w_mem_kv, xq_norm_g=v_xq_norm_g, xk_norm_g=v_xk_norm_g,
              out_norm_g=v_out_norm_g, w_out=v_w_out)

    delta, new_m, new_v = {}, {}, {}
    small_names = [n for n, _, _ in SMALL_VECTORS] + list(SMALL_MATRICES)
    (res_in, res_out, res_kv), res = adamw(
        [(w_in[0].T, r_in.reshape(D_IN // 4, D_MODEL), m_w_in[0].T, v_w_in[0].T),
         (w_out[0], r_out.reshape(D_MODEL // 4, D_MODEL), m_w_out[0], v_w_out[0]),
         (w_mem_kv[0], r_kv.reshape(D_MODEL // 4, 2 * XATT_W), m_w_mem_kv[0], v_w_mem_kv[0])],
        r_small, [weights[n] for n in small_names], [ms[n] for n in small_names], [vs[n] for n in small_names])
    grads["w_in"], delta["w_in"], new_m["w_in"], new_v["w_in"] = (r.T[None] for r in res_in)
    grads["w_out"], delta["w_out"], new_m["w_out"], new_v["w_out"] = (r[None] for r in res_out)
    grads["w_mem_kv"], delta["w_mem_kv"], new_m["w_mem_kv"], new_v["w_mem_kv"] = (r[None] for r in res_kv)
    nall = len(small_names)
    for k, into in enumerate((grads, delta, new_m, new_v)):
        into.update(zip(small_names, res[k * nall:(k + 1) * nall]))
    loss = res[-1].reshape(())

    order = ("norm_g", "mem_norm_g", "w_in", "conv_w", "conv_b", "w_rg", "b_rg", "w_ig", "b_ig", "lru_lambda",
             "q_norm_g", "k_norm_g", "sinks", "w_mem_kv", "xq_norm_g", "xk_norm_g", "out_norm_g", "w_out")
    return (loss, gx[None], *[grads[n] for n in order], *[delta[n] for n in order], *[new_m[n] for n in order],
            *[new_v[n] for n in order])
```

```python
import jax
import jax.numpy as jnp
import numpy as np
from jax import lax
from jax.experimental import pallas as pl
from jax.experimental.pallas import tpu as pltpu

F32 = jnp.float32
_MXU = jnp.bfloat16
_WIRE = jnp.bfloat16

D_MODEL = 1024
MEM_LEN = 256
HEAD = 64
LRU_W = 512
LRU_BLOCKS = 8
CONV_K = 4
LRU_C = 8.0
SWA_W = 256
KV_W = 128
XATT_W = 256
BLOCK = 128
D_IN = 2304
ROPE_THETA = 500000.0
ROPE_DIM = 16
EPS = 1e-6
NEG_INF = -1e30
C_LRUX, C_LRUG, C_SQ, C_SK, C_SV, C_SWAG, C_XQ, C_XG = 0, 512, 1024, 1280, 1408, 1536, 1792, 2048
G_Q, G_K, G_XQ, G_XK, GAINS_W = 0, 128, 256, 512, 768

ADAM_LR, ADAM_B1, ADAM_B2, ADAM_EPS, ADAM_WD, ADAM_STEP = 0.001, 0.9, 0.999, 1e-08, 0.01, 10

N_CHIPS = 4
ROW_TILE = 256
VMEM_LIMIT = 56 * 1024 * 1024
ADAM_BLOCK_BYTES = 640 * 1024
MESH = pl.DeviceIdType.MESH


def _mm(a, b):
    return jnp.dot(a.astype(_MXU), b.astype(_MXU), preferred_element_type=F32)


def _mm_nt(a, b):
    return lax.dot_general(a.astype(_MXU), b.astype(_MXU), (((1,), (1,)), ((), ())), preferred_element_type=F32)


def _mm_tn(a, b):
    return lax.dot_general(a.astype(_MXU), b.astype(_MXU), (((0,), (0,)), ((), ())), preferred_element_type=F32)


def _group_matrix(width):
    r = lax.shift_right_logical(lax.broadcasted_iota(jnp.int32, (width, width), 0), 6)
    c = lax.shift_right_logical(lax.broadcasted_iota(jnp.int32, (width, width), 1), 6)
    return (r == c).astype(_MXU)


def _seg_mean(x, gm):
    return jnp.dot(x.astype(_MXU), gm, preferred_element_type=F32) * (1.0 / HEAD)


def _row_mean(x):
    return jnp.mean(x, axis=-1, keepdims=True)


def _col_sum(x):
    return jnp.sum(x, axis=0, keepdims=True)


def _sigmoid(x):
    return jax.nn.sigmoid(x)


def _softplus(z):
    e = jnp.exp(-jnp.abs(z))
    u = 1.0 + e
    log1p_e = jnp.where(u == 1.0, e, jnp.log(u) * (e / (u - 1.0)))
    return jnp.maximum(z, 0.0) + log1p_e


def _rope(t, c, s1, s2):
    return t * c + pltpu.roll(t, 120, 1) * s1 + pltpu.roll(t, 8, 1) * s2


def _rope_bwd(d, c, s1, s2):
    return d * c + pltpu.roll(d * s1, 8, 1) + pltpu.roll(d * s2, 120, 1)


def _fold_heads(v):
    out = v
    for k in range(1, v.shape[1] // HEAD):
        out = out + pltpu.roll(v, HEAD * k, 1)
    return out


def _lane_mask(width, lo, hi):
    lane = lax.broadcasted_iota(jnp.int32, (1, width), 1)
    return ((lane >= lo) & (lane < hi)).astype(F32)


def _swa_mask(first_block):
    qi = lax.broadcasted_iota(jnp.int32, (BLOCK, 2 * BLOCK), 0)
    kj = lax.broadcasted_iota(jnp.int32, (BLOCK, 2 * BLOCK), 1)
    rel = qi + BLOCK - kj
    ok = (rel >= 0) & (rel < BLOCK)
    return ok & (jnp.logical_not(first_block) | (kj >= BLOCK))


def _place_kv(t, scale):
    lo = t * (_lane_mask(KV_W, 0, HEAD) * scale)
    hi = t * (_lane_mask(KV_W, HEAD, KV_W) * scale)
    return [a.astype(_MXU) for a in (lo, pltpu.roll(lo, HEAD, 1), pltpu.roll(hi, HEAD, 1), hi)]


def _unplace_kv(d):
    return (_lane_mask(KV_W, 0, HEAD) * (d[0] + pltpu.roll(d[1], HEAD, 1))
            + _lane_mask(KV_W, HEAD, KV_W) * (d[3] + pltpu.roll(d[2], HEAD, 1)))


def _swa_probs(qh, ka, mask, sink):
    s = _mm_nt(qh, ka)
    s = jnp.where(mask, s, NEG_INF)
    m = jnp.maximum(jnp.max(s, axis=-1, keepdims=True), sink)
    p = jnp.exp(s - m)
    esink = jnp.exp(sink - m)
    inv = 1.0 / (jnp.sum(p, axis=-1, keepdims=True) + esink)
    return p * inv, esink * inv


def _mem_probs(s_all):
    out = []
    for j in range(4):
        s = s_all[:, MEM_LEN * j:MEM_LEN * (j + 1)]
        p = jnp.exp(s - jnp.max(s, axis=-1, keepdims=True))
        out.append(p * (1.0 / jnp.sum(p, axis=-1, keepdims=True)))
    return out


def _head_rows(t, scale):
    return jnp.concatenate([t * (_lane_mask(XATT_W, HEAD * j, HEAD * (j + 1)) * scale) for j in range(4)], axis=0)


def _lru_gates(xc, wg_ref, brg, big, lam):
    p0 = _mm(xc[:, :256], wg_ref[0])
    p1 = _mm(xc[:, 256:], wg_ref[1])
    rg = _sigmoid(jnp.concatenate([p0[:, :256], p1[:, :256]], axis=1) + brg)
    ig = _sigmoid(jnp.concatenate([p0[:, 256:], p1[:, 256:]], axis=1) + big)
    sp = _softplus(-lam)
    la = (-LRU_C) * rg * sp
    a = jnp.exp(la)
    th = jnp.tanh(la)
    one_minus_a2 = (-2.0 * th) / (1.0 - th)
    return rg, ig, sp, a, jnp.sqrt(one_minus_a2)


def _const_spec(shape, single=False):
    zeros = (0,) * len(shape)
    if single:
        return pl.BlockSpec(shape, lambda i: zeros, pipeline_mode=pl.Buffered(1))
    return pl.BlockSpec(shape, lambda i: zeros)


def _chip_of(x, y):
    return 2 * x + y


def _partners(x, y, c):
    north = c == 1
    near = (jnp.where(north, 1 - x, x), jnp.where(north, y, 1 - y))
    far = (jnp.where(north, x, 1 - x), jnp.where(north, 1 - y, y))
    return near, far, (1 - x, 1 - y)


def gather_weights(win_t, wout, wkv, conv_w, w_rg, w_ig, head_gains, mem, mem_g):
    arrs = (win_t, wout, wkv)
    n = len(arrs)
    pieces = [(a, 0, arr.shape[0] // 2) for a, arr in enumerate(arrs)]
    npc = len(pieces)

    def body(a0, a1, a2, cw_in, wrg_ref, wig_ref, q_ref, k_ref, xq_ref, xk_ref, mem_ref, mg_ref,
             o0, o1, o2, cw_out, wg_ref, gn_ref, km_ref, vm_ref, s0, s1, s2, cw, ocw, send, recv, lsem):
        ins, outs = (s0, s1, s2), (o0, o1, o2)
        for src, dst in zip((a0, a1, a2), ins):
            dst[...] = src[...].astype(dst.dtype)
        cw[...] = jnp.zeros(cw.shape, F32)
        cw[0:CONV_K, :] = cw_in[0]
        x, y, c = lax.axis_index("x"), lax.axis_index("y"), lax.axis_index("c")
        sibling = (x, y, 1 - c)
        near, far, diag = _partners(x, y, c)
        chips = [near, far, diag]
        me = _chip_of(x, y)

        def landed(p, chip, half):
            a, off, rows_ = pieces[p]
            r = ins[a].shape[0]
            return outs[a].at[pl.ds(pl.multiple_of(chip * r + half * (r // 2) + off, 16), rows_)]

        def mine(p):
            a, off, rows_ = pieces[p]
            return ins[a].at[pl.ds(pl.multiple_of(c * (ins[a].shape[0] // 2) + off, 16), rows_)]

        def copy(k, src, dst, to):
            return pltpu.make_async_remote_copy(src_ref=src, dst_ref=dst, send_sem=send.at[k], recv_sem=recv.at[k],
                                                device_id=to, device_id_type=MESH)

        def cw_rows(chip):
            return ocw.at[pl.ds(pl.multiple_of(chip * 8, 8), 8)]

        locals_ = []
        for a in range(n):
            r = ins[a].shape[0]
            locals_.append(pltpu.make_async_copy(ins[a], outs[a].at[pl.ds(pl.multiple_of(me * r, 16), r)], lsem.at[a]))
        locals_.append(pltpu.make_async_copy(cw, cw_rows(me), lsem.at[n]))
        for cp in locals_:
            cp.start()

        sent = []
        for p in range(npc):
            for j in range(2):
                sent.append(copy(p * 6 + j, mine(p), landed(p, me, c), (*chips[j], c)))
        for j, chip in enumerate(chips):
            sent.append(copy(npc * 6 + j, cw, cw_rows(me), (*chip, c)))
        for cp in sent:
            cp.start()

        gn_ref[...] = jnp.concatenate([q_ref[...]] * 2 + [k_ref[...]] * 2 + [xq_ref[...]] * 4 + [xk_ref[...]] * 4,
                                      axis=1)
        zeros = lambda lanes: [jnp.zeros((HEAD, lanes), F32)] if lanes else []
        for h in range(2):
            for b in range(4):
                row = []
                for w_ref in (wrg_ref, wig_ref):
                    row += zeros(HEAD * b) + [w_ref[0, 4 * h + b]] + zeros(HEAD * (3 - b))
                wg_ref[h, HEAD * b:HEAD * (b + 1), :] = jnp.concatenate(row, axis=1).astype(wg_ref.dtype)

        for j in range(3):
            for p in range(npc):
                got = landed(p, _chip_of(*chips[j]), c)
                copy(p * 6 + j, got, got, sibling).wait_recv()
                if j == 0:
                    sent.append(copy(p * 6 + 2, got, got, (*far, c)))
                    sent[-1].start()
                sent.append(copy(p * 6 + 3 + j, got, got, sibling))
                sent[-1].start()
        for p in range(npc):
            for j in range(3):
                got = landed(p, _chip_of(*chips[(1, 0, 2)[j]]), 1 - c)
                copy(p * 6 + 3 + j, got, got, sibling).wait_recv()
        for j, chip in enumerate(chips):
            got = cw_rows(_chip_of(*chip))
            copy(npc * 6 + j, got, got, (*chip, c)).wait_recv()
        for cp in sent:
            cp.wait_send()
        for cp in locals_:
            cp.wait()
        for chip in range(N_CHIPS):
            cw_out[:, 128 * chip:128 * (chip + 1)] = ocw[8 * chip:8 * chip + CONV_K, :]

        mem_v = mem_ref[...]
        mn = mem_v * lax.rsqrt(_row_mean(mem_v * mem_v) + EPS) * mg_ref[...]
        mkv = _mm(mn, o2[...])
        kpre = mkv[:, :XATT_W]
        km = kpre * lax.rsqrt(_seg_mean(kpre * kpre, _group_matrix(XATT_W)) + EPS) * gn_ref[:, G_XK:GAINS_W]
        km_ref[...] = _head_rows(km, 0.125).astype(km_ref.dtype)
        vm_ref[...] = _head_rows(mkv[:, XATT_W:], 1.0).astype(vm_ref.dtype)

    vm = pl.BlockSpec(memory_space=pltpu.VMEM)
    hbm = pl.BlockSpec(memory_space=pl.ANY)
    head_rows = jax.ShapeDtypeStruct((4 * MEM_LEN, XATT_W), _MXU)
    out_shape = tuple(jax.ShapeDtypeStruct((N_CHIPS * a.shape[0],) + a.shape[1:], _MXU) for a in arrs) + (
        jax.ShapeDtypeStruct((CONV_K, LRU_W), F32), jax.ShapeDtypeStruct((2, 256, 512), _MXU),
        jax.ShapeDtypeStruct((1, GAINS_W), F32), head_rows, head_rows)
    n_rdma = npc * 6 + 3
    return pl.pallas_call(
        body, name="gather_weights", out_shape=out_shape,
        in_specs=[vm] * 12, out_specs=(hbm, hbm, vm, vm, vm, vm, vm, vm),
        scratch_shapes=[pltpu.VMEM(a.shape, _MXU) for a in arrs] + [
            pltpu.VMEM((8, 128), F32), pltpu.VMEM((N_CHIPS * 8, 128), F32),
            pltpu.SemaphoreType.DMA((n_rdma,)), pltpu.SemaphoreType.DMA((n_rdma,)), pltpu.SemaphoreType.DMA((n + 1,))],
        compiler_params=pltpu.CompilerParams(vmem_limit_bytes=VMEM_LIMIT),
    )(win_t, wout, wkv, conv_w, w_rg, w_ig, *head_gains, mem, mem_g)


def _mem_bwd_and_pack(mem_ref, g_ref, w_ref, gn_ref, dkm_ref, dvm_ref, gg_ref, loss_ref, vectors, gw_ref, pk_ref):
    first_row = {name: (row, width) for name, row, width in SMALL_VECTORS}
    half_rows = SMALL_ROWS // 2
    pk_ref[...] = jnp.zeros(pk_ref.shape, F32)

    def rows_at(at, n):
        assert at // half_rows == (at + n - 1) // half_rows
        return at // half_rows, slice(at % half_rows, at % half_rows + n), slice(None)

    def put(name, src):
        row, width = first_row[name]
        per_row = 1 if width < 128 else src.shape[1] // 128
        for t in range(src.shape[0]):
            for r in range(per_row):
                pk_ref[rows_at(row + per_row * t + r, 1)] = src[t:t + 1, 128 * r:128 * (r + 1)]

    for name, ref in vectors.items():
        put(name, ref)
    pk_ref[rows_at(LOSS_ROW, 1)] = loss_ref[0:1, :]
    upper = lax.broadcasted_iota(jnp.int32, (HEAD, 128), 1) >= HEAD
    for h in range(2):
        for b in range(4):
            rg = gg_ref[h, HEAD * b:HEAD * (b + 1), 128 * (b // 2):128 * (b // 2 + 1)]
            ig = gg_ref[h, HEAD * b:HEAD * (b + 1), 256 + 128 * (b // 2):256 + 128 * (b // 2 + 1)]
            if b % 2:
                rg = pltpu.roll(rg, HEAD, axis=1)
            else:
                ig = pltpu.roll(ig, HEAD, axis=1)
            pk_ref[rows_at(GATES_ROW + HEAD * (4 * h + b), HEAD)] = jnp.where(upper, ig, rg)

    mem_v = mem_ref[...]
    mh = mem_v * lax.rsqrt(_row_mean(mem_v * mem_v) + EPS)
    mn = mh * g_ref[...]
    mkv = _mm(mn, w_ref[...])
    kpre = mkv[:, :XATT_W]
    gm = _group_matrix(XATT_W)
    rk = lax.rsqrt(_seg_mean(kpre * kpre, gm) + EPS)
    kn = kpre * rk
    dk = jnp.zeros((MEM_LEN, XATT_W), F32)
    dv = jnp.zeros((MEM_LEN, XATT_W), F32)
    for j in range(4):
        mj = _lane_mask(XATT_W, HEAD * j, HEAD * (j + 1))
        dk = dk + dkm_ref[:, MEM_LEN * j:MEM_LEN * (j + 1)].T * (mj * 0.125)
        dv = dv + dvm_ref[:, MEM_LEN * j:MEM_LEN * (j + 1)].T * mj
    put("xk_norm_g", _fold_heads(_col_sum(dk * kn)))
    dkn = dk * gn_ref[:, G_XK:GAINS_W]
    dkpre = rk * (dkn - kn * _seg_mean(dkn * kn, gm))
    dmkv = jnp.concatenate([dkpre, dv], axis=1)
    gw_ref[...] = _mm_tn(mn, dmkv).reshape(gw_ref.shape)
    dmn = _mm_nt(dmkv, w_ref[...])
    put("mem_norm_g", _col_sum(dmn * mh))


def layer_fwd(x, tgt, rc, rs1, rs2, ng, win_t, cw, cb, wg, brg, big, lam, gains, sinks, km, vm, og, wout):
    seq = x.shape[0]
    tm = min(ROW_TILE, seq)
    nt = seq // tm
    nb = tm // BLOCK

    def body(x_ref, t_ref, c_ref, s1_ref, s2_ref, ng_ref, win_ref, cw_ref, cb_ref, wg_ref, brg_ref, big_ref, lam_ref,
             gn_ref, sink_ref, km_ref, vm_ref, og_ref, wout_ref,
             proj_ref, ya_ref, yb_ref, yc_ref, ycat_ref, xn_ref, dout_ref, pswa_ref, pmem_ref, psink_ref, gates_ref,
             a_ref, loss_ref,
             ext_ref, b_scr, hc_ref, kp_ref, vp_ref, lacc_ref):
        i = pl.program_id(0)

        @pl.when(i == 0)
        def _():
            ext_ref[0:8, :] = jnp.zeros((8, LRU_W), F32)
            hc_ref[...] = jnp.zeros_like(hc_ref)
            kp_ref[...] = jnp.zeros_like(kp_ref)
            vp_ref[...] = jnp.zeros_like(vp_ref)
            lacc_ref[...] = jnp.zeros_like(lacc_ref)

        xv = x_ref[...]
        xn = (xv * lax.rsqrt(_row_mean(xv * xv) + EPS) * ng_ref[...]).astype(_MXU)
        xn_ref[...] = xn.astype(xn_ref.dtype)
        proj_ref[...] = _mm_nt(xn, win_ref[...])

        u = proj_ref[:, C_LRUX:C_LRUX + LRU_W]
        ext_ref[8:8 + tm, :] = u
        xc = cb_ref[...]
        for k in range(CONV_K):
            xc = xc + cw_ref[k:k + 1, :] * ext_ref[pl.ds(5 + k, tm), :]
        ext_ref[0:8, :] = u[tm - 8:tm, :]
        rg, ig, sp, a, sq = _lru_gates(xc, wg_ref, brg_ref[...], big_ref[...], lam_ref[...])
        for k, t in enumerate((xc, rg, ig, sq)):
            gates_ref[:, LRU_W * k:LRU_W * (k + 1)] = t.astype(gates_ref.dtype)
        a_ref[...] = a
        b_scr[...] = sq * (ig * xc)
        row8 = lax.broadcasted_iota(jnp.int32, (8, LRU_W), 0)

        def scan_step(g, carry):
            r0 = pl.multiple_of(g * 8, 8)
            av = a_ref[pl.ds(r0, 8), :]
            bv = b_scr[pl.ds(r0, 8), :]
            for d in (1, 2, 4):
                a_sh = jnp.where(row8 >= d, pltpu.roll(av, d, 0), 1.0)
                b_sh = jnp.where(row8 >= d, pltpu.roll(bv, d, 0), 0.0)
                bv = bv + av * b_sh
                av = av * a_sh
            hv = bv + av * carry
            ya_ref[pl.ds(r0, 8), :] = hv
            return hv[7:8, :]

        hc_ref[0:1, :] = lax.fori_loop(0, tm // 8, scan_step, hc_ref[0:1, :], unroll=True)

        gm128 = _group_matrix(KV_W)
        cv, s1v, s2v = c_ref[...], s1_ref[...], s2_ref[...]

        def head_norm_rope(t, g):
            n = t * lax.rsqrt(_seg_mean(t * t, gm128) + EPS)
            return _rope(n * g, cv, s1v, s2v)

        qs_ = (head_norm_rope(proj_ref[:, C_SQ:C_SQ + 128], gn_ref[:, G_Q:G_K]).astype(_MXU),
               head_norm_rope(proj_ref[:, C_SQ + 128:C_SQ + 256], gn_ref[:, G_Q:G_K]).astype(_MXU))
        kr = head_norm_rope(proj_ref[:, C_SK:C_SK + KV_W], gn_ref[:, G_K:G_XQ])
        sv = proj_ref[:, C_SV:C_SV + KV_W]
        ka = _place_kv(jnp.concatenate([kp_ref[...], kr], axis=0), 0.125)
        va = _place_kv(jnp.concatenate([vp_ref[...], sv], axis=0), 1.0)
        kp_ref[...] = kr[tm - BLOCK:tm, :]
        vp_ref[...] = sv[tm - BLOCK:tm, :]
        lane128 = lax.broadcasted_iota(jnp.int32, (1, 128), 1)
        for b in range(nb):
            mask = _swa_mask((i == 0) & (b == 0)) if b == 0 else _swa_mask(False)
            band = slice(BLOCK * b, BLOCK * b + 2 * BLOCK)
            blk = slice(BLOCK * b, BLOCK * (b + 1))
            psink = jnp.zeros((BLOCK, 128), F32)
            for j in range(4):
                p, pk = _swa_probs(qs_[j // 2][blk], ka[j][band], mask, sink_ref[0, j])
                pswa_ref[blk, 2 * BLOCK * j:2 * BLOCK * (j + 1)] = p.astype(pswa_ref.dtype)
                psink = jnp.where(lane128 == j, pk, psink)
            psink_ref[blk, :] = psink
            for h in range(2):
                yb_ref[blk, KV_W * h:KV_W * (h + 1)] = _mm(
                    pswa_ref[blk, 4 * BLOCK * h:4 * BLOCK * (h + 1)],
                    jnp.concatenate([va[2 * h][band], va[2 * h + 1][band]], axis=0))

        gm256 = _group_matrix(XATT_W)
        xq = proj_ref[:, C_XQ:C_XQ + XATT_W]
        qx = xq * lax.rsqrt(_seg_mean(xq * xq, gm256) + EPS) * gn_ref[:, G_XQ:G_XK]
        pm = _mem_probs(_mm_nt(qx, km_ref[...]))
        for j in range(4):
            pmem_ref[:, MEM_LEN * j:MEM_LEN * (j + 1)] = pm[j].astype(pmem_ref.dtype)
        yc = _mm(pmem_ref[...], vm_ref[...])
        yc_ref[...] = yc

        def gated(y, g, gate):
            return y * lax.rsqrt(_row_mean(y * y) + EPS) * g * (gate * _sigmoid(gate))

        ogv = og_ref[...]
        za = gated(ya_ref[...], ogv[:, :512], proj_ref[:, C_LRUG:C_LRUG + LRU_W])
        zb = gated(yb_ref[...], ogv[:, 512:768], proj_ref[:, C_SWAG:C_SWAG + SWA_W])
        zc = gated(yc, ogv[:, 768:], proj_ref[:, C_XG:C_XG + XATT_W])
        ycat_ref[:, 0:512] = za.astype(ycat_ref.dtype)
        ycat_ref[:, 512:768] = zb.astype(ycat_ref.dtype)
        ycat_ref[:, 768:1024] = zc.astype(ycat_ref.dtype)
        out = xv + _mm(ycat_ref[...], wout_ref[...])
        err = out - t_ref[...]
        dout_ref[...] = (err * (1.0 / D_MODEL)).astype(dout_ref.dtype)
        lacc_ref[...] = lacc_ref[...] + (0.5 / D_MODEL) * jnp.sum(err * err)

        @pl.when(i == nt - 1)
        def _():
            loss_ref[...] = lacc_ref[...]

    def rows(ncol):
        return pl.BlockSpec((tm, ncol), lambda i: (i, 0))

    in_specs = [rows(D_MODEL), rows(D_MODEL), rows(128), rows(128), rows(128),
                _const_spec((1, D_MODEL)), _const_spec((D_IN, D_MODEL), True), _const_spec((CONV_K, LRU_W)),
                _const_spec((1, LRU_W)), _const_spec((2, 256, 512), True), _const_spec((1, LRU_W)),
                _const_spec((1, LRU_W)), _const_spec((1, LRU_W)), _const_spec((1, GAINS_W)), pl.BlockSpec(memory_space=pltpu.SMEM),
                _const_spec((4 * MEM_LEN, XATT_W), True), _const_spec((4 * MEM_LEN, XATT_W), True),
                _const_spec((1, D_MODEL)), _const_spec((D_MODEL, D_MODEL), True)]
    out_shape = (jax.ShapeDtypeStruct((seq, D_IN), F32), jax.ShapeDtypeStruct((seq, LRU_W), F32),
                 jax.ShapeDtypeStruct((seq, SWA_W), F32), jax.ShapeDtypeStruct((seq, XATT_W), F32),
                 jax.ShapeDtypeStruct((seq, D_MODEL), _MXU), jax.ShapeDtypeStruct((seq, D_MODEL), _MXU),
                 jax.ShapeDtypeStruct((seq, D_MODEL), _MXU), jax.ShapeDtypeStruct((seq, 4 * 2 * BLOCK), _MXU),
                 jax.ShapeDtypeStruct((seq, 4 * MEM_LEN), _MXU), jax.ShapeDtypeStruct((seq, 128), F32),
                 jax.ShapeDtypeStruct((seq, 4 * LRU_W), _MXU), jax.ShapeDtypeStruct((seq, LRU_W), F32),
                 jax.ShapeDtypeStruct((8, 128), F32))
    out_specs = (rows(D_IN), rows(LRU_W), rows(SWA_W), rows(XATT_W), rows(D_MODEL), rows(D_MODEL), rows(D_MODEL),
                 rows(4 * 2 * BLOCK), rows(4 * MEM_LEN), rows(128), rows(4 * LRU_W), rows(LRU_W),
                 _const_spec((8, 128)))
    scratch = [pltpu.VMEM((tm + 8, LRU_W), F32), pltpu.VMEM((tm, LRU_W), F32),
               pltpu.VMEM((8, LRU_W), F32), pltpu.VMEM((BLOCK, KV_W), F32), pltpu.VMEM((BLOCK, KV_W), F32),
               pltpu.VMEM((8, 128), F32)]
    return pl.pallas_call(
        body, name="layer_fwd", grid=(nt,), out_shape=out_shape, in_specs=in_specs, out_specs=out_specs,
        scratch_shapes=scratch,
        compiler_params=pltpu.CompilerParams(dimension_semantics=("arbitrary",), vmem_limit_bytes=VMEM_LIMIT),
    )(x, tgt, rc, rs1, rs2, ng, win_t, cw, cb, wg, brg, big, lam, gains, sinks, km, vm, og, wout)


def weight_grads(ycat, dout, dproj, xn, mem_operands, vectors, early_at, late_at):
    seq, ncol = xn.shape
    blk = 256
    n_out, n_in = ycat.shape[1] // blk, dproj.shape[1] // blk
    assert n_out == N_CHIPS and late_at[0] >= n_out
    g_out = jax.ShapeDtypeStruct((N_CHIPS, 2, blk // 2, dout.shape[1]), F32)
    g_kv = jax.ShapeDtypeStruct((N_CHIPS, 2, D_MODEL // 8, 2 * XATT_W), F32)
    g_small = jax.ShapeDtypeStruct((2, SMALL_ROWS // 2, 128), F32)
    shape_e, scratch_e = _hosted_reduce_shapes([g_kv], g_small)
    shape_l, scratch_l = _hosted_reduce_shapes([g_out], None)
    n_mem, names = len(mem_operands), tuple(vectors)

    def body(l1_ref, r1_ref, l2_ref, r2_ref, *refs):
        mem_refs, vec_refs = refs[:n_mem], refs[n_mem:n_mem + len(names)]
        o_ref, sum_out, sum_kv, sum_sm, gout_scr, gkv_scr, pack_scr, *scratch = refs[n_mem + len(names):]
        j = pl.program_id(0)

        @pl.when(j == 0)
        def _():
            _mem_bwd_and_pack(*mem_refs, dict(zip(names, vec_refs)), gkv_scr, pack_scr)

        def reduce_stages(closing):
            _hosted_reduce(j, n_out + n_in, closing, early_at, (gkv_scr, pack_scr), (sum_kv, sum_sm),
                           scratch[:len(scratch_e)], True)
            _hosted_reduce(j, n_out + n_in, closing, late_at, (gout_scr,), (sum_out,), scratch[len(scratch_e):], False)

        reduce_stages(False)

        @pl.when(j < n_out)
        def _():
            gout_scr[j] = _mm_tn(l1_ref[...], r1_ref[...]).reshape(g_out.shape[1:])

        @pl.when(j >= n_out)
        def _():
            o_ref[...] = _mm_tn(l2_ref[...], r2_ref[...])

        reduce_stages(True)

    vm = pl.BlockSpec(memory_space=pltpu.VMEM)
    hbm = pl.BlockSpec(memory_space=pl.ANY)
    return pl.pallas_call(
        body, name="weight_grads", grid=(n_out + n_in,),
        out_shape=(jax.ShapeDtypeStruct((dproj.shape[1], ncol), F32), *shape_l, *shape_e),
        in_specs=[pl.BlockSpec((seq, blk), lambda j: (0, jnp.minimum(j, n_out - 1))), _const_spec(dout.shape, True),
                  pl.BlockSpec((seq, blk), lambda j: (0, jnp.maximum(j - n_out, 0))), _const_spec(xn.shape, True)]
        + [vm] * (n_mem + len(names)),
        out_specs=(pl.BlockSpec((blk, ncol), lambda j: (jnp.maximum(j - n_out, 0), 0)), hbm, hbm, hbm),
        scratch_shapes=[pltpu.VMEM(s.shape, F32) for s in (g_out, g_kv, g_small)] + scratch_e + scratch_l,
        compiler_params=pltpu.CompilerParams(dimension_semantics=("arbitrary",), vmem_limit_bytes=VMEM_LIMIT),
    )(ycat, dout, dproj, xn, *mem_operands, *vectors.values())


def layer_bwd(x, dout, proj, ya, yb, yc, pswa, pmem, psink, gates, a_all, rc, rs1, rs2, ng, win_t, cw, wg, lam, gains,
              km, vm, og, wout):
    seq = x.shape[0]
    tm = min(ROW_TILE, seq)
    nt = seq // tm
    nb = tm // BLOCK

    def body(x_ref, dout_ref, proj_ref, ya_ref, yb_ref, yc_ref, pswa_ref, pmem_ref, psink_ref, gates_ref, a_ref,
             c_ref, s1_ref, s2_ref,
             yah_ref, kvh_ref, ch_ref, s1h_ref, s2h_ref,
             ng_ref, win_ref, cw_ref, wg_ref, lam_ref, gn_ref, km_ref, vm_ref, og_ref, wout_ref,
             gx_ref, dproj_ref, gwg_ref, dkm_ref, dvm_ref, gng_ref, gog_ref, gcb_ref, gbrg_ref, gbig_ref, glam_ref,
             gcw_ref, gqn_ref, gkn_ref, gxqn_ref, gsink_ref,
             hext_ref, aext_ref, an_scr, dh_scr, g_scr, dxc_ext, gcar_ref, dkcar_ref, dvcar_ref):
        i = pl.program_id(0)
        tile = nt - 1 - i
        first_tile = tile == 0

        @pl.when(i == 0)
        def _():
            for r in (gwg_ref, dkm_ref, dvm_ref, gng_ref, gog_ref, gcb_ref, gbrg_ref, gbig_ref, glam_ref, gcw_ref,
                      gqn_ref, gkn_ref, gxqn_ref, gsink_ref, gcar_ref, dkcar_ref, dvcar_ref):
                r[...] = jnp.zeros_like(r)
            dxc_ext[tm:tm + 8, :] = jnp.zeros((8, LRU_W), F32)
            aext_ref[tm:tm + 8, :] = jnp.zeros((8, LRU_W), F32)

        xv = x_ref[...]
        dov = dout_ref[...]
        dz = _mm_nt(dov, wout_ref[...])
        ogv = og_ref[...]

        def group_bwd(y, gate, g, dzg):
            r = lax.rsqrt(_row_mean(y * y) + EPS)
            n = y * r
            sg = _sigmoid(gate)
            dgate = dzg * (n * g) * (sg * (1.0 + gate * (1.0 - sg)))
            dng = dzg * (gate * sg)
            dn = dng * g
            return r * (dn - n * _row_mean(dn * n)), dgate, _col_sum(dng * n)

        dya, dga, goa = group_bwd(ya_ref[...], proj_ref[:, C_LRUG:C_LRUG + LRU_W], ogv[:, :512], dz[:, :512])
        dyb, dgb, gob = group_bwd(yb_ref[...], proj_ref[:, C_SWAG:C_SWAG + SWA_W], ogv[:, 512:768], dz[:, 512:768])
        dyc, dgc, goc = group_bwd(yc_ref[...], proj_ref[:, C_XG:C_XG + XATT_W], ogv[:, 768:], dz[:, 768:])
        gog_ref[...] += jnp.concatenate([goa, gob, goc], axis=1)
        dproj_ref[:, C_LRUG:C_LRUG + LRU_W] = dga.astype(dproj_ref.dtype)
        dproj_ref[:, C_SWAG:C_SWAG + SWA_W] = dgb.astype(dproj_ref.dtype)
        dproj_ref[:, C_XG:C_XG + XATT_W] = dgc.astype(dproj_ref.dtype)

        gm256 = _group_matrix(XATT_W)
        xq = proj_ref[:, C_XQ:C_XQ + XATT_W]
        rq = lax.rsqrt(_seg_mean(xq * xq, gm256) + EPS)
        qn = xq * rq
        qx = qn * gn_ref[:, G_XQ:G_XK]
        qxb = qx.astype(_MXU)
        dycb = dyc.astype(_MXU)
        dp_all = _mm_nt(dycb, vm_ref[...])
        dsm = []
        for j in range(4):
            pj = pmem_ref[:, MEM_LEN * j:MEM_LEN * (j + 1)].astype(F32)
            dp = dp_all[:, MEM_LEN * j:MEM_LEN * (j + 1)]
            dsm.append((pj * (dp - jnp.sum(pj * dp, axis=-1, keepdims=True))).astype(_MXU))
        ds_all = jnp.concatenate(dsm, axis=1)
        dvm_ref[...] += _mm_tn(dycb, pmem_ref[...])
        dkm_ref[...] += _mm_tn(qxb, ds_all)
        dqx = _mm(ds_all, km_ref[...])
        gxqn_ref[...] += _col_sum(dqx * qn)
        dqn = dqx * gn_ref[:, G_XQ:G_XK]
        dproj_ref[:, C_XQ:C_XQ + XATT_W] = (rq * (dqn - qn * _seg_mean(dqn * qn, gm256))).astype(dproj_ref.dtype)

        gm128 = _group_matrix(KV_W)
        cv, s1v, s2v = c_ref[...], s1_ref[...], s2_ref[...]

        def head_norm(t):
            r = lax.rsqrt(_seg_mean(t * t, gm128) + EPS)
            return t * r, r

        qn_, qr_ = zip(head_norm(proj_ref[:, C_SQ:C_SQ + 128]), head_norm(proj_ref[:, C_SQ + 128:C_SQ + 256]))
        qrope = [_rope(qn_[h] * gn_ref[:, G_Q:G_K], cv, s1v, s2v).astype(_MXU) for h in range(2)]
        kn, krr = head_norm(proj_ref[:, C_SK:C_SK + KV_W])
        kr = _rope(kn * gn_ref[:, G_K:G_XQ], cv, s1v, s2v)
        khn, _ = head_norm(kvh_ref[:, 0:KV_W])
        khr = _rope(khn * gn_ref[:, G_K:G_XQ], ch_ref[...], s1h_ref[...], s2h_ref[...])
        ka = _place_kv(jnp.concatenate([khr, kr], axis=0), 0.125)
        va = _place_kv(jnp.concatenate([kvh_ref[:, KV_W:2 * KV_W], proj_ref[:, C_SV:C_SV + KV_W]], axis=0), 1.0)
        lane128 = lax.broadcasted_iota(jnp.int32, (1, 128), 1)
        gsink = jnp.zeros((1, 128), F32)
        dk_band, dv_band, dq_blk = [], [], []
        for b in range(nb):
            band = slice(BLOCK * b, BLOCK * b + 2 * BLOCK)
            blk = slice(BLOCK * b, BLOCK * (b + 1))
            dka, dva, dsb = [], [], []
            deltas = jnp.zeros((BLOCK, 128), F32)
            for j in range(4):
                qh = qrope[j // 2][blk]
                doh = dyb[blk, KV_W * (j // 2):KV_W * (j // 2 + 1)].astype(_MXU)
                pb = pswa_ref[blk, 2 * BLOCK * j:2 * BLOCK * (j + 1)]
                p = pb.astype(F32)
                dp = _mm_nt(doh, va[j][band])
                delta = jnp.sum(p * dp, axis=-1, keepdims=True)
                ds = (p * (dp - delta)).astype(_MXU)
                deltas = jnp.where(lane128 == j, delta, deltas)
                dva.append(_mm_tn(pb, doh))
                dka.append(_mm_tn(ds, qh))
                dsb.append(ds)
            gsink = gsink - _col_sum(psink_ref[blk, :] * deltas)
            dk_band.append(_unplace_kv(dka) * 0.125)
            dv_band.append(_unplace_kv(dva))
            dq_blk.append([_mm(jnp.concatenate(dsb[2 * h:2 * h + 2], axis=1),
                               jnp.concatenate([ka[2 * h][band], ka[2 * h + 1][band]], axis=0)) for h in range(2)])
        gsink_ref[...] += gsink
        dk_rows = [dk_band[b][BLOCK:] + (dk_band[b + 1][:BLOCK] if b + 1 < nb else dkcar_ref[...]) for b in range(nb)]
        dv_rows = [dv_band[b][BLOCK:] + (dv_band[b + 1][:BLOCK] if b + 1 < nb else dvcar_ref[...]) for b in range(nb)]
        dkcar_ref[...] = dk_band[0][:BLOCK]
        dvcar_ref[...] = dv_band[0][:BLOCK]
        dkg = _rope_bwd(jnp.concatenate(dk_rows, axis=0), cv, s1v, s2v)
        gkn = _col_sum(dkg * kn)
        dkn = dkg * gn_ref[:, G_K:G_XQ]
        dproj_ref[:, C_SK:C_SK + KV_W] = (krr * (dkn - kn * _seg_mean(dkn * kn, gm128))).astype(dproj_ref.dtype)
        dproj_ref[:, C_SV:C_SV + KV_W] = jnp.concatenate(dv_rows, axis=0).astype(dproj_ref.dtype)
        gqn = jnp.zeros((1, 128), F32)
        for h in range(2):
            dqg = _rope_bwd(jnp.concatenate([dq_blk[b][h] for b in range(nb)], axis=0), cv, s1v, s2v)
            gqn = gqn + _col_sum(dqg * qn_[h])
            dqn_ = dqg * gn_ref[:, G_Q:G_K]
            dproj_ref[:, C_SQ + 128 * h:C_SQ + 128 * (h + 1)] = (
                qr_[h] * (dqn_ - qn_[h] * _seg_mean(dqn_ * qn_[h], gm128))).astype(dproj_ref.dtype)
        gqn_ref[...] += gqn
        gkn_ref[...] += gkn

        u = proj_ref[:, C_LRUX:C_LRUX + LRU_W]
        xc, rg, ig, sq = (gates_ref[:, LRU_W * k:LRU_W * (k + 1)].astype(F32) for k in range(4))
        a = a_ref[...]
        sp = _softplus(-lam_ref[...])
        hext_ref[0:8, :] = jnp.where(first_tile, 0.0, yah_ref[...])
        hext_ref[8:8 + tm, :] = ya_ref[...]
        hprev = hext_ref[pl.ds(7, tm), :]
        aext_ref[0:tm, :] = a
        an_scr[...] = aext_ref[pl.ds(1, tm), :]
        dh_scr[...] = dya
        dh_scr[tm - 1:tm, :] = dh_scr[tm - 1:tm, :] + gcar_ref[0:1, :]
        row8 = lax.broadcasted_iota(jnp.int32, (8, LRU_W), 0)

        def scan_step(gi, carry):
            r0 = pl.multiple_of((tm // 8 - 1 - gi) * 8, 8)
            av = an_scr[pl.ds(r0, 8), :]
            bv = dh_scr[pl.ds(r0, 8), :]
            for d in (1, 2, 4):
                a_sh = jnp.where(row8 < 8 - d, pltpu.roll(av, 8 - d, 0), 1.0)
                b_sh = jnp.where(row8 < 8 - d, pltpu.roll(bv, 8 - d, 0), 0.0)
                bv = bv + av * b_sh
                av = av * a_sh
            gv = bv + av * carry
            g_scr[pl.ds(r0, 8), :] = gv
            return gv[0:1, :]

        g0 = lax.fori_loop(0, tm // 8, scan_step, jnp.zeros((1, LRU_W), F32), unroll=True)
        gcar_ref[0:1, :] = a[0:1, :] * g0
        gv = g_scr[...]
        da = gv * hprev
        dig = gv * sq * xc
        dxc = gv * sq * ig
        dla = da * a - gv * (ig * xc) * ((a * a) / sq)
        drg = dla * ((-LRU_C) * sp)
        glam_ref[...] += _col_sum(dla * rg)
        dpr = drg * rg * (1.0 - rg)
        dpi = dig * ig * (1.0 - ig)
        gbrg_ref[...] += _col_sum(dpr)
        gbig_ref[...] += _col_sum(dpi)
        dpre0 = jnp.concatenate([dpr[:, :256], dpi[:, :256]], axis=1).astype(_MXU)
        dpre1 = jnp.concatenate([dpr[:, 256:], dpi[:, 256:]], axis=1).astype(_MXU)
        gwg_ref[0] += _mm_tn(xc[:, :256], dpre0)
        gwg_ref[1] += _mm_tn(xc[:, 256:], dpre1)
        dxc = dxc + jnp.concatenate([_mm_nt(dpre0, wg_ref[0]), _mm_nt(dpre1, wg_ref[1])], axis=1)
        gcb_ref[...] += _col_sum(dxc)
        dxc_ext[0:tm, :] = dxc
        du = jnp.zeros((tm, LRU_W), F32)
        for k in range(CONV_K):
            later = dxc_ext[pl.ds(3 - k, tm), :]
            gcw_ref[k:k + 1, :] += _col_sum(later * u)
            du = du + cw_ref[k:k + 1, :] * later
        dxc_ext[tm:tm + 8, :] = dxc[0:8, :]
        dproj_ref[:, C_LRUX:C_LRUX + LRU_W] = du.astype(dproj_ref.dtype)

        dxn = _mm(dproj_ref[...], win_ref[...])
        rx = lax.rsqrt(_row_mean(xv * xv) + EPS)
        xh = xv * rx
        gng_ref[...] += _col_sum(dxn * xh)
        dxh = dxn * ng_ref[...]
        gx_ref[...] = dov.astype(F32) + rx * (dxh - xh * _row_mean(dxh * xh))

        @pl.when(i == nt - 1)
        def _():
            glam_ref[...] = glam_ref[...] * (LRU_C * _sigmoid(-lam_ref[...]))
            for r in (gqn_ref, gkn_ref, gxqn_ref):
                r[...] = _fold_heads(r[...])

    def rows(ncol, arr_cols_block=0):
        return pl.BlockSpec((tm, ncol), lambda i: (nt - 1 - i, arr_cols_block))

    def halo(nrow, ncol, colblk=0):
        per = tm // nrow
        return pl.BlockSpec((nrow, ncol), lambda i: (jnp.maximum((nt - 1 - i) * per - 1, 0), colblk))

    in_specs = [rows(D_MODEL), rows(D_MODEL), rows(D_IN), rows(LRU_W), rows(SWA_W), rows(XATT_W),
                rows(4 * 2 * BLOCK), rows(4 * MEM_LEN), rows(128), rows(4 * LRU_W), rows(LRU_W),
                rows(128), rows(128), rows(128),
                halo(8, LRU_W), halo(BLOCK, 2 * KV_W, C_SK // (2 * KV_W)),
                halo(BLOCK, 128), halo(BLOCK, 128), halo(BLOCK, 128),
                _const_spec((1, D_MODEL)), _const_spec((D_IN, D_MODEL), True), _const_spec((CONV_K, LRU_W)),
                _const_spec((2, 256, 512), True), _const_spec((1, LRU_W)), _const_spec((1, GAINS_W)),
                _const_spec((4 * MEM_LEN, XATT_W), True), _const_spec((4 * MEM_LEN, XATT_W), True),
                _const_spec((1, D_MODEL)), _const_spec((D_MODEL, D_MODEL), True)]
    small = [(2, 256, 512), (XATT_W, 4 * MEM_LEN), (XATT_W, 4 * MEM_LEN), (1, D_MODEL), (1, D_MODEL), (1, LRU_W), (1, LRU_W),
             (1, LRU_W), (1, LRU_W), (CONV_K, LRU_W), (1, 128), (1, 128), (1, XATT_W), (1, 128)]
    out_shape = (jax.ShapeDtypeStruct((seq, D_MODEL), F32), jax.ShapeDtypeStruct((seq, D_IN), _MXU)) + tuple(
        jax.ShapeDtypeStruct(s, F32) for s in small)
    out_specs = (rows(D_MODEL), rows(D_IN)) + tuple(_const_spec(s) for s in small)
    scratch = [pltpu.VMEM((tm + 8, LRU_W), F32), pltpu.VMEM((tm + 8, LRU_W), F32),
               pltpu.VMEM((tm, LRU_W), F32), pltpu.VMEM((tm, LRU_W), F32), pltpu.VMEM((tm, LRU_W), F32),
               pltpu.VMEM((tm + 8, LRU_W), F32),
               pltpu.VMEM((8, LRU_W), F32), pltpu.VMEM((BLOCK, KV_W), F32), pltpu.VMEM((BLOCK, KV_W), F32)]
    return pl.pallas_call(
        body, name="layer_bwd", grid=(nt,), out_shape=out_shape, in_specs=in_specs, out_specs=out_specs,
        scratch_shapes=scratch,
        compiler_params=pltpu.CompilerParams(dimension_semantics=("arbitrary",), vmem_limit_bytes=VMEM_LIMIT),
    )(x, dout, proj, ya, yb, yc, pswa, pmem, psink, gates, a_all, rc, rs1, rs2, ya, proj, rc, rs1, rs2,
      ng, win_t, cw, wg, lam, gains, km, vm, og, wout)


def _reduce_protocol(big, sm, outs, osm, r1, r1s, wire, r2, r2s, wire2, ps, own, send, recv, lsem):
    nbig = len(big)
    x, y, c = lax.axis_index("x"), lax.axis_index("y"), lax.axis_index("c")
    sibling = (x, y, 1 - c)
    near, far, diag = _partners(x, y, c)
    me, near_id, far_id, diag_id = _chip_of(x, y), _chip_of(*near), _chip_of(*far), _chip_of(*diag)

    def copy(k, src, dst, to):
        return pltpu.make_async_remote_copy(src_ref=src, dst_ref=dst, send_sem=send.at[k], recv_sem=recv.at[k],
                                            device_id=to, device_id_type=MESH)

    def sent(stage, a):
        if a == nbig:
            src, dst, to = ((sm.at[1 - c], r1s, sibling), (r1s, r2s.at[0], (*near, c)), (ps, r2s.at[1], (*far, c)),
                            (osm.at[c], osm.at[c], sibling))[stage]
            return [copy(5 * nbig + stage, src, dst, to)]
        if stage == 0:
            return [copy(5 * a, big[a].at[:, 1 - c], r1[a], sibling)]
        if stage == 1:
            return [copy(5 * a + 1, wire[a].at[near_id], r2[a].at[0], (*near, c)),
                    copy(5 * a + 2, wire[a].at[diag_id], r2[a].at[1], (*near, c))]
        if stage == 2:
            return [copy(5 * a + 3, wire2[a], r2[a].at[2], (*far, c))]
        return [copy(5 * a + 4, outs[a].at[c], outs[a].at[c], sibling)]

    arrays = range(nbig + (sm is not None))

    def start(stage, a):
        for cp in sent(stage, a):
            cp.start()

    def arrived(k, ref):
        copy(k, ref, ref, sibling).wait_recv()

    def loads():
        return [pltpu.make_async_copy(big[a].at[:, c], own[a], lsem.at[a]) for a in range(nbig)]

    def stage0():
        for a in arrays:
            start(0, a)
        for cp in loads():
            cp.start()

    def stage1():
        for a in range(nbig):
            loads()[a].wait()
            arrived(5 * a, r1[a])
            for k in range(N_CHIPS):
                r1[a][k] = own[a][k] + r1[a][k]
                wire[a][k] = r1[a][k].astype(wire[a].dtype)
            start(1, a)
        if sm is not None:
            arrived(5 * nbig, r1s)
            r1s[...] = sm[c] + r1s[...]
            start(1, nbig)

    def stage2():
        for a in range(nbig):
            arrived(5 * a + 1, r2[a].at[0])
            arrived(5 * a + 2, r2[a].at[1])
            r1[a][me] = r1[a][me] + r2[a][0].astype(F32)
            wire2[a][...] = (r1[a][far_id] + r2[a][1].astype(F32)).astype(wire2[a].dtype)
            start(2, a)
        if sm is not None:
            arrived(5 * nbig + 1, r2s.at[0])
            ps[...] = r1s[...] + r2s[0]
            start(2, nbig)

    def stage3():
        for a in range(nbig):
            arrived(5 * a + 3, r2[a].at[2])
            outs[a][c] = r1[a][me] + r2[a][2].astype(F32)
            start(3, a)
        if sm is not None:
            arrived(5 * nbig + 2, r2s.at[1])
            osm[c] = ps[...] + r2s[1]
            start(3, nbig)

    def stage4():
        for a in range(nbig):
            arrived(5 * a + 4, outs[a].at[1 - c])
        if sm is not None:
            arrived(5 * nbig + 3, osm.at[1 - c])
        for stage in range(4):
            for a in arrays:
                for cp in sent(stage, a):
                    cp.wait_send()

    return [stage0, stage1, stage2, stage3, stage4]


def _reduce_buffers(bigs, g_small):
    half = [b.shape[2:] for b in bigs]
    sm_half = None if g_small is None else g_small.shape[1:]
    out_shape = [jax.ShapeDtypeStruct((2,) + h, F32) for h in half]
    small = lambda lead: [] if g_small is None else [pltpu.VMEM(lead + sm_half, F32)]
    if g_small is not None:
        out_shape.append(jax.ShapeDtypeStruct(g_small.shape, F32))
    n_sem = 5 * len(bigs) + 4
    scratch = ([pltpu.VMEM((N_CHIPS,) + h, F32) for h in half] + small(())
               + [pltpu.VMEM((N_CHIPS,) + h, _WIRE) for h in half]
               + [pltpu.VMEM((3,) + h, _WIRE) for h in half] + small((2,))
               + [pltpu.VMEM(h, _WIRE) for h in half] + small(())
               + [pltpu.VMEM((N_CHIPS,) + h, F32) for h in half]
               + [pltpu.SemaphoreType.DMA((n_sem,)), pltpu.SemaphoreType.DMA((n_sem,)),
                  pltpu.SemaphoreType.DMA((len(bigs),))])
    return out_shape, scratch


def _split_reduce_refs(refs, nbig, has_small):
    it = iter(refs)
    take = lambda n: [next(it) for _ in range(n)]
    one = lambda: next(it) if has_small else None
    big, sm = take(nbig), one()
    outs, osm = take(nbig), one()
    r1, r1s, wire, r2, r2s, wire2, ps, own = take(nbig), one(), take(nbig), take(nbig), one(), take(nbig), one(), take(nbig)
    send, recv, lsem = take(3)
    return big, sm, outs, osm, r1, r1s, wire, r2, r2s, wire2, ps, own, send, recv, lsem


def _hosted_reduce_shapes(bigs, g_small):
    red_shape, scratch = _reduce_buffers(bigs, g_small)
    nres = len(red_shape)
    return red_shape, [pltpu.VMEM(r.shape, r.dtype) for r in red_shape] + scratch + [pltpu.SemaphoreType.DMA((nres,))]


def _hosted_reduce(step, n_steps, closing, stage_at, operands, results, scratch, has_small):
    nres = len(results)
    sums, rest, fsem = scratch[:nres], scratch[nres:-1], scratch[-1]
    refs = tuple(operands) + tuple(sums) + tuple(rest)

    def to_results():
        out = [pltpu.make_async_copy(sums[k], results[k], fsem.at[k]) for k in range(nres)]
        for cp in out:
            cp.start()
        for cp in out:
            cp.wait()

    stages = _reduce_protocol(*_split_reduce_refs(refs, nres - has_small, has_small))

    def last_stage():
        stages[-1]()
        to_results()

    for at, stage in zip(stage_at, stages[:-1] + [last_stage]):
        if closing == (at == n_steps):
            pl.when(step == min(at, n_steps - 1))(stage)


def reduce_grads(big, name, parts):
    chips, halves, rows_, cols = big.shape
    sub = jax.ShapeDtypeStruct((chips, halves, rows_ // parts, cols), big.dtype)

    def body(b_ref, o_ref, *scratch):
        refs = [b_ref.at[:, :, s] for s in range(parts)] + [o_ref.at[:, s] for s in range(parts)] + list(scratch)
        for stage in _reduce_protocol(*_split_reduce_refs(refs, parts, False)):
            stage()

    _, scratch = _reduce_buffers([sub] * parts, None)
    return pl.pallas_call(
        body, name=name, out_shape=jax.ShapeDtypeStruct((halves, parts, rows_ // parts, cols), F32),
        in_specs=[pl.BlockSpec(memory_space=pl.ANY)], out_specs=pl.BlockSpec(memory_space=pltpu.VMEM),
        scratch_shapes=scratch, compiler_params=pltpu.CompilerParams(vmem_limit_bytes=VMEM_LIMIT),
    )(big.reshape(chips, halves, parts, rows_ // parts, cols))


def adamw(items, g_pack, ws, ms, vs):
    plan, total = [], 0
    for w, _, _, _ in items:
        rows_, cols = w.shape
        tr = max(t for t in range(8, rows_ + 1, 8) if rows_ % t == 0 and t * cols * 4 <= ADAM_BLOCK_BYTES)
        plan.append((total, rows_ // tr, tr, cols))
        total += rows_ // tr
    nin, n = 4 * len(items), len(ws)

    def body(*refs):
        i = pl.program_id(0)
        small_in = refs[nin:nin + 1 + 3 * n]
        outs = refs[nin + 1 + 3 * n:]

        @pl.when(i == 0)
        def _():
            _adamw_small(small_in[0], *(small_in[1 + k * n:1 + (k + 1) * n] for k in range(3)),
                         *(outs[nin + k * n:nin + (k + 1) * n] for k in range(4)), outs[-1])

        for k, (first, steps, _, _) in enumerate(plan):
            w_ref, g_ref, m_ref, v_ref = refs[4 * k:4 * k + 4]
            go_ref, d_ref, nm_ref, nv_ref = outs[4 * k:4 * k + 4]

            @pl.when((i >= first) & (i < first + steps))
            def _():
                gv = g_ref[...]
                go_ref[...] = gv
                d_ref[...], nm_ref[...], nv_ref[...] = _adam_update(w_ref[...], gv, m_ref[...], v_ref[...])

    specs, shapes = [], []
    for (first, steps, tr, cols), (w, _, _, _) in zip(plan, items):
        spec = pl.BlockSpec((tr, cols), lambda i, first=first, steps=steps: (jnp.clip(i - first, 0, steps - 1), 0))
        specs += [spec] * 4
        shapes += [jax.ShapeDtypeStruct(w.shape, F32)] * 4
    vm = pl.BlockSpec(memory_space=pltpu.VMEM)
    like = [jax.ShapeDtypeStruct(w.shape, F32) for w in ws]
    res = pl.pallas_call(
        body, name="adamw", grid=(total,), out_shape=(*shapes, *like * 4, jax.ShapeDtypeStruct((1, 1), F32)),
        in_specs=specs + [vm] * (1 + 3 * n), out_specs=(*specs, *[vm] * (4 * n + 1)),
        compiler_params=pltpu.CompilerParams(dimension_semantics=("arbitrary",)),
    )(*[a for item in items for a in item], g_pack, *ws, *ms, *vs)
    return [res[4 * k:4 * k + 4] for k in range(len(items))], res[nin:]


def _adam_update(w, g, m, v):
    nm = ADAM_B1 * m + (1.0 - ADAM_B1) * g
    nv = ADAM_B2 * v + (1.0 - ADAM_B2) * (g * g)
    m_hat = nm / (1.0 - ADAM_B1 ** ADAM_STEP)
    v_hat = nv / (1.0 - ADAM_B2 ** ADAM_STEP)
    return (-ADAM_LR) * (m_hat / (jnp.sqrt(v_hat) + ADAM_EPS) + ADAM_WD * w), nm, nv


def _adamw_small(pk, w_refs, m_refs, v_refs, g_out, d_out, nm_out, nv_out, loss_ref):
    nvec = len(SMALL_VECTORS)
    loss_ref[...] = pk[LOSS_ROW:LOSS_ROW + 1, 0:1]
    chip = 2 * lax.axis_index("x") + lax.axis_index("y")
    for k, (name, row, width) in enumerate(SMALL_VECTORS):
        if name == "conv_w":
            g = jnp.concatenate([pk[pl.ds(row + 4 * t + chip, 1), :] for t in range(CONV_K)], axis=0)[None]
        elif width >= 128:
            g = jnp.concatenate([pk[row + r:row + r + 1, :] for r in range(width // 128)], axis=1)
        else:
            g = pk[row:row + 1, 0:width]
        g_out[k][...] = g
        d_out[k][...], nm_out[k][...], nv_out[k][...] = _adam_update(w_refs[k][...], g, m_refs[k][...], v_refs[k][...])
    for k in range(nvec, nvec + len(SMALL_MATRICES)):
        for b in range(LRU_BLOCKS):
            rows_ = pk[GATES_ROW + HEAD * b:GATES_ROW + HEAD * (b + 1), :]
            g = (pltpu.roll(rows_, HEAD, axis=1) if k > nvec else rows_)[:, 0:HEAD]
            g_out[k][0, b] = g
            d_out[k][0, b], nm_out[k][0, b], nv_out[k][0, b] = _adam_update(
                w_refs[k][0, b], g, m_refs[k][0, b], v_refs[k][0, b])


SMALL_VECTORS = (("norm_g", 512, 1024), ("mem_norm_g", 520, 1024), ("conv_w", 528, 512), ("conv_b", 544, 512),
                 ("b_rg", 548, 512), ("b_ig", 552, 512), ("lru_lambda", 556, 512), ("q_norm_g", 560, 64),
                 ("k_norm_g", 561, 64), ("sinks", 562, 4), ("xq_norm_g", 563, 64), ("xk_norm_g", 564, 64),
                 ("out_norm_g", 565, 1024))
LOSS_ROW = 573
SMALL_MATRICES = ("w_rg", "w_ig")
GATES_ROW = 0
SMALL_ROWS = 640


def _rope_tables(seq):
    pos = np.arange(seq, dtype=np.float32)
    inv_freq = (np.float32(ROPE_THETA) ** (-(np.arange(0, ROPE_DIM, 2, dtype=np.float32) / np.float32(ROPE_DIM)))
                ).astype(np.float32)
    ang = (pos[:, None] * inv_freq[None, :]).astype(np.float32)
    cos, sin = np.cos(ang).astype(np.float32), np.sin(ang).astype(np.float32)
    z = lambda n: np.zeros((seq, n), np.float32)
    c64 = np.concatenate([cos, cos, np.ones((seq, HEAD - ROPE_DIM), np.float32)], axis=1)
    s1_64 = np.concatenate([-sin, z(HEAD - 8)], axis=1)
    s2_64 = np.concatenate([z(8), sin, z(HEAD - ROPE_DIM)], axis=1)
    return tuple(jnp.asarray(np.concatenate([t, t], axis=1)) for t in (c64, s1_64, s2_64))


def kernel(x, mem, norm_g, mem_norm_g, w_in, conv_w, conv_b, w_rg, b_rg, w_ig, b_ig, lru_lambda, q_norm_g, k_norm_g, sinks, w_mem_kv, xq_norm_g, xk_norm_g, out_norm_g, w_out, loss_target, m_norm_g, m_mem_norm_g, m_w_in, m_conv_w, m_conv_b, m_w_rg, m_b_rg, m_w_ig, m_b_ig, m_lru_lambda, m_q_norm_g, m_k_norm_g, m_sinks, m_w_mem_kv, m_xq_norm_g, m_xk_norm_g, m_out_norm_g, m_w_out, v_norm_g, v_mem_norm_g, v_w_in, v_conv_w, v_conv_b, v_w_rg, v_b_rg, v_w_ig, v_b_ig, v_lru_lambda, v_q_norm_g, v_k_norm_g, v_sinks, v_w_mem_kv, v_xq_norm_g, v_xk_norm_g, v_out_norm_g, v_w_out):
    seq = x.shape[1]
    xs, tgt, mems = x[0], loss_target[0], mem[0]

    win_t, wout, wkv, cw, wg, gains, km, vm = gather_weights(
        w_in[0].T, w_out[0], w_mem_kv[0], conv_w, w_rg, w_ig, (q_norm_g, k_norm_g, xq_norm_g, xk_norm_g), mems,
        mem_norm_g)
    rc, rs1, rs2 = _rope_tables(seq)
    proj, ya, yb, yc, ycat, xn, dout, pswa, pmem, psink, gates, a_all, loss8 = layer_fwd(
        xs, tgt, rc, rs1, rs2, norm_g, win_t, cw, conv_b, wg, b_rg, b_ig, lru_lambda, gains, sinks, km, vm,
        out_norm_g, wout)
    (gx, dproj, g_wg, dkm, dvm, g_ng, g_og, g_cb, g_brg, g_big, g_lam, g_cw, g_qn, g_kn, g_xqn, g_sink) = layer_bwd(
        xs, dout, proj, ya, yb, yc, pswa, pmem, psink, gates, a_all, rc, rs1, rs2, norm_g, win_t, cw, wg, lru_lambda,
        gains, km, vm, out_norm_g, wout)
    g_win_t, r_out, r_kv, r_small = weight_grads(
        ycat, dout, dproj, xn, (mems, mem_norm_g, wkv, gains, dkm, dvm, g_wg, loss8), dict(
            norm_g=g_ng, conv_w=g_cw, conv_b=g_cb, b_rg=g_brg, b_ig=g_big, lru_lambda=g_lam, q_norm_g=g_qn,
            k_norm_g=g_kn, sinks=g_sink, xq_norm_g=g_xqn, out_norm_g=g_og), (0, 1, 4, 7, 8), (4, 5, 9, 11, 13))
    r_in = reduce_grads(g_win_t.reshape(N_CHIPS, 2, D_IN // 8, D_MODEL), "reduce_w_in", 9)

    r_small = r_small.reshape(SMALL_ROWS, 128)
    grads = {}
    weights = dict(norm_g=norm_g, mem_norm_g=mem_norm_g, w_in=w_in, conv_w=conv_w, conv_b=conv_b, w_rg=w_rg, b_rg=b_rg,
                   w_ig=w_ig, b_ig=b_ig, lru_lambda=lru_lambda, q_norm_g=q_norm_g, k_norm_g=k_norm_g, sinks=sinks,
                   w_mem_kv=w_mem_kv, xq_norm_g=xq_norm_g, xk_norm_g=xk_norm_g, out_norm_g=out_norm_g, w_out=w_out)
    ms = dict(norm_g=m_norm_g, mem_norm_g=m_mem_norm_g, w_in=m_w_in, conv_w=m_conv_w, conv_b=m_conv_b, w_rg=m_w_rg,
              b_rg=m_b_rg, w_ig=m_w_ig, b_ig=m_b_ig, lru_lambda=m_lru_lambda, q_norm_g=m_q_norm_g, k_norm_g=m_k_norm_g,
              sinks=m_sinks, w_mem_kv=m_w_mem_kv, xq_norm_g=m_xq_norm_g, xk_norm_g=m_xk_norm_g,
              out_norm_g=m_out_norm_g, w_out=m_w_out)
    vs = dict(norm_g=v_norm_g, mem_norm_g=v_mem_norm_g, w_in=v_w_in, conv_w=v_conv_w, conv_b=v_conv_b, w_rg=v_w_rg,
              b_rg=v_b_rg, w_ig=v_w_ig, b_ig=v_b_ig, lru_lambda=v_lru_lambda, q_norm_g=v_q_norm_g, k_norm_g=v_k_norm_g,
              sinks=v_sinks, w_mem_kv=v_w_mem_kv, xq_norm_g=v_xq_norm_g, xk_norm_g=v_xk_norm_g,
              out_norm_g=v_out_norm_g, w_out=v_w_out)

    delta, new_m, new_v = {}, {}, {}
    small_names = [n for n, _, _ in SMALL_VECTORS] + list(SMALL_MATRICES)
    (res_in, res_out, res_kv), res = adamw(
        [(w_in[0].T, r_in.reshape(D_IN // 4, D_MODEL), m_w_in[0].T, v_w_in[0].T),
         (w_out[0], r_out.reshape(D_MODEL // 4, D_MODEL), m_w_out[0], v_w_out[0]),
         (w_mem_kv[0], r_kv.reshape(D_MODEL // 4, 2 * XATT_W), m_w_mem_kv[0], v_w_mem_kv[0])],
        r_small, [weights[n] for n in small_names], [ms[n] for n in small_names], [vs[n] for n in small_names])
    grads["w_in"], delta["w_in"], new_m["w_in"], new_v["w_in"] = (r.T[None] for r in res_in)
    grads["w_out"], delta["w_out"], new_m["w_out"], new_v["w_out"] = (r[None] for r in res_out)
    grads["w_mem_kv"], delta["w_mem_kv"], new_m["w_mem_kv"], new_v["w_mem_kv"] = (r[None] for r in res_kv)
    nall = len(small_names)
    for k, into in enumerate((grads, delta, new_m, new_v)):
        into.update(zip(small_names, res[k * nall:(k + 1) * nall]))
    loss = res[-1].reshape(())

    order = ("norm_g", "mem_norm_g", "w_in", "conv_w", "conv_b", "w_rg", "b_rg", "w_ig", "b_ig", "lru_lambda",
             "q_norm_g", "k_norm_g", "sinks", "w_mem_kv", "xq_norm_g", "xk_norm_g", "out_norm_g", "w_out")
    return (loss, gx[None], *[grads[n] for n in order], *[delta[n] for n in order], *[new_m[n] for n in order],
            *[new_v[n] for n in order])
```

```python
import jax
import jax.numpy as jnp
import numpy as np
from jax import lax
from jax.experimental import pallas as pl
from jax.experimental.pallas import tpu as pltpu

F32 = jnp.float32
_MXU = jnp.bfloat16
_WIRE = jnp.bfloat16

D_MODEL = 1024
MEM_LEN = 256
HEAD = 64
LRU_W = 512
LRU_BLOCKS = 8
CONV_K = 4
LRU_C = 8.0
SWA_W = 256
KV_W = 128
XATT_W = 256
BLOCK = 128
D_IN = 2304
ROPE_THETA = 500000.0
ROPE_DIM = 16
EPS = 1e-6
NEG_INF = -1e30
C_LRUX, C_LRUG, C_SQ, C_SK, C_SV, C_SWAG, C_XQ, C_XG = 0, 512, 1024, 1280, 1408, 1536, 1792, 2048
G_Q, G_K, G_XQ, G_XK, GAINS_W = 0, 128, 256, 512, 768

ADAM_LR, ADAM_B1, ADAM_B2, ADAM_EPS, ADAM_WD, ADAM_STEP = 0.001, 0.9, 0.999, 1e-08, 0.01, 10

N_CHIPS = 4
ROW_TILE = 256
VMEM_LIMIT = 56 * 1024 * 1024
ADAM_BLOCK_BYTES = 640 * 1024
MESH = pl.DeviceIdType.MESH


def _mm(a, b):
    return jnp.dot(a.astype(_MXU), b.astype(_MXU), preferred_element_type=F32)


def _mm_nt(a, b):
    return lax.dot_general(a.astype(_MXU), b.astype(_MXU), (((1,), (1,)), ((), ())), preferred_element_type=F32)


def _mm_tn(a, b):
    return lax.dot_general(a.astype(_MXU), b.astype(_MXU), (((0,), (0,)), ((), ())), preferred_element_type=F32)


def _group_matrix(width):
    r = lax.shift_right_logical(lax.broadcasted_iota(jnp.int32, (width, width), 0), 6)
    c = lax.shift_right_logical(lax.broadcasted_iota(jnp.int32, (width, width), 1), 6)
    return (r == c).astype(_MXU)


def _seg_mean(x, gm):
    return jnp.dot(x.astype(_MXU), gm, preferred_element_type=F32) * (1.0 / HEAD)


def _row_mean(x):
    return jnp.mean(x, axis=-1, keepdims=True)


def _col_sum(x):
    return jnp.sum(x, axis=0, keepdims=True)


def _sigmoid(x):
    return jax.nn.sigmoid(x)


def _softplus(z):
    e = jnp.exp(-jnp.abs(z))
    u = 1.0 + e
    log1p_e = jnp.where(u == 1.0, e, jnp.log(u) * (e / (u - 1.0)))
    return jnp.maximum(z, 0.0) + log1p_e


def _rope(t, c, s1, s2):
    return t * c + pltpu.roll(t, 120, 1) * s1 + pltpu.roll(t, 8, 1) * s2


def _rope_bwd(d, c, s1, s2):
    return d * c + pltpu.roll(d * s1, 8, 1) + pltpu.roll(d * s2, 120, 1)


def _fold_heads(v):
    out = v
    for k in range(1, v.shape[1] // HEAD):
        out = out + pltpu.roll(v, HEAD * k, 1)
    return out


def _lane_mask(width, lo, hi):
    lane = lax.broadcasted_iota(jnp.int32, (1, width), 1)
    return ((lane >= lo) & (lane < hi)).astype(F32)


def _swa_mask(first_block):
    qi = lax.broadcasted_iota(jnp.int32, (BLOCK, 2 * BLOCK), 0)
    kj = lax.broadcasted_iota(jnp.int32, (BLOCK, 2 * BLOCK), 1)
    rel = qi + BLOCK - kj
    ok = (rel >= 0) & (rel < BLOCK)
    return ok & (jnp.logical_not(first_block) | (kj >= BLOCK))


def _place_kv(t, scale):
    lo = t * (_lane_mask(KV_W, 0, HEAD) * scale)
    hi = t * (_lane_mask(KV_W, HEAD, KV_W) * scale)
    return [a.astype(_MXU) for a in (lo, pltpu.roll(lo, HEAD, 1), pltpu.roll(hi, HEAD, 1), hi)]


def _unplace_kv(d):
    return (_lane_mask(KV_W, 0, HEAD) * (d[0] + pltpu.roll(d[1], HEAD, 1))
            + _lane_mask(KV_W, HEAD, KV_W) * (d[3] + pltpu.roll(d[2], HEAD, 1)))


def _swa_probs(qh, ka, mask, sink):
    s = _mm_nt(qh, ka)
    s = jnp.where(mask, s, NEG_INF)
    m = jnp.maximum(jnp.max(s, axis=-1, keepdims=True), sink)
    p = jnp.exp(s - m)
    esink = jnp.exp(sink - m)
    inv = 1.0 / (jnp.sum(p, axis=-1, keepdims=True) + esink)
    return p * inv, esink * inv


def _mem_probs(s_all):
    out = []
    for j in range(4):
        s = s_all[:, MEM_LEN * j:MEM_LEN * (j + 1)]
        p = jnp.exp(s - jnp.max(s, axis=-1, keepdims=True))
        out.append(p * (1.0 / jnp.sum(p, axis=-1, keepdims=True)))
    return out


def _head_rows(t, scale):
    return jnp.concatenate([t * (_lane_mask(XATT_W, HEAD * j, HEAD * (j + 1)) * scale) for j in range(4)], axis=0)


def _lru_gates(xc, wg_ref, brg, big, lam):
    p0 = _mm(xc[:, :256], wg_ref[0])
    p1 = _mm(xc[:, 256:], wg_ref[1])
    rg = _sigmoid(jnp.concatenate([p0[:, :256], p1[:, :256]], axis=1) + brg)
    ig = _sigmoid(jnp.concatenate([p0[:, 256:], p1[:, 256:]], axis=1) + big)
    sp = _softplus(-lam)
    la = (-LRU_C) * rg * sp
    a = jnp.exp(la)
    th = jnp.tanh(la)
    one_minus_a2 = (-2.0 * th) / (1.0 - th)
    return rg, ig, sp, a, jnp.sqrt(one_minus_a2)


def _const_spec(shape, single=False):
    zeros = (0,) * len(shape)
    if single:
        return pl.BlockSpec(shape, lambda i: zeros, pipeline_mode=pl.Buffered(1))
    return pl.BlockSpec(shape, lambda i: zeros)


def _chip_of(x, y):
    return 2 * x + y


def _partners(x, y, c):
    north = c == 1
    near = (jnp.where(north, 1 - x, x), jnp.where(north, y, 1 - y))
    far = (jnp.where(north, x, 1 - x), jnp.where(north, 1 - y, y))
    return near, far, (1 - x, 1 - y)


def gather_weights(win_t, wout, wkv, conv_w, w_rg, w_ig, head_gains, mem, mem_g):
    arrs = (win_t, wout, wkv)
    n = len(arrs)
    pieces = [(a, 0, arr.shape[0] // 2) for a, arr in enumerate(arrs)]
    npc = len(pieces)

    def body(a0, a1, a2, cw_in, wrg_ref, wig_ref, q_ref, k_ref, xq_ref, xk_ref, mem_ref, mg_ref,
             o0, o1, o2, cw_out, wg_ref, gn_ref, km_ref, vm_ref, s0, s1, s2, cw, ocw, send, recv, lsem):
        ins, outs = (s0, s1, s2), (o0, o1, o2)
        for src, dst in zip((a0, a1, a2), ins):
            dst[...] = src[...].astype(dst.dtype)
        cw[...] = jnp.zeros(cw.shape, F32)
        cw[0:CONV_K, :] = cw_in[0]
        x, y, c = lax.axis_index("x"), lax.axis_index("y"), lax.axis_index("c")
        sibling = (x, y, 1 - c)
        near, far, diag = _partners(x, y, c)
        chips = [near, far, diag]
        me = _chip_of(x, y)

        def landed(p, chip, half):
            a, off, rows_ = pieces[p]
            r = ins[a].shape[0]
            return outs[a].at[pl.ds(pl.multiple_of(chip * r + half * (r // 2) + off, 16), rows_)]

        def mine(p):
            a, off, rows_ = pieces[p]
            return ins[a].at[pl.ds(pl.multiple_of(c * (ins[a].shape[0] // 2) + off, 16), rows_)]

        def copy(k, src, dst, to):
            return pltpu.make_async_remote_copy(src_ref=src, dst_ref=dst, send_sem=send.at[k], recv_sem=recv.at[k],
                                                device_id=to, device_id_type=MESH)

        def cw_rows(chip):
            return ocw.at[pl.ds(pl.multiple_of(chip * 8, 8), 8)]

        locals_ = []
        for a in range(n):
            r = ins[a].shape[0]
            locals_.append(pltpu.make_async_copy(ins[a], outs[a].at[pl.ds(pl.multiple_of(me * r, 16), r)], lsem.at[a]))
        locals_.append(pltpu.make_async_copy(cw, cw_rows(me), lsem.at[n]))
        for cp in locals_:
            cp.start()

        sent = []
        for p in range(npc):
            for j in range(2):
                sent.append(copy(p * 6 + j, mine(p), landed(p, me, c), (*chips[j], c)))
        for j, chip in enumerate(chips):
            sent.append(copy(npc * 6 + j, cw, cw_rows(me), (*chip, c)))
        for cp in sent:
            cp.start()

        gn_ref[...] = jnp.concatenate([q_ref[...]] * 2 + [k_ref[...]] * 2 + [xq_ref[...]] * 4 + [xk_ref[...]] * 4,
                                      axis=1)
        zeros = lambda lanes: [jnp.zeros((HEAD, lanes), F32)] if lanes else []
        for h in range(2):
            for b in range(4):
                row = []
                for w_ref in (wrg_ref, wig_ref):
                    row += zeros(HEAD * b) + [w_ref[0, 4 * h + b]] + zeros(HEAD * (3 - b))
                wg_ref[h, HEAD * b:HEAD * (b + 1), :] = jnp.concatenate(row, axis=1).astype(wg_ref.dtype)

        for j in range(3):
            for p in range(npc):
                got = landed(p, _chip_of(*chips[j]), c)
                copy(p * 6 + j, got, got, sibling).wait_recv()
                if j == 0:
                    sent.append(copy(p * 6 + 2, got, got, (*far, c)))
                    sent[-1].start()
                sent.append(copy(p * 6 + 3 + j, got, got, sibling))
                sent[-1].start()
        for p in range(npc):
            for j in range(3):
                got = landed(p, _chip_of(*chips[(1, 0, 2)[j]]), 1 - c)
                copy(p * 6 + 3 + j, got, got, sibling).wait_recv()
        for j, chip in enumerate(chips):
            got = cw_rows(_chip_of(*chip))
            copy(npc * 6 + j, got, got, (*chip, c)).wait_recv()
        for cp in sent:
            cp.wait_send()
        for cp in locals_:
            cp.wait()
        for chip in range(N_CHIPS):
            cw_out[:, 128 * chip:128 * (chip + 1)] = ocw[8 * chip:8 * chip + CONV_K, :]

        mem_v = mem_ref[...]
        mn = mem_v * lax.rsqrt(_row_mean(mem_v * mem_v) + EPS) * mg_ref[...]
        mkv = _mm(mn, o2[...])
        kpre = mkv[:, :XATT_W]
        km = kpre * lax.rsqrt(_seg_mean(kpre * kpre, _group_matrix(XATT_W)) + EPS) * gn_ref[:, G_XK:GAINS_W]
        km_ref[...] = _head_rows(km, 0.125).astype(km_ref.dtype)
        vm_ref[...] = _head_rows(mkv[:, XATT_W:], 1.0).astype(vm_ref.dtype)

    vm = pl.BlockSpec(memory_space=pltpu.VMEM)
    hbm = pl.BlockSpec(memory_space=pl.ANY)
    head_rows = jax.ShapeDtypeStruct((4 * MEM_LEN, XATT_W), _MXU)
    out_shape = tuple(jax.ShapeDtypeStruct((N_CHIPS * a.shape[0],) + a.shape[1:], _MXU) for a in arrs) + (
        jax.ShapeDtypeStruct((CONV_K, LRU_W), F32), jax.ShapeDtypeStruct((2, 256, 512), _MXU),
        jax.ShapeDtypeStruct((1, GAINS_W), F32), head_rows, head_rows)
    n_rdma = npc * 6 + 3
    return pl.pallas_call(
        body, name="gather_weights", out_shape=out_shape,
        in_specs=[vm] * 12, out_specs=(hbm, hbm, vm, vm, vm, vm, vm, vm),
        scratch_shapes=[pltpu.VMEM(a.shape, _MXU) for a in arrs] + [
            pltpu.VMEM((8, 128), F32), pltpu.VMEM((N_CHIPS * 8, 128), F32),
            pltpu.SemaphoreType.DMA((n_rdma,)), pltpu.SemaphoreType.DMA((n_rdma,)), pltpu.SemaphoreType.DMA((n + 1,))],
        compiler_params=pltpu.CompilerParams(vmem_limit_bytes=VMEM_LIMIT),
    )(win_t, wout, wkv, conv_w, w_rg, w_ig, *head_gains, mem, mem_g)


def _mem_bwd_and_pack(mem_ref, g_ref, w_ref, gn_ref, dkm_ref, dvm_ref, gg_ref, loss_ref, vectors, gw_ref, pk_ref):
    first_row = {name: (row, width) for name, row, width in SMALL_VECTORS}
    half_rows = SMALL_ROWS // 2
    pk_ref[...] = jnp.zeros(pk_ref.shape, F32)

    def rows_at(at, n):
        assert at // half_rows == (at + n - 1) // half_rows
        return at // half_rows, slice(at % half_rows, at % half_rows + n), slice(None)

    def put(name, src):
        row, width = first_row[name]
        per_row = 1 if width < 128 else src.shape[1] // 128
        for t in range(src.shape[0]):
            for r in range(per_row):
                pk_ref[rows_at(row + per_row * t + r, 1)] = src[t:t + 1, 128 * r:128 * (r + 1)]

    for name, ref in vectors.items():
        put(name, ref)
    pk_ref[rows_at(LOSS_ROW, 1)] = loss_ref[0:1, :]
    upper = lax.broadcasted_iota(jnp.int32, (HEAD, 128), 1) >= HEAD
    for h in range(2):
        for b in range(4):
            rg = gg_ref[h, HEAD * b:HEAD * (b + 1), 128 * (b // 2):128 * (b // 2 + 1)]
            ig = gg_ref[h, HEAD * b:HEAD * (b + 1), 256 + 128 * (b // 2):256 + 128 * (b // 2 + 1)]
            if b % 2:
                rg = pltpu.roll(rg, HEAD, axis=1)
            else:
                ig = pltpu.roll(ig, HEAD, axis=1)
            pk_ref[rows_at(GATES_ROW + HEAD * (4 * h + b), HEAD)] = jnp.where(upper, ig, rg)

    mem_v = mem_ref[...]
    mh = mem_v * lax.rsqrt(_row_mean(mem_v * mem_v) + EPS)
    mn = mh * g_ref[...]
    mkv = _mm(mn, w_ref[...])
    kpre = mkv[:, :XATT_W]
    gm = _group_matrix(XATT_W)
    rk = lax.rsqrt(_seg_mean(kpre * kpre, gm) + EPS)
    kn = kpre * rk
    dk = jnp.zeros((MEM_LEN, XATT_W), F32)
    dv = jnp.zeros((MEM_LEN, XATT_W), F32)
    for j in range(4):
        mj = _lane_mask(XATT_W, HEAD * j, HEAD * (j + 1))
        dk = dk + dkm_ref[:, MEM_LEN * j:MEM_LEN * (j + 1)].T * (mj * 0.125)
        dv = dv + dvm_ref[:, MEM_LEN * j:MEM_LEN * (j + 1)].T * mj
    put("xk_norm_g", _fold_heads(_col_sum(dk * kn)))
    dkn = dk * gn_ref[:, G_XK:GAINS_W]
    dkpre = rk * (dkn - kn * _seg_mean(dkn * kn, gm))
    dmkv = jnp.concatenate([dkpre, dv], axis=1)
    gw_ref[...] = _mm_tn(mn, dmkv).reshape(gw_ref.shape)
    dmn = _mm_nt(dmkv, w_ref[...])
    put("mem_norm_g", _col_sum(dmn * mh))


def layer_fwd(x, tgt, rc, rs1, rs2, ng, win_t, cw, cb, wg, brg, big, lam, gains, sinks, km, vm, og, wout):
    seq = x.shape[0]
    tm = min(ROW_TILE, seq)
    nt = seq // tm
    nb = tm // BLOCK

    def body(x_ref, t_ref, c_ref, s1_ref, s2_ref, ng_ref, win_ref, cw_ref, cb_ref, wg_ref, brg_ref, big_ref, lam_ref,
             gn_ref, sink_ref, km_ref, vm_ref, og_ref, wout_ref,
             proj_ref, ya_ref, yb_ref, yc_ref, ycat_ref, xn_ref, dout_ref, pswa_ref, pmem_ref, psink_ref, gates_ref,
             a_ref, loss_ref,
             ext_ref, b_scr, hc_ref, kp_ref, vp_ref, lacc_ref):
        i = pl.program_id(0)

        @pl.when(i == 0)
        def _():
            ext_ref[0:8, :] = jnp.zeros((8, LRU_W), F32)
            hc_ref[...] = jnp.zeros_like(hc_ref)
            kp_ref[...] = jnp.zeros_like(kp_ref)
            vp_ref[...] = jnp.zeros_like(vp_ref)
            lacc_ref[...] = jnp.zeros_like(lacc_ref)

        xv = x_ref[...]
        xn = (xv * lax.rsqrt(_row_mean(xv * xv) + EPS) * ng_ref[...]).astype(_MXU)
        xn_ref[...] = xn.astype(xn_ref.dtype)
        proj_ref[...] = _mm_nt(xn, win_ref[...])

        u = proj_ref[:, C_LRUX:C_LRUX + LRU_W]
        ext_ref[8:8 + tm, :] = u
        xc = cb_ref[...]
        for k in range(CONV_K):
            xc = xc + cw_ref[k:k + 1, :] * ext_ref[pl.ds(5 + k, tm), :]
        ext_ref[0:8, :] = u[tm - 8:tm, :]
        rg, ig, sp, a, sq = _lru_gates(xc, wg_ref, brg_ref[...], big_ref[...], lam_ref[...])
        for k, t in enumerate((xc, rg, ig, sq)):
            gates_ref[:, LRU_W * k:LRU_W * (k + 1)] = t.astype(gates_ref.dtype)
        a_ref[...] = a
        b_scr[...] = sq * (ig * xc)
        row8 = lax.broadcasted_iota(jnp.int32, (8, LRU_W), 0)

        def scan_step(g, carry):
            r0 = pl.multiple_of(g * 8, 8)
            av = a_ref[pl.ds(r0, 8), :]
            bv = b_scr[pl.ds(r0, 8), :]
            for d in (1, 2, 4):
                a_sh = jnp.where(row8 >= d, pltpu.roll(av, d, 0), 1.0)
                b_sh = jnp.where(row8 >= d, pltpu.roll(bv, d, 0), 0.0)
                bv = bv + av * b_sh
                av = av * a_sh
            hv = bv + av * carry
            ya_ref[pl.ds(r0, 8), :] = hv
            return hv[7:8, :]

        hc_ref[0:1, :] = lax.fori_loop(0, tm // 8, scan_step, hc_ref[0:1, :], unroll=True)

        gm128 = _group_matrix(KV_W)
        cv, s1v, s2v = c_ref[...], s1_ref[...], s2_ref[...]

        def head_norm_rope(t, g):
            n = t * lax.rsqrt(_seg_mean(t * t, gm128) + EPS)
            return _rope(n * g, cv, s1v, s2v)

        qs_ = (head_norm_rope(proj_ref[:, C_SQ:C_SQ + 128], gn_ref[:, G_Q:G_K]).astype(_MXU),
               head_norm_rope(proj_ref[:, C_SQ + 128:C_SQ + 256], gn_ref[:, G_Q:G_K]).astype(_MXU))
        kr = head_norm_rope(proj_ref[:, C_SK:C_SK + KV_W], gn_ref[:, G_K:G_XQ])
        sv = proj_ref[:, C_SV:C_SV + KV_W]
        ka = _place_kv(jnp.concatenate([kp_ref[...], kr], axis=0), 0.125)
        va = _place_kv(jnp.concatenate([vp_ref[...], sv], axis=0), 1.0)
        kp_ref[...] = kr[tm - BLOCK:tm, :]
        vp_ref[...] = sv[tm - BLOCK:tm, :]
        lane128 = lax.broadcasted_iota(jnp.int32, (1, 128), 1)
        for b in range(nb):
            mask = _swa_mask((i == 0) & (b == 0)) if b == 0 else _swa_mask(False)
            band = slice(BLOCK * b, BLOCK * b + 2 * BLOCK)
            blk = slice(BLOCK * b, BLOCK * (b + 1))
            psink = jnp.zeros((BLOCK, 128), F32)
            for j in range(4):
                p, pk = _swa_probs(qs_[j // 2][blk], ka[j][band], mask, sink_ref[0, j])
                pswa_ref[blk, 2 * BLOCK * j:2 * BLOCK * (j + 1)] = p.astype(pswa_ref.dtype)
                psink = jnp.where(lane128 == j, pk, psink)
            psink_ref[blk, :] = psink
            for h in range(2):
                yb_ref[blk, KV_W * h:KV_W * (h + 1)] = _mm(
                    pswa_ref[blk, 4 * BLOCK * h:4 * BLOCK * (h + 1)],
                    jnp.concatenate([va[2 * h][band], va[2 * h + 1][band]], axis=0))

        gm256 = _group_matrix(XATT_W)
        xq = proj_ref[:, C_XQ:C_XQ + XATT_W]
        qx = xq * lax.rsqrt(_seg_mean(xq * xq, gm256) + EPS) * gn_ref[:, G_XQ:G_XK]
        pm = _mem_probs(_mm_nt(qx, km_ref[...]))
        for j in range(4):
            pmem_ref[:, MEM_LEN * j:MEM_LEN * (j + 1)] = pm[j].astype(pmem_ref.dtype)
        yc = _mm(pmem_ref[...], vm_ref[...])
        yc_ref[...] = yc

        def gated(y, g, gate):
            return y * lax.rsqrt(_row_mean(y * y) + EPS) * g * (gate * _sigmoid(gate))

        ogv = og_ref[...]
        za = gated(ya_ref[...], ogv[:, :512], proj_ref[:, C_LRUG:C_LRUG + LRU_W])
        zb = gated(yb_ref[...], ogv[:, 512:768], proj_ref[:, C_SWAG:C_SWAG + SWA_W])
        zc = gated(yc, ogv[:, 768:], proj_ref[:, C_XG:C_XG + XATT_W])
        ycat_ref[:, 0:512] = za.astype(ycat_ref.dtype)
        ycat_ref[:, 512:768] = zb.astype(ycat_ref.dtype)
        ycat_ref[:, 768:1024] = zc.astype(ycat_ref.dtype)
        out = xv + _mm(ycat_ref[...], wout_ref[...])
        err = out - t_ref[...]
        dout_ref[...] = (err * (1.0 / D_MODEL)).astype(dout_ref.dtype)
        lacc_ref[...] = lacc_ref[...] + (0.5 / D_MODEL) * jnp.sum(err * err)

        @pl.when(i == nt - 1)
        def _():
            loss_ref[...] = lacc_ref[...]

    def rows(ncol):
        return pl.BlockSpec((tm, ncol), lambda i: (i, 0))

    in_specs = [rows(D_MODEL), rows(D_MODEL), rows(128), rows(128), rows(128),
                _const_spec((1, D_MODEL)), _const_spec((D_IN, D_MODEL), True), _const_spec((CONV_K, LRU_W)),
                _const_spec((1, LRU_W)), _const_spec((2, 256, 512), True), _const_spec((1, LRU_W)),
                _const_spec((1, LRU_W)), _const_spec((1, LRU_W)), _const_spec((1, GAINS_W)), pl.BlockSpec(memory_space=pltpu.SMEM),
                _const_spec((4 * MEM_LEN, XATT_W), True), _const_spec((4 * MEM_LEN, XATT_W), True),
                _const_spec((1, D_MODEL)), _const_spec((D_MODEL, D_MODEL), True)]
    out_shape = (jax.ShapeDtypeStruct((seq, D_IN), F32), jax.ShapeDtypeStruct((seq, LRU_W), F32),
                 jax.ShapeDtypeStruct((seq, SWA_W), F32), jax.ShapeDtypeStruct((seq, XATT_W), F32),
                 jax.ShapeDtypeStruct((seq, D_MODEL), _MXU), jax.ShapeDtypeStruct((seq, D_MODEL), _MXU),
                 jax.ShapeDtypeStruct((seq, D_MODEL), _MXU), jax.ShapeDtypeStruct((seq, 4 * 2 * BLOCK), _MXU),
                 jax.ShapeDtypeStruct((seq, 4 * MEM_LEN), _MXU), jax.ShapeDtypeStruct((seq, 128), F32),
                 jax.ShapeDtypeStruct((seq, 4 * LRU_W), _MXU), jax.ShapeDtypeStruct((seq, LRU_W), F32),
                 jax.ShapeDtypeStruct((8, 128), F32))
    out_specs = (rows(D_IN), rows(LRU_W), rows(SWA_W), rows(XATT_W), rows(D_MODEL), rows(D_MODEL), rows(D_MODEL),
                 rows(4 * 2 * BLOCK), rows(4 * MEM_LEN), rows(128), rows(4 * LRU_W), rows(LRU_W),
                 _const_spec((8, 128)))
    scratch = [pltpu.VMEM((tm + 8, LRU_W), F32), pltpu.VMEM((tm, LRU_W), F32),
               pltpu.VMEM((8, LRU_W), F32), pltpu.VMEM((BLOCK, KV_W), F32), pltpu.VMEM((BLOCK, KV_W), F32),
               pltpu.VMEM((8, 128), F32)]
    return pl.pallas_call(
        body, name="layer_fwd", grid=(nt,), out_shape=out_shape, in_specs=in_specs, out_specs=out_specs,
        scratch_shapes=scratch,
        compiler_params=pltpu.CompilerParams(dimension_semantics=("arbitrary",), vmem_limit_bytes=VMEM_LIMIT),
    )(x, tgt, rc, rs1, rs2, ng, win_t, cw, cb, wg, brg, big, lam, gains, sinks, km, vm, og, wout)


def weight_grads(ycat, dout, dproj, xn, mem_operands, vectors, early_at, late_at):
    seq, ncol = xn.shape
    blk = 256
    n_out, n_in = ycat.shape[1] // blk, dproj.shape[1] // blk
    assert n_out == N_CHIPS and late_at[0] >= n_out
    g_out = jax.ShapeDtypeStruct((N_CHIPS, 2, blk // 2, dout.shape[1]), F32)
    g_kv = jax.ShapeDtypeStruct((N_CHIPS, 2, D_MODEL // 8, 2 * XATT_W), F32)
    g_small = jax.ShapeDtypeStruct((2, SMALL_ROWS // 2, 128), F32)
    shape_e, scratch_e = _hosted_reduce_shapes([g_kv], g_small)
    shape_l, scratch_l = _hosted_reduce_shapes([g_out], None)
    n_mem, names = len(mem_operands), tuple(vectors)

    def body(l1_ref, r1_ref, l2_ref, r2_ref, *refs):
        mem_refs, vec_refs = refs[:n_mem], refs[n_mem:n_mem + len(names)]
        o_ref, sum_out, sum_kv, sum_sm, gout_scr, gkv_scr, pack_scr, *scratch = refs[n_mem + len(names):]
        j = pl.program_id(0)

        @pl.when(j == 0)
        def _():
            _mem_bwd_and_pack(*mem_refs, dict(zip(names, vec_refs)), gkv_scr, pack_scr)

        def reduce_stages(closing):
            _hosted_reduce(j, n_out + n_in, closing, early_at, (gkv_scr, pack_scr), (sum_kv, sum_sm),
                           scratch[:len(scratch_e)], True)
            _hosted_reduce(j, n_out + n_in, closing, late_at, (gout_scr,), (sum_out,), scratch[len(scratch_e):], False)

        reduce_stages(False)

        @pl.when(j < n_out)
        def _():
            gout_scr[j] = _mm_tn(l1_ref[...], r1_ref[...]).reshape(g_out.shape[1:])

        @pl.when(j >= n_out)
        def _():
            o_ref[...] = _mm_tn(l2_ref[...], r2_ref[...])

        reduce_stages(True)

    vm = pl.BlockSpec(memory_space=pltpu.VMEM)
    hbm = pl.BlockSpec(memory_space=pl.ANY)
    return pl.pallas_call(
        body, name="weight_grads", grid=(n_out + n_in,),
        out_shape=(jax.ShapeDtypeStruct((dproj.shape[1], ncol), F32), *shape_l, *shape_e),
        in_specs=[pl.BlockSpec((seq, blk), lambda j: (0, jnp.minimum(j, n_out - 1))), _const_spec(dout.shape, True),
                  pl.BlockSpec((seq, blk), lambda j: (0, jnp.maximum(j - n_out, 0))), _const_spec(xn.shape, True)]
        + [vm] * (n_mem + len(names)),
        out_specs=(pl.BlockSpec((blk, ncol), lambda j: (jnp.maximum(j - n_out, 0), 0)), hbm, hbm, hbm),
        scratch_shapes=[pltpu.VMEM(s.shape, F32) for s in (g_out, g_kv, g_small)] + scratch_e + scratch_l,
        compiler_params=pltpu.CompilerParams(dimension_semantics=("arbitrary",), vmem_limit_bytes=VMEM_LIMIT),
    )(ycat, dout, dproj, xn, *mem_operands, *vectors.values())


def layer_bwd(x, dout, proj, ya, yb, yc, pswa, pmem, psink, gates, a_all, rc, rs1, rs2, ng, win_t, cw, wg, lam, gains,
              km, vm, og, wout):
    seq = x.shape[0]
    tm = min(ROW_TILE, seq)
    nt = seq // tm
    nb = tm // BLOCK

    def body(x_ref, dout_ref, proj_ref, ya_ref, yb_ref, yc_ref, pswa_ref, pmem_ref, psink_ref, gates_ref, a_ref,
             c_ref, s1_ref, s2_ref,
             yah_ref, kvh_ref, ch_ref, s1h_ref, s2h_ref,
             ng_ref, win_ref, cw_ref, wg_ref, lam_ref, gn_ref, km_ref, vm_ref, og_ref, wout_ref,
             gx_ref, dproj_ref, gwg_ref, dkm_ref, dvm_ref, gng_ref, gog_ref, gcb_ref, gbrg_ref, gbig_ref, glam_ref,
             gcw_ref, gqn_ref, gkn_ref, gxqn_ref, gsink_ref,
             hext_ref, aext_ref, an_scr, dh_scr, g_scr, dxc_ext, gcar_ref, dkcar_ref, dvcar_ref):
        i = pl.program_id(0)
        tile = nt - 1 - i
        first_tile = tile == 0

        @pl.when(i == 0)
        def _():
            for r in (gwg_ref, dkm_ref, dvm_ref, gng_ref, gog_ref, gcb_ref, gbrg_ref, gbig_ref, glam_ref, gcw_ref,
                      gqn_ref, gkn_ref, gxqn_ref, gsink_ref, gcar_ref, dkcar_ref, dvcar_ref):
                r[...] = jnp.zeros_like(r)
            dxc_ext[tm:tm + 8, :] = jnp.zeros((8, LRU_W), F32)
            aext_ref[tm:tm + 8, :] = jnp.zeros((8, LRU_W), F32)

        xv = x_ref[...]
        dov = dout_ref[...]
        dz = _mm_nt(dov, wout_ref[...])
        ogv = og_ref[...]

        def group_bwd(y, gate, g, dzg):
            r = lax.rsqrt(_row_mean(y * y) + EPS)
            n = y * r
            sg = _sigmoid(gate)
            dgate = dzg * (n * g) * (sg * (1.0 + gate * (1.0 - sg)))
            dng = dzg * (gate * sg)
            dn = dng * g
            return r * (dn - n * _row_mean(dn * n)), dgate, _col_sum(dng * n)

        dya, dga, goa = group_bwd(ya_ref[...], proj_ref[:, C_LRUG:C_LRUG + LRU_W], ogv[:, :512], dz[:, :512])
        dyb, dgb, gob = group_bwd(yb_ref[...], proj_ref[:, C_SWAG:C_SWAG + SWA_W], ogv[:, 512:768], dz[:, 512:768])
        dyc, dgc, goc = group_bwd(yc_ref[...], proj_ref[:, C_XG:C_XG + XATT_W], ogv[:, 768:], dz[:, 768:])
        gog_ref[...] += jnp.concatenate([goa, gob, goc], axis=1)
        dproj_ref[:, C_LRUG:C_LRUG + LRU_W] = dga.astype(dproj_ref.dtype)
        dproj_ref[:, C_SWAG:C_SWAG + SWA_W] = dgb.astype(dproj_ref.dtype)
        dproj_ref[:, C_XG:C_XG + XATT_W] = dgc.astype(dproj_ref.dtype)

        gm256 = _group_matrix(XATT_W)
        xq = proj_ref[:, C_XQ:C_XQ + XATT_W]
        rq = lax.rsqrt(_seg_mean(xq * xq, gm256) + EPS)
        qn = xq * rq
        qx = qn * gn_ref[:, G_XQ:G_XK]
        qxb = qx.astype(_MXU)
        dycb = dyc.astype(_MXU)
        dp_all = _mm_nt(dycb, vm_ref[...])
        dsm = []
        for j in range(4):
            pj = pmem_ref[:, MEM_LEN * j:MEM_LEN * (j + 1)].astype(F32)
            dp = dp_all[:, MEM_LEN * j:MEM_LEN * (j + 1)]
            dsm.append((pj * (dp - jnp.sum(pj * dp, axis=-1, keepdims=True))).astype(_MXU))
        ds_all = jnp.concatenate(dsm, axis=1)
        dvm_ref[...] += _mm_tn(dycb, pmem_ref[...])
        dkm_ref[...] += _mm_tn(qxb, ds_all)
        dqx = _mm(ds_all, km_ref[...])
        gxqn_ref[...] += _col_sum(dqx * qn)
        dqn = dqx * gn_ref[:, G_XQ:G_XK]
        dproj_ref[:, C_XQ:C_XQ + XATT_W] = (rq * (dqn - qn * _seg_mean(dqn * qn, gm256))).astype(dproj_ref.dtype)

        gm128 = _group_matrix(KV_W)
        cv, s1v, s2v = c_ref[...], s1_ref[...], s2_ref[...]

        def head_norm(t):
            r = lax.rsqrt(_seg_mean(t * t, gm128) + EPS)
            return t * r, r

        qn_, qr_ = zip(head_norm(proj_ref[:, C_SQ:C_SQ + 128]), head_norm(proj_ref[:, C_SQ + 128:C_SQ + 256]))
        qrope = [_rope(qn_[h] * gn_ref[:, G_Q:G_K], cv, s1v, s2v).astype(_MXU) for h in range(2)]
        kn, krr = head_norm(proj_ref[:, C_SK:C_SK + KV_W])
        kr = _rope(kn * gn_ref[:, G_K:G_XQ], cv, s1v, s2v)
        khn, _ = head_norm(kvh_ref[:, 0:KV_W])
        khr = _rope(khn * gn_ref[:, G_K:G_XQ], ch_ref[...], s1h_ref[...], s2h_ref[...])
        ka = _place_kv(jnp.concatenate([khr, kr], axis=0), 0.125)
        va = _place_kv(jnp.concatenate([kvh_ref[:, KV_W:2 * KV_W], proj_ref[:, C_SV:C_SV + KV_W]], axis=0), 1.0)
        lane128 = lax.broadcasted_iota(jnp.int32, (1, 128), 1)
        gsink = jnp.zeros((1, 128), F32)
        dk_band, dv_band, dq_blk = [], [], []
        for b in range(nb):
            band = slice(BLOCK * b, BLOCK * b + 2 * BLOCK)
            blk = slice(BLOCK * b, BLOCK * (b + 1))
            dka, dva, dsb = [], [], []
            deltas = jnp.zeros((BLOCK, 128), F32)
            for j in range(4):
                qh = qrope[j // 2][blk]
                doh = dyb[blk, KV_W * (j // 2):KV_W * (j // 2 + 1)].astype(_MXU)
                pb = pswa_ref[blk, 2 * BLOCK * j:2 * BLOCK * (j + 1)]
                p = pb.astype(F32)
                dp = _mm_nt(doh, va[j][band])
                delta = jnp.sum(p * dp, axis=-1, keepdims=True)
                ds = (p * (dp - delta)).astype(_MXU)
                deltas = jnp.where(lane128 == j, delta, deltas)
                dva.append(_mm_tn(pb, doh))
                dka.append(_mm_tn(ds, qh))
                dsb.append(ds)
            gsink = gsink - _col_sum(psink_ref[blk, :] * deltas)
            dk_band.append(_unplace_kv(dka) * 0.125)
            dv_band.append(_unplace_kv(dva))
            dq_blk.append([_mm(jnp.concatenate(dsb[2 * h:2 * h + 2], axis=1),
                               jnp.concatenate([ka[2 * h][band], ka[2 * h + 1][band]], axis=0)) for h in range(2)])
        gsink_ref[...] += gsink
        dk_rows = [dk_band[b][BLOCK:] + (dk_band[b + 1][:BLOCK] if b + 1 < nb else dkcar_ref[...]) for b in range(nb)]
        dv_rows = [dv_band[b][BLOCK:] + (dv_band[b + 1][:BLOCK] if b + 1 < nb else dvcar_ref[...]) for b in range(nb)]
        dkcar_ref[...] = dk_band[0][:BLOCK]
        dvcar_ref[...] = dv_band[0][:BLOCK]
        dkg = _rope_bwd(jnp.concatenate(dk_rows, axis=0), cv, s1v, s2v)
        gkn = _col_sum(dkg * kn)
        dkn = dkg * gn_ref[:, G_K:G_XQ]
        dproj_ref[:, C_SK:C_SK + KV_W] = (krr * (dkn - kn * _seg_mean(dkn * kn, gm128))).astype(dproj_ref.dtype)
        dproj_ref[:, C_SV:C_SV + KV_W] = jnp.concatenate(dv_rows, axis=0).astype(dproj_ref.dtype)
        gqn = jnp.zeros((1, 128), F32)
        for h in range(2):
            dqg = _rope_bwd(jnp.concatenate([dq_blk[b][h] for b in range(nb)], axis=0), cv, s1v, s2v)
            gqn = gqn + _col_sum(dqg * qn_[h])
            dqn_ = dqg * gn_ref[:, G_Q:G_K]
            dproj_ref[:, C_SQ + 128 * h:C_SQ + 128 * (h + 1)] = (
                qr_[h] * (dqn_ - qn_[h] * _seg_mean(dqn_ * qn_[h], gm128))).astype(dproj_ref.dtype)
        gqn_ref[...] += gqn
        gkn_ref[...] += gkn

        u = proj_ref[:, C_LRUX:C_LRUX + LRU_W]
        xc, rg, ig, sq = (gates_ref[:, LRU_W * k:LRU_W * (k + 1)].astype(F32) for k in range(4))
        a = a_ref[...]
        sp = _softplus(-lam_ref[...])
        hext_ref[0:8, :] = jnp.where(first_tile, 0.0, yah_ref[...])
        hext_ref[8:8 + tm, :] = ya_ref[...]
        hprev = hext_ref[pl.ds(7, tm), :]
        aext_ref[0:tm, :] = a
        an_scr[...] = aext_ref[pl.ds(1, tm), :]
        dh_scr[...] = dya
        dh_scr[tm - 1:tm, :] = dh_scr[tm - 1:tm, :] + gcar_ref[0:1, :]
        row8 = lax.broadcasted_iota(jnp.int32, (8, LRU_W), 0)

        def scan_step(gi, carry):
            r0 = pl.multiple_of((tm // 8 - 1 - gi) * 8, 8)
            av = an_scr[pl.ds(r0, 8), :]
            bv = dh_scr[pl.ds(r0, 8), :]
            for d in (1, 2, 4):
                a_sh = jnp.where(row8 < 8 - d, pltpu.roll(av, 8 - d, 0), 1.0)
                b_sh = jnp.where(row8 < 8 - d, pltpu.roll(bv, 8 - d, 0), 0.0)
                bv = bv + av * b_sh
                av = av * a_sh
            gv = bv + av * carry
            g_scr[pl.ds(r0, 8), :] = gv
            return gv[0:1, :]

        g0 = lax.fori_loop(0, tm // 8, scan_step, jnp.zeros((1, LRU_W), F32), unroll=True)
        gcar_ref[0:1, :] = a[0:1, :] * g0
        gv = g_scr[...]
        da = gv * hprev
        dig = gv * sq * xc
        dxc = gv * sq * ig
        dla = da * a - gv * (ig * xc) * ((a * a) / sq)
        drg = dla * ((-LRU_C) * sp)
        glam_ref[...] += _col_sum(dla * rg)
        dpr = drg * rg * (1.0 - rg)
        dpi = dig * ig * (1.0 - ig)
        gbrg_ref[...] += _col_sum(dpr)
        gbig_ref[...] += _col_sum(dpi)
        dpre0 = jnp.concatenate([dpr[:, :256], dpi[:, :256]], axis=1).astype(_MXU)
        dpre1 = jnp.concatenate([dpr[:, 256:], dpi[:, 256:]], axis=1).astype(_MXU)
        gwg_ref[0] += _mm_tn(xc[:, :256], dpre0)
        gwg_ref[1] += _mm_tn(xc[:, 256:], dpre1)
        dxc = dxc + jnp.concatenate([_mm_nt(dpre0, wg_ref[0]), _mm_nt(dpre1, wg_ref[1])], axis=1)
        gcb_ref[...] += _col_sum(dxc)
        dxc_ext[0:tm, :] = dxc
        du = jnp.zeros((tm, LRU_W), F32)
        for k in range(CONV_K):
            later = dxc_ext[pl.ds(3 - k, tm), :]
            gcw_ref[k:k + 1, :] += _col_sum(later * u)
            du = du + cw_ref[k:k + 1, :] * later
        dxc_ext[tm:tm + 8, :] = dxc[0:8, :]
        dproj_ref[:, C_LRUX:C_LRUX + LRU_W] = du.astype(dproj_ref.dtype)

        dxn = _mm(dproj_ref[...], win_ref[...])
        rx = lax.rsqrt(_row_mean(xv * xv) + EPS)
        xh = xv * rx
        gng_ref[...] += _col_sum(dxn * xh)
        dxh = dxn * ng_ref[...]
        gx_ref[...] = dov.astype(F32) + rx * (dxh - xh * _row_mean(dxh * xh))

        @pl.when(i == nt - 1)
        def _():
            glam_ref[...] = glam_ref[...] * (LRU_C * _sigmoid(-lam_ref[...]))
            for r in (gqn_ref, gkn_ref, gxqn_ref):
                r[...] = _fold_heads(r[...])

    def rows(ncol, arr_cols_block=0):
        return pl.BlockSpec((tm, ncol), lambda i: (nt - 1 - i, arr_cols_block))

    def halo(nrow, ncol, colblk=0):
        per = tm // nrow
        return pl.BlockSpec((nrow, ncol), lambda i: (jnp.maximum((nt - 1 - i) * per - 1, 0), colblk))

    in_specs = [rows(D_MODEL), rows(D_MODEL), rows(D_IN), rows(LRU_W), rows(SWA_W), rows(XATT_W),
                rows(4 * 2 * BLOCK), rows(4 * MEM_LEN), rows(128), rows(4 * LRU_W), rows(LRU_W),
                rows(128), rows(128), rows(128),
                halo(8, LRU_W), halo(BLOCK, 2 * KV_W, C_SK // (2 * KV_W)),
                halo(BLOCK, 128), halo(BLOCK, 128), halo(BLOCK, 128),
                _const_spec((1, D_MODEL)), _const_spec((D_IN, D_MODEL), True), _const_spec((CONV_K, LRU_W)),
                _const_spec((2, 256, 512), True), _const_spec((1, LRU_W)), _const_spec((1, GAINS_W)),
                _const_spec((4 * MEM_LEN, XATT_W), True), _const_spec((4 * MEM_LEN, XATT_W), True),
                _const_spec((1, D_MODEL)), _const_spec((D_MODEL, D_MODEL), True)]
    small = [(2, 256, 512), (XATT_W, 4 * MEM_LEN), (XATT_W, 4 * MEM_LEN), (1, D_MODEL), (1, D_MODEL), (1, LRU_W), (1, LRU_W),
             (1, LRU_W), (1, LRU_W), (CONV_K, LRU_W), (1, 128), (1, 128), (1, XATT_W), (1, 128)]
    out_shape = (jax.ShapeDtypeStruct((seq, D_MODEL), F32), jax.ShapeDtypeStruct((seq, D_IN), _MXU)) + tuple(
        jax.ShapeDtypeStruct(s, F32) for s in small)
    out_specs = (rows(D_MODEL), rows(D_IN)) + tuple(_const_spec(s) for s in small)
    scratch = [pltpu.VMEM((tm + 8, LRU_W), F32), pltpu.VMEM((tm + 8, LRU_W), F32),
               pltpu.VMEM((tm, LRU_W), F32), pltpu.VMEM((tm, LRU_W), F32), pltpu.VMEM((tm, LRU_W), F32),
               pltpu.VMEM((tm + 8, LRU_W), F32),
               pltpu.VMEM((8, LRU_W), F32), pltpu.VMEM((BLOCK, KV_W), F32), pltpu.VMEM((BLOCK, KV_W), F32)]
    return pl.pallas_call(
        body, name="layer_bwd", grid=(nt,), out_shape=out_shape, in_specs=in_specs, out_specs=out_specs,
        scratch_shapes=scratch,
        compiler_params=pltpu.CompilerParams(dimension_semantics=("arbitrary",), vmem_limit_bytes=VMEM_LIMIT),
    )(x, dout, proj, ya, yb, yc, pswa, pmem, psink, gates, a_all, rc, rs1, rs2, ya, proj, rc, rs1, rs2,
      ng, win_t, cw, wg, lam, gains, km, vm, og, wout)


def _reduce_protocol(big, sm, outs, osm, r1, r1s, wire, r2, r2s, wire2, ps, own, send, recv, lsem):
    nbig = len(big)
    x, y, c = lax.axis_index("x"), lax.axis_index("y"), lax.axis_index("c")
    sibling = (x, y, 1 - c)
    near, far, diag = _partners(x, y, c)
    me, near_id, far_id, diag_id = _chip_of(x, y), _chip_of(*near), _chip_of(*far), _chip_of(*diag)

    def copy(k, src, dst, to):
        return pltpu.make_async_remote_copy(src_ref=src, dst_ref=dst, send_sem=send.at[k], recv_sem=recv.at[k],
                                            device_id=to, device_id_type=MESH)

    def sent(stage, a):
        if a == nbig:
            src, dst, to = ((sm.at[1 - c], r1s, sibling), (r1s, r2s.at[0], (*near, c)), (ps, r2s.at[1], (*far, c)),
                            (osm.at[c], osm.at[c], sibling))[stage]
            return [copy(5 * nbig + stage, src, dst, to)]
        if stage == 0:
            return [copy(5 * a, big[a].at[:, 1 - c], r1[a], sibling)]
        if stage == 1:
            return [copy(5 * a + 1, wire[a].at[near_id], r2[a].at[0], (*near, c)),
                    copy(5 * a + 2, wire[a].at[diag_id], r2[a].at[1], (*near, c))]
        if stage == 2:
            return [copy(5 * a + 3, wire2[a], r2[a].at[2], (*far, c))]
        return [copy(5 * a + 4, outs[a].at[c], outs[a].at[c], sibling)]

    arrays = range(nbig + (sm is not None))

    def start(stage, a):
        for cp in sent(stage, a):
            cp.start()

    def arrived(k, ref):
        copy(k, ref, ref, sibling).wait_recv()

    def loads():
        return [pltpu.make_async_copy(big[a].at[:, c], own[a], lsem.at[a]) for a in range(nbig)]

    def stage0():
        for a in arrays:
            start(0, a)
        for cp in loads():
            cp.start()

    def stage1():
        for a in range(nbig):
            loads()[a].wait()
            arrived(5 * a, r1[a])
            for k in range(N_CHIPS):
                r1[a][k] = own[a][k] + r1[a][k]
                wire[a][k] = r1[a][k].astype(wire[a].dtype)
            start(1, a)
        if sm is not None:
            arrived(5 * nbig, r1s)
            r1s[...] = sm[c] + r1s[...]
            start(1, nbig)

    def stage2():
        for a in range(nbig):
            arrived(5 * a + 1, r2[a].at[0])
            arrived(5 * a + 2, r2[a].at[1])
            r1[a][me] = r1[a][me] + r2[a][0].astype(F32)
            wire2[a][...] = (r1[a][far_id] + r2[a][1].astype(F32)).astype(wire2[a].dtype)
            start(2, a)
        if sm is not None:
            arrived(5 * nbig + 1, r2s.at[0])
            ps[...] = r1s[...] + r2s[0]
            start(2, nbig)

    def stage3():
        for a in range(nbig):
            arrived(5 * a + 3, r2[a].at[2])
            outs[a][c] = r1[a][me] + r2[a][2].astype(F32)
            start(3, a)
        if sm is not None:
            arrived(5 * nbig + 2, r2s.at[1])
            osm[c] = ps[...] + r2s[1]
            start(3, nbig)

    def stage4():
        for a in range(nbig):
            arrived(5 * a + 4, outs[a].at[1 - c])
        if sm is not None:
            arrived(5 * nbig + 3, osm.at[1 - c])
        for stage in range(4):
            for a in arrays:
                for cp in sent(stage, a):
                    cp.wait_send()

    return [stage0, stage1, stage2, stage3, stage4]


def _reduce_buffers(bigs, g_small):
    half = [b.shape[2:] for b in bigs]
    sm_half = None if g_small is None else g_small.shape[1:]
    out_shape = [jax.ShapeDtypeStruct((2,) + h, F32) for h in half]
    small = lambda lead: [] if g_small is None else [pltpu.VMEM(lead + sm_half, F32)]
    if g_small is not None:
        out_shape.append(jax.ShapeDtypeStruct(g_small.shape, F32))
    n_sem = 5 * len(bigs) + 4
    scratch = ([pltpu.VMEM((N_CHIPS,) + h, F32) for h in half] + small(())
               + [pltpu.VMEM((N_CHIPS,) + h, _WIRE) for h in half]
               + [pltpu.VMEM((3,) + h, _WIRE) for h in half] + small((2,))
               + [pltpu.VMEM(h, _WIRE) for h in half] + small(())
               + [pltpu.VMEM((N_CHIPS,) + h, F32) for h in half]
               + [pltpu.SemaphoreType.DMA((n_sem,)), pltpu.SemaphoreType.DMA((n_sem,)),
                  pltpu.SemaphoreType.DMA((len(bigs),))])
    return out_shape, scratch


def _split_reduce_refs(refs, nbig, has_small):
    it = iter(refs)
    take = lambda n: [next(it) for _ in range(n)]
    one = lambda: next(it) if has_small else None
    big, sm = take(nbig), one()
    outs, osm = take(nbig), one()
    r1, r1s, wire, r2, r2s, wire2, ps, own = take(nbig), one(), take(nbig), take(nbig), one(), take(nbig), one(), take(nbig)
    send, recv, lsem = take(3)
    return big, sm, outs, osm, r1, r1s, wire, r2, r2s, wire2, ps, own, send, recv, lsem


def _hosted_reduce_shapes(bigs, g_small):
    red_shape, scratch = _reduce_buffers(bigs, g_small)
    nres = len(red_shape)
    return red_shape, [pltpu.VMEM(r.shape, r.dtype) for r in red_shape] + scratch + [pltpu.SemaphoreType.DMA((nres,))]


def _hosted_reduce(step, n_steps, closing, stage_at, operands, results, scratch, has_small):
    nres = len(results)
    sums, rest, fsem = scratch[:nres], scratch[nres:-1], scratch[-1]
    refs = tuple(operands) + tuple(sums) + tuple(rest)

    def to_results():
        out = [pltpu.make_async_copy(sums[k], results[k], fsem.at[k]) for k in range(nres)]
        for cp in out:
            cp.start()
        for cp in out:
            cp.wait()

    stages = _reduce_protocol(*_split_reduce_refs(refs, nres - has_small, has_small))

    def last_stage():
        stages[-1]()
        to_results()

    for at, stage in zip(stage_at, stages[:-1] + [last_stage]):
        if closing == (at == n_steps):
            pl.when(step == min(at, n_steps - 1))(stage)


def reduce_grads(big, name, parts):
    chips, halves, rows_, cols = big.shape
    sub = jax.ShapeDtypeStruct((chips, halves, rows_ // parts, cols), big.dtype)

    def body(b_ref, o_ref, *scratch):
        refs = [b_ref.at[:, :, s] for s in range(parts)] + [o_ref.at[:, s] for s in range(parts)] + list(scratch)
        for stage in _reduce_protocol(*_split_reduce_refs(refs, parts, False)):
            stage()

    _, scratch = _reduce_buffers([sub] * parts, None)
    return pl.pallas_call(
        body, name=name, out_shape=jax.ShapeDtypeStruct((halves, parts, rows_ // parts, cols), F32),
        in_specs=[pl.BlockSpec(memory_space=pl.ANY)], out_specs=pl.BlockSpec(memory_space=pltpu.VMEM),
        scratch_shapes=scratch, compiler_params=pltpu.CompilerParams(vmem_limit_bytes=VMEM_LIMIT),
    )(big.reshape(chips, halves, parts, rows_ // parts, cols))


def adamw(items, g_pack, ws, ms, vs):
    blocks = []
    for k, (w, _, _, _) in enumerate(items):
        rows_, cols = w.shape
        tr = max(t for t in range(8, rows_ + 1, 8) if rows_ % t == 0 and t * cols * 4 <= ADAM_BLOCK_BYTES)
        blocks += [(k, r, tr) for r in range(0, rows_, tr)]
    nin, n = 4 * len(items), len(ws)

    def body(*refs):
        mats_in, small_in = refs[:nin], refs[nin:nin + 1 + 3 * n]
        n_out = nin + 4 * n + 1
        outs, scratch = refs[nin + 1 + 3 * n:nin + 1 + 3 * n + n_out], refs[nin + 1 + 3 * n + n_out:]
        vin, vout, (lsem, ssem) = scratch[:nin], scratch[nin:2 * nin], scratch[2 * nin:]
        loads = [[pltpu.make_async_copy(mats_in[4 * k + q].at[pl.ds(r, tr)], vin[4 * k + q].at[pl.ds(r, tr)],
                                        lsem.at[4 * c + q]) for q in range(4)] for c, (k, r, tr) in enumerate(blocks)]
        for cps in loads:
            for cp in cps:
                cp.start()
        _adamw_small(small_in[0], *(small_in[1 + k * n:1 + (k + 1) * n] for k in range(3)),
                     *(outs[nin + k * n:nin + (k + 1) * n] for k in range(4)), outs[-1])
        stores = []
        for c, (k, r, tr) in enumerate(blocks):
            for cp in loads[c]:
                cp.wait()
            w_v, g_v, m_v, v_v = (vin[4 * k + q][r:r + tr, :] for q in range(4))
            vout[4 * k + 1][r:r + tr, :], vout[4 * k + 2][r:r + tr, :], vout[4 * k + 3][r:r + tr, :] = _adam_update(
                w_v, g_v, m_v, v_v)
            for q in range(4):
                src = vin[4 * k + 1] if q == 0 else vout[4 * k + q]
                stores.append(pltpu.make_async_copy(src.at[pl.ds(r, tr)], outs[4 * k + q].at[pl.ds(r, tr)],
                                                    ssem.at[4 * c + q]))
                stores[-1].start()
        for cp in stores:
            cp.wait()

    shapes = [jax.ShapeDtypeStruct(w.shape, F32) for w, _, _, _ in items for _ in range(4)]
    vm = pl.BlockSpec(memory_space=pltpu.VMEM)
    hbm = pl.BlockSpec(memory_space=pl.ANY)
    like = [jax.ShapeDtypeStruct(w.shape, F32) for w in ws]
    res = pl.pallas_call(
        body, name="adamw", out_shape=(*shapes, *like * 4, jax.ShapeDtypeStruct((1, 1), F32)),
        in_specs=[hbm] * nin + [vm] * (1 + 3 * n), out_specs=(*[hbm] * nin, *[vm] * (4 * n + 1)),
        scratch_shapes=[pltpu.VMEM(s.shape, F32) for s in shapes] * 2 + [pltpu.SemaphoreType.DMA((4 * len(blocks),))] * 2,
        compiler_params=pltpu.CompilerParams(vmem_limit_bytes=VMEM_LIMIT),
    )(*[a for item in items for a in item], g_pack, *ws, *ms, *vs)
    return [res[4 * k:4 * k + 4] for k in range(len(items))], res[nin:]


def _adam_update(w, g, m, v):
    nm = ADAM_B1 * m + (1.0 - ADAM_B1) * g
    nv = ADAM_B2 * v + (1.0 - ADAM_B2) * (g * g)
    m_hat = nm / (1.0 - ADAM_B1 ** ADAM_STEP)
    v_hat = nv / (1.0 - ADAM_B2 ** ADAM_STEP)
    return (-ADAM_LR) * (m_hat / (jnp.sqrt(v_hat) + ADAM_EPS) + ADAM_WD * w), nm, nv


def _adamw_small(pk, w_refs, m_refs, v_refs, g_out, d_out, nm_out, nv_out, loss_ref):
    nvec = len(SMALL_VECTORS)
    loss_ref[...] = pk[LOSS_ROW:LOSS_ROW + 1, 0:1]
    chip = 2 * lax.axis_index("x") + lax.axis_index("y")
    for k, (name, row, width) in enumerate(SMALL_VECTORS):
        if name == "conv_w":
            g = jnp.concatenate([pk[pl.ds(row + 4 * t + chip, 1), :] for t in range(CONV_K)], axis=0)[None]
        elif width >= 128:
            g = jnp.concatenate([pk[row + r:row + r + 1, :] for r in range(width // 128)], axis=1)
        else:
            g = pk[row:row + 1, 0:width]
        g_out[k][...] = g
        d_out[k][...], nm_out[k][...], nv_out[k][...] = _adam_update(w_refs[k][...], g, m_refs[k][...], v_refs[k][...])
    for k in range(nvec, nvec + len(SMALL_MATRICES)):
        for b in range(LRU_BLOCKS):
            rows_ = pk[GATES_ROW + HEAD * b:GATES_ROW + HEAD * (b + 1), :]
            g = (pltpu.roll(rows_, HEAD, axis=1) if k > nvec else rows_)[:, 0:HEAD]
            g_out[k][0, b] = g
            d_out[k][0, b], nm_out[k][0, b], nv_out[k][0, b] = _adam_update(
                w_refs[k][0, b], g, m_refs[k][0, b], v_refs[k][0, b])


SMALL_VECTORS = (("norm_g", 512, 1024), ("mem_norm_g", 520, 1024), ("conv_w", 528, 512), ("conv_b", 544, 512),
                 ("b_rg", 548, 512), ("b_ig", 552, 512), ("lru_lambda", 556, 512), ("q_norm_g", 560, 64),
                 ("k_norm_g", 561, 64), ("sinks", 562, 4), ("xq_norm_g", 563, 64), ("xk_norm_g", 564, 64),
                 ("out_norm_g", 565, 1024))
LOSS_ROW = 573
SMALL_MATRICES = ("w_rg", "w_ig")
GATES_ROW = 0
SMALL_ROWS = 640


def _rope_tables(seq):
    pos = np.arange(seq, dtype=np.float32)
    inv_freq = (np.float32(ROPE_THETA) ** (-(np.arange(0, ROPE_DIM, 2, dtype=np.float32) / np.float32(ROPE_DIM)))
                ).astype(np.float32)
    ang = (pos[:, None] * inv_freq[None, :]).astype(np.float32)
    cos, sin = np.cos(ang).astype(np.float32), np.sin(ang).astype(np.float32)
    z = lambda n: np.zeros((seq, n), np.float32)
    c64 = np.concatenate([cos, cos, np.ones((seq, HEAD - ROPE_DIM), np.float32)], axis=1)
    s1_64 = np.concatenate([-sin, z(HEAD - 8)], axis=1)
    s2_64 = np.concatenate([z(8), sin, z(HEAD - ROPE_DIM)], axis=1)
    return tuple(jnp.asarray(np.concatenate([t, t], axis=1)) for t in (c64, s1_64, s2_64))


def kernel(x, mem, norm_g, mem_norm_g, w_in, conv_w, conv_b, w_rg, b_rg, w_ig, b_ig, lru_lambda, q_norm_g, k_norm_g, sinks, w_mem_kv, xq_norm_g, xk_norm_g, out_norm_g, w_out, loss_target, m_norm_g, m_mem_norm_g, m_w_in, m_conv_w, m_conv_b, m_w_rg, m_b_rg, m_w_ig, m_b_ig, m_lru_lambda, m_q_norm_g, m_k_norm_g, m_sinks, m_w_mem_kv, m_xq_norm_g, m_xk_norm_g, m_out_norm_g, m_w_out, v_norm_g, v_mem_norm_g, v_w_in, v_conv_w, v_conv_b, v_w_rg, v_b_rg, v_w_ig, v_b_ig, v_lru_lambda, v_q_norm_g, v_k_norm_g, v_sinks, v_w_mem_kv, v_xq_norm_g, v_xk_norm_g, v_out_norm_g, v_w_out):
    seq = x.shape[1]
    xs, tgt, mems = x[0], loss_target[0], mem[0]

    win_t, wout, wkv, cw, wg, gains, km, vm = gather_weights(
        w_in[0].T, w_out[0], w_mem_kv[0], conv_w, w_rg, w_ig, (q_norm_g, k_norm_g, xq_norm_g, xk_norm_g), mems,
        mem_norm_g)
    rc, rs1, rs2 = _rope_tables(seq)
    proj, ya, yb, yc, ycat, xn, dout, pswa, pmem, psink, gates, a_all, loss8 = layer_fwd(
        xs, tgt, rc, rs1, rs2, norm_g, win_t, cw, conv_b, wg, b_rg, b_ig, lru_lambda, gains, sinks, km, vm,
        out_norm_g, wout)
    (gx, dproj, g_wg, dkm, dvm, g_ng, g_og, g_cb, g_brg, g_big, g_lam, g_cw, g_qn, g_kn, g_xqn, g_sink) = layer_bwd(
        xs, dout, proj, ya, yb, yc, pswa, pmem, psink, gates, a_all, rc, rs1, rs2, norm_g, win_t, cw, wg, lru_lambda,
        gains, km, vm, out_norm_g, wout)
    g_win_t, r_out, r_kv, r_small = weight_grads(
        ycat, dout, dproj, xn, (mems, mem_norm_g, wkv, gains, dkm, dvm, g_wg, loss8), dict(
            norm_g=g_ng, conv_w=g_cw, conv_b=g_cb, b_rg=g_brg, b_ig=g_big, lru_lambda=g_lam, q_norm_g=g_qn,
            k_norm_g=g_kn, sinks=g_sink, xq_norm_g=g_xqn, out_norm_g=g_og), (0, 1, 4, 7, 8), (4, 5, 9, 11, 13))
    r_in = reduce_grads(g_win_t.reshape(N_CHIPS, 2, D_IN // 8, D_MODEL), "reduce_w_in", 6)

    r_small = r_small.reshape(SMALL_ROWS, 128)
    grads = {}
    weights = dict(norm_g=norm_g, mem_norm_g=mem_norm_g, w_in=w_in, conv_w=conv_w, conv_b=conv_b, w_rg=w_rg, b_rg=b_rg,
                   w_ig=w_ig, b_ig=b_ig, lru_lambda=lru_lambda, q_norm_g=q_norm_g, k_norm_g=k_norm_g, sinks=sinks,
                   w_mem_kv=w_mem_kv, xq_norm_g=xq_norm_g, xk_norm_g=xk_norm_g, out_norm_g=out_norm_g, w_out=w_out)
    ms = dict(norm_g=m_norm_g, mem_norm_g=m_mem_norm_g, w_in=m_w_in, conv_w=m_conv_w, conv_b=m_conv_b, w_rg=m_w_rg,
              b_rg=m_b_rg, w_ig=m_w_ig, b_ig=m_b_ig, lru_lambda=m_lru_lambda, q_norm_g=m_q_norm_g, k_norm_g=m_k_norm_g,
              sinks=m_sinks, w_mem_kv=m_w_mem_kv, xq_norm_g=m_xq_norm_g, xk_norm_g=m_xk_norm_g,
              out_norm_g=m_out_norm_g, w_out=m_w_out)
    vs = dict(norm_g=v_norm_g, mem_norm_g=v_mem_norm_g, w_in=v_w_in, conv_w=v_conv_w, conv_b=v_conv_b, w_rg=v_w_rg,
              b_rg=v_b_rg, w_ig=v_w_ig, b_ig=v_b_ig, lru_lambda=v_lru_lambda, q_norm_g=v_q_norm_g, k_norm_g=v_k_norm_g,
              sinks=v_sinks, w_mem_kv=v_w_mem_kv, xq_norm_g=v_xq_norm_g, xk_norm_g=v_xk_norm_g,
              out_norm_g=v_out_norm_g, w_out=v_w_out)

    delta, new_m, new_v = {}, {}, {}
    small_names = [n for n, _, _ in SMALL_VECTORS] + list(SMALL_MATRICES)
    (res_in, res_out, res_kv), res = adamw(
        [(w_in[0].T, r_in.reshape(D_IN // 4, D_MODEL), m_w_in[0].T, v_w_in[0].T),
         (w_out[0], r_out.reshape(D_MODEL // 4, D_MODEL), m_w_out[0], v_w_out[0]),
         (w_mem_kv[0], r_kv.reshape(D_MODEL // 4, 2 * XATT_W), m_w_mem_kv[0], v_w_mem_kv[0])],
        r_small, [weights[n] for n in small_names], [ms[n] for n in small_names], [vs[n] for n in small_names])
    grads["w_in"], delta["w_in"], new_m["w_in"], new_v["w_in"] = (r.T[None] for r in res_in)
    grads["w_out"], delta["w_out"], new_m["w_out"], new_v["w_out"] = (r[None] for r in res_out)
    grads["w_mem_kv"], delta["w_mem_kv"], new_m["w_mem_kv"], new_v["w_mem_kv"] = (r[None] for r in res_kv)
    nall = len(small_names)
    for k, into in enumerate((grads, delta, new_m, new_v)):
        into.update(zip(small_names, res[k * nall:(k + 1) * nall]))
    loss = res[-1].reshape(())

    order = ("norm_g", "mem_norm_g", "w_in", "conv_w", "conv_b", "w_rg", "b_rg", "w_ig", "b_ig", "lru_lambda",
             "q_norm_g", "k_norm_g", "sinks", "w_mem_kv", "xq_norm_g", "xk_norm_g", "out_norm_g", "w_out")
    return (loss, gx[None], *[grads[n] for n in order], *[delta[n] for n in order], *[new_m[n] for n in order],
            *[new_v[n] for n in order])
```

```python
import jax
import jax.numpy as jnp
import numpy as np
from jax import lax
from jax.experimental import pallas as pl
from jax.experimental.pallas import tpu as pltpu

F32 = jnp.float32
_MXU = jnp.bfloat16
_WIRE = jnp.bfloat16

D_MODEL = 1024
MEM_LEN = 256
HEAD = 64
LRU_W = 512
LRU_BLOCKS = 8
CONV_K = 4
LRU_C = 8.0
SWA_W = 256
KV_W = 128
XATT_W = 256
BLOCK = 128
D_IN = 2304
ROPE_THETA = 500000.0
ROPE_DIM = 16
EPS = 1e-6
NEG_INF = -1e30
C_LRUX, C_LRUG, C_SQ, C_SK, C_SV, C_SWAG, C_XQ, C_XG = 0, 512, 1024, 1280, 1408, 1536, 1792, 2048
G_Q, G_K, G_XQ, G_XK, GAINS_W = 0, 128, 256, 512, 768

ADAM_LR, ADAM_B1, ADAM_B2, ADAM_EPS, ADAM_WD, ADAM_STEP = 0.001, 0.9, 0.999, 1e-08, 0.01, 10

N_CHIPS = 4
ROW_TILE = 256
VMEM_LIMIT = 56 * 1024 * 1024
ADAM_BLOCK_BYTES = 320 * 1024
MESH = pl.DeviceIdType.MESH


def _mm(a, b):
    return jnp.dot(a.astype(_MXU), b.astype(_MXU), preferred_element_type=F32)


def _mm_nt(a, b):
    return lax.dot_general(a.astype(_MXU), b.astype(_MXU), (((1,), (1,)), ((), ())), preferred_element_type=F32)


def _mm_tn(a, b):
    return lax.dot_general(a.astype(_MXU), b.astype(_MXU), (((0,), (0,)), ((), ())), preferred_element_type=F32)


def _group_matrix(width):
    r = lax.shift_right_logical(lax.broadcasted_iota(jnp.int32, (width, width), 0), 6)
    c = lax.shift_right_logical(lax.broadcasted_iota(jnp.int32, (width, width), 1), 6)
    return (r == c).astype(_MXU)


def _seg_mean(x, gm):
    return jnp.dot(x.astype(_MXU), gm, preferred_element_type=F32) * (1.0 / HEAD)


def _row_mean(x):
    return jnp.mean(x, axis=-1, keepdims=True)


def _col_sum(x):
    return jnp.sum(x, axis=0, keepdims=True)


def _sigmoid(x):
    return jax.nn.sigmoid(x)


def _softplus(z):
    e = jnp.exp(-jnp.abs(z))
    u = 1.0 + e
    log1p_e = jnp.where(u == 1.0, e, jnp.log(u) * (e / (u - 1.0)))
    return jnp.maximum(z, 0.0) + log1p_e


def _rope(t, c, s1, s2):
    return t * c + pltpu.roll(t, 120, 1) * s1 + pltpu.roll(t, 8, 1) * s2


def _rope_bwd(d, c, s1, s2):
    return d * c + pltpu.roll(d * s1, 8, 1) + pltpu.roll(d * s2, 120, 1)


def _fold_heads(v):
    out = v
    for k in range(1, v.shape[1] // HEAD):
        out = out + pltpu.roll(v, HEAD * k, 1)
    return out


def _lane_mask(width, lo, hi):
    lane = lax.broadcasted_iota(jnp.int32, (1, width), 1)
    return ((lane >= lo) & (lane < hi)).astype(F32)


def _swa_mask(first_block):
    qi = lax.broadcasted_iota(jnp.int32, (BLOCK, 2 * BLOCK), 0)
    kj = lax.broadcasted_iota(jnp.int32, (BLOCK, 2 * BLOCK), 1)
    rel = qi + BLOCK - kj
    ok = (rel >= 0) & (rel < BLOCK)
    return ok & (jnp.logical_not(first_block) | (kj >= BLOCK))


def _place_kv(t, scale):
    lo = t * (_lane_mask(KV_W, 0, HEAD) * scale)
    hi = t * (_lane_mask(KV_W, HEAD, KV_W) * scale)
    return [a.astype(_MXU) for a in (lo, pltpu.roll(lo, HEAD, 1), pltpu.roll(hi, HEAD, 1), hi)]


def _unplace_kv(d):
    return (_lane_mask(KV_W, 0, HEAD) * (d[0] + pltpu.roll(d[1], HEAD, 1))
            + _lane_mask(KV_W, HEAD, KV_W) * (d[3] + pltpu.roll(d[2], HEAD, 1)))


def _swa_probs(qh, ka, mask, sink):
    s = _mm_nt(qh, ka)
    s = jnp.where(mask, s, NEG_INF)
    m = jnp.maximum(jnp.max(s, axis=-1, keepdims=True), sink)
    p = jnp.exp(s - m)
    esink = jnp.exp(sink - m)
    inv = 1.0 / (jnp.sum(p, axis=-1, keepdims=True) + esink)
    return p * inv, esink * inv


def _mem_probs(s_all):
    out = []
    for j in range(4):
        s = s_all[:, MEM_LEN * j:MEM_LEN * (j + 1)]
        p = jnp.exp(s - jnp.max(s, axis=-1, keepdims=True))
        out.append(p * (1.0 / jnp.sum(p, axis=-1, keepdims=True)))
    return out


def _head_rows(t, scale):
    return jnp.concatenate([t * (_lane_mask(XATT_W, HEAD * j, HEAD * (j + 1)) * scale) for j in range(4)], axis=0)


def _lru_gates(xc, wg_ref, brg, big, lam):
    p0 = _mm(xc[:, :256], wg_ref[0])
    p1 = _mm(xc[:, 256:], wg_ref[1])
    rg = _sigmoid(jnp.concatenate([p0[:, :256], p1[:, :256]], axis=1) + brg)
    ig = _sigmoid(jnp.concatenate([p0[:, 256:], p1[:, 256:]], axis=1) + big)
    sp = _softplus(-lam)
    la = (-LRU_C) * rg * sp
    a = jnp.exp(la)
    th = jnp.tanh(la)
    one_minus_a2 = (-2.0 * th) / (1.0 - th)
    return rg, ig, sp, a, jnp.sqrt(one_minus_a2)


def _const_spec(shape, single=False):
    zeros = (0,) * len(shape)
    if single:
        return pl.BlockSpec(shape, lambda i: zeros, pipeline_mode=pl.Buffered(1))
    return pl.BlockSpec(shape, lambda i: zeros)


def _chip_of(x, y):
    return 2 * x + y


def _partners(x, y, c):
    north = c == 1
    near = (jnp.where(north, 1 - x, x), jnp.where(north, y, 1 - y))
    far = (jnp.where(north, x, 1 - x), jnp.where(north, 1 - y, y))
    return near, far, (1 - x, 1 - y)


def gather_weights(win_t, wout, wkv, conv_w, w_rg, w_ig, head_gains, mem, mem_g):
    arrs = (win_t, wout, wkv)
    n = len(arrs)
    pieces = [(a, 0, arr.shape[0] // 2) for a, arr in enumerate(arrs)]
    npc = len(pieces)

    def body(a0, a1, a2, cw_in, wrg_ref, wig_ref, q_ref, k_ref, xq_ref, xk_ref, mem_ref, mg_ref,
             o0, o1, o2, cw_out, wg_ref, gn_ref, km_ref, vm_ref, s0, s1, s2, cw, ocw, send, recv, lsem):
        ins, outs = (s0, s1, s2), (o0, o1, o2)
        for src, dst in zip((a0, a1, a2), ins):
            dst[...] = src[...].astype(dst.dtype)
        cw[...] = jnp.zeros(cw.shape, F32)
        cw[0:CONV_K, :] = cw_in[0]
        x, y, c = lax.axis_index("x"), lax.axis_index("y"), lax.axis_index("c")
        sibling = (x, y, 1 - c)
        near, far, diag = _partners(x, y, c)
        chips = [near, far, diag]
        me = _chip_of(x, y)

        def landed(p, chip, half):
            a, off, rows_ = pieces[p]
            r = ins[a].shape[0]
            return outs[a].at[pl.ds(pl.multiple_of(chip * r + half * (r // 2) + off, 16), rows_)]

        def mine(p):
            a, off, rows_ = pieces[p]
            return ins[a].at[pl.ds(pl.multiple_of(c * (ins[a].shape[0] // 2) + off, 16), rows_)]

        def copy(k, src, dst, to):
            return pltpu.make_async_remote_copy(src_ref=src, dst_ref=dst, send_sem=send.at[k], recv_sem=recv.at[k],
                                                device_id=to, device_id_type=MESH)

        def cw_rows(chip):
            return ocw.at[pl.ds(pl.multiple_of(chip * 8, 8), 8)]

        locals_ = []
        for a in range(n):
            r = ins[a].shape[0]
            locals_.append(pltpu.make_async_copy(ins[a], outs[a].at[pl.ds(pl.multiple_of(me * r, 16), r)], lsem.at[a]))
        locals_.append(pltpu.make_async_copy(cw, cw_rows(me), lsem.at[n]))
        for cp in locals_:
            cp.start()

        sent = []
        for p in range(npc):
            for j in range(2):
                sent.append(copy(p * 6 + j, mine(p), landed(p, me, c), (*chips[j], c)))
        for j, chip in enumerate(chips):
            sent.append(copy(npc * 6 + j, cw, cw_rows(me), (*chip, c)))
        for cp in sent:
            cp.start()

        gn_ref[...] = jnp.concatenate([q_ref[...]] * 2 + [k_ref[...]] * 2 + [xq_ref[...]] * 4 + [xk_ref[...]] * 4,
                                      axis=1)
        zeros = lambda lanes: [jnp.zeros((HEAD, lanes), F32)] if lanes else []
        for h in range(2):
            for b in range(4):
                row = []
                for w_ref in (wrg_ref, wig_ref):
                    row += zeros(HEAD * b) + [w_ref[0, 4 * h + b]] + zeros(HEAD * (3 - b))
                wg_ref[h, HEAD * b:HEAD * (b + 1), :] = jnp.concatenate(row, axis=1).astype(wg_ref.dtype)

        for j in range(3):
            for p in range(npc):
                got = landed(p, _chip_of(*chips[j]), c)
                copy(p * 6 + j, got, got, sibling).wait_recv()
                if j == 0:
                    sent.append(copy(p * 6 + 2, got, got, (*far, c)))
                    sent[-1].start()
                sent.append(copy(p * 6 + 3 + j, got, got, sibling))
                sent[-1].start()
        for p in range(npc):
            for j in range(3):
                got = landed(p, _chip_of(*chips[(1, 0, 2)[j]]), 1 - c)
                copy(p * 6 + 3 + j, got, got, sibling).wait_recv()
        for j, chip in enumerate(chips):
            got = cw_rows(_chip_of(*chip))
            copy(npc * 6 + j, got, got, (*chip, c)).wait_recv()
        for cp in sent:
            cp.wait_send()
        for cp in locals_:
            cp.wait()
        for chip in range(N_CHIPS):
            cw_out[:, 128 * chip:128 * (chip + 1)] = ocw[8 * chip:8 * chip + CONV_K, :]

        mem_v = mem_ref[...]
        mn = mem_v * lax.rsqrt(_row_mean(mem_v * mem_v) + EPS) * mg_ref[...]
        mkv = _mm(mn, o2[...])
        kpre = mkv[:, :XATT_W]
        km = kpre * lax.rsqrt(_seg_mean(kpre * kpre, _group_matrix(XATT_W)) + EPS) * gn_ref[:, G_XK:GAINS_W]
        km_ref[...] = _head_rows(km, 0.125).astype(km_ref.dtype)
        vm_ref[...] = _head_rows(mkv[:, XATT_W:], 1.0).astype(vm_ref.dtype)

    vm = pl.BlockSpec(memory_space=pltpu.VMEM)
    hbm = pl.BlockSpec(memory_space=pl.ANY)
    head_rows = jax.ShapeDtypeStruct((4 * MEM_LEN, XATT_W), _MXU)
    out_shape = tuple(jax.ShapeDtypeStruct((N_CHIPS * a.shape[0],) + a.shape[1:], _MXU) for a in arrs) + (
        jax.ShapeDtypeStruct((CONV_K, LRU_W), F32), jax.ShapeDtypeStruct((2, 256, 512), _MXU),
        jax.ShapeDtypeStruct((1, GAINS_W), F32), head_rows, head_rows)
    n_rdma = npc * 6 + 3
    return pl.pallas_call(
        body, name="gather_weights", out_shape=out_shape,
        in_specs=[vm] * 12, out_specs=(hbm, hbm, vm, vm, vm, vm, vm, vm),
        scratch_shapes=[pltpu.VMEM(a.shape, _MXU) for a in arrs] + [
            pltpu.VMEM((8, 128), F32), pltpu.VMEM((N_CHIPS * 8, 128), F32),
            pltpu.SemaphoreType.DMA((n_rdma,)), pltpu.SemaphoreType.DMA((n_rdma,)), pltpu.SemaphoreType.DMA((n + 1,))],
        compiler_params=pltpu.CompilerParams(vmem_limit_bytes=VMEM_LIMIT),
    )(win_t, wout, wkv, conv_w, w_rg, w_ig, *head_gains, mem, mem_g)


def _mem_bwd_and_pack(mem_ref, g_ref, w_ref, gn_ref, dkm_ref, dvm_ref, gg_ref, loss_ref, vectors, gw_ref, pk_ref):
    first_row = {name: (row, width) for name, row, width in SMALL_VECTORS}
    half_rows = SMALL_ROWS // 2
    pk_ref[...] = jnp.zeros(pk_ref.shape, F32)

    def rows_at(at, n):
        assert at // half_rows == (at + n - 1) // half_rows
        return at // half_rows, slice(at % half_rows, at % half_rows + n), slice(None)

    def put(name, src):
        row, width = first_row[name]
        per_row = 1 if width < 128 else src.shape[1] // 128
        for t in range(src.shape[0]):
            for r in range(per_row):
                pk_ref[rows_at(row + per_row * t + r, 1)] = src[t:t + 1, 128 * r:128 * (r + 1)]

    for name, ref in vectors.items():
        put(name, ref)
    pk_ref[rows_at(LOSS_ROW, 1)] = loss_ref[0:1, :]
    upper = lax.broadcasted_iota(jnp.int32, (HEAD, 128), 1) >= HEAD
    for h in range(2):
        for b in range(4):
            rg = gg_ref[h, HEAD * b:HEAD * (b + 1), 128 * (b // 2):128 * (b // 2 + 1)]
            ig = gg_ref[h, HEAD * b:HEAD * (b + 1), 256 + 128 * (b // 2):256 + 128 * (b // 2 + 1)]
            if b % 2:
                rg = pltpu.roll(rg, HEAD, axis=1)
            else:
                ig = pltpu.roll(ig, HEAD, axis=1)
            pk_ref[rows_at(GATES_ROW + HEAD * (4 * h + b), HEAD)] = jnp.where(upper, ig, rg)

    mem_v = mem_ref[...]
    mh = mem_v * lax.rsqrt(_row_mean(mem_v * mem_v) + EPS)
    mn = mh * g_ref[...]
    mkv = _mm(mn, w_ref[...])
    kpre = mkv[:, :XATT_W]
    gm = _group_matrix(XATT_W)
    rk = lax.rsqrt(_seg_mean(kpre * kpre, gm) + EPS)
    kn = kpre * rk
    dk = jnp.zeros((MEM_LEN, XATT_W), F32)
    dv = jnp.zeros((MEM_LEN, XATT_W), F32)
    for j in range(4):
        mj = _lane_mask(XATT_W, HEAD * j, HEAD * (j + 1))
        dk = dk + dkm_ref[:, MEM_LEN * j:MEM_LEN * (j + 1)].T * (mj * 0.125)
        dv = dv + dvm_ref[:, MEM_LEN * j:MEM_LEN * (j + 1)].T * mj
    put("xk_norm_g", _fold_heads(_col_sum(dk * kn)))
    dkn = dk * gn_ref[:, G_XK:GAINS_W]
    dkpre = rk * (dkn - kn * _seg_mean(dkn * kn, gm))
    dmkv = jnp.concatenate([dkpre, dv], axis=1)
    gw_ref[...] = _mm_tn(mn, dmkv).reshape(gw_ref.shape)
    dmn = _mm_nt(dmkv, w_ref[...])
    put("mem_norm_g", _col_sum(dmn * mh))


def layer_fwd(x, tgt, rc, rs1, rs2, ng, win_t, cw, cb, wg, brg, big, lam, gains, sinks, km, vm, og, wout):
    seq = x.shape[0]
    tm = min(ROW_TILE, seq)
    nt = seq // tm
    nb = tm // BLOCK

    def body(x_ref, t_ref, c_ref, s1_ref, s2_ref, ng_ref, win_ref, cw_ref, cb_ref, wg_ref, brg_ref, big_ref, lam_ref,
             gn_ref, sink_ref, km_ref, vm_ref, og_ref, wout_ref,
             proj_ref, ya_ref, yb_ref, yc_ref, ycat_ref, xn_ref, dout_ref, pswa_ref, pmem_ref, psink_ref, gates_ref,
             a_ref, loss_ref,
             ext_ref, b_scr, hc_ref, kp_ref, vp_ref, lacc_ref):
        i = pl.program_id(0)

        @pl.when(i == 0)
        def _():
            ext_ref[0:8, :] = jnp.zeros((8, LRU_W), F32)
            hc_ref[...] = jnp.zeros_like(hc_ref)
            kp_ref[...] = jnp.zeros_like(kp_ref)
            vp_ref[...] = jnp.zeros_like(vp_ref)
            lacc_ref[...] = jnp.zeros_like(lacc_ref)

        xv = x_ref[...]
        xn = (xv * lax.rsqrt(_row_mean(xv * xv) + EPS) * ng_ref[...]).astype(_MXU)
        xn_ref[...] = xn.astype(xn_ref.dtype)
        proj_ref[...] = _mm_nt(xn, win_ref[...])

        u = proj_ref[:, C_LRUX:C_LRUX + LRU_W]
        ext_ref[8:8 + tm, :] = u
        xc = cb_ref[...]
        for k in range(CONV_K):
            xc = xc + cw_ref[k:k + 1, :] * ext_ref[pl.ds(5 + k, tm), :]
        ext_ref[0:8, :] = u[tm - 8:tm, :]
        rg, ig, sp, a, sq = _lru_gates(xc, wg_ref, brg_ref[...], big_ref[...], lam_ref[...])
        for k, t in enumerate((xc, rg, ig, sq)):
            gates_ref[:, LRU_W * k:LRU_W * (k + 1)] = t.astype(gates_ref.dtype)
        a_ref[...] = a
        b_scr[...] = sq * (ig * xc)
        row8 = lax.broadcasted_iota(jnp.int32, (8, LRU_W), 0)

        def scan_step(g, carry):
            r0 = pl.multiple_of(g * 8, 8)
            av = a_ref[pl.ds(r0, 8), :]
            bv = b_scr[pl.ds(r0, 8), :]
            for d in (1, 2, 4):
                a_sh = jnp.where(row8 >= d, pltpu.roll(av, d, 0), 1.0)
                b_sh = jnp.where(row8 >= d, pltpu.roll(bv, d, 0), 0.0)
                bv = bv + av * b_sh
                av = av * a_sh
            hv = bv + av * carry
            ya_ref[pl.ds(r0, 8), :] = hv
            return hv[7:8, :]

        hc_ref[0:1, :] = lax.fori_loop(0, tm // 8, scan_step, hc_ref[0:1, :], unroll=True)

        gm128 = _group_matrix(KV_W)
        cv, s1v, s2v = c_ref[...], s1_ref[...], s2_ref[...]

        def head_norm_rope(t, g):
            n = t * lax.rsqrt(_seg_mean(t * t, gm128) + EPS)
            return _rope(n * g, cv, s1v, s2v)

        qs_ = (head_norm_rope(proj_ref[:, C_SQ:C_SQ + 128], gn_ref[:, G_Q:G_K]).astype(_MXU),
               head_norm_rope(proj_ref[:, C_SQ + 128:C_SQ + 256], gn_ref[:, G_Q:G_K]).astype(_MXU))
        kr = head_norm_rope(proj_ref[:, C_SK:C_SK + KV_W], gn_ref[:, G_K:G_XQ])
        sv = proj_ref[:, C_SV:C_SV + KV_W]
        ka = _place_kv(jnp.concatenate([kp_ref[...], kr], axis=0), 0.125)
        va = _place_kv(jnp.concatenate([vp_ref[...], sv], axis=0), 1.0)
        kp_ref[...] = kr[tm - BLOCK:tm, :]
        vp_ref[...] = sv[tm - BLOCK:tm, :]
        lane128 = lax.broadcasted_iota(jnp.int32, (1, 128), 1)
        for b in range(nb):
            mask = _swa_mask((i == 0) & (b == 0)) if b == 0 else _swa_mask(False)
            band = slice(BLOCK * b, BLOCK * b + 2 * BLOCK)
            blk = slice(BLOCK * b, BLOCK * (b + 1))
            psink = jnp.zeros((BLOCK, 128), F32)
            for j in range(4):
                p, pk = _swa_probs(qs_[j // 2][blk], ka[j][band], mask, sink_ref[0, j])
                pswa_ref[blk, 2 * BLOCK * j:2 * BLOCK * (j + 1)] = p.astype(pswa_ref.dtype)
                psink = jnp.where(lane128 == j, pk, psink)
            psink_ref[blk, :] = psink
            for h in range(2):
                yb_ref[blk, KV_W * h:KV_W * (h + 1)] = _mm(
                    pswa_ref[blk, 4 * BLOCK * h:4 * BLOCK * (h + 1)],
                    jnp.concatenate([va[2 * h][band], va[2 * h + 1][band]], axis=0))

        gm256 = _group_matrix(XATT_W)
        xq = proj_ref[:, C_XQ:C_XQ + XATT_W]
        qx = xq * lax.rsqrt(_seg_mean(xq * xq, gm256) + EPS) * gn_ref[:, G_XQ:G_XK]
        pm = _mem_probs(_mm_nt(qx, km_ref[...]))
        for j in range(4):
            pmem_ref[:, MEM_LEN * j:MEM_LEN * (j + 1)] = pm[j].astype(pmem_ref.dtype)
        yc = _mm(pmem_ref[...], vm_ref[...])
        yc_ref[...] = yc

        def gated(y, g, gate):
            return y * lax.rsqrt(_row_mean(y * y) + EPS) * g * (gate * _sigmoid(gate))

        ogv = og_ref[...]
        za = gated(ya_ref[...], ogv[:, :512], proj_ref[:, C_LRUG:C_LRUG + LRU_W])
        zb = gated(yb_ref[...], ogv[:, 512:768], proj_ref[:, C_SWAG:C_SWAG + SWA_W])
        zc = gated(yc, ogv[:, 768:], proj_ref[:, C_XG:C_XG + XATT_W])
        ycat_ref[:, 0:512] = za.astype(ycat_ref.dtype)
        ycat_ref[:, 512:768] = zb.astype(ycat_ref.dtype)
        ycat_ref[:, 768:1024] = zc.astype(ycat_ref.dtype)
        out = xv + _mm(ycat_ref[...], wout_ref[...])
        err = out - t_ref[...]
        dout_ref[...] = (err * (1.0 / D_MODEL)).astype(dout_ref.dtype)
        lacc_ref[...] = lacc_ref[...] + (0.5 / D_MODEL) * jnp.sum(err * err)

        @pl.when(i == nt - 1)
        def _():
            loss_ref[...] = lacc_ref[...]

    def rows(ncol):
        return pl.BlockSpec((tm, ncol), lambda i: (i, 0))

    in_specs = [rows(D_MODEL), rows(D_MODEL), rows(128), rows(128), rows(128),
                _const_spec((1, D_MODEL)), _const_spec((D_IN, D_MODEL), True), _const_spec((CONV_K, LRU_W)),
                _const_spec((1, LRU_W)), _const_spec((2, 256, 512), True), _const_spec((1, LRU_W)),
                _const_spec((1, LRU_W)), _const_spec((1, LRU_W)), _const_spec((1, GAINS_W)), pl.BlockSpec(memory_space=pltpu.SMEM),
                _const_spec((4 * MEM_LEN, XATT_W), True), _const_spec((4 * MEM_LEN, XATT_W), True),
                _const_spec((1, D_MODEL)), _const_spec((D_MODEL, D_MODEL), True)]
    out_shape = (jax.ShapeDtypeStruct((seq, D_IN), F32), jax.ShapeDtypeStruct((seq, LRU_W), F32),
                 jax.ShapeDtypeStruct((seq, SWA_W), F32), jax.ShapeDtypeStruct((seq, XATT_W), F32),
                 jax.ShapeDtypeStruct((seq, D_MODEL), _MXU), jax.ShapeDtypeStruct((seq, D_MODEL), _MXU),
                 jax.ShapeDtypeStruct((seq, D_MODEL), _MXU), jax.ShapeDtypeStruct((seq, 4 * 2 * BLOCK), _MXU),
                 jax.ShapeDtypeStruct((seq, 4 * MEM_LEN), _MXU), jax.ShapeDtypeStruct((seq, 128), F32),
                 jax.ShapeDtypeStruct((seq, 4 * LRU_W), _MXU), jax.ShapeDtypeStruct((seq, LRU_W), F32),
                 jax.ShapeDtypeStruct((8, 128), F32))
    out_specs = (rows(D_IN), rows(LRU_W), rows(SWA_W), rows(XATT_W), rows(D_MODEL), rows(D_MODEL), rows(D_MODEL),
                 rows(4 * 2 * BLOCK), rows(4 * MEM_LEN), rows(128), rows(4 * LRU_W), rows(LRU_W),
                 _const_spec((8, 128)))
    scratch = [pltpu.VMEM((tm + 8, LRU_W), F32), pltpu.VMEM((tm, LRU_W), F32),
               pltpu.VMEM((8, LRU_W), F32), pltpu.VMEM((BLOCK, KV_W), F32), pltpu.VMEM((BLOCK, KV_W), F32),
               pltpu.VMEM((8, 128), F32)]
    return pl.pallas_call(
        body, name="layer_fwd", grid=(nt,), out_shape=out_shape, in_specs=in_specs, out_specs=out_specs,
        scratch_shapes=scratch,
        compiler_params=pltpu.CompilerParams(dimension_semantics=("arbitrary",), vmem_limit_bytes=VMEM_LIMIT),
    )(x, tgt, rc, rs1, rs2, ng, win_t, cw, cb, wg, brg, big, lam, gains, sinks, km, vm, og, wout)


def weight_grads(ycat, dout, dproj, xn, mem_operands, vectors, early_at, late_at):
    seq, ncol = xn.shape
    blk = 256
    n_out, n_in = ycat.shape[1] // blk, dproj.shape[1] // blk
    assert n_out == N_CHIPS and late_at[0] >= n_out
    g_out = jax.ShapeDtypeStruct((N_CHIPS, 2, blk // 2, dout.shape[1]), F32)
    g_kv = jax.ShapeDtypeStruct((N_CHIPS, 2, D_MODEL // 8, 2 * XATT_W), F32)
    g_small = jax.ShapeDtypeStruct((2, SMALL_ROWS // 2, 128), F32)
    shape_e, scratch_e = _hosted_reduce_shapes([g_kv], g_small)
    shape_l, scratch_l = _hosted_reduce_shapes([g_out], None)
    n_mem, names = len(mem_operands), tuple(vectors)

    def body(l1_ref, r1_ref, l2_ref, r2_ref, *refs):
        mem_refs, vec_refs = refs[:n_mem], refs[n_mem:n_mem + len(names)]
        o_ref, sum_out, sum_kv, sum_sm, gout_scr, gkv_scr, pack_scr, *scratch = refs[n_mem + len(names):]
        j = pl.program_id(0)

        @pl.when(j == 0)
        def _():
            _mem_bwd_and_pack(*mem_refs, dict(zip(names, vec_refs)), gkv_scr, pack_scr)

        def reduce_stages(closing):
            _hosted_reduce(j, n_out + n_in, closing, early_at, (gkv_scr, pack_scr), (sum_kv, sum_sm),
                           scratch[:len(scratch_e)], True)
            _hosted_reduce(j, n_out + n_in, closing, late_at, (gout_scr,), (sum_out,), scratch[len(scratch_e):], False)

        reduce_stages(False)

        @pl.when(j < n_out)
        def _():
            gout_scr[j] = _mm_tn(l1_ref[...], r1_ref[...]).reshape(g_out.shape[1:])

        @pl.when(j >= n_out)
        def _():
            o_ref[...] = _mm_tn(l2_ref[...], r2_ref[...])

        reduce_stages(True)

    vm = pl.BlockSpec(memory_space=pltpu.VMEM)
    hbm = pl.BlockSpec(memory_space=pl.ANY)
    return pl.pallas_call(
        body, name="weight_grads", grid=(n_out + n_in,),
        out_shape=(jax.ShapeDtypeStruct((dproj.shape[1], ncol), F32), *shape_l, *shape_e),
        in_specs=[pl.BlockSpec((seq, blk), lambda j: (0, jnp.minimum(j, n_out - 1))), _const_spec(dout.shape, True),
                  pl.BlockSpec((seq, blk), lambda j: (0, jnp.maximum(j - n_out, 0))), _const_spec(xn.shape, True)]
        + [vm] * (n_mem + len(names)),
        out_specs=(pl.BlockSpec((blk, ncol), lambda j: (jnp.maximum(j - n_out, 0), 0)), hbm, hbm, hbm),
        scratch_shapes=[pltpu.VMEM(s.shape, F32) for s in (g_out, g_kv, g_small)] + scratch_e + scratch_l,
        compiler_params=pltpu.CompilerParams(dimension_semantics=("arbitrary",), vmem_limit_bytes=VMEM_LIMIT),
    )(ycat, dout, dproj, xn, *mem_operands, *vectors.values())


def layer_bwd(x, dout, proj, ya, yb, yc, pswa, pmem, psink, gates, a_all, rc, rs1, rs2, ng, win_t, cw, wg, lam, gains,
              km, vm, og, wout):
    seq = x.shape[0]
    tm = min(ROW_TILE, seq)
    nt = seq // tm
    nb = tm // BLOCK

    def body(x_ref, dout_ref, proj_ref, ya_ref, yb_ref, yc_ref, pswa_ref, pmem_ref, psink_ref, gates_ref, a_ref,
             c_ref, s1_ref, s2_ref,
             yah_ref, kvh_ref, ch_ref, s1h_ref, s2h_ref,
             ng_ref, win_ref, cw_ref, wg_ref, lam_ref, gn_ref, km_ref, vm_ref, og_ref, wout_ref,
             gx_ref, dproj_ref, gwg_ref, dkm_ref, dvm_ref, gng_ref, gog_ref, gcb_ref, gbrg_ref, gbig_ref, glam_ref,
             gcw_ref, gqn_ref, gkn_ref, gxqn_ref, gsink_ref,
             hext_ref, aext_ref, an_scr, dh_scr, g_scr, dxc_ext, gcar_ref, dkcar_ref, dvcar_ref):
        i = pl.program_id(0)
        tile = nt - 1 - i
        first_tile = tile == 0

        @pl.when(i == 0)
        def _():
            for r in (gwg_ref, dkm_ref, dvm_ref, gng_ref, gog_ref, gcb_ref, gbrg_ref, gbig_ref, glam_ref, gcw_ref,
                      gqn_ref, gkn_ref, gxqn_ref, gsink_ref, gcar_ref, dkcar_ref, dvcar_ref):
                r[...] = jnp.zeros_like(r)
            dxc_ext[tm:tm + 8, :] = jnp.zeros((8, LRU_W), F32)
            aext_ref[tm:tm + 8, :] = jnp.zeros((8, LRU_W), F32)

        xv = x_ref[...]
        dov = dout_ref[...]
        dz = _mm_nt(dov, wout_ref[...])
        ogv = og_ref[...]

        def group_bwd(y, gate, g, dzg):
            r = lax.rsqrt(_row_mean(y * y) + EPS)
            n = y * r
            sg = _sigmoid(gate)
            dgate = dzg * (n * g) * (sg * (1.0 + gate * (1.0 - sg)))
            dng = dzg * (gate * sg)
            dn = dng * g
            return r * (dn - n * _row_mean(dn * n)), dgate, _col_sum(dng * n)

        dya, dga, goa = group_bwd(ya_ref[...], proj_ref[:, C_LRUG:C_LRUG + LRU_W], ogv[:, :512], dz[:, :512])
        dyb, dgb, gob = group_bwd(yb_ref[...], proj_ref[:, C_SWAG:C_SWAG + SWA_W], ogv[:, 512:768], dz[:, 512:768])
        dyc, dgc, goc = group_bwd(yc_ref[...], proj_ref[:, C_XG:C_XG + XATT_W], ogv[:, 768:], dz[:, 768:])
        gog_ref[...] += jnp.concatenate([goa, gob, goc], axis=1)
        dproj_ref[:, C_LRUG:C_LRUG + LRU_W] = dga.astype(dproj_ref.dtype)
        dproj_ref[:, C_SWAG:C_SWAG + SWA_W] = dgb.astype(dproj_ref.dtype)
        dproj_ref[:, C_XG:C_XG + XATT_W] = dgc.astype(dproj_ref.dtype)

        gm256 = _group_matrix(XATT_W)
        xq = proj_ref[:, C_XQ:C_XQ + XATT_W]
        rq = lax.rsqrt(_seg_mean(xq * xq, gm256) + EPS)
        qn = xq * rq
        qx = qn * gn_ref[:, G_XQ:G_XK]
        qxb = qx.astype(_MXU)
        dycb = dyc.astype(_MXU)
        dp_all = _mm_nt(dycb, vm_ref[...])
        dsm = []
        for j in range(4):
            pj = pmem_ref[:, MEM_LEN * j:MEM_LEN * (j + 1)].astype(F32)
            dp = dp_all[:, MEM_LEN * j:MEM_LEN * (j + 1)]
            dsm.append((pj * (dp - jnp.sum(pj * dp, axis=-1, keepdims=True))).astype(_MXU))
        ds_all = jnp.concatenate(dsm, axis=1)
        dvm_ref[...] += _mm_tn(dycb, pmem_ref[...])
        dkm_ref[...] += _mm_tn(qxb, ds_all)
        dqx = _mm(ds_all, km_ref[...])
        gxqn_ref[...] += _col_sum(dqx * qn)
        dqn = dqx * gn_ref[:, G_XQ:G_XK]
        dproj_ref[:, C_XQ:C_XQ + XATT_W] = (rq * (dqn - qn * _seg_mean(dqn * qn, gm256))).astype(dproj_ref.dtype)

        gm128 = _group_matrix(KV_W)
        cv, s1v, s2v = c_ref[...], s1_ref[...], s2_ref[...]

        def head_norm(t):
            r = lax.rsqrt(_seg_mean(t * t, gm128) + EPS)
            return t * r, r

        qn_, qr_ = zip(head_norm(proj_ref[:, C_SQ:C_SQ + 128]), head_norm(proj_ref[:, C_SQ + 128:C_SQ + 256]))
        qrope = [_rope(qn_[h] * gn_ref[:, G_Q:G_K], cv, s1v, s2v).astype(_MXU) for h in range(2)]
        kn, krr = head_norm(proj_ref[:, C_SK:C_SK + KV_W])
        kr = _rope(kn * gn_ref[:, G_K:G_XQ], cv, s1v, s2v)
        khn, _ = head_norm(kvh_ref[:, 0:KV_W])
        khr = _rope(khn * gn_ref[:, G_K:G_XQ], ch_ref[...], s1h_ref[...], s2h_ref[...])
        ka = _place_kv(jnp.concatenate([khr, kr], axis=0), 0.125)
        va = _place_kv(jnp.concatenate([kvh_ref[:, KV_W:2 * KV_W], proj_ref[:, C_SV:C_SV + KV_W]], axis=0), 1.0)
        lane128 = lax.broadcasted_iota(jnp.int32, (1, 128), 1)
        gsink = jnp.zeros((1, 128), F32)
        dk_band, dv_band, dq_blk = [], [], []
        for b in range(nb):
            band = slice(BLOCK * b, BLOCK * b + 2 * BLOCK)
            blk = slice(BLOCK * b, BLOCK * (b + 1))
            dka, dva, dsb = [], [], []
            deltas = jnp.zeros((BLOCK, 128), F32)
            for j in range(4):
                qh = qrope[j // 2][blk]
                doh = dyb[blk, KV_W * (j // 2):KV_W * (j // 2 + 1)].astype(_MXU)
                pb = pswa_ref[blk, 2 * BLOCK * j:2 * BLOCK * (j + 1)]
                p = pb.astype(F32)
                dp = _mm_nt(doh, va[j][band])
                delta = jnp.sum(p * dp, axis=-1, keepdims=True)
                ds = (p * (dp - delta)).astype(_MXU)
                deltas = jnp.where(lane128 == j, delta, deltas)
                dva.append(_mm_tn(pb, doh))
                dka.append(_mm_tn(ds, qh))
                dsb.append(ds)
            gsink = gsink - _col_sum(psink_ref[blk, :] * deltas)
            dk_band.append(_unplace_kv(dka) * 0.125)
            dv_band.append(_unplace_kv(dva))
            dq_blk.append([_mm(jnp.concatenate(dsb[2 * h:2 * h + 2], axis=1),
                               jnp.concatenate([ka[2 * h][band], ka[2 * h + 1][band]], axis=0)) for h in range(2)])
        gsink_ref[...] += gsink
        dk_rows = [dk_band[b][BLOCK:] + (dk_band[b + 1][:BLOCK] if b + 1 < nb else dkcar_ref[...]) for b in range(nb)]
        dv_rows = [dv_band[b][BLOCK:] + (dv_band[b + 1][:BLOCK] if b + 1 < nb else dvcar_ref[...]) for b in range(nb)]
        dkcar_ref[...] = dk_band[0][:BLOCK]
        dvcar_ref[...] = dv_band[0][:BLOCK]
        dkg = _rope_bwd(jnp.concatenate(dk_rows, axis=0), cv, s1v, s2v)
        gkn = _col_sum(dkg * kn)
        dkn = dkg * gn_ref[:, G_K:G_XQ]
        dproj_ref[:, C_SK:C_SK + KV_W] = (krr * (dkn - kn * _seg_mean(dkn * kn, gm128))).astype(dproj_ref.dtype)
        dproj_ref[:, C_SV:C_SV + KV_W] = jnp.concatenate(dv_rows, axis=0).astype(dproj_ref.dtype)
        gqn = jnp.zeros((1, 128), F32)
        for h in range(2):
            dqg = _rope_bwd(jnp.concatenate([dq_blk[b][h] for b in range(nb)], axis=0), cv, s1v, s2v)
            gqn = gqn + _col_sum(dqg * qn_[h])
            dqn_ = dqg * gn_ref[:, G_Q:G_K]
            dproj_ref[:, C_SQ + 128 * h:C_SQ + 128 * (h + 1)] = (
                qr_[h] * (dqn_ - qn_[h] * _seg_mean(dqn_ * qn_[h], gm128))).astype(dproj_ref.dtype)
        gqn_ref[...] += gqn
        gkn_ref[...] += gkn

        u = proj_ref[:, C_LRUX:C_LRUX + LRU_W]
        xc, rg, ig, sq = (gates_ref[:, LRU_W * k:LRU_W * (k + 1)].astype(F32) for k in range(4))
        a = a_ref[...]
        sp = _softplus(-lam_ref[...])
        hext_ref[0:8, :] = jnp.where(first_tile, 0.0, yah_ref[...])
        hext_ref[8:8 + tm, :] = ya_ref[...]
        hprev = hext_ref[pl.ds(7, tm), :]
        aext_ref[0:tm, :] = a
        an_scr[...] = aext_ref[pl.ds(1, tm), :]
        dh_scr[...] = dya
        dh_scr[tm - 1:tm, :] = dh_scr[tm - 1:tm, :] + gcar_ref[0:1, :]
        row8 = lax.broadcasted_iota(jnp.int32, (8, LRU_W), 0)

        def scan_step(gi, carry):
            r0 = pl.multiple_of((tm // 8 - 1 - gi) * 8, 8)
            av = an_scr[pl.ds(r0, 8), :]
            bv = dh_scr[pl.ds(r0, 8), :]
            for d in (1, 2, 4):
                a_sh = jnp.where(row8 < 8 - d, pltpu.roll(av, 8 - d, 0), 1.0)
                b_sh = jnp.where(row8 < 8 - d, pltpu.roll(bv, 8 - d, 0), 0.0)
                bv = bv + av * b_sh
                av = av * a_sh
            gv = bv + av * carry
            g_scr[pl.ds(r0, 8), :] = gv
            return gv[0:1, :]

        g0 = lax.fori_loop(0, tm // 8, scan_step, jnp.zeros((1, LRU_W), F32), unroll=True)
        gcar_ref[0:1, :] = a[0:1, :] * g0
        gv = g_scr[...]
        da = gv * hprev
        dig = gv * sq * xc
        dxc = gv * sq * ig
        dla = da * a - gv * (ig * xc) * ((a * a) / sq)
        drg = dla * ((-LRU_C) * sp)
        glam_ref[...] += _col_sum(dla * rg)
        dpr = drg * rg * (1.0 - rg)
        dpi = dig * ig * (1.0 - ig)
        gbrg_ref[...] += _col_sum(dpr)
        gbig_ref[...] += _col_sum(dpi)
        dpre0 = jnp.concatenate([dpr[:, :256], dpi[:, :256]], axis=1).astype(_MXU)
        dpre1 = jnp.concatenate([dpr[:, 256:], dpi[:, 256:]], axis=1).astype(_MXU)
        gwg_ref[0] += _mm_tn(xc[:, :256], dpre0)
        gwg_ref[1] += _mm_tn(xc[:, 256:], dpre1)
        dxc = dxc + jnp.concatenate([_mm_nt(dpre0, wg_ref[0]), _mm_nt(dpre1, wg_ref[1])], axis=1)
        gcb_ref[...] += _col_sum(dxc)
        dxc_ext[0:tm, :] = dxc
        du = jnp.zeros((tm, LRU_W), F32)
        for k in range(CONV_K):
            later = dxc_ext[pl.ds(3 - k, tm), :]
            gcw_ref[k:k + 1, :] += _col_sum(later * u)
            du = du + cw_ref[k:k + 1, :] * later
        dxc_ext[tm:tm + 8, :] = dxc[0:8, :]
        dproj_ref[:, C_LRUX:C_LRUX + LRU_W] = du.astype(dproj_ref.dtype)

        dxn = _mm(dproj_ref[...], win_ref[...])
        rx = lax.rsqrt(_row_mean(xv * xv) + EPS)
        xh = xv * rx
        gng_ref[...] += _col_sum(dxn * xh)
        dxh = dxn * ng_ref[...]
        gx_ref[...] = dov.astype(F32) + rx * (dxh - xh * _row_mean(dxh * xh))

        @pl.when(i == nt - 1)
        def _():
            glam_ref[...] = glam_ref[...] * (LRU_C * _sigmoid(-lam_ref[...]))
            for r in (gqn_ref, gkn_ref, gxqn_ref):
                r[...] = _fold_heads(r[...])

    def rows(ncol, arr_cols_block=0):
        return pl.BlockSpec((tm, ncol), lambda i: (nt - 1 - i, arr_cols_block))

    def halo(nrow, ncol, colblk=0):
        per = tm // nrow
        return pl.BlockSpec((nrow, ncol), lambda i: (jnp.maximum((nt - 1 - i) * per - 1, 0), colblk))

    in_specs = [rows(D_MODEL), rows(D_MODEL), rows(D_IN), rows(LRU_W), rows(SWA_W), rows(XATT_W),
                rows(4 * 2 * BLOCK), rows(4 * MEM_LEN), rows(128), rows(4 * LRU_W), rows(LRU_W),
                rows(128), rows(128), rows(128),
                halo(8, LRU_W), halo(BLOCK, 2 * KV_W, C_SK // (2 * KV_W)),
                halo(BLOCK, 128), halo(BLOCK, 128), halo(BLOCK, 128),
                _const_spec((1, D_MODEL)), _const_spec((D_IN, D_MODEL), True), _const_spec((CONV_K, LRU_W)),
                _const_spec((2, 256, 512), True), _const_spec((1, LRU_W)), _const_spec((1, GAINS_W)),
                _const_spec((4 * MEM_LEN, XATT_W), True), _const_spec((4 * MEM_LEN, XATT_W), True),
                _const_spec((1, D_MODEL)), _const_spec((D_MODEL, D_MODEL), True)]
    small = [(2, 256, 512), (XATT_W, 4 * MEM_LEN), (XATT_W, 4 * MEM_LEN), (1, D_MODEL), (1, D_MODEL), (1, LRU_W), (1, LRU_W),
             (1, LRU_W), (1, LRU_W), (CONV_K, LRU_W), (1, 128), (1, 128), (1, XATT_W), (1, 128)]
    out_shape = (jax.ShapeDtypeStruct((seq, D_MODEL), F32), jax.ShapeDtypeStruct((seq, D_IN), _MXU)) + tuple(
        jax.ShapeDtypeStruct(s, F32) for s in small)
    out_specs = (rows(D_MODEL), rows(D_IN)) + tuple(_const_spec(s) for s in small)
    scratch = [pltpu.VMEM((tm + 8, LRU_W), F32), pltpu.VMEM((tm + 8, LRU_W), F32),
               pltpu.VMEM((tm, LRU_W), F32), pltpu.VMEM((tm, LRU_W), F32), pltpu.VMEM((tm, LRU_W), F32),
               pltpu.VMEM((tm + 8, LRU_W), F32),
               pltpu.VMEM((8, LRU_W), F32), pltpu.VMEM((BLOCK, KV_W), F32), pltpu.VMEM((BLOCK, KV_W), F32)]
    return pl.pallas_call(
        body, name="layer_bwd", grid=(nt,), out_shape=out_shape, in_specs=in_specs, out_specs=out_specs,
        scratch_shapes=scratch,
        compiler_params=pltpu.CompilerParams(dimension_semantics=("arbitrary",), vmem_limit_bytes=VMEM_LIMIT),
    )(x, dout, proj, ya, yb, yc, pswa, pmem, psink, gates, a_all, rc, rs1, rs2, ya, proj, rc, rs1, rs2,
      ng, win_t, cw, wg, lam, gains, km, vm, og, wout)


def _reduce_protocol(big, sm, outs, osm, r1, r1s, wire, r2, r2s, wire2, ps, own, send, recv, lsem):
    nbig = len(big)
    x, y, c = lax.axis_index("x"), lax.axis_index("y"), lax.axis_index("c")
    sibling = (x, y, 1 - c)
    near, far, diag = _partners(x, y, c)
    me, near_id, far_id, diag_id = _chip_of(x, y), _chip_of(*near), _chip_of(*far), _chip_of(*diag)

    def copy(k, src, dst, to):
        return pltpu.make_async_remote_copy(src_ref=src, dst_ref=dst, send_sem=send.at[k], recv_sem=recv.at[k],
                                            device_id=to, device_id_type=MESH)

    def sent(stage, a):
        if a == nbig:
            src, dst, to = ((sm.at[1 - c], r1s, sibling), (r1s, r2s.at[0], (*near, c)), (ps, r2s.at[1], (*far, c)),
                            (osm.at[c], osm.at[c], sibling))[stage]
            return [copy(5 * nbig + stage, src, dst, to)]
        if stage == 0:
            return [copy(5 * a, big[a].at[:, 1 - c], r1[a], sibling)]
        if stage == 1:
            return [copy(5 * a + 1, wire[a].at[near_id], r2[a].at[0], (*near, c)),
                    copy(5 * a + 2, wire[a].at[diag_id], r2[a].at[1], (*near, c))]
        if stage == 2:
            return [copy(5 * a + 3, wire2[a], r2[a].at[2], (*far, c))]
        return [copy(5 * a + 4, outs[a].at[c], outs[a].at[c], sibling)]

    arrays = range(nbig + (sm is not None))

    def start(stage, a):
        for cp in sent(stage, a):
            cp.start()

    def arrived(k, ref):
        copy(k, ref, ref, sibling).wait_recv()

    def loads():
        return [pltpu.make_async_copy(big[a].at[:, c], own[a], lsem.at[a]) for a in range(nbig)]

    def stage0():
        for a in arrays:
            start(0, a)
        for cp in loads():
            cp.start()

    def stage1():
        for a in range(nbig):
            loads()[a].wait()
            arrived(5 * a, r1[a])
            for k in range(N_CHIPS):
                r1[a][k] = own[a][k] + r1[a][k]
                wire[a][k] = r1[a][k].astype(wire[a].dtype)
            start(1, a)
        if sm is not None:
            arrived(5 * nbig, r1s)
            r1s[...] = sm[c] + r1s[...]
            start(1, nbig)

    def stage2():
        for a in range(nbig):
            arrived(5 * a + 1, r2[a].at[0])
            arrived(5 * a + 2, r2[a].at[1])
            r1[a][me] = r1[a][me] + r2[a][0].astype(F32)
            wire2[a][...] = (r1[a][far_id] + r2[a][1].astype(F32)).astype(wire2[a].dtype)
            start(2, a)
        if sm is not None:
            arrived(5 * nbig + 1, r2s.at[0])
            ps[...] = r1s[...] + r2s[0]
            start(2, nbig)

    def stage3():
        for a in range(nbig):
            arrived(5 * a + 3, r2[a].at[2])
            outs[a][c] = r1[a][me] + r2[a][2].astype(F32)
            start(3, a)
        if sm is not None:
            arrived(5 * nbig + 2, r2s.at[1])
            osm[c] = ps[...] + r2s[1]
            start(3, nbig)

    def stage4():
        for a in range(nbig):
            arrived(5 * a + 4, outs[a].at[1 - c])
        if sm is not None:
            arrived(5 * nbig + 3, osm.at[1 - c])
        for stage in range(4):
            for a in arrays:
                for cp in sent(stage, a):
                    cp.wait_send()

    return [stage0, stage1, stage2, stage3, stage4]


def _reduce_buffers(bigs, g_small):
    half = [b.shape[2:] for b in bigs]
    sm_half = None if g_small is None else g_small.shape[1:]
    out_shape = [jax.ShapeDtypeStruct((2,) + h, F32) for h in half]
    small = lambda lead: [] if g_small is None else [pltpu.VMEM(lead + sm_half, F32)]
    if g_small is not None:
        out_shape.append(jax.ShapeDtypeStruct(g_small.shape, F32))
    n_sem = 5 * len(bigs) + 4
    scratch = ([pltpu.VMEM((N_CHIPS,) + h, F32) for h in half] + small(())
               + [pltpu.VMEM((N_CHIPS,) + h, _WIRE) for h in half]
               + [pltpu.VMEM((3,) + h, _WIRE) for h in half] + small((2,))
               + [pltpu.VMEM(h, _WIRE) for h in half] + small(())
               + [pltpu.VMEM((N_CHIPS,) + h, F32) for h in half]
               + [pltpu.SemaphoreType.DMA((n_sem,)), pltpu.SemaphoreType.DMA((n_sem,)),
                  pltpu.SemaphoreType.DMA((len(bigs),))])
    return out_shape, scratch


def _split_reduce_refs(refs, nbig, has_small):
    it = iter(refs)
    take = lambda n: [next(it) for _ in range(n)]
    one = lambda: next(it) if has_small else None
    big, sm = take(nbig), one()
    outs, osm = take(nbig), one()
    r1, r1s, wire, r2, r2s, wire2, ps, own = take(nbig), one(), take(nbig), take(nbig), one(), take(nbig), one(), take(nbig)
    send, recv, lsem = take(3)
    return big, sm, outs, osm, r1, r1s, wire, r2, r2s, wire2, ps, own, send, recv, lsem


def _hosted_reduce_shapes(bigs, g_small):
    red_shape, scratch = _reduce_buffers(bigs, g_small)
    nres = len(red_shape)
    return red_shape, [pltpu.VMEM(r.shape, r.dtype) for r in red_shape] + scratch + [pltpu.SemaphoreType.DMA((nres,))]


def _hosted_reduce(step, n_steps, closing, stage_at, operands, results, scratch, has_small):
    nres = len(results)
    sums, rest, fsem = scratch[:nres], scratch[nres:-1], scratch[-1]
    refs = tuple(operands) + tuple(sums) + tuple(rest)

    def to_results():
        out = [pltpu.make_async_copy(sums[k], results[k], fsem.at[k]) for k in range(nres)]
        for cp in out:
            cp.start()
        for cp in out:
            cp.wait()

    stages = _reduce_protocol(*_split_reduce_refs(refs, nres - has_small, has_small))

    def last_stage():
        stages[-1]()
        to_results()

    for at, stage in zip(stage_at, stages[:-1] + [last_stage]):
        if closing == (at == n_steps):
            pl.when(step == min(at, n_steps - 1))(stage)


def reduce_grads(big, name, parts):
    chips, halves, rows_, cols = big.shape
    sub = jax.ShapeDtypeStruct((chips, halves, rows_ // parts, cols), big.dtype)

    def body(b_ref, o_ref, *scratch):
        refs = [b_ref.at[:, :, s] for s in range(parts)] + [o_ref.at[:, s] for s in range(parts)] + list(scratch)
        for stage in _reduce_protocol(*_split_reduce_refs(refs, parts, False)):
            stage()

    _, scratch = _reduce_buffers([sub] * parts, None)
    return pl.pallas_call(
        body, name=name, out_shape=jax.ShapeDtypeStruct((halves, parts, rows_ // parts, cols), F32),
        in_specs=[pl.BlockSpec(memory_space=pl.ANY)], out_specs=pl.BlockSpec(memory_space=pltpu.VMEM),
        scratch_shapes=scratch, compiler_params=pltpu.CompilerParams(vmem_limit_bytes=VMEM_LIMIT),
    )(big.reshape(chips, halves, parts, rows_ // parts, cols))


def adamw(items, g_pack, ws, ms, vs):
    blocks = []
    for k, (w, _, _, _) in enumerate(items):
        rows_, cols = w.shape
        tr = max(t for t in range(8, rows_ + 1, 8) if rows_ % t == 0 and t * cols * 4 <= ADAM_BLOCK_BYTES)
        blocks += [(k, r, tr) for r in range(0, rows_, tr)]
    nin, n = 4 * len(items), len(ws)

    def body(*refs):
        mats_in, small_in = refs[:nin], refs[nin:nin + 1 + 3 * n]
        n_out = nin + 4 * n + 1
        outs, scratch = refs[nin + 1 + 3 * n:nin + 1 + 3 * n + n_out], refs[nin + 1 + 3 * n + n_out:]
        buf, (lsem, ssem) = scratch[:nin], scratch[nin:]
        loads = [[pltpu.make_async_copy(mats_in[4 * k + q].at[pl.ds(r, tr)], buf[4 * k + q].at[pl.ds(r, tr)],
                                        lsem.at[4 * c + q]) for q in range(4)] for c, (k, r, tr) in enumerate(blocks)]
        for cps in loads:
            for cp in cps:
                cp.start()
        _adamw_small(small_in[0], *(small_in[1 + k * n:1 + (k + 1) * n] for k in range(3)),
                     *(outs[nin + k * n:nin + (k + 1) * n] for k in range(4)), outs[-1])
        stores = []
        for c, (k, r, tr) in enumerate(blocks):
            for cp in loads[c]:
                cp.wait()
            w_ref, g_ref, m_ref, v_ref = (buf[4 * k + q].at[pl.ds(r, tr)] for q in range(4))
            w_ref[...], m_ref[...], v_ref[...] = _adam_update(w_ref[...], g_ref[...], m_ref[...], v_ref[...])
            for q, src in enumerate((g_ref, w_ref, m_ref, v_ref)):
                stores.append(pltpu.make_async_copy(src, outs[4 * k + q].at[pl.ds(r, tr)], ssem.at[4 * c + q]))
                stores[-1].start()
        for cp in stores:
            cp.wait()

    shapes = [jax.ShapeDtypeStruct(w.shape, F32) for w, _, _, _ in items for _ in range(4)]
    vm = pl.BlockSpec(memory_space=pltpu.VMEM)
    hbm = pl.BlockSpec(memory_space=pl.ANY)
    like = [jax.ShapeDtypeStruct(w.shape, F32) for w in ws]
    res = pl.pallas_call(
        body, name="adamw", out_shape=(*shapes, *like * 4, jax.ShapeDtypeStruct((1, 1), F32)),
        in_specs=[hbm] * nin + [vm] * (1 + 3 * n), out_specs=(*[hbm] * nin, *[vm] * (4 * n + 1)),
        scratch_shapes=[pltpu.VMEM(s.shape, F32) for s in shapes] + [pltpu.SemaphoreType.DMA((4 * len(blocks),))] * 2,
        compiler_params=pltpu.CompilerParams(vmem_limit_bytes=VMEM_LIMIT),
    )(*[a for item in items for a in item], g_pack, *ws, *ms, *vs)
    return [res[4 * k:4 * k + 4] for k in range(len(items))], res[nin:]


def _adam_update(w, g, m, v):
    nm = ADAM_B1 * m + (1.0 - ADAM_B1) * g
    nv = ADAM_B2 * v + (1.0 - ADAM_B2) * (g * g)
    m_hat = nm / (1.0 - ADAM_B1 ** ADAM_STEP)
    v_hat = nv / (1.0 - ADAM_B2 ** ADAM_STEP)
    return (-ADAM_LR) * (m_hat / (jnp.sqrt(v_hat) + ADAM_EPS) + ADAM_WD * w), nm, nv


def _adamw_small(pk, w_refs, m_refs, v_refs, g_out, d_out, nm_out, nv_out, loss_ref):
    nvec = len(SMALL_VECTORS)
    loss_ref[...] = pk[LOSS_ROW:LOSS_ROW + 1, 0:1]
    chip = 2 * lax.axis_index("x") + lax.axis_index("y")
    for k, (name, row, width) in enumerate(SMALL_VECTORS):
        if name == "conv_w":
            g = jnp.concatenate([pk[pl.ds(row + 4 * t + chip, 1), :] for t in range(CONV_K)], axis=0)[None]
        elif width >= 128:
            g = jnp.concatenate([pk[row + r:row + r + 1, :] for r in range(width // 128)], axis=1)
        else:
            g = pk[row:row + 1, 0:width]
        g_out[k][...] = g
        d_out[k][...], nm_out[k][...], nv_out[k][...] = _adam_update(w_refs[k][...], g, m_refs[k][...], v_refs[k][...])
    for k in range(nvec, nvec + len(SMALL_MATRICES)):
        for b in range(LRU_BLOCKS):
            rows_ = pk[GATES_ROW + HEAD * b:GATES_ROW + HEAD * (b + 1), :]
            g = (pltpu.roll(rows_, HEAD, axis=1) if k > nvec else rows_)[:, 0:HEAD]
            g_out[k][0, b] = g
            d_out[k][0, b], nm_out[k][0, b], nv_out[k][0, b] = _adam_update(
                w_refs[k][0, b], g, m_refs[k][0, b], v_refs[k][0, b])


SMALL_VECTORS = (("norm_g", 512, 1024), ("mem_norm_g", 520, 1024), ("conv_w", 528, 512), ("conv_b", 544, 512),
                 ("b_rg", 548, 512), ("b_ig", 552, 512), ("lru_lambda", 556, 512), ("q_norm_g", 560, 64),
                 ("k_norm_g", 561, 64), ("sinks", 562, 4), ("xq_norm_g", 563, 64), ("xk_norm_g", 564, 64),
                 ("out_norm_g", 565, 1024))
LOSS_ROW = 573
SMALL_MATRICES = ("w_rg", "w_ig")
GATES_ROW = 0
SMALL_ROWS = 640


def _rope_tables(seq):
    pos = np.arange(seq, dtype=np.float32)
    inv_freq = (np.float32(ROPE_THETA) ** (-(np.arange(0, ROPE_DIM, 2, dtype=np.float32) / np.float32(ROPE_DIM)))
                ).astype(np.float32)
    ang = (pos[:, None] * inv_freq[None, :]).astype(np.float32)
    cos, sin = np.cos(ang).astype(np.float32), np.sin(ang).astype(np.float32)
    z = lambda n: np.zeros((seq, n), np.float32)
    c64 = np.concatenate([cos, cos, np.ones((seq, HEAD - ROPE_DIM), np.float32)], axis=1)
    s1_64 = np.concatenate([-sin, z(HEAD - 8)], axis=1)
    s2_64 = np.concatenate([z(8), sin, z(HEAD - ROPE_DIM)], axis=1)
    return tuple(jnp.asarray(np.concatenate([t, t], axis=1)) for t in (c64, s1_64, s2_64))


def kernel(x, mem, norm_g, mem_norm_g, w_in, conv_w, conv_b, w_rg, b_rg, w_ig, b_ig, lru_lambda, q_norm_g, k_norm_g, sinks, w_mem_kv, xq_norm_g, xk_norm_g, out_norm_g, w_out, loss_target, m_norm_g, m_mem_norm_g, m_w_in, m_conv_w, m_conv_b, m_w_rg, m_b_rg, m_w_ig, m_b_ig, m_lru_lambda, m_q_norm_g, m_k_norm_g, m_sinks, m_w_mem_kv, m_xq_norm_g, m_xk_norm_g, m_out_norm_g, m_w_out, v_norm_g, v_mem_norm_g, v_w_in, v_conv_w, v_conv_b, v_w_rg, v_b_rg, v_w_ig, v_b_ig, v_lru_lambda, v_q_norm_g, v_k_norm_g, v_sinks, v_w_mem_kv, v_xq_norm_g, v_xk_norm_g, v_out_norm_g, v_w_out):
    seq = x.shape[1]
    xs, tgt, mems = x[0], loss_target[0], mem[0]

    win_t, wout, wkv, cw, wg, gains, km, vm = gather_weights(
        w_in[0].T, w_out[0], w_mem_kv[0], conv_w, w_rg, w_ig, (q_norm_g, k_norm_g, xq_norm_g, xk_norm_g), mems,
        mem_norm_g)
    rc, rs1, rs2 = _rope_tables(seq)
    proj, ya, yb, yc, ycat, xn, dout, pswa, pmem, psink, gates, a_all, loss8 = layer_fwd(
        xs, tgt, rc, rs1, rs2, norm_g, win_t, cw, conv_b, wg, b_rg, b_ig, lru_lambda, gains, sinks, km, vm,
        out_norm_g, wout)
    (gx, dproj, g_wg, dkm, dvm, g_ng, g_og, g_cb, g_brg, g_big, g_lam, g_cw, g_qn, g_kn, g_xqn, g_sink) = layer_bwd(
        xs, dout, proj, ya, yb, yc, pswa, pmem, psink, gates, a_all, rc, rs1, rs2, norm_g, win_t, cw, wg, lru_lambda,
        gains, km, vm, out_norm_g, wout)
    g_win_t, r_out, r_kv, r_small = weight_grads(
        ycat, dout, dproj, xn, (mems, mem_norm_g, wkv, gains, dkm, dvm, g_wg, loss8), dict(
            norm_g=g_ng, conv_w=g_cw, conv_b=g_cb, b_rg=g_brg, b_ig=g_big, lru_lambda=g_lam, q_norm_g=g_qn,
            k_norm_g=g_kn, sinks=g_sink, xq_norm_g=g_xqn, out_norm_g=g_og), (0, 1, 4, 7, 8), (4, 5, 9, 11, 13))
    r_in = reduce_grads(g_win_t.reshape(N_CHIPS, 2, D_IN // 8, D_MODEL), "reduce_w_in", 6)

    r_small = r_small.reshape(SMALL_ROWS, 128)
    grads = {}
    weights = dict(norm_g=norm_g, mem_norm_g=mem_norm_g, w_in=w_in, conv_w=conv_w, conv_b=conv_b, w_rg=w_rg, b_rg=b_rg,
                   w_ig=w_ig, b_ig=b_ig, lru_lambda=lru_lambda, q_norm_g=q_norm_g, k_norm_g=k_norm_g, sinks=sinks,
                   w_mem_kv=w_mem_kv, xq_norm_g=xq_norm_g, xk_norm_g=xk_norm_g, out_norm_g=out_norm_g, w_out=w_out)
    ms = dict(norm_g=m_norm_g, mem_norm_g=m_mem_norm_g, w_in=m_w_in, conv_w=m_conv_w, conv_b=m_conv_b, w_rg=m_w_rg,
              b_rg=m_b_rg, w_ig=m_w_ig, b_ig=m_b_ig, lru_lambda=m_lru_lambda, q_norm_g=m_q_norm_g, k_norm_g=m_k_norm_g,
              sinks=m_sinks, w_mem_kv=m_w_mem_kv, xq_norm_g=m_xq_norm_g, xk_norm_g=m_xk_norm_g,
              out_norm_g=m_out_norm_g, w_out=m_w_out)
    vs = dict(norm_g=v_norm_g, mem_norm_g=v_mem_norm_g, w_in=v_w_in, conv_w=v_conv_w, conv_b=v_conv_b, w_rg=v_w_rg,
              b_rg=v_b_rg, w_ig=v_w_ig, b_ig=v_b_ig, lru_lambda=v_lru_lambda, q_norm_g=v_q_norm_g, k_norm_g=v_k_norm_g,
              sinks=v_sinks, w_mem_kv=v_w_mem_kv, xq_norm_g=v_xq_norm_g, xk_norm_g=v_xk_norm_g,
              out_norm_g=v_out_norm_g, w_out=v_w_out)

    delta, new_m, new_v = {}, {}, {}
    small_names = [n for n, _, _ in SMALL_VECTORS] + list(SMALL_MATRICES)
    (res_in, res_out, res_kv), res = adamw(
        [(w_in[0].T, r_in.reshape(D_IN // 4, D_MODEL), m_w_in[0].T, v_w_in[0].T),
         (w_out[0], r_out.reshape(D_MODEL // 4, D_MODEL), m_w_out[0], v_w_out[0]),
         (w_mem_kv[0], r_kv.reshape(D_MODEL // 4, 2 * XATT_W), m_w_mem_kv[0], v_w_mem_kv[0])],
        r_small, [weights[n] for n in small_names], [ms[n] for n in small_names], [vs[n] for n in small_names])
    grads["w_in"], delta["w_in"], new_m["w_in"], new_v["w_in"] = (r.T[None] for r in res_in)
    grads["w_out"], delta["w_out"], new_m["w_out"], new_v["w_out"] = (r[None] for r in res_out)
    grads["w_mem_kv"], delta["w_mem_kv"], new_m["w_mem_kv"], new_v["w_mem_kv"] = (r[None] for r in res_kv)
    nall = len(small_names)
    for k, into in enumerate((grads, delta, new_m, new_v)):
        into.update(zip(small_names, res[k * nall:(k + 1) * nall]))
    loss = res[-1].reshape(())

    order = ("norm_g", "mem_norm_g", "w_in", "conv_w", "conv_b", "w_rg", "b_rg", "w_ig", "b_ig", "lru_lambda",
             "q_norm_g", "k_norm_g", "sinks", "w_mem_kv", "xq_norm_g", "xk_norm_g", "out_norm_g", "w_out")
    return (loss, gx[None], *[grads[n] for n in order], *[delta[n] for n in order], *[new_m[n] for n in order],
            *[new_v[n] for n in order])
```

```python
import jax
import jax.numpy as jnp
import numpy as np
from jax import lax
from jax.experimental import pallas as pl
from jax.experimental.pallas import tpu as pltpu

F32 = jnp.float32
_MXU = jnp.bfloat16
_WIRE = jnp.bfloat16

D_MODEL = 1024
MEM_LEN = 256
HEAD = 64
LRU_W = 512
LRU_BLOCKS = 8
CONV_K = 4
LRU_C = 8.0
SWA_W = 256
KV_W = 128
XATT_W = 256
BLOCK = 128
D_IN = 2304
ROPE_THETA = 500000.0
ROPE_DIM = 16
EPS = 1e-6
NEG_INF = -1e30
C_LRUX, C_LRUG, C_SQ, C_SK, C_SV, C_SWAG, C_XQ, C_XG = 0, 512, 1024, 1280, 1408, 1536, 1792, 2048
G_Q, G_K, G_XQ, G_XK, GAINS_W = 0, 128, 256, 512, 768

ADAM_LR, ADAM_B1, ADAM_B2, ADAM_EPS, ADAM_WD, ADAM_STEP = 0.001, 0.9, 0.999, 1e-08, 0.01, 10

N_CHIPS = 4
ROW_TILE = 256
VMEM_LIMIT = 56 * 1024 * 1024
ADAM_BLOCK_BYTES = 640 * 1024
MESH = pl.DeviceIdType.MESH


def _mm(a, b):
    return jnp.dot(a.astype(_MXU), b.astype(_MXU), preferred_element_type=F32)


def _mm_nt(a, b):
    return lax.dot_general(a.astype(_MXU), b.astype(_MXU), (((1,), (1,)), ((), ())), preferred_element_type=F32)


def _mm_tn(a, b):
    return lax.dot_general(a.astype(_MXU), b.astype(_MXU), (((0,), (0,)), ((), ())), preferred_element_type=F32)


def _group_matrix(width):
    r = lax.shift_right_logical(lax.broadcasted_iota(jnp.int32, (width, width), 0), 6)
    c = lax.shift_right_logical(lax.broadcasted_iota(jnp.int32, (width, width), 1), 6)
    return (r == c).astype(_MXU)


def _seg_mean(x, gm):
    return jnp.dot(x.astype(_MXU), gm, preferred_element_type=F32) * (1.0 / HEAD)


def _row_mean(x):
    return jnp.mean(x, axis=-1, keepdims=True)


def _col_sum(x):
    return jnp.sum(x, axis=0, keepdims=True)


def _sigmoid(x):
    return jax.nn.sigmoid(x)


def _softplus(z):
    e = jnp.exp(-jnp.abs(z))
    u = 1.0 + e
    log1p_e = jnp.where(u == 1.0, e, jnp.log(u) * (e / (u - 1.0)))
    return jnp.maximum(z, 0.0) + log1p_e


def _rope(t, c, s1, s2):
    return t * c + pltpu.roll(t, 120, 1) * s1 + pltpu.roll(t, 8, 1) * s2


def _rope_bwd(d, c, s1, s2):
    return d * c + pltpu.roll(d * s1, 8, 1) + pltpu.roll(d * s2, 120, 1)


def _fold_heads(v):
    out = v
    for k in range(1, v.shape[1] // HEAD):
        out = out + pltpu.roll(v, HEAD * k, 1)
    return out


def _lane_mask(width, lo, hi):
    lane = lax.broadcasted_iota(jnp.int32, (1, width), 1)
    return ((lane >= lo) & (lane < hi)).astype(F32)


def _swa_mask(first_block):
    qi = lax.broadcasted_iota(jnp.int32, (BLOCK, 2 * BLOCK), 0)
    kj = lax.broadcasted_iota(jnp.int32, (BLOCK, 2 * BLOCK), 1)
    rel = qi + BLOCK - kj
    ok = (rel >= 0) & (rel < BLOCK)
    return ok & (jnp.logical_not(first_block) | (kj >= BLOCK))


def _place_kv(t, scale):
    lo = t * (_lane_mask(KV_W, 0, HEAD) * scale)
    hi = t * (_lane_mask(KV_W, HEAD, KV_W) * scale)
    return [a.astype(_MXU) for a in (lo, pltpu.roll(lo, HEAD, 1), pltpu.roll(hi, HEAD, 1), hi)]


def _unplace_kv(d):
    return (_lane_mask(KV_W, 0, HEAD) * (d[0] + pltpu.roll(d[1], HEAD, 1))
            + _lane_mask(KV_W, HEAD, KV_W) * (d[3] + pltpu.roll(d[2], HEAD, 1)))


def _swa_probs(qh, ka, mask, sink):
    s = _mm_nt(qh, ka)
    s = jnp.where(mask, s, NEG_INF)
    m = jnp.maximum(jnp.max(s, axis=-1, keepdims=True), sink)
    p = jnp.exp(s - m)
    esink = jnp.exp(sink - m)
    inv = 1.0 / (jnp.sum(p, axis=-1, keepdims=True) + esink)
    return p * inv, esink * inv


def _mem_probs(s_all):
    out = []
    for j in range(4):
        s = s_all[:, MEM_LEN * j:MEM_LEN * (j + 1)]
        p = jnp.exp(s - jnp.max(s, axis=-1, keepdims=True))
        out.append(p * (1.0 / jnp.sum(p, axis=-1, keepdims=True)))
    return out


def _head_rows(t, scale):
    return jnp.concatenate([t * (_lane_mask(XATT_W, HEAD * j, HEAD * (j + 1)) * scale) for j in range(4)], axis=0)


def _lru_gates(xc, wg_ref, brg, big, lam):
    p0 = _mm(xc[:, :256], wg_ref[0])
    p1 = _mm(xc[:, 256:], wg_ref[1])
    rg = _sigmoid(jnp.concatenate([p0[:, :256], p1[:, :256]], axis=1) + brg)
    ig = _sigmoid(jnp.concatenate([p0[:, 256:], p1[:, 256:]], axis=1) + big)
    sp = _softplus(-lam)
    la = (-LRU_C) * rg * sp
    a = jnp.exp(la)
    th = jnp.tanh(la)
    one_minus_a2 = (-2.0 * th) / (1.0 - th)
    return rg, ig, sp, a, jnp.sqrt(one_minus_a2)


def _const_spec(shape, single=False):
    zeros = (0,) * len(shape)
    if single:
        return pl.BlockSpec(shape, lambda i: zeros, pipeline_mode=pl.Buffered(1))
    return pl.BlockSpec(shape, lambda i: zeros)


def _chip_of(x, y):
    return 2 * x + y


def _partners(x, y, c):
    north = c == 1
    near = (jnp.where(north, 1 - x, x), jnp.where(north, y, 1 - y))
    far = (jnp.where(north, x, 1 - x), jnp.where(north, 1 - y, y))
    return near, far, (1 - x, 1 - y)


def gather_weights(win_t, wout, wkv, conv_w, w_rg, w_ig, head_gains, mem, mem_g):
    arrs = (win_t, wout, wkv)
    n = len(arrs)
    pieces = [(a, 0, arr.shape[0] // 2) for a, arr in enumerate(arrs)]
    npc = len(pieces)

    def body(a0, a1, a2, cw_in, wrg_ref, wig_ref, q_ref, k_ref, xq_ref, xk_ref, mem_ref, mg_ref,
             o0, o1, o2, cw_out, wg_ref, gn_ref, km_ref, vm_ref, s0, s1, s2, cw, ocw, send, recv, lsem):
        ins, outs = (s0, s1, s2), (o0, o1, o2)
        for src, dst in zip((a0, a1, a2), ins):
            dst[...] = src[...].astype(dst.dtype)
        cw[...] = jnp.zeros(cw.shape, F32)
        cw[0:CONV_K, :] = cw_in[0]
        x, y, c = lax.axis_index("x"), lax.axis_index("y"), lax.axis_index("c")
        sibling = (x, y, 1 - c)
        near, far, diag = _partners(x, y, c)
        chips = [near, far, diag]
        me = _chip_of(x, y)

        def landed(p, chip, half):
            a, off, rows_ = pieces[p]
            r = ins[a].shape[0]
            return outs[a].at[pl.ds(pl.multiple_of(chip * r + half * (r // 2) + off, 16), rows_)]

        def mine(p):
            a, off, rows_ = pieces[p]
            return ins[a].at[pl.ds(pl.multiple_of(c * (ins[a].shape[0] // 2) + off, 16), rows_)]

        def copy(k, src, dst, to):
            return pltpu.make_async_remote_copy(src_ref=src, dst_ref=dst, send_sem=send.at[k], recv_sem=recv.at[k],
                                                device_id=to, device_id_type=MESH)

        def cw_rows(chip):
            return ocw.at[pl.ds(pl.multiple_of(chip * 8, 8), 8)]

        locals_ = []
        for a in range(n):
            r = ins[a].shape[0]
            locals_.append(pltpu.make_async_copy(ins[a], outs[a].at[pl.ds(pl.multiple_of(me * r, 16), r)], lsem.at[a]))
        locals_.append(pltpu.make_async_copy(cw, cw_rows(me), lsem.at[n]))
        for cp in locals_:
            cp.start()

        sent = []
        for p in range(npc):
            for j in range(2):
                sent.append(copy(p * 6 + j, mine(p), landed(p, me, c), (*chips[j], c)))
        for j, chip in enumerate(chips):
            sent.append(copy(npc * 6 + j, cw, cw_rows(me), (*chip, c)))
        for cp in sent:
            cp.start()

        gn_ref[...] = jnp.concatenate([q_ref[...]] * 2 + [k_ref[...]] * 2 + [xq_ref[...]] * 4 + [xk_ref[...]] * 4,
                                      axis=1)
        zeros = lambda lanes: [jnp.zeros((HEAD, lanes), F32)] if lanes else []
        for h in range(2):
            for b in range(4):
                row = []
                for w_ref in (wrg_ref, wig_ref):
                    row += zeros(HEAD * b) + [w_ref[0, 4 * h + b]] + zeros(HEAD * (3 - b))
                wg_ref[h, HEAD * b:HEAD * (b + 1), :] = jnp.concatenate(row, axis=1).astype(wg_ref.dtype)

        for j in range(3):
            for p in range(npc):
                got = landed(p, _chip_of(*chips[j]), c)
                copy(p * 6 + j, got, got, sibling).wait_recv()
                if j == 0:
                    sent.append(copy(p * 6 + 2, got, got, (*far, c)))
                    sent[-1].start()
                sent.append(copy(p * 6 + 3 + j, got, got, sibling))
                sent[-1].start()
        for p in range(npc):
            for j in range(3):
                got = landed(p, _chip_of(*chips[(1, 0, 2)[j]]), 1 - c)
                copy(p * 6 + 3 + j, got, got, sibling).wait_recv()
        for j, chip in enumerate(chips):
            got = cw_rows(_chip_of(*chip))
            copy(npc * 6 + j, got, got, (*chip, c)).wait_recv()
        for cp in sent:
            cp.wait_send()
        for cp in locals_:
            cp.wait()
        for chip in range(N_CHIPS):
            cw_out[:, 128 * chip:128 * (chip + 1)] = ocw[8 * chip:8 * chip + CONV_K, :]

        mem_v = mem_ref[...]
        mn = mem_v * lax.rsqrt(_row_mean(mem_v * mem_v) + EPS) * mg_ref[...]
        mkv = _mm(mn, o2[...])
        kpre = mkv[:, :XATT_W]
        km = kpre * lax.rsqrt(_seg_mean(kpre * kpre, _group_matrix(XATT_W)) + EPS) * gn_ref[:, G_XK:GAINS_W]
        km_ref[...] = _head_rows(km, 0.125).astype(km_ref.dtype)
        vm_ref[...] = _head_rows(mkv[:, XATT_W:], 1.0).astype(vm_ref.dtype)

    vm = pl.BlockSpec(memory_space=pltpu.VMEM)
    hbm = pl.BlockSpec(memory_space=pl.ANY)
    head_rows = jax.ShapeDtypeStruct((4 * MEM_LEN, XATT_W), _MXU)
    out_shape = tuple(jax.ShapeDtypeStruct((N_CHIPS * a.shape[0],) + a.shape[1:], _MXU) for a in arrs) + (
        jax.ShapeDtypeStruct((CONV_K, LRU_W), F32), jax.ShapeDtypeStruct((2, 256, 512), _MXU),
        jax.ShapeDtypeStruct((1, GAINS_W), F32), head_rows, head_rows)
    n_rdma = npc * 6 + 3
    return pl.pallas_call(
        body, name="gather_weights", out_shape=out_shape,
        in_specs=[vm] * 12, out_specs=(hbm, hbm, vm, vm, vm, vm, vm, vm),
        scratch_shapes=[pltpu.VMEM(a.shape, _MXU) for a in arrs] + [
            pltpu.VMEM((8, 128), F32), pltpu.VMEM((N_CHIPS * 8, 128), F32),
            pltpu.SemaphoreType.DMA((n_rdma,)), pltpu.SemaphoreType.DMA((n_rdma,)), pltpu.SemaphoreType.DMA((n + 1,))],
        compiler_params=pltpu.CompilerParams(vmem_limit_bytes=VMEM_LIMIT),
    )(win_t, wout, wkv, conv_w, w_rg, w_ig, *head_gains, mem, mem_g)


def _mem_bwd_and_pack(mem_ref, g_ref, w_ref, gn_ref, dkm_ref, dvm_ref, gg_ref, loss_ref, vectors, gw_ref, pk_ref):
    first_row = {name: (row, width) for name, row, width in SMALL_VECTORS}
    half_rows = SMALL_ROWS // 2
    pk_ref[...] = jnp.zeros(pk_ref.shape, F32)

    def rows_at(at, n):
        assert at // half_rows == (at + n - 1) // half_rows
        return at // half_rows, slice(at % half_rows, at % half_rows + n), slice(None)

    def put(name, src):
        row, width = first_row[name]
        per_row = 1 if width < 128 else src.shape[1] // 128
        for t in range(src.shape[0]):
            for r in range(per_row):
                pk_ref[rows_at(row + per_row * t + r, 1)] = src[t:t + 1, 128 * r:128 * (r + 1)]

    for name, ref in vectors.items():
        put(name, ref)
    pk_ref[rows_at(LOSS_ROW, 1)] = loss_ref[0:1, :]
    upper = lax.broadcasted_iota(jnp.int32, (HEAD, 128), 1) >= HEAD
    for h in range(2):
        for b in range(4):
            rg = gg_ref[h, HEAD * b:HEAD * (b + 1), 128 * (b // 2):128 * (b // 2 + 1)]
            ig = gg_ref[h, HEAD * b:HEAD * (b + 1), 256 + 128 * (b // 2):256 + 128 * (b // 2 + 1)]
            if b % 2:
                rg = pltpu.roll(rg, HEAD, axis=1)
            else:
                ig = pltpu.roll(ig, HEAD, axis=1)
            pk_ref[rows_at(GATES_ROW + HEAD * (4 * h + b), HEAD)] = jnp.where(upper, ig, rg)

    mem_v = mem_ref[...]
    mh = mem_v * lax.rsqrt(_row_mean(mem_v * mem_v) + EPS)
    mn = mh * g_ref[...]
    mkv = _mm(mn, w_ref[...])
    kpre = mkv[:, :XATT_W]
    gm = _group_matrix(XATT_W)
    rk = lax.rsqrt(_seg_mean(kpre * kpre, gm) + EPS)
    kn = kpre * rk
    dk = jnp.zeros((MEM_LEN, XATT_W), F32)
    dv = jnp.zeros((MEM_LEN, XATT_W), F32)
    for j in range(4):
        mj = _lane_mask(XATT_W, HEAD * j, HEAD * (j + 1))
        dk = dk + dkm_ref[:, MEM_LEN * j:MEM_LEN * (j + 1)].T * (mj * 0.125)
        dv = dv + dvm_ref[:, MEM_LEN * j:MEM_LEN * (j + 1)].T * mj
    put("xk_norm_g", _fold_heads(_col_sum(dk * kn)))
    dkn = dk * gn_ref[:, G_XK:GAINS_W]
    dkpre = rk * (dkn - kn * _seg_mean(dkn * kn, gm))
    dmkv = jnp.concatenate([dkpre, dv], axis=1)
    gw_ref[...] = _mm_tn(mn, dmkv).reshape(gw_ref.shape)
    dmn = _mm_nt(dmkv, w_ref[...])
    put("mem_norm_g", _col_sum(dmn * mh))


def layer_fwd(x, tgt, rc, rs1, rs2, ng, win_t, cw, cb, wg, brg, big, lam, gains, sinks, km, vm, og, wout):
    seq = x.shape[0]
    tm = min(ROW_TILE, seq)
    nt = seq // tm
    nb = tm // BLOCK

    def body(x_ref, t_ref, c_ref, s1_ref, s2_ref, ng_ref, win_ref, cw_ref, cb_ref, wg_ref, brg_ref, big_ref, lam_ref,
             gn_ref, sink_ref, km_ref, vm_ref, og_ref, wout_ref,
             proj_ref, ya_ref, yb_ref, yc_ref, ycat_ref, xn_ref, dout_ref, pswa_ref, pmem_ref, psink_ref, gates_ref,
             a_ref, loss_ref,
             ext_ref, b_scr, hc_ref, kp_ref, vp_ref, lacc_ref):
        i = pl.program_id(0)

        @pl.when(i == 0)
        def _():
            ext_ref[0:8, :] = jnp.zeros((8, LRU_W), F32)
            hc_ref[...] = jnp.zeros_like(hc_ref)
            kp_ref[...] = jnp.zeros_like(kp_ref)
            vp_ref[...] = jnp.zeros_like(vp_ref)
            lacc_ref[...] = jnp.zeros_like(lacc_ref)

        xv = x_ref[...]
        xn = (xv * lax.rsqrt(_row_mean(xv * xv) + EPS) * ng_ref[...]).astype(_MXU)
        xn_ref[...] = xn.astype(xn_ref.dtype)
        proj_ref[...] = _mm_nt(xn, win_ref[...])

        u = proj_ref[:, C_LRUX:C_LRUX + LRU_W]
        ext_ref[8:8 + tm, :] = u
        xc = cb_ref[...]
        for k in range(CONV_K):
            xc = xc + cw_ref[k:k + 1, :] * ext_ref[pl.ds(5 + k, tm), :]
        ext_ref[0:8, :] = u[tm - 8:tm, :]
        rg, ig, sp, a, sq = _lru_gates(xc, wg_ref, brg_ref[...], big_ref[...], lam_ref[...])
        for k, t in enumerate((xc, rg, ig, sq)):
            gates_ref[:, LRU_W * k:LRU_W * (k + 1)] = t.astype(gates_ref.dtype)
        a_ref[...] = a
        b_scr[...] = sq * (ig * xc)
        row8 = lax.broadcasted_iota(jnp.int32, (8, LRU_W), 0)

        def scan_step(g, carry):
            r0 = pl.multiple_of(g * 8, 8)
            av = a_ref[pl.ds(r0, 8), :]
            bv = b_scr[pl.ds(r0, 8), :]
            for d in (1, 2, 4):
                a_sh = jnp.where(row8 >= d, pltpu.roll(av, d, 0), 1.0)
                b_sh = jnp.where(row8 >= d, pltpu.roll(bv, d, 0), 0.0)
                bv = bv + av * b_sh
                av = av * a_sh
            hv = bv + av * carry
            ya_ref[pl.ds(r0, 8), :] = hv
            return hv[7:8, :]

        hc_ref[0:1, :] = lax.fori_loop(0, tm // 8, scan_step, hc_ref[0:1, :], unroll=True)

        gm128 = _group_matrix(KV_W)
        cv, s1v, s2v = c_ref[...], s1_ref[...], s2_ref[...]

        def head_norm_rope(t, g):
            n = t * lax.rsqrt(_seg_mean(t * t, gm128) + EPS)
            return _rope(n * g, cv, s1v, s2v)

        qs_ = (head_norm_rope(proj_ref[:, C_SQ:C_SQ + 128], gn_ref[:, G_Q:G_K]).astype(_MXU),
               head_norm_rope(proj_ref[:, C_SQ + 128:C_SQ + 256], gn_ref[:, G_Q:G_K]).astype(_MXU))
        kr = head_norm_rope(proj_ref[:, C_SK:C_SK + KV_W], gn_ref[:, G_K:G_XQ])
        sv = proj_ref[:, C_SV:C_SV + KV_W]
        ka = _place_kv(jnp.concatenate([kp_ref[...], kr], axis=0), 0.125)
        va = _place_kv(jnp.concatenate([vp_ref[...], sv], axis=0), 1.0)
        kp_ref[...] = kr[tm - BLOCK:tm, :]
        vp_ref[...] = sv[tm - BLOCK:tm, :]
        lane128 = lax.broadcasted_iota(jnp.int32, (1, 128), 1)
        for b in range(nb):
            mask = _swa_mask((i == 0) & (b == 0)) if b == 0 else _swa_mask(False)
            band = slice(BLOCK * b, BLOCK * b + 2 * BLOCK)
            blk = slice(BLOCK * b, BLOCK * (b + 1))
            psink = jnp.zeros((BLOCK, 128), F32)
            for j in range(4):
                p, pk = _swa_probs(qs_[j // 2][blk], ka[j][band], mask, sink_ref[0, j])
                pswa_ref[blk, 2 * BLOCK * j:2 * BLOCK * (j + 1)] = p.astype(pswa_ref.dtype)
                psink = jnp.where(lane128 == j, pk, psink)
            psink_ref[blk, :] = psink
            for h in range(2):
                yb_ref[blk, KV_W * h:KV_W * (h + 1)] = _mm(
                    pswa_ref[blk, 4 * BLOCK * h:4 * BLOCK * (h + 1)],
                    jnp.concatenate([va[2 * h][band], va[2 * h + 1][band]], axis=0))

        gm256 = _group_matrix(XATT_W)
        xq = proj_ref[:, C_XQ:C_XQ + XATT_W]
        qx = xq * lax.rsqrt(_seg_mean(xq * xq, gm256) + EPS) * gn_ref[:, G_XQ:G_XK]
        pm = _mem_probs(_mm_nt(qx, km_ref[...]))
        for j in range(4):
            pmem_ref[:, MEM_LEN * j:MEM_LEN * (j + 1)] = pm[j].astype(pmem_ref.dtype)
        yc = _mm(pmem_ref[...], vm_ref[...])
        yc_ref[...] = yc

        def gated(y, g, gate):
            return y * lax.rsqrt(_row_mean(y * y) + EPS) * g * (gate * _sigmoid(gate))

        ogv = og_ref[...]
        za = gated(ya_ref[...], ogv[:, :512], proj_ref[:, C_LRUG:C_LRUG + LRU_W])
        zb = gated(yb_ref[...], ogv[:, 512:768], proj_ref[:, C_SWAG:C_SWAG + SWA_W])
        zc = gated(yc, ogv[:, 768:], proj_ref[:, C_XG:C_XG + XATT_W])
        ycat_ref[:, 0:512] = za.astype(ycat_ref.dtype)
        ycat_ref[:, 512:768] = zb.astype(ycat_ref.dtype)
        ycat_ref[:, 768:1024] = zc.astype(ycat_ref.dtype)
        out = xv + _mm(ycat_ref[...], wout_ref[...])
        err = out - t_ref[...]
        dout_ref[...] = (err * (1.0 / D_MODEL)).astype(dout_ref.dtype)
        lacc_ref[...] = lacc_ref[...] + (0.5 / D_MODEL) * jnp.sum(err * err)

        @pl.when(i == nt - 1)
        def _():
            loss_ref[...] = lacc_ref[...]

    def rows(ncol):
        return pl.BlockSpec((tm, ncol), lambda i: (i, 0))

    in_specs = [rows(D_MODEL), rows(D_MODEL), rows(128), rows(128), rows(128),
                _const_spec((1, D_MODEL)), _const_spec((D_IN, D_MODEL), True), _const_spec((CONV_K, LRU_W)),
                _const_spec((1, LRU_W)), _const_spec((2, 256, 512), True), _const_spec((1, LRU_W)),
                _const_spec((1, LRU_W)), _const_spec((1, LRU_W)), _const_spec((1, GAINS_W)), pl.BlockSpec(memory_space=pltpu.SMEM),
                _const_spec((4 * MEM_LEN, XATT_W), True), _const_spec((4 * MEM_LEN, XATT_W), True),
                _const_spec((1, D_MODEL)), _const_spec((D_MODEL, D_MODEL), True)]
    out_shape = (jax.ShapeDtypeStruct((seq, D_IN), F32), jax.ShapeDtypeStruct((seq, LRU_W), F32),
                 jax.ShapeDtypeStruct((seq, SWA_W), F32), jax.ShapeDtypeStruct((seq, XATT_W), F32),
                 jax.ShapeDtypeStruct((seq, D_MODEL), _MXU), jax.ShapeDtypeStruct((seq, D_MODEL), _MXU),
                 jax.ShapeDtypeStruct((seq, D_MODEL), _MXU), jax.ShapeDtypeStruct((seq, 4 * 2 * BLOCK), _MXU),
                 jax.ShapeDtypeStruct((seq, 4 * MEM_LEN), _MXU), jax.ShapeDtypeStruct((seq, 128), F32),
                 jax.ShapeDtypeStruct((seq, 4 * LRU_W), _MXU), jax.ShapeDtypeStruct((seq, LRU_W), F32),
                 jax.ShapeDtypeStruct((8, 128), F32))
    out_specs = (rows(D_IN), rows(LRU_W), rows(SWA_W), rows(XATT_W), rows(D_MODEL), rows(D_MODEL), rows(D_MODEL),
                 rows(4 * 2 * BLOCK), rows(4 * MEM_LEN), rows(128), rows(4 * LRU_W), rows(LRU_W),
                 _const_spec((8, 128)))
    scratch = [pltpu.VMEM((tm + 8, LRU_W), F32), pltpu.VMEM((tm, LRU_W), F32),
               pltpu.VMEM((8, LRU_W), F32), pltpu.VMEM((BLOCK, KV_W), F32), pltpu.VMEM((BLOCK, KV_W), F32),
               pltpu.VMEM((8, 128), F32)]
    return pl.pallas_call(
        body, name="layer_fwd", grid=(nt,), out_shape=out_shape, in_specs=in_specs, out_specs=out_specs,
        scratch_shapes=scratch,
        compiler_params=pltpu.CompilerParams(dimension_semantics=("arbitrary",), vmem_limit_bytes=VMEM_LIMIT),
    )(x, tgt, rc, rs1, rs2, ng, win_t, cw, cb, wg, brg, big, lam, gains, sinks, km, vm, og, wout)


def weight_grads(ycat, dout, dproj, xn, mem_operands, vectors, early_at, late_at):
    seq, ncol = xn.shape
    blk = 256
    n_out, n_in = ycat.shape[1] // blk, dproj.shape[1] // blk
    assert n_out == N_CHIPS and late_at[0] >= n_out
    g_out = jax.ShapeDtypeStruct((N_CHIPS, 2, blk // 2, dout.shape[1]), F32)
    g_kv = jax.ShapeDtypeStruct((N_CHIPS, 2, D_MODEL // 8, 2 * XATT_W), F32)
    g_small = jax.ShapeDtypeStruct((2, SMALL_ROWS // 2, 128), F32)
    shape_e, scratch_e = _hosted_reduce_shapes([g_kv], g_small)
    shape_l, scratch_l = _hosted_reduce_shapes([g_out], None)
    n_mem, names = len(mem_operands), tuple(vectors)

    def body(l1_ref, r1_ref, l2_ref, r2_ref, *refs):
        mem_refs, vec_refs = refs[:n_mem], refs[n_mem:n_mem + len(names)]
        o_ref, sum_out, sum_kv, sum_sm, gout_scr, gkv_scr, pack_scr, *scratch = refs[n_mem + len(names):]
        j = pl.program_id(0)

        @pl.when(j == 0)
        def _():
            _mem_bwd_and_pack(*mem_refs, dict(zip(names, vec_refs)), gkv_scr, pack_scr)

        def reduce_stages(closing):
            _hosted_reduce(j, n_out + n_in, closing, early_at, (gkv_scr, pack_scr), (sum_kv, sum_sm),
                           scratch[:len(scratch_e)], True)
            _hosted_reduce(j, n_out + n_in, closing, late_at, (gout_scr,), (sum_out,), scratch[len(scratch_e):], False)

        reduce_stages(False)

        @pl.when(j < n_out)
        def _():
            gout_scr[j] = _mm_tn(l1_ref[...], r1_ref[...]).reshape(g_out.shape[1:])

        @pl.when(j >= n_out)
        def _():
            o_ref[...] = _mm_tn(l2_ref[...], r2_ref[...])

        reduce_stages(True)

    vm = pl.BlockSpec(memory_space=pltpu.VMEM)
    hbm = pl.BlockSpec(memory_space=pl.ANY)
    return pl.pallas_call(
        body, name="weight_grads", grid=(n_out + n_in,),
        out_shape=(jax.ShapeDtypeStruct((dproj.shape[1], ncol), F32), *shape_l, *shape_e),
        in_specs=[pl.BlockSpec((seq, blk), lambda j: (0, jnp.minimum(j, n_out - 1))), _const_spec(dout.shape, True),
                  pl.BlockSpec((seq, blk), lambda j: (0, jnp.maximum(j - n_out, 0))), _const_spec(xn.shape, True)]
        + [vm] * (n_mem + len(names)),
        out_specs=(pl.BlockSpec((blk, ncol), lambda j: (jnp.maximum(j - n_out, 0), 0)), hbm, hbm, hbm),
        scratch_shapes=[pltpu.VMEM(s.shape, F32) for s in (g_out, g_kv, g_small)] + scratch_e + scratch_l,
        compiler_params=pltpu.CompilerParams(dimension_semantics=("arbitrary",), vmem_limit_bytes=VMEM_LIMIT),
    )(ycat, dout, dproj, xn, *mem_operands, *vectors.values())


def layer_bwd(x, dout, proj, ya, yb, yc, pswa, pmem, psink, gates, a_all, rc, rs1, rs2, ng, win_t, cw, wg, lam, gains,
              km, vm, og, wout):
    seq = x.shape[0]
    tm = min(ROW_TILE, seq)
    nt = seq // tm
    nb = tm // BLOCK

    def body(x_ref, dout_ref, proj_ref, ya_ref, yb_ref, yc_ref, pswa_ref, pmem_ref, psink_ref, gates_ref, a_ref,
             c_ref, s1_ref, s2_ref,
             yah_ref, kvh_ref, ch_ref, s1h_ref, s2h_ref,
             ng_ref, win_ref, cw_ref, wg_ref, lam_ref, gn_ref, km_ref, vm_ref, og_ref, wout_ref,
             gx_ref, dproj_ref, gwg_ref, dkm_ref, dvm_ref, gng_ref, gog_ref, gcb_ref, gbrg_ref, gbig_ref, glam_ref,
             gcw_ref, gqn_ref, gkn_ref, gxqn_ref, gsink_ref,
             hext_ref, aext_ref, an_scr, dh_scr, g_scr, dxc_ext, gcar_ref, dkcar_ref, dvcar_ref):
        i = pl.program_id(0)
        tile = nt - 1 - i
        first_tile = tile == 0

        @pl.when(i == 0)
        def _():
            for r in (gwg_ref, dkm_ref, dvm_ref, gng_ref, gog_ref, gcb_ref, gbrg_ref, gbig_ref, glam_ref, gcw_ref,
                      gqn_ref, gkn_ref, gxqn_ref, gsink_ref, gcar_ref, dkcar_ref, dvcar_ref):
                r[...] = jnp.zeros_like(r)
            dxc_ext[tm:tm + 8, :] = jnp.zeros((8, LRU_W), F32)
            aext_ref[tm:tm + 8, :] = jnp.zeros((8, LRU_W), F32)

        xv = x_ref[...]
        dov = dout_ref[...]
        dz = _mm_nt(dov, wout_ref[...])
        ogv = og_ref[...]

        def group_bwd(y, gate, g, dzg):
            r = lax.rsqrt(_row_mean(y * y) + EPS)
            n = y * r
            sg = _sigmoid(gate)
            dgate = dzg * (n * g) * (sg * (1.0 + gate * (1.0 - sg)))
            dng = dzg * (gate * sg)
            dn = dng * g
            return r * (dn - n * _row_mean(dn * n)), dgate, _col_sum(dng * n)

        dya, dga, goa = group_bwd(ya_ref[...], proj_ref[:, C_LRUG:C_LRUG + LRU_W], ogv[:, :512], dz[:, :512])
        dyb, dgb, gob = group_bwd(yb_ref[...], proj_ref[:, C_SWAG:C_SWAG + SWA_W], ogv[:, 512:768], dz[:, 512:768])
        dyc, dgc, goc = group_bwd(yc_ref[...], proj_ref[:, C_XG:C_XG + XATT_W], ogv[:, 768:], dz[:, 768:])
        gog_ref[...] += jnp.concatenate([goa, gob, goc], axis=1)
        dproj_ref[:, C_LRUG:C_LRUG + LRU_W] = dga.astype(dproj_ref.dtype)
        dproj_ref[:, C_SWAG:C_SWAG + SWA_W] = dgb.astype(dproj_ref.dtype)
        dproj_ref[:, C_XG:C_XG + XATT_W] = dgc.astype(dproj_ref.dtype)

        gm256 = _group_matrix(XATT_W)
        xq = proj_ref[:, C_XQ:C_XQ + XATT_W]
        rq = lax.rsqrt(_seg_mean(xq * xq, gm256) + EPS)
        qn = xq * rq
        qx = qn * gn_ref[:, G_XQ:G_XK]
        qxb = qx.astype(_MXU)
        dycb = dyc.astype(_MXU)
        dp_all = _mm_nt(dycb, vm_ref[...])
        dsm = []
        for j in range(4):
            pj = pmem_ref[:, MEM_LEN * j:MEM_LEN * (j + 1)].astype(F32)
            dp = dp_all[:, MEM_LEN * j:MEM_LEN * (j + 1)]
            dsm.append((pj * (dp - jnp.sum(pj * dp, axis=-1, keepdims=True))).astype(_MXU))
        ds_all = jnp.concatenate(dsm, axis=1)
        dvm_ref[...] += _mm_tn(dycb, pmem_ref[...])
        dkm_ref[...] += _mm_tn(qxb, ds_all)
        dqx = _mm(ds_all, km_ref[...])
        gxqn_ref[...] += _col_sum(dqx * qn)
        dqn = dqx * gn_ref[:, G_XQ:G_XK]
        dproj_ref[:, C_XQ:C_XQ + XATT_W] = (rq * (dqn - qn * _seg_mean(dqn * qn, gm256))).astype(dproj_ref.dtype)

        gm128 = _group_matrix(KV_W)
        cv, s1v, s2v = c_ref[...], s1_ref[...], s2_ref[...]

        def head_norm(t):
            r = lax.rsqrt(_seg_mean(t * t, gm128) + EPS)
            return t * r, r

        qn_, qr_ = zip(head_norm(proj_ref[:, C_SQ:C_SQ + 128]), head_norm(proj_ref[:, C_SQ + 128:C_SQ + 256]))
        qrope = [_rope(qn_[h] * gn_ref[:, G_Q:G_K], cv, s1v, s2v).astype(_MXU) for h in range(2)]
        kn, krr = head_norm(proj_ref[:, C_SK:C_SK + KV_W])
        kr = _rope(kn * gn_ref[:, G_K:G_XQ], cv, s1v, s2v)
        khn, _ = head_norm(kvh_ref[:, 0:KV_W])
        khr = _rope(khn * gn_ref[:, G_K:G_XQ], ch_ref[...], s1h_ref[...], s2h_ref[...])
        ka = _place_kv(jnp.concatenate([khr, kr], axis=0), 0.125)
        va = _place_kv(jnp.concatenate([kvh_ref[:, KV_W:2 * KV_W], proj_ref[:, C_SV:C_SV + KV_W]], axis=0), 1.0)
        lane128 = lax.broadcasted_iota(jnp.int32, (1, 128), 1)
        gsink = jnp.zeros((1, 128), F32)
        dk_band, dv_band, dq_blk = [], [], []
        for b in range(nb):
            band = slice(BLOCK * b, BLOCK * b + 2 * BLOCK)
            blk = slice(BLOCK * b, BLOCK * (b + 1))
            dka, dva, dsb = [], [], []
            deltas = jnp.zeros((BLOCK, 128), F32)
            for j in range(4):
                qh = qrope[j // 2][blk]
                doh = dyb[blk, KV_W * (j // 2):KV_W * (j // 2 + 1)].astype(_MXU)
                pb = pswa_ref[blk, 2 * BLOCK * j:2 * BLOCK * (j + 1)]
                p = pb.astype(F32)
                dp = _mm_nt(doh, va[j][band])
                delta = jnp.sum(p * dp, axis=-1, keepdims=True)
                ds = (p * (dp - delta)).astype(_MXU)
                deltas = jnp.where(lane128 == j, delta, deltas)
                dva.append(_mm_tn(pb, doh))
                dka.append(_mm_tn(ds, qh))
                dsb.append(ds)
            gsink = gsink - _col_sum(psink_ref[blk, :] * deltas)
            dk_band.append(_unplace_kv(dka) * 0.125)
            dv_band.append(_unplace_kv(dva))
            dq_blk.append([_mm(jnp.concatenate(dsb[2 * h:2 * h + 2], axis=1),
                               jnp.concatenate([ka[2 * h][band], ka[2 * h + 1][band]], axis=0)) for h in range(2)])
        gsink_ref[...] += gsink
        dk_rows = [dk_band[b][BLOCK:] + (dk_band[b + 1][:BLOCK] if b + 1 < nb else dkcar_ref[...]) for b in range(nb)]
        dv_rows = [dv_band[b][BLOCK:] + (dv_band[b + 1][:BLOCK] if b + 1 < nb else dvcar_ref[...]) for b in range(nb)]
        dkcar_ref[...] = dk_band[0][:BLOCK]
        dvcar_ref[...] = dv_band[0][:BLOCK]
        dkg = _rope_bwd(jnp.concatenate(dk_rows, axis=0), cv, s1v, s2v)
        gkn = _col_sum(dkg * kn)
        dkn = dkg * gn_ref[:, G_K:G_XQ]
        dproj_ref[:, C_SK:C_SK + KV_W] = (krr * (dkn - kn * _seg_mean(dkn * kn, gm128))).astype(dproj_ref.dtype)
        dproj_ref[:, C_SV:C_SV + KV_W] = jnp.concatenate(dv_rows, axis=0).astype(dproj_ref.dtype)
        gqn = jnp.zeros((1, 128), F32)
        for h in range(2):
            dqg = _rope_bwd(jnp.concatenate([dq_blk[b][h] for b in range(nb)], axis=0), cv, s1v, s2v)
            gqn = gqn + _col_sum(dqg * qn_[h])
            dqn_ = dqg * gn_ref[:, G_Q:G_K]
            dproj_ref[:, C_SQ + 128 * h:C_SQ + 128 * (h + 1)] = (
                qr_[h] * (dqn_ - qn_[h] * _seg_mean(dqn_ * qn_[h], gm128))).astype(dproj_ref.dtype)
        gqn_ref[...] += gqn
        gkn_ref[...] += gkn

        u = proj_ref[:, C_LRUX:C_LRUX + LRU_W]
        xc, rg, ig, sq = (gates_ref[:, LRU_W * k:LRU_W * (k + 1)].astype(F32) for k in range(4))
        a = a_ref[...]
        sp = _softplus(-lam_ref[...])
        hext_ref[0:8, :] = jnp.where(first_tile, 0.0, yah_ref[...])
        hext_ref[8:8 + tm, :] = ya_ref[...]
        hprev = hext_ref[pl.ds(7, tm), :]
        aext_ref[0:tm, :] = a
        an_scr[...] = aext_ref[pl.ds(1, tm), :]
        dh_scr[...] = dya
        dh_scr[tm - 1:tm, :] = dh_scr[tm - 1:tm, :] + gcar_ref[0:1, :]
        row8 = lax.broadcasted_iota(jnp.int32, (8, LRU_W), 0)

        def scan_step(gi, carry):
            r0 = pl.multiple_of((tm // 8 - 1 - gi) * 8, 8)
            av = an_scr[pl.ds(r0, 8), :]
            bv = dh_scr[pl.ds(r0, 8), :]
            for d in (1, 2, 4):
                a_sh = jnp.where(row8 < 8 - d, pltpu.roll(av, 8 - d, 0), 1.0)
                b_sh = jnp.where(row8 < 8 - d, pltpu.roll(bv, 8 - d, 0), 0.0)
                bv = bv + av * b_sh
                av = av * a_sh
            gv = bv + av * carry
            g_scr[pl.ds(r0, 8), :] = gv
            return gv[0:1, :]

        g0 = lax.fori_loop(0, tm // 8, scan_step, jnp.zeros((1, LRU_W), F32), unroll=True)
        gcar_ref[0:1, :] = a[0:1, :] * g0
        gv = g_scr[...]
        da = gv * hprev
        dig = gv * sq * xc
        dxc = gv * sq * ig
        dla = da * a - gv * (ig * xc) * ((a * a) / sq)
        drg = dla * ((-LRU_C) * sp)
        glam_ref[...] += _col_sum(dla * rg)
        dpr = drg * rg * (1.0 - rg)
        dpi = dig * ig * (1.0 - ig)
        gbrg_ref[...] += _col_sum(dpr)
        gbig_ref[...] += _col_sum(dpi)
        dpre0 = jnp.concatenate([dpr[:, :256], dpi[:, :256]], axis=1).astype(_MXU)
        dpre1 = jnp.concatenate([dpr[:, 256:], dpi[:, 256:]], axis=1).astype(_MXU)
        gwg_ref[0] += _mm_tn(xc[:, :256], dpre0)
        gwg_ref[1] += _mm_tn(xc[:, 256:], dpre1)
        dxc = dxc + jnp.concatenate([_mm_nt(dpre0, wg_ref[0]), _mm_nt(dpre1, wg_ref[1])], axis=1)
        gcb_ref[...] += _col_sum(dxc)
        dxc_ext[0:tm, :] = dxc
        du = jnp.zeros((tm, LRU_W), F32)
        for k in range(CONV_K):
            later = dxc_ext[pl.ds(3 - k, tm), :]
            gcw_ref[k:k + 1, :] += _col_sum(later * u)
            du = du + cw_ref[k:k + 1, :] * later
        dxc_ext[tm:tm + 8, :] = dxc[0:8, :]
        dproj_ref[:, C_LRUX:C_LRUX + LRU_W] = du.astype(dproj_ref.dtype)

        dxn = _mm(dproj_ref[...], win_ref[...])
        rx = lax.rsqrt(_row_mean(xv * xv) + EPS)
        xh = xv * rx
        gng_ref[...] += _col_sum(dxn * xh)
        dxh = dxn * ng_ref[...]
        gx_ref[...] = dov.astype(F32) + rx * (dxh - xh * _row_mean(dxh * xh))

        @pl.when(i == nt - 1)
        def _():
            glam_ref[...] = glam_ref[...] * (LRU_C * _sigmoid(-lam_ref[...]))
            for r in (gqn_ref, gkn_ref, gxqn_ref):
                r[...] = _fold_heads(r[...])

    def rows(ncol, arr_cols_block=0):
        return pl.BlockSpec((tm, ncol), lambda i: (nt - 1 - i, arr_cols_block))

    def halo(nrow, ncol, colblk=0):
        per = tm // nrow
        return pl.BlockSpec((nrow, ncol), lambda i: (jnp.maximum((nt - 1 - i) * per - 1, 0), colblk))

    in_specs = [rows(D_MODEL), rows(D_MODEL), rows(D_IN), rows(LRU_W), rows(SWA_W), rows(XATT_W),
                rows(4 * 2 * BLOCK), rows(4 * MEM_LEN), rows(128), rows(4 * LRU_W), rows(LRU_W),
                rows(128), rows(128), rows(128),
                halo(8, LRU_W), halo(BLOCK, 2 * KV_W, C_SK // (2 * KV_W)),
                halo(BLOCK, 128), halo(BLOCK, 128), halo(BLOCK, 128),
                _const_spec((1, D_MODEL)), _const_spec((D_IN, D_MODEL), True), _const_spec((CONV_K, LRU_W)),
                _const_spec((2, 256, 512), True), _const_spec((1, LRU_W)), _const_spec((1, GAINS_W)),
                _const_spec((4 * MEM_LEN, XATT_W), True), _const_spec((4 * MEM_LEN, XATT_W), True),
                _const_spec((1, D_MODEL)), _const_spec((D_MODEL, D_MODEL), True)]
    small = [(2, 256, 512), (XATT_W, 4 * MEM_LEN), (XATT_W, 4 * MEM_LEN), (1, D_MODEL), (1, D_MODEL), (1, LRU_W), (1, LRU_W),
             (1, LRU_W), (1, LRU_W), (CONV_K, LRU_W), (1, 128), (1, 128), (1, XATT_W), (1, 128)]
    out_shape = (jax.ShapeDtypeStruct((seq, D_MODEL), F32), jax.ShapeDtypeStruct((seq, D_IN), _MXU)) + tuple(
        jax.ShapeDtypeStruct(s, F32) for s in small)
    out_specs = (rows(D_MODEL), rows(D_IN)) + tuple(_const_spec(s) for s in small)
    scratch = [pltpu.VMEM((tm + 8, LRU_W), F32), pltpu.VMEM((tm + 8, LRU_W), F32),
               pltpu.VMEM((tm, LRU_W), F32), pltpu.VMEM((tm, LRU_W), F32), pltpu.VMEM((tm, LRU_W), F32),
               pltpu.VMEM((tm + 8, LRU_W), F32),
               pltpu.VMEM((8, LRU_W), F32), pltpu.VMEM((BLOCK, KV_W), F32), pltpu.VMEM((BLOCK, KV_W), F32)]
    return pl.pallas_call(
        body, name="layer_bwd", grid=(nt,), out_shape=out_shape, in_specs=in_specs, out_specs=out_specs,
        scratch_shapes=scratch,
        compiler_params=pltpu.CompilerParams(dimension_semantics=("arbitrary",), vmem_limit_bytes=VMEM_LIMIT),
    )(x, dout, proj, ya, yb, yc, pswa, pmem, psink, gates, a_all, rc, rs1, rs2, ya, proj, rc, rs1, rs2,
      ng, win_t, cw, wg, lam, gains, km, vm, og, wout)


def _reduce_protocol(big, sm, outs, osm, r1, r1s, wire, r2, r2s, wire2, ps, own, send, recv, lsem):
    nbig = len(big)
    x, y, c = lax.axis_index("x"), lax.axis_index("y"), lax.axis_index("c")
    sibling = (x, y, 1 - c)
    near, far, diag = _partners(x, y, c)
    me, near_id, far_id, diag_id = _chip_of(x, y), _chip_of(*near), _chip_of(*far), _chip_of(*diag)

    def copy(k, src, dst, to):
        return pltpu.make_async_remote_copy(src_ref=src, dst_ref=dst, send_sem=send.at[k], recv_sem=recv.at[k],
                                            device_id=to, device_id_type=MESH)

    def sent(stage, a):
        if a == nbig:
            src, dst, to = ((sm.at[1 - c], r1s, sibling), (r1s, r2s.at[0], (*near, c)), (ps, r2s.at[1], (*far, c)),
                            (osm.at[c], osm.at[c], sibling))[stage]
            return [copy(5 * nbig + stage, src, dst, to)]
        if stage == 0:
            return [copy(5 * a, big[a].at[:, 1 - c], r1[a], sibling)]
        if stage == 1:
            return [copy(5 * a + 1, wire[a].at[near_id], r2[a].at[0], (*near, c)),
                    copy(5 * a + 2, wire[a].at[diag_id], r2[a].at[1], (*near, c))]
        if stage == 2:
            return [copy(5 * a + 3, wire2[a], r2[a].at[2], (*far, c))]
        return [copy(5 * a + 4, outs[a].at[c], outs[a].at[c], sibling)]

    arrays = range(nbig + (sm is not None))

    def start(stage, a):
        for cp in sent(stage, a):
            cp.start()

    def arrived(k, ref):
        copy(k, ref, ref, sibling).wait_recv()

    def loads():
        return [pltpu.make_async_copy(big[a].at[:, c], own[a], lsem.at[a]) for a in range(nbig)]

    def stage0():
        for a in arrays:
            start(0, a)
        for cp in loads():
            cp.start()

    def stage1():
        for a in range(nbig):
            loads()[a].wait()
            arrived(5 * a, r1[a])
            for k in range(N_CHIPS):
                r1[a][k] = own[a][k] + r1[a][k]
                wire[a][k] = r1[a][k].astype(wire[a].dtype)
            start(1, a)
        if sm is not None:
            arrived(5 * nbig, r1s)
            r1s[...] = sm[c] + r1s[...]
            start(1, nbig)

    def stage2():
        for a in range(nbig):
            arrived(5 * a + 1, r2[a].at[0])
            arrived(5 * a + 2, r2[a].at[1])
            r1[a][me] = r1[a][me] + r2[a][0].astype(F32)
            wire2[a][...] = (r1[a][far_id] + r2[a][1].astype(F32)).astype(wire2[a].dtype)
            start(2, a)
        if sm is not None:
            arrived(5 * nbig + 1, r2s.at[0])
            ps[...] = r1s[...] + r2s[0]
            start(2, nbig)

    def stage3():
        for a in range(nbig):
            arrived(5 * a + 3, r2[a].at[2])
            outs[a][c] = r1[a][me] + r2[a][2].astype(F32)
            start(3, a)
        if sm is not None:
            arrived(5 * nbig + 2, r2s.at[1])
            osm[c] = ps[...] + r2s[1]
            start(3, nbig)

    def stage4():
        for a in range(nbig):
            arrived(5 * a + 4, outs[a].at[1 - c])
        if sm is not None:
            arrived(5 * nbig + 3, osm.at[1 - c])
        for stage in range(4):
            for a in arrays:
                for cp in sent(stage, a):
                    cp.wait_send()

    return [stage0, stage1, stage2, stage3, stage4]


def _reduce_buffers(bigs, g_small):
    half = [b.shape[2:] for b in bigs]
    sm_half = None if g_small is None else g_small.shape[1:]
    out_shape = [jax.ShapeDtypeStruct((2,) + h, F32) for h in half]
    small = lambda lead: [] if g_small is None else [pltpu.VMEM(lead + sm_half, F32)]
    if g_small is not None:
        out_shape.append(jax.ShapeDtypeStruct(g_small.shape, F32))
    n_sem = 5 * len(bigs) + 4
    scratch = ([pltpu.VMEM((N_CHIPS,) + h, F32) for h in half] + small(())
               + [pltpu.VMEM((N_CHIPS,) + h, _WIRE) for h in half]
               + [pltpu.VMEM((3,) + h, _WIRE) for h in half] + small((2,))
               + [pltpu.VMEM(h, _WIRE) for h in half] + small(())
               + [pltpu.VMEM((N_CHIPS,) + h, F32) for h in half]
               + [pltpu.SemaphoreType.DMA((n_sem,)), pltpu.SemaphoreType.DMA((n_sem,)),
                  pltpu.SemaphoreType.DMA((len(bigs),))])
    return out_shape, scratch


def _split_reduce_refs(refs, nbig, has_small):
    it = iter(refs)
    take = lambda n: [next(it) for _ in range(n)]
    one = lambda: next(it) if has_small else None
    big, sm = take(nbig), one()
    outs, osm = take(nbig), one()
    r1, r1s, wire, r2, r2s, wire2, ps, own = take(nbig), one(), take(nbig), take(nbig), one(), take(nbig), one(), take(nbig)
    send, recv, lsem = take(3)
    return big, sm, outs, osm, r1, r1s, wire, r2, r2s, wire2, ps, own, send, recv, lsem


def _hosted_reduce_shapes(bigs, g_small):
    red_shape, scratch = _reduce_buffers(bigs, g_small)
    nres = len(red_shape)
    return red_shape, [pltpu.VMEM(r.shape, r.dtype) for r in red_shape] + scratch + [pltpu.SemaphoreType.DMA((nres,))]


def _hosted_reduce(step, n_steps, closing, stage_at, operands, results, scratch, has_small):
    nres = len(results)
    sums, rest, fsem = scratch[:nres], scratch[nres:-1], scratch[-1]
    refs = tuple(operands) + tuple(sums) + tuple(rest)

    def to_results():
        out = [pltpu.make_async_copy(sums[k], results[k], fsem.at[k]) for k in range(nres)]
        for cp in out:
            cp.start()
        for cp in out:
            cp.wait()

    stages = _reduce_protocol(*_split_reduce_refs(refs, nres - has_small, has_small))

    def last_stage():
        stages[-1]()
        to_results()

    for at, stage in zip(stage_at, stages[:-1] + [last_stage]):
        if closing == (at == n_steps):
            pl.when(step == min(at, n_steps - 1))(stage)


def reduce_grads(big, name, parts):
    chips, halves, rows_, cols = big.shape
    sub = jax.ShapeDtypeStruct((chips, halves, rows_ // parts, cols), big.dtype)

    def body(b_ref, o_ref, *scratch):
        refs = [b_ref.at[:, :, s] for s in range(parts)] + [o_ref.at[:, s] for s in range(parts)] + list(scratch)
        for stage in _reduce_protocol(*_split_reduce_refs(refs, parts, False)):
            stage()

    _, scratch = _reduce_buffers([sub] * parts, None)
    return pl.pallas_call(
        body, name=name, out_shape=jax.ShapeDtypeStruct((halves, parts, rows_ // parts, cols), F32),
        in_specs=[pl.BlockSpec(memory_space=pl.ANY)], out_specs=pl.BlockSpec(memory_space=pltpu.VMEM),
        scratch_shapes=scratch, compiler_params=pltpu.CompilerParams(vmem_limit_bytes=VMEM_LIMIT),
    )(big.reshape(chips, halves, parts, rows_ // parts, cols))


def adamw(items, g_pack, ws, ms, vs):
    blocks = []
    for k, (w, _, _, _) in enumerate(items):
        rows_, cols = w.shape
        tr = max(t for t in range(8, rows_ + 1, 8) if rows_ % t == 0 and t * cols * 4 <= ADAM_BLOCK_BYTES)
        blocks += [(k, r, tr) for r in range(0, rows_, tr)]
    nin, n = 4 * len(items), len(ws)

    def body(*refs):
        mats_in, small_in = refs[:nin], refs[nin:nin + 1 + 3 * n]
        n_out = nin + 4 * n + 1
        outs, scratch = refs[nin + 1 + 3 * n:nin + 1 + 3 * n + n_out], refs[nin + 1 + 3 * n + n_out:]
        buf, (lsem, ssem) = scratch[:nin], scratch[nin:]
        loads = [[pltpu.make_async_copy(mats_in[4 * k + q].at[pl.ds(r, tr)], buf[4 * k + q].at[pl.ds(r, tr)],
                                        lsem.at[4 * c + q]) for q in range(4)] for c, (k, r, tr) in enumerate(blocks)]
        for cps in loads:
            for cp in cps:
                cp.start()
        _adamw_small(small_in[0], *(small_in[1 + k * n:1 + (k + 1) * n] for k in range(3)),
                     *(outs[nin + k * n:nin + (k + 1) * n] for k in range(4)), outs[-1])
        stores = []
        for c, (k, r, tr) in enumerate(blocks):
            for cp in loads[c]:
                cp.wait()
            w_ref, g_ref, m_ref, v_ref = (buf[4 * k + q].at[pl.ds(r, tr)] for q in range(4))
            w_ref[...], m_ref[...], v_ref[...] = _adam_update(w_ref[...], g_ref[...], m_ref[...], v_ref[...])
            for q, src in enumerate((g_ref, w_ref, m_ref, v_ref)):
                stores.append(pltpu.make_async_copy(src, outs[4 * k + q].at[pl.ds(r, tr)], ssem.at[4 * c + q]))
                stores[-1].start()
        for cp in stores:
            cp.wait()

    shapes = [jax.ShapeDtypeStruct(w.shape, F32) for w, _, _, _ in items for _ in range(4)]
    vm = pl.BlockSpec(memory_space=pltpu.VMEM)
    hbm = pl.BlockSpec(memory_space=pl.ANY)
    like = [jax.ShapeDtypeStruct(w.shape, F32) for w in ws]
    res = pl.pallas_call(
        body, name="adamw", out_shape=(*shapes, *like * 4, jax.ShapeDtypeStruct((1, 1), F32)),
        in_specs=[hbm] * nin + [vm] * (1 + 3 * n), out_specs=(*[hbm] * nin, *[vm] * (4 * n + 1)),
        scratch_shapes=[pltpu.VMEM(s.shape, F32) for s in shapes] + [pltpu.SemaphoreType.DMA((4 * len(blocks),))] * 2,
        compiler_params=pltpu.CompilerParams(vmem_limit_bytes=VMEM_LIMIT),
    )(*[a for item in items for a in item], g_pack, *ws, *ms, *vs)
    return [res[4 * k:4 * k + 4] for k in range(len(items))], res[nin:]


def _adam_update(w, g, m, v):
    nm = ADAM_B1 * m + (1.0 - ADAM_B1) * g
    nv = ADAM_B2 * v + (1.0 - ADAM_B2) * (g * g)
    m_hat = nm / (1.0 - ADAM_B1 ** ADAM_STEP)
    v_hat = nv / (1.0 - ADAM_B2 ** ADAM_STEP)
    return (-ADAM_LR) * (m_hat / (jnp.sqrt(v_hat) + ADAM_EPS) + ADAM_WD * w), nm, nv


def _adamw_small(pk, w_refs, m_refs, v_refs, g_out, d_out, nm_out, nv_out, loss_ref):
    nvec = len(SMALL_VECTORS)
    loss_ref[...] = pk[LOSS_ROW:LOSS_ROW + 1, 0:1]
    chip = 2 * lax.axis_index("x") + lax.axis_index("y")
    for k, (name, row, width) in enumerate(SMALL_VECTORS):
        if name == "conv_w":
            g = jnp.concatenate([pk[pl.ds(row + 4 * t + chip, 1), :] for t in range(CONV_K)], axis=0)[None]
        elif width >= 128:
            g = jnp.concatenate([pk[row + r:row + r + 1, :] for r in range(width // 128)], axis=1)
        else:
            g = pk[row:row + 1, 0:width]
        g_out[k][...] = g
        d_out[k][...], nm_out[k][...], nv_out[k][...] = _adam_update(w_refs[k][...], g, m_refs[k][...], v_refs[k][...])
    for k in range(nvec, nvec + len(SMALL_MATRICES)):
        for b in range(LRU_BLOCKS):
            rows_ = pk[GATES_ROW + HEAD * b:GATES_ROW + HEAD * (b + 1), :]
            g = (pltpu.roll(rows_, HEAD, axis=1) if k > nvec else rows_)[:, 0:HEAD]
            g_out[k][0, b] = g
            d_out[k][0, b], nm_out[k][0, b], nv_out[k][0, b] = _adam_update(
                w_refs[k][0, b], g, m_refs[k][0, b], v_refs[k][0, b])


SMALL_VECTORS = (("norm_g", 512, 1024), ("mem_norm_g", 520, 1024), ("conv_w", 528, 512), ("conv_b", 544, 512),
                 ("b_rg", 548, 512), ("b_ig", 552, 512), ("lru_lambda", 556, 512), ("q_norm_g", 560, 64),
                 ("k_norm_g", 561, 64), ("sinks", 562, 4), ("xq_norm_g", 563, 64), ("xk_norm_g", 564, 64),
                 ("out_norm_g", 565, 1024))
LOSS_ROW = 573
SMALL_MATRICES = ("w_rg", "w_ig")
GATES_ROW = 0
SMALL_ROWS = 640


def _rope_tables(seq):
    pos = np.arange(seq, dtype=np.float32)
    inv_freq = (np.float32(ROPE_THETA) ** (-(np.arange(0, ROPE_DIM, 2, dtype=np.float32) / np.float32(ROPE_DIM)))
                ).astype(np.float32)
    ang = (pos[:, None] * inv_freq[None, :]).astype(np.float32)
    cos, sin = np.cos(ang).astype(np.float32), np.sin(ang).astype(np.float32)
    z = lambda n: np.zeros((seq, n), np.float32)
    c64 = np.concatenate([cos, cos, np.ones((seq, HEAD - ROPE_DIM), np.float32)], axis=1)
    s1_64 = np.concatenate([-sin, z(HEAD - 8)], axis=1)
    s2_64 = np.concatenate([z(8), sin, z(HEAD - ROPE_DIM)], axis=1)
    return tuple(jnp.asarray(np.concatenate([t, t], axis=1)) for t in (c64, s1_64, s2_64))


def kernel(x, mem, norm_g, mem_norm_g, w_in, conv_w, conv_b, w_rg, b_rg, w_ig, b_ig, lru_lambda, q_norm_g, k_norm_g, sinks, w_mem_kv, xq_norm_g, xk_norm_g, out_norm_g, w_out, loss_target, m_norm_g, m_mem_norm_g, m_w_in, m_conv_w, m_conv_b, m_w_rg, m_b_rg, m_w_ig, m_b_ig, m_lru_lambda, m_q_norm_g, m_k_norm_g, m_sinks, m_w_mem_kv, m_xq_norm_g, m_xk_norm_g, m_out_norm_g, m_w_out, v_norm_g, v_mem_norm_g, v_w_in, v_conv_w, v_conv_b, v_w_rg, v_b_rg, v_w_ig, v_b_ig, v_lru_lambda, v_q_norm_g, v_k_norm_g, v_sinks, v_w_mem_kv, v_xq_norm_g, v_xk_norm_g, v_out_norm_g, v_w_out):
    seq = x.shape[1]
    xs, tgt, mems = x[0], loss_target[0], mem[0]

    win_t, wout, wkv, cw, wg, gains, km, vm = gather_weights(
        w_in[0].T, w_out[0], w_mem_kv[0], conv_w, w_rg, w_ig, (q_norm_g, k_norm_g, xq_norm_g, xk_norm_g), mems,
        mem_norm_g)
    rc, rs1, rs2 = _rope_tables(seq)
    proj, ya, yb, yc, ycat, xn, dout, pswa, pmem, psink, gates, a_all, loss8 = layer_fwd(
        xs, tgt, rc, rs1, rs2, norm_g, win_t, cw, conv_b, wg, b_rg, b_ig, lru_lambda, gains, sinks, km, vm,
        out_norm_g, wout)
    (gx, dproj, g_wg, dkm, dvm, g_ng, g_og, g_cb, g_brg, g_big, g_lam, g_cw, g_qn, g_kn, g_xqn, g_sink) = layer_bwd(
        xs, dout, proj, ya, yb, yc, pswa, pmem, psink, gates, a_all, rc, rs1, rs2, norm_g, win_t, cw, wg, lru_lambda,
        gains, km, vm, out_norm_g, wout)
    g_win_t, r_out, r_kv, r_small = weight_grads(
        ycat, dout, dproj, xn, (mems, mem_norm_g, wkv, gains, dkm, dvm, g_wg, loss8), dict(
            norm_g=g_ng, conv_w=g_cw, conv_b=g_cb, b_rg=g_brg, b_ig=g_big, lru_lambda=g_lam, q_norm_g=g_qn,
            k_norm_g=g_kn, sinks=g_sink, xq_norm_g=g_xqn, out_norm_g=g_og), (0, 1, 4, 7, 8), (4, 5, 9, 11, 13))
    r_in = reduce_grads(g_win_t.reshape(N_CHIPS, 2, D_IN // 8, D_MODEL), "reduce_w_in", 6)

    r_small = r_small.reshape(SMALL_ROWS, 128)
    grads = {}
    weights = dict(norm_g=norm_g, mem_norm_g=mem_norm_g, w_in=w_in, conv_w=conv_w, conv_b=conv_b, w_rg=w_rg, b_rg=b_rg,
                   w_ig=w_ig, b_ig=b_ig, lru_lambda=lru_lambda, q_norm_g=q_norm_g, k_norm_g=k_norm_g, sinks=sinks,
                   w_mem_kv=w_mem_kv, xq_norm_g=xq_norm_g, xk_norm_g=xk_norm_g, out_norm_g=out_norm_g, w_out=w_out)
    ms = dict(norm_g=m_norm_g, mem_norm_g=m_mem_norm_g, w_in=m_w_in, conv_w=m_conv_w, conv_b=m_conv_b, w_rg=m_w_rg,
              b_rg=m_b_rg, w_ig=m_w_ig, b_ig=m_b_ig, lru_lambda=m_lru_lambda, q_norm_g=m_q_norm_g, k_norm_g=m_k_norm_g,
              sinks=m_sinks, w_mem_kv=m_w_mem_kv, xq_norm_g=m_xq_norm_g, xk_norm_g=m_xk_norm_g,
              out_norm_g=m_out_norm_g, w_out=m_w_out)
    vs = dict(norm_g=v_norm_g, mem_norm_g=v_mem_norm_g, w_in=v_w_in, conv_w=v_conv_w, conv_b=v_conv_b, w_rg=v_w_rg,
              b_rg=v_b_rg, w_ig=v_w_ig, b_ig=v_b_ig, lru_lambda=v_lru_lambda, q_norm_g=v_q_norm_g, k_norm_g=v_k_norm_g,
              sinks=v_sinks, w_mem_kv=v_w_mem_kv, xq_norm_g=v_xq_norm_g, xk_norm_g=v_xk_norm_g,
              out_norm_g=v_out_norm_g, w_out=v_w_out)

    delta, new_m, new_v = {}, {}, {}
    small_names = [n for n, _, _ in SMALL_VECTORS] + list(SMALL_MATRICES)
    (res_in, res_out, res_kv), res = adamw(
        [(w_in[0].T, r_in.reshape(D_IN // 4, D_MODEL), m_w_in[0].T, v_w_in[0].T),
         (w_out[0], r_out.reshape(D_MODEL // 4, D_MODEL), m_w_out[0], v_w_out[0]),
         (w_mem_kv[0], r_kv.reshape(D_MODEL // 4, 2 * XATT_W), m_w_mem_kv[0], v_w_mem_kv[0])],
        r_small, [weights[n] for n in small_names], [ms[n] for n in small_names], [vs[n] for n in small_names])
    grads["w_in"], delta["w_in"], new_m["w_in"], new_v["w_in"] = (r.T[None] for r in res_in)
    grads["w_out"], delta["w_out"], new_m["w_out"], new_v["w_out"] = (r[None] for r in res_out)
    grads["w_mem_kv"], delta["w_mem_kv"], new_m["w_mem_kv"], new_v["w_mem_kv"] = (r[None] for r in res_kv)
    nall = len(small_names)
    for k, into in enumerate((grads, delta, new_m, new_v)):
        into.update(zip(small_names, res[k * nall:(k + 1) * nall]))
    loss = res[-1].reshape(())

    order = ("norm_g", "mem_norm_g", "w_in", "conv_w", "conv_b", "w_rg", "b_rg", "w_ig", "b_ig", "lru_lambda",
             "q_norm_g", "k_norm_g", "sinks", "w_mem_kv", "xq_norm_g", "xk_norm_g", "out_norm_g", "w_out")
    return (loss, gx[None], *[grads[n] for n in order], *[delta[n] for n in order], *[new_m[n] for n in order],
            *[new_v[n] for n in order])
```

```python
import jax
import jax.numpy as jnp
import numpy as np
from jax import lax
from jax.experimental import pallas as pl
from jax.experimental.pallas import tpu as pltpu

F32 = jnp.float32
_MXU = jnp.bfloat16
_WIRE = jnp.bfloat16

D_MODEL = 1024
MEM_LEN = 256
HEAD = 64
LRU_W = 512
LRU_BLOCKS = 8
CONV_K = 4
LRU_C = 8.0
SWA_W = 256
KV_W = 128
XATT_W = 256
BLOCK = 128
D_IN = 2304
ROPE_THETA = 500000.0
ROPE_DIM = 16
EPS = 1e-6
NEG_INF = -1e30
C_LRUX, C_LRUG, C_SQ, C_SK, C_SV, C_SWAG, C_XQ, C_XG = 0, 512, 1024, 1280, 1408, 1536, 1792, 2048
G_Q, G_K, G_XQ, G_XK, GAINS_W = 0, 128, 256, 512, 768

ADAM_LR, ADAM_B1, ADAM_B2, ADAM_EPS, ADAM_WD, ADAM_STEP = 0.001, 0.9, 0.999, 1e-08, 0.01, 10

N_CHIPS = 4
ROW_TILE = 256
VMEM_LIMIT = 56 * 1024 * 1024
ADAM_BLOCK_BYTES = 640 * 1024
MESH = pl.DeviceIdType.MESH


def _mm(a, b):
    return jnp.dot(a.astype(_MXU), b.astype(_MXU), preferred_element_type=F32)


def _mm_nt(a, b):
    return lax.dot_general(a.astype(_MXU), b.astype(_MXU), (((1,), (1,)), ((), ())), preferred_element_type=F32)


def _mm_tn(a, b):
    return lax.dot_general(a.astype(_MXU), b.astype(_MXU), (((0,), (0,)), ((), ())), preferred_element_type=F32)


def _group_matrix(width):
    r = lax.shift_right_logical(lax.broadcasted_iota(jnp.int32, (width, width), 0), 6)
    c = lax.shift_right_logical(lax.broadcasted_iota(jnp.int32, (width, width), 1), 6)
    return (r == c).astype(_MXU)


def _seg_mean(x, gm):
    return jnp.dot(x.astype(_MXU), gm, preferred_element_type=F32) * (1.0 / HEAD)


def _row_mean(x):
    return jnp.mean(x, axis=-1, keepdims=True)


def _col_sum(x):
    return jnp.sum(x, axis=0, keepdims=True)


def _sigmoid(x):
    return jax.nn.sigmoid(x)


def _softplus(z):
    e = jnp.exp(-jnp.abs(z))
    u = 1.0 + e
    log1p_e = jnp.where(u == 1.0, e, jnp.log(u) * (e / (u - 1.0)))
    return jnp.maximum(z, 0.0) + log1p_e


def _rope(t, c, s1, s2):
    return t * c + pltpu.roll(t, 120, 1) * s1 + pltpu.roll(t, 8, 1) * s2


def _rope_bwd(d, c, s1, s2):
    return d * c + pltpu.roll(d * s1, 8, 1) + pltpu.roll(d * s2, 120, 1)


def _fold_heads(v):
    out = v
    for k in range(1, v.shape[1] // HEAD):
        out = out + pltpu.roll(v, HEAD * k, 1)
    return out


def _lane_mask(width, lo, hi):
    lane = lax.broadcasted_iota(jnp.int32, (1, width), 1)
    return ((lane >= lo) & (lane < hi)).astype(F32)


def _swa_mask(first_block):
    qi = lax.broadcasted_iota(jnp.int32, (BLOCK, 2 * BLOCK), 0)
    kj = lax.broadcasted_iota(jnp.int32, (BLOCK, 2 * BLOCK), 1)
    rel = qi + BLOCK - kj
    ok = (rel >= 0) & (rel < BLOCK)
    return ok & (jnp.logical_not(first_block) | (kj >= BLOCK))


def _place_kv(t, scale):
    lo = t * (_lane_mask(KV_W, 0, HEAD) * scale)
    hi = t * (_lane_mask(KV_W, HEAD, KV_W) * scale)
    return [a.astype(_MXU) for a in (lo, pltpu.roll(lo, HEAD, 1), pltpu.roll(hi, HEAD, 1), hi)]


def _unplace_kv(d):
    return (_lane_mask(KV_W, 0, HEAD) * (d[0] + pltpu.roll(d[1], HEAD, 1))
            + _lane_mask(KV_W, HEAD, KV_W) * (d[3] + pltpu.roll(d[2], HEAD, 1)))


def _swa_probs(qh, ka, mask, sink):
    s = _mm_nt(qh, ka)
    s = jnp.where(mask, s, NEG_INF)
    m = jnp.maximum(jnp.max(s, axis=-1, keepdims=True), sink)
    p = jnp.exp(s - m)
    esink = jnp.exp(sink - m)
    inv = 1.0 / (jnp.sum(p, axis=-1, keepdims=True) + esink)
    return p * inv, esink * inv


def _mem_probs(s_all):
    out = []
    for j in range(4):
        s = s_all[:, MEM_LEN * j:MEM_LEN * (j + 1)]
        p = jnp.exp(s - jnp.max(s, axis=-1, keepdims=True))
        out.append(p * (1.0 / jnp.sum(p, axis=-1, keepdims=True)))
    return out


def _head_rows(t, scale):
    return jnp.concatenate([t * (_lane_mask(XATT_W, HEAD * j, HEAD * (j + 1)) * scale) for j in range(4)], axis=0)


def _lru_gates(xc, wg_ref, brg, big, lam):
    p0 = _mm(xc[:, :256], wg_ref[0])
    p1 = _mm(xc[:, 256:], wg_ref[1])
    rg = _sigmoid(jnp.concatenate([p0[:, :256], p1[:, :256]], axis=1) + brg)
    ig = _sigmoid(jnp.concatenate([p0[:, 256:], p1[:, 256:]], axis=1) + big)
    sp = _softplus(-lam)
    la = (-LRU_C) * rg * sp
    a = jnp.exp(la)
    th = jnp.tanh(la)
    one_minus_a2 = (-2.0 * th) / (1.0 - th)
    return rg, ig, sp, a, jnp.sqrt(one_minus_a2)


def _const_spec(shape, single=False):
    zeros = (0,) * len(shape)
    if single:
        return pl.BlockSpec(shape, lambda i: zeros, pipeline_mode=pl.Buffered(1))
    return pl.BlockSpec(shape, lambda i: zeros)


def _chip_of(x, y):
    return 2 * x + y


def _partners(x, y, c):
    north = c == 1
    near = (jnp.where(north, 1 - x, x), jnp.where(north, y, 1 - y))
    far = (jnp.where(north, x, 1 - x), jnp.where(north, 1 - y, y))
    return near, far, (1 - x, 1 - y)


def gather_weights(win_t, wout, wkv, conv_w, w_rg, w_ig, head_gains, mem, mem_g):
    arrs = (win_t, wout, wkv)
    n = len(arrs)
    pieces = [(a, 0, arr.shape[0] // 2) for a, arr in enumerate(arrs)]
    npc = len(pieces)

    def body(a0, a1, a2, cw_in, wrg_ref, wig_ref, q_ref, k_ref, xq_ref, xk_ref, mem_ref, mg_ref,
             o0, o1, o2, cw_out, wg_ref, gn_ref, km_ref, vm_ref, s0, s1, s2, cw, ocw, send, recv, lsem):
        ins, outs = (s0, s1, s2), (o0, o1, o2)
        for src, dst in zip((a0, a1, a2), ins):
            dst[...] = src[...].astype(dst.dtype)
        cw[...] = jnp.zeros(cw.shape, F32)
        cw[0:CONV_K, :] = cw_in[0]
        x, y, c = lax.axis_index("x"), lax.axis_index("y"), lax.axis_index("c")
        sibling = (x, y, 1 - c)
        near, far, diag = _partners(x, y, c)
        chips = [near, far, diag]
        me = _chip_of(x, y)

        def landed(p, chip, half):
            a, off, rows_ = pieces[p]
            r = ins[a].shape[0]
            return outs[a].at[pl.ds(pl.multiple_of(chip * r + half * (r // 2) + off, 16), rows_)]

        def mine(p):
            a, off, rows_ = pieces[p]
            return ins[a].at[pl.ds(pl.multiple_of(c * (ins[a].shape[0] // 2) + off, 16), rows_)]

        def copy(k, src, dst, to):
            return pltpu.make_async_remote_copy(src_ref=src, dst_ref=dst, send_sem=send.at[k], recv_sem=recv.at[k],
                                                device_id=to, device_id_type=MESH)

        def cw_rows(chip):
            return ocw.at[pl.ds(pl.multiple_of(chip * 8, 8), 8)]

        locals_ = []
        for a in range(n):
            r = ins[a].shape[0]
            locals_.append(pltpu.make_async_copy(ins[a], outs[a].at[pl.ds(pl.multiple_of(me * r, 16), r)], lsem.at[a]))
        locals_.append(pltpu.make_async_copy(cw, cw_rows(me), lsem.at[n]))
        for cp in locals_:
            cp.start()

        sent = []
        for p in range(npc):
            for j in range(2):
                sent.append(copy(p * 6 + j, mine(p), landed(p, me, c), (*chips[j], c)))
        for j, chip in enumerate(chips):
            sent.append(copy(npc * 6 + j, cw, cw_rows(me), (*chip, c)))
        for cp in sent:
            cp.start()

        gn_ref[...] = jnp.concatenate([q_ref[...]] * 2 + [k_ref[...]] * 2 + [xq_ref[...]] * 4 + [xk_ref[...]] * 4,
                                      axis=1)
        zeros = lambda lanes: [jnp.zeros((HEAD, lanes), F32)] if lanes else []
        for h in range(2):
            for b in range(4):
                row = []
                for w_ref in (wrg_ref, wig_ref):
                    row += zeros(HEAD * b) + [w_ref[0, 4 * h + b]] + zeros(HEAD * (3 - b))
                wg_ref[h, HEAD * b:HEAD * (b + 1), :] = jnp.concatenate(row, axis=1).astype(wg_ref.dtype)

        for j in range(3):
            for p in range(npc):
                got = landed(p, _chip_of(*chips[j]), c)
                copy(p * 6 + j, got, got, sibling).wait_recv()
                if j == 0:
                    sent.append(copy(p * 6 + 2, got, got, (*far, c)))
                    sent[-1].start()
                sent.append(copy(p * 6 + 3 + j, got, got, sibling))
                sent[-1].start()
        for p in range(npc):
            for j in range(3):
                got = landed(p, _chip_of(*chips[(1, 0, 2)[j]]), 1 - c)
                copy(p * 6 + 3 + j, got, got, sibling).wait_recv()
        for j, chip in enumerate(chips):
            got = cw_rows(_chip_of(*chip))
            copy(npc * 6 + j, got, got, (*chip, c)).wait_recv()
        for cp in sent:
            cp.wait_send()
        for cp in locals_:
            cp.wait()
        for chip in range(N_CHIPS):
            cw_out[:, 128 * chip:128 * (chip + 1)] = ocw[8 * chip:8 * chip + CONV_K, :]

        mem_v = mem_ref[...]
        mn = mem_v * lax.rsqrt(_row_mean(mem_v * mem_v) + EPS) * mg_ref[...]
        mkv = _mm(mn, o2[...])
        kpre = mkv[:, :XATT_W]
        km = kpre * lax.rsqrt(_seg_mean(kpre * kpre, _group_matrix(XATT_W)) + EPS) * gn_ref[:, G_XK:GAINS_W]
        km_ref[...] = _head_rows(km, 0.125).astype(km_ref.dtype)
        vm_ref[...] = _head_rows(mkv[:, XATT_W:], 1.0).astype(vm_ref.dtype)

    vm = pl.BlockSpec(memory_space=pltpu.VMEM)
    hbm = pl.BlockSpec(memory_space=pl.ANY)
    head_rows = jax.ShapeDtypeStruct((4 * MEM_LEN, XATT_W), _MXU)
    out_shape = tuple(jax.ShapeDtypeStruct((N_CHIPS * a.shape[0],) + a.shape[1:], _MXU) for a in arrs) + (
        jax.ShapeDtypeStruct((CONV_K, LRU_W), F32), jax.ShapeDtypeStruct((2, 256, 512), _MXU),
        jax.ShapeDtypeStruct((1, GAINS_W), F32), head_rows, head_rows)
    n_rdma = npc * 6 + 3
    return pl.pallas_call(
        body, name="gather_weights", out_shape=out_shape,
        in_specs=[vm] * 12, out_specs=(hbm, hbm, vm, vm, vm, vm, vm, vm),
        scratch_shapes=[pltpu.VMEM(a.shape, _MXU) for a in arrs] + [
            pltpu.VMEM((8, 128), F32), pltpu.VMEM((N_CHIPS * 8, 128), F32),
            pltpu.SemaphoreType.DMA((n_rdma,)), pltpu.SemaphoreType.DMA((n_rdma,)), pltpu.SemaphoreType.DMA((n + 1,))],
        compiler_params=pltpu.CompilerParams(vmem_limit_bytes=VMEM_LIMIT),
    )(win_t, wout, wkv, conv_w, w_rg, w_ig, *head_gains, mem, mem_g)


def _mem_bwd_and_pack(mem_ref, g_ref, w_ref, gn_ref, dkm_ref, dvm_ref, gg_ref, loss_ref, vectors, gw_ref, pk_ref):
    first_row = {name: (row, width) for name, row, width in SMALL_VECTORS}
    half_rows = SMALL_ROWS // 2
    pk_ref[...] = jnp.zeros(pk_ref.shape, F32)

    def rows_at(at, n):
        assert at // half_rows == (at + n - 1) // half_rows
        return at // half_rows, slice(at % half_rows, at % half_rows + n), slice(None)

    def put(name, src):
        row, width = first_row[name]
        per_row = 1 if width < 128 else src.shape[1] // 128
        for t in range(src.shape[0]):
            for r in range(per_row):
                pk_ref[rows_at(row + per_row * t + r, 1)] = src[t:t + 1, 128 * r:128 * (r + 1)]

    for name, ref in vectors.items():
        put(name, ref)
    pk_ref[rows_at(LOSS_ROW, 1)] = loss_ref[0:1, :]
    upper = lax.broadcasted_iota(jnp.int32, (HEAD, 128), 1) >= HEAD
    for h in range(2):
        for b in range(4):
            rg = gg_ref[h, HEAD * b:HEAD * (b + 1), 128 * (b // 2):128 * (b // 2 + 1)]
            ig = gg_ref[h, HEAD * b:HEAD * (b + 1), 256 + 128 * (b // 2):256 + 128 * (b // 2 + 1)]
            if b % 2:
                rg = pltpu.roll(rg, HEAD, axis=1)
            else:
                ig = pltpu.roll(ig, HEAD, axis=1)
            pk_ref[rows_at(GATES_ROW + HEAD * (4 * h + b), HEAD)] = jnp.where(upper, ig, rg)

    mem_v = mem_ref[...]
    mh = mem_v * lax.rsqrt(_row_mean(mem_v * mem_v) + EPS)
    mn = mh * g_ref[...]
    mkv = _mm(mn, w_ref[...])
    kpre = mkv[:, :XATT_W]
    gm = _group_matrix(XATT_W)
    rk = lax.rsqrt(_seg_mean(kpre * kpre, gm) + EPS)
    kn = kpre * rk
    dk = jnp.zeros((MEM_LEN, XATT_W), F32)
    dv = jnp.zeros((MEM_LEN, XATT_W), F32)
    for j in range(4):
        mj = _lane_mask(XATT_W, HEAD * j, HEAD * (j + 1))
        dk = dk + dkm_ref[:, MEM_LEN * j:MEM_LEN * (j + 1)].T * (mj * 0.125)
        dv = dv + dvm_ref[:, MEM_LEN * j:MEM_LEN * (j + 1)].T * mj
    put("xk_norm_g", _fold_heads(_col_sum(dk * kn)))
    dkn = dk * gn_ref[:, G_XK:GAINS_W]
    dkpre = rk * (dkn - kn * _seg_mean(dkn * kn, gm))
    dmkv = jnp.concatenate([dkpre, dv], axis=1)
    gw_ref[...] = _mm_tn(mn, dmkv).reshape(gw_ref.shape)
    dmn = _mm_nt(dmkv, w_ref[...])
    put("mem_norm_g", _col_sum(dmn * mh))


def layer_fwd(x, tgt, rc, rs1, rs2, ng, win_t, cw, cb, wg, brg, big, lam, gains, sinks, km, vm, og, wout):
    seq = x.shape[0]
    tm = min(ROW_TILE, seq)
    nt = seq // tm
    nb = tm // BLOCK

    def body(x_ref, t_ref, c_ref, s1_ref, s2_ref, ng_ref, win_ref, cw_ref, cb_ref, wg_ref, brg_ref, big_ref, lam_ref,
             gn_ref, sink_ref, km_ref, vm_ref, og_ref, wout_ref,
             proj_ref, ya_ref, yb_ref, yc_ref, ycat_ref, xn_ref, dout_ref, pswa_ref, pmem_ref, psink_ref, gates_ref,
             a_ref, loss_ref,
             ext_ref, b_scr, hc_ref, kp_ref, vp_ref, lacc_ref):
        i = pl.program_id(0)

        @pl.when(i == 0)
        def _():
            ext_ref[0:8, :] = jnp.zeros((8, LRU_W), F32)
            hc_ref[...] = jnp.zeros_like(hc_ref)
            kp_ref[...] = jnp.zeros_like(kp_ref)
            vp_ref[...] = jnp.zeros_like(vp_ref)
            lacc_ref[...] = jnp.zeros_like(lacc_ref)

        xv = x_ref[...]
        xn = (xv * lax.rsqrt(_row_mean(xv * xv) + EPS) * ng_ref[...]).astype(_MXU)
        xn_ref[...] = xn.astype(xn_ref.dtype)
        proj_ref[...] = _mm_nt(xn, win_ref[...])

        u = proj_ref[:, C_LRUX:C_LRUX + LRU_W]
        ext_ref[8:8 + tm, :] = u
        xc = cb_ref[...]
        for k in range(CONV_K):
            xc = xc + cw_ref[k:k + 1, :] * ext_ref[pl.ds(5 + k, tm), :]
        ext_ref[0:8, :] = u[tm - 8:tm, :]
        rg, ig, sp, a, sq = _lru_gates(xc, wg_ref, brg_ref[...], big_ref[...], lam_ref[...])
        for k, t in enumerate((xc, rg, ig, sq)):
            gates_ref[:, LRU_W * k:LRU_W * (k + 1)] = t.astype(gates_ref.dtype)
        a_ref[...] = a
        b_scr[...] = sq * (ig * xc)
        row8 = lax.broadcasted_iota(jnp.int32, (8, LRU_W), 0)

        def scan_step(g, carry):
            r0 = pl.multiple_of(g * 8, 8)
            av = a_ref[pl.ds(r0, 8), :]
            bv = b_scr[pl.ds(r0, 8), :]
            for d in (1, 2, 4):
                a_sh = jnp.where(row8 >= d, pltpu.roll(av, d, 0), 1.0)
                b_sh = jnp.where(row8 >= d, pltpu.roll(bv, d, 0), 0.0)
                bv = bv + av * b_sh
                av = av * a_sh
            hv = bv + av * carry
            ya_ref[pl.ds(r0, 8), :] = hv
            return hv[7:8, :]

        hc_ref[0:1, :] = lax.fori_loop(0, tm // 8, scan_step, hc_ref[0:1, :], unroll=True)

        gm128 = _group_matrix(KV_W)
        cv, s1v, s2v = c_ref[...], s1_ref[...], s2_ref[...]

        def head_norm_rope(t, g):
            n = t * lax.rsqrt(_seg_mean(t * t, gm128) + EPS)
            return _rope(n * g, cv, s1v, s2v)

        qs_ = (head_norm_rope(proj_ref[:, C_SQ:C_SQ + 128], gn_ref[:, G_Q:G_K]).astype(_MXU),
               head_norm_rope(proj_ref[:, C_SQ + 128:C_SQ + 256], gn_ref[:, G_Q:G_K]).astype(_MXU))
        kr = head_norm_rope(proj_ref[:, C_SK:C_SK + KV_W], gn_ref[:, G_K:G_XQ])
        sv = proj_ref[:, C_SV:C_SV + KV_W]
        ka = _place_kv(jnp.concatenate([kp_ref[...], kr], axis=0), 0.125)
        va = _place_kv(jnp.concatenate([vp_ref[...], sv], axis=0), 1.0)
        kp_ref[...] = kr[tm - BLOCK:tm, :]
        vp_ref[...] = sv[tm - BLOCK:tm, :]
        lane128 = lax.broadcasted_iota(jnp.int32, (1, 128), 1)
        for b in range(nb):
            mask = _swa_mask((i == 0) & (b == 0)) if b == 0 else _swa_mask(False)
            band = slice(BLOCK * b, BLOCK * b + 2 * BLOCK)
            blk = slice(BLOCK * b, BLOCK * (b + 1))
            psink = jnp.zeros((BLOCK, 128), F32)
            for j in range(4):
                p, pk = _swa_probs(qs_[j // 2][blk], ka[j][band], mask, sink_ref[0, j])
                pswa_ref[blk, 2 * BLOCK * j:2 * BLOCK * (j + 1)] = p.astype(pswa_ref.dtype)
                psink = jnp.where(lane128 == j, pk, psink)
            psink_ref[blk, :] = psink
            for h in range(2):
                yb_ref[blk, KV_W * h:KV_W * (h + 1)] = _mm(
                    pswa_ref[blk, 4 * BLOCK * h:4 * BLOCK * (h + 1)],
                    jnp.concatenate([va[2 * h][band], va[2 * h + 1][band]], axis=0))

        gm256 = _group_matrix(XATT_W)
        xq = proj_ref[:, C_XQ:C_XQ + XATT_W]
        qx = xq * lax.rsqrt(_seg_mean(xq * xq, gm256) + EPS) * gn_ref[:, G_XQ:G_XK]
        pm = _mem_probs(_mm_nt(qx, km_ref[...]))
        for j in range(4):
            pmem_ref[:, MEM_LEN * j:MEM_LEN * (j + 1)] = pm[j].astype(pmem_ref.dtype)
        yc = _mm(pmem_ref[...], vm_ref[...])
        yc_ref[...] = yc

        def gated(y, g, gate):
            return y * lax.rsqrt(_row_mean(y * y) + EPS) * g * (gate * _sigmoid(gate))

        ogv = og_ref[...]
        za = gated(ya_ref[...], ogv[:, :512], proj_ref[:, C_LRUG:C_LRUG + LRU_W])
        zb = gated(yb_ref[...], ogv[:, 512:768], proj_ref[:, C_SWAG:C_SWAG + SWA_W])
        zc = gated(yc, ogv[:, 768:], proj_ref[:, C_XG:C_XG + XATT_W])
        ycat_ref[:, 0:512] = za.astype(ycat_ref.dtype)
        ycat_ref[:, 512:768] = zb.astype(ycat_ref.dtype)
        ycat_ref[:, 768:1024] = zc.astype(ycat_ref.dtype)
        out = xv + _mm(ycat_ref[...], wout_ref[...])
        err = out - t_ref[...]
        dout_ref[...] = (err * (1.0 / D_MODEL)).astype(dout_ref.dtype)
        lacc_ref[...] = lacc_ref[...] + (0.5 / D_MODEL) * jnp.sum(err * err)

        @pl.when(i == nt - 1)
        def _():
            loss_ref[...] = lacc_ref[...]

    def rows(ncol):
        return pl.BlockSpec((tm, ncol), lambda i: (i, 0))

    in_specs = [rows(D_MODEL), rows(D_MODEL), rows(128), rows(128), rows(128),
                _const_spec((1, D_MODEL)), _const_spec((D_IN, D_MODEL), True), _const_spec((CONV_K, LRU_W)),
                _const_spec((1, LRU_W)), _const_spec((2, 256, 512), True), _const_spec((1, LRU_W)),
                _const_spec((1, LRU_W)), _const_spec((1, LRU_W)), _const_spec((1, GAINS_W)), pl.BlockSpec(memory_space=pltpu.SMEM),
                _const_spec((4 * MEM_LEN, XATT_W), True), _const_spec((4 * MEM_LEN, XATT_W), True),
                _const_spec((1, D_MODEL)), _const_spec((D_MODEL, D_MODEL), True)]
    out_shape = (jax.ShapeDtypeStruct((seq, D_IN), F32), jax.ShapeDtypeStruct((seq, LRU_W), F32),
                 jax.ShapeDtypeStruct((seq, SWA_W), F32), jax.ShapeDtypeStruct((seq, XATT_W), F32),
                 jax.ShapeDtypeStruct((seq, D_MODEL), _MXU), jax.ShapeDtypeStruct((seq, D_MODEL), _MXU),
                 jax.ShapeDtypeStruct((seq, D_MODEL), _MXU), jax.ShapeDtypeStruct((seq, 4 * 2 * BLOCK), _MXU),
                 jax.ShapeDtypeStruct((seq, 4 * MEM_LEN), _MXU), jax.ShapeDtypeStruct((seq, 128), F32),
                 jax.ShapeDtypeStruct((seq, 4 * LRU_W), _MXU), jax.ShapeDtypeStruct((seq, LRU_W), F32),
                 jax.ShapeDtypeStruct((8, 128), F32))
    out_specs = (rows(D_IN), rows(LRU_W), rows(SWA_W), rows(XATT_W), rows(D_MODEL), rows(D_MODEL), rows(D_MODEL),
                 rows(4 * 2 * BLOCK), rows(4 * MEM_LEN), rows(128), rows(4 * LRU_W), rows(LRU_W),
                 _const_spec((8, 128)))
    scratch = [pltpu.VMEM((tm + 8, LRU_W), F32), pltpu.VMEM((tm, LRU_W), F32),
               pltpu.VMEM((8, LRU_W), F32), pltpu.VMEM((BLOCK, KV_W), F32), pltpu.VMEM((BLOCK, KV_W), F32),
               pltpu.VMEM((8, 128), F32)]
    return pl.pallas_call(
        body, name="layer_fwd", grid=(nt,), out_shape=out_shape, in_specs=in_specs, out_specs=out_specs,
        scratch_shapes=scratch,
        compiler_params=pltpu.CompilerParams(dimension_semantics=("arbitrary",), vmem_limit_bytes=VMEM_LIMIT),
    )(x, tgt, rc, rs1, rs2, ng, win_t, cw, cb, wg, brg, big, lam, gains, sinks, km, vm, og, wout)


def weight_grads(ycat, dout, dproj, xn, mem_operands, vectors, early_at, late_at):
    seq, ncol = xn.shape
    blk = 256
    n_out, n_in = ycat.shape[1] // blk, dproj.shape[1] // blk
    assert n_out == N_CHIPS and late_at[0] >= n_out
    g_out = jax.ShapeDtypeStruct((N_CHIPS, 2, blk // 2, dout.shape[1]), F32)
    g_kv = jax.ShapeDtypeStruct((N_CHIPS, 2, D_MODEL // 8, 2 * XATT_W), F32)
    g_small = jax.ShapeDtypeStruct((2, SMALL_ROWS // 2, 128), F32)
    shape_e, scratch_e = _hosted_reduce_shapes([g_kv], g_small)
    shape_l, scratch_l = _hosted_reduce_shapes([g_out], None)
    n_mem, names = len(mem_operands), tuple(vectors)

    def body(l1_ref, r1_ref, l2_ref, r2_hbm, *refs):
        mem_refs, vec_refs = refs[:n_mem], refs[n_mem:n_mem + len(names)]
        o_ref, sum_out, sum_kv, sum_sm, gout_scr, gkv_scr, pack_scr, r2_ref, r2_sem, *scratch = refs[n_mem + len(names):]
        j = pl.program_id(0)
        r2_copy = pltpu.make_async_copy(r2_hbm, r2_ref, r2_sem.at[0])

        @pl.when(j == 0)
        def _():
            r2_copy.start()
            _mem_bwd_and_pack(*mem_refs, dict(zip(names, vec_refs)), gkv_scr, pack_scr)

        def reduce_stages(closing):
            _hosted_reduce(j, n_out + n_in, closing, early_at, (gkv_scr, pack_scr), (sum_kv, sum_sm),
                           scratch[:len(scratch_e)], True)
            _hosted_reduce(j, n_out + n_in, closing, late_at, (gout_scr,), (sum_out,), scratch[len(scratch_e):], False)

        reduce_stages(False)

        @pl.when(j < n_out)
        def _():
            gout_scr[j] = _mm_tn(l1_ref[...], r1_ref[...]).reshape(g_out.shape[1:])

        pl.when(j == n_out)(r2_copy.wait)

        @pl.when(j >= n_out)
        def _():
            o_ref[...] = _mm_tn(l2_ref[...], r2_ref[...])

        reduce_stages(True)

    vm = pl.BlockSpec(memory_space=pltpu.VMEM)
    hbm = pl.BlockSpec(memory_space=pl.ANY)
    return pl.pallas_call(
        body, name="weight_grads", grid=(n_out + n_in,),
        out_shape=(jax.ShapeDtypeStruct((dproj.shape[1], ncol), F32), *shape_l, *shape_e),
        in_specs=[pl.BlockSpec((seq, blk), lambda j: (0, jnp.minimum(j, n_out - 1))), _const_spec(dout.shape, True),
                  pl.BlockSpec((seq, blk), lambda j: (0, jnp.maximum(j - n_out, 0))), hbm]
        + [vm] * (n_mem + len(names)),
        out_specs=(pl.BlockSpec((blk, ncol), lambda j: (jnp.maximum(j - n_out, 0), 0)), hbm, hbm, hbm),
        scratch_shapes=[pltpu.VMEM(s.shape, F32) for s in (g_out, g_kv, g_small)]
        + [pltpu.VMEM(xn.shape, xn.dtype), pltpu.SemaphoreType.DMA((1,))] + scratch_e + scratch_l,
        compiler_params=pltpu.CompilerParams(dimension_semantics=("arbitrary",), vmem_limit_bytes=VMEM_LIMIT),
    )(ycat, dout, dproj, xn, *mem_operands, *vectors.values())


def layer_bwd(x, dout, proj, ya, yb, yc, pswa, pmem, psink, gates, a_all, rc, rs1, rs2, ng, win_t, cw, wg, lam, gains,
              km, vm, og, wout):
    seq = x.shape[0]
    tm = min(ROW_TILE, seq)
    nt = seq // tm
    nb = tm // BLOCK

    def body(x_ref, dout_ref, proj_ref, ya_ref, yb_ref, yc_ref, pswa_ref, pmem_ref, psink_ref, gates_ref, a_ref,
             c_ref, s1_ref, s2_ref,
             yah_ref, kvh_ref, ch_ref, s1h_ref, s2h_ref,
             ng_ref, win_ref, cw_ref, wg_ref, lam_ref, gn_ref, km_ref, vm_ref, og_ref, wout_ref,
             gx_ref, dproj_ref, gwg_ref, dkm_ref, dvm_ref, gng_ref, gog_ref, gcb_ref, gbrg_ref, gbig_ref, glam_ref,
             gcw_ref, gqn_ref, gkn_ref, gxqn_ref, gsink_ref,
             hext_ref, aext_ref, an_scr, dh_scr, g_scr, dxc_ext, gcar_ref, dkcar_ref, dvcar_ref):
        i = pl.program_id(0)
        tile = nt - 1 - i
        first_tile = tile == 0

        @pl.when(i == 0)
        def _():
            for r in (gwg_ref, dkm_ref, dvm_ref, gng_ref, gog_ref, gcb_ref, gbrg_ref, gbig_ref, glam_ref, gcw_ref,
                      gqn_ref, gkn_ref, gxqn_ref, gsink_ref, gcar_ref, dkcar_ref, dvcar_ref):
                r[...] = jnp.zeros_like(r)
            dxc_ext[tm:tm + 8, :] = jnp.zeros((8, LRU_W), F32)
            aext_ref[tm:tm + 8, :] = jnp.zeros((8, LRU_W), F32)

        xv = x_ref[...]
        dov = dout_ref[...]
        dz = _mm_nt(dov, wout_ref[...])
        ogv = og_ref[...]

        def group_bwd(y, gate, g, dzg):
            r = lax.rsqrt(_row_mean(y * y) + EPS)
            n = y * r
            sg = _sigmoid(gate)
            dgate = dzg * (n * g) * (sg * (1.0 + gate * (1.0 - sg)))
            dng = dzg * (gate * sg)
            dn = dng * g
            return r * (dn - n * _row_mean(dn * n)), dgate, _col_sum(dng * n)

        dya, dga, goa = group_bwd(ya_ref[...], proj_ref[:, C_LRUG:C_LRUG + LRU_W], ogv[:, :512], dz[:, :512])
        dyb, dgb, gob = group_bwd(yb_ref[...], proj_ref[:, C_SWAG:C_SWAG + SWA_W], ogv[:, 512:768], dz[:, 512:768])
        dyc, dgc, goc = group_bwd(yc_ref[...], proj_ref[:, C_XG:C_XG + XATT_W], ogv[:, 768:], dz[:, 768:])
        gog_ref[...] += jnp.concatenate([goa, gob, goc], axis=1)
        dproj_ref[:, C_LRUG:C_LRUG + LRU_W] = dga.astype(dproj_ref.dtype)
        dproj_ref[:, C_SWAG:C_SWAG + SWA_W] = dgb.astype(dproj_ref.dtype)
        dproj_ref[:, C_XG:C_XG + XATT_W] = dgc.astype(dproj_ref.dtype)

        gm256 = _group_matrix(XATT_W)
        xq = proj_ref[:, C_XQ:C_XQ + XATT_W]
        rq = lax.rsqrt(_seg_mean(xq * xq, gm256) + EPS)
        qn = xq * rq
        qx = qn * gn_ref[:, G_XQ:G_XK]
        qxb = qx.astype(_MXU)
        dycb = dyc.astype(_MXU)
        dp_all = _mm_nt(dycb, vm_ref[...])
        dsm = []
        for j in range(4):
            pj = pmem_ref[:, MEM_LEN * j:MEM_LEN * (j + 1)].astype(F32)
            dp = dp_all[:, MEM_LEN * j:MEM_LEN * (j + 1)]
            dsm.append((pj * (dp - jnp.sum(pj * dp, axis=-1, keepdims=True))).astype(_MXU))
        ds_all = jnp.concatenate(dsm, axis=1)
        dvm_ref[...] += _mm_tn(dycb, pmem_ref[...])
        dkm_ref[...] += _mm_tn(qxb, ds_all)
        dqx = _mm(ds_all, km_ref[...])
        gxqn_ref[...] += _col_sum(dqx * qn)
        dqn = dqx * gn_ref[:, G_XQ:G_XK]
        dproj_ref[:, C_XQ:C_XQ + XATT_W] = (rq * (dqn - qn * _seg_mean(dqn * qn, gm256))).astype(dproj_ref.dtype)

        gm128 = _group_matrix(KV_W)
        cv, s1v, s2v = c_ref[...], s1_ref[...], s2_ref[...]

        def head_norm(t):
            r = lax.rsqrt(_seg_mean(t * t, gm128) + EPS)
            return t * r, r

        qn_, qr_ = zip(head_norm(proj_ref[:, C_SQ:C_SQ + 128]), head_norm(proj_ref[:, C_SQ + 128:C_SQ + 256]))
        qrope = [_rope(qn_[h] * gn_ref[:, G_Q:G_K], cv, s1v, s2v).astype(_MXU) for h in range(2)]
        kn, krr = head_norm(proj_ref[:, C_SK:C_SK + KV_W])
        kr = _rope(kn * gn_ref[:, G_K:G_XQ], cv, s1v, s2v)
        khn, _ = head_norm(kvh_ref[:, 0:KV_W])
        khr = _rope(khn * gn_ref[:, G_K:G_XQ], ch_ref[...], s1h_ref[...], s2h_ref[...])
        ka = _place_kv(jnp.concatenate([khr, kr], axis=0), 0.125)
        va = _place_kv(jnp.concatenate([kvh_ref[:, KV_W:2 * KV_W], proj_ref[:, C_SV:C_SV + KV_W]], axis=0), 1.0)
        lane128 = lax.broadcasted_iota(jnp.int32, (1, 128), 1)
        gsink = jnp.zeros((1, 128), F32)
        dk_band, dv_band, dq_blk = [], [], []
        for b in range(nb):
            band = slice(BLOCK * b, BLOCK * b + 2 * BLOCK)
            blk = slice(BLOCK * b, BLOCK * (b + 1))
            dka, dva, dsb = [], [], []
            deltas = jnp.zeros((BLOCK, 128), F32)
            for j in range(4):
                qh = qrope[j // 2][blk]
                doh = dyb[blk, KV_W * (j // 2):KV_W * (j // 2 + 1)].astype(_MXU)
                pb = pswa_ref[blk, 2 * BLOCK * j:2 * BLOCK * (j + 1)]
                p = pb.astype(F32)
                dp = _mm_nt(doh, va[j][band])
                delta = jnp.sum(p * dp, axis=-1, keepdims=True)
                ds = (p * (dp - delta)).astype(_MXU)
                deltas = jnp.where(lane128 == j, delta, deltas)
                dva.append(_mm_tn(pb, doh))
                dka.append(_mm_tn(ds, qh))
                dsb.append(ds)
            gsink = gsink - _col_sum(psink_ref[blk, :] * deltas)
            dk_band.append(_unplace_kv(dka) * 0.125)
            dv_band.append(_unplace_kv(dva))
            dq_blk.append([_mm(jnp.concatenate(dsb[2 * h:2 * h + 2], axis=1),
                               jnp.concatenate([ka[2 * h][band], ka[2 * h + 1][band]], axis=0)) for h in range(2)])
        gsink_ref[...] += gsink
        dk_rows = [dk_band[b][BLOCK:] + (dk_band[b + 1][:BLOCK] if b + 1 < nb else dkcar_ref[...]) for b in range(nb)]
        dv_rows = [dv_band[b][BLOCK:] + (dv_band[b + 1][:BLOCK] if b + 1 < nb else dvcar_ref[...]) for b in range(nb)]
        dkcar_ref[...] = dk_band[0][:BLOCK]
        dvcar_ref[...] = dv_band[0][:BLOCK]
        dkg = _rope_bwd(jnp.concatenate(dk_rows, axis=0), cv, s1v, s2v)
        gkn = _col_sum(dkg * kn)
        dkn = dkg * gn_ref[:, G_K:G_XQ]
        dproj_ref[:, C_SK:C_SK + KV_W] = (krr * (dkn - kn * _seg_mean(dkn * kn, gm128))).astype(dproj_ref.dtype)
        dproj_ref[:, C_SV:C_SV + KV_W] = jnp.concatenate(dv_rows, axis=0).astype(dproj_ref.dtype)
        gqn = jnp.zeros((1, 128), F32)
        for h in range(2):
            dqg = _rope_bwd(jnp.concatenate([dq_blk[b][h] for b in range(nb)], axis=0), cv, s1v, s2v)
            gqn = gqn + _col_sum(dqg * qn_[h])
            dqn_ = dqg * gn_ref[:, G_Q:G_K]
            dproj_ref[:, C_SQ + 128 * h:C_SQ + 128 * (h + 1)] = (
                qr_[h] * (dqn_ - qn_[h] * _seg_mean(dqn_ * qn_[h], gm128))).astype(dproj_ref.dtype)
        gqn_ref[...] += gqn
        gkn_ref[...] += gkn

        u = proj_ref[:, C_LRUX:C_LRUX + LRU_W]
        xc, rg, ig, sq = (gates_ref[:, LRU_W * k:LRU_W * (k + 1)].astype(F32) for k in range(4))
        a = a_ref[...]
        sp = _softplus(-lam_ref[...])
        hext_ref[0:8, :] = jnp.where(first_tile, 0.0, yah_ref[...])
        hext_ref[8:8 + tm, :] = ya_ref[...]
        hprev = hext_ref[pl.ds(7, tm), :]
        aext_ref[0:tm, :] = a
        an_scr[...] = aext_ref[pl.ds(1, tm), :]
        dh_scr[...] = dya
        dh_scr[tm - 1:tm, :] = dh_scr[tm - 1:tm, :] + gcar_ref[0:1, :]
        row8 = lax.broadcasted_iota(jnp.int32, (8, LRU_W), 0)

        def scan_step(gi, carry):
            r0 = pl.multiple_of((tm // 8 - 1 - gi) * 8, 8)
            av = an_scr[pl.ds(r0, 8), :]
            bv = dh_scr[pl.ds(r0, 8), :]
            for d in (1, 2, 4):
                a_sh = jnp.where(row8 < 8 - d, pltpu.roll(av, 8 - d, 0), 1.0)
                b_sh = jnp.where(row8 < 8 - d, pltpu.roll(bv, 8 - d, 0), 0.0)
                bv = bv + av * b_sh
                av = av * a_sh
            gv = bv + av * carry
            g_scr[pl.ds(r0, 8), :] = gv
            return gv[0:1, :]

        g0 = lax.fori_loop(0, tm // 8, scan_step, jnp.zeros((1, LRU_W), F32), unroll=True)
        gcar_ref[0:1, :] = a[0:1, :] * g0
        gv = g_scr[...]
        da = gv * hprev
        dig = gv * sq * xc
        dxc = gv * sq * ig
        dla = da * a - gv * (ig * xc) * ((a * a) / sq)
        drg = dla * ((-LRU_C) * sp)
        glam_ref[...] += _col_sum(dla * rg)
        dpr = drg * rg * (1.0 - rg)
        dpi = dig * ig * (1.0 - ig)
        gbrg_ref[...] += _col_sum(dpr)
        gbig_ref[...] += _col_sum(dpi)
        dpre0 = jnp.concatenate([dpr[:, :256], dpi[:, :256]], axis=1).astype(_MXU)
        dpre1 = jnp.concatenate([dpr[:, 256:], dpi[:, 256:]], axis=1).astype(_MXU)
        gwg_ref[0] += _mm_tn(xc[:, :256], dpre0)
        gwg_ref[1] += _mm_tn(xc[:, 256:], dpre1)
        dxc = dxc + jnp.concatenate([_mm_nt(dpre0, wg_ref[0]), _mm_nt(dpre1, wg_ref[1])], axis=1)
        gcb_ref[...] += _col_sum(dxc)
        dxc_ext[0:tm, :] = dxc
        du = jnp.zeros((tm, LRU_W), F32)
        for k in range(CONV_K):
            later = dxc_ext[pl.ds(3 - k, tm), :]
            gcw_ref[k:k + 1, :] += _col_sum(later * u)
            du = du + cw_ref[k:k + 1, :] * later
        dxc_ext[tm:tm + 8, :] = dxc[0:8, :]
        dproj_ref[:, C_LRUX:C_LRUX + LRU_W] = du.astype(dproj_ref.dtype)

        dxn = _mm(dproj_ref[...], win_ref[...])
        rx = lax.rsqrt(_row_mean(xv * xv) + EPS)
        xh = xv * rx
        gng_ref[...] += _col_sum(dxn * xh)
        dxh = dxn * ng_ref[...]
        gx_ref[...] = dov.astype(F32) + rx * (dxh - xh * _row_mean(dxh * xh))

        @pl.when(i == nt - 1)
        def _():
            glam_ref[...] = glam_ref[...] * (LRU_C * _sigmoid(-lam_ref[...]))
            for r in (gqn_ref, gkn_ref, gxqn_ref):
                r[...] = _fold_heads(r[...])

    def rows(ncol, arr_cols_block=0):
        return pl.BlockSpec((tm, ncol), lambda i: (nt - 1 - i, arr_cols_block))

    def halo(nrow, ncol, colblk=0):
        per = tm // nrow
        return pl.BlockSpec((nrow, ncol), lambda i: (jnp.maximum((nt - 1 - i) * per - 1, 0), colblk))

    in_specs = [rows(D_MODEL), rows(D_MODEL), rows(D_IN), rows(LRU_W), rows(SWA_W), rows(XATT_W),
                rows(4 * 2 * BLOCK), rows(4 * MEM_LEN), rows(128), rows(4 * LRU_W), rows(LRU_W),
                rows(128), rows(128), rows(128),
                halo(8, LRU_W), halo(BLOCK, 2 * KV_W, C_SK // (2 * KV_W)),
                halo(BLOCK, 128), halo(BLOCK, 128), halo(BLOCK, 128),
                _const_spec((1, D_MODEL)), _const_spec((D_IN, D_MODEL), True), _const_spec((CONV_K, LRU_W)),
                _const_spec((2, 256, 512), True), _const_spec((1, LRU_W)), _const_spec((1, GAINS_W)),
                _const_spec((4 * MEM_LEN, XATT_W), True), _const_spec((4 * MEM_LEN, XATT_W), True),
                _const_spec((1, D_MODEL)), _const_spec((D_MODEL, D_MODEL), True)]
    small = [(2, 256, 512), (XATT_W, 4 * MEM_LEN), (XATT_W, 4 * MEM_LEN), (1, D_MODEL), (1, D_MODEL), (1, LRU_W), (1, LRU_W),
             (1, LRU_W), (1, LRU_W), (CONV_K, LRU_W), (1, 128), (1, 128), (1, XATT_W), (1, 128)]
    out_shape = (jax.ShapeDtypeStruct((seq, D_MODEL), F32), jax.ShapeDtypeStruct((seq, D_IN), _MXU)) + tuple(
        jax.ShapeDtypeStruct(s, F32) for s in small)
    out_specs = (rows(D_MODEL), rows(D_IN)) + tuple(_const_spec(s) for s in small)
    scratch = [pltpu.VMEM((tm + 8, LRU_W), F32), pltpu.VMEM((tm + 8, LRU_W), F32),
               pltpu.VMEM((tm, LRU_W), F32), pltpu.VMEM((tm, LRU_W), F32), pltpu.VMEM((tm, LRU_W), F32),
               pltpu.VMEM((tm + 8, LRU_W), F32),
               pltpu.VMEM((8, LRU_W), F32), pltpu.VMEM((BLOCK, KV_W), F32), pltpu.VMEM((BLOCK, KV_W), F32)]
    return pl.pallas_call(
        body, name="layer_bwd", grid=(nt,), out_shape=out_shape, in_specs=in_specs, out_specs=out_specs,
        scratch_shapes=scratch,
        compiler_params=pltpu.CompilerParams(dimension_semantics=("arbitrary",), vmem_limit_bytes=VMEM_LIMIT),
    )(x, dout, proj, ya, yb, yc, pswa, pmem, psink, gates, a_all, rc, rs1, rs2, ya, proj, rc, rs1, rs2,
      ng, win_t, cw, wg, lam, gains, km, vm, og, wout)


def _reduce_protocol(big, sm, outs, osm, r1, r1s, wire, r2, r2s, wire2, ps, own, send, recv, lsem):
    nbig = len(big)
    x, y, c = lax.axis_index("x"), lax.axis_index("y"), lax.axis_index("c")
    sibling = (x, y, 1 - c)
    near, far, diag = _partners(x, y, c)
    me, near_id, far_id, diag_id = _chip_of(x, y), _chip_of(*near), _chip_of(*far), _chip_of(*diag)

    def copy(k, src, dst, to):
        return pltpu.make_async_remote_copy(src_ref=src, dst_ref=dst, send_sem=send.at[k], recv_sem=recv.at[k],
                                            device_id=to, device_id_type=MESH)

    def sent(stage, a):
        if a == nbig:
            src, dst, to = ((sm.at[1 - c], r1s, sibling), (r1s, r2s.at[0], (*near, c)), (ps, r2s.at[1], (*far, c)),
                            (osm.at[c], osm.at[c], sibling))[stage]
            return [copy(5 * nbig + stage, src, dst, to)]
        if stage == 0:
            return [copy(5 * a, big[a].at[:, 1 - c], r1[a], sibling)]
        if stage == 1:
            return [copy(5 * a + 1, wire[a].at[near_id], r2[a].at[0], (*near, c)),
                    copy(5 * a + 2, wire[a].at[diag_id], r2[a].at[1], (*near, c))]
        if stage == 2:
            return [copy(5 * a + 3, wire2[a], r2[a].at[2], (*far, c))]
        return [copy(5 * a + 4, outs[a].at[c], outs[a].at[c], sibling)]

    arrays = range(nbig + (sm is not None))

    def start(stage, a):
        for cp in sent(stage, a):
            cp.start()

    def arrived(k, ref):
        copy(k, ref, ref, sibling).wait_recv()

    def loads():
        return [pltpu.make_async_copy(big[a].at[:, c], own[a], lsem.at[a]) for a in range(nbig)]

    def stage0():
        for a in arrays:
            start(0, a)
        for cp in loads():
            cp.start()

    def stage1():
        for a in range(nbig):
            loads()[a].wait()
            arrived(5 * a, r1[a])
            for k in range(N_CHIPS):
                r1[a][k] = own[a][k] + r1[a][k]
                wire[a][k] = r1[a][k].astype(wire[a].dtype)
            start(1, a)
        if sm is not None:
            arrived(5 * nbig, r1s)
            r1s[...] = sm[c] + r1s[...]
            start(1, nbig)

    def stage2():
        for a in range(nbig):
            arrived(5 * a + 1, r2[a].at[0])
            arrived(5 * a + 2, r2[a].at[1])
            r1[a][me] = r1[a][me] + r2[a][0].astype(F32)
            wire2[a][...] = (r1[a][far_id] + r2[a][1].astype(F32)).astype(wire2[a].dtype)
            start(2, a)
        if sm is not None:
            arrived(5 * nbig + 1, r2s.at[0])
            ps[...] = r1s[...] + r2s[0]
            start(2, nbig)

    def stage3():
        for a in range(nbig):
            arrived(5 * a + 3, r2[a].at[2])
            outs[a][c] = r1[a][me] + r2[a][2].astype(F32)
            start(3, a)
        if sm is not None:
            arrived(5 * nbig + 2, r2s.at[1])
            osm[c] = ps[...] + r2s[1]
            start(3, nbig)

    def stage4():
        for a in range(nbig):
            arrived(5 * a + 4, outs[a].at[1 - c])
        if sm is not None:
            arrived(5 * nbig + 3, osm.at[1 - c])
        for stage in range(4):
            for a in arrays:
                for cp in sent(stage, a):
                    cp.wait_send()

    return [stage0, stage1, stage2, stage3, stage4]


def _reduce_buffers(bigs, g_small):
    half = [b.shape[2:] for b in bigs]
    sm_half = None if g_small is None else g_small.shape[1:]
    out_shape = [jax.ShapeDtypeStruct((2,) + h, F32) for h in half]
    small = lambda lead: [] if g_small is None else [pltpu.VMEM(lead + sm_half, F32)]
    if g_small is not None:
        out_shape.append(jax.ShapeDtypeStruct(g_small.shape, F32))
    n_sem = 5 * len(bigs) + 4
    scratch = ([pltpu.VMEM((N_CHIPS,) + h, F32) for h in half] + small(())
               + [pltpu.VMEM((N_CHIPS,) + h, _WIRE) for h in half]
               + [pltpu.VMEM((3,) + h, _WIRE) for h in half] + small((2,))
               + [pltpu.VMEM(h, _WIRE) for h in half] + small(())
               + [pltpu.VMEM((N_CHIPS,) + h, F32) for h in half]
               + [pltpu.SemaphoreType.DMA((n_sem,)), pltpu.SemaphoreType.DMA((n_sem,)),
                  pltpu.SemaphoreType.DMA((len(bigs),))])
    return out_shape, scratch


def _split_reduce_refs(refs, nbig, has_small):
    it = iter(refs)
    take = lambda n: [next(it) for _ in range(n)]
    one = lambda: next(it) if has_small else None
    big, sm = take(nbig), one()
    outs, osm = take(nbig), one()
    r1, r1s, wire, r2, r2s, wire2, ps, own = take(nbig), one(), take(nbig), take(nbig), one(), take(nbig), one(), take(nbig)
    send, recv, lsem = take(3)
    return big, sm, outs, osm, r1, r1s, wire, r2, r2s, wire2, ps, own, send, recv, lsem


def _hosted_reduce_shapes(bigs, g_small):
    red_shape, scratch = _reduce_buffers(bigs, g_small)
    nres = len(red_shape)
    return red_shape, [pltpu.VMEM(r.shape, r.dtype) for r in red_shape] + scratch + [pltpu.SemaphoreType.DMA((nres,))]


def _hosted_reduce(step, n_steps, closing, stage_at, operands, results, scratch, has_small):
    nres = len(results)
    sums, rest, fsem = scratch[:nres], scratch[nres:-1], scratch[-1]
    refs = tuple(operands) + tuple(sums) + tuple(rest)

    def to_results():
        out = [pltpu.make_async_copy(sums[k], results[k], fsem.at[k]) for k in range(nres)]
        for cp in out:
            cp.start()
        for cp in out:
            cp.wait()

    stages = _reduce_protocol(*_split_reduce_refs(refs, nres - has_small, has_small))

    def last_stage():
        stages[-1]()
        to_results()

    for at, stage in zip(stage_at, stages[:-1] + [last_stage]):
        if closing == (at == n_steps):
            pl.when(step == min(at, n_steps - 1))(stage)


def reduce_grads(big, name, parts):
    chips, halves, rows_, cols = big.shape
    sub = jax.ShapeDtypeStruct((chips, halves, rows_ // parts, cols), big.dtype)

    def body(b_ref, o_ref, *scratch):
        refs = [b_ref.at[:, :, s] for s in range(parts)] + [o_ref.at[:, s] for s in range(parts)] + list(scratch)
        for stage in _reduce_protocol(*_split_reduce_refs(refs, parts, False)):
            stage()

    _, scratch = _reduce_buffers([sub] * parts, None)
    return pl.pallas_call(
        body, name=name, out_shape=jax.ShapeDtypeStruct((halves, parts, rows_ // parts, cols), F32),
        in_specs=[pl.BlockSpec(memory_space=pl.ANY)], out_specs=pl.BlockSpec(memory_space=pltpu.VMEM),
        scratch_shapes=scratch, compiler_params=pltpu.CompilerParams(vmem_limit_bytes=VMEM_LIMIT),
    )(big.reshape(chips, halves, parts, rows_ // parts, cols))


def adamw(items, g_pack, ws, ms, vs):
    blocks = []
    for k, (w, _, _, _) in enumerate(items):
        rows_, cols = w.shape
        tr = max(t for t in range(8, rows_ + 1, 8) if rows_ % t == 0 and t * cols * 4 <= ADAM_BLOCK_BYTES)
        blocks += [(k, r, tr) for r in range(0, rows_, tr)]
    nin, n = 4 * len(items), len(ws)

    def body(*refs):
        mats_in, small_in = refs[:nin], refs[nin:nin + 1 + 3 * n]
        n_out = nin + 4 * n + 1
        outs, scratch = refs[nin + 1 + 3 * n:nin + 1 + 3 * n + n_out], refs[nin + 1 + 3 * n + n_out:]
        buf, (lsem, ssem) = scratch[:nin], scratch[nin:]
        loads = [[pltpu.make_async_copy(mats_in[4 * k + q].at[pl.ds(r, tr)], buf[4 * k + q].at[pl.ds(r, tr)],
                                        lsem.at[4 * c + q]) for q in range(4)] for c, (k, r, tr) in enumerate(blocks)]
        for cps in loads:
            for cp in cps:
                cp.start()
        _adamw_small(small_in[0], *(small_in[1 + k * n:1 + (k + 1) * n] for k in range(3)),
                     *(outs[nin + k * n:nin + (k + 1) * n] for k in range(4)), outs[-1])
        stores = []
        for c, (k, r, tr) in enumerate(blocks):
            for cp in loads[c]:
                cp.wait()
            w_ref, g_ref, m_ref, v_ref = (buf[4 * k + q].at[pl.ds(r, tr)] for q in range(4))
            w_ref[...], m_ref[...], v_ref[...] = _adam_update(w_ref[...], g_ref[...], m_ref[...], v_ref[...])
            for q, src in enumerate((g_ref, w_ref, m_ref, v_ref)):
                stores.append(pltpu.make_async_copy(src, outs[4 * k + q].at[pl.ds(r, tr)], ssem.at[4 * c + q]))
                stores[-1].start()
        for cp in stores:
            cp.wait()

    shapes = [jax.ShapeDtypeStruct(w.shape, F32) for w, _, _, _ in items for _ in range(4)]
    vm = pl.BlockSpec(memory_space=pltpu.VMEM)
    hbm = pl.BlockSpec(memory_space=pl.ANY)
    like = [jax.ShapeDtypeStruct(w.shape, F32) for w in ws]
    res = pl.pallas_call(
        body, name="adamw", out_shape=(*shapes, *like * 4, jax.ShapeDtypeStruct((1, 1), F32)),
        in_specs=[hbm] * nin + [vm] * (1 + 3 * n), out_specs=(*[hbm] * nin, *[vm] * (4 * n + 1)),
        scratch_shapes=[pltpu.VMEM(s.shape, F32) for s in shapes] + [pltpu.SemaphoreType.DMA((4 * len(blocks),))] * 2,
        compiler_params=pltpu.CompilerParams(vmem_limit_bytes=VMEM_LIMIT),
    )(*[a for item in items for a in item], g_pack, *ws, *ms, *vs)
    return [res[4 * k:4 * k + 4] for k in range(len(items))], res[nin:]


def _adam_update(w, g, m, v):
    nm = ADAM_B1 * m + (1.0 - ADAM_B1) * g
    nv = ADAM_B2 * v + (1.0 - ADAM_B2) * (g * g)
    m_hat = nm / (1.0 - ADAM_B1 ** ADAM_STEP)
    v_hat = nv / (1.0 - ADAM_B2 ** ADAM_STEP)
    return (-ADAM_LR) * (m_hat / (jnp.sqrt(v_hat) + ADAM_EPS) + ADAM_WD * w), nm, nv


def _adamw_small(pk, w_refs, m_refs, v_refs, g_out, d_out, nm_out, nv_out, loss_ref):
    nvec = len(SMALL_VECTORS)
    loss_ref[...] = pk[LOSS_ROW:LOSS_ROW + 1, 0:1]
    chip = 2 * lax.axis_index("x") + lax.axis_index("y")
    for k, (name, row, width) in enumerate(SMALL_VECTORS):
        if name == "conv_w":
            g = jnp.concatenate([pk[pl.ds(row + 4 * t + chip, 1), :] for t in range(CONV_K)], axis=0)[None]
        elif width >= 128:
            g = jnp.concatenate([pk[row + r:row + r + 1, :] for r in range(width // 128)], axis=1)
        else:
            g = pk[row:row + 1, 0:width]
        g_out[k][...] = g
        d_out[k][...], nm_out[k][...], nv_out[k][...] = _adam_update(w_refs[k][...], g, m_refs[k][...], v_refs[k][...])
    for k in range(nvec, nvec + len(SMALL_MATRICES)):
        for b in range(LRU_BLOCKS):
            rows_ = pk[GATES_ROW + HEAD * b:GATES_ROW + HEAD * (b + 1), :]
            g = (pltpu.roll(rows_, HEAD, axis=1) if k > nvec else rows_)[:, 0:HEAD]
            g_out[k][0, b] = g
            d_out[k][0, b], nm_out[k][0, b], nv_out[k][0, b] = _adam_update(
                w_refs[k][0, b], g, m_refs[k][0, b], v_refs[k][0, b])


SMALL_VECTORS = (("norm_g", 512, 1024), ("mem_norm_g", 520, 1024), ("conv_w", 528, 512), ("conv_b", 544, 512),
                 ("b_rg", 548, 512), ("b_ig", 552, 512), ("lru_lambda", 556, 512), ("q_norm_g", 560, 64),
                 ("k_norm_g", 561, 64), ("sinks", 562, 4), ("xq_norm_g", 563, 64), ("xk_norm_g", 564, 64),
                 ("out_norm_g", 565, 1024))
LOSS_ROW = 573
SMALL_MATRICES = ("w_rg", "w_ig")
GATES_ROW = 0
SMALL_ROWS = 640


def _rope_tables(seq):
    pos = np.arange(seq, dtype=np.float32)
    inv_freq = (np.float32(ROPE_THETA) ** (-(np.arange(0, ROPE_DIM, 2, dtype=np.float32) / np.float32(ROPE_DIM)))
                ).astype(np.float32)
    ang = (pos[:, None] * inv_freq[None, :]).astype(np.float32)
    cos, sin = np.cos(ang).astype(np.float32), np.sin(ang).astype(np.float32)
    z = lambda n: np.zeros((seq, n), np.float32)
    c64 = np.concatenate([cos, cos, np.ones((seq, HEAD - ROPE_DIM), np.float32)], axis=1)
    s1_64 = np.concatenate([-sin, z(HEAD - 8)], axis=1)
    s2_64 = np.concatenate([z(8), sin, z(HEAD - ROPE_DIM)], axis=1)
    return tuple(jnp.asarray(np.concatenate([t, t], axis=1)) for t in (c64, s1_64, s2_64))


def kernel(x, mem, norm_g, mem_norm_g, w_in, conv_w, conv_b, w_rg, b_rg, w_ig, b_ig, lru_lambda, q_norm_g, k_norm_g, sinks, w_mem_kv, xq_norm_g, xk_norm_g, out_norm_g, w_out, loss_target, m_norm_g, m_mem_norm_g, m_w_in, m_conv_w, m_conv_b, m_w_rg, m_b_rg, m_w_ig, m_b_ig, m_lru_lambda, m_q_norm_g, m_k_norm_g, m_sinks, m_w_mem_kv, m_xq_norm_g, m_xk_norm_g, m_out_norm_g, m_w_out, v_norm_g, v_mem_norm_g, v_w_in, v_conv_w, v_conv_b, v_w_rg, v_b_rg, v_w_ig, v_b_ig, v_lru_lambda, v_q_norm_g, v_k_norm_g, v_sinks, v_w_mem_kv, v_xq_norm_g, v_xk_norm_g, v_out_norm_g, v_w_out):
    seq = x.shape[1]
    xs, tgt, mems = x[0], loss_target[0], mem[0]

    win_t, wout, wkv, cw, wg, gains, km, vm = gather_weights(
        w_in[0].T, w_out[0], w_mem_kv[0], conv_w, w_rg, w_ig, (q_norm_g, k_norm_g, xq_norm_g, xk_norm_g), mems,
        mem_norm_g)
    rc, rs1, rs2 = _rope_tables(seq)
    proj, ya, yb, yc, ycat, xn, dout, pswa, pmem, psink, gates, a_all, loss8 = layer_fwd(
        xs, tgt, rc, rs1, rs2, norm_g, win_t, cw, conv_b, wg, b_rg, b_ig, lru_lambda, gains, sinks, km, vm,
        out_norm_g, wout)
    (gx, dproj, g_wg, dkm, dvm, g_ng, g_og, g_cb, g_brg, g_big, g_lam, g_cw, g_qn, g_kn, g_xqn, g_sink) = layer_bwd(
        xs, dout, proj, ya, yb, yc, pswa, pmem, psink, gates, a_all, rc, rs1, rs2, norm_g, win_t, cw, wg, lru_lambda,
        gains, km, vm, out_norm_g, wout)
    g_win_t, r_out, r_kv, r_small = weight_grads(
        ycat, dout, dproj, xn, (mems, mem_norm_g, wkv, gains, dkm, dvm, g_wg, loss8), dict(
            norm_g=g_ng, conv_w=g_cw, conv_b=g_cb, b_rg=g_brg, b_ig=g_big, lru_lambda=g_lam, q_norm_g=g_qn,
            k_norm_g=g_kn, sinks=g_sink, xq_norm_g=g_xqn, out_norm_g=g_og), (0, 1, 4, 7, 8), (4, 5, 9, 11, 13))
    r_in = reduce_grads(g_win_t.reshape(N_CHIPS, 2, D_IN // 8, D_MODEL), "reduce_w_in", 6)

    r_small = r_small.reshape(SMALL_ROWS, 128)
    grads = {}
    weights = dict(norm_g=norm_g, mem_norm_g=mem_norm_g, w_in=w_in, conv_w=conv_w, conv_b=conv_b, w_rg=w_rg, b_rg=b_rg,
                   w_ig=w_ig, b_ig=b_ig, lru_lambda=lru_lambda, q_norm_g=q_norm_g, k_norm_g=k_norm_g, sinks=sinks,
                   w_mem_kv=w_mem_kv, xq_norm_g=xq_norm_g, xk_norm_g=xk_norm_g, out_norm_g=out_norm_g, w_out=w_out)
    ms = dict(norm_g=m_norm_g, mem_norm_g=m_mem_norm_g, w_in=m_w_in, conv_w=m_conv_w, conv_b=m_conv_b, w_rg=m_w_rg,
              b_rg=m_b_rg, w_ig=m_w_ig, b_ig=m_b_ig, lru_lambda=m_lru_lambda, q_norm_g=m_q_norm_g, k_norm_g=m_k_norm_g,
              sinks=m_sinks, w_mem_kv=m_w_mem_kv, xq_norm_g=m_xq_norm_g, xk_norm_g=m_xk_norm_g,
              out_norm_g=m_out_norm_g, w_out=m_w_out)
    vs = dict(norm_g=v_norm_g, mem_norm_g=v_mem_norm_g, w_in=v_w_in, conv_w=v_conv_w, conv_b=v_conv_b, w_rg=v_w_rg,
              b_rg=v_b_rg, w_ig=v_w_ig, b_ig=v_b_ig, lru_lambda=v_lru_lambda, q_norm_g=v_q_norm_g, k_norm_g=v_k_norm_g,
              sinks=v_sinks, w_mem_kv=v_w_mem_kv, xq_norm_g=v_xq_norm_g, xk_norm_g=v_xk_norm_g,
              out_norm_g=v_out_norm_g, w_out=v_w_out)

    delta, new_m, new_v = {}, {}, {}
    small_names = [n for n, _, _ in SMALL_VECTORS] + list(SMALL_MATRICES)
    (res_in, res_out, res_kv), res = adamw(
        [(w_in[0].T, r_in.reshape(D_IN // 4, D_MODEL), m_w_in[0].T, v_w_in[0].T),
         (w_out[0], r_out.reshape(D_MODEL // 4, D_MODEL), m_w_out[0], v_w_out[0]),
         (w_mem_kv[0], r_kv.reshape(D_MODEL // 4, 2 * XATT_W), m_w_mem_kv[0], v_w_mem_kv[0])],
        r_small, [weights[n] for n in small_names], [ms[n] for n in small_names], [vs[n] for n in small_names])
    grads["w_in"], delta["w_in"], new_m["w_in"], new_v["w_in"] = (r.T[None] for r in res_in)
    grads["w_out"], delta["w_out"], new_m["w_out"], new_v["w_out"] = (r[None] for r in res_out)
    grads["w_mem_kv"], delta["w_mem_kv"], new_m["w_mem_kv"], new_v["w_mem_kv"] = (r[None] for r in res_kv)
    nall = len(small_names)
    for k, into in enumerate((grads, delta, new_m, new_v)):
        into.update(zip(small_names, res[k * nall:(k + 1) * nall]))
    loss = res[-1].reshape(())

    order = ("norm_g", "mem_norm_g", "w_in", "conv_w", "conv_b", "w_rg", "b_rg", "w_ig", "b_ig", "lru_lambda",
             "q_norm_g", "k_norm_g", "sinks", "w_mem_kv", "xq_norm_g", "xk_norm_g", "out_norm_g", "w_out")
    return (loss, gx[None], *[grads[n] for n in order], *[delta[n] for n in order], *[new_m[n] for n in order],
            *[new_v[n] for n in order])
```

```python
import jax
import jax.numpy as jnp
import numpy as np
from jax import lax
from jax.experimental import pallas as pl
from jax.experimental.pallas import tpu as pltpu

F32 = jnp.float32
_MXU = jnp.bfloat16
_WIRE = jnp.bfloat16

D_MODEL = 1024
MEM_LEN = 256
HEAD = 64
LRU_W = 512
LRU_BLOCKS = 8
CONV_K = 4
LRU_C = 8.0
SWA_W = 256
KV_W = 128
XATT_W = 256
BLOCK = 128
D_IN = 2304
ROPE_THETA = 500000.0
ROPE_DIM = 16
EPS = 1e-6
NEG_INF = -1e30
C_LRUX, C_LRUG, C_SQ, C_SK, C_SV, C_SWAG, C_XQ, C_XG = 0, 512, 1024, 1280, 1408, 1536, 1792, 2048
G_Q, G_K, G_XQ, G_XK, GAINS_W = 0, 128, 256, 512, 768

ADAM_LR, ADAM_B1, ADAM_B2, ADAM_EPS, ADAM_WD, ADAM_STEP = 0.001, 0.9, 0.999, 1e-08, 0.01, 10

N_CHIPS = 4
ROW_TILE = 256
VMEM_LIMIT = 56 * 1024 * 1024
ADAM_BLOCK_BYTES = 640 * 1024
MESH = pl.DeviceIdType.MESH


def _mm(a, b):
    return jnp.dot(a.astype(_MXU), b.astype(_MXU), preferred_element_type=F32)


def _mm_nt(a, b):
    return lax.dot_general(a.astype(_MXU), b.astype(_MXU), (((1,), (1,)), ((), ())), preferred_element_type=F32)


def _mm_tn(a, b):
    return lax.dot_general(a.astype(_MXU), b.astype(_MXU), (((0,), (0,)), ((), ())), preferred_element_type=F32)


def _group_matrix(width):
    r = lax.shift_right_logical(lax.broadcasted_iota(jnp.int32, (width, width), 0), 6)
    c = lax.shift_right_logical(lax.broadcasted_iota(jnp.int32, (width, width), 1), 6)
    return (r == c).astype(_MXU)


def _seg_mean(x, gm):
    return jnp.dot(x.astype(_MXU), gm, preferred_element_type=F32) * (1.0 / HEAD)


def _row_mean(x):
    return jnp.mean(x, axis=-1, keepdims=True)


def _col_sum(x):
    return jnp.sum(x, axis=0, keepdims=True)


def _sigmoid(x):
    return jax.nn.sigmoid(x)


def _softplus(z):
    e = jnp.exp(-jnp.abs(z))
    u = 1.0 + e
    log1p_e = jnp.where(u == 1.0, e, jnp.log(u) * (e / (u - 1.0)))
    return jnp.maximum(z, 0.0) + log1p_e


def _rope(t, c, s1, s2):
    return t * c + pltpu.roll(t, 120, 1) * s1 + pltpu.roll(t, 8, 1) * s2


def _rope_bwd(d, c, s1, s2):
    return d * c + pltpu.roll(d * s1, 8, 1) + pltpu.roll(d * s2, 120, 1)


def _fold_heads(v):
    out = v
    for k in range(1, v.shape[1] // HEAD):
        out = out + pltpu.roll(v, HEAD * k, 1)
    return out


def _lane_mask(width, lo, hi):
    lane = lax.broadcasted_iota(jnp.int32, (1, width), 1)
    return ((lane >= lo) & (lane < hi)).astype(F32)


def _swa_mask(first_block):
    qi = lax.broadcasted_iota(jnp.int32, (BLOCK, 2 * BLOCK), 0)
    kj = lax.broadcasted_iota(jnp.int32, (BLOCK, 2 * BLOCK), 1)
    rel = qi + BLOCK - kj
    ok = (rel >= 0) & (rel < BLOCK)
    return ok & (jnp.logical_not(first_block) | (kj >= BLOCK))


def _place_kv(t, scale):
    lo = t * (_lane_mask(KV_W, 0, HEAD) * scale)
    hi = t * (_lane_mask(KV_W, HEAD, KV_W) * scale)
    return [a.astype(_MXU) for a in (lo, pltpu.roll(lo, HEAD, 1), pltpu.roll(hi, HEAD, 1), hi)]


def _unplace_kv(d):
    return (_lane_mask(KV_W, 0, HEAD) * (d[0] + pltpu.roll(d[1], HEAD, 1))
            + _lane_mask(KV_W, HEAD, KV_W) * (d[3] + pltpu.roll(d[2], HEAD, 1)))


def _swa_probs(qh, ka, mask, sink):
    s = _mm_nt(qh, ka)
    s = jnp.where(mask, s, NEG_INF)
    m = jnp.maximum(jnp.max(s, axis=-1, keepdims=True), sink)
    p = jnp.exp(s - m)
    esink = jnp.exp(sink - m)
    inv = 1.0 / (jnp.sum(p, axis=-1, keepdims=True) + esink)
    return p * inv, esink * inv


def _mem_probs(s_all):
    out = []
    for j in range(4):
        s = s_all[:, MEM_LEN * j:MEM_LEN * (j + 1)]
        p = jnp.exp(s - jnp.max(s, axis=-1, keepdims=True))
        out.append(p * (1.0 / jnp.sum(p, axis=-1, keepdims=True)))
    return out


def _head_rows(t, scale):
    return jnp.concatenate([t * (_lane_mask(XATT_W, HEAD * j, HEAD * (j + 1)) * scale) for j in range(4)], axis=0)


def _lru_gates(xc, wg_ref, brg, big, lam):
    p0 = _mm(xc[:, :256], wg_ref[0])
    p1 = _mm(xc[:, 256:], wg_ref[1])
    rg = _sigmoid(jnp.concatenate([p0[:, :256], p1[:, :256]], axis=1) + brg)
    ig = _sigmoid(jnp.concatenate([p0[:, 256:], p1[:, 256:]], axis=1) + big)
    sp = _softplus(-lam)
    la = (-LRU_C) * rg * sp
    a = jnp.exp(la)
    th = jnp.tanh(la)
    one_minus_a2 = (-2.0 * th) / (1.0 - th)
    return rg, ig, sp, a, jnp.sqrt(one_minus_a2)


def _const_spec(shape, single=False):
    zeros = (0,) * len(shape)
    if single:
        return pl.BlockSpec(shape, lambda i: zeros, pipeline_mode=pl.Buffered(1))
    return pl.BlockSpec(shape, lambda i: zeros)


def _chip_of(x, y):
    return 2 * x + y


def _partners(x, y, c):
    north = c == 1
    near = (jnp.where(north, 1 - x, x), jnp.where(north, y, 1 - y))
    far = (jnp.where(north, x, 1 - x), jnp.where(north, 1 - y, y))
    return near, far, (1 - x, 1 - y)


def gather_weights(win_t, wout, wkv, conv_w, w_rg, w_ig, head_gains, mem, mem_g):
    arrs = (win_t, wout, wkv)
    n = len(arrs)
    pieces = [(a, 0, arr.shape[0] // 2) for a, arr in enumerate(arrs)]
    npc = len(pieces)

    def body(a0, a1, a2, cw_in, wrg_ref, wig_ref, q_ref, k_ref, xq_ref, xk_ref, mem_ref, mg_ref,
             o0, o1, o2, cw_out, wg_ref, gn_ref, km_ref, vm_ref, s0, s1, s2, cw, ocw, send, recv, lsem):
        ins, outs = (s0, s1, s2), (o0, o1, o2)
        for src, dst in zip((a0, a1, a2), ins):
            dst[...] = src[...].astype(dst.dtype)
        cw[...] = jnp.zeros(cw.shape, F32)
        cw[0:CONV_K, :] = cw_in[0]
        x, y, c = lax.axis_index("x"), lax.axis_index("y"), lax.axis_index("c")
        sibling = (x, y, 1 - c)
        near, far, diag = _partners(x, y, c)
        chips = [near, far, diag]
        me = _chip_of(x, y)

        def landed(p, chip, half):
            a, off, rows_ = pieces[p]
            r = ins[a].shape[0]
            return outs[a].at[pl.ds(pl.multiple_of(chip * r + half * (r // 2) + off, 16), rows_)]

        def mine(p):
            a, off, rows_ = pieces[p]
            return ins[a].at[pl.ds(pl.multiple_of(c * (ins[a].shape[0] // 2) + off, 16), rows_)]

        def copy(k, src, dst, to):
            return pltpu.make_async_remote_copy(src_ref=src, dst_ref=dst, send_sem=send.at[k], recv_sem=recv.at[k],
                                                device_id=to, device_id_type=MESH)

        def cw_rows(chip):
            return ocw.at[pl.ds(pl.multiple_of(chip * 8, 8), 8)]

        locals_ = []
        for a in range(n):
            r = ins[a].shape[0]
            locals_.append(pltpu.make_async_copy(ins[a], outs[a].at[pl.ds(pl.multiple_of(me * r, 16), r)], lsem.at[a]))
        locals_.append(pltpu.make_async_copy(cw, cw_rows(me), lsem.at[n]))
        for cp in locals_:
            cp.start()

        sent = []
        for p in range(npc):
            for j in range(2):
                sent.append(copy(p * 6 + j, mine(p), landed(p, me, c), (*chips[j], c)))
        for j, chip in enumerate(chips):
            sent.append(copy(npc * 6 + j, cw, cw_rows(me), (*chip, c)))
        for cp in sent:
            cp.start()

        gn_ref[...] = jnp.concatenate([q_ref[...]] * 2 + [k_ref[...]] * 2 + [xq_ref[...]] * 4 + [xk_ref[...]] * 4,
                                      axis=1)
        zeros = lambda lanes: [jnp.zeros((HEAD, lanes), F32)] if lanes else []
        for h in range(2):
            for b in range(4):
                row = []
                for w_ref in (wrg_ref, wig_ref):
                    row += zeros(HEAD * b) + [w_ref[0, 4 * h + b]] + zeros(HEAD * (3 - b))
                wg_ref[h, HEAD * b:HEAD * (b + 1), :] = jnp.concatenate(row, axis=1).astype(wg_ref.dtype)

        for j in range(3):
            for p in range(npc):
                got = landed(p, _chip_of(*chips[j]), c)
                copy(p * 6 + j, got, got, sibling).wait_recv()
                if j == 0:
                    sent.append(copy(p * 6 + 2, got, got, (*far, c)))
                    sent[-1].start()
                sent.append(copy(p * 6 + 3 + j, got, got, sibling))
                sent[-1].start()
        for p in range(npc):
            for j in range(3):
                got = landed(p, _chip_of(*chips[(1, 0, 2)[j]]), 1 - c)
                copy(p * 6 + 3 + j, got, got, sibling).wait_recv()
        for j, chip in enumerate(chips):
            got = cw_rows(_chip_of(*chip))
            copy(npc * 6 + j, got, got, (*chip, c)).wait_recv()
        for cp in sent:
            cp.wait_send()
        for cp in locals_:
            cp.wait()
        for chip in range(N_CHIPS):
            cw_out[:, 128 * chip:128 * (chip + 1)] = ocw[8 * chip:8 * chip + CONV_K, :]

        mem_v = mem_ref[...]
        mn = mem_v * lax.rsqrt(_row_mean(mem_v * mem_v) + EPS) * mg_ref[...]
        mkv = _mm(mn, o2[...])
        kpre = mkv[:, :XATT_W]
        km = kpre * lax.rsqrt(_seg_mean(kpre * kpre, _group_matrix(XATT_W)) + EPS) * gn_ref[:, G_XK:GAINS_W]
        km_ref[...] = _head_rows(km, 0.125).astype(km_ref.dtype)
        vm_ref[...] = _head_rows(mkv[:, XATT_W:], 1.0).astype(vm_ref.dtype)

    vm = pl.BlockSpec(memory_space=pltpu.VMEM)
    hbm = pl.BlockSpec(memory_space=pl.ANY)
    head_rows = jax.ShapeDtypeStruct((4 * MEM_LEN, XATT_W), _MXU)
    out_shape = tuple(jax.ShapeDtypeStruct((N_CHIPS * a.shape[0],) + a.shape[1:], _MXU) for a in arrs) + (
        jax.ShapeDtypeStruct((CONV_K, LRU_W), F32), jax.ShapeDtypeStruct((2, 256, 512), _MXU),
        jax.ShapeDtypeStruct((1, GAINS_W), F32), head_rows, head_rows)
    n_rdma = npc * 6 + 3
    return pl.pallas_call(
        body, name="gather_weights", out_shape=out_shape,
        in_specs=[vm] * 12, out_specs=(hbm, hbm, vm, vm, vm, vm, vm, vm),
        scratch_shapes=[pltpu.VMEM(a.shape, _MXU) for a in arrs] + [
            pltpu.VMEM((8, 128), F32), pltpu.VMEM((N_CHIPS * 8, 128), F32),
            pltpu.SemaphoreType.DMA((n_rdma,)), pltpu.SemaphoreType.DMA((n_rdma,)), pltpu.SemaphoreType.DMA((n + 1,))],
        compiler_params=pltpu.CompilerParams(vmem_limit_bytes=VMEM_LIMIT),
    )(win_t, wout, wkv, conv_w, w_rg, w_ig, *head_gains, mem, mem_g)


def _mem_bwd_and_pack(mem_ref, g_ref, w_ref, gn_ref, dkm_ref, dvm_ref, gg_ref, loss_ref, vectors, gw_ref, pk_ref):
    first_row = {name: (row, width) for name, row, width in SMALL_VECTORS}
    half_rows = SMALL_ROWS // 2
    pk_ref[...] = jnp.zeros(pk_ref.shape, F32)

    def rows_at(at, n):
        assert at // half_rows == (at + n - 1) // half_rows
        return at // half_rows, slice(at % half_rows, at % half_rows + n), slice(None)

    def put(name, src):
        row, width = first_row[name]
        per_row = 1 if width < 128 else src.shape[1] // 128
        for t in range(src.shape[0]):
            for r in range(per_row):
                pk_ref[rows_at(row + per_row * t + r, 1)] = src[t:t + 1, 128 * r:128 * (r + 1)]

    for name, ref in vectors.items():
        put(name, ref)
    pk_ref[rows_at(LOSS_ROW, 1)] = loss_ref[0:1, :]
    upper = lax.broadcasted_iota(jnp.int32, (HEAD, 128), 1) >= HEAD
    for h in range(2):
        for b in range(4):
            rg = gg_ref[h, HEAD * b:HEAD * (b + 1), 128 * (b // 2):128 * (b // 2 + 1)]
            ig = gg_ref[h, HEAD * b:HEAD * (b + 1), 256 + 128 * (b // 2):256 + 128 * (b // 2 + 1)]
            if b % 2:
                rg = pltpu.roll(rg, HEAD, axis=1)
            else:
                ig = pltpu.roll(ig, HEAD, axis=1)
            pk_ref[rows_at(GATES_ROW + HEAD * (4 * h + b), HEAD)] = jnp.where(upper, ig, rg)

    mem_v = mem_ref[...]
    mh = mem_v * lax.rsqrt(_row_mean(mem_v * mem_v) + EPS)
    mn = mh * g_ref[...]
    mkv = _mm(mn, w_ref[...])
    kpre = mkv[:, :XATT_W]
    gm = _group_matrix(XATT_W)
    rk = lax.rsqrt(_seg_mean(kpre * kpre, gm) + EPS)
    kn = kpre * rk
    dk = jnp.zeros((MEM_LEN, XATT_W), F32)
    dv = jnp.zeros((MEM_LEN, XATT_W), F32)
    for j in range(4):
        mj = _lane_mask(XATT_W, HEAD * j, HEAD * (j + 1))
        dk = dk + dkm_ref[:, MEM_LEN * j:MEM_LEN * (j + 1)].T * (mj * 0.125)
        dv = dv + dvm_ref[:, MEM_LEN * j:MEM_LEN * (j + 1)].T * mj
    put("xk_norm_g", _fold_heads(_col_sum(dk * kn)))
    dkn = dk * gn_ref[:, G_XK:GAINS_W]
    dkpre = rk * (dkn - kn * _seg_mean(dkn * kn, gm))
    dmkv = jnp.concatenate([dkpre, dv], axis=1)
    gw_ref[...] = _mm_tn(mn, dmkv).reshape(gw_ref.shape)
    dmn = _mm_nt(dmkv, w_ref[...])
    put("mem_norm_g", _col_sum(dmn * mh))


def layer_fwd(x, tgt, rc, rs1, rs2, ng, win_t, cw, cb, wg, brg, big, lam, gains, sinks, km, vm, og, wout):
    seq = x.shape[0]
    tm = min(ROW_TILE, seq)
    nt = seq // tm
    nb = tm // BLOCK

    def body(x_ref, t_ref, c_ref, s1_ref, s2_ref, ng_ref, win_ref, cw_ref, cb_ref, wg_ref, brg_ref, big_ref, lam_ref,
             gn_ref, sink_ref, km_ref, vm_ref, og_ref, wout_ref,
             proj_ref, ya_ref, yb_ref, yc_ref, ycat_ref, xn_ref, dout_ref, pswa_ref, pmem_ref, psink_ref, gates_ref,
             a_ref, loss_ref,
             ext_ref, b_scr, hc_ref, kp_ref, vp_ref, lacc_ref):
        i = pl.program_id(0)

        @pl.when(i == 0)
        def _():
            ext_ref[0:8, :] = jnp.zeros((8, LRU_W), F32)
            hc_ref[...] = jnp.zeros_like(hc_ref)
            kp_ref[...] = jnp.zeros_like(kp_ref)
            vp_ref[...] = jnp.zeros_like(vp_ref)
            lacc_ref[...] = jnp.zeros_like(lacc_ref)

        xv = x_ref[...]
        xn = (xv * lax.rsqrt(_row_mean(xv * xv) + EPS) * ng_ref[...]).astype(_MXU)
        xn_ref[...] = xn.astype(xn_ref.dtype)
        proj_ref[...] = _mm_nt(xn, win_ref[...])

        u = proj_ref[:, C_LRUX:C_LRUX + LRU_W]
        ext_ref[8:8 + tm, :] = u
        xc = cb_ref[...]
        for k in range(CONV_K):
            xc = xc + cw_ref[k:k + 1, :] * ext_ref[pl.ds(5 + k, tm), :]
        ext_ref[0:8, :] = u[tm - 8:tm, :]
        rg, ig, sp, a, sq = _lru_gates(xc, wg_ref, brg_ref[...], big_ref[...], lam_ref[...])
        for k, t in enumerate((xc, rg, ig, sq)):
            gates_ref[:, LRU_W * k:LRU_W * (k + 1)] = t.astype(gates_ref.dtype)
        a_ref[...] = a
        b_scr[...] = sq * (ig * xc)
        row8 = lax.broadcasted_iota(jnp.int32, (8, LRU_W), 0)

        def scan_step(g, carry):
            r0 = pl.multiple_of(g * 8, 8)
            av = a_ref[pl.ds(r0, 8), :]
            bv = b_scr[pl.ds(r0, 8), :]
            for d in (1, 2, 4):
                a_sh = jnp.where(row8 >= d, pltpu.roll(av, d, 0), 1.0)
                b_sh = jnp.where(row8 >= d, pltpu.roll(bv, d, 0), 0.0)
                bv = bv + av * b_sh
                av = av * a_sh
            hv = bv + av * carry
            ya_ref[pl.ds(r0, 8), :] = hv
            return hv[7:8, :]

        hc_ref[0:1, :] = lax.fori_loop(0, tm // 8, scan_step, hc_ref[0:1, :], unroll=True)

        gm128 = _group_matrix(KV_W)
        cv, s1v, s2v = c_ref[...], s1_ref[...], s2_ref[...]

        def head_norm_rope(t, g):
            n = t * lax.rsqrt(_seg_mean(t * t, gm128) + EPS)
            return _rope(n * g, cv, s1v, s2v)

        qs_ = (head_norm_rope(proj_ref[:, C_SQ:C_SQ + 128], gn_ref[:, G_Q:G_K]).astype(_MXU),
               head_norm_rope(proj_ref[:, C_SQ + 128:C_SQ + 256], gn_ref[:, G_Q:G_K]).astype(_MXU))
        kr = head_norm_rope(proj_ref[:, C_SK:C_SK + KV_W], gn_ref[:, G_K:G_XQ])
        sv = proj_ref[:, C_SV:C_SV + KV_W]
        ka = _place_kv(jnp.concatenate([kp_ref[...], kr], axis=0), 0.125)
        va = _place_kv(jnp.concatenate([vp_ref[...], sv], axis=0), 1.0)
        kp_ref[...] = kr[tm - BLOCK:tm, :]
        vp_ref[...] = sv[tm - BLOCK:tm, :]
        lane128 = lax.broadcasted_iota(jnp.int32, (1, 128), 1)
        for b in range(nb):
            mask = _swa_mask((i == 0) & (b == 0)) if b == 0 else _swa_mask(False)
            band = slice(BLOCK * b, BLOCK * b + 2 * BLOCK)
            blk = slice(BLOCK * b, BLOCK * (b + 1))
            psink = jnp.zeros((BLOCK, 128), F32)
            for j in range(4):
                p, pk = _swa_probs(qs_[j // 2][blk], ka[j][band], mask, sink_ref[0, j])
                pswa_ref[blk, 2 * BLOCK * j:2 * BLOCK * (j + 1)] = p.astype(pswa_ref.dtype)
                psink = jnp.where(lane128 == j, pk, psink)
            psink_ref[blk, :] = psink
            for h in range(2):
                yb_ref[blk, KV_W * h:KV_W * (h + 1)] = _mm(
                    pswa_ref[blk, 4 * BLOCK * h:4 * BLOCK * (h + 1)],
                    jnp.concatenate([va[2 * h][band], va[2 * h + 1][band]], axis=0))

        gm256 = _group_matrix(XATT_W)
        xq = proj_ref[:, C_XQ:C_XQ + XATT_W]
        qx = xq * lax.rsqrt(_seg_mean(xq * xq, gm256) + EPS) * gn_ref[:, G_XQ:G_XK]
        pm = _mem_probs(_mm_nt(qx, km_ref[...]))
        for j in range(4):
            pmem_ref[:, MEM_LEN * j:MEM_LEN * (j + 1)] = pm[j].astype(pmem_ref.dtype)
        yc = _mm(pmem_ref[...], vm_ref[...])
        yc_ref[...] = yc

        def gated(y, g, gate):
            return y * lax.rsqrt(_row_mean(y * y) + EPS) * g * (gate * _sigmoid(gate))

        ogv = og_ref[...]
        za = gated(ya_ref[...], ogv[:, :512], proj_ref[:, C_LRUG:C_LRUG + LRU_W])
        zb = gated(yb_ref[...], ogv[:, 512:768], proj_ref[:, C_SWAG:C_SWAG + SWA_W])
        zc = gated(yc, ogv[:, 768:], proj_ref[:, C_XG:C_XG + XATT_W])
        ycat_ref[:, 0:512] = za.astype(ycat_ref.dtype)
        ycat_ref[:, 512:768] = zb.astype(ycat_ref.dtype)
        ycat_ref[:, 768:1024] = zc.astype(ycat_ref.dtype)
        out = xv + _mm(ycat_ref[...], wout_ref[...])
        err = out - t_ref[...]
        dout_ref[...] = (err * (1.0 / D_MODEL)).astype(dout_ref.dtype)
        lacc_ref[...] = lacc_ref[...] + (0.5 / D_MODEL) * jnp.sum(err * err)

        @pl.when(i == nt - 1)
        def _():
            loss_ref[...] = lacc_ref[...]

    def rows(ncol):
        return pl.BlockSpec((tm, ncol), lambda i: (i, 0))

    in_specs = [rows(D_MODEL), rows(D_MODEL), rows(128), rows(128), rows(128),
                _const_spec((1, D_MODEL)), _const_spec((D_IN, D_MODEL), True), _const_spec((CONV_K, LRU_W)),
                _const_spec((1, LRU_W)), _const_spec((2, 256, 512), True), _const_spec((1, LRU_W)),
                _const_spec((1, LRU_W)), _const_spec((1, LRU_W)), _const_spec((1, GAINS_W)), pl.BlockSpec(memory_space=pltpu.SMEM),
                _const_spec((4 * MEM_LEN, XATT_W), True), _const_spec((4 * MEM_LEN, XATT_W), True),
                _const_spec((1, D_MODEL)), _const_spec((D_MODEL, D_MODEL), True)]
    out_shape = (jax.ShapeDtypeStruct((seq, D_IN), F32), jax.ShapeDtypeStruct((seq, LRU_W), F32),
                 jax.ShapeDtypeStruct((seq, SWA_W), F32), jax.ShapeDtypeStruct((seq, XATT_W), F32),
                 jax.ShapeDtypeStruct((seq, D_MODEL), _MXU), jax.ShapeDtypeStruct((seq, D_MODEL), _MXU),
                 jax.ShapeDtypeStruct((seq, D_MODEL), _MXU), jax.ShapeDtypeStruct((seq, 4 * 2 * BLOCK), _MXU),
                 jax.ShapeDtypeStruct((seq, 4 * MEM_LEN), _MXU), jax.ShapeDtypeStruct((seq, 128), F32),
                 jax.ShapeDtypeStruct((seq, 4 * LRU_W), _MXU), jax.ShapeDtypeStruct((seq, LRU_W), F32),
                 jax.ShapeDtypeStruct((8, 128), F32))
    out_specs = (rows(D_IN), rows(LRU_W), rows(SWA_W), rows(XATT_W), rows(D_MODEL), rows(D_MODEL), rows(D_MODEL),
                 rows(4 * 2 * BLOCK), rows(4 * MEM_LEN), rows(128), rows(4 * LRU_W), rows(LRU_W),
                 _const_spec((8, 128)))
    scratch = [pltpu.VMEM((tm + 8, LRU_W), F32), pltpu.VMEM((tm, LRU_W), F32),
               pltpu.VMEM((8, LRU_W), F32), pltpu.VMEM((BLOCK, KV_W), F32), pltpu.VMEM((BLOCK, KV_W), F32),
               pltpu.VMEM((8, 128), F32)]
    return pl.pallas_call(
        body, name="layer_fwd", grid=(nt,), out_shape=out_shape, in_specs=in_specs, out_specs=out_specs,
        scratch_shapes=scratch,
        compiler_params=pltpu.CompilerParams(dimension_semantics=("arbitrary",), vmem_limit_bytes=VMEM_LIMIT),
    )(x, tgt, rc, rs1, rs2, ng, win_t, cw, cb, wg, brg, big, lam, gains, sinks, km, vm, og, wout)


def weight_grads(ycat, dout, dproj, xn, mem_operands, vectors, early_at, late_at):
    seq, ncol = xn.shape
    blk = 256
    n_out, n_in = ycat.shape[1] // blk, dproj.shape[1] // blk
    assert n_out == N_CHIPS and late_at[0] >= n_out
    g_out = jax.ShapeDtypeStruct((N_CHIPS, 2, blk // 2, dout.shape[1]), F32)
    g_kv = jax.ShapeDtypeStruct((N_CHIPS, 2, D_MODEL // 8, 2 * XATT_W), F32)
    g_small = jax.ShapeDtypeStruct((2, SMALL_ROWS // 2, 128), F32)
    shape_e, scratch_e = _hosted_reduce_shapes([g_kv], g_small)
    shape_l, scratch_l = _hosted_reduce_shapes([g_out], None)
    n_mem, names = len(mem_operands), tuple(vectors)

    def body(l1_ref, r1_hbm, l2_ref, r2_hbm, *refs):
        mem_refs, vec_refs = refs[:n_mem], refs[n_mem:n_mem + len(names)]
        (o_ref, sum_out, sum_kv, sum_sm, gout_scr, gkv_scr, pack_scr, r1_ref, r2_ref, r_sem,
         *scratch) = refs[n_mem + len(names):]
        j = pl.program_id(0)
        r1_copy = pltpu.make_async_copy(r1_hbm, r1_ref, r_sem.at[0])
        r2_copy = pltpu.make_async_copy(r2_hbm, r2_ref, r_sem.at[1])

        @pl.when(j == 0)
        def _():
            r1_copy.start()
            r2_copy.start()
            _mem_bwd_and_pack(*mem_refs, dict(zip(names, vec_refs)), gkv_scr, pack_scr)

        def reduce_stages(closing):
            _hosted_reduce(j, n_out + n_in, closing, early_at, (gkv_scr, pack_scr), (sum_kv, sum_sm),
                           scratch[:len(scratch_e)], True)
            _hosted_reduce(j, n_out + n_in, closing, late_at, (gout_scr,), (sum_out,), scratch[len(scratch_e):], False)

        reduce_stages(False)
        pl.when(j == 0)(r1_copy.wait)

        @pl.when(j < n_out)
        def _():
            gout_scr[j] = _mm_tn(l1_ref[...], r1_ref[...]).reshape(g_out.shape[1:])

        pl.when(j == n_out)(r2_copy.wait)

        @pl.when(j >= n_out)
        def _():
            o_ref[...] = _mm_tn(l2_ref[...], r2_ref[...])

        reduce_stages(True)

    vm = pl.BlockSpec(memory_space=pltpu.VMEM)
    hbm = pl.BlockSpec(memory_space=pl.ANY)
    return pl.pallas_call(
        body, name="weight_grads", grid=(n_out + n_in,),
        out_shape=(jax.ShapeDtypeStruct((dproj.shape[1], ncol), F32), *shape_l, *shape_e),
        in_specs=[pl.BlockSpec((seq, blk), lambda j: (0, jnp.minimum(j, n_out - 1))), hbm,
                  pl.BlockSpec((seq, blk), lambda j: (0, jnp.maximum(j - n_out, 0))), hbm]
        + [vm] * (n_mem + len(names)),
        out_specs=(pl.BlockSpec((blk, ncol), lambda j: (jnp.maximum(j - n_out, 0), 0)), hbm, hbm, hbm),
        scratch_shapes=[pltpu.VMEM(s.shape, F32) for s in (g_out, g_kv, g_small)]
        + [pltpu.VMEM(dout.shape, dout.dtype), pltpu.VMEM(xn.shape, xn.dtype), pltpu.SemaphoreType.DMA((2,))]
        + scratch_e + scratch_l,
        compiler_params=pltpu.CompilerParams(dimension_semantics=("arbitrary",), vmem_limit_bytes=VMEM_LIMIT),
    )(ycat, dout, dproj, xn, *mem_operands, *vectors.values())


def layer_bwd(x, dout, proj, ya, yb, yc, pswa, pmem, psink, gates, a_all, rc, rs1, rs2, ng, win_t, cw, wg, lam, gains,
              km, vm, og, wout):
    seq = x.shape[0]
    tm = min(ROW_TILE, seq)
    nt = seq // tm
    nb = tm // BLOCK

    def body(x_ref, dout_ref, proj_ref, ya_ref, yb_ref, yc_ref, pswa_ref, pmem_ref, psink_ref, gates_ref, a_ref,
             c_ref, s1_ref, s2_ref,
             yah_ref, kvh_ref, ch_ref, s1h_ref, s2h_ref,
             ng_ref, win_ref, cw_ref, wg_ref, lam_ref, gn_ref, km_ref, vm_ref, og_ref, wout_ref,
             gx_ref, dproj_ref, gwg_ref, dkm_ref, dvm_ref, gng_ref, gog_ref, gcb_ref, gbrg_ref, gbig_ref, glam_ref,
             gcw_ref, gqn_ref, gkn_ref, gxqn_ref, gsink_ref,
             hext_ref, aext_ref, an_scr, dh_scr, g_scr, dxc_ext, gcar_ref, dkcar_ref, dvcar_ref):
        i = pl.program_id(0)
        tile = nt - 1 - i
        first_tile = tile == 0

        @pl.when(i == 0)
        def _():
            for r in (gwg_ref, dkm_ref, dvm_ref, gng_ref, gog_ref, gcb_ref, gbrg_ref, gbig_ref, glam_ref, gcw_ref,
                      gqn_ref, gkn_ref, gxqn_ref, gsink_ref, gcar_ref, dkcar_ref, dvcar_ref):
                r[...] = jnp.zeros_like(r)
            dxc_ext[tm:tm + 8, :] = jnp.zeros((8, LRU_W), F32)
            aext_ref[tm:tm + 8, :] = jnp.zeros((8, LRU_W), F32)

        xv = x_ref[...]
        dov = dout_ref[...]
        dz = _mm_nt(dov, wout_ref[...])
        ogv = og_ref[...]

        def group_bwd(y, gate, g, dzg):
            r = lax.rsqrt(_row_mean(y * y) + EPS)
            n = y * r
            sg = _sigmoid(gate)
            dgate = dzg * (n * g) * (sg * (1.0 + gate * (1.0 - sg)))
            dng = dzg * (gate * sg)
            dn = dng * g
            return r * (dn - n * _row_mean(dn * n)), dgate, _col_sum(dng * n)

        dya, dga, goa = group_bwd(ya_ref[...], proj_ref[:, C_LRUG:C_LRUG + LRU_W], ogv[:, :512], dz[:, :512])
        dyb, dgb, gob = group_bwd(yb_ref[...], proj_ref[:, C_SWAG:C_SWAG + SWA_W], ogv[:, 512:768], dz[:, 512:768])
        dyc, dgc, goc = group_bwd(yc_ref[...], proj_ref[:, C_XG:C_XG + XATT_W], ogv[:, 768:], dz[:, 768:])
        gog_ref[...] += jnp.concatenate([goa, gob, goc], axis=1)
        dproj_ref[:, C_LRUG:C_LRUG + LRU_W] = dga.astype(dproj_ref.dtype)
        dproj_ref[:, C_SWAG:C_SWAG + SWA_W] = dgb.astype(dproj_ref.dtype)
        dproj_ref[:, C_XG:C_XG + XATT_W] = dgc.astype(dproj_ref.dtype)

        gm256 = _group_matrix(XATT_W)
        xq = proj_ref[:, C_XQ:C_XQ + XATT_W]
        rq = lax.rsqrt(_seg_mean(xq * xq, gm256) + EPS)
        qn = xq * rq
        qx = qn * gn_ref[:, G_XQ:G_XK]
        qxb = qx.astype(_MXU)
        dycb = dyc.astype(_MXU)
        dp_all = _mm_nt(dycb, vm_ref[...])
        dsm = []
        for j in range(4):
            pj = pmem_ref[:, MEM_LEN * j:MEM_LEN * (j + 1)].astype(F32)
            dp = dp_all[:, MEM_LEN * j:MEM_LEN * (j + 1)]
            dsm.append((pj * (dp - jnp.sum(pj * dp, axis=-1, keepdims=True))).astype(_MXU))
        ds_all = jnp.concatenate(dsm, axis=1)
        dvm_ref[...] += _mm_tn(dycb, pmem_ref[...])
        dkm_ref[...] += _mm_tn(qxb, ds_all)
        dqx = _mm(ds_all, km_ref[...])
        gxqn_ref[...] += _col_sum(dqx * qn)
        dqn = dqx * gn_ref[:, G_XQ:G_XK]
        dproj_ref[:, C_XQ:C_XQ + XATT_W] = (rq * (dqn - qn * _seg_mean(dqn * qn, gm256))).astype(dproj_ref.dtype)

        gm128 = _group_matrix(KV_W)
        cv, s1v, s2v = c_ref[...], s1_ref[...], s2_ref[...]

        def head_norm(t):
            r = lax.rsqrt(_seg_mean(t * t, gm128) + EPS)
            return t * r, r

        qn_, qr_ = zip(head_norm(proj_ref[:, C_SQ:C_SQ + 128]), head_norm(proj_ref[:, C_SQ + 128:C_SQ + 256]))
        qrope = [_rope(qn_[h] * gn_ref[:, G_Q:G_K], cv, s1v, s2v).astype(_MXU) for h in range(2)]
        kn, krr = head_norm(proj_ref[:, C_SK:C_SK + KV_W])
        kr = _rope(kn * gn_ref[:, G_K:G_XQ], cv, s1v, s2v)
        khn, _ = head_norm(kvh_ref[:, 0:KV_W])
        khr = _rope(khn * gn_ref[:, G_K:G_XQ], ch_ref[...], s1h_ref[...], s2h_ref[...])
        ka = _place_kv(jnp.concatenate([khr, kr], axis=0), 0.125)
        va = _place_kv(jnp.concatenate([kvh_ref[:, KV_W:2 * KV_W], proj_ref[:, C_SV:C_SV + KV_W]], axis=0), 1.0)
        lane128 = lax.broadcasted_iota(jnp.int32, (1, 128), 1)
        gsink = jnp.zeros((1, 128), F32)
        dk_band, dv_band, dq_blk = [], [], []
        for b in range(nb):
            band = slice(BLOCK * b, BLOCK * b + 2 * BLOCK)
            blk = slice(BLOCK * b, BLOCK * (b + 1))
            dka, dva, dsb = [], [], []
            deltas = jnp.zeros((BLOCK, 128), F32)
            for j in range(4):
                qh = qrope[j // 2][blk]
                doh = dyb[blk, KV_W * (j // 2):KV_W * (j // 2 + 1)].astype(_MXU)
                pb = pswa_ref[blk, 2 * BLOCK * j:2 * BLOCK * (j + 1)]
                p = pb.astype(F32)
                dp = _mm_nt(doh, va[j][band])
                delta = jnp.sum(p * dp, axis=-1, keepdims=True)
                ds = (p * (dp - delta)).astype(_MXU)
                deltas = jnp.where(lane128 == j, delta, deltas)
                dva.append(_mm_tn(pb, doh))
                dka.append(_mm_tn(ds, qh))
                dsb.append(ds)
            gsink = gsink - _col_sum(psink_ref[blk, :] * deltas)
            dk_band.append(_unplace_kv(dka) * 0.125)
            dv_band.append(_unplace_kv(dva))
            dq_blk.append([_mm(jnp.concatenate(dsb[2 * h:2 * h + 2], axis=1),
                               jnp.concatenate([ka[2 * h][band], ka[2 * h + 1][band]], axis=0)) for h in range(2)])
        gsink_ref[...] += gsink
        dk_rows = [dk_band[b][BLOCK:] + (dk_band[b + 1][:BLOCK] if b + 1 < nb else dkcar_ref[...]) for b in range(nb)]
        dv_rows = [dv_band[b][BLOCK:] + (dv_band[b + 1][:BLOCK] if b + 1 < nb else dvcar_ref[...]) for b in range(nb)]
        dkcar_ref[...] = dk_band[0][:BLOCK]
        dvcar_ref[...] = dv_band[0][:BLOCK]
        dkg = _rope_bwd(jnp.concatenate(dk_rows, axis=0), cv, s1v, s2v)
        gkn = _col_sum(dkg * kn)
        dkn = dkg * gn_ref[:, G_K:G_XQ]
        dproj_ref[:, C_SK:C_SK + KV_W] = (krr * (dkn - kn * _seg_mean(dkn * kn, gm128))).astype(dproj_ref.dtype)
        dproj_ref[:, C_SV:C_SV + KV_W] = jnp.concatenate(dv_rows, axis=0).astype(dproj_ref.dtype)
        gqn = jnp.zeros((1, 128), F32)
        for h in range(2):
            dqg = _rope_bwd(jnp.concatenate([dq_blk[b][h] for b in range(nb)], axis=0), cv, s1v, s2v)
            gqn = gqn + _col_sum(dqg * qn_[h])
            dqn_ = dqg * gn_ref[:, G_Q:G_K]
            dproj_ref[:, C_SQ + 128 * h:C_SQ + 128 * (h + 1)] = (
                qr_[h] * (dqn_ - qn_[h] * _seg_mean(dqn_ * qn_[h], gm128))).astype(dproj_ref.dtype)
        gqn_ref[...] += gqn
        gkn_ref[...] += gkn

        u = proj_ref[:, C_LRUX:C_LRUX + LRU_W]
        xc, rg, ig, sq = (gates_ref[:, LRU_W * k:LRU_W * (k + 1)].astype(F32) for k in range(4))
        a = a_ref[...]
        sp = _softplus(-lam_ref[...])
        hext_ref[0:8, :] = jnp.where(first_tile, 0.0, yah_ref[...])
        hext_ref[8:8 + tm, :] = ya_ref[...]
        hprev = hext_ref[pl.ds(7, tm), :]
        aext_ref[0:tm, :] = a
        an_scr[...] = aext_ref[pl.ds(1, tm), :]
        dh_scr[...] = dya
        dh_scr[tm - 1:tm, :] = dh_scr[tm - 1:tm, :] + gcar_ref[0:1, :]
        row8 = lax.broadcasted_iota(jnp.int32, (8, LRU_W), 0)

        def scan_step(gi, carry):
            r0 = pl.multiple_of((tm // 8 - 1 - gi) * 8, 8)
            av = an_scr[pl.ds(r0, 8), :]
            bv = dh_scr[pl.ds(r0, 8), :]
            for d in (1, 2, 4):
                a_sh = jnp.where(row8 < 8 - d, pltpu.roll(av, 8 - d, 0), 1.0)
                b_sh = jnp.where(row8 < 8 - d, pltpu.roll(bv, 8 - d, 0), 0.0)
                bv = bv + av * b_sh
                av = av * a_sh
            gv = bv + av * carry
            g_scr[pl.ds(r0, 8), :] = gv
            return gv[0:1, :]

        g0 = lax.fori_loop(0, tm // 8, scan_step, jnp.zeros((1, LRU_W), F32), unroll=True)
        gcar_ref[0:1, :] = a[0:1, :] * g0
        gv = g_scr[...]
        da = gv * hprev
        dig = gv * sq * xc
        dxc = gv * sq * ig
        dla = da * a - gv * (ig * xc) * ((a * a) / sq)
        drg = dla * ((-LRU_C) * sp)
        glam_ref[...] += _col_sum(dla * rg)
        dpr = drg * rg * (1.0 - rg)
        dpi = dig * ig * (1.0 - ig)
        gbrg_ref[...] += _col_sum(dpr)
        gbig_ref[...] += _col_sum(dpi)
        dpre0 = jnp.concatenate([dpr[:, :256], dpi[:, :256]], axis=1).astype(_MXU)
        dpre1 = jnp.concatenate([dpr[:, 256:], dpi[:, 256:]], axis=1).astype(_MXU)
        gwg_ref[0] += _mm_tn(xc[:, :256], dpre0)
        gwg_ref[1] += _mm_tn(xc[:, 256:], dpre1)
        dxc = dxc + jnp.concatenate([_mm_nt(dpre0, wg_ref[0]), _mm_nt(dpre1, wg_ref[1])], axis=1)
        gcb_ref[...] += _col_sum(dxc)
        dxc_ext[0:tm, :] = dxc
        du = jnp.zeros((tm, LRU_W), F32)
        for k in range(CONV_K):
            later = dxc_ext[pl.ds(3 - k, tm), :]
            gcw_ref[k:k + 1, :] += _col_sum(later * u)
            du = du + cw_ref[k:k + 1, :] * later
        dxc_ext[tm:tm + 8, :] = dxc[0:8, :]
        dproj_ref[:, C_LRUX:C_LRUX + LRU_W] = du.astype(dproj_ref.dtype)

        dxn = _mm(dproj_ref[...], win_ref[...])
        rx = lax.rsqrt(_row_mean(xv * xv) + EPS)
        xh = xv * rx
        gng_ref[...] += _col_sum(dxn * xh)
        dxh = dxn * ng_ref[...]
        gx_ref[...] = dov.astype(F32) + rx * (dxh - xh * _row_mean(dxh * xh))

        @pl.when(i == nt - 1)
        def _():
            glam_ref[...] = glam_ref[...] * (LRU_C * _sigmoid(-lam_ref[...]))
            for r in (gqn_ref, gkn_ref, gxqn_ref):
                r[...] = _fold_heads(r[...])

    def rows(ncol, arr_cols_block=0):
        return pl.BlockSpec((tm, ncol), lambda i: (nt - 1 - i, arr_cols_block))

    def halo(nrow, ncol, colblk=0):
        per = tm // nrow
        return pl.BlockSpec((nrow, ncol), lambda i: (jnp.maximum((nt - 1 - i) * per - 1, 0), colblk))

    in_specs = [rows(D_MODEL), rows(D_MODEL), rows(D_IN), rows(LRU_W), rows(SWA_W), rows(XATT_W),
                rows(4 * 2 * BLOCK), rows(4 * MEM_LEN), rows(128), rows(4 * LRU_W), rows(LRU_W),
                rows(128), rows(128), rows(128),
                halo(8, LRU_W), halo(BLOCK, 2 * KV_W, C_SK // (2 * KV_W)),
                halo(BLOCK, 128), halo(BLOCK, 128), halo(BLOCK, 128),
                _const_spec((1, D_MODEL)), _const_spec((D_IN, D_MODEL), True), _const_spec((CONV_K, LRU_W)),
                _const_spec((2, 256, 512), True), _const_spec((1, LRU_W)), _const_spec((1, GAINS_W)),
                _const_spec((4 * MEM_LEN, XATT_W), True), _const_spec((4 * MEM_LEN, XATT_W), True),
                _const_spec((1, D_MODEL)), _const_spec((D_MODEL, D_MODEL), True)]
    small = [(2, 256, 512), (XATT_W, 4 * MEM_LEN), (XATT_W, 4 * MEM_LEN), (1, D_MODEL), (1, D_MODEL), (1, LRU_W), (1, LRU_W),
             (1, LRU_W), (1, LRU_W), (CONV_K, LRU_W), (1, 128), (1, 128), (1, XATT_W), (1, 128)]
    out_shape = (jax.ShapeDtypeStruct((seq, D_MODEL), F32), jax.ShapeDtypeStruct((seq, D_IN), _MXU)) + tuple(
        jax.ShapeDtypeStruct(s, F32) for s in small)
    out_specs = (rows(D_MODEL), rows(D_IN)) + tuple(_const_spec(s) for s in small)
    scratch = [pltpu.VMEM((tm + 8, LRU_W), F32), pltpu.VMEM((tm + 8, LRU_W), F32),
               pltpu.VMEM((tm, LRU_W), F32), pltpu.VMEM((tm, LRU_W), F32), pltpu.VMEM((tm, LRU_W), F32),
               pltpu.VMEM((tm + 8, LRU_W), F32),
               pltpu.VMEM((8, LRU_W), F32), pltpu.VMEM((BLOCK, KV_W), F32), pltpu.VMEM((BLOCK, KV_W), F32)]
    return pl.pallas_call(
        body, name="layer_bwd", grid=(nt,), out_shape=out_shape, in_specs=in_specs, out_specs=out_specs,
        scratch_shapes=scratch,
        compiler_params=pltpu.CompilerParams(dimension_semantics=("arbitrary",), vmem_limit_bytes=VMEM_LIMIT),
    )(x, dout, proj, ya, yb, yc, pswa, pmem, psink, gates, a_all, rc, rs1, rs2, ya, proj, rc, rs1, rs2,
      ng, win_t, cw, wg, lam, gains, km, vm, og, wout)


def _reduce_protocol(big, sm, outs, osm, r1, r1s, wire, r2, r2s, wire2, ps, own, send, recv, lsem):
    nbig = len(big)
    x, y, c = lax.axis_index("x"), lax.axis_index("y"), lax.axis_index("c")
    sibling = (x, y, 1 - c)
    near, far, diag = _partners(x, y, c)
    me, near_id, far_id, diag_id = _chip_of(x, y), _chip_of(*near), _chip_of(*far), _chip_of(*diag)

    def copy(k, src, dst, to):
        return pltpu.make_async_remote_copy(src_ref=src, dst_ref=dst, send_sem=send.at[k], recv_sem=recv.at[k],
                                            device_id=to, device_id_type=MESH)

    def sent(stage, a):
        if a == nbig:
            src, dst, to = ((sm.at[1 - c], r1s, sibling), (r1s, r2s.at[0], (*near, c)), (ps, r2s.at[1], (*far, c)),
                            (osm.at[c], osm.at[c], sibling))[stage]
            return [copy(5 * nbig + stage, src, dst, to)]
        if stage == 0:
            return [copy(5 * a, big[a].at[:, 1 - c], r1[a], sibling)]
        if stage == 1:
            return [copy(5 * a + 1, wire[a].at[near_id], r2[a].at[0], (*near, c)),
                    copy(5 * a + 2, wire[a].at[diag_id], r2[a].at[1], (*near, c))]
        if stage == 2:
            return [copy(5 * a + 3, wire2[a], r2[a].at[2], (*far, c))]
        return [copy(5 * a + 4, outs[a].at[c], outs[a].at[c], sibling)]

    arrays = range(nbig + (sm is not None))

    def start(stage, a):
        for cp in sent(stage, a):
            cp.start()

    def arrived(k, ref):
        copy(k, ref, ref, sibling).wait_recv()

    def loads():
        return [pltpu.make_async_copy(big[a].at[:, c], own[a], lsem.at[a]) for a in range(nbig)]

    def stage0():
        for a in arrays:
            start(0, a)
        for cp in loads():
            cp.start()

    def stage1():
        for a in range(nbig):
            loads()[a].wait()
            arrived(5 * a, r1[a])
            for k in range(N_CHIPS):
                r1[a][k] = own[a][k] + r1[a][k]
                wire[a][k] = r1[a][k].astype(wire[a].dtype)
            start(1, a)
        if sm is not None:
            arrived(5 * nbig, r1s)
            r1s[...] = sm[c] + r1s[...]
            start(1, nbig)

    def stage2():
        for a in range(nbig):
            arrived(5 * a + 1, r2[a].at[0])
            arrived(5 * a + 2, r2[a].at[1])
            r1[a][me] = r1[a][me] + r2[a][0].astype(F32)
            wire2[a][...] = (r1[a][far_id] + r2[a][1].astype(F32)).astype(wire2[a].dtype)
            start(2, a)
        if sm is not None:
            arrived(5 * nbig + 1, r2s.at[0])
            ps[...] = r1s[...] + r2s[0]
            start(2, nbig)

    def stage3():
        for a in range(nbig):
            arrived(5 * a + 3, r2[a].at[2])
            outs[a][c] = r1[a][me] + r2[a][2].astype(F32)
            start(3, a)
        if sm is not None:
            arrived(5 * nbig + 2, r2s.at[1])
            osm[c] = ps[...] + r2s[1]
            start(3, nbig)

    def stage4():
        for a in range(nbig):
            arrived(5 * a + 4, outs[a].at[1 - c])
        if sm is not None:
            arrived(5 * nbig + 3, osm.at[1 - c])
        for stage in range(4):
            for a in arrays:
                for cp in sent(stage, a):
                    cp.wait_send()

    return [stage0, stage1, stage2, stage3, stage4]


def _reduce_buffers(bigs, g_small):
    half = [b.shape[2:] for b in bigs]
    sm_half = None if g_small is None else g_small.shape[1:]
    out_shape = [jax.ShapeDtypeStruct((2,) + h, F32) for h in half]
    small = lambda lead: [] if g_small is None else [pltpu.VMEM(lead + sm_half, F32)]
    if g_small is not None:
        out_shape.append(jax.ShapeDtypeStruct(g_small.shape, F32))
    n_sem = 5 * len(bigs) + 4
    scratch = ([pltpu.VMEM((N_CHIPS,) + h, F32) for h in half] + small(())
               + [pltpu.VMEM((N_CHIPS,) + h, _WIRE) for h in half]
               + [pltpu.VMEM((3,) + h, _WIRE) for h in half] + small((2,))
               + [pltpu.VMEM(h, _WIRE) for h in half] + small(())
               + [pltpu.VMEM((N_CHIPS,) + h, F32) for h in half]
               + [pltpu.SemaphoreType.DMA((n_sem,)), pltpu.SemaphoreType.DMA((n_sem,)),
                  pltpu.SemaphoreType.DMA((len(bigs),))])
    return out_shape, scratch


def _split_reduce_refs(refs, nbig, has_small):
    it = iter(refs)
    take = lambda n: [next(it) for _ in range(n)]
    one = lambda: next(it) if has_small else None
    big, sm = take(nbig), one()
    outs, osm = take(nbig), one()
    r1, r1s, wire, r2, r2s, wire2, ps, own = take(nbig), one(), take(nbig), take(nbig), one(), take(nbig), one(), take(nbig)
    send, recv, lsem = take(3)
    return big, sm, outs, osm, r1, r1s, wire, r2, r2s, wire2, ps, own, send, recv, lsem


def _hosted_reduce_shapes(bigs, g_small):
    red_shape, scratch = _reduce_buffers(bigs, g_small)
    nres = len(red_shape)
    return red_shape, [pltpu.VMEM(r.shape, r.dtype) for r in red_shape] + scratch + [pltpu.SemaphoreType.DMA((nres,))]


def _hosted_reduce(step, n_steps, closing, stage_at, operands, results, scratch, has_small):
    nres = len(results)
    sums, rest, fsem = scratch[:nres], scratch[nres:-1], scratch[-1]
    refs = tuple(operands) + tuple(sums) + tuple(rest)

    def to_results():
        out = [pltpu.make_async_copy(sums[k], results[k], fsem.at[k]) for k in range(nres)]
        for cp in out:
            cp.start()
        for cp in out:
            cp.wait()

    stages = _reduce_protocol(*_split_reduce_refs(refs, nres - has_small, has_small))

    def last_stage():
        stages[-1]()
        to_results()

    for at, stage in zip(stage_at, stages[:-1] + [last_stage]):
        if closing == (at == n_steps):
            pl.when(step == min(at, n_steps - 1))(stage)


def reduce_grads(big, name, parts):
    chips, halves, rows_, cols = big.shape
    sub = jax.ShapeDtypeStruct((chips, halves, rows_ // parts, cols), big.dtype)

    def body(b_ref, o_ref, *scratch):
        refs = [b_ref.at[:, :, s] for s in range(parts)] + [o_ref.at[:, s] for s in range(parts)] + list(scratch)
        for stage in _reduce_protocol(*_split_reduce_refs(refs, parts, False)):
            stage()

    _, scratch = _reduce_buffers([sub] * parts, None)
    return pl.pallas_call(
        body, name=name, out_shape=jax.ShapeDtypeStruct((halves, parts, rows_ // parts, cols), F32),
        in_specs=[pl.BlockSpec(memory_space=pl.ANY)], out_specs=pl.BlockSpec(memory_space=pltpu.VMEM),
        scratch_shapes=scratch, compiler_params=pltpu.CompilerParams(vmem_limit_bytes=VMEM_LIMIT),
    )(big.reshape(chips, halves, parts, rows_ // parts, cols))


def adamw(items, g_pack, ws, ms, vs):
    blocks = []
    for k, (w, _, _, _) in enumerate(items):
        rows_, cols = w.shape
        tr = max(t for t in range(8, rows_ + 1, 8) if rows_ % t == 0 and t * cols * 4 <= ADAM_BLOCK_BYTES)
        blocks += [(k, r, tr) for r in range(0, rows_, tr)]
    nin, n = 4 * len(items), len(ws)

    def body(*refs):
        mats_in, small_in = refs[:nin], refs[nin:nin + 1 + 3 * n]
        n_out = nin + 4 * n + 1
        outs, scratch = refs[nin + 1 + 3 * n:nin + 1 + 3 * n + n_out], refs[nin + 1 + 3 * n + n_out:]
        buf, (lsem, ssem) = scratch[:nin], scratch[nin:]
        loads = [[pltpu.make_async_copy(mats_in[4 * k + q].at[pl.ds(r, tr)], buf[4 * k + q].at[pl.ds(r, tr)],
                                        lsem.at[4 * c + q]) for q in range(4)] for c, (k, r, tr) in enumerate(blocks)]
        for cps in loads:
            for cp in cps:
                cp.start()
        _adamw_small(small_in[0], *(small_in[1 + k * n:1 + (k + 1) * n] for k in range(3)),
                     *(outs[nin + k * n:nin + (k + 1) * n] for k in range(4)), outs[-1])
        stores = []
        for c, (k, r, tr) in enumerate(blocks):
            for cp in loads[c]:
                cp.wait()
            w_ref, g_ref, m_ref, v_ref = (buf[4 * k + q].at[pl.ds(r, tr)] for q in range(4))
            w_ref[...], m_ref[...], v_ref[...] = _adam_update(w_ref[...], g_ref[...], m_ref[...], v_ref[...])
            for q, src in enumerate((g_ref, w_ref, m_ref, v_ref)):
                stores.append(pltpu.make_async_copy(src, outs[4 * k + q].at[pl.ds(r, tr)], ssem.at[4 * c + q]))
                stores[-1].start()
        for cp in stores:
            cp.wait()

    shapes = [jax.ShapeDtypeStruct(w.shape, F32) for w, _, _, _ in items for _ in range(4)]
    vm = pl.BlockSpec(memory_space=pltpu.VMEM)
    hbm = pl.BlockSpec(memory_space=pl.ANY)
    like = [jax.ShapeDtypeStruct(w.shape, F32) for w in ws]
    res = pl.pallas_call(
        body, name="adamw", out_shape=(*shapes, *like * 4, jax.ShapeDtypeStruct((1, 1), F32)),
        in_specs=[hbm] * nin + [vm] * (1 + 3 * n), out_specs=(*[hbm] * nin, *[vm] * (4 * n + 1)),
        scratch_shapes=[pltpu.VMEM(s.shape, F32) for s in shapes] + [pltpu.SemaphoreType.DMA((4 * len(blocks),))] * 2,
        compiler_params=pltpu.CompilerParams(vmem_limit_bytes=VMEM_LIMIT),
    )(*[a for item in items for a in item], g_pack, *ws, *ms, *vs)
    return [res[4 * k:4 * k + 4] for k in range(len(items))], res[nin:]


def _adam_update(w, g, m, v):
    nm = ADAM_B1 * m + (1.0 - ADAM_B1) * g
    nv = ADAM_B2 * v + (1.0 - ADAM_B2) * (g * g)
    m_hat = nm / (1.0 - ADAM_B1 ** ADAM_STEP)
    v_hat = nv / (1.0 - ADAM_B2 ** ADAM_STEP)
    return (-ADAM_LR) * (m_hat / (jnp.sqrt(v_hat) + ADAM_EPS) + ADAM_WD * w), nm, nv


def _adamw_small(pk, w_refs, m_refs, v_refs, g_out, d_out, nm_out, nv_out, loss_ref):
    nvec = len(SMALL_VECTORS)
    loss_ref[...] = pk[LOSS_ROW:LOSS_ROW + 1, 0:1]
    chip = 2 * lax.axis_index("x") + lax.axis_index("y")
    for k, (name, row, width) in enumerate(SMALL_VECTORS):
        if name == "conv_w":
            g = jnp.concatenate([pk[pl.ds(row + 4 * t + chip, 1), :] for t in range(CONV_K)], axis=0)[None]
        elif width >= 128:
            g = jnp.concatenate([pk[row + r:row + r + 1, :] for r in range(width // 128)], axis=1)
        else:
            g = pk[row:row + 1, 0:width]
        g_out[k][...] = g
        d_out[k][...], nm_out[k][...], nv_out[k][...] = _adam_update(w_refs[k][...], g, m_refs[k][...], v_refs[k][...])
    for k in range(nvec, nvec + len(SMALL_MATRICES)):
        for b in range(LRU_BLOCKS):
            rows_ = pk[GATES_ROW + HEAD * b:GATES_ROW + HEAD * (b + 1), :]
            g = (pltpu.roll(rows_, HEAD, axis=1) if k > nvec else rows_)[:, 0:HEAD]
            g_out[k][0, b] = g
            d_out[k][0, b], nm_out[k][0, b], nv_out[k][0, b] = _adam_update(
                w_refs[k][0, b], g, m_refs[k][0, b], v_refs[k][0, b])


SMALL_VECTORS = (("norm_g", 512, 1024), ("mem_norm_g", 520, 1024), ("conv_w", 528, 512), ("conv_b", 544, 512),
                 ("b_rg", 548, 512), ("b_ig", 552, 512), ("lru_lambda", 556, 512), ("q_norm_g", 560, 64),
                 ("k_norm_g", 561, 64), ("sinks", 562, 4), ("xq_norm_g", 563, 64), ("xk_norm_g", 564, 64),
                 ("out_norm_g", 565, 1024))
LOSS_ROW = 573
SMALL_MATRICES = ("w_rg", "w_ig")
GATES_ROW = 0
SMALL_ROWS = 640


def _rope_tables(seq):
    pos = np.arange(seq, dtype=np.float32)
    inv_freq = (np.float32(ROPE_THETA) ** (-(np.arange(0, ROPE_DIM, 2, dtype=np.float32) / np.float32(ROPE_DIM)))
                ).astype(np.float32)
    ang = (pos[:, None] * inv_freq[None, :]).astype(np.float32)
    cos, sin = np.cos(ang).astype(np.float32), np.sin(ang).astype(np.float32)
    z = lambda n: np.zeros((seq, n), np.float32)
    c64 = np.concatenate([cos, cos, np.ones((seq, HEAD - ROPE_DIM), np.float32)], axis=1)
    s1_64 = np.concatenate([-sin, z(HEAD - 8)], axis=1)
    s2_64 = np.concatenate([z(8), sin, z(HEAD - ROPE_DIM)], axis=1)
    return tuple(jnp.asarray(np.concatenate([t, t], axis=1)) for t in (c64, s1_64, s2_64))


def kernel(x, mem, norm_g, mem_norm_g, w_in, conv_w, conv_b, w_rg, b_rg, w_ig, b_ig, lru_lambda, q_norm_g, k_norm_g, sinks, w_mem_kv, xq_norm_g, xk_norm_g, out_norm_g, w_out, loss_target, m_norm_g, m_mem_norm_g, m_w_in, m_conv_w, m_conv_b, m_w_rg, m_b_rg, m_w_ig, m_b_ig, m_lru_lambda, m_q_norm_g, m_k_norm_g, m_sinks, m_w_mem_kv, m_xq_norm_g, m_xk_norm_g, m_out_norm_g, m_w_out, v_norm_g, v_mem_norm_g, v_w_in, v_conv_w, v_conv_b, v_w_rg, v_b_rg, v_w_ig, v_b_ig, v_lru_lambda, v_q_norm_g, v_k_norm_g, v_sinks, v_w_mem_kv, v_xq_norm_g, v_xk_norm_g, v_out_norm_g, v_w_out):
    seq = x.shape[1]
    xs, tgt, mems = x[0], loss_target[0], mem[0]

    win_t, wout, wkv, cw, wg, gains, km, vm = gather_weights(
        w_in[0].T, w_out[0], w_mem_kv[0], conv_w, w_rg, w_ig, (q_norm_g, k_norm_g, xq_norm_g, xk_norm_g), mems,
        mem_norm_g)
    rc, rs1, rs2 = _rope_tables(seq)
    proj, ya, yb, yc, ycat, xn, dout, pswa, pmem, psink, gates, a_all, loss8 = layer_fwd(
        xs, tgt, rc, rs1, rs2, norm_g, win_t, cw, conv_b, wg, b_rg, b_ig, lru_lambda, gains, sinks, km, vm,
        out_norm_g, wout)
    (gx, dproj, g_wg, dkm, dvm, g_ng, g_og, g_cb, g_brg, g_big, g_lam, g_cw, g_qn, g_kn, g_xqn, g_sink) = layer_bwd(
        xs, dout, proj, ya, yb, yc, pswa, pmem, psink, gates, a_all, rc, rs1, rs2, norm_g, win_t, cw, wg, lru_lambda,
        gains, km, vm, out_norm_g, wout)
    g_win_t, r_out, r_kv, r_small = weight_grads(
        ycat, dout, dproj, xn, (mems, mem_norm_g, wkv, gains, dkm, dvm, g_wg, loss8), dict(
            norm_g=g_ng, conv_w=g_cw, conv_b=g_cb, b_rg=g_brg, b_ig=g_big, lru_lambda=g_lam, q_norm_g=g_qn,
            k_norm_g=g_kn, sinks=g_sink, xq_norm_g=g_xqn, out_norm_g=g_og), (0, 1, 4, 7, 8), (4, 5, 9, 11, 13))
    r_in = reduce_grads(g_win_t.reshape(N_CHIPS, 2, D_IN // 8, D_MODEL), "reduce_w_in", 6)

    r_small = r_small.reshape(SMALL_ROWS, 128)
    grads = {}
    weights = dict(norm_g=norm_g, mem_norm_g=mem_norm_g, w_in=w_in, conv_w=conv_w, conv_b=conv_b, w_rg=w_rg, b_rg=b_rg,
                   w_ig=w_ig, b_ig=b_ig, lru_lambda=lru_lambda, q_norm_g=q_norm_g, k_norm_g=k_norm_g, sinks=sinks,
                   w_mem_kv=w_mem_kv, xq_norm_g=xq_norm_g, xk_norm_g=xk_norm_g, out_norm_g=out_norm_g, w_out=w_out)
    ms = dict(norm_g=m_norm_g, mem_norm_g=m_mem_norm_g, w_in=m_w_in, conv_w=m_conv_w, conv_b=m_conv_b, w_rg=m_w_rg,
              b_rg=m_b_rg, w_ig=m_w_ig, b_ig=m_b_ig, lru_lambda=m_lru_lambda, q_norm_g=m_q_norm_g, k_norm_g=m_k_norm_g,
              sinks=m_sinks, w_mem_kv=m_w_mem_kv, xq_norm_g=m_xq_norm_g, xk_norm_g=m_xk_norm_g,
              out_norm_g=m_out_norm_g, w_out=m_w_out)
    vs = dict(norm_g=v_norm_g, mem_norm_g=v_mem_norm_g, w_in=v_w_in, conv_w=v_conv_w, conv_b=v_conv_b, w_rg=v_w_rg,
              b_rg=v_b_rg, w_ig=v_w_ig, b_ig=v_b_ig, lru_lambda=v_lru_lambda, q_norm_g=v_q_norm_g, k_norm_g=v_k_norm_g,
              sinks=v_sinks, w_mem_kv=v_w_mem_kv, xq_norm_g=v_xq_norm_g, xk_norm_g=v_xk_norm_g,
              out_norm_g=v_out_norm_g, w_out=v_w_out)

    delta, new_m, new_v = {}, {}, {}
    small_names = [n for n, _, _ in SMALL_VECTORS] + list(SMALL_MATRICES)
    (res_in, res_out, res_kv), res = adamw(
        [(w_in[0].T, r_in.reshape(D_IN // 4, D_MODEL), m_w_in[0].T, v_w_in[0].T),
         (w_out[0], r_out.reshape(D_MODEL // 4, D_MODEL), m_w_out[0], v_w_out[0]),
         (w_mem_kv[0], r_kv.reshape(D_MODEL // 4, 2 * XATT_W), m_w_mem_kv[0], v_w_mem_kv[0])],
        r_small, [weights[n] for n in small_names], [ms[n] for n in small_names], [vs[n] for n in small_names])
    grads["w_in"], delta["w_in"], new_m["w_in"], new_v["w_in"] = (r.T[None] for r in res_in)
    grads["w_out"], delta["w_out"], new_m["w_out"], new_v["w_out"] = (r[None] for r in res_out)
    grads["w_mem_kv"], delta["w_mem_kv"], new_m["w_mem_kv"], new_v["w_mem_kv"] = (r[None] for r in res_kv)
    nall = len(small_names)
    for k, into in enumerate((grads, delta, new_m, new_v)):
        into.update(zip(small_names, res[k * nall:(k + 1) * nall]))
    loss = res[-1].reshape(())

    order = ("norm_g", "mem_norm_g", "w_in", "conv_w", "conv_b", "w_rg", "b_rg", "w_ig", "b_ig", "lru_lambda",
             "q_norm_g", "k_norm_g", "sinks", "w_mem_kv", "xq_norm_g", "xk_norm_g", "out_norm_g", "w_out")
    return (loss, gx[None], *[grads[n] for n in order], *[delta[n] for n in order], *[new_m[n] for n in order],
            *[new_v[n] for n in order])
```

```python
import jax
import jax.numpy as jnp
import numpy as np
from jax import lax
from jax.experimental import pallas as pl
from jax.experimental.pallas import tpu as pltpu

F32 = jnp.float32
_MXU = jnp.bfloat16
_WIRE = jnp.bfloat16

D_MODEL = 1024
MEM_LEN = 256
HEAD = 64
LRU_W = 512
LRU_BLOCKS = 8
CONV_K = 4
LRU_C = 8.0
SWA_W = 256
KV_W = 128
XATT_W = 256
BLOCK = 128
D_IN = 2304
ROPE_THETA = 500000.0
ROPE_DIM = 16
EPS = 1e-6
NEG_INF = -1e30
C_LRUX, C_LRUG, C_SQ, C_SK, C_SV, C_SWAG, C_XQ, C_XG = 0, 512, 1024, 1280, 1408, 1536, 1792, 2048
G_Q, G_K, G_XQ, G_XK, GAINS_W = 0, 128, 256, 512, 768

ADAM_LR, ADAM_B1, ADAM_B2, ADAM_EPS, ADAM_WD, ADAM_STEP = 0.001, 0.9, 0.999, 1e-08, 0.01, 10

N_CHIPS = 4
ROW_TILE = 256
VMEM_LIMIT = 56 * 1024 * 1024
ADAM_BLOCK_BYTES = 1280 * 1024
MESH = pl.DeviceIdType.MESH


def _mm(a, b):
    return jnp.dot(a.astype(_MXU), b.astype(_MXU), preferred_element_type=F32)


def _mm_nt(a, b):
    return lax.dot_general(a.astype(_MXU), b.astype(_MXU), (((1,), (1,)), ((), ())), preferred_element_type=F32)


def _mm_tn(a, b):
    return lax.dot_general(a.astype(_MXU), b.astype(_MXU), (((0,), (0,)), ((), ())), preferred_element_type=F32)


def _group_matrix(width):
    r = lax.shift_right_logical(lax.broadcasted_iota(jnp.int32, (width, width), 0), 6)
    c = lax.shift_right_logical(lax.broadcasted_iota(jnp.int32, (width, width), 1), 6)
    return (r == c).astype(_MXU)


def _seg_mean(x, gm):
    return jnp.dot(x.astype(_MXU), gm, preferred_element_type=F32) * (1.0 / HEAD)


def _row_mean(x):
    return jnp.mean(x, axis=-1, keepdims=True)


def _col_sum(x):
    return jnp.sum(x, axis=0, keepdims=True)


def _sigmoid(x):
    return jax.nn.sigmoid(x)


def _softplus(z):
    e = jnp.exp(-jnp.abs(z))
    u = 1.0 + e
    log1p_e = jnp.where(u == 1.0, e, jnp.log(u) * (e / (u - 1.0)))
    return jnp.maximum(z, 0.0) + log1p_e


def _rope(t, c, s1, s2):
    return t * c + pltpu.roll(t, 120, 1) * s1 + pltpu.roll(t, 8, 1) * s2


def _rope_bwd(d, c, s1, s2):
    return d * c + pltpu.roll(d * s1, 8, 1) + pltpu.roll(d * s2, 120, 1)


def _fold_heads(v):
    out = v
    for k in range(1, v.shape[1] // HEAD):
        out = out + pltpu.roll(v, HEAD * k, 1)
    return out


def _lane_mask(width, lo, hi):
    lane = lax.broadcasted_iota(jnp.int32, (1, width), 1)
    return ((lane >= lo) & (lane < hi)).astype(F32)


def _swa_mask(first_block):
    qi = lax.broadcasted_iota(jnp.int32, (BLOCK, 2 * BLOCK), 0)
    kj = lax.broadcasted_iota(jnp.int32, (BLOCK, 2 * BLOCK), 1)
    rel = qi + BLOCK - kj
    ok = (rel >= 0) & (rel < BLOCK)
    return ok & (jnp.logical_not(first_block) | (kj >= BLOCK))


def _place_kv(t, scale):
    lo = t * (_lane_mask(KV_W, 0, HEAD) * scale)
    hi = t * (_lane_mask(KV_W, HEAD, KV_W) * scale)
    return [a.astype(_MXU) for a in (lo, pltpu.roll(lo, HEAD, 1), pltpu.roll(hi, HEAD, 1), hi)]


def _unplace_kv(d):
    return (_lane_mask(KV_W, 0, HEAD) * (d[0] + pltpu.roll(d[1], HEAD, 1))
            + _lane_mask(KV_W, HEAD, KV_W) * (d[3] + pltpu.roll(d[2], HEAD, 1)))


def _swa_probs(qh, ka, mask, sink):
    s = _mm_nt(qh, ka)
    s = jnp.where(mask, s, NEG_INF)
    m = jnp.maximum(jnp.max(s, axis=-1, keepdims=True), sink)
    p = jnp.exp(s - m)
    esink = jnp.exp(sink - m)
    inv = 1.0 / (jnp.sum(p, axis=-1, keepdims=True) + esink)
    return p * inv, esink * inv


def _mem_probs(s_all):
    out = []
    for j in range(4):
        s = s_all[:, MEM_LEN * j:MEM_LEN * (j + 1)]
        p = jnp.exp(s - jnp.max(s, axis=-1, keepdims=True))
        out.append(p * (1.0 / jnp.sum(p, axis=-1, keepdims=True)))
    return out


def _head_rows(t, scale):
    return jnp.concatenate([t * (_lane_mask(XATT_W, HEAD * j, HEAD * (j + 1)) * scale) for j in range(4)], axis=0)


def _lru_gates(xc, wg_ref, brg, big, lam):
    p0 = _mm(xc[:, :256], wg_ref[0])
    p1 = _mm(xc[:, 256:], wg_ref[1])
    rg = _sigmoid(jnp.concatenate([p0[:, :256], p1[:, :256]], axis=1) + brg)
    ig = _sigmoid(jnp.concatenate([p0[:, 256:], p1[:, 256:]], axis=1) + big)
    sp = _softplus(-lam)
    la = (-LRU_C) * rg * sp
    a = jnp.exp(la)
    th = jnp.tanh(la)
    one_minus_a2 = (-2.0 * th) / (1.0 - th)
    return rg, ig, sp, a, jnp.sqrt(one_minus_a2)


def _const_spec(shape, single=False):
    zeros = (0,) * len(shape)
    if single:
        return pl.BlockSpec(shape, lambda i: zeros, pipeline_mode=pl.Buffered(1))
    return pl.BlockSpec(shape, lambda i: zeros)


def _chip_of(x, y):
    return 2 * x + y


def _partners(x, y, c):
    north = c == 1
    near = (jnp.where(north, 1 - x, x), jnp.where(north, y, 1 - y))
    far = (jnp.where(north, x, 1 - x), jnp.where(north, 1 - y, y))
    return near, far, (1 - x, 1 - y)


def gather_weights(win_t, wout, wkv, conv_w, w_rg, w_ig, head_gains, mem, mem_g):
    arrs = (win_t, wout, wkv)
    n = len(arrs)
    pieces = [(a, 0, arr.shape[0] // 2) for a, arr in enumerate(arrs)]
    npc = len(pieces)

    def body(a0, a1, a2, cw_in, wrg_ref, wig_ref, q_ref, k_ref, xq_ref, xk_ref, mem_ref, mg_ref,
             o0, o1, o2, cw_out, wg_ref, gn_ref, km_ref, vm_ref, s0, s1, s2, cw, ocw, send, recv, lsem):
        ins, outs = (s0, s1, s2), (o0, o1, o2)
        for src, dst in zip((a0, a1, a2), ins):
            dst[...] = src[...].astype(dst.dtype)
        cw[...] = jnp.zeros(cw.shape, F32)
        cw[0:CONV_K, :] = cw_in[0]
        x, y, c = lax.axis_index("x"), lax.axis_index("y"), lax.axis_index("c")
        sibling = (x, y, 1 - c)
        near, far, diag = _partners(x, y, c)
        chips = [near, far, diag]
        me = _chip_of(x, y)

        def landed(p, chip, half):
            a, off, rows_ = pieces[p]
            r = ins[a].shape[0]
            return outs[a].at[pl.ds(pl.multiple_of(chip * r + half * (r // 2) + off, 16), rows_)]

        def mine(p):
            a, off, rows_ = pieces[p]
            return ins[a].at[pl.ds(pl.multiple_of(c * (ins[a].shape[0] // 2) + off, 16), rows_)]

        def copy(k, src, dst, to):
            return pltpu.make_async_remote_copy(src_ref=src, dst_ref=dst, send_sem=send.at[k], recv_sem=recv.at[k],
                                                device_id=to, device_id_type=MESH)

        def cw_rows(chip):
            return ocw.at[pl.ds(pl.multiple_of(chip * 8, 8), 8)]

        locals_ = []
        for a in range(n):
            r = ins[a].shape[0]
            locals_.append(pltpu.make_async_copy(ins[a], outs[a].at[pl.ds(pl.multiple_of(me * r, 16), r)], lsem.at[a]))
        locals_.append(pltpu.make_async_copy(cw, cw_rows(me), lsem.at[n]))
        for cp in locals_:
            cp.start()

        sent = []
        for p in range(npc):
            for j in range(2):
                sent.append(copy(p * 6 + j, mine(p), landed(p, me, c), (*chips[j], c)))
        for j, chip in enumerate(chips):
            sent.append(copy(npc * 6 + j, cw, cw_rows(me), (*chip, c)))
        for cp in sent:
            cp.start()

        gn_ref[...] = jnp.concatenate([q_ref[...]] * 2 + [k_ref[...]] * 2 + [xq_ref[...]] * 4 + [xk_ref[...]] * 4,
                                      axis=1)
        zeros = lambda lanes: [jnp.zeros((HEAD, lanes), F32)] if lanes else []
        for h in range(2):
            for b in range(4):
                row = []
                for w_ref in (wrg_ref, wig_ref):
                    row += zeros(HEAD * b) + [w_ref[0, 4 * h + b]] + zeros(HEAD * (3 - b))
                wg_ref[h, HEAD * b:HEAD * (b + 1), :] = jnp.concatenate(row, axis=1).astype(wg_ref.dtype)

        for j in range(3):
            for p in range(npc):
                got = landed(p, _chip_of(*chips[j]), c)
                copy(p * 6 + j, got, got, sibling).wait_recv()
                if j == 0:
                    sent.append(copy(p * 6 + 2, got, got, (*far, c)))
                    sent[-1].start()
                sent.append(copy(p * 6 + 3 + j, got, got, sibling))
                sent[-1].start()
        for p in range(npc):
            for j in range(3):
                got = landed(p, _chip_of(*chips[(1, 0, 2)[j]]), 1 - c)
                copy(p * 6 + 3 + j, got, got, sibling).wait_recv()
        for j, chip in enumerate(chips):
            got = cw_rows(_chip_of(*chip))
            copy(npc * 6 + j, got, got, (*chip, c)).wait_recv()
        for cp in sent:
            cp.wait_send()
        for cp in locals_:
            cp.wait()
        for chip in range(N_CHIPS):
            cw_out[:, 128 * chip:128 * (chip + 1)] = ocw[8 * chip:8 * chip + CONV_K, :]

        mem_v = mem_ref[...]
        mn = mem_v * lax.rsqrt(_row_mean(mem_v * mem_v) + EPS) * mg_ref[...]
        mkv = _mm(mn, o2[...])
        kpre = mkv[:, :XATT_W]
        km = kpre * lax.rsqrt(_seg_mean(kpre * kpre, _group_matrix(XATT_W)) + EPS) * gn_ref[:, G_XK:GAINS_W]
        km_ref[...] = _head_rows(km, 0.125).astype(km_ref.dtype)
        vm_ref[...] = _head_rows(mkv[:, XATT_W:], 1.0).astype(vm_ref.dtype)

    vm = pl.BlockSpec(memory_space=pltpu.VMEM)
    hbm = pl.BlockSpec(memory_space=pl.ANY)
    head_rows = jax.ShapeDtypeStruct((4 * MEM_LEN, XATT_W), _MXU)
    out_shape = tuple(jax.ShapeDtypeStruct((N_CHIPS * a.shape[0],) + a.shape[1:], _MXU) for a in arrs) + (
        jax.ShapeDtypeStruct((CONV_K, LRU_W), F32), jax.ShapeDtypeStruct((2, 256, 512), _MXU),
        jax.ShapeDtypeStruct((1, GAINS_W), F32), head_rows, head_rows)
    n_rdma = npc * 6 + 3
    return pl.pallas_call(
        body, name="gather_weights", out_shape=out_shape,
        in_specs=[vm] * 12, out_specs=(hbm, hbm, vm, vm, vm, vm, vm, vm),
        scratch_shapes=[pltpu.VMEM(a.shape, _MXU) for a in arrs] + [
            pltpu.VMEM((8, 128), F32), pltpu.VMEM((N_CHIPS * 8, 128), F32),
            pltpu.SemaphoreType.DMA((n_rdma,)), pltpu.SemaphoreType.DMA((n_rdma,)), pltpu.SemaphoreType.DMA((n + 1,))],
        compiler_params=pltpu.CompilerParams(vmem_limit_bytes=VMEM_LIMIT),
    )(win_t, wout, wkv, conv_w, w_rg, w_ig, *head_gains, mem, mem_g)


def _mem_bwd_and_pack(mem_ref, g_ref, w_ref, gn_ref, dkm_ref, dvm_ref, gg_ref, loss_ref, vectors, gw_ref, pk_ref):
    first_row = {name: (row, width) for name, row, width in SMALL_VECTORS}
    half_rows = SMALL_ROWS // 2
    pk_ref[...] = jnp.zeros(pk_ref.shape, F32)

    def rows_at(at, n):
        assert at // half_rows == (at + n - 1) // half_rows
        return at // half_rows, slice(at % half_rows, at % half_rows + n), slice(None)

    def put(name, src):
        row, width = first_row[name]
        per_row = 1 if width < 128 else src.shape[1] // 128
        for t in range(src.shape[0]):
            for r in range(per_row):
                pk_ref[rows_at(row + per_row * t + r, 1)] = src[t:t + 1, 128 * r:128 * (r + 1)]

    for name, ref in vectors.items():
        put(name, ref)
    pk_ref[rows_at(LOSS_ROW, 1)] = loss_ref[0:1, :]
    upper = lax.broadcasted_iota(jnp.int32, (HEAD, 128), 1) >= HEAD
    for h in range(2):
        for b in range(4):
            rg = gg_ref[h, HEAD * b:HEAD * (b + 1), 128 * (b // 2):128 * (b // 2 + 1)]
            ig = gg_ref[h, HEAD * b:HEAD * (b + 1), 256 + 128 * (b // 2):256 + 128 * (b // 2 + 1)]
            if b % 2:
                rg = pltpu.roll(rg, HEAD, axis=1)
            else:
                ig = pltpu.roll(ig, HEAD, axis=1)
            pk_ref[rows_at(GATES_ROW + HEAD * (4 * h + b), HEAD)] = jnp.where(upper, ig, rg)

    mem_v = mem_ref[...]
    mh = mem_v * lax.rsqrt(_row_mean(mem_v * mem_v) + EPS)
    mn = mh * g_ref[...]
    mkv = _mm(mn, w_ref[...])
    kpre = mkv[:, :XATT_W]
    gm = _group_matrix(XATT_W)
    rk = lax.rsqrt(_seg_mean(kpre * kpre, gm) + EPS)
    kn = kpre * rk
    dk = jnp.zeros((MEM_LEN, XATT_W), F32)
    dv = jnp.zeros((MEM_LEN, XATT_W), F32)
    for j in range(4):
        mj = _lane_mask(XATT_W, HEAD * j, HEAD * (j + 1))
        dk = dk + dkm_ref[:, MEM_LEN * j:MEM_LEN * (j + 1)].T * (mj * 0.125)
        dv = dv + dvm_ref[:, MEM_LEN * j:MEM_LEN * (j + 1)].T * mj
    put("xk_norm_g", _fold_heads(_col_sum(dk * kn)))
    dkn = dk * gn_ref[:, G_XK:GAINS_W]
    dkpre = rk * (dkn - kn * _seg_mean(dkn * kn, gm))
    dmkv = jnp.concatenate([dkpre, dv], axis=1)
    gw_ref[...] = _mm_tn(mn, dmkv).reshape(gw_ref.shape)
    dmn = _mm_nt(dmkv, w_ref[...])
    put("mem_norm_g", _col_sum(dmn * mh))


def layer_fwd(x, tgt, rc, rs1, rs2, ng, win_t, cw, cb, wg, brg, big, lam, gains, sinks, km, vm, og, wout):
    seq = x.shape[0]
    tm = min(ROW_TILE, seq)
    nt = seq // tm
    nb = tm // BLOCK

    def body(x_ref, t_ref, c_ref, s1_ref, s2_ref, ng_ref, win_ref, cw_ref, cb_ref, wg_ref, brg_ref, big_ref, lam_ref,
             gn_ref, sink_ref, km_ref, vm_ref, og_ref, wout_ref,
             proj_ref, ya_ref, yb_ref, yc_ref, ycat_ref, xn_ref, dout_ref, pswa_ref, pmem_ref, psink_ref, gates_ref,
             a_ref, loss_ref,
             ext_ref, b_scr, hc_ref, kp_ref, vp_ref, lacc_ref):
        i = pl.program_id(0)

        @pl.when(i == 0)
        def _():
            ext_ref[0:8, :] = jnp.zeros((8, LRU_W), F32)
            hc_ref[...] = jnp.zeros_like(hc_ref)
            kp_ref[...] = jnp.zeros_like(kp_ref)
            vp_ref[...] = jnp.zeros_like(vp_ref)
            lacc_ref[...] = jnp.zeros_like(lacc_ref)

        xv = x_ref[...]
        xn = (xv * lax.rsqrt(_row_mean(xv * xv) + EPS) * ng_ref[...]).astype(_MXU)
        xn_ref[...] = xn.astype(xn_ref.dtype)
        proj_ref[...] = _mm_nt(xn, win_ref[...])

        u = proj_ref[:, C_LRUX:C_LRUX + LRU_W]
        ext_ref[8:8 + tm, :] = u
        xc = cb_ref[...]
        for k in range(CONV_K):
            xc = xc + cw_ref[k:k + 1, :] * ext_ref[pl.ds(5 + k, tm), :]
        ext_ref[0:8, :] = u[tm - 8:tm, :]
        rg, ig, sp, a, sq = _lru_gates(xc, wg_ref, brg_ref[...], big_ref[...], lam_ref[...])
        for k, t in enumerate((xc, rg, ig, sq)):
            gates_ref[:, LRU_W * k:LRU_W * (k + 1)] = t.astype(gates_ref.dtype)
        a_ref[...] = a
        b_scr[...] = sq * (ig * xc)
        row8 = lax.broadcasted_iota(jnp.int32, (8, LRU_W), 0)

        def scan_step(g, carry):
            r0 = pl.multiple_of(g * 8, 8)
            av = a_ref[pl.ds(r0, 8), :]
            bv = b_scr[pl.ds(r0, 8), :]
            for d in (1, 2, 4):
                a_sh = jnp.where(row8 >= d, pltpu.roll(av, d, 0), 1.0)
                b_sh = jnp.where(row8 >= d, pltpu.roll(bv, d, 0), 0.0)
                bv = bv + av * b_sh
                av = av * a_sh
            hv = bv + av * carry
            ya_ref[pl.ds(r0, 8), :] = hv
            return hv[7:8, :]

        hc_ref[0:1, :] = lax.fori_loop(0, tm // 8, scan_step, hc_ref[0:1, :], unroll=True)

        gm128 = _group_matrix(KV_W)
        cv, s1v, s2v = c_ref[...], s1_ref[...], s2_ref[...]

        def head_norm_rope(t, g):
            n = t * lax.rsqrt(_seg_mean(t * t, gm128) + EPS)
            return _rope(n * g, cv, s1v, s2v)

        qs_ = (head_norm_rope(proj_ref[:, C_SQ:C_SQ + 128], gn_ref[:, G_Q:G_K]).astype(_MXU),
               head_norm_rope(proj_ref[:, C_SQ + 128:C_SQ + 256], gn_ref[:, G_Q:G_K]).astype(_MXU))
        kr = head_norm_rope(proj_ref[:, C_SK:C_SK + KV_W], gn_ref[:, G_K:G_XQ])
        sv = proj_ref[:, C_SV:C_SV + KV_W]
        ka = _place_kv(jnp.concatenate([kp_ref[...], kr], axis=0), 0.125)
        va = _place_kv(jnp.concatenate([vp_ref[...], sv], axis=0), 1.0)
        kp_ref[...] = kr[tm - BLOCK:tm, :]
        vp_ref[...] = sv[tm - BLOCK:tm, :]
        lane128 = lax.broadcasted_iota(jnp.int32, (1, 128), 1)
        for b in range(nb):
            mask = _swa_mask((i == 0) & (b == 0)) if b == 0 else _swa_mask(False)
            band = slice(BLOCK * b, BLOCK * b + 2 * BLOCK)
            blk = slice(BLOCK * b, BLOCK * (b + 1))
            psink = jnp.zeros((BLOCK, 128), F32)
            for j in range(4):
                p, pk = _swa_probs(qs_[j // 2][blk], ka[j][band], mask, sink_ref[0, j])
                pswa_ref[blk, 2 * BLOCK * j:2 * BLOCK * (j + 1)] = p.astype(pswa_ref.dtype)
                psink = jnp.where(lane128 == j, pk, psink)
            psink_ref[blk, :] = psink
            for h in range(2):
                yb_ref[blk, KV_W * h:KV_W * (h + 1)] = _mm(
                    pswa_ref[blk, 4 * BLOCK * h:4 * BLOCK * (h + 1)],
                    jnp.concatenate([va[2 * h][band], va[2 * h + 1][band]], axis=0))

        gm256 = _group_matrix(XATT_W)
        xq = proj_ref[:, C_XQ:C_XQ + XATT_W]
        qx = xq * lax.rsqrt(_seg_mean(xq * xq, gm256) + EPS) * gn_ref[:, G_XQ:G_XK]
        pm = _mem_probs(_mm_nt(qx, km_ref[...]))
        for j in range(4):
            pmem_ref[:, MEM_LEN * j:MEM_LEN * (j + 1)] = pm[j].astype(pmem_ref.dtype)
        yc = _mm(pmem_ref[...], vm_ref[...])
        yc_ref[...] = yc

        def gated(y, g, gate):
            return y * lax.rsqrt(_row_mean(y * y) + EPS) * g * (gate * _sigmoid(gate))

        ogv = og_ref[...]
        za = gated(ya_ref[...], ogv[:, :512], proj_ref[:, C_LRUG:C_LRUG + LRU_W])
        zb = gated(yb_ref[...], ogv[:, 512:768], proj_ref[:, C_SWAG:C_SWAG + SWA_W])
        zc = gated(yc, ogv[:, 768:], proj_ref[:, C_XG:C_XG + XATT_W])
        ycat_ref[:, 0:512] = za.astype(ycat_ref.dtype)
        ycat_ref[:, 512:768] = zb.astype(ycat_ref.dtype)
        ycat_ref[:, 768:1024] = zc.astype(ycat_ref.dtype)
        out = xv + _mm(ycat_ref[...], wout_ref[...])
        err = out - t_ref[...]
        dout_ref[...] = (err * (1.0 / D_MODEL)).astype(dout_ref.dtype)
        lacc_ref[...] = lacc_ref[...] + (0.5 / D_MODEL) * jnp.sum(err * err)

        @pl.when(i == nt - 1)
        def _():
            loss_ref[...] = lacc_ref[...]

    def rows(ncol):
        return pl.BlockSpec((tm, ncol), lambda i: (i, 0))

    in_specs = [rows(D_MODEL), rows(D_MODEL), rows(128), rows(128), rows(128),
                _const_spec((1, D_MODEL)), _const_spec((D_IN, D_MODEL), True), _const_spec((CONV_K, LRU_W)),
                _const_spec((1, LRU_W)), _const_spec((2, 256, 512), True), _const_spec((1, LRU_W)),
                _const_spec((1, LRU_W)), _const_spec((1, LRU_W)), _const_spec((1, GAINS_W)), pl.BlockSpec(memory_space=pltpu.SMEM),
                _const_spec((4 * MEM_LEN, XATT_W), True), _const_spec((4 * MEM_LEN, XATT_W), True),
                _const_spec((1, D_MODEL)), _const_spec((D_MODEL, D_MODEL), True)]
    out_shape = (jax.ShapeDtypeStruct((seq, D_IN), F32), jax.ShapeDtypeStruct((seq, LRU_W), F32),
                 jax.ShapeDtypeStruct((seq, SWA_W), F32), jax.ShapeDtypeStruct((seq, XATT_W), F32),
                 jax.ShapeDtypeStruct((seq, D_MODEL), _MXU), jax.ShapeDtypeStruct((seq, D_MODEL), _MXU),
                 jax.ShapeDtypeStruct((seq, D_MODEL), _MXU), jax.ShapeDtypeStruct((seq, 4 * 2 * BLOCK), _MXU),
                 jax.ShapeDtypeStruct((seq, 4 * MEM_LEN), _MXU), jax.ShapeDtypeStruct((seq, 128), F32),
                 jax.ShapeDtypeStruct((seq, 4 * LRU_W), _MXU), jax.ShapeDtypeStruct((seq, LRU_W), F32),
                 jax.ShapeDtypeStruct((8, 128), F32))
    out_specs = (rows(D_IN), rows(LRU_W), rows(SWA_W), rows(XATT_W), rows(D_MODEL), rows(D_MODEL), rows(D_MODEL),
                 rows(4 * 2 * BLOCK), rows(4 * MEM_LEN), rows(128), rows(4 * LRU_W), rows(LRU_W),
                 _const_spec((8, 128)))
    scratch = [pltpu.VMEM((tm + 8, LRU_W), F32), pltpu.VMEM((tm, LRU_W), F32),
               pltpu.VMEM((8, LRU_W), F32), pltpu.VMEM((BLOCK, KV_W), F32), pltpu.VMEM((BLOCK, KV_W), F32),
               pltpu.VMEM((8, 128), F32)]
    return pl.pallas_call(
        body, name="layer_fwd", grid=(nt,), out_shape=out_shape, in_specs=in_specs, out_specs=out_specs,
        scratch_shapes=scratch,
        compiler_params=pltpu.CompilerParams(dimension_semantics=("arbitrary",), vmem_limit_bytes=VMEM_LIMIT),
    )(x, tgt, rc, rs1, rs2, ng, win_t, cw, cb, wg, brg, big, lam, gains, sinks, km, vm, og, wout)


def weight_grads(ycat, dout, dproj, xn, mem_operands, vectors, early_at, late_at):
    seq, ncol = xn.shape
    blk = 256
    n_out, n_in = ycat.shape[1] // blk, dproj.shape[1] // blk
    assert n_out == N_CHIPS and late_at[0] >= n_out
    g_out = jax.ShapeDtypeStruct((N_CHIPS, 2, blk // 2, dout.shape[1]), F32)
    g_kv = jax.ShapeDtypeStruct((N_CHIPS, 2, D_MODEL // 8, 2 * XATT_W), F32)
    g_small = jax.ShapeDtypeStruct((2, SMALL_ROWS // 2, 128), F32)
    shape_e, scratch_e = _hosted_reduce_shapes([g_kv], g_small)
    shape_l, scratch_l = _hosted_reduce_shapes([g_out], None)
    n_mem, names = len(mem_operands), tuple(vectors)

    def body(l1_ref, r1_ref, l2_ref, r2_hbm, *refs):
        mem_refs, vec_refs = refs[:n_mem], refs[n_mem:n_mem + len(names)]
        o_ref, sum_out, sum_kv, sum_sm, gout_scr, gkv_scr, pack_scr, r2_ref, r2_sem, *scratch = refs[n_mem + len(names):]
        j = pl.program_id(0)
        r2_copy = pltpu.make_async_copy(r2_hbm, r2_ref, r2_sem.at[0])

        @pl.when(j == 0)
        def _():
            r2_copy.start()
            _mem_bwd_and_pack(*mem_refs, dict(zip(names, vec_refs)), gkv_scr, pack_scr)

        def reduce_stages(closing):
            _hosted_reduce(j, n_out + n_in, closing, early_at, (gkv_scr, pack_scr), (sum_kv, sum_sm),
                           scratch[:len(scratch_e)], True)
            _hosted_reduce(j, n_out + n_in, closing, late_at, (gout_scr,), (sum_out,), scratch[len(scratch_e):], False)

        reduce_stages(False)

        @pl.when(j < n_out)
        def _():
            gout_scr[j] = _mm_tn(l1_ref[...], r1_ref[...]).reshape(g_out.shape[1:])

        pl.when(j == n_out)(r2_copy.wait)

        @pl.when(j >= n_out)
        def _():
            o_ref[...] = _mm_tn(l2_ref[...], r2_ref[...])

        reduce_stages(True)

    vm = pl.BlockSpec(memory_space=pltpu.VMEM)
    hbm = pl.BlockSpec(memory_space=pl.ANY)
    return pl.pallas_call(
        body, name="weight_grads", grid=(n_out + n_in,),
        out_shape=(jax.ShapeDtypeStruct((dproj.shape[1], ncol), F32), *shape_l, *shape_e),
        in_specs=[pl.BlockSpec((seq, blk), lambda j: (0, jnp.minimum(j, n_out - 1))), _const_spec(dout.shape, True),
                  pl.BlockSpec((seq, blk), lambda j: (0, jnp.maximum(j - n_out, 0))), hbm]
        + [vm] * (n_mem + len(names)),
        out_specs=(pl.BlockSpec((blk, ncol), lambda j: (jnp.maximum(j - n_out, 0), 0)), hbm, hbm, hbm),
        scratch_shapes=[pltpu.VMEM(s.shape, F32) for s in (g_out, g_kv, g_small)]
        + [pltpu.VMEM(xn.shape, xn.dtype), pltpu.SemaphoreType.DMA((1,))] + scratch_e + scratch_l,
        compiler_params=pltpu.CompilerParams(dimension_semantics=("arbitrary",), vmem_limit_bytes=VMEM_LIMIT),
    )(ycat, dout, dproj, xn, *mem_operands, *vectors.values())


def layer_bwd(x, dout, proj, ya, yb, yc, pswa, pmem, psink, gates, a_all, rc, rs1, rs2, ng, win_t, cw, wg, lam, gains,
              km, vm, og, wout):
    seq = x.shape[0]
    tm = min(ROW_TILE, seq)
    nt = seq // tm
    nb = tm // BLOCK

    def body(x_ref, dout_ref, proj_ref, ya_ref, yb_ref, yc_ref, pswa_ref, pmem_ref, psink_ref, gates_ref, a_ref,
             c_ref, s1_ref, s2_ref,
             yah_ref, kvh_ref, ch_ref, s1h_ref, s2h_ref,
             ng_ref, win_ref, cw_ref, wg_ref, lam_ref, gn_ref, km_ref, vm_ref, og_ref, wout_ref,
             gx_ref, dproj_ref, gwg_ref, dkm_ref, dvm_ref, gng_ref, gog_ref, gcb_ref, gbrg_ref, gbig_ref, glam_ref,
             gcw_ref, gqn_ref, gkn_ref, gxqn_ref, gsink_ref,
             hext_ref, aext_ref, an_scr, dh_scr, g_scr, dxc_ext, gcar_ref, dkcar_ref, dvcar_ref):
        i = pl.program_id(0)
        tile = nt - 1 - i
        first_tile = tile == 0

        @pl.when(i == 0)
        def _():
            for r in (gwg_ref, dkm_ref, dvm_ref, gng_ref, gog_ref, gcb_ref, gbrg_ref, gbig_ref, glam_ref, gcw_ref,
                      gqn_ref, gkn_ref, gxqn_ref, gsink_ref, gcar_ref, dkcar_ref, dvcar_ref):
                r[...] = jnp.zeros_like(r)
            dxc_ext[tm:tm + 8, :] = jnp.zeros((8, LRU_W), F32)
            aext_ref[tm:tm + 8, :] = jnp.zeros((8, LRU_W), F32)

        xv = x_ref[...]
        dov = dout_ref[...]
        dz = _mm_nt(dov, wout_ref[...])
        ogv = og_ref[...]

        def group_bwd(y, gate, g, dzg):
            r = lax.rsqrt(_row_mean(y * y) + EPS)
            n = y * r
            sg = _sigmoid(gate)
            dgate = dzg * (n * g) * (sg * (1.0 + gate * (1.0 - sg)))
            dng = dzg * (gate * sg)
            dn = dng * g
            return r * (dn - n * _row_mean(dn * n)), dgate, _col_sum(dng * n)

        dya, dga, goa = group_bwd(ya_ref[...], proj_ref[:, C_LRUG:C_LRUG + LRU_W], ogv[:, :512], dz[:, :512])
        dyb, dgb, gob = group_bwd(yb_ref[...], proj_ref[:, C_SWAG:C_SWAG + SWA_W], ogv[:, 512:768], dz[:, 512:768])
        dyc, dgc, goc = group_bwd(yc_ref[...], proj_ref[:, C_XG:C_XG + XATT_W], ogv[:, 768:], dz[:, 768:])
        gog_ref[...] += jnp.concatenate([goa, gob, goc], axis=1)
        dproj_ref[:, C_LRUG:C_LRUG + LRU_W] = dga.astype(dproj_ref.dtype)
        dproj_ref[:, C_SWAG:C_SWAG + SWA_W] = dgb.astype(dproj_ref.dtype)
        dproj_ref[:, C_XG:C_XG + XATT_W] = dgc.astype(dproj_ref.dtype)

        gm256 = _group_matrix(XATT_W)
        xq = proj_ref[:, C_XQ:C_XQ + XATT_W]
        rq = lax.rsqrt(_seg_mean(xq * xq, gm256) + EPS)
        qn = xq * rq
        qx = qn * gn_ref[:, G_XQ:G_XK]
        qxb = qx.astype(_MXU)
        dycb = dyc.astype(_MXU)
        dp_all = _mm_nt(dycb, vm_ref[...])
        dsm = []
        for j in range(4):
            pj = pmem_ref[:, MEM_LEN * j:MEM_LEN * (j + 1)].astype(F32)
            dp = dp_all[:, MEM_LEN * j:MEM_LEN * (j + 1)]
            dsm.append((pj * (dp - jnp.sum(pj * dp, axis=-1, keepdims=True))).astype(_MXU))
        ds_all = jnp.concatenate(dsm, axis=1)
        dvm_ref[...] += _mm_tn(dycb, pmem_ref[...])
        dkm_ref[...] += _mm_tn(qxb, ds_all)
        dqx = _mm(ds_all, km_ref[...])
        gxqn_ref[...] += _col_sum(dqx * qn)
        dqn = dqx * gn_ref[:, G_XQ:G_XK]
        dproj_ref[:, C_XQ:C_XQ + XATT_W] = (rq * (dqn - qn * _seg_mean(dqn * qn, gm256))).astype(dproj_ref.dtype)

        gm128 = _group_matrix(KV_W)
        cv, s1v, s2v = c_ref[...], s1_ref[...], s2_ref[...]

        def head_norm(t):
            r = lax.rsqrt(_seg_mean(t * t, gm128) + EPS)
            return t * r, r

        qn_, qr_ = zip(head_norm(proj_ref[:, C_SQ:C_SQ + 128]), head_norm(proj_ref[:, C_SQ + 128:C_SQ + 256]))
        qrope = [_rope(qn_[h] * gn_ref[:, G_Q:G_K], cv, s1v, s2v).astype(_MXU) for h in range(2)]
        kn, krr = head_norm(proj_ref[:, C_SK:C_SK + KV_W])
        kr = _rope(kn * gn_ref[:, G_K:G_XQ], cv, s1v, s2v)
        khn, _ = head_norm(kvh_ref[:, 0:KV_W])
        khr = _rope(khn * gn_ref[:, G_K:G_XQ], ch_ref[...], s1h_ref[...], s2h_ref[...])
        ka = _place_kv(jnp.concatenate([khr, kr], axis=0), 0.125)
        va = _place_kv(jnp.concatenate([kvh_ref[:, KV_W:2 * KV_W], proj_ref[:, C_SV:C_SV + KV_W]], axis=0), 1.0)
        lane128 = lax.broadcasted_iota(jnp.int32, (1, 128), 1)
        gsink = jnp.zeros((1, 128), F32)
        dk_band, dv_band, dq_blk = [], [], []
        for b in range(nb):
            band = slice(BLOCK * b, BLOCK * b + 2 * BLOCK)
            blk = slice(BLOCK * b, BLOCK * (b + 1))
            dka, dva, dsb = [], [], []
            deltas = jnp.zeros((BLOCK, 128), F32)
            for j in range(4):
                qh = qrope[j // 2][blk]
                doh = dyb[blk, KV_W * (j // 2):KV_W * (j // 2 + 1)].astype(_MXU)
                pb = pswa_ref[blk, 2 * BLOCK * j:2 * BLOCK * (j + 1)]
                p = pb.astype(F32)
                dp = _mm_nt(doh, va[j][band])
                delta = jnp.sum(p * dp, axis=-1, keepdims=True)
                ds = (p * (dp - delta)).astype(_MXU)
                deltas = jnp.where(lane128 == j, delta, deltas)
                dva.append(_mm_tn(pb, doh))
                dka.append(_mm_tn(ds, qh))
                dsb.append(ds)
            gsink = gsink - _col_sum(psink_ref[blk, :] * deltas)
            dk_band.append(_unplace_kv(dka) * 0.125)
            dv_band.append(_unplace_kv(dva))
            dq_blk.append([_mm(jnp.concatenate(dsb[2 * h:2 * h + 2], axis=1),
                               jnp.concatenate([ka[2 * h][band], ka[2 * h + 1][band]], axis=0)) for h in range(2)])
        gsink_ref[...] += gsink
        dk_rows = [dk_band[b][BLOCK:] + (dk_band[b + 1][:BLOCK] if b + 1 < nb else dkcar_ref[...]) for b in range(nb)]
        dv_rows = [dv_band[b][BLOCK:] + (dv_band[b + 1][:BLOCK] if b + 1 < nb else dvcar_ref[...]) for b in range(nb)]
        dkcar_ref[...] = dk_band[0][:BLOCK]
        dvcar_ref[...] = dv_band[0][:BLOCK]
        dkg = _rope_bwd(jnp.concatenate(dk_rows, axis=0), cv, s1v, s2v)
        gkn = _col_sum(dkg * kn)
        dkn = dkg * gn_ref[:, G_K:G_XQ]
        dproj_ref[:, C_SK:C_SK + KV_W] = (krr * (dkn - kn * _seg_mean(dkn * kn, gm128))).astype(dproj_ref.dtype)
        dproj_ref[:, C_SV:C_SV + KV_W] = jnp.concatenate(dv_rows, axis=0).astype(dproj_ref.dtype)
        gqn = jnp.zeros((1, 128), F32)
        for h in range(2):
            dqg = _rope_bwd(jnp.concatenate([dq_blk[b][h] for b in range(nb)], axis=0), cv, s1v, s2v)
            gqn = gqn + _col_sum(dqg * qn_[h])
            dqn_ = dqg * gn_ref[:, G_Q:G_K]
            dproj_ref[:, C_SQ + 128 * h:C_SQ + 128 * (h + 1)] = (
                qr_[h] * (dqn_ - qn_[h] * _seg_mean(dqn_ * qn_[h], gm128))).astype(dproj_ref.dtype)
        gqn_ref[...] += gqn
        gkn_ref[...] += gkn

        u = proj_ref[:, C_LRUX:C_LRUX + LRU_W]
        xc, rg, ig, sq = (gates_ref[:, LRU_W * k:LRU_W * (k + 1)].astype(F32) for k in range(4))
        a = a_ref[...]
        sp = _softplus(-lam_ref[...])
        hext_ref[0:8, :] = jnp.where(first_tile, 0.0, yah_ref[...])
        hext_ref[8:8 + tm, :] = ya_ref[...]
        hprev = hext_ref[pl.ds(7, tm), :]
        aext_ref[0:tm, :] = a
        an_scr[...] = aext_ref[pl.ds(1, tm), :]
        dh_scr[...] = dya
        dh_scr[tm - 1:tm, :] = dh_scr[tm - 1:tm, :] + gcar_ref[0:1, :]
        row8 = lax.broadcasted_iota(jnp.int32, (8, LRU_W), 0)

        def scan_step(gi, carry):
            r0 = pl.multiple_of((tm // 8 - 1 - gi) * 8, 8)
            av = an_scr[pl.ds(r0, 8), :]
            bv = dh_scr[pl.ds(r0, 8), :]
            for d in (1, 2, 4):
                a_sh = jnp.where(row8 < 8 - d, pltpu.roll(av, 8 - d, 0), 1.0)
                b_sh = jnp.where(row8 < 8 - d, pltpu.roll(bv, 8 - d, 0), 0.0)
                bv = bv + av * b_sh
                av = av * a_sh
            gv = bv + av * carry
            g_scr[pl.ds(r0, 8), :] = gv
            return gv[0:1, :]

        g0 = lax.fori_loop(0, tm // 8, scan_step, jnp.zeros((1, LRU_W), F32), unroll=True)
        gcar_ref[0:1, :] = a[0:1, :] * g0
        gv = g_scr[...]
        da = gv * hprev
        dig = gv * sq * xc
        dxc = gv * sq * ig
        dla = da * a - gv * (ig * xc) * ((a * a) / sq)
        drg = dla * ((-LRU_C) * sp)
        glam_ref[...] += _col_sum(dla * rg)
        dpr = drg * rg * (1.0 - rg)
        dpi = dig * ig * (1.0 - ig)
        gbrg_ref[...] += _col_sum(dpr)
        gbig_ref[...] += _col_sum(dpi)
        dpre0 = jnp.concatenate([dpr[:, :256], dpi[:, :256]], axis=1).astype(_MXU)
        dpre1 = jnp.concatenate([dpr[:, 256:], dpi[:, 256:]], axis=1).astype(_MXU)
        gwg_ref[0] += _mm_tn(xc[:, :256], dpre0)
        gwg_ref[1] += _mm_tn(xc[:, 256:], dpre1)
        dxc = dxc + jnp.concatenate([_mm_nt(dpre0, wg_ref[0]), _mm_nt(dpre1, wg_ref[1])], axis=1)
        gcb_ref[...] += _col_sum(dxc)
        dxc_ext[0:tm, :] = dxc
        du = jnp.zeros((tm, LRU_W), F32)
        for k in range(CONV_K):
            later = dxc_ext[pl.ds(3 - k, tm), :]
            gcw_ref[k:k + 1, :] += _col_sum(later * u)
            du = du + cw_ref[k:k + 1, :] * later
        dxc_ext[tm:tm + 8, :] = dxc[0:8, :]
        dproj_ref[:, C_LRUX:C_LRUX + LRU_W] = du.astype(dproj_ref.dtype)

        dxn = _mm(dproj_ref[...], win_ref[...])
        rx = lax.rsqrt(_row_mean(xv * xv) + EPS)
        xh = xv * rx
        gng_ref[...] += _col_sum(dxn * xh)
        dxh = dxn * ng_ref[...]
        gx_ref[...] = dov.astype(F32) + rx * (dxh - xh * _row_mean(dxh * xh))

        @pl.when(i == nt - 1)
        def _():
            glam_ref[...] = glam_ref[...] * (LRU_C * _sigmoid(-lam_ref[...]))
            for r in (gqn_ref, gkn_ref, gxqn_ref):
                r[...] = _fold_heads(r[...])

    def rows(ncol, arr_cols_block=0):
        return pl.BlockSpec((tm, ncol), lambda i: (nt - 1 - i, arr_cols_block))

    def halo(nrow, ncol, colblk=0):
        per = tm // nrow
        return pl.BlockSpec((nrow, ncol), lambda i: (jnp.maximum((nt - 1 - i) * per - 1, 0), colblk))

    in_specs = [rows(D_MODEL), rows(D_MODEL), rows(D_IN), rows(LRU_W), rows(SWA_W), rows(XATT_W),
                rows(4 * 2 * BLOCK), rows(4 * MEM_LEN), rows(128), rows(4 * LRU_W), rows(LRU_W),
                rows(128), rows(128), rows(128),
                halo(8, LRU_W), halo(BLOCK, 2 * KV_W, C_SK // (2 * KV_W)),
                halo(BLOCK, 128), halo(BLOCK, 128), halo(BLOCK, 128),
                _const_spec((1, D_MODEL)), _const_spec((D_IN, D_MODEL), True), _const_spec((CONV_K, LRU_W)),
                _const_spec((2, 256, 512), True), _const_spec((1, LRU_W)), _const_spec((1, GAINS_W)),
                _const_spec((4 * MEM_LEN, XATT_W), True), _const_spec((4 * MEM_LEN, XATT_W), True),
                _const_spec((1, D_MODEL)), _const_spec((D_MODEL, D_MODEL), True)]
    small = [(2, 256, 512), (XATT_W, 4 * MEM_LEN), (XATT_W, 4 * MEM_LEN), (1, D_MODEL), (1, D_MODEL), (1, LRU_W), (1, LRU_W),
             (1, LRU_W), (1, LRU_W), (CONV_K, LRU_W), (1, 128), (1, 128), (1, XATT_W), (1, 128)]
    out_shape = (jax.ShapeDtypeStruct((seq, D_MODEL), F32), jax.ShapeDtypeStruct((seq, D_IN), _MXU)) + tuple(
        jax.ShapeDtypeStruct(s, F32) for s in small)
    out_specs = (rows(D_MODEL), rows(D_IN)) + tuple(_const_spec(s) for s in small)
    scratch = [pltpu.VMEM((tm + 8, LRU_W), F32), pltpu.VMEM((tm + 8, LRU_W), F32),
               pltpu.VMEM((tm, LRU_W), F32), pltpu.VMEM((tm, LRU_W), F32), pltpu.VMEM((tm, LRU_W), F32),
               pltpu.VMEM((tm + 8, LRU_W), F32),
               pltpu.VMEM((8, LRU_W), F32), pltpu.VMEM((BLOCK, KV_W), F32), pltpu.VMEM((BLOCK, KV_W), F32)]
    return pl.pallas_call(
        body, name="layer_bwd", grid=(nt,), out_shape=out_shape, in_specs=in_specs, out_specs=out_specs,
        scratch_shapes=scratch,
        compiler_params=pltpu.CompilerParams(dimension_semantics=("arbitrary",), vmem_limit_bytes=VMEM_LIMIT),
    )(x, dout, proj, ya, yb, yc, pswa, pmem, psink, gates, a_all, rc, rs1, rs2, ya, proj, rc, rs1, rs2,
      ng, win_t, cw, wg, lam, gains, km, vm, og, wout)


def _reduce_protocol(big, sm, outs, osm, r1, r1s, wire, r2, r2s, wire2, ps, own, send, recv, lsem):
    nbig = len(big)
    x, y, c = lax.axis_index("x"), lax.axis_index("y"), lax.axis_index("c")
    sibling = (x, y, 1 - c)
    near, far, diag = _partners(x, y, c)
    me, near_id, far_id, diag_id = _chip_of(x, y), _chip_of(*near), _chip_of(*far), _chip_of(*diag)

    def copy(k, src, dst, to):
        return pltpu.make_async_remote_copy(src_ref=src, dst_ref=dst, send_sem=send.at[k], recv_sem=recv.at[k],
                                            device_id=to, device_id_type=MESH)

    def sent(stage, a):
        if a == nbig:
            src, dst, to = ((sm.at[1 - c], r1s, sibling), (r1s, r2s.at[0], (*near, c)), (ps, r2s.at[1], (*far, c)),
                            (osm.at[c], osm.at[c], sibling))[stage]
            return [copy(5 * nbig + stage, src, dst, to)]
        if stage == 0:
            return [copy(5 * a, big[a].at[:, 1 - c], r1[a], sibling)]
        if stage == 1:
            return [copy(5 * a + 1, wire[a].at[near_id], r2[a].at[0], (*near, c)),
                    copy(5 * a + 2, wire[a].at[diag_id], r2[a].at[1], (*near, c))]
        if stage == 2:
            return [copy(5 * a + 3, wire2[a], r2[a].at[2], (*far, c))]
        return [copy(5 * a + 4, outs[a].at[c], outs[a].at[c], sibling)]

    arrays = range(nbig + (sm is not None))

    def start(stage, a):
        for cp in sent(stage, a):
            cp.start()

    def arrived(k, ref):
        copy(k, ref, ref, sibling).wait_recv()

    def loads():
        return [pltpu.make_async_copy(big[a].at[:, c], own[a], lsem.at[a]) for a in range(nbig)]

    def stage0():
        for a in arrays:
            start(0, a)
        for cp in loads():
            cp.start()

    def stage1():
        for a in range(nbig):
            loads()[a].wait()
            arrived(5 * a, r1[a])
            for k in range(N_CHIPS):
                r1[a][k] = own[a][k] + r1[a][k]
                wire[a][k] = r1[a][k].astype(wire[a].dtype)
            start(1, a)
        if sm is not None:
            arrived(5 * nbig, r1s)
            r1s[...] = sm[c] + r1s[...]
            start(1, nbig)

    def stage2():
        for a in range(nbig):
            arrived(5 * a + 1, r2[a].at[0])
            arrived(5 * a + 2, r2[a].at[1])
            r1[a][me] = r1[a][me] + r2[a][0].astype(F32)
            wire2[a][...] = (r1[a][far_id] + r2[a][1].astype(F32)).astype(wire2[a].dtype)
            start(2, a)
        if sm is not None:
            arrived(5 * nbig + 1, r2s.at[0])
            ps[...] = r1s[...] + r2s[0]
            start(2, nbig)

    def stage3():
        for a in range(nbig):
            arrived(5 * a + 3, r2[a].at[2])
            outs[a][c] = r1[a][me] + r2[a][2].astype(F32)
            start(3, a)
        if sm is not None:
            arrived(5 * nbig + 2, r2s.at[1])
            osm[c] = ps[...] + r2s[1]
            start(3, nbig)

    def stage4():
        for a in range(nbig):
            arrived(5 * a + 4, outs[a].at[1 - c])
        if sm is not None:
            arrived(5 * nbig + 3, osm.at[1 - c])
        for stage in range(4):
            for a in arrays:
                for cp in sent(stage, a):
                    cp.wait_send()

    return [stage0, stage1, stage2, stage3, stage4]


def _reduce_buffers(bigs, g_small):
    half = [b.shape[2:] for b in bigs]
    sm_half = None if g_small is None else g_small.shape[1:]
    out_shape = [jax.ShapeDtypeStruct((2,) + h, F32) for h in half]
    small = lambda lead: [] if g_small is None else [pltpu.VMEM(lead + sm_half, F32)]
    if g_small is not None:
        out_shape.append(jax.ShapeDtypeStruct(g_small.shape, F32))
    n_sem = 5 * len(bigs) + 4
    scratch = ([pltpu.VMEM((N_CHIPS,) + h, F32) for h in half] + small(())
               + [pltpu.VMEM((N_CHIPS,) + h, _WIRE) for h in half]
               + [pltpu.VMEM((3,) + h, _WIRE) for h in half] + small((2,))
               + [pltpu.VMEM(h, _WIRE) for h in half] + small(())
               + [pltpu.VMEM((N_CHIPS,) + h, F32) for h in half]
               + [pltpu.SemaphoreType.DMA((n_sem,)), pltpu.SemaphoreType.DMA((n_sem,)),
                  pltpu.SemaphoreType.DMA((len(bigs),))])
    return out_shape, scratch


def _split_reduce_refs(refs, nbig, has_small):
    it = iter(refs)
    take = lambda n: [next(it) for _ in range(n)]
    one = lambda: next(it) if has_small else None
    big, sm = take(nbig), one()
    outs, osm = take(nbig), one()
    r1, r1s, wire, r2, r2s, wire2, ps, own = take(nbig), one(), take(nbig), take(nbig), one(), take(nbig), one(), take(nbig)
    send, recv, lsem = take(3)
    return big, sm, outs, osm, r1, r1s, wire, r2, r2s, wire2, ps, own, send, recv, lsem


def _hosted_reduce_shapes(bigs, g_small):
    red_shape, scratch = _reduce_buffers(bigs, g_small)
    nres = len(red_shape)
    return red_shape, [pltpu.VMEM(r.shape, r.dtype) for r in red_shape] + scratch + [pltpu.SemaphoreType.DMA((nres,))]


def _hosted_reduce(step, n_steps, closing, stage_at, operands, results, scratch, has_small):
    nres = len(results)
    sums, rest, fsem = scratch[:nres], scratch[nres:-1], scratch[-1]
    refs = tuple(operands) + tuple(sums) + tuple(rest)

    def to_results():
        out = [pltpu.make_async_copy(sums[k], results[k], fsem.at[k]) for k in range(nres)]
        for cp in out:
            cp.start()
        for cp in out:
            cp.wait()

    stages = _reduce_protocol(*_split_reduce_refs(refs, nres - has_small, has_small))

    def last_stage():
        stages[-1]()
        to_results()

    for at, stage in zip(stage_at, stages[:-1] + [last_stage]):
        if closing == (at == n_steps):
            pl.when(step == min(at, n_steps - 1))(stage)


def reduce_grads(big, name, parts):
    chips, halves, rows_, cols = big.shape
    sub = jax.ShapeDtypeStruct((chips, halves, rows_ // parts, cols), big.dtype)

    def body(b_ref, o_ref, *scratch):
        refs = [b_ref.at[:, :, s] for s in range(parts)] + [o_ref.at[:, s] for s in range(parts)] + list(scratch)
        for stage in _reduce_protocol(*_split_reduce_refs(refs, parts, False)):
            stage()

    _, scratch = _reduce_buffers([sub] * parts, None)
    return pl.pallas_call(
        body, name=name, out_shape=jax.ShapeDtypeStruct((halves, parts, rows_ // parts, cols), F32),
        in_specs=[pl.BlockSpec(memory_space=pl.ANY)], out_specs=pl.BlockSpec(memory_space=pltpu.VMEM),
        scratch_shapes=scratch, compiler_params=pltpu.CompilerParams(vmem_limit_bytes=VMEM_LIMIT),
    )(big.reshape(chips, halves, parts, rows_ // parts, cols))


def adamw(items, g_pack, ws, ms, vs):
    blocks = []
    for k, (w, _, _, _) in enumerate(items):
        rows_, cols = w.shape
        tr = max(t for t in range(8, rows_ + 1, 8) if rows_ % t == 0 and t * cols * 4 <= ADAM_BLOCK_BYTES)
        blocks += [(k, r, tr) for r in range(0, rows_, tr)]
    nin, n = 4 * len(items), len(ws)

    def body(*refs):
        mats_in, small_in = refs[:nin], refs[nin:nin + 1 + 3 * n]
        n_out = nin + 4 * n + 1
        outs, scratch = refs[nin + 1 + 3 * n:nin + 1 + 3 * n + n_out], refs[nin + 1 + 3 * n + n_out:]
        buf, (lsem, ssem) = scratch[:nin], scratch[nin:]
        loads = [[pltpu.make_async_copy(mats_in[4 * k + q].at[pl.ds(r, tr)], buf[4 * k + q].at[pl.ds(r, tr)],
                                        lsem.at[4 * c + q]) for q in range(4)] for c, (k, r, tr) in enumerate(blocks)]
        for cps in loads:
            for cp in cps:
                cp.start()
        _adamw_small(small_in[0], *(small_in[1 + k * n:1 + (k + 1) * n] for k in range(3)),
                     *(outs[nin + k * n:nin + (k + 1) * n] for k in range(4)), outs[-1])
        stores = []
        for c, (k, r, tr) in enumerate(blocks):
            for cp in loads[c]:
                cp.wait()
            w_ref, g_ref, m_ref, v_ref = (buf[4 * k + q].at[pl.ds(r, tr)] for q in range(4))
            w_ref[...], m_ref[...], v_ref[...] = _adam_update(w_ref[...], g_ref[...], m_ref[...], v_ref[...])
            for q, src in enumerate((g_ref, w_ref, m_ref, v_ref)):
                stores.append(pltpu.make_async_copy(src, outs[4 * k + q].at[pl.ds(r, tr)], ssem.at[4 * c + q]))
                stores[-1].start()
        for cp in stores:
            cp.wait()

    shapes = [jax.ShapeDtypeStruct(w.shape, F32) for w, _, _, _ in items for _ in range(4)]
    vm = pl.BlockSpec(memory_space=pltpu.VMEM)
    hbm = pl.BlockSpec(memory_space=pl.ANY)
    like = [jax.ShapeDtypeStruct(w.shape, F32) for w in ws]
    res = pl.pallas_call(
        body, name="adamw", out_shape=(*shapes, *like * 4, jax.ShapeDtypeStruct((1, 1), F32)),
        in_specs=[hbm] * nin + [vm] * (1 + 3 * n), out_specs=(*[hbm] * nin, *[vm] * (4 * n + 1)),
        scratch_shapes=[pltpu.VMEM(s.shape, F32) for s in shapes] + [pltpu.SemaphoreType.DMA((4 * len(blocks),))] * 2,
        compiler_params=pltpu.CompilerParams(vmem_limit_bytes=VMEM_LIMIT),
    )(*[a for item in items for a in item], g_pack, *ws, *ms, *vs)
    return [res[4 * k:4 * k + 4] for k in range(len(items))], res[nin:]


def _adam_update(w, g, m, v):
    nm = ADAM_B1 * m + (1.0 - ADAM_B1) * g
    nv = ADAM_B2 * v + (1.0 - ADAM_B2) * (g * g)
    m_hat = nm / (1.0 - ADAM_B1 ** ADAM_STEP)
    v_hat = nv / (1.0 - ADAM_B2 ** ADAM_STEP)
    return (-ADAM_LR) * (m_hat / (jnp.sqrt(v_hat) + ADAM_EPS) + ADAM_WD * w), nm, nv


def _adamw_small(pk, w_refs, m_refs, v_refs, g_out, d_out, nm_out, nv_out, loss_ref):
    nvec = len(SMALL_VECTORS)
    loss_ref[...] = pk[LOSS_ROW:LOSS_ROW + 1, 0:1]
    chip = 2 * lax.axis_index("x") + lax.axis_index("y")
    for k, (name, row, width) in enumerate(SMALL_VECTORS):
        if name == "conv_w":
            g = jnp.concatenate([pk[pl.ds(row + 4 * t + chip, 1), :] for t in range(CONV_K)], axis=0)[None]
        elif width >= 128:
            g = jnp.concatenate([pk[row + r:row + r + 1, :] for r in range(width // 128)], axis=1)
        else:
            g = pk[row:row + 1, 0:width]
        g_out[k][...] = g
        d_out[k][...], nm_out[k][...], nv_out[k][...] = _adam_update(w_refs[k][...], g, m_refs[k][...], v_refs[k][...])
    for k in range(nvec, nvec + len(SMALL_MATRICES)):
        for b in range(LRU_BLOCKS):
            rows_ = pk[GATES_ROW + HEAD * b:GATES_ROW + HEAD * (b + 1), :]
            g = (pltpu.roll(rows_, HEAD, axis=1) if k > nvec else rows_)[:, 0:HEAD]
            g_out[k][0, b] = g
            d_out[k][0, b], nm_out[k][0, b], nv_out[k][0, b] = _adam_update(
                w_refs[k][0, b], g, m_refs[k][0, b], v_refs[k][0, b])


SMALL_VECTORS = (("norm_g", 512, 1024), ("mem_norm_g", 520, 1024), ("conv_w", 528, 512), ("conv_b", 544, 512),
                 ("b_rg", 548, 512), ("b_ig", 552, 512), ("lru_lambda", 556, 512), ("q_norm_g", 560, 64),
                 ("k_norm_g", 561, 64), ("sinks", 562, 4), ("xq_norm_g", 563, 64), ("xk_norm_g", 564, 64),
                 ("out_norm_g", 565, 1024))
LOSS_ROW = 573
SMALL_MATRICES = ("w_rg", "w_ig")
GATES_ROW = 0
SMALL_ROWS = 640


def _rope_tables(seq):
    pos = np.arange(seq, dtype=np.float32)
    inv_freq = (np.float32(ROPE_THETA) ** (-(np.arange(0, ROPE_DIM, 2, dtype=np.float32) / np.float32(ROPE_DIM)))
                ).astype(np.float32)
    ang = (pos[:, None] * inv_freq[None, :]).astype(np.float32)
    cos, sin = np.cos(ang).astype(np.float32), np.sin(ang).astype(np.float32)
    z = lambda n: np.zeros((seq, n), np.float32)
    c64 = np.concatenate([cos, cos, np.ones((seq, HEAD - ROPE_DIM), np.float32)], axis=1)
    s1_64 = np.concatenate([-sin, z(HEAD - 8)], axis=1)
    s2_64 = np.concatenate([z(8), sin, z(HEAD - ROPE_DIM)], axis=1)
    return tuple(jnp.asarray(np.concatenate([t, t], axis=1)) for t in (c64, s1_64, s2_64))


def kernel(x, mem, norm_g, mem_norm_g, w_in, conv_w, conv_b, w_rg, b_rg, w_ig, b_ig, lru_lambda, q_norm_g, k_norm_g, sinks, w_mem_kv, xq_norm_g, xk_norm_g, out_norm_g, w_out, loss_target, m_norm_g, m_mem_norm_g, m_w_in, m_conv_w, m_conv_b, m_w_rg, m_b_rg, m_w_ig, m_b_ig, m_lru_lambda, m_q_norm_g, m_k_norm_g, m_sinks, m_w_mem_kv, m_xq_norm_g, m_xk_norm_g, m_out_norm_g, m_w_out, v_norm_g, v_mem_norm_g, v_w_in, v_conv_w, v_conv_b, v_w_rg, v_b_rg, v_w_ig, v_b_ig, v_lru_lambda, v_q_norm_g, v_k_norm_g, v_sinks, v_w_mem_kv, v_xq_norm_g, v_xk_norm_g, v_out_norm_g, v_w_out):
    seq = x.shape[1]
    xs, tgt, mems = x[0], loss_target[0], mem[0]

    win_t, wout, wkv, cw, wg, gains, km, vm = gather_weights(
        w_in[0].T, w_out[0], w_mem_kv[0], conv_w, w_rg, w_ig, (q_norm_g, k_norm_g, xq_norm_g, xk_norm_g), mems,
        mem_norm_g)
    rc, rs1, rs2 = _rope_tables(seq)
    proj, ya, yb, yc, ycat, xn, dout, pswa, pmem, psink, gates, a_all, loss8 = layer_fwd(
        xs, tgt, rc, rs1, rs2, norm_g, win_t, cw, conv_b, wg, b_rg, b_ig, lru_lambda, gains, sinks, km, vm,
        out_norm_g, wout)
    (gx, dproj, g_wg, dkm, dvm, g_ng, g_og, g_cb, g_brg, g_big, g_lam, g_cw, g_qn, g_kn, g_xqn, g_sink) = layer_bwd(
        xs, dout, proj, ya, yb, yc, pswa, pmem, psink, gates, a_all, rc, rs1, rs2, norm_g, win_t, cw, wg, lru_lambda,
        gains, km, vm, out_norm_g, wout)
    g_win_t, r_out, r_kv, r_small = weight_grads(
        ycat, dout, dproj, xn, (mems, mem_norm_g, wkv, gains, dkm, dvm, g_wg, loss8), dict(
            norm_g=g_ng, conv_w=g_cw, conv_b=g_cb, b_rg=g_brg, b_ig=g_big, lru_lambda=g_lam, q_norm_g=g_qn,
            k_norm_g=g_kn, sinks=g_sink, xq_norm_g=g_xqn, out_norm_g=g_og), (0, 1, 4, 7, 8), (4, 5, 9, 11, 13))
    r_in = reduce_grads(g_win_t.reshape(N_CHIPS, 2, D_IN // 8, D_MODEL), "reduce_w_in", 6)

    r_small = r_small.reshape(SMALL_ROWS, 128)
    grads = {}
    weights = dict(norm_g=norm_g, mem_norm_g=mem_norm_g, w_in=w_in, conv_w=conv_w, conv_b=conv_b, w_rg=w_rg, b_rg=b_rg,
                   w_ig=w_ig, b_ig=b_ig, lru_lambda=lru_lambda, q_norm_g=q_norm_g, k_norm_g=k_norm_g, sinks=sinks,
                   w_mem_kv=w_mem_kv, xq_norm_g=xq_norm_g, xk_norm_g=xk_norm_g, out_norm_g=out_norm_g, w_out=w_out)
    ms = dict(norm_g=m_norm_g, mem_norm_g=m_mem_norm_g, w_in=m_w_in, conv_w=m_conv_w, conv_b=m_conv_b, w_rg=m_w_rg,
              b_rg=m_b_rg, w_ig=m_w_ig, b_ig=m_b_ig, lru_lambda=m_lru_lambda, q_norm_g=m_q_norm_g, k_norm_g=m_k_norm_g,
              sinks=m_sinks, w_mem_kv=m_w_mem_kv, xq_norm_g=m_xq_norm_g, xk_norm_g=m_xk_norm_g,
              out_norm_g=m_out_norm_g, w_out=m_w_out)
    vs = dict(norm_g=v_norm_g, mem_norm_g=v_mem_norm_g, w_in=v_w_in, conv_w=v_conv_w, conv_b=v_conv_b, w_rg=v_w_rg,
              b_rg=v_b_rg, w_ig=v_w_ig, b_ig=v_b_ig, lru_lambda=v_lru_lambda, q_norm_g=v_q_norm_g, k_norm_g=v_k_norm_g,
              sinks=v_sinks, w_mem_kv=v_w_mem_kv, xq_norm_g=v_xq_norm_g, xk_norm_g=v_xk_norm_g,
              out_norm_g=v_out_norm_g, w_out=v_w_out)

    delta, new_m, new_v = {}, {}, {}
    small_names = [n for n, _, _ in SMALL_VECTORS] + list(SMALL_MATRICES)
    (res_in, res_out, res_kv), res = adamw(
        [(w_in[0].T, r_in.reshape(D_IN // 4, D_MODEL), m_w_in[0].T, v_w_in[0].T),
         (w_out[0], r_out.reshape(D_MODEL // 4, D_MODEL), m_w_out[0], v_w_out[0]),
         (w_mem_kv[0], r_kv.reshape(D_MODEL // 4, 2 * XATT_W), m_w_mem_kv[0], v_w_mem_kv[0])],
        r_small, [weights[n] for n in small_names], [ms[n] for n in small_names], [vs[n] for n in small_names])
    grads["w_in"], delta["w_in"], new_m["w_in"], new_v["w_in"] = (r.T[None] for r in res_in)
    grads["w_out"], delta["w_out"], new_m["w_out"], new_v["w_out"] = (r[None] for r in res_out)
    grads["w_mem_kv"], delta["w_mem_kv"], new_m["w_mem_kv"], new_v["w_mem_kv"] = (r[None] for r in res_kv)
    nall = len(small_names)
    for k, into in enumerate((grads, delta, new_m, new_v)):
        into.update(zip(small_names, res[k * nall:(k + 1) * nall]))
    loss = res[-1].reshape(())

    order = ("norm_g", "mem_norm_g", "w_in", "conv_w", "conv_b", "w_rg", "b_rg", "w_ig", "b_ig", "lru_lambda",
             "q_norm_g", "k_norm_g", "sinks", "w_mem_kv", "xq_norm_g", "xk_norm_g", "out_norm_g", "w_out")
    return (loss, gx[None], *[grads[n] for n in order], *[delta[n] for n in order], *[new_m[n] for n in order],
            *[new_v[n] for n in order])
```

```python
import jax
import jax.numpy as jnp
import numpy as np
from jax import lax
from jax.experimental import pallas as pl
from jax.experimental.pallas import tpu as pltpu

F32 = jnp.float32
_MXU = jnp.bfloat16
_WIRE = jnp.bfloat16

D_MODEL = 1024
MEM_LEN = 256
HEAD = 64
LRU_W = 512
LRU_BLOCKS = 8
CONV_K = 4
LRU_C = 8.0
SWA_W = 256
KV_W = 128
XATT_W = 256
BLOCK = 128
D_IN = 2304
ROPE_THETA = 500000.0
ROPE_DIM = 16
EPS = 1e-6
NEG_INF = -1e30
C_LRUX, C_LRUG, C_SQ, C_SK, C_SV, C_SWAG, C_XQ, C_XG = 0, 512, 1024, 1280, 1408, 1536, 1792, 2048
G_Q, G_K, G_XQ, G_XK, GAINS_W = 0, 128, 256, 512, 768

ADAM_LR, ADAM_B1, ADAM_B2, ADAM_EPS, ADAM_WD, ADAM_STEP = 0.001, 0.9, 0.999, 1e-08, 0.01, 10

N_CHIPS = 4
ROW_TILE = 256
VMEM_LIMIT = 56 * 1024 * 1024
ADAM_BLOCK_BYTES = 2304 * 1024
MESH = pl.DeviceIdType.MESH


def _mm(a, b):
    return jnp.dot(a.astype(_MXU), b.astype(_MXU), preferred_element_type=F32)


def _mm_nt(a, b):
    return lax.dot_general(a.astype(_MXU), b.astype(_MXU), (((1,), (1,)), ((), ())), preferred_element_type=F32)


def _mm_tn(a, b):
    return lax.dot_general(a.astype(_MXU), b.astype(_MXU), (((0,), (0,)), ((), ())), preferred_element_type=F32)


def _group_matrix(width):
    r = lax.shift_right_logical(lax.broadcasted_iota(jnp.int32, (width, width), 0), 6)
    c = lax.shift_right_logical(lax.broadcasted_iota(jnp.int32, (width, width), 1), 6)
    return (r == c).astype(_MXU)


def _seg_mean(x, gm):
    return jnp.dot(x.astype(_MXU), gm, preferred_element_type=F32) * (1.0 / HEAD)


def _row_mean(x):
    return jnp.mean(x, axis=-1, keepdims=True)


def _col_sum(x):
    return jnp.sum(x, axis=0, keepdims=True)


def _sigmoid(x):
    return jax.nn.sigmoid(x)


def _softplus(z):
    e = jnp.exp(-jnp.abs(z))
    u = 1.0 + e
    log1p_e = jnp.where(u == 1.0, e, jnp.log(u) * (e / (u - 1.0)))
    return jnp.maximum(z, 0.0) + log1p_e


def _rope(t, c, s1, s2):
    return t * c + pltpu.roll(t, 120, 1) * s1 + pltpu.roll(t, 8, 1) * s2


def _rope_bwd(d, c, s1, s2):
    return d * c + pltpu.roll(d * s1, 8, 1) + pltpu.roll(d * s2, 120, 1)


def _fold_heads(v):
    out = v
    for k in range(1, v.shape[1] // HEAD):
        out = out + pltpu.roll(v, HEAD * k, 1)
    return out


def _lane_mask(width, lo, hi):
    lane = lax.broadcasted_iota(jnp.int32, (1, width), 1)
    return ((lane >= lo) & (lane < hi)).astype(F32)


def _swa_mask(first_block):
    qi = lax.broadcasted_iota(jnp.int32, (BLOCK, 2 * BLOCK), 0)
    kj = lax.broadcasted_iota(jnp.int32, (BLOCK, 2 * BLOCK), 1)
    rel = qi + BLOCK - kj
    ok = (rel >= 0) & (rel < BLOCK)
    return ok & (jnp.logical_not(first_block) | (kj >= BLOCK))


def _place_kv(t, scale):
    lo = t * (_lane_mask(KV_W, 0, HEAD) * scale)
    hi = t * (_lane_mask(KV_W, HEAD, KV_W) * scale)
    return [a.astype(_MXU) for a in (lo, pltpu.roll(lo, HEAD, 1), pltpu.roll(hi, HEAD, 1), hi)]


def _unplace_kv(d):
    return (_lane_mask(KV_W, 0, HEAD) * (d[0] + pltpu.roll(d[1], HEAD, 1))
            + _lane_mask(KV_W, HEAD, KV_W) * (d[3] + pltpu.roll(d[2], HEAD, 1)))


def _swa_probs(qh, ka, mask, sink):
    s = _mm_nt(qh, ka)
    s = jnp.where(mask, s, NEG_INF)
    m = jnp.maximum(jnp.max(s, axis=-1, keepdims=True), sink)
    p = jnp.exp(s - m)
    esink = jnp.exp(sink - m)
    inv = 1.0 / (jnp.sum(p, axis=-1, keepdims=True) + esink)
    return p * inv, esink * inv


def _mem_probs(s_all):
    out = []
    for j in range(4):
        s = s_all[:, MEM_LEN * j:MEM_LEN * (j + 1)]
        p = jnp.exp(s - jnp.max(s, axis=-1, keepdims=True))
        out.append(p * (1.0 / jnp.sum(p, axis=-1, keepdims=True)))
    return out


def _head_rows(t, scale):
    return jnp.concatenate([t * (_lane_mask(XATT_W, HEAD * j, HEAD * (j + 1)) * scale) for j in range(4)], axis=0)


def _lru_gates(xc, wg_ref, brg, big, lam):
    p0 = _mm(xc[:, :256], wg_ref[0])
    p1 = _mm(xc[:, 256:], wg_ref[1])
    rg = _sigmoid(jnp.concatenate([p0[:, :256], p1[:, :256]], axis=1) + brg)
    ig = _sigmoid(jnp.concatenate([p0[:, 256:], p1[:, 256:]], axis=1) + big)
    sp = _softplus(-lam)
    la = (-LRU_C) * rg * sp
    a = jnp.exp(la)
    th = jnp.tanh(la)
    one_minus_a2 = (-2.0 * th) / (1.0 - th)
    return rg, ig, sp, a, jnp.sqrt(one_minus_a2)


def _const_spec(shape, single=False):
    zeros = (0,) * len(shape)
    if single:
        return pl.BlockSpec(shape, lambda i: zeros, pipeline_mode=pl.Buffered(1))
    return pl.BlockSpec(shape, lambda i: zeros)


def _chip_of(x, y):
    return 2 * x + y


def _partners(x, y, c):
    north = c == 1
    near = (jnp.where(north, 1 - x, x), jnp.where(north, y, 1 - y))
    far = (jnp.where(north, x, 1 - x), jnp.where(north, 1 - y, y))
    return near, far, (1 - x, 1 - y)


def gather_weights(win_t, wout, wkv, conv_w, w_rg, w_ig, head_gains, mem, mem_g):
    arrs = (win_t, wout, wkv)
    n = len(arrs)
    pieces = [(a, 0, arr.shape[0] // 2) for a, arr in enumerate(arrs)]
    npc = len(pieces)

    def body(a0, a1, a2, cw_in, wrg_ref, wig_ref, q_ref, k_ref, xq_ref, xk_ref, mem_ref, mg_ref,
             o0, o1, o2, cw_out, wg_ref, gn_ref, km_ref, vm_ref, s0, s1, s2, cw, ocw, send, recv, lsem):
        ins, outs = (s0, s1, s2), (o0, o1, o2)
        for src, dst in zip((a0, a1, a2), ins):
            dst[...] = src[...].astype(dst.dtype)
        cw[...] = jnp.zeros(cw.shape, F32)
        cw[0:CONV_K, :] = cw_in[0]
        x, y, c = lax.axis_index("x"), lax.axis_index("y"), lax.axis_index("c")
        sibling = (x, y, 1 - c)
        near, far, diag = _partners(x, y, c)
        chips = [near, far, diag]
        me = _chip_of(x, y)

        def landed(p, chip, half):
            a, off, rows_ = pieces[p]
            r = ins[a].shape[0]
            return outs[a].at[pl.ds(pl.multiple_of(chip * r + half * (r // 2) + off, 16), rows_)]

        def mine(p):
            a, off, rows_ = pieces[p]
            return ins[a].at[pl.ds(pl.multiple_of(c * (ins[a].shape[0] // 2) + off, 16), rows_)]

        def copy(k, src, dst, to):
            return pltpu.make_async_remote_copy(src_ref=src, dst_ref=dst, send_sem=send.at[k], recv_sem=recv.at[k],
                                                device_id=to, device_id_type=MESH)

        def cw_rows(chip):
            return ocw.at[pl.ds(pl.multiple_of(chip * 8, 8), 8)]

        locals_ = []
        for a in range(n):
            r = ins[a].shape[0]
            locals_.append(pltpu.make_async_copy(ins[a], outs[a].at[pl.ds(pl.multiple_of(me * r, 16), r)], lsem.at[a]))
        locals_.append(pltpu.make_async_copy(cw, cw_rows(me), lsem.at[n]))
        for cp in locals_:
            cp.start()

        sent = []
        for p in range(npc):
            for j in range(2):
                sent.append(copy(p * 6 + j, mine(p), landed(p, me, c), (*chips[j], c)))
        for j, chip in enumerate(chips):
            sent.append(copy(npc * 6 + j, cw, cw_rows(me), (*chip, c)))
        for cp in sent:
            cp.start()

        gn_ref[...] = jnp.concatenate([q_ref[...]] * 2 + [k_ref[...]] * 2 + [xq_ref[...]] * 4 + [xk_ref[...]] * 4,
                                      axis=1)
        zeros = lambda lanes: [jnp.zeros((HEAD, lanes), F32)] if lanes else []
        for h in range(2):
            for b in range(4):
                row = []
                for w_ref in (wrg_ref, wig_ref):
                    row += zeros(HEAD * b) + [w_ref[0, 4 * h + b]] + zeros(HEAD * (3 - b))
                wg_ref[h, HEAD * b:HEAD * (b + 1), :] = jnp.concatenate(row, axis=1).astype(wg_ref.dtype)

        for j in range(3):
            for p in range(npc):
                got = landed(p, _chip_of(*chips[j]), c)
                copy(p * 6 + j, got, got, sibling).wait_recv()
                if j == 0:
                    sent.append(copy(p * 6 + 2, got, got, (*far, c)))
                    sent[-1].start()
                sent.append(copy(p * 6 + 3 + j, got, got, sibling))
                sent[-1].start()
        for p in range(npc):
            for j in range(3):
                got = landed(p, _chip_of(*chips[(1, 0, 2)[j]]), 1 - c)
                copy(p * 6 + 3 + j, got, got, sibling).wait_recv()
        for j, chip in enumerate(chips):
            got = cw_rows(_chip_of(*chip))
            copy(npc * 6 + j, got, got, (*chip, c)).wait_recv()
        for cp in sent:
            cp.wait_send()
        for cp in locals_:
            cp.wait()
        for chip in range(N_CHIPS):
            cw_out[:, 128 * chip:128 * (chip + 1)] = ocw[8 * chip:8 * chip + CONV_K, :]

        mem_v = mem_ref[...]
        mn = mem_v * lax.rsqrt(_row_mean(mem_v * mem_v) + EPS) * mg_ref[...]
        mkv = _mm(mn, o2[...])
        kpre = mkv[:, :XATT_W]
        km = kpre * lax.rsqrt(_seg_mean(kpre * kpre, _group_matrix(XATT_W)) + EPS) * gn_ref[:, G_XK:GAINS_W]
        km_ref[...] = _head_rows(km, 0.125).astype(km_ref.dtype)
        vm_ref[...] = _head_rows(mkv[:, XATT_W:], 1.0).astype(vm_ref.dtype)

    vm = pl.BlockSpec(memory_space=pltpu.VMEM)
    hbm = pl.BlockSpec(memory_space=pl.ANY)
    head_rows = jax.ShapeDtypeStruct((4 * MEM_LEN, XATT_W), _MXU)
    out_shape = tuple(jax.ShapeDtypeStruct((N_CHIPS * a.shape[0],) + a.shape[1:], _MXU) for a in arrs) + (
        jax.ShapeDtypeStruct((CONV_K, LRU_W), F32), jax.ShapeDtypeStruct((2, 256, 512), _MXU),
        jax.ShapeDtypeStruct((1, GAINS_W), F32), head_rows, head_rows)
    n_rdma = npc * 6 + 3
    return pl.pallas_call(
        body, name="gather_weights", out_shape=out_shape,
        in_specs=[vm] * 12, out_specs=(hbm, hbm, vm, vm, vm, vm, vm, vm),
        scratch_shapes=[pltpu.VMEM(a.shape, _MXU) for a in arrs] + [
            pltpu.VMEM((8, 128), F32), pltpu.VMEM((N_CHIPS * 8, 128), F32),
            pltpu.SemaphoreType.DMA((n_rdma,)), pltpu.SemaphoreType.DMA((n_rdma,)), pltpu.SemaphoreType.DMA((n + 1,))],
        compiler_params=pltpu.CompilerParams(vmem_limit_bytes=VMEM_LIMIT),
    )(win_t, wout, wkv, conv_w, w_rg, w_ig, *head_gains, mem, mem_g)


def _mem_bwd_and_pack(mem_ref, g_ref, w_ref, gn_ref, dkm_ref, dvm_ref, gg_ref, loss_ref, vectors, gw_ref, pk_ref):
    first_row = {name: (row, width) for name, row, width in SMALL_VECTORS}
    half_rows = SMALL_ROWS // 2
    pk_ref[...] = jnp.zeros(pk_ref.shape, F32)

    def rows_at(at, n):
        assert at // half_rows == (at + n - 1) // half_rows
        return at // half_rows, slice(at % half_rows, at % half_rows + n), slice(None)

    def put(name, src):
        row, width = first_row[name]
        per_row = 1 if width < 128 else src.shape[1] // 128
        for t in range(src.shape[0]):
            for r in range(per_row):
                pk_ref[rows_at(row + per_row * t + r, 1)] = src[t:t + 1, 128 * r:128 * (r + 1)]

    for name, ref in vectors.items():
        put(name, ref)
    pk_ref[rows_at(LOSS_ROW, 1)] = loss_ref[0:1, :]
    upper = lax.broadcasted_iota(jnp.int32, (HEAD, 128), 1) >= HEAD
    for h in range(2):
        for b in range(4):
            rg = gg_ref[h, HEAD * b:HEAD * (b + 1), 128 * (b // 2):128 * (b // 2 + 1)]
            ig = gg_ref[h, HEAD * b:HEAD * (b + 1), 256 + 128 * (b // 2):256 + 128 * (b // 2 + 1)]
            if b % 2:
                rg = pltpu.roll(rg, HEAD, axis=1)
            else:
                ig = pltpu.roll(ig, HEAD, axis=1)
            pk_ref[rows_at(GATES_ROW + HEAD * (4 * h + b), HEAD)] = jnp.where(upper, ig, rg)

    mem_v = mem_ref[...]
    mh = mem_v * lax.rsqrt(_row_mean(mem_v * mem_v) + EPS)
    mn = mh * g_ref[...]
    mkv = _mm(mn, w_ref[...])
    kpre = mkv[:, :XATT_W]
    gm = _group_matrix(XATT_W)
    rk = lax.rsqrt(_seg_mean(kpre * kpre, gm) + EPS)
    kn = kpre * rk
    dk = jnp.zeros((MEM_LEN, XATT_W), F32)
    dv = jnp.zeros((MEM_LEN, XATT_W), F32)
    for j in range(4):
        mj = _lane_mask(XATT_W, HEAD * j, HEAD * (j + 1))
        dk = dk + dkm_ref[:, MEM_LEN * j:MEM_LEN * (j + 1)].T * (mj * 0.125)
        dv = dv + dvm_ref[:, MEM_LEN * j:MEM_LEN * (j + 1)].T * mj
    put("xk_norm_g", _fold_heads(_col_sum(dk * kn)))
    dkn = dk * gn_ref[:, G_XK:GAINS_W]
    dkpre = rk * (dkn - kn * _seg_mean(dkn * kn, gm))
    dmkv = jnp.concatenate([dkpre, dv], axis=1)
    gw_ref[...] = _mm_tn(mn, dmkv).reshape(gw_ref.shape)
    dmn = _mm_nt(dmkv, w_ref[...])
    put("mem_norm_g", _col_sum(dmn * mh))


def layer_fwd(x, tgt, rc, rs1, rs2, ng, win_t, cw, cb, wg, brg, big, lam, gains, sinks, km, vm, og, wout):
    seq = x.shape[0]
    tm = min(ROW_TILE, seq)
    nt = seq // tm
    nb = tm // BLOCK

    def body(x_ref, t_ref, c_ref, s1_ref, s2_ref, ng_ref, win_ref, cw_ref, cb_ref, wg_ref, brg_ref, big_ref, lam_ref,
             gn_ref, sink_ref, km_ref, vm_ref, og_ref, wout_ref,
             proj_ref, ya_ref, yb_ref, yc_ref, ycat_ref, xn_ref, dout_ref, pswa_ref, pmem_ref, psink_ref, gates_ref,
             a_ref, loss_ref,
             ext_ref, b_scr, hc_ref, kp_ref, vp_ref, lacc_ref):
        i = pl.program_id(0)

        @pl.when(i == 0)
        def _():
            ext_ref[0:8, :] = jnp.zeros((8, LRU_W), F32)
            hc_ref[...] = jnp.zeros_like(hc_ref)
            kp_ref[...] = jnp.zeros_like(kp_ref)
            vp_ref[...] = jnp.zeros_like(vp_ref)
            lacc_ref[...] = jnp.zeros_like(lacc_ref)

        xv = x_ref[...]
        xn = (xv * lax.rsqrt(_row_mean(xv * xv) + EPS) * ng_ref[...]).astype(_MXU)
        xn_ref[...] = xn.astype(xn_ref.dtype)
        proj_ref[...] = _mm_nt(xn, win_ref[...])

        u = proj_ref[:, C_LRUX:C_LRUX + LRU_W]
        ext_ref[8:8 + tm, :] = u
        xc = cb_ref[...]
        for k in range(CONV_K):
            xc = xc + cw_ref[k:k + 1, :] * ext_ref[pl.ds(5 + k, tm), :]
        ext_ref[0:8, :] = u[tm - 8:tm, :]
        rg, ig, sp, a, sq = _lru_gates(xc, wg_ref, brg_ref[...], big_ref[...], lam_ref[...])
        for k, t in enumerate((xc, rg, ig, sq)):
            gates_ref[:, LRU_W * k:LRU_W * (k + 1)] = t.astype(gates_ref.dtype)
        a_ref[...] = a
        b_scr[...] = sq * (ig * xc)
        row8 = lax.broadcasted_iota(jnp.int32, (8, LRU_W), 0)

        def scan_step(g, carry):
            r0 = pl.multiple_of(g * 8, 8)
            av = a_ref[pl.ds(r0, 8), :]
            bv = b_scr[pl.ds(r0, 8), :]
            for d in (1, 2, 4):
                a_sh = jnp.where(row8 >= d, pltpu.roll(av, d, 0), 1.0)
                b_sh = jnp.where(row8 >= d, pltpu.roll(bv, d, 0), 0.0)
                bv = bv + av * b_sh
                av = av * a_sh
            hv = bv + av * carry
            ya_ref[pl.ds(r0, 8), :] = hv
            return hv[7:8, :]

        hc_ref[0:1, :] = lax.fori_loop(0, tm // 8, scan_step, hc_ref[0:1, :], unroll=True)

        gm128 = _group_matrix(KV_W)
        cv, s1v, s2v = c_ref[...], s1_ref[...], s2_ref[...]

        def head_norm_rope(t, g):
            n = t * lax.rsqrt(_seg_mean(t * t, gm128) + EPS)
            return _rope(n * g, cv, s1v, s2v)

        qs_ = (head_norm_rope(proj_ref[:, C_SQ:C_SQ + 128], gn_ref[:, G_Q:G_K]).astype(_MXU),
               head_norm_rope(proj_ref[:, C_SQ + 128:C_SQ + 256], gn_ref[:, G_Q:G_K]).astype(_MXU))
        kr = head_norm_rope(proj_ref[:, C_SK:C_SK + KV_W], gn_ref[:, G_K:G_XQ])
        sv = proj_ref[:, C_SV:C_SV + KV_W]
        ka = _place_kv(jnp.concatenate([kp_ref[...], kr], axis=0), 0.125)
        va = _place_kv(jnp.concatenate([vp_ref[...], sv], axis=0), 1.0)
        kp_ref[...] = kr[tm - BLOCK:tm, :]
        vp_ref[...] = sv[tm - BLOCK:tm, :]
        lane128 = lax.broadcasted_iota(jnp.int32, (1, 128), 1)
        for b in range(nb):
            mask = _swa_mask((i == 0) & (b == 0)) if b == 0 else _swa_mask(False)
            band = slice(BLOCK * b, BLOCK * b + 2 * BLOCK)
            blk = slice(BLOCK * b, BLOCK * (b + 1))
            psink = jnp.zeros((BLOCK, 128), F32)
            for j in range(4):
                p, pk = _swa_probs(qs_[j // 2][blk], ka[j][band], mask, sink_ref[0, j])
                pswa_ref[blk, 2 * BLOCK * j:2 * BLOCK * (j + 1)] = p.astype(pswa_ref.dtype)
                psink = jnp.where(lane128 == j, pk, psink)
            psink_ref[blk, :] = psink
            for h in range(2):
                yb_ref[blk, KV_W * h:KV_W * (h + 1)] = _mm(
                    pswa_ref[blk, 4 * BLOCK * h:4 * BLOCK * (h + 1)],
                    jnp.concatenate([va[2 * h][band], va[2 * h + 1][band]], axis=0))

        gm256 = _group_matrix(XATT_W)
        xq = proj_ref[:, C_XQ:C_XQ + XATT_W]
        qx = xq * lax.rsqrt(_seg_mean(xq * xq, gm256) + EPS) * gn_ref[:, G_XQ:G_XK]
        pm = _mem_probs(_mm_nt(qx, km_ref[...]))
        for j in range(4):
            pmem_ref[:, MEM_LEN * j:MEM_LEN * (j + 1)] = pm[j].astype(pmem_ref.dtype)
        yc = _mm(pmem_ref[...], vm_ref[...])
        yc_ref[...] = yc

        def gated(y, g, gate):
            return y * lax.rsqrt(_row_mean(y * y) + EPS) * g * (gate * _sigmoid(gate))

        ogv = og_ref[...]
        za = gated(ya_ref[...], ogv[:, :512], proj_ref[:, C_LRUG:C_LRUG + LRU_W])
        zb = gated(yb_ref[...], ogv[:, 512:768], proj_ref[:, C_SWAG:C_SWAG + SWA_W])
        zc = gated(yc, ogv[:, 768:], proj_ref[:, C_XG:C_XG + XATT_W])
        ycat_ref[:, 0:512] = za.astype(ycat_ref.dtype)
        ycat_ref[:, 512:768] = zb.astype(ycat_ref.dtype)
        ycat_ref[:, 768:1024] = zc.astype(ycat_ref.dtype)
        out = xv + _mm(ycat_ref[...], wout_ref[...])
        err = out - t_ref[...]
        dout_ref[...] = (err * (1.0 / D_MODEL)).astype(dout_ref.dtype)
        lacc_ref[...] = lacc_ref[...] + (0.5 / D_MODEL) * jnp.sum(err * err)

        @pl.when(i == nt - 1)
        def _():
            loss_ref[...] = lacc_ref[...]

    def rows(ncol):
        return pl.BlockSpec((tm, ncol), lambda i: (i, 0))

    in_specs = [rows(D_MODEL), rows(D_MODEL), rows(128), rows(128), rows(128),
                _const_spec((1, D_MODEL)), _const_spec((D_IN, D_MODEL), True), _const_spec((CONV_K, LRU_W)),
                _const_spec((1, LRU_W)), _const_spec((2, 256, 512), True), _const_spec((1, LRU_W)),
                _const_spec((1, LRU_W)), _const_spec((1, LRU_W)), _const_spec((1, GAINS_W)), pl.BlockSpec(memory_space=pltpu.SMEM),
                _const_spec((4 * MEM_LEN, XATT_W), True), _const_spec((4 * MEM_LEN, XATT_W), True),
                _const_spec((1, D_MODEL)), _const_spec((D_MODEL, D_MODEL), True)]
    out_shape = (jax.ShapeDtypeStruct((seq, D_IN), F32), jax.ShapeDtypeStruct((seq, LRU_W), F32),
                 jax.ShapeDtypeStruct((seq, SWA_W), F32), jax.ShapeDtypeStruct((seq, XATT_W), F32),
                 jax.ShapeDtypeStruct((seq, D_MODEL), _MXU), jax.ShapeDtypeStruct((seq, D_MODEL), _MXU),
                 jax.ShapeDtypeStruct((seq, D_MODEL), _MXU), jax.ShapeDtypeStruct((seq, 4 * 2 * BLOCK), _MXU),
                 jax.ShapeDtypeStruct((seq, 4 * MEM_LEN), _MXU), jax.ShapeDtypeStruct((seq, 128), F32),
                 jax.ShapeDtypeStruct((seq, 4 * LRU_W), _MXU), jax.ShapeDtypeStruct((seq, LRU_W), F32),
                 jax.ShapeDtypeStruct((8, 128), F32))
    out_specs = (rows(D_IN), rows(LRU_W), rows(SWA_W), rows(XATT_W), rows(D_MODEL), rows(D_MODEL), rows(D_MODEL),
                 rows(4 * 2 * BLOCK), rows(4 * MEM_LEN), rows(128), rows(4 * LRU_W), rows(LRU_W),
                 _const_spec((8, 128)))
    scratch = [pltpu.VMEM((tm + 8, LRU_W), F32), pltpu.VMEM((tm, LRU_W), F32),
               pltpu.VMEM((8, LRU_W), F32), pltpu.VMEM((BLOCK, KV_W), F32), pltpu.VMEM((BLOCK, KV_W), F32),
               pltpu.VMEM((8, 128), F32)]
    return pl.pallas_call(
        body, name="layer_fwd", grid=(nt,), out_shape=out_shape, in_specs=in_specs, out_specs=out_specs,
        scratch_shapes=scratch,
        compiler_params=pltpu.CompilerParams(dimension_semantics=("arbitrary",), vmem_limit_bytes=VMEM_LIMIT),
    )(x, tgt, rc, rs1, rs2, ng, win_t, cw, cb, wg, brg, big, lam, gains, sinks, km, vm, og, wout)


def weight_grads(ycat, dout, dproj, xn, mem_operands, vectors, early_at, late_at):
    seq, ncol = xn.shape
    blk = 256
    n_out, n_in = ycat.shape[1] // blk, dproj.shape[1] // blk
    assert n_out == N_CHIPS and late_at[0] >= n_out
    g_out = jax.ShapeDtypeStruct((N_CHIPS, 2, blk // 2, dout.shape[1]), F32)
    g_kv = jax.ShapeDtypeStruct((N_CHIPS, 2, D_MODEL // 8, 2 * XATT_W), F32)
    g_small = jax.ShapeDtypeStruct((2, SMALL_ROWS // 2, 128), F32)
    shape_e, scratch_e = _hosted_reduce_shapes([g_kv], g_small)
    shape_l, scratch_l = _hosted_reduce_shapes([g_out], None)
    n_mem, names = len(mem_operands), tuple(vectors)

    def body(l1_ref, r1_ref, l2_ref, r2_hbm, *refs):
        mem_refs, vec_refs = refs[:n_mem], refs[n_mem:n_mem + len(names)]
        o_ref, sum_out, sum_kv, sum_sm, gout_scr, gkv_scr, pack_scr, r2_ref, r2_sem, *scratch = refs[n_mem + len(names):]
        j = pl.program_id(0)
        r2_copy = pltpu.make_async_copy(r2_hbm, r2_ref, r2_sem.at[0])

        @pl.when(j == 0)
        def _():
            r2_copy.start()
            _mem_bwd_and_pack(*mem_refs, dict(zip(names, vec_refs)), gkv_scr, pack_scr)

        def reduce_stages(closing):
            _hosted_reduce(j, n_out + n_in, closing, early_at, (gkv_scr, pack_scr), (sum_kv, sum_sm),
                           scratch[:len(scratch_e)], True)
            _hosted_reduce(j, n_out + n_in, closing, late_at, (gout_scr,), (sum_out,), scratch[len(scratch_e):], False)

        reduce_stages(False)

        @pl.when(j < n_out)
        def _():
            gout_scr[j] = _mm_tn(l1_ref[...], r1_ref[...]).reshape(g_out.shape[1:])

        pl.when(j == n_out)(r2_copy.wait)

        @pl.when(j >= n_out)
        def _():
            o_ref[...] = _mm_tn(l2_ref[...], r2_ref[...])

        reduce_stages(True)

    vm = pl.BlockSpec(memory_space=pltpu.VMEM)
    hbm = pl.BlockSpec(memory_space=pl.ANY)
    return pl.pallas_call(
        body, name="weight_grads", grid=(n_out + n_in,),
        out_shape=(jax.ShapeDtypeStruct((dproj.shape[1], ncol), F32), *shape_l, *shape_e),
        in_specs=[pl.BlockSpec((seq, blk), lambda j: (0, jnp.minimum(j, n_out - 1))), _const_spec(dout.shape, True),
                  pl.BlockSpec((seq, blk), lambda j: (0, jnp.maximum(j - n_out, 0))), hbm]
        + [vm] * (n_mem + len(names)),
        out_specs=(pl.BlockSpec((blk, ncol), lambda j: (jnp.maximum(j - n_out, 0), 0)), hbm, hbm, hbm),
        scratch_shapes=[pltpu.VMEM(s.shape, F32) for s in (g_out, g_kv, g_small)]
        + [pltpu.VMEM(xn.shape, xn.dtype), pltpu.SemaphoreType.DMA((1,))] + scratch_e + scratch_l,
        compiler_params=pltpu.CompilerParams(dimension_semantics=("arbitrary",), vmem_limit_bytes=VMEM_LIMIT),
    )(ycat, dout, dproj, xn, *mem_operands, *vectors.values())


def layer_bwd(x, dout, proj, ya, yb, yc, pswa, pmem, psink, gates, a_all, rc, rs1, rs2, ng, win_t, cw, wg, lam, gains,
              km, vm, og, wout):
    seq = x.shape[0]
    tm = min(ROW_TILE, seq)
    nt = seq // tm
    nb = tm // BLOCK

    def body(x_ref, dout_ref, proj_ref, ya_ref, yb_ref, yc_ref, pswa_ref, pmem_ref, psink_ref, gates_ref, a_ref,
             c_ref, s1_ref, s2_ref,
             yah_ref, kvh_ref, ch_ref, s1h_ref, s2h_ref,
             ng_ref, win_ref, cw_ref, wg_ref, lam_ref, gn_ref, km_ref, vm_ref, og_ref, wout_ref,
             gx_ref, dproj_ref, gwg_ref, dkm_ref, dvm_ref, gng_ref, gog_ref, gcb_ref, gbrg_ref, gbig_ref, glam_ref,
             gcw_ref, gqn_ref, gkn_ref, gxqn_ref, gsink_ref,
             hext_ref, aext_ref, an_scr, dh_scr, g_scr, dxc_ext, gcar_ref, dkcar_ref, dvcar_ref):
        i = pl.program_id(0)
        tile = nt - 1 - i
        first_tile = tile == 0

        @pl.when(i == 0)
        def _():
            for r in (gwg_ref, dkm_ref, dvm_ref, gng_ref, gog_ref, gcb_ref, gbrg_ref, gbig_ref, glam_ref, gcw_ref,
                      gqn_ref, gkn_ref, gxqn_ref, gsink_ref, gcar_ref, dkcar_ref, dvcar_ref):
                r[...] = jnp.zeros_like(r)
            dxc_ext[tm:tm + 8, :] = jnp.zeros((8, LRU_W), F32)
            aext_ref[tm:tm + 8, :] = jnp.zeros((8, LRU_W), F32)

        xv = x_ref[...]
        dov = dout_ref[...]
        dz = _mm_nt(dov, wout_ref[...])
        ogv = og_ref[...]

        def group_bwd(y, gate, g, dzg):
            r = lax.rsqrt(_row_mean(y * y) + EPS)
            n = y * r
            sg = _sigmoid(gate)
            dgate = dzg * (n * g) * (sg * (1.0 + gate * (1.0 - sg)))
            dng = dzg * (gate * sg)
            dn = dng * g
            return r * (dn - n * _row_mean(dn * n)), dgate, _col_sum(dng * n)

        dya, dga, goa = group_bwd(ya_ref[...], proj_ref[:, C_LRUG:C_LRUG + LRU_W], ogv[:, :512], dz[:, :512])
        dyb, dgb, gob = group_bwd(yb_ref[...], proj_ref[:, C_SWAG:C_SWAG + SWA_W], ogv[:, 512:768], dz[:, 512:768])
        dyc, dgc, goc = group_bwd(yc_ref[...], proj_ref[:, C_XG:C_XG + XATT_W], ogv[:, 768:], dz[:, 768:])
        gog_ref[...] += jnp.concatenate([goa, gob, goc], axis=1)
        dproj_ref[:, C_LRUG:C_LRUG + LRU_W] = dga.astype(dproj_ref.dtype)
        dproj_ref[:, C_SWAG:C_SWAG + SWA_W] = dgb.astype(dproj_ref.dtype)
        dproj_ref[:, C_XG:C_XG + XATT_W] = dgc.astype(dproj_ref.dtype)

        gm256 = _group_matrix(XATT_W)
        xq = proj_ref[:, C_XQ:C_XQ + XATT_W]
        rq = lax.rsqrt(_seg_mean(xq * xq, gm256) + EPS)
        qn = xq * rq
        qx = qn * gn_ref[:, G_XQ:G_XK]
        qxb = qx.astype(_MXU)
        dycb = dyc.astype(_MXU)
        dp_all = _mm_nt(dycb, vm_ref[...])
        dsm = []
        for j in range(4):
            pj = pmem_ref[:, MEM_LEN * j:MEM_LEN * (j + 1)].astype(F32)
            dp = dp_all[:, MEM_LEN * j:MEM_LEN * (j + 1)]
            dsm.append((pj * (dp - jnp.sum(pj * dp, axis=-1, keepdims=True))).astype(_MXU))
        ds_all = jnp.concatenate(dsm, axis=1)
        dvm_ref[...] += _mm_tn(dycb, pmem_ref[...])
        dkm_ref[...] += _mm_tn(qxb, ds_all)
        dqx = _mm(ds_all, km_ref[...])
        gxqn_ref[...] += _col_sum(dqx * qn)
        dqn = dqx * gn_ref[:, G_XQ:G_XK]
        dproj_ref[:, C_XQ:C_XQ + XATT_W] = (rq * (dqn - qn * _seg_mean(dqn * qn, gm256))).astype(dproj_ref.dtype)

        gm128 = _group_matrix(KV_W)
        cv, s1v, s2v = c_ref[...], s1_ref[...], s2_ref[...]

        def head_norm(t):
            r = lax.rsqrt(_seg_mean(t * t, gm128) + EPS)
            return t * r, r

        qn_, qr_ = zip(head_norm(proj_ref[:, C_SQ:C_SQ + 128]), head_norm(proj_ref[:, C_SQ + 128:C_SQ + 256]))
        qrope = [_rope(qn_[h] * gn_ref[:, G_Q:G_K], cv, s1v, s2v).astype(_MXU) for h in range(2)]
        kn, krr = head_norm(proj_ref[:, C_SK:C_SK + KV_W])
        kr = _rope(kn * gn_ref[:, G_K:G_XQ], cv, s1v, s2v)
        khn, _ = head_norm(kvh_ref[:, 0:KV_W])
        khr = _rope(khn * gn_ref[:, G_K:G_XQ], ch_ref[...], s1h_ref[...], s2h_ref[...])
        ka = _place_kv(jnp.concatenate([khr, kr], axis=0), 0.125)
        va = _place_kv(jnp.concatenate([kvh_ref[:, KV_W:2 * KV_W], proj_ref[:, C_SV:C_SV + KV_W]], axis=0), 1.0)
        lane128 = lax.broadcasted_iota(jnp.int32, (1, 128), 1)
        gsink = jnp.zeros((1, 128), F32)
        dk_band, dv_band, dq_blk = [], [], []
        for b in range(nb):
            band = slice(BLOCK * b, BLOCK * b + 2 * BLOCK)
            blk = slice(BLOCK * b, BLOCK * (b + 1))
            dka, dva, dsb = [], [], []
            deltas = jnp.zeros((BLOCK, 128), F32)
            for j in range(4):
                qh = qrope[j // 2][blk]
                doh = dyb[blk, KV_W * (j // 2):KV_W * (j // 2 + 1)].astype(_MXU)
                pb = pswa_ref[blk, 2 * BLOCK * j:2 * BLOCK * (j + 1)]
                p = pb.astype(F32)
                dp = _mm_nt(doh, va[j][band])
                delta = jnp.sum(p * dp, axis=-1, keepdims=True)
                ds = (p * (dp - delta)).astype(_MXU)
                deltas = jnp.where(lane128 == j, delta, deltas)
                dva.append(_mm_tn(pb, doh))
                dka.append(_mm_tn(ds, qh))
                dsb.append(ds)
            gsink = gsink - _col_sum(psink_ref[blk, :] * deltas)
            dk_band.append(_unplace_kv(dka) * 0.125)
            dv_band.append(_unplace_kv(dva))
            dq_blk.append([_mm(jnp.concatenate(dsb[2 * h:2 * h + 2], axis=1),
                               jnp.concatenate([ka[2 * h][band], ka[2 * h + 1][band]], axis=0)) for h in range(2)])
        gsink_ref[...] += gsink
        dk_rows = [dk_band[b][BLOCK:] + (dk_band[b + 1][:BLOCK] if b + 1 < nb else dkcar_ref[...]) for b in range(nb)]
        dv_rows = [dv_band[b][BLOCK:] + (dv_band[b + 1][:BLOCK] if b + 1 < nb else dvcar_ref[...]) for b in range(nb)]
        dkcar_ref[...] = dk_band[0][:BLOCK]
        dvcar_ref[...] = dv_band[0][:BLOCK]
        dkg = _rope_bwd(jnp.concatenate(dk_rows, axis=0), cv, s1v, s2v)
        gkn = _col_sum(dkg * kn)
        dkn = dkg * gn_ref[:, G_K:G_XQ]
        dproj_ref[:, C_SK:C_SK + KV_W] = (krr * (dkn - kn * _seg_mean(dkn * kn, gm128))).astype(dproj_ref.dtype)
        dproj_ref[:, C_SV:C_SV + KV_W] = jnp.concatenate(dv_rows, axis=0).astype(dproj_ref.dtype)
        gqn = jnp.zeros((1, 128), F32)
        for h in range(2):
            dqg = _rope_bwd(jnp.concatenate([dq_blk[b][h] for b in range(nb)], axis=0), cv, s1v, s2v)
            gqn = gqn + _col_sum(dqg * qn_[h])
            dqn_ = dqg * gn_ref[:, G_Q:G_K]
            dproj_ref[:, C_SQ + 128 * h:C_SQ + 128 * (h + 1)] = (
                qr_[h] * (dqn_ - qn_[h] * _seg_mean(dqn_ * qn_[h], gm128))).astype(dproj_ref.dtype)
        gqn_ref[...] += gqn
        gkn_ref[...] += gkn

        u = proj_ref[:, C_LRUX:C_LRUX + LRU_W]
        xc, rg, ig, sq = (gates_ref[:, LRU_W * k:LRU_W * (k + 1)].astype(F32) for k in range(4))
        a = a_ref[...]
        sp = _softplus(-lam_ref[...])
        hext_ref[0:8, :] = jnp.where(first_tile, 0.0, yah_ref[...])
        hext_ref[8:8 + tm, :] = ya_ref[...]
        hprev = hext_ref[pl.ds(7, tm), :]
        aext_ref[0:tm, :] = a
        an_scr[...] = aext_ref[pl.ds(1, tm), :]
        dh_scr[...] = dya
        dh_scr[tm - 1:tm, :] = dh_scr[tm - 1:tm, :] + gcar_ref[0:1, :]
        row8 = lax.broadcasted_iota(jnp.int32, (8, LRU_W), 0)

        def scan_step(gi, carry):
            r0 = pl.multiple_of((tm // 8 - 1 - gi) * 8, 8)
            av = an_scr[pl.ds(r0, 8), :]
            bv = dh_scr[pl.ds(r0, 8), :]
            for d in (1, 2, 4):
                a_sh = jnp.where(row8 < 8 - d, pltpu.roll(av, 8 - d, 0), 1.0)
                b_sh = jnp.where(row8 < 8 - d, pltpu.roll(bv, 8 - d, 0), 0.0)
                bv = bv + av * b_sh
                av = av * a_sh
            gv = bv + av * carry
            g_scr[pl.ds(r0, 8), :] = gv
            return gv[0:1, :]

        g0 = lax.fori_loop(0, tm // 8, scan_step, jnp.zeros((1, LRU_W), F32), unroll=True)
        gcar_ref[0:1, :] = a[0:1, :] * g0
        gv = g_scr[...]
        da = gv * hprev
        dig = gv * sq * xc
        dxc = gv * sq * ig
        dla = da * a - gv * (ig * xc) * ((a * a) / sq)
        drg = dla * ((-LRU_C) * sp)
        glam_ref[...] += _col_sum(dla * rg)
        dpr = drg * rg * (1.0 - rg)
        dpi = dig * ig * (1.0 - ig)
        gbrg_ref[...] += _col_sum(dpr)
        gbig_ref[...] += _col_sum(dpi)
        dpre0 = jnp.concatenate([dpr[:, :256], dpi[:, :256]], axis=1).astype(_MXU)
        dpre1 = jnp.concatenate([dpr[:, 256:], dpi[:, 256:]], axis=1).astype(_MXU)
        gwg_ref[0] += _mm_tn(xc[:, :256], dpre0)
        gwg_ref[1] += _mm_tn(xc[:, 256:], dpre1)
        dxc = dxc + jnp.concatenate([_mm_nt(dpre0, wg_ref[0]), _mm_nt(dpre1, wg_ref[1])], axis=1)
        gcb_ref[...] += _col_sum(dxc)
        dxc_ext[0:tm, :] = dxc
        du = jnp.zeros((tm, LRU_W), F32)
        for k in range(CONV_K):
            later = dxc_ext[pl.ds(3 - k, tm), :]
            gcw_ref[k:k + 1, :] += _col_sum(later * u)
            du = du + cw_ref[k:k + 1, :] * later
        dxc_ext[tm:tm + 8, :] = dxc[0:8, :]
        dproj_ref[:, C_LRUX:C_LRUX + LRU_W] = du.astype(dproj_ref.dtype)

        dxn = _mm(dproj_ref[...], win_ref[...])
        rx = lax.rsqrt(_row_mean(xv * xv) + EPS)
        xh = xv * rx
        gng_ref[...] += _col_sum(dxn * xh)
        dxh = dxn * ng_ref[...]
        gx_ref[...] = dov.astype(F32) + rx * (dxh - xh * _row_mean(dxh * xh))

        @pl.when(i == nt - 1)
        def _():
            glam_ref[...] = glam_ref[...] * (LRU_C * _sigmoid(-lam_ref[...]))
            for r in (gqn_ref, gkn_ref, gxqn_ref):
                r[...] = _fold_heads(r[...])

    def rows(ncol, arr_cols_block=0):
        return pl.BlockSpec((tm, ncol), lambda i: (nt - 1 - i, arr_cols_block))

    def halo(nrow, ncol, colblk=0):
        per = tm // nrow
        return pl.BlockSpec((nrow, ncol), lambda i: (jnp.maximum((nt - 1 - i) * per - 1, 0), colblk))

    in_specs = [rows(D_MODEL), rows(D_MODEL), rows(D_IN), rows(LRU_W), rows(SWA_W), rows(XATT_W),
                rows(4 * 2 * BLOCK), rows(4 * MEM_LEN), rows(128), rows(4 * LRU_W), rows(LRU_W),
                rows(128), rows(128), rows(128),
                halo(8, LRU_W), halo(BLOCK, 2 * KV_W, C_SK // (2 * KV_W)),
                halo(BLOCK, 128), halo(BLOCK, 128), halo(BLOCK, 128),
                _const_spec((1, D_MODEL)), _const_spec((D_IN, D_MODEL), True), _const_spec((CONV_K, LRU_W)),
                _const_spec((2, 256, 512), True), _const_spec((1, LRU_W)), _const_spec((1, GAINS_W)),
                _const_spec((4 * MEM_LEN, XATT_W), True), _const_spec((4 * MEM_LEN, XATT_W), True),
                _const_spec((1, D_MODEL)), _const_spec((D_MODEL, D_MODEL), True)]
    small = [(2, 256, 512), (XATT_W, 4 * MEM_LEN), (XATT_W, 4 * MEM_LEN), (1, D_MODEL), (1, D_MODEL), (1, LRU_W), (1, LRU_W),
             (1, LRU_W), (1, LRU_W), (CONV_K, LRU_W), (1, 128), (1, 128), (1, XATT_W), (1, 128)]
    out_shape = (jax.ShapeDtypeStruct((seq, D_MODEL), F32), jax.ShapeDtypeStruct((seq, D_IN), _MXU)) + tuple(
        jax.ShapeDtypeStruct(s, F32) for s in small)
    out_specs = (rows(D_MODEL), rows(D_IN)) + tuple(_const_spec(s) for s in small)
    scratch = [pltpu.VMEM((tm + 8, LRU_W), F32), pltpu.VMEM((tm + 8, LRU_W), F32),
               pltpu.VMEM((tm, LRU_W), F32), pltpu.VMEM((tm, LRU_W), F32), pltpu.VMEM((tm, LRU_W), F32),
               pltpu.VMEM((tm + 8, LRU_W), F32),
               pltpu.VMEM((8, LRU_W), F32), pltpu.VMEM((BLOCK, KV_W), F32), pltpu.VMEM((BLOCK, KV_W), F32)]
    return pl.pallas_call(
        body, name="layer_bwd", grid=(nt,), out_shape=out_shape, in_specs=in_specs, out_specs=out_specs,
        scratch_shapes=scratch,
        compiler_params=pltpu.CompilerParams(dimension_semantics=("arbitrary",), vmem_limit_bytes=VMEM_LIMIT),
    )(x, dout, proj, ya, yb, yc, pswa, pmem, psink, gates, a_all, rc, rs1, rs2, ya, proj, rc, rs1, rs2,
      ng, win_t, cw, wg, lam, gains, km, vm, og, wout)


def _reduce_protocol(big, sm, outs, osm, r1, r1s, wire, r2, r2s, wire2, ps, own, send, recv, lsem):
    nbig = len(big)
    x, y, c = lax.axis_index("x"), lax.axis_index("y"), lax.axis_index("c")
    sibling = (x, y, 1 - c)
    near, far, diag = _partners(x, y, c)
    me, near_id, far_id, diag_id = _chip_of(x, y), _chip_of(*near), _chip_of(*far), _chip_of(*diag)

    def copy(k, src, dst, to):
        return pltpu.make_async_remote_copy(src_ref=src, dst_ref=dst, send_sem=send.at[k], recv_sem=recv.at[k],
                                            device_id=to, device_id_type=MESH)

    def sent(stage, a):
        if a == nbig:
            src, dst, to = ((sm.at[1 - c], r1s, sibling), (r1s, r2s.at[0], (*near, c)), (ps, r2s.at[1], (*far, c)),
                            (osm.at[c], osm.at[c], sibling))[stage]
            return [copy(5 * nbig + stage, src, dst, to)]
        if stage == 0:
            return [copy(5 * a, big[a].at[:, 1 - c], r1[a], sibling)]
        if stage == 1:
            return [copy(5 * a + 1, wire[a].at[near_id], r2[a].at[0], (*near, c)),
                    copy(5 * a + 2, wire[a].at[diag_id], r2[a].at[1], (*near, c))]
        if stage == 2:
            return [copy(5 * a + 3, wire2[a], r2[a].at[2], (*far, c))]
        return [copy(5 * a + 4, outs[a].at[c], outs[a].at[c], sibling)]

    arrays = range(nbig + (sm is not None))

    def start(stage, a):
        for cp in sent(stage, a):
            cp.start()

    def arrived(k, ref):
        copy(k, ref, ref, sibling).wait_recv()

    def loads():
        return [pltpu.make_async_copy(big[a].at[:, c], own[a], lsem.at[a]) for a in range(nbig)]

    def stage0():
        for a in arrays:
            start(0, a)
        for cp in loads():
            cp.start()

    def stage1():
        for a in range(nbig):
            loads()[a].wait()
            arrived(5 * a, r1[a])
            for k in range(N_CHIPS):
                r1[a][k] = own[a][k] + r1[a][k]
                wire[a][k] = r1[a][k].astype(wire[a].dtype)
            start(1, a)
        if sm is not None:
            arrived(5 * nbig, r1s)
            r1s[...] = sm[c] + r1s[...]
            start(1, nbig)

    def stage2():
        for a in range(nbig):
            arrived(5 * a + 1, r2[a].at[0])
            arrived(5 * a + 2, r2[a].at[1])
            r1[a][me] = r1[a][me] + r2[a][0].astype(F32)
            wire2[a][...] = (r1[a][far_id] + r2[a][1].astype(F32)).astype(wire2[a].dtype)
            start(2, a)
        if sm is not None:
            arrived(5 * nbig + 1, r2s.at[0])
            ps[...] = r1s[...] + r2s[0]
            start(2, nbig)

    def stage3():
        for a in range(nbig):
            arrived(5 * a + 3, r2[a].at[2])
            outs[a][c] = r1[a][me] + r2[a][2].astype(F32)
            start(3, a)
        if sm is not None:
            arrived(5 * nbig + 2, r2s.at[1])
            osm[c] = ps[...] + r2s[1]
            start(3, nbig)

    def stage4():
        for a in range(nbig):
            arrived(5 * a + 4, outs[a].at[1 - c])
        if sm is not None:
            arrived(5 * nbig + 3, osm.at[1 - c])
        for stage in range(4):
            for a in arrays:
                for cp in sent(stage, a):
                    cp.wait_send()

    return [stage0, stage1, stage2, stage3, stage4]


def _reduce_buffers(bigs, g_small):
    half = [b.shape[2:] for b in bigs]
    sm_half = None if g_small is None else g_small.shape[1:]
    out_shape = [jax.ShapeDtypeStruct((2,) + h, F32) for h in half]
    small = lambda lead: [] if g_small is None else [pltpu.VMEM(lead + sm_half, F32)]
    if g_small is not None:
        out_shape.append(jax.ShapeDtypeStruct(g_small.shape, F32))
    n_sem = 5 * len(bigs) + 4
    scratch = ([pltpu.VMEM((N_CHIPS,) + h, F32) for h in half] + small(())
               + [pltpu.VMEM((N_CHIPS,) + h, _WIRE) for h in half]
               + [pltpu.VMEM((3,) + h, _WIRE) for h in half] + small((2,))
               + [pltpu.VMEM(h, _WIRE) for h in half] + small(())
               + [pltpu.VMEM((N_CHIPS,) + h, F32) for h in half]
               + [pltpu.SemaphoreType.DMA((n_sem,)), pltpu.SemaphoreType.DMA((n_sem,)),
                  pltpu.SemaphoreType.DMA((len(bigs),))])
    return out_shape, scratch


def _split_reduce_refs(refs, nbig, has_small):
    it = iter(refs)
    take = lambda n: [next(it) for _ in range(n)]
    one = lambda: next(it) if has_small else None
    big, sm = take(nbig), one()
    outs, osm = take(nbig), one()
    r1, r1s, wire, r2, r2s, wire2, ps, own = take(nbig), one(), take(nbig), take(nbig), one(), take(nbig), one(), take(nbig)
    send, recv, lsem = take(3)
    return big, sm, outs, osm, r1, r1s, wire, r2, r2s, wire2, ps, own, send, recv, lsem


def _hosted_reduce_shapes(bigs, g_small):
    red_shape, scratch = _reduce_buffers(bigs, g_small)
    nres = len(red_shape)
    return red_shape, [pltpu.VMEM(r.shape, r.dtype) for r in red_shape] + scratch + [pltpu.SemaphoreType.DMA((nres,))]


def _hosted_reduce(step, n_steps, closing, stage_at, operands, results, scratch, has_small):
    nres = len(results)
    sums, rest, fsem = scratch[:nres], scratch[nres:-1], scratch[-1]
    refs = tuple(operands) + tuple(sums) + tuple(rest)

    def to_results():
        out = [pltpu.make_async_copy(sums[k], results[k], fsem.at[k]) for k in range(nres)]
        for cp in out:
            cp.start()
        for cp in out:
            cp.wait()

    stages = _reduce_protocol(*_split_reduce_refs(refs, nres - has_small, has_small))

    def last_stage():
        stages[-1]()
        to_results()

    for at, stage in zip(stage_at, stages[:-1] + [last_stage]):
        if closing == (at == n_steps):
            pl.when(step == min(at, n_steps - 1))(stage)


def reduce_grads(big, name, parts):
    chips, halves, rows_, cols = big.shape
    sub = jax.ShapeDtypeStruct((chips, halves, rows_ // parts, cols), big.dtype)

    def body(b_ref, o_ref, *scratch):
        refs = [b_ref.at[:, :, s] for s in range(parts)] + [o_ref.at[:, s] for s in range(parts)] + list(scratch)
        for stage in _reduce_protocol(*_split_reduce_refs(refs, parts, False)):
            stage()

    _, scratch = _reduce_buffers([sub] * parts, None)
    return pl.pallas_call(
        body, name=name, out_shape=jax.ShapeDtypeStruct((halves, parts, rows_ // parts, cols), F32),
        in_specs=[pl.BlockSpec(memory_space=pl.ANY)], out_specs=pl.BlockSpec(memory_space=pltpu.VMEM),
        scratch_shapes=scratch, compiler_params=pltpu.CompilerParams(vmem_limit_bytes=VMEM_LIMIT),
    )(big.reshape(chips, halves, parts, rows_ // parts, cols))


def adamw(items, g_pack, ws, ms, vs):
    blocks = []
    for k, (w, _, _, _) in enumerate(items):
        rows_, cols = w.shape
        tr = max(t for t in range(8, rows_ + 1, 8) if rows_ % t == 0 and t * cols * 4 <= ADAM_BLOCK_BYTES)
        blocks += [(k, r, tr) for r in range(0, rows_, tr)]
    nin, n = 4 * len(items), len(ws)

    def body(*refs):
        mats_in, small_in = refs[:nin], refs[nin:nin + 1 + 3 * n]
        n_out = nin + 4 * n + 1
        outs, scratch = refs[nin + 1 + 3 * n:nin + 1 + 3 * n + n_out], refs[nin + 1 + 3 * n + n_out:]
        buf, (lsem, ssem) = scratch[:nin], scratch[nin:]
        loads = [[pltpu.make_async_copy(mats_in[4 * k + q].at[pl.ds(r, tr)], buf[4 * k + q].at[pl.ds(r, tr)],
                                        lsem.at[4 * c + q]) for q in range(4)] for c, (k, r, tr) in enumerate(blocks)]
        for cps in loads:
            for cp in cps:
                cp.start()
        _adamw_small(small_in[0], *(small_in[1 + k * n:1 + (k + 1) * n] for k in range(3)),
                     *(outs[nin + k * n:nin + (k + 1) * n] for k in range(4)), outs[-1])
        stores = []
        for c, (k, r, tr) in enumerate(blocks):
            for cp in loads[c]:
                cp.wait()
            w_ref, g_ref, m_ref, v_ref = (buf[4 * k + q].at[pl.ds(r, tr)] for q in range(4))
            w_ref[...], m_ref[...], v_ref[...] = _adam_update(w_ref[...], g_ref[...], m_ref[...], v_ref[...])
            for q, src in enumerate((g_ref, w_ref, m_ref, v_ref)):
                stores.append(pltpu.make_async_copy(src, outs[4 * k + q].at[pl.ds(r, tr)], ssem.at[4 * c + q]))
                stores[-1].start()
        for cp in stores:
            cp.wait()

    shapes = [jax.ShapeDtypeStruct(w.shape, F32) for w, _, _, _ in items for _ in range(4)]
    vm = pl.BlockSpec(memory_space=pltpu.VMEM)
    hbm = pl.BlockSpec(memory_space=pl.ANY)
    like = [jax.ShapeDtypeStruct(w.shape, F32) for w in ws]
    res = pl.pallas_call(
        body, name="adamw", out_shape=(*shapes, *like * 4, jax.ShapeDtypeStruct((1, 1), F32)),
        in_specs=[hbm] * nin + [vm] * (1 + 3 * n), out_specs=(*[hbm] * nin, *[vm] * (4 * n + 1)),
        scratch_shapes=[pltpu.VMEM(s.shape, F32) for s in shapes] + [pltpu.SemaphoreType.DMA((4 * len(blocks),))] * 2,
        compiler_params=pltpu.CompilerParams(vmem_limit_bytes=VMEM_LIMIT),
    )(*[a for item in items for a in item], g_pack, *ws, *ms, *vs)
    return [res[4 * k:4 * k + 4] for k in range(len(items))], res[nin:]


def _adam_update(w, g, m, v):
    nm = ADAM_B1 * m + (1.0 - ADAM_B1) * g
    nv = ADAM_B2 * v + (1.0 - ADAM_B2) * (g * g)
    m_hat = nm / (1.0 - ADAM_B1 ** ADAM_STEP)
    v_hat = nv / (1.0 - ADAM_B2 ** ADAM_STEP)
    return (-ADAM_LR) * (m_hat / (jnp.sqrt(v_hat) + ADAM_EPS) + ADAM_WD * w), nm, nv


def _adamw_small(pk, w_refs, m_refs, v_refs, g_out, d_out, nm_out, nv_out, loss_ref):
    nvec = len(SMALL_VECTORS)
    loss_ref[...] = pk[LOSS_ROW:LOSS_ROW + 1, 0:1]
    chip = 2 * lax.axis_index("x") + lax.axis_index("y")
    for k, (name, row, width) in enumerate(SMALL_VECTORS):
        if name == "conv_w":
            g = jnp.concatenate([pk[pl.ds(row + 4 * t + chip, 1), :] for t in range(CONV_K)], axis=0)[None]
        elif width >= 128:
            g = jnp.concatenate([pk[row + r:row + r + 1, :] for r in range(width // 128)], axis=1)
        else:
            g = pk[row:row + 1, 0:width]
        g_out[k][...] = g
        d_out[k][...], nm_out[k][...], nv_out[k][...] = _adam_update(w_refs[k][...], g, m_refs[k][...], v_refs[k][...])
    for k in range(nvec, nvec + len(SMALL_MATRICES)):
        for b in range(LRU_BLOCKS):
            rows_ = pk[GATES_ROW + HEAD * b:GATES_ROW + HEAD * (b + 1), :]
            g = (pltpu.roll(rows_, HEAD, axis=1) if k > nvec else rows_)[:, 0:HEAD]
            g_out[k][0, b] = g
            d_out[k][0, b], nm_out[k][0, b], nv_out[k][0, b] = _adam_update(
                w_refs[k][0, b], g, m_refs[k][0, b], v_refs[k][0, b])


SMALL_VECTORS = (("norm_g", 512, 1024), ("mem_norm_g", 520, 1024), ("conv_w", 528, 512), ("conv_b", 544, 512),
                 ("b_rg", 548, 512), ("b_ig", 552, 512), ("lru_lambda", 556, 512), ("q_norm_g", 560, 64),
                 ("k_norm_g", 561, 64), ("sinks", 562, 4), ("xq_norm_g", 563, 64), ("xk_norm_g", 564, 64),
                 ("out_norm_g", 565, 1024))
LOSS_ROW = 573
SMALL_MATRICES = ("w_rg", "w_ig")
GATES_ROW = 0
SMALL_ROWS = 640


def _rope_tables(seq):
    pos = np.arange(seq, dtype=np.float32)
    inv_freq = (np.float32(ROPE_THETA) ** (-(np.arange(0, ROPE_DIM, 2, dtype=np.float32) / np.float32(ROPE_DIM)))
                ).astype(np.float32)
    ang = (pos[:, None] * inv_freq[None, :]).astype(np.float32)
    cos, sin = np.cos(ang).astype(np.float32), np.sin(ang).astype(np.float32)
    z = lambda n: np.zeros((seq, n), np.float32)
    c64 = np.concatenate([cos, cos, np.ones((seq, HEAD - ROPE_DIM), np.float32)], axis=1)
    s1_64 = np.concatenate([-sin, z(HEAD - 8)], axis=1)
    s2_64 = np.concatenate([z(8), sin, z(HEAD - ROPE_DIM)], axis=1)
    return tuple(jnp.asarray(np.concatenate([t, t], axis=1)) for t in (c64, s1_64, s2_64))


def kernel(x, mem, norm_g, mem_norm_g, w_in, conv_w, conv_b, w_rg, b_rg, w_ig, b_ig, lru_lambda, q_norm_g, k_norm_g, sinks, w_mem_kv, xq_norm_g, xk_norm_g, out_norm_g, w_out, loss_target, m_norm_g, m_mem_norm_g, m_w_in, m_conv_w, m_conv_b, m_w_rg, m_b_rg, m_w_ig, m_b_ig, m_lru_lambda, m_q_norm_g, m_k_norm_g, m_sinks, m_w_mem_kv, m_xq_norm_g, m_xk_norm_g, m_out_norm_g, m_w_out, v_norm_g, v_mem_norm_g, v_w_in, v_conv_w, v_conv_b, v_w_rg, v_b_rg, v_w_ig, v_b_ig, v_lru_lambda, v_q_norm_g, v_k_norm_g, v_sinks, v_w_mem_kv, v_xq_norm_g, v_xk_norm_g, v_out_norm_g, v_w_out):
    seq = x.shape[1]
    xs, tgt, mems = x[0], loss_target[0], mem[0]

    win_t, wout, wkv, cw, wg, gains, km, vm = gather_weights(
        w_in[0].T, w_out[0], w_mem_kv[0], conv_w, w_rg, w_ig, (q_norm_g, k_norm_g, xq_norm_g, xk_norm_g), mems,
        mem_norm_g)
    rc, rs1, rs2 = _rope_tables(seq)
    proj, ya, yb, yc, ycat, xn, dout, pswa, pmem, psink, gates, a_all, loss8 = layer_fwd(
        xs, tgt, rc, rs1, rs2, norm_g, win_t, cw, conv_b, wg, b_rg, b_ig, lru_lambda, gains, sinks, km, vm,
        out_norm_g, wout)
    (gx, dproj, g_wg, dkm, dvm, g_ng, g_og, g_cb, g_brg, g_big, g_lam, g_cw, g_qn, g_kn, g_xqn, g_sink) = layer_bwd(
        xs, dout, proj, ya, yb, yc, pswa, pmem, psink, gates, a_all, rc, rs1, rs2, norm_g, win_t, cw, wg, lru_lambda,
        gains, km, vm, out_norm_g, wout)
    g_win_t, r_out, r_kv, r_small = weight_grads(
        ycat, dout, dproj, xn, (mems, mem_norm_g, wkv, gains, dkm, dvm, g_wg, loss8), dict(
            norm_g=g_ng, conv_w=g_cw, conv_b=g_cb, b_rg=g_brg, b_ig=g_big, lru_lambda=g_lam, q_norm_g=g_qn,
            k_norm_g=g_kn, sinks=g_sink, xq_norm_g=g_xqn, out_norm_g=g_og), (0, 1, 4, 7, 8), (4, 5, 9, 11, 13))
    r_in = reduce_grads(g_win_t.reshape(N_CHIPS, 2, D_IN // 8, D_MODEL), "reduce_w_in", 6)

    r_small = r_small.reshape(SMALL_ROWS, 128)
    grads = {}
    weights = dict(norm_g=norm_g, mem_norm_g=mem_norm_g, w_in=w_in, conv_w=conv_w, conv_b=conv_b, w_rg=w_rg, b_rg=b_rg,
                   w_ig=w_ig, b_ig=b_ig, lru_lambda=lru_lambda, q_norm_g=q_norm_g, k_norm_g=k_norm_g, sinks=sinks,
                   w_mem_kv=w_mem_kv, xq_norm_g=xq_norm_g, xk_norm_g=xk_norm_g, out_norm_g=out_norm_g, w_out=w_out)
    ms = dict(norm_g=m_norm_g, mem_norm_g=m_mem_norm_g, w_in=m_w_in, conv_w=m_conv_w, conv_b=m_conv_b, w_rg=m_w_rg,
              b_rg=m_b_rg, w_ig=m_w_ig, b_ig=m_b_ig, lru_lambda=m_lru_lambda, q_norm_g=m_q_norm_g, k_norm_g=m_k_norm_g,
              sinks=m_sinks, w_mem_kv=m_w_mem_kv, xq_norm_g=m_xq_norm_g, xk_norm_g=m_xk_norm_g,
              out_norm_g=m_out_norm_g, w_out=m_w_out)
    vs = dict(norm_g=v_norm_g, mem_norm_g=v_mem_norm_g, w_in=v_w_in, conv_w=v_conv_w, conv_b=v_conv_b, w_rg=v_w_rg,
              b_rg=v_b_rg, w_ig=v_w_ig, b_ig=v_b_ig, lru_lambda=v_lru_lambda, q_norm_g=v_q_norm_g, k_norm_g=v_k_norm_g,
              sinks=v_sinks, w_mem_kv=v_w_mem_kv, xq_norm_g=v_xq_norm_g, xk_norm_g=v_xk_norm_g,
              out_norm_g=v_out_norm_g, w_out=v_w_out)

    delta, new_m, new_v = {}, {}, {}
    small_names = [n for n, _, _ in SMALL_VECTORS] + list(SMALL_MATRICES)
    (res_in, res_out, res_kv), res = adamw(
        [(w_in[0].T, r_in.reshape(D_IN // 4, D_MODEL), m_w_in[0].T, v_w_in[0].T),
         (w_out[0], r_out.reshape(D_MODEL // 4, D_MODEL), m_w_out[0], v_w_out[0]),
         (w_mem_kv[0], r_kv.reshape(D_MODEL // 4, 2 * XATT_W), m_w_mem_kv[0], v_w_mem_kv[0])],
        r_small, [weights[n] for n in small_names], [ms[n] for n in small_names], [vs[n] for n in small_names])
    grads["w_in"], delta["w_in"], new_m["w_in"], new_v["w_in"] = (r.T[None] for r in res_in)
    grads["w_out"], delta["w_out"], new_m["w_out"], new_v["w_out"] = (r[None] for r in res_out)
    grads["w_mem_kv"], delta["w_mem_kv"], new_m["w_mem_kv"], new_v["w_mem_kv"] = (r[None] for r in res_kv)
    nall = len(small_names)
    for k, into in enumerate((grads, delta, new_m, new_v)):
        into.update(zip(small_names, res[k * nall:(k + 1) * nall]))
    loss = res[-1].reshape(())

    order = ("norm_g", "mem_norm_g", "w_in", "conv_w", "conv_b", "w_rg", "b_rg", "w_ig", "b_ig", "lru_lambda",
             "q_norm_g", "k_norm_g", "sinks", "w_mem_kv", "xq_norm_g", "xk_norm_g", "out_norm_g", "w_out")
    return (loss, gx[None], *[grads[n] for n in order], *[delta[n] for n in order], *[new_m[n] for n in order],
            *[new_v[n] for n in order])
```

```python
import jax
import jax.numpy as jnp
import numpy as np
from jax import lax
from jax.experimental import pallas as pl
from jax.experimental.pallas import tpu as pltpu

F32 = jnp.float32
_MXU = jnp.bfloat16
_WIRE = jnp.bfloat16

D_MODEL = 1024
MEM_LEN = 256
HEAD = 64
LRU_W = 512
LRU_BLOCKS = 8
CONV_K = 4
LRU_C = 8.0
SWA_W = 256
KV_W = 128
XATT_W = 256
BLOCK = 128
D_IN = 2304
ROPE_THETA = 500000.0
ROPE_DIM = 16
EPS = 1e-6
NEG_INF = -1e30
C_LRUX, C_LRUG, C_SQ, C_SK, C_SV, C_SWAG, C_XQ, C_XG = 0, 512, 1024, 1280, 1408, 1536, 1792, 2048
G_Q, G_K, G_XQ, G_XK, GAINS_W = 0, 128, 256, 512, 768

ADAM_LR, ADAM_B1, ADAM_B2, ADAM_EPS, ADAM_WD, ADAM_STEP = 0.001, 0.9, 0.999, 1e-08, 0.01, 10

N_CHIPS = 4
ROW_TILE = 256
VMEM_LIMIT = 56 * 1024 * 1024
ADAM_BLOCK_BYTES = 1280 * 1024
MESH = pl.DeviceIdType.MESH


def _mm(a, b):
    return jnp.dot(a.astype(_MXU), b.astype(_MXU), preferred_element_type=F32)


def _mm_nt(a, b):
    return lax.dot_general(a.astype(_MXU), b.astype(_MXU), (((1,), (1,)), ((), ())), preferred_element_type=F32)


def _mm_tn(a, b):
    return lax.dot_general(a.astype(_MXU), b.astype(_MXU), (((0,), (0,)), ((), ())), preferred_element_type=F32)


def _group_matrix(width):
    r = lax.shift_right_logical(lax.broadcasted_iota(jnp.int32, (width, width), 0), 6)
    c = lax.shift_right_logical(lax.broadcasted_iota(jnp.int32, (width, width), 1), 6)
    return (r == c).astype(_MXU)


def _seg_mean(x, gm):
    return jnp.dot(x.astype(_MXU), gm, preferred_element_type=F32) * (1.0 / HEAD)


def _row_mean(x):
    return jnp.mean(x, axis=-1, keepdims=True)


def _col_sum(x):
    return jnp.sum(x, axis=0, keepdims=True)


def _sigmoid(x):
    return jax.nn.sigmoid(x)


def _softplus(z):
    e = jnp.exp(-jnp.abs(z))
    u = 1.0 + e
    log1p_e = jnp.where(u == 1.0, e, jnp.log(u) * (e / (u - 1.0)))
    return jnp.maximum(z, 0.0) + log1p_e


def _rope(t, c, s1, s2):
    return t * c + pltpu.roll(t, 120, 1) * s1 + pltpu.roll(t, 8, 1) * s2


def _rope_bwd(d, c, s1, s2):
    return d * c + pltpu.roll(d * s1, 8, 1) + pltpu.roll(d * s2, 120, 1)


def _fold_heads(v):
    out = v
    for k in range(1, v.shape[1] // HEAD):
        out = out + pltpu.roll(v, HEAD * k, 1)
    return out


def _lane_mask(width, lo, hi):
    lane = lax.broadcasted_iota(jnp.int32, (1, width), 1)
    return ((lane >= lo) & (lane < hi)).astype(F32)


def _swa_mask(first_block):
    qi = lax.broadcasted_iota(jnp.int32, (BLOCK, 2 * BLOCK), 0)
    kj = lax.broadcasted_iota(jnp.int32, (BLOCK, 2 * BLOCK), 1)
    rel = qi + BLOCK - kj
    ok = (rel >= 0) & (rel < BLOCK)
    return ok & (jnp.logical_not(first_block) | (kj >= BLOCK))


def _place_kv(t, scale):
    lo = t * (_lane_mask(KV_W, 0, HEAD) * scale)
    hi = t * (_lane_mask(KV_W, HEAD, KV_W) * scale)
    return [a.astype(_MXU) for a in (lo, pltpu.roll(lo, HEAD, 1), pltpu.roll(hi, HEAD, 1), hi)]


def _unplace_kv(d):
    return (_lane_mask(KV_W, 0, HEAD) * (d[0] + pltpu.roll(d[1], HEAD, 1))
            + _lane_mask(KV_W, HEAD, KV_W) * (d[3] + pltpu.roll(d[2], HEAD, 1)))


def _swa_probs(qh, ka, mask, sink):
    s = _mm_nt(qh, ka)
    s = jnp.where(mask, s, NEG_INF)
    m = jnp.maximum(jnp.max(s, axis=-1, keepdims=True), sink)
    p = jnp.exp(s - m)
    esink = jnp.exp(sink - m)
    inv = 1.0 / (jnp.sum(p, axis=-1, keepdims=True) + esink)
    return p * inv, esink * inv


def _mem_probs(s_all):
    out = []
    for j in range(4):
        s = s_all[:, MEM_LEN * j:MEM_LEN * (j + 1)]
        p = jnp.exp(s - jnp.max(s, axis=-1, keepdims=True))
        out.append(p * (1.0 / jnp.sum(p, axis=-1, keepdims=True)))
    return out


def _head_rows(t, scale):
    return jnp.concatenate([t * (_lane_mask(XATT_W, HEAD * j, HEAD * (j + 1)) * scale) for j in range(4)], axis=0)


def _lru_gates(xc, wg_ref, brg, big, lam):
    p0 = _mm(xc[:, :256], wg_ref[0])
    p1 = _mm(xc[:, 256:], wg_ref[1])
    rg = _sigmoid(jnp.concatenate([p0[:, :256], p1[:, :256]], axis=1) + brg)
    ig = _sigmoid(jnp.concatenate([p0[:, 256:], p1[:, 256:]], axis=1) + big)
    sp = _softplus(-lam)
    la = (-LRU_C) * rg * sp
    a = jnp.exp(la)
    th = jnp.tanh(la)
    one_minus_a2 = (-2.0 * th) / (1.0 - th)
    return rg, ig, sp, a, jnp.sqrt(one_minus_a2)


def _const_spec(shape, single=False):
    zeros = (0,) * len(shape)
    if single:
        return pl.BlockSpec(shape, lambda i: zeros, pipeline_mode=pl.Buffered(1))
    return pl.BlockSpec(shape, lambda i: zeros)


def _chip_of(x, y):
    return 2 * x + y


def _partners(x, y, c):
    north = c == 1
    near = (jnp.where(north, 1 - x, x), jnp.where(north, y, 1 - y))
    far = (jnp.where(north, x, 1 - x), jnp.where(north, 1 - y, y))
    return near, far, (1 - x, 1 - y)


def gather_weights(win_t, wout, wkv, conv_w, w_rg, w_ig, head_gains, mem, mem_g):
    arrs = (win_t, wout, wkv)
    n = len(arrs)
    pieces = [(a, 0, arr.shape[0] // 2) for a, arr in enumerate(arrs)]
    npc = len(pieces)

    def body(a0, a1, a2, cw_in, wrg_ref, wig_ref, q_ref, k_ref, xq_ref, xk_ref, mem_ref, mg_ref,
             o0, o1, o2, cw_out, wg_ref, gn_ref, km_ref, vm_ref, s0, s1, s2, cw, ocw, send, recv, lsem):
        ins, outs = (s0, s1, s2), (o0, o1, o2)
        for src, dst in zip((a0, a1, a2), ins):
            dst[...] = src[...].astype(dst.dtype)
        cw[...] = jnp.zeros(cw.shape, F32)
        cw[0:CONV_K, :] = cw_in[0]
        x, y, c = lax.axis_index("x"), lax.axis_index("y"), lax.axis_index("c")
        sibling = (x, y, 1 - c)
        near, far, diag = _partners(x, y, c)
        chips = [near, far, diag]
        me = _chip_of(x, y)

        def landed(p, chip, half):
            a, off, rows_ = pieces[p]
            r = ins[a].shape[0]
            return outs[a].at[pl.ds(pl.multiple_of(chip * r + half * (r // 2) + off, 16), rows_)]

        def mine(p):
            a, off, rows_ = pieces[p]
            return ins[a].at[pl.ds(pl.multiple_of(c * (ins[a].shape[0] // 2) + off, 16), rows_)]

        def copy(k, src, dst, to):
            return pltpu.make_async_remote_copy(src_ref=src, dst_ref=dst, send_sem=send.at[k], recv_sem=recv.at[k],
                                                device_id=to, device_id_type=MESH)

        def cw_rows(chip):
            return ocw.at[pl.ds(pl.multiple_of(chip * 8, 8), 8)]

        locals_ = []
        for a in range(n):
            r = ins[a].shape[0]
            locals_.append(pltpu.make_async_copy(ins[a], outs[a].at[pl.ds(pl.multiple_of(me * r, 16), r)], lsem.at[a]))
        locals_.append(pltpu.make_async_copy(cw, cw_rows(me), lsem.at[n]))
        for cp in locals_:
            cp.start()

        sent = []
        for p in range(npc):
            for j in range(2):
                sent.append(copy(p * 6 + j, mine(p), landed(p, me, c), (*chips[j], c)))
        for j, chip in enumerate(chips):
            sent.append(copy(npc * 6 + j, cw, cw_rows(me), (*chip, c)))
        for cp in sent:
            cp.start()

        gn_ref[...] = jnp.concatenate([q_ref[...]] * 2 + [k_ref[...]] * 2 + [xq_ref[...]] * 4 + [xk_ref[...]] * 4,
                                      axis=1)
        zeros = lambda lanes: [jnp.zeros((HEAD, lanes), F32)] if lanes else []
        for h in range(2):
            for b in range(4):
                row = []
                for w_ref in (wrg_ref, wig_ref):
                    row += zeros(HEAD * b) + [w_ref[0, 4 * h + b]] + zeros(HEAD * (3 - b))
                wg_ref[h, HEAD * b:HEAD * (b + 1), :] = jnp.concatenate(row, axis=1).astype(wg_ref.dtype)

        for j in range(3):
            for p in range(npc):
                got = landed(p, _chip_of(*chips[j]), c)
                copy(p * 6 + j, got, got, sibling).wait_recv()
                if j == 0:
                    sent.append(copy(p * 6 + 2, got, got, (*far, c)))
                    sent[-1].start()
                sent.append(copy(p * 6 + 3 + j, got, got, sibling))
                sent[-1].start()
        for p in range(npc):
            for j in range(3):
                got = landed(p, _chip_of(*chips[(1, 0, 2)[j]]), 1 - c)
                copy(p * 6 + 3 + j, got, got, sibling).wait_recv()
        for j, chip in enumerate(chips):
            got = cw_rows(_chip_of(*chip))
            copy(npc * 6 + j, got, got, (*chip, c)).wait_recv()
        for cp in sent:
            cp.wait_send()
        for cp in locals_:
            cp.wait()
        for chip in range(N_CHIPS):
            cw_out[:, 128 * chip:128 * (chip + 1)] = ocw[8 * chip:8 * chip + CONV_K, :]

        mem_v = mem_ref[...]
        mn = mem_v * lax.rsqrt(_row_mean(mem_v * mem_v) + EPS) * mg_ref[...]
        mkv = _mm(mn, o2[...])
        kpre = mkv[:, :XATT_W]
        km = kpre * lax.rsqrt(_seg_mean(kpre * kpre, _group_matrix(XATT_W)) + EPS) * gn_ref[:, G_XK:GAINS_W]
        km_ref[...] = _head_rows(km, 0.125).astype(km_ref.dtype)
        vm_ref[...] = _head_rows(mkv[:, XATT_W:], 1.0).astype(vm_ref.dtype)

    vm = pl.BlockSpec(memory_space=pltpu.VMEM)
    hbm = pl.BlockSpec(memory_space=pl.ANY)
    head_rows = jax.ShapeDtypeStruct((4 * MEM_LEN, XATT_W), _MXU)
    out_shape = tuple(jax.ShapeDtypeStruct((N_CHIPS * a.shape[0],) + a.shape[1:], _MXU) for a in arrs) + (
        jax.ShapeDtypeStruct((CONV_K, LRU_W), F32), jax.ShapeDtypeStruct((2, 256, 512), _MXU),
        jax.ShapeDtypeStruct((1, GAINS_W), F32), head_rows, head_rows)
    n_rdma = npc * 6 + 3
    return pl.pallas_call(
        body, name="gather_weights", out_shape=out_shape,
        in_specs=[vm] * 12, out_specs=(hbm, hbm, vm, vm, vm, vm, vm, vm),
        scratch_shapes=[pltpu.VMEM(a.shape, _MXU) for a in arrs] + [
            pltpu.VMEM((8, 128), F32), pltpu.VMEM((N_CHIPS * 8, 128), F32),
            pltpu.SemaphoreType.DMA((n_rdma,)), pltpu.SemaphoreType.DMA((n_rdma,)), pltpu.SemaphoreType.DMA((n + 1,))],
        compiler_params=pltpu.CompilerParams(vmem_limit_bytes=VMEM_LIMIT),
    )(win_t, wout, wkv, conv_w, w_rg, w_ig, *head_gains, mem, mem_g)


def _mem_bwd_and_pack(mem_ref, g_ref, w_ref, gn_ref, dkm_ref, dvm_ref, gg_ref, loss_ref, vectors, gw_ref, pk_ref):
    first_row = {name: (row, width) for name, row, width in SMALL_VECTORS}
    half_rows = SMALL_ROWS // 2
    pk_ref[...] = jnp.zeros(pk_ref.shape, F32)

    def rows_at(at, n):
        assert at // half_rows == (at + n - 1) // half_rows
        return at // half_rows, slice(at % half_rows, at % half_rows + n), slice(None)

    def put(name, src):
        row, width = first_row[name]
        per_row = 1 if width < 128 else src.shape[1] // 128
        for t in range(src.shape[0]):
            for r in range(per_row):
                pk_ref[rows_at(row + per_row * t + r, 1)] = src[t:t + 1, 128 * r:128 * (r + 1)]

    for name, ref in vectors.items():
        put(name, ref)
    pk_ref[rows_at(LOSS_ROW, 1)] = loss_ref[0:1, :]
    upper = lax.broadcasted_iota(jnp.int32, (HEAD, 128), 1) >= HEAD
    for h in range(2):
        for b in range(4):
            rg = gg_ref[h, HEAD * b:HEAD * (b + 1), 128 * (b // 2):128 * (b // 2 + 1)]
            ig = gg_ref[h, HEAD * b:HEAD * (b + 1), 256 + 128 * (b // 2):256 + 128 * (b // 2 + 1)]
            if b % 2:
                rg = pltpu.roll(rg, HEAD, axis=1)
            else:
                ig = pltpu.roll(ig, HEAD, axis=1)
            pk_ref[rows_at(GATES_ROW + HEAD * (4 * h + b), HEAD)] = jnp.where(upper, ig, rg)

    mem_v = mem_ref[...]
    mh = mem_v * lax.rsqrt(_row_mean(mem_v * mem_v) + EPS)
    mn = mh * g_ref[...]
    mkv = _mm(mn, w_ref[...])
    kpre = mkv[:, :XATT_W]
    gm = _group_matrix(XATT_W)
    rk = lax.rsqrt(_seg_mean(kpre * kpre, gm) + EPS)
    kn = kpre * rk
    dk = jnp.zeros((MEM_LEN, XATT_W), F32)
    dv = jnp.zeros((MEM_LEN, XATT_W), F32)
    for j in range(4):
        mj = _lane_mask(XATT_W, HEAD * j, HEAD * (j + 1))
        dk = dk + dkm_ref[:, MEM_LEN * j:MEM_LEN * (j + 1)].T * (mj * 0.125)
        dv = dv + dvm_ref[:, MEM_LEN * j:MEM_LEN * (j + 1)].T * mj
    put("xk_norm_g", _fold_heads(_col_sum(dk * kn)))
    dkn = dk * gn_ref[:, G_XK:GAINS_W]
    dkpre = rk * (dkn - kn * _seg_mean(dkn * kn, gm))
    dmkv = jnp.concatenate([dkpre, dv], axis=1)
    gw_ref[...] = _mm_tn(mn, dmkv).reshape(gw_ref.shape)
    dmn = _mm_nt(dmkv, w_ref[...])
    put("mem_norm_g", _col_sum(dmn * mh))


def layer_fwd(x, tgt, rc, rs1, rs2, ng, win_t, cw, cb, wg, brg, big, lam, gains, sinks, km, vm, og, wout):
    seq = x.shape[0]
    tm = min(ROW_TILE, seq)
    nt = seq // tm
    nb = tm // BLOCK

    def body(x_ref, t_ref, c_ref, s1_ref, s2_ref, ng_ref, win_ref, cw_ref, cb_ref, wg_ref, brg_ref, big_ref, lam_ref,
             gn_ref, sink_ref, km_ref, vm_ref, og_ref, wout_ref,
             proj_ref, ya_ref, yb_ref, yc_ref, ycat_ref, xn_ref, dout_ref, pswa_ref, pmem_ref, psink_ref, gates_ref,
             a_ref, loss_ref,
             ext_ref, b_scr, hc_ref, kp_ref, vp_ref, lacc_ref):
        i = pl.program_id(0)

        @pl.when(i == 0)
        def _():
            ext_ref[0:8, :] = jnp.zeros((8, LRU_W), F32)
            hc_ref[...] = jnp.zeros_like(hc_ref)
            kp_ref[...] = jnp.zeros_like(kp_ref)
            vp_ref[...] = jnp.zeros_like(vp_ref)
            lacc_ref[...] = jnp.zeros_like(lacc_ref)

        xv = x_ref[...]
        xn = (xv * lax.rsqrt(_row_mean(xv * xv) + EPS) * ng_ref[...]).astype(_MXU)
        xn_ref[...] = xn.astype(xn_ref.dtype)
        proj_ref[...] = _mm_nt(xn, win_ref[...])

        u = proj_ref[:, C_LRUX:C_LRUX + LRU_W]
        ext_ref[8:8 + tm, :] = u
        xc = cb_ref[...]
        for k in range(CONV_K):
            xc = xc + cw_ref[k:k + 1, :] * ext_ref[pl.ds(5 + k, tm), :]
        ext_ref[0:8, :] = u[tm - 8:tm, :]
        rg, ig, sp, a, sq = _lru_gates(xc, wg_ref, brg_ref[...], big_ref[...], lam_ref[...])
        for k, t in enumerate((xc, rg, ig, sq)):
            gates_ref[:, LRU_W * k:LRU_W * (k + 1)] = t.astype(gates_ref.dtype)
        a_ref[...] = a
        b_scr[...] = sq * (ig * xc)
        row8 = lax.broadcasted_iota(jnp.int32, (8, LRU_W), 0)

        def scan_step(g, carry):
            r0 = pl.multiple_of(g * 8, 8)
            av = a_ref[pl.ds(r0, 8), :]
            bv = b_scr[pl.ds(r0, 8), :]
            for d in (1, 2, 4):
                a_sh = jnp.where(row8 >= d, pltpu.roll(av, d, 0), 1.0)
                b_sh = jnp.where(row8 >= d, pltpu.roll(bv, d, 0), 0.0)
                bv = bv + av * b_sh
                av = av * a_sh
            hv = bv + av * carry
            ya_ref[pl.ds(r0, 8), :] = hv
            return hv[7:8, :]

        hc_ref[0:1, :] = lax.fori_loop(0, tm // 8, scan_step, hc_ref[0:1, :], unroll=True)

        gm128 = _group_matrix(KV_W)
        cv, s1v, s2v = c_ref[...], s1_ref[...], s2_ref[...]

        def head_norm_rope(t, g):
            n = t * lax.rsqrt(_seg_mean(t * t, gm128) + EPS)
            return _rope(n * g, cv, s1v, s2v)

        qs_ = (head_norm_rope(proj_ref[:, C_SQ:C_SQ + 128], gn_ref[:, G_Q:G_K]).astype(_MXU),
               head_norm_rope(proj_ref[:, C_SQ + 128:C_SQ + 256], gn_ref[:, G_Q:G_K]).astype(_MXU))
        kr = head_norm_rope(proj_ref[:, C_SK:C_SK + KV_W], gn_ref[:, G_K:G_XQ])
        sv = proj_ref[:, C_SV:C_SV + KV_W]
        ka = _place_kv(jnp.concatenate([kp_ref[...], kr], axis=0), 0.125)
        va = _place_kv(jnp.concatenate([vp_ref[...], sv], axis=0), 1.0)
        kp_ref[...] = kr[tm - BLOCK:tm, :]
        vp_ref[...] = sv[tm - BLOCK:tm, :]
        lane128 = lax.broadcasted_iota(jnp.int32, (1, 128), 1)
        for b in range(nb):
            mask = _swa_mask((i == 0) & (b == 0)) if b == 0 else _swa_mask(False)
            band = slice(BLOCK * b, BLOCK * b + 2 * BLOCK)
            blk = slice(BLOCK * b, BLOCK * (b + 1))
            psink = jnp.zeros((BLOCK, 128), F32)
            for j in range(4):
                p, pk = _swa_probs(qs_[j // 2][blk], ka[j][band], mask, sink_ref[0, j])
                pswa_ref[blk, 2 * BLOCK * j:2 * BLOCK * (j + 1)] = p.astype(pswa_ref.dtype)
                psink = jnp.where(lane128 == j, pk, psink)
            psink_ref[blk, :] = psink
            for h in range(2):
                yb_ref[blk, KV_W * h:KV_W * (h + 1)] = _mm(
                    pswa_ref[blk, 4 * BLOCK * h:4 * BLOCK * (h + 1)],
                    jnp.concatenate([va[2 * h][band], va[2 * h + 1][band]], axis=0))

        gm256 = _group_matrix(XATT_W)
        xq = proj_ref[:, C_XQ:C_XQ + XATT_W]
        qx = xq * lax.rsqrt(_seg_mean(xq * xq, gm256) + EPS) * gn_ref[:, G_XQ:G_XK]
        pm = _mem_probs(_mm_nt(qx, km_ref[...]))
        for j in range(4):
            pmem_ref[:, MEM_LEN * j:MEM_LEN * (j + 1)] = pm[j].astype(pmem_ref.dtype)
        yc = _mm(pmem_ref[...], vm_ref[...])
        yc_ref[...] = yc

        def gated(y, g, gate):
            return y * lax.rsqrt(_row_mean(y * y) + EPS) * g * (gate * _sigmoid(gate))

        ogv = og_ref[...]
        za = gated(ya_ref[...], ogv[:, :512], proj_ref[:, C_LRUG:C_LRUG + LRU_W])
        zb = gated(yb_ref[...], ogv[:, 512:768], proj_ref[:, C_SWAG:C_SWAG + SWA_W])
        zc = gated(yc, ogv[:, 768:], proj_ref[:, C_XG:C_XG + XATT_W])
        ycat_ref[:, 0:512] = za.astype(ycat_ref.dtype)
        ycat_ref[:, 512:768] = zb.astype(ycat_ref.dtype)
        ycat_ref[:, 768:1024] = zc.astype(ycat_ref.dtype)
        out = xv + _mm(ycat_ref[...], wout_ref[...])
        err = out - t_ref[...]
        dout_ref[...] = (err * (1.0 / D_MODEL)).astype(dout_ref.dtype)
        lacc_ref[...] = lacc_ref[...] + (0.5 / D_MODEL) * jnp.sum(err * err)

        @pl.when(i == nt - 1)
        def _():
            loss_ref[...] = lacc_ref[...]

    def rows(ncol):
        return pl.BlockSpec((tm, ncol), lambda i: (i, 0))

    in_specs = [rows(D_MODEL), rows(D_MODEL), rows(128), rows(128), rows(128),
                _const_spec((1, D_MODEL)), _const_spec((D_IN, D_MODEL), True), _const_spec((CONV_K, LRU_W)),
                _const_spec((1, LRU_W)), _const_spec((2, 256, 512), True), _const_spec((1, LRU_W)),
                _const_spec((1, LRU_W)), _const_spec((1, LRU_W)), _const_spec((1, GAINS_W)), pl.BlockSpec(memory_space=pltpu.SMEM),
                _const_spec((4 * MEM_LEN, XATT_W), True), _const_spec((4 * MEM_LEN, XATT_W), True),
                _const_spec((1, D_MODEL)), _const_spec((D_MODEL, D_MODEL), True)]
    out_shape = (jax.ShapeDtypeStruct((seq, D_IN), F32), jax.ShapeDtypeStruct((seq, LRU_W), F32),
                 jax.ShapeDtypeStruct((seq, SWA_W), F32), jax.ShapeDtypeStruct((seq, XATT_W), F32),
                 jax.ShapeDtypeStruct((seq, D_MODEL), _MXU), jax.ShapeDtypeStruct((seq, D_MODEL), _MXU),
                 jax.ShapeDtypeStruct((seq, D_MODEL), _MXU), jax.ShapeDtypeStruct((seq, 4 * 2 * BLOCK), _MXU),
                 jax.ShapeDtypeStruct((seq, 4 * MEM_LEN), _MXU), jax.ShapeDtypeStruct((seq, 128), F32),
                 jax.ShapeDtypeStruct((seq, 4 * LRU_W), _MXU), jax.ShapeDtypeStruct((seq, LRU_W), F32),
                 jax.ShapeDtypeStruct((8, 128), F32))
    out_specs = (rows(D_IN), rows(LRU_W), rows(SWA_W), rows(XATT_W), rows(D_MODEL), rows(D_MODEL), rows(D_MODEL),
                 rows(4 * 2 * BLOCK), rows(4 * MEM_LEN), rows(128), rows(4 * LRU_W), rows(LRU_W),
                 _const_spec((8, 128)))
    scratch = [pltpu.VMEM((tm + 8, LRU_W), F32), pltpu.VMEM((tm, LRU_W), F32),
               pltpu.VMEM((8, LRU_W), F32), pltpu.VMEM((BLOCK, KV_W), F32), pltpu.VMEM((BLOCK, KV_W), F32),
               pltpu.VMEM((8, 128), F32)]
    return pl.pallas_call(
        body, name="layer_fwd", grid=(nt,), out_shape=out_shape, in_specs=in_specs, out_specs=out_specs,
        scratch_shapes=scratch,
        compiler_params=pltpu.CompilerParams(dimension_semantics=("arbitrary",), vmem_limit_bytes=VMEM_LIMIT),
    )(x, tgt, rc, rs1, rs2, ng, win_t, cw, cb, wg, brg, big, lam, gains, sinks, km, vm, og, wout)


def weight_grads(ycat, dout, dproj, xn, mem_operands, vectors, early_at, late_at):
    seq, ncol = xn.shape
    blk = 256
    n_out, n_in = ycat.shape[1] // blk, dproj.shape[1] // blk
    assert n_out == N_CHIPS and late_at[0] >= n_out
    g_out = jax.ShapeDtypeStruct((N_CHIPS, 2, blk // 2, dout.shape[1]), F32)
    g_kv = jax.ShapeDtypeStruct((N_CHIPS, 2, D_MODEL // 8, 2 * XATT_W), F32)
    g_small = jax.ShapeDtypeStruct((2, SMALL_ROWS // 2, 128), F32)
    shape_e, scratch_e = _hosted_reduce_shapes([g_kv], g_small)
    shape_l, scratch_l = _hosted_reduce_shapes([g_out], None)
    n_mem, names = len(mem_operands), tuple(vectors)

    def body(l1_ref, r1_ref, l2_ref, r2_hbm, *refs):
        mem_refs, vec_refs = refs[:n_mem], refs[n_mem:n_mem + len(names)]
        o_ref, sum_out, sum_kv, sum_sm, gout_scr, gkv_scr, pack_scr, r2_ref, r2_sem, *scratch = refs[n_mem + len(names):]
        j = pl.program_id(0)
        r2_copy = pltpu.make_async_copy(r2_hbm, r2_ref, r2_sem.at[0])

        @pl.when(j == 0)
        def _():
            r2_copy.start()
            _mem_bwd_and_pack(*mem_refs, dict(zip(names, vec_refs)), gkv_scr, pack_scr)

        def reduce_stages(closing):
            _hosted_reduce(j, n_out + n_in, closing, early_at, (gkv_scr, pack_scr), (sum_kv, sum_sm),
                           scratch[:len(scratch_e)], True)
            _hosted_reduce(j, n_out + n_in, closing, late_at, (gout_scr,), (sum_out,), scratch[len(scratch_e):], False)

        reduce_stages(False)

        @pl.when(j < n_out)
        def _():
            gout_scr[j] = _mm_tn(l1_ref[...], r1_ref[...]).reshape(g_out.shape[1:])

        pl.when(j == n_out)(r2_copy.wait)

        @pl.when(j >= n_out)
        def _():
            o_ref[...] = _mm_tn(l2_ref[...], r2_ref[...])

        reduce_stages(True)

    vm = pl.BlockSpec(memory_space=pltpu.VMEM)
    hbm = pl.BlockSpec(memory_space=pl.ANY)
    return pl.pallas_call(
        body, name="weight_grads", grid=(n_out + n_in,),
        out_shape=(jax.ShapeDtypeStruct((dproj.shape[1], ncol), F32), *shape_l, *shape_e),
        in_specs=[pl.BlockSpec((seq, blk), lambda j: (0, jnp.minimum(j, n_out - 1))), _const_spec(dout.shape, True),
                  pl.BlockSpec((seq, blk), lambda j: (0, jnp.maximum(j - n_out, 0))), hbm]
        + [vm] * (n_mem + len(names)),
        out_specs=(pl.BlockSpec((blk, ncol), lambda j: (jnp.maximum(j - n_out, 0), 0)), hbm, hbm, hbm),
        scratch_shapes=[pltpu.VMEM(s.shape, F32) for s in (g_out, g_kv, g_small)]
        + [pltpu.VMEM(xn.shape, xn.dtype), pltpu.SemaphoreType.DMA((1,))] + scratch_e + scratch_l,
        compiler_params=pltpu.CompilerParams(dimension_semantics=("arbitrary",), vmem_limit_bytes=VMEM_LIMIT),
    )(ycat, dout, dproj, xn, *mem_operands, *vectors.values())


def layer_bwd(x, dout, proj, ya, yb, yc, pswa, pmem, psink, gates, a_all, rc, rs1, rs2, ng, win_t, cw, wg, lam, gains,
              km, vm, og, wout):
    seq = x.shape[0]
    tm = min(ROW_TILE, seq)
    nt = seq // tm
    nb = tm // BLOCK

    def body(x_ref, dout_ref, proj_ref, ya_ref, yb_ref, yc_ref, pswa_ref, pmem_ref, psink_ref, gates_ref, a_ref,
             c_ref, s1_ref, s2_ref,
             yah_ref, kvh_ref, ch_ref, s1h_ref, s2h_ref,
             ng_ref, win_ref, cw_ref, wg_ref, lam_ref, gn_ref, km_ref, vm_ref, og_ref, wout_ref,
             gx_ref, dproj_ref, gwg_ref, dkm_ref, dvm_ref, gng_ref, gog_ref, gcb_ref, gbrg_ref, gbig_ref, glam_ref,
             gcw_ref, gqn_ref, gkn_ref, gxqn_ref, gsink_ref,
             hext_ref, aext_ref, an_scr, dh_scr, g_scr, dxc_ext, gcar_ref, dkcar_ref, dvcar_ref):
        i = pl.program_id(0)
        tile = nt - 1 - i
        first_tile = tile == 0

        @pl.when(i == 0)
        def _():
            for r in (gwg_ref, dkm_ref, dvm_ref, gng_ref, gog_ref, gcb_ref, gbrg_ref, gbig_ref, glam_ref, gcw_ref,
                      gqn_ref, gkn_ref, gxqn_ref, gsink_ref, gcar_ref, dkcar_ref, dvcar_ref):
                r[...] = jnp.zeros_like(r)
            dxc_ext[tm:tm + 8, :] = jnp.zeros((8, LRU_W), F32)
            aext_ref[tm:tm + 8, :] = jnp.zeros((8, LRU_W), F32)

        xv = x_ref[...]
        dov = dout_ref[...]
        dz = _mm_nt(dov, wout_ref[...])
        ogv = og_ref[...]

        def group_bwd(y, gate, g, dzg):
            r = lax.rsqrt(_row_mean(y * y) + EPS)
            n = y * r
            sg = _sigmoid(gate)
            dgate = dzg * (n * g) * (sg * (1.0 + gate * (1.0 - sg)))
            dng = dzg * (gate * sg)
            dn = dng * g
            return r * (dn - n * _row_mean(dn * n)), dgate, _col_sum(dng * n)

        dya, dga, goa = group_bwd(ya_ref[...], proj_ref[:, C_LRUG:C_LRUG + LRU_W], ogv[:, :512], dz[:, :512])
        dyb, dgb, gob = group_bwd(yb_ref[...], proj_ref[:, C_SWAG:C_SWAG + SWA_W], ogv[:, 512:768], dz[:, 512:768])
        dyc, dgc, goc = group_bwd(yc_ref[...], proj_ref[:, C_XG:C_XG + XATT_W], ogv[:, 768:], dz[:, 768:])
        gog_ref[...] += jnp.concatenate([goa, gob, goc], axis=1)
        dproj_ref[:, C_LRUG:C_LRUG + LRU_W] = dga.astype(dproj_ref.dtype)
        dproj_ref[:, C_SWAG:C_SWAG + SWA_W] = dgb.astype(dproj_ref.dtype)
        dproj_ref[:, C_XG:C_XG + XATT_W] = dgc.astype(dproj_ref.dtype)

        gm256 = _group_matrix(XATT_W)
        xq = proj_ref[:, C_XQ:C_XQ + XATT_W]
        rq = lax.rsqrt(_seg_mean(xq * xq, gm256) + EPS)
        qn = xq * rq
        qx = qn * gn_ref[:, G_XQ:G_XK]
        qxb = qx.astype(_MXU)
        dycb = dyc.astype(_MXU)
        dp_all = _mm_nt(dycb, vm_ref[...])
        dsm = []
        for j in range(4):
            pj = pmem_ref[:, MEM_LEN * j:MEM_LEN * (j + 1)].astype(F32)
            dp = dp_all[:, MEM_LEN * j:MEM_LEN * (j + 1)]
            dsm.append((pj * (dp - jnp.sum(pj * dp, axis=-1, keepdims=True))).astype(_MXU))
        ds_all = jnp.concatenate(dsm, axis=1)
        dvm_ref[...] += _mm_tn(dycb, pmem_ref[...])
        dkm_ref[...] += _mm_tn(qxb, ds_all)
        dqx = _mm(ds_all, km_ref[...])
        gxqn_ref[...] += _col_sum(dqx * qn)
        dqn = dqx * gn_ref[:, G_XQ:G_XK]
        dproj_ref[:, C_XQ:C_XQ + XATT_W] = (rq * (dqn - qn * _seg_mean(dqn * qn, gm256))).astype(dproj_ref.dtype)

        gm128 = _group_matrix(KV_W)
        cv, s1v, s2v = c_ref[...], s1_ref[...], s2_ref[...]

        def head_norm(t):
            r = lax.rsqrt(_seg_mean(t * t, gm128) + EPS)
            return t * r, r

        qn_, qr_ = zip(head_norm(proj_ref[:, C_SQ:C_SQ + 128]), head_norm(proj_ref[:, C_SQ + 128:C_SQ + 256]))
        qrope = [_rope(qn_[h] * gn_ref[:, G_Q:G_K], cv, s1v, s2v).astype(_MXU) for h in range(2)]
        kn, krr = head_norm(proj_ref[:, C_SK:C_SK + KV_W])
        kr = _rope(kn * gn_ref[:, G_K:G_XQ], cv, s1v, s2v)
        khn, _ = head_norm(kvh_ref[:, 0:KV_W])
        khr = _rope(khn * gn_ref[:, G_K:G_XQ], ch_ref[...], s1h_ref[...], s2h_ref[...])
        ka = _place_kv(jnp.concatenate([khr, kr], axis=0), 0.125)
        va = _place_kv(jnp.concatenate([kvh_ref[:, KV_W:2 * KV_W], proj_ref[:, C_SV:C_SV + KV_W]], axis=0), 1.0)
        lane128 = lax.broadcasted_iota(jnp.int32, (1, 128), 1)
        gsink = jnp.zeros((1, 128), F32)
        dk_band, dv_band, dq_blk = [], [], []
        for b in range(nb):
            band = slice(BLOCK * b, BLOCK * b + 2 * BLOCK)
            blk = slice(BLOCK * b, BLOCK * (b + 1))
            dka, dva, dsb = [], [], []
            deltas = jnp.zeros((BLOCK, 128), F32)
            for j in range(4):
                qh = qrope[j // 2][blk]
                doh = dyb[blk, KV_W * (j // 2):KV_W * (j // 2 + 1)].astype(_MXU)
                pb = pswa_ref[blk, 2 * BLOCK * j:2 * BLOCK * (j + 1)]
                p = pb.astype(F32)
                dp = _mm_nt(doh, va[j][band])
                delta = jnp.sum(p * dp, axis=-1, keepdims=True)
                ds = (p * (dp - delta)).astype(_MXU)
                deltas = jnp.where(lane128 == j, delta, deltas)
                dva.append(_mm_tn(pb, doh))
                dka.append(_mm_tn(ds, qh))
                dsb.append(ds)
            gsink = gsink - _col_sum(psink_ref[blk, :] * deltas)
            dk_band.append(_unplace_kv(dka) * 0.125)
            dv_band.append(_unplace_kv(dva))
            dq_blk.append([_mm(jnp.concatenate(dsb[2 * h:2 * h + 2], axis=1),
                               jnp.concatenate([ka[2 * h][band], ka[2 * h + 1][band]], axis=0)) for h in range(2)])
        gsink_ref[...] += gsink
        dk_rows = [dk_band[b][BLOCK:] + (dk_band[b + 1][:BLOCK] if b + 1 < nb else dkcar_ref[...]) for b in range(nb)]
        dv_rows = [dv_band[b][BLOCK:] + (dv_band[b + 1][:BLOCK] if b + 1 < nb else dvcar_ref[...]) for b in range(nb)]
        dkcar_ref[...] = dk_band[0][:BLOCK]
        dvcar_ref[...] = dv_band[0][:BLOCK]
        dkg = _rope_bwd(jnp.concatenate(dk_rows, axis=0), cv, s1v, s2v)
        gkn = _col_sum(dkg * kn)
        dkn = dkg * gn_ref[:, G_K:G_XQ]
        dproj_ref[:, C_SK:C_SK + KV_W] = (krr * (dkn - kn * _seg_mean(dkn * kn, gm128))).astype(dproj_ref.dtype)
        dproj_ref[:, C_SV:C_SV + KV_W] = jnp.concatenate(dv_rows, axis=0).astype(dproj_ref.dtype)
        gqn = jnp.zeros((1, 128), F32)
        for h in range(2):
            dqg = _rope_bwd(jnp.concatenate([dq_blk[b][h] for b in range(nb)], axis=0), cv, s1v, s2v)
            gqn = gqn + _col_sum(dqg * qn_[h])
            dqn_ = dqg * gn_ref[:, G_Q:G_K]
            dproj_ref[:, C_SQ + 128 * h:C_SQ + 128 * (h + 1)] = (
                qr_[h] * (dqn_ - qn_[h] * _seg_mean(dqn_ * qn_[h], gm128))).astype(dproj_ref.dtype)
        gqn_ref[...] += gqn
        gkn_ref[...] += gkn

        u = proj_ref[:, C_LRUX:C_LRUX + LRU_W]
        xc, rg, ig, sq = (gates_ref[:, LRU_W * k:LRU_W * (k + 1)].astype(F32) for k in range(4))
        a = a_ref[...]
        sp = _softplus(-lam_ref[...])
        hext_ref[0:8, :] = jnp.where(first_tile, 0.0, yah_ref[...])
        hext_ref[8:8 + tm, :] = ya_ref[...]
        hprev = hext_ref[pl.ds(7, tm), :]
        aext_ref[0:tm, :] = a
        an_scr[...] = aext_ref[pl.ds(1, tm), :]
        dh_scr[...] = dya
        dh_scr[tm - 1:tm, :] = dh_scr[tm - 1:tm, :] + gcar_ref[0:1, :]
        row8 = lax.broadcasted_iota(jnp.int32, (8, LRU_W), 0)

        def scan_step(gi, carry):
            r0 = pl.multiple_of((tm // 8 - 1 - gi) * 8, 8)
            av = an_scr[pl.ds(r0, 8), :]
            bv = dh_scr[pl.ds(r0, 8), :]
            for d in (1, 2, 4):
                a_sh = jnp.where(row8 < 8 - d, pltpu.roll(av, 8 - d, 0), 1.0)
                b_sh = jnp.where(row8 < 8 - d, pltpu.roll(bv, 8 - d, 0), 0.0)
                bv = bv + av * b_sh
                av = av * a_sh
            gv = bv + av * carry
            g_scr[pl.ds(r0, 8), :] = gv
            return gv[0:1, :]

        g0 = lax.fori_loop(0, tm // 8, scan_step, jnp.zeros((1, LRU_W), F32), unroll=True)
        gcar_ref[0:1, :] = a[0:1, :] * g0
        gv = g_scr[...]
        da = gv * hprev
        dig = gv * sq * xc
        dxc = gv * sq * ig
        dla = da * a - gv * (ig * xc) * ((a * a) / sq)
        drg = dla * ((-LRU_C) * sp)
        glam_ref[...] += _col_sum(dla * rg)
        dpr = drg * rg * (1.0 - rg)
        dpi = dig * ig * (1.0 - ig)
        gbrg_ref[...] += _col_sum(dpr)
        gbig_ref[...] += _col_sum(dpi)
        dpre0 = jnp.concatenate([dpr[:, :256], dpi[:, :256]], axis=1).astype(_MXU)
        dpre1 = jnp.concatenate([dpr[:, 256:], dpi[:, 256:]], axis=1).astype(_MXU)
        gwg_ref[0] += _mm_tn(xc[:, :256], dpre0)
        gwg_ref[1] += _mm_tn(xc[:, 256:], dpre1)
        dxc = dxc + jnp.concatenate([_mm_nt(dpre0, wg_ref[0]), _mm_nt(dpre1, wg_ref[1])], axis=1)
        gcb_ref[...] += _col_sum(dxc)
        dxc_ext[0:tm, :] = dxc
        du = jnp.zeros((tm, LRU_W), F32)
        for k in range(CONV_K):
            later = dxc_ext[pl.ds(3 - k, tm), :]
            gcw_ref[k:k + 1, :] += _col_sum(later * u)
            du = du + cw_ref[k:k + 1, :] * later
        dxc_ext[tm:tm + 8, :] = dxc[0:8, :]
        dproj_ref[:, C_LRUX:C_LRUX + LRU_W] = du.astype(dproj_ref.dtype)

        dxn = _mm(dproj_ref[...], win_ref[...])
        rx = lax.rsqrt(_row_mean(xv * xv) + EPS)
        xh = xv * rx
        gng_ref[...] += _col_sum(dxn * xh)
        dxh = dxn * ng_ref[...]
        gx_ref[...] = dov.astype(F32) + rx * (dxh - xh * _row_mean(dxh * xh))

        @pl.when(i == nt - 1)
        def _():
            glam_ref[...] = glam_ref[...] * (LRU_C * _sigmoid(-lam_ref[...]))
            for r in (gqn_ref, gkn_ref, gxqn_ref):
                r[...] = _fold_heads(r[...])

    def rows(ncol, arr_cols_block=0):
        return pl.BlockSpec((tm, ncol), lambda i: (nt - 1 - i, arr_cols_block))

    def halo(nrow, ncol, colblk=0):
        per = tm // nrow
        return pl.BlockSpec((nrow, ncol), lambda i: (jnp.maximum((nt - 1 - i) * per - 1, 0), colblk))

    in_specs = [rows(D_MODEL), rows(D_MODEL), rows(D_IN), rows(LRU_W), rows(SWA_W), rows(XATT_W),
                rows(4 * 2 * BLOCK), rows(4 * MEM_LEN), rows(128), rows(4 * LRU_W), rows(LRU_W),
                rows(128), rows(128), rows(128),
                halo(8, LRU_W), halo(BLOCK, 2 * KV_W, C_SK // (2 * KV_W)),
                halo(BLOCK, 128), halo(BLOCK, 128), halo(BLOCK, 128),
                _const_spec((1, D_MODEL)), _const_spec((D_IN, D_MODEL), True), _const_spec((CONV_K, LRU_W)),
                _const_spec((2, 256, 512), True), _const_spec((1, LRU_W)), _const_spec((1, GAINS_W)),
                _const_spec((4 * MEM_LEN, XATT_W), True), _const_spec((4 * MEM_LEN, XATT_W), True),
                _const_spec((1, D_MODEL)), _const_spec((D_MODEL, D_MODEL), True)]
    small = [(2, 256, 512), (XATT_W, 4 * MEM_LEN), (XATT_W, 4 * MEM_LEN), (1, D_MODEL), (1, D_MODEL), (1, LRU_W), (1, LRU_W),
             (1, LRU_W), (1, LRU_W), (CONV_K, LRU_W), (1, 128), (1, 128), (1, XATT_W), (1, 128)]
    out_shape = (jax.ShapeDtypeStruct((seq, D_MODEL), F32), jax.ShapeDtypeStruct((seq, D_IN), _MXU)) + tuple(
        jax.ShapeDtypeStruct(s, F32) for s in small)
    out_specs = (rows(D_MODEL), rows(D_IN)) + tuple(_const_spec(s) for s in small)
    scratch = [pltpu.VMEM((tm + 8, LRU_W), F32), pltpu.VMEM((tm + 8, LRU_W), F32),
               pltpu.VMEM((tm, LRU_W), F32), pltpu.VMEM((tm, LRU_W), F32), pltpu.VMEM((tm, LRU_W), F32),
               pltpu.VMEM((tm + 8, LRU_W), F32),
               pltpu.VMEM((8, LRU_W), F32), pltpu.VMEM((BLOCK, KV_W), F32), pltpu.VMEM((BLOCK, KV_W), F32)]
    return pl.pallas_call(
        body, name="layer_bwd", grid=(nt,), out_shape=out_shape, in_specs=in_specs, out_specs=out_specs,
        scratch_shapes=scratch,
        compiler_params=pltpu.CompilerParams(dimension_semantics=("arbitrary",), vmem_limit_bytes=VMEM_LIMIT),
    )(x, dout, proj, ya, yb, yc, pswa, pmem, psink, gates, a_all, rc, rs1, rs2, ya, proj, rc, rs1, rs2,
      ng, win_t, cw, wg, lam, gains, km, vm, og, wout)


def _reduce_protocol(big, sm, outs, osm, r1, r1s, wire, r2, r2s, wire2, ps, own, send, recv, lsem):
    nbig = len(big)
    x, y, c = lax.axis_index("x"), lax.axis_index("y"), lax.axis_index("c")
    sibling = (x, y, 1 - c)
    near, far, diag = _partners(x, y, c)
    me, near_id, far_id, diag_id = _chip_of(x, y), _chip_of(*near), _chip_of(*far), _chip_of(*diag)

    def copy(k, src, dst, to):
        return pltpu.make_async_remote_copy(src_ref=src, dst_ref=dst, send_sem=send.at[k], recv_sem=recv.at[k],
                                            device_id=to, device_id_type=MESH)

    def sent(stage, a):
        if a == nbig:
            src, dst, to = ((sm.at[1 - c], r1s, sibling), (r1s, r2s.at[0], (*near, c)), (ps, r2s.at[1], (*far, c)),
                            (osm.at[c], osm.at[c], sibling))[stage]
            return [copy(5 * nbig + stage, src, dst, to)]
        if stage == 0:
            return [copy(5 * a, big[a].at[:, 1 - c], r1[a], sibling)]
        if stage == 1:
            return [copy(5 * a + 1, wire[a].at[near_id], r2[a].at[0], (*near, c)),
                    copy(5 * a + 2, wire[a].at[diag_id], r2[a].at[1], (*near, c))]
        if stage == 2:
            return [copy(5 * a + 3, wire2[a], r2[a].at[2], (*far, c))]
        return [copy(5 * a + 4, outs[a].at[c], outs[a].at[c], sibling)]

    arrays = range(nbig + (sm is not None))

    def start(stage, a):
        for cp in sent(stage, a):
            cp.start()

    def arrived(k, ref):
        copy(k, ref, ref, sibling).wait_recv()

    def loads():
        return [pltpu.make_async_copy(big[a].at[:, c], own[a], lsem.at[a]) for a in range(nbig)]

    def stage0():
        for a in arrays:
            start(0, a)
        for cp in loads():
            cp.start()

    def stage1():
        for a in range(nbig):
            loads()[a].wait()
            arrived(5 * a, r1[a])
            for k in range(N_CHIPS):
                r1[a][k] = own[a][k] + r1[a][k]
                wire[a][k] = r1[a][k].astype(wire[a].dtype)
            start(1, a)
        if sm is not None:
            arrived(5 * nbig, r1s)
            r1s[...] = sm[c] + r1s[...]
            start(1, nbig)

    def stage2():
        for a in range(nbig):
            arrived(5 * a + 1, r2[a].at[0])
            arrived(5 * a + 2, r2[a].at[1])
            r1[a][me] = r1[a][me] + r2[a][0].astype(F32)
            wire2[a][...] = (r1[a][far_id] + r2[a][1].astype(F32)).astype(wire2[a].dtype)
            start(2, a)
        if sm is not None:
            arrived(5 * nbig + 1, r2s.at[0])
            ps[...] = r1s[...] + r2s[0]
            start(2, nbig)

    def stage3():
        for a in range(nbig):
            arrived(5 * a + 3, r2[a].at[2])
            outs[a][c] = r1[a][me] + r2[a][2].astype(F32)
            start(3, a)
        if sm is not None:
            arrived(5 * nbig + 2, r2s.at[1])
            osm[c] = ps[...] + r2s[1]
            start(3, nbig)

    def stage4():
        for a in range(nbig):
            arrived(5 * a + 4, outs[a].at[1 - c])
        if sm is not None:
            arrived(5 * nbig + 3, osm.at[1 - c])
        for stage in range(4):
            for a in arrays:
                for cp in sent(stage, a):
                    cp.wait_send()

    return [stage0, stage1, stage2, stage3, stage4]


def _reduce_buffers(bigs, g_small):
    half = [b.shape[2:] for b in bigs]
    sm_half = None if g_small is None else g_small.shape[1:]
    out_shape = [jax.ShapeDtypeStruct((2,) + h, F32) for h in half]
    small = lambda lead: [] if g_small is None else [pltpu.VMEM(lead + sm_half, F32)]
    if g_small is not None:
        out_shape.append(jax.ShapeDtypeStruct(g_small.shape, F32))
    n_sem = 5 * len(bigs) + 4
    scratch = ([pltpu.VMEM((N_CHIPS,) + h, F32) for h in half] + small(())
               + [pltpu.VMEM((N_CHIPS,) + h, _WIRE) for h in half]
               + [pltpu.VMEM((3,) + h, _WIRE) for h in half] + small((2,))
               + [pltpu.VMEM(h, _WIRE) for h in half] + small(())
               + [pltpu.VMEM((N_CHIPS,) + h, F32) for h in half]
               + [pltpu.SemaphoreType.DMA((n_sem,)), pltpu.SemaphoreType.DMA((n_sem,)),
                  pltpu.SemaphoreType.DMA((len(bigs),))])
    return out_shape, scratch


def _split_reduce_refs(refs, nbig, has_small):
    it = iter(refs)
    take = lambda n: [next(it) for _ in range(n)]
    one = lambda: next(it) if has_small else None
    big, sm = take(nbig), one()
    outs, osm = take(nbig), one()
    r1, r1s, wire, r2, r2s, wire2, ps, own = take(nbig), one(), take(nbig), take(nbig), one(), take(nbig), one(), take(nbig)
    send, recv, lsem = take(3)
    return big, sm, outs, osm, r1, r1s, wire, r2, r2s, wire2, ps, own, send, recv, lsem


def _hosted_reduce_shapes(bigs, g_small):
    red_shape, scratch = _reduce_buffers(bigs, g_small)
    nres = len(red_shape)
    return red_shape, [pltpu.VMEM(r.shape, r.dtype) for r in red_shape] + scratch + [pltpu.SemaphoreType.DMA((nres,))]


def _hosted_reduce(step, n_steps, closing, stage_at, operands, results, scratch, has_small):
    nres = len(results)
    sums, rest, fsem = scratch[:nres], scratch[nres:-1], scratch[-1]
    refs = tuple(operands) + tuple(sums) + tuple(rest)

    def to_results():
        out = [pltpu.make_async_copy(sums[k], results[k], fsem.at[k]) for k in range(nres)]
        for cp in out:
            cp.start()
        for cp in out:
            cp.wait()

    stages = _reduce_protocol(*_split_reduce_refs(refs, nres - has_small, has_small))

    def last_stage():
        stages[-1]()
        to_results()

    for at, stage in zip(stage_at, stages[:-1] + [last_stage]):
        if closing == (at == n_steps):
            pl.when(step == min(at, n_steps - 1))(stage)


def reduce_grads(big, name, parts):
    chips, halves, rows_, cols = big.shape
    sub = jax.ShapeDtypeStruct((chips, halves, rows_ // parts, cols), big.dtype)

    def body(b_ref, o_ref, *scratch):
        refs = [b_ref.at[:, :, s] for s in range(parts)] + [o_ref.at[:, s] for s in range(parts)] + list(scratch)
        for stage in _reduce_protocol(*_split_reduce_refs(refs, parts, False)):
            stage()

    _, scratch = _reduce_buffers([sub] * parts, None)
    return pl.pallas_call(
        body, name=name, out_shape=jax.ShapeDtypeStruct((halves, parts, rows_ // parts, cols), F32),
        in_specs=[pl.BlockSpec(memory_space=pl.ANY)], out_specs=pl.BlockSpec(memory_space=pltpu.VMEM),
        scratch_shapes=scratch, compiler_params=pltpu.CompilerParams(vmem_limit_bytes=VMEM_LIMIT),
    )(big.reshape(chips, halves, parts, rows_ // parts, cols))


def adamw(items, g_pack, ws, ms, vs):
    blocks = []
    for k, (w, _, _, _) in enumerate(items):
        rows_, cols = w.shape
        tr = max(t for t in range(8, rows_ + 1, 8) if rows_ % t == 0 and t * cols * 4 <= ADAM_BLOCK_BYTES)
        blocks += [(k, r, tr) for r in range(0, rows_, tr)]
    nin, n = 4 * len(items), len(ws)

    def body(*refs):
        mats_in, small_in = refs[:nin], refs[nin:nin + 1 + 3 * n]
        n_out = nin + 4 * n + 1
        outs, scratch = refs[nin + 1 + 3 * n:nin + 1 + 3 * n + n_out], refs[nin + 1 + 3 * n + n_out:]
        buf, (lsem, ssem) = scratch[:nin], scratch[nin:]
        loads = [[pltpu.make_async_copy(mats_in[4 * k + q].at[pl.ds(r, tr)], buf[4 * k + q].at[pl.ds(r, tr)],
                                        lsem.at[4 * c + q]) for q in range(4)] for c, (k, r, tr) in enumerate(blocks)]
        for cps in loads:
            for cp in cps:
                cp.start()
        _adamw_small(small_in[0], *(small_in[1 + k * n:1 + (k + 1) * n] for k in range(3)),
                     *(outs[nin + k * n:nin + (k + 1) * n] for k in range(4)), outs[-1])
        stores = []
        for c, (k, r, tr) in enumerate(blocks):
            for cp in loads[c]:
                cp.wait()
            w_ref, g_ref, m_ref, v_ref = (buf[4 * k + q].at[pl.ds(r, tr)] for q in range(4))
            w_ref[...], m_ref[...], v_ref[...] = _adam_update(w_ref[...], g_ref[...], m_ref[...], v_ref[...])
            for q, src in enumerate((g_ref, w_ref, m_ref, v_ref)):
                stores.append(pltpu.make_async_copy(src, outs[4 * k + q].at[pl.ds(r, tr)], ssem.at[4 * c + q]))
                stores[-1].start(priority=1)
        for cp in stores:
            cp.wait()

    shapes = [jax.ShapeDtypeStruct(w.shape, F32) for w, _, _, _ in items for _ in range(4)]
    vm = pl.BlockSpec(memory_space=pltpu.VMEM)
    hbm = pl.BlockSpec(memory_space=pl.ANY)
    like = [jax.ShapeDtypeStruct(w.shape, F32) for w in ws]
    res = pl.pallas_call(
        body, name="adamw", out_shape=(*shapes, *like * 4, jax.ShapeDtypeStruct((1, 1), F32)),
        in_specs=[hbm] * nin + [vm] * (1 + 3 * n), out_specs=(*[hbm] * nin, *[vm] * (4 * n + 1)),
        scratch_shapes=[pltpu.VMEM(s.shape, F32) for s in shapes] + [pltpu.SemaphoreType.DMA((4 * len(blocks),))] * 2,
        compiler_params=pltpu.CompilerParams(vmem_limit_bytes=VMEM_LIMIT),
    )(*[a for item in items for a in item], g_pack, *ws, *ms, *vs)
    return [res[4 * k:4 * k + 4] for k in range(len(items))], res[nin:]


def _adam_update(w, g, m, v):
    nm = ADAM_B1 * m + (1.0 - ADAM_B1) * g
    nv = ADAM_B2 * v + (1.0 - ADAM_B2) * (g * g)
    m_hat = nm / (1.0 - ADAM_B1 ** ADAM_STEP)
    v_hat = nv / (1.0 - ADAM_B2 ** ADAM_STEP)
    return (-ADAM_LR) * (m_hat / (jnp.sqrt(v_hat) + ADAM_EPS) + ADAM_WD * w), nm, nv


def _adamw_small(pk, w_refs, m_refs, v_refs, g_out, d_out, nm_out, nv_out, loss_ref):
    nvec = len(SMALL_VECTORS)
    loss_ref[...] = pk[LOSS_ROW:LOSS_ROW + 1, 0:1]
    chip = 2 * lax.axis_index("x") + lax.axis_index("y")
    for k, (name, row, width) in enumerate(SMALL_VECTORS):
        if name == "conv_w":
            g = jnp.concatenate([pk[pl.ds(row + 4 * t + chip, 1), :] for t in range(CONV_K)], axis=0)[None]
        elif width >= 128:
            g = jnp.concatenate([pk[row + r:row + r + 1, :] for r in range(width // 128)], axis=1)
        else:
            g = pk[row:row + 1, 0:width]
        g_out[k][...] = g
        d_out[k][...], nm_out[k][...], nv_out[k][...] = _adam_update(w_refs[k][...], g, m_refs[k][...], v_refs[k][...])
    for k in range(nvec, nvec + len(SMALL_MATRICES)):
        for b in range(LRU_BLOCKS):
            rows_ = pk[GATES_ROW + HEAD * b:GATES_ROW + HEAD * (b + 1), :]
            g = (pltpu.roll(rows_, HEAD, axis=1) if k > nvec else rows_)[:, 0:HEAD]
            g_out[k][0, b] = g
            d_out[k][0, b], nm_out[k][0, b], nv_out[k][0, b] = _adam_update(
                w_refs[k][0, b], g, m_refs[k][0, b], v_refs[k][0, b])


SMALL_VECTORS = (("norm_g", 512, 1024), ("mem_norm_g", 520, 1024), ("conv_w", 528, 512), ("conv_b", 544, 512),
                 ("b_rg", 548, 512), ("b_ig", 552, 512), ("lru_lambda", 556, 512), ("q_norm_g", 560, 64),
                 ("k_norm_g", 561, 64), ("sinks", 562, 4), ("xq_norm_g", 563, 64), ("xk_norm_g", 564, 64),
                 ("out_norm_g", 565, 1024))
LOSS_ROW = 573
SMALL_MATRICES = ("w_rg", "w_ig")
GATES_ROW = 0
SMALL_ROWS = 640


def _rope_tables(seq):
    pos = np.arange(seq, dtype=np.float32)
    inv_freq = (np.float32(ROPE_THETA) ** (-(np.arange(0, ROPE_DIM, 2, dtype=np.float32) / np.float32(ROPE_DIM)))
                ).astype(np.float32)
    ang = (pos[:, None] * inv_freq[None, :]).astype(np.float32)
    cos, sin = np.cos(ang).astype(np.float32), np.sin(ang).astype(np.float32)
    z = lambda n: np.zeros((seq, n), np.float32)
    c64 = np.concatenate([cos, cos, np.ones((seq, HEAD - ROPE_DIM), np.float32)], axis=1)
    s1_64 = np.concatenate([-sin, z(HEAD - 8)], axis=1)
    s2_64 = np.concatenate([z(8), sin, z(HEAD - ROPE_DIM)], axis=1)
    return tuple(jnp.asarray(np.concatenate([t, t], axis=1)) for t in (c64, s1_64, s2_64))


def kernel(x, mem, norm_g, mem_norm_g, w_in, conv_w, conv_b, w_rg, b_rg, w_ig, b_ig, lru_lambda, q_norm_g, k_norm_g, sinks, w_mem_kv, xq_norm_g, xk_norm_g, out_norm_g, w_out, loss_target, m_norm_g, m_mem_norm_g, m_w_in, m_conv_w, m_conv_b, m_w_rg, m_b_rg, m_w_ig, m_b_ig, m_lru_lambda, m_q_norm_g, m_k_norm_g, m_sinks, m_w_mem_kv, m_xq_norm_g, m_xk_norm_g, m_out_norm_g, m_w_out, v_norm_g, v_mem_norm_g, v_w_in, v_conv_w, v_conv_b, v_w_rg, v_b_rg, v_w_ig, v_b_ig, v_lru_lambda, v_q_norm_g, v_k_norm_g, v_sinks, v_w_mem_kv, v_xq_norm_g, v_xk_norm_g, v_out_norm_g, v_w_out):
    seq = x.shape[1]
    xs, tgt, mems = x[0], loss_target[0], mem[0]

    win_t, wout, wkv, cw, wg, gains, km, vm = gather_weights(
        w_in[0].T, w_out[0], w_mem_kv[0], conv_w, w_rg, w_ig, (q_norm_g, k_norm_g, xq_norm_g, xk_norm_g), mems,
        mem_norm_g)
    rc, rs1, rs2 = _rope_tables(seq)
    proj, ya, yb, yc, ycat, xn, dout, pswa, pmem, psink, gates, a_all, loss8 = layer_fwd(
        xs, tgt, rc, rs1, rs2, norm_g, win_t, cw, conv_b, wg, b_rg, b_ig, lru_lambda, gains, sinks, km, vm,
        out_norm_g, wout)
    (gx, dproj, g_wg, dkm, dvm, g_ng, g_og, g_cb, g_brg, g_big, g_lam, g_cw, g_qn, g_kn, g_xqn, g_sink) = layer_bwd(
        xs, dout, proj, ya, yb, yc, pswa, pmem, psink, gates, a_all, rc, rs1, rs2, norm_g, win_t, cw, wg, lru_lambda,
        gains, km, vm, out_norm_g, wout)
    g_win_t, r_out, r_kv, r_small = weight_grads(
        ycat, dout, dproj, xn, (mems, mem_norm_g, wkv, gains, dkm, dvm, g_wg, loss8), dict(
            norm_g=g_ng, conv_w=g_cw, conv_b=g_cb, b_rg=g_brg, b_ig=g_big, lru_lambda=g_lam, q_norm_g=g_qn,
            k_norm_g=g_kn, sinks=g_sink, xq_norm_g=g_xqn, out_norm_g=g_og), (0, 1, 4, 7, 8), (4, 5, 9, 11, 13))
    r_in = reduce_grads(g_win_t.reshape(N_CHIPS, 2, D_IN // 8, D_MODEL), "reduce_w_in", 6)

    r_small = r_small.reshape(SMALL_ROWS, 128)
    grads = {}
    weights = dict(norm_g=norm_g, mem_norm_g=mem_norm_g, w_in=w_in, conv_w=conv_w, conv_b=conv_b, w_rg=w_rg, b_rg=b_rg,
                   w_ig=w_ig, b_ig=b_ig, lru_lambda=lru_lambda, q_norm_g=q_norm_g, k_norm_g=k_norm_g, sinks=sinks,
                   w_mem_kv=w_mem_kv, xq_norm_g=xq_norm_g, xk_norm_g=xk_norm_g, out_norm_g=out_norm_g, w_out=w_out)
    ms = dict(norm_g=m_norm_g, mem_norm_g=m_mem_norm_g, w_in=m_w_in, conv_w=m_conv_w, conv_b=m_conv_b, w_rg=m_w_rg,
              b_rg=m_b_rg, w_ig=m_w_ig, b_ig=m_b_ig, lru_lambda=m_lru_lambda, q_norm_g=m_q_norm_g, k_norm_g=m_k_norm_g,
              sinks=m_sinks, w_mem_kv=m_w_mem_kv, xq_norm_g=m_xq_norm_g, xk_norm_g=m_xk_norm_g,
              out_norm_g=m_out_norm_g, w_out=m_w_out)
    vs = dict(norm_g=v_norm_g, mem_norm_g=v_mem_norm_g, w_in=v_w_in, conv_w=v_conv_w, conv_b=v_conv_b, w_rg=v_w_rg,
              b_rg=v_b_rg, w_ig=v_w_ig, b_ig=v_b_ig, lru_lambda=v_lru_lambda, q_norm_g=v_q_norm_g, k_norm_g=v_k_norm_g,
              sinks=v_sinks, w_mem_kv=v_w_mem_kv, xq_norm_g=v_xq_norm_g, xk_norm_g=v_xk_norm_g,
              out_norm_g=v_out_norm_g, w_out=v_w_out)

    delta, new_m, new_v = {}, {}, {}
    small_names = [n for n, _, _ in SMALL_VECTORS] + list(SMALL_MATRICES)
    (res_in, res_out, res_kv), res = adamw(
        [(w_in[0].T, r_in.reshape(D_IN // 4, D_MODEL), m_w_in[0].T, v_w_in[0].T),
         (w_out[0], r_out.reshape(D_MODEL // 4, D_MODEL), m_w_out[0], v_w_out[0]),
         (w_mem_kv[0], r_kv.reshape(D_MODEL // 4, 2 * XATT_W), m_w_mem_kv[0], v_w_mem_kv[0])],
        r_small, [weights[n] for n in small_names], [ms[n] for n in small_names], [vs[n] for n in small_names])
    grads["w_in"], delta["w_in"], new_m["w_in"], new_v["w_in"] = (r.T[None] for r in res_in)
    grads["w_out"], delta["w_out"], new_m["w_out"], new_v["w_out"] = (r[None] for r in res_out)
    grads["w_mem_kv"], delta["w_mem_kv"], new_m["w_mem_kv"], new_v["w_mem_kv"] = (r[None] for r in res_kv)
    nall = len(small_names)
    for k, into in enumerate((grads, delta, new_m, new_v)):
        into.update(zip(small_names, res[k * nall:(k + 1) * nall]))
    loss = res[-1].reshape(())

    order = ("norm_g", "mem_norm_g", "w_in", "conv_w", "conv_b", "w_rg", "b_rg", "w_ig", "b_ig", "lru_lambda",
             "q_norm_g", "k_norm_g", "sinks", "w_mem_kv", "xq_norm_g", "xk_norm_g", "out_norm_g", "w_out")
    return (loss, gx[None], *[grads[n] for n in order], *[delta[n] for n in order], *[new_m[n] for n in order],
            *[new_v[n] for n in order])
```
